```python
import jax, jax.numpy as jnp
from jax import lax
import numpy as np

D_MODEL = 2048
BATCH = 8
SEQ = 4096
DEPTH = 1

GRID_W = 64
CTX_LEN = 256
D_INNER = D_MODEL
W_A = D_INNER // 2
W_B = D_INNER - W_A
HG_HEADS = 8
HG_DK = W_A // HG_HEADS
HG_DV = W_A // HG_HEADS
ML_HEADS = 4
ML_DH = W_B // ML_HEADS
CHUNK = 64
CONV_K = 3
N_IN = 5 * W_A + 5 * W_B + 4 * ML_HEADS
ALPHA = (2 * DEPTH) ** 0.25
BETA = (8 * DEPTH) ** -0.25
LN_EPS = 1e-5
NORM_EPS = 1e-6

kernel_name = "hymba_hgrn2_mlstm_bidir_dit_block"


def layer_norm(a, g, b):
    af = a.astype(jnp.float32)
    mu = jnp.mean(af, axis=-1, keepdims=True)
    var = jnp.mean(jnp.square(af - mu), axis=-1, keepdims=True)
    out = (af - mu) * lax.rsqrt(var + LN_EPS) * g.astype(jnp.float32) + b.astype(jnp.float32)
    return out.astype(a.dtype)


def modulate(a, shift, scale):
    af = a.astype(jnp.float32)
    mu = jnp.mean(af, axis=-1, keepdims=True)
    var = jnp.mean(jnp.square(af - mu), axis=-1, keepdims=True)
    n = (af - mu) * lax.rsqrt(var + LN_EPS)
    return (n * (1.0 + scale.astype(jnp.float32)) + shift.astype(jnp.float32)).astype(a.dtype)


def rms_norm(a):
    return a * lax.rsqrt(jnp.mean(jnp.square(a), axis=-1, keepdims=True) + NORM_EPS)


def head_layer_norm(a):
    mu = jnp.mean(a, axis=-1, keepdims=True)
    var = jnp.mean(jnp.square(a - mu), axis=-1, keepdims=True)
    return (a - mu) * lax.rsqrt(var + NORM_EPS)


def heads(a, n_heads):
    return a.reshape(a.shape[:2] + (n_heads, a.shape[-1] // n_heads))


def flip(a):
    return jnp.flip(a, axis=1)


def to_chunks(a):
    bsz, t = a.shape[:2]
    a = a.reshape((bsz, t // CHUNK, CHUNK) + a.shape[2:])
    return jnp.swapaxes(jnp.moveaxis(a, 1, 0), 2, 3)


def from_chunks(o):
    o = jnp.moveaxis(jnp.swapaxes(o, 2, 3), 0, 1)
    return o.reshape((o.shape[0], o.shape[1] * o.shape[2]) + o.shape[3:])


def hgrn2_scan(q, k, v, logf, s0):
    mask = jnp.tril(jnp.ones((CHUNK, CHUNK), dtype=bool))[:, :, None]

    def step(s, inp):
        qc, kc, vc, gc = inp
        b = jnp.cumsum(gc, axis=2)
        o_inter = jnp.einsum('bhtk,bhkv->bhtv', qc * jnp.exp(b), s)
        diff = b[:, :, :, None, :] - b[:, :, None, :, :]
        decay = jnp.exp(jnp.where(mask, diff, -jnp.inf))
        scores = jnp.einsum('bhtk,bhsk,bhtsk->bhts', qc, kc, decay)
        o = o_inter + jnp.einsum('bhts,bhsv->bhtv', scores, vc)
        b_last = b[:, :, -1]
        k_dec = kc * jnp.exp(b_last[:, :, None, :] - b)
        s_new = jnp.exp(b_last)[..., None] * s + jnp.einsum('bhsk,bhsv->bhkv', k_dec, vc)
        return s_new, o

    s_fin, o = lax.scan(step, s0, (to_chunks(q), to_chunks(k), to_chunks(v), to_chunks(logf)))
    return from_chunks(o), s_fin


def mlstm_scan(q, k, v, log_i, log_f, state):
    mask = jnp.tril(jnp.ones((CHUNK, CHUNK), dtype=bool))

    def step(carry, inp):
        c_mat, n_vec, m = carry
        qc, kc, vc, ic, fc = inp
        b = jnp.cumsum(fc, axis=-1)
        log_w = jnp.where(mask, b[..., :, None] - b[..., None, :] + ic[..., None, :], -jnp.inf)
        m_inter = b + m[..., None]
        m_t = jnp.maximum(m_inter, jnp.max(log_w, axis=-1))
        w_inter = jnp.exp(m_inter - m_t)
        w_qk = jnp.exp(log_w - m_t[..., None]) * jnp.einsum('bhtk,bhsk->bhts', qc, kc)
        num = (w_inter[..., None] * jnp.einsum('bhvk,bhtk->bhtv', c_mat, qc)
               + jnp.einsum('bhts,bhsv->bhtv', w_qk, vc))
        den = w_inter * jnp.einsum('bhk,bhtk->bht', n_vec, qc) + jnp.sum(w_qk, axis=-1)
        h = num / jnp.maximum(jnp.abs(den), jnp.exp(-m_t))[..., None]
        m_new = m_t[..., -1]
        w_s = jnp.exp(b[..., -1:] - b + ic - m_new[..., None])
        decay = jnp.exp(b[..., -1] + m - m_new)
        c_new = decay[..., None, None] * c_mat + jnp.einsum('bhsv,bhsk->bhvk', w_s[..., None] * vc, kc)
        n_new = decay[..., None] * n_vec + jnp.einsum('bhs,bhsk->bhk', w_s, kc)
        return (c_new, n_new, m_new), h

    st, h = lax.scan(step, state, (to_chunks(q), to_chunks(k), to_chunks(v), to_chunks(log_i), to_chunks(log_f)))
    return from_chunks(h), st


def short_conv(a, w, b, grid):
    ch = a.shape[-1]
    w = w.astype(jnp.float32)
    if grid:
        rows = a.shape[1] // GRID_W
        img = a.reshape(a.shape[0], rows, GRID_W, ch)
        out = lax.conv_general_dilated(img, w[:, :, None, :], (1, 1), 'SAME',
                                       dimension_numbers=('NHWC', 'HWIO', 'NHWC'), feature_group_count=ch)
        out = out.reshape(a.shape)
    else:
        out = lax.conv_general_dilated(a, w[1][:, None, :], (1,), 'SAME',
                                       dimension_numbers=('NWC', 'WIO', 'NWC'), feature_group_count=ch)
    return jax.nn.silu(out + b.astype(jnp.float32))


def hgrn2_gates(z, lb):
    f = lb + (1.0 - lb) * jax.nn.sigmoid(z)
    return jnp.log(f), 1.0 - f


def zero_states(bsz):
    hg = jnp.zeros((bsz, HG_HEADS, HG_DK, HG_DV), jnp.float32)
    ml = (jnp.zeros((bsz, ML_HEADS, ML_DH, ML_DH), jnp.float32),
          jnp.zeros((bsz, ML_HEADS, ML_DH), jnp.float32),
          jnp.zeros((bsz, ML_HEADS), jnp.float32))
    return (hg, hg, ml, ml)


def mix(u, grid, states, lb, conv_w_l, conv_b_l, gate_b_l, hg_norm_l, ml_norm_l):
    u = u.astype(jnp.float32)
    bsz, t = u.shape[:2]
    splits = [W_A, 2 * W_A, 3 * W_A, 4 * W_A, 5 * W_A,
              5 * W_A + 2 * W_B, 5 * W_A + 3 * W_B, 5 * W_A + 4 * W_B, 5 * W_A + 5 * W_B]
    a_q, a_ff, a_fb, a_i, a_z, b_qk, b_v, b_o, b_z, b_g = jnp.split(u, splits, axis=-1)
    hg_f0, hg_b0, ml_f0, ml_b0 = states

    q_a = heads(jax.nn.silu(a_q), HG_HEADS)
    v_a = heads(a_i, HG_HEADS)
    logf_f, k_f = hgrn2_gates(a_ff, lb[0])
    logf_b, k_b = hgrn2_gates(a_fb, lb[1])
    o_f, hg_f = hgrn2_scan(q_a, heads(k_f, HG_HEADS), v_a, heads(logf_f, HG_HEADS), hg_f0)
    o_b, hg_b = hgrn2_scan(flip(q_a), flip(heads(k_b, HG_HEADS)), flip(v_a), flip(heads(logf_b, HG_HEADS)), hg_b0)
    o_a = rms_norm(o_f + flip(o_b)) * hg_norm_l.astype(jnp.float32).reshape(HG_HEADS, HG_DV)
    y_a = o_a.reshape(bsz, t, W_A) * jax.nn.silu(a_z)

    qk = short_conv(b_qk, conv_w_l, conv_b_l, grid)
    q_m, k_m = jnp.split(qk, 2, axis=-1)
    q_m = heads(q_m, ML_HEADS)
    k_m = heads(k_m, ML_HEADS) * (ML_DH ** -0.5)
    v_m = heads(b_v, ML_HEADS)
    g = b_g.reshape(bsz, t, 4, ML_HEADS) + gate_b_l.astype(jnp.float32)
    log_i_f, log_i_b = g[:, :, 0], g[:, :, 1]
    log_f_f, log_f_b = jax.nn.log_sigmoid(g[:, :, 2]), jax.nn.log_sigmoid(g[:, :, 3])
    h_f, ml_f = mlstm_scan(q_m, k_m, v_m, log_i_f, log_f_f, ml_f0)
    h_b, ml_b = mlstm_scan(flip(q_m), flip(k_m), flip(v_m), flip(log_i_b), flip(log_f_b), ml_b0)
    h = head_layer_norm(h_f + flip(h_b)) * ml_norm_l.astype(jnp.float32).reshape(ML_HEADS, ML_DH)
    y_b = h.reshape(bsz, t, W_B) * jax.nn.sigmoid(b_o) * jax.nn.silu(b_z)

    return jnp.concatenate([y_a, y_b], axis=-1), (hg_f, hg_b, ml_f, ml_b)


def _fwd_setup_inputs(seed: int = 0) -> dict:
    key = jax.random.key(seed)
    ks = jax.random.split(key, 20)
    f32 = jnp.float32
    x = jax.random.normal(ks[0], (BATCH, SEQ, D_MODEL), f32)
    c = jax.random.normal(ks[1], (BATCH, D_MODEL), f32)
    ctx = jax.random.normal(ks[2], (BATCH, CTX_LEN, D_MODEL), f32)
    c_ctx = jax.random.normal(ks[3], (D_MODEL,), f32)
    w_mod = jax.random.normal(ks[4], (DEPTH, D_MODEL, 3 * D_MODEL), f32) * (0.5 * D_MODEL ** -0.5)
    b_mod = jax.random.normal(ks[5], (DEPTH, 3 * D_MODEL), f32) * 0.02
    w_in = jax.random.normal(ks[6], (DEPTH, D_MODEL, N_IN), f32) * (D_MODEL ** -0.5)
    conv_w = jax.random.normal(ks[7], (DEPTH, CONV_K, CONV_K, 2 * W_B), f32) * (1.0 / CONV_K)
    conv_b = jax.random.normal(ks[8], (DEPTH, 2 * W_B), f32) * 0.02
    hg_lb = jax.random.normal(ks[9], (2, DEPTH + 1, W_A), f32) * 0.1
    ig_b = jax.random.normal(ks[10], (DEPTH, 2, ML_HEADS), f32) * 0.1
    fg_b = jnp.linspace(3.0, 6.0, ML_HEADS, dtype=f32)[None, None, :] + 0.1 * jax.random.normal(ks[11], (DEPTH, 2, ML_HEADS), f32)
    ml_gate_b = jnp.concatenate([ig_b, fg_b], axis=1)
    hg_norm_w = 1.0 + 0.02 * jax.random.normal(ks[12], (DEPTH, W_A), f32)
    ml_norm_w = 1.0 + 0.02 * jax.random.normal(ks[13], (DEPTH, W_B), f32)
    w_out = jax.random.normal(ks[14], (DEPTH, D_INNER, D_MODEL), f32) * (BETA * D_INNER ** -0.5)
    ln_g = 1.0 + 0.02 * jax.random.normal(ks[15], (DEPTH, D_MODEL), f32)
    ln_b = 0.02 * jax.random.normal(ks[16], (DEPTH, D_MODEL), f32)
    return {"x": x, "c": c, "ctx": ctx, "c_ctx": c_ctx, "w_mod": w_mod, "b_mod": b_mod, "w_in": w_in,
            "conv_w": conv_w, "conv_b": conv_b, "hg_lb": hg_lb, "ml_gate_b": ml_gate_b,
            "hg_norm_w": hg_norm_w, "ml_norm_w": ml_norm_w, "w_out": w_out, "ln_g": ln_g, "ln_b": ln_b}


def _fwd_reference(x, c, ctx, c_ctx, w_mod, b_mod, w_in, conv_w, conv_b, hg_lb, ml_gate_b,
              hg_norm_w, ml_norm_w, w_out, ln_g, ln_b):
    lower = jnp.cumsum(jax.nn.softmax(hg_lb.astype(jnp.float32), axis=1), axis=1)
    states0 = zero_states(ctx.shape[0])
    for layer in range(DEPTH):
        mod_x = jax.nn.silu(c) @ w_mod[layer] + b_mod[layer]
        mod_c = jax.nn.silu(c_ctx) @ w_mod[layer] + b_mod[layer]
        shift_x, scale_x, gate_x = jnp.split(mod_x[:, None, :], 3, axis=-1)
        shift_c, scale_c, gate_c = jnp.split(mod_c, 3, axis=-1)
        lp = (lower[:, layer], conv_w[layer], conv_b[layer], ml_gate_b[layer], hg_norm_w[layer], ml_norm_w[layer])
        u_ctx = modulate(ctx, shift_c, scale_c) @ w_in[layer]
        y_ctx, ctx_states = mix(u_ctx, False, states0, *lp)
        u_x = modulate(x, shift_x, scale_x) @ w_in[layer]
        y_x, _ = mix(u_x, True, ctx_states, *lp)
        x = layer_norm(ALPHA * x + gate_x * (y_x.astype(x.dtype) @ w_out[layer]), ln_g[layer], ln_b[layer])
        if layer < DEPTH - 1:
            ctx = layer_norm(ALPHA * ctx + gate_c * (y_ctx.astype(ctx.dtype) @ w_out[layer]), ln_g[layer], ln_b[layer])
    return x


import jax as _jax
import jax.numpy as _jnp

TWIN_FORMAT = 'train_step'
FWD_PARAMS = ['x', 'c', 'ctx', 'c_ctx', 'w_mod', 'b_mod', 'w_in', 'conv_w', 'conv_b', 'hg_lb', 'ml_gate_b', 'hg_norm_w', 'ml_norm_w', 'w_out', 'ln_g', 'ln_b']
TWIN_WEIGHTS = ['c_ctx', 'w_mod', 'b_mod', 'w_in', 'conv_w', 'conv_b', 'hg_lb', 'ml_gate_b', 'hg_norm_w', 'ml_norm_w', 'w_out', 'ln_g', 'ln_b']
TWIN_DIFF_INPUT = 'x'
TWIN_INPUTS = ['x', 'c', 'ctx', 'c_ctx', 'w_mod', 'b_mod', 'w_in', 'conv_w', 'conv_b', 'hg_lb', 'ml_gate_b', 'hg_norm_w', 'ml_norm_w', 'w_out', 'ln_g', 'ln_b', 'loss_target', 'm_c_ctx', 'm_w_mod', 'm_b_mod', 'm_w_in', 'm_conv_w', 'm_conv_b', 'm_hg_lb', 'm_ml_gate_b', 'm_hg_norm_w', 'm_ml_norm_w', 'm_w_out', 'm_ln_g', 'm_ln_b', 'v_c_ctx', 'v_w_mod', 'v_b_mod', 'v_w_in', 'v_conv_w', 'v_conv_b', 'v_hg_lb', 'v_ml_gate_b', 'v_hg_norm_w', 'v_ml_norm_w', 'v_w_out', 'v_ln_g', 'v_ln_b']
TWIN_OUTPUTS = ['loss', 'grad_x', 'grad_c_ctx', 'grad_w_mod', 'grad_b_mod', 'grad_w_in', 'grad_conv_w', 'grad_conv_b', 'grad_hg_lb', 'grad_ml_gate_b', 'grad_hg_norm_w', 'grad_ml_norm_w', 'grad_w_out', 'grad_ln_g', 'grad_ln_b', 'delta_c_ctx', 'delta_w_mod', 'delta_b_mod', 'delta_w_in', 'delta_conv_w', 'delta_conv_b', 'delta_hg_lb', 'delta_ml_gate_b', 'delta_hg_norm_w', 'delta_ml_norm_w', 'delta_w_out', 'delta_ln_g', 'delta_ln_b', 'new_m_c_ctx', 'new_m_w_mod', 'new_m_b_mod', 'new_m_w_in', 'new_m_conv_w', 'new_m_conv_b', 'new_m_hg_lb', 'new_m_ml_gate_b', 'new_m_hg_norm_w', 'new_m_ml_norm_w', 'new_m_w_out', 'new_m_ln_g', 'new_m_ln_b', 'new_v_c_ctx', 'new_v_w_mod', 'new_v_b_mod', 'new_v_w_in', 'new_v_conv_w', 'new_v_conv_b', 'new_v_hg_lb', 'new_v_ml_gate_b', 'new_v_hg_norm_w', 'new_v_ml_norm_w', 'new_v_w_out', 'new_v_ln_g', 'new_v_ln_b']
TWIN_LEAF_KINDS = {'loss': 'loss', 'grad_x': 'grad_x', 'grad_c_ctx': 'grad_w', 'grad_w_mod': 'grad_w', 'grad_b_mod': 'grad_w', 'grad_w_in': 'grad_w', 'grad_conv_w': 'grad_w', 'grad_conv_b': 'grad_w', 'grad_hg_lb': 'grad_w', 'grad_ml_gate_b': 'grad_w', 'grad_hg_norm_w': 'grad_w', 'grad_ml_norm_w': 'grad_w', 'grad_w_out': 'grad_w', 'grad_ln_g': 'grad_w', 'grad_ln_b': 'grad_w', 'delta_c_ctx': 'delta_w', 'delta_w_mod': 'delta_w', 'delta_b_mod': 'delta_w', 'delta_w_in': 'delta_w', 'delta_conv_w': 'delta_w', 'delta_conv_b': 'delta_w', 'delta_hg_lb': 'delta_w', 'delta_ml_gate_b': 'delta_w', 'delta_hg_norm_w': 'delta_w', 'delta_ml_norm_w': 'delta_w', 'delta_w_out': 'delta_w', 'delta_ln_g': 'delta_w', 'delta_ln_b': 'delta_w', 'new_m_c_ctx': 'new_m', 'new_m_w_mod': 'new_m', 'new_m_b_mod': 'new_m', 'new_m_w_in': 'new_m', 'new_m_conv_w': 'new_m', 'new_m_conv_b': 'new_m', 'new_m_hg_lb': 'new_m', 'new_m_ml_gate_b': 'new_m', 'new_m_hg_norm_w': 'new_m', 'new_m_ml_norm_w': 'new_m', 'new_m_w_out': 'new_m', 'new_m_ln_g': 'new_m', 'new_m_ln_b': 'new_m', 'new_v_c_ctx': 'new_v', 'new_v_w_mod': 'new_v', 'new_v_b_mod': 'new_v', 'new_v_w_in': 'new_v', 'new_v_conv_w': 'new_v', 'new_v_conv_b': 'new_v', 'new_v_hg_lb': 'new_v', 'new_v_ml_gate_b': 'new_v', 'new_v_hg_norm_w': 'new_v', 'new_v_ml_norm_w': 'new_v', 'new_v_w_out': 'new_v', 'new_v_ln_g': 'new_v', 'new_v_ln_b': 'new_v'}


def _forward(args):
    return _fwd_reference(*[args[k] for k in FWD_PARAMS])


def _output_shape():
    def fwd():
        inp = _fwd_setup_inputs(0)
        return _fwd_reference(*[inp[k] for k in FWD_PARAMS])
    out = _jax.eval_shape(fwd)
    return out.shape, out.dtype

N_MICROBATCH = 1
ADAM_LR = 0.001
ADAM_B1 = 0.9
ADAM_B2 = 0.999
ADAM_EPS = 1e-08
ADAM_WD = 0.01
ADAM_STEP = 10
PER_EXAMPLE_BATCH_AXIS = {'x': 0, 'c': 0, 'ctx': 0, 'loss_target': 0}
SHARED_INPUTS = []
_WEIGHT_DTYPES = {'c_ctx': _jnp.float32, 'w_mod': _jnp.float32, 'b_mod': _jnp.float32, 'w_in': _jnp.float32, 'conv_w': _jnp.float32, 'conv_b': _jnp.float32, 'hg_lb': _jnp.float32, 'ml_gate_b': _jnp.float32, 'hg_norm_w': _jnp.float32, 'ml_norm_w': _jnp.float32, 'w_out': _jnp.float32, 'ln_g': _jnp.float32, 'ln_b': _jnp.float32}
MOMENT_SCALE = {'c_ctx': 1.077679e-03, 'w_mod': 1.100705e-02, 'b_mod': 1.861943e-02, 'w_in': 4.730752e-03, 'conv_w': 1.404237e-03, 'conv_b': 1.469775e-03, 'hg_lb': 4.090843e-04, 'ml_gate_b': 1.274546e-02, 'hg_norm_w': 9.216450e-03, 'ml_norm_w': 5.132264e-03, 'w_out': 1.224268e-02, 'ln_g': 1.597951e+01, 'ln_b': 3.108284e-01}


def _to_microbatches(a, axis):
    t = _jnp.moveaxis(a, axis, 0)
    t = t.reshape((N_MICROBATCH, t.shape[0] // N_MICROBATCH) + t.shape[1:])
    return _jnp.moveaxis(t, 1, axis + 1)


def setup_inputs(seed: int = 0) -> dict:
    inp = _fwd_setup_inputs(seed)
    key = _jax.random.fold_in(_jax.random.key(seed), 7919)
    shape, _ = _output_shape()
    out = dict(inp)
    out["loss_target"] = _jax.random.normal(_jax.random.fold_in(key, 0), shape, _jnp.float32)
    for i, name in enumerate(TWIN_WEIGHTS):
        w = inp[name].astype(_jnp.float32)
        if MOMENT_SCALE is None:
            s = _jnp.sqrt(_jnp.mean(_jnp.square(w)) + 1e-30)
        else:
            s = MOMENT_SCALE[name]
        km, kv = _jax.random.split(_jax.random.fold_in(key, i + 1))
        out[name] = w
        out["m_" + name] = s * _jax.random.normal(km, w.shape, _jnp.float32)
        out["v_" + name] = (s * s) * _jax.random.uniform(kv, w.shape, _jnp.float32, 0.5, 1.5)
    if N_MICROBATCH > 1:
        for name, axis in PER_EXAMPLE_BATCH_AXIS.items():
            out[name] = _to_microbatches(out[name], axis)
    return {'x': out['x'], 'c': out['c'], 'ctx': out['ctx'], 'c_ctx': out['c_ctx'], 'w_mod': out['w_mod'], 'b_mod': out['b_mod'], 'w_in': out['w_in'], 'conv_w': out['conv_w'], 'conv_b': out['conv_b'], 'hg_lb': out['hg_lb'], 'ml_gate_b': out['ml_gate_b'], 'hg_norm_w': out['hg_norm_w'], 'ml_norm_w': out['ml_norm_w'], 'w_out': out['w_out'], 'ln_g': out['ln_g'], 'ln_b': out['ln_b'], 'loss_target': out['loss_target'], 'm_c_ctx': out['m_c_ctx'], 'm_w_mod': out['m_w_mod'], 'm_b_mod': out['m_b_mod'], 'm_w_in': out['m_w_in'], 'm_conv_w': out['m_conv_w'], 'm_conv_b': out['m_conv_b'], 'm_hg_lb': out['m_hg_lb'], 'm_ml_gate_b': out['m_ml_gate_b'], 'm_hg_norm_w': out['m_hg_norm_w'], 'm_ml_norm_w': out['m_ml_norm_w'], 'm_w_out': out['m_w_out'], 'm_ln_g': out['m_ln_g'], 'm_ln_b': out['m_ln_b'], 'v_c_ctx': out['v_c_ctx'], 'v_w_mod': out['v_w_mod'], 'v_b_mod': out['v_b_mod'], 'v_w_in': out['v_w_in'], 'v_conv_w': out['v_conv_w'], 'v_conv_b': out['v_conv_b'], 'v_hg_lb': out['v_hg_lb'], 'v_ml_gate_b': out['v_ml_gate_b'], 'v_hg_norm_w': out['v_hg_norm_w'], 'v_ml_norm_w': out['v_ml_norm_w'], 'v_w_out': out['v_w_out'], 'v_ln_g': out['v_ln_g'], 'v_ln_b': out['v_ln_b']}


def _loss(weights, diff, rest, loss_target):
    with _jax.named_scope("forward"):
        args = {**rest, TWIN_DIFF_INPUT: diff, **{k: w.astype(_WEIGHT_DTYPES[k]) for k, w in weights.items()}}
        y = _forward(args)
    with _jax.named_scope("loss_head"):
        err = _jnp.square(y.astype(_jnp.float32) - loss_target)
        return 0.5 * _jnp.sum(_jnp.mean(err, axis=-1)) if err.ndim else 0.5 * err


def _adamw(w, g, m, v):
    m = ADAM_B1 * m + (1.0 - ADAM_B1) * g
    v = ADAM_B2 * v + (1.0 - ADAM_B2) * _jnp.square(g)
    m_hat = m / (1.0 - ADAM_B1 ** ADAM_STEP)
    v_hat = v / (1.0 - ADAM_B2 ** ADAM_STEP)
    delta = -ADAM_LR * (m_hat / (_jnp.sqrt(v_hat) + ADAM_EPS) + ADAM_WD * w)
    return delta, m, v


def reference(x, c, ctx, c_ctx, w_mod, b_mod, w_in, conv_w, conv_b, hg_lb, ml_gate_b, hg_norm_w, ml_norm_w, w_out, ln_g, ln_b, loss_target, m_c_ctx, m_w_mod, m_b_mod, m_w_in, m_conv_w, m_conv_b, m_hg_lb, m_ml_gate_b, m_hg_norm_w, m_ml_norm_w, m_w_out, m_ln_g, m_ln_b, v_c_ctx, v_w_mod, v_b_mod, v_w_in, v_conv_w, v_conv_b, v_hg_lb, v_ml_gate_b, v_hg_norm_w, v_ml_norm_w, v_w_out, v_ln_g, v_ln_b):
    given = dict(x=x, c=c, ctx=ctx, c_ctx=c_ctx, w_mod=w_mod, b_mod=b_mod, w_in=w_in, conv_w=conv_w, conv_b=conv_b, hg_lb=hg_lb, ml_gate_b=ml_gate_b, hg_norm_w=hg_norm_w, ml_norm_w=ml_norm_w, w_out=w_out, ln_g=ln_g, ln_b=ln_b, loss_target=loss_target, m_c_ctx=m_c_ctx, m_w_mod=m_w_mod, m_b_mod=m_b_mod, m_w_in=m_w_in, m_conv_w=m_conv_w, m_conv_b=m_conv_b, m_hg_lb=m_hg_lb, m_ml_gate_b=m_ml_gate_b, m_hg_norm_w=m_hg_norm_w, m_ml_norm_w=m_ml_norm_w, m_w_out=m_w_out, m_ln_g=m_ln_g, m_ln_b=m_ln_b, v_c_ctx=v_c_ctx, v_w_mod=v_w_mod, v_b_mod=v_b_mod, v_w_in=v_w_in, v_conv_w=v_conv_w, v_conv_b=v_conv_b, v_hg_lb=v_hg_lb, v_ml_gate_b=v_ml_gate_b, v_hg_norm_w=v_hg_norm_w, v_ml_norm_w=v_ml_norm_w, v_w_out=v_w_out, v_ln_g=v_ln_g, v_ln_b=v_ln_b)
    weights = {n: given[n] for n in TWIN_WEIGHTS}
    shared = {n: given[n] for n in SHARED_INPUTS}
    per_example = {n: given[n] for n in ['x', 'c', 'ctx']}
    grad_fn = _jax.value_and_grad(_loss, argnums=(0, 1))

    def one_microbatch(ex, loss_target):
        ex = dict(ex)
        diff = ex.pop(TWIN_DIFF_INPUT)
        return grad_fn(weights, diff, {**shared, **ex}, loss_target)

    if N_MICROBATCH == 1:
        loss, (grad_w, grad_x) = one_microbatch(per_example, given["loss_target"])
    else:
        def body(carry, xs):
            loss_sum, grad_sum = carry
            l_k, (gw_k, gx_k) = one_microbatch(xs[0], xs[1])
            with _jax.named_scope("update"):
                return (loss_sum + l_k, _jax.tree.map(_jnp.add, grad_sum, gw_k)), gx_k

        init = (_jnp.zeros((), _jnp.float32), _jax.tree.map(_jnp.zeros_like, weights))
        (loss, grad_w), grad_x = _jax.lax.scan(body, init, (per_example, given["loss_target"]))
    with _jax.named_scope("update"):
        delta_w, new_m, new_v = {}, {}, {}
        for n in TWIN_WEIGHTS:
            delta_w[n], new_m[n], new_v[n] = _adamw(weights[n], grad_w[n], given["m_" + n], given["v_" + n])
    return (loss, grad_x, *[grad_w[n] for n in TWIN_WEIGHTS], *[delta_w[n] for n in TWIN_WEIGHTS],
            *[new_m[n] for n in TWIN_WEIGHTS], *[new_v[n] for n in TWIN_WEIGHTS])
```

```python
import functools
import math

import jax
import jax.numpy as jnp
from jax import lax
from jax.experimental import pallas as pl
from jax.experimental.pallas import tpu as pltpu

F32 = jnp.float32
BF16 = jnp.bfloat16
HIGHEST = lax.Precision.HIGHEST
MESH = pl.DeviceIdType.MESH

CHUNK = 64
GRID_W = 64
HG_DK = 128
LANE = 128
SUBLANE_BF16 = 16
ALPHA = 2.0 ** 0.25
LN_EPS = 1e-5
NORM_EPS = 1e-6
ADAM_LR = 0.001
ADAM_B1 = 0.9
ADAM_B2 = 0.999
ADAM_EPS = 1e-08
ADAM_WD = 0.01
ADAM_STEP = 10
VMEM_LIMIT = 56 * 1024 * 1024
N_CHIPS = 4
N_DEV = 8


def _params(sem=None):
    return pltpu.CompilerParams(dimension_semantics=sem, vmem_limit_bytes=VMEM_LIMIT)


def _largest_divisor(n, cap, multiple=1):
    best = None
    for d in range(multiple, min(n, cap) + 1, multiple):
        if n % d == 0:
            best = d
    assert best is not None, (n, cap, multiple)
    return best


def _sigmoid(x):
    return jax.nn.sigmoid(x)


def _silu(x):
    return x * jax.nn.sigmoid(x)


def _dot(a, b, dims, precision=None):
    return lax.dot_general(a, b, (dims, ((), ())), precision=precision, preferred_element_type=F32)


def _nn(a, b, precision=None):
    return _dot(a, b, ((1,), (0,)), precision)


def _nt(a, b, precision=None):
    return _dot(a, b, ((1,), (1,)), precision)


def _tn(a, b, precision=None):
    return _dot(a, b, ((0,), (0,)), precision)


def _visible(rev):
    r = lax.broadcasted_iota(jnp.int32, (CHUNK, CHUNK), 0)
    c = lax.broadcasted_iota(jnp.int32, (CHUNK, CHUNK), 1)
    return (r <= c) if rev else (r >= c)


def _hg_chunk(states, aq, af, ai, lb0, lb1, rev):
    n_heads = len(states)
    lb = _sigmoid(lb0 - lb1)
    f = lb + (1.0 - lb) * _sigmoid(af)
    g = jnp.log(f)
    k = 1.0 - f
    q = _silu(aq)
    vis = _visible(rev)
    b = _nn(vis.astype(F32), g, HIGHEST)
    last = 0 if rev else CHUNK - 1
    b_end = b[last:last + 1]
    b_mid = b[CHUNK // 2:CHUNK // 2 + 1]
    q_inter = q * jnp.exp(b)
    q_intra = q * jnp.exp(b - b_mid)
    k_intra = k * jnp.exp(b_mid - b)
    k_dec = k * jnp.exp(b_end - b)
    e_end = jnp.exp(b_end)
    new_states, outs = [], []
    for h in range(n_heads):
        sl = slice(h * HG_DK, (h + 1) * HG_DK)
        s_t = states[h]
        scores = jnp.where(vis, _nt(q_intra[:, sl], k_intra[:, sl]), 0.0)
        outs.append(_nt(q_inter[:, sl], s_t) + _nn(scores, ai[:, sl]))
        new_states.append(e_end[:, sl] * s_t + _tn(ai[:, sl], k_dec[:, sl]))
    return new_states, jnp.concatenate(outs, axis=1)


def _ml_chunk(state, q, k, v, g, gb, rev, d):
    cms, nvs, mbs = state
    n_heads = len(cms)
    dh = q.shape[1] // n_heads
    ga = g + gb
    log_f_all = jax.nn.log_sigmoid(ga)
    vis = _visible(rev)
    b_all = _nn(vis.astype(F32), log_f_all, HIGHEST)
    last = 0 if rev else CHUNK - 1
    k = k * (dh ** -0.5)
    new_c, new_n, new_m, outs = [], [], [], []
    for h in range(n_heads):
        ci = d * n_heads + h
        cf = (2 + d) * n_heads + h
        sl = slice(h * dh, (h + 1) * dh)
        qh, kh, vh = q[:, sl], k[:, sl], v[:, sl]
        li = ga[:, ci:ci + 1]
        b = b_all[:, cf:cf + 1]
        m = mbs[h][:, 0:1]
        row = jnp.transpose(li - b)
        log_w = jnp.where(vis, b + row, -jnp.inf)
        m_inter = b + m
        m_t = jnp.maximum(m_inter, jnp.max(log_w, axis=-1, keepdims=True))
        w_inter = jnp.exp(m_inter - m_t)
        w_qk = jnp.exp(log_w - m_t) * _nt(qh, kh)
        num = w_inter * _nt(qh, cms[h]) + _nn(w_qk, vh)
        den = w_inter * jnp.sum(qh * nvs[h], axis=-1, keepdims=True) + jnp.sum(w_qk, axis=-1, keepdims=True)
        outs.append(num / jnp.maximum(jnp.abs(den), jnp.exp(-m_t)))
        m_new = m_t[last:last + 1]
        b_end = b[last:last + 1]
        w_s = jnp.exp(b_end - b + li - m_new)
        decay = jnp.exp(b_end + m - m_new)
        new_c.append(decay * cms[h] + _tn(w_s * vh, kh))
        new_n.append(decay * nvs[h] + jnp.sum(w_s * kh, axis=0, keepdims=True))
        new_m.append(jnp.broadcast_to(m_new, (1, LANE)))
    return (new_c, new_n, new_m), jnp.concatenate(outs, axis=1)


def _post_fn(o_f, o_b, az, h_f, h_b, bo, bz, wa, wb, n_hg, n_ml):
    o = o_f + o_b
    parts = []
    for h in range(n_hg):
        s = o[:, h * HG_DK:(h + 1) * HG_DK]
        parts.append(s * lax.rsqrt(jnp.mean(s * s, axis=-1, keepdims=True) + NORM_EPS))
    y_a = jnp.concatenate(parts, axis=1) * wa * _silu(az)
    hh = h_f + h_b
    dh = hh.shape[1] // n_ml
    parts = []
    for h in range(n_ml):
        s = hh[:, h * dh:(h + 1) * dh]
        mu = jnp.mean(s, axis=-1, keepdims=True)
        sc = s - mu
        parts.append(sc * lax.rsqrt(jnp.mean(sc * sc, axis=-1, keepdims=True) + NORM_EPS))
    y_b = jnp.concatenate(parts, axis=1) * wb * _sigmoid(bo) * _silu(bz)
    return jnp.concatenate([y_a, y_b], axis=1)


def _chip_of(dev):
    return 2 * dev[0] + dev[1]


def _index_of(dev):
    return 4 * dev[0] + 2 * dev[1] + dev[2]


def _exchange(name, srcs, out_shapes, transfers, local_copies=()):
    n_in, n_out, n_t, n_l = len(srcs), len(out_shapes), len(transfers), len(local_copies)

    def body(*refs):
        ins, outs = refs[:n_in], refs[n_in:n_in + n_out]
        send_sems, recv_sems, local_sems = refs[n_in + n_out:]
        me = (lax.axis_index("x"), lax.axis_index("y"), lax.axis_index("c"))

        def pick(ref, fn, *who):
            return ref if fn is None else ref.at[fn(*who)]

        sends, recvs, locs = [], [], []
        for t, (mask, si, sfn, di, dfn) in enumerate(transfers):
            peer = tuple(1 - p if flip else p for p, flip in zip(me, mask))
            sends.append(pltpu.make_async_remote_copy(
                src_ref=pick(ins[si], sfn, me, peer), dst_ref=pick(outs[di], dfn, me, peer),
                send_sem=send_sems.at[t], recv_sem=recv_sems.at[t], device_id=peer, device_id_type=MESH))
            landing = pick(outs[di], dfn, peer, me)
            recvs.append(pltpu.make_async_remote_copy(
                src_ref=landing, dst_ref=landing,
                send_sem=send_sems.at[t], recv_sem=recv_sems.at[t], device_id=peer, device_id_type=MESH))
        for l, (si, sfn, di, dfn) in enumerate(local_copies):
            locs.append(pltpu.make_async_copy(pick(ins[si], sfn, me), pick(outs[di], dfn, me), local_sems.at[l]))
        for cp in locs + sends:
            cp.start()
        for cp in recvs:
            cp.wait_recv()
        for cp in sends:
            cp.wait_send()
        for cp in locs:
            cp.wait()

    hbm = pl.BlockSpec(memory_space=pltpu.HBM)
    return pl.pallas_call(
        body, name=name, out_shape=tuple(out_shapes),
        in_specs=[hbm] * n_in, out_specs=tuple([hbm] * n_out),
        scratch_shapes=[pltpu.SemaphoreType.DMA((n_t,)), pltpu.SemaphoreType.DMA((n_t,)),
                        pltpu.SemaphoreType.DMA((max(n_l, 1),))],
    )(*srcs)


ALL_MASKS = [(mx, my, mc) for mx in (0, 1) for my in (0, 1) for mc in (0, 1)][1:]
CHIP_MASKS = [(1, 0, 0), (0, 1, 0), (1, 1, 0)]
SIBLING_MASK = (0, 0, 1)


def _all_gather8(name, v):
    out = jax.ShapeDtypeStruct((N_DEV,) + v.shape, v.dtype)
    slot = lambda sender, receiver: _index_of(sender)
    transfers = [(mask, 0, None, 0, slot) for mask in ALL_MASKS]
    return _exchange(name, [v], [out], transfers, [(0, None, 0, lambda me: _index_of(me))])[0]


def _all_gather_chips(name, arrays):
    outs = [jax.ShapeDtypeStruct((N_CHIPS,) + a.shape, a.dtype) for a in arrays]
    slot = lambda sender, receiver: _chip_of(sender)
    transfers = [(mask, i, None, i, slot) for i in range(len(arrays)) for mask in CHIP_MASKS]
    local = [(i, None, i, lambda me: _chip_of(me)) for i in range(len(arrays))]
    return _exchange(name, arrays, outs, transfers, local)


def _sibling_swap(name, arrays):
    outs = [jax.ShapeDtypeStruct(a.shape, a.dtype) for a in arrays]
    return _exchange(name, arrays, outs, [(SIBLING_MASK, i, None, i, None) for i in range(len(arrays))])


def _chip_scatter(name, arrays):
    outs = [jax.ShapeDtypeStruct(a.shape, a.dtype) for a in arrays]
    transfers = [(mask, i, lambda s, r: _chip_of(r), i, lambda s, r: _chip_of(s))
                 for i in range(len(arrays)) for mask in CHIP_MASKS]
    local = [(i, lambda me: _chip_of(me), i, lambda me: _chip_of(me)) for i in range(len(arrays))]
    return _exchange(name, arrays, outs, transfers, local)


def _sibling_join(name, arrays):
    outs = [jax.ShapeDtypeStruct((2,) + a.shape, a.dtype) for a in arrays]
    transfers = [(SIBLING_MASK, i, None, i, lambda s, r: s[2]) for i in range(len(arrays))]
    local = [(i, None, i, lambda me: me[2]) for i in range(len(arrays))]
    return _exchange(name, arrays, outs, transfers, local)


def _mm_nn(name, a, b, tm, tn, out_dtype):
    m, k = a.shape
    n = b.shape[1]

    def body(a_ref, b_ref, o_ref):
        o_ref[...] = _nn(a_ref[...], b_ref[...]).astype(out_dtype)

    return pl.pallas_call(
        body, name=name, grid=(n // tn, m // tm),
        in_specs=[pl.BlockSpec((tm, k), lambda j, i: (i, 0)), pl.BlockSpec((k, tn), lambda j, i: (0, j))],
        out_specs=pl.BlockSpec((tm, tn), lambda j, i: (i, j)),
        out_shape=jax.ShapeDtypeStruct((m, n), out_dtype),
        compiler_params=_params(("parallel", "parallel")),
    )(a, b)


def _mm_nt(name, a, b, tm, tk):
    m, kc = a.shape
    n = b.shape[0]

    def body(a_ref, b_ref, o_ref):
        @pl.when(pl.program_id(1) == 0)
        def _():
            o_ref[...] = jnp.zeros_like(o_ref)
        o_ref[...] += _nt(a_ref[...], b_ref[...])

    return pl.pallas_call(
        body, name=name, grid=(m // tm, kc // tk),
        in_specs=[pl.BlockSpec((tm, tk), lambda i, kk: (i, kk)), pl.BlockSpec((n, tk), lambda i, kk: (0, kk))],
        out_specs=pl.BlockSpec((tm, n), lambda i, kk: (i, 0)),
        out_shape=jax.ShapeDtypeStruct((m, n), F32),
        compiler_params=_params(("parallel", "arbitrary")),
    )(a, b)


def _mm_tn(name, a, b, tk, tn):
    kr, m = a.shape
    n = b.shape[1]

    def body(a_ref, b_ref, o_ref):
        @pl.when(pl.program_id(1) == 0)
        def _():
            o_ref[...] = jnp.zeros_like(o_ref)
        o_ref[...] += _tn(a_ref[...], b_ref[...])

    return pl.pallas_call(
        body, name=name, grid=(n // tn, kr // tk),
        in_specs=[pl.BlockSpec((tk, m), lambda j, kk: (kk, 0)), pl.BlockSpec((tk, tn), lambda j, kk: (kk, j))],
        out_specs=pl.BlockSpec((m, tn), lambda j, kk: (0, j)),
        out_shape=jax.ShapeDtypeStruct((m, n), F32),
        compiler_params=_params(("parallel", "arbitrary")),
    )(a, b)


def _modulate_fwd(xc, prm, t_rows, tm):
    r, dm = xc.shape
    first_ctx = t_rows // tm

    def body(x_ref, p_ref, h_ref):
        x = x_ref[...]
        mu = jnp.mean(x, axis=-1, keepdims=True)
        xm = x - mu
        n = xm * lax.rsqrt(jnp.mean(xm * xm, axis=-1, keepdims=True) + LN_EPS)
        h_ref[...] = (n * (1.0 + p_ref[0, 1:2, :]) + p_ref[0, 0:1, :]).astype(BF16)

    return pl.pallas_call(
        body, name="modulate_fwd", grid=(r // tm,),
        in_specs=[pl.BlockSpec((tm, dm), lambda i: (i, 0)),
                  pl.BlockSpec((1, 8, dm), lambda i: ((i >= first_ctx).astype(jnp.int32), 0, 0))],
        out_specs=pl.BlockSpec((tm, dm), lambda i: (i, 0)),
        out_shape=jax.ShapeDtypeStruct((r, dm), BF16),
        compiler_params=_params(("parallel",)),
    )(xc, prm)


def _modulate_bwd(xc, dh, prm, gx_direct, t_rows, tm):
    r, dm = xc.shape
    first_ctx = t_rows // tm
    cls = lambda i: (i >= first_ctx).astype(jnp.int32)

    def body(x_ref, dh_ref, p_ref, gd_ref, gx_ref, acc_ref):
        i = pl.program_id(0)

        @pl.when((i == 0) | (i == first_ctx))
        def _():
            acc_ref[...] = jnp.zeros_like(acc_ref)

        x = x_ref[...]
        dh_v = dh_ref[...]
        mu = jnp.mean(x, axis=-1, keepdims=True)
        xm = x - mu
        rstd = lax.rsqrt(jnp.mean(xm * xm, axis=-1, keepdims=True) + LN_EPS)
        n = xm * rstd
        acc_ref[0, 0:1, :] += jnp.sum(dh_v, axis=0, keepdims=True)
        acc_ref[0, 1:2, :] += jnp.sum(dh_v * n, axis=0, keepdims=True)
        dn = dh_v * (1.0 + p_ref[0, 1:2, :])
        dx = rstd * (dn - jnp.mean(dn, axis=-1, keepdims=True) - n * jnp.mean(dn * n, axis=-1, keepdims=True))
        gx_ref[...] = dx + gd_ref[...]

    return pl.pallas_call(
        body, name="modulate_bwd", grid=(r // tm,),
        in_specs=[pl.BlockSpec((tm, dm), lambda i: (i, 0)), pl.BlockSpec((tm, dm), lambda i: (i, 0)),
                  pl.BlockSpec((1, 8, dm), lambda i: (cls(i), 0, 0)),
                  pl.BlockSpec((tm, dm), lambda i: (jnp.minimum(i, first_ctx - 1), 0))],
        out_specs=(pl.BlockSpec((tm, dm), lambda i: (i, 0)), pl.BlockSpec((1, 8, dm), lambda i: (cls(i), 0, 0))),
        out_shape=(jax.ShapeDtypeStruct((r, dm), F32), jax.ShapeDtypeStruct((2, 8, dm), F32)),
        compiler_params=_params(("arbitrary",)),
    )(xc, dh, prm, gx_direct)


def _conv_parts(t_rows, c_rows):
    return ((0, t_rows, t_rows // GRID_W, GRID_W), (t_rows, c_rows, 1, c_rows))


def _tap_valid(n, rows_g, width_g, a, b, lanes):
    t = lax.broadcasted_iota(jnp.int32, (n, lanes), 0)
    shift = width_g.bit_length() - 1
    assert 1 << shift == width_g
    rr = jnp.right_shift(t, shift) + (a - 1)
    cc = jnp.bitwise_and(t, width_g - 1) + (b - 1)
    return (rr >= 0) & (rr < rows_g) & (cc >= 0) & (cc < width_g)


def _rows_from(x, off):
    s = (-off) % x.shape[0]
    return x if s == 0 else pltpu.roll(x, s, 0)


def _conv_pre(xs, w_ref, b_ref, rows_g, width_g):
    acc = jnp.broadcast_to(b_ref[...], xs.shape)
    for a in range(3):
        if rows_g == 1 and a != 1:
            continue
        for b in range(3):
            off = (a - 1) * width_g + (b - 1)
            valid = _tap_valid(xs.shape[0], rows_g, width_g, a, b, xs.shape[1])
            acc = acc + jnp.where(valid, _rows_from(xs, off), 0.0) * w_ref[a * 3 + b:a * 3 + b + 1, :]
    return acc


def _conv_fwd(u, conv_w9, conv_b, t_rows, c_rows, w, ct):
    r = u.shape[0]
    base = 5 * w // ct

    def body(x_ref, w_ref, b_ref, o_ref):
        for r0, n, rows_g, width_g in _conv_parts(t_rows, c_rows):
            o_ref[r0:r0 + n, :] = _silu(_conv_pre(x_ref[r0:r0 + n, :], w_ref, b_ref, rows_g, width_g))

    return pl.pallas_call(
        body, name="conv_fwd", grid=(2 * w // ct,),
        in_specs=[pl.BlockSpec((r, ct), lambda i: (0, base + i)), pl.BlockSpec((9, ct), lambda i: (0, i)),
                  pl.BlockSpec((1, ct), lambda i: (0, i))],
        out_specs=pl.BlockSpec((r, ct), lambda i: (0, i)),
        out_shape=jax.ShapeDtypeStruct((r, 2 * w), F32),
        compiler_params=_params(("parallel",)),
    )(u, conv_w9, conv_b)


def _conv_bwd(u, dqk, conv_w9, conv_b, t_rows, c_rows, w, ct):
    r = u.shape[0]
    base = 5 * w // ct

    def body(x_ref, d_ref, w_ref, b_ref, dx_ref, dw_ref, db_ref):
        dw = [jnp.zeros((1, ct), F32) for _ in range(9)]
        db = jnp.zeros((1, ct), F32)
        for r0, n, rows_g, width_g in _conv_parts(t_rows, c_rows):
            xs = x_ref[r0:r0 + n, :]
            pre = _conv_pre(xs, w_ref, b_ref, rows_g, width_g)
            sg = _sigmoid(pre)
            dpre = d_ref[r0:r0 + n, :] * (sg * (1.0 + pre * (1.0 - sg)))
            db = db + jnp.sum(dpre, axis=0, keepdims=True)
            dx = jnp.zeros_like(xs)
            for a in range(3):
                if rows_g == 1 and a != 1:
                    continue
                for b in range(3):
                    off = (a - 1) * width_g + (b - 1)
                    valid = _tap_valid(n, rows_g, width_g, a, b, ct)
                    dw[a * 3 + b] = dw[a * 3 + b] + jnp.sum(
                        jnp.where(valid, _rows_from(xs, off), 0.0) * dpre, axis=0, keepdims=True)
                    dx = dx + _rows_from(jnp.where(valid, dpre, 0.0) * w_ref[a * 3 + b:a * 3 + b + 1, :], -off)
            dx_ref[r0:r0 + n, :] = dx
        for tap in range(9):
            dw_ref[tap:tap + 1, :] = dw[tap]
        db_ref[...] = db

    return pl.pallas_call(
        body, name="conv_bwd", grid=(2 * w // ct,),
        in_specs=[pl.BlockSpec((r, ct), lambda i: (0, base + i)), pl.BlockSpec((r, ct), lambda i: (0, i)),
                  pl.BlockSpec((9, ct), lambda i: (0, i)), pl.BlockSpec((1, ct), lambda i: (0, i))],
        out_specs=(pl.BlockSpec((r, ct), lambda i: (0, i)), pl.BlockSpec((9, ct), lambda i: (0, i)),
                   pl.BlockSpec((1, ct), lambda i: (0, i))),
        out_shape=(jax.ShapeDtypeStruct((r, 2 * w), F32), jax.ShapeDtypeStruct((9, 2 * w), F32),
                   jax.ShapeDtypeStruct((1, 2 * w), F32)),
        compiler_params=_params(("parallel",)),
    )(u, dqk, conv_w9, conv_b)


def _scan_order(n_lat, n_ctx, rev):
    n = n_lat + n_ctx
    if rev:
        return lambda j: n - 1 - j
    return lambda j: (j + n_lat) % n


def _hg_scan_fwd(u, lb, rev, w, n_lat, n_ctx):
    r = u.shape[0]
    n_heads = w // HG_DK
    n_chunks = n_lat + n_ctx
    nat = _scan_order(n_lat, n_ctx, rev)
    col_f = 2 if rev else 1

    def body(aq, af, ai, lb_ref, o_ref, save_ref, s_ref):
        @pl.when(pl.program_id(0) == 0)
        def _():
            s_ref[...] = jnp.zeros_like(s_ref)
        save_ref[0] = s_ref[...]
        new, o = _hg_chunk([s_ref[h] for h in range(n_heads)], aq[...], af[...], ai[...],
                           lb_ref[0:1, :], lb_ref[1:2, :], rev)
        o_ref[...] = o
        for h in range(n_heads):
            s_ref[h] = new[h]

    return pl.pallas_call(
        body, name="hg_scan_fwd_%d" % rev, grid=(n_chunks,),
        in_specs=[pl.BlockSpec((CHUNK, w), lambda j: (nat(j), 0)), pl.BlockSpec((CHUNK, w), lambda j: (nat(j), col_f)),
                  pl.BlockSpec((CHUNK, w), lambda j: (nat(j), 3)), pl.BlockSpec((2, w), lambda j: (0, 0))],
        out_specs=(pl.BlockSpec((CHUNK, w), lambda j: (nat(j), 0)),
                   pl.BlockSpec((1, n_heads, HG_DK, HG_DK), lambda j: (j, 0, 0, 0))),
        out_shape=(jax.ShapeDtypeStruct((r, w), F32), jax.ShapeDtypeStruct((n_chunks, n_heads, HG_DK, HG_DK), F32)),
        scratch_shapes=[pltpu.VMEM((n_heads, HG_DK, HG_DK), F32)],
        compiler_params=_params(("arbitrary",)),
    )(u, u, u, lb)


def _hg_scan_bwd(u, lb, saved, d_o, prev, rev, w, n_lat, n_ctx):
    r = u.shape[0]
    n_heads = w // HG_DK
    n_chunks = n_lat + n_ctx
    order = _scan_order(n_lat, n_ctx, rev)
    step = lambda jj: n_chunks - 1 - jj
    nat = lambda jj: order(step(jj))
    col_f = 2 if rev else 1
    n_prev = 0 if prev is None else 2

    def body(*refs):
        aq, af, ai, lb_ref, save_ref, do_ref = refs[:6]
        prev_refs = refs[6:6 + n_prev]
        daq_ref, daf_ref, dai_ref, dlb_ref, ds_ref = refs[6 + n_prev:]
        jj = pl.program_id(0)

        @pl.when(jj == 0)
        def _():
            ds_ref[...] = jnp.zeros_like(ds_ref)
            dlb_ref[...] = jnp.zeros_like(dlb_ref)

        f = lambda st, a, b, c, l0, l1: _hg_chunk(st, a, b, c, l0, l1, rev)
        _, vjp = jax.vjp(f, [save_ref[0, h] for h in range(n_heads)], aq[...], af[...], ai[...],
                         lb_ref[0:1, :], lb_ref[1:2, :])
        d_out = do_ref[...] * (nat(jj) < n_lat).astype(F32)
        dst, daq, daf, dai, dl0, dl1 = vjp(([ds_ref[h] for h in range(n_heads)], d_out))
        for h in range(n_heads):
            ds_ref[h] = dst[h]
        if n_prev:
            daq = daq + prev_refs[0][...]
            dai = dai + prev_refs[1][...]
        daq_ref[...] = daq
        daf_ref[...] = daf
        dai_ref[...] = dai
        dlb_ref[0:1, :] += dl0
        dlb_ref[1:2, :] += dl1

    row = lambda jj: (nat(jj), 0)
    in_specs = [pl.BlockSpec((CHUNK, w), row), pl.BlockSpec((CHUNK, w), lambda jj: (nat(jj), col_f)),
                pl.BlockSpec((CHUNK, w), lambda jj: (nat(jj), 3)), pl.BlockSpec((2, w), lambda jj: (0, 0)),
                pl.BlockSpec((1, n_heads, HG_DK, HG_DK), lambda jj: (step(jj), 0, 0, 0)),
                pl.BlockSpec((CHUNK, w), lambda jj: (jnp.minimum(nat(jj), n_lat - 1), 0))]
    in_specs += [pl.BlockSpec((CHUNK, w), row)] * n_prev
    sds = jax.ShapeDtypeStruct((r, w), F32)
    return pl.pallas_call(
        body, name="hg_scan_bwd_%d" % rev, grid=(n_chunks,),
        in_specs=in_specs,
        out_specs=(pl.BlockSpec((CHUNK, w), row), pl.BlockSpec((CHUNK, w), row), pl.BlockSpec((CHUNK, w), row),
                   pl.BlockSpec((2, w), lambda jj: (0, 0))),
        out_shape=(sds, sds, sds, jax.ShapeDtypeStruct((2, w), F32)),
        scratch_shapes=[pltpu.VMEM((n_heads, HG_DK, HG_DK), F32)],
        compiler_params=_params(("arbitrary",)),
    )(u, u, u, lb, saved, d_o, *(prev or ()))


def _ml_state_shapes(n_chunks, n_heads, dh):
    return (jax.ShapeDtypeStruct((n_chunks, n_heads, dh, dh), F32),
            jax.ShapeDtypeStruct((n_chunks, n_heads, 1, dh), F32),
            jax.ShapeDtypeStruct((n_chunks, n_heads, 1, LANE), F32))


def _ml_state_specs(n_heads, dh, index):
    return (pl.BlockSpec((1, n_heads, dh, dh), lambda j: (index(j), 0, 0, 0)),
            pl.BlockSpec((1, n_heads, 1, dh), lambda j: (index(j), 0, 0, 0)),
            pl.BlockSpec((1, n_heads, 1, LANE), lambda j: (index(j), 0, 0, 0)))


def _ml_state_scratch(n_heads, dh):
    return [pltpu.VMEM((n_heads, dh, dh), F32), pltpu.VMEM((n_heads, 1, dh), F32), pltpu.VMEM((n_heads, 1, LANE), F32)]


def _ml_scan_fwd(qk, u, gate_b, rev, w, n_heads, n_lat, n_ctx):
    r = u.shape[0]
    dh = w // n_heads
    n_chunks = n_lat + n_ctx
    nat = _scan_order(n_lat, n_ctx, rev)
    d = 1 if rev else 0

    def body(q, k, v, g, gb, h_ref, sc, sn, sm, c_ref, n_ref, m_ref):
        @pl.when(pl.program_id(0) == 0)
        def _():
            c_ref[...] = jnp.zeros_like(c_ref)
            n_ref[...] = jnp.zeros_like(n_ref)
            m_ref[...] = jnp.zeros_like(m_ref)
        sc[0] = c_ref[...]
        sn[0] = n_ref[...]
        sm[0] = m_ref[...]
        state = ([c_ref[h] for h in range(n_heads)], [n_ref[h] for h in range(n_heads)],
                 [m_ref[h] for h in range(n_heads)])
        (nc, nn_, nm), o = _ml_chunk(state, q[...], k[...], v[...], g[...], gb[...], rev, d)
        h_ref[...] = o
        for h in range(n_heads):
            c_ref[h] = nc[h]
            n_ref[h] = nn_[h]
            m_ref[h] = nm[h]

    return pl.pallas_call(
        body, name="ml_scan_fwd_%d" % rev, grid=(n_chunks,),
        in_specs=[pl.BlockSpec((CHUNK, w), lambda j: (nat(j), 0)), pl.BlockSpec((CHUNK, w), lambda j: (nat(j), 1)),
                  pl.BlockSpec((CHUNK, w), lambda j: (nat(j), 7)),
                  pl.BlockSpec((CHUNK, LANE), lambda j: (nat(j), 10 * w // LANE)),
                  pl.BlockSpec((1, LANE), lambda j: (0, 0))],
        out_specs=(pl.BlockSpec((CHUNK, w), lambda j: (nat(j), 0)),) + _ml_state_specs(n_heads, dh, lambda j: j),
        out_shape=(jax.ShapeDtypeStruct((r, w), F32),) + _ml_state_shapes(n_chunks, n_heads, dh),
        scratch_shapes=_ml_state_scratch(n_heads, dh),
        compiler_params=_params(("arbitrary",)),
    )(qk, qk, u, u, gate_b)


def _ml_scan_bwd(qk, u, gate_b, saved, d_h, prev, rev, w, n_heads, n_lat, n_ctx):
    r = u.shape[0]
    dh = w // n_heads
    n_chunks = n_lat + n_ctx
    order = _scan_order(n_lat, n_ctx, rev)
    step = lambda jj: n_chunks - 1 - jj
    nat = lambda jj: order(step(jj))
    d = 1 if rev else 0
    n_prev = 0 if prev is None else 3

    def body(*refs):
        q, k, v, g, gb, sc, sn, sm, dh_ref = refs[:9]
        prev_refs = refs[9:9 + n_prev]
        dqk_ref, dv_ref, dg_ref, dgb_ref, dc_ref, dn_ref, dm_ref = refs[9 + n_prev:]
        jj = pl.program_id(0)

        @pl.when(jj == 0)
        def _():
            dc_ref[...] = jnp.zeros_like(dc_ref)
            dn_ref[...] = jnp.zeros_like(dn_ref)
            dm_ref[...] = jnp.zeros_like(dm_ref)
            dgb_ref[...] = jnp.zeros_like(dgb_ref)

        state = ([sc[0, h] for h in range(n_heads)], [sn[0, h] for h in range(n_heads)],
                 [sm[0, h] for h in range(n_heads)])
        f = lambda st, a, b, c, gg, bb: _ml_chunk(st, a, b, c, gg, bb, rev, d)
        _, vjp = jax.vjp(f, state, q[...], k[...], v[...], g[...], gb[...])
        d_state = ([dc_ref[h] for h in range(n_heads)], [dn_ref[h] for h in range(n_heads)],
                   [dm_ref[h] for h in range(n_heads)])
        d_out = dh_ref[...] * (nat(jj) < n_lat).astype(F32)
        (dc, dn, dm), dq, dk, dv, dg, dgb = vjp((d_state, d_out))
        for h in range(n_heads):
            dc_ref[h] = dc[h]
            dn_ref[h] = dn[h]
            dm_ref[h] = dm[h]
        if n_prev:
            dq = dq + prev_refs[0][:, 0:w]
            dk = dk + prev_refs[0][:, w:2 * w]
            dv = dv + prev_refs[1][...]
            dg = dg + prev_refs[2][...]
        dqk_ref[:, 0:w] = dq
        dqk_ref[:, w:2 * w] = dk
        dv_ref[...] = dv
        dg_ref[...] = dg
        dgb_ref[...] += dgb

    row = lambda jj: (nat(jj), 0)
    in_specs = [pl.BlockSpec((CHUNK, w), row), pl.BlockSpec((CHUNK, w), lambda jj: (nat(jj), 1)),
                pl.BlockSpec((CHUNK, w), lambda jj: (nat(jj), 7)),
                pl.BlockSpec((CHUNK, LANE), lambda jj: (nat(jj), 10 * w // LANE)),
                pl.BlockSpec((1, LANE), lambda jj: (0, 0))]
    in_specs += list(_ml_state_specs(n_heads, dh, step))
    in_specs += [pl.BlockSpec((CHUNK, w), lambda jj: (jnp.minimum(nat(jj), n_lat - 1), 0))]
    if n_prev:
        in_specs += [pl.BlockSpec((CHUNK, 2 * w), row), pl.BlockSpec((CHUNK, w), row), pl.BlockSpec((CHUNK, LANE), row)]
    return pl.pallas_call(
        body, name="ml_scan_bwd_%d" % rev, grid=(n_chunks,),
        in_specs=in_specs,
        out_specs=(pl.BlockSpec((CHUNK, 2 * w), row), pl.BlockSpec((CHUNK, w), row), pl.BlockSpec((CHUNK, LANE), row),
                   pl.BlockSpec((1, LANE), lambda jj: (0, 0))),
        out_shape=(jax.ShapeDtypeStruct((r, 2 * w), F32), jax.ShapeDtypeStruct((r, w), F32),
                   jax.ShapeDtypeStruct((r, LANE), F32), jax.ShapeDtypeStruct((1, LANE), F32)),
        scratch_shapes=_ml_state_scratch(n_heads, dh),
        compiler_params=_params(("arbitrary",)),
    )(qk, qk, u, u, gate_b, *saved, d_h, *(prev or ()))


def _post_specs(w, tm, lat_tiles, cols):
    return [pl.BlockSpec((tm, w), (lambda i, cb=cb: (jnp.minimum(i, lat_tiles - 1), cb))) for cb in cols]


def _post_fwd(o_f, o_b, h_f, h_b, u, wa, wb, t_rows, w, n_hg, n_ml, tm):
    lat_tiles = t_rows // tm

    def body(of, ob, hf, hb, az, bo, bz, wa_ref, wb_ref, y_ref):
        y_ref[...] = _post_fn(of[...], ob[...], az[...], hf[...], hb[...], bo[...], bz[...],
                              wa_ref[...], wb_ref[...], n_hg, n_ml).astype(BF16)

    rows = pl.BlockSpec((tm, w), lambda i: (i, 0))
    vec = pl.BlockSpec((1, w), lambda i: (0, 0))
    return pl.pallas_call(
        body, name="post_fwd", grid=(lat_tiles,),
        in_specs=[rows] * 4 + _post_specs(w, tm, lat_tiles, (4, 8, 9)) + [vec, vec],
        out_specs=pl.BlockSpec((tm, 2 * w), lambda i: (i, 0)),
        out_shape=jax.ShapeDtypeStruct((t_rows, 2 * w), BF16),
        compiler_params=_params(("parallel",)),
    )(o_f, o_b, h_f, h_b, u, u, u, wa, wb)


def _post_bwd(o_f, o_b, h_f, h_b, u, wa, wb, dy, t_rows, w, n_hg, n_ml, tm):
    r = u.shape[0]
    lat_tiles = t_rows // tm
    lat = lambda i: (jnp.minimum(i, lat_tiles - 1), 0)

    def body(of, ob, hf, hb, az, bo, bz, wa_ref, wb_ref, dy_ref, do_ref, dh_ref, daz_ref, dbo_ref, dbz_ref,
             dwa_ref, dwb_ref):
        i = pl.program_id(0)

        @pl.when(i == 0)
        def _():
            dwa_ref[...] = jnp.zeros_like(dwa_ref)
            dwb_ref[...] = jnp.zeros_like(dwb_ref)

        @pl.when(i < lat_tiles)
        def _():
            f = functools.partial(_post_fn, n_hg=n_hg, n_ml=n_ml)
            _, vjp = jax.vjp(f, of[...], ob[...], az[...], hf[...], hb[...], bo[...], bz[...], wa_ref[...], wb_ref[...])
            d_of, _, d_az, d_hf, _, d_bo, d_bz, d_wa, d_wb = vjp(dy_ref[...])
            do_ref[...] = d_of
            dh_ref[...] = d_hf
            daz_ref[...] = d_az
            dbo_ref[...] = d_bo
            dbz_ref[...] = d_bz
            dwa_ref[...] += d_wa
            dwb_ref[...] += d_wb

        @pl.when(i >= lat_tiles)
        def _():
            daz_ref[...] = jnp.zeros_like(daz_ref)
            dbo_ref[...] = jnp.zeros_like(dbo_ref)
            dbz_ref[...] = jnp.zeros_like(dbz_ref)

    lat_rows = pl.BlockSpec((tm, w), lat)
    all_rows = pl.BlockSpec((tm, w), lambda i: (i, 0))
    vec = pl.BlockSpec((1, w), lambda i: (0, 0))
    sd_t = jax.ShapeDtypeStruct((t_rows, w), F32)
    sd_r = jax.ShapeDtypeStruct((r, w), F32)
    sd_v = jax.ShapeDtypeStruct((1, w), F32)
    return pl.pallas_call(
        body, name="post_bwd", grid=(r // tm,),
        in_specs=[lat_rows] * 4 + _post_specs(w, tm, lat_tiles, (4, 8, 9)) + [vec, vec]
        + [pl.BlockSpec((tm, 2 * w), lat)],
        out_specs=(lat_rows, lat_rows, all_rows, all_rows, all_rows, vec, vec),
        out_shape=(sd_t, sd_t, sd_r, sd_r, sd_r, sd_v, sd_v),
        compiler_params=_params(("arbitrary",)),
    )(o_f, o_b, h_f, h_b, u, u, u, wa, wb, dy)


OUT_ROW_GATE, OUT_ROW_LN_G, OUT_ROW_LN_B, OUT_ROW_LOSS = 0, 1, 2, 3


def _out_block(y, w_out, x, target, prm, tm):
    t_rows, dm = x.shape
    di = y.shape[1]

    def body(y_ref, w_ref, x_ref, t_ref, p_ref, dz_ref, dy_ref, gx_ref, acc_ref):
        @pl.when(pl.program_id(0) == 0)
        def _():
            acc_ref[...] = jnp.zeros_like(acc_ref)

        gate, ln_g, ln_b = p_ref[0:1, :], p_ref[1:2, :], p_ref[2:3, :]
        z = _nn(y_ref[...], w_ref[...])
        res = ALPHA * x_ref[...] + gate * z
        mu = jnp.mean(res, axis=-1, keepdims=True)
        rc = res - mu
        rstd = lax.rsqrt(jnp.mean(rc * rc, axis=-1, keepdims=True) + LN_EPS)
        rn = rc * rstd
        err = rn * ln_g + ln_b - t_ref[...]
        d_out = err * (1.0 / dm)
        d_rn = d_out * ln_g
        d_res = rstd * (d_rn - jnp.mean(d_rn, axis=-1, keepdims=True)
                        - rn * jnp.mean(d_rn * rn, axis=-1, keepdims=True))
        acc_ref[OUT_ROW_GATE:OUT_ROW_GATE + 1, :] += jnp.sum(d_res * z, axis=0, keepdims=True)
        acc_ref[OUT_ROW_LN_G:OUT_ROW_LN_G + 1, :] += jnp.sum(d_out * rn, axis=0, keepdims=True)
        acc_ref[OUT_ROW_LN_B:OUT_ROW_LN_B + 1, :] += jnp.sum(d_out, axis=0, keepdims=True)
        acc_ref[OUT_ROW_LOSS:OUT_ROW_LOSS + 1, :] += (0.5 / dm) * jnp.sum(err * err, axis=0, keepdims=True)
        gx_ref[...] = ALPHA * d_res
        dz = (d_res * gate).astype(BF16)
        dz_ref[...] = dz
        dy_ref[...] = _nt(dz, w_ref[...])

    rows_d = pl.BlockSpec((tm, dm), lambda i: (i, 0))
    rows_i = pl.BlockSpec((tm, di), lambda i: (i, 0))
    return pl.pallas_call(
        body, name="out_block", grid=(t_rows // tm,),
        in_specs=[rows_i, pl.BlockSpec((di, dm), lambda i: (0, 0)), rows_d, rows_d,
                  pl.BlockSpec((8, dm), lambda i: (0, 0))],
        out_specs=(rows_d, rows_i, rows_d, pl.BlockSpec((8, dm), lambda i: (0, 0))),
        out_shape=(jax.ShapeDtypeStruct((t_rows, dm), BF16), jax.ShapeDtypeStruct((t_rows, di), F32),
                   jax.ShapeDtypeStruct((t_rows, dm), F32), jax.ShapeDtypeStruct((8, dm), F32)),
        compiler_params=_params(("arbitrary",)),
    )(y, w_out, x, target, prm)


def _mod_fwd(c16, w_mod, tn):
    dm, n = w_mod.shape

    def body(c_ref, w_ref, o_ref, a_ref):
        a = _silu(c_ref[...])
        a_ref[...] = a
        o_ref[...] = _nn(a, w_ref[...], HIGHEST)

    return pl.pallas_call(
        body, name="mod_fwd", grid=(n // tn,),
        in_specs=[pl.BlockSpec((16, dm), lambda j: (0, 0)), pl.BlockSpec((dm, tn), lambda j: (0, j))],
        out_specs=(pl.BlockSpec((16, tn), lambda j: (0, j)), pl.BlockSpec((16, dm), lambda j: (0, 0))),
        out_shape=(jax.ShapeDtypeStruct((16, n), F32), jax.ShapeDtypeStruct((16, dm), F32)),
        compiler_params=_params(("arbitrary",)),
    )(c16, w_mod)


def _mod_bwd(a16, dm16, w_mod, tn):
    dm, n = w_mod.shape

    def body(a_ref, d_ref, w_ref, dw_ref, dc_ref):
        @pl.when(pl.program_id(0) == 0)
        def _():
            dc_ref[...] = jnp.zeros_like(dc_ref)
        dw_ref[...] = _tn(a_ref[...], d_ref[...], HIGHEST)
        dc_ref[...] += _nt(d_ref[...], w_ref[...], HIGHEST)

    return pl.pallas_call(
        body, name="mod_bwd", grid=(n // tn,),
        in_specs=[pl.BlockSpec((16, dm), lambda j: (0, 0)), pl.BlockSpec((16, tn), lambda j: (0, j)),
                  pl.BlockSpec((dm, tn), lambda j: (0, j))],
        out_specs=(pl.BlockSpec((dm, tn), lambda j: (0, j)), pl.BlockSpec((16, dm), lambda j: (0, 0))),
        out_shape=(jax.ShapeDtypeStruct((dm, n), F32), jax.ShapeDtypeStruct((16, dm), F32)),
        compiler_params=_params(("arbitrary",)),
    )(a16, dm16, w_mod)


def _sum_devices(g, fold_rows):
    n_dev, rows, n = g.shape

    def body(g_ref, s_ref, t_ref):
        s = g_ref[0]
        for dev in range(1, n_dev):
            s = s + g_ref[dev]
        t_ref[...] = jnp.broadcast_to(jnp.sum(s, axis=-1, keepdims=True), (rows, LANE))
        s_ref[...] = s
        s_ref[0:fold_rows, :] = s[0:fold_rows] + s[fold_rows:2 * fold_rows]

    return pl.pallas_call(
        body, name="sum_devices",
        out_shape=(jax.ShapeDtypeStruct((rows, n), F32), jax.ShapeDtypeStruct((rows, LANE), F32)),
        compiler_params=_params(),
    )(g)


def _c_ctx_grad(parts, c_ctx_row):
    def body(p_ref, c_ref, o_ref):
        s = p_ref[0]
        for chip in range(1, N_CHIPS):
            s = s + p_ref[2 * chip]
        cv = c_ref[...]
        sg = _sigmoid(cv)
        o_ref[...] = s * (sg * (1.0 + cv * (1.0 - sg)))

    return pl.pallas_call(
        body, name="c_ctx_grad", out_shape=jax.ShapeDtypeStruct(parts.shape[1:], F32), compiler_params=_params(),
    )(parts, c_ctx_row)


def _sum_pair(name, mine, got):
    def body(a_ref, b_ref, o_ref):
        o_ref[...] = (a_ref[...] + b_ref[...]).astype(BF16)

    k, rows, n = mine.shape
    tr = _largest_divisor(rows, max(SUBLANE_BF16, (1 << 18) // n), SUBLANE_BF16)
    spec = pl.BlockSpec((1, tr, n), lambda kk, i: (kk, i, 0))
    return pl.pallas_call(
        body, name=name, grid=(k, rows // tr), in_specs=[spec, spec], out_specs=spec,
        out_shape=jax.ShapeDtypeStruct(mine.shape, BF16), compiler_params=_params(("parallel", "parallel")),
    )(mine, got)


def _sum_chips(name, got):
    k, rows, n = got.shape
    tr = _largest_divisor(rows, max(SUBLANE_BF16, (1 << 18) // n), SUBLANE_BF16)

    def body(g_ref, o_ref):
        total = g_ref[0].astype(F32)
        for kk in range(1, k):
            total = total + g_ref[kk].astype(F32)
        o_ref[...] = total

    return pl.pallas_call(
        body, name=name, grid=(rows // tr,),
        in_specs=[pl.BlockSpec((k, tr, n), lambda i: (0, i, 0))], out_specs=pl.BlockSpec((tr, n), lambda i: (i, 0)),
        out_shape=jax.ShapeDtypeStruct((rows, n), F32), compiler_params=_params(("parallel",)),
    )(got)


def _adamw(name, w, g, m, v):
    rows, n = w.shape
    tr = rows if rows % 8 else _largest_divisor(rows, max(8, (1 << 18) // n), 8)

    def body(w_ref, g_ref, m_ref, v_ref, d_ref, mo_ref, vo_ref):
        gv = g_ref[...]
        m2 = ADAM_B1 * m_ref[...] + (1.0 - ADAM_B1) * gv
        v2 = ADAM_B2 * v_ref[...] + (1.0 - ADAM_B2) * jnp.square(gv)
        m_hat = m2 / (1.0 - ADAM_B1 ** ADAM_STEP)
        v_hat = v2 / (1.0 - ADAM_B2 ** ADAM_STEP)
        d_ref[...] = -ADAM_LR * (m_hat / (jnp.sqrt(v_hat) + ADAM_EPS) + ADAM_WD * w_ref[...])
        mo_ref[...] = m2
        vo_ref[...] = v2

    spec = pl.BlockSpec((tr, n), lambda i: (i, 0))
    sds = jax.ShapeDtypeStruct((rows, n), F32)
    return pl.pallas_call(
        body, name=name, grid=(rows // tr,), in_specs=[spec] * 4, out_specs=(spec,) * 3,
        out_shape=(sds, sds, sds), compiler_params=_params(("parallel",)),
    )(w, g, m, v)


PACK_LANES = 1024


def _pack(pieces):
    flat = jnp.concatenate([p.reshape(-1) for p in pieces])
    total = -(-flat.shape[0] // (8 * PACK_LANES)) * 8 * PACK_LANES
    return jnp.pad(flat, (0, total - flat.shape[0])).reshape(-1, PACK_LANES)


def _unpack(packed, shapes):
    flat = packed.reshape(-1)
    out, off = [], 0
    for shp in shapes:
        size = math.prod(shp)
        out.append(flat[off:off + size].reshape(shp))
        off += size
    return out


def _rows8(rows, width):
    flat = [r.reshape(width) for r in rows] + [jnp.zeros(((8 - len(rows)) * width,), F32)]
    return jnp.concatenate(flat).reshape(8, width)


def _reduce_scatter(tag, mine, other):
    got = _sibling_swap("rs_pair_" + tag, [other])[0]
    pair = _sum_pair("rs_pair_sum_" + tag, mine, got)
    landed = _chip_scatter("rs_chips_" + tag, [pair])[0]
    half = _sum_chips("rs_chip_sum_" + tag, landed)
    both = _sibling_join("rs_join_" + tag, [half])[0]
    return both.reshape(2 * half.shape[0], half.shape[1])


def kernel(x, c, ctx, c_ctx, w_mod, b_mod, w_in, conv_w, conv_b, hg_lb, ml_gate_b, hg_norm_w, ml_norm_w, w_out, ln_g, ln_b, loss_target, m_c_ctx, m_w_mod, m_b_mod, m_w_in, m_conv_w, m_conv_b, m_hg_lb, m_ml_gate_b, m_hg_norm_w, m_ml_norm_w, m_w_out, m_ln_g, m_ln_b, v_c_ctx, v_w_mod, v_b_mod, v_w_in, v_conv_w, v_conv_b, v_hg_lb, v_ml_gate_b, v_hg_norm_w, v_ml_norm_w, v_w_out, v_ln_g, v_ln_b):
    t_rows, dm = x.shape[1], x.shape[2]
    c_rows = ctx.shape[1]
    w = hg_norm_w.shape[1]
    n_ml = ml_gate_b.shape[-1]
    n_hg = w // HG_DK
    di = 2 * w
    n_in = 10 * w + 4 * n_ml
    ns = w_in.shape[2]
    nm = w_mod.shape[2]
    n_pad = 10 * w + LANE
    r_rows = t_rows + c_rows
    n_lat, n_ctx = t_rows // CHUNK, c_rows // CHUNK
    assert ml_norm_w.shape[1] == w and di == dm and N_CHIPS * ns == n_in and N_CHIPS * nm == 3 * dm
    assert w_out.shape[1] * N_CHIPS == di and 4 * n_ml <= LANE and t_rows % GRID_W == 0

    xi, yi, ci = lax.axis_index("x"), lax.axis_index("y"), lax.axis_index("c")
    chip = 2 * xi + yi
    dev = 4 * xi + 2 * yi + ci

    tm = _largest_divisor(math.gcd(t_rows, c_rows), 256, 8)
    tm_mm = _largest_divisor(r_rows, 1088, SUBLANE_BF16)
    tn_mm = LANE * _largest_divisor(n_pad // LANE, 9)
    tn_mod = _largest_divisor(nm, 512, LANE)

    shard_shapes = [(dm,), (2, 2, w // N_CHIPS), (3, 3, di // N_CHIPS)]
    g1 = _all_gather8("gather_inputs", _pack([c, hg_lb, conv_w]))
    per_dev = [_unpack(g1[i], shard_shapes) for i in range(N_DEV)]
    c_all = jnp.stack([p[0] for p in per_dev])
    lb_full = jnp.concatenate([per_dev[2 * k][1] for k in range(N_CHIPS)], axis=-1)
    conv_w9 = jnp.concatenate([per_dev[2 * k][2] for k in range(N_CHIPS)], axis=-1).reshape(9, di)

    c16 = jnp.concatenate([c_all, c_ctx[None], jnp.zeros((16 - N_DEV - 1, dm), F32)])
    mod_part, a16 = _mod_fwd(c16, w_mod[0], tn_mod)
    g2 = _all_gather8("gather_mod", mod_part)
    mod_all = jnp.concatenate([g2[2 * k] for k in range(N_CHIPS)], axis=1) + b_mod
    mod_x = lax.dynamic_index_in_dim(mod_all, dev, 0, keepdims=False).reshape(3, dm)
    mod_c = mod_all[N_DEV].reshape(3, dm)
    prm = jnp.stack([_rows8(list(mod_x), dm), _rows8(list(mod_c), dm)])

    gw_in, gw_out = _all_gather_chips("gather_weights", [w_in[0].astype(BF16), w_out[0].astype(BF16)])
    w_full = jnp.concatenate([gw_in[k] for k in range(N_CHIPS)] + [jnp.zeros((dm, n_pad - n_in), BF16)], axis=1)
    w_out_full = gw_out.reshape(di, dm)

    xc = jnp.concatenate([x[0], ctx[0]])
    hc = _modulate_fwd(xc, prm, t_rows, tm)
    u = _mm_nn("in_proj", hc, w_full, tm_mm, tn_mm, F32)
    lb_f, lb_b = lb_full[0], lb_full[1]
    o_f, s_f = _hg_scan_fwd(u, lb_f, False, w, n_lat, n_ctx)
    o_b, s_b = _hg_scan_fwd(u, lb_b, True, w, n_lat, n_ctx)
    qk = _conv_fwd(u, conv_w9, conv_b, t_rows, c_rows, w, LANE)
    gate_b_row = jnp.pad(ml_gate_b.reshape(1, -1), ((0, 0), (0, LANE - 4 * n_ml)))
    h_f, *sv_f = _ml_scan_fwd(qk, u, gate_b_row, False, w, n_ml, n_lat, n_ctx)
    h_b, *sv_b = _ml_scan_fwd(qk, u, gate_b_row, True, w, n_ml, n_lat, n_ctx)
    y = _post_fwd(o_f, o_b, h_f, h_b, u, hg_norm_w, ml_norm_w, t_rows, w, n_hg, n_ml, tm)
    prm_out = _rows8([mod_x[2], ln_g, ln_b], dm)
    dz, dy, gx_direct, acc_out = _out_block(y, w_out_full, x[0], loss_target[0], prm_out, tm // 2)

    d_w_out = _mm_tn("d_w_out", y, dz, _largest_divisor(t_rows, 1024, SUBLANE_BF16),
                     _largest_divisor(dm, 1024, LANE))
    d_o, d_h, d_az, d_bo, d_bz, d_wa, d_wb = _post_bwd(o_f, o_b, h_f, h_b, u, hg_norm_w, ml_norm_w, dy,
                                                        t_rows, w, n_hg, n_ml, tm)
    d_aq1, d_aff, d_ai1, d_lb_f = _hg_scan_bwd(u, lb_f, s_f, d_o, None, False, w, n_lat, n_ctx)
    d_aq, d_afb, d_ai, d_lb_b = _hg_scan_bwd(u, lb_b, s_b, d_o, (d_aq1, d_ai1), True, w, n_lat, n_ctx)
    d_qk1, d_v1, d_g1, d_gb_f = _ml_scan_bwd(qk, u, gate_b_row, sv_f, d_h, None, False, w, n_ml, n_lat, n_ctx)
    d_qk, d_v, d_g, d_gb_b = _ml_scan_bwd(qk, u, gate_b_row, sv_b, d_h, (d_qk1, d_v1, d_g1), True,
                                          w, n_ml, n_lat, n_ctx)
    d_bqk, d_cw, d_cb = _conv_bwd(u, d_qk, conv_w9, conv_b, t_rows, c_rows, w, LANE)
    du = jnp.concatenate([d_aq, d_aff, d_afb, d_ai, d_az, d_bqk, d_v, d_bo, d_bz, d_g], axis=1).astype(BF16)
    d_w_in = _mm_tn("d_w_in", hc, du, tm_mm, tn_mm)
    d_hc = _mm_nt("d_h", du, w_full, tm_mm, tn_mm)
    gx_all, acc_mod = _modulate_bwd(xc, d_hc, prm, gx_direct, t_rows, tm)
    grad_x = gx_all[:t_rows][None]

    half_in = dm // 2
    mine_in = lax.dynamic_slice_in_dim(d_w_in, ci * half_in, half_in, 0)
    other_in = lax.dynamic_slice_in_dim(d_w_in, (1 - ci) * half_in, half_in, 0)
    pieces_in = lambda a: jnp.stack([a[:, k * ns:(k + 1) * ns] for k in range(N_CHIPS)])
    g_w_in = _reduce_scatter("w_in", pieces_in(mine_in), pieces_in(other_in))
    d_w_out4 = d_w_out.reshape(N_CHIPS, 2, di // (2 * N_CHIPS), dm)
    g_w_out = _reduce_scatter("w_out", lax.dynamic_index_in_dim(d_w_out4, ci, 1, keepdims=False),
                              lax.dynamic_index_in_dim(d_w_out4, 1 - ci, 1, keepdims=False))

    zero_row = jnp.zeros((dm,), F32)
    d_gb = jnp.concatenate([d_gb_f[:, 0:n_ml], d_gb_b[:, n_ml:2 * n_ml], d_gb_f[:, 2 * n_ml:3 * n_ml],
                            d_gb_b[:, 3 * n_ml:4 * n_ml], jnp.zeros((1, dm - 4 * n_ml), F32)], axis=1)
    rows = [acc_mod[0, 0], acc_mod[0, 1], acc_out[OUT_ROW_GATE],
            acc_mod[1, 0], acc_mod[1, 1], zero_row]
    rows += list(d_cw) + [d_cb[0], d_lb_f.reshape(dm), d_lb_b.reshape(dm),
                          jnp.concatenate([d_wa[0], d_wb[0]]), acc_out[OUT_ROW_LN_G], acc_out[OUT_ROW_LN_B],
                          acc_out[OUT_ROW_LOSS], d_gb[0], zero_row]
    ROW_CW, ROW_CB, ROW_LB, ROW_NORM, ROW_LN_G, ROW_LN_B, ROW_LOSS, ROW_GB = 6, 15, 16, 18, 19, 20, 21, 22
    g3 = _all_gather8("gather_small_grads", jnp.concatenate([r.reshape(dm) for r in rows]).reshape(len(rows), dm))
    sums, totals = _sum_devices(g3, 3)
    loss = totals[ROW_LOSS, 0]
    dm16 = jnp.concatenate([g3[:, 0:3, :].reshape(N_DEV, 3 * dm), sums[3:6].reshape(1, 3 * dm),
                            jnp.zeros((16 - N_DEV - 1, 3 * dm), F32)])
    g_w_mod, dc16 = _mod_bwd(a16, lax.dynamic_slice_in_dim(dm16, chip * nm, nm, 1), w_mod[0], tn_mod)
    g4 = _all_gather8("gather_c_ctx", jnp.pad(dc16[N_DEV:N_DEV + 1], ((0, 7), (0, 0))))
    g_c_ctx = _c_ctx_grad(g4, jnp.broadcast_to(c_ctx[None], (8, dm)))[0]

    chip_cols = lambda a, width: lax.dynamic_slice_in_dim(a, chip * width, width, a.ndim - 1)
    grads = {
        "c_ctx": g_c_ctx,
        "w_mod": g_w_mod[None],
        "b_mod": sums[0:3].reshape(1, 3 * dm),
        "w_in": g_w_in[None],
        "conv_w": chip_cols(sums[ROW_CW:ROW_CW + 9].reshape(1, 3, 3, di), di // N_CHIPS),
        "conv_b": sums[ROW_CB][None],
        "hg_lb": chip_cols(sums[ROW_LB:ROW_LB + 2].reshape(2, 2, w), w // N_CHIPS),
        "ml_gate_b": sums[ROW_GB, 0:4 * n_ml].reshape(1, 4, n_ml),
        "hg_norm_w": sums[ROW_NORM, 0:w][None],
        "ml_norm_w": sums[ROW_NORM, w:2 * w][None],
        "w_out": g_w_out[None],
        "ln_g": sums[ROW_LN_G][None],
        "ln_b": sums[ROW_LN_B][None],
    }
    weights = dict(c_ctx=c_ctx, w_mod=w_mod, b_mod=b_mod, w_in=w_in, conv_w=conv_w, conv_b=conv_b, hg_lb=hg_lb,
                   ml_gate_b=ml_gate_b, hg_norm_w=hg_norm_w, ml_norm_w=ml_norm_w, w_out=w_out, ln_g=ln_g, ln_b=ln_b)
    mom1 = dict(c_ctx=m_c_ctx, w_mod=m_w_mod, b_mod=m_b_mod, w_in=m_w_in, conv_w=m_conv_w, conv_b=m_conv_b,
                hg_lb=m_hg_lb, ml_gate_b=m_ml_gate_b, hg_norm_w=m_hg_norm_w, ml_norm_w=m_ml_norm_w, w_out=m_w_out,
                ln_g=m_ln_g, ln_b=m_ln_b)
    mom2 = dict(c_ctx=v_c_ctx, w_mod=v_w_mod, b_mod=v_b_mod, w_in=v_w_in, conv_w=v_conv_w, conv_b=v_conv_b,
                hg_lb=v_hg_lb, ml_gate_b=v_ml_gate_b, hg_norm_w=v_hg_norm_w, ml_norm_w=v_ml_norm_w, w_out=v_w_out,
                ln_g=v_ln_g, ln_b=v_ln_b)
    names = list(weights)
    big = ("w_mod", "w_in", "w_out")
    small = [n for n in names if n not in big]

    delta, new_m, new_v = {}, {}, {}
    for n in big:
        as2d = lambda a: a.reshape(a.shape[-2], a.shape[-1])
        res = _adamw("adamw_" + n, as2d(weights[n]), as2d(grads[n]), as2d(mom1[n]), as2d(mom2[n]))
        delta[n], new_m[n], new_v[n] = (a.reshape(weights[n].shape) for a in res)
    small_shapes = [weights[n].shape for n in small]
    res = _adamw("adamw_small", *(_pack([src[n] for n in small]) for src in (weights, grads, mom1, mom2)))
    for out, packed in zip((delta, new_m, new_v), res):
        for n, a in zip(small, _unpack(packed, small_shapes)):
            out[n] = a

    return (loss, grad_x, *[grads[n].reshape(weights[n].shape) for n in names], *[delta[n] for n in names],
            *[new_m[n] for n in names], *[new_v[n] for n in names])
```

```python
import functools
import math

import jax
import jax.numpy as jnp
from jax import lax
from jax.experimental import pallas as pl
from jax.experimental.pallas import tpu as pltpu

F32 = jnp.float32
BF16 = jnp.bfloat16
HIGHEST = lax.Precision.HIGHEST
MESH = pl.DeviceIdType.MESH

CHUNK = 64
GRID_W = 64
HG_DK = 128
LANE = 128
SUBLANE_BF16 = 16
ALPHA = 2.0 ** 0.25
LN_EPS = 1e-5
NORM_EPS = 1e-6
ADAM_LR = 0.001
ADAM_B1 = 0.9
ADAM_B2 = 0.999
ADAM_EPS = 1e-08
ADAM_WD = 0.01
ADAM_STEP = 10
VMEM_LIMIT = 56 * 1024 * 1024
N_CHIPS = 4
N_DEV = 8


def _params(sem=None):
    return pltpu.CompilerParams(dimension_semantics=sem, vmem_limit_bytes=VMEM_LIMIT)


def _largest_divisor(n, cap, multiple=1):
    best = None
    for d in range(multiple, min(n, cap) + 1, multiple):
        if n % d == 0:
            best = d
    assert best is not None, (n, cap, multiple)
    return best


def _sigmoid(x):
    return jax.nn.sigmoid(x)


def _silu(x):
    return x * jax.nn.sigmoid(x)


def _dot(a, b, dims, precision=None):
    return lax.dot_general(a, b, (dims, ((), ())), precision=precision, preferred_element_type=F32)


def _nn(a, b, precision=None):
    return _dot(a, b, ((1,), (0,)), precision)


def _nt(a, b, precision=None):
    return _dot(a, b, ((1,), (1,)), precision)


def _tn(a, b, precision=None):
    return _dot(a, b, ((0,), (0,)), precision)


def _visible(rev):
    r = lax.broadcasted_iota(jnp.int32, (CHUNK, CHUNK), 0)
    c = lax.broadcasted_iota(jnp.int32, (CHUNK, CHUNK), 1)
    return (r <= c) if rev else (r >= c)


def _hg_chunk(states, aq, af, ai, lb0, lb1, rev):
    n_heads = len(states)
    lb = _sigmoid(lb0 - lb1)
    f = lb + (1.0 - lb) * _sigmoid(af)
    g = jnp.log(f)
    k = 1.0 - f
    q = _silu(aq)
    vis = _visible(rev)
    b = _nn(vis.astype(F32), g, HIGHEST)
    last = 0 if rev else CHUNK - 1
    b_end = b[last:last + 1]
    b_mid = b[CHUNK // 2:CHUNK // 2 + 1]
    q_inter = q * jnp.exp(b)
    q_intra = q * jnp.exp(b - b_mid)
    k_intra = k * jnp.exp(b_mid - b)
    k_dec = k * jnp.exp(b_end - b)
    e_end = jnp.exp(b_end)
    new_states, outs = [], []
    for h in range(n_heads):
        sl = slice(h * HG_DK, (h + 1) * HG_DK)
        s_t = states[h]
        scores = jnp.where(vis, _nt(q_intra[:, sl], k_intra[:, sl]), 0.0)
        outs.append(_nt(q_inter[:, sl], s_t) + _nn(scores, ai[:, sl]))
        new_states.append(e_end[:, sl] * s_t + _tn(ai[:, sl], k_dec[:, sl]))
    return new_states, jnp.concatenate(outs, axis=1)


def _ml_chunk(state, q, k, v, g, gb, rev, d):
    cms, nvs, mbs = state
    n_heads = len(cms)
    dh = q.shape[1] // n_heads
    ga = g + gb
    log_f_all = jax.nn.log_sigmoid(ga)
    vis = _visible(rev)
    b_all = _nn(vis.astype(F32), log_f_all, HIGHEST)
    last = 0 if rev else CHUNK - 1
    k = k * (dh ** -0.5)
    new_c, new_n, new_m, outs = [], [], [], []
    for h in range(n_heads):
        ci = d * n_heads + h
        cf = (2 + d) * n_heads + h
        sl = slice(h * dh, (h + 1) * dh)
        qh, kh, vh = q[:, sl], k[:, sl], v[:, sl]
        li = ga[:, ci:ci + 1]
        b = b_all[:, cf:cf + 1]
        m = mbs[h][:, 0:1]
        row = jnp.transpose(li - b)
        log_w = jnp.where(vis, b + row, -jnp.inf)
        m_inter = b + m
        m_t = jnp.maximum(m_inter, jnp.max(log_w, axis=-1, keepdims=True))
        w_inter = jnp.exp(m_inter - m_t)
        w_qk = jnp.exp(log_w - m_t) * _nt(qh, kh)
        num = w_inter * _nt(qh, cms[h]) + _nn(w_qk, vh)
        den = w_inter * jnp.sum(qh * nvs[h], axis=-1, keepdims=True) + jnp.sum(w_qk, axis=-1, keepdims=True)
        outs.append(num / jnp.maximum(jnp.abs(den), jnp.exp(-m_t)))
        m_new = m_t[last:last + 1]
        b_end = b[last:last + 1]
        w_s = jnp.exp(b_end - b + li - m_new)
        decay = jnp.exp(b_end + m - m_new)
        new_c.append(decay * cms[h] + _tn(w_s * vh, kh))
        new_n.append(decay * nvs[h] + jnp.sum(w_s * kh, axis=0, keepdims=True))
        new_m.append(jnp.broadcast_to(m_new, (1, LANE)))
    return (new_c, new_n, new_m), jnp.concatenate(outs, axis=1)


def _post_fn(o_f, o_b, az, h_f, h_b, bo, bz, wa, wb, n_hg, n_ml):
    o = o_f + o_b
    parts = []
    for h in range(n_hg):
        s = o[:, h * HG_DK:(h + 1) * HG_DK]
        parts.append(s * lax.rsqrt(jnp.mean(s * s, axis=-1, keepdims=True) + NORM_EPS))
    y_a = jnp.concatenate(parts, axis=1) * wa * _silu(az)
    hh = h_f + h_b
    dh = hh.shape[1] // n_ml
    parts = []
    for h in range(n_ml):
        s = hh[:, h * dh:(h + 1) * dh]
        mu = jnp.mean(s, axis=-1, keepdims=True)
        sc = s - mu
        parts.append(sc * lax.rsqrt(jnp.mean(sc * sc, axis=-1, keepdims=True) + NORM_EPS))
    y_b = jnp.concatenate(parts, axis=1) * wb * _sigmoid(bo) * _silu(bz)
    return jnp.concatenate([y_a, y_b], axis=1)


def _chip_of(dev):
    return 2 * dev[0] + dev[1]


def _index_of(dev):
    return 4 * dev[0] + 2 * dev[1] + dev[2]


def _exchange(name, srcs, out_shapes, transfers, local_copies=()):
    n_in, n_out, n_t, n_l = len(srcs), len(out_shapes), len(transfers), len(local_copies)

    def body(*refs):
        ins, outs = refs[:n_in], refs[n_in:n_in + n_out]
        send_sems, recv_sems, local_sems = refs[n_in + n_out:]
        me = (lax.axis_index("x"), lax.axis_index("y"), lax.axis_index("c"))

        def pick(ref, fn, *who):
            return ref if fn is None else ref.at[fn(*who)]

        sends, recvs, locs = [], [], []
        for t, (mask, si, sfn, di, dfn) in enumerate(transfers):
            peer = tuple(1 - p if flip else p for p, flip in zip(me, mask))
            sends.append(pltpu.make_async_remote_copy(
                src_ref=pick(ins[si], sfn, me, peer), dst_ref=pick(outs[di], dfn, me, peer),
                send_sem=send_sems.at[t], recv_sem=recv_sems.at[t], device_id=peer, device_id_type=MESH))
            landing = pick(outs[di], dfn, peer, me)
            recvs.append(pltpu.make_async_remote_copy(
                src_ref=landing, dst_ref=landing,
                send_sem=send_sems.at[t], recv_sem=recv_sems.at[t], device_id=peer, device_id_type=MESH))
        for l, (si, sfn, di, dfn) in enumerate(local_copies):
            locs.append(pltpu.make_async_copy(pick(ins[si], sfn, me), pick(outs[di], dfn, me), local_sems.at[l]))
        for cp in locs + sends:
            cp.start()
        for cp in recvs:
            cp.wait_recv()
        for cp in sends:
            cp.wait_send()
        for cp in locs:
            cp.wait()

    hbm = pl.BlockSpec(memory_space=pltpu.HBM)
    return pl.pallas_call(
        body, name=name, out_shape=tuple(out_shapes),
        in_specs=[hbm] * n_in, out_specs=tuple([hbm] * n_out),
        scratch_shapes=[pltpu.SemaphoreType.DMA((n_t,)), pltpu.SemaphoreType.DMA((n_t,)),
                        pltpu.SemaphoreType.DMA((max(n_l, 1),))],
    )(*srcs)


ALL_MASKS = [(mx, my, mc) for mx in (0, 1) for my in (0, 1) for mc in (0, 1)][1:]
CHIP_MASKS = [(1, 0, 0), (0, 1, 0), (1, 1, 0)]
SIBLING_MASK = (0, 0, 1)


def _all_gather8(name, v):
    out = jax.ShapeDtypeStruct((N_DEV,) + v.shape, v.dtype)
    slot = lambda sender, receiver: _index_of(sender)
    transfers = [(mask, 0, None, 0, slot) for mask in ALL_MASKS]
    return _exchange(name, [v], [out], transfers, [(0, None, 0, lambda me: _index_of(me))])[0]


def _all_gather_chips(name, arrays):
    outs = [jax.ShapeDtypeStruct((N_CHIPS,) + a.shape, a.dtype) for a in arrays]
    slot = lambda sender, receiver: _chip_of(sender)
    transfers = [(mask, i, None, i, slot) for i in range(len(arrays)) for mask in CHIP_MASKS]
    return _exchange(name, arrays, outs, transfers)


def _sibling_swap(name, arrays):
    outs = [jax.ShapeDtypeStruct(a.shape, a.dtype) for a in arrays]
    return _exchange(name, arrays, outs, [(SIBLING_MASK, i, None, i, None) for i in range(len(arrays))])


def _chip_scatter(name, arrays):
    outs = [jax.ShapeDtypeStruct(a.shape, a.dtype) for a in arrays]
    transfers = [(mask, i, lambda s, r: _chip_of(r), i, lambda s, r: _chip_of(s))
                 for i in range(len(arrays)) for mask in CHIP_MASKS]
    return _exchange(name, arrays, outs, transfers)


def _own_block(chip, own, blocks):
    sel = (lax.broadcasted_iota(jnp.int32, (N_CHIPS,) + (1,) * (blocks.ndim - 1), 0) == chip)
    return jnp.where(sel, own if own.ndim == blocks.ndim else own[None], blocks)


def _join_halves(ci, mine, other, axis):
    return jnp.where(ci == 0, jnp.concatenate([mine, other], axis=axis), jnp.concatenate([other, mine], axis=axis))


def _mm_nn(name, a, b, tm, tn, out_dtype):
    m, k = a.shape
    n = b.shape[1]

    def body(a_ref, b_ref, o_ref):
        o_ref[...] = _nn(a_ref[...], b_ref[...]).astype(out_dtype)

    return pl.pallas_call(
        body, name=name, grid=(n // tn, m // tm),
        in_specs=[pl.BlockSpec((tm, k), lambda j, i: (i, 0)), pl.BlockSpec((k, tn), lambda j, i: (0, j))],
        out_specs=pl.BlockSpec((tm, tn), lambda j, i: (i, j)),
        out_shape=jax.ShapeDtypeStruct((m, n), out_dtype),
        compiler_params=_params(("parallel", "parallel")),
    )(a, b)


def _mm_nt(name, a, b, tm, tk):
    m, kc = a.shape
    n = b.shape[0]

    def body(a_ref, b_ref, o_ref):
        @pl.when(pl.program_id(1) == 0)
        def _():
            o_ref[...] = jnp.zeros_like(o_ref)
        o_ref[...] += _nt(a_ref[...], b_ref[...])

    return pl.pallas_call(
        body, name=name, grid=(m // tm, kc // tk),
        in_specs=[pl.BlockSpec((tm, tk), lambda i, kk: (i, kk)), pl.BlockSpec((n, tk), lambda i, kk: (0, kk))],
        out_specs=pl.BlockSpec((tm, n), lambda i, kk: (i, 0)),
        out_shape=jax.ShapeDtypeStruct((m, n), F32),
        compiler_params=_params(("parallel", "arbitrary")),
    )(a, b)


def _mm_tn(name, a, b, tk, tn):
    kr, m = a.shape
    n = b.shape[1]

    def body(a_ref, b_ref, o_ref):
        @pl.when(pl.program_id(1) == 0)
        def _():
            o_ref[...] = jnp.zeros_like(o_ref)
        o_ref[...] += _tn(a_ref[...], b_ref[...])

    return pl.pallas_call(
        body, name=name, grid=(n // tn, kr // tk),
        in_specs=[pl.BlockSpec((tk, m), lambda j, kk: (kk, 0)), pl.BlockSpec((tk, tn), lambda j, kk: (kk, j))],
        out_specs=pl.BlockSpec((m, tn), lambda j, kk: (0, j)),
        out_shape=jax.ShapeDtypeStruct((m, n), F32),
        compiler_params=_params(("parallel", "arbitrary")),
    )(a, b)


def _modulate_fwd(xc, prm, t_rows, tm):
    r, dm = xc.shape
    first_ctx = t_rows // tm

    def body(x_ref, p_ref, h_ref):
        x = x_ref[...]
        mu = jnp.mean(x, axis=-1, keepdims=True)
        xm = x - mu
        n = xm * lax.rsqrt(jnp.mean(xm * xm, axis=-1, keepdims=True) + LN_EPS)
        h_ref[...] = (n * (1.0 + p_ref[0, 1:2, :]) + p_ref[0, 0:1, :]).astype(BF16)

    return pl.pallas_call(
        body, name="modulate_fwd", grid=(r // tm,),
        in_specs=[pl.BlockSpec((tm, dm), lambda i: (i, 0)),
                  pl.BlockSpec((1, 8, dm), lambda i: ((i >= first_ctx).astype(jnp.int32), 0, 0))],
        out_specs=pl.BlockSpec((tm, dm), lambda i: (i, 0)),
        out_shape=jax.ShapeDtypeStruct((r, dm), BF16),
        compiler_params=_params(("parallel",)),
    )(xc, prm)


def _modulate_bwd(xc, dh, prm, gx_direct, t_rows, tm):
    r, dm = xc.shape
    first_ctx = t_rows // tm
    cls = lambda i: (i >= first_ctx).astype(jnp.int32)

    def body(x_ref, dh_ref, p_ref, gd_ref, gx_ref, acc_ref):
        i = pl.program_id(0)

        @pl.when((i == 0) | (i == first_ctx))
        def _():
            acc_ref[...] = jnp.zeros_like(acc_ref)

        x = x_ref[...]
        dh_v = dh_ref[...]
        mu = jnp.mean(x, axis=-1, keepdims=True)
        xm = x - mu
        rstd = lax.rsqrt(jnp.mean(xm * xm, axis=-1, keepdims=True) + LN_EPS)
        n = xm * rstd
        acc_ref[0, 0:1, :] += jnp.sum(dh_v, axis=0, keepdims=True)
        acc_ref[0, 1:2, :] += jnp.sum(dh_v * n, axis=0, keepdims=True)
        dn = dh_v * (1.0 + p_ref[0, 1:2, :])
        dx = rstd * (dn - jnp.mean(dn, axis=-1, keepdims=True) - n * jnp.mean(dn * n, axis=-1, keepdims=True))
        gx_ref[...] = dx + gd_ref[...]

    return pl.pallas_call(
        body, name="modulate_bwd", grid=(r // tm,),
        in_specs=[pl.BlockSpec((tm, dm), lambda i: (i, 0)), pl.BlockSpec((tm, dm), lambda i: (i, 0)),
                  pl.BlockSpec((1, 8, dm), lambda i: (cls(i), 0, 0)),
                  pl.BlockSpec((tm, dm), lambda i: (jnp.minimum(i, first_ctx - 1), 0))],
        out_specs=(pl.BlockSpec((tm, dm), lambda i: (i, 0)), pl.BlockSpec((1, 8, dm), lambda i: (cls(i), 0, 0))),
        out_shape=(jax.ShapeDtypeStruct((r, dm), F32), jax.ShapeDtypeStruct((2, 8, dm), F32)),
        compiler_params=_params(("arbitrary",)),
    )(xc, dh, prm, gx_direct)


def _conv_parts(t_rows, c_rows):
    return ((0, t_rows, t_rows // GRID_W, GRID_W), (t_rows, c_rows, 1, c_rows))


def _tap_valid(n, rows_g, width_g, a, b, lanes):
    t = lax.broadcasted_iota(jnp.int32, (n, lanes), 0)
    shift = width_g.bit_length() - 1
    assert 1 << shift == width_g
    rr = jnp.right_shift(t, shift) + (a - 1)
    cc = jnp.bitwise_and(t, width_g - 1) + (b - 1)
    return (rr >= 0) & (rr < rows_g) & (cc >= 0) & (cc < width_g)


def _rows_from(x, off):
    s = (-off) % x.shape[0]
    return x if s == 0 else pltpu.roll(x, s, 0)


def _conv_pre(xs, w_ref, b_ref, rows_g, width_g):
    acc = jnp.broadcast_to(b_ref[...], xs.shape)
    for a in range(3):
        if rows_g == 1 and a != 1:
            continue
        for b in range(3):
            off = (a - 1) * width_g + (b - 1)
            valid = _tap_valid(xs.shape[0], rows_g, width_g, a, b, xs.shape[1])
            acc = acc + jnp.where(valid, _rows_from(xs, off), 0.0) * w_ref[a * 3 + b:a * 3 + b + 1, :]
    return acc


def _conv_fwd(u, conv_w9, conv_b, t_rows, c_rows, w, ct):
    r = u.shape[0]
    base = 5 * w // ct

    def body(x_ref, w_ref, b_ref, o_ref):
        for r0, n, rows_g, width_g in _conv_parts(t_rows, c_rows):
            o_ref[r0:r0 + n, :] = _silu(_conv_pre(x_ref[r0:r0 + n, :], w_ref, b_ref, rows_g, width_g))

    return pl.pallas_call(
        body, name="conv_fwd", grid=(2 * w // ct,),
        in_specs=[pl.BlockSpec((r, ct), lambda i: (0, base + i)), pl.BlockSpec((9, ct), lambda i: (0, i)),
                  pl.BlockSpec((1, ct), lambda i: (0, i))],
        out_specs=pl.BlockSpec((r, ct), lambda i: (0, i)),
        out_shape=jax.ShapeDtypeStruct((r, 2 * w), F32),
        compiler_params=_params(("parallel",)),
    )(u, conv_w9, conv_b)


def _conv_bwd(u, dqk_pair, conv_w9, conv_b, t_rows, c_rows, w, ct):
    r = u.shape[0]
    base = 5 * w // ct

    def body(x_ref, d1_ref, d2_ref, w_ref, b_ref, dx_ref, dw_ref, db_ref):
        dw = [jnp.zeros((1, ct), F32) for _ in range(9)]
        db = jnp.zeros((1, ct), F32)
        for r0, n, rows_g, width_g in _conv_parts(t_rows, c_rows):
            xs = x_ref[r0:r0 + n, :]
            pre = _conv_pre(xs, w_ref, b_ref, rows_g, width_g)
            sg = _sigmoid(pre)
            dpre = (d1_ref[r0:r0 + n, :] + d2_ref[r0:r0 + n, :]) * (sg * (1.0 + pre * (1.0 - sg)))
            db = db + jnp.sum(dpre, axis=0, keepdims=True)
            dx = jnp.zeros_like(xs)
            for a in range(3):
                if rows_g == 1 and a != 1:
                    continue
                for b in range(3):
                    off = (a - 1) * width_g + (b - 1)
                    valid = _tap_valid(n, rows_g, width_g, a, b, ct)
                    dw[a * 3 + b] = dw[a * 3 + b] + jnp.sum(
                        jnp.where(valid, _rows_from(xs, off), 0.0) * dpre, axis=0, keepdims=True)
                    dx = dx + _rows_from(jnp.where(valid, dpre, 0.0) * w_ref[a * 3 + b:a * 3 + b + 1, :], -off)
            dx_ref[r0:r0 + n, :] = dx
        for tap in range(9):
            dw_ref[tap:tap + 1, :] = dw[tap]
        db_ref[...] = db

    return pl.pallas_call(
        body, name="conv_bwd", grid=(2 * w // ct,),
        in_specs=[pl.BlockSpec((r, ct), lambda i: (0, base + i)), pl.BlockSpec((r, ct), lambda i: (0, i)),
                  pl.BlockSpec((r, ct), lambda i: (0, i)),
                  pl.BlockSpec((9, ct), lambda i: (0, i)), pl.BlockSpec((1, ct), lambda i: (0, i))],
        out_specs=(pl.BlockSpec((r, ct), lambda i: (0, i)), pl.BlockSpec((9, ct), lambda i: (0, i)),
                   pl.BlockSpec((1, ct), lambda i: (0, i))),
        out_shape=(jax.ShapeDtypeStruct((r, 2 * w), F32), jax.ShapeDtypeStruct((9, 2 * w), F32),
                   jax.ShapeDtypeStruct((1, 2 * w), F32)),
        compiler_params=_params(("parallel",)),
    )(u, dqk_pair[0], dqk_pair[1], conv_w9, conv_b)


def _assemble_du(groups, gates, n_pad, tm):
    flat, layout = [], []
    for entry in list(groups) + [gates]:
        parts = entry if isinstance(entry, (tuple, list)) else (entry,)
        layout.append((len(flat), len(parts), parts[0].shape[1]))
        flat += list(parts)
    r = flat[0].shape[0]

    def body(*refs):
        o_ref = refs[-1]
        col = 0
        for first, count, width in layout:
            val = refs[first][...]
            for extra in range(1, count):
                val = val + refs[first + extra][...]
            o_ref[:, col:col + width] = val.astype(BF16)
            col += width
        assert col == n_pad

    return pl.pallas_call(
        body, name="assemble_du", grid=(r // tm,),
        in_specs=[pl.BlockSpec((tm, a.shape[1]), lambda i: (i, 0)) for a in flat],
        out_specs=pl.BlockSpec((tm, n_pad), lambda i: (i, 0)),
        out_shape=jax.ShapeDtypeStruct((r, n_pad), BF16),
        compiler_params=_params(("parallel",)),
    )(*flat)


def _scan_order(n_lat, n_ctx, rev):
    n = n_lat + n_ctx
    if rev:
        return lambda j: n - 1 - j
    return lambda j: (j + n_lat) % n


DIRS = (False, True)


def _hg_scan_fwd(u, lb_full, w, n_lat, n_ctx):
    r = u.shape[0]
    n_heads = w // HG_DK
    n_chunks = n_lat + n_ctx
    nat = [_scan_order(n_lat, n_ctx, rev) for rev in DIRS]

    def body(*refs):
        ins, outs, scratch = refs[:8], refs[8:12], refs[12:]
        for d, rev in enumerate(DIRS):
            aq, af, ai, lb_ref = ins[4 * d:4 * d + 4]
            o_ref, save_ref = outs[2 * d:2 * d + 2]
            s_ref = scratch[d]

            @pl.when(pl.program_id(0) == 0)
            def _():
                s_ref[...] = jnp.zeros_like(s_ref)
            save_ref[0] = s_ref[...]
            new, o = _hg_chunk([s_ref[h] for h in range(n_heads)], aq[...], af[...], ai[...],
                               lb_ref[0, 0:1, :], lb_ref[0, 1:2, :], rev)
            o_ref[...] = o
            for h in range(n_heads):
                s_ref[h] = new[h]

    in_specs, out_specs, out_shape = [], [], []
    for d in range(2):
        in_specs += [pl.BlockSpec((CHUNK, w), lambda j, d=d: (nat[d](j), 0)),
                     pl.BlockSpec((CHUNK, w), lambda j, d=d: (nat[d](j), 1 + d)),
                     pl.BlockSpec((CHUNK, w), lambda j, d=d: (nat[d](j), 3)),
                     pl.BlockSpec((1, 2, w), lambda j, d=d: (d, 0, 0))]
        out_specs += [pl.BlockSpec((CHUNK, w), lambda j, d=d: (nat[d](j), 0)),
                      pl.BlockSpec((1, n_heads, HG_DK, HG_DK), lambda j: (j, 0, 0, 0))]
        out_shape += [jax.ShapeDtypeStruct((r, w), F32),
                      jax.ShapeDtypeStruct((n_chunks, n_heads, HG_DK, HG_DK), F32)]
    o_f, s_f, o_b, s_b = pl.pallas_call(
        body, name="hg_scan_fwd", grid=(n_chunks,), in_specs=in_specs, out_specs=tuple(out_specs),
        out_shape=tuple(out_shape), scratch_shapes=[pltpu.VMEM((n_heads, HG_DK, HG_DK), F32)] * 2,
        compiler_params=_params(("arbitrary",)),
    )(u, u, u, lb_full, u, u, u, lb_full)
    return (o_f, o_b), (s_f, s_b)


def _hg_scan_bwd(u, lb_full, saved, d_o, w, n_lat, n_ctx):
    r = u.shape[0]
    n_heads = w // HG_DK
    n_chunks = n_lat + n_ctx
    step = lambda jj: n_chunks - 1 - jj
    nat = [(lambda jj, o=_scan_order(n_lat, n_ctx, rev): o(step(jj))) for rev in DIRS]

    def body(*refs):
        ins, outs, scratch = refs[:12], refs[12:20], refs[20:]
        jj = pl.program_id(0)
        for d, rev in enumerate(DIRS):
            aq, af, ai, lb_ref, save_ref, do_ref = ins[6 * d:6 * d + 6]
            daq_ref, daf_ref, dai_ref, dlb_ref = outs[4 * d:4 * d + 4]
            ds_ref = scratch[d]

            @pl.when(jj == 0)
            def _():
                ds_ref[...] = jnp.zeros_like(ds_ref)
                dlb_ref[...] = jnp.zeros_like(dlb_ref)

            f = lambda st, a, b, c, l0, l1, rev=rev: _hg_chunk(st, a, b, c, l0, l1, rev)
            _, vjp = jax.vjp(f, [save_ref[0, h] for h in range(n_heads)], aq[...], af[...], ai[...],
                             lb_ref[0, 0:1, :], lb_ref[0, 1:2, :])
            d_out = do_ref[...] * (nat[d](jj) < n_lat).astype(F32)
            dst, daq, daf, dai, dl0, dl1 = vjp(([ds_ref[h] for h in range(n_heads)], d_out))
            for h in range(n_heads):
                ds_ref[h] = dst[h]
            daq_ref[...] = daq
            daf_ref[...] = daf
            dai_ref[...] = dai
            dlb_ref[0:1, :] += dl0
            dlb_ref[1:2, :] += dl1

    in_specs, out_specs, out_shape, operands = [], [], [], []
    for d in range(2):
        row = lambda jj, d=d: (nat[d](jj), 0)
        in_specs += [pl.BlockSpec((CHUNK, w), row),
                     pl.BlockSpec((CHUNK, w), lambda jj, d=d: (nat[d](jj), 1 + d)),
                     pl.BlockSpec((CHUNK, w), lambda jj, d=d: (nat[d](jj), 3)),
                     pl.BlockSpec((1, 2, w), lambda jj, d=d: (d, 0, 0)),
                     pl.BlockSpec((1, n_heads, HG_DK, HG_DK), lambda jj: (step(jj), 0, 0, 0)),
                     pl.BlockSpec((CHUNK, w), lambda jj, d=d: (jnp.minimum(nat[d](jj), n_lat - 1), 0))]
        operands += [u, u, u, lb_full, saved[d], d_o]
        out_specs += [pl.BlockSpec((CHUNK, w), row)] * 3 + [pl.BlockSpec((2, w), lambda jj: (0, 0))]
        out_shape += [jax.ShapeDtypeStruct((r, w), F32)] * 3 + [jax.ShapeDtypeStruct((2, w), F32)]
    res = pl.pallas_call(
        body, name="hg_scan_bwd", grid=(n_chunks,), in_specs=in_specs, out_specs=tuple(out_specs),
        out_shape=tuple(out_shape), scratch_shapes=[pltpu.VMEM((n_heads, HG_DK, HG_DK), F32)] * 2,
        compiler_params=_params(("arbitrary",)),
    )(*operands)
    return res[0:4], res[4:8]


def _ml_state_shapes(n_chunks, n_heads, dh):
    return (jax.ShapeDtypeStruct((n_chunks, n_heads, dh, dh), F32),
            jax.ShapeDtypeStruct((n_chunks, n_heads, 1, dh), F32),
            jax.ShapeDtypeStruct((n_chunks, n_heads, 1, LANE), F32))


def _ml_state_specs(n_heads, dh, index):
    return (pl.BlockSpec((1, n_heads, dh, dh), lambda j: (index(j), 0, 0, 0)),
            pl.BlockSpec((1, n_heads, 1, dh), lambda j: (index(j), 0, 0, 0)),
            pl.BlockSpec((1, n_heads, 1, LANE), lambda j: (index(j), 0, 0, 0)))


def _ml_state_scratch(n_heads, dh):
    return [pltpu.VMEM((n_heads, dh, dh), F32), pltpu.VMEM((n_heads, 1, dh), F32), pltpu.VMEM((n_heads, 1, LANE), F32)]


def _ml_scan_fwd(qk, u, gate_b, w, n_heads, n_lat, n_ctx):
    r = u.shape[0]
    dh = w // n_heads
    n_chunks = n_lat + n_ctx
    nat = [_scan_order(n_lat, n_ctx, rev) for rev in DIRS]

    def body(*refs):
        ins, outs, scratch = refs[:10], refs[10:18], refs[18:]
        for d, rev in enumerate(DIRS):
            q, k, v, g, gb = ins[5 * d:5 * d + 5]
            h_ref, sc, sn, sm = outs[4 * d:4 * d + 4]
            c_ref, n_ref, m_ref = scratch[3 * d:3 * d + 3]

            @pl.when(pl.program_id(0) == 0)
            def _():
                c_ref[...] = jnp.zeros_like(c_ref)
                n_ref[...] = jnp.zeros_like(n_ref)
                m_ref[...] = jnp.zeros_like(m_ref)
            sc[0] = c_ref[...]
            sn[0] = n_ref[...]
            sm[0] = m_ref[...]
            state = ([c_ref[h] for h in range(n_heads)], [n_ref[h] for h in range(n_heads)],
                     [m_ref[h] for h in range(n_heads)])
            (nc, nn_, nm), o = _ml_chunk(state, q[...], k[...], v[...], g[...], gb[...], rev, d)
            h_ref[...] = o
            for h in range(n_heads):
                c_ref[h] = nc[h]
                n_ref[h] = nn_[h]
                m_ref[h] = nm[h]

    in_specs, out_specs, out_shape = [], [], []
    for d in range(2):
        in_specs += [pl.BlockSpec((CHUNK, w), lambda j, d=d: (nat[d](j), 0)),
                     pl.BlockSpec((CHUNK, w), lambda j, d=d: (nat[d](j), 1)),
                     pl.BlockSpec((CHUNK, w), lambda j, d=d: (nat[d](j), 7)),
                     pl.BlockSpec((CHUNK, LANE), lambda j, d=d: (nat[d](j), 10 * w // LANE)),
                     pl.BlockSpec((1, LANE), lambda j: (0, 0))]
        out_specs += [pl.BlockSpec((CHUNK, w), lambda j, d=d: (nat[d](j), 0))]
        out_specs += list(_ml_state_specs(n_heads, dh, lambda j: j))
        out_shape += [jax.ShapeDtypeStruct((r, w), F32)] + list(_ml_state_shapes(n_chunks, n_heads, dh))
    res = pl.pallas_call(
        body, name="ml_scan_fwd", grid=(n_chunks,), in_specs=in_specs, out_specs=tuple(out_specs),
        out_shape=tuple(out_shape), scratch_shapes=_ml_state_scratch(n_heads, dh) * 2,
        compiler_params=_params(("arbitrary",)),
    )(qk, qk, u, u, gate_b, qk, qk, u, u, gate_b)
    return (res[0], res[4]), (res[1:4], res[5:8])


def _ml_scan_bwd(qk, u, gate_b, saved, d_h, w, n_heads, n_lat, n_ctx):
    r = u.shape[0]
    dh = w // n_heads
    n_chunks = n_lat + n_ctx
    step = lambda jj: n_chunks - 1 - jj
    nat = [(lambda jj, o=_scan_order(n_lat, n_ctx, rev): o(step(jj))) for rev in DIRS]

    def body(*refs):
        ins, outs, scratch = refs[:18], refs[18:26], refs[26:]
        jj = pl.program_id(0)
        for d, rev in enumerate(DIRS):
            q, k, v, g, gb, sc, sn, sm, dh_ref = ins[9 * d:9 * d + 9]
            dqk_ref, dv_ref, dg_ref, dgb_ref = outs[4 * d:4 * d + 4]
            dc_ref, dn_ref, dm_ref = scratch[3 * d:3 * d + 3]

            @pl.when(jj == 0)
            def _():
                dc_ref[...] = jnp.zeros_like(dc_ref)
                dn_ref[...] = jnp.zeros_like(dn_ref)
                dm_ref[...] = jnp.zeros_like(dm_ref)
                dgb_ref[...] = jnp.zeros_like(dgb_ref)

            state = ([sc[0, h] for h in range(n_heads)], [sn[0, h] for h in range(n_heads)],
                     [sm[0, h] for h in range(n_heads)])
            f = lambda st, a, b, c, gg, bb, rev=rev, d=d: _ml_chunk(st, a, b, c, gg, bb, rev, d)
            _, vjp = jax.vjp(f, state, q[...], k[...], v[...], g[...], gb[...])
            d_state = ([dc_ref[h] for h in range(n_heads)], [dn_ref[h] for h in range(n_heads)],
                       [dm_ref[h] for h in range(n_heads)])
            d_out = dh_ref[...] * (nat[d](jj) < n_lat).astype(F32)
            (dc, dn, dm), dq, dk, dv, dg, dgb = vjp((d_state, d_out))
            for h in range(n_heads):
                dc_ref[h] = dc[h]
                dn_ref[h] = dn[h]
                dm_ref[h] = dm[h]
            dqk_ref[:, 0:w] = dq
            dqk_ref[:, w:2 * w] = dk
            dv_ref[...] = dv
            dg_ref[...] = dg
            dgb_ref[...] += dgb

    in_specs, out_specs, out_shape, operands = [], [], [], []
    for d in range(2):
        row = lambda jj, d=d: (nat[d](jj), 0)
        in_specs += [pl.BlockSpec((CHUNK, w), row), pl.BlockSpec((CHUNK, w), lambda jj, d=d: (nat[d](jj), 1)),
                     pl.BlockSpec((CHUNK, w), lambda jj, d=d: (nat[d](jj), 7)),
                     pl.BlockSpec((CHUNK, LANE), lambda jj, d=d: (nat[d](jj), 10 * w // LANE)),
                     pl.BlockSpec((1, LANE), lambda jj: (0, 0))]
        in_specs += list(_ml_state_specs(n_heads, dh, step))
        in_specs += [pl.BlockSpec((CHUNK, w), lambda jj, d=d: (jnp.minimum(nat[d](jj), n_lat - 1), 0))]
        operands += [qk, qk, u, u, gate_b, *saved[d], d_h]
        out_specs += [pl.BlockSpec((CHUNK, 2 * w), row), pl.BlockSpec((CHUNK, w), row),
                      pl.BlockSpec((CHUNK, LANE), row), pl.BlockSpec((1, LANE), lambda jj: (0, 0))]
        out_shape += [jax.ShapeDtypeStruct((r, 2 * w), F32), jax.ShapeDtypeStruct((r, w), F32),
                      jax.ShapeDtypeStruct((r, LANE), F32), jax.ShapeDtypeStruct((1, LANE), F32)]
    res = pl.pallas_call(
        body, name="ml_scan_bwd", grid=(n_chunks,), in_specs=in_specs, out_specs=tuple(out_specs),
        out_shape=tuple(out_shape), scratch_shapes=_ml_state_scratch(n_heads, dh) * 2,
        compiler_params=_params(("arbitrary",)),
    )(*operands)
    return res[0:4], res[4:8]


def _post_specs(w, tm, lat_tiles, cols):
    return [pl.BlockSpec((tm, w), (lambda i, cb=cb: (jnp.minimum(i, lat_tiles - 1), cb))) for cb in cols]


def _post_fwd(o_f, o_b, h_f, h_b, u, wa, wb, t_rows, w, n_hg, n_ml, tm):
    lat_tiles = t_rows // tm

    def body(of, ob, hf, hb, az, bo, bz, wa_ref, wb_ref, y_ref):
        y_ref[...] = _post_fn(of[...], ob[...], az[...], hf[...], hb[...], bo[...], bz[...],
                              wa_ref[...], wb_ref[...], n_hg, n_ml).astype(BF16)

    rows = pl.BlockSpec((tm, w), lambda i: (i, 0))
    vec = pl.BlockSpec((1, w), lambda i: (0, 0))
    return pl.pallas_call(
        body, name="post_fwd", grid=(lat_tiles,),
        in_specs=[rows] * 4 + _post_specs(w, tm, lat_tiles, (4, 8, 9)) + [vec, vec],
        out_specs=pl.BlockSpec((tm, 2 * w), lambda i: (i, 0)),
        out_shape=jax.ShapeDtypeStruct((t_rows, 2 * w), BF16),
        compiler_params=_params(("parallel",)),
    )(o_f, o_b, h_f, h_b, u, u, u, wa, wb)


def _post_bwd(o_f, o_b, h_f, h_b, u, wa, wb, dy, t_rows, w, n_hg, n_ml, tm):
    r = u.shape[0]
    lat_tiles = t_rows // tm
    lat = lambda i: (jnp.minimum(i, lat_tiles - 1), 0)

    def body(of, ob, hf, hb, az, bo, bz, wa_ref, wb_ref, dy_ref, do_ref, dh_ref, daz_ref, dbo_ref, dbz_ref,
             dwa_ref, dwb_ref):
        i = pl.program_id(0)

        @pl.when(i == 0)
        def _():
            dwa_ref[...] = jnp.zeros_like(dwa_ref)
            dwb_ref[...] = jnp.zeros_like(dwb_ref)

        @pl.when(i < lat_tiles)
        def _():
            f = functools.partial(_post_fn, n_hg=n_hg, n_ml=n_ml)
            _, vjp = jax.vjp(f, of[...], ob[...], az[...], hf[...], hb[...], bo[...], bz[...], wa_ref[...], wb_ref[...])
            d_of, _, d_az, d_hf, _, d_bo, d_bz, d_wa, d_wb = vjp(dy_ref[...])
            do_ref[...] = d_of
            dh_ref[...] = d_hf
            daz_ref[...] = d_az
            dbo_ref[...] = d_bo
            dbz_ref[...] = d_bz
            dwa_ref[...] += d_wa
            dwb_ref[...] += d_wb

        @pl.when(i >= lat_tiles)
        def _():
            daz_ref[...] = jnp.zeros_like(daz_ref)
            dbo_ref[...] = jnp.zeros_like(dbo_ref)
            dbz_ref[...] = jnp.zeros_like(dbz_ref)

    lat_rows = pl.BlockSpec((tm, w), lat)
    all_rows = pl.BlockSpec((tm, w), lambda i: (i, 0))
    vec = pl.BlockSpec((1, w), lambda i: (0, 0))
    sd_t = jax.ShapeDtypeStruct((t_rows, w), F32)
    sd_r = jax.ShapeDtypeStruct((r, w), F32)
    sd_v = jax.ShapeDtypeStruct((1, w), F32)
    return pl.pallas_call(
        body, name="post_bwd", grid=(r // tm,),
        in_specs=[lat_rows] * 4 + _post_specs(w, tm, lat_tiles, (4, 8, 9)) + [vec, vec]
        + [pl.BlockSpec((tm, 2 * w), lat)],
        out_specs=(lat_rows, lat_rows, all_rows, all_rows, all_rows, vec, vec),
        out_shape=(sd_t, sd_t, sd_r, sd_r, sd_r, sd_v, sd_v),
        compiler_params=_params(("arbitrary",)),
    )(o_f, o_b, h_f, h_b, u, u, u, wa, wb, dy)


OUT_ROW_GATE, OUT_ROW_LN_G, OUT_ROW_LN_B, OUT_ROW_LOSS = 0, 1, 2, 3


def _out_block(y, w_out, x, target, prm, tm):
    t_rows, dm = x.shape
    di = y.shape[1]

    def body(y_ref, w_ref, x_ref, t_ref, p_ref, dz_ref, dy_ref, gx_ref, acc_ref):
        @pl.when(pl.program_id(0) == 0)
        def _():
            acc_ref[...] = jnp.zeros_like(acc_ref)

        gate, ln_g, ln_b = p_ref[0:1, :], p_ref[1:2, :], p_ref[2:3, :]
        z = _nn(y_ref[...], w_ref[...])
        res = ALPHA * x_ref[...] + gate * z
        mu = jnp.mean(res, axis=-1, keepdims=True)
        rc = res - mu
        rstd = lax.rsqrt(jnp.mean(rc * rc, axis=-1, keepdims=True) + LN_EPS)
        rn = rc * rstd
        err = rn * ln_g + ln_b - t_ref[...]
        d_out = err * (1.0 / dm)
        d_rn = d_out * ln_g
        d_res = rstd * (d_rn - jnp.mean(d_rn, axis=-1, keepdims=True)
                        - rn * jnp.mean(d_rn * rn, axis=-1, keepdims=True))
        acc_ref[OUT_ROW_GATE:OUT_ROW_GATE + 1, :] += jnp.sum(d_res * z, axis=0, keepdims=True)
        acc_ref[OUT_ROW_LN_G:OUT_ROW_LN_G + 1, :] += jnp.sum(d_out * rn, axis=0, keepdims=True)
        acc_ref[OUT_ROW_LN_B:OUT_ROW_LN_B + 1, :] += jnp.sum(d_out, axis=0, keepdims=True)
        acc_ref[OUT_ROW_LOSS:OUT_ROW_LOSS + 1, :] += (0.5 / dm) * jnp.sum(err * err, axis=0, keepdims=True)
        gx_ref[...] = ALPHA * d_res
        dz = (d_res * gate).astype(BF16)
        dz_ref[...] = dz
        dy_ref[...] = _nt(dz, w_ref[...])

    rows_d = pl.BlockSpec((tm, dm), lambda i: (i, 0))
    rows_i = pl.BlockSpec((tm, di), lambda i: (i, 0))
    return pl.pallas_call(
        body, name="out_block", grid=(t_rows // tm,),
        in_specs=[rows_i, pl.BlockSpec((di, dm), lambda i: (0, 0)), rows_d, rows_d,
                  pl.BlockSpec((8, dm), lambda i: (0, 0))],
        out_specs=(rows_d, rows_i, rows_d, pl.BlockSpec((8, dm), lambda i: (0, 0))),
        out_shape=(jax.ShapeDtypeStruct((t_rows, dm), BF16), jax.ShapeDtypeStruct((t_rows, di), F32),
                   jax.ShapeDtypeStruct((t_rows, dm), F32), jax.ShapeDtypeStruct((8, dm), F32)),
        compiler_params=_params(("arbitrary",)),
    )(y, w_out, x, target, prm)


def _mod_fwd(c16, w_mod, tn):
    dm, n = w_mod.shape

    def body(c_ref, w_ref, o_ref, a_ref):
        a = _silu(c_ref[...])
        a_ref[...] = a
        o_ref[...] = _nn(a, w_ref[...], HIGHEST)

    return pl.pallas_call(
        body, name="mod_fwd", grid=(n // tn,),
        in_specs=[pl.BlockSpec((16, dm), lambda j: (0, 0)), pl.BlockSpec((dm, tn), lambda j: (0, j))],
        out_specs=(pl.BlockSpec((16, tn), lambda j: (0, j)), pl.BlockSpec((16, dm), lambda j: (0, 0))),
        out_shape=(jax.ShapeDtypeStruct((16, n), F32), jax.ShapeDtypeStruct((16, dm), F32)),
        compiler_params=_params(("arbitrary",)),
    )(c16, w_mod)


def _mod_bwd(a16, dm16, w_mod, tn):
    dm, n = w_mod.shape

    def body(a_ref, d_ref, w_ref, dw_ref, dc_ref):
        @pl.when(pl.program_id(0) == 0)
        def _():
            dc_ref[...] = jnp.zeros_like(dc_ref)
        dw_ref[...] = _tn(a_ref[...], d_ref[...], HIGHEST)
        dc_ref[...] += _nt(d_ref[...], w_ref[...], HIGHEST)

    return pl.pallas_call(
        body, name="mod_bwd", grid=(n // tn,),
        in_specs=[pl.BlockSpec((16, dm), lambda j: (0, 0)), pl.BlockSpec((16, tn), lambda j: (0, j)),
                  pl.BlockSpec((dm, tn), lambda j: (0, j))],
        out_specs=(pl.BlockSpec((dm, tn), lambda j: (0, j)), pl.BlockSpec((16, dm), lambda j: (0, 0))),
        out_shape=(jax.ShapeDtypeStruct((dm, n), F32), jax.ShapeDtypeStruct((16, dm), F32)),
        compiler_params=_params(("arbitrary",)),
    )(a16, dm16, w_mod)


def _sum_devices(g, fold_rows):
    n_dev, rows, n = g.shape

    def body(g_ref, s_ref, t_ref):
        s = g_ref[0]
        for dev in range(1, n_dev):
            s = s + g_ref[dev]
        t_ref[...] = jnp.broadcast_to(jnp.sum(s, axis=-1, keepdims=True), (rows, LANE))
        s_ref[...] = s
        s_ref[0:fold_rows, :] = s[0:fold_rows] + s[fold_rows:2 * fold_rows]

    return pl.pallas_call(
        body, name="sum_devices",
        out_shape=(jax.ShapeDtypeStruct((rows, n), F32), jax.ShapeDtypeStruct((rows, LANE), F32)),
        compiler_params=_params(),
    )(g)


def _c_ctx_grad(parts, c_ctx_row):
    def body(p_ref, c_ref, o_ref):
        s = p_ref[0]
        for chip in range(1, N_CHIPS):
            s = s + p_ref[2 * chip]
        cv = c_ref[...]
        sg = _sigmoid(cv)
        o_ref[...] = s * (sg * (1.0 + cv * (1.0 - sg)))

    return pl.pallas_call(
        body, name="c_ctx_grad", out_shape=jax.ShapeDtypeStruct(parts.shape[1:], F32), compiler_params=_params(),
    )(parts, c_ctx_row)


def _sum_pair(name, mine, got):
    def body(a_ref, b_ref, o_ref):
        o_ref[...] = (a_ref[...] + b_ref[...]).astype(BF16)

    k, rows, n = mine.shape
    tr = _largest_divisor(rows, max(SUBLANE_BF16, (1 << 18) // n), SUBLANE_BF16)
    spec = pl.BlockSpec((1, tr, n), lambda kk, i: (kk, i, 0))
    return pl.pallas_call(
        body, name=name, grid=(k, rows // tr), in_specs=[spec, spec], out_specs=spec,
        out_shape=jax.ShapeDtypeStruct(mine.shape, BF16), compiler_params=_params(("parallel", "parallel")),
    )(mine, got)


def _sum_chips(name, got):
    k, rows, n = got.shape
    tr = _largest_divisor(rows, max(SUBLANE_BF16, (1 << 18) // n), SUBLANE_BF16)

    def body(g_ref, o_ref):
        total = g_ref[0].astype(F32)
        for kk in range(1, k):
            total = total + g_ref[kk].astype(F32)
        o_ref[...] = total

    return pl.pallas_call(
        body, name=name, grid=(rows // tr,),
        in_specs=[pl.BlockSpec((k, tr, n), lambda i: (0, i, 0))], out_specs=pl.BlockSpec((tr, n), lambda i: (i, 0)),
        out_shape=jax.ShapeDtypeStruct((rows, n), F32), compiler_params=_params(("parallel",)),
    )(got)


def _adamw(name, w, g, m, v):
    rows, n = w.shape
    tr = rows if rows % 8 else _largest_divisor(rows, max(8, (1 << 18) // n), 8)

    def body(w_ref, g_ref, m_ref, v_ref, d_ref, mo_ref, vo_ref):
        gv = g_ref[...]
        m2 = ADAM_B1 * m_ref[...] + (1.0 - ADAM_B1) * gv
        v2 = ADAM_B2 * v_ref[...] + (1.0 - ADAM_B2) * jnp.square(gv)
        m_hat = m2 / (1.0 - ADAM_B1 ** ADAM_STEP)
        v_hat = v2 / (1.0 - ADAM_B2 ** ADAM_STEP)
        d_ref[...] = -ADAM_LR * (m_hat / (jnp.sqrt(v_hat) + ADAM_EPS) + ADAM_WD * w_ref[...])
        mo_ref[...] = m2
        vo_ref[...] = v2

    spec = pl.BlockSpec((tr, n), lambda i: (i, 0))
    sds = jax.ShapeDtypeStruct((rows, n), F32)
    return pl.pallas_call(
        body, name=name, grid=(rows // tr,), in_specs=[spec] * 4, out_specs=(spec,) * 3,
        out_shape=(sds, sds, sds), compiler_params=_params(("parallel",)),
    )(w, g, m, v)


PACK_LANES = 1024


def _pack(pieces):
    flat = jnp.concatenate([p.reshape(-1) for p in pieces])
    total = -(-flat.shape[0] // (8 * PACK_LANES)) * 8 * PACK_LANES
    return jnp.pad(flat, (0, total - flat.shape[0])).reshape(-1, PACK_LANES)


def _unpack(packed, shapes):
    flat = packed.reshape(-1)
    out, off = [], 0
    for shp in shapes:
        size = math.prod(shp)
        out.append(flat[off:off + size].reshape(shp))
        off += size
    return out


def _rows8(rows, width):
    flat = [r.reshape(width) for r in rows] + [jnp.zeros(((8 - len(rows)) * width,), F32)]
    return jnp.concatenate(flat).reshape(8, width)


def _reduce_scatter(tag, chip, ci, mine, other):
    got = _sibling_swap("rs_pair_" + tag, [other])[0]
    pair = _sum_pair("rs_pair_sum_" + tag, mine, got)
    landed = _own_block(chip, pair, _chip_scatter("rs_chips_" + tag, [pair])[0])
    half = _sum_chips("rs_chip_sum_" + tag, landed)
    return _join_halves(ci, half, _sibling_swap("rs_join_" + tag, [half])[0], 0)


def kernel(x, c, ctx, c_ctx, w_mod, b_mod, w_in, conv_w, conv_b, hg_lb, ml_gate_b, hg_norm_w, ml_norm_w, w_out, ln_g, ln_b, loss_target, m_c_ctx, m_w_mod, m_b_mod, m_w_in, m_conv_w, m_conv_b, m_hg_lb, m_ml_gate_b, m_hg_norm_w, m_ml_norm_w, m_w_out, m_ln_g, m_ln_b, v_c_ctx, v_w_mod, v_b_mod, v_w_in, v_conv_w, v_conv_b, v_hg_lb, v_ml_gate_b, v_hg_norm_w, v_ml_norm_w, v_w_out, v_ln_g, v_ln_b):
    t_rows, dm = x.shape[1], x.shape[2]
    c_rows = ctx.shape[1]
    w = hg_norm_w.shape[1]
    n_ml = ml_gate_b.shape[-1]
    n_hg = w // HG_DK
    di = 2 * w
    n_in = 10 * w + 4 * n_ml
    ns = w_in.shape[2]
    nm = w_mod.shape[2]
    n_pad = 10 * w + LANE
    r_rows = t_rows + c_rows
    n_lat, n_ctx = t_rows // CHUNK, c_rows // CHUNK
    assert ml_norm_w.shape[1] == w and di == dm and N_CHIPS * ns == n_in and N_CHIPS * nm == 3 * dm
    assert w_out.shape[1] * N_CHIPS == di and 4 * n_ml <= LANE and t_rows % GRID_W == 0

    xi, yi, ci = lax.axis_index("x"), lax.axis_index("y"), lax.axis_index("c")
    chip = 2 * xi + yi
    dev = 4 * xi + 2 * yi + ci

    tm = _largest_divisor(math.gcd(t_rows, c_rows), 256, 8)
    tm_mm = _largest_divisor(r_rows, 1088, SUBLANE_BF16)
    tn_mm = LANE * _largest_divisor(n_pad // LANE, 9)
    tn_mod = _largest_divisor(nm, 512, LANE)

    shard_shapes = [(dm,), (2, 2, w // N_CHIPS), (3, 3, di // N_CHIPS)]
    g1 = _all_gather8("gather_inputs", _pack([c, hg_lb, conv_w]))
    per_dev = [_unpack(g1[i], shard_shapes) for i in range(N_DEV)]
    c_all = jnp.stack([p[0] for p in per_dev])
    lb_full = jnp.concatenate([per_dev[2 * k][1] for k in range(N_CHIPS)], axis=-1)
    conv_w9 = jnp.concatenate([per_dev[2 * k][2] for k in range(N_CHIPS)], axis=-1).reshape(9, di)

    c16 = jnp.concatenate([c_all, c_ctx[None], jnp.zeros((16 - N_DEV - 1, dm), F32)])
    mod_part, a16 = _mod_fwd(c16, w_mod[0], tn_mod)
    g2 = _all_gather8("gather_mod", mod_part)
    mod_all = jnp.concatenate([g2[2 * k] for k in range(N_CHIPS)], axis=1) + b_mod
    mod_x = lax.dynamic_index_in_dim(mod_all, dev, 0, keepdims=False).reshape(3, dm)
    mod_c = mod_all[N_DEV].reshape(3, dm)
    prm = jnp.stack([_rows8(list(mod_x), dm), _rows8(list(mod_c), dm)])

    halves = [lax.dynamic_slice_in_dim(a[0].astype(BF16), ci * (a.shape[1] // 2), a.shape[1] // 2, 0)
              for a in (w_in, w_out)]
    fetched = [_own_block(chip, own, got)
               for own, got in zip(halves, _all_gather_chips("gather_weights", halves))]
    swapped = _sibling_swap("gather_weights_pair", fetched)
    gw_in, gw_out = [_join_halves(ci, a, b, 1) for a, b in zip(fetched, swapped)]
    w_full = jnp.concatenate([gw_in[k] for k in range(N_CHIPS)] + [jnp.zeros((dm, n_pad - n_in), BF16)], axis=1)
    w_out_full = gw_out.reshape(di, dm)

    xc = jnp.concatenate([x[0], ctx[0]])
    hc = _modulate_fwd(xc, prm, t_rows, tm)
    u = _mm_nn("in_proj", hc, w_full, tm_mm, tn_mm, F32)
    (o_f, o_b), hg_saved = _hg_scan_fwd(u, lb_full, w, n_lat, n_ctx)
    qk = _conv_fwd(u, conv_w9, conv_b, t_rows, c_rows, w, LANE)
    gate_b_row = jnp.pad(ml_gate_b.reshape(1, -1), ((0, 0), (0, LANE - 4 * n_ml)))
    (h_f, h_b), ml_saved = _ml_scan_fwd(qk, u, gate_b_row, w, n_ml, n_lat, n_ctx)
    y = _post_fwd(o_f, o_b, h_f, h_b, u, hg_norm_w, ml_norm_w, t_rows, w, n_hg, n_ml, tm)
    prm_out = _rows8([mod_x[2], ln_g, ln_b], dm)
    dz, dy, gx_direct, acc_out = _out_block(y, w_out_full, x[0], loss_target[0], prm_out, tm // 2)

    d_w_out = _mm_tn("d_w_out", y, dz, _largest_divisor(t_rows, 1024, SUBLANE_BF16),
                     _largest_divisor(dm, 1024, LANE))
    d_o, d_h, d_az, d_bo, d_bz, d_wa, d_wb = _post_bwd(o_f, o_b, h_f, h_b, u, hg_norm_w, ml_norm_w, dy,
                                                        t_rows, w, n_hg, n_ml, tm)
    (d_aq_f, d_aff, d_ai_f, d_lb_f), (d_aq_b, d_afb, d_ai_b, d_lb_b) = _hg_scan_bwd(
        u, lb_full, hg_saved, d_o, w, n_lat, n_ctx)
    (d_qk_f, d_v_f, d_g_f, d_gb_f), (d_qk_b, d_v_b, d_g_b, d_gb_b) = _ml_scan_bwd(
        qk, u, gate_b_row, ml_saved, d_h, w, n_ml, n_lat, n_ctx)
    d_bqk, d_cw, d_cb = _conv_bwd(u, (d_qk_f, d_qk_b), conv_w9, conv_b, t_rows, c_rows, w, LANE)
    du = _assemble_du([(d_aq_f, d_aq_b), d_aff, d_afb, (d_ai_f, d_ai_b), d_az, d_bqk, (d_v_f, d_v_b), d_bo, d_bz],
                      (d_g_f, d_g_b), n_pad, tm // 2)
    d_w_in = _mm_tn("d_w_in", hc, du, tm_mm, tn_mm)
    d_hc = _mm_nt("d_h", du, w_full, tm_mm, tn_mm)
    gx_all, acc_mod = _modulate_bwd(xc, d_hc, prm, gx_direct, t_rows, tm)
    grad_x = gx_all[:t_rows][None]

    half_in = dm // 2
    mine_in = lax.dynamic_slice_in_dim(d_w_in, ci * half_in, half_in, 0)
    other_in = lax.dynamic_slice_in_dim(d_w_in, (1 - ci) * half_in, half_in, 0)
    pieces_in = lambda a: jnp.stack([a[:, k * ns:(k + 1) * ns] for k in range(N_CHIPS)])
    g_w_in = _reduce_scatter("w_in", chip, ci, pieces_in(mine_in), pieces_in(other_in))
    d_w_out4 = d_w_out.reshape(N_CHIPS, 2, di // (2 * N_CHIPS), dm)
    g_w_out = _reduce_scatter("w_out", chip, ci, lax.dynamic_index_in_dim(d_w_out4, ci, 1, keepdims=False),
                              lax.dynamic_index_in_dim(d_w_out4, 1 - ci, 1, keepdims=False))

    zero_row = jnp.zeros((dm,), F32)
    d_gb = jnp.concatenate([d_gb_f[:, 0:n_ml], d_gb_b[:, n_ml:2 * n_ml], d_gb_f[:, 2 * n_ml:3 * n_ml],
                            d_gb_b[:, 3 * n_ml:4 * n_ml], jnp.zeros((1, dm - 4 * n_ml), F32)], axis=1)
    rows = [acc_mod[0, 0], acc_mod[0, 1], acc_out[OUT_ROW_GATE],
            acc_mod[1, 0], acc_mod[1, 1], zero_row]
    rows += list(d_cw) + [d_cb[0], d_lb_f.reshape(dm), d_lb_b.reshape(dm),
                          jnp.concatenate([d_wa[0], d_wb[0]]), acc_out[OUT_ROW_LN_G], acc_out[OUT_ROW_LN_B],
                          acc_out[OUT_ROW_LOSS], d_gb[0], zero_row]
    ROW_CW, ROW_CB, ROW_LB, ROW_NORM, ROW_LN_G, ROW_LN_B, ROW_LOSS, ROW_GB = 6, 15, 16, 18, 19, 20, 21, 22
    g3 = _all_gather8("gather_small_grads", jnp.concatenate([r.reshape(dm) for r in rows]).reshape(len(rows), dm))
    sums, totals = _sum_devices(g3, 3)
    loss = totals[ROW_LOSS, 0]
    dm16 = jnp.concatenate([g3[:, 0:3, :].reshape(N_DEV, 3 * dm), sums[3:6].reshape(1, 3 * dm),
                            jnp.zeros((16 - N_DEV - 1, 3 * dm), F32)])
    g_w_mod, dc16 = _mod_bwd(a16, lax.dynamic_slice_in_dim(dm16, chip * nm, nm, 1), w_mod[0], tn_mod)
    g4 = _all_gather8("gather_c_ctx", jnp.pad(dc16[N_DEV:N_DEV + 1], ((0, 7), (0, 0))))
    g_c_ctx = _c_ctx_grad(g4, jnp.broadcast_to(c_ctx[None], (8, dm)))[0]

    chip_cols = lambda a, width: lax.dynamic_slice_in_dim(a, chip * width, width, a.ndim - 1)
    grads = {
        "c_ctx": g_c_ctx,
        "w_mod": g_w_mod[None],
        "b_mod": sums[0:3].reshape(1, 3 * dm),
        "w_in": g_w_in[None],
        "conv_w": chip_cols(sums[ROW_CW:ROW_CW + 9].reshape(1, 3, 3, di), di // N_CHIPS),
        "conv_b": sums[ROW_CB][None],
        "hg_lb": chip_cols(sums[ROW_LB:ROW_LB + 2].reshape(2, 2, w), w // N_CHIPS),
        "ml_gate_b": sums[ROW_GB, 0:4 * n_ml].reshape(1, 4, n_ml),
        "hg_norm_w": sums[ROW_NORM, 0:w][None],
        "ml_norm_w": sums[ROW_NORM, w:2 * w][None],
        "w_out": g_w_out[None],
        "ln_g": sums[ROW_LN_G][None],
        "ln_b": sums[ROW_LN_B][None],
    }
    weights = dict(c_ctx=c_ctx, w_mod=w_mod, b_mod=b_mod, w_in=w_in, conv_w=conv_w, conv_b=conv_b, hg_lb=hg_lb,
                   ml_gate_b=ml_gate_b, hg_norm_w=hg_norm_w, ml_norm_w=ml_norm_w, w_out=w_out, ln_g=ln_g, ln_b=ln_b)
    mom1 = dict(c_ctx=m_c_ctx, w_mod=m_w_mod, b_mod=m_b_mod, w_in=m_w_in, conv_w=m_conv_w, conv_b=m_conv_b,
                hg_lb=m_hg_lb, ml_gate_b=m_ml_gate_b, hg_norm_w=m_hg_norm_w, ml_norm_w=m_ml_norm_w, w_out=m_w_out,
                ln_g=m_ln_g, ln_b=m_ln_b)
    mom2 = dict(c_ctx=v_c_ctx, w_mod=v_w_mod, b_mod=v_b_mod, w_in=v_w_in, conv_w=v_conv_w, conv_b=v_conv_b,
                hg_lb=v_hg_lb, ml_gate_b=v_ml_gate_b, hg_norm_w=v_hg_norm_w, ml_norm_w=v_ml_norm_w, w_out=v_w_out,
                ln_g=v_ln_g, ln_b=v_ln_b)
    names = list(weights)
    big = ("w_mod", "w_in", "w_out")
    small = [n for n in names if n not in big]

    delta, new_m, new_v = {}, {}, {}
    for n in big:
        as2d = lambda a: a.reshape(a.shape[-2], a.shape[-1])
        res = _adamw("adamw_" + n, as2d(weights[n]), as2d(grads[n]), as2d(mom1[n]), as2d(mom2[n]))
        delta[n], new_m[n], new_v[n] = (a.reshape(weights[n].shape) for a in res)
    small_shapes = [weights[n].shape for n in small]
    res = _adamw("adamw_small", *(_pack([src[n] for n in small]) for src in (weights, grads, mom1, mom2)))
    for out, packed in zip((delta, new_m, new_v), res):
        for n, a in zip(small, _unpack(packed, small_shapes)):
            out[n] = a

    return (loss, grad_x, *[grads[n].reshape(weights[n].shape) for n in names], *[delta[n] for n in names],
            *[new_m[n] for n in names], *[new_v[n] for n in names])
```

```python
import functools
import math

import jax
import jax.numpy as jnp
from jax import lax
from jax.experimental import pallas as pl
from jax.experimental.pallas import tpu as pltpu

F32 = jnp.float32
BF16 = jnp.bfloat16
HIGHEST = lax.Precision.HIGHEST
MESH = pl.DeviceIdType.MESH

HG_CHUNK = 64
ML_CHUNK = 256
GRID_W = 64
HG_DK = 128
LANE = 128
SUBLANE_BF16 = 16
ALPHA = 2.0 ** 0.25
LN_EPS = 1e-5
NORM_EPS = 1e-6
ADAM_LR = 0.001
ADAM_B1 = 0.9
ADAM_B2 = 0.999
ADAM_EPS = 1e-08
ADAM_WD = 0.01
ADAM_STEP = 10
VMEM_LIMIT = 56 * 1024 * 1024
N_CHIPS = 4
N_DEV = 8


def _params(sem=None):
    return pltpu.CompilerParams(dimension_semantics=sem, vmem_limit_bytes=VMEM_LIMIT)


def _largest_divisor(n, cap, multiple=1):
    best = None
    for d in range(multiple, min(n, cap) + 1, multiple):
        if n % d == 0:
            best = d
    assert best is not None, (n, cap, multiple)
    return best


def _sigmoid(x):
    return jax.nn.sigmoid(x)


def _silu(x):
    return x * jax.nn.sigmoid(x)


def _dot(a, b, dims, precision=None):
    return lax.dot_general(a, b, (dims, ((), ())), precision=precision, preferred_element_type=F32)


def _nn(a, b, precision=None):
    return _dot(a, b, ((1,), (0,)), precision)


def _nt(a, b, precision=None):
    return _dot(a, b, ((1,), (1,)), precision)


def _tn(a, b, precision=None):
    return _dot(a, b, ((0,), (0,)), precision)


def _visible(n, rev):
    r = lax.broadcasted_iota(jnp.int32, (n, n), 0)
    c = lax.broadcasted_iota(jnp.int32, (n, n), 1)
    return (r <= c) if rev else (r >= c)


def _hg_chunk(states, aq, af, ai, lb0, lb1, rev):
    n_heads = len(states)
    lb = _sigmoid(lb0 - lb1)
    f = lb + (1.0 - lb) * _sigmoid(af)
    g = jnp.log(f)
    k = 1.0 - f
    q = _silu(aq)
    chunk = aq.shape[0]
    vis = _visible(chunk, rev)
    b = _nn(vis.astype(F32), g, HIGHEST)
    last = 0 if rev else chunk - 1
    b_end = b[last:last + 1]
    b_mid = b[chunk // 2:chunk // 2 + 1]
    q_inter = q * jnp.exp(b)
    q_intra = q * jnp.exp(b - b_mid)
    k_intra = k * jnp.exp(b_mid - b)
    k_dec = k * jnp.exp(b_end - b)
    e_end = jnp.exp(b_end)
    new_states, outs = [], []
    for h in range(n_heads):
        sl = slice(h * HG_DK, (h + 1) * HG_DK)
        s_t = states[h]
        scores = jnp.where(vis, _nt(q_intra[:, sl], k_intra[:, sl]), 0.0)
        outs.append(_nt(q_inter[:, sl], s_t) + _nn(scores, ai[:, sl]))
        new_states.append(e_end[:, sl] * s_t + _tn(ai[:, sl], k_dec[:, sl]))
    return new_states, jnp.concatenate(outs, axis=1)


def _ml_chunk(state, q, k, v, g, gb, rev, d):
    cms, nvs, mbs = state
    n_heads = len(cms)
    dh = q.shape[1] // n_heads
    ga = g + gb
    log_f_all = jax.nn.log_sigmoid(ga)
    chunk = q.shape[0]
    vis = _visible(chunk, rev)
    b_all = _nn(vis.astype(F32), log_f_all, HIGHEST)
    last = 0 if rev else chunk - 1
    k = k * (dh ** -0.5)
    new_c, new_n, new_m, outs = [], [], [], []
    for h in range(n_heads):
        ci = d * n_heads + h
        cf = (2 + d) * n_heads + h
        sl = slice(h * dh, (h + 1) * dh)
        qh, kh, vh = q[:, sl], k[:, sl], v[:, sl]
        li = ga[:, ci:ci + 1]
        b = b_all[:, cf:cf + 1]
        m = mbs[h][:, 0:1]
        row = jnp.transpose(li - b)
        log_w = jnp.where(vis, b + row, -jnp.inf)
        m_inter = b + m
        m_t = jnp.maximum(m_inter, jnp.max(log_w, axis=-1, keepdims=True))
        w_inter = jnp.exp(m_inter - m_t)
        w_qk = jnp.exp(log_w - m_t) * _nt(qh, kh)
        num = w_inter * _nt(qh, cms[h]) + _nn(w_qk, vh)
        den = w_inter * jnp.sum(qh * nvs[h], axis=-1, keepdims=True) + jnp.sum(w_qk, axis=-1, keepdims=True)
        outs.append(num / jnp.maximum(jnp.abs(den), jnp.exp(-m_t)))
        m_new = m_t[last:last + 1]
        b_end = b[last:last + 1]
        w_s = jnp.exp(b_end - b + li - m_new)
        decay = jnp.exp(b_end + m - m_new)
        new_c.append(decay * cms[h] + _tn(w_s * vh, kh))
        new_n.append(decay * nvs[h] + jnp.sum(w_s * kh, axis=0, keepdims=True))
        new_m.append(jnp.broadcast_to(m_new, (1, LANE)))
    return (new_c, new_n, new_m), jnp.concatenate(outs, axis=1)


def _post_fn(o_f, o_b, az, h_f, h_b, bo, bz, wa, wb, n_hg, n_ml):
    o = o_f + o_b
    parts = []
    for h in range(n_hg):
        s = o[:, h * HG_DK:(h + 1) * HG_DK]
        parts.append(s * lax.rsqrt(jnp.mean(s * s, axis=-1, keepdims=True) + NORM_EPS))
    y_a = jnp.concatenate(parts, axis=1) * wa * _silu(az)
    hh = h_f + h_b
    dh = hh.shape[1] // n_ml
    parts = []
    for h in range(n_ml):
        s = hh[:, h * dh:(h + 1) * dh]
        mu = jnp.mean(s, axis=-1, keepdims=True)
        sc = s - mu
        parts.append(sc * lax.rsqrt(jnp.mean(sc * sc, axis=-1, keepdims=True) + NORM_EPS))
    y_b = jnp.concatenate(parts, axis=1) * wb * _sigmoid(bo) * _silu(bz)
    return jnp.concatenate([y_a, y_b], axis=1)


def _chip_of(dev):
    return 2 * dev[0] + dev[1]


def _index_of(dev):
    return 4 * dev[0] + 2 * dev[1] + dev[2]


def _exchange(name, srcs, out_shapes, transfers, local_copies=()):
    n_in, n_out, n_t, n_l = len(srcs), len(out_shapes), len(transfers), len(local_copies)

    def body(*refs):
        ins, outs = refs[:n_in], refs[n_in:n_in + n_out]
        send_sems, recv_sems, local_sems = refs[n_in + n_out:]
        me = (lax.axis_index("x"), lax.axis_index("y"), lax.axis_index("c"))

        def pick(ref, fn, *who):
            return ref if fn is None else ref.at[fn(*who)]

        sends, recvs, locs = [], [], []
        for t, (mask, si, sfn, di, dfn) in enumerate(transfers):
            peer = tuple(1 - p if flip else p for p, flip in zip(me, mask))
            sends.append(pltpu.make_async_remote_copy(
                src_ref=pick(ins[si], sfn, me, peer), dst_ref=pick(outs[di], dfn, me, peer),
                send_sem=send_sems.at[t], recv_sem=recv_sems.at[t], device_id=peer, device_id_type=MESH))
            landing = pick(outs[di], dfn, peer, me)
            recvs.append(pltpu.make_async_remote_copy(
                src_ref=landing, dst_ref=landing,
                send_sem=send_sems.at[t], recv_sem=recv_sems.at[t], device_id=peer, device_id_type=MESH))
        for l, (si, sfn, di, dfn) in enumerate(local_copies):
            locs.append(pltpu.make_async_copy(pick(ins[si], sfn, me), pick(outs[di], dfn, me), local_sems.at[l]))
        for cp in locs + sends:
            cp.start()
        for cp in recvs:
            cp.wait_recv()
        for cp in sends:
            cp.wait_send()
        for cp in locs:
            cp.wait()

    hbm = pl.BlockSpec(memory_space=pltpu.HBM)
    return pl.pallas_call(
        body, name=name, out_shape=tuple(out_shapes),
        in_specs=[hbm] * n_in, out_specs=tuple([hbm] * n_out),
        scratch_shapes=[pltpu.SemaphoreType.DMA((n_t,)), pltpu.SemaphoreType.DMA((n_t,)),
                        pltpu.SemaphoreType.DMA((max(n_l, 1),))],
    )(*srcs)


ALL_MASKS = [(mx, my, mc) for mx in (0, 1) for my in (0, 1) for mc in (0, 1)][1:]
CHIP_MASKS = [(1, 0, 0), (0, 1, 0), (1, 1, 0)]
SIBLING_MASK = (0, 0, 1)


def _all_gather8(name, v):
    out = jax.ShapeDtypeStruct((N_DEV,) + v.shape, v.dtype)
    slot = lambda sender, receiver: _index_of(sender)
    transfers = [(mask, 0, None, 0, slot) for mask in ALL_MASKS]
    return _exchange(name, [v], [out], transfers, [(0, None, 0, lambda me: _index_of(me))])[0]


def _all_gather_chips(name, arrays):
    outs = [jax.ShapeDtypeStruct((N_CHIPS,) + a.shape, a.dtype) for a in arrays]
    slot = lambda sender, receiver: _chip_of(sender)
    transfers = [(mask, i, None, i, slot) for i in range(len(arrays)) for mask in CHIP_MASKS]
    return _exchange(name, arrays, outs, transfers)


def _sibling_swap(name, arrays):
    outs = [jax.ShapeDtypeStruct(a.shape, a.dtype) for a in arrays]
    return _exchange(name, arrays, outs, [(SIBLING_MASK, i, None, i, None) for i in range(len(arrays))])


def _chip_scatter(name, arrays):
    outs = [jax.ShapeDtypeStruct(a.shape, a.dtype) for a in arrays]
    transfers = [(mask, i, lambda s, r: _chip_of(r), i, lambda s, r: _chip_of(s))
                 for i in range(len(arrays)) for mask in CHIP_MASKS]
    return _exchange(name, arrays, outs, transfers)


def _own_block(chip, own, blocks):
    sel = (lax.broadcasted_iota(jnp.int32, (N_CHIPS,) + (1,) * (blocks.ndim - 1), 0) == chip)
    return jnp.where(sel, own if own.ndim == blocks.ndim else own[None], blocks)


def _join_halves(ci, mine, other, axis):
    return jnp.where(ci == 0, jnp.concatenate([mine, other], axis=axis), jnp.concatenate([other, mine], axis=axis))


def _mm_nn(name, a, b, tm, tn, out_dtype):
    m, k = a.shape
    n = b.shape[1]

    def body(a_ref, b_ref, o_ref):
        o_ref[...] = _nn(a_ref[...], b_ref[...]).astype(out_dtype)

    return pl.pallas_call(
        body, name=name, grid=(n // tn, m // tm),
        in_specs=[pl.BlockSpec((tm, k), lambda j, i: (i, 0)), pl.BlockSpec((k, tn), lambda j, i: (0, j))],
        out_specs=pl.BlockSpec((tm, tn), lambda j, i: (i, j)),
        out_shape=jax.ShapeDtypeStruct((m, n), out_dtype),
        compiler_params=_params(("parallel", "parallel")),
    )(a, b)


def _mm_nt(name, a, b, tm, tk):
    m, kc = a.shape
    n = b.shape[0]

    def body(a_ref, b_ref, o_ref):
        @pl.when(pl.program_id(1) == 0)
        def _():
            o_ref[...] = jnp.zeros_like(o_ref)
        o_ref[...] += _nt(a_ref[...], b_ref[...])

    return pl.pallas_call(
        body, name=name, grid=(m // tm, kc // tk),
        in_specs=[pl.BlockSpec((tm, tk), lambda i, kk: (i, kk)), pl.BlockSpec((n, tk), lambda i, kk: (0, kk))],
        out_specs=pl.BlockSpec((tm, n), lambda i, kk: (i, 0)),
        out_shape=jax.ShapeDtypeStruct((m, n), F32),
        compiler_params=_params(("parallel", "arbitrary")),
    )(a, b)


def _mm_tn(name, a, b, tk, tn):
    kr, m = a.shape
    n = b.shape[1]

    def body(a_ref, b_ref, o_ref):
        @pl.when(pl.program_id(1) == 0)
        def _():
            o_ref[...] = jnp.zeros_like(o_ref)
        o_ref[...] += _tn(a_ref[...], b_ref[...])

    return pl.pallas_call(
        body, name=name, grid=(n // tn, kr // tk),
        in_specs=[pl.BlockSpec((tk, m), lambda j, kk: (kk, 0)), pl.BlockSpec((tk, tn), lambda j, kk: (kk, j))],
        out_specs=pl.BlockSpec((m, tn), lambda j, kk: (0, j)),
        out_shape=jax.ShapeDtypeStruct((m, n), F32),
        compiler_params=_params(("parallel", "arbitrary")),
    )(a, b)


def _modulate_fwd(xc, prm, t_rows, tm):
    r, dm = xc.shape
    first_ctx = t_rows // tm

    def body(x_ref, p_ref, h_ref):
        x = x_ref[...]
        mu = jnp.mean(x, axis=-1, keepdims=True)
        xm = x - mu
        n = xm * lax.rsqrt(jnp.mean(xm * xm, axis=-1, keepdims=True) + LN_EPS)
        h_ref[...] = (n * (1.0 + p_ref[0, 1:2, :]) + p_ref[0, 0:1, :]).astype(BF16)

    return pl.pallas_call(
        body, name="modulate_fwd", grid=(r // tm,),
        in_specs=[pl.BlockSpec((tm, dm), lambda i: (i, 0)),
                  pl.BlockSpec((1, 8, dm), lambda i: ((i >= first_ctx).astype(jnp.int32), 0, 0))],
        out_specs=pl.BlockSpec((tm, dm), lambda i: (i, 0)),
        out_shape=jax.ShapeDtypeStruct((r, dm), BF16),
        compiler_params=_params(("parallel",)),
    )(xc, prm)


def _modulate_bwd(xc, dh, prm, gx_direct, t_rows, tm):
    r, dm = xc.shape
    first_ctx = t_rows // tm
    cls = lambda i: (i >= first_ctx).astype(jnp.int32)

    def body(x_ref, dh_ref, p_ref, gd_ref, gx_ref, acc_ref):
        i = pl.program_id(0)

        @pl.when((i == 0) | (i == first_ctx))
        def _():
            acc_ref[...] = jnp.zeros_like(acc_ref)

        x = x_ref[...]
        dh_v = dh_ref[...]
        mu = jnp.mean(x, axis=-1, keepdims=True)
        xm = x - mu
        rstd = lax.rsqrt(jnp.mean(xm * xm, axis=-1, keepdims=True) + LN_EPS)
        n = xm * rstd
        acc_ref[0, 0:1, :] += jnp.sum(dh_v, axis=0, keepdims=True)
        acc_ref[0, 1:2, :] += jnp.sum(dh_v * n, axis=0, keepdims=True)
        dn = dh_v * (1.0 + p_ref[0, 1:2, :])
        dx = rstd * (dn - jnp.mean(dn, axis=-1, keepdims=True) - n * jnp.mean(dn * n, axis=-1, keepdims=True))
        gx_ref[...] = dx + gd_ref[...]

    return pl.pallas_call(
        body, name="modulate_bwd", grid=(r // tm,),
        in_specs=[pl.BlockSpec((tm, dm), lambda i: (i, 0)), pl.BlockSpec((tm, dm), lambda i: (i, 0)),
                  pl.BlockSpec((1, 8, dm), lambda i: (cls(i), 0, 0)),
                  pl.BlockSpec((tm, dm), lambda i: (jnp.minimum(i, first_ctx - 1), 0))],
        out_specs=(pl.BlockSpec((tm, dm), lambda i: (i, 0)), pl.BlockSpec((1, 8, dm), lambda i: (cls(i), 0, 0))),
        out_shape=(jax.ShapeDtypeStruct((r, dm), F32), jax.ShapeDtypeStruct((2, 8, dm), F32)),
        compiler_params=_params(("arbitrary",)),
    )(xc, dh, prm, gx_direct)


def _conv_parts(t_rows, c_rows):
    return ((0, t_rows, t_rows // GRID_W, GRID_W), (t_rows, c_rows, 1, c_rows))


def _tap_valid(n, rows_g, width_g, a, b, lanes):
    t = lax.broadcasted_iota(jnp.int32, (n, lanes), 0)
    shift = width_g.bit_length() - 1
    assert 1 << shift == width_g
    rr = jnp.right_shift(t, shift) + (a - 1)
    cc = jnp.bitwise_and(t, width_g - 1) + (b - 1)
    return (rr >= 0) & (rr < rows_g) & (cc >= 0) & (cc < width_g)


def _rows_from(x, off):
    s = (-off) % x.shape[0]
    return x if s == 0 else pltpu.roll(x, s, 0)


def _conv_pre(xs, w_ref, b_ref, rows_g, width_g):
    acc = jnp.broadcast_to(b_ref[...], xs.shape)
    for a in range(3):
        if rows_g == 1 and a != 1:
            continue
        for b in range(3):
            off = (a - 1) * width_g + (b - 1)
            valid = _tap_valid(xs.shape[0], rows_g, width_g, a, b, xs.shape[1])
            acc = acc + jnp.where(valid, _rows_from(xs, off), 0.0) * w_ref[a * 3 + b:a * 3 + b + 1, :]
    return acc


def _conv_fwd(u, conv_w9, conv_b, t_rows, c_rows, w, ct):
    r = u.shape[0]
    base = 5 * w // ct

    def body(x_ref, w_ref, b_ref, o_ref):
        for r0, n, rows_g, width_g in _conv_parts(t_rows, c_rows):
            o_ref[r0:r0 + n, :] = _silu(_conv_pre(x_ref[r0:r0 + n, :], w_ref, b_ref, rows_g, width_g))

    return pl.pallas_call(
        body, name="conv_fwd", grid=(2 * w // ct,),
        in_specs=[pl.BlockSpec((r, ct), lambda i: (0, base + i)), pl.BlockSpec((9, ct), lambda i: (0, i)),
                  pl.BlockSpec((1, ct), lambda i: (0, i))],
        out_specs=pl.BlockSpec((r, ct), lambda i: (0, i)),
        out_shape=jax.ShapeDtypeStruct((r, 2 * w), F32),
        compiler_params=_params(("parallel",)),
    )(u, conv_w9, conv_b)


def _conv_bwd(u, dqk_pair, conv_w9, conv_b, t_rows, c_rows, w, ct):
    r = u.shape[0]
    base = 5 * w // ct

    def body(x_ref, d1_ref, d2_ref, w_ref, b_ref, dx_ref, dw_ref, db_ref):
        dw = [jnp.zeros((1, ct), F32) for _ in range(9)]
        db = jnp.zeros((1, ct), F32)
        for r0, n, rows_g, width_g in _conv_parts(t_rows, c_rows):
            xs = x_ref[r0:r0 + n, :]
            pre = _conv_pre(xs, w_ref, b_ref, rows_g, width_g)
            sg = _sigmoid(pre)
            dpre = (d1_ref[r0:r0 + n, :] + d2_ref[r0:r0 + n, :]) * (sg * (1.0 + pre * (1.0 - sg)))
            db = db + jnp.sum(dpre, axis=0, keepdims=True)
            dx = jnp.zeros_like(xs)
            for a in range(3):
                if rows_g == 1 and a != 1:
                    continue
                for b in range(3):
                    off = (a - 1) * width_g + (b - 1)
                    valid = _tap_valid(n, rows_g, width_g, a, b, ct)
                    dw[a * 3 + b] = dw[a * 3 + b] + jnp.sum(
                        jnp.where(valid, _rows_from(xs, off), 0.0) * dpre, axis=0, keepdims=True)
                    dx = dx + _rows_from(jnp.where(valid, dpre, 0.0) * w_ref[a * 3 + b:a * 3 + b + 1, :], -off)
            dx_ref[r0:r0 + n, :] = dx
        for tap in range(9):
            dw_ref[tap:tap + 1, :] = dw[tap]
        db_ref[...] = db

    return pl.pallas_call(
        body, name="conv_bwd", grid=(2 * w // ct,),
        in_specs=[pl.BlockSpec((r, ct), lambda i: (0, base + i)), pl.BlockSpec((r, ct), lambda i: (0, i)),
                  pl.BlockSpec((r, ct), lambda i: (0, i)),
                  pl.BlockSpec((9, ct), lambda i: (0, i)), pl.BlockSpec((1, ct), lambda i: (0, i))],
        out_specs=(pl.BlockSpec((r, ct), lambda i: (0, i)), pl.BlockSpec((9, ct), lambda i: (0, i)),
                   pl.BlockSpec((1, ct), lambda i: (0, i))),
        out_shape=(jax.ShapeDtypeStruct((r, 2 * w), F32), jax.ShapeDtypeStruct((9, 2 * w), F32),
                   jax.ShapeDtypeStruct((1, 2 * w), F32)),
        compiler_params=_params(("parallel",)),
    )(u, dqk_pair[0], dqk_pair[1], conv_w9, conv_b)


def _assemble_du(groups, gates, n_pad, tm):
    flat, layout = [], []
    for entry in list(groups) + [gates]:
        parts = entry if isinstance(entry, (tuple, list)) else (entry,)
        layout.append((len(flat), len(parts), parts[0].shape[1]))
        flat += list(parts)
    r = flat[0].shape[0]

    def body(*refs):
        o_ref = refs[-1]
        col = 0
        for first, count, width in layout:
            val = refs[first][...]
            for extra in range(1, count):
                val = val + refs[first + extra][...]
            o_ref[:, col:col + width] = val.astype(BF16)
            col += width
        assert col == n_pad

    return pl.pallas_call(
        body, name="assemble_du", grid=(r // tm,),
        in_specs=[pl.BlockSpec((tm, a.shape[1]), lambda i: (i, 0)) for a in flat],
        out_specs=pl.BlockSpec((tm, n_pad), lambda i: (i, 0)),
        out_shape=jax.ShapeDtypeStruct((r, n_pad), BF16),
        compiler_params=_params(("parallel",)),
    )(*flat)


def _scan_order(n_lat, n_ctx, rev):
    n = n_lat + n_ctx
    if rev:
        return lambda j: n - 1 - j
    return lambda j: (j + n_lat) % n


DIRS = (False, True)


def _hg_scan_fwd(u, lb_full, w, n_lat, n_ctx, chunk):
    r = u.shape[0]
    n_heads = w // HG_DK
    n_chunks = n_lat + n_ctx
    nat = [_scan_order(n_lat, n_ctx, rev) for rev in DIRS]

    def body(*refs):
        ins, outs, scratch = refs[:8], refs[8:12], refs[12:]

        @pl.when(pl.program_id(0) == 0)
        def _():
            for s_ref in scratch:
                s_ref[...] = jnp.zeros_like(s_ref)

        results = []
        for d, rev in enumerate(DIRS):
            aq, af, ai, lb_ref = ins[4 * d:4 * d + 4]
            state = [scratch[d][h] for h in range(n_heads)]
            results.append((state, _hg_chunk(state, aq[...], af[...], ai[...],
                                             lb_ref[0, 0:1, :], lb_ref[0, 1:2, :], rev)))
        for d, (state, (new, o)) in enumerate(results):
            o_ref, save_ref = outs[2 * d:2 * d + 2]
            o_ref[...] = o
            for h in range(n_heads):
                save_ref[0, h] = state[h]
                scratch[d][h] = new[h]

    in_specs, out_specs, out_shape = [], [], []
    for d in range(2):
        in_specs += [pl.BlockSpec((chunk,w), lambda j, d=d: (nat[d](j), 0)),
                     pl.BlockSpec((chunk,w), lambda j, d=d: (nat[d](j), 1 + d)),
                     pl.BlockSpec((chunk,w), lambda j, d=d: (nat[d](j), 3)),
                     pl.BlockSpec((1, 2, w), lambda j, d=d: (d, 0, 0))]
        out_specs += [pl.BlockSpec((chunk,w), lambda j, d=d: (nat[d](j), 0)),
                      pl.BlockSpec((1, n_heads, HG_DK, HG_DK), lambda j: (j, 0, 0, 0))]
        out_shape += [jax.ShapeDtypeStruct((r, w), F32),
                      jax.ShapeDtypeStruct((n_chunks, n_heads, HG_DK, HG_DK), F32)]
    o_f, s_f, o_b, s_b = pl.pallas_call(
        body, name="hg_scan_fwd", grid=(n_chunks,), in_specs=in_specs, out_specs=tuple(out_specs),
        out_shape=tuple(out_shape), scratch_shapes=[pltpu.VMEM((n_heads, HG_DK, HG_DK), F32)] * 2,
        compiler_params=_params(("arbitrary",)),
    )(u, u, u, lb_full, u, u, u, lb_full)
    return (o_f, o_b), (s_f, s_b)


def _hg_scan_bwd(u, lb_full, saved, d_o, w, n_lat, n_ctx, chunk):
    r = u.shape[0]
    n_heads = w // HG_DK
    n_chunks = n_lat + n_ctx
    step = lambda jj: n_chunks - 1 - jj
    nat = [(lambda jj, o=_scan_order(n_lat, n_ctx, rev): o(step(jj))) for rev in DIRS]

    def body(*refs):
        ins, outs, scratch = refs[:12], refs[12:20], refs[20:]
        jj = pl.program_id(0)

        @pl.when(jj == 0)
        def _():
            for d in range(2):
                scratch[d][...] = jnp.zeros_like(scratch[d])
                outs[4 * d + 3][...] = jnp.zeros_like(outs[4 * d + 3])

        results = []
        for d, rev in enumerate(DIRS):
            aq, af, ai, lb_ref, save_ref, do_ref = ins[6 * d:6 * d + 6]
            f = lambda st, a, b, c, l0, l1, rev=rev: _hg_chunk(st, a, b, c, l0, l1, rev)
            _, vjp = jax.vjp(f, [save_ref[0, h] for h in range(n_heads)], aq[...], af[...], ai[...],
                             lb_ref[0, 0:1, :], lb_ref[0, 1:2, :])
            d_out = do_ref[...] * (nat[d](jj) < n_lat).astype(F32)
            results.append(vjp(([scratch[d][h] for h in range(n_heads)], d_out)))
        for d, (dst, daq, daf, dai, dl0, dl1) in enumerate(results):
            daq_ref, daf_ref, dai_ref, dlb_ref = outs[4 * d:4 * d + 4]
            for h in range(n_heads):
                scratch[d][h] = dst[h]
            daq_ref[...] = daq
            daf_ref[...] = daf
            dai_ref[...] = dai
            dlb_ref[0:1, :] += dl0
            dlb_ref[1:2, :] += dl1

    in_specs, out_specs, out_shape, operands = [], [], [], []
    for d in range(2):
        row = lambda jj, d=d: (nat[d](jj), 0)
        in_specs += [pl.BlockSpec((chunk,w), row),
                     pl.BlockSpec((chunk,w), lambda jj, d=d: (nat[d](jj), 1 + d)),
                     pl.BlockSpec((chunk,w), lambda jj, d=d: (nat[d](jj), 3)),
                     pl.BlockSpec((1, 2, w), lambda jj, d=d: (d, 0, 0)),
                     pl.BlockSpec((1, n_heads, HG_DK, HG_DK), lambda jj: (step(jj), 0, 0, 0)),
                     pl.BlockSpec((chunk,w), lambda jj, d=d: (jnp.minimum(nat[d](jj), n_lat - 1), 0))]
        operands += [u, u, u, lb_full, saved[d], d_o]
        out_specs += [pl.BlockSpec((chunk,w), row)] * 3 + [pl.BlockSpec((2, w), lambda jj: (0, 0))]
        out_shape += [jax.ShapeDtypeStruct((r, w), F32)] * 3 + [jax.ShapeDtypeStruct((2, w), F32)]
    res = pl.pallas_call(
        body, name="hg_scan_bwd", grid=(n_chunks,), in_specs=in_specs, out_specs=tuple(out_specs),
        out_shape=tuple(out_shape), scratch_shapes=[pltpu.VMEM((n_heads, HG_DK, HG_DK), F32)] * 2,
        compiler_params=_params(("arbitrary",)),
    )(*operands)
    return res[0:4], res[4:8]


def _ml_state_shapes(n_chunks, n_heads, dh):
    return (jax.ShapeDtypeStruct((n_chunks, n_heads, dh, dh), F32),
            jax.ShapeDtypeStruct((n_chunks, n_heads, 1, dh), F32),
            jax.ShapeDtypeStruct((n_chunks, n_heads, 1, LANE), F32))


def _ml_state_specs(n_heads, dh, index):
    return (pl.BlockSpec((1, n_heads, dh, dh), lambda j: (index(j), 0, 0, 0)),
            pl.BlockSpec((1, n_heads, 1, dh), lambda j: (index(j), 0, 0, 0)),
            pl.BlockSpec((1, n_heads, 1, LANE), lambda j: (index(j), 0, 0, 0)))


def _ml_state_scratch(n_heads, dh):
    return [pltpu.VMEM((n_heads, dh, dh), F32), pltpu.VMEM((n_heads, 1, dh), F32), pltpu.VMEM((n_heads, 1, LANE), F32)]


def _ml_scan_fwd(qk, u, gate_b, w, n_heads, n_lat, n_ctx, chunk):
    r = u.shape[0]
    dh = w // n_heads
    n_chunks = n_lat + n_ctx
    nat = [_scan_order(n_lat, n_ctx, rev) for rev in DIRS]

    def body(*refs):
        ins, outs, scratch = refs[:10], refs[10:18], refs[18:]

        @pl.when(pl.program_id(0) == 0)
        def _():
            for s_ref in scratch:
                s_ref[...] = jnp.zeros_like(s_ref)

        results = []
        for d, rev in enumerate(DIRS):
            q, k, v, g, gb = ins[5 * d:5 * d + 5]
            state = tuple([ref[h] for h in range(n_heads)] for ref in scratch[3 * d:3 * d + 3])
            results.append((state, _ml_chunk(state, q[...], k[...], v[...], g[...], gb[...], rev, d)))
        for d, (state, (new, o)) in enumerate(results):
            outs[4 * d][...] = o
            for part in range(3):
                for h in range(n_heads):
                    outs[4 * d + 1 + part][0, h] = state[part][h]
                    scratch[3 * d + part][h] = new[part][h]

    in_specs, out_specs, out_shape = [], [], []
    for d in range(2):
        in_specs += [pl.BlockSpec((chunk,w), lambda j, d=d: (nat[d](j), 0)),
                     pl.BlockSpec((chunk,w), lambda j, d=d: (nat[d](j), 1)),
                     pl.BlockSpec((chunk,w), lambda j, d=d: (nat[d](j), 7)),
                     pl.BlockSpec((chunk,LANE), lambda j, d=d: (nat[d](j), 10 * w // LANE)),
                     pl.BlockSpec((1, LANE), lambda j: (0, 0))]
        out_specs += [pl.BlockSpec((chunk,w), lambda j, d=d: (nat[d](j), 0))]
        out_specs += list(_ml_state_specs(n_heads, dh, lambda j: j))
        out_shape += [jax.ShapeDtypeStruct((r, w), F32)] + list(_ml_state_shapes(n_chunks, n_heads, dh))
    res = pl.pallas_call(
        body, name="ml_scan_fwd", grid=(n_chunks,), in_specs=in_specs, out_specs=tuple(out_specs),
        out_shape=tuple(out_shape), scratch_shapes=_ml_state_scratch(n_heads, dh) * 2,
        compiler_params=_params(("arbitrary",)),
    )(qk, qk, u, u, gate_b, qk, qk, u, u, gate_b)
    return (res[0], res[4]), (res[1:4], res[5:8])


def _ml_scan_bwd(qk, u, gate_b, saved, d_h, w, n_heads, n_lat, n_ctx, chunk):
    r = u.shape[0]
    dh = w // n_heads
    n_chunks = n_lat + n_ctx
    step = lambda jj: n_chunks - 1 - jj
    nat = [(lambda jj, o=_scan_order(n_lat, n_ctx, rev): o(step(jj))) for rev in DIRS]

    def body(*refs):
        ins, outs, scratch = refs[:18], refs[18:26], refs[26:]
        jj = pl.program_id(0)

        @pl.when(jj == 0)
        def _():
            for s_ref in scratch:
                s_ref[...] = jnp.zeros_like(s_ref)
            for d in range(2):
                outs[4 * d + 3][...] = jnp.zeros_like(outs[4 * d + 3])

        results = []
        for d, rev in enumerate(DIRS):
            q, k, v, g, gb, sc, sn, sm, dh_ref = ins[9 * d:9 * d + 9]
            state = tuple([ref[0, h] for h in range(n_heads)] for ref in (sc, sn, sm))
            f = lambda st, a, b, c, gg, bb, rev=rev, d=d: _ml_chunk(st, a, b, c, gg, bb, rev, d)
            _, vjp = jax.vjp(f, state, q[...], k[...], v[...], g[...], gb[...])
            d_state = tuple([ref[h] for h in range(n_heads)] for ref in scratch[3 * d:3 * d + 3])
            d_out = dh_ref[...] * (nat[d](jj) < n_lat).astype(F32)
            results.append(vjp((d_state, d_out)))
        for d, (d_state, dq, dk, dv, dg, dgb) in enumerate(results):
            dqk_ref, dv_ref, dg_ref, dgb_ref = outs[4 * d:4 * d + 4]
            for part in range(3):
                for h in range(n_heads):
                    scratch[3 * d + part][h] = d_state[part][h]
            dqk_ref[:, 0:w] = dq
            dqk_ref[:, w:2 * w] = dk
            dv_ref[...] = dv
            dg_ref[...] = dg
            dgb_ref[...] += dgb

    in_specs, out_specs, out_shape, operands = [], [], [], []
    for d in range(2):
        row = lambda jj, d=d: (nat[d](jj), 0)
        in_specs += [pl.BlockSpec((chunk,w), row), pl.BlockSpec((chunk,w), lambda jj, d=d: (nat[d](jj), 1)),
                     pl.BlockSpec((chunk,w), lambda jj, d=d: (nat[d](jj), 7)),
                     pl.BlockSpec((chunk,LANE), lambda jj, d=d: (nat[d](jj), 10 * w // LANE)),
                     pl.BlockSpec((1, LANE), lambda jj: (0, 0))]
        in_specs += list(_ml_state_specs(n_heads, dh, step))
        in_specs += [pl.BlockSpec((chunk,w), lambda jj, d=d: (jnp.minimum(nat[d](jj), n_lat - 1), 0))]
        operands += [qk, qk, u, u, gate_b, *saved[d], d_h]
        out_specs += [pl.BlockSpec((chunk,2 * w), row), pl.BlockSpec((chunk,w), row),
                      pl.BlockSpec((chunk,LANE), row), pl.BlockSpec((1, LANE), lambda jj: (0, 0))]
        out_shape += [jax.ShapeDtypeStruct((r, 2 * w), F32), jax.ShapeDtypeStruct((r, w), F32),
                      jax.ShapeDtypeStruct((r, LANE), F32), jax.ShapeDtypeStruct((1, LANE), F32)]
    res = pl.pallas_call(
        body, name="ml_scan_bwd", grid=(n_chunks,), in_specs=in_specs, out_specs=tuple(out_specs),
        out_shape=tuple(out_shape), scratch_shapes=_ml_state_scratch(n_heads, dh) * 2,
        compiler_params=_params(("arbitrary",)),
    )(*operands)
    return res[0:4], res[4:8]


def _post_specs(w, tm, lat_tiles, cols):
    return [pl.BlockSpec((tm, w), (lambda i, cb=cb: (jnp.minimum(i, lat_tiles - 1), cb))) for cb in cols]


def _post_fwd(o_f, o_b, h_f, h_b, u, wa, wb, t_rows, w, n_hg, n_ml, tm):
    lat_tiles = t_rows // tm

    def body(of, ob, hf, hb, az, bo, bz, wa_ref, wb_ref, y_ref):
        y_ref[...] = _post_fn(of[...], ob[...], az[...], hf[...], hb[...], bo[...], bz[...],
                              wa_ref[...], wb_ref[...], n_hg, n_ml).astype(BF16)

    rows = pl.BlockSpec((tm, w), lambda i: (i, 0))
    vec = pl.BlockSpec((1, w), lambda i: (0, 0))
    return pl.pallas_call(
        body, name="post_fwd", grid=(lat_tiles,),
        in_specs=[rows] * 4 + _post_specs(w, tm, lat_tiles, (4, 8, 9)) + [vec, vec],
        out_specs=pl.BlockSpec((tm, 2 * w), lambda i: (i, 0)),
        out_shape=jax.ShapeDtypeStruct((t_rows, 2 * w), BF16),
        compiler_params=_params(("parallel",)),
    )(o_f, o_b, h_f, h_b, u, u, u, wa, wb)


def _post_bwd(o_f, o_b, h_f, h_b, u, wa, wb, dy, t_rows, w, n_hg, n_ml, tm):
    r = u.shape[0]
    lat_tiles = t_rows // tm
    lat = lambda i: (jnp.minimum(i, lat_tiles - 1), 0)

    def body(of, ob, hf, hb, az, bo, bz, wa_ref, wb_ref, dy_ref, do_ref, dh_ref, daz_ref, dbo_ref, dbz_ref,
             dwa_ref, dwb_ref):
        i = pl.program_id(0)

        @pl.when(i == 0)
        def _():
            dwa_ref[...] = jnp.zeros_like(dwa_ref)
            dwb_ref[...] = jnp.zeros_like(dwb_ref)

        @pl.when(i < lat_tiles)
        def _():
            f = functools.partial(_post_fn, n_hg=n_hg, n_ml=n_ml)
            _, vjp = jax.vjp(f, of[...], ob[...], az[...], hf[...], hb[...], bo[...], bz[...], wa_ref[...], wb_ref[...])
            d_of, _, d_az, d_hf, _, d_bo, d_bz, d_wa, d_wb = vjp(dy_ref[...])
            do_ref[...] = d_of
            dh_ref[...] = d_hf
            daz_ref[...] = d_az
            dbo_ref[...] = d_bo
            dbz_ref[...] = d_bz
            dwa_ref[...] += d_wa
            dwb_ref[...] += d_wb

        @pl.when(i >= lat_tiles)
        def _():
            daz_ref[...] = jnp.zeros_like(daz_ref)
            dbo_ref[...] = jnp.zeros_like(dbo_ref)
            dbz_ref[...] = jnp.zeros_like(dbz_ref)

    lat_rows = pl.BlockSpec((tm, w), lat)
    all_rows = pl.BlockSpec((tm, w), lambda i: (i, 0))
    vec = pl.BlockSpec((1, w), lambda i: (0, 0))
    sd_t = jax.ShapeDtypeStruct((t_rows, w), F32)
    sd_r = jax.ShapeDtypeStruct((r, w), F32)
    sd_v = jax.ShapeDtypeStruct((1, w), F32)
    return pl.pallas_call(
        body, name="post_bwd", grid=(r // tm,),
        in_specs=[lat_rows] * 4 + _post_specs(w, tm, lat_tiles, (4, 8, 9)) + [vec, vec]
        + [pl.BlockSpec((tm, 2 * w), lat)],
        out_specs=(lat_rows, lat_rows, all_rows, all_rows, all_rows, vec, vec),
        out_shape=(sd_t, sd_t, sd_r, sd_r, sd_r, sd_v, sd_v),
        compiler_params=_params(("arbitrary",)),
    )(o_f, o_b, h_f, h_b, u, u, u, wa, wb, dy)


OUT_ROW_GATE, OUT_ROW_LN_G, OUT_ROW_LN_B, OUT_ROW_LOSS = 0, 1, 2, 3


def _out_block(y, w_out, x, target, prm, tm):
    t_rows, dm = x.shape
    di = y.shape[1]

    def body(y_ref, w_ref, x_ref, t_ref, p_ref, dz_ref, dy_ref, gx_ref, acc_ref):
        @pl.when(pl.program_id(0) == 0)
        def _():
            acc_ref[...] = jnp.zeros_like(acc_ref)

        gate, ln_g, ln_b = p_ref[0:1, :], p_ref[1:2, :], p_ref[2:3, :]
        z = _nn(y_ref[...], w_ref[...])
        res = ALPHA * x_ref[...] + gate * z
        mu = jnp.mean(res, axis=-1, keepdims=True)
        rc = res - mu
        rstd = lax.rsqrt(jnp.mean(rc * rc, axis=-1, keepdims=True) + LN_EPS)
        rn = rc * rstd
        err = rn * ln_g + ln_b - t_ref[...]
        d_out = err * (1.0 / dm)
        d_rn = d_out * ln_g
        d_res = rstd * (d_rn - jnp.mean(d_rn, axis=-1, keepdims=True)
                        - rn * jnp.mean(d_rn * rn, axis=-1, keepdims=True))
        acc_ref[OUT_ROW_GATE:OUT_ROW_GATE + 1, :] += jnp.sum(d_res * z, axis=0, keepdims=True)
        acc_ref[OUT_ROW_LN_G:OUT_ROW_LN_G + 1, :] += jnp.sum(d_out * rn, axis=0, keepdims=True)
        acc_ref[OUT_ROW_LN_B:OUT_ROW_LN_B + 1, :] += jnp.sum(d_out, axis=0, keepdims=True)
        acc_ref[OUT_ROW_LOSS:OUT_ROW_LOSS + 1, :] += (0.5 / dm) * jnp.sum(err * err, axis=0, keepdims=True)
        gx_ref[...] = ALPHA * d_res
        dz = (d_res * gate).astype(BF16)
        dz_ref[...] = dz
        dy_ref[...] = _nt(dz, w_ref[...])

    rows_d = pl.BlockSpec((tm, dm), lambda i: (i, 0))
    rows_i = pl.BlockSpec((tm, di), lambda i: (i, 0))
    return pl.pallas_call(
        body, name="out_block", grid=(t_rows // tm,),
        in_specs=[rows_i, pl.BlockSpec((di, dm), lambda i: (0, 0)), rows_d, rows_d,
                  pl.BlockSpec((8, dm), lambda i: (0, 0))],
        out_specs=(rows_d, rows_i, rows_d, pl.BlockSpec((8, dm), lambda i: (0, 0))),
        out_shape=(jax.ShapeDtypeStruct((t_rows, dm), BF16), jax.ShapeDtypeStruct((t_rows, di), F32),
                   jax.ShapeDtypeStruct((t_rows, dm), F32), jax.ShapeDtypeStruct((8, dm), F32)),
        compiler_params=_params(("arbitrary",)),
    )(y, w_out, x, target, prm)


def _mod_fwd(c16, w_mod, tn):
    dm, n = w_mod.shape

    def body(c_ref, w_ref, o_ref, a_ref):
        a = _silu(c_ref[...])
        a_ref[...] = a
        o_ref[...] = _nn(a, w_ref[...], HIGHEST)

    return pl.pallas_call(
        body, name="mod_fwd", grid=(n // tn,),
        in_specs=[pl.BlockSpec((16, dm), lambda j: (0, 0)), pl.BlockSpec((dm, tn), lambda j: (0, j))],
        out_specs=(pl.BlockSpec((16, tn), lambda j: (0, j)), pl.BlockSpec((16, dm), lambda j: (0, 0))),
        out_shape=(jax.ShapeDtypeStruct((16, n), F32), jax.ShapeDtypeStruct((16, dm), F32)),
        compiler_params=_params(("arbitrary",)),
    )(c16, w_mod)


def _mod_bwd(a16, dm16, w_mod, tn):
    dm, n = w_mod.shape

    def body(a_ref, d_ref, w_ref, dw_ref, dc_ref):
        @pl.when(pl.program_id(0) == 0)
        def _():
            dc_ref[...] = jnp.zeros_like(dc_ref)
        dw_ref[...] = _tn(a_ref[...], d_ref[...], HIGHEST)
        dc_ref[...] += _nt(d_ref[...], w_ref[...], HIGHEST)

    return pl.pallas_call(
        body, name="mod_bwd", grid=(n // tn,),
        in_specs=[pl.BlockSpec((16, dm), lambda j: (0, 0)), pl.BlockSpec((16, tn), lambda j: (0, j)),
                  pl.BlockSpec((dm, tn), lambda j: (0, j))],
        out_specs=(pl.BlockSpec((dm, tn), lambda j: (0, j)), pl.BlockSpec((16, dm), lambda j: (0, 0))),
        out_shape=(jax.ShapeDtypeStruct((dm, n), F32), jax.ShapeDtypeStruct((16, dm), F32)),
        compiler_params=_params(("arbitrary",)),
    )(a16, dm16, w_mod)


def _sum_devices(g, fold_rows):
    n_dev, rows, n = g.shape

    def body(g_ref, s_ref, t_ref):
        s = g_ref[0]
        for dev in range(1, n_dev):
            s = s + g_ref[dev]
        t_ref[...] = jnp.broadcast_to(jnp.sum(s, axis=-1, keepdims=True), (rows, LANE))
        s_ref[...] = s
        s_ref[0:fold_rows, :] = s[0:fold_rows] + s[fold_rows:2 * fold_rows]

    return pl.pallas_call(
        body, name="sum_devices",
        out_shape=(jax.ShapeDtypeStruct((rows, n), F32), jax.ShapeDtypeStruct((rows, LANE), F32)),
        compiler_params=_params(),
    )(g)


def _c_ctx_grad(parts, c_ctx_row):
    def body(p_ref, c_ref, o_ref):
        s = p_ref[0]
        for chip in range(1, N_CHIPS):
            s = s + p_ref[2 * chip]
        cv = c_ref[...]
        sg = _sigmoid(cv)
        o_ref[...] = s * (sg * (1.0 + cv * (1.0 - sg)))

    return pl.pallas_call(
        body, name="c_ctx_grad", out_shape=jax.ShapeDtypeStruct(parts.shape[1:], F32), compiler_params=_params(),
    )(parts, c_ctx_row)


def _sum_pair(name, mine, got):
    def body(a_ref, b_ref, o_ref):
        o_ref[...] = (a_ref[...] + b_ref[...]).astype(BF16)

    k, rows, n = mine.shape
    tr = _largest_divisor(rows, max(SUBLANE_BF16, (1 << 18) // n), SUBLANE_BF16)
    spec = pl.BlockSpec((1, tr, n), lambda kk, i: (kk, i, 0))
    return pl.pallas_call(
        body, name=name, grid=(k, rows // tr), in_specs=[spec, spec], out_specs=spec,
        out_shape=jax.ShapeDtypeStruct(mine.shape, BF16), compiler_params=_params(("parallel", "parallel")),
    )(mine, got)


def _sum_chips(name, got):
    k, rows, n = got.shape
    tr = _largest_divisor(rows, max(SUBLANE_BF16, (1 << 18) // n), SUBLANE_BF16)

    def body(g_ref, o_ref):
        total = g_ref[0].astype(F32)
        for kk in range(1, k):
            total = total + g_ref[kk].astype(F32)
        o_ref[...] = total

    return pl.pallas_call(
        body, name=name, grid=(rows // tr,),
        in_specs=[pl.BlockSpec((k, tr, n), lambda i: (0, i, 0))], out_specs=pl.BlockSpec((tr, n), lambda i: (i, 0)),
        out_shape=jax.ShapeDtypeStruct((rows, n), F32), compiler_params=_params(("parallel",)),
    )(got)


def _adamw(name, w, g, m, v):
    rows, n = w.shape
    tr = rows if rows % 8 else _largest_divisor(rows, max(8, (1 << 18) // n), 8)

    def body(w_ref, g_ref, m_ref, v_ref, d_ref, mo_ref, vo_ref):
        gv = g_ref[...]
        m2 = ADAM_B1 * m_ref[...] + (1.0 - ADAM_B1) * gv
        v2 = ADAM_B2 * v_ref[...] + (1.0 - ADAM_B2) * jnp.square(gv)
        m_hat = m2 / (1.0 - ADAM_B1 ** ADAM_STEP)
        v_hat = v2 / (1.0 - ADAM_B2 ** ADAM_STEP)
        d_ref[...] = -ADAM_LR * (m_hat / (jnp.sqrt(v_hat) + ADAM_EPS) + ADAM_WD * w_ref[...])
        mo_ref[...] = m2
        vo_ref[...] = v2

    spec = pl.BlockSpec((tr, n), lambda i: (i, 0))
    sds = jax.ShapeDtypeStruct((rows, n), F32)
    return pl.pallas_call(
        body, name=name, grid=(rows // tr,), in_specs=[spec] * 4, out_specs=(spec,) * 3,
        out_shape=(sds, sds, sds), compiler_params=_params(("parallel",)),
    )(w, g, m, v)


PACK_LANES = 1024


def _pack(pieces):
    flat = jnp.concatenate([p.reshape(-1) for p in pieces])
    total = -(-flat.shape[0] // (8 * PACK_LANES)) * 8 * PACK_LANES
    return jnp.pad(flat, (0, total - flat.shape[0])).reshape(-1, PACK_LANES)


def _unpack(packed, shapes):
    flat = packed.reshape(-1)
    out, off = [], 0
    for shp in shapes:
        size = math.prod(shp)
        out.append(flat[off:off + size].reshape(shp))
        off += size
    return out


def _rows8(rows, width):
    flat = [r.reshape(width) for r in rows] + [jnp.zeros(((8 - len(rows)) * width,), F32)]
    return jnp.concatenate(flat).reshape(8, width)


def _reduce_scatter(tag, chip, ci, mine, other):
    got = _sibling_swap("rs_pair_" + tag, [other])[0]
    pair = _sum_pair("rs_pair_sum_" + tag, mine, got)
    landed = _own_block(chip, pair, _chip_scatter("rs_chips_" + tag, [pair])[0])
    half = _sum_chips("rs_chip_sum_" + tag, landed)
    return _join_halves(ci, half, _sibling_swap("rs_join_" + tag, [half])[0], 0)


def kernel(x, c, ctx, c_ctx, w_mod, b_mod, w_in, conv_w, conv_b, hg_lb, ml_gate_b, hg_norm_w, ml_norm_w, w_out, ln_g, ln_b, loss_target, m_c_ctx, m_w_mod, m_b_mod, m_w_in, m_conv_w, m_conv_b, m_hg_lb, m_ml_gate_b, m_hg_norm_w, m_ml_norm_w, m_w_out, m_ln_g, m_ln_b, v_c_ctx, v_w_mod, v_b_mod, v_w_in, v_conv_w, v_conv_b, v_hg_lb, v_ml_gate_b, v_hg_norm_w, v_ml_norm_w, v_w_out, v_ln_g, v_ln_b):
    t_rows, dm = x.shape[1], x.shape[2]
    c_rows = ctx.shape[1]
    w = hg_norm_w.shape[1]
    n_ml = ml_gate_b.shape[-1]
    n_hg = w // HG_DK
    di = 2 * w
    n_in = 10 * w + 4 * n_ml
    ns = w_in.shape[2]
    nm = w_mod.shape[2]
    n_pad = 10 * w + LANE
    r_rows = t_rows + c_rows
    row_gcd = math.gcd(t_rows, c_rows)
    hg_chunk, ml_chunk = math.gcd(HG_CHUNK, row_gcd), math.gcd(ML_CHUNK, row_gcd)
    hg_counts = (t_rows // hg_chunk, c_rows // hg_chunk, hg_chunk)
    ml_counts = (t_rows // ml_chunk, c_rows // ml_chunk, ml_chunk)
    assert ml_norm_w.shape[1] == w and di == dm and N_CHIPS * ns == n_in and N_CHIPS * nm == 3 * dm
    assert w_out.shape[1] * N_CHIPS == di and 4 * n_ml <= LANE and t_rows % GRID_W == 0

    xi, yi, ci = lax.axis_index("x"), lax.axis_index("y"), lax.axis_index("c")
    chip = 2 * xi + yi
    dev = 4 * xi + 2 * yi + ci

    tm = _largest_divisor(math.gcd(t_rows, c_rows), 256, 8)
    tm_mm = _largest_divisor(r_rows, 1088, SUBLANE_BF16)
    tn_mm = LANE * _largest_divisor(n_pad // LANE, 9)
    tn_mod = _largest_divisor(nm, 512, LANE)

    shard_shapes = [(dm,), (2, 2, w // N_CHIPS), (3, 3, di // N_CHIPS)]
    g1 = _all_gather8("gather_inputs", _pack([c, hg_lb, conv_w]))
    per_dev = [_unpack(g1[i], shard_shapes) for i in range(N_DEV)]
    c_all = jnp.stack([p[0] for p in per_dev])
    lb_full = jnp.concatenate([per_dev[2 * k][1] for k in range(N_CHIPS)], axis=-1)
    conv_w9 = jnp.concatenate([per_dev[2 * k][2] for k in range(N_CHIPS)], axis=-1).reshape(9, di)

    c16 = jnp.concatenate([c_all, c_ctx[None], jnp.zeros((16 - N_DEV - 1, dm), F32)])
    mod_part, a16 = _mod_fwd(c16, w_mod[0], tn_mod)
    g2 = _all_gather8("gather_mod", mod_part)
    mod_all = jnp.concatenate([g2[2 * k] for k in range(N_CHIPS)], axis=1) + b_mod
    mod_x = lax.dynamic_index_in_dim(mod_all, dev, 0, keepdims=False).reshape(3, dm)
    mod_c = mod_all[N_DEV].reshape(3, dm)
    prm = jnp.stack([_rows8(list(mod_x), dm), _rows8(list(mod_c), dm)])

    halves = [lax.dynamic_slice_in_dim(a[0].astype(BF16), ci * (a.shape[1] // 2), a.shape[1] // 2, 0)
              for a in (w_in, w_out)]
    fetched = [_own_block(chip, own, got)
               for own, got in zip(halves, _all_gather_chips("gather_weights", halves))]
    swapped = _sibling_swap("gather_weights_pair", fetched)
    gw_in, gw_out = [_join_halves(ci, a, b, 1) for a, b in zip(fetched, swapped)]
    w_full = jnp.concatenate([gw_in[k] for k in range(N_CHIPS)] + [jnp.zeros((dm, n_pad - n_in), BF16)], axis=1)
    w_out_full = gw_out.reshape(di, dm)

    xc = jnp.concatenate([x[0], ctx[0]])
    hc = _modulate_fwd(xc, prm, t_rows, tm)
    u = _mm_nn("in_proj", hc, w_full, tm_mm, tn_mm, F32)
    (o_f, o_b), hg_saved = _hg_scan_fwd(u, lb_full, w, *hg_counts)
    qk = _conv_fwd(u, conv_w9, conv_b, t_rows, c_rows, w, LANE)
    gate_b_row = jnp.pad(ml_gate_b.reshape(1, -1), ((0, 0), (0, LANE - 4 * n_ml)))
    (h_f, h_b), ml_saved = _ml_scan_fwd(qk, u, gate_b_row, w, n_ml, *ml_counts)
    y = _post_fwd(o_f, o_b, h_f, h_b, u, hg_norm_w, ml_norm_w, t_rows, w, n_hg, n_ml, tm)
    prm_out = _rows8([mod_x[2], ln_g, ln_b], dm)
    dz, dy, gx_direct, acc_out = _out_block(y, w_out_full, x[0], loss_target[0], prm_out, tm // 2)

    d_w_out = _mm_tn("d_w_out", y, dz, _largest_divisor(t_rows, 1024, SUBLANE_BF16),
                     _largest_divisor(dm, 1024, LANE))
    d_o, d_h, d_az, d_bo, d_bz, d_wa, d_wb = _post_bwd(o_f, o_b, h_f, h_b, u, hg_norm_w, ml_norm_w, dy,
                                                        t_rows, w, n_hg, n_ml, tm)
    (d_aq_f, d_aff, d_ai_f, d_lb_f), (d_aq_b, d_afb, d_ai_b, d_lb_b) = _hg_scan_bwd(
        u, lb_full, hg_saved, d_o, w, *hg_counts)
    (d_qk_f, d_v_f, d_g_f, d_gb_f), (d_qk_b, d_v_b, d_g_b, d_gb_b) = _ml_scan_bwd(
        qk, u, gate_b_row, ml_saved, d_h, w, n_ml, *ml_counts)
    d_bqk, d_cw, d_cb = _conv_bwd(u, (d_qk_f, d_qk_b), conv_w9, conv_b, t_rows, c_rows, w, LANE)
    du = _assemble_du([(d_aq_f, d_aq_b), d_aff, d_afb, (d_ai_f, d_ai_b), d_az, d_bqk, (d_v_f, d_v_b), d_bo, d_bz],
                      (d_g_f, d_g_b), n_pad, tm // 2)
    d_w_in = _mm_tn("d_w_in", hc, du, tm_mm, tn_mm)
    d_hc = _mm_nt("d_h", du, w_full, tm_mm, tn_mm)
    gx_all, acc_mod = _modulate_bwd(xc, d_hc, prm, gx_direct, t_rows, tm)
    grad_x = gx_all[:t_rows][None]

    half_in = dm // 2
    mine_in = lax.dynamic_slice_in_dim(d_w_in, ci * half_in, half_in, 0)
    other_in = lax.dynamic_slice_in_dim(d_w_in, (1 - ci) * half_in, half_in, 0)
    pieces_in = lambda a: jnp.stack([a[:, k * ns:(k + 1) * ns] for k in range(N_CHIPS)])
    g_w_in = _reduce_scatter("w_in", chip, ci, pieces_in(mine_in), pieces_in(other_in))
    d_w_out4 = d_w_out.reshape(N_CHIPS, 2, di // (2 * N_CHIPS), dm)
    g_w_out = _reduce_scatter("w_out", chip, ci, lax.dynamic_index_in_dim(d_w_out4, ci, 1, keepdims=False),
                              lax.dynamic_index_in_dim(d_w_out4, 1 - ci, 1, keepdims=False))

    zero_row = jnp.zeros((dm,), F32)
    d_gb = jnp.concatenate([d_gb_f[:, 0:n_ml], d_gb_b[:, n_ml:2 * n_ml], d_gb_f[:, 2 * n_ml:3 * n_ml],
                            d_gb_b[:, 3 * n_ml:4 * n_ml], jnp.zeros((1, dm - 4 * n_ml), F32)], axis=1)
    rows = [acc_mod[0, 0], acc_mod[0, 1], acc_out[OUT_ROW_GATE],
            acc_mod[1, 0], acc_mod[1, 1], zero_row]
    rows += list(d_cw) + [d_cb[0], d_lb_f.reshape(dm), d_lb_b.reshape(dm),
                          jnp.concatenate([d_wa[0], d_wb[0]]), acc_out[OUT_ROW_LN_G], acc_out[OUT_ROW_LN_B],
                          acc_out[OUT_ROW_LOSS], d_gb[0], zero_row]
    ROW_CW, ROW_CB, ROW_LB, ROW_NORM, ROW_LN_G, ROW_LN_B, ROW_LOSS, ROW_GB = 6, 15, 16, 18, 19, 20, 21, 22
    g3 = _all_gather8("gather_small_grads", jnp.concatenate([r.reshape(dm) for r in rows]).reshape(len(rows), dm))
    sums, totals = _sum_devices(g3, 3)
    loss = totals[ROW_LOSS, 0]
    dm16 = jnp.concatenate([g3[:, 0:3, :].reshape(N_DEV, 3 * dm), sums[3:6].reshape(1, 3 * dm),
                            jnp.zeros((16 - N_DEV - 1, 3 * dm), F32)])
    g_w_mod, dc16 = _mod_bwd(a16, lax.dynamic_slice_in_dim(dm16, chip * nm, nm, 1), w_mod[0], tn_mod)
    g4 = _all_gather8("gather_c_ctx", jnp.pad(dc16[N_DEV:N_DEV + 1], ((0, 7), (0, 0))))
    g_c_ctx = _c_ctx_grad(g4, jnp.broadcast_to(c_ctx[None], (8, dm)))[0]

    chip_cols = lambda a, width: lax.dynamic_slice_in_dim(a, chip * width, width, a.ndim - 1)
    grads = {
        "c_ctx": g_c_ctx,
        "w_mod": g_w_mod[None],
        "b_mod": sums[0:3].reshape(1, 3 * dm),
        "w_in": g_w_in[None],
        "conv_w": chip_cols(sums[ROW_CW:ROW_CW + 9].reshape(1, 3, 3, di), di // N_CHIPS),
        "conv_b": sums[ROW_CB][None],
        "hg_lb": chip_cols(sums[ROW_LB:ROW_LB + 2].reshape(2, 2, w), w // N_CHIPS),
        "ml_gate_b": sums[ROW_GB, 0:4 * n_ml].reshape(1, 4, n_ml),
        "hg_norm_w": sums[ROW_NORM, 0:w][None],
        "ml_norm_w": sums[ROW_NORM, w:2 * w][None],
        "w_out": g_w_out[None],
        "ln_g": sums[ROW_LN_G][None],
        "ln_b": sums[ROW_LN_B][None],
    }
    weights = dict(c_ctx=c_ctx, w_mod=w_mod, b_mod=b_mod, w_in=w_in, conv_w=conv_w, conv_b=conv_b, hg_lb=hg_lb,
                   ml_gate_b=ml_gate_b, hg_norm_w=hg_norm_w, ml_norm_w=ml_norm_w, w_out=w_out, ln_g=ln_g, ln_b=ln_b)
    mom1 = dict(c_ctx=m_c_ctx, w_mod=m_w_mod, b_mod=m_b_mod, w_in=m_w_in, conv_w=m_conv_w, conv_b=m_conv_b,
                hg_lb=m_hg_lb, ml_gate_b=m_ml_gate_b, hg_norm_w=m_hg_norm_w, ml_norm_w=m_ml_norm_w, w_out=m_w_out,
                ln_g=m_ln_g, ln_b=m_ln_b)
    mom2 = dict(c_ctx=v_c_ctx, w_mod=v_w_mod, b_mod=v_b_mod, w_in=v_w_in, conv_w=v_conv_w, conv_b=v_conv_b,
                hg_lb=v_hg_lb, ml_gate_b=v_ml_gate_b, hg_norm_w=v_hg_norm_w, ml_norm_w=v_ml_norm_w, w_out=v_w_out,
                ln_g=v_ln_g, ln_b=v_ln_b)
    names = list(weights)
    big = ("w_mod", "w_in", "w_out")
    small = [n for n in names if n not in big]

    delta, new_m, new_v = {}, {}, {}
    for n in big:
        as2d = lambda a: a.reshape(a.shape[-2], a.shape[-1])
        res = _adamw("adamw_" + n, as2d(weights[n]), as2d(grads[n]), as2d(mom1[n]), as2d(mom2[n]))
        delta[n], new_m[n], new_v[n] = (a.reshape(weights[n].shape) for a in res)
    small_shapes = [weights[n].shape for n in small]
    res = _adamw("adamw_small", *(_pack([src[n] for n in small]) for src in (weights, grads, mom1, mom2)))
    for out, packed in zip((delta, new_m, new_v), res):
        for n, a in zip(small, _unpack(packed, small_shapes)):
            out[n] = a

    return (loss, grad_x, *[grads[n].reshape(weights[n].shape) for n in names], *[delta[n] for n in names],
            *[new_m[n] for n in names], *[new_v[n] for n in names])
```

```python
import functools
import math

import jax
import jax.numpy as jnp
from jax import lax
from jax.experimental import pallas as pl
from jax.experimental.pallas import tpu as pltpu

F32 = jnp.float32
BF16 = jnp.bfloat16
HIGHEST = lax.Precision.HIGHEST
MESH = pl.DeviceIdType.MESH

HG_CHUNK = 64
ML_CHUNK = 256
GRID_W = 64
HG_DK = 128
LANE = 128
SUBLANE_BF16 = 16
ALPHA = 2.0 ** 0.25
LN_EPS = 1e-5
NORM_EPS = 1e-6
ADAM_LR = 0.001
ADAM_B1 = 0.9
ADAM_B2 = 0.999
ADAM_EPS = 1e-08
ADAM_WD = 0.01
ADAM_STEP = 10
VMEM_LIMIT = 56 * 1024 * 1024
N_CHIPS = 4
N_DEV = 8


def _params(sem=None):
    return pltpu.CompilerParams(dimension_semantics=sem, vmem_limit_bytes=VMEM_LIMIT)


def _largest_divisor(n, cap, multiple=1):
    best = None
    for d in range(multiple, min(n, cap) + 1, multiple):
        if n % d == 0:
            best = d
    assert best is not None, (n, cap, multiple)
    return best


def _sigmoid(x):
    return jax.nn.sigmoid(x)


def _silu(x):
    return x * jax.nn.sigmoid(x)


def _dot(a, b, dims, precision=None):
    return lax.dot_general(a, b, (dims, ((), ())), precision=precision, preferred_element_type=F32)


def _nn(a, b, precision=None):
    return _dot(a, b, ((1,), (0,)), precision)


def _nt(a, b, precision=None):
    return _dot(a, b, ((1,), (1,)), precision)


def _tn(a, b, precision=None):
    return _dot(a, b, ((0,), (0,)), precision)


def _visible(n, rev):
    r = lax.broadcasted_iota(jnp.int32, (n, n), 0)
    c = lax.broadcasted_iota(jnp.int32, (n, n), 1)
    return (r <= c) if rev else (r >= c)


def _hg_chunk(states, aq, af, ai, lb0, lb1, rev):
    n_heads = len(states)
    lb = _sigmoid(lb0 - lb1)
    f = lb + (1.0 - lb) * _sigmoid(af)
    g = jnp.log(f)
    k = 1.0 - f
    q = _silu(aq)
    chunk = aq.shape[0]
    vis = _visible(chunk, rev)
    b = _nn(vis.astype(F32), g, HIGHEST)
    last = 0 if rev else chunk - 1
    b_end = b[last:last + 1]
    b_mid = b[chunk // 2:chunk // 2 + 1]
    q_inter = q * jnp.exp(b)
    q_intra = q * jnp.exp(b - b_mid)
    k_intra = k * jnp.exp(b_mid - b)
    k_dec = k * jnp.exp(b_end - b)
    e_end = jnp.exp(b_end)
    new_states, outs = [], []
    for h in range(n_heads):
        sl = slice(h * HG_DK, (h + 1) * HG_DK)
        s_t = states[h]
        scores = jnp.where(vis, _nt(q_intra[:, sl], k_intra[:, sl]), 0.0)
        outs.append(_nt(q_inter[:, sl], s_t) + _nn(scores, ai[:, sl]))
        new_states.append(e_end[:, sl] * s_t + _tn(ai[:, sl], k_dec[:, sl]))
    return new_states, jnp.concatenate(outs, axis=1)


def _ml_chunk(state, q, k, v, g, gb, rev, d):
    cms, nvs, mbs = state
    n_heads = len(cms)
    dh = q.shape[1] // n_heads
    ga = g + gb
    log_f_all = jax.nn.log_sigmoid(ga)
    chunk = q.shape[0]
    vis = _visible(chunk, rev)
    b_all = _nn(vis.astype(F32), log_f_all, HIGHEST)
    last = 0 if rev else chunk - 1
    k = k * (dh ** -0.5)
    new_c, new_n, new_m, outs = [], [], [], []
    for h in range(n_heads):
        ci = d * n_heads + h
        cf = (2 + d) * n_heads + h
        sl = slice(h * dh, (h + 1) * dh)
        qh, kh, vh = q[:, sl], k[:, sl], v[:, sl]
        li = ga[:, ci:ci + 1]
        b = b_all[:, cf:cf + 1]
        m = mbs[h][:, 0:1]
        row = jnp.transpose(li - b)
        log_w = jnp.where(vis, b + row, -jnp.inf)
        m_inter = b + m
        m_t = jnp.maximum(m_inter, jnp.max(log_w, axis=-1, keepdims=True))
        w_inter = jnp.exp(m_inter - m_t)
        w_qk = jnp.exp(log_w - m_t) * _nt(qh, kh)
        num = w_inter * _nt(qh, cms[h]) + _nn(w_qk, vh)
        den = w_inter * jnp.sum(qh * nvs[h], axis=-1, keepdims=True) + jnp.sum(w_qk, axis=-1, keepdims=True)
        outs.append(num / jnp.maximum(jnp.abs(den), jnp.exp(-m_t)))
        m_new = m_t[last:last + 1]
        b_end = b[last:last + 1]
        w_s = jnp.exp(b_end - b + li - m_new)
        decay = jnp.exp(b_end + m - m_new)
        new_c.append(decay * cms[h] + _tn(w_s * vh, kh))
        new_n.append(decay * nvs[h] + jnp.sum(w_s * kh, axis=0, keepdims=True))
        new_m.append(jnp.broadcast_to(m_new, (1, LANE)))
    return (new_c, new_n, new_m), jnp.concatenate(outs, axis=1)


def _post_fn(o_f, o_b, az, h_f, h_b, bo, bz, wa, wb, n_hg, n_ml):
    o = o_f + o_b
    parts = []
    for h in range(n_hg):
        s = o[:, h * HG_DK:(h + 1) * HG_DK]
        parts.append(s * lax.rsqrt(jnp.mean(s * s, axis=-1, keepdims=True) + NORM_EPS))
    y_a = jnp.concatenate(parts, axis=1) * wa * _silu(az)
    hh = h_f + h_b
    dh = hh.shape[1] // n_ml
    parts = []
    for h in range(n_ml):
        s = hh[:, h * dh:(h + 1) * dh]
        mu = jnp.mean(s, axis=-1, keepdims=True)
        sc = s - mu
        parts.append(sc * lax.rsqrt(jnp.mean(sc * sc, axis=-1, keepdims=True) + NORM_EPS))
    y_b = jnp.concatenate(parts, axis=1) * wb * _sigmoid(bo) * _silu(bz)
    return jnp.concatenate([y_a, y_b], axis=1)


def _chip_of(dev):
    return 2 * dev[0] + dev[1]


def _index_of(dev):
    return 4 * dev[0] + 2 * dev[1] + dev[2]


def _exchange(name, srcs, out_shapes, transfers, local_copies=()):
    n_in, n_out, n_t, n_l = len(srcs), len(out_shapes), len(transfers), len(local_copies)

    def body(*refs):
        ins, outs = refs[:n_in], refs[n_in:n_in + n_out]
        send_sems, recv_sems, local_sems = refs[n_in + n_out:]
        me = (lax.axis_index("x"), lax.axis_index("y"), lax.axis_index("c"))

        def pick(ref, fn, *who):
            return ref if fn is None else ref.at[fn(*who)]

        sends, recvs, locs = [], [], []
        for t, (mask, si, sfn, di, dfn) in enumerate(transfers):
            peer = tuple(1 - p if flip else p for p, flip in zip(me, mask))
            sends.append(pltpu.make_async_remote_copy(
                src_ref=pick(ins[si], sfn, me, peer), dst_ref=pick(outs[di], dfn, me, peer),
                send_sem=send_sems.at[t], recv_sem=recv_sems.at[t], device_id=peer, device_id_type=MESH))
            landing = pick(outs[di], dfn, peer, me)
            recvs.append(pltpu.make_async_remote_copy(
                src_ref=landing, dst_ref=landing,
                send_sem=send_sems.at[t], recv_sem=recv_sems.at[t], device_id=peer, device_id_type=MESH))
        for l, (si, sfn, di, dfn) in enumerate(local_copies):
            locs.append(pltpu.make_async_copy(pick(ins[si], sfn, me), pick(outs[di], dfn, me), local_sems.at[l]))
        for cp in locs + sends:
            cp.start()
        for cp in recvs:
            cp.wait_recv()
        for cp in sends:
            cp.wait_send()
        for cp in locs:
            cp.wait()

    hbm = pl.BlockSpec(memory_space=pltpu.HBM)
    return pl.pallas_call(
        body, name=name, out_shape=tuple(out_shapes),
        in_specs=[hbm] * n_in, out_specs=tuple([hbm] * n_out),
        scratch_shapes=[pltpu.SemaphoreType.DMA((n_t,)), pltpu.SemaphoreType.DMA((n_t,)),
                        pltpu.SemaphoreType.DMA((max(n_l, 1),))],
    )(*srcs)


ALL_MASKS = [(mx, my, mc) for mx in (0, 1) for my in (0, 1) for mc in (0, 1)][1:]
CHIP_MASKS = [(1, 0, 0), (0, 1, 0), (1, 1, 0)]
SIBLING_MASK = (0, 0, 1)


def _all_gather8(name, v):
    out = jax.ShapeDtypeStruct((N_DEV,) + v.shape, v.dtype)
    slot = lambda sender, receiver: _index_of(sender)
    transfers = [(mask, 0, None, 0, slot) for mask in ALL_MASKS]
    return _exchange(name, [v], [out], transfers, [(0, None, 0, lambda me: _index_of(me))])[0]


def _all_gather_chips(name, arrays):
    outs = [jax.ShapeDtypeStruct((N_CHIPS,) + a.shape, a.dtype) for a in arrays]
    slot = lambda sender, receiver: _chip_of(sender)
    transfers = [(mask, i, None, i, slot) for i in range(len(arrays)) for mask in CHIP_MASKS]
    return _exchange(name, arrays, outs, transfers)


def _sibling_swap(name, arrays):
    outs = [jax.ShapeDtypeStruct(a.shape, a.dtype) for a in arrays]
    return _exchange(name, arrays, outs, [(SIBLING_MASK, i, None, i, None) for i in range(len(arrays))])


def _chip_scatter(name, arrays):
    outs = [jax.ShapeDtypeStruct(a.shape, a.dtype) for a in arrays]
    transfers = [(mask, i, lambda s, r: _chip_of(r), i, lambda s, r: _chip_of(s))
                 for i in range(len(arrays)) for mask in CHIP_MASKS]
    return _exchange(name, arrays, outs, transfers)


def _own_block(chip, own, blocks):
    sel = (lax.broadcasted_iota(jnp.int32, (N_CHIPS,) + (1,) * (blocks.ndim - 1), 0) == chip)
    return jnp.where(sel, own if own.ndim == blocks.ndim else own[None], blocks)


def _join_halves(ci, mine, other, axis):
    return jnp.where(ci == 0, jnp.concatenate([mine, other], axis=axis), jnp.concatenate([other, mine], axis=axis))


def _mm_nt(name, a, b, tm, tn, out_dtype):
    m, k = a.shape
    n = b.shape[0]

    def body(a_ref, b_ref, o_ref):
        o_ref[...] = _nt(a_ref[...], b_ref[...]).astype(out_dtype)

    return pl.pallas_call(
        body, name=name, grid=(n // tn, m // tm),
        in_specs=[pl.BlockSpec((tm, k), lambda j, i: (i, 0)), pl.BlockSpec((tn, k), lambda j, i: (j, 0))],
        out_specs=pl.BlockSpec((tm, tn), lambda j, i: (i, j)),
        out_shape=jax.ShapeDtypeStruct((m, n), out_dtype),
        compiler_params=_params(("parallel", "parallel")),
    )(a, b)


def _mm_acc(name, a, b, tm, tk):
    m, kc = a.shape
    n = b.shape[1]

    def body(a_ref, b_ref, o_ref):
        @pl.when(pl.program_id(1) == 0)
        def _():
            o_ref[...] = jnp.zeros_like(o_ref)
        o_ref[...] += _nn(a_ref[...], b_ref[...])

    return pl.pallas_call(
        body, name=name, grid=(m // tm, kc // tk),
        in_specs=[pl.BlockSpec((tm, tk), lambda i, kk: (i, kk)), pl.BlockSpec((tk, n), lambda i, kk: (kk, 0))],
        out_specs=pl.BlockSpec((tm, n), lambda i, kk: (i, 0)),
        out_shape=jax.ShapeDtypeStruct((m, n), F32),
        compiler_params=_params(("parallel", "arbitrary")),
    )(a, b)


def _mm_tn(name, a, b, tm, tk):
    kr, m = a.shape
    n = b.shape[1]

    def body(a_ref, b_ref, o_ref):
        @pl.when(pl.program_id(1) == 0)
        def _():
            o_ref[...] = jnp.zeros_like(o_ref)
        o_ref[...] += _tn(a_ref[...], b_ref[...])

    return pl.pallas_call(
        body, name=name, grid=(m // tm, kr // tk),
        in_specs=[pl.BlockSpec((tk, tm), lambda i, kk: (kk, i)), pl.BlockSpec((tk, n), lambda i, kk: (kk, 0))],
        out_specs=pl.BlockSpec((tm, n), lambda i, kk: (i, 0)),
        out_shape=jax.ShapeDtypeStruct((m, n), F32),
        compiler_params=_params(("parallel", "arbitrary")),
    )(a, b)


def _modulate_fwd(xc, prm, t_rows, tm):
    r, dm = xc.shape
    first_ctx = t_rows // tm

    def body(x_ref, p_ref, h_ref):
        x = x_ref[...]
        mu = jnp.mean(x, axis=-1, keepdims=True)
        xm = x - mu
        n = xm * lax.rsqrt(jnp.mean(xm * xm, axis=-1, keepdims=True) + LN_EPS)
        h_ref[...] = (n * (1.0 + p_ref[0, 1:2, :]) + p_ref[0, 0:1, :]).astype(BF16)

    return pl.pallas_call(
        body, name="modulate_fwd", grid=(r // tm,),
        in_specs=[pl.BlockSpec((tm, dm), lambda i: (i, 0)),
                  pl.BlockSpec((1, 8, dm), lambda i: ((i >= first_ctx).astype(jnp.int32), 0, 0))],
        out_specs=pl.BlockSpec((tm, dm), lambda i: (i, 0)),
        out_shape=jax.ShapeDtypeStruct((r, dm), BF16),
        compiler_params=_params(("parallel",)),
    )(xc, prm)


def _modulate_bwd(xc, dh, prm, gx_direct, t_rows, tm):
    r, dm = xc.shape
    first_ctx = t_rows // tm
    cls = lambda i: (i >= first_ctx).astype(jnp.int32)

    def body(x_ref, dh_ref, p_ref, gd_ref, gx_ref, acc_ref):
        i = pl.program_id(0)

        @pl.when((i == 0) | (i == first_ctx))
        def _():
            acc_ref[...] = jnp.zeros_like(acc_ref)

        x = x_ref[...]
        dh_v = dh_ref[...]
        mu = jnp.mean(x, axis=-1, keepdims=True)
        xm = x - mu
        rstd = lax.rsqrt(jnp.mean(xm * xm, axis=-1, keepdims=True) + LN_EPS)
        n = xm * rstd
        acc_ref[0, 0:1, :] += jnp.sum(dh_v, axis=0, keepdims=True)
        acc_ref[0, 1:2, :] += jnp.sum(dh_v * n, axis=0, keepdims=True)
        dn = dh_v * (1.0 + p_ref[0, 1:2, :])
        dx = rstd * (dn - jnp.mean(dn, axis=-1, keepdims=True) - n * jnp.mean(dn * n, axis=-1, keepdims=True))
        gx_ref[...] = dx + gd_ref[...]

    return pl.pallas_call(
        body, name="modulate_bwd", grid=(r // tm,),
        in_specs=[pl.BlockSpec((tm, dm), lambda i: (i, 0)), pl.BlockSpec((tm, dm), lambda i: (i, 0)),
                  pl.BlockSpec((1, 8, dm), lambda i: (cls(i), 0, 0)),
                  pl.BlockSpec((tm, dm), lambda i: (jnp.minimum(i, first_ctx - 1), 0))],
        out_specs=(pl.BlockSpec((tm, dm), lambda i: (i, 0)), pl.BlockSpec((1, 8, dm), lambda i: (cls(i), 0, 0))),
        out_shape=(jax.ShapeDtypeStruct((r, dm), F32), jax.ShapeDtypeStruct((2, 8, dm), F32)),
        compiler_params=_params(("arbitrary",)),
    )(xc, dh, prm, gx_direct)


def _conv_parts(t_rows, c_rows):
    return ((0, t_rows, t_rows // GRID_W, GRID_W), (t_rows, c_rows, 1, c_rows))


def _col_shifts(x2, rows_g, width_g):
    n, ct = x2.shape
    col = lax.broadcasted_iota(jnp.int32, (width_g, ct), 0)
    as_grid = lambda a: a.reshape(rows_g, width_g, ct)
    left = as_grid(pltpu.roll(x2, 1, 0)) * (col >= 1).astype(F32)
    right = as_grid(pltpu.roll(x2, n - 1, 0)) * (col <= width_g - 2).astype(F32)
    return [left, as_grid(x2), right]


def _row_shift(y3, a):
    if a == 1:
        return y3
    if y3.shape[0] == 1:
        return jnp.zeros_like(y3)
    zero = jnp.zeros_like(y3[:1])
    return jnp.concatenate([zero, y3[:-1]], axis=0) if a == 0 else jnp.concatenate([y3[1:], zero], axis=0)


def _conv_taps(cols, w_ref, flip):
    rows_g = cols[0].shape[0]
    acc = None
    for a in range(3):
        if rows_g == 1 and a != 1:
            continue
        inner = None
        for b in range(3):
            tap = (2 - a) * 3 + (2 - b) if flip else a * 3 + b
            term = cols[b] * w_ref[tap:tap + 1, :]
            inner = term if inner is None else inner + term
        inner = _row_shift(inner, a)
        acc = inner if acc is None else acc + inner
    return acc


def _conv_fwd(u, conv_w9, conv_b, t_rows, c_rows, w, ct):
    r = u.shape[0]
    base = 5 * w // ct

    def body(x_ref, w_ref, b_ref, o_ref):
        for r0, n, rows_g, width_g in _conv_parts(t_rows, c_rows):
            pre = _conv_taps(_col_shifts(x_ref[r0:r0 + n, :], rows_g, width_g), w_ref, False) + b_ref[...]
            o_ref[r0:r0 + n, :] = _silu(pre).reshape(n, ct)

    return pl.pallas_call(
        body, name="conv_fwd", grid=(2 * w // ct,),
        in_specs=[pl.BlockSpec((r, ct), lambda i: (0, base + i)), pl.BlockSpec((9, ct), lambda i: (0, i)),
                  pl.BlockSpec((1, ct), lambda i: (0, i))],
        out_specs=pl.BlockSpec((r, ct), lambda i: (0, i)),
        out_shape=jax.ShapeDtypeStruct((r, 2 * w), F32),
        compiler_params=_params(("parallel",)),
    )(u, conv_w9, conv_b)


def _conv_bwd(u, dqk_pair, conv_w9, conv_b, t_rows, c_rows, w, ct):
    r = u.shape[0]
    base = 5 * w // ct

    def body(x_ref, d1_ref, d2_ref, w_ref, b_ref, dx_ref, dw_ref, db_ref):
        dw = [jnp.zeros((1, ct), F32) for _ in range(9)]
        db = jnp.zeros((1, ct), F32)
        for r0, n, rows_g, width_g in _conv_parts(t_rows, c_rows):
            cols = _col_shifts(x_ref[r0:r0 + n, :], rows_g, width_g)
            pre = (_conv_taps(cols, w_ref, False) + b_ref[...]).reshape(n, ct)
            sg = _sigmoid(pre)
            dpre = (d1_ref[r0:r0 + n, :] + d2_ref[r0:r0 + n, :]) * (sg * (1.0 + pre * (1.0 - sg)))
            db = db + jnp.sum(dpre, axis=0, keepdims=True)
            dx_ref[r0:r0 + n, :] = _conv_taps(_col_shifts(dpre, rows_g, width_g), w_ref, True).reshape(n, ct)
            dpre3 = dpre.reshape(rows_g, width_g, ct)
            for a in range(3):
                if rows_g == 1 and a != 1:
                    continue
                moved = _row_shift(dpre3, 2 - a)
                for b in range(3):
                    prod = jnp.sum(cols[b] * moved, axis=0)
                    dw[a * 3 + b] = dw[a * 3 + b] + jnp.sum(prod, axis=0, keepdims=True)
        for tap in range(9):
            dw_ref[tap:tap + 1, :] = dw[tap]
        db_ref[...] = db

    return pl.pallas_call(
        body, name="conv_bwd", grid=(2 * w // ct,),
        in_specs=[pl.BlockSpec((r, ct), lambda i: (0, base + i)), pl.BlockSpec((r, ct), lambda i: (0, i)),
                  pl.BlockSpec((r, ct), lambda i: (0, i)),
                  pl.BlockSpec((9, ct), lambda i: (0, i)), pl.BlockSpec((1, ct), lambda i: (0, i))],
        out_specs=(pl.BlockSpec((r, ct), lambda i: (0, i)), pl.BlockSpec((9, ct), lambda i: (0, i)),
                   pl.BlockSpec((1, ct), lambda i: (0, i))),
        out_shape=(jax.ShapeDtypeStruct((r, 2 * w), F32), jax.ShapeDtypeStruct((9, 2 * w), F32),
                   jax.ShapeDtypeStruct((1, 2 * w), F32)),
        compiler_params=_params(("parallel",)),
    )(u, dqk_pair[0], dqk_pair[1], conv_w9, conv_b)


def _assemble_du(groups, gates, n_pad, tm):
    flat, layout = [], []
    for entry in list(groups) + [gates]:
        parts = entry if isinstance(entry, (tuple, list)) else (entry,)
        layout.append((len(flat), len(parts), parts[0].shape[1]))
        flat += list(parts)
    r = flat[0].shape[0]

    def body(*refs):
        o_ref = refs[-1]
        col = 0
        for first, count, width in layout:
            val = refs[first][...]
            for extra in range(1, count):
                val = val + refs[first + extra][...]
            o_ref[:, col:col + width] = val.astype(BF16)
            col += width
        assert col == n_pad

    return pl.pallas_call(
        body, name="assemble_du", grid=(r // tm,),
        in_specs=[pl.BlockSpec((tm, a.shape[1]), lambda i: (i, 0)) for a in flat],
        out_specs=pl.BlockSpec((tm, n_pad), lambda i: (i, 0)),
        out_shape=jax.ShapeDtypeStruct((r, n_pad), BF16),
        compiler_params=_params(("parallel",)),
    )(*flat)


def _scan_order(n_lat, n_ctx, rev):
    n = n_lat + n_ctx
    if rev:
        return lambda j: n - 1 - j
    return lambda j: (j + n_lat) % n


DIRS = (False, True)


def _hg_scan_fwd(u, lb_full, w, n_lat, n_ctx, chunk):
    r = u.shape[0]
    n_heads = w // HG_DK
    n_chunks = n_lat + n_ctx
    nat = [_scan_order(n_lat, n_ctx, rev) for rev in DIRS]

    def body(*refs):
        ins, outs, scratch = refs[:8], refs[8:12], refs[12:]

        @pl.when(pl.program_id(0) == 0)
        def _():
            for s_ref in scratch:
                s_ref[...] = jnp.zeros_like(s_ref)

        results = []
        for d, rev in enumerate(DIRS):
            aq, af, ai, lb_ref = ins[4 * d:4 * d + 4]
            state = [scratch[d][h] for h in range(n_heads)]
            results.append((state, _hg_chunk(state, aq[...], af[...], ai[...],
                                             lb_ref[0, 0:1, :], lb_ref[0, 1:2, :], rev)))
        for d, (state, (new, o)) in enumerate(results):
            o_ref, save_ref = outs[2 * d:2 * d + 2]
            o_ref[...] = o
            for h in range(n_heads):
                save_ref[0, h] = state[h]
                scratch[d][h] = new[h]

    in_specs, out_specs, out_shape = [], [], []
    for d in range(2):
        in_specs += [pl.BlockSpec((chunk,w), lambda j, d=d: (nat[d](j), 0)),
                     pl.BlockSpec((chunk,w), lambda j, d=d: (nat[d](j), 1 + d)),
                     pl.BlockSpec((chunk,w), lambda j, d=d: (nat[d](j), 3)),
                     pl.BlockSpec((1, 2, w), lambda j, d=d: (d, 0, 0))]
        out_specs += [pl.BlockSpec((chunk,w), lambda j, d=d: (nat[d](j), 0)),
                      pl.BlockSpec((1, n_heads, HG_DK, HG_DK), lambda j: (j, 0, 0, 0))]
        out_shape += [jax.ShapeDtypeStruct((r, w), F32),
                      jax.ShapeDtypeStruct((n_chunks, n_heads, HG_DK, HG_DK), F32)]
    o_f, s_f, o_b, s_b = pl.pallas_call(
        body, name="hg_scan_fwd", grid=(n_chunks,), in_specs=in_specs, out_specs=tuple(out_specs),
        out_shape=tuple(out_shape), scratch_shapes=[pltpu.VMEM((n_heads, HG_DK, HG_DK), F32)] * 2,
        compiler_params=_params(("arbitrary",)),
    )(u, u, u, lb_full, u, u, u, lb_full)
    return (o_f, o_b), (s_f, s_b)


def _hg_scan_bwd(u, lb_full, saved, d_o, w, n_lat, n_ctx, chunk):
    r = u.shape[0]
    n_heads = w // HG_DK
    n_chunks = n_lat + n_ctx
    step = lambda jj: n_chunks - 1 - jj
    nat = [(lambda jj, o=_scan_order(n_lat, n_ctx, rev): o(step(jj))) for rev in DIRS]

    def body(*refs):
        ins, outs, scratch = refs[:12], refs[12:20], refs[20:]
        jj = pl.program_id(0)

        @pl.when(jj == 0)
        def _():
            for d in range(2):
                scratch[d][...] = jnp.zeros_like(scratch[d])
                outs[4 * d + 3][...] = jnp.zeros_like(outs[4 * d + 3])

        results = []
        for d, rev in enumerate(DIRS):
            aq, af, ai, lb_ref, save_ref, do_ref = ins[6 * d:6 * d + 6]
            f = lambda st, a, b, c, l0, l1, rev=rev: _hg_chunk(st, a, b, c, l0, l1, rev)
            _, vjp = jax.vjp(f, [save_ref[0, h] for h in range(n_heads)], aq[...], af[...], ai[...],
                             lb_ref[0, 0:1, :], lb_ref[0, 1:2, :])
            d_out = do_ref[...] * (nat[d](jj) < n_lat).astype(F32)
            results.append(vjp(([scratch[d][h] for h in range(n_heads)], d_out)))
        for d, (dst, daq, daf, dai, dl0, dl1) in enumerate(results):
            daq_ref, daf_ref, dai_ref, dlb_ref = outs[4 * d:4 * d + 4]
            for h in range(n_heads):
                scratch[d][h] = dst[h]
            daq_ref[...] = daq
            daf_ref[...] = daf
            dai_ref[...] = dai
            dlb_ref[0:1, :] += dl0
            dlb_ref[1:2, :] += dl1

    in_specs, out_specs, out_shape, operands = [], [], [], []
    for d in range(2):
        row = lambda jj, d=d: (nat[d](jj), 0)
        in_specs += [pl.BlockSpec((chunk,w), row),
                     pl.BlockSpec((chunk,w), lambda jj, d=d: (nat[d](jj), 1 + d)),
                     pl.BlockSpec((chunk,w), lambda jj, d=d: (nat[d](jj), 3)),
                     pl.BlockSpec((1, 2, w), lambda jj, d=d: (d, 0, 0)),
                     pl.BlockSpec((1, n_heads, HG_DK, HG_DK), lambda jj: (step(jj), 0, 0, 0)),
                     pl.BlockSpec((chunk,w), lambda jj, d=d: (jnp.minimum(nat[d](jj), n_lat - 1), 0))]
        operands += [u, u, u, lb_full, saved[d], d_o]
        out_specs += [pl.BlockSpec((chunk,w), row)] * 3 + [pl.BlockSpec((2, w), lambda jj: (0, 0))]
        out_shape += [jax.ShapeDtypeStruct((r, w), F32)] * 3 + [jax.ShapeDtypeStruct((2, w), F32)]
    res = pl.pallas_call(
        body, name="hg_scan_bwd", grid=(n_chunks,), in_specs=in_specs, out_specs=tuple(out_specs),
        out_shape=tuple(out_shape), scratch_shapes=[pltpu.VMEM((n_heads, HG_DK, HG_DK), F32)] * 2,
        compiler_params=_params(("arbitrary",)),
    )(*operands)
    return res[0:4], res[4:8]


def _ml_state_shapes(n_chunks, n_heads, dh):
    return (jax.ShapeDtypeStruct((n_chunks, n_heads, dh, dh), F32),
            jax.ShapeDtypeStruct((n_chunks, n_heads, 1, dh), F32),
            jax.ShapeDtypeStruct((n_chunks, n_heads, 1, LANE), F32))


def _ml_state_specs(n_heads, dh, index):
    return (pl.BlockSpec((1, n_heads, dh, dh), lambda j: (index(j), 0, 0, 0)),
            pl.BlockSpec((1, n_heads, 1, dh), lambda j: (index(j), 0, 0, 0)),
            pl.BlockSpec((1, n_heads, 1, LANE), lambda j: (index(j), 0, 0, 0)))


def _ml_state_scratch(n_heads, dh):
    return [pltpu.VMEM((n_heads, dh, dh), F32), pltpu.VMEM((n_heads, 1, dh), F32), pltpu.VMEM((n_heads, 1, LANE), F32)]


def _ml_scan_fwd(qk, u, gate_b, w, n_heads, n_lat, n_ctx, chunk):
    r = u.shape[0]
    dh = w // n_heads
    n_chunks = n_lat + n_ctx
    nat = [_scan_order(n_lat, n_ctx, rev) for rev in DIRS]

    def body(*refs):
        ins, outs, scratch = refs[:10], refs[10:18], refs[18:]

        @pl.when(pl.program_id(0) == 0)
        def _():
            for s_ref in scratch:
                s_ref[...] = jnp.zeros_like(s_ref)

        results = []
        for d, rev in enumerate(DIRS):
            q, k, v, g, gb = ins[5 * d:5 * d + 5]
            state = tuple([ref[h] for h in range(n_heads)] for ref in scratch[3 * d:3 * d + 3])
            results.append((state, _ml_chunk(state, q[...], k[...], v[...], g[...], gb[...], rev, d)))
        for d, (state, (new, o)) in enumerate(results):
            outs[4 * d][...] = o
            for part in range(3):
                for h in range(n_heads):
                    outs[4 * d + 1 + part][0, h] = state[part][h]
                    scratch[3 * d + part][h] = new[part][h]

    in_specs, out_specs, out_shape = [], [], []
    for d in range(2):
        in_specs += [pl.BlockSpec((chunk,w), lambda j, d=d: (nat[d](j), 0)),
                     pl.BlockSpec((chunk,w), lambda j, d=d: (nat[d](j), 1)),
                     pl.BlockSpec((chunk,w), lambda j, d=d: (nat[d](j), 7)),
                     pl.BlockSpec((chunk,LANE), lambda j, d=d: (nat[d](j), 10 * w // LANE)),
                     pl.BlockSpec((1, LANE), lambda j: (0, 0))]
        out_specs += [pl.BlockSpec((chunk,w), lambda j, d=d: (nat[d](j), 0))]
        out_specs += list(_ml_state_specs(n_heads, dh, lambda j: j))
        out_shape += [jax.ShapeDtypeStruct((r, w), F32)] + list(_ml_state_shapes(n_chunks, n_heads, dh))
    res = pl.pallas_call(
        body, name="ml_scan_fwd", grid=(n_chunks,), in_specs=in_specs, out_specs=tuple(out_specs),
        out_shape=tuple(out_shape), scratch_shapes=_ml_state_scratch(n_heads, dh) * 2,
        compiler_params=_params(("arbitrary",)),
    )(qk, qk, u, u, gate_b, qk, qk, u, u, gate_b)
    return (res[0], res[4]), (res[1:4], res[5:8])


def _ml_scan_bwd(qk, u, gate_b, saved, d_h, w, n_heads, n_lat, n_ctx, chunk):
    r = u.shape[0]
    dh = w // n_heads
    n_chunks = n_lat + n_ctx
    step = lambda jj: n_chunks - 1 - jj
    nat = [(lambda jj, o=_scan_order(n_lat, n_ctx, rev): o(step(jj))) for rev in DIRS]

    def body(*refs):
        ins, outs, scratch = refs[:18], refs[18:26], refs[26:]
        jj = pl.program_id(0)

        @pl.when(jj == 0)
        def _():
            for s_ref in scratch:
                s_ref[...] = jnp.zeros_like(s_ref)
            for d in range(2):
                outs[4 * d + 3][...] = jnp.zeros_like(outs[4 * d + 3])

        results = []
        for d, rev in enumerate(DIRS):
            q, k, v, g, gb, sc, sn, sm, dh_ref = ins[9 * d:9 * d + 9]
            state = tuple([ref[0, h] for h in range(n_heads)] for ref in (sc, sn, sm))
            f = lambda st, a, b, c, gg, bb, rev=rev, d=d: _ml_chunk(st, a, b, c, gg, bb, rev, d)
            _, vjp = jax.vjp(f, state, q[...], k[...], v[...], g[...], gb[...])
            d_state = tuple([ref[h] for h in range(n_heads)] for ref in scratch[3 * d:3 * d + 3])
            d_out = dh_ref[...] * (nat[d](jj) < n_lat).astype(F32)
            results.append(vjp((d_state, d_out)))
        for d, (d_state, dq, dk, dv, dg, dgb) in enumerate(results):
            dqk_ref, dv_ref, dg_ref, dgb_ref = outs[4 * d:4 * d + 4]
            for part in range(3):
                for h in range(n_heads):
                    scratch[3 * d + part][h] = d_state[part][h]
            dqk_ref[:, 0:w] = dq
            dqk_ref[:, w:2 * w] = dk
            dv_ref[...] = dv
            dg_ref[...] = dg
            dgb_ref[...] += dgb

    in_specs, out_specs, out_shape, operands = [], [], [], []
    for d in range(2):
        row = lambda jj, d=d: (nat[d](jj), 0)
        in_specs += [pl.BlockSpec((chunk,w), row), pl.BlockSpec((chunk,w), lambda jj, d=d: (nat[d](jj), 1)),
                     pl.BlockSpec((chunk,w), lambda jj, d=d: (nat[d](jj), 7)),
                     pl.BlockSpec((chunk,LANE), lambda jj, d=d: (nat[d](jj), 10 * w // LANE)),
                     pl.BlockSpec((1, LANE), lambda jj: (0, 0))]
        in_specs += list(_ml_state_specs(n_heads, dh, step))
        in_specs += [pl.BlockSpec((chunk,w), lambda jj, d=d: (jnp.minimum(nat[d](jj), n_lat - 1), 0))]
        operands += [qk, qk, u, u, gate_b, *saved[d], d_h]
        out_specs += [pl.BlockSpec((chunk,2 * w), row), pl.BlockSpec((chunk,w), row),
                      pl.BlockSpec((chunk,LANE), row), pl.BlockSpec((1, LANE), lambda jj: (0, 0))]
        out_shape += [jax.ShapeDtypeStruct((r, 2 * w), F32), jax.ShapeDtypeStruct((r, w), F32),
                      jax.ShapeDtypeStruct((r, LANE), F32), jax.ShapeDtypeStruct((1, LANE), F32)]
    res = pl.pallas_call(
        body, name="ml_scan_bwd", grid=(n_chunks,), in_specs=in_specs, out_specs=tuple(out_specs),
        out_shape=tuple(out_shape), scratch_shapes=_ml_state_scratch(n_heads, dh) * 2,
        compiler_params=_params(("arbitrary",)),
    )(*operands)
    return res[0:4], res[4:8]


def _post_specs(w, tm, lat_tiles, cols):
    return [pl.BlockSpec((tm, w), (lambda i, cb=cb: (jnp.minimum(i, lat_tiles - 1), cb))) for cb in cols]


def _post_fwd(o_f, o_b, h_f, h_b, u, wa, wb, t_rows, w, n_hg, n_ml, tm):
    lat_tiles = t_rows // tm

    def body(of, ob, hf, hb, az, bo, bz, wa_ref, wb_ref, y_ref):
        y_ref[...] = _post_fn(of[...], ob[...], az[...], hf[...], hb[...], bo[...], bz[...],
                              wa_ref[...], wb_ref[...], n_hg, n_ml).astype(BF16)

    rows = pl.BlockSpec((tm, w), lambda i: (i, 0))
    vec = pl.BlockSpec((1, w), lambda i: (0, 0))
    return pl.pallas_call(
        body, name="post_fwd", grid=(lat_tiles,),
        in_specs=[rows] * 4 + _post_specs(w, tm, lat_tiles, (4, 8, 9)) + [vec, vec],
        out_specs=pl.BlockSpec((tm, 2 * w), lambda i: (i, 0)),
        out_shape=jax.ShapeDtypeStruct((t_rows, 2 * w), BF16),
        compiler_params=_params(("parallel",)),
    )(o_f, o_b, h_f, h_b, u, u, u, wa, wb)


def _post_bwd(o_f, o_b, h_f, h_b, u, wa, wb, dy, t_rows, w, n_hg, n_ml, tm):
    r = u.shape[0]
    lat_tiles = t_rows // tm
    lat = lambda i: (jnp.minimum(i, lat_tiles - 1), 0)

    def body(of, ob, hf, hb, az, bo, bz, wa_ref, wb_ref, dy_ref, do_ref, dh_ref, daz_ref, dbo_ref, dbz_ref,
             dwa_ref, dwb_ref):
        i = pl.program_id(0)

        @pl.when(i == 0)
        def _():
            dwa_ref[...] = jnp.zeros_like(dwa_ref)
            dwb_ref[...] = jnp.zeros_like(dwb_ref)

        @pl.when(i < lat_tiles)
        def _():
            f = functools.partial(_post_fn, n_hg=n_hg, n_ml=n_ml)
            _, vjp = jax.vjp(f, of[...], ob[...], az[...], hf[...], hb[...], bo[...], bz[...], wa_ref[...], wb_ref[...])
            d_of, _, d_az, d_hf, _, d_bo, d_bz, d_wa, d_wb = vjp(dy_ref[...])
            do_ref[...] = d_of
            dh_ref[...] = d_hf
            daz_ref[...] = d_az
            dbo_ref[...] = d_bo
            dbz_ref[...] = d_bz
            dwa_ref[...] += d_wa
            dwb_ref[...] += d_wb

        @pl.when(i >= lat_tiles)
        def _():
            daz_ref[...] = jnp.zeros_like(daz_ref)
            dbo_ref[...] = jnp.zeros_like(dbo_ref)
            dbz_ref[...] = jnp.zeros_like(dbz_ref)

    lat_rows = pl.BlockSpec((tm, w), lat)
    all_rows = pl.BlockSpec((tm, w), lambda i: (i, 0))
    vec = pl.BlockSpec((1, w), lambda i: (0, 0))
    sd_t = jax.ShapeDtypeStruct((t_rows, w), F32)
    sd_r = jax.ShapeDtypeStruct((r, w), F32)
    sd_v = jax.ShapeDtypeStruct((1, w), F32)
    return pl.pallas_call(
        body, name="post_bwd", grid=(r // tm,),
        in_specs=[lat_rows] * 4 + _post_specs(w, tm, lat_tiles, (4, 8, 9)) + [vec, vec]
        + [pl.BlockSpec((tm, 2 * w), lat)],
        out_specs=(lat_rows, lat_rows, all_rows, all_rows, all_rows, vec, vec),
        out_shape=(sd_t, sd_t, sd_r, sd_r, sd_r, sd_v, sd_v),
        compiler_params=_params(("arbitrary",)),
    )(o_f, o_b, h_f, h_b, u, u, u, wa, wb, dy)


OUT_ROW_GATE, OUT_ROW_LN_G, OUT_ROW_LN_B, OUT_ROW_LOSS = 0, 1, 2, 3


def _out_block(y, w_out, x, target, prm, tm):
    t_rows, dm = x.shape
    di = y.shape[1]

    def body(y_ref, w_ref, x_ref, t_ref, p_ref, dz_ref, dy_ref, gx_ref, acc_ref):
        @pl.when(pl.program_id(0) == 0)
        def _():
            acc_ref[...] = jnp.zeros_like(acc_ref)

        gate, ln_g, ln_b = p_ref[0:1, :], p_ref[1:2, :], p_ref[2:3, :]
        z = _nn(y_ref[...], w_ref[...])
        res = ALPHA * x_ref[...] + gate * z
        mu = jnp.mean(res, axis=-1, keepdims=True)
        rc = res - mu
        rstd = lax.rsqrt(jnp.mean(rc * rc, axis=-1, keepdims=True) + LN_EPS)
        rn = rc * rstd
        err = rn * ln_g + ln_b - t_ref[...]
        d_out = err * (1.0 / dm)
        d_rn = d_out * ln_g
        d_res = rstd * (d_rn - jnp.mean(d_rn, axis=-1, keepdims=True)
                        - rn * jnp.mean(d_rn * rn, axis=-1, keepdims=True))
        acc_ref[OUT_ROW_GATE:OUT_ROW_GATE + 1, :] += jnp.sum(d_res * z, axis=0, keepdims=True)
        acc_ref[OUT_ROW_LN_G:OUT_ROW_LN_G + 1, :] += jnp.sum(d_out * rn, axis=0, keepdims=True)
        acc_ref[OUT_ROW_LN_B:OUT_ROW_LN_B + 1, :] += jnp.sum(d_out, axis=0, keepdims=True)
        acc_ref[OUT_ROW_LOSS:OUT_ROW_LOSS + 1, :] += (0.5 / dm) * jnp.sum(err * err, axis=0, keepdims=True)
        gx_ref[...] = ALPHA * d_res
        dz = (d_res * gate).astype(BF16)
        dz_ref[...] = dz
        dy_ref[...] = _nt(dz, w_ref[...])

    rows_d = pl.BlockSpec((tm, dm), lambda i: (i, 0))
    rows_i = pl.BlockSpec((tm, di), lambda i: (i, 0))
    return pl.pallas_call(
        body, name="out_block", grid=(t_rows // tm,),
        in_specs=[rows_i, pl.BlockSpec((di, dm), lambda i: (0, 0)), rows_d, rows_d,
                  pl.BlockSpec((8, dm), lambda i: (0, 0))],
        out_specs=(rows_d, rows_i, rows_d, pl.BlockSpec((8, dm), lambda i: (0, 0))),
        out_shape=(jax.ShapeDtypeStruct((t_rows, dm), BF16), jax.ShapeDtypeStruct((t_rows, di), F32),
                   jax.ShapeDtypeStruct((t_rows, dm), F32), jax.ShapeDtypeStruct((8, dm), F32)),
        compiler_params=_params(("arbitrary",)),
    )(y, w_out, x, target, prm)


def _mod_fwd(c16, w_mod, tn):
    dm, n = w_mod.shape

    def body(c_ref, w_ref, o_ref, a_ref):
        a = _silu(c_ref[...])
        a_ref[...] = a
        o_ref[...] = _nn(a, w_ref[...], HIGHEST)

    return pl.pallas_call(
        body, name="mod_fwd", grid=(n // tn,),
        in_specs=[pl.BlockSpec((16, dm), lambda j: (0, 0)), pl.BlockSpec((dm, tn), lambda j: (0, j))],
        out_specs=(pl.BlockSpec((16, tn), lambda j: (0, j)), pl.BlockSpec((16, dm), lambda j: (0, 0))),
        out_shape=(jax.ShapeDtypeStruct((16, n), F32), jax.ShapeDtypeStruct((16, dm), F32)),
        compiler_params=_params(("arbitrary",)),
    )(c16, w_mod)


def _mod_bwd(a16, dm16, w_mod, tn):
    dm, n = w_mod.shape

    def body(a_ref, d_ref, w_ref, dw_ref, dc_ref):
        @pl.when(pl.program_id(0) == 0)
        def _():
            dc_ref[...] = jnp.zeros_like(dc_ref)
        dw_ref[...] = _tn(a_ref[...], d_ref[...], HIGHEST)
        dc_ref[...] += _nt(d_ref[...], w_ref[...], HIGHEST)

    return pl.pallas_call(
        body, name="mod_bwd", grid=(n // tn,),
        in_specs=[pl.BlockSpec((16, dm), lambda j: (0, 0)), pl.BlockSpec((16, tn), lambda j: (0, j)),
                  pl.BlockSpec((dm, tn), lambda j: (0, j))],
        out_specs=(pl.BlockSpec((dm, tn), lambda j: (0, j)), pl.BlockSpec((16, dm), lambda j: (0, 0))),
        out_shape=(jax.ShapeDtypeStruct((dm, n), F32), jax.ShapeDtypeStruct((16, dm), F32)),
        compiler_params=_params(("arbitrary",)),
    )(a16, dm16, w_mod)


def _sum_devices(g, fold_rows):
    n_dev, rows, n = g.shape

    def body(g_ref, s_ref, t_ref):
        s = g_ref[0]
        for dev in range(1, n_dev):
            s = s + g_ref[dev]
        t_ref[...] = jnp.broadcast_to(jnp.sum(s, axis=-1, keepdims=True), (rows, LANE))
        s_ref[...] = s
        s_ref[0:fold_rows, :] = s[0:fold_rows] + s[fold_rows:2 * fold_rows]

    return pl.pallas_call(
        body, name="sum_devices",
        out_shape=(jax.ShapeDtypeStruct((rows, n), F32), jax.ShapeDtypeStruct((rows, LANE), F32)),
        compiler_params=_params(),
    )(g)


def _c_ctx_grad(parts, c_ctx_row):
    def body(p_ref, c_ref, o_ref):
        s = p_ref[0]
        for chip in range(1, N_CHIPS):
            s = s + p_ref[2 * chip]
        cv = c_ref[...]
        sg = _sigmoid(cv)
        o_ref[...] = s * (sg * (1.0 + cv * (1.0 - sg)))

    return pl.pallas_call(
        body, name="c_ctx_grad", out_shape=jax.ShapeDtypeStruct(parts.shape[1:], F32), compiler_params=_params(),
    )(parts, c_ctx_row)


def _sum_pair(name, mine, got):
    def body(a_ref, b_ref, o_ref):
        o_ref[...] = (a_ref[...] + b_ref[...]).astype(BF16)

    k, rows, n = mine.shape
    tl = _largest_divisor(n, max(LANE, (1 << 18) // rows), LANE)
    spec = pl.BlockSpec((1, rows, tl), lambda kk, i: (kk, 0, i))
    return pl.pallas_call(
        body, name=name, grid=(k, n // tl), in_specs=[spec, spec], out_specs=spec,
        out_shape=jax.ShapeDtypeStruct(mine.shape, BF16), compiler_params=_params(("parallel", "parallel")),
    )(mine, got)


def _sum_chips(name, got):
    k, rows, n = got.shape
    tl = _largest_divisor(n, max(LANE, (1 << 18) // rows), LANE)

    def body(g_ref, o_ref):
        total = g_ref[0].astype(F32)
        for kk in range(1, k):
            total = total + g_ref[kk].astype(F32)
        o_ref[...] = total

    return pl.pallas_call(
        body, name=name, grid=(n // tl,),
        in_specs=[pl.BlockSpec((k, rows, tl), lambda i: (0, 0, i))], out_specs=pl.BlockSpec((rows, tl), lambda i: (0, i)),
        out_shape=jax.ShapeDtypeStruct((rows, n), F32), compiler_params=_params(("parallel",)),
    )(got)


def _adamw_update(w, g, m, v):
    m2 = ADAM_B1 * m + (1.0 - ADAM_B1) * g
    v2 = ADAM_B2 * v + (1.0 - ADAM_B2) * jnp.square(g)
    m_hat = m2 / (1.0 - ADAM_B1 ** ADAM_STEP)
    v_hat = v2 / (1.0 - ADAM_B2 ** ADAM_STEP)
    return -ADAM_LR * (m_hat / (jnp.sqrt(v_hat) + ADAM_EPS) + ADAM_WD * w), m2, v2


def _adamw(name, w, g, m, v):
    rows, n = w.shape
    tr = rows if rows % 8 else _largest_divisor(rows, max(8, (1 << 18) // n), 8)

    def body(w_ref, g_ref, m_ref, v_ref, d_ref, mo_ref, vo_ref):
        d_ref[...], mo_ref[...], vo_ref[...] = _adamw_update(w_ref[...], g_ref[...], m_ref[...], v_ref[...])

    spec = pl.BlockSpec((tr, n), lambda i: (i, 0))
    sds = jax.ShapeDtypeStruct((rows, n), F32)
    return pl.pallas_call(
        body, name=name, grid=(rows // tr,), in_specs=[spec] * 4, out_specs=(spec,) * 3,
        out_shape=(sds, sds, sds), compiler_params=_params(("parallel",)),
    )(w, g, m, v)


FLAT_TILE = 8 * LANE


def _adamw_flat(name, w, g, m, v):
    n = w.shape[0]
    tiles = n // FLAT_TILE
    assert tiles * FLAT_TILE == n
    per_block = _largest_divisor(tiles, (3 << 20) // (4 * FLAT_TILE))

    def body(w_ref, g_ref, m_ref, v_ref, d_ref, mo_ref, vo_ref):
        def one_tile(i, carry):
            sl = pl.ds(pl.multiple_of(i * FLAT_TILE, FLAT_TILE), FLAT_TILE)
            d_ref[sl], mo_ref[sl], vo_ref[sl] = _adamw_update(w_ref[sl], g_ref[sl], m_ref[sl], v_ref[sl])
            return carry
        lax.fori_loop(0, per_block, one_tile, 0)

    spec = pl.BlockSpec((per_block * FLAT_TILE,), lambda i: (i,))
    sds = jax.ShapeDtypeStruct((n,), F32)
    return pl.pallas_call(
        body, name=name, grid=(tiles // per_block,), in_specs=[spec] * 4, out_specs=(spec,) * 3,
        out_shape=(sds, sds, sds), compiler_params=_params(("parallel",)),
    )(w, g, m, v)


PACK_LANES = 1024


def _pack(pieces):
    flat = jnp.concatenate([p.reshape(-1) for p in pieces])
    total = -(-flat.shape[0] // (8 * PACK_LANES)) * 8 * PACK_LANES
    return jnp.pad(flat, (0, total - flat.shape[0])).reshape(-1, PACK_LANES)


def _unpack(packed, shapes):
    flat = packed.reshape(-1)
    out, off = [], 0
    for shp in shapes:
        size = math.prod(shp)
        out.append(flat[off:off + size].reshape(shp))
        off += size
    return out


def _rows8(rows, width):
    flat = [r.reshape(width) for r in rows] + [jnp.zeros(((8 - len(rows)) * width,), F32)]
    return jnp.concatenate(flat).reshape(8, width)


def _reduce_scatter(tag, chip, ci, mine, other, axis):
    got = _sibling_swap("rs_pair_" + tag, [other])[0]
    pair = _sum_pair("rs_pair_sum_" + tag, mine, got)
    landed = _own_block(chip, pair, _chip_scatter("rs_chips_" + tag, [pair])[0])
    half = _sum_chips("rs_chip_sum_" + tag, landed)
    return _join_halves(ci, half, _sibling_swap("rs_join_" + tag, [half])[0], axis)


def kernel(x, c, ctx, c_ctx, w_mod, b_mod, w_in, conv_w, conv_b, hg_lb, ml_gate_b, hg_norm_w, ml_norm_w, w_out, ln_g, ln_b, loss_target, m_c_ctx, m_w_mod, m_b_mod, m_w_in, m_conv_w, m_conv_b, m_hg_lb, m_ml_gate_b, m_hg_norm_w, m_ml_norm_w, m_w_out, m_ln_g, m_ln_b, v_c_ctx, v_w_mod, v_b_mod, v_w_in, v_conv_w, v_conv_b, v_hg_lb, v_ml_gate_b, v_hg_norm_w, v_ml_norm_w, v_w_out, v_ln_g, v_ln_b):
    t_rows, dm = x.shape[1], x.shape[2]
    c_rows = ctx.shape[1]
    w = hg_norm_w.shape[1]
    n_ml = ml_gate_b.shape[-1]
    n_hg = w // HG_DK
    di = 2 * w
    n_in = 10 * w + 4 * n_ml
    ns = w_in.shape[2]
    nm = w_mod.shape[2]
    n_pad = 10 * w + LANE
    r_rows = t_rows + c_rows
    row_gcd = math.gcd(t_rows, c_rows)
    hg_chunk, ml_chunk = math.gcd(HG_CHUNK, row_gcd), math.gcd(ML_CHUNK, row_gcd)
    hg_counts = (t_rows // hg_chunk, c_rows // hg_chunk, hg_chunk)
    ml_counts = (t_rows // ml_chunk, c_rows // ml_chunk, ml_chunk)
    assert ml_norm_w.shape[1] == w and di == dm and N_CHIPS * ns == n_in and N_CHIPS * nm == 3 * dm
    assert w_out.shape[1] * N_CHIPS == di and 4 * n_ml <= LANE and t_rows % GRID_W == 0

    xi, yi, ci = lax.axis_index("x"), lax.axis_index("y"), lax.axis_index("c")
    chip = 2 * xi + yi
    dev = 4 * xi + 2 * yi + ci

    tm = _largest_divisor(math.gcd(t_rows, c_rows), 256, 8)
    tm_mm = _largest_divisor(r_rows, 1088, SUBLANE_BF16)
    tn_mm = LANE * _largest_divisor(n_pad // LANE, 9)
    tn_mod = _largest_divisor(nm, 512, LANE)

    shard_shapes = [(dm,), (2, 2, w // N_CHIPS), (3, 3, di // N_CHIPS)]
    g1 = _all_gather8("gather_inputs", _pack([c, hg_lb, conv_w]))
    per_dev = [_unpack(g1[i], shard_shapes) for i in range(N_DEV)]
    c_all = jnp.stack([p[0] for p in per_dev])
    lb_full = jnp.concatenate([per_dev[2 * k][1] for k in range(N_CHIPS)], axis=-1)
    conv_w9 = jnp.concatenate([per_dev[2 * k][2] for k in range(N_CHIPS)], axis=-1).reshape(9, di)

    c16 = jnp.concatenate([c_all, c_ctx[None], jnp.zeros((16 - N_DEV - 1, dm), F32)])
    mod_part, a16 = _mod_fwd(c16, w_mod[0], tn_mod)
    g2 = _all_gather8("gather_mod", mod_part)
    mod_all = jnp.concatenate([g2[2 * k] for k in range(N_CHIPS)], axis=1) + b_mod
    mod_x = lax.dynamic_index_in_dim(mod_all, dev, 0, keepdims=False).reshape(3, dm)
    mod_c = mod_all[N_DEV].reshape(3, dm)
    prm = jnp.stack([_rows8(list(mod_x), dm), _rows8(list(mod_c), dm)])

    as_t = lambda a: jnp.transpose(a[0])
    halves = [lax.dynamic_slice_in_dim(as_t(w_in).astype(BF16), ci * (dm // 2), dm // 2, 1),
              lax.dynamic_slice_in_dim(w_out[0].astype(BF16), ci * (di // (2 * N_CHIPS)), di // (2 * N_CHIPS), 0)]
    fetched = [_own_block(chip, own, got)
               for own, got in zip(halves, _all_gather_chips("gather_weights", halves))]
    swapped = _sibling_swap("gather_weights_pair", fetched)
    gw_in = _join_halves(ci, fetched[0], swapped[0], 2)
    gw_out = _join_halves(ci, fetched[1], swapped[1], 1)
    wt_full = jnp.concatenate([gw_in.reshape(n_in, dm), jnp.zeros((n_pad - n_in, dm), BF16)])
    w_out_full = gw_out.reshape(di, dm)

    xc = jnp.concatenate([x[0], ctx[0]])
    hc = _modulate_fwd(xc, prm, t_rows, tm)
    u = _mm_nt("in_proj", hc, wt_full, tm_mm, tn_mm, F32)
    (o_f, o_b), hg_saved = _hg_scan_fwd(u, lb_full, w, *hg_counts)
    qk = _conv_fwd(u, conv_w9, conv_b, t_rows, c_rows, w, LANE)
    gate_b_row = jnp.pad(ml_gate_b.reshape(1, -1), ((0, 0), (0, LANE - 4 * n_ml)))
    (h_f, h_b), ml_saved = _ml_scan_fwd(qk, u, gate_b_row, w, n_ml, *ml_counts)
    y = _post_fwd(o_f, o_b, h_f, h_b, u, hg_norm_w, ml_norm_w, t_rows, w, n_hg, n_ml, tm)
    prm_out = _rows8([mod_x[2], ln_g, ln_b], dm)
    dz, dy, gx_direct, acc_out = _out_block(y, w_out_full, x[0], loss_target[0], prm_out, tm // 2)

    d_w_out = _mm_tn("d_w_out", y, dz, _largest_divisor(di, 1024, LANE),
                     _largest_divisor(t_rows, 1024, SUBLANE_BF16))
    d_o, d_h, d_az, d_bo, d_bz, d_wa, d_wb = _post_bwd(o_f, o_b, h_f, h_b, u, hg_norm_w, ml_norm_w, dy,
                                                        t_rows, w, n_hg, n_ml, tm)
    (d_aq_f, d_aff, d_ai_f, d_lb_f), (d_aq_b, d_afb, d_ai_b, d_lb_b) = _hg_scan_bwd(
        u, lb_full, hg_saved, d_o, w, *hg_counts)
    (d_qk_f, d_v_f, d_g_f, d_gb_f), (d_qk_b, d_v_b, d_g_b, d_gb_b) = _ml_scan_bwd(
        qk, u, gate_b_row, ml_saved, d_h, w, n_ml, *ml_counts)
    d_bqk, d_cw, d_cb = _conv_bwd(u, (d_qk_f, d_qk_b), conv_w9, conv_b, t_rows, c_rows, w, LANE)
    du = _assemble_du([(d_aq_f, d_aq_b), d_aff, d_afb, (d_ai_f, d_ai_b), d_az, d_bqk, (d_v_f, d_v_b), d_bo, d_bz],
                      (d_g_f, d_g_b), n_pad, tm // 2)
    d_wt_in = _mm_tn("d_w_in", du, hc, tn_mm, tm_mm)
    d_hc = _mm_acc("d_h", du, wt_full, tm_mm, tn_mm)
    gx_all, acc_mod = _modulate_bwd(xc, d_hc, prm, gx_direct, t_rows, tm)
    grad_x = gx_all[:t_rows][None]

    half_in = dm // 2
    mine_in = lax.dynamic_slice_in_dim(d_wt_in, ci * half_in, half_in, 1)
    other_in = lax.dynamic_slice_in_dim(d_wt_in, (1 - ci) * half_in, half_in, 1)
    pieces_in = lambda a: jnp.stack([a[k * ns:(k + 1) * ns] for k in range(N_CHIPS)])
    g_wt_in = _reduce_scatter("w_in", chip, ci, pieces_in(mine_in), pieces_in(other_in), 1)
    d_w_out4 = d_w_out.reshape(N_CHIPS, 2, di // (2 * N_CHIPS), dm)
    g_w_out = _reduce_scatter("w_out", chip, ci, lax.dynamic_index_in_dim(d_w_out4, ci, 1, keepdims=False),
                              lax.dynamic_index_in_dim(d_w_out4, 1 - ci, 1, keepdims=False), 0)

    zero_row = jnp.zeros((dm,), F32)
    d_gb = jnp.concatenate([d_gb_f[:, 0:n_ml], d_gb_b[:, n_ml:2 * n_ml], d_gb_f[:, 2 * n_ml:3 * n_ml],
                            d_gb_b[:, 3 * n_ml:4 * n_ml], jnp.zeros((1, dm - 4 * n_ml), F32)], axis=1)
    rows = [acc_mod[0, 0], acc_mod[0, 1], acc_out[OUT_ROW_GATE],
            acc_mod[1, 0], acc_mod[1, 1], zero_row]
    rows += list(d_cw) + [d_cb[0], d_lb_f.reshape(dm), d_lb_b.reshape(dm),
                          jnp.concatenate([d_wa[0], d_wb[0]]), acc_out[OUT_ROW_LN_G], acc_out[OUT_ROW_LN_B],
                          acc_out[OUT_ROW_LOSS], d_gb[0], zero_row]
    ROW_CW, ROW_CB, ROW_LB, ROW_NORM, ROW_LN_G, ROW_LN_B, ROW_LOSS, ROW_GB = 6, 15, 16, 18, 19, 20, 21, 22
    g3 = _all_gather8("gather_small_grads", jnp.concatenate([r.reshape(dm) for r in rows]).reshape(len(rows), dm))
    sums, totals = _sum_devices(g3, 3)
    loss = totals[ROW_LOSS, 0]
    dm16 = jnp.concatenate([g3[:, 0:3, :].reshape(N_DEV, 3 * dm), sums[3:6].reshape(1, 3 * dm),
                            jnp.zeros((16 - N_DEV - 1, 3 * dm), F32)])
    g_w_mod, dc16 = _mod_bwd(a16, lax.dynamic_slice_in_dim(dm16, chip * nm, nm, 1), w_mod[0], tn_mod)
    g4 = _all_gather8("gather_c_ctx", jnp.pad(dc16[N_DEV:N_DEV + 1], ((0, 7), (0, 0))))
    g_c_ctx = _c_ctx_grad(g4, jnp.broadcast_to(c_ctx[None], (8, dm)))[0]

    chip_cols = lambda a, width: lax.dynamic_slice_in_dim(a, chip * width, width, a.ndim - 1)
    grads = {
        "c_ctx": g_c_ctx,
        "w_mod": g_w_mod[None],
        "b_mod": sums[0:3].reshape(1, 3 * dm),
        "w_in": jnp.transpose(g_wt_in.reshape(-1).reshape(ns, dm))[None],
        "conv_w": chip_cols(sums[ROW_CW:ROW_CW + 9].reshape(1, 3, 3, di), di // N_CHIPS),
        "conv_b": sums[ROW_CB][None],
        "hg_lb": chip_cols(sums[ROW_LB:ROW_LB + 2].reshape(2, 2, w), w // N_CHIPS),
        "ml_gate_b": sums[ROW_GB, 0:4 * n_ml].reshape(1, 4, n_ml),
        "hg_norm_w": sums[ROW_NORM, 0:w][None],
        "ml_norm_w": sums[ROW_NORM, w:2 * w][None],
        "w_out": g_w_out[None],
        "ln_g": sums[ROW_LN_G][None],
        "ln_b": sums[ROW_LN_B][None],
    }
    weights = dict(c_ctx=c_ctx, w_mod=w_mod, b_mod=b_mod, w_in=w_in, conv_w=conv_w, conv_b=conv_b, hg_lb=hg_lb,
                   ml_gate_b=ml_gate_b, hg_norm_w=hg_norm_w, ml_norm_w=ml_norm_w, w_out=w_out, ln_g=ln_g, ln_b=ln_b)
    mom1 = dict(c_ctx=m_c_ctx, w_mod=m_w_mod, b_mod=m_b_mod, w_in=m_w_in, conv_w=m_conv_w, conv_b=m_conv_b,
                hg_lb=m_hg_lb, ml_gate_b=m_ml_gate_b, hg_norm_w=m_hg_norm_w, ml_norm_w=m_ml_norm_w, w_out=m_w_out,
                ln_g=m_ln_g, ln_b=m_ln_b)
    mom2 = dict(c_ctx=v_c_ctx, w_mod=v_w_mod, b_mod=v_b_mod, w_in=v_w_in, conv_w=v_conv_w, conv_b=v_conv_b,
                hg_lb=v_hg_lb, ml_gate_b=v_ml_gate_b, hg_norm_w=v_hg_norm_w, ml_norm_w=v_ml_norm_w, w_out=v_w_out,
                ln_g=v_ln_g, ln_b=v_ln_b)
    names = list(weights)
    big = ("w_mod", "w_in", "w_out")
    small = [n for n in names if n not in big]

    delta, new_m, new_v = {}, {}, {}
    for n in ("w_mod", "w_out"):
        as2d = lambda a: a.reshape(a.shape[-2], a.shape[-1])
        res = _adamw("adamw_" + n, as2d(weights[n]), as2d(grads[n]), as2d(mom1[n]), as2d(mom2[n]))
        delta[n], new_m[n], new_v[n] = (a.reshape(weights[n].shape) for a in res)
    flat_t = lambda a: as_t(a).reshape(-1)
    res = _adamw_flat("adamw_w_in", flat_t(w_in), g_wt_in.reshape(-1), flat_t(m_w_in), flat_t(v_w_in))
    delta["w_in"], new_m["w_in"], new_v["w_in"] = (jnp.transpose(a.reshape(ns, dm))[None] for a in res)
    small_shapes = [weights[n].shape for n in small]
    res = _adamw("adamw_small", *(_pack([src[n] for n in small]) for src in (weights, grads, mom1, mom2)))
    for out, packed in zip((delta, new_m, new_v), res):
        for n, a in zip(small, _unpack(packed, small_shapes)):
            out[n] = a

    return (loss, grad_x, *[grads[n].reshape(weights[n].shape) for n in names], *[delta[n] for n in names],
            *[new_m[n] for n in names], *[new_v[n] for n in names])
```

```python
import functools
import math

import jax
import jax.numpy as jnp
from jax import lax
from jax.experimental import pallas as pl
from jax.experimental.pallas import tpu as pltpu

F32 = jnp.float32
BF16 = jnp.bfloat16
HIGHEST = lax.Precision.HIGHEST
MESH = pl.DeviceIdType.MESH

HG_CHUNK = 64
ML_CHUNK = 256
GRID_W = 64
HG_DK = 128
LANE = 128
SUBLANE_BF16 = 16
ALPHA = 2.0 ** 0.25
LN_EPS = 1e-5
NORM_EPS = 1e-6
ADAM_LR = 0.001
ADAM_B1 = 0.9
ADAM_B2 = 0.999
ADAM_EPS = 1e-08
ADAM_WD = 0.01
ADAM_STEP = 10
VMEM_LIMIT = 56 * 1024 * 1024
N_CHIPS = 4
N_DEV = 8


def _params(sem=None):
    return pltpu.CompilerParams(dimension_semantics=sem, vmem_limit_bytes=VMEM_LIMIT)


def _largest_divisor(n, cap, multiple=1):
    best = None
    for d in range(multiple, min(n, cap) + 1, multiple):
        if n % d == 0:
            best = d
    assert best is not None, (n, cap, multiple)
    return best


def _sigmoid(x):
    return jax.nn.sigmoid(x)


def _silu(x):
    return x * jax.nn.sigmoid(x)


def _dot(a, b, dims, precision=None):
    return lax.dot_general(a, b, (dims, ((), ())), precision=precision, preferred_element_type=F32)


def _nn(a, b, precision=None):
    return _dot(a, b, ((1,), (0,)), precision)


def _nt(a, b, precision=None):
    return _dot(a, b, ((1,), (1,)), precision)


def _tn(a, b, precision=None):
    return _dot(a, b, ((0,), (0,)), precision)


def _visible(n, rev):
    r = lax.broadcasted_iota(jnp.int32, (n, n), 0)
    c = lax.broadcasted_iota(jnp.int32, (n, n), 1)
    return (r <= c) if rev else (r >= c)


def _hg_chunk(states, aq, af, ai, lb0, lb1, rev):
    n_heads = len(states)
    lb = _sigmoid(lb0 - lb1)
    f = lb + (1.0 - lb) * _sigmoid(af)
    g = jnp.log(f)
    k = 1.0 - f
    q = _silu(aq)
    chunk = aq.shape[0]
    vis = _visible(chunk, rev)
    b = _nn(vis.astype(F32), g, HIGHEST)
    last = 0 if rev else chunk - 1
    b_end = b[last:last + 1]
    b_mid = b[chunk // 2:chunk // 2 + 1]
    q_inter = q * jnp.exp(b)
    q_intra = q * jnp.exp(b - b_mid)
    k_intra = k * jnp.exp(b_mid - b)
    k_dec = k * jnp.exp(b_end - b)
    e_end = jnp.exp(b_end)
    new_states, outs = [], []
    for h in range(n_heads):
        sl = slice(h * HG_DK, (h + 1) * HG_DK)
        s_t = states[h]
        scores = jnp.where(vis, _nt(q_intra[:, sl], k_intra[:, sl]), 0.0)
        outs.append(_nt(q_inter[:, sl], s_t) + _nn(scores, ai[:, sl]))
        new_states.append(e_end[:, sl] * s_t + _tn(ai[:, sl], k_dec[:, sl]))
    return new_states, jnp.concatenate(outs, axis=1)


def _ml_chunk(state, q, k, v, g, gb, rev, d):
    cms, nvs, mbs = state
    n_heads = len(cms)
    dh = q.shape[1] // n_heads
    ga = g + gb
    log_f_all = jax.nn.log_sigmoid(ga)
    chunk = q.shape[0]
    vis = _visible(chunk, rev)
    b_all = _nn(vis.astype(F32), log_f_all, HIGHEST)
    last = 0 if rev else chunk - 1
    k = k * (dh ** -0.5)
    new_c, new_n, new_m, outs = [], [], [], []
    for h in range(n_heads):
        ci = d * n_heads + h
        cf = (2 + d) * n_heads + h
        sl = slice(h * dh, (h + 1) * dh)
        qh, kh, vh = q[:, sl], k[:, sl], v[:, sl]
        li = ga[:, ci:ci + 1]
        b = b_all[:, cf:cf + 1]
        m = mbs[h][:, 0:1]
        row = jnp.transpose(li - b)
        log_w = jnp.where(vis, b + row, -jnp.inf)
        m_inter = b + m
        m_t = jnp.maximum(m_inter, jnp.max(log_w, axis=-1, keepdims=True))
        w_inter = jnp.exp(m_inter - m_t)
        w_qk = jnp.exp(log_w - m_t) * _nt(qh, kh)
        num = w_inter * _nt(qh, cms[h]) + _nn(w_qk, vh)
        den = w_inter * jnp.sum(qh * nvs[h], axis=-1, keepdims=True) + jnp.sum(w_qk, axis=-1, keepdims=True)
        outs.append(num / jnp.maximum(jnp.abs(den), jnp.exp(-m_t)))
        m_new = m_t[last:last + 1]
        b_end = b[last:last + 1]
        w_s = jnp.exp(b_end - b + li - m_new)
        decay = jnp.exp(b_end + m - m_new)
        new_c.append(decay * cms[h] + _tn(w_s * vh, kh))
        new_n.append(decay * nvs[h] + jnp.sum(w_s * kh, axis=0, keepdims=True))
        new_m.append(jnp.broadcast_to(m_new, (1, LANE)))
    return (new_c, new_n, new_m), jnp.concatenate(outs, axis=1)


def _post_fn(o_f, o_b, az, h_f, h_b, bo, bz, wa, wb, n_hg, n_ml):
    o = o_f + o_b
    parts = []
    for h in range(n_hg):
        s = o[:, h * HG_DK:(h + 1) * HG_DK]
        parts.append(s * lax.rsqrt(jnp.mean(s * s, axis=-1, keepdims=True) + NORM_EPS))
    y_a = jnp.concatenate(parts, axis=1) * wa * _silu(az)
    hh = h_f + h_b
    dh = hh.shape[1] // n_ml
    parts = []
    for h in range(n_ml):
        s = hh[:, h * dh:(h + 1) * dh]
        mu = jnp.mean(s, axis=-1, keepdims=True)
        sc = s - mu
        parts.append(sc * lax.rsqrt(jnp.mean(sc * sc, axis=-1, keepdims=True) + NORM_EPS))
    y_b = jnp.concatenate(parts, axis=1) * wb * _sigmoid(bo) * _silu(bz)
    return jnp.concatenate([y_a, y_b], axis=1)


def _chip_of(dev):
    return 2 * dev[0] + dev[1]


def _index_of(dev):
    return 4 * dev[0] + 2 * dev[1] + dev[2]


def _exchange(name, srcs, out_shapes, transfers, local_copies=()):
    n_in, n_out, n_t, n_l = len(srcs), len(out_shapes), len(transfers), len(local_copies)

    def body(*refs):
        ins, outs = refs[:n_in], refs[n_in:n_in + n_out]
        send_sems, recv_sems, local_sems = refs[n_in + n_out:]
        me = (lax.axis_index("x"), lax.axis_index("y"), lax.axis_index("c"))

        def pick(ref, fn, *who):
            return ref if fn is None else ref.at[fn(*who)]

        sends, recvs, locs = [], [], []
        for t, (mask, si, sfn, di, dfn) in enumerate(transfers):
            peer = tuple(1 - p if flip else p for p, flip in zip(me, mask))
            sends.append(pltpu.make_async_remote_copy(
                src_ref=pick(ins[si], sfn, me, peer), dst_ref=pick(outs[di], dfn, me, peer),
                send_sem=send_sems.at[t], recv_sem=recv_sems.at[t], device_id=peer, device_id_type=MESH))
            landing = pick(outs[di], dfn, peer, me)
            recvs.append(pltpu.make_async_remote_copy(
                src_ref=landing, dst_ref=landing,
                send_sem=send_sems.at[t], recv_sem=recv_sems.at[t], device_id=peer, device_id_type=MESH))
        for l, (si, sfn, di, dfn) in enumerate(local_copies):
            locs.append(pltpu.make_async_copy(pick(ins[si], sfn, me), pick(outs[di], dfn, me), local_sems.at[l]))
        for cp in locs + sends:
            cp.start()
        for cp in recvs:
            cp.wait_recv()
        for cp in sends:
            cp.wait_send()
        for cp in locs:
            cp.wait()

    hbm = pl.BlockSpec(memory_space=pltpu.HBM)
    return pl.pallas_call(
        body, name=name, out_shape=tuple(out_shapes),
        in_specs=[hbm] * n_in, out_specs=tuple([hbm] * n_out),
        scratch_shapes=[pltpu.SemaphoreType.DMA((n_t,)), pltpu.SemaphoreType.DMA((n_t,)),
                        pltpu.SemaphoreType.DMA((max(n_l, 1),))],
    )(*srcs)


ALL_MASKS = [(mx, my, mc) for mx in (0, 1) for my in (0, 1) for mc in (0, 1)][1:]
CHIP_MASKS = [(1, 0, 0), (0, 1, 0), (1, 1, 0)]
SIBLING_MASK = (0, 0, 1)


def _all_gather8(name, v):
    out = jax.ShapeDtypeStruct((N_DEV,) + v.shape, v.dtype)
    slot = lambda sender, receiver: _index_of(sender)
    transfers = [(mask, 0, None, 0, slot) for mask in ALL_MASKS]
    return _exchange(name, [v], [out], transfers, [(0, None, 0, lambda me: _index_of(me))])[0]


def _all_gather_chips(name, arrays):
    outs = [jax.ShapeDtypeStruct((N_CHIPS,) + a.shape, a.dtype) for a in arrays]
    slot = lambda sender, receiver: _chip_of(sender)
    transfers = [(mask, i, None, i, slot) for i in range(len(arrays)) for mask in CHIP_MASKS]
    return _exchange(name, arrays, outs, transfers)


def _sibling_swap(name, arrays):
    outs = [jax.ShapeDtypeStruct(a.shape, a.dtype) for a in arrays]
    return _exchange(name, arrays, outs, [(SIBLING_MASK, i, None, i, None) for i in range(len(arrays))])


def _chip_scatter(name, arrays):
    outs = [jax.ShapeDtypeStruct(a.shape, a.dtype) for a in arrays]
    transfers = [(mask, i, lambda s, r: _chip_of(r), i, lambda s, r: _chip_of(s))
                 for i in range(len(arrays)) for mask in CHIP_MASKS]
    return _exchange(name, arrays, outs, transfers)


def _own_block(chip, own, blocks):
    sel = (lax.broadcasted_iota(jnp.int32, (N_CHIPS,) + (1,) * (blocks.ndim - 1), 0) == chip)
    return jnp.where(sel, own if own.ndim == blocks.ndim else own[None], blocks)


def _join_halves(ci, mine, other, axis):
    return jnp.where(ci == 0, jnp.concatenate([mine, other], axis=axis), jnp.concatenate([other, mine], axis=axis))


def _mm_nt(name, a, b, tm, tn, out_dtype):
    m, k = a.shape
    n = b.shape[0]

    def body(a_ref, b_ref, o_ref):
        o_ref[...] = _nt(a_ref[...], b_ref[...]).astype(out_dtype)

    return pl.pallas_call(
        body, name=name, grid=(n // tn, m // tm),
        in_specs=[pl.BlockSpec((tm, k), lambda j, i: (i, 0)), pl.BlockSpec((tn, k), lambda j, i: (j, 0))],
        out_specs=pl.BlockSpec((tm, tn), lambda j, i: (i, j)),
        out_shape=jax.ShapeDtypeStruct((m, n), out_dtype),
        compiler_params=_params(("parallel", "parallel")),
    )(a, b)


def _mm_acc(name, a, b, tm, tk):
    m, kc = a.shape
    n = b.shape[1]

    def body(a_ref, b_ref, o_ref):
        @pl.when(pl.program_id(1) == 0)
        def _():
            o_ref[...] = jnp.zeros_like(o_ref)
        o_ref[...] += _nn(a_ref[...], b_ref[...])

    return pl.pallas_call(
        body, name=name, grid=(m // tm, kc // tk),
        in_specs=[pl.BlockSpec((tm, tk), lambda i, kk: (i, kk)), pl.BlockSpec((tk, n), lambda i, kk: (kk, 0))],
        out_specs=pl.BlockSpec((tm, n), lambda i, kk: (i, 0)),
        out_shape=jax.ShapeDtypeStruct((m, n), F32),
        compiler_params=_params(("parallel", "arbitrary")),
    )(a, b)


def _mm_tn(name, a, b, tm, tk):
    kr, m = a.shape
    n = b.shape[1]

    def body(a_ref, b_ref, o_ref):
        @pl.when(pl.program_id(1) == 0)
        def _():
            o_ref[...] = jnp.zeros_like(o_ref)
        o_ref[...] += _tn(a_ref[...], b_ref[...])

    return pl.pallas_call(
        body, name=name, grid=(m // tm, kr // tk),
        in_specs=[pl.BlockSpec((tk, tm), lambda i, kk: (kk, i)), pl.BlockSpec((tk, n), lambda i, kk: (kk, 0))],
        out_specs=pl.BlockSpec((tm, n), lambda i, kk: (i, 0)),
        out_shape=jax.ShapeDtypeStruct((m, n), F32),
        compiler_params=_params(("parallel", "arbitrary")),
    )(a, b)


def _modulate_fwd(x, ctx, prm, tm):
    t_rows, dm = x.shape
    lat = t_rows // tm
    r = t_rows + ctx.shape[0]

    def body(x_ref, c_ref, p_ref, h_ref):
        xv = jnp.where(pl.program_id(0) >= lat, c_ref[...], x_ref[...])
        mu = jnp.mean(xv, axis=-1, keepdims=True)
        xm = xv - mu
        n = xm * lax.rsqrt(jnp.mean(xm * xm, axis=-1, keepdims=True) + LN_EPS)
        h_ref[...] = (n * (1.0 + p_ref[0, 1:2, :]) + p_ref[0, 0:1, :]).astype(BF16)

    return pl.pallas_call(
        body, name="modulate_fwd", grid=(r // tm,),
        in_specs=[pl.BlockSpec((tm, dm), lambda i: (jnp.minimum(i, lat - 1), 0)),
                  pl.BlockSpec((tm, dm), lambda i: (jnp.maximum(i - lat, 0), 0)),
                  pl.BlockSpec((1, 8, dm), lambda i: ((i >= lat).astype(jnp.int32), 0, 0))],
        out_specs=pl.BlockSpec((tm, dm), lambda i: (i, 0)),
        out_shape=jax.ShapeDtypeStruct((r, dm), BF16),
        compiler_params=_params(("parallel",)),
    )(x, ctx, prm)


def _modulate_bwd(x, ctx, dh, prm, gx_direct, tm):
    t_rows, dm = x.shape
    lat, n_ct = t_rows // tm, ctx.shape[0] // tm
    is_ctx = lambda i: i < n_ct
    cls = lambda i: is_ctx(i).astype(jnp.int32)
    lat_tile = lambda i: (jnp.maximum(i - n_ct, 0), 0)

    def body(x_ref, c_ref, dh_ref, p_ref, gd_ref, gx_ref, acc_ref):
        i = pl.program_id(0)

        @pl.when((i == 0) | (i == n_ct))
        def _():
            acc_ref[...] = jnp.zeros_like(acc_ref)

        x = jnp.where(is_ctx(i), c_ref[...], x_ref[...])
        dh_v = dh_ref[...]
        mu = jnp.mean(x, axis=-1, keepdims=True)
        xm = x - mu
        rstd = lax.rsqrt(jnp.mean(xm * xm, axis=-1, keepdims=True) + LN_EPS)
        n = xm * rstd
        acc_ref[0, 0:1, :] += jnp.sum(dh_v, axis=0, keepdims=True)
        acc_ref[0, 1:2, :] += jnp.sum(dh_v * n, axis=0, keepdims=True)
        dn = dh_v * (1.0 + p_ref[0, 1:2, :])
        dx = rstd * (dn - jnp.mean(dn, axis=-1, keepdims=True) - n * jnp.mean(dn * n, axis=-1, keepdims=True))
        gx_ref[...] = dx + gd_ref[...]

    return pl.pallas_call(
        body, name="modulate_bwd", grid=(n_ct + lat,),
        in_specs=[pl.BlockSpec((tm, dm), lat_tile),
                  pl.BlockSpec((tm, dm), lambda i: (jnp.minimum(i, n_ct - 1), 0)),
                  pl.BlockSpec((tm, dm), lambda i: (jnp.where(is_ctx(i), lat + i, i - n_ct), 0)),
                  pl.BlockSpec((1, 8, dm), lambda i: (cls(i), 0, 0)),
                  pl.BlockSpec((tm, dm), lat_tile)],
        out_specs=(pl.BlockSpec((tm, dm), lat_tile), pl.BlockSpec((1, 8, dm), lambda i: (cls(i), 0, 0))),
        out_shape=(jax.ShapeDtypeStruct((t_rows, dm), F32), jax.ShapeDtypeStruct((2, 8, dm), F32)),
        compiler_params=_params(("arbitrary",)),
    )(x, ctx, dh, prm, gx_direct)


def _conv_parts(t_rows, c_rows):
    return ((0, t_rows, t_rows // GRID_W, GRID_W), (t_rows, c_rows, 1, c_rows))


def _col_shifts(x2, rows_g, width_g):
    n, ct = x2.shape
    col = lax.broadcasted_iota(jnp.int32, (width_g, ct), 0)
    as_grid = lambda a: a.reshape(rows_g, width_g, ct)
    left = as_grid(pltpu.roll(x2, 1, 0)) * (col >= 1).astype(F32)
    right = as_grid(pltpu.roll(x2, n - 1, 0)) * (col <= width_g - 2).astype(F32)
    return [left, as_grid(x2), right]


def _row_shift(y3, a):
    if a == 1:
        return y3
    if y3.shape[0] == 1:
        return jnp.zeros_like(y3)
    zero = jnp.zeros_like(y3[:1])
    return jnp.concatenate([zero, y3[:-1]], axis=0) if a == 0 else jnp.concatenate([y3[1:], zero], axis=0)


def _conv_taps(cols, w_ref, flip):
    rows_g = cols[0].shape[0]
    acc = None
    for a in range(3):
        if rows_g == 1 and a != 1:
            continue
        inner = None
        for b in range(3):
            tap = (2 - a) * 3 + (2 - b) if flip else a * 3 + b
            term = cols[b] * w_ref[tap:tap + 1, :]
            inner = term if inner is None else inner + term
        inner = _row_shift(inner, a)
        acc = inner if acc is None else acc + inner
    return acc


def _conv_fwd(u, conv_w9, conv_b, t_rows, c_rows, w, ct):
    r = u.shape[0]
    base = 5 * w // ct

    def body(x_ref, w_ref, b_ref, o_ref):
        for r0, n, rows_g, width_g in _conv_parts(t_rows, c_rows):
            pre = _conv_taps(_col_shifts(x_ref[r0:r0 + n, :], rows_g, width_g), w_ref, False) + b_ref[...]
            o_ref[r0:r0 + n, :] = _silu(pre).reshape(n, ct)

    return pl.pallas_call(
        body, name="conv_fwd", grid=(2 * w // ct,),
        in_specs=[pl.BlockSpec((r, ct), lambda i: (0, base + i)), pl.BlockSpec((9, ct), lambda i: (0, i)),
                  pl.BlockSpec((1, ct), lambda i: (0, i))],
        out_specs=pl.BlockSpec((r, ct), lambda i: (0, i)),
        out_shape=jax.ShapeDtypeStruct((r, 2 * w), F32),
        compiler_params=_params(("parallel",)),
    )(u, conv_w9, conv_b)


def _conv_bwd(u, dqk_pair, conv_w9, conv_b, t_rows, c_rows, w, ct):
    r = u.shape[0]
    base = 5 * w // ct

    def body(x_ref, d1_ref, d2_ref, w_ref, b_ref, dx_ref, dw_ref, db_ref):
        dw = [jnp.zeros((1, ct), F32) for _ in range(9)]
        db = jnp.zeros((1, ct), F32)
        for r0, n, rows_g, width_g in _conv_parts(t_rows, c_rows):
            cols = _col_shifts(x_ref[r0:r0 + n, :], rows_g, width_g)
            pre = (_conv_taps(cols, w_ref, False) + b_ref[...]).reshape(n, ct)
            sg = _sigmoid(pre)
            dpre = (d1_ref[r0:r0 + n, :] + d2_ref[r0:r0 + n, :]) * (sg * (1.0 + pre * (1.0 - sg)))
            db = db + jnp.sum(dpre, axis=0, keepdims=True)
            dx_ref[r0:r0 + n, :] = _conv_taps(_col_shifts(dpre, rows_g, width_g), w_ref, True).reshape(n, ct)
            dpre3 = dpre.reshape(rows_g, width_g, ct)
            for a in range(3):
                if rows_g == 1 and a != 1:
                    continue
                moved = _row_shift(dpre3, 2 - a)
                for b in range(3):
                    prod = jnp.sum(cols[b] * moved, axis=0)
                    dw[a * 3 + b] = dw[a * 3 + b] + jnp.sum(prod, axis=0, keepdims=True)
        for tap in range(9):
            dw_ref[tap:tap + 1, :] = dw[tap]
        db_ref[...] = db

    return pl.pallas_call(
        body, name="conv_bwd", grid=(2 * w // ct,),
        in_specs=[pl.BlockSpec((r, ct), lambda i: (0, base + i)), pl.BlockSpec((r, ct), lambda i: (0, i)),
                  pl.BlockSpec((r, ct), lambda i: (0, i)),
                  pl.BlockSpec((9, ct), lambda i: (0, i)), pl.BlockSpec((1, ct), lambda i: (0, i))],
        out_specs=(pl.BlockSpec((r, ct), lambda i: (0, i)), pl.BlockSpec((9, ct), lambda i: (0, i)),
                   pl.BlockSpec((1, ct), lambda i: (0, i))),
        out_shape=(jax.ShapeDtypeStruct((r, 2 * w), F32), jax.ShapeDtypeStruct((9, 2 * w), F32),
                   jax.ShapeDtypeStruct((1, 2 * w), F32)),
        compiler_params=_params(("parallel",)),
    )(u, dqk_pair[0], dqk_pair[1], conv_w9, conv_b)


def _assemble_du(groups, gates, n_pad, tm):
    flat, layout = [], []
    for entry in list(groups) + [gates]:
        parts = entry if isinstance(entry, (tuple, list)) else (entry,)
        layout.append((len(flat), len(parts), parts[0].shape[1]))
        flat += list(parts)
    r = flat[0].shape[0]

    def body(*refs):
        o_ref = refs[-1]
        col = 0
        for first, count, width in layout:
            val = refs[first][...]
            for extra in range(1, count):
                val = val + refs[first + extra][...]
            o_ref[:, col:col + width] = val.astype(BF16)
            col += width
        assert col == n_pad

    return pl.pallas_call(
        body, name="assemble_du", grid=(r // tm,),
        in_specs=[pl.BlockSpec((tm, a.shape[1]), lambda i: (i, 0)) for a in flat],
        out_specs=pl.BlockSpec((tm, n_pad), lambda i: (i, 0)),
        out_shape=jax.ShapeDtypeStruct((r, n_pad), BF16),
        compiler_params=_params(("parallel",)),
    )(*flat)


def _scan_order(n_lat, n_ctx, rev):
    n = n_lat + n_ctx
    if rev:
        return lambda j: n - 1 - j
    return lambda j: (j + n_lat) % n


DIRS = (False, True)


def _hg_scan_fwd(u, lb_full, w, n_lat, n_ctx, chunk):
    r = u.shape[0]
    n_heads = w // HG_DK
    n_chunks = n_lat + n_ctx
    nat = [_scan_order(n_lat, n_ctx, rev) for rev in DIRS]

    def body(*refs):
        ins, outs, scratch = refs[:8], refs[8:12], refs[12:]

        @pl.when(pl.program_id(0) == 0)
        def _():
            for s_ref in scratch:
                s_ref[...] = jnp.zeros_like(s_ref)

        results = []
        for d, rev in enumerate(DIRS):
            aq, af, ai, lb_ref = ins[4 * d:4 * d + 4]
            state = [scratch[d][h] for h in range(n_heads)]
            results.append((state, _hg_chunk(state, aq[...], af[...], ai[...],
                                             lb_ref[0, 0:1, :], lb_ref[0, 1:2, :], rev)))
        for d, (state, (new, o)) in enumerate(results):
            o_ref, save_ref = outs[2 * d:2 * d + 2]
            o_ref[...] = o
            for h in range(n_heads):
                save_ref[0, h] = state[h]
                scratch[d][h] = new[h]

    in_specs, out_specs, out_shape = [], [], []
    for d in range(2):
        in_specs += [pl.BlockSpec((chunk,w), lambda j, d=d: (nat[d](j), 0)),
                     pl.BlockSpec((chunk,w), lambda j, d=d: (nat[d](j), 1 + d)),
                     pl.BlockSpec((chunk,w), lambda j, d=d: (nat[d](j), 3)),
                     pl.BlockSpec((1, 2, w), lambda j, d=d: (d, 0, 0))]
        out_specs += [pl.BlockSpec((chunk,w), lambda j, d=d: (nat[d](j), 0)),
                      pl.BlockSpec((1, n_heads, HG_DK, HG_DK), lambda j: (j, 0, 0, 0))]
        out_shape += [jax.ShapeDtypeStruct((r, w), F32),
                      jax.ShapeDtypeStruct((n_chunks, n_heads, HG_DK, HG_DK), F32)]
    o_f, s_f, o_b, s_b = pl.pallas_call(
        body, name="hg_scan_fwd", grid=(n_chunks,), in_specs=in_specs, out_specs=tuple(out_specs),
        out_shape=tuple(out_shape), scratch_shapes=[pltpu.VMEM((n_heads, HG_DK, HG_DK), F32)] * 2,
        compiler_params=_params(("arbitrary",)),
    )(u, u, u, lb_full, u, u, u, lb_full)
    return (o_f, o_b), (s_f, s_b)


def _hg_scan_bwd(u, lb_full, saved, d_o, w, n_lat, n_ctx, chunk):
    r = u.shape[0]
    n_heads = w // HG_DK
    n_chunks = n_lat + n_ctx
    step = lambda jj: n_chunks - 1 - jj
    nat = [(lambda jj, o=_scan_order(n_lat, n_ctx, rev): o(step(jj))) for rev in DIRS]

    def body(*refs):
        ins, outs, scratch = refs[:12], refs[12:20], refs[20:]
        jj = pl.program_id(0)

        @pl.when(jj == 0)
        def _():
            for d in range(2):
                scratch[d][...] = jnp.zeros_like(scratch[d])
                outs[4 * d + 3][...] = jnp.zeros_like(outs[4 * d + 3])

        results = []
        for d, rev in enumerate(DIRS):
            aq, af, ai, lb_ref, save_ref, do_ref = ins[6 * d:6 * d + 6]
            f = lambda st, a, b, c, l0, l1, rev=rev: _hg_chunk(st, a, b, c, l0, l1, rev)
            _, vjp = jax.vjp(f, [save_ref[0, h] for h in range(n_heads)], aq[...], af[...], ai[...],
                             lb_ref[0, 0:1, :], lb_ref[0, 1:2, :])
            d_out = do_ref[...] * (nat[d](jj) < n_lat).astype(F32)
            results.append(vjp(([scratch[d][h] for h in range(n_heads)], d_out)))
        for d, (dst, daq, daf, dai, dl0, dl1) in enumerate(results):
            daq_ref, daf_ref, dai_ref, dlb_ref = outs[4 * d:4 * d + 4]
            for h in range(n_heads):
                scratch[d][h] = dst[h]
            daq_ref[...] = daq
            daf_ref[...] = daf
            dai_ref[...] = dai
            dlb_ref[0:1, :] += dl0
            dlb_ref[1:2, :] += dl1

    in_specs, out_specs, out_shape, operands = [], [], [], []
    for d in range(2):
        row = lambda jj, d=d: (nat[d](jj), 0)
        in_specs += [pl.BlockSpec((chunk,w), row),
                     pl.BlockSpec((chunk,w), lambda jj, d=d: (nat[d](jj), 1 + d)),
                     pl.BlockSpec((chunk,w), lambda jj, d=d: (nat[d](jj), 3)),
                     pl.BlockSpec((1, 2, w), lambda jj, d=d: (d, 0, 0)),
                     pl.BlockSpec((1, n_heads, HG_DK, HG_DK), lambda jj: (step(jj), 0, 0, 0)),
                     pl.BlockSpec((chunk,w), lambda jj, d=d: (jnp.minimum(nat[d](jj), n_lat - 1), 0))]
        operands += [u, u, u, lb_full, saved[d], d_o]
        out_specs += [pl.BlockSpec((chunk,w), row)] * 3 + [pl.BlockSpec((2, w), lambda jj: (0, 0))]
        out_shape += [jax.ShapeDtypeStruct((r, w), F32)] * 3 + [jax.ShapeDtypeStruct((2, w), F32)]
    res = pl.pallas_call(
        body, name="hg_scan_bwd", grid=(n_chunks,), in_specs=in_specs, out_specs=tuple(out_specs),
        out_shape=tuple(out_shape), scratch_shapes=[pltpu.VMEM((n_heads, HG_DK, HG_DK), F32)] * 2,
        compiler_params=_params(("arbitrary",)),
    )(*operands)
    return res[0:4], res[4:8]


def _ml_state_shapes(n_chunks, n_heads, dh):
    return (jax.ShapeDtypeStruct((n_chunks, n_heads, dh, dh), F32),
            jax.ShapeDtypeStruct((n_chunks, n_heads, 1, dh), F32),
            jax.ShapeDtypeStruct((n_chunks, n_heads, 1, LANE), F32))


def _ml_state_specs(n_heads, dh, index):
    return (pl.BlockSpec((1, n_heads, dh, dh), lambda j: (index(j), 0, 0, 0)),
            pl.BlockSpec((1, n_heads, 1, dh), lambda j: (index(j), 0, 0, 0)),
            pl.BlockSpec((1, n_heads, 1, LANE), lambda j: (index(j), 0, 0, 0)))


def _ml_state_scratch(n_heads, dh):
    return [pltpu.VMEM((n_heads, dh, dh), F32), pltpu.VMEM((n_heads, 1, dh), F32), pltpu.VMEM((n_heads, 1, LANE), F32)]


def _ml_scan_fwd(qk, u, gate_b, w, n_heads, n_lat, n_ctx, chunk):
    r = u.shape[0]
    dh = w // n_heads
    n_chunks = n_lat + n_ctx
    nat = [_scan_order(n_lat, n_ctx, rev) for rev in DIRS]

    def body(*refs):
        ins, outs, scratch = refs[:10], refs[10:18], refs[18:]

        @pl.when(pl.program_id(0) == 0)
        def _():
            for s_ref in scratch:
                s_ref[...] = jnp.zeros_like(s_ref)

        results = []
        for d, rev in enumerate(DIRS):
            q, k, v, g, gb = ins[5 * d:5 * d + 5]
            state = tuple([ref[h] for h in range(n_heads)] for ref in scratch[3 * d:3 * d + 3])
            results.append((state, _ml_chunk(state, q[...], k[...], v[...], g[...], gb[...], rev, d)))
        for d, (state, (new, o)) in enumerate(results):
            outs[4 * d][...] = o
            for part in range(3):
                for h in range(n_heads):
                    outs[4 * d + 1 + part][0, h] = state[part][h]
                    scratch[3 * d + part][h] = new[part][h]

    in_specs, out_specs, out_shape = [], [], []
    for d in range(2):
        in_specs += [pl.BlockSpec((chunk,w), lambda j, d=d: (nat[d](j), 0)),
                     pl.BlockSpec((chunk,w), lambda j, d=d: (nat[d](j), 1)),
                     pl.BlockSpec((chunk,w), lambda j, d=d: (nat[d](j), 7)),
                     pl.BlockSpec((chunk,LANE), lambda j, d=d: (nat[d](j), 10 * w // LANE)),
                     pl.BlockSpec((1, LANE), lambda j: (0, 0))]
        out_specs += [pl.BlockSpec((chunk,w), lambda j, d=d: (nat[d](j), 0))]
        out_specs += list(_ml_state_specs(n_heads, dh, lambda j: j))
        out_shape += [jax.ShapeDtypeStruct((r, w), F32)] + list(_ml_state_shapes(n_chunks, n_heads, dh))
    res = pl.pallas_call(
        body, name="ml_scan_fwd", grid=(n_chunks,), in_specs=in_specs, out_specs=tuple(out_specs),
        out_shape=tuple(out_shape), scratch_shapes=_ml_state_scratch(n_heads, dh) * 2,
        compiler_params=_params(("arbitrary",)),
    )(qk, qk, u, u, gate_b, qk, qk, u, u, gate_b)
    return (res[0], res[4]), (res[1:4], res[5:8])


def _ml_scan_bwd(qk, u, gate_b, saved, d_h, w, n_heads, n_lat, n_ctx, chunk):
    r = u.shape[0]
    dh = w // n_heads
    n_chunks = n_lat + n_ctx
    step = lambda jj: n_chunks - 1 - jj
    nat = [(lambda jj, o=_scan_order(n_lat, n_ctx, rev): o(step(jj))) for rev in DIRS]

    def body(*refs):
        ins, outs, scratch = refs[:18], refs[18:26], refs[26:]
        jj = pl.program_id(0)

        @pl.when(jj == 0)
        def _():
            for s_ref in scratch:
                s_ref[...] = jnp.zeros_like(s_ref)
            for d in range(2):
                outs[4 * d + 3][...] = jnp.zeros_like(outs[4 * d + 3])

        results = []
        for d, rev in enumerate(DIRS):
            q, k, v, g, gb, sc, sn, sm, dh_ref = ins[9 * d:9 * d + 9]
            state = tuple([ref[0, h] for h in range(n_heads)] for ref in (sc, sn, sm))
            f = lambda st, a, b, c, gg, bb, rev=rev, d=d: _ml_chunk(st, a, b, c, gg, bb, rev, d)
            _, vjp = jax.vjp(f, state, q[...], k[...], v[...], g[...], gb[...])
            d_state = tuple([ref[h] for h in range(n_heads)] for ref in scratch[3 * d:3 * d + 3])
            d_out = dh_ref[...] * (nat[d](jj) < n_lat).astype(F32)
            results.append(vjp((d_state, d_out)))
        for d, (d_state, dq, dk, dv, dg, dgb) in enumerate(results):
            dqk_ref, dv_ref, dg_ref, dgb_ref = outs[4 * d:4 * d + 4]
            for part in range(3):
                for h in range(n_heads):
                    scratch[3 * d + part][h] = d_state[part][h]
            dqk_ref[:, 0:w] = dq
            dqk_ref[:, w:2 * w] = dk
            dv_ref[...] = dv
            dg_ref[...] = dg
            dgb_ref[...] += dgb

    in_specs, out_specs, out_shape, operands = [], [], [], []
    for d in range(2):
        row = lambda jj, d=d: (nat[d](jj), 0)
        in_specs += [pl.BlockSpec((chunk,w), row), pl.BlockSpec((chunk,w), lambda jj, d=d: (nat[d](jj), 1)),
                     pl.BlockSpec((chunk,w), lambda jj, d=d: (nat[d](jj), 7)),
                     pl.BlockSpec((chunk,LANE), lambda jj, d=d: (nat[d](jj), 10 * w // LANE)),
                     pl.BlockSpec((1, LANE), lambda jj: (0, 0))]
        in_specs += list(_ml_state_specs(n_heads, dh, step))
        in_specs += [pl.BlockSpec((chunk,w), lambda jj, d=d: (jnp.minimum(nat[d](jj), n_lat - 1), 0))]
        operands += [qk, qk, u, u, gate_b, *saved[d], d_h]
        out_specs += [pl.BlockSpec((chunk,2 * w), row), pl.BlockSpec((chunk,w), row),
                      pl.BlockSpec((chunk,LANE), row), pl.BlockSpec((1, LANE), lambda jj: (0, 0))]
        out_shape += [jax.ShapeDtypeStruct((r, 2 * w), F32), jax.ShapeDtypeStruct((r, w), F32),
                      jax.ShapeDtypeStruct((r, LANE), F32), jax.ShapeDtypeStruct((1, LANE), F32)]
    res = pl.pallas_call(
        body, name="ml_scan_bwd", grid=(n_chunks,), in_specs=in_specs, out_specs=tuple(out_specs),
        out_shape=tuple(out_shape), scratch_shapes=_ml_state_scratch(n_heads, dh) * 2,
        compiler_params=_params(("arbitrary",)),
    )(*operands)
    return res[0:4], res[4:8]


def _post_specs(w, tm, lat_tiles, cols):
    return [pl.BlockSpec((tm, w), (lambda i, cb=cb: (jnp.minimum(i, lat_tiles - 1), cb))) for cb in cols]


def _post_fwd(o_f, o_b, h_f, h_b, u, wa, wb, t_rows, w, n_hg, n_ml, tm):
    lat_tiles = t_rows // tm

    def body(of, ob, hf, hb, az, bo, bz, wa_ref, wb_ref, y_ref):
        y_ref[...] = _post_fn(of[...], ob[...], az[...], hf[...], hb[...], bo[...], bz[...],
                              wa_ref[...], wb_ref[...], n_hg, n_ml).astype(BF16)

    rows = pl.BlockSpec((tm, w), lambda i: (i, 0))
    vec = pl.BlockSpec((1, w), lambda i: (0, 0))
    return pl.pallas_call(
        body, name="post_fwd", grid=(lat_tiles,),
        in_specs=[rows] * 4 + _post_specs(w, tm, lat_tiles, (4, 8, 9)) + [vec, vec],
        out_specs=pl.BlockSpec((tm, 2 * w), lambda i: (i, 0)),
        out_shape=jax.ShapeDtypeStruct((t_rows, 2 * w), BF16),
        compiler_params=_params(("parallel",)),
    )(o_f, o_b, h_f, h_b, u, u, u, wa, wb)


def _post_bwd(o_f, o_b, h_f, h_b, u, wa, wb, dy, t_rows, w, n_hg, n_ml, tm):
    r = u.shape[0]
    lat_tiles = t_rows // tm
    lat = lambda i: (jnp.minimum(i, lat_tiles - 1), 0)

    def body(of, ob, hf, hb, az, bo, bz, wa_ref, wb_ref, dy_ref, do_ref, dh_ref, daz_ref, dbo_ref, dbz_ref,
             dwa_ref, dwb_ref):
        i = pl.program_id(0)

        @pl.when(i == 0)
        def _():
            dwa_ref[...] = jnp.zeros_like(dwa_ref)
            dwb_ref[...] = jnp.zeros_like(dwb_ref)

        @pl.when(i < lat_tiles)
        def _():
            f = functools.partial(_post_fn, n_hg=n_hg, n_ml=n_ml)
            _, vjp = jax.vjp(f, of[...], ob[...], az[...], hf[...], hb[...], bo[...], bz[...], wa_ref[...], wb_ref[...])
            d_of, _, d_az, d_hf, _, d_bo, d_bz, d_wa, d_wb = vjp(dy_ref[...])
            do_ref[...] = d_of
            dh_ref[...] = d_hf
            daz_ref[...] = d_az
            dbo_ref[...] = d_bo
            dbz_ref[...] = d_bz
            dwa_ref[...] += d_wa
            dwb_ref[...] += d_wb

        @pl.when(i >= lat_tiles)
        def _():
            daz_ref[...] = jnp.zeros_like(daz_ref)
            dbo_ref[...] = jnp.zeros_like(dbo_ref)
            dbz_ref[...] = jnp.zeros_like(dbz_ref)

    lat_rows = pl.BlockSpec((tm, w), lat)
    all_rows = pl.BlockSpec((tm, w), lambda i: (i, 0))
    vec = pl.BlockSpec((1, w), lambda i: (0, 0))
    sd_t = jax.ShapeDtypeStruct((t_rows, w), F32)
    sd_r = jax.ShapeDtypeStruct((r, w), F32)
    sd_v = jax.ShapeDtypeStruct((1, w), F32)
    return pl.pallas_call(
        body, name="post_bwd", grid=(r // tm,),
        in_specs=[lat_rows] * 4 + _post_specs(w, tm, lat_tiles, (4, 8, 9)) + [vec, vec]
        + [pl.BlockSpec((tm, 2 * w), lat)],
        out_specs=(lat_rows, lat_rows, all_rows, all_rows, all_rows, vec, vec),
        out_shape=(sd_t, sd_t, sd_r, sd_r, sd_r, sd_v, sd_v),
        compiler_params=_params(("arbitrary",)),
    )(o_f, o_b, h_f, h_b, u, u, u, wa, wb, dy)


OUT_ROW_GATE, OUT_ROW_LN_G, OUT_ROW_LN_B, OUT_ROW_LOSS = 0, 1, 2, 3


def _out_block(y, w_out, x, target, prm, tm):
    t_rows, dm = x.shape
    di = y.shape[1]

    def body(y_ref, w_ref, x_ref, t_ref, p_ref, dz_ref, dy_ref, gx_ref, acc_ref):
        @pl.when(pl.program_id(0) == 0)
        def _():
            acc_ref[...] = jnp.zeros_like(acc_ref)

        gate, ln_g, ln_b = p_ref[0:1, :], p_ref[1:2, :], p_ref[2:3, :]
        z = _nn(y_ref[...], w_ref[...])
        res = ALPHA * x_ref[...] + gate * z
        mu = jnp.mean(res, axis=-1, keepdims=True)
        rc = res - mu
        rstd = lax.rsqrt(jnp.mean(rc * rc, axis=-1, keepdims=True) + LN_EPS)
        rn = rc * rstd
        err = rn * ln_g + ln_b - t_ref[...]
        d_out = err * (1.0 / dm)
        d_rn = d_out * ln_g
        d_res = rstd * (d_rn - jnp.mean(d_rn, axis=-1, keepdims=True)
                        - rn * jnp.mean(d_rn * rn, axis=-1, keepdims=True))
        acc_ref[OUT_ROW_GATE:OUT_ROW_GATE + 1, :] += jnp.sum(d_res * z, axis=0, keepdims=True)
        acc_ref[OUT_ROW_LN_G:OUT_ROW_LN_G + 1, :] += jnp.sum(d_out * rn, axis=0, keepdims=True)
        acc_ref[OUT_ROW_LN_B:OUT_ROW_LN_B + 1, :] += jnp.sum(d_out, axis=0, keepdims=True)
        acc_ref[OUT_ROW_LOSS:OUT_ROW_LOSS + 1, :] += (0.5 / dm) * jnp.sum(err * err, axis=0, keepdims=True)
        gx_ref[...] = ALPHA * d_res
        dz = (d_res * gate).astype(BF16)
        dz_ref[...] = dz
        dy_ref[...] = _nt(dz, w_ref[...])

    rows_d = pl.BlockSpec((tm, dm), lambda i: (i, 0))
    rows_i = pl.BlockSpec((tm, di), lambda i: (i, 0))
    return pl.pallas_call(
        body, name="out_block", grid=(t_rows // tm,),
        in_specs=[rows_i, pl.BlockSpec((di, dm), lambda i: (0, 0)), rows_d, rows_d,
                  pl.BlockSpec((8, dm), lambda i: (0, 0))],
        out_specs=(rows_d, rows_i, rows_d, pl.BlockSpec((8, dm), lambda i: (0, 0))),
        out_shape=(jax.ShapeDtypeStruct((t_rows, dm), BF16), jax.ShapeDtypeStruct((t_rows, di), F32),
                   jax.ShapeDtypeStruct((t_rows, dm), F32), jax.ShapeDtypeStruct((8, dm), F32)),
        compiler_params=_params(("arbitrary",)),
    )(y, w_out, x, target, prm)


def _mod_fwd(c16, w_mod, tn):
    dm, n = w_mod.shape

    def body(c_ref, w_ref, o_ref, a_ref):
        a = _silu(c_ref[...])
        a_ref[...] = a
        o_ref[...] = _nn(a, w_ref[...], HIGHEST)

    return pl.pallas_call(
        body, name="mod_fwd", grid=(n // tn,),
        in_specs=[pl.BlockSpec((16, dm), lambda j: (0, 0)), pl.BlockSpec((dm, tn), lambda j: (0, j))],
        out_specs=(pl.BlockSpec((16, tn), lambda j: (0, j)), pl.BlockSpec((16, dm), lambda j: (0, 0))),
        out_shape=(jax.ShapeDtypeStruct((16, n), F32), jax.ShapeDtypeStruct((16, dm), F32)),
        compiler_params=_params(("arbitrary",)),
    )(c16, w_mod)


def _mod_bwd(a16, dm16, w_mod, tn):
    dm, n = w_mod.shape

    def body(a_ref, d_ref, w_ref, dw_ref, dc_ref):
        @pl.when(pl.program_id(0) == 0)
        def _():
            dc_ref[...] = jnp.zeros_like(dc_ref)
        dw_ref[...] = _tn(a_ref[...], d_ref[...], HIGHEST)
        dc_ref[...] += _nt(d_ref[...], w_ref[...], HIGHEST)

    return pl.pallas_call(
        body, name="mod_bwd", grid=(n // tn,),
        in_specs=[pl.BlockSpec((16, dm), lambda j: (0, 0)), pl.BlockSpec((16, tn), lambda j: (0, j)),
                  pl.BlockSpec((dm, tn), lambda j: (0, j))],
        out_specs=(pl.BlockSpec((dm, tn), lambda j: (0, j)), pl.BlockSpec((16, dm), lambda j: (0, 0))),
        out_shape=(jax.ShapeDtypeStruct((dm, n), F32), jax.ShapeDtypeStruct((16, dm), F32)),
        compiler_params=_params(("arbitrary",)),
    )(a16, dm16, w_mod)


def _sum_devices(g, fold_rows):
    n_dev, rows, n = g.shape

    def body(g_ref, s_ref, t_ref):
        s = g_ref[0]
        for dev in range(1, n_dev):
            s = s + g_ref[dev]
        t_ref[...] = jnp.broadcast_to(jnp.sum(s, axis=-1, keepdims=True), (rows, LANE))
        s_ref[...] = s
        s_ref[0:fold_rows, :] = s[0:fold_rows] + s[fold_rows:2 * fold_rows]

    return pl.pallas_call(
        body, name="sum_devices",
        out_shape=(jax.ShapeDtypeStruct((rows, n), F32), jax.ShapeDtypeStruct((rows, LANE), F32)),
        compiler_params=_params(),
    )(g)


def _c_ctx_grad(parts, c_ctx_row):
    def body(p_ref, c_ref, o_ref):
        s = p_ref[0]
        for chip in range(1, N_CHIPS):
            s = s + p_ref[2 * chip]
        cv = c_ref[...]
        sg = _sigmoid(cv)
        o_ref[...] = s * (sg * (1.0 + cv * (1.0 - sg)))

    return pl.pallas_call(
        body, name="c_ctx_grad", out_shape=jax.ShapeDtypeStruct(parts.shape[1:], F32), compiler_params=_params(),
    )(parts, c_ctx_row)


def _sum_pair(name, mine, got):
    def body(a_ref, b_ref, o_ref):
        o_ref[...] = (a_ref[...] + b_ref[...]).astype(BF16)

    k, rows, n = mine.shape
    tl = _largest_divisor(n, max(LANE, (1 << 18) // rows), LANE)
    spec = pl.BlockSpec((1, rows, tl), lambda kk, i: (kk, 0, i))
    return pl.pallas_call(
        body, name=name, grid=(k, n // tl), in_specs=[spec, spec], out_specs=spec,
        out_shape=jax.ShapeDtypeStruct(mine.shape, BF16), compiler_params=_params(("parallel", "parallel")),
    )(mine, got)


def _sum_chips(name, got):
    k, rows, n = got.shape
    tl = _largest_divisor(n, max(LANE, (1 << 18) // rows), LANE)

    def body(g_ref, o_ref):
        total = g_ref[0].astype(F32)
        for kk in range(1, k):
            total = total + g_ref[kk].astype(F32)
        o_ref[...] = total

    return pl.pallas_call(
        body, name=name, grid=(n // tl,),
        in_specs=[pl.BlockSpec((k, rows, tl), lambda i: (0, 0, i))], out_specs=pl.BlockSpec((rows, tl), lambda i: (0, i)),
        out_shape=jax.ShapeDtypeStruct((rows, n), F32), compiler_params=_params(("parallel",)),
    )(got)


def _adamw_update(w, g, m, v):
    m2 = ADAM_B1 * m + (1.0 - ADAM_B1) * g
    v2 = ADAM_B2 * v + (1.0 - ADAM_B2) * jnp.square(g)
    m_hat = m2 / (1.0 - ADAM_B1 ** ADAM_STEP)
    v_hat = v2 / (1.0 - ADAM_B2 ** ADAM_STEP)
    return -ADAM_LR * (m_hat / (jnp.sqrt(v_hat) + ADAM_EPS) + ADAM_WD * w), m2, v2


def _adamw(name, w, g, m, v):
    rows, n = w.shape
    if rows % 8 == 0:
        tr = _largest_divisor(rows, max(8, (1 << 18) // n), 8)
        block, index, steps = (tr, n), (lambda i: (i, 0)), rows // tr
    else:
        tl = _largest_divisor(n, max(LANE, (1 << 18) // rows), LANE)
        block, index, steps = (rows, tl), (lambda i: (0, i)), n // tl

    def body(w_ref, g_ref, m_ref, v_ref, d_ref, mo_ref, vo_ref):
        d_ref[...], mo_ref[...], vo_ref[...] = _adamw_update(w_ref[...], g_ref[...], m_ref[...], v_ref[...])

    spec = pl.BlockSpec(block, index)
    sds = jax.ShapeDtypeStruct((rows, n), F32)
    return pl.pallas_call(
        body, name=name, grid=(steps,), in_specs=[spec] * 4, out_specs=(spec,) * 3,
        out_shape=(sds, sds, sds), compiler_params=_params(("parallel",)),
    )(w, g, m, v)


PACK_LANES = 1024


def _pack(pieces):
    flat = jnp.concatenate([p.reshape(-1) for p in pieces])
    total = -(-flat.shape[0] // (8 * PACK_LANES)) * 8 * PACK_LANES
    return jnp.pad(flat, (0, total - flat.shape[0])).reshape(-1, PACK_LANES)


def _unpack(packed, shapes):
    flat = packed.reshape(-1)
    out, off = [], 0
    for shp in shapes:
        size = math.prod(shp)
        out.append(flat[off:off + size].reshape(shp))
        off += size
    return out


def _rows8(rows, width):
    flat = [r.reshape(width) for r in rows] + [jnp.zeros(((8 - len(rows)) * width,), F32)]
    return jnp.concatenate(flat).reshape(8, width)


def _reduce_scatter(tag, chip, ci, mine, other, axis):
    got = _sibling_swap("rs_pair_" + tag, [other])[0]
    pair = _sum_pair("rs_pair_sum_" + tag, mine, got)
    landed = _own_block(chip, pair, _chip_scatter("rs_chips_" + tag, [pair])[0])
    half = _sum_chips("rs_chip_sum_" + tag, landed)
    return _join_halves(ci, half, _sibling_swap("rs_join_" + tag, [half])[0], axis)


def kernel(x, c, ctx, c_ctx, w_mod, b_mod, w_in, conv_w, conv_b, hg_lb, ml_gate_b, hg_norm_w, ml_norm_w, w_out, ln_g, ln_b, loss_target, m_c_ctx, m_w_mod, m_b_mod, m_w_in, m_conv_w, m_conv_b, m_hg_lb, m_ml_gate_b, m_hg_norm_w, m_ml_norm_w, m_w_out, m_ln_g, m_ln_b, v_c_ctx, v_w_mod, v_b_mod, v_w_in, v_conv_w, v_conv_b, v_hg_lb, v_ml_gate_b, v_hg_norm_w, v_ml_norm_w, v_w_out, v_ln_g, v_ln_b):
    t_rows, dm = x.shape[1], x.shape[2]
    c_rows = ctx.shape[1]
    w = hg_norm_w.shape[1]
    n_ml = ml_gate_b.shape[-1]
    n_hg = w // HG_DK
    di = 2 * w
    n_in = 10 * w + 4 * n_ml
    ns = w_in.shape[2]
    nm = w_mod.shape[2]
    n_pad = 10 * w + LANE
    r_rows = t_rows + c_rows
    row_gcd = math.gcd(t_rows, c_rows)
    hg_chunk, ml_chunk = math.gcd(HG_CHUNK, row_gcd), math.gcd(ML_CHUNK, row_gcd)
    hg_counts = (t_rows // hg_chunk, c_rows // hg_chunk, hg_chunk)
    ml_counts = (t_rows // ml_chunk, c_rows // ml_chunk, ml_chunk)
    assert ml_norm_w.shape[1] == w and di == dm and N_CHIPS * ns == n_in and N_CHIPS * nm == 3 * dm
    assert w_out.shape[1] * N_CHIPS == di and 4 * n_ml <= LANE and t_rows % GRID_W == 0

    xi, yi, ci = lax.axis_index("x"), lax.axis_index("y"), lax.axis_index("c")
    chip = 2 * xi + yi
    dev = 4 * xi + 2 * yi + ci

    tm = _largest_divisor(math.gcd(t_rows, c_rows), 256, 8)
    tm_mm = _largest_divisor(r_rows, 1088, SUBLANE_BF16)
    tn_mm = LANE * _largest_divisor(n_pad // LANE, 9)
    tn_mod = _largest_divisor(nm, 512, LANE)

    shard_shapes = [(dm,), (2, 2, w // N_CHIPS), (3, 3, di // N_CHIPS)]
    g1 = _all_gather8("gather_inputs", _pack([c, hg_lb, conv_w]))
    per_dev = [_unpack(g1[i], shard_shapes) for i in range(N_DEV)]
    c_all = jnp.stack([p[0] for p in per_dev])
    lb_full = jnp.concatenate([per_dev[2 * k][1] for k in range(N_CHIPS)], axis=-1)
    conv_w9 = jnp.concatenate([per_dev[2 * k][2] for k in range(N_CHIPS)], axis=-1).reshape(9, di)

    c16 = jnp.concatenate([c_all, c_ctx[None], jnp.zeros((16 - N_DEV - 1, dm), F32)])
    mod_part, a16 = _mod_fwd(c16, w_mod[0], tn_mod)
    g2 = _all_gather8("gather_mod", mod_part)
    mod_all = jnp.concatenate([g2[2 * k] for k in range(N_CHIPS)], axis=1) + b_mod
    mod_x = lax.dynamic_index_in_dim(mod_all, dev, 0, keepdims=False).reshape(3, dm)
    mod_c = mod_all[N_DEV].reshape(3, dm)
    prm = jnp.stack([_rows8(list(mod_x), dm), _rows8(list(mod_c), dm)])

    as_t = lambda a: jnp.transpose(a[0])
    halves = [lax.dynamic_slice_in_dim(as_t(w_in).astype(BF16), ci * (dm // 2), dm // 2, 1),
              lax.dynamic_slice_in_dim(w_out[0].astype(BF16), ci * (di // (2 * N_CHIPS)), di // (2 * N_CHIPS), 0)]
    fetched = [_own_block(chip, own, got)
               for own, got in zip(halves, _all_gather_chips("gather_weights", halves))]
    swapped = _sibling_swap("gather_weights_pair", fetched)
    gw_in = _join_halves(ci, fetched[0], swapped[0], 2)
    gw_out = _join_halves(ci, fetched[1], swapped[1], 1)
    wt_full = jnp.concatenate([gw_in.reshape(n_in, dm), jnp.zeros((n_pad - n_in, dm), BF16)])
    w_out_full = gw_out.reshape(di, dm)

    hc = _modulate_fwd(x[0], ctx[0], prm, tm)
    u = _mm_nt("in_proj", hc, wt_full, tm_mm, tn_mm, F32)
    (o_f, o_b), hg_saved = _hg_scan_fwd(u, lb_full, w, *hg_counts)
    qk = _conv_fwd(u, conv_w9, conv_b, t_rows, c_rows, w, LANE)
    gate_b_row = jnp.pad(ml_gate_b.reshape(1, -1), ((0, 0), (0, LANE - 4 * n_ml)))
    (h_f, h_b), ml_saved = _ml_scan_fwd(qk, u, gate_b_row, w, n_ml, *ml_counts)
    y = _post_fwd(o_f, o_b, h_f, h_b, u, hg_norm_w, ml_norm_w, t_rows, w, n_hg, n_ml, tm)
    prm_out = _rows8([mod_x[2], ln_g, ln_b], dm)
    dz, dy, gx_direct, acc_out = _out_block(y, w_out_full, x[0], loss_target[0], prm_out, tm // 2)

    d_w_out = _mm_tn("d_w_out", y, dz, _largest_divisor(di, 1024, LANE),
                     _largest_divisor(t_rows, 1024, SUBLANE_BF16))
    d_o, d_h, d_az, d_bo, d_bz, d_wa, d_wb = _post_bwd(o_f, o_b, h_f, h_b, u, hg_norm_w, ml_norm_w, dy,
                                                        t_rows, w, n_hg, n_ml, tm)
    (d_aq_f, d_aff, d_ai_f, d_lb_f), (d_aq_b, d_afb, d_ai_b, d_lb_b) = _hg_scan_bwd(
        u, lb_full, hg_saved, d_o, w, *hg_counts)
    (d_qk_f, d_v_f, d_g_f, d_gb_f), (d_qk_b, d_v_b, d_g_b, d_gb_b) = _ml_scan_bwd(
        qk, u, gate_b_row, ml_saved, d_h, w, n_ml, *ml_counts)
    d_bqk, d_cw, d_cb = _conv_bwd(u, (d_qk_f, d_qk_b), conv_w9, conv_b, t_rows, c_rows, w, LANE)
    du = _assemble_du([(d_aq_f, d_aq_b), d_aff, d_afb, (d_ai_f, d_ai_b), d_az, d_bqk, (d_v_f, d_v_b), d_bo, d_bz],
                      (d_g_f, d_g_b), n_pad, tm // 2)
    d_wt_in = _mm_tn("d_w_in", du, hc, tn_mm, tm_mm)
    d_hc = _mm_acc("d_h", du, wt_full, tm_mm, tn_mm)
    gx, acc_mod = _modulate_bwd(x[0], ctx[0], d_hc, prm, gx_direct, tm)
    grad_x = gx[None]

    half_in = dm // 2
    mine_in = lax.dynamic_slice_in_dim(d_wt_in, ci * half_in, half_in, 1)
    other_in = lax.dynamic_slice_in_dim(d_wt_in, (1 - ci) * half_in, half_in, 1)
    pieces_in = lambda a: jnp.stack([a[k * ns:(k + 1) * ns] for k in range(N_CHIPS)])
    g_wt_in = _reduce_scatter("w_in", chip, ci, pieces_in(mine_in), pieces_in(other_in), 1)
    d_w_out4 = d_w_out.reshape(N_CHIPS, 2, di // (2 * N_CHIPS), dm)
    g_w_out = _reduce_scatter("w_out", chip, ci, lax.dynamic_index_in_dim(d_w_out4, ci, 1, keepdims=False),
                              lax.dynamic_index_in_dim(d_w_out4, 1 - ci, 1, keepdims=False), 0)

    zero_row = jnp.zeros((dm,), F32)
    d_gb = jnp.concatenate([d_gb_f[:, 0:n_ml], d_gb_b[:, n_ml:2 * n_ml], d_gb_f[:, 2 * n_ml:3 * n_ml],
                            d_gb_b[:, 3 * n_ml:4 * n_ml], jnp.zeros((1, dm - 4 * n_ml), F32)], axis=1)
    rows = [acc_mod[0, 0], acc_mod[0, 1], acc_out[OUT_ROW_GATE],
            acc_mod[1, 0], acc_mod[1, 1], zero_row]
    rows += list(d_cw) + [d_cb[0], d_lb_f.reshape(dm), d_lb_b.reshape(dm),
                          jnp.concatenate([d_wa[0], d_wb[0]]), acc_out[OUT_ROW_LN_G], acc_out[OUT_ROW_LN_B],
                          acc_out[OUT_ROW_LOSS], d_gb[0], zero_row]
    ROW_CW, ROW_CB, ROW_LB, ROW_NORM, ROW_LN_G, ROW_LN_B, ROW_LOSS, ROW_GB = 6, 15, 16, 18, 19, 20, 21, 22
    g3 = _all_gather8("gather_small_grads", jnp.concatenate([r.reshape(dm) for r in rows]).reshape(len(rows), dm))
    sums, totals = _sum_devices(g3, 3)
    loss = totals[ROW_LOSS, 0]
    dm16 = jnp.concatenate([g3[:, 0:3, :].reshape(N_DEV, 3 * dm), sums[3:6].reshape(1, 3 * dm),
                            jnp.zeros((16 - N_DEV - 1, 3 * dm), F32)])
    g_w_mod, dc16 = _mod_bwd(a16, lax.dynamic_slice_in_dim(dm16, chip * nm, nm, 1), w_mod[0], tn_mod)
    g4 = _all_gather8("gather_c_ctx", jnp.pad(dc16[N_DEV:N_DEV + 1], ((0, 7), (0, 0))))
    g_c_ctx = _c_ctx_grad(g4, jnp.broadcast_to(c_ctx[None], (8, dm)))[0]

    chip_cols = lambda a, width: lax.dynamic_slice_in_dim(a, chip * width, width, a.ndim - 1)
    grads = {
        "c_ctx": g_c_ctx,
        "w_mod": g_w_mod[None],
        "b_mod": sums[0:3].reshape(1, 3 * dm),
        "w_in": jnp.transpose(g_wt_in)[None],
        "conv_w": chip_cols(sums[ROW_CW:ROW_CW + 9].reshape(1, 3, 3, di), di // N_CHIPS),
        "conv_b": sums[ROW_CB][None],
        "hg_lb": chip_cols(sums[ROW_LB:ROW_LB + 2].reshape(2, 2, w), w // N_CHIPS),
        "ml_gate_b": sums[ROW_GB, 0:4 * n_ml].reshape(1, 4, n_ml),
        "hg_norm_w": sums[ROW_NORM, 0:w][None],
        "ml_norm_w": sums[ROW_NORM, w:2 * w][None],
        "w_out": g_w_out[None],
        "ln_g": sums[ROW_LN_G][None],
        "ln_b": sums[ROW_LN_B][None],
    }
    weights = dict(c_ctx=c_ctx, w_mod=w_mod, b_mod=b_mod, w_in=w_in, conv_w=conv_w, conv_b=conv_b, hg_lb=hg_lb,
                   ml_gate_b=ml_gate_b, hg_norm_w=hg_norm_w, ml_norm_w=ml_norm_w, w_out=w_out, ln_g=ln_g, ln_b=ln_b)
    mom1 = dict(c_ctx=m_c_ctx, w_mod=m_w_mod, b_mod=m_b_mod, w_in=m_w_in, conv_w=m_conv_w, conv_b=m_conv_b,
                hg_lb=m_hg_lb, ml_gate_b=m_ml_gate_b, hg_norm_w=m_hg_norm_w, ml_norm_w=m_ml_norm_w, w_out=m_w_out,
                ln_g=m_ln_g, ln_b=m_ln_b)
    mom2 = dict(c_ctx=v_c_ctx, w_mod=v_w_mod, b_mod=v_b_mod, w_in=v_w_in, conv_w=v_conv_w, conv_b=v_conv_b,
                hg_lb=v_hg_lb, ml_gate_b=v_ml_gate_b, hg_norm_w=v_hg_norm_w, ml_norm_w=v_ml_norm_w, w_out=v_w_out,
                ln_g=v_ln_g, ln_b=v_ln_b)
    names = list(weights)
    big = ("w_mod", "w_in", "w_out")
    small = [n for n in names if n not in big]

    delta, new_m, new_v = {}, {}, {}
    for n in ("w_mod", "w_out"):
        as2d = lambda a: a.reshape(a.shape[-2], a.shape[-1])
        res = _adamw("adamw_" + n, as2d(weights[n]), as2d(grads[n]), as2d(mom1[n]), as2d(mom2[n]))
        delta[n], new_m[n], new_v[n] = (a.reshape(weights[n].shape) for a in res)
    res = _adamw("adamw_w_in", as_t(w_in), g_wt_in, as_t(m_w_in), as_t(v_w_in))
    delta["w_in"], new_m["w_in"], new_v["w_in"] = (jnp.transpose(a)[None] for a in res)
    small_shapes = [weights[n].shape for n in small]
    res = _adamw("adamw_small", *(_pack([src[n] for n in small]) for src in (weights, grads, mom1, mom2)))
    for out, packed in zip((delta, new_m, new_v), res):
        for n, a in zip(small, _unpack(packed, small_shapes)):
            out[n] = a

    return (loss, grad_x, *[grads[n].reshape(weights[n].shape) for n in names], *[delta[n] for n in names],
            *[new_m[n] for n in names], *[new_v[n] for n in names])
```

```python
import functools
import math

import jax
import jax.numpy as jnp
from jax import lax
from jax.experimental import pallas as pl
from jax.experimental.pallas import tpu as pltpu

F32 = jnp.float32
BF16 = jnp.bfloat16
HIGHEST = lax.Precision.HIGHEST
MESH = pl.DeviceIdType.MESH

HG_CHUNK = 64
ML_CHUNK = 256
GRID_W = 64
HG_DK = 128
LANE = 128
SUBLANE_BF16 = 16
ALPHA = 2.0 ** 0.25
LN_EPS = 1e-5
NORM_EPS = 1e-6
ADAM_LR = 0.001
ADAM_B1 = 0.9
ADAM_B2 = 0.999
ADAM_EPS = 1e-08
ADAM_WD = 0.01
ADAM_STEP = 10
VMEM_LIMIT = 56 * 1024 * 1024
N_CHIPS = 4
N_DEV = 8


def _params(sem=None):
    return pltpu.CompilerParams(dimension_semantics=sem, vmem_limit_bytes=VMEM_LIMIT)


def _largest_divisor(n, cap, multiple=1):
    best = None
    for d in range(multiple, min(n, cap) + 1, multiple):
        if n % d == 0:
            best = d
    assert best is not None, (n, cap, multiple)
    return best


def _sigmoid(x):
    return jax.nn.sigmoid(x)


def _silu(x):
    return x * jax.nn.sigmoid(x)


def _dot(a, b, dims, precision=None):
    return lax.dot_general(a, b, (dims, ((), ())), precision=precision, preferred_element_type=F32)


def _nn(a, b, precision=None):
    return _dot(a, b, ((1,), (0,)), precision)


def _nt(a, b, precision=None):
    return _dot(a, b, ((1,), (1,)), precision)


def _tn(a, b, precision=None):
    return _dot(a, b, ((0,), (0,)), precision)


def _visible(n, rev):
    r = lax.broadcasted_iota(jnp.int32, (n, n), 0)
    c = lax.broadcasted_iota(jnp.int32, (n, n), 1)
    return (r <= c) if rev else (r >= c)


def _hg_chunk(states, aq, af, ai, lb0, lb1, rev):
    n_heads = len(states)
    lb = _sigmoid(lb0 - lb1)
    f = lb + (1.0 - lb) * _sigmoid(af)
    g = jnp.log(f)
    k = 1.0 - f
    q = _silu(aq)
    chunk = aq.shape[0]
    vis = _visible(chunk, rev)
    b = _nn(vis.astype(F32), g, HIGHEST)
    last = 0 if rev else chunk - 1
    b_end = b[last:last + 1]
    b_mid = b[chunk // 2:chunk // 2 + 1]
    q_inter = q * jnp.exp(b)
    q_intra = q * jnp.exp(b - b_mid)
    k_intra = k * jnp.exp(b_mid - b)
    k_dec = k * jnp.exp(b_end - b)
    e_end = jnp.exp(b_end)
    new_states, outs = [], []
    for h in range(n_heads):
        sl = slice(h * HG_DK, (h + 1) * HG_DK)
        s_t = states[h]
        scores = jnp.where(vis, _nt(q_intra[:, sl], k_intra[:, sl]), 0.0)
        outs.append(_nt(q_inter[:, sl], s_t) + _nn(scores, ai[:, sl]))
        new_states.append(e_end[:, sl] * s_t + _tn(ai[:, sl], k_dec[:, sl]))
    return new_states, jnp.concatenate(outs, axis=1)


def _ml_chunk(state, q, k, v, g, gb, rev, d):
    cms, nvs, mbs = state
    n_heads = len(cms)
    dh = q.shape[1] // n_heads
    ga = g + gb
    log_f_all = jax.nn.log_sigmoid(ga)
    chunk = q.shape[0]
    vis = _visible(chunk, rev)
    b_all = _nn(vis.astype(F32), log_f_all, HIGHEST)
    last = 0 if rev else chunk - 1
    k = k * (dh ** -0.5)
    new_c, new_n, new_m, outs = [], [], [], []
    for h in range(n_heads):
        ci = d * n_heads + h
        cf = (2 + d) * n_heads + h
        sl = slice(h * dh, (h + 1) * dh)
        qh, kh, vh = q[:, sl], k[:, sl], v[:, sl]
        li = ga[:, ci:ci + 1]
        b = b_all[:, cf:cf + 1]
        m = mbs[h][:, 0:1]
        row = jnp.transpose(li - b)
        log_w = jnp.where(vis, b + row, -jnp.inf)
        m_inter = b + m
        m_t = jnp.maximum(m_inter, jnp.max(log_w, axis=-1, keepdims=True))
        w_inter = jnp.exp(m_inter - m_t)
        w_qk = jnp.exp(log_w - m_t) * _nt(qh, kh)
        num = w_inter * _nt(qh, cms[h]) + _nn(w_qk, vh)
        den = w_inter * jnp.sum(qh * nvs[h], axis=-1, keepdims=True) + jnp.sum(w_qk, axis=-1, keepdims=True)
        outs.append(num / jnp.maximum(jnp.abs(den), jnp.exp(-m_t)))
        m_new = m_t[last:last + 1]
        b_end = b[last:last + 1]
        w_s = jnp.exp(b_end - b + li - m_new)
        decay = jnp.exp(b_end + m - m_new)
        new_c.append(decay * cms[h] + _tn(w_s * vh, kh))
        new_n.append(decay * nvs[h] + jnp.sum(w_s * kh, axis=0, keepdims=True))
        new_m.append(jnp.broadcast_to(m_new, (1, LANE)))
    return (new_c, new_n, new_m), jnp.concatenate(outs, axis=1)


def _post_fn(o_f, o_b, az, h_f, h_b, bo, bz, wa, wb, n_hg, n_ml):
    o = o_f + o_b
    parts = []
    for h in range(n_hg):
        s = o[:, h * HG_DK:(h + 1) * HG_DK]
        parts.append(s * lax.rsqrt(jnp.mean(s * s, axis=-1, keepdims=True) + NORM_EPS))
    y_a = jnp.concatenate(parts, axis=1) * wa * _silu(az)
    hh = h_f + h_b
    dh = hh.shape[1] // n_ml
    parts = []
    for h in range(n_ml):
        s = hh[:, h * dh:(h + 1) * dh]
        mu = jnp.mean(s, axis=-1, keepdims=True)
        sc = s - mu
        parts.append(sc * lax.rsqrt(jnp.mean(sc * sc, axis=-1, keepdims=True) + NORM_EPS))
    y_b = jnp.concatenate(parts, axis=1) * wb * _sigmoid(bo) * _silu(bz)
    return jnp.concatenate([y_a, y_b], axis=1)


def _chip_of(dev):
    return 2 * dev[0] + dev[1]


def _index_of(dev):
    return 4 * dev[0] + 2 * dev[1] + dev[2]


def _exchange(name, srcs, out_shapes, transfers, local_copies=()):
    n_in, n_out, n_t, n_l = len(srcs), len(out_shapes), len(transfers), len(local_copies)

    def body(*refs):
        ins, outs = refs[:n_in], refs[n_in:n_in + n_out]
        send_sems, recv_sems, local_sems = refs[n_in + n_out:]
        me = (lax.axis_index("x"), lax.axis_index("y"), lax.axis_index("c"))

        def pick(ref, fn, *who):
            return ref if fn is None else ref.at[fn(*who)]

        sends, recvs, locs = [], [], []
        for t, (mask, si, sfn, di, dfn) in enumerate(transfers):
            peer = tuple(1 - p if flip else p for p, flip in zip(me, mask))
            sends.append(pltpu.make_async_remote_copy(
                src_ref=pick(ins[si], sfn, me, peer), dst_ref=pick(outs[di], dfn, me, peer),
                send_sem=send_sems.at[t], recv_sem=recv_sems.at[t], device_id=peer, device_id_type=MESH))
            landing = pick(outs[di], dfn, peer, me)
            recvs.append(pltpu.make_async_remote_copy(
                src_ref=landing, dst_ref=landing,
                send_sem=send_sems.at[t], recv_sem=recv_sems.at[t], device_id=peer, device_id_type=MESH))
        for l, (si, sfn, di, dfn) in enumerate(local_copies):
            locs.append(pltpu.make_async_copy(pick(ins[si], sfn, me), pick(outs[di], dfn, me), local_sems.at[l]))
        for cp in locs + sends:
            cp.start()
        for cp in recvs:
            cp.wait_recv()
        for cp in sends:
            cp.wait_send()
        for cp in locs:
            cp.wait()

    hbm = pl.BlockSpec(memory_space=pltpu.HBM)
    return pl.pallas_call(
        body, name=name, out_shape=tuple(out_shapes),
        in_specs=[hbm] * n_in, out_specs=tuple([hbm] * n_out),
        scratch_shapes=[pltpu.SemaphoreType.DMA((n_t,)), pltpu.SemaphoreType.DMA((n_t,)),
                        pltpu.SemaphoreType.DMA((max(n_l, 1),))],
    )(*srcs)


ALL_MASKS = [(mx, my, mc) for mx in (0, 1) for my in (0, 1) for mc in (0, 1)][1:]
CHIP_MASKS = [(1, 0, 0), (0, 1, 0), (1, 1, 0)]
SIBLING_MASK = (0, 0, 1)


def _all_gather8(name, v):
    out = jax.ShapeDtypeStruct((N_DEV,) + v.shape, v.dtype)
    slot = lambda sender, receiver: _index_of(sender)
    transfers = [(mask, 0, None, 0, slot) for mask in ALL_MASKS]
    return _exchange(name, [v], [out], transfers, [(0, None, 0, lambda me: _index_of(me))])[0]


def _all_gather_chips(name, arrays):
    outs = [jax.ShapeDtypeStruct((N_CHIPS,) + a.shape, a.dtype) for a in arrays]
    slot = lambda sender, receiver: _chip_of(sender)
    transfers = [(mask, i, None, i, slot) for i in range(len(arrays)) for mask in CHIP_MASKS]
    return _exchange(name, arrays, outs, transfers)


def _sibling_swap(name, arrays):
    outs = [jax.ShapeDtypeStruct(a.shape, a.dtype) for a in arrays]
    return _exchange(name, arrays, outs, [(SIBLING_MASK, i, None, i, None) for i in range(len(arrays))])


def _chip_scatter(name, arrays):
    outs = [jax.ShapeDtypeStruct(a.shape, a.dtype) for a in arrays]
    transfers = [(mask, i, lambda s, r: _chip_of(r), i, lambda s, r: _chip_of(s))
                 for i in range(len(arrays)) for mask in CHIP_MASKS]
    return _exchange(name, arrays, outs, transfers)


def _own_block(chip, own, blocks):
    sel = (lax.broadcasted_iota(jnp.int32, (N_CHIPS,) + (1,) * (blocks.ndim - 1), 0) == chip)
    return jnp.where(sel, own if own.ndim == blocks.ndim else own[None], blocks)


def _join_halves(ci, mine, other, axis):
    return jnp.where(ci == 0, jnp.concatenate([mine, other], axis=axis), jnp.concatenate([other, mine], axis=axis))


def _mm_nt(name, a, b, tm, tn, out_dtype):
    m, k = a.shape
    n = b.shape[0]

    def body(a_ref, b_ref, o_ref):
        o_ref[...] = _nt(a_ref[...], b_ref[...]).astype(out_dtype)

    return pl.pallas_call(
        body, name=name, grid=(n // tn, m // tm),
        in_specs=[pl.BlockSpec((tm, k), lambda j, i: (i, 0)), pl.BlockSpec((tn, k), lambda j, i: (j, 0))],
        out_specs=pl.BlockSpec((tm, tn), lambda j, i: (i, j)),
        out_shape=jax.ShapeDtypeStruct((m, n), out_dtype),
        compiler_params=_params(("parallel", "parallel")),
    )(a, b)


def _chip_scatter_copies(src_ref, dst_ref, send_sems, recv_sems):
    me = (lax.axis_index("x"), lax.axis_index("y"), lax.axis_index("c"))
    sends, recvs = [], []
    for t, mask in enumerate(CHIP_MASKS):
        peer = tuple(1 - p if flip else p for p, flip in zip(me, mask))
        sends.append(pltpu.make_async_remote_copy(
            src_ref=src_ref.at[_chip_of(peer)], dst_ref=dst_ref.at[_chip_of(me)],
            send_sem=send_sems.at[t], recv_sem=recv_sems.at[t], device_id=peer, device_id_type=MESH))
        landing = dst_ref.at[_chip_of(peer)]
        recvs.append(pltpu.make_async_remote_copy(
            src_ref=landing, dst_ref=landing,
            send_sem=send_sems.at[t], recv_sem=recv_sems.at[t], device_id=peer, device_id_type=MESH))
    return sends, recvs


def _mm_acc_scatter(name, a, b, tm, tk, pieces):
    m, kc = a.shape
    n = b.shape[1]
    steps_m, steps_k = m // tm, kc // tk

    def body(a_ref, b_ref, p_ref, o_ref, landed_ref, send_sems, recv_sems):
        i, kk = pl.program_id(0), pl.program_id(1)
        sends, recvs = _chip_scatter_copies(p_ref, landed_ref, send_sems, recv_sems)

        @pl.when((i == 0) & (kk == 0))
        def _():
            for cp in sends:
                cp.start()

        @pl.when(kk == 0)
        def _():
            o_ref[...] = jnp.zeros_like(o_ref)
        o_ref[...] += _nn(a_ref[...], b_ref[...])

        @pl.when((i == steps_m - 1) & (kk == steps_k - 1))
        def _():
            for cp in recvs:
                cp.wait_recv()
            for cp in sends:
                cp.wait_send()

    hbm = pl.BlockSpec(memory_space=pltpu.HBM)
    return pl.pallas_call(
        body, name=name, grid=(steps_m, steps_k),
        in_specs=[pl.BlockSpec((tm, tk), lambda i, kk: (i, kk)), pl.BlockSpec((tk, n), lambda i, kk: (kk, 0)), hbm],
        out_specs=(pl.BlockSpec((tm, n), lambda i, kk: (i, 0)), hbm),
        out_shape=(jax.ShapeDtypeStruct((m, n), F32), jax.ShapeDtypeStruct(pieces.shape, pieces.dtype)),
        scratch_shapes=[pltpu.SemaphoreType.DMA((len(CHIP_MASKS),)), pltpu.SemaphoreType.DMA((len(CHIP_MASKS),))],
        compiler_params=_params(("arbitrary", "arbitrary")),
    )(a, b, pieces)


def _mm_tn(name, a, b, tm, tk):
    kr, m = a.shape
    n = b.shape[1]

    def body(a_ref, b_ref, o_ref):
        @pl.when(pl.program_id(1) == 0)
        def _():
            o_ref[...] = jnp.zeros_like(o_ref)
        o_ref[...] += _tn(a_ref[...], b_ref[...])

    return pl.pallas_call(
        body, name=name, grid=(m // tm, kr // tk),
        in_specs=[pl.BlockSpec((tk, tm), lambda i, kk: (kk, i)), pl.BlockSpec((tk, n), lambda i, kk: (kk, 0))],
        out_specs=pl.BlockSpec((tm, n), lambda i, kk: (i, 0)),
        out_shape=jax.ShapeDtypeStruct((m, n), F32),
        compiler_params=_params(("parallel", "arbitrary")),
    )(a, b)


def _modulate_fwd(x, ctx, prm, tm):
    t_rows, dm = x.shape
    lat = t_rows // tm
    r = t_rows + ctx.shape[0]

    def body(x_ref, c_ref, p_ref, h_ref):
        xv = jnp.where(pl.program_id(0) >= lat, c_ref[...], x_ref[...])
        mu = jnp.mean(xv, axis=-1, keepdims=True)
        xm = xv - mu
        n = xm * lax.rsqrt(jnp.mean(xm * xm, axis=-1, keepdims=True) + LN_EPS)
        h_ref[...] = (n * (1.0 + p_ref[0, 1:2, :]) + p_ref[0, 0:1, :]).astype(BF16)

    return pl.pallas_call(
        body, name="modulate_fwd", grid=(r // tm,),
        in_specs=[pl.BlockSpec((tm, dm), lambda i: (jnp.minimum(i, lat - 1), 0)),
                  pl.BlockSpec((tm, dm), lambda i: (jnp.maximum(i - lat, 0), 0)),
                  pl.BlockSpec((1, 8, dm), lambda i: ((i >= lat).astype(jnp.int32), 0, 0))],
        out_specs=pl.BlockSpec((tm, dm), lambda i: (i, 0)),
        out_shape=jax.ShapeDtypeStruct((r, dm), BF16),
        compiler_params=_params(("parallel",)),
    )(x, ctx, prm)


def _modulate_bwd(x, ctx, dh, prm, gx_direct, tm):
    t_rows, dm = x.shape
    lat, n_ct = t_rows // tm, ctx.shape[0] // tm
    is_ctx = lambda i: i < n_ct
    cls = lambda i: is_ctx(i).astype(jnp.int32)
    lat_tile = lambda i: (jnp.maximum(i - n_ct, 0), 0)

    def body(x_ref, c_ref, dh_ref, p_ref, gd_ref, gx_ref, acc_ref):
        i = pl.program_id(0)

        @pl.when((i == 0) | (i == n_ct))
        def _():
            acc_ref[...] = jnp.zeros_like(acc_ref)

        x = jnp.where(is_ctx(i), c_ref[...], x_ref[...])
        dh_v = dh_ref[...]
        mu = jnp.mean(x, axis=-1, keepdims=True)
        xm = x - mu
        rstd = lax.rsqrt(jnp.mean(xm * xm, axis=-1, keepdims=True) + LN_EPS)
        n = xm * rstd
        acc_ref[0, 0:1, :] += jnp.sum(dh_v, axis=0, keepdims=True)
        acc_ref[0, 1:2, :] += jnp.sum(dh_v * n, axis=0, keepdims=True)
        dn = dh_v * (1.0 + p_ref[0, 1:2, :])
        dx = rstd * (dn - jnp.mean(dn, axis=-1, keepdims=True) - n * jnp.mean(dn * n, axis=-1, keepdims=True))
        gx_ref[...] = dx + gd_ref[...]

    return pl.pallas_call(
        body, name="modulate_bwd", grid=(n_ct + lat,),
        in_specs=[pl.BlockSpec((tm, dm), lat_tile),
                  pl.BlockSpec((tm, dm), lambda i: (jnp.minimum(i, n_ct - 1), 0)),
                  pl.BlockSpec((tm, dm), lambda i: (jnp.where(is_ctx(i), lat + i, i - n_ct), 0)),
                  pl.BlockSpec((1, 8, dm), lambda i: (cls(i), 0, 0)),
                  pl.BlockSpec((tm, dm), lat_tile)],
        out_specs=(pl.BlockSpec((tm, dm), lat_tile), pl.BlockSpec((1, 8, dm), lambda i: (cls(i), 0, 0))),
        out_shape=(jax.ShapeDtypeStruct((t_rows, dm), F32), jax.ShapeDtypeStruct((2, 8, dm), F32)),
        compiler_params=_params(("arbitrary",)),
    )(x, ctx, dh, prm, gx_direct)


def _conv_parts(t_rows, c_rows):
    return ((0, t_rows, t_rows // GRID_W, GRID_W), (t_rows, c_rows, 1, c_rows))


def _col_shifts(x2, rows_g, width_g):
    n, ct = x2.shape
    col = lax.broadcasted_iota(jnp.int32, (width_g, ct), 0)
    as_grid = lambda a: a.reshape(rows_g, width_g, ct)
    left = as_grid(pltpu.roll(x2, 1, 0)) * (col >= 1).astype(F32)
    right = as_grid(pltpu.roll(x2, n - 1, 0)) * (col <= width_g - 2).astype(F32)
    return [left, as_grid(x2), right]


def _row_shift(y3, a):
    if a == 1:
        return y3
    if y3.shape[0] == 1:
        return jnp.zeros_like(y3)
    zero = jnp.zeros_like(y3[:1])
    return jnp.concatenate([zero, y3[:-1]], axis=0) if a == 0 else jnp.concatenate([y3[1:], zero], axis=0)


def _conv_taps(cols, w_ref, flip):
    rows_g = cols[0].shape[0]
    acc = None
    for a in range(3):
        if rows_g == 1 and a != 1:
            continue
        inner = None
        for b in range(3):
            tap = (2 - a) * 3 + (2 - b) if flip else a * 3 + b
            term = cols[b] * w_ref[tap:tap + 1, :]
            inner = term if inner is None else inner + term
        inner = _row_shift(inner, a)
        acc = inner if acc is None else acc + inner
    return acc


def _conv_fwd(u, conv_w9, conv_b, t_rows, c_rows, w, ct):
    r = u.shape[0]
    base = 5 * w // ct

    def body(x_ref, w_ref, b_ref, o_ref):
        for r0, n, rows_g, width_g in _conv_parts(t_rows, c_rows):
            pre = _conv_taps(_col_shifts(x_ref[r0:r0 + n, :], rows_g, width_g), w_ref, False) + b_ref[...]
            o_ref[r0:r0 + n, :] = _silu(pre).reshape(n, ct)

    return pl.pallas_call(
        body, name="conv_fwd", grid=(2 * w // ct,),
        in_specs=[pl.BlockSpec((r, ct), lambda i: (0, base + i)), pl.BlockSpec((9, ct), lambda i: (0, i)),
                  pl.BlockSpec((1, ct), lambda i: (0, i))],
        out_specs=pl.BlockSpec((r, ct), lambda i: (0, i)),
        out_shape=jax.ShapeDtypeStruct((r, 2 * w), F32),
        compiler_params=_params(("parallel",)),
    )(u, conv_w9, conv_b)


def _conv_bwd(u, dqk_pair, conv_w9, conv_b, t_rows, c_rows, w, ct):
    r = u.shape[0]
    base = 5 * w // ct

    def body(x_ref, d1_ref, d2_ref, w_ref, b_ref, dx_ref, dw_ref, db_ref):
        dw = [jnp.zeros((1, ct), F32) for _ in range(9)]
        db = jnp.zeros((1, ct), F32)
        for r0, n, rows_g, width_g in _conv_parts(t_rows, c_rows):
            cols = _col_shifts(x_ref[r0:r0 + n, :], rows_g, width_g)
            pre = (_conv_taps(cols, w_ref, False) + b_ref[...]).reshape(n, ct)
            sg = _sigmoid(pre)
            dpre = (d1_ref[r0:r0 + n, :] + d2_ref[r0:r0 + n, :]) * (sg * (1.0 + pre * (1.0 - sg)))
            db = db + jnp.sum(dpre, axis=0, keepdims=True)
            dx_ref[r0:r0 + n, :] = _conv_taps(_col_shifts(dpre, rows_g, width_g), w_ref, True).reshape(n, ct)
            dpre3 = dpre.reshape(rows_g, width_g, ct)
            for a in range(3):
                if rows_g == 1 and a != 1:
                    continue
                moved = _row_shift(dpre3, 2 - a)
                for b in range(3):
                    prod = jnp.sum(cols[b] * moved, axis=0)
                    dw[a * 3 + b] = dw[a * 3 + b] + jnp.sum(prod, axis=0, keepdims=True)
        for tap in range(9):
            dw_ref[tap:tap + 1, :] = dw[tap]
        db_ref[...] = db

    return pl.pallas_call(
        body, name="conv_bwd", grid=(2 * w // ct,),
        in_specs=[pl.BlockSpec((r, ct), lambda i: (0, base + i)), pl.BlockSpec((r, ct), lambda i: (0, i)),
                  pl.BlockSpec((r, ct), lambda i: (0, i)),
                  pl.BlockSpec((9, ct), lambda i: (0, i)), pl.BlockSpec((1, ct), lambda i: (0, i))],
        out_specs=(pl.BlockSpec((r, ct), lambda i: (0, i)), pl.BlockSpec((9, ct), lambda i: (0, i)),
                   pl.BlockSpec((1, ct), lambda i: (0, i))),
        out_shape=(jax.ShapeDtypeStruct((r, 2 * w), F32), jax.ShapeDtypeStruct((9, 2 * w), F32),
                   jax.ShapeDtypeStruct((1, 2 * w), F32)),
        compiler_params=_params(("parallel",)),
    )(u, dqk_pair[0], dqk_pair[1], conv_w9, conv_b)


def _assemble_du(groups, gates, n_pad, tm):
    flat, layout = [], []
    for entry in list(groups) + [gates]:
        parts = entry if isinstance(entry, (tuple, list)) else (entry,)
        layout.append((len(flat), len(parts), parts[0].shape[1]))
        flat += list(parts)
    r = flat[0].shape[0]

    def body(*refs):
        o_ref = refs[-1]
        col = 0
        for first, count, width in layout:
            val = refs[first][...]
            for extra in range(1, count):
                val = val + refs[first + extra][...]
            o_ref[:, col:col + width] = val.astype(BF16)
            col += width
        assert col == n_pad

    return pl.pallas_call(
        body, name="assemble_du", grid=(r // tm,),
        in_specs=[pl.BlockSpec((tm, a.shape[1]), lambda i: (i, 0)) for a in flat],
        out_specs=pl.BlockSpec((tm, n_pad), lambda i: (i, 0)),
        out_shape=jax.ShapeDtypeStruct((r, n_pad), BF16),
        compiler_params=_params(("parallel",)),
    )(*flat)


def _scan_order(n_lat, n_ctx, rev):
    n = n_lat + n_ctx
    if rev:
        return lambda j: n - 1 - j
    return lambda j: (j + n_lat) % n


DIRS = (False, True)


def _hg_scan_fwd(u, lb_full, w, n_lat, n_ctx, chunk):
    r = u.shape[0]
    n_heads = w // HG_DK
    n_chunks = n_lat + n_ctx
    nat = [_scan_order(n_lat, n_ctx, rev) for rev in DIRS]

    def body(*refs):
        ins, outs, scratch = refs[:8], refs[8:12], refs[12:]

        @pl.when(pl.program_id(0) == 0)
        def _():
            for s_ref in scratch:
                s_ref[...] = jnp.zeros_like(s_ref)

        results = []
        for d, rev in enumerate(DIRS):
            aq, af, ai, lb_ref = ins[4 * d:4 * d + 4]
            state = [scratch[d][h] for h in range(n_heads)]
            results.append((state, _hg_chunk(state, aq[...], af[...], ai[...],
                                             lb_ref[0, 0:1, :], lb_ref[0, 1:2, :], rev)))
        for d, (state, (new, o)) in enumerate(results):
            o_ref, save_ref = outs[2 * d:2 * d + 2]
            o_ref[...] = o
            for h in range(n_heads):
                save_ref[0, h] = state[h]
                scratch[d][h] = new[h]

    in_specs, out_specs, out_shape = [], [], []
    for d in range(2):
        in_specs += [pl.BlockSpec((chunk,w), lambda j, d=d: (nat[d](j), 0)),
                     pl.BlockSpec((chunk,w), lambda j, d=d: (nat[d](j), 1 + d)),
                     pl.BlockSpec((chunk,w), lambda j, d=d: (nat[d](j), 3)),
                     pl.BlockSpec((1, 2, w), lambda j, d=d: (d, 0, 0))]
        out_specs += [pl.BlockSpec((chunk,w), lambda j, d=d: (nat[d](j), 0)),
                      pl.BlockSpec((1, n_heads, HG_DK, HG_DK), lambda j: (j, 0, 0, 0))]
        out_shape += [jax.ShapeDtypeStruct((r, w), F32),
                      jax.ShapeDtypeStruct((n_chunks, n_heads, HG_DK, HG_DK), F32)]
    o_f, s_f, o_b, s_b = pl.pallas_call(
        body, name="hg_scan_fwd", grid=(n_chunks,), in_specs=in_specs, out_specs=tuple(out_specs),
        out_shape=tuple(out_shape), scratch_shapes=[pltpu.VMEM((n_heads, HG_DK, HG_DK), F32)] * 2,
        compiler_params=_params(("arbitrary",)),
    )(u, u, u, lb_full, u, u, u, lb_full)
    return (o_f, o_b), (s_f, s_b)


def _hg_scan_bwd(u, lb_full, saved, d_o, w, n_lat, n_ctx, chunk):
    r = u.shape[0]
    n_heads = w // HG_DK
    n_chunks = n_lat + n_ctx
    step = lambda jj: n_chunks - 1 - jj
    nat = [(lambda jj, o=_scan_order(n_lat, n_ctx, rev): o(step(jj))) for rev in DIRS]

    def body(*refs):
        ins, outs, scratch = refs[:12], refs[12:20], refs[20:]
        jj = pl.program_id(0)

        @pl.when(jj == 0)
        def _():
            for d in range(2):
                scratch[d][...] = jnp.zeros_like(scratch[d])
                outs[4 * d + 3][...] = jnp.zeros_like(outs[4 * d + 3])

        results = []
        for d, rev in enumerate(DIRS):
            aq, af, ai, lb_ref, save_ref, do_ref = ins[6 * d:6 * d + 6]
            f = lambda st, a, b, c, l0, l1, rev=rev: _hg_chunk(st, a, b, c, l0, l1, rev)
            _, vjp = jax.vjp(f, [save_ref[0, h] for h in range(n_heads)], aq[...], af[...], ai[...],
                             lb_ref[0, 0:1, :], lb_ref[0, 1:2, :])
            d_out = do_ref[...] * (nat[d](jj) < n_lat).astype(F32)
            results.append(vjp(([scratch[d][h] for h in range(n_heads)], d_out)))
        for d, (dst, daq, daf, dai, dl0, dl1) in enumerate(results):
            daq_ref, daf_ref, dai_ref, dlb_ref = outs[4 * d:4 * d + 4]
            for h in range(n_heads):
                scratch[d][h] = dst[h]
            daq_ref[...] = daq
            daf_ref[...] = daf
            dai_ref[...] = dai
            dlb_ref[0:1, :] += dl0
            dlb_ref[1:2, :] += dl1

    in_specs, out_specs, out_shape, operands = [], [], [], []
    for d in range(2):
        row = lambda jj, d=d: (nat[d](jj), 0)
        in_specs += [pl.BlockSpec((chunk,w), row),
                     pl.BlockSpec((chunk,w), lambda jj, d=d: (nat[d](jj), 1 + d)),
                     pl.BlockSpec((chunk,w), lambda jj, d=d: (nat[d](jj), 3)),
                     pl.BlockSpec((1, 2, w), lambda jj, d=d: (d, 0, 0)),
                     pl.BlockSpec((1, n_heads, HG_DK, HG_DK), lambda jj: (step(jj), 0, 0, 0)),
                     pl.BlockSpec((chunk,w), lambda jj, d=d: (jnp.minimum(nat[d](jj), n_lat - 1), 0))]
        operands += [u, u, u, lb_full, saved[d], d_o]
        out_specs += [pl.BlockSpec((chunk,w), row)] * 3 + [pl.BlockSpec((2, w), lambda jj: (0, 0))]
        out_shape += [jax.ShapeDtypeStruct((r, w), F32)] * 3 + [jax.ShapeDtypeStruct((2, w), F32)]
    res = pl.pallas_call(
        body, name="hg_scan_bwd", grid=(n_chunks,), in_specs=in_specs, out_specs=tuple(out_specs),
        out_shape=tuple(out_shape), scratch_shapes=[pltpu.VMEM((n_heads, HG_DK, HG_DK), F32)] * 2,
        compiler_params=_params(("arbitrary",)),
    )(*operands)
    return res[0:4], res[4:8]


def _ml_state_shapes(n_chunks, n_heads, dh):
    return (jax.ShapeDtypeStruct((n_chunks, n_heads, dh, dh), F32),
            jax.ShapeDtypeStruct((n_chunks, n_heads, 1, dh), F32),
            jax.ShapeDtypeStruct((n_chunks, n_heads, 1, LANE), F32))


def _ml_state_specs(n_heads, dh, index):
    return (pl.BlockSpec((1, n_heads, dh, dh), lambda j: (index(j), 0, 0, 0)),
            pl.BlockSpec((1, n_heads, 1, dh), lambda j: (index(j), 0, 0, 0)),
            pl.BlockSpec((1, n_heads, 1, LANE), lambda j: (index(j), 0, 0, 0)))


def _ml_state_scratch(n_heads, dh):
    return [pltpu.VMEM((n_heads, dh, dh), F32), pltpu.VMEM((n_heads, 1, dh), F32), pltpu.VMEM((n_heads, 1, LANE), F32)]


def _ml_scan_fwd(qk, u, gate_b, w, n_heads, n_lat, n_ctx, chunk):
    r = u.shape[0]
    dh = w // n_heads
    n_chunks = n_lat + n_ctx
    nat = [_scan_order(n_lat, n_ctx, rev) for rev in DIRS]

    def body(*refs):
        ins, outs, scratch = refs[:10], refs[10:18], refs[18:]

        @pl.when(pl.program_id(0) == 0)
        def _():
            for s_ref in scratch:
                s_ref[...] = jnp.zeros_like(s_ref)

        results = []
        for d, rev in enumerate(DIRS):
            q, k, v, g, gb = ins[5 * d:5 * d + 5]
            state = tuple([ref[h] for h in range(n_heads)] for ref in scratch[3 * d:3 * d + 3])
            results.append((state, _ml_chunk(state, q[...], k[...], v[...], g[...], gb[...], rev, d)))
        for d, (state, (new, o)) in enumerate(results):
            outs[4 * d][...] = o
            for part in range(3):
                for h in range(n_heads):
                    outs[4 * d + 1 + part][0, h] = state[part][h]
                    scratch[3 * d + part][h] = new[part][h]

    in_specs, out_specs, out_shape = [], [], []
    for d in range(2):
        in_specs += [pl.BlockSpec((chunk,w), lambda j, d=d: (nat[d](j), 0)),
                     pl.BlockSpec((chunk,w), lambda j, d=d: (nat[d](j), 1)),
                     pl.BlockSpec((chunk,w), lambda j, d=d: (nat[d](j), 7)),
                     pl.BlockSpec((chunk,LANE), lambda j, d=d: (nat[d](j), 10 * w // LANE)),
                     pl.BlockSpec((1, LANE), lambda j: (0, 0))]
        out_specs += [pl.BlockSpec((chunk,w), lambda j, d=d: (nat[d](j), 0))]
        out_specs += list(_ml_state_specs(n_heads, dh, lambda j: j))
        out_shape += [jax.ShapeDtypeStruct((r, w), F32)] + list(_ml_state_shapes(n_chunks, n_heads, dh))
    res = pl.pallas_call(
        body, name="ml_scan_fwd", grid=(n_chunks,), in_specs=in_specs, out_specs=tuple(out_specs),
        out_shape=tuple(out_shape), scratch_shapes=_ml_state_scratch(n_heads, dh) * 2,
        compiler_params=_params(("arbitrary",)),
    )(qk, qk, u, u, gate_b, qk, qk, u, u, gate_b)
    return (res[0], res[4]), (res[1:4], res[5:8])


def _ml_scan_bwd(qk, u, gate_b, saved, d_h, w, n_heads, n_lat, n_ctx, chunk):
    r = u.shape[0]
    dh = w // n_heads
    n_chunks = n_lat + n_ctx
    step = lambda jj: n_chunks - 1 - jj
    nat = [(lambda jj, o=_scan_order(n_lat, n_ctx, rev): o(step(jj))) for rev in DIRS]

    def body(*refs):
        ins, outs, scratch = refs[:18], refs[18:26], refs[26:]
        jj = pl.program_id(0)

        @pl.when(jj == 0)
        def _():
            for s_ref in scratch:
                s_ref[...] = jnp.zeros_like(s_ref)
            for d in range(2):
                outs[4 * d + 3][...] = jnp.zeros_like(outs[4 * d + 3])

        results = []
        for d, rev in enumerate(DIRS):
            q, k, v, g, gb, sc, sn, sm, dh_ref = ins[9 * d:9 * d + 9]
            state = tuple([ref[0, h] for h in range(n_heads)] for ref in (sc, sn, sm))
            f = lambda st, a, b, c, gg, bb, rev=rev, d=d: _ml_chunk(st, a, b, c, gg, bb, rev, d)
            _, vjp = jax.vjp(f, state, q[...], k[...], v[...], g[...], gb[...])
            d_state = tuple([ref[h] for h in range(n_heads)] for ref in scratch[3 * d:3 * d + 3])
            d_out = dh_ref[...] * (nat[d](jj) < n_lat).astype(F32)
            results.append(vjp((d_state, d_out)))
        for d, (d_state, dq, dk, dv, dg, dgb) in enumerate(results):
            dqk_ref, dv_ref, dg_ref, dgb_ref = outs[4 * d:4 * d + 4]
            for part in range(3):
                for h in range(n_heads):
                    scratch[3 * d + part][h] = d_state[part][h]
            dqk_ref[:, 0:w] = dq
            dqk_ref[:, w:2 * w] = dk
            dv_ref[...] = dv
            dg_ref[...] = dg
            dgb_ref[...] += dgb

    in_specs, out_specs, out_shape, operands = [], [], [], []
    for d in range(2):
        row = lambda jj, d=d: (nat[d](jj), 0)
        in_specs += [pl.BlockSpec((chunk,w), row), pl.BlockSpec((chunk,w), lambda jj, d=d: (nat[d](jj), 1)),
                     pl.BlockSpec((chunk,w), lambda jj, d=d: (nat[d](jj), 7)),
                     pl.BlockSpec((chunk,LANE), lambda jj, d=d: (nat[d](jj), 10 * w // LANE)),
                     pl.BlockSpec((1, LANE), lambda jj: (0, 0))]
        in_specs += list(_ml_state_specs(n_heads, dh, step))
        in_specs += [pl.BlockSpec((chunk,w), lambda jj, d=d: (jnp.minimum(nat[d](jj), n_lat - 1), 0))]
        operands += [qk, qk, u, u, gate_b, *saved[d], d_h]
        out_specs += [pl.BlockSpec((chunk,2 * w), row), pl.BlockSpec((chunk,w), row),
                      pl.BlockSpec((chunk,LANE), row), pl.BlockSpec((1, LANE), lambda jj: (0, 0))]
        out_shape += [jax.ShapeDtypeStruct((r, 2 * w), F32), jax.ShapeDtypeStruct((r, w), F32),
                      jax.ShapeDtypeStruct((r, LANE), F32), jax.ShapeDtypeStruct((1, LANE), F32)]
    res = pl.pallas_call(
        body, name="ml_scan_bwd", grid=(n_chunks,), in_specs=in_specs, out_specs=tuple(out_specs),
        out_shape=tuple(out_shape), scratch_shapes=_ml_state_scratch(n_heads, dh) * 2,
        compiler_params=_params(("arbitrary",)),
    )(*operands)
    return res[0:4], res[4:8]


def _post_specs(w, tm, lat_tiles, cols):
    return [pl.BlockSpec((tm, w), (lambda i, cb=cb: (jnp.minimum(i, lat_tiles - 1), cb))) for cb in cols]


def _post_fwd(o_f, o_b, h_f, h_b, u, wa, wb, t_rows, w, n_hg, n_ml, tm):
    lat_tiles = t_rows // tm

    def body(of, ob, hf, hb, az, bo, bz, wa_ref, wb_ref, y_ref):
        y_ref[...] = _post_fn(of[...], ob[...], az[...], hf[...], hb[...], bo[...], bz[...],
                              wa_ref[...], wb_ref[...], n_hg, n_ml).astype(BF16)

    rows = pl.BlockSpec((tm, w), lambda i: (i, 0))
    vec = pl.BlockSpec((1, w), lambda i: (0, 0))
    return pl.pallas_call(
        body, name="post_fwd", grid=(lat_tiles,),
        in_specs=[rows] * 4 + _post_specs(w, tm, lat_tiles, (4, 8, 9)) + [vec, vec],
        out_specs=pl.BlockSpec((tm, 2 * w), lambda i: (i, 0)),
        out_shape=jax.ShapeDtypeStruct((t_rows, 2 * w), BF16),
        compiler_params=_params(("parallel",)),
    )(o_f, o_b, h_f, h_b, u, u, u, wa, wb)


def _post_bwd(o_f, o_b, h_f, h_b, u, wa, wb, dy, t_rows, w, n_hg, n_ml, tm):
    r = u.shape[0]
    lat_tiles = t_rows // tm
    lat = lambda i: (jnp.minimum(i, lat_tiles - 1), 0)

    def body(of, ob, hf, hb, az, bo, bz, wa_ref, wb_ref, dy_ref, do_ref, dh_ref, daz_ref, dbo_ref, dbz_ref,
             dwa_ref, dwb_ref):
        i = pl.program_id(0)

        @pl.when(i == 0)
        def _():
            dwa_ref[...] = jnp.zeros_like(dwa_ref)
            dwb_ref[...] = jnp.zeros_like(dwb_ref)

        @pl.when(i < lat_tiles)
        def _():
            f = functools.partial(_post_fn, n_hg=n_hg, n_ml=n_ml)
            _, vjp = jax.vjp(f, of[...], ob[...], az[...], hf[...], hb[...], bo[...], bz[...], wa_ref[...], wb_ref[...])
            d_of, _, d_az, d_hf, _, d_bo, d_bz, d_wa, d_wb = vjp(dy_ref[...])
            do_ref[...] = d_of
            dh_ref[...] = d_hf
            daz_ref[...] = d_az
            dbo_ref[...] = d_bo
            dbz_ref[...] = d_bz
            dwa_ref[...] += d_wa
            dwb_ref[...] += d_wb

        @pl.when(i >= lat_tiles)
        def _():
            daz_ref[...] = jnp.zeros_like(daz_ref)
            dbo_ref[...] = jnp.zeros_like(dbo_ref)
            dbz_ref[...] = jnp.zeros_like(dbz_ref)

    lat_rows = pl.BlockSpec((tm, w), lat)
    all_rows = pl.BlockSpec((tm, w), lambda i: (i, 0))
    vec = pl.BlockSpec((1, w), lambda i: (0, 0))
    sd_t = jax.ShapeDtypeStruct((t_rows, w), F32)
    sd_r = jax.ShapeDtypeStruct((r, w), F32)
    sd_v = jax.ShapeDtypeStruct((1, w), F32)
    return pl.pallas_call(
        body, name="post_bwd", grid=(r // tm,),
        in_specs=[lat_rows] * 4 + _post_specs(w, tm, lat_tiles, (4, 8, 9)) + [vec, vec]
        + [pl.BlockSpec((tm, 2 * w), lat)],
        out_specs=(lat_rows, lat_rows, all_rows, all_rows, all_rows, vec, vec),
        out_shape=(sd_t, sd_t, sd_r, sd_r, sd_r, sd_v, sd_v),
        compiler_params=_params(("arbitrary",)),
    )(o_f, o_b, h_f, h_b, u, u, u, wa, wb, dy)


OUT_ROW_GATE, OUT_ROW_LN_G, OUT_ROW_LN_B, OUT_ROW_LOSS = 0, 1, 2, 3


def _out_block(y, w_out, x, target, prm, tm):
    t_rows, dm = x.shape
    di = y.shape[1]

    def body(y_ref, w_ref, x_ref, t_ref, p_ref, dz_ref, dy_ref, gx_ref, acc_ref):
        @pl.when(pl.program_id(0) == 0)
        def _():
            acc_ref[...] = jnp.zeros_like(acc_ref)

        gate, ln_g, ln_b = p_ref[0:1, :], p_ref[1:2, :], p_ref[2:3, :]
        z = _nn(y_ref[...], w_ref[...])
        res = ALPHA * x_ref[...] + gate * z
        mu = jnp.mean(res, axis=-1, keepdims=True)
        rc = res - mu
        rstd = lax.rsqrt(jnp.mean(rc * rc, axis=-1, keepdims=True) + LN_EPS)
        rn = rc * rstd
        err = rn * ln_g + ln_b - t_ref[...]
        d_out = err * (1.0 / dm)
        d_rn = d_out * ln_g
        d_res = rstd * (d_rn - jnp.mean(d_rn, axis=-1, keepdims=True)
                        - rn * jnp.mean(d_rn * rn, axis=-1, keepdims=True))
        acc_ref[OUT_ROW_GATE:OUT_ROW_GATE + 1, :] += jnp.sum(d_res * z, axis=0, keepdims=True)
        acc_ref[OUT_ROW_LN_G:OUT_ROW_LN_G + 1, :] += jnp.sum(d_out * rn, axis=0, keepdims=True)
        acc_ref[OUT_ROW_LN_B:OUT_ROW_LN_B + 1, :] += jnp.sum(d_out, axis=0, keepdims=True)
        acc_ref[OUT_ROW_LOSS:OUT_ROW_LOSS + 1, :] += (0.5 / dm) * jnp.sum(err * err, axis=0, keepdims=True)
        gx_ref[...] = ALPHA * d_res
        dz = (d_res * gate).astype(BF16)
        dz_ref[...] = dz
        dy_ref[...] = _nt(dz, w_ref[...])

    rows_d = pl.BlockSpec((tm, dm), lambda i: (i, 0))
    rows_i = pl.BlockSpec((tm, di), lambda i: (i, 0))
    return pl.pallas_call(
        body, name="out_block", grid=(t_rows // tm,),
        in_specs=[rows_i, pl.BlockSpec((di, dm), lambda i: (0, 0)), rows_d, rows_d,
                  pl.BlockSpec((8, dm), lambda i: (0, 0))],
        out_specs=(rows_d, rows_i, rows_d, pl.BlockSpec((8, dm), lambda i: (0, 0))),
        out_shape=(jax.ShapeDtypeStruct((t_rows, dm), BF16), jax.ShapeDtypeStruct((t_rows, di), F32),
                   jax.ShapeDtypeStruct((t_rows, dm), F32), jax.ShapeDtypeStruct((8, dm), F32)),
        compiler_params=_params(("arbitrary",)),
    )(y, w_out, x, target, prm)


def _mod_fwd(c16, w_mod, tn):
    dm, n = w_mod.shape

    def body(c_ref, w_ref, o_ref, a_ref):
        a = _silu(c_ref[...])
        a_ref[...] = a
        o_ref[...] = _nn(a, w_ref[...], HIGHEST)

    return pl.pallas_call(
        body, name="mod_fwd", grid=(n // tn,),
        in_specs=[pl.BlockSpec((16, dm), lambda j: (0, 0)), pl.BlockSpec((dm, tn), lambda j: (0, j))],
        out_specs=(pl.BlockSpec((16, tn), lambda j: (0, j)), pl.BlockSpec((16, dm), lambda j: (0, 0))),
        out_shape=(jax.ShapeDtypeStruct((16, n), F32), jax.ShapeDtypeStruct((16, dm), F32)),
        compiler_params=_params(("arbitrary",)),
    )(c16, w_mod)


def _mod_bwd(a16, dm16, w_mod, tn):
    dm, n = w_mod.shape

    def body(a_ref, d_ref, w_ref, dw_ref, dc_ref):
        @pl.when(pl.program_id(0) == 0)
        def _():
            dc_ref[...] = jnp.zeros_like(dc_ref)
        dw_ref[...] = _tn(a_ref[...], d_ref[...], HIGHEST)
        dc_ref[...] += _nt(d_ref[...], w_ref[...], HIGHEST)

    return pl.pallas_call(
        body, name="mod_bwd", grid=(n // tn,),
        in_specs=[pl.BlockSpec((16, dm), lambda j: (0, 0)), pl.BlockSpec((16, tn), lambda j: (0, j)),
                  pl.BlockSpec((dm, tn), lambda j: (0, j))],
        out_specs=(pl.BlockSpec((dm, tn), lambda j: (0, j)), pl.BlockSpec((16, dm), lambda j: (0, 0))),
        out_shape=(jax.ShapeDtypeStruct((dm, n), F32), jax.ShapeDtypeStruct((16, dm), F32)),
        compiler_params=_params(("arbitrary",)),
    )(a16, dm16, w_mod)


def _sum_devices(g, fold_rows):
    n_dev, rows, n = g.shape

    def body(g_ref, s_ref, t_ref):
        s = g_ref[0]
        for dev in range(1, n_dev):
            s = s + g_ref[dev]
        t_ref[...] = jnp.broadcast_to(jnp.sum(s, axis=-1, keepdims=True), (rows, LANE))
        s_ref[...] = s
        s_ref[0:fold_rows, :] = s[0:fold_rows] + s[fold_rows:2 * fold_rows]

    return pl.pallas_call(
        body, name="sum_devices",
        out_shape=(jax.ShapeDtypeStruct((rows, n), F32), jax.ShapeDtypeStruct((rows, LANE), F32)),
        compiler_params=_params(),
    )(g)


def _c_ctx_grad(parts, c_ctx_row):
    def body(p_ref, c_ref, o_ref):
        s = p_ref[0]
        for chip in range(1, N_CHIPS):
            s = s + p_ref[2 * chip]
        cv = c_ref[...]
        sg = _sigmoid(cv)
        o_ref[...] = s * (sg * (1.0 + cv * (1.0 - sg)))

    return pl.pallas_call(
        body, name="c_ctx_grad", out_shape=jax.ShapeDtypeStruct(parts.shape[1:], F32), compiler_params=_params(),
    )(parts, c_ctx_row)


def _sum_pair(name, mine, got):
    def body(a_ref, b_ref, o_ref):
        o_ref[...] = (a_ref[...] + b_ref[...]).astype(BF16)

    k, rows, n = mine.shape
    tl = _largest_divisor(n, max(LANE, (1 << 18) // rows), LANE)
    spec = pl.BlockSpec((1, rows, tl), lambda kk, i: (kk, 0, i))
    return pl.pallas_call(
        body, name=name, grid=(k, n // tl), in_specs=[spec, spec], out_specs=spec,
        out_shape=jax.ShapeDtypeStruct(mine.shape, BF16), compiler_params=_params(("parallel", "parallel")),
    )(mine, got)


def _sum_chips(name, got):
    k, rows, n = got.shape
    tl = _largest_divisor(n, max(LANE, (1 << 18) // rows), LANE)

    def body(g_ref, o_ref):
        total = g_ref[0].astype(F32)
        for kk in range(1, k):
            total = total + g_ref[kk].astype(F32)
        o_ref[...] = total

    return pl.pallas_call(
        body, name=name, grid=(n // tl,),
        in_specs=[pl.BlockSpec((k, rows, tl), lambda i: (0, 0, i))], out_specs=pl.BlockSpec((rows, tl), lambda i: (0, i)),
        out_shape=jax.ShapeDtypeStruct((rows, n), F32), compiler_params=_params(("parallel",)),
    )(got)


def _adamw_update(w, g, m, v):
    m2 = ADAM_B1 * m + (1.0 - ADAM_B1) * g
    v2 = ADAM_B2 * v + (1.0 - ADAM_B2) * jnp.square(g)
    m_hat = m2 / (1.0 - ADAM_B1 ** ADAM_STEP)
    v_hat = v2 / (1.0 - ADAM_B2 ** ADAM_STEP)
    return -ADAM_LR * (m_hat / (jnp.sqrt(v_hat) + ADAM_EPS) + ADAM_WD * w), m2, v2


def _adamw(name, w, g, m, v):
    rows, n = w.shape
    if rows % 8 == 0:
        tr = _largest_divisor(rows, max(8, (1 << 18) // n), 8)
        block, index, steps = (tr, n), (lambda i: (i, 0)), rows // tr
    else:
        tl = _largest_divisor(n, max(LANE, (1 << 18) // rows), LANE)
        block, index, steps = (rows, tl), (lambda i: (0, i)), n // tl

    def body(w_ref, g_ref, m_ref, v_ref, d_ref, mo_ref, vo_ref):
        d_ref[...], mo_ref[...], vo_ref[...] = _adamw_update(w_ref[...], g_ref[...], m_ref[...], v_ref[...])

    spec = pl.BlockSpec(block, index)
    sds = jax.ShapeDtypeStruct((rows, n), F32)
    return pl.pallas_call(
        body, name=name, grid=(steps,), in_specs=[spec] * 4, out_specs=(spec,) * 3,
        out_shape=(sds, sds, sds), compiler_params=_params(("parallel",)),
    )(w, g, m, v)


PACK_LANES = 1024


def _pack(pieces):
    flat = jnp.concatenate([p.reshape(-1) for p in pieces])
    total = -(-flat.shape[0] // (8 * PACK_LANES)) * 8 * PACK_LANES
    return jnp.pad(flat, (0, total - flat.shape[0])).reshape(-1, PACK_LANES)


def _unpack(packed, shapes):
    flat = packed.reshape(-1)
    out, off = [], 0
    for shp in shapes:
        size = math.prod(shp)
        out.append(flat[off:off + size].reshape(shp))
        off += size
    return out


def _rows8(rows, width):
    flat = [r.reshape(width) for r in rows] + [jnp.zeros(((8 - len(rows)) * width,), F32)]
    return jnp.concatenate(flat).reshape(8, width)


def _reduce_scatter(tag, chip, ci, mine, other, axis, scatter=None):
    got = _sibling_swap("rs_pair_" + tag, [other])[0]
    pair = _sum_pair("rs_pair_sum_" + tag, mine, got)
    extra, landed = (None, _chip_scatter("rs_chips_" + tag, [pair])[0]) if scatter is None else scatter(pair)
    half = _sum_chips("rs_chip_sum_" + tag, _own_block(chip, pair, landed))
    return _join_halves(ci, half, _sibling_swap("rs_join_" + tag, [half])[0], axis), extra


def kernel(x, c, ctx, c_ctx, w_mod, b_mod, w_in, conv_w, conv_b, hg_lb, ml_gate_b, hg_norm_w, ml_norm_w, w_out, ln_g, ln_b, loss_target, m_c_ctx, m_w_mod, m_b_mod, m_w_in, m_conv_w, m_conv_b, m_hg_lb, m_ml_gate_b, m_hg_norm_w, m_ml_norm_w, m_w_out, m_ln_g, m_ln_b, v_c_ctx, v_w_mod, v_b_mod, v_w_in, v_conv_w, v_conv_b, v_hg_lb, v_ml_gate_b, v_hg_norm_w, v_ml_norm_w, v_w_out, v_ln_g, v_ln_b):
    t_rows, dm = x.shape[1], x.shape[2]
    c_rows = ctx.shape[1]
    w = hg_norm_w.shape[1]
    n_ml = ml_gate_b.shape[-1]
    n_hg = w // HG_DK
    di = 2 * w
    n_in = 10 * w + 4 * n_ml
    ns = w_in.shape[2]
    nm = w_mod.shape[2]
    n_pad = 10 * w + LANE
    r_rows = t_rows + c_rows
    row_gcd = math.gcd(t_rows, c_rows)
    hg_chunk, ml_chunk = math.gcd(HG_CHUNK, row_gcd), math.gcd(ML_CHUNK, row_gcd)
    hg_counts = (t_rows // hg_chunk, c_rows // hg_chunk, hg_chunk)
    ml_counts = (t_rows // ml_chunk, c_rows // ml_chunk, ml_chunk)
    assert ml_norm_w.shape[1] == w and di == dm and N_CHIPS * ns == n_in and N_CHIPS * nm == 3 * dm
    assert w_out.shape[1] * N_CHIPS == di and 4 * n_ml <= LANE and t_rows % GRID_W == 0

    xi, yi, ci = lax.axis_index("x"), lax.axis_index("y"), lax.axis_index("c")
    chip = 2 * xi + yi
    dev = 4 * xi + 2 * yi + ci

    tm = _largest_divisor(math.gcd(t_rows, c_rows), 256, 8)
    tm_mm = _largest_divisor(r_rows, 1088, SUBLANE_BF16)
    tn_mm = LANE * _largest_divisor(n_pad // LANE, 9)
    tn_mod = _largest_divisor(nm, 512, LANE)

    shard_shapes = [(dm,), (2, 2, w // N_CHIPS), (3, 3, di // N_CHIPS)]
    g1 = _all_gather8("gather_inputs", _pack([c, hg_lb, conv_w]))
    per_dev = [_unpack(g1[i], shard_shapes) for i in range(N_DEV)]
    c_all = jnp.stack([p[0] for p in per_dev])
    lb_full = jnp.concatenate([per_dev[2 * k][1] for k in range(N_CHIPS)], axis=-1)
    conv_w9 = jnp.concatenate([per_dev[2 * k][2] for k in range(N_CHIPS)], axis=-1).reshape(9, di)

    c16 = jnp.concatenate([c_all, c_ctx[None], jnp.zeros((16 - N_DEV - 1, dm), F32)])
    mod_part, a16 = _mod_fwd(c16, w_mod[0], tn_mod)
    g2 = _all_gather8("gather_mod", mod_part)
    mod_all = jnp.concatenate([g2[2 * k] for k in range(N_CHIPS)], axis=1) + b_mod
    mod_x = lax.dynamic_index_in_dim(mod_all, dev, 0, keepdims=False).reshape(3, dm)
    mod_c = mod_all[N_DEV].reshape(3, dm)
    prm = jnp.stack([_rows8(list(mod_x), dm), _rows8(list(mod_c), dm)])

    as_t = lambda a: jnp.transpose(a[0])
    halves = [lax.dynamic_slice_in_dim(as_t(w_in).astype(BF16), ci * (dm // 2), dm // 2, 1),
              lax.dynamic_slice_in_dim(w_out[0].astype(BF16), ci * (di // (2 * N_CHIPS)), di // (2 * N_CHIPS), 0)]
    fetched = [_own_block(chip, own, got)
               for own, got in zip(halves, _all_gather_chips("gather_weights", halves))]
    swapped = _sibling_swap("gather_weights_pair", fetched)
    gw_in = _join_halves(ci, fetched[0], swapped[0], 2)
    gw_out = _join_halves(ci, fetched[1], swapped[1], 1)
    wt_full = jnp.concatenate([gw_in.reshape(n_in, dm), jnp.zeros((n_pad - n_in, dm), BF16)])
    w_out_full = gw_out.reshape(di, dm)

    hc = _modulate_fwd(x[0], ctx[0], prm, tm)
    u = _mm_nt("in_proj", hc, wt_full, tm_mm, tn_mm, F32)
    (o_f, o_b), hg_saved = _hg_scan_fwd(u, lb_full, w, *hg_counts)
    qk = _conv_fwd(u, conv_w9, conv_b, t_rows, c_rows, w, LANE)
    gate_b_row = jnp.pad(ml_gate_b.reshape(1, -1), ((0, 0), (0, LANE - 4 * n_ml)))
    (h_f, h_b), ml_saved = _ml_scan_fwd(qk, u, gate_b_row, w, n_ml, *ml_counts)
    y = _post_fwd(o_f, o_b, h_f, h_b, u, hg_norm_w, ml_norm_w, t_rows, w, n_hg, n_ml, tm)
    prm_out = _rows8([mod_x[2], ln_g, ln_b], dm)
    dz, dy, gx_direct, acc_out = _out_block(y, w_out_full, x[0], loss_target[0], prm_out, tm // 2)

    d_w_out = _mm_tn("d_w_out", y, dz, _largest_divisor(di, 1024, LANE),
                     _largest_divisor(t_rows, 1024, SUBLANE_BF16))
    d_o, d_h, d_az, d_bo, d_bz, d_wa, d_wb = _post_bwd(o_f, o_b, h_f, h_b, u, hg_norm_w, ml_norm_w, dy,
                                                        t_rows, w, n_hg, n_ml, tm)
    (d_aq_f, d_aff, d_ai_f, d_lb_f), (d_aq_b, d_afb, d_ai_b, d_lb_b) = _hg_scan_bwd(
        u, lb_full, hg_saved, d_o, w, *hg_counts)
    (d_qk_f, d_v_f, d_g_f, d_gb_f), (d_qk_b, d_v_b, d_g_b, d_gb_b) = _ml_scan_bwd(
        qk, u, gate_b_row, ml_saved, d_h, w, n_ml, *ml_counts)
    d_bqk, d_cw, d_cb = _conv_bwd(u, (d_qk_f, d_qk_b), conv_w9, conv_b, t_rows, c_rows, w, LANE)
    du = _assemble_du([(d_aq_f, d_aq_b), d_aff, d_afb, (d_ai_f, d_ai_b), d_az, d_bqk, (d_v_f, d_v_b), d_bo, d_bz],
                      (d_g_f, d_g_b), n_pad, tm // 2)
    d_wt_in = _mm_tn("d_w_in", du, hc, tn_mm, tm_mm)

    half_in = dm // 2
    mine_in = lax.dynamic_slice_in_dim(d_wt_in, ci * half_in, half_in, 1)
    other_in = lax.dynamic_slice_in_dim(d_wt_in, (1 - ci) * half_in, half_in, 1)
    pieces_in = lambda a: jnp.stack([a[k * ns:(k + 1) * ns] for k in range(N_CHIPS)])
    g_wt_in, d_hc = _reduce_scatter(
        "w_in", chip, ci, pieces_in(mine_in), pieces_in(other_in), 1,
        scatter=lambda pieces: _mm_acc_scatter("d_h_rs_chips_w_in", du, wt_full, tm_mm, tn_mm, pieces))
    d_w_out4 = d_w_out.reshape(N_CHIPS, 2, di // (2 * N_CHIPS), dm)
    g_w_out, _ = _reduce_scatter("w_out", chip, ci, lax.dynamic_index_in_dim(d_w_out4, ci, 1, keepdims=False),
                                 lax.dynamic_index_in_dim(d_w_out4, 1 - ci, 1, keepdims=False), 0)
    gx, acc_mod = _modulate_bwd(x[0], ctx[0], d_hc, prm, gx_direct, tm)
    grad_x = gx[None]

    zero_row = jnp.zeros((dm,), F32)
    d_gb = jnp.concatenate([d_gb_f[:, 0:n_ml], d_gb_b[:, n_ml:2 * n_ml], d_gb_f[:, 2 * n_ml:3 * n_ml],
                            d_gb_b[:, 3 * n_ml:4 * n_ml], jnp.zeros((1, dm - 4 * n_ml), F32)], axis=1)
    rows = [acc_mod[0, 0], acc_mod[0, 1], acc_out[OUT_ROW_GATE],
            acc_mod[1, 0], acc_mod[1, 1], zero_row]
    rows += list(d_cw) + [d_cb[0], d_lb_f.reshape(dm), d_lb_b.reshape(dm),
                          jnp.concatenate([d_wa[0], d_wb[0]]), acc_out[OUT_ROW_LN_G], acc_out[OUT_ROW_LN_B],
                          acc_out[OUT_ROW_LOSS], d_gb[0], zero_row]
    ROW_CW, ROW_CB, ROW_LB, ROW_NORM, ROW_LN_G, ROW_LN_B, ROW_LOSS, ROW_GB = 6, 15, 16, 18, 19, 20, 21, 22
    g3 = _all_gather8("gather_small_grads", jnp.concatenate([r.reshape(dm) for r in rows]).reshape(len(rows), dm))
    sums, totals = _sum_devices(g3, 3)
    loss = totals[ROW_LOSS, 0]
    dm16 = jnp.concatenate([g3[:, 0:3, :].reshape(N_DEV, 3 * dm), sums[3:6].reshape(1, 3 * dm),
                            jnp.zeros((16 - N_DEV - 1, 3 * dm), F32)])
    g_w_mod, dc16 = _mod_bwd(a16, lax.dynamic_slice_in_dim(dm16, chip * nm, nm, 1), w_mod[0], tn_mod)
    g4 = _all_gather8("gather_c_ctx", jnp.pad(dc16[N_DEV:N_DEV + 1], ((0, 7), (0, 0))))
    g_c_ctx = _c_ctx_grad(g4, jnp.broadcast_to(c_ctx[None], (8, dm)))[0]

    chip_cols = lambda a, width: lax.dynamic_slice_in_dim(a, chip * width, width, a.ndim - 1)
    grads = {
        "c_ctx": g_c_ctx,
        "w_mod": g_w_mod[None],
        "b_mod": sums[0:3].reshape(1, 3 * dm),
        "w_in": jnp.transpose(g_wt_in)[None],
        "conv_w": chip_cols(sums[ROW_CW:ROW_CW + 9].reshape(1, 3, 3, di), di // N_CHIPS),
        "conv_b": sums[ROW_CB][None],
        "hg_lb": chip_cols(sums[ROW_LB:ROW_LB + 2].reshape(2, 2, w), w // N_CHIPS),
        "ml_gate_b": sums[ROW_GB, 0:4 * n_ml].reshape(1, 4, n_ml),
        "hg_norm_w": sums[ROW_NORM, 0:w][None],
        "ml_norm_w": sums[ROW_NORM, w:2 * w][None],
        "w_out": g_w_out[None],
        "ln_g": sums[ROW_LN_G][None],
        "ln_b": sums[ROW_LN_B][None],
    }
    weights = dict(c_ctx=c_ctx, w_mod=w_mod, b_mod=b_mod, w_in=w_in, conv_w=conv_w, conv_b=conv_b, hg_lb=hg_lb,
                   ml_gate_b=ml_gate_b, hg_norm_w=hg_norm_w, ml_norm_w=ml_norm_w, w_out=w_out, ln_g=ln_g, ln_b=ln_b)
    mom1 = dict(c_ctx=m_c_ctx, w_mod=m_w_mod, b_mod=m_b_mod, w_in=m_w_in, conv_w=m_conv_w, conv_b=m_conv_b,
                hg_lb=m_hg_lb, ml_gate_b=m_ml_gate_b, hg_norm_w=m_hg_norm_w, ml_norm_w=m_ml_norm_w, w_out=m_w_out,
                ln_g=m_ln_g, ln_b=m_ln_b)
    mom2 = dict(c_ctx=v_c_ctx, w_mod=v_w_mod, b_mod=v_b_mod, w_in=v_w_in, conv_w=v_conv_w, conv_b=v_conv_b,
                hg_lb=v_hg_lb, ml_gate_b=v_ml_gate_b, hg_norm_w=v_hg_norm_w, ml_norm_w=v_ml_norm_w, w_out=v_w_out,
                ln_g=v_ln_g, ln_b=v_ln_b)
    names = list(weights)
    big = ("w_mod", "w_in", "w_out")
    small = [n for n in names if n not in big]

    delta, new_m, new_v = {}, {}, {}
    for n in ("w_mod", "w_out"):
        as2d = lambda a: a.reshape(a.shape[-2], a.shape[-1])
        res = _adamw("adamw_" + n, as2d(weights[n]), as2d(grads[n]), as2d(mom1[n]), as2d(mom2[n]))
        delta[n], new_m[n], new_v[n] = (a.reshape(weights[n].shape) for a in res)
    res = _adamw("adamw_w_in", as_t(w_in), g_wt_in, as_t(m_w_in), as_t(v_w_in))
    delta["w_in"], new_m["w_in"], new_v["w_in"] = (jnp.transpose(a)[None] for a in res)
    small_shapes = [weights[n].shape for n in small]
    res = _adamw("adamw_small", *(_pack([src[n] for n in small]) for src in (weights, grads, mom1, mom2)))
    for out, packed in zip((delta, new_m, new_v), res):
        for n, a in zip(small, _unpack(packed, small_shapes)):
            out[n] = a

    return (loss, grad_x, *[grads[n].reshape(weights[n].shape) for n in names], *[delta[n] for n in names],
            *[new_m[n] for n in names], *[new_v[n] for n in names])
```

```python
import functools
import math

import jax
import jax.numpy as jnp
from jax import lax
from jax.experimental import pallas as pl
from jax.experimental.pallas import tpu as pltpu

F32 = jnp.float32
BF16 = jnp.bfloat16
HIGHEST = lax.Precision.HIGHEST
MESH = pl.DeviceIdType.MESH

HG_CHUNK = 64
ML_CHUNK = 256
GRID_W = 64
HG_DK = 128
LANE = 128
SUBLANE_BF16 = 16
ALPHA = 2.0 ** 0.25
LN_EPS = 1e-5
NORM_EPS = 1e-6
ADAM_LR = 0.001
ADAM_B1 = 0.9
ADAM_B2 = 0.999
ADAM_EPS = 1e-08
ADAM_WD = 0.01
ADAM_STEP = 10
VMEM_LIMIT = 56 * 1024 * 1024
N_CHIPS = 4
N_DEV = 8


def _params(sem=None):
    return pltpu.CompilerParams(dimension_semantics=sem, vmem_limit_bytes=VMEM_LIMIT)


def _largest_divisor(n, cap, multiple=1):
    best = None
    for d in range(multiple, min(n, cap) + 1, multiple):
        if n % d == 0:
            best = d
    assert best is not None, (n, cap, multiple)
    return best


def _sigmoid(x):
    return jax.nn.sigmoid(x)


def _silu(x):
    return x * jax.nn.sigmoid(x)


def _dot(a, b, dims, precision=None):
    return lax.dot_general(a, b, (dims, ((), ())), precision=precision, preferred_element_type=F32)


def _nn(a, b, precision=None):
    return _dot(a, b, ((1,), (0,)), precision)


def _nt(a, b, precision=None):
    return _dot(a, b, ((1,), (1,)), precision)


def _tn(a, b, precision=None):
    return _dot(a, b, ((0,), (0,)), precision)


def _visible(n, rev):
    r = lax.broadcasted_iota(jnp.int32, (n, n), 0)
    c = lax.broadcasted_iota(jnp.int32, (n, n), 1)
    return (r <= c) if rev else (r >= c)


def _hg_chunk(states, aq, af, ai, lb0, lb1, rev):
    n_heads = len(states)
    lb = _sigmoid(lb0 - lb1)
    f = lb + (1.0 - lb) * _sigmoid(af)
    g = jnp.log(f)
    k = 1.0 - f
    q = _silu(aq)
    chunk = aq.shape[0]
    vis = _visible(chunk, rev)
    b = _nn(vis.astype(F32), g, HIGHEST)
    last = 0 if rev else chunk - 1
    b_end = b[last:last + 1]
    b_mid = b[chunk // 2:chunk // 2 + 1]
    q_inter = q * jnp.exp(b)
    q_intra = q * jnp.exp(b - b_mid)
    k_intra = k * jnp.exp(b_mid - b)
    k_dec = k * jnp.exp(b_end - b)
    e_end = jnp.exp(b_end)
    new_states, outs = [], []
    for h in range(n_heads):
        sl = slice(h * HG_DK, (h + 1) * HG_DK)
        s_t = states[h]
        scores = jnp.where(vis, _nt(q_intra[:, sl], k_intra[:, sl]), 0.0)
        outs.append(_nt(q_inter[:, sl], s_t) + _nn(scores, ai[:, sl]))
        new_states.append(e_end[:, sl] * s_t + _tn(ai[:, sl], k_dec[:, sl]))
    return new_states, jnp.concatenate(outs, axis=1)


def _ml_chunk(state, q, k, v, g, gb, rev, d):
    cms, nvs, mbs = state
    n_heads = len(cms)
    dh = q.shape[1] // n_heads
    ga = g + gb
    log_f_all = jax.nn.log_sigmoid(ga)
    chunk = q.shape[0]
    vis = _visible(chunk, rev)
    b_all = _nn(vis.astype(F32), log_f_all, HIGHEST)
    last = 0 if rev else chunk - 1
    k = k * (dh ** -0.5)
    new_c, new_n, new_m, outs = [], [], [], []
    for h in range(n_heads):
        ci = d * n_heads + h
        cf = (2 + d) * n_heads + h
        sl = slice(h * dh, (h + 1) * dh)
        qh, kh, vh = q[:, sl], k[:, sl], v[:, sl]
        li = ga[:, ci:ci + 1]
        b = b_all[:, cf:cf + 1]
        m = mbs[h][:, 0:1]
        row = jnp.transpose(li - b)
        log_w = jnp.where(vis, b + row, -jnp.inf)
        m_inter = b + m
        m_t = jnp.maximum(m_inter, jnp.max(log_w, axis=-1, keepdims=True))
        w_inter = jnp.exp(m_inter - m_t)
        w_qk = jnp.exp(log_w - m_t) * _nt(qh, kh)
        num = w_inter * _nt(qh, cms[h]) + _nn(w_qk, vh)
        den = w_inter * jnp.sum(qh * nvs[h], axis=-1, keepdims=True) + jnp.sum(w_qk, axis=-1, keepdims=True)
        outs.append(num / jnp.maximum(jnp.abs(den), jnp.exp(-m_t)))
        m_new = m_t[last:last + 1]
        b_end = b[last:last + 1]
        w_s = jnp.exp(b_end - b + li - m_new)
        decay = jnp.exp(b_end + m - m_new)
        new_c.append(decay * cms[h] + _tn(w_s * vh, kh))
        new_n.append(decay * nvs[h] + jnp.sum(w_s * kh, axis=0, keepdims=True))
        new_m.append(jnp.broadcast_to(m_new, (1, LANE)))
    return (new_c, new_n, new_m), jnp.concatenate(outs, axis=1)


def _post_fn(o_f, o_b, az, h_f, h_b, bo, bz, wa, wb, n_hg, n_ml):
    o = o_f + o_b
    parts = []
    for h in range(n_hg):
        s = o[:, h * HG_DK:(h + 1) * HG_DK]
        parts.append(s * lax.rsqrt(jnp.mean(s * s, axis=-1, keepdims=True) + NORM_EPS))
    y_a = jnp.concatenate(parts, axis=1) * wa * _silu(az)
    hh = h_f + h_b
    dh = hh.shape[1] // n_ml
    parts = []
    for h in range(n_ml):
        s = hh[:, h * dh:(h + 1) * dh]
        mu = jnp.mean(s, axis=-1, keepdims=True)
        sc = s - mu
        parts.append(sc * lax.rsqrt(jnp.mean(sc * sc, axis=-1, keepdims=True) + NORM_EPS))
    y_b = jnp.concatenate(parts, axis=1) * wb * _sigmoid(bo) * _silu(bz)
    return jnp.concatenate([y_a, y_b], axis=1)


def _chip_of(dev):
    return 2 * dev[0] + dev[1]


def _index_of(dev):
    return 4 * dev[0] + 2 * dev[1] + dev[2]


class _Exchange:
    def __init__(self, srcs, out_shapes, transfers, local_copies=()):
        self.srcs, self.out_shapes = list(srcs), list(out_shapes)
        self.transfers, self.local_copies = list(transfers), list(local_copies)

    def scratch(self):
        return [pltpu.SemaphoreType.DMA((len(self.transfers),)), pltpu.SemaphoreType.DMA((len(self.transfers),)),
                pltpu.SemaphoreType.DMA((max(len(self.local_copies), 1),))]

    def copies(self, ins, outs, send_sems, recv_sems, local_sems):
        me = (lax.axis_index("x"), lax.axis_index("y"), lax.axis_index("c"))

        def pick(ref, fn, *who):
            return ref if fn is None else ref.at[fn(*who)]

        sends, recvs, locs = [], [], []
        for t, (mask, si, sfn, di, dfn) in enumerate(self.transfers):
            peer = tuple(1 - p if flip else p for p, flip in zip(me, mask))
            sends.append(pltpu.make_async_remote_copy(
                src_ref=pick(ins[si], sfn, me, peer), dst_ref=pick(outs[di], dfn, me, peer),
                send_sem=send_sems.at[t], recv_sem=recv_sems.at[t], device_id=peer, device_id_type=MESH))
            landing = pick(outs[di], dfn, peer, me)
            recvs.append(pltpu.make_async_remote_copy(
                src_ref=landing, dst_ref=landing,
                send_sem=send_sems.at[t], recv_sem=recv_sems.at[t], device_id=peer, device_id_type=MESH))
        for l, (si, sfn, di, dfn) in enumerate(self.local_copies):
            locs.append(pltpu.make_async_copy(pick(ins[si], sfn, me), pick(outs[di], dfn, me), local_sems.at[l]))

        def start():
            for cp in locs + sends:
                cp.start()

        def wait():
            for cp in recvs:
                cp.wait_recv()
            for cp in sends:
                cp.wait_send()
            for cp in locs:
                cp.wait()

        return start, wait

    def run(self, name):
        n_in, n_out = len(self.srcs), len(self.out_shapes)

        def body(*refs):
            start, wait = self.copies(refs[:n_in], refs[n_in:n_in + n_out], *refs[n_in + n_out:])
            start()
            wait()

        hbm = pl.BlockSpec(memory_space=pltpu.HBM)
        return pl.pallas_call(
            body, name=name, out_shape=tuple(self.out_shapes), in_specs=[hbm] * n_in,
            out_specs=tuple([hbm] * n_out), scratch_shapes=self.scratch(),
        )(*self.srcs)


def _call(body, operands, *, name, grid, in_specs, out_specs, out_shape, scratch_shapes=(), sem=None, rider=None):
    out_specs, out_shape, scratch_shapes = list(out_specs), list(out_shape), list(scratch_shapes)
    if rider is None:
        res = pl.pallas_call(
            body, name=name, grid=grid, in_specs=list(in_specs), out_specs=tuple(out_specs),
            out_shape=tuple(out_shape), scratch_shapes=scratch_shapes, compiler_params=_params(sem),
        )(*operands)
        return list(res), []
    counts = (len(in_specs), len(rider.srcs), len(out_specs), len(rider.out_shapes), len(scratch_shapes), 3)

    def full(*refs):
        groups, pos = [], 0
        for k in counts:
            groups.append(refs[pos:pos + k])
            pos += k
        own_in, ex_in, own_out, ex_out, own_scr, ex_scr = groups
        ids = [pl.program_id(a) for a in range(len(grid))]
        first = functools.reduce(jnp.logical_and, [i == 0 for i in ids])
        last = functools.reduce(jnp.logical_and, [i == g - 1 for i, g in zip(ids, grid)])
        start, wait = rider.copies(ex_in, ex_out, *ex_scr)
        pl.when(first)(start)
        body(*own_in, *own_out, *own_scr)
        pl.when(last)(wait)

    hbm = pl.BlockSpec(memory_space=pltpu.HBM)
    res = pl.pallas_call(
        full, name=name, grid=grid, in_specs=list(in_specs) + [hbm] * counts[1],
        out_specs=tuple(out_specs + [hbm] * counts[3]), out_shape=tuple(out_shape + rider.out_shapes),
        scratch_shapes=scratch_shapes + rider.scratch(), compiler_params=_params(("arbitrary",) * len(grid)),
    )(*operands, *rider.srcs)
    return list(res[:counts[2]]), list(res[counts[2]:])


ALL_MASKS = [(mx, my, mc) for mx in (0, 1) for my in (0, 1) for mc in (0, 1)][1:]
CHIP_MASKS = [(1, 0, 0), (0, 1, 0), (1, 1, 0)]
SIBLING_MASK = (0, 0, 1)


def _all_gather8(v):
    out = jax.ShapeDtypeStruct((N_DEV,) + v.shape, v.dtype)
    slot = lambda sender, receiver: _index_of(sender)
    transfers = [(mask, 0, None, 0, slot) for mask in ALL_MASKS]
    return _Exchange([v], [out], transfers, [(0, None, 0, lambda me: _index_of(me))])


def _all_gather_chips(arrays):
    outs = [jax.ShapeDtypeStruct((N_CHIPS,) + a.shape, a.dtype) for a in arrays]
    slot = lambda sender, receiver: _chip_of(sender)
    return _Exchange(arrays, outs, [(mask, i, None, i, slot) for i in range(len(arrays)) for mask in CHIP_MASKS])


def _sibling_swap(arrays):
    outs = [jax.ShapeDtypeStruct(a.shape, a.dtype) for a in arrays]
    return _Exchange(arrays, outs, [(SIBLING_MASK, i, None, i, None) for i in range(len(arrays))])


def _chip_scatter(arrays):
    outs = [jax.ShapeDtypeStruct(a.shape, a.dtype) for a in arrays]
    transfers = [(mask, i, lambda s, r: _chip_of(r), i, lambda s, r: _chip_of(s))
                 for i in range(len(arrays)) for mask in CHIP_MASKS]
    return _Exchange(arrays, outs, transfers)


def _own_block(chip, own, blocks):
    sel = (lax.broadcasted_iota(jnp.int32, (N_CHIPS,) + (1,) * (blocks.ndim - 1), 0) == chip)
    return jnp.where(sel, own if own.ndim == blocks.ndim else own[None], blocks)


def _join_halves(ci, mine, other, axis):
    return jnp.where(ci == 0, jnp.concatenate([mine, other], axis=axis), jnp.concatenate([other, mine], axis=axis))


def _mm_nt(name, a, b, tm, tn, out_dtype, rider=None):
    m, k = a.shape
    n = b.shape[0]

    def body(a_ref, b_ref, o_ref):
        o_ref[...] = _nt(a_ref[...], b_ref[...]).astype(out_dtype)

    (out,), rode = _call(
        body, (a, b), name=name, grid=(n // tn, m // tm),
        in_specs=[pl.BlockSpec((tm, k), lambda j, i: (i, 0)), pl.BlockSpec((tn, k), lambda j, i: (j, 0))],
        out_specs=[pl.BlockSpec((tm, tn), lambda j, i: (i, j))],
        out_shape=[jax.ShapeDtypeStruct((m, n), out_dtype)], sem=("parallel", "parallel"), rider=rider)
    return out, rode


def _mm_acc(name, a, b, tm, tk, rider=None):
    m, kc = a.shape
    n = b.shape[1]

    def body(a_ref, b_ref, o_ref):
        @pl.when(pl.program_id(1) == 0)
        def _():
            o_ref[...] = jnp.zeros_like(o_ref)
        o_ref[...] += _nn(a_ref[...], b_ref[...])

    (out,), rode = _call(
        body, (a, b), name=name, grid=(m // tm, kc // tk),
        in_specs=[pl.BlockSpec((tm, tk), lambda i, kk: (i, kk)), pl.BlockSpec((tk, n), lambda i, kk: (kk, 0))],
        out_specs=[pl.BlockSpec((tm, n), lambda i, kk: (i, 0))],
        out_shape=[jax.ShapeDtypeStruct((m, n), F32)], sem=("parallel", "arbitrary"), rider=rider)
    return out, rode


def _mm_tn(name, a, b, tm, tk):
    kr, m = a.shape
    n = b.shape[1]

    def body(a_ref, b_ref, o_ref):
        @pl.when(pl.program_id(1) == 0)
        def _():
            o_ref[...] = jnp.zeros_like(o_ref)
        o_ref[...] += _tn(a_ref[...], b_ref[...])

    return pl.pallas_call(
        body, name=name, grid=(m // tm, kr // tk),
        in_specs=[pl.BlockSpec((tk, tm), lambda i, kk: (kk, i)), pl.BlockSpec((tk, n), lambda i, kk: (kk, 0))],
        out_specs=pl.BlockSpec((tm, n), lambda i, kk: (i, 0)),
        out_shape=jax.ShapeDtypeStruct((m, n), F32),
        compiler_params=_params(("parallel", "arbitrary")),
    )(a, b)


def _modulate_fwd(x, ctx, prm, tm):
    t_rows, dm = x.shape
    lat = t_rows // tm
    r = t_rows + ctx.shape[0]

    def body(x_ref, c_ref, p_ref, h_ref):
        xv = jnp.where(pl.program_id(0) >= lat, c_ref[...], x_ref[...])
        mu = jnp.mean(xv, axis=-1, keepdims=True)
        xm = xv - mu
        n = xm * lax.rsqrt(jnp.mean(xm * xm, axis=-1, keepdims=True) + LN_EPS)
        h_ref[...] = (n * (1.0 + p_ref[0, 1:2, :]) + p_ref[0, 0:1, :]).astype(BF16)

    return pl.pallas_call(
        body, name="modulate_fwd", grid=(r // tm,),
        in_specs=[pl.BlockSpec((tm, dm), lambda i: (jnp.minimum(i, lat - 1), 0)),
                  pl.BlockSpec((tm, dm), lambda i: (jnp.maximum(i - lat, 0), 0)),
                  pl.BlockSpec((1, 8, dm), lambda i: ((i >= lat).astype(jnp.int32), 0, 0))],
        out_specs=pl.BlockSpec((tm, dm), lambda i: (i, 0)),
        out_shape=jax.ShapeDtypeStruct((r, dm), BF16),
        compiler_params=_params(("parallel",)),
    )(x, ctx, prm)


def _modulate_bwd(x, ctx, dh, prm, gx_direct, tm, rider=None):
    t_rows, dm = x.shape
    lat, n_ct = t_rows // tm, ctx.shape[0] // tm
    is_ctx = lambda i: i < n_ct
    cls = lambda i: is_ctx(i).astype(jnp.int32)
    lat_tile = lambda i: (jnp.maximum(i - n_ct, 0), 0)

    def body(x_ref, c_ref, dh_ref, p_ref, gd_ref, gx_ref, acc_ref):
        i = pl.program_id(0)

        @pl.when((i == 0) | (i == n_ct))
        def _():
            acc_ref[...] = jnp.zeros_like(acc_ref)

        x = jnp.where(is_ctx(i), c_ref[...], x_ref[...])
        dh_v = dh_ref[...]
        mu = jnp.mean(x, axis=-1, keepdims=True)
        xm = x - mu
        rstd = lax.rsqrt(jnp.mean(xm * xm, axis=-1, keepdims=True) + LN_EPS)
        n = xm * rstd
        acc_ref[0, 0:1, :] += jnp.sum(dh_v, axis=0, keepdims=True)
        acc_ref[0, 1:2, :] += jnp.sum(dh_v * n, axis=0, keepdims=True)
        dn = dh_v * (1.0 + p_ref[0, 1:2, :])
        dx = rstd * (dn - jnp.mean(dn, axis=-1, keepdims=True) - n * jnp.mean(dn * n, axis=-1, keepdims=True))
        gx_ref[...] = dx + gd_ref[...]

    return _call(
        body, (x, ctx, dh, prm, gx_direct), name="modulate_bwd", grid=(n_ct + lat,),
        in_specs=[pl.BlockSpec((tm, dm), lat_tile),
                  pl.BlockSpec((tm, dm), lambda i: (jnp.minimum(i, n_ct - 1), 0)),
                  pl.BlockSpec((tm, dm), lambda i: (jnp.where(is_ctx(i), lat + i, i - n_ct), 0)),
                  pl.BlockSpec((1, 8, dm), lambda i: (cls(i), 0, 0)),
                  pl.BlockSpec((tm, dm), lat_tile)],
        out_specs=(pl.BlockSpec((tm, dm), lat_tile), pl.BlockSpec((1, 8, dm), lambda i: (cls(i), 0, 0))),
        out_shape=(jax.ShapeDtypeStruct((t_rows, dm), F32), jax.ShapeDtypeStruct((2, 8, dm), F32)),
        sem=("arbitrary",), rider=rider)


def _conv_parts(t_rows, c_rows):
    return ((0, t_rows, t_rows // GRID_W, GRID_W), (t_rows, c_rows, 1, c_rows))


def _col_shifts(x2, rows_g, width_g):
    n, ct = x2.shape
    col = lax.broadcasted_iota(jnp.int32, (width_g, ct), 0)
    as_grid = lambda a: a.reshape(rows_g, width_g, ct)
    left = as_grid(pltpu.roll(x2, 1, 0)) * (col >= 1).astype(F32)
    right = as_grid(pltpu.roll(x2, n - 1, 0)) * (col <= width_g - 2).astype(F32)
    return [left, as_grid(x2), right]


def _row_shift(y3, a):
    if a == 1:
        return y3
    if y3.shape[0] == 1:
        return jnp.zeros_like(y3)
    zero = jnp.zeros_like(y3[:1])
    return jnp.concatenate([zero, y3[:-1]], axis=0) if a == 0 else jnp.concatenate([y3[1:], zero], axis=0)


def _conv_taps(cols, w_ref, flip):
    rows_g = cols[0].shape[0]
    acc = None
    for a in range(3):
        if rows_g == 1 and a != 1:
            continue
        inner = None
        for b in range(3):
            tap = (2 - a) * 3 + (2 - b) if flip else a * 3 + b
            term = cols[b] * w_ref[tap:tap + 1, :]
            inner = term if inner is None else inner + term
        inner = _row_shift(inner, a)
        acc = inner if acc is None else acc + inner
    return acc


def _conv_fwd(u, conv_w9, conv_b, t_rows, c_rows, w, ct):
    r = u.shape[0]
    base = 5 * w // ct

    def body(x_ref, w_ref, b_ref, o_ref):
        for r0, n, rows_g, width_g in _conv_parts(t_rows, c_rows):
            pre = _conv_taps(_col_shifts(x_ref[r0:r0 + n, :], rows_g, width_g), w_ref, False) + b_ref[...]
            o_ref[r0:r0 + n, :] = _silu(pre).reshape(n, ct)

    return pl.pallas_call(
        body, name="conv_fwd", grid=(2 * w // ct,),
        in_specs=[pl.BlockSpec((r, ct), lambda i: (0, base + i)), pl.BlockSpec((9, ct), lambda i: (0, i)),
                  pl.BlockSpec((1, ct), lambda i: (0, i))],
        out_specs=pl.BlockSpec((r, ct), lambda i: (0, i)),
        out_shape=jax.ShapeDtypeStruct((r, 2 * w), F32),
        compiler_params=_params(("parallel",)),
    )(u, conv_w9, conv_b)


def _conv_bwd(u, dqk_pair, conv_w9, conv_b, t_rows, c_rows, w, ct):
    r = u.shape[0]
    base = 5 * w // ct

    def body(x_ref, d1_ref, d2_ref, w_ref, b_ref, dx_ref, dw_ref, db_ref):
        dw = [jnp.zeros((1, ct), F32) for _ in range(9)]
        db = jnp.zeros((1, ct), F32)
        for r0, n, rows_g, width_g in _conv_parts(t_rows, c_rows):
            cols = _col_shifts(x_ref[r0:r0 + n, :], rows_g, width_g)
            pre = (_conv_taps(cols, w_ref, False) + b_ref[...]).reshape(n, ct)
            sg = _sigmoid(pre)
            dpre = (d1_ref[r0:r0 + n, :] + d2_ref[r0:r0 + n, :]) * (sg * (1.0 + pre * (1.0 - sg)))
            db = db + jnp.sum(dpre, axis=0, keepdims=True)
            dx_ref[r0:r0 + n, :] = _conv_taps(_col_shifts(dpre, rows_g, width_g), w_ref, True).reshape(n, ct)
            dpre3 = dpre.reshape(rows_g, width_g, ct)
            for a in range(3):
                if rows_g == 1 and a != 1:
                    continue
                moved = _row_shift(dpre3, 2 - a)
                for b in range(3):
                    prod = jnp.sum(cols[b] * moved, axis=0)
                    dw[a * 3 + b] = dw[a * 3 + b] + jnp.sum(prod, axis=0, keepdims=True)
        for tap in range(9):
            dw_ref[tap:tap + 1, :] = dw[tap]
        db_ref[...] = db

    return pl.pallas_call(
        body, name="conv_bwd", grid=(2 * w // ct,),
        in_specs=[pl.BlockSpec((r, ct), lambda i: (0, base + i)), pl.BlockSpec((r, ct), lambda i: (0, i)),
                  pl.BlockSpec((r, ct), lambda i: (0, i)),
                  pl.BlockSpec((9, ct), lambda i: (0, i)), pl.BlockSpec((1, ct), lambda i: (0, i))],
        out_specs=(pl.BlockSpec((r, ct), lambda i: (0, i)), pl.BlockSpec((9, ct), lambda i: (0, i)),
                   pl.BlockSpec((1, ct), lambda i: (0, i))),
        out_shape=(jax.ShapeDtypeStruct((r, 2 * w), F32), jax.ShapeDtypeStruct((9, 2 * w), F32),
                   jax.ShapeDtypeStruct((1, 2 * w), F32)),
        compiler_params=_params(("parallel",)),
    )(u, dqk_pair[0], dqk_pair[1], conv_w9, conv_b)


def _assemble_du(groups, gates, n_pad, tm):
    flat, layout = [], []
    for entry in list(groups) + [gates]:
        parts = entry if isinstance(entry, (tuple, list)) else (entry,)
        layout.append((len(flat), len(parts), parts[0].shape[1]))
        flat += list(parts)
    r = flat[0].shape[0]

    def body(*refs):
        o_ref = refs[-1]
        col = 0
        for first, count, width in layout:
            val = refs[first][...]
            for extra in range(1, count):
                val = val + refs[first + extra][...]
            o_ref[:, col:col + width] = val.astype(BF16)
            col += width
        assert col == n_pad

    return pl.pallas_call(
        body, name="assemble_du", grid=(r // tm,),
        in_specs=[pl.BlockSpec((tm, a.shape[1]), lambda i: (i, 0)) for a in flat],
        out_specs=pl.BlockSpec((tm, n_pad), lambda i: (i, 0)),
        out_shape=jax.ShapeDtypeStruct((r, n_pad), BF16),
        compiler_params=_params(("parallel",)),
    )(*flat)


def _scan_order(n_lat, n_ctx, rev):
    n = n_lat + n_ctx
    if rev:
        return lambda j: n - 1 - j
    return lambda j: (j + n_lat) % n


DIRS = (False, True)


def _hg_scan_fwd(u, lb_full, w, n_lat, n_ctx, chunk, rider=None):
    r = u.shape[0]
    n_heads = w // HG_DK
    n_chunks = n_lat + n_ctx
    nat = [_scan_order(n_lat, n_ctx, rev) for rev in DIRS]

    def body(*refs):
        ins, outs, scratch = refs[:8], refs[8:12], refs[12:]

        @pl.when(pl.program_id(0) == 0)
        def _():
            for s_ref in scratch:
                s_ref[...] = jnp.zeros_like(s_ref)

        results = []
        for d, rev in enumerate(DIRS):
            aq, af, ai, lb_ref = ins[4 * d:4 * d + 4]
            state = [scratch[d][h] for h in range(n_heads)]
            results.append((state, _hg_chunk(state, aq[...], af[...], ai[...],
                                             lb_ref[0, 0:1, :], lb_ref[0, 1:2, :], rev)))
        for d, (state, (new, o)) in enumerate(results):
            o_ref, save_ref = outs[2 * d:2 * d + 2]
            o_ref[...] = o
            for h in range(n_heads):
                save_ref[0, h] = state[h]
                scratch[d][h] = new[h]

    in_specs, out_specs, out_shape = [], [], []
    for d in range(2):
        in_specs += [pl.BlockSpec((chunk,w), lambda j, d=d: (nat[d](j), 0)),
                     pl.BlockSpec((chunk,w), lambda j, d=d: (nat[d](j), 1 + d)),
                     pl.BlockSpec((chunk,w), lambda j, d=d: (nat[d](j), 3)),
                     pl.BlockSpec((1, 2, w), lambda j, d=d: (d, 0, 0))]
        out_specs += [pl.BlockSpec((chunk,w), lambda j, d=d: (nat[d](j), 0)),
                      pl.BlockSpec((1, n_heads, HG_DK, HG_DK), lambda j: (j, 0, 0, 0))]
        out_shape += [jax.ShapeDtypeStruct((r, w), F32),
                      jax.ShapeDtypeStruct((n_chunks, n_heads, HG_DK, HG_DK), F32)]
    (o_f, s_f, o_b, s_b), rode = _call(
        body, (u, u, u, lb_full, u, u, u, lb_full), name="hg_scan_fwd", grid=(n_chunks,), in_specs=in_specs,
        out_specs=out_specs, out_shape=out_shape, scratch_shapes=[pltpu.VMEM((n_heads, HG_DK, HG_DK), F32)] * 2,
        sem=("arbitrary",), rider=rider)
    return (o_f, o_b), (s_f, s_b), rode


def _hg_scan_bwd(u, lb_full, saved, d_o, w, n_lat, n_ctx, chunk, rider=None):
    r = u.shape[0]
    n_heads = w // HG_DK
    n_chunks = n_lat + n_ctx
    step = lambda jj: n_chunks - 1 - jj
    nat = [(lambda jj, o=_scan_order(n_lat, n_ctx, rev): o(step(jj))) for rev in DIRS]

    def body(*refs):
        ins, outs, scratch = refs[:12], refs[12:20], refs[20:]
        jj = pl.program_id(0)

        @pl.when(jj == 0)
        def _():
            for d in range(2):
                scratch[d][...] = jnp.zeros_like(scratch[d])
                outs[4 * d + 3][...] = jnp.zeros_like(outs[4 * d + 3])

        results = []
        for d, rev in enumerate(DIRS):
            aq, af, ai, lb_ref, save_ref, do_ref = ins[6 * d:6 * d + 6]
            f = lambda st, a, b, c, l0, l1, rev=rev: _hg_chunk(st, a, b, c, l0, l1, rev)
            _, vjp = jax.vjp(f, [save_ref[0, h] for h in range(n_heads)], aq[...], af[...], ai[...],
                             lb_ref[0, 0:1, :], lb_ref[0, 1:2, :])
            d_out = do_ref[...] * (nat[d](jj) < n_lat).astype(F32)
            results.append(vjp(([scratch[d][h] for h in range(n_heads)], d_out)))
        for d, (dst, daq, daf, dai, dl0, dl1) in enumerate(results):
            daq_ref, daf_ref, dai_ref, dlb_ref = outs[4 * d:4 * d + 4]
            for h in range(n_heads):
                scratch[d][h] = dst[h]
            daq_ref[...] = daq
            daf_ref[...] = daf
            dai_ref[...] = dai
            dlb_ref[0:1, :] += dl0
            dlb_ref[1:2, :] += dl1

    in_specs, out_specs, out_shape, operands = [], [], [], []
    for d in range(2):
        row = lambda jj, d=d: (nat[d](jj), 0)
        in_specs += [pl.BlockSpec((chunk,w), row),
                     pl.BlockSpec((chunk,w), lambda jj, d=d: (nat[d](jj), 1 + d)),
                     pl.BlockSpec((chunk,w), lambda jj, d=d: (nat[d](jj), 3)),
                     pl.BlockSpec((1, 2, w), lambda jj, d=d: (d, 0, 0)),
                     pl.BlockSpec((1, n_heads, HG_DK, HG_DK), lambda jj: (step(jj), 0, 0, 0)),
                     pl.BlockSpec((chunk,w), lambda jj, d=d: (jnp.minimum(nat[d](jj), n_lat - 1), 0))]
        operands += [u, u, u, lb_full, saved[d], d_o]
        out_specs += [pl.BlockSpec((chunk,w), row)] * 3 + [pl.BlockSpec((2, w), lambda jj: (0, 0))]
        out_shape += [jax.ShapeDtypeStruct((r, w), F32)] * 3 + [jax.ShapeDtypeStruct((2, w), F32)]
    res, rode = _call(
        body, operands, name="hg_scan_bwd", grid=(n_chunks,), in_specs=in_specs, out_specs=out_specs,
        out_shape=out_shape, scratch_shapes=[pltpu.VMEM((n_heads, HG_DK, HG_DK), F32)] * 2,
        sem=("arbitrary",), rider=rider)
    return res[0:4], res[4:8], rode


def _ml_state_shapes(n_chunks, n_heads, dh):
    return (jax.ShapeDtypeStruct((n_chunks, n_heads, dh, dh), F32),
            jax.ShapeDtypeStruct((n_chunks, n_heads, 1, dh), F32),
            jax.ShapeDtypeStruct((n_chunks, n_heads, 1, LANE), F32))


def _ml_state_specs(n_heads, dh, index):
    return (pl.BlockSpec((1, n_heads, dh, dh), lambda j: (index(j), 0, 0, 0)),
            pl.BlockSpec((1, n_heads, 1, dh), lambda j: (index(j), 0, 0, 0)),
            pl.BlockSpec((1, n_heads, 1, LANE), lambda j: (index(j), 0, 0, 0)))


def _ml_state_scratch(n_heads, dh):
    return [pltpu.VMEM((n_heads, dh, dh), F32), pltpu.VMEM((n_heads, 1, dh), F32), pltpu.VMEM((n_heads, 1, LANE), F32)]


def _ml_scan_fwd(qk, u, gate_b, w, n_heads, n_lat, n_ctx, chunk):
    r = u.shape[0]
    dh = w // n_heads
    n_chunks = n_lat + n_ctx
    nat = [_scan_order(n_lat, n_ctx, rev) for rev in DIRS]

    def body(*refs):
        ins, outs, scratch = refs[:10], refs[10:18], refs[18:]

        @pl.when(pl.program_id(0) == 0)
        def _():
            for s_ref in scratch:
                s_ref[...] = jnp.zeros_like(s_ref)

        results = []
        for d, rev in enumerate(DIRS):
            q, k, v, g, gb = ins[5 * d:5 * d + 5]
            state = tuple([ref[h] for h in range(n_heads)] for ref in scratch[3 * d:3 * d + 3])
            results.append((state, _ml_chunk(state, q[...], k[...], v[...], g[...], gb[...], rev, d)))
        for d, (state, (new, o)) in enumerate(results):
            outs[4 * d][...] = o
            for part in range(3):
                for h in range(n_heads):
                    outs[4 * d + 1 + part][0, h] = state[part][h]
                    scratch[3 * d + part][h] = new[part][h]

    in_specs, out_specs, out_shape = [], [], []
    for d in range(2):
        in_specs += [pl.BlockSpec((chunk,w), lambda j, d=d: (nat[d](j), 0)),
                     pl.BlockSpec((chunk,w), lambda j, d=d: (nat[d](j), 1)),
                     pl.BlockSpec((chunk,w), lambda j, d=d: (nat[d](j), 7)),
                     pl.BlockSpec((chunk,LANE), lambda j, d=d: (nat[d](j), 10 * w // LANE)),
                     pl.BlockSpec((1, LANE), lambda j: (0, 0))]
        out_specs += [pl.BlockSpec((chunk,w), lambda j, d=d: (nat[d](j), 0))]
        out_specs += list(_ml_state_specs(n_heads, dh, lambda j: j))
        out_shape += [jax.ShapeDtypeStruct((r, w), F32)] + list(_ml_state_shapes(n_chunks, n_heads, dh))
    res = pl.pallas_call(
        body, name="ml_scan_fwd", grid=(n_chunks,), in_specs=in_specs, out_specs=tuple(out_specs),
        out_shape=tuple(out_shape), scratch_shapes=_ml_state_scratch(n_heads, dh) * 2,
        compiler_params=_params(("arbitrary",)),
    )(qk, qk, u, u, gate_b, qk, qk, u, u, gate_b)
    return (res[0], res[4]), (res[1:4], res[5:8])


def _ml_scan_bwd(qk, u, gate_b, saved, d_h, w, n_heads, n_lat, n_ctx, chunk, rider=None):
    r = u.shape[0]
    dh = w // n_heads
    n_chunks = n_lat + n_ctx
    step = lambda jj: n_chunks - 1 - jj
    nat = [(lambda jj, o=_scan_order(n_lat, n_ctx, rev): o(step(jj))) for rev in DIRS]

    def body(*refs):
        ins, outs, scratch = refs[:18], refs[18:26], refs[26:]
        jj = pl.program_id(0)

        @pl.when(jj == 0)
        def _():
            for s_ref in scratch:
                s_ref[...] = jnp.zeros_like(s_ref)
            for d in range(2):
                outs[4 * d + 3][...] = jnp.zeros_like(outs[4 * d + 3])

        results = []
        for d, rev in enumerate(DIRS):
            q, k, v, g, gb, sc, sn, sm, dh_ref = ins[9 * d:9 * d + 9]
            state = tuple([ref[0, h] for h in range(n_heads)] for ref in (sc, sn, sm))
            f = lambda st, a, b, c, gg, bb, rev=rev, d=d: _ml_chunk(st, a, b, c, gg, bb, rev, d)
            _, vjp = jax.vjp(f, state, q[...], k[...], v[...], g[...], gb[...])
            d_state = tuple([ref[h] for h in range(n_heads)] for ref in scratch[3 * d:3 * d + 3])
            d_out = dh_ref[...] * (nat[d](jj) < n_lat).astype(F32)
            results.append(vjp((d_state, d_out)))
        for d, (d_state, dq, dk, dv, dg, dgb) in enumerate(results):
            dqk_ref, dv_ref, dg_ref, dgb_ref = outs[4 * d:4 * d + 4]
            for part in range(3):
                for h in range(n_heads):
                    scratch[3 * d + part][h] = d_state[part][h]
            dqk_ref[:, 0:w] = dq
            dqk_ref[:, w:2 * w] = dk
            dv_ref[...] = dv
            dg_ref[...] = dg
            dgb_ref[...] += dgb

    in_specs, out_specs, out_shape, operands = [], [], [], []
    for d in range(2):
        row = lambda jj, d=d: (nat[d](jj), 0)
        in_specs += [pl.BlockSpec((chunk,w), row), pl.BlockSpec((chunk,w), lambda jj, d=d: (nat[d](jj), 1)),
                     pl.BlockSpec((chunk,w), lambda jj, d=d: (nat[d](jj), 7)),
                     pl.BlockSpec((chunk,LANE), lambda jj, d=d: (nat[d](jj), 10 * w // LANE)),
                     pl.BlockSpec((1, LANE), lambda jj: (0, 0))]
        in_specs += list(_ml_state_specs(n_heads, dh, step))
        in_specs += [pl.BlockSpec((chunk,w), lambda jj, d=d: (jnp.minimum(nat[d](jj), n_lat - 1), 0))]
        operands += [qk, qk, u, u, gate_b, *saved[d], d_h]
        out_specs += [pl.BlockSpec((chunk,2 * w), row), pl.BlockSpec((chunk,w), row),
                      pl.BlockSpec((chunk,LANE), row), pl.BlockSpec((1, LANE), lambda jj: (0, 0))]
        out_shape += [jax.ShapeDtypeStruct((r, 2 * w), F32), jax.ShapeDtypeStruct((r, w), F32),
                      jax.ShapeDtypeStruct((r, LANE), F32), jax.ShapeDtypeStruct((1, LANE), F32)]
    res, rode = _call(
        body, operands, name="ml_scan_bwd", grid=(n_chunks,), in_specs=in_specs, out_specs=out_specs,
        out_shape=out_shape, scratch_shapes=_ml_state_scratch(n_heads, dh) * 2, sem=("arbitrary",), rider=rider)
    return res[0:4], res[4:8], rode


def _post_specs(w, tm, lat_tiles, cols):
    return [pl.BlockSpec((tm, w), (lambda i, cb=cb: (jnp.minimum(i, lat_tiles - 1), cb))) for cb in cols]


def _post_fwd(o_f, o_b, h_f, h_b, u, wa, wb, t_rows, w, n_hg, n_ml, tm):
    lat_tiles = t_rows // tm

    def body(of, ob, hf, hb, az, bo, bz, wa_ref, wb_ref, y_ref):
        y_ref[...] = _post_fn(of[...], ob[...], az[...], hf[...], hb[...], bo[...], bz[...],
                              wa_ref[...], wb_ref[...], n_hg, n_ml).astype(BF16)

    rows = pl.BlockSpec((tm, w), lambda i: (i, 0))
    vec = pl.BlockSpec((1, w), lambda i: (0, 0))
    return pl.pallas_call(
        body, name="post_fwd", grid=(lat_tiles,),
        in_specs=[rows] * 4 + _post_specs(w, tm, lat_tiles, (4, 8, 9)) + [vec, vec],
        out_specs=pl.BlockSpec((tm, 2 * w), lambda i: (i, 0)),
        out_shape=jax.ShapeDtypeStruct((t_rows, 2 * w), BF16),
        compiler_params=_params(("parallel",)),
    )(o_f, o_b, h_f, h_b, u, u, u, wa, wb)


def _post_bwd(o_f, o_b, h_f, h_b, u, wa, wb, dy, t_rows, w, n_hg, n_ml, tm, rider=None):
    r = u.shape[0]
    lat_tiles = t_rows // tm
    lat = lambda i: (jnp.minimum(i, lat_tiles - 1), 0)

    def body(of, ob, hf, hb, az, bo, bz, wa_ref, wb_ref, dy_ref, do_ref, dh_ref, daz_ref, dbo_ref, dbz_ref,
             dwa_ref, dwb_ref):
        i = pl.program_id(0)

        @pl.when(i == 0)
        def _():
            dwa_ref[...] = jnp.zeros_like(dwa_ref)
            dwb_ref[...] = jnp.zeros_like(dwb_ref)

        @pl.when(i < lat_tiles)
        def _():
            f = functools.partial(_post_fn, n_hg=n_hg, n_ml=n_ml)
            _, vjp = jax.vjp(f, of[...], ob[...], az[...], hf[...], hb[...], bo[...], bz[...], wa_ref[...], wb_ref[...])
            d_of, _, d_az, d_hf, _, d_bo, d_bz, d_wa, d_wb = vjp(dy_ref[...])
            do_ref[...] = d_of
            dh_ref[...] = d_hf
            daz_ref[...] = d_az
            dbo_ref[...] = d_bo
            dbz_ref[...] = d_bz
            dwa_ref[...] += d_wa
            dwb_ref[...] += d_wb

        @pl.when(i >= lat_tiles)
        def _():
            daz_ref[...] = jnp.zeros_like(daz_ref)
            dbo_ref[...] = jnp.zeros_like(dbo_ref)
            dbz_ref[...] = jnp.zeros_like(dbz_ref)

    lat_rows = pl.BlockSpec((tm, w), lat)
    all_rows = pl.BlockSpec((tm, w), lambda i: (i, 0))
    vec = pl.BlockSpec((1, w), lambda i: (0, 0))
    sd_t = jax.ShapeDtypeStruct((t_rows, w), F32)
    sd_r = jax.ShapeDtypeStruct((r, w), F32)
    sd_v = jax.ShapeDtypeStruct((1, w), F32)
    return _call(
        body, (o_f, o_b, h_f, h_b, u, u, u, wa, wb, dy), name="post_bwd", grid=(r // tm,),
        in_specs=[lat_rows] * 4 + _post_specs(w, tm, lat_tiles, (4, 8, 9)) + [vec, vec]
        + [pl.BlockSpec((tm, 2 * w), lat)],
        out_specs=(lat_rows, lat_rows, all_rows, all_rows, all_rows, vec, vec),
        out_shape=(sd_t, sd_t, sd_r, sd_r, sd_r, sd_v, sd_v), sem=("arbitrary",), rider=rider)


OUT_ROW_GATE, OUT_ROW_LN_G, OUT_ROW_LN_B, OUT_ROW_LOSS = 0, 1, 2, 3


def _out_block(y, w_out, x, target, prm, tm):
    t_rows, dm = x.shape
    di = y.shape[1]

    def body(y_ref, w_ref, x_ref, t_ref, p_ref, dz_ref, dy_ref, gx_ref, acc_ref):
        @pl.when(pl.program_id(0) == 0)
        def _():
            acc_ref[...] = jnp.zeros_like(acc_ref)

        gate, ln_g, ln_b = p_ref[0:1, :], p_ref[1:2, :], p_ref[2:3, :]
        z = _nn(y_ref[...], w_ref[...])
        res = ALPHA * x_ref[...] + gate * z
        mu = jnp.mean(res, axis=-1, keepdims=True)
        rc = res - mu
        rstd = lax.rsqrt(jnp.mean(rc * rc, axis=-1, keepdims=True) + LN_EPS)
        rn = rc * rstd
        err = rn * ln_g + ln_b - t_ref[...]
        d_out = err * (1.0 / dm)
        d_rn = d_out * ln_g
        d_res = rstd * (d_rn - jnp.mean(d_rn, axis=-1, keepdims=True)
                        - rn * jnp.mean(d_rn * rn, axis=-1, keepdims=True))
        acc_ref[OUT_ROW_GATE:OUT_ROW_GATE + 1, :] += jnp.sum(d_res * z, axis=0, keepdims=True)
        acc_ref[OUT_ROW_LN_G:OUT_ROW_LN_G + 1, :] += jnp.sum(d_out * rn, axis=0, keepdims=True)
        acc_ref[OUT_ROW_LN_B:OUT_ROW_LN_B + 1, :] += jnp.sum(d_out, axis=0, keepdims=True)
        acc_ref[OUT_ROW_LOSS:OUT_ROW_LOSS + 1, :] += (0.5 / dm) * jnp.sum(err * err, axis=0, keepdims=True)
        gx_ref[...] = ALPHA * d_res
        dz = (d_res * gate).astype(BF16)
        dz_ref[...] = dz
        dy_ref[...] = _nt(dz, w_ref[...])

    rows_d = pl.BlockSpec((tm, dm), lambda i: (i, 0))
    rows_i = pl.BlockSpec((tm, di), lambda i: (i, 0))
    return pl.pallas_call(
        body, name="out_block", grid=(t_rows // tm,),
        in_specs=[rows_i, pl.BlockSpec((di, dm), lambda i: (0, 0)), rows_d, rows_d,
                  pl.BlockSpec((8, dm), lambda i: (0, 0))],
        out_specs=(rows_d, rows_i, rows_d, pl.BlockSpec((8, dm), lambda i: (0, 0))),
        out_shape=(jax.ShapeDtypeStruct((t_rows, dm), BF16), jax.ShapeDtypeStruct((t_rows, di), F32),
                   jax.ShapeDtypeStruct((t_rows, dm), F32), jax.ShapeDtypeStruct((8, dm), F32)),
        compiler_params=_params(("arbitrary",)),
    )(y, w_out, x, target, prm)


def _mod_fwd(c16, w_mod, tn):
    dm, n = w_mod.shape

    def body(c_ref, w_ref, o_ref, a_ref):
        a = _silu(c_ref[...])
        a_ref[...] = a
        o_ref[...] = _nn(a, w_ref[...], HIGHEST)

    return pl.pallas_call(
        body, name="mod_fwd", grid=(n // tn,),
        in_specs=[pl.BlockSpec((16, dm), lambda j: (0, 0)), pl.BlockSpec((dm, tn), lambda j: (0, j))],
        out_specs=(pl.BlockSpec((16, tn), lambda j: (0, j)), pl.BlockSpec((16, dm), lambda j: (0, 0))),
        out_shape=(jax.ShapeDtypeStruct((16, n), F32), jax.ShapeDtypeStruct((16, dm), F32)),
        compiler_params=_params(("arbitrary",)),
    )(c16, w_mod)


def _mod_bwd(a16, dm16, w_mod, tn):
    dm, n = w_mod.shape

    def body(a_ref, d_ref, w_ref, dw_ref, dc_ref):
        @pl.when(pl.program_id(0) == 0)
        def _():
            dc_ref[...] = jnp.zeros_like(dc_ref)
        dw_ref[...] = _tn(a_ref[...], d_ref[...], HIGHEST)
        dc_ref[...] += _nt(d_ref[...], w_ref[...], HIGHEST)

    return pl.pallas_call(
        body, name="mod_bwd", grid=(n // tn,),
        in_specs=[pl.BlockSpec((16, dm), lambda j: (0, 0)), pl.BlockSpec((16, tn), lambda j: (0, j)),
                  pl.BlockSpec((dm, tn), lambda j: (0, j))],
        out_specs=(pl.BlockSpec((dm, tn), lambda j: (0, j)), pl.BlockSpec((16, dm), lambda j: (0, 0))),
        out_shape=(jax.ShapeDtypeStruct((dm, n), F32), jax.ShapeDtypeStruct((16, dm), F32)),
        compiler_params=_params(("arbitrary",)),
    )(a16, dm16, w_mod)


def _sum_devices(g, fold_rows):
    n_dev, rows, n = g.shape

    def body(g_ref, s_ref, t_ref):
        s = g_ref[0]
        for dev in range(1, n_dev):
            s = s + g_ref[dev]
        t_ref[...] = jnp.broadcast_to(jnp.sum(s, axis=-1, keepdims=True), (rows, LANE))
        s_ref[...] = s
        s_ref[0:fold_rows, :] = s[0:fold_rows] + s[fold_rows:2 * fold_rows]

    return pl.pallas_call(
        body, name="sum_devices",
        out_shape=(jax.ShapeDtypeStruct((rows, n), F32), jax.ShapeDtypeStruct((rows, LANE), F32)),
        compiler_params=_params(),
    )(g)


def _c_ctx_grad(parts, c_ctx_row):
    def body(p_ref, c_ref, o_ref):
        s = p_ref[0]
        for chip in range(1, N_CHIPS):
            s = s + p_ref[2 * chip]
        cv = c_ref[...]
        sg = _sigmoid(cv)
        o_ref[...] = s * (sg * (1.0 + cv * (1.0 - sg)))

    return pl.pallas_call(
        body, name="c_ctx_grad", out_shape=jax.ShapeDtypeStruct(parts.shape[1:], F32), compiler_params=_params(),
    )(parts, c_ctx_row)


def _sum_pair(name, mine, got):
    def body(a_ref, b_ref, o_ref):
        o_ref[...] = (a_ref[...] + b_ref[...]).astype(BF16)

    k, rows, n = mine.shape
    tl = _largest_divisor(n, max(LANE, (1 << 18) // rows), LANE)
    spec = pl.BlockSpec((1, rows, tl), lambda kk, i: (kk, 0, i))
    return pl.pallas_call(
        body, name=name, grid=(k, n // tl), in_specs=[spec, spec], out_specs=spec,
        out_shape=jax.ShapeDtypeStruct(mine.shape, BF16), compiler_params=_params(("parallel", "parallel")),
    )(mine, got)


def _sum_chips(name, got):
    k, rows, n = got.shape
    tl = _largest_divisor(n, max(LANE, (1 << 18) // rows), LANE)

    def body(g_ref, o_ref):
        total = g_ref[0].astype(F32)
        for kk in range(1, k):
            total = total + g_ref[kk].astype(F32)
        o_ref[...] = total

    return pl.pallas_call(
        body, name=name, grid=(n // tl,),
        in_specs=[pl.BlockSpec((k, rows, tl), lambda i: (0, 0, i))], out_specs=pl.BlockSpec((rows, tl), lambda i: (0, i)),
        out_shape=jax.ShapeDtypeStruct((rows, n), F32), compiler_params=_params(("parallel",)),
    )(got)


def _adamw_update(w, g, m, v):
    m2 = ADAM_B1 * m + (1.0 - ADAM_B1) * g
    v2 = ADAM_B2 * v + (1.0 - ADAM_B2) * jnp.square(g)
    m_hat = m2 / (1.0 - ADAM_B1 ** ADAM_STEP)
    v_hat = v2 / (1.0 - ADAM_B2 ** ADAM_STEP)
    return -ADAM_LR * (m_hat / (jnp.sqrt(v_hat) + ADAM_EPS) + ADAM_WD * w), m2, v2


def _adamw(name, w, g, m, v, rider=None):
    rows, n = w.shape
    if rows % 8 == 0:
        tr = _largest_divisor(rows, max(8, (1 << 18) // n), 8)
        block, index, steps = (tr, n), (lambda i: (i, 0)), rows // tr
    else:
        tl = _largest_divisor(n, max(LANE, (1 << 18) // rows), LANE)
        block, index, steps = (rows, tl), (lambda i: (0, i)), n // tl

    def body(w_ref, g_ref, m_ref, v_ref, d_ref, mo_ref, vo_ref):
        d_ref[...], mo_ref[...], vo_ref[...] = _adamw_update(w_ref[...], g_ref[...], m_ref[...], v_ref[...])

    spec = pl.BlockSpec(block, index)
    sds = jax.ShapeDtypeStruct((rows, n), F32)
    return _call(body, (w, g, m, v), name=name, grid=(steps,), in_specs=[spec] * 4, out_specs=(spec,) * 3,
                 out_shape=(sds, sds, sds), sem=("parallel",), rider=rider)


PACK_LANES = 1024


def _pack(pieces):
    flat = jnp.concatenate([p.reshape(-1) for p in pieces])
    total = -(-flat.shape[0] // (8 * PACK_LANES)) * 8 * PACK_LANES
    return jnp.pad(flat, (0, total - flat.shape[0])).reshape(-1, PACK_LANES)


def _unpack(packed, shapes):
    flat = packed.reshape(-1)
    out, off = [], 0
    for shp in shapes:
        size = math.prod(shp)
        out.append(flat[off:off + size].reshape(shp))
        off += size
    return out


def _rows8(rows, width):
    flat = [r.reshape(width) for r in rows] + [jnp.zeros(((8 - len(rows)) * width,), F32)]
    return jnp.concatenate(flat).reshape(8, width)


def kernel(x, c, ctx, c_ctx, w_mod, b_mod, w_in, conv_w, conv_b, hg_lb, ml_gate_b, hg_norm_w, ml_norm_w, w_out, ln_g, ln_b, loss_target, m_c_ctx, m_w_mod, m_b_mod, m_w_in, m_conv_w, m_conv_b, m_hg_lb, m_ml_gate_b, m_hg_norm_w, m_ml_norm_w, m_w_out, m_ln_g, m_ln_b, v_c_ctx, v_w_mod, v_b_mod, v_w_in, v_conv_w, v_conv_b, v_hg_lb, v_ml_gate_b, v_hg_norm_w, v_ml_norm_w, v_w_out, v_ln_g, v_ln_b):
    t_rows, dm = x.shape[1], x.shape[2]
    c_rows = ctx.shape[1]
    w = hg_norm_w.shape[1]
    n_ml = ml_gate_b.shape[-1]
    n_hg = w // HG_DK
    di = 2 * w
    n_in = 10 * w + 4 * n_ml
    ns = w_in.shape[2]
    nm = w_mod.shape[2]
    n_pad = 10 * w + LANE
    r_rows = t_rows + c_rows
    row_gcd = math.gcd(t_rows, c_rows)
    hg_chunk, ml_chunk = math.gcd(HG_CHUNK, row_gcd), math.gcd(ML_CHUNK, row_gcd)
    hg_counts = (t_rows // hg_chunk, c_rows // hg_chunk, hg_chunk)
    ml_counts = (t_rows // ml_chunk, c_rows // ml_chunk, ml_chunk)
    assert ml_norm_w.shape[1] == w and di == dm and N_CHIPS * ns == n_in and N_CHIPS * nm == 3 * dm
    assert w_out.shape[1] * N_CHIPS == di and 4 * n_ml <= LANE and t_rows % GRID_W == 0

    xi, yi, ci = lax.axis_index("x"), lax.axis_index("y"), lax.axis_index("c")
    chip = 2 * xi + yi
    dev = 4 * xi + 2 * yi + ci

    tm = _largest_divisor(math.gcd(t_rows, c_rows), 256, 8)
    tm_mm = _largest_divisor(r_rows, 1088, SUBLANE_BF16)
    tn_mm = LANE * _largest_divisor(n_pad // LANE, 9)
    tn_mod = _largest_divisor(nm, 512, LANE)

    shard_shapes = [(dm,), (2, 2, w // N_CHIPS), (3, 3, di // N_CHIPS)]
    g1 = _all_gather8(_pack([c, hg_lb, conv_w])).run("gather_inputs")[0]
    per_dev = [_unpack(g1[i], shard_shapes) for i in range(N_DEV)]
    c_all = jnp.stack([p[0] for p in per_dev])
    lb_full = jnp.concatenate([per_dev[2 * k][1] for k in range(N_CHIPS)], axis=-1)
    conv_w9 = jnp.concatenate([per_dev[2 * k][2] for k in range(N_CHIPS)], axis=-1).reshape(9, di)

    c16 = jnp.concatenate([c_all, c_ctx[None], jnp.zeros((16 - N_DEV - 1, dm), F32)])
    mod_part, a16 = _mod_fwd(c16, w_mod[0], tn_mod)
    g2 = _all_gather8(mod_part).run("gather_mod")[0]
    mod_all = jnp.concatenate([g2[2 * k] for k in range(N_CHIPS)], axis=1) + b_mod
    mod_x = lax.dynamic_index_in_dim(mod_all, dev, 0, keepdims=False).reshape(3, dm)
    mod_c = mod_all[N_DEV].reshape(3, dm)
    prm = jnp.stack([_rows8(list(mod_x), dm), _rows8(list(mod_c), dm)])

    as_t = lambda a: jnp.transpose(a[0])
    half_in = lax.dynamic_slice_in_dim(as_t(w_in).astype(BF16), ci * (dm // 2), dm // 2, 1)
    half_out = lax.dynamic_slice_in_dim(w_out[0].astype(BF16), ci * (di // (2 * N_CHIPS)), di // (2 * N_CHIPS), 0)
    fetched_in = _own_block(chip, half_in, _all_gather_chips([half_in]).run("gather_w_in")[0])
    gw_in = _join_halves(ci, fetched_in, _sibling_swap([fetched_in]).run("gather_w_in_pair")[0], 2)
    wt_full = jnp.concatenate([gw_in.reshape(n_in, dm), jnp.zeros((n_pad - n_in, dm), BF16)])

    hc = _modulate_fwd(x[0], ctx[0], prm, tm)
    u, (got_out,) = _mm_nt("in_proj", hc, wt_full, tm_mm, tn_mm, F32, rider=_all_gather_chips([half_out]))
    fetched_out = _own_block(chip, half_out, got_out)
    (o_f, o_b), hg_saved, (swapped_out,) = _hg_scan_fwd(u, lb_full, w, *hg_counts,
                                                         rider=_sibling_swap([fetched_out]))
    w_out_full = _join_halves(ci, fetched_out, swapped_out, 1).reshape(di, dm)
    qk = _conv_fwd(u, conv_w9, conv_b, t_rows, c_rows, w, LANE)
    gate_b_row = jnp.pad(ml_gate_b.reshape(1, -1), ((0, 0), (0, LANE - 4 * n_ml)))
    (h_f, h_b), ml_saved = _ml_scan_fwd(qk, u, gate_b_row, w, n_ml, *ml_counts)
    y = _post_fwd(o_f, o_b, h_f, h_b, u, hg_norm_w, ml_norm_w, t_rows, w, n_hg, n_ml, tm)
    prm_out = _rows8([mod_x[2], ln_g, ln_b], dm)
    dz, dy, gx_direct, acc_out = _out_block(y, w_out_full, x[0], loss_target[0], prm_out, tm // 2)

    d_w_out = _mm_tn("d_w_out", y, dz, _largest_divisor(di, 1024, LANE),
                     _largest_divisor(t_rows, 1024, SUBLANE_BF16))
    d_w_out4 = d_w_out.reshape(N_CHIPS, 2, di // (2 * N_CHIPS), dm)
    mine_out = lax.dynamic_index_in_dim(d_w_out4, ci, 1, keepdims=False)
    other_out = lax.dynamic_index_in_dim(d_w_out4, 1 - ci, 1, keepdims=False)
    (d_o, d_h, d_az, d_bo, d_bz, d_wa, d_wb), (got_out,) = _post_bwd(
        o_f, o_b, h_f, h_b, u, hg_norm_w, ml_norm_w, dy, t_rows, w, n_hg, n_ml, tm, rider=_sibling_swap([other_out]))
    pair_out = _sum_pair("rs_pair_sum_w_out", mine_out, got_out)
    (d_aq_f, d_aff, d_ai_f, d_lb_f), (d_aq_b, d_afb, d_ai_b, d_lb_b), (landed_out,) = _hg_scan_bwd(
        u, lb_full, hg_saved, d_o, w, *hg_counts, rider=_chip_scatter([pair_out]))
    half_g_out = _sum_chips("rs_chip_sum_w_out", _own_block(chip, pair_out, landed_out))
    (d_qk_f, d_v_f, d_g_f, d_gb_f), (d_qk_b, d_v_b, d_g_b, d_gb_b), (sibling_out,) = _ml_scan_bwd(
        qk, u, gate_b_row, ml_saved, d_h, w, n_ml, *ml_counts, rider=_sibling_swap([half_g_out]))
    g_w_out = _join_halves(ci, half_g_out, sibling_out, 0)
    d_bqk, d_cw, d_cb = _conv_bwd(u, (d_qk_f, d_qk_b), conv_w9, conv_b, t_rows, c_rows, w, LANE)
    du = _assemble_du([(d_aq_f, d_aq_b), d_aff, d_afb, (d_ai_f, d_ai_b), d_az, d_bqk, (d_v_f, d_v_b), d_bo, d_bz],
                      (d_g_f, d_g_b), n_pad, tm // 2)
    d_wt_in = _mm_tn("d_w_in", du, hc, tn_mm, tm_mm)

    mine_in = lax.dynamic_slice_in_dim(d_wt_in, ci * (dm // 2), dm // 2, 1)
    other_in = lax.dynamic_slice_in_dim(d_wt_in, (1 - ci) * (dm // 2), dm // 2, 1)
    pieces_in = lambda a: jnp.stack([a[k * ns:(k + 1) * ns] for k in range(N_CHIPS)])
    got_in = _sibling_swap([pieces_in(other_in)]).run("rs_pair_w_in")[0]
    pair_in = _sum_pair("rs_pair_sum_w_in", pieces_in(mine_in), got_in)
    d_hc, (landed_in,) = _mm_acc("d_h", du, wt_full, tm_mm, tn_mm, rider=_chip_scatter([pair_in]))
    half_g_in = _sum_chips("rs_chip_sum_w_in", _own_block(chip, pair_in, landed_in))
    (gx, acc_mod), (sibling_in,) = _modulate_bwd(x[0], ctx[0], d_hc, prm, gx_direct, tm,
                                                 rider=_sibling_swap([half_g_in]))
    g_wt_in = _join_halves(ci, half_g_in, sibling_in, 1)
    grad_x = gx[None]

    zero_row = jnp.zeros((dm,), F32)
    d_gb = jnp.concatenate([d_gb_f[:, 0:n_ml], d_gb_b[:, n_ml:2 * n_ml], d_gb_f[:, 2 * n_ml:3 * n_ml],
                            d_gb_b[:, 3 * n_ml:4 * n_ml], jnp.zeros((1, dm - 4 * n_ml), F32)], axis=1)
    rows = [acc_mod[0, 0], acc_mod[0, 1], acc_out[OUT_ROW_GATE],
            acc_mod[1, 0], acc_mod[1, 1], zero_row]
    rows += list(d_cw) + [d_cb[0], d_lb_f.reshape(dm), d_lb_b.reshape(dm),
                          jnp.concatenate([d_wa[0], d_wb[0]]), acc_out[OUT_ROW_LN_G], acc_out[OUT_ROW_LN_B],
                          acc_out[OUT_ROW_LOSS], d_gb[0], zero_row]
    ROW_CW, ROW_CB, ROW_LB, ROW_NORM, ROW_LN_G, ROW_LN_B, ROW_LOSS, ROW_GB = 6, 15, 16, 18, 19, 20, 21, 22
    delta, new_m, new_v = {}, {}, {}
    small_rows = jnp.concatenate([r.reshape(dm) for r in rows]).reshape(len(rows), dm)
    res, (g3,) = _adamw("adamw_w_in", as_t(w_in), g_wt_in, as_t(m_w_in), as_t(v_w_in), rider=_all_gather8(small_rows))
    delta["w_in"], new_m["w_in"], new_v["w_in"] = (jnp.transpose(a)[None] for a in res)
    sums, totals = _sum_devices(g3, 3)
    loss = totals[ROW_LOSS, 0]
    dm16 = jnp.concatenate([g3[:, 0:3, :].reshape(N_DEV, 3 * dm), sums[3:6].reshape(1, 3 * dm),
                            jnp.zeros((16 - N_DEV - 1, 3 * dm), F32)])
    g_w_mod, dc16 = _mod_bwd(a16, lax.dynamic_slice_in_dim(dm16, chip * nm, nm, 1), w_mod[0], tn_mod)
    res, (g4,) = _adamw("adamw_w_mod", w_mod[0], g_w_mod, m_w_mod[0], v_w_mod[0],
                        rider=_all_gather8(jnp.pad(dc16[N_DEV:N_DEV + 1], ((0, 7), (0, 0)))))
    delta["w_mod"], new_m["w_mod"], new_v["w_mod"] = (a[None] for a in res)
    g_c_ctx = _c_ctx_grad(g4, jnp.broadcast_to(c_ctx[None], (8, dm)))[0]

    chip_cols = lambda a, width: lax.dynamic_slice_in_dim(a, chip * width, width, a.ndim - 1)
    grads = {
        "c_ctx": g_c_ctx,
        "w_mod": g_w_mod[None],
        "b_mod": sums[0:3].reshape(1, 3 * dm),
        "w_in": jnp.transpose(g_wt_in)[None],
        "conv_w": chip_cols(sums[ROW_CW:ROW_CW + 9].reshape(1, 3, 3, di), di // N_CHIPS),
        "conv_b": sums[ROW_CB][None],
        "hg_lb": chip_cols(sums[ROW_LB:ROW_LB + 2].reshape(2, 2, w), w // N_CHIPS),
        "ml_gate_b": sums[ROW_GB, 0:4 * n_ml].reshape(1, 4, n_ml),
        "hg_norm_w": sums[ROW_NORM, 0:w][None],
        "ml_norm_w": sums[ROW_NORM, w:2 * w][None],
        "w_out": g_w_out[None],
        "ln_g": sums[ROW_LN_G][None],
        "ln_b": sums[ROW_LN_B][None],
    }
    weights = dict(c_ctx=c_ctx, w_mod=w_mod, b_mod=b_mod, w_in=w_in, conv_w=conv_w, conv_b=conv_b, hg_lb=hg_lb,
                   ml_gate_b=ml_gate_b, hg_norm_w=hg_norm_w, ml_norm_w=ml_norm_w, w_out=w_out, ln_g=ln_g, ln_b=ln_b)
    mom1 = dict(c_ctx=m_c_ctx, w_mod=m_w_mod, b_mod=m_b_mod, w_in=m_w_in, conv_w=m_conv_w, conv_b=m_conv_b,
                hg_lb=m_hg_lb, ml_gate_b=m_ml_gate_b, hg_norm_w=m_hg_norm_w, ml_norm_w=m_ml_norm_w, w_out=m_w_out,
                ln_g=m_ln_g, ln_b=m_ln_b)
    mom2 = dict(c_ctx=v_c_ctx, w_mod=v_w_mod, b_mod=v_b_mod, w_in=v_w_in, conv_w=v_conv_w, conv_b=v_conv_b,
                hg_lb=v_hg_lb, ml_gate_b=v_ml_gate_b, hg_norm_w=v_hg_norm_w, ml_norm_w=v_ml_norm_w, w_out=v_w_out,
                ln_g=v_ln_g, ln_b=v_ln_b)
    names = list(weights)
    big = ("w_mod", "w_in", "w_out")
    small = [n for n in names if n not in big]

    res, _ = _adamw("adamw_w_out", w_out[0], g_w_out, m_w_out[0], v_w_out[0])
    delta["w_out"], new_m["w_out"], new_v["w_out"] = (a[None] for a in res)
    small_shapes = [weights[n].shape for n in small]
    res, _ = _adamw("adamw_small", *(_pack([src[n] for n in small]) for src in (weights, grads, mom1, mom2)))
    for out, packed in zip((delta, new_m, new_v), res):
        for n, a in zip(small, _unpack(packed, small_shapes)):
            out[n] = a

    return (loss, grad_x, *[grads[n].reshape(weights[n].shape) for n in names], *[delta[n] for n in names],
            *[new_m[n] for n in names], *[new_v[n] for n in names])
```

```python
import functools
import math

import jax
import jax.numpy as jnp
from jax import lax
from jax.experimental import pallas as pl
from jax.experimental.pallas import tpu as pltpu

F32 = jnp.float32
BF16 = jnp.bfloat16
HIGHEST = lax.Precision.HIGHEST
MESH = pl.DeviceIdType.MESH

HG_CHUNK = 64
ML_CHUNK = 256
GRID_W = 64
HG_DK = 128
LANE = 128
SUBLANE_BF16 = 16
ALPHA = 2.0 ** 0.25
LN_EPS = 1e-5
NORM_EPS = 1e-6
ADAM_LR = 0.001
ADAM_B1 = 0.9
ADAM_B2 = 0.999
ADAM_EPS = 1e-08
ADAM_WD = 0.01
ADAM_STEP = 10
VMEM_LIMIT = 56 * 1024 * 1024
N_CHIPS = 4
N_DEV = 8


def _params(sem=None):
    return pltpu.CompilerParams(dimension_semantics=sem, vmem_limit_bytes=VMEM_LIMIT)


def _largest_divisor(n, cap, multiple=1):
    best = None
    for d in range(multiple, min(n, cap) + 1, multiple):
        if n % d == 0:
            best = d
    assert best is not None, (n, cap, multiple)
    return best


def _sigmoid(x):
    return jax.nn.sigmoid(x)


def _silu(x):
    return x * jax.nn.sigmoid(x)


def _dot(a, b, dims, precision=None):
    return lax.dot_general(a, b, (dims, ((), ())), precision=precision, preferred_element_type=F32)


def _nn(a, b, precision=None):
    return _dot(a, b, ((1,), (0,)), precision)


def _nt(a, b, precision=None):
    return _dot(a, b, ((1,), (1,)), precision)


def _tn(a, b, precision=None):
    return _dot(a, b, ((0,), (0,)), precision)


def _visible(n, rev):
    r = lax.broadcasted_iota(jnp.int32, (n, n), 0)
    c = lax.broadcasted_iota(jnp.int32, (n, n), 1)
    return (r <= c) if rev else (r >= c)


def _mask_matmul(mask, x):
    mb = mask.astype(BF16)
    hi = x.astype(BF16)
    lo = (x - hi.astype(F32)).astype(BF16)
    return _nn(mb, hi) + _nn(mb, lo)


@functools.partial(jax.custom_vjp, nondiff_argnums=(1,))
def _cumulative(x, rev):
    return _mask_matmul(_visible(x.shape[0], rev), x)


def _cumulative_fwd(x, rev):
    return _cumulative(x, rev), None


def _cumulative_bwd(rev, _, ct):
    return (_mask_matmul(_visible(ct.shape[0], not rev), ct),)


_cumulative.defvjp(_cumulative_fwd, _cumulative_bwd)


def _hg_chunk(states, aq, af, ai, lb0, lb1, rev):
    n_heads = len(states)
    lb = _sigmoid(lb0 - lb1)
    f = lb + (1.0 - lb) * _sigmoid(af)
    g = jnp.log(f)
    k = 1.0 - f
    q = _silu(aq)
    chunk = aq.shape[0]
    vis = _visible(chunk, rev)
    b = _cumulative(g, rev)
    last = 0 if rev else chunk - 1
    b_end = b[last:last + 1]
    b_mid = b[chunk // 2:chunk // 2 + 1]
    q_inter = q * jnp.exp(b)
    q_intra = q * jnp.exp(b - b_mid)
    k_intra = k * jnp.exp(b_mid - b)
    k_dec = k * jnp.exp(b_end - b)
    e_end = jnp.exp(b_end)
    new_states, outs = [], []
    for h in range(n_heads):
        sl = slice(h * HG_DK, (h + 1) * HG_DK)
        s_t = states[h]
        scores = jnp.where(vis, _nt(q_intra[:, sl], k_intra[:, sl]), 0.0)
        outs.append(_nt(q_inter[:, sl], s_t) + _nn(scores, ai[:, sl]))
        new_states.append(e_end[:, sl] * s_t + _tn(ai[:, sl], k_dec[:, sl]))
    return new_states, jnp.concatenate(outs, axis=1)


def _ml_chunk(state, q, k, v, g, gb, rev, d):
    cms, nvs, mbs = state
    n_heads = len(cms)
    dh = q.shape[1] // n_heads
    ga = g + gb
    log_f_all = jax.nn.log_sigmoid(ga)
    chunk = q.shape[0]
    vis = _visible(chunk, rev)
    b_all = _cumulative(log_f_all, rev)
    last = 0 if rev else chunk - 1
    k = k * (dh ** -0.5)
    new_c, new_n, new_m, outs = [], [], [], []
    for h in range(n_heads):
        ci = d * n_heads + h
        cf = (2 + d) * n_heads + h
        sl = slice(h * dh, (h + 1) * dh)
        qh, kh, vh = q[:, sl], k[:, sl], v[:, sl]
        li = ga[:, ci:ci + 1]
        b = b_all[:, cf:cf + 1]
        m = mbs[h][:, 0:1]
        row = jnp.transpose(li - b)
        log_w = jnp.where(vis, b + row, -jnp.inf)
        m_inter = b + m
        m_t = jnp.maximum(m_inter, jnp.max(log_w, axis=-1, keepdims=True))
        w_inter = jnp.exp(m_inter - m_t)
        w_qk = jnp.exp(log_w - m_t) * _nt(qh, kh)
        num = w_inter * _nt(qh, cms[h]) + _nn(w_qk, vh)
        den = w_inter * jnp.sum(qh * nvs[h], axis=-1, keepdims=True) + jnp.sum(w_qk, axis=-1, keepdims=True)
        outs.append(num / jnp.maximum(jnp.abs(den), jnp.exp(-m_t)))
        m_new = m_t[last:last + 1]
        b_end = b[last:last + 1]
        w_s = jnp.exp(b_end - b + li - m_new)
        decay = jnp.exp(b_end + m - m_new)
        new_c.append(decay * cms[h] + _tn(w_s * vh, kh))
        new_n.append(decay * nvs[h] + jnp.sum(w_s * kh, axis=0, keepdims=True))
        new_m.append(jnp.broadcast_to(m_new, (1, LANE)))
    return (new_c, new_n, new_m), jnp.concatenate(outs, axis=1)


def _post_fn(o_f, o_b, az, h_f, h_b, bo, bz, wa, wb, n_hg, n_ml):
    o = o_f + o_b
    parts = []
    for h in range(n_hg):
        s = o[:, h * HG_DK:(h + 1) * HG_DK]
        parts.append(s * lax.rsqrt(jnp.mean(s * s, axis=-1, keepdims=True) + NORM_EPS))
    y_a = jnp.concatenate(parts, axis=1) * wa * _silu(az)
    hh = h_f + h_b
    dh = hh.shape[1] // n_ml
    parts = []
    for h in range(n_ml):
        s = hh[:, h * dh:(h + 1) * dh]
        mu = jnp.mean(s, axis=-1, keepdims=True)
        sc = s - mu
        parts.append(sc * lax.rsqrt(jnp.mean(sc * sc, axis=-1, keepdims=True) + NORM_EPS))
    y_b = jnp.concatenate(parts, axis=1) * wb * _sigmoid(bo) * _silu(bz)
    return jnp.concatenate([y_a, y_b], axis=1)


def _chip_of(dev):
    return 2 * dev[0] + dev[1]


def _index_of(dev):
    return 4 * dev[0] + 2 * dev[1] + dev[2]


class _Exchange:
    def __init__(self, srcs, out_shapes, transfers, local_copies=()):
        self.srcs, self.out_shapes = list(srcs), list(out_shapes)
        self.transfers, self.local_copies = list(transfers), list(local_copies)

    def scratch(self):
        return [pltpu.SemaphoreType.DMA((len(self.transfers),)), pltpu.SemaphoreType.DMA((len(self.transfers),)),
                pltpu.SemaphoreType.DMA((max(len(self.local_copies), 1),))]

    def copies(self, ins, outs, send_sems, recv_sems, local_sems):
        me = (lax.axis_index("x"), lax.axis_index("y"), lax.axis_index("c"))

        def pick(ref, fn, *who):
            return ref if fn is None else ref.at[fn(*who)]

        sends, recvs, locs = [], [], []
        for t, (mask, si, sfn, di, dfn) in enumerate(self.transfers):
            peer = tuple(1 - p if flip else p for p, flip in zip(me, mask))
            sends.append(pltpu.make_async_remote_copy(
                src_ref=pick(ins[si], sfn, me, peer), dst_ref=pick(outs[di], dfn, me, peer),
                send_sem=send_sems.at[t], recv_sem=recv_sems.at[t], device_id=peer, device_id_type=MESH))
            landing = pick(outs[di], dfn, peer, me)
            recvs.append(pltpu.make_async_remote_copy(
                src_ref=landing, dst_ref=landing,
                send_sem=send_sems.at[t], recv_sem=recv_sems.at[t], device_id=peer, device_id_type=MESH))
        for l, (si, sfn, di, dfn) in enumerate(self.local_copies):
            locs.append(pltpu.make_async_copy(pick(ins[si], sfn, me), pick(outs[di], dfn, me), local_sems.at[l]))

        def start():
            for cp in locs + sends:
                cp.start()

        def wait():
            for cp in recvs:
                cp.wait_recv()
            for cp in sends:
                cp.wait_send()
            for cp in locs:
                cp.wait()

        return start, wait

    def run(self, name):
        n_in, n_out = len(self.srcs), len(self.out_shapes)

        def body(*refs):
            start, wait = self.copies(refs[:n_in], refs[n_in:n_in + n_out], *refs[n_in + n_out:])
            start()
            wait()

        hbm = pl.BlockSpec(memory_space=pltpu.HBM)
        return pl.pallas_call(
            body, name=name, out_shape=tuple(self.out_shapes), in_specs=[hbm] * n_in,
            out_specs=tuple([hbm] * n_out), scratch_shapes=self.scratch(),
        )(*self.srcs)


def _call(body, operands, *, name, grid, in_specs, out_specs, out_shape, scratch_shapes=(), sem=None, rider=None):
    out_specs, out_shape, scratch_shapes = list(out_specs), list(out_shape), list(scratch_shapes)
    if rider is None:
        res = pl.pallas_call(
            body, name=name, grid=grid, in_specs=list(in_specs), out_specs=tuple(out_specs),
            out_shape=tuple(out_shape), scratch_shapes=scratch_shapes, compiler_params=_params(sem),
        )(*operands)
        return list(res), []
    counts = (len(in_specs), len(rider.srcs), len(out_specs), len(rider.out_shapes), len(scratch_shapes), 3)

    def full(*refs):
        groups, pos = [], 0
        for k in counts:
            groups.append(refs[pos:pos + k])
            pos += k
        own_in, ex_in, own_out, ex_out, own_scr, ex_scr = groups
        ids = [pl.program_id(a) for a in range(len(grid))]
        first = functools.reduce(jnp.logical_and, [i == 0 for i in ids])
        last = functools.reduce(jnp.logical_and, [i == g - 1 for i, g in zip(ids, grid)])
        start, wait = rider.copies(ex_in, ex_out, *ex_scr)
        pl.when(first)(start)
        body(*own_in, *own_out, *own_scr)
        pl.when(last)(wait)

    hbm = pl.BlockSpec(memory_space=pltpu.HBM)
    res = pl.pallas_call(
        full, name=name, grid=grid, in_specs=list(in_specs) + [hbm] * counts[1],
        out_specs=tuple(out_specs + [hbm] * counts[3]), out_shape=tuple(out_shape + rider.out_shapes),
        scratch_shapes=scratch_shapes + rider.scratch(), compiler_params=_params(("arbitrary",) * len(grid)),
    )(*operands, *rider.srcs)
    return list(res[:counts[2]]), list(res[counts[2]:])


ALL_MASKS = [(mx, my, mc) for mx in (0, 1) for my in (0, 1) for mc in (0, 1)][1:]
CHIP_MASKS = [(1, 0, 0), (0, 1, 0), (1, 1, 0)]
SIBLING_MASK = (0, 0, 1)


def _all_gather8(v):
    out = jax.ShapeDtypeStruct((N_DEV,) + v.shape, v.dtype)
    slot = lambda sender, receiver: _index_of(sender)
    transfers = [(mask, 0, None, 0, slot) for mask in ALL_MASKS]
    return _Exchange([v], [out], transfers, [(0, None, 0, lambda me: _index_of(me))])


def _all_gather_chips(arrays):
    outs = [jax.ShapeDtypeStruct((N_CHIPS,) + a.shape, a.dtype) for a in arrays]
    slot = lambda sender, receiver: _chip_of(sender)
    return _Exchange(arrays, outs, [(mask, i, None, i, slot) for i in range(len(arrays)) for mask in CHIP_MASKS])


def _sibling_swap(arrays):
    outs = [jax.ShapeDtypeStruct(a.shape, a.dtype) for a in arrays]
    return _Exchange(arrays, outs, [(SIBLING_MASK, i, None, i, None) for i in range(len(arrays))])


def _chip_scatter(arrays):
    outs = [jax.ShapeDtypeStruct(a.shape, a.dtype) for a in arrays]
    transfers = [(mask, i, lambda s, r: _chip_of(r), i, lambda s, r: _chip_of(s))
                 for i in range(len(arrays)) for mask in CHIP_MASKS]
    return _Exchange(arrays, outs, transfers)


def _own_block(chip, own, blocks):
    sel = (lax.broadcasted_iota(jnp.int32, (N_CHIPS,) + (1,) * (blocks.ndim - 1), 0) == chip)
    return jnp.where(sel, own if own.ndim == blocks.ndim else own[None], blocks)


def _join_halves(ci, mine, other, axis):
    return jnp.where(ci == 0, jnp.concatenate([mine, other], axis=axis), jnp.concatenate([other, mine], axis=axis))


def _mm_nt(name, a, b, tm, tn, out_dtype, rider=None):
    m, k = a.shape
    n = b.shape[0]

    def body(a_ref, b_ref, o_ref):
        o_ref[...] = _nt(a_ref[...], b_ref[...]).astype(out_dtype)

    (out,), rode = _call(
        body, (a, b), name=name, grid=(n // tn, m // tm),
        in_specs=[pl.BlockSpec((tm, k), lambda j, i: (i, 0)), pl.BlockSpec((tn, k), lambda j, i: (j, 0))],
        out_specs=[pl.BlockSpec((tm, tn), lambda j, i: (i, j))],
        out_shape=[jax.ShapeDtypeStruct((m, n), out_dtype)], sem=("parallel", "parallel"), rider=rider)
    return out, rode


def _mm_acc(name, a, b, tm, tk, rider=None):
    m, kc = a.shape
    n = b.shape[1]

    def body(a_ref, b_ref, o_ref):
        @pl.when(pl.program_id(1) == 0)
        def _():
            o_ref[...] = jnp.zeros_like(o_ref)
        o_ref[...] += _nn(a_ref[...], b_ref[...])

    (out,), rode = _call(
        body, (a, b), name=name, grid=(m // tm, kc // tk),
        in_specs=[pl.BlockSpec((tm, tk), lambda i, kk: (i, kk)), pl.BlockSpec((tk, n), lambda i, kk: (kk, 0))],
        out_specs=[pl.BlockSpec((tm, n), lambda i, kk: (i, 0))],
        out_shape=[jax.ShapeDtypeStruct((m, n), F32)], sem=("parallel", "arbitrary"), rider=rider)
    return out, rode


def _mm_tn(name, a, b, tm, tk, with_bf16=False):
    kr, m = a.shape
    n = b.shape[1]
    steps_k = kr // tk

    def body(a_ref, b_ref, o_ref, *narrow):
        @pl.when(pl.program_id(1) == 0)
        def _():
            o_ref[...] = jnp.zeros_like(o_ref)
        o_ref[...] += _tn(a_ref[...], b_ref[...])
        if with_bf16:
            @pl.when(pl.program_id(1) == steps_k - 1)
            def _():
                narrow[0][...] = o_ref[...].astype(BF16)

    out_spec = pl.BlockSpec((tm, n), lambda i, kk: (i, 0))
    res = pl.pallas_call(
        body, name=name, grid=(m // tm, steps_k),
        in_specs=[pl.BlockSpec((tk, tm), lambda i, kk: (kk, i)), pl.BlockSpec((tk, n), lambda i, kk: (kk, 0))],
        out_specs=(out_spec,) * (2 if with_bf16 else 1),
        out_shape=(jax.ShapeDtypeStruct((m, n), F32),) + ((jax.ShapeDtypeStruct((m, n), BF16),) if with_bf16 else ()),
        compiler_params=_params(("parallel", "arbitrary")),
    )(a, b)
    return res if with_bf16 else res[0]


def _modulate_fwd(x, ctx, prm, tm):
    t_rows, dm = x.shape
    lat = t_rows // tm
    r = t_rows + ctx.shape[0]

    def body(x_ref, c_ref, p_ref, h_ref):
        xv = jnp.where(pl.program_id(0) >= lat, c_ref[...], x_ref[...])
        mu = jnp.mean(xv, axis=-1, keepdims=True)
        xm = xv - mu
        n = xm * lax.rsqrt(jnp.mean(xm * xm, axis=-1, keepdims=True) + LN_EPS)
        h_ref[...] = (n * (1.0 + p_ref[0, 1:2, :]) + p_ref[0, 0:1, :]).astype(BF16)

    return pl.pallas_call(
        body, name="modulate_fwd", grid=(r // tm,),
        in_specs=[pl.BlockSpec((tm, dm), lambda i: (jnp.minimum(i, lat - 1), 0)),
                  pl.BlockSpec((tm, dm), lambda i: (jnp.maximum(i - lat, 0), 0)),
                  pl.BlockSpec((1, 8, dm), lambda i: ((i >= lat).astype(jnp.int32), 0, 0))],
        out_specs=pl.BlockSpec((tm, dm), lambda i: (i, 0)),
        out_shape=jax.ShapeDtypeStruct((r, dm), BF16),
        compiler_params=_params(("parallel",)),
    )(x, ctx, prm)


def _modulate_bwd(x, ctx, dh, prm, gx_direct, tm, rider=None):
    t_rows, dm = x.shape
    lat, n_ct = t_rows // tm, ctx.shape[0] // tm
    is_ctx = lambda i: i < n_ct
    cls = lambda i: is_ctx(i).astype(jnp.int32)
    lat_tile = lambda i: (jnp.maximum(i - n_ct, 0), 0)

    def body(x_ref, c_ref, dh_ref, p_ref, gd_ref, gx_ref, acc_ref):
        i = pl.program_id(0)

        @pl.when((i == 0) | (i == n_ct))
        def _():
            acc_ref[...] = jnp.zeros_like(acc_ref)

        x = jnp.where(is_ctx(i), c_ref[...], x_ref[...])
        dh_v = dh_ref[...]
        mu = jnp.mean(x, axis=-1, keepdims=True)
        xm = x - mu
        rstd = lax.rsqrt(jnp.mean(xm * xm, axis=-1, keepdims=True) + LN_EPS)
        n = xm * rstd
        acc_ref[0, 0:1, :] += jnp.sum(dh_v, axis=0, keepdims=True)
        acc_ref[0, 1:2, :] += jnp.sum(dh_v * n, axis=0, keepdims=True)
        dn = dh_v * (1.0 + p_ref[0, 1:2, :])
        dx = rstd * (dn - jnp.mean(dn, axis=-1, keepdims=True) - n * jnp.mean(dn * n, axis=-1, keepdims=True))
        gx_ref[...] = dx + gd_ref[...]

    return _call(
        body, (x, ctx, dh, prm, gx_direct), name="modulate_bwd", grid=(n_ct + lat,),
        in_specs=[pl.BlockSpec((tm, dm), lat_tile),
                  pl.BlockSpec((tm, dm), lambda i: (jnp.minimum(i, n_ct - 1), 0)),
                  pl.BlockSpec((tm, dm), lambda i: (jnp.where(is_ctx(i), lat + i, i - n_ct), 0)),
                  pl.BlockSpec((1, 8, dm), lambda i: (cls(i), 0, 0)),
                  pl.BlockSpec((tm, dm), lat_tile)],
        out_specs=(pl.BlockSpec((tm, dm), lat_tile), pl.BlockSpec((1, 8, dm), lambda i: (cls(i), 0, 0))),
        out_shape=(jax.ShapeDtypeStruct((t_rows, dm), F32), jax.ShapeDtypeStruct((2, 8, dm), F32)),
        sem=("arbitrary",), rider=rider)


def _conv_parts(t_rows, c_rows):
    return ((0, t_rows, t_rows // GRID_W, GRID_W), (t_rows, c_rows, 1, c_rows))


def _col_shifts(x2, rows_g, width_g):
    n, ct = x2.shape
    col = lax.broadcasted_iota(jnp.int32, (width_g, ct), 0)
    as_grid = lambda a: a.reshape(rows_g, width_g, ct)
    left = as_grid(pltpu.roll(x2, 1, 0)) * (col >= 1).astype(F32)
    right = as_grid(pltpu.roll(x2, n - 1, 0)) * (col <= width_g - 2).astype(F32)
    return [left, as_grid(x2), right]


def _row_shift(y3, a):
    if a == 1:
        return y3
    if y3.shape[0] == 1:
        return jnp.zeros_like(y3)
    zero = jnp.zeros_like(y3[:1])
    return jnp.concatenate([zero, y3[:-1]], axis=0) if a == 0 else jnp.concatenate([y3[1:], zero], axis=0)


def _conv_taps(cols, w_ref, flip):
    rows_g = cols[0].shape[0]
    acc = None
    for a in range(3):
        if rows_g == 1 and a != 1:
            continue
        inner = None
        for b in range(3):
            tap = (2 - a) * 3 + (2 - b) if flip else a * 3 + b
            term = cols[b] * w_ref[tap:tap + 1, :]
            inner = term if inner is None else inner + term
        inner = _row_shift(inner, a)
        acc = inner if acc is None else acc + inner
    return acc


def _conv_fwd(u, conv_w9, conv_b, t_rows, c_rows, w, ct):
    r = u.shape[0]
    base = 5 * w // ct

    def body(x_ref, w_ref, b_ref, o_ref):
        for r0, n, rows_g, width_g in _conv_parts(t_rows, c_rows):
            pre = _conv_taps(_col_shifts(x_ref[r0:r0 + n, :], rows_g, width_g), w_ref, False) + b_ref[...]
            o_ref[r0:r0 + n, :] = _silu(pre).reshape(n, ct)

    return pl.pallas_call(
        body, name="conv_fwd", grid=(2 * w // ct,),
        in_specs=[pl.BlockSpec((r, ct), lambda i: (0, base + i)), pl.BlockSpec((9, ct), lambda i: (0, i)),
                  pl.BlockSpec((1, ct), lambda i: (0, i))],
        out_specs=pl.BlockSpec((r, ct), lambda i: (0, i)),
        out_shape=jax.ShapeDtypeStruct((r, 2 * w), F32),
        compiler_params=_params(("parallel",)),
    )(u, conv_w9, conv_b)


def _conv_bwd(u, dqk_pair, conv_w9, conv_b, t_rows, c_rows, w, ct):
    r = u.shape[0]
    base = 5 * w // ct

    def body(x_ref, d1_ref, d2_ref, w_ref, b_ref, dx_ref, dw_ref, db_ref):
        dw = [jnp.zeros((1, ct), F32) for _ in range(9)]
        db = jnp.zeros((1, ct), F32)
        for r0, n, rows_g, width_g in _conv_parts(t_rows, c_rows):
            cols = _col_shifts(x_ref[r0:r0 + n, :], rows_g, width_g)
            pre = (_conv_taps(cols, w_ref, False) + b_ref[...]).reshape(n, ct)
            sg = _sigmoid(pre)
            dpre = (d1_ref[r0:r0 + n, :] + d2_ref[r0:r0 + n, :]) * (sg * (1.0 + pre * (1.0 - sg)))
            db = db + jnp.sum(dpre, axis=0, keepdims=True)
            dx_ref[r0:r0 + n, :] = _conv_taps(_col_shifts(dpre, rows_g, width_g), w_ref, True).reshape(n, ct)
            dpre3 = dpre.reshape(rows_g, width_g, ct)
            for a in range(3):
                if rows_g == 1 and a != 1:
                    continue
                moved = _row_shift(dpre3, 2 - a)
                for b in range(3):
                    prod = jnp.sum(cols[b] * moved, axis=0)
                    dw[a * 3 + b] = dw[a * 3 + b] + jnp.sum(prod, axis=0, keepdims=True)
        for tap in range(9):
            dw_ref[tap:tap + 1, :] = dw[tap]
        db_ref[...] = db

    return pl.pallas_call(
        body, name="conv_bwd", grid=(2 * w // ct,),
        in_specs=[pl.BlockSpec((r, ct), lambda i: (0, base + i)), pl.BlockSpec((r, ct), lambda i: (0, i)),
                  pl.BlockSpec((r, ct), lambda i: (0, i)),
                  pl.BlockSpec((9, ct), lambda i: (0, i)), pl.BlockSpec((1, ct), lambda i: (0, i))],
        out_specs=(pl.BlockSpec((r, ct), lambda i: (0, i)), pl.BlockSpec((9, ct), lambda i: (0, i)),
                   pl.BlockSpec((1, ct), lambda i: (0, i))),
        out_shape=(jax.ShapeDtypeStruct((r, 2 * w), F32), jax.ShapeDtypeStruct((9, 2 * w), F32),
                   jax.ShapeDtypeStruct((1, 2 * w), F32)),
        compiler_params=_params(("parallel",)),
    )(u, dqk_pair[0], dqk_pair[1], conv_w9, conv_b)


def _assemble_du(groups, gates, n_pad, tm):
    flat, layout = [], []
    for entry in list(groups) + [gates]:
        parts = entry if isinstance(entry, (tuple, list)) else (entry,)
        layout.append((len(flat), len(parts), parts[0].shape[1]))
        flat += list(parts)
    r = flat[0].shape[0]

    def body(*refs):
        o_ref = refs[-1]
        col = 0
        for first, count, width in layout:
            val = refs[first][...]
            for extra in range(1, count):
                val = val + refs[first + extra][...]
            o_ref[:, col:col + width] = val.astype(BF16)
            col += width
        assert col == n_pad

    return pl.pallas_call(
        body, name="assemble_du", grid=(r // tm,),
        in_specs=[pl.BlockSpec((tm, a.shape[1]), lambda i: (i, 0)) for a in flat],
        out_specs=pl.BlockSpec((tm, n_pad), lambda i: (i, 0)),
        out_shape=jax.ShapeDtypeStruct((r, n_pad), BF16),
        compiler_params=_params(("parallel",)),
    )(*flat)


def _scan_order(n_lat, n_ctx, rev):
    n = n_lat + n_ctx
    if rev:
        return lambda j: n - 1 - j
    return lambda j: (j + n_lat) % n


DIRS = (False, True)


def _hg_scan_fwd(u, lb_full, w, n_lat, n_ctx, chunk, rider=None):
    r = u.shape[0]
    n_heads = w // HG_DK
    n_chunks = n_lat + n_ctx
    nat = [_scan_order(n_lat, n_ctx, rev) for rev in DIRS]

    def body(*refs):
        ins, outs, scratch = refs[:8], refs[8:12], refs[12:]

        @pl.when(pl.program_id(0) == 0)
        def _():
            for s_ref in scratch:
                s_ref[...] = jnp.zeros_like(s_ref)

        results = []
        for d, rev in enumerate(DIRS):
            aq, af, ai, lb_ref = ins[4 * d:4 * d + 4]
            state = [scratch[d][h] for h in range(n_heads)]
            results.append((state, _hg_chunk(state, aq[...], af[...], ai[...],
                                             lb_ref[0, 0:1, :], lb_ref[0, 1:2, :], rev)))
        for d, (state, (new, o)) in enumerate(results):
            o_ref, save_ref = outs[2 * d:2 * d + 2]
            o_ref[...] = o
            for h in range(n_heads):
                save_ref[0, h] = state[h]
                scratch[d][h] = new[h]

    in_specs, out_specs, out_shape = [], [], []
    for d in range(2):
        in_specs += [pl.BlockSpec((chunk,w), lambda j, d=d: (nat[d](j), 0)),
                     pl.BlockSpec((chunk,w), lambda j, d=d: (nat[d](j), 1 + d)),
                     pl.BlockSpec((chunk,w), lambda j, d=d: (nat[d](j), 3)),
                     pl.BlockSpec((1, 2, w), lambda j, d=d: (d, 0, 0))]
        out_specs += [pl.BlockSpec((chunk,w), lambda j, d=d: (nat[d](j), 0)),
                      pl.BlockSpec((1, n_heads, HG_DK, HG_DK), lambda j: (j, 0, 0, 0))]
        out_shape += [jax.ShapeDtypeStruct((r, w), F32),
                      jax.ShapeDtypeStruct((n_chunks, n_heads, HG_DK, HG_DK), F32)]
    (o_f, s_f, o_b, s_b), rode = _call(
        body, (u, u, u, lb_full, u, u, u, lb_full), name="hg_scan_fwd", grid=(n_chunks,), in_specs=in_specs,
        out_specs=out_specs, out_shape=out_shape, scratch_shapes=[pltpu.VMEM((n_heads, HG_DK, HG_DK), F32)] * 2,
        sem=("arbitrary",), rider=rider)
    return (o_f, o_b), (s_f, s_b), rode


def _hg_scan_bwd(u, lb_full, saved, d_o, w, n_lat, n_ctx, chunk, rider=None):
    r = u.shape[0]
    n_heads = w // HG_DK
    n_chunks = n_lat + n_ctx
    step = lambda jj: n_chunks - 1 - jj
    nat = [(lambda jj, o=_scan_order(n_lat, n_ctx, rev): o(step(jj))) for rev in DIRS]

    def body(*refs):
        ins, outs, scratch = refs[:12], refs[12:20], refs[20:]
        jj = pl.program_id(0)

        @pl.when(jj == 0)
        def _():
            for d in range(2):
                scratch[d][...] = jnp.zeros_like(scratch[d])
                outs[4 * d + 3][...] = jnp.zeros_like(outs[4 * d + 3])

        results = []
        for d, rev in enumerate(DIRS):
            aq, af, ai, lb_ref, save_ref, do_ref = ins[6 * d:6 * d + 6]
            f = lambda st, a, b, c, l0, l1, rev=rev: _hg_chunk(st, a, b, c, l0, l1, rev)
            _, vjp = jax.vjp(f, [save_ref[0, h] for h in range(n_heads)], aq[...], af[...], ai[...],
                             lb_ref[0, 0:1, :], lb_ref[0, 1:2, :])
            d_out = do_ref[...] * (nat[d](jj) < n_lat).astype(F32)
            results.append(vjp(([scratch[d][h] for h in range(n_heads)], d_out)))
        for d, (dst, daq, daf, dai, dl0, dl1) in enumerate(results):
            daq_ref, daf_ref, dai_ref, dlb_ref = outs[4 * d:4 * d + 4]
            for h in range(n_heads):
                scratch[d][h] = dst[h]
            daq_ref[...] = daq
            daf_ref[...] = daf
            dai_ref[...] = dai
            dlb_ref[0:1, :] += dl0
            dlb_ref[1:2, :] += dl1

    in_specs, out_specs, out_shape, operands = [], [], [], []
    for d in range(2):
        row = lambda jj, d=d: (nat[d](jj), 0)
        in_specs += [pl.BlockSpec((chunk,w), row),
                     pl.BlockSpec((chunk,w), lambda jj, d=d: (nat[d](jj), 1 + d)),
                     pl.BlockSpec((chunk,w), lambda jj, d=d: (nat[d](jj), 3)),
                     pl.BlockSpec((1, 2, w), lambda jj, d=d: (d, 0, 0)),
                     pl.BlockSpec((1, n_heads, HG_DK, HG_DK), lambda jj: (step(jj), 0, 0, 0)),
                     pl.BlockSpec((chunk,w), lambda jj, d=d: (jnp.minimum(nat[d](jj), n_lat - 1), 0))]
        operands += [u, u, u, lb_full, saved[d], d_o]
        out_specs += [pl.BlockSpec((chunk,w), row)] * 3 + [pl.BlockSpec((2, w), lambda jj: (0, 0))]
        out_shape += [jax.ShapeDtypeStruct((r, w), F32)] * 3 + [jax.ShapeDtypeStruct((2, w), F32)]
    res, rode = _call(
        body, operands, name="hg_scan_bwd", grid=(n_chunks,), in_specs=in_specs, out_specs=out_specs,
        out_shape=out_shape, scratch_shapes=[pltpu.VMEM((n_heads, HG_DK, HG_DK), F32)] * 2,
        sem=("arbitrary",), rider=rider)
    return res[0:4], res[4:8], rode


def _ml_state_shapes(n_chunks, n_heads, dh):
    return (jax.ShapeDtypeStruct((n_chunks, n_heads, dh, dh), F32),
            jax.ShapeDtypeStruct((n_chunks, n_heads, 1, dh), F32),
            jax.ShapeDtypeStruct((n_chunks, n_heads, 1, LANE), F32))


def _ml_state_specs(n_heads, dh, index):
    return (pl.BlockSpec((1, n_heads, dh, dh), lambda j: (index(j), 0, 0, 0)),
            pl.BlockSpec((1, n_heads, 1, dh), lambda j: (index(j), 0, 0, 0)),
            pl.BlockSpec((1, n_heads, 1, LANE), lambda j: (index(j), 0, 0, 0)))


def _ml_state_scratch(n_heads, dh):
    return [pltpu.VMEM((n_heads, dh, dh), F32), pltpu.VMEM((n_heads, 1, dh), F32), pltpu.VMEM((n_heads, 1, LANE), F32)]


def _ml_scan_fwd(qk, u, gate_b, w, n_heads, n_lat, n_ctx, chunk):
    r = u.shape[0]
    dh = w // n_heads
    n_chunks = n_lat + n_ctx
    nat = [_scan_order(n_lat, n_ctx, rev) for rev in DIRS]

    def body(*refs):
        ins, outs, scratch = refs[:10], refs[10:18], refs[18:]

        @pl.when(pl.program_id(0) == 0)
        def _():
            for s_ref in scratch:
                s_ref[...] = jnp.zeros_like(s_ref)

        results = []
        for d, rev in enumerate(DIRS):
            q, k, v, g, gb = ins[5 * d:5 * d + 5]
            state = tuple([ref[h] for h in range(n_heads)] for ref in scratch[3 * d:3 * d + 3])
            results.append((state, _ml_chunk(state, q[...], k[...], v[...], g[...], gb[...], rev, d)))
        for d, (state, (new, o)) in enumerate(results):
            outs[4 * d][...] = o
            for part in range(3):
                for h in range(n_heads):
                    outs[4 * d + 1 + part][0, h] = state[part][h]
                    scratch[3 * d + part][h] = new[part][h]

    in_specs, out_specs, out_shape = [], [], []
    for d in range(2):
        in_specs += [pl.BlockSpec((chunk,w), lambda j, d=d: (nat[d](j), 0)),
                     pl.BlockSpec((chunk,w), lambda j, d=d: (nat[d](j), 1)),
                     pl.BlockSpec((chunk,w), lambda j, d=d: (nat[d](j), 7)),
                     pl.BlockSpec((chunk,LANE), lambda j, d=d: (nat[d](j), 10 * w // LANE)),
                     pl.BlockSpec((1, LANE), lambda j: (0, 0))]
        out_specs += [pl.BlockSpec((chunk,w), lambda j, d=d: (nat[d](j), 0))]
        out_specs += list(_ml_state_specs(n_heads, dh, lambda j: j))
        out_shape += [jax.ShapeDtypeStruct((r, w), F32)] + list(_ml_state_shapes(n_chunks, n_heads, dh))
    res = pl.pallas_call(
        body, name="ml_scan_fwd", grid=(n_chunks,), in_specs=in_specs, out_specs=tuple(out_specs),
        out_shape=tuple(out_shape), scratch_shapes=_ml_state_scratch(n_heads, dh) * 2,
        compiler_params=_params(("arbitrary",)),
    )(qk, qk, u, u, gate_b, qk, qk, u, u, gate_b)
    return (res[0], res[4]), (res[1:4], res[5:8])


def _ml_scan_bwd(qk, u, gate_b, saved, d_h, w, n_heads, n_lat, n_ctx, chunk, rider=None):
    r = u.shape[0]
    dh = w // n_heads
    n_chunks = n_lat + n_ctx
    step = lambda jj: n_chunks - 1 - jj
    nat = [(lambda jj, o=_scan_order(n_lat, n_ctx, rev): o(step(jj))) for rev in DIRS]

    def body(*refs):
        ins, outs, scratch = refs[:18], refs[18:26], refs[26:]
        jj = pl.program_id(0)

        @pl.when(jj == 0)
        def _():
            for s_ref in scratch:
                s_ref[...] = jnp.zeros_like(s_ref)
            for d in range(2):
                outs[4 * d + 3][...] = jnp.zeros_like(outs[4 * d + 3])

        results = []
        for d, rev in enumerate(DIRS):
            q, k, v, g, gb, sc, sn, sm, dh_ref = ins[9 * d:9 * d + 9]
            state = tuple([ref[0, h] for h in range(n_heads)] for ref in (sc, sn, sm))
            f = lambda st, a, b, c, gg, bb, rev=rev, d=d: _ml_chunk(st, a, b, c, gg, bb, rev, d)
            _, vjp = jax.vjp(f, state, q[...], k[...], v[...], g[...], gb[...])
            d_state = tuple([ref[h] for h in range(n_heads)] for ref in scratch[3 * d:3 * d + 3])
            d_out = dh_ref[...] * (nat[d](jj) < n_lat).astype(F32)
            results.append(vjp((d_state, d_out)))
        for d, (d_state, dq, dk, dv, dg, dgb) in enumerate(results):
            dqk_ref, dv_ref, dg_ref, dgb_ref = outs[4 * d:4 * d + 4]
            for part in range(3):
                for h in range(n_heads):
                    scratch[3 * d + part][h] = d_state[part][h]
            dqk_ref[:, 0:w] = dq
            dqk_ref[:, w:2 * w] = dk
            dv_ref[...] = dv
            dg_ref[...] = dg
            dgb_ref[...] += dgb

    in_specs, out_specs, out_shape, operands = [], [], [], []
    for d in range(2):
        row = lambda jj, d=d: (nat[d](jj), 0)
        in_specs += [pl.BlockSpec((chunk,w), row), pl.BlockSpec((chunk,w), lambda jj, d=d: (nat[d](jj), 1)),
                     pl.BlockSpec((chunk,w), lambda jj, d=d: (nat[d](jj), 7)),
                     pl.BlockSpec((chunk,LANE), lambda jj, d=d: (nat[d](jj), 10 * w // LANE)),
                     pl.BlockSpec((1, LANE), lambda jj: (0, 0))]
        in_specs += list(_ml_state_specs(n_heads, dh, step))
        in_specs += [pl.BlockSpec((chunk,w), lambda jj, d=d: (jnp.minimum(nat[d](jj), n_lat - 1), 0))]
        operands += [qk, qk, u, u, gate_b, *saved[d], d_h]
        out_specs += [pl.BlockSpec((chunk,2 * w), row), pl.BlockSpec((chunk,w), row),
                      pl.BlockSpec((chunk,LANE), row), pl.BlockSpec((1, LANE), lambda jj: (0, 0))]
        out_shape += [jax.ShapeDtypeStruct((r, 2 * w), F32), jax.ShapeDtypeStruct((r, w), F32),
                      jax.ShapeDtypeStruct((r, LANE), F32), jax.ShapeDtypeStruct((1, LANE), F32)]
    res, rode = _call(
        body, operands, name="ml_scan_bwd", grid=(n_chunks,), in_specs=in_specs, out_specs=out_specs,
        out_shape=out_shape, scratch_shapes=_ml_state_scratch(n_heads, dh) * 2, sem=("arbitrary",), rider=rider)
    return res[0:4], res[4:8], rode


def _post_specs(w, tm, lat_tiles, cols):
    return [pl.BlockSpec((tm, w), (lambda i, cb=cb: (jnp.minimum(i, lat_tiles - 1), cb))) for cb in cols]


def _post_fwd(o_f, o_b, h_f, h_b, u, wa, wb, t_rows, w, n_hg, n_ml, tm):
    lat_tiles = t_rows // tm

    def body(of, ob, hf, hb, az, bo, bz, wa_ref, wb_ref, y_ref):
        y_ref[...] = _post_fn(of[...], ob[...], az[...], hf[...], hb[...], bo[...], bz[...],
                              wa_ref[...], wb_ref[...], n_hg, n_ml).astype(BF16)

    rows = pl.BlockSpec((tm, w), lambda i: (i, 0))
    vec = pl.BlockSpec((1, w), lambda i: (0, 0))
    return pl.pallas_call(
        body, name="post_fwd", grid=(lat_tiles,),
        in_specs=[rows] * 4 + _post_specs(w, tm, lat_tiles, (4, 8, 9)) + [vec, vec],
        out_specs=pl.BlockSpec((tm, 2 * w), lambda i: (i, 0)),
        out_shape=jax.ShapeDtypeStruct((t_rows, 2 * w), BF16),
        compiler_params=_params(("parallel",)),
    )(o_f, o_b, h_f, h_b, u, u, u, wa, wb)


def _post_bwd(o_f, o_b, h_f, h_b, u, wa, wb, dy, t_rows, w, n_hg, n_ml, tm, rider=None):
    r = u.shape[0]
    lat_tiles = t_rows // tm
    lat = lambda i: (jnp.minimum(i, lat_tiles - 1), 0)

    def body(of, ob, hf, hb, az, bo, bz, wa_ref, wb_ref, dy_ref, do_ref, dh_ref, daz_ref, dbo_ref, dbz_ref,
             dwa_ref, dwb_ref):
        i = pl.program_id(0)

        @pl.when(i == 0)
        def _():
            dwa_ref[...] = jnp.zeros_like(dwa_ref)
            dwb_ref[...] = jnp.zeros_like(dwb_ref)

        @pl.when(i < lat_tiles)
        def _():
            f = functools.partial(_post_fn, n_hg=n_hg, n_ml=n_ml)
            _, vjp = jax.vjp(f, of[...], ob[...], az[...], hf[...], hb[...], bo[...], bz[...], wa_ref[...], wb_ref[...])
            d_of, _, d_az, d_hf, _, d_bo, d_bz, d_wa, d_wb = vjp(dy_ref[...])
            do_ref[...] = d_of
            dh_ref[...] = d_hf
            daz_ref[...] = d_az
            dbo_ref[...] = d_bo
            dbz_ref[...] = d_bz
            dwa_ref[...] += d_wa
            dwb_ref[...] += d_wb

        @pl.when(i >= lat_tiles)
        def _():
            daz_ref[...] = jnp.zeros_like(daz_ref)
            dbo_ref[...] = jnp.zeros_like(dbo_ref)
            dbz_ref[...] = jnp.zeros_like(dbz_ref)

    lat_rows = pl.BlockSpec((tm, w), lat)
    all_rows = pl.BlockSpec((tm, w), lambda i: (i, 0))
    vec = pl.BlockSpec((1, w), lambda i: (0, 0))
    sd_t = jax.ShapeDtypeStruct((t_rows, w), F32)
    sd_r = jax.ShapeDtypeStruct((r, w), F32)
    sd_v = jax.ShapeDtypeStruct((1, w), F32)
    return _call(
        body, (o_f, o_b, h_f, h_b, u, u, u, wa, wb, dy), name="post_bwd", grid=(r // tm,),
        in_specs=[lat_rows] * 4 + _post_specs(w, tm, lat_tiles, (4, 8, 9)) + [vec, vec]
        + [pl.BlockSpec((tm, 2 * w), lat)],
        out_specs=(lat_rows, lat_rows, all_rows, all_rows, all_rows, vec, vec),
        out_shape=(sd_t, sd_t, sd_r, sd_r, sd_r, sd_v, sd_v), sem=("arbitrary",), rider=rider)


OUT_ROW_GATE, OUT_ROW_LN_G, OUT_ROW_LN_B, OUT_ROW_LOSS = 0, 1, 2, 3


def _out_block(y, w_out, x, target, prm, tm):
    t_rows, dm = x.shape
    di = y.shape[1]

    def body(y_ref, w_ref, x_ref, t_ref, p_ref, dz_ref, dy_ref, gx_ref, acc_ref):
        @pl.when(pl.program_id(0) == 0)
        def _():
            acc_ref[...] = jnp.zeros_like(acc_ref)

        gate, ln_g, ln_b = p_ref[0:1, :], p_ref[1:2, :], p_ref[2:3, :]
        z = _nn(y_ref[...], w_ref[...])
        res = ALPHA * x_ref[...] + gate * z
        mu = jnp.mean(res, axis=-1, keepdims=True)
        rc = res - mu
        rstd = lax.rsqrt(jnp.mean(rc * rc, axis=-1, keepdims=True) + LN_EPS)
        rn = rc * rstd
        err = rn * ln_g + ln_b - t_ref[...]
        d_out = err * (1.0 / dm)
        d_rn = d_out * ln_g
        d_res = rstd * (d_rn - jnp.mean(d_rn, axis=-1, keepdims=True)
                        - rn * jnp.mean(d_rn * rn, axis=-1, keepdims=True))
        acc_ref[OUT_ROW_GATE:OUT_ROW_GATE + 1, :] += jnp.sum(d_res * z, axis=0, keepdims=True)
        acc_ref[OUT_ROW_LN_G:OUT_ROW_LN_G + 1, :] += jnp.sum(d_out * rn, axis=0, keepdims=True)
        acc_ref[OUT_ROW_LN_B:OUT_ROW_LN_B + 1, :] += jnp.sum(d_out, axis=0, keepdims=True)
        acc_ref[OUT_ROW_LOSS:OUT_ROW_LOSS + 1, :] += (0.5 / dm) * jnp.sum(err * err, axis=0, keepdims=True)
        gx_ref[...] = ALPHA * d_res
        dz = (d_res * gate).astype(BF16)
        dz_ref[...] = dz
        dy_ref[...] = _nt(dz, w_ref[...])

    rows_d = pl.BlockSpec((tm, dm), lambda i: (i, 0))
    rows_i = pl.BlockSpec((tm, di), lambda i: (i, 0))
    return pl.pallas_call(
        body, name="out_block", grid=(t_rows // tm,),
        in_specs=[rows_i, pl.BlockSpec((di, dm), lambda i: (0, 0)), rows_d, rows_d,
                  pl.BlockSpec((8, dm), lambda i: (0, 0))],
        out_specs=(rows_d, rows_i, rows_d, pl.BlockSpec((8, dm), lambda i: (0, 0))),
        out_shape=(jax.ShapeDtypeStruct((t_rows, dm), BF16), jax.ShapeDtypeStruct((t_rows, di), F32),
                   jax.ShapeDtypeStruct((t_rows, dm), F32), jax.ShapeDtypeStruct((8, dm), F32)),
        compiler_params=_params(("arbitrary",)),
    )(y, w_out, x, target, prm)


def _mod_fwd(c16, w_mod, tn):
    dm, n = w_mod.shape

    def body(c_ref, w_ref, o_ref, a_ref):
        a = _silu(c_ref[...])
        a_ref[...] = a
        o_ref[...] = _nn(a, w_ref[...], HIGHEST)

    return pl.pallas_call(
        body, name="mod_fwd", grid=(n // tn,),
        in_specs=[pl.BlockSpec((16, dm), lambda j: (0, 0)), pl.BlockSpec((dm, tn), lambda j: (0, j))],
        out_specs=(pl.BlockSpec((16, tn), lambda j: (0, j)), pl.BlockSpec((16, dm), lambda j: (0, 0))),
        out_shape=(jax.ShapeDtypeStruct((16, n), F32), jax.ShapeDtypeStruct((16, dm), F32)),
        compiler_params=_params(("arbitrary",)),
    )(c16, w_mod)


def _mod_bwd(a16, dm16, w_mod, tn):
    dm, n = w_mod.shape

    def body(a_ref, d_ref, w_ref, dw_ref, dc_ref):
        @pl.when(pl.program_id(0) == 0)
        def _():
            dc_ref[...] = jnp.zeros_like(dc_ref)
        dw_ref[...] = _tn(a_ref[...], d_ref[...], HIGHEST)
        dc_ref[...] += _nt(d_ref[...], w_ref[...], HIGHEST)

    return pl.pallas_call(
        body, name="mod_bwd", grid=(n // tn,),
        in_specs=[pl.BlockSpec((16, dm), lambda j: (0, 0)), pl.BlockSpec((16, tn), lambda j: (0, j)),
                  pl.BlockSpec((dm, tn), lambda j: (0, j))],
        out_specs=(pl.BlockSpec((dm, tn), lambda j: (0, j)), pl.BlockSpec((16, dm), lambda j: (0, 0))),
        out_shape=(jax.ShapeDtypeStruct((dm, n), F32), jax.ShapeDtypeStruct((16, dm), F32)),
        compiler_params=_params(("arbitrary",)),
    )(a16, dm16, w_mod)


def _sum_devices(g, fold_rows):
    n_dev, rows, n = g.shape

    def body(g_ref, s_ref, t_ref):
        s = g_ref[0]
        for dev in range(1, n_dev):
            s = s + g_ref[dev]
        t_ref[...] = jnp.broadcast_to(jnp.sum(s, axis=-1, keepdims=True), (rows, LANE))
        s_ref[...] = s
        s_ref[0:fold_rows, :] = s[0:fold_rows] + s[fold_rows:2 * fold_rows]

    return pl.pallas_call(
        body, name="sum_devices",
        out_shape=(jax.ShapeDtypeStruct((rows, n), F32), jax.ShapeDtypeStruct((rows, LANE), F32)),
        compiler_params=_params(),
    )(g)


def _c_ctx_grad(parts, c_ctx_row):
    def body(p_ref, c_ref, o_ref):
        s = p_ref[0]
        for chip in range(1, N_CHIPS):
            s = s + p_ref[2 * chip]
        cv = c_ref[...]
        sg = _sigmoid(cv)
        o_ref[...] = s * (sg * (1.0 + cv * (1.0 - sg)))

    return pl.pallas_call(
        body, name="c_ctx_grad", out_shape=jax.ShapeDtypeStruct(parts.shape[1:], F32), compiler_params=_params(),
    )(parts, c_ctx_row)


def _sum_pair(name, mine, got):
    def body(a_ref, b_ref, o_ref):
        o_ref[...] = (a_ref[...] + b_ref[...]).astype(BF16)

    k, rows, n = mine.shape
    tl = _largest_divisor(n, max(LANE, (1 << 18) // rows), LANE)
    spec = pl.BlockSpec((1, rows, tl), lambda kk, i: (kk, 0, i))
    return pl.pallas_call(
        body, name=name, grid=(k, n // tl), in_specs=[spec, spec], out_specs=spec,
        out_shape=jax.ShapeDtypeStruct(mine.shape, BF16), compiler_params=_params(("parallel", "parallel")),
    )(mine, got)


def _sum_pair_lanes(name, full, got, ci):
    rows, n = got.shape
    tr = _largest_divisor(rows, max(SUBLANE_BF16, (1 << 19) // n), SUBLANE_BF16)

    def body(ci_ref, a_ref, b_ref, o_ref):
        o_ref[...] = (a_ref[...] + b_ref[...].astype(F32)).astype(BF16)

    return pl.pallas_call(
        body, name=name,
        grid_spec=pltpu.PrefetchScalarGridSpec(
            num_scalar_prefetch=1, grid=(rows // tr,),
            in_specs=[pl.BlockSpec((tr, n), lambda i, c: (i, c[0])), pl.BlockSpec((tr, n), lambda i, c: (i, 0))],
            out_specs=pl.BlockSpec((tr, n), lambda i, c: (i, 0))),
        out_shape=jax.ShapeDtypeStruct((rows, n), BF16), compiler_params=_params(("parallel",)),
    )(ci.reshape(1).astype(jnp.int32), full, got)


def _sum_chips(name, got):
    k, rows, n = got.shape
    tl = _largest_divisor(n, max(LANE, (1 << 18) // rows), LANE)

    def body(g_ref, o_ref):
        total = g_ref[0].astype(F32)
        for kk in range(1, k):
            total = total + g_ref[kk].astype(F32)
        o_ref[...] = total

    return pl.pallas_call(
        body, name=name, grid=(n // tl,),
        in_specs=[pl.BlockSpec((k, rows, tl), lambda i: (0, 0, i))], out_specs=pl.BlockSpec((rows, tl), lambda i: (0, i)),
        out_shape=jax.ShapeDtypeStruct((rows, n), F32), compiler_params=_params(("parallel",)),
    )(got)


def _adamw_update(w, g, m, v):
    m2 = ADAM_B1 * m + (1.0 - ADAM_B1) * g
    v2 = ADAM_B2 * v + (1.0 - ADAM_B2) * jnp.square(g)
    m_hat = m2 / (1.0 - ADAM_B1 ** ADAM_STEP)
    v_hat = v2 / (1.0 - ADAM_B2 ** ADAM_STEP)
    return -ADAM_LR * (m_hat / (jnp.sqrt(v_hat) + ADAM_EPS) + ADAM_WD * w), m2, v2


def _adamw(name, w, g, m, v, rider=None):
    rows, n = w.shape
    if rows % 8 == 0:
        tr = _largest_divisor(rows, max(8, (1 << 18) // n), 8)
        block, index, steps = (tr, n), (lambda i: (i, 0)), rows // tr
    else:
        tl = _largest_divisor(n, max(LANE, (1 << 18) // rows), LANE)
        block, index, steps = (rows, tl), (lambda i: (0, i)), n // tl

    def body(w_ref, g_ref, m_ref, v_ref, d_ref, mo_ref, vo_ref):
        d_ref[...], mo_ref[...], vo_ref[...] = _adamw_update(w_ref[...], g_ref[...], m_ref[...], v_ref[...])

    spec = pl.BlockSpec(block, index)
    sds = jax.ShapeDtypeStruct((rows, n), F32)
    return _call(body, (w, g, m, v), name=name, grid=(steps,), in_specs=[spec] * 4, out_specs=(spec,) * 3,
                 out_shape=(sds, sds, sds), sem=("parallel",), rider=rider)


PACK_LANES = 1024


def _pack(pieces):
    flat = jnp.concatenate([p.reshape(-1) for p in pieces])
    total = -(-flat.shape[0] // (8 * PACK_LANES)) * 8 * PACK_LANES
    return jnp.pad(flat, (0, total - flat.shape[0])).reshape(-1, PACK_LANES)


def _unpack(packed, shapes):
    flat = packed.reshape(-1)
    out, off = [], 0
    for shp in shapes:
        size = math.prod(shp)
        out.append(flat[off:off + size].reshape(shp))
        off += size
    return out


def _rows8(rows, width):
    flat = [r.reshape(width) for r in rows] + [jnp.zeros(((8 - len(rows)) * width,), F32)]
    return jnp.concatenate(flat).reshape(8, width)


def kernel(x, c, ctx, c_ctx, w_mod, b_mod, w_in, conv_w, conv_b, hg_lb, ml_gate_b, hg_norm_w, ml_norm_w, w_out, ln_g, ln_b, loss_target, m_c_ctx, m_w_mod, m_b_mod, m_w_in, m_conv_w, m_conv_b, m_hg_lb, m_ml_gate_b, m_hg_norm_w, m_ml_norm_w, m_w_out, m_ln_g, m_ln_b, v_c_ctx, v_w_mod, v_b_mod, v_w_in, v_conv_w, v_conv_b, v_hg_lb, v_ml_gate_b, v_hg_norm_w, v_ml_norm_w, v_w_out, v_ln_g, v_ln_b):
    t_rows, dm = x.shape[1], x.shape[2]
    c_rows = ctx.shape[1]
    w = hg_norm_w.shape[1]
    n_ml = ml_gate_b.shape[-1]
    n_hg = w // HG_DK
    di = 2 * w
    n_in = 10 * w + 4 * n_ml
    ns = w_in.shape[2]
    nm = w_mod.shape[2]
    n_pad = 10 * w + LANE
    r_rows = t_rows + c_rows
    row_gcd = math.gcd(t_rows, c_rows)
    hg_chunk, ml_chunk = math.gcd(HG_CHUNK, row_gcd), math.gcd(ML_CHUNK, row_gcd)
    hg_counts = (t_rows // hg_chunk, c_rows // hg_chunk, hg_chunk)
    ml_counts = (t_rows // ml_chunk, c_rows // ml_chunk, ml_chunk)
    assert ml_norm_w.shape[1] == w and di == dm and N_CHIPS * ns == n_in and N_CHIPS * nm == 3 * dm
    assert w_out.shape[1] * N_CHIPS == di and 4 * n_ml <= LANE and t_rows % GRID_W == 0

    xi, yi, ci = lax.axis_index("x"), lax.axis_index("y"), lax.axis_index("c")
    chip = 2 * xi + yi
    dev = 4 * xi + 2 * yi + ci

    tm = _largest_divisor(math.gcd(t_rows, c_rows), 256, 8)
    tm_mm = _largest_divisor(r_rows, 1088, SUBLANE_BF16)
    tn_mm = LANE * _largest_divisor(n_pad // LANE, 9)
    tn_mod = _largest_divisor(nm, 512, LANE)

    shard_shapes = [(dm,), (2, 2, w // N_CHIPS), (3, 3, di // N_CHIPS)]
    g1 = _all_gather8(_pack([c, hg_lb, conv_w])).run("gather_inputs")[0]
    per_dev = [_unpack(g1[i], shard_shapes) for i in range(N_DEV)]
    c_all = jnp.stack([p[0] for p in per_dev])
    lb_full = jnp.concatenate([per_dev[2 * k][1] for k in range(N_CHIPS)], axis=-1)
    conv_w9 = jnp.concatenate([per_dev[2 * k][2] for k in range(N_CHIPS)], axis=-1).reshape(9, di)

    c16 = jnp.concatenate([c_all, c_ctx[None], jnp.zeros((16 - N_DEV - 1, dm), F32)])
    mod_part, a16 = _mod_fwd(c16, w_mod[0], tn_mod)
    g2 = _all_gather8(mod_part).run("gather_mod")[0]
    mod_all = jnp.concatenate([g2[2 * k] for k in range(N_CHIPS)], axis=1) + b_mod
    mod_x = lax.dynamic_index_in_dim(mod_all, dev, 0, keepdims=False).reshape(3, dm)
    mod_c = mod_all[N_DEV].reshape(3, dm)
    prm = jnp.stack([_rows8(list(mod_x), dm), _rows8(list(mod_c), dm)])

    as_t = lambda a: jnp.transpose(a[0])
    half_in = lax.dynamic_slice_in_dim(as_t(w_in).astype(BF16), ci * (dm // 2), dm // 2, 1)
    half_out = lax.dynamic_slice_in_dim(w_out[0].astype(BF16), ci * (di // (2 * N_CHIPS)), di // (2 * N_CHIPS), 0)
    fetched_in = _own_block(chip, half_in, _all_gather_chips([half_in]).run("gather_w_in")[0])
    gw_in = _join_halves(ci, fetched_in, _sibling_swap([fetched_in]).run("gather_w_in_pair")[0], 2)
    wt_full = jnp.concatenate([gw_in.reshape(n_in, dm), jnp.zeros((n_pad - n_in, dm), BF16)])

    hc = _modulate_fwd(x[0], ctx[0], prm, tm)
    u, (got_out,) = _mm_nt("in_proj", hc, wt_full, tm_mm, tn_mm, F32, rider=_all_gather_chips([half_out]))
    fetched_out = _own_block(chip, half_out, got_out)
    (o_f, o_b), hg_saved, (swapped_out,) = _hg_scan_fwd(u, lb_full, w, *hg_counts,
                                                         rider=_sibling_swap([fetched_out]))
    w_out_full = _join_halves(ci, fetched_out, swapped_out, 1).reshape(di, dm)
    qk = _conv_fwd(u, conv_w9, conv_b, t_rows, c_rows, w, LANE)
    gate_b_row = jnp.pad(ml_gate_b.reshape(1, -1), ((0, 0), (0, LANE - 4 * n_ml)))
    (h_f, h_b), ml_saved = _ml_scan_fwd(qk, u, gate_b_row, w, n_ml, *ml_counts)
    y = _post_fwd(o_f, o_b, h_f, h_b, u, hg_norm_w, ml_norm_w, t_rows, w, n_hg, n_ml, tm)
    prm_out = _rows8([mod_x[2], ln_g, ln_b], dm)
    dz, dy, gx_direct, acc_out = _out_block(y, w_out_full, x[0], loss_target[0], prm_out, tm)

    d_w_out = _mm_tn("d_w_out", y, dz, _largest_divisor(di, 1024, LANE),
                     _largest_divisor(t_rows, 1024, SUBLANE_BF16))
    d_w_out4 = d_w_out.reshape(N_CHIPS, 2, di // (2 * N_CHIPS), dm)
    mine_out = lax.dynamic_index_in_dim(d_w_out4, ci, 1, keepdims=False)
    other_out = lax.dynamic_index_in_dim(d_w_out4, 1 - ci, 1, keepdims=False)
    (d_o, d_h, d_az, d_bo, d_bz, d_wa, d_wb), (got_out,) = _post_bwd(
        o_f, o_b, h_f, h_b, u, hg_norm_w, ml_norm_w, dy, t_rows, w, n_hg, n_ml, tm, rider=_sibling_swap([other_out]))
    pair_out = _sum_pair("rs_pair_sum_w_out", mine_out, got_out)
    (d_aq_f, d_aff, d_ai_f, d_lb_f), (d_aq_b, d_afb, d_ai_b, d_lb_b), (landed_out,) = _hg_scan_bwd(
        u, lb_full, hg_saved, d_o, w, *hg_counts, rider=_chip_scatter([pair_out]))
    half_g_out = _sum_chips("rs_chip_sum_w_out", _own_block(chip, pair_out, landed_out))
    (d_qk_f, d_v_f, d_g_f, d_gb_f), (d_qk_b, d_v_b, d_g_b, d_gb_b), (sibling_out,) = _ml_scan_bwd(
        qk, u, gate_b_row, ml_saved, d_h, w, n_ml, *ml_counts, rider=_sibling_swap([half_g_out]))
    g_w_out = _join_halves(ci, half_g_out, sibling_out, 0)
    d_bqk, d_cw, d_cb = _conv_bwd(u, (d_qk_f, d_qk_b), conv_w9, conv_b, t_rows, c_rows, w, LANE)
    du = _assemble_du([(d_aq_f, d_aq_b), d_aff, d_afb, (d_ai_f, d_ai_b), d_az, d_bqk, (d_v_f, d_v_b), d_bo, d_bz],
                      (d_g_f, d_g_b), n_pad, tm // 2)
    d_wt_in, d_wt_in_bf16 = _mm_tn("d_w_in", du, hc, tn_mm, tm_mm, with_bf16=True)

    lanes_of = lambda core: (slice(None), pl.ds(core * (dm // 2), dm // 2))
    got_in = _Exchange([d_wt_in_bf16], [jax.ShapeDtypeStruct((n_pad, dm // 2), BF16)],
                       [(SIBLING_MASK, 0, lambda s, r: lanes_of(r[2]), 0, None)]).run("rs_pair_w_in")[0]
    pair_half = _sum_pair_lanes("rs_pair_sum_w_in", d_wt_in, got_in, ci)
    pair_in = jnp.stack([pair_half[k * ns:(k + 1) * ns] for k in range(N_CHIPS)])
    d_hc, (landed_in,) = _mm_acc("d_h", du, wt_full, tm_mm, tn_mm, rider=_chip_scatter([pair_in]))
    half_g_in = _sum_chips("rs_chip_sum_w_in", _own_block(chip, pair_in, landed_in))
    (gx, acc_mod), (sibling_in,) = _modulate_bwd(x[0], ctx[0], d_hc, prm, gx_direct, tm,
                                                 rider=_sibling_swap([half_g_in]))
    g_wt_in = _join_halves(ci, half_g_in, sibling_in, 1)
    grad_x = gx[None]

    zero_row = jnp.zeros((dm,), F32)
    d_gb = jnp.concatenate([d_gb_f[:, 0:n_ml], d_gb_b[:, n_ml:2 * n_ml], d_gb_f[:, 2 * n_ml:3 * n_ml],
                            d_gb_b[:, 3 * n_ml:4 * n_ml], jnp.zeros((1, dm - 4 * n_ml), F32)], axis=1)
    rows = [acc_mod[0, 0], acc_mod[0, 1], acc_out[OUT_ROW_GATE],
            acc_mod[1, 0], acc_mod[1, 1], zero_row]
    rows += list(d_cw) + [d_cb[0], d_lb_f.reshape(dm), d_lb_b.reshape(dm),
                          jnp.concatenate([d_wa[0], d_wb[0]]), acc_out[OUT_ROW_LN_G], acc_out[OUT_ROW_LN_B],
                          acc_out[OUT_ROW_LOSS], d_gb[0], zero_row]
    ROW_CW, ROW_CB, ROW_LB, ROW_NORM, ROW_LN_G, ROW_LN_B, ROW_LOSS, ROW_GB = 6, 15, 16, 18, 19, 20, 21, 22
    delta, new_m, new_v = {}, {}, {}
    small_rows = jnp.concatenate([r.reshape(dm) for r in rows]).reshape(len(rows), dm)
    res, (g3,) = _adamw("adamw_w_in", as_t(w_in), g_wt_in, as_t(m_w_in), as_t(v_w_in), rider=_all_gather8(small_rows))
    delta["w_in"], new_m["w_in"], new_v["w_in"] = (jnp.transpose(a)[None] for a in res)
    sums, totals = _sum_devices(g3, 3)
    loss = totals[ROW_LOSS, 0]
    dm16 = jnp.concatenate([g3[:, 0:3, :].reshape(N_DEV, 3 * dm), sums[3:6].reshape(1, 3 * dm),
                            jnp.zeros((16 - N_DEV - 1, 3 * dm), F32)])
    g_w_mod, dc16 = _mod_bwd(a16, lax.dynamic_slice_in_dim(dm16, chip * nm, nm, 1), w_mod[0], tn_mod)
    res, (g4,) = _adamw("adamw_w_mod", w_mod[0], g_w_mod, m_w_mod[0], v_w_mod[0],
                        rider=_all_gather8(jnp.pad(dc16[N_DEV:N_DEV + 1], ((0, 7), (0, 0)))))
    delta["w_mod"], new_m["w_mod"], new_v["w_mod"] = (a[None] for a in res)
    g_c_ctx = _c_ctx_grad(g4, jnp.broadcast_to(c_ctx[None], (8, dm)))[0]

    chip_cols = lambda a, width: lax.dynamic_slice_in_dim(a, chip * width, width, a.ndim - 1)
    grads = {
        "c_ctx": g_c_ctx,
        "w_mod": g_w_mod[None],
        "b_mod": sums[0:3].reshape(1, 3 * dm),
        "w_in": jnp.transpose(g_wt_in)[None],
        "conv_w": chip_cols(sums[ROW_CW:ROW_CW + 9].reshape(1, 3, 3, di), di // N_CHIPS),
        "conv_b": sums[ROW_CB][None],
        "hg_lb": chip_cols(sums[ROW_LB:ROW_LB + 2].reshape(2, 2, w), w // N_CHIPS),
        "ml_gate_b": sums[ROW_GB, 0:4 * n_ml].reshape(1, 4, n_ml),
        "hg_norm_w": sums[ROW_NORM, 0:w][None],
        "ml_norm_w": sums[ROW_NORM, w:2 * w][None],
        "w_out": g_w_out[None],
        "ln_g": sums[ROW_LN_G][None],
        "ln_b": sums[ROW_LN_B][None],
    }
    weights = dict(c_ctx=c_ctx, w_mod=w_mod, b_mod=b_mod, w_in=w_in, conv_w=conv_w, conv_b=conv_b, hg_lb=hg_lb,
                   ml_gate_b=ml_gate_b, hg_norm_w=hg_norm_w, ml_norm_w=ml_norm_w, w_out=w_out, ln_g=ln_g, ln_b=ln_b)
    mom1 = dict(c_ctx=m_c_ctx, w_mod=m_w_mod, b_mod=m_b_mod, w_in=m_w_in, conv_w=m_conv_w, conv_b=m_conv_b,
                hg_lb=m_hg_lb, ml_gate_b=m_ml_gate_b, hg_norm_w=m_hg_norm_w, ml_norm_w=m_ml_norm_w, w_out=m_w_out,
                ln_g=m_ln_g, ln_b=m_ln_b)
    mom2 = dict(c_ctx=v_c_ctx, w_mod=v_w_mod, b_mod=v_b_mod, w_in=v_w_in, conv_w=v_conv_w, conv_b=v_conv_b,
                hg_lb=v_hg_lb, ml_gate_b=v_ml_gate_b, hg_norm_w=v_hg_norm_w, ml_norm_w=v_ml_norm_w, w_out=v_w_out,
                ln_g=v_ln_g, ln_b=v_ln_b)
    names = list(weights)
    big = ("w_mod", "w_in", "w_out")
    small = [n for n in names if n not in big]

    res, _ = _adamw("adamw_w_out", w_out[0], g_w_out, m_w_out[0], v_w_out[0])
    delta["w_out"], new_m["w_out"], new_v["w_out"] = (a[None] for a in res)
    small_shapes = [weights[n].shape for n in small]
    res, _ = _adamw("adamw_small", *(_pack([src[n] for n in small]) for src in (weights, grads, mom1, mom2)))
    for out, packed in zip((delta, new_m, new_v), res):
        for n, a in zip(small, _unpack(packed, small_shapes)):
            out[n] = a

    return (loss, grad_x, *[grads[n].reshape(weights[n].shape) for n in names], *[delta[n] for n in names],
            *[new_m[n] for n in names], *[new_v[n] for n in names])
```

```python
import functools
import math

import jax
import jax.numpy as jnp
from jax import lax
from jax.experimental import pallas as pl
from jax.experimental.pallas import tpu as pltpu

F32 = jnp.float32
BF16 = jnp.bfloat16
HIGHEST = lax.Precision.HIGHEST
MESH = pl.DeviceIdType.MESH

HG_CHUNK = 64
ML_CHUNK = 256
GRID_W = 64
HG_DK = 128
LANE = 128
SUBLANE_BF16 = 16
ALPHA = 2.0 ** 0.25
LN_EPS = 1e-5
NORM_EPS = 1e-6
ADAM_LR = 0.001
ADAM_B1 = 0.9
ADAM_B2 = 0.999
ADAM_EPS = 1e-08
ADAM_WD = 0.01
ADAM_STEP = 10
VMEM_LIMIT = 56 * 1024 * 1024
N_CHIPS = 4
N_DEV = 8


def _params(sem=None):
    return pltpu.CompilerParams(dimension_semantics=sem, vmem_limit_bytes=VMEM_LIMIT)


def _largest_divisor(n, cap, multiple=1):
    best = None
    for d in range(multiple, min(n, cap) + 1, multiple):
        if n % d == 0:
            best = d
    assert best is not None, (n, cap, multiple)
    return best


def _sigmoid(x):
    return jax.nn.sigmoid(x)


def _silu(x):
    return x * jax.nn.sigmoid(x)


def _dot(a, b, dims, precision=None):
    return lax.dot_general(a, b, (dims, ((), ())), precision=precision, preferred_element_type=F32)


def _nn(a, b, precision=None):
    return _dot(a, b, ((1,), (0,)), precision)


def _nt(a, b, precision=None):
    return _dot(a, b, ((1,), (1,)), precision)


def _tn(a, b, precision=None):
    return _dot(a, b, ((0,), (0,)), precision)


def _visible(n, rev):
    r = lax.broadcasted_iota(jnp.int32, (n, n), 0)
    c = lax.broadcasted_iota(jnp.int32, (n, n), 1)
    return (r <= c) if rev else (r >= c)


def _mask_matmul(mask, x):
    mb = mask.astype(BF16)
    hi = x.astype(BF16)
    lo = (x - hi.astype(F32)).astype(BF16)
    return _nn(mb, hi) + _nn(mb, lo)


@functools.partial(jax.custom_vjp, nondiff_argnums=(1,))
def _cumulative(x, rev):
    return _mask_matmul(_visible(x.shape[0], rev), x)


def _cumulative_fwd(x, rev):
    return _cumulative(x, rev), None


def _cumulative_bwd(rev, _, ct):
    return (_mask_matmul(_visible(ct.shape[0], not rev), ct),)


_cumulative.defvjp(_cumulative_fwd, _cumulative_bwd)


def _hg_chunk(states, aq, af, ai, lb0, lb1, rev):
    n_heads = len(states)
    lb = _sigmoid(lb0 - lb1)
    f = lb + (1.0 - lb) * _sigmoid(af)
    g = jnp.log(f)
    k = 1.0 - f
    q = _silu(aq)
    chunk = aq.shape[0]
    vis = _visible(chunk, rev)
    b = _cumulative(g, rev)
    last = 0 if rev else chunk - 1
    b_end = b[last:last + 1]
    b_mid = b[chunk // 2:chunk // 2 + 1]
    q_inter = q * jnp.exp(b)
    q_intra = q * jnp.exp(b - b_mid)
    k_intra = k * jnp.exp(b_mid - b)
    k_dec = k * jnp.exp(b_end - b)
    e_end = jnp.exp(b_end)
    new_states, outs = [], []
    for h in range(n_heads):
        sl = slice(h * HG_DK, (h + 1) * HG_DK)
        s_t = states[h]
        scores = jnp.where(vis, _nt(q_intra[:, sl], k_intra[:, sl]), 0.0)
        outs.append(_nt(q_inter[:, sl], s_t) + _nn(scores, ai[:, sl]))
        new_states.append(e_end[:, sl] * s_t + _tn(ai[:, sl], k_dec[:, sl]))
    return new_states, jnp.concatenate(outs, axis=1)


def _ml_chunk(state, q, k, v, g, gb, rev, d):
    cms, nvs, mbs = state
    n_heads = len(cms)
    dh = q.shape[1] // n_heads
    ga = g + gb
    log_f_all = jax.nn.log_sigmoid(ga)
    chunk = q.shape[0]
    vis = _visible(chunk, rev)
    b_all = _cumulative(log_f_all, rev)
    last = 0 if rev else chunk - 1
    k = k * (dh ** -0.5)
    new_c, new_n, new_m, outs = [], [], [], []
    for h in range(n_heads):
        ci = d * n_heads + h
        cf = (2 + d) * n_heads + h
        sl = slice(h * dh, (h + 1) * dh)
        qh, kh, vh = q[:, sl], k[:, sl], v[:, sl]
        li = ga[:, ci:ci + 1]
        b = b_all[:, cf:cf + 1]
        m = mbs[h][:, 0:1]
        row = jnp.transpose(li - b)
        log_w = jnp.where(vis, b + row, -jnp.inf)
        m_inter = b + m
        m_t = jnp.maximum(m_inter, jnp.max(log_w, axis=-1, keepdims=True))
        w_inter = jnp.exp(m_inter - m_t)
        w_qk = jnp.exp(log_w - m_t) * _nt(qh, kh)
        num = w_inter * _nt(qh, cms[h]) + _nn(w_qk, vh)
        den = w_inter * jnp.sum(qh * nvs[h], axis=-1, keepdims=True) + jnp.sum(w_qk, axis=-1, keepdims=True)
        outs.append(num / jnp.maximum(jnp.abs(den), jnp.exp(-m_t)))
        m_new = m_t[last:last + 1]
        b_end = b[last:last + 1]
        w_s = jnp.exp(b_end - b + li - m_new)
        decay = jnp.exp(b_end + m - m_new)
        new_c.append(decay * cms[h] + _tn(w_s * vh, kh))
        new_n.append(decay * nvs[h] + jnp.sum(w_s * kh, axis=0, keepdims=True))
        new_m.append(jnp.broadcast_to(m_new, (1, LANE)))
    return (new_c, new_n, new_m), jnp.concatenate(outs, axis=1)


def _post_fn(o_f, o_b, az, h_f, h_b, bo, bz, wa, wb, n_hg, n_ml):
    o = o_f + o_b
    parts = []
    for h in range(n_hg):
        s = o[:, h * HG_DK:(h + 1) * HG_DK]
        parts.append(s * lax.rsqrt(jnp.mean(s * s, axis=-1, keepdims=True) + NORM_EPS))
    y_a = jnp.concatenate(parts, axis=1) * wa * _silu(az)
    hh = h_f + h_b
    dh = hh.shape[1] // n_ml
    parts = []
    for h in range(n_ml):
        s = hh[:, h * dh:(h + 1) * dh]
        mu = jnp.mean(s, axis=-1, keepdims=True)
        sc = s - mu
        parts.append(sc * lax.rsqrt(jnp.mean(sc * sc, axis=-1, keepdims=True) + NORM_EPS))
    y_b = jnp.concatenate(parts, axis=1) * wb * _sigmoid(bo) * _silu(bz)
    return jnp.concatenate([y_a, y_b], axis=1)


def _chip_of(dev):
    return 2 * dev[0] + dev[1]


def _index_of(dev):
    return 4 * dev[0] + 2 * dev[1] + dev[2]


class _Exchange:
    def __init__(self, srcs, out_shapes, transfers, local_copies=()):
        self.srcs, self.out_shapes = list(srcs), list(out_shapes)
        self.transfers, self.local_copies = list(transfers), list(local_copies)

    def scratch(self):
        return [pltpu.SemaphoreType.DMA((len(self.transfers),)), pltpu.SemaphoreType.DMA((len(self.transfers),)),
                pltpu.SemaphoreType.DMA((max(len(self.local_copies), 1),))]

    def copies(self, ins, outs, send_sems, recv_sems, local_sems):
        me = (lax.axis_index("x"), lax.axis_index("y"), lax.axis_index("c"))

        def pick(ref, fn, *who):
            return ref if fn is None else ref.at[fn(*who)]

        sends, recvs, locs = [], [], []
        for t, (mask, si, sfn, di, dfn) in enumerate(self.transfers):
            peer = tuple(1 - p if flip else p for p, flip in zip(me, mask))
            sends.append(pltpu.make_async_remote_copy(
                src_ref=pick(ins[si], sfn, me, peer), dst_ref=pick(outs[di], dfn, me, peer),
                send_sem=send_sems.at[t], recv_sem=recv_sems.at[t], device_id=peer, device_id_type=MESH))
            landing = pick(outs[di], dfn, peer, me)
            recvs.append(pltpu.make_async_remote_copy(
                src_ref=landing, dst_ref=landing,
                send_sem=send_sems.at[t], recv_sem=recv_sems.at[t], device_id=peer, device_id_type=MESH))
        for l, (si, sfn, di, dfn) in enumerate(self.local_copies):
            locs.append(pltpu.make_async_copy(pick(ins[si], sfn, me), pick(outs[di], dfn, me), local_sems.at[l]))

        def start():
            for cp in locs + sends:
                cp.start()

        def wait():
            for cp in recvs:
                cp.wait_recv()
            for cp in sends:
                cp.wait_send()
            for cp in locs:
                cp.wait()

        return start, wait

    def run(self, name):
        n_in, n_out = len(self.srcs), len(self.out_shapes)

        def body(*refs):
            start, wait = self.copies(refs[:n_in], refs[n_in:n_in + n_out], *refs[n_in + n_out:])
            start()
            wait()

        hbm = pl.BlockSpec(memory_space=pltpu.HBM)
        return pl.pallas_call(
            body, name=name, out_shape=tuple(self.out_shapes), in_specs=[hbm] * n_in,
            out_specs=tuple([hbm] * n_out), scratch_shapes=self.scratch(),
        )(*self.srcs)


def _call(body, operands, *, name, grid, in_specs, out_specs, out_shape, scratch_shapes=(), sem=None, rider=None):
    out_specs, out_shape, scratch_shapes = list(out_specs), list(out_shape), list(scratch_shapes)
    if rider is None:
        res = pl.pallas_call(
            body, name=name, grid=grid, in_specs=list(in_specs), out_specs=tuple(out_specs),
            out_shape=tuple(out_shape), scratch_shapes=scratch_shapes, compiler_params=_params(sem),
        )(*operands)
        return list(res), []
    counts = (len(in_specs), len(rider.srcs), len(out_specs), len(rider.out_shapes), len(scratch_shapes), 3)

    def full(*refs):
        groups, pos = [], 0
        for k in counts:
            groups.append(refs[pos:pos + k])
            pos += k
        own_in, ex_in, own_out, ex_out, own_scr, ex_scr = groups
        ids = [pl.program_id(a) for a in range(len(grid))]
        first = functools.reduce(jnp.logical_and, [i == 0 for i in ids])
        last = functools.reduce(jnp.logical_and, [i == g - 1 for i, g in zip(ids, grid)])
        start, wait = rider.copies(ex_in, ex_out, *ex_scr)
        pl.when(first)(start)
        body(*own_in, *own_out, *own_scr)
        pl.when(last)(wait)

    hbm = pl.BlockSpec(memory_space=pltpu.HBM)
    res = pl.pallas_call(
        full, name=name, grid=grid, in_specs=list(in_specs) + [hbm] * counts[1],
        out_specs=tuple(out_specs + [hbm] * counts[3]), out_shape=tuple(out_shape + rider.out_shapes),
        scratch_shapes=scratch_shapes + rider.scratch(), compiler_params=_params(("arbitrary",) * len(grid)),
    )(*operands, *rider.srcs)
    return list(res[:counts[2]]), list(res[counts[2]:])


ALL_MASKS = [(mx, my, mc) for mx in (0, 1) for my in (0, 1) for mc in (0, 1)][1:]
CHIP_MASKS = [(1, 0, 0), (0, 1, 0), (1, 1, 0)]
SIBLING_MASK = (0, 0, 1)


def _all_gather8(v):
    out = jax.ShapeDtypeStruct((N_DEV,) + v.shape, v.dtype)
    slot = lambda sender, receiver: _index_of(sender)
    transfers = [(mask, 0, None, 0, slot) for mask in ALL_MASKS]
    return _Exchange([v], [out], transfers, [(0, None, 0, lambda me: _index_of(me))])


def _all_gather_chips(arrays):
    outs = [jax.ShapeDtypeStruct((N_CHIPS,) + a.shape, a.dtype) for a in arrays]
    slot = lambda sender, receiver: _chip_of(sender)
    return _Exchange(arrays, outs, [(mask, i, None, i, slot) for i in range(len(arrays)) for mask in CHIP_MASKS])


def _sibling_swap(arrays):
    outs = [jax.ShapeDtypeStruct(a.shape, a.dtype) for a in arrays]
    return _Exchange(arrays, outs, [(SIBLING_MASK, i, None, i, None) for i in range(len(arrays))])


def _chip_scatter(arrays):
    outs = [jax.ShapeDtypeStruct(a.shape, a.dtype) for a in arrays]
    transfers = [(mask, i, lambda s, r: _chip_of(r), i, lambda s, r: _chip_of(s))
                 for i in range(len(arrays)) for mask in CHIP_MASKS]
    return _Exchange(arrays, outs, transfers)


def _own_block(chip, own, blocks):
    sel = (lax.broadcasted_iota(jnp.int32, (N_CHIPS,) + (1,) * (blocks.ndim - 1), 0) == chip)
    return jnp.where(sel, own if own.ndim == blocks.ndim else own[None], blocks)


def _join_halves(ci, mine, other, axis):
    return jnp.where(ci == 0, jnp.concatenate([mine, other], axis=axis), jnp.concatenate([other, mine], axis=axis))


def _mm_nt(name, a, b, tm, tn, out_dtype, rider=None):
    m, k = a.shape
    n = b.shape[0]

    def body(a_ref, b_ref, o_ref):
        o_ref[...] = _nt(a_ref[...], b_ref[...]).astype(out_dtype)

    (out,), rode = _call(
        body, (a, b), name=name, grid=(n // tn, m // tm),
        in_specs=[pl.BlockSpec((tm, k), lambda j, i: (i, 0)), pl.BlockSpec((tn, k), lambda j, i: (j, 0))],
        out_specs=[pl.BlockSpec((tm, tn), lambda j, i: (i, j))],
        out_shape=[jax.ShapeDtypeStruct((m, n), out_dtype)], sem=("parallel", "parallel"), rider=rider)
    return out, rode


def _mm_acc(name, a, b, tm, tk, rider=None):
    m, kc = a.shape
    n = b.shape[1]

    def body(a_ref, b_ref, o_ref):
        @pl.when(pl.program_id(1) == 0)
        def _():
            o_ref[...] = jnp.zeros_like(o_ref)
        o_ref[...] += _nn(a_ref[...], b_ref[...])

    (out,), rode = _call(
        body, (a, b), name=name, grid=(m // tm, kc // tk),
        in_specs=[pl.BlockSpec((tm, tk), lambda i, kk: (i, kk)), pl.BlockSpec((tk, n), lambda i, kk: (kk, 0))],
        out_specs=[pl.BlockSpec((tm, n), lambda i, kk: (i, 0))],
        out_shape=[jax.ShapeDtypeStruct((m, n), F32)], sem=("parallel", "arbitrary"), rider=rider)
    return out, rode


def _mm_tn(name, a, b, tm, tk, with_bf16=False):
    kr, m = a.shape
    n = b.shape[1]
    steps_k = kr // tk

    def body(a_ref, b_ref, o_ref, *narrow):
        @pl.when(pl.program_id(1) == 0)
        def _():
            o_ref[...] = jnp.zeros_like(o_ref)
        o_ref[...] += _tn(a_ref[...], b_ref[...])
        if with_bf16:
            @pl.when(pl.program_id(1) == steps_k - 1)
            def _():
                narrow[0][...] = o_ref[...].astype(BF16)

    out_spec = pl.BlockSpec((tm, n), lambda i, kk: (i, 0))
    res = pl.pallas_call(
        body, name=name, grid=(m // tm, steps_k),
        in_specs=[pl.BlockSpec((tk, tm), lambda i, kk: (kk, i)), pl.BlockSpec((tk, n), lambda i, kk: (kk, 0))],
        out_specs=(out_spec,) * (2 if with_bf16 else 1),
        out_shape=(jax.ShapeDtypeStruct((m, n), F32),) + ((jax.ShapeDtypeStruct((m, n), BF16),) if with_bf16 else ()),
        compiler_params=_params(("parallel", "arbitrary")),
    )(a, b)
    return res if with_bf16 else res[0]


def _modulate_fwd(x, ctx, prm, tm):
    t_rows, dm = x.shape
    lat = t_rows // tm
    r = t_rows + ctx.shape[0]

    def body(x_ref, c_ref, p_ref, h_ref):
        xv = jnp.where(pl.program_id(0) >= lat, c_ref[...], x_ref[...])
        mu = jnp.mean(xv, axis=-1, keepdims=True)
        xm = xv - mu
        n = xm * lax.rsqrt(jnp.mean(xm * xm, axis=-1, keepdims=True) + LN_EPS)
        h_ref[...] = (n * (1.0 + p_ref[0, 1:2, :]) + p_ref[0, 0:1, :]).astype(BF16)

    return pl.pallas_call(
        body, name="modulate_fwd", grid=(r // tm,),
        in_specs=[pl.BlockSpec((tm, dm), lambda i: (jnp.minimum(i, lat - 1), 0)),
                  pl.BlockSpec((tm, dm), lambda i: (jnp.maximum(i - lat, 0), 0)),
                  pl.BlockSpec((1, 8, dm), lambda i: ((i >= lat).astype(jnp.int32), 0, 0))],
        out_specs=pl.BlockSpec((tm, dm), lambda i: (i, 0)),
        out_shape=jax.ShapeDtypeStruct((r, dm), BF16),
        compiler_params=_params(("parallel",)),
    )(x, ctx, prm)


def _modulate_bwd(x, ctx, dh, prm, gx_direct, tm, rider=None):
    t_rows, dm = x.shape
    lat, n_ct = t_rows // tm, ctx.shape[0] // tm
    is_ctx = lambda i: i < n_ct
    cls = lambda i: is_ctx(i).astype(jnp.int32)
    lat_tile = lambda i: (jnp.maximum(i - n_ct, 0), 0)

    def body(x_ref, c_ref, dh_ref, p_ref, gd_ref, gx_ref, acc_ref):
        i = pl.program_id(0)

        @pl.when((i == 0) | (i == n_ct))
        def _():
            acc_ref[...] = jnp.zeros_like(acc_ref)

        x = jnp.where(is_ctx(i), c_ref[...], x_ref[...])
        dh_v = dh_ref[...]
        mu = jnp.mean(x, axis=-1, keepdims=True)
        xm = x - mu
        rstd = lax.rsqrt(jnp.mean(xm * xm, axis=-1, keepdims=True) + LN_EPS)
        n = xm * rstd
        acc_ref[0, 0:1, :] += jnp.sum(dh_v, axis=0, keepdims=True)
        acc_ref[0, 1:2, :] += jnp.sum(dh_v * n, axis=0, keepdims=True)
        dn = dh_v * (1.0 + p_ref[0, 1:2, :])
        dx = rstd * (dn - jnp.mean(dn, axis=-1, keepdims=True) - n * jnp.mean(dn * n, axis=-1, keepdims=True))
        gx_ref[...] = dx + gd_ref[...]

    return _call(
        body, (x, ctx, dh, prm, gx_direct), name="modulate_bwd", grid=(n_ct + lat,),
        in_specs=[pl.BlockSpec((tm, dm), lat_tile),
                  pl.BlockSpec((tm, dm), lambda i: (jnp.minimum(i, n_ct - 1), 0)),
                  pl.BlockSpec((tm, dm), lambda i: (jnp.where(is_ctx(i), lat + i, i - n_ct), 0)),
                  pl.BlockSpec((1, 8, dm), lambda i: (cls(i), 0, 0)),
                  pl.BlockSpec((tm, dm), lat_tile)],
        out_specs=(pl.BlockSpec((tm, dm), lat_tile), pl.BlockSpec((1, 8, dm), lambda i: (cls(i), 0, 0))),
        out_shape=(jax.ShapeDtypeStruct((t_rows, dm), F32), jax.ShapeDtypeStruct((2, 8, dm), F32)),
        sem=("arbitrary",), rider=rider)


def _conv_parts(t_rows, c_rows):
    return ((0, t_rows, t_rows // GRID_W, GRID_W), (t_rows, c_rows, 1, c_rows))


def _col_shifts(x2, rows_g, width_g):
    n, ct = x2.shape
    col = lax.broadcasted_iota(jnp.int32, (width_g, ct), 0)
    as_grid = lambda a: a.reshape(rows_g, width_g, ct)
    left = as_grid(pltpu.roll(x2, 1, 0)) * (col >= 1).astype(F32)
    right = as_grid(pltpu.roll(x2, n - 1, 0)) * (col <= width_g - 2).astype(F32)
    return [left, as_grid(x2), right]


def _row_shift(y3, a):
    if a == 1:
        return y3
    if y3.shape[0] == 1:
        return jnp.zeros_like(y3)
    zero = jnp.zeros_like(y3[:1])
    return jnp.concatenate([zero, y3[:-1]], axis=0) if a == 0 else jnp.concatenate([y3[1:], zero], axis=0)


def _conv_taps(cols, w_ref, flip):
    rows_g = cols[0].shape[0]
    acc = None
    for a in range(3):
        if rows_g == 1 and a != 1:
            continue
        inner = None
        for b in range(3):
            tap = (2 - a) * 3 + (2 - b) if flip else a * 3 + b
            term = cols[b] * w_ref[tap:tap + 1, :]
            inner = term if inner is None else inner + term
        inner = _row_shift(inner, a)
        acc = inner if acc is None else acc + inner
    return acc


def _conv_fwd(u, conv_w9, conv_b, t_rows, c_rows, w, ct):
    r = u.shape[0]
    base = 5 * w // ct

    def body(x_ref, w_ref, b_ref, o_ref):
        for r0, n, rows_g, width_g in _conv_parts(t_rows, c_rows):
            pre = _conv_taps(_col_shifts(x_ref[r0:r0 + n, :], rows_g, width_g), w_ref, False) + b_ref[...]
            o_ref[r0:r0 + n, :] = _silu(pre).reshape(n, ct)

    return pl.pallas_call(
        body, name="conv_fwd", grid=(2 * w // ct,),
        in_specs=[pl.BlockSpec((r, ct), lambda i: (0, base + i)), pl.BlockSpec((9, ct), lambda i: (0, i)),
                  pl.BlockSpec((1, ct), lambda i: (0, i))],
        out_specs=pl.BlockSpec((r, ct), lambda i: (0, i)),
        out_shape=jax.ShapeDtypeStruct((r, 2 * w), F32),
        compiler_params=_params(("parallel",)),
    )(u, conv_w9, conv_b)


def _conv_bwd(u, dqk_pair, conv_w9, conv_b, t_rows, c_rows, w, ct):
    r = u.shape[0]
    base = 5 * w // ct

    def body(x_ref, d1_ref, d2_ref, w_ref, b_ref, dx_ref, dw_ref, db_ref):
        dw = [jnp.zeros((1, ct), F32) for _ in range(9)]
        db = jnp.zeros((1, ct), F32)
        for r0, n, rows_g, width_g in _conv_parts(t_rows, c_rows):
            cols = _col_shifts(x_ref[r0:r0 + n, :], rows_g, width_g)
            pre = (_conv_taps(cols, w_ref, False) + b_ref[...]).reshape(n, ct)
            sg = _sigmoid(pre)
            dpre = (d1_ref[r0:r0 + n, :] + d2_ref[r0:r0 + n, :]) * (sg * (1.0 + pre * (1.0 - sg)))
            db = db + jnp.sum(dpre, axis=0, keepdims=True)
            dx_ref[r0:r0 + n, :] = _conv_taps(_col_shifts(dpre, rows_g, width_g), w_ref, True).reshape(n, ct)
            dpre3 = dpre.reshape(rows_g, width_g, ct)
            for a in range(3):
                if rows_g == 1 and a != 1:
                    continue
                moved = _row_shift(dpre3, 2 - a)
                for b in range(3):
                    prod = jnp.sum(cols[b] * moved, axis=0)
                    dw[a * 3 + b] = dw[a * 3 + b] + jnp.sum(prod, axis=0, keepdims=True)
        for tap in range(9):
            dw_ref[tap:tap + 1, :] = dw[tap]
        db_ref[...] = db

    return pl.pallas_call(
        body, name="conv_bwd", grid=(2 * w // ct,),
        in_specs=[pl.BlockSpec((r, ct), lambda i: (0, base + i)), pl.BlockSpec((r, ct), lambda i: (0, i)),
                  pl.BlockSpec((r, ct), lambda i: (0, i)),
                  pl.BlockSpec((9, ct), lambda i: (0, i)), pl.BlockSpec((1, ct), lambda i: (0, i))],
        out_specs=(pl.BlockSpec((r, ct), lambda i: (0, i)), pl.BlockSpec((9, ct), lambda i: (0, i)),
                   pl.BlockSpec((1, ct), lambda i: (0, i))),
        out_shape=(jax.ShapeDtypeStruct((r, 2 * w), F32), jax.ShapeDtypeStruct((9, 2 * w), F32),
                   jax.ShapeDtypeStruct((1, 2 * w), F32)),
        compiler_params=_params(("parallel",)),
    )(u, dqk_pair[0], dqk_pair[1], conv_w9, conv_b)


def _assemble_du(groups, gates, n_pad, tm):
    flat, layout = [], []
    for entry in list(groups) + [gates]:
        parts = entry if isinstance(entry, (tuple, list)) else (entry,)
        layout.append((len(flat), len(parts), parts[0].shape[1]))
        flat += list(parts)
    r = flat[0].shape[0]

    def body(*refs):
        o_ref = refs[-1]
        col = 0
        for first, count, width in layout:
            val = refs[first][...]
            for extra in range(1, count):
                val = val + refs[first + extra][...]
            o_ref[:, col:col + width] = val.astype(BF16)
            col += width
        assert col == n_pad

    return pl.pallas_call(
        body, name="assemble_du", grid=(r // tm,),
        in_specs=[pl.BlockSpec((tm, a.shape[1]), lambda i: (i, 0)) for a in flat],
        out_specs=pl.BlockSpec((tm, n_pad), lambda i: (i, 0)),
        out_shape=jax.ShapeDtypeStruct((r, n_pad), BF16),
        compiler_params=_params(("parallel",)),
    )(*flat)


def _scan_order(n_lat, n_ctx, rev):
    n = n_lat + n_ctx
    if rev:
        return lambda j: n - 1 - j
    return lambda j: (j + n_lat) % n


DIRS = (False, True)


def _hg_scan_fwd(u, lb_full, w, n_lat, n_ctx, chunk, rider=None):
    r = u.shape[0]
    n_heads = w // HG_DK
    n_chunks = n_lat + n_ctx
    nat = [_scan_order(n_lat, n_ctx, rev) for rev in DIRS]

    def body(*refs):
        ins, outs, scratch = refs[:8], refs[8:12], refs[12:]

        @pl.when(pl.program_id(0) == 0)
        def _():
            for s_ref in scratch:
                s_ref[...] = jnp.zeros_like(s_ref)

        results = []
        for d, rev in enumerate(DIRS):
            aq, af, ai, lb_ref = ins[4 * d:4 * d + 4]
            state = [scratch[d][h] for h in range(n_heads)]
            results.append((state, _hg_chunk(state, aq[...], af[...], ai[...],
                                             lb_ref[0, 0:1, :], lb_ref[0, 1:2, :], rev)))
        for d, (state, (new, o)) in enumerate(results):
            o_ref, save_ref = outs[2 * d:2 * d + 2]
            o_ref[...] = o
            for h in range(n_heads):
                save_ref[0, h] = state[h]
                scratch[d][h] = new[h]

    in_specs, out_specs, out_shape = [], [], []
    for d in range(2):
        in_specs += [pl.BlockSpec((chunk,w), lambda j, d=d: (nat[d](j), 0)),
                     pl.BlockSpec((chunk,w), lambda j, d=d: (nat[d](j), 1 + d)),
                     pl.BlockSpec((chunk,w), lambda j, d=d: (nat[d](j), 3)),
                     pl.BlockSpec((1, 2, w), lambda j, d=d: (d, 0, 0))]
        out_specs += [pl.BlockSpec((chunk,w), lambda j, d=d: (nat[d](j), 0)),
                      pl.BlockSpec((1, n_heads, HG_DK, HG_DK), lambda j: (j, 0, 0, 0))]
        out_shape += [jax.ShapeDtypeStruct((r, w), F32),
                      jax.ShapeDtypeStruct((n_chunks, n_heads, HG_DK, HG_DK), F32)]
    (o_f, s_f, o_b, s_b), rode = _call(
        body, (u, u, u, lb_full, u, u, u, lb_full), name="hg_scan_fwd", grid=(n_chunks,), in_specs=in_specs,
        out_specs=out_specs, out_shape=out_shape, scratch_shapes=[pltpu.VMEM((n_heads, HG_DK, HG_DK), F32)] * 2,
        sem=("arbitrary",), rider=rider)
    return (o_f, o_b), (s_f, s_b), rode


def _hg_scan_bwd(u, lb_full, saved, d_o, w, n_lat, n_ctx, chunk, rider=None):
    r = u.shape[0]
    n_heads = w // HG_DK
    n_chunks = n_lat + n_ctx
    step = lambda jj: n_chunks - 1 - jj
    nat = [(lambda jj, o=_scan_order(n_lat, n_ctx, rev): o(step(jj))) for rev in DIRS]

    def body(*refs):
        ins, outs, scratch = refs[:12], refs[12:20], refs[20:]
        jj = pl.program_id(0)

        @pl.when(jj == 0)
        def _():
            for d in range(2):
                scratch[d][...] = jnp.zeros_like(scratch[d])
                outs[4 * d + 3][...] = jnp.zeros_like(outs[4 * d + 3])

        results = []
        for d, rev in enumerate(DIRS):
            aq, af, ai, lb_ref, save_ref, do_ref = ins[6 * d:6 * d + 6]
            f = lambda st, a, b, c, l0, l1, rev=rev: _hg_chunk(st, a, b, c, l0, l1, rev)
            _, vjp = jax.vjp(f, [save_ref[0, h] for h in range(n_heads)], aq[...], af[...], ai[...],
                             lb_ref[0, 0:1, :], lb_ref[0, 1:2, :])
            d_out = do_ref[...] * (nat[d](jj) < n_lat).astype(F32)
            results.append(vjp(([scratch[d][h] for h in range(n_heads)], d_out)))
        for d, (dst, daq, daf, dai, dl0, dl1) in enumerate(results):
            daq_ref, daf_ref, dai_ref, dlb_ref = outs[4 * d:4 * d + 4]
            for h in range(n_heads):
                scratch[d][h] = dst[h]
            daq_ref[...] = daq
            daf_ref[...] = daf
            dai_ref[...] = dai
            dlb_ref[0:1, :] += dl0
            dlb_ref[1:2, :] += dl1

    in_specs, out_specs, out_shape, operands = [], [], [], []
    for d in range(2):
        row = lambda jj, d=d: (nat[d](jj), 0)
        in_specs += [pl.BlockSpec((chunk,w), row),
                     pl.BlockSpec((chunk,w), lambda jj, d=d: (nat[d](jj), 1 + d)),
                     pl.BlockSpec((chunk,w), lambda jj, d=d: (nat[d](jj), 3)),
                     pl.BlockSpec((1, 2, w), lambda jj, d=d: (d, 0, 0)),
                     pl.BlockSpec((1, n_heads, HG_DK, HG_DK), lambda jj: (step(jj), 0, 0, 0)),
                     pl.BlockSpec((chunk,w), lambda jj, d=d: (jnp.minimum(nat[d](jj), n_lat - 1), 0))]
        operands += [u, u, u, lb_full, saved[d], d_o]
        out_specs += [pl.BlockSpec((chunk,w), row)] * 3 + [pl.BlockSpec((2, w), lambda jj: (0, 0))]
        out_shape += [jax.ShapeDtypeStruct((r, w), F32)] * 3 + [jax.ShapeDtypeStruct((2, w), F32)]
    res, rode = _call(
        body, operands, name="hg_scan_bwd", grid=(n_chunks,), in_specs=in_specs, out_specs=out_specs,
        out_shape=out_shape, scratch_shapes=[pltpu.VMEM((n_heads, HG_DK, HG_DK), F32)] * 2,
        sem=("arbitrary",), rider=rider)
    return res[0:4], res[4:8], rode


def _ml_state_shapes(n_chunks, n_heads, dh):
    return (jax.ShapeDtypeStruct((n_chunks, n_heads, dh, dh), F32),
            jax.ShapeDtypeStruct((n_chunks, n_heads, 1, dh), F32),
            jax.ShapeDtypeStruct((n_chunks, n_heads, 1, LANE), F32))


def _ml_state_specs(n_heads, dh, index):
    return (pl.BlockSpec((1, n_heads, dh, dh), lambda j: (index(j), 0, 0, 0)),
            pl.BlockSpec((1, n_heads, 1, dh), lambda j: (index(j), 0, 0, 0)),
            pl.BlockSpec((1, n_heads, 1, LANE), lambda j: (index(j), 0, 0, 0)))


def _ml_state_scratch(n_heads, dh):
    return [pltpu.VMEM((n_heads, dh, dh), F32), pltpu.VMEM((n_heads, 1, dh), F32), pltpu.VMEM((n_heads, 1, LANE), F32)]


def _ml_scan_fwd(qk, u, gate_b, w, n_heads, n_lat, n_ctx, chunk):
    r = u.shape[0]
    dh = w // n_heads
    n_chunks = n_lat + n_ctx
    nat = [_scan_order(n_lat, n_ctx, rev) for rev in DIRS]

    def body(*refs):
        ins, outs, scratch = refs[:10], refs[10:18], refs[18:]

        @pl.when(pl.program_id(0) == 0)
        def _():
            for s_ref in scratch:
                s_ref[...] = jnp.zeros_like(s_ref)

        results = []
        for d, rev in enumerate(DIRS):
            q, k, v, g, gb = ins[5 * d:5 * d + 5]
            state = tuple([ref[h] for h in range(n_heads)] for ref in scratch[3 * d:3 * d + 3])
            results.append((state, _ml_chunk(state, q[...], k[...], v[...], g[...], gb[...], rev, d)))
        for d, (state, (new, o)) in enumerate(results):
            outs[4 * d][...] = o
            for part in range(3):
                for h in range(n_heads):
                    outs[4 * d + 1 + part][0, h] = state[part][h]
                    scratch[3 * d + part][h] = new[part][h]

    in_specs, out_specs, out_shape = [], [], []
    for d in range(2):
        in_specs += [pl.BlockSpec((chunk,w), lambda j, d=d: (nat[d](j), 0)),
                     pl.BlockSpec((chunk,w), lambda j, d=d: (nat[d](j), 1)),
                     pl.BlockSpec((chunk,w), lambda j, d=d: (nat[d](j), 7)),
                     pl.BlockSpec((chunk,LANE), lambda j, d=d: (nat[d](j), 10 * w // LANE)),
                     pl.BlockSpec((1, LANE), lambda j: (0, 0))]
        out_specs += [pl.BlockSpec((chunk,w), lambda j, d=d: (nat[d](j), 0))]
        out_specs += list(_ml_state_specs(n_heads, dh, lambda j: j))
        out_shape += [jax.ShapeDtypeStruct((r, w), F32)] + list(_ml_state_shapes(n_chunks, n_heads, dh))
    res = pl.pallas_call(
        body, name="ml_scan_fwd", grid=(n_chunks,), in_specs=in_specs, out_specs=tuple(out_specs),
        out_shape=tuple(out_shape), scratch_shapes=_ml_state_scratch(n_heads, dh) * 2,
        compiler_params=_params(("arbitrary",)),
    )(qk, qk, u, u, gate_b, qk, qk, u, u, gate_b)
    return (res[0], res[4]), (res[1:4], res[5:8])


def _ml_scan_bwd(qk, u, gate_b, saved, d_h, w, n_heads, n_lat, n_ctx, chunk, rider=None):
    r = u.shape[0]
    dh = w // n_heads
    n_chunks = n_lat + n_ctx
    step = lambda jj: n_chunks - 1 - jj
    nat = [(lambda jj, o=_scan_order(n_lat, n_ctx, rev): o(step(jj))) for rev in DIRS]

    def body(*refs):
        ins, outs, scratch = refs[:18], refs[18:26], refs[26:]
        jj = pl.program_id(0)

        @pl.when(jj == 0)
        def _():
            for s_ref in scratch:
                s_ref[...] = jnp.zeros_like(s_ref)
            for d in range(2):
                outs[4 * d + 3][...] = jnp.zeros_like(outs[4 * d + 3])

        results = []
        for d, rev in enumerate(DIRS):
            q, k, v, g, gb, sc, sn, sm, dh_ref = ins[9 * d:9 * d + 9]
            state = tuple([ref[0, h] for h in range(n_heads)] for ref in (sc, sn, sm))
            f = lambda st, a, b, c, gg, bb, rev=rev, d=d: _ml_chunk(st, a, b, c, gg, bb, rev, d)
            _, vjp = jax.vjp(f, state, q[...], k[...], v[...], g[...], gb[...])
            d_state = tuple([ref[h] for h in range(n_heads)] for ref in scratch[3 * d:3 * d + 3])
            d_out = dh_ref[...] * (nat[d](jj) < n_lat).astype(F32)
            results.append(vjp((d_state, d_out)))
        for d, (d_state, dq, dk, dv, dg, dgb) in enumerate(results):
            dqk_ref, dv_ref, dg_ref, dgb_ref = outs[4 * d:4 * d + 4]
            for part in range(3):
                for h in range(n_heads):
                    scratch[3 * d + part][h] = d_state[part][h]
            dqk_ref[:, 0:w] = dq
            dqk_ref[:, w:2 * w] = dk
            dv_ref[...] = dv
            dg_ref[...] = dg
            dgb_ref[...] += dgb

    in_specs, out_specs, out_shape, operands = [], [], [], []
    for d in range(2):
        row = lambda jj, d=d: (nat[d](jj), 0)
        in_specs += [pl.BlockSpec((chunk,w), row), pl.BlockSpec((chunk,w), lambda jj, d=d: (nat[d](jj), 1)),
                     pl.BlockSpec((chunk,w), lambda jj, d=d: (nat[d](jj), 7)),
                     pl.BlockSpec((chunk,LANE), lambda jj, d=d: (nat[d](jj), 10 * w // LANE)),
                     pl.BlockSpec((1, LANE), lambda jj: (0, 0))]
        in_specs += list(_ml_state_specs(n_heads, dh, step))
        in_specs += [pl.BlockSpec((chunk,w), lambda jj, d=d: (jnp.minimum(nat[d](jj), n_lat - 1), 0))]
        operands += [qk, qk, u, u, gate_b, *saved[d], d_h]
        out_specs += [pl.BlockSpec((chunk,2 * w), row), pl.BlockSpec((chunk,w), row),
                      pl.BlockSpec((chunk,LANE), row), pl.BlockSpec((1, LANE), lambda jj: (0, 0))]
        out_shape += [jax.ShapeDtypeStruct((r, 2 * w), F32), jax.ShapeDtypeStruct((r, w), F32),
                      jax.ShapeDtypeStruct((r, LANE), F32), jax.ShapeDtypeStruct((1, LANE), F32)]
    res, rode = _call(
        body, operands, name="ml_scan_bwd", grid=(n_chunks,), in_specs=in_specs, out_specs=out_specs,
        out_shape=out_shape, scratch_shapes=_ml_state_scratch(n_heads, dh) * 2, sem=("arbitrary",), rider=rider)
    return res[0:4], res[4:8], rode


def _post_specs(w, tm, lat_tiles, cols):
    return [pl.BlockSpec((tm, w), (lambda i, cb=cb: (jnp.minimum(i, lat_tiles - 1), cb))) for cb in cols]


def _post_fwd(o_f, o_b, h_f, h_b, u, wa, wb, t_rows, w, n_hg, n_ml, tm):
    lat_tiles = t_rows // tm

    def body(of, ob, hf, hb, az, bo, bz, wa_ref, wb_ref, y_ref):
        y_ref[...] = _post_fn(of[...], ob[...], az[...], hf[...], hb[...], bo[...], bz[...],
                              wa_ref[...], wb_ref[...], n_hg, n_ml).astype(BF16)

    rows = pl.BlockSpec((tm, w), lambda i: (i, 0))
    vec = pl.BlockSpec((1, w), lambda i: (0, 0))
    return pl.pallas_call(
        body, name="post_fwd", grid=(lat_tiles,),
        in_specs=[rows] * 4 + _post_specs(w, tm, lat_tiles, (4, 8, 9)) + [vec, vec],
        out_specs=pl.BlockSpec((tm, 2 * w), lambda i: (i, 0)),
        out_shape=jax.ShapeDtypeStruct((t_rows, 2 * w), BF16),
        compiler_params=_params(("parallel",)),
    )(o_f, o_b, h_f, h_b, u, u, u, wa, wb)


def _post_bwd(o_f, o_b, h_f, h_b, u, wa, wb, dy, t_rows, w, n_hg, n_ml, tm, rider=None):
    r = u.shape[0]
    lat_tiles = t_rows // tm
    lat = lambda i: (jnp.minimum(i, lat_tiles - 1), 0)

    def body(of, ob, hf, hb, az, bo, bz, wa_ref, wb_ref, dy_ref, do_ref, dh_ref, daz_ref, dbo_ref, dbz_ref,
             dwa_ref, dwb_ref):
        i = pl.program_id(0)

        @pl.when(i == 0)
        def _():
            dwa_ref[...] = jnp.zeros_like(dwa_ref)
            dwb_ref[...] = jnp.zeros_like(dwb_ref)

        @pl.when(i < lat_tiles)
        def _():
            f = functools.partial(_post_fn, n_hg=n_hg, n_ml=n_ml)
            _, vjp = jax.vjp(f, of[...], ob[...], az[...], hf[...], hb[...], bo[...], bz[...], wa_ref[...], wb_ref[...])
            d_of, _, d_az, d_hf, _, d_bo, d_bz, d_wa, d_wb = vjp(dy_ref[...])
            do_ref[...] = d_of
            dh_ref[...] = d_hf
            daz_ref[...] = d_az
            dbo_ref[...] = d_bo
            dbz_ref[...] = d_bz
            dwa_ref[...] += d_wa
            dwb_ref[...] += d_wb

        @pl.when(i >= lat_tiles)
        def _():
            daz_ref[...] = jnp.zeros_like(daz_ref)
            dbo_ref[...] = jnp.zeros_like(dbo_ref)
            dbz_ref[...] = jnp.zeros_like(dbz_ref)

    lat_rows = pl.BlockSpec((tm, w), lat)
    all_rows = pl.BlockSpec((tm, w), lambda i: (i, 0))
    vec = pl.BlockSpec((1, w), lambda i: (0, 0))
    sd_t = jax.ShapeDtypeStruct((t_rows, w), F32)
    sd_r = jax.ShapeDtypeStruct((r, w), F32)
    sd_v = jax.ShapeDtypeStruct((1, w), F32)
    return _call(
        body, (o_f, o_b, h_f, h_b, u, u, u, wa, wb, dy), name="post_bwd", grid=(r // tm,),
        in_specs=[lat_rows] * 4 + _post_specs(w, tm, lat_tiles, (4, 8, 9)) + [vec, vec]
        + [pl.BlockSpec((tm, 2 * w), lat)],
        out_specs=(lat_rows, lat_rows, all_rows, all_rows, all_rows, vec, vec),
        out_shape=(sd_t, sd_t, sd_r, sd_r, sd_r, sd_v, sd_v), sem=("arbitrary",), rider=rider)


OUT_ROW_GATE, OUT_ROW_LN_G, OUT_ROW_LN_B, OUT_ROW_LOSS = 0, 1, 2, 3


def _out_block(y, w_out, x, target, prm, tm):
    t_rows, dm = x.shape
    di = y.shape[1]

    def body(y_ref, w_ref, x_ref, t_ref, p_ref, dz_ref, dy_ref, gx_ref, acc_ref):
        @pl.when(pl.program_id(0) == 0)
        def _():
            acc_ref[...] = jnp.zeros_like(acc_ref)

        gate, ln_g, ln_b = p_ref[0:1, :], p_ref[1:2, :], p_ref[2:3, :]
        z = _nn(y_ref[...], w_ref[...])
        res = ALPHA * x_ref[...] + gate * z
        mu = jnp.mean(res, axis=-1, keepdims=True)
        rc = res - mu
        rstd = lax.rsqrt(jnp.mean(rc * rc, axis=-1, keepdims=True) + LN_EPS)
        rn = rc * rstd
        err = rn * ln_g + ln_b - t_ref[...]
        d_out = err * (1.0 / dm)
        d_rn = d_out * ln_g
        d_res = rstd * (d_rn - jnp.mean(d_rn, axis=-1, keepdims=True)
                        - rn * jnp.mean(d_rn * rn, axis=-1, keepdims=True))
        acc_ref[OUT_ROW_GATE:OUT_ROW_GATE + 1, :] += jnp.sum(d_res * z, axis=0, keepdims=True)
        acc_ref[OUT_ROW_LN_G:OUT_ROW_LN_G + 1, :] += jnp.sum(d_out * rn, axis=0, keepdims=True)
        acc_ref[OUT_ROW_LN_B:OUT_ROW_LN_B + 1, :] += jnp.sum(d_out, axis=0, keepdims=True)
        acc_ref[OUT_ROW_LOSS:OUT_ROW_LOSS + 1, :] += (0.5 / dm) * jnp.sum(err * err, axis=0, keepdims=True)
        gx_ref[...] = ALPHA * d_res
        dz = (d_res * gate).astype(BF16)
        dz_ref[...] = dz
        dy_ref[...] = _nt(dz, w_ref[...])

    rows_d = pl.BlockSpec((tm, dm), lambda i: (i, 0))
    rows_i = pl.BlockSpec((tm, di), lambda i: (i, 0))
    return pl.pallas_call(
        body, name="out_block", grid=(t_rows // tm,),
        in_specs=[rows_i, pl.BlockSpec((di, dm), lambda i: (0, 0)), rows_d, rows_d,
                  pl.BlockSpec((8, dm), lambda i: (0, 0))],
        out_specs=(rows_d, rows_i, rows_d, pl.BlockSpec((8, dm), lambda i: (0, 0))),
        out_shape=(jax.ShapeDtypeStruct((t_rows, dm), BF16), jax.ShapeDtypeStruct((t_rows, di), F32),
                   jax.ShapeDtypeStruct((t_rows, dm), F32), jax.ShapeDtypeStruct((8, dm), F32)),
        compiler_params=_params(("arbitrary",)),
    )(y, w_out, x, target, prm)


def _mod_fwd(c16, w_mod, tn):
    dm, n = w_mod.shape

    def body(c_ref, w_ref, o_ref, a_ref):
        a = _silu(c_ref[...])
        a_ref[...] = a
        o_ref[...] = _nn(a, w_ref[...], HIGHEST)

    return pl.pallas_call(
        body, name="mod_fwd", grid=(n // tn,),
        in_specs=[pl.BlockSpec((16, dm), lambda j: (0, 0)), pl.BlockSpec((dm, tn), lambda j: (0, j))],
        out_specs=(pl.BlockSpec((16, tn), lambda j: (0, j)), pl.BlockSpec((16, dm), lambda j: (0, 0))),
        out_shape=(jax.ShapeDtypeStruct((16, n), F32), jax.ShapeDtypeStruct((16, dm), F32)),
        compiler_params=_params(("arbitrary",)),
    )(c16, w_mod)


def _mod_bwd(a16, dm16, w_mod, tn):
    dm, n = w_mod.shape

    def body(a_ref, d_ref, w_ref, dw_ref, dc_ref):
        @pl.when(pl.program_id(0) == 0)
        def _():
            dc_ref[...] = jnp.zeros_like(dc_ref)
        dw_ref[...] = _tn(a_ref[...], d_ref[...], HIGHEST)
        dc_ref[...] += _nt(d_ref[...], w_ref[...], HIGHEST)

    return pl.pallas_call(
        body, name="mod_bwd", grid=(n // tn,),
        in_specs=[pl.BlockSpec((16, dm), lambda j: (0, 0)), pl.BlockSpec((16, tn), lambda j: (0, j)),
                  pl.BlockSpec((dm, tn), lambda j: (0, j))],
        out_specs=(pl.BlockSpec((dm, tn), lambda j: (0, j)), pl.BlockSpec((16, dm), lambda j: (0, 0))),
        out_shape=(jax.ShapeDtypeStruct((dm, n), F32), jax.ShapeDtypeStruct((16, dm), F32)),
        compiler_params=_params(("arbitrary",)),
    )(a16, dm16, w_mod)


def _sum_devices(g, fold_rows):
    n_dev, rows, n = g.shape

    def body(g_ref, s_ref, t_ref):
        s = g_ref[0]
        for dev in range(1, n_dev):
            s = s + g_ref[dev]
        t_ref[...] = jnp.broadcast_to(jnp.sum(s, axis=-1, keepdims=True), (rows, LANE))
        s_ref[...] = s
        s_ref[0:fold_rows, :] = s[0:fold_rows] + s[fold_rows:2 * fold_rows]

    return pl.pallas_call(
        body, name="sum_devices",
        out_shape=(jax.ShapeDtypeStruct((rows, n), F32), jax.ShapeDtypeStruct((rows, LANE), F32)),
        compiler_params=_params(),
    )(g)


def _c_ctx_grad(parts, c_ctx_row):
    def body(p_ref, c_ref, o_ref):
        s = p_ref[0]
        for chip in range(1, N_CHIPS):
            s = s + p_ref[2 * chip]
        cv = c_ref[...]
        sg = _sigmoid(cv)
        o_ref[...] = s * (sg * (1.0 + cv * (1.0 - sg)))

    return pl.pallas_call(
        body, name="c_ctx_grad", out_shape=jax.ShapeDtypeStruct(parts.shape[1:], F32), compiler_params=_params(),
    )(parts, c_ctx_row)


def _sum_pair(name, mine, got):
    def body(a_ref, b_ref, o_ref):
        o_ref[...] = (a_ref[...] + b_ref[...]).astype(BF16)

    k, rows, n = mine.shape
    tl = _largest_divisor(n, max(LANE, (1 << 18) // rows), LANE)
    spec = pl.BlockSpec((1, rows, tl), lambda kk, i: (kk, 0, i))
    return pl.pallas_call(
        body, name=name, grid=(k, n // tl), in_specs=[spec, spec], out_specs=spec,
        out_shape=jax.ShapeDtypeStruct(mine.shape, BF16), compiler_params=_params(("parallel", "parallel")),
    )(mine, got)


def _sum_pair_lanes(name, full, got, ci):
    rows, n = got.shape
    tr = _largest_divisor(rows, max(SUBLANE_BF16, (1 << 19) // n), SUBLANE_BF16)

    def body(ci_ref, a_ref, b_ref, o_ref):
        o_ref[...] = (a_ref[...] + b_ref[...].astype(F32)).astype(BF16)

    return pl.pallas_call(
        body, name=name,
        grid_spec=pltpu.PrefetchScalarGridSpec(
            num_scalar_prefetch=1, grid=(rows // tr,),
            in_specs=[pl.BlockSpec((tr, n), lambda i, c: (i, c[0])), pl.BlockSpec((tr, n), lambda i, c: (i, 0))],
            out_specs=pl.BlockSpec((tr, n), lambda i, c: (i, 0))),
        out_shape=jax.ShapeDtypeStruct((rows, n), BF16), compiler_params=_params(("parallel",)),
    )(ci.reshape(1).astype(jnp.int32), full, got)


def _sum_chips(name, got, own, chip):
    k, rows, n = got.shape
    tl = _largest_divisor(n, max(LANE, (1 << 18) // rows), LANE)

    def body(chip_ref, g_ref, own_ref, o_ref):
        total = None
        for kk in range(k):
            term = jnp.where(chip_ref[0] == kk, own_ref[0], g_ref[kk]).astype(F32)
            total = term if total is None else total + term
        o_ref[...] = total

    return pl.pallas_call(
        body, name=name,
        grid_spec=pltpu.PrefetchScalarGridSpec(
            num_scalar_prefetch=1, grid=(n // tl,),
            in_specs=[pl.BlockSpec((k, rows, tl), lambda i, c: (0, 0, i)),
                      pl.BlockSpec((1, rows, tl), lambda i, c: (c[0], 0, i))],
            out_specs=pl.BlockSpec((rows, tl), lambda i, c: (0, i))),
        out_shape=jax.ShapeDtypeStruct((rows, n), F32), compiler_params=_params(("parallel",)),
    )(chip.reshape(1).astype(jnp.int32), got, own)


def _adamw_update(w, g, m, v):
    m2 = ADAM_B1 * m + (1.0 - ADAM_B1) * g
    v2 = ADAM_B2 * v + (1.0 - ADAM_B2) * jnp.square(g)
    m_hat = m2 / (1.0 - ADAM_B1 ** ADAM_STEP)
    v_hat = v2 / (1.0 - ADAM_B2 ** ADAM_STEP)
    return -ADAM_LR * (m_hat / (jnp.sqrt(v_hat) + ADAM_EPS) + ADAM_WD * w), m2, v2


def _adamw(name, w, g, m, v, rider=None):
    rows, n = w.shape
    if rows % 8 == 0:
        tr = _largest_divisor(rows, max(8, (1 << 18) // n), 8)
        block, index, steps = (tr, n), (lambda i: (i, 0)), rows // tr
    else:
        tl = _largest_divisor(n, max(LANE, (1 << 18) // rows), LANE)
        block, index, steps = (rows, tl), (lambda i: (0, i)), n // tl

    def body(w_ref, g_ref, m_ref, v_ref, d_ref, mo_ref, vo_ref):
        d_ref[...], mo_ref[...], vo_ref[...] = _adamw_update(w_ref[...], g_ref[...], m_ref[...], v_ref[...])

    spec = pl.BlockSpec(block, index)
    sds = jax.ShapeDtypeStruct((rows, n), F32)
    return _call(body, (w, g, m, v), name=name, grid=(steps,), in_specs=[spec] * 4, out_specs=(spec,) * 3,
                 out_shape=(sds, sds, sds), sem=("parallel",), rider=rider)


PACK_LANES = 1024


def _pack(pieces):
    flat = jnp.concatenate([p.reshape(-1) for p in pieces])
    total = -(-flat.shape[0] // (8 * PACK_LANES)) * 8 * PACK_LANES
    return jnp.pad(flat, (0, total - flat.shape[0])).reshape(-1, PACK_LANES)


def _unpack(packed, shapes):
    flat = packed.reshape(-1)
    out, off = [], 0
    for shp in shapes:
        size = math.prod(shp)
        out.append(flat[off:off + size].reshape(shp))
        off += size
    return out


def _rows8(rows, width):
    flat = [r.reshape(width) for r in rows] + [jnp.zeros(((8 - len(rows)) * width,), F32)]
    return jnp.concatenate(flat).reshape(8, width)


def kernel(x, c, ctx, c_ctx, w_mod, b_mod, w_in, conv_w, conv_b, hg_lb, ml_gate_b, hg_norm_w, ml_norm_w, w_out, ln_g, ln_b, loss_target, m_c_ctx, m_w_mod, m_b_mod, m_w_in, m_conv_w, m_conv_b, m_hg_lb, m_ml_gate_b, m_hg_norm_w, m_ml_norm_w, m_w_out, m_ln_g, m_ln_b, v_c_ctx, v_w_mod, v_b_mod, v_w_in, v_conv_w, v_conv_b, v_hg_lb, v_ml_gate_b, v_hg_norm_w, v_ml_norm_w, v_w_out, v_ln_g, v_ln_b):
    t_rows, dm = x.shape[1], x.shape[2]
    c_rows = ctx.shape[1]
    w = hg_norm_w.shape[1]
    n_ml = ml_gate_b.shape[-1]
    n_hg = w // HG_DK
    di = 2 * w
    n_in = 10 * w + 4 * n_ml
    ns = w_in.shape[2]
    nm = w_mod.shape[2]
    n_pad = 10 * w + LANE
    r_rows = t_rows + c_rows
    row_gcd = math.gcd(t_rows, c_rows)
    hg_chunk, ml_chunk = math.gcd(HG_CHUNK, row_gcd), math.gcd(ML_CHUNK, row_gcd)
    hg_counts = (t_rows // hg_chunk, c_rows // hg_chunk, hg_chunk)
    ml_counts = (t_rows // ml_chunk, c_rows // ml_chunk, ml_chunk)
    assert ml_norm_w.shape[1] == w and di == dm and N_CHIPS * ns == n_in and N_CHIPS * nm == 3 * dm
    assert w_out.shape[1] * N_CHIPS == di and 4 * n_ml <= LANE and t_rows % GRID_W == 0

    xi, yi, ci = lax.axis_index("x"), lax.axis_index("y"), lax.axis_index("c")
    chip = 2 * xi + yi
    dev = 4 * xi + 2 * yi + ci

    tm = _largest_divisor(math.gcd(t_rows, c_rows), 256, 8)
    tm_mm = _largest_divisor(r_rows, 1088, SUBLANE_BF16)
    tn_mm = LANE * _largest_divisor(n_pad // LANE, 9)
    tn_mod = _largest_divisor(nm, 512, LANE)

    shard_shapes = [(dm,), (2, 2, w // N_CHIPS), (3, 3, di // N_CHIPS)]
    g1 = _all_gather8(_pack([c, hg_lb, conv_w])).run("gather_inputs")[0]
    per_dev = [_unpack(g1[i], shard_shapes) for i in range(N_DEV)]
    c_all = jnp.stack([p[0] for p in per_dev])
    lb_full = jnp.concatenate([per_dev[2 * k][1] for k in range(N_CHIPS)], axis=-1)
    conv_w9 = jnp.concatenate([per_dev[2 * k][2] for k in range(N_CHIPS)], axis=-1).reshape(9, di)

    c16 = jnp.concatenate([c_all, c_ctx[None], jnp.zeros((16 - N_DEV - 1, dm), F32)])
    mod_part, a16 = _mod_fwd(c16, w_mod[0], tn_mod)
    g2 = _all_gather8(mod_part).run("gather_mod")[0]
    mod_all = jnp.concatenate([g2[2 * k] for k in range(N_CHIPS)], axis=1) + b_mod
    mod_x = lax.dynamic_index_in_dim(mod_all, dev, 0, keepdims=False).reshape(3, dm)
    mod_c = mod_all[N_DEV].reshape(3, dm)
    prm = jnp.stack([_rows8(list(mod_x), dm), _rows8(list(mod_c), dm)])

    as_t = lambda a: jnp.transpose(a[0])
    half_in = lax.dynamic_slice_in_dim(as_t(w_in).astype(BF16), ci * (dm // 2), dm // 2, 1)
    half_out = lax.dynamic_slice_in_dim(w_out[0].astype(BF16), ci * (di // (2 * N_CHIPS)), di // (2 * N_CHIPS), 0)
    fetched_in = _own_block(chip, half_in, _all_gather_chips([half_in]).run("gather_w_in")[0])
    sibling_in = _sibling_swap([fetched_in]).run("gather_w_in_pair")[0]
    wt_full = jnp.concatenate([_join_halves(ci, fetched_in[k], sibling_in[k], 1) for k in range(N_CHIPS)]
                              + [jnp.zeros((n_pad - n_in, dm), BF16)])

    hc = _modulate_fwd(x[0], ctx[0], prm, tm)
    u, (got_out,) = _mm_nt("in_proj", hc, wt_full, tm_mm, tn_mm, F32, rider=_all_gather_chips([half_out]))
    fetched_out = _own_block(chip, half_out, got_out)
    (o_f, o_b), hg_saved, (swapped_out,) = _hg_scan_fwd(u, lb_full, w, *hg_counts,
                                                         rider=_sibling_swap([fetched_out]))
    w_out_full = _join_halves(ci, fetched_out, swapped_out, 1).reshape(di, dm)
    qk = _conv_fwd(u, conv_w9, conv_b, t_rows, c_rows, w, LANE)
    gate_b_row = jnp.pad(ml_gate_b.reshape(1, -1), ((0, 0), (0, LANE - 4 * n_ml)))
    (h_f, h_b), ml_saved = _ml_scan_fwd(qk, u, gate_b_row, w, n_ml, *ml_counts)
    y = _post_fwd(o_f, o_b, h_f, h_b, u, hg_norm_w, ml_norm_w, t_rows, w, n_hg, n_ml, tm)
    prm_out = _rows8([mod_x[2], ln_g, ln_b], dm)
    dz, dy, gx_direct, acc_out = _out_block(y, w_out_full, x[0], loss_target[0], prm_out, tm)

    d_w_out = _mm_tn("d_w_out", y, dz, _largest_divisor(di, 1024, LANE),
                     _largest_divisor(t_rows, 1024, SUBLANE_BF16))
    d_w_out4 = d_w_out.reshape(N_CHIPS, 2, di // (2 * N_CHIPS), dm)
    mine_out = lax.dynamic_index_in_dim(d_w_out4, ci, 1, keepdims=False)
    other_out = lax.dynamic_index_in_dim(d_w_out4, 1 - ci, 1, keepdims=False)
    (d_o, d_h, d_az, d_bo, d_bz, d_wa, d_wb), (got_out,) = _post_bwd(
        o_f, o_b, h_f, h_b, u, hg_norm_w, ml_norm_w, dy, t_rows, w, n_hg, n_ml, tm, rider=_sibling_swap([other_out]))
    pair_out = _sum_pair("rs_pair_sum_w_out", mine_out, got_out)
    (d_aq_f, d_aff, d_ai_f, d_lb_f), (d_aq_b, d_afb, d_ai_b, d_lb_b), (landed_out,) = _hg_scan_bwd(
        u, lb_full, hg_saved, d_o, w, *hg_counts, rider=_chip_scatter([pair_out]))
    half_g_out = _sum_chips("rs_chip_sum_w_out", landed_out, pair_out, chip)
    (d_qk_f, d_v_f, d_g_f, d_gb_f), (d_qk_b, d_v_b, d_g_b, d_gb_b), (sibling_out,) = _ml_scan_bwd(
        qk, u, gate_b_row, ml_saved, d_h, w, n_ml, *ml_counts, rider=_sibling_swap([half_g_out]))
    g_w_out = _join_halves(ci, half_g_out, sibling_out, 0)
    d_bqk, d_cw, d_cb = _conv_bwd(u, (d_qk_f, d_qk_b), conv_w9, conv_b, t_rows, c_rows, w, LANE)
    du = _assemble_du([(d_aq_f, d_aq_b), d_aff, d_afb, (d_ai_f, d_ai_b), d_az, d_bqk, (d_v_f, d_v_b), d_bo, d_bz],
                      (d_g_f, d_g_b), n_pad, tm // 2)
    d_wt_in, d_wt_in_bf16 = _mm_tn("d_w_in", du, hc, tn_mm, tm_mm, with_bf16=True)

    lanes_of = lambda core: (slice(None), pl.ds(core * (dm // 2), dm // 2))
    got_in = _Exchange([d_wt_in_bf16], [jax.ShapeDtypeStruct((n_pad, dm // 2), BF16)],
                       [(SIBLING_MASK, 0, lambda s, r: lanes_of(r[2]), 0, None)]).run("rs_pair_w_in")[0]
    pair_half = _sum_pair_lanes("rs_pair_sum_w_in", d_wt_in, got_in, ci)
    pair_in = jnp.stack([pair_half[k * ns:(k + 1) * ns] for k in range(N_CHIPS)])
    d_hc, (landed_in,) = _mm_acc("d_h", du, wt_full, tm_mm, tn_mm, rider=_chip_scatter([pair_in]))
    half_g_in = _sum_chips("rs_chip_sum_w_in", landed_in, pair_in, chip)
    g_wt_in = _join_halves(ci, half_g_in, _sibling_swap([half_g_in]).run("rs_join_w_in")[0], 1)
    (gx, acc_mod), _ = _modulate_bwd(x[0], ctx[0], d_hc, prm, gx_direct, tm)
    grad_x = gx[None]

    zero_row = jnp.zeros((dm,), F32)
    d_gb = jnp.concatenate([d_gb_f[:, 0:n_ml], d_gb_b[:, n_ml:2 * n_ml], d_gb_f[:, 2 * n_ml:3 * n_ml],
                            d_gb_b[:, 3 * n_ml:4 * n_ml], jnp.zeros((1, dm - 4 * n_ml), F32)], axis=1)
    rows = [acc_mod[0, 0], acc_mod[0, 1], acc_out[OUT_ROW_GATE],
            acc_mod[1, 0], acc_mod[1, 1], zero_row]
    rows += list(d_cw) + [d_cb[0], d_lb_f.reshape(dm), d_lb_b.reshape(dm),
                          jnp.concatenate([d_wa[0], d_wb[0]]), acc_out[OUT_ROW_LN_G], acc_out[OUT_ROW_LN_B],
                          acc_out[OUT_ROW_LOSS], d_gb[0], zero_row]
    ROW_CW, ROW_CB, ROW_LB, ROW_NORM, ROW_LN_G, ROW_LN_B, ROW_LOSS, ROW_GB = 6, 15, 16, 18, 19, 20, 21, 22
    delta, new_m, new_v = {}, {}, {}
    small_rows = jnp.concatenate([r.reshape(dm) for r in rows]).reshape(len(rows), dm)
    g3 = _all_gather8(small_rows).run("gather_small_grads")[0]
    sums, totals = _sum_devices(g3, 3)
    loss = totals[ROW_LOSS, 0]
    dm16 = jnp.concatenate([g3[:, 0:3, :].reshape(N_DEV, 3 * dm), sums[3:6].reshape(1, 3 * dm),
                            jnp.zeros((16 - N_DEV - 1, 3 * dm), F32)])
    g_w_mod, dc16 = _mod_bwd(a16, lax.dynamic_slice_in_dim(dm16, chip * nm, nm, 1), w_mod[0], tn_mod)
    g4 = _all_gather8(jnp.pad(dc16[N_DEV:N_DEV + 1], ((0, 7), (0, 0)))).run("gather_c_ctx")[0]
    g_c_ctx = _c_ctx_grad(g4, jnp.broadcast_to(c_ctx[None], (8, dm)))[0]
    res, _ = _adamw("adamw_w_in", as_t(w_in), g_wt_in, as_t(m_w_in), as_t(v_w_in))
    delta["w_in"], new_m["w_in"], new_v["w_in"] = (jnp.transpose(a)[None] for a in res)
    res, _ = _adamw("adamw_w_mod", w_mod[0], g_w_mod, m_w_mod[0], v_w_mod[0])
    delta["w_mod"], new_m["w_mod"], new_v["w_mod"] = (a[None] for a in res)

    chip_cols = lambda a, width: lax.dynamic_slice_in_dim(a, chip * width, width, a.ndim - 1)
    grads = {
        "c_ctx": g_c_ctx,
        "w_mod": g_w_mod[None],
        "b_mod": sums[0:3].reshape(1, 3 * dm),
        "w_in": jnp.transpose(g_wt_in)[None],
        "conv_w": chip_cols(sums[ROW_CW:ROW_CW + 9].reshape(1, 3, 3, di), di // N_CHIPS),
        "conv_b": sums[ROW_CB][None],
        "hg_lb": chip_cols(sums[ROW_LB:ROW_LB + 2].reshape(2, 2, w), w // N_CHIPS),
        "ml_gate_b": sums[ROW_GB, 0:4 * n_ml].reshape(1, 4, n_ml),
        "hg_norm_w": sums[ROW_NORM, 0:w][None],
        "ml_norm_w": sums[ROW_NORM, w:2 * w][None],
        "w_out": g_w_out[None],
        "ln_g": sums[ROW_LN_G][None],
        "ln_b": sums[ROW_LN_B][None],
    }
    weights = dict(c_ctx=c_ctx, w_mod=w_mod, b_mod=b_mod, w_in=w_in, conv_w=conv_w, conv_b=conv_b, hg_lb=hg_lb,
                   ml_gate_b=ml_gate_b, hg_norm_w=hg_norm_w, ml_norm_w=ml_norm_w, w_out=w_out, ln_g=ln_g, ln_b=ln_b)
    mom1 = dict(c_ctx=m_c_ctx, w_mod=m_w_mod, b_mod=m_b_mod, w_in=m_w_in, conv_w=m_conv_w, conv_b=m_conv_b,
                hg_lb=m_hg_lb, ml_gate_b=m_ml_gate_b, hg_norm_w=m_hg_norm_w, ml_norm_w=m_ml_norm_w, w_out=m_w_out,
                ln_g=m_ln_g, ln_b=m_ln_b)
    mom2 = dict(c_ctx=v_c_ctx, w_mod=v_w_mod, b_mod=v_b_mod, w_in=v_w_in, conv_w=v_conv_w, conv_b=v_conv_b,
                hg_lb=v_hg_lb, ml_gate_b=v_ml_gate_b, hg_norm_w=v_hg_norm_w, ml_norm_w=v_ml_norm_w, w_out=v_w_out,
                ln_g=v_ln_g, ln_b=v_ln_b)
    names = list(weights)
    big = ("w_mod", "w_in", "w_out")
    small = [n for n in names if n not in big]

    res, _ = _adamw("adamw_w_out", w_out[0], g_w_out, m_w_out[0], v_w_out[0])
    delta["w_out"], new_m["w_out"], new_v["w_out"] = (a[None] for a in res)
    small_shapes = [weights[n].shape for n in small]
    res, _ = _adamw("adamw_small", *(_pack([src[n] for n in small]) for src in (weights, grads, mom1, mom2)))
    for out, packed in zip((delta, new_m, new_v), res):
        for n, a in zip(small, _unpack(packed, small_shapes)):
            out[n] = a

    return (loss, grad_x, *[grads[n].reshape(weights[n].shape) for n in names], *[delta[n] for n in names],
            *[new_m[n] for n in names], *[new_v[n] for n in names])
```

```python
import functools
import math

import jax
import jax.numpy as jnp
from jax import lax
from jax.experimental import pallas as pl
from jax.experimental.pallas import tpu as pltpu

F32 = jnp.float32
BF16 = jnp.bfloat16
HIGHEST = lax.Precision.HIGHEST
MESH = pl.DeviceIdType.MESH

HG_CHUNK = 64
ML_CHUNK = 256
GRID_W = 64
HG_DK = 128
LANE = 128
SUBLANE_BF16 = 16
ALPHA = 2.0 ** 0.25
LN_EPS = 1e-5
NORM_EPS = 1e-6
ADAM_LR = 0.001
ADAM_B1 = 0.9
ADAM_B2 = 0.999
ADAM_EPS = 1e-08
ADAM_WD = 0.01
ADAM_STEP = 10
VMEM_LIMIT = 56 * 1024 * 1024
N_CHIPS = 4
N_DEV = 8


def _params(sem=None):
    return pltpu.CompilerParams(dimension_semantics=sem, vmem_limit_bytes=VMEM_LIMIT)


def _largest_divisor(n, cap, multiple=1):
    best = None
    for d in range(multiple, min(n, cap) + 1, multiple):
        if n % d == 0:
            best = d
    assert best is not None, (n, cap, multiple)
    return best


def _sigmoid(x):
    return jax.nn.sigmoid(x)


def _silu(x):
    return x * jax.nn.sigmoid(x)


def _dot(a, b, dims, precision=None):
    return lax.dot_general(a, b, (dims, ((), ())), precision=precision, preferred_element_type=F32)


def _nn(a, b, precision=None):
    return _dot(a, b, ((1,), (0,)), precision)


def _nt(a, b, precision=None):
    return _dot(a, b, ((1,), (1,)), precision)


def _tn(a, b, precision=None):
    return _dot(a, b, ((0,), (0,)), precision)


def _narrow(x):
    return x.astype(BF16)


@jax.custom_vjp
def _bnn(a, b):
    return _nn(_narrow(a), _narrow(b))


def _bnn_fwd(a, b):
    an, bn = _narrow(a), _narrow(b)
    return _nn(an, bn), (an, bn)


def _bnn_bwd(res, ct):
    an, bn = res
    ctn = _narrow(ct)
    return _nt(ctn, bn), _tn(an, ctn)


_bnn.defvjp(_bnn_fwd, _bnn_bwd)


@jax.custom_vjp
def _bnt(a, b):
    return _nt(_narrow(a), _narrow(b))


def _bnt_fwd(a, b):
    an, bn = _narrow(a), _narrow(b)
    return _nt(an, bn), (an, bn)


def _bnt_bwd(res, ct):
    an, bn = res
    ctn = _narrow(ct)
    return _nn(ctn, bn), _tn(ctn, an)


_bnt.defvjp(_bnt_fwd, _bnt_bwd)


@jax.custom_vjp
def _btn(a, b):
    return _tn(_narrow(a), _narrow(b))


def _btn_fwd(a, b):
    an, bn = _narrow(a), _narrow(b)
    return _tn(an, bn), (an, bn)


def _btn_bwd(res, ct):
    an, bn = res
    ctn = _narrow(ct)
    return _nt(bn, ctn), _nn(an, ctn)


_btn.defvjp(_btn_fwd, _btn_bwd)


def _visible(n, rev):
    r = lax.broadcasted_iota(jnp.int32, (n, n), 0)
    c = lax.broadcasted_iota(jnp.int32, (n, n), 1)
    return (r <= c) if rev else (r >= c)


def _mask_matmul(mask, x):
    mb = mask.astype(BF16)
    hi = x.astype(BF16)
    lo = (x - hi.astype(F32)).astype(BF16)
    return _nn(mb, hi) + _nn(mb, lo)


@functools.partial(jax.custom_vjp, nondiff_argnums=(1,))
def _cumulative(x, rev):
    return _mask_matmul(_visible(x.shape[0], rev), x)


def _cumulative_fwd(x, rev):
    return _cumulative(x, rev), None


def _cumulative_bwd(rev, _, ct):
    return (_mask_matmul(_visible(ct.shape[0], not rev), ct),)


_cumulative.defvjp(_cumulative_fwd, _cumulative_bwd)


def _hg_chunk(states, aq, af, ai, lb0, lb1, rev):
    n_heads = len(states)
    lb = _sigmoid(lb0 - lb1)
    f = lb + (1.0 - lb) * _sigmoid(af)
    g = jnp.log(f)
    k = 1.0 - f
    q = _silu(aq)
    chunk = aq.shape[0]
    vis = _visible(chunk, rev)
    b = _cumulative(g, rev)
    last = 0 if rev else chunk - 1
    b_end = b[last:last + 1]
    b_mid = b[chunk // 2:chunk // 2 + 1]
    q_inter = q * jnp.exp(b)
    q_intra = q * jnp.exp(b - b_mid)
    k_intra = k * jnp.exp(b_mid - b)
    k_dec = k * jnp.exp(b_end - b)
    e_end = jnp.exp(b_end)
    new_states, outs = [], []
    for h in range(n_heads):
        sl = slice(h * HG_DK, (h + 1) * HG_DK)
        s_t = states[h]
        scores = jnp.where(vis, _nt(q_intra[:, sl], k_intra[:, sl]), 0.0)
        outs.append(_nt(q_inter[:, sl], s_t) + _nn(scores, ai[:, sl]))
        new_states.append(e_end[:, sl] * s_t + _tn(ai[:, sl], k_dec[:, sl]))
    return new_states, jnp.concatenate(outs, axis=1)


def _ml_chunk(state, q, k, v, g, gb, rev, d):
    cms, nvs, mbs = state
    n_heads = len(cms)
    dh = q.shape[1] // n_heads
    ga = g + gb
    log_f_all = jax.nn.log_sigmoid(ga)
    chunk = q.shape[0]
    vis = _visible(chunk, rev)
    b_all = _cumulative(log_f_all, rev)
    last = 0 if rev else chunk - 1
    k = k * (dh ** -0.5)
    new_c, new_n, new_m, outs = [], [], [], []
    for h in range(n_heads):
        ci = d * n_heads + h
        cf = (2 + d) * n_heads + h
        sl = slice(h * dh, (h + 1) * dh)
        qh, kh, vh = q[:, sl], k[:, sl], v[:, sl]
        li = ga[:, ci:ci + 1]
        b = b_all[:, cf:cf + 1]
        m = mbs[h][:, 0:1]
        row = jnp.transpose(li - b)
        log_w = jnp.where(vis, b + row, -jnp.inf)
        m_inter = b + m
        m_t = jnp.maximum(m_inter, jnp.max(log_w, axis=-1, keepdims=True))
        w_inter = jnp.exp(m_inter - m_t)
        w_qk = jnp.exp(log_w - m_t) * _bnt(qh, kh)
        num = w_inter * _bnt(qh, cms[h]) + _bnn(w_qk, vh)
        den = w_inter * jnp.sum(qh * nvs[h], axis=-1, keepdims=True) + jnp.sum(w_qk, axis=-1, keepdims=True)
        outs.append(num / jnp.maximum(jnp.abs(den), jnp.exp(-m_t)))
        m_new = m_t[last:last + 1]
        b_end = b[last:last + 1]
        w_s = jnp.exp(b_end - b + li - m_new)
        decay = jnp.exp(b_end + m - m_new)
        new_c.append(decay * cms[h] + _btn(w_s * vh, kh))
        new_n.append(decay * nvs[h] + jnp.sum(w_s * kh, axis=0, keepdims=True))
        new_m.append(jnp.broadcast_to(m_new, (1, LANE)))
    return (new_c, new_n, new_m), jnp.concatenate(outs, axis=1)


def _post_fn(o_f, o_b, az, h_f, h_b, bo, bz, wa, wb, n_hg, n_ml):
    o = o_f + o_b
    parts = []
    for h in range(n_hg):
        s = o[:, h * HG_DK:(h + 1) * HG_DK]
        parts.append(s * lax.rsqrt(jnp.mean(s * s, axis=-1, keepdims=True) + NORM_EPS))
    y_a = jnp.concatenate(parts, axis=1) * wa * _silu(az)
    hh = h_f + h_b
    dh = hh.shape[1] // n_ml
    parts = []
    for h in range(n_ml):
        s = hh[:, h * dh:(h + 1) * dh]
        mu = jnp.mean(s, axis=-1, keepdims=True)
        sc = s - mu
        parts.append(sc * lax.rsqrt(jnp.mean(sc * sc, axis=-1, keepdims=True) + NORM_EPS))
    y_b = jnp.concatenate(parts, axis=1) * wb * _sigmoid(bo) * _silu(bz)
    return jnp.concatenate([y_a, y_b], axis=1)


def _chip_of(dev):
    return 2 * dev[0] + dev[1]


def _index_of(dev):
    return 4 * dev[0] + 2 * dev[1] + dev[2]


class _Exchange:
    def __init__(self, srcs, out_shapes, transfers, local_copies=()):
        self.srcs, self.out_shapes = list(srcs), list(out_shapes)
        self.transfers, self.local_copies = list(transfers), list(local_copies)

    def scratch(self):
        return [pltpu.SemaphoreType.DMA((len(self.transfers),)), pltpu.SemaphoreType.DMA((len(self.transfers),)),
                pltpu.SemaphoreType.DMA((max(len(self.local_copies), 1),))]

    def copies(self, ins, outs, send_sems, recv_sems, local_sems):
        me = (lax.axis_index("x"), lax.axis_index("y"), lax.axis_index("c"))

        def pick(ref, fn, *who):
            return ref if fn is None else ref.at[fn(*who)]

        sends, recvs, locs = [], [], []
        for t, (mask, si, sfn, di, dfn) in enumerate(self.transfers):
            peer = tuple(1 - p if flip else p for p, flip in zip(me, mask))
            sends.append(pltpu.make_async_remote_copy(
                src_ref=pick(ins[si], sfn, me, peer), dst_ref=pick(outs[di], dfn, me, peer),
                send_sem=send_sems.at[t], recv_sem=recv_sems.at[t], device_id=peer, device_id_type=MESH))
            landing = pick(outs[di], dfn, peer, me)
            recvs.append(pltpu.make_async_remote_copy(
                src_ref=landing, dst_ref=landing,
                send_sem=send_sems.at[t], recv_sem=recv_sems.at[t], device_id=peer, device_id_type=MESH))
        for l, (si, sfn, di, dfn) in enumerate(self.local_copies):
            locs.append(pltpu.make_async_copy(pick(ins[si], sfn, me), pick(outs[di], dfn, me), local_sems.at[l]))

        def start():
            for cp in locs + sends:
                cp.start()

        def wait():
            for cp in recvs:
                cp.wait_recv()
            for cp in sends:
                cp.wait_send()
            for cp in locs:
                cp.wait()

        return start, wait

    def run(self, name):
        n_in, n_out = len(self.srcs), len(self.out_shapes)

        def body(*refs):
            start, wait = self.copies(refs[:n_in], refs[n_in:n_in + n_out], *refs[n_in + n_out:])
            start()
            wait()

        hbm = pl.BlockSpec(memory_space=pltpu.HBM)
        return pl.pallas_call(
            body, name=name, out_shape=tuple(self.out_shapes), in_specs=[hbm] * n_in,
            out_specs=tuple([hbm] * n_out), scratch_shapes=self.scratch(),
        )(*self.srcs)


def _call(body, operands, *, name, grid, in_specs, out_specs, out_shape, scratch_shapes=(), sem=None, rider=None):
    out_specs, out_shape, scratch_shapes = list(out_specs), list(out_shape), list(scratch_shapes)
    if rider is None:
        res = pl.pallas_call(
            body, name=name, grid=grid, in_specs=list(in_specs), out_specs=tuple(out_specs),
            out_shape=tuple(out_shape), scratch_shapes=scratch_shapes, compiler_params=_params(sem),
        )(*operands)
        return list(res), []
    counts = (len(in_specs), len(rider.srcs), len(out_specs), len(rider.out_shapes), len(scratch_shapes), 3)

    def full(*refs):
        groups, pos = [], 0
        for k in counts:
            groups.append(refs[pos:pos + k])
            pos += k
        own_in, ex_in, own_out, ex_out, own_scr, ex_scr = groups
        ids = [pl.program_id(a) for a in range(len(grid))]
        first = functools.reduce(jnp.logical_and, [i == 0 for i in ids])
        last = functools.reduce(jnp.logical_and, [i == g - 1 for i, g in zip(ids, grid)])
        start, wait = rider.copies(ex_in, ex_out, *ex_scr)
        pl.when(first)(start)
        body(*own_in, *own_out, *own_scr)
        pl.when(last)(wait)

    hbm = pl.BlockSpec(memory_space=pltpu.HBM)
    res = pl.pallas_call(
        full, name=name, grid=grid, in_specs=list(in_specs) + [hbm] * counts[1],
        out_specs=tuple(out_specs + [hbm] * counts[3]), out_shape=tuple(out_shape + rider.out_shapes),
        scratch_shapes=scratch_shapes + rider.scratch(), compiler_params=_params(("arbitrary",) * len(grid)),
    )(*operands, *rider.srcs)
    return list(res[:counts[2]]), list(res[counts[2]:])


ALL_MASKS = [(mx, my, mc) for mx in (0, 1) for my in (0, 1) for mc in (0, 1)][1:]
CHIP_MASKS = [(1, 0, 0), (0, 1, 0), (1, 1, 0)]
SIBLING_MASK = (0, 0, 1)


def _all_gather8(v):
    out = jax.ShapeDtypeStruct((N_DEV,) + v.shape, v.dtype)
    slot = lambda sender, receiver: _index_of(sender)
    transfers = [(mask, 0, None, 0, slot) for mask in ALL_MASKS]
    return _Exchange([v], [out], transfers, [(0, None, 0, lambda me: _index_of(me))])


def _all_gather_chips(arrays):
    outs = [jax.ShapeDtypeStruct((N_CHIPS,) + a.shape, a.dtype) for a in arrays]
    slot = lambda sender, receiver: _chip_of(sender)
    return _Exchange(arrays, outs, [(mask, i, None, i, slot) for i in range(len(arrays)) for mask in CHIP_MASKS])


def _sibling_swap(arrays):
    outs = [jax.ShapeDtypeStruct(a.shape, a.dtype) for a in arrays]
    return _Exchange(arrays, outs, [(SIBLING_MASK, i, None, i, None) for i in range(len(arrays))])


def _chip_scatter(arrays):
    outs = [jax.ShapeDtypeStruct(a.shape, a.dtype) for a in arrays]
    transfers = [(mask, i, lambda s, r: _chip_of(r), i, lambda s, r: _chip_of(s))
                 for i in range(len(arrays)) for mask in CHIP_MASKS]
    return _Exchange(arrays, outs, transfers)


def _own_block(chip, own, blocks):
    sel = (lax.broadcasted_iota(jnp.int32, (N_CHIPS,) + (1,) * (blocks.ndim - 1), 0) == chip)
    return jnp.where(sel, own if own.ndim == blocks.ndim else own[None], blocks)


def _join_halves(ci, mine, other, axis):
    return jnp.where(ci == 0, jnp.concatenate([mine, other], axis=axis), jnp.concatenate([other, mine], axis=axis))


def _mm_nt(name, a, b, tm, tn, out_dtype, rider=None):
    m, k = a.shape
    n = b.shape[0]

    def body(a_ref, b_ref, o_ref):
        o_ref[...] = _nt(a_ref[...], b_ref[...]).astype(out_dtype)

    (out,), rode = _call(
        body, (a, b), name=name, grid=(n // tn, m // tm),
        in_specs=[pl.BlockSpec((tm, k), lambda j, i: (i, 0)), pl.BlockSpec((tn, k), lambda j, i: (j, 0))],
        out_specs=[pl.BlockSpec((tm, tn), lambda j, i: (i, j))],
        out_shape=[jax.ShapeDtypeStruct((m, n), out_dtype)], sem=("parallel", "parallel"), rider=rider)
    return out, rode


def _mm_acc(name, a, b, tm, tk, rider=None):
    m, kc = a.shape
    n = b.shape[1]

    def body(a_ref, b_ref, o_ref):
        @pl.when(pl.program_id(1) == 0)
        def _():
            o_ref[...] = jnp.zeros_like(o_ref)
        o_ref[...] += _nn(a_ref[...], b_ref[...])

    (out,), rode = _call(
        body, (a, b), name=name, grid=(m // tm, kc // tk),
        in_specs=[pl.BlockSpec((tm, tk), lambda i, kk: (i, kk)), pl.BlockSpec((tk, n), lambda i, kk: (kk, 0))],
        out_specs=[pl.BlockSpec((tm, n), lambda i, kk: (i, 0))],
        out_shape=[jax.ShapeDtypeStruct((m, n), F32)], sem=("parallel", "arbitrary"), rider=rider)
    return out, rode


def _mm_tn(name, a, b, tm, tk, with_bf16=False):
    kr, m = a.shape
    n = b.shape[1]
    steps_k = kr // tk

    def body(a_ref, b_ref, o_ref, *narrow):
        @pl.when(pl.program_id(1) == 0)
        def _():
            o_ref[...] = jnp.zeros_like(o_ref)
        o_ref[...] += _tn(a_ref[...], b_ref[...])
        if with_bf16:
            @pl.when(pl.program_id(1) == steps_k - 1)
            def _():
                narrow[0][...] = o_ref[...].astype(BF16)

    out_spec = pl.BlockSpec((tm, n), lambda i, kk: (i, 0))
    res = pl.pallas_call(
        body, name=name, grid=(m // tm, steps_k),
        in_specs=[pl.BlockSpec((tk, tm), lambda i, kk: (kk, i)), pl.BlockSpec((tk, n), lambda i, kk: (kk, 0))],
        out_specs=(out_spec,) * (2 if with_bf16 else 1),
        out_shape=(jax.ShapeDtypeStruct((m, n), F32),) + ((jax.ShapeDtypeStruct((m, n), BF16),) if with_bf16 else ()),
        compiler_params=_params(("parallel", "arbitrary")),
    )(a, b)
    return res if with_bf16 else res[0]


def _modulate_fwd(x, ctx, prm, tm):
    t_rows, dm = x.shape
    lat = t_rows // tm
    r = t_rows + ctx.shape[0]

    def body(x_ref, c_ref, p_ref, h_ref):
        xv = jnp.where(pl.program_id(0) >= lat, c_ref[...], x_ref[...])
        mu = jnp.mean(xv, axis=-1, keepdims=True)
        xm = xv - mu
        n = xm * lax.rsqrt(jnp.mean(xm * xm, axis=-1, keepdims=True) + LN_EPS)
        h_ref[...] = (n * (1.0 + p_ref[0, 1:2, :]) + p_ref[0, 0:1, :]).astype(BF16)

    return pl.pallas_call(
        body, name="modulate_fwd", grid=(r // tm,),
        in_specs=[pl.BlockSpec((tm, dm), lambda i: (jnp.minimum(i, lat - 1), 0)),
                  pl.BlockSpec((tm, dm), lambda i: (jnp.maximum(i - lat, 0), 0)),
                  pl.BlockSpec((1, 8, dm), lambda i: ((i >= lat).astype(jnp.int32), 0, 0))],
        out_specs=pl.BlockSpec((tm, dm), lambda i: (i, 0)),
        out_shape=jax.ShapeDtypeStruct((r, dm), BF16),
        compiler_params=_params(("parallel",)),
    )(x, ctx, prm)


def _modulate_bwd(x, ctx, dh, prm, gx_direct, tm, rider=None):
    t_rows, dm = x.shape
    lat, n_ct = t_rows // tm, ctx.shape[0] // tm
    is_ctx = lambda i: i < n_ct
    cls = lambda i: is_ctx(i).astype(jnp.int32)
    lat_tile = lambda i: (jnp.maximum(i - n_ct, 0), 0)

    def body(x_ref, c_ref, dh_ref, p_ref, gd_ref, gx_ref, acc_ref):
        i = pl.program_id(0)

        @pl.when((i == 0) | (i == n_ct))
        def _():
            acc_ref[...] = jnp.zeros_like(acc_ref)

        x = jnp.where(is_ctx(i), c_ref[...], x_ref[...])
        dh_v = dh_ref[...]
        mu = jnp.mean(x, axis=-1, keepdims=True)
        xm = x - mu
        rstd = lax.rsqrt(jnp.mean(xm * xm, axis=-1, keepdims=True) + LN_EPS)
        n = xm * rstd
        acc_ref[0, 0:1, :] += jnp.sum(dh_v, axis=0, keepdims=True)
        acc_ref[0, 1:2, :] += jnp.sum(dh_v * n, axis=0, keepdims=True)
        dn = dh_v * (1.0 + p_ref[0, 1:2, :])
        dx = rstd * (dn - jnp.mean(dn, axis=-1, keepdims=True) - n * jnp.mean(dn * n, axis=-1, keepdims=True))
        gx_ref[...] = dx + gd_ref[...]

    return _call(
        body, (x, ctx, dh, prm, gx_direct), name="modulate_bwd", grid=(n_ct + lat,),
        in_specs=[pl.BlockSpec((tm, dm), lat_tile),
                  pl.BlockSpec((tm, dm), lambda i: (jnp.minimum(i, n_ct - 1), 0)),
                  pl.BlockSpec((tm, dm), lambda i: (jnp.where(is_ctx(i), lat + i, i - n_ct), 0)),
                  pl.BlockSpec((1, 8, dm), lambda i: (cls(i), 0, 0)),
                  pl.BlockSpec((tm, dm), lat_tile)],
        out_specs=(pl.BlockSpec((tm, dm), lat_tile), pl.BlockSpec((1, 8, dm), lambda i: (cls(i), 0, 0))),
        out_shape=(jax.ShapeDtypeStruct((t_rows, dm), F32), jax.ShapeDtypeStruct((2, 8, dm), F32)),
        sem=("arbitrary",), rider=rider)


def _conv_parts(t_rows, c_rows):
    return ((0, t_rows, t_rows // GRID_W, GRID_W), (t_rows, c_rows, 1, c_rows))


def _col_shifts(x2, rows_g, width_g):
    n, ct = x2.shape
    col = lax.broadcasted_iota(jnp.int32, (width_g, ct), 0)
    as_grid = lambda a: a.reshape(rows_g, width_g, ct)
    left = as_grid(pltpu.roll(x2, 1, 0)) * (col >= 1).astype(F32)
    right = as_grid(pltpu.roll(x2, n - 1, 0)) * (col <= width_g - 2).astype(F32)
    return [left, as_grid(x2), right]


def _row_shift(y3, a):
    if a == 1:
        return y3
    if y3.shape[0] == 1:
        return jnp.zeros_like(y3)
    zero = jnp.zeros_like(y3[:1])
    return jnp.concatenate([zero, y3[:-1]], axis=0) if a == 0 else jnp.concatenate([y3[1:], zero], axis=0)


def _conv_taps(cols, w_ref, flip):
    rows_g = cols[0].shape[0]
    acc = None
    for a in range(3):
        if rows_g == 1 and a != 1:
            continue
        inner = None
        for b in range(3):
            tap = (2 - a) * 3 + (2 - b) if flip else a * 3 + b
            term = cols[b] * w_ref[tap:tap + 1, :]
            inner = term if inner is None else inner + term
        inner = _row_shift(inner, a)
        acc = inner if acc is None else acc + inner
    return acc


def _conv_fwd(u, conv_w9, conv_b, t_rows, c_rows, w, ct):
    r = u.shape[0]
    base = 5 * w // ct

    def body(x_ref, w_ref, b_ref, o_ref):
        for r0, n, rows_g, width_g in _conv_parts(t_rows, c_rows):
            pre = _conv_taps(_col_shifts(x_ref[r0:r0 + n, :], rows_g, width_g), w_ref, False) + b_ref[...]
            o_ref[r0:r0 + n, :] = _silu(pre).reshape(n, ct)

    return pl.pallas_call(
        body, name="conv_fwd", grid=(2 * w // ct,),
        in_specs=[pl.BlockSpec((r, ct), lambda i: (0, base + i)), pl.BlockSpec((9, ct), lambda i: (0, i)),
                  pl.BlockSpec((1, ct), lambda i: (0, i))],
        out_specs=pl.BlockSpec((r, ct), lambda i: (0, i)),
        out_shape=jax.ShapeDtypeStruct((r, 2 * w), F32),
        compiler_params=_params(("parallel",)),
    )(u, conv_w9, conv_b)


def _conv_bwd(u, dqk_pair, conv_w9, conv_b, t_rows, c_rows, w, ct):
    r = u.shape[0]
    base = 5 * w // ct

    def body(x_ref, d1_ref, d2_ref, w_ref, b_ref, dx_ref, dw_ref, db_ref):
        dw = [jnp.zeros((1, ct), F32) for _ in range(9)]
        db = jnp.zeros((1, ct), F32)
        for r0, n, rows_g, width_g in _conv_parts(t_rows, c_rows):
            cols = _col_shifts(x_ref[r0:r0 + n, :], rows_g, width_g)
            pre = (_conv_taps(cols, w_ref, False) + b_ref[...]).reshape(n, ct)
            sg = _sigmoid(pre)
            dpre = (d1_ref[r0:r0 + n, :] + d2_ref[r0:r0 + n, :]) * (sg * (1.0 + pre * (1.0 - sg)))
            db = db + jnp.sum(dpre, axis=0, keepdims=True)
            dx_ref[r0:r0 + n, :] = _conv_taps(_col_shifts(dpre, rows_g, width_g), w_ref, True).reshape(n, ct)
            dpre3 = dpre.reshape(rows_g, width_g, ct)
            for a in range(3):
                if rows_g == 1 and a != 1:
                    continue
                moved = _row_shift(dpre3, 2 - a)
                for b in range(3):
                    prod = jnp.sum(cols[b] * moved, axis=0)
                    dw[a * 3 + b] = dw[a * 3 + b] + jnp.sum(prod, axis=0, keepdims=True)
        for tap in range(9):
            dw_ref[tap:tap + 1, :] = dw[tap]
        db_ref[...] = db

    return pl.pallas_call(
        body, name="conv_bwd", grid=(2 * w // ct,),
        in_specs=[pl.BlockSpec((r, ct), lambda i: (0, base + i)), pl.BlockSpec((r, ct), lambda i: (0, i)),
                  pl.BlockSpec((r, ct), lambda i: (0, i)),
                  pl.BlockSpec((9, ct), lambda i: (0, i)), pl.BlockSpec((1, ct), lambda i: (0, i))],
        out_specs=(pl.BlockSpec((r, ct), lambda i: (0, i)), pl.BlockSpec((9, ct), lambda i: (0, i)),
                   pl.BlockSpec((1, ct), lambda i: (0, i))),
        out_shape=(jax.ShapeDtypeStruct((r, 2 * w), F32), jax.ShapeDtypeStruct((9, 2 * w), F32),
                   jax.ShapeDtypeStruct((1, 2 * w), F32)),
        compiler_params=_params(("parallel",)),
    )(u, dqk_pair[0], dqk_pair[1], conv_w9, conv_b)


def _assemble_du(groups, gates, n_pad, tm):
    flat, layout = [], []
    for entry in list(groups) + [gates]:
        parts = entry if isinstance(entry, (tuple, list)) else (entry,)
        layout.append((len(flat), len(parts), parts[0].shape[1]))
        flat += list(parts)
    r = flat[0].shape[0]

    def body(*refs):
        o_ref = refs[-1]
        col = 0
        for first, count, width in layout:
            val = refs[first][...]
            for extra in range(1, count):
                val = val + refs[first + extra][...]
            o_ref[:, col:col + width] = val.astype(BF16)
            col += width
        assert col == n_pad

    return pl.pallas_call(
        body, name="assemble_du", grid=(r // tm,),
        in_specs=[pl.BlockSpec((tm, a.shape[1]), lambda i: (i, 0)) for a in flat],
        out_specs=pl.BlockSpec((tm, n_pad), lambda i: (i, 0)),
        out_shape=jax.ShapeDtypeStruct((r, n_pad), BF16),
        compiler_params=_params(("parallel",)),
    )(*flat)


def _scan_order(n_lat, n_ctx, rev):
    n = n_lat + n_ctx
    if rev:
        return lambda j: n - 1 - j
    return lambda j: (j + n_lat) % n


DIRS = (False, True)


def _hg_scan_fwd(u, lb_full, w, n_lat, n_ctx, chunk, rider=None):
    r = u.shape[0]
    n_heads = w // HG_DK
    n_chunks = n_lat + n_ctx
    nat = [_scan_order(n_lat, n_ctx, rev) for rev in DIRS]

    def body(*refs):
        ins, outs, scratch = refs[:8], refs[8:12], refs[12:]

        @pl.when(pl.program_id(0) == 0)
        def _():
            for s_ref in scratch:
                s_ref[...] = jnp.zeros_like(s_ref)

        results = []
        for d, rev in enumerate(DIRS):
            aq, af, ai, lb_ref = ins[4 * d:4 * d + 4]
            state = [scratch[d][h] for h in range(n_heads)]
            results.append((state, _hg_chunk(state, aq[...], af[...], ai[...],
                                             lb_ref[0, 0:1, :], lb_ref[0, 1:2, :], rev)))
        for d, (state, (new, o)) in enumerate(results):
            o_ref, save_ref = outs[2 * d:2 * d + 2]
            o_ref[...] = o
            for h in range(n_heads):
                save_ref[0, h] = state[h]
                scratch[d][h] = new[h]

    in_specs, out_specs, out_shape = [], [], []
    for d in range(2):
        in_specs += [pl.BlockSpec((chunk,w), lambda j, d=d: (nat[d](j), 0)),
                     pl.BlockSpec((chunk,w), lambda j, d=d: (nat[d](j), 1 + d)),
                     pl.BlockSpec((chunk,w), lambda j, d=d: (nat[d](j), 3)),
                     pl.BlockSpec((1, 2, w), lambda j, d=d: (d, 0, 0))]
        out_specs += [pl.BlockSpec((chunk,w), lambda j, d=d: (nat[d](j), 0)),
                      pl.BlockSpec((1, n_heads, HG_DK, HG_DK), lambda j: (j, 0, 0, 0))]
        out_shape += [jax.ShapeDtypeStruct((r, w), F32),
                      jax.ShapeDtypeStruct((n_chunks, n_heads, HG_DK, HG_DK), F32)]
    (o_f, s_f, o_b, s_b), rode = _call(
        body, (u, u, u, lb_full, u, u, u, lb_full), name="hg_scan_fwd", grid=(n_chunks,), in_specs=in_specs,
        out_specs=out_specs, out_shape=out_shape, scratch_shapes=[pltpu.VMEM((n_heads, HG_DK, HG_DK), F32)] * 2,
        sem=("arbitrary",), rider=rider)
    return (o_f, o_b), (s_f, s_b), rode


def _hg_scan_bwd(u, lb_full, saved, d_o, w, n_lat, n_ctx, chunk, rider=None):
    r = u.shape[0]
    n_heads = w // HG_DK
    n_chunks = n_lat + n_ctx
    step = lambda jj: n_chunks - 1 - jj
    nat = [(lambda jj, o=_scan_order(n_lat, n_ctx, rev): o(step(jj))) for rev in DIRS]

    def body(*refs):
        ins, outs, scratch = refs[:12], refs[12:20], refs[20:]
        jj = pl.program_id(0)

        @pl.when(jj == 0)
        def _():
            for d in range(2):
                scratch[d][...] = jnp.zeros_like(scratch[d])
                outs[4 * d + 3][...] = jnp.zeros_like(outs[4 * d + 3])

        results = []
        for d, rev in enumerate(DIRS):
            aq, af, ai, lb_ref, save_ref, do_ref = ins[6 * d:6 * d + 6]
            f = lambda st, a, b, c, l0, l1, rev=rev: _hg_chunk(st, a, b, c, l0, l1, rev)
            _, vjp = jax.vjp(f, [save_ref[0, h] for h in range(n_heads)], aq[...], af[...], ai[...],
                             lb_ref[0, 0:1, :], lb_ref[0, 1:2, :])
            d_out = do_ref[...] * (nat[d](jj) < n_lat).astype(F32)
            results.append(vjp(([scratch[d][h] for h in range(n_heads)], d_out)))
        for d, (dst, daq, daf, dai, dl0, dl1) in enumerate(results):
            daq_ref, daf_ref, dai_ref, dlb_ref = outs[4 * d:4 * d + 4]
            for h in range(n_heads):
                scratch[d][h] = dst[h]
            daq_ref[...] = daq
            daf_ref[...] = daf
            dai_ref[...] = dai
            dlb_ref[0:1, :] += dl0
            dlb_ref[1:2, :] += dl1

    in_specs, out_specs, out_shape, operands = [], [], [], []
    for d in range(2):
        row = lambda jj, d=d: (nat[d](jj), 0)
        in_specs += [pl.BlockSpec((chunk,w), row),
                     pl.BlockSpec((chunk,w), lambda jj, d=d: (nat[d](jj), 1 + d)),
                     pl.BlockSpec((chunk,w), lambda jj, d=d: (nat[d](jj), 3)),
                     pl.BlockSpec((1, 2, w), lambda jj, d=d: (d, 0, 0)),
                     pl.BlockSpec((1, n_heads, HG_DK, HG_DK), lambda jj: (step(jj), 0, 0, 0)),
                     pl.BlockSpec((chunk,w), lambda jj, d=d: (jnp.minimum(nat[d](jj), n_lat - 1), 0))]
        operands += [u, u, u, lb_full, saved[d], d_o]
        out_specs += [pl.BlockSpec((chunk,w), row)] * 3 + [pl.BlockSpec((2, w), lambda jj: (0, 0))]
        out_shape += [jax.ShapeDtypeStruct((r, w), F32)] * 3 + [jax.ShapeDtypeStruct((2, w), F32)]
    res, rode = _call(
        body, operands, name="hg_scan_bwd", grid=(n_chunks,), in_specs=in_specs, out_specs=out_specs,
        out_shape=out_shape, scratch_shapes=[pltpu.VMEM((n_heads, HG_DK, HG_DK), F32)] * 2,
        sem=("arbitrary",), rider=rider)
    return res[0:4], res[4:8], rode


def _ml_state_shapes(n_chunks, n_heads, dh):
    return (jax.ShapeDtypeStruct((n_chunks, n_heads, dh, dh), F32),
            jax.ShapeDtypeStruct((n_chunks, n_heads, 1, dh), F32),
            jax.ShapeDtypeStruct((n_chunks, n_heads, 1, LANE), F32))


def _ml_state_specs(n_heads, dh, index):
    return (pl.BlockSpec((1, n_heads, dh, dh), lambda j: (index(j), 0, 0, 0)),
            pl.BlockSpec((1, n_heads, 1, dh), lambda j: (index(j), 0, 0, 0)),
            pl.BlockSpec((1, n_heads, 1, LANE), lambda j: (index(j), 0, 0, 0)))


def _ml_state_scratch(n_heads, dh):
    return [pltpu.VMEM((n_heads, dh, dh), F32), pltpu.VMEM((n_heads, 1, dh), F32), pltpu.VMEM((n_heads, 1, LANE), F32)]


def _ml_scan_fwd(qk, u, gate_b, w, n_heads, n_lat, n_ctx, chunk):
    r = u.shape[0]
    dh = w // n_heads
    n_chunks = n_lat + n_ctx
    nat = [_scan_order(n_lat, n_ctx, rev) for rev in DIRS]

    def body(*refs):
        ins, outs, scratch = refs[:10], refs[10:18], refs[18:]

        @pl.when(pl.program_id(0) == 0)
        def _():
            for s_ref in scratch:
                s_ref[...] = jnp.zeros_like(s_ref)

        results = []
        for d, rev in enumerate(DIRS):
            q, k, v, g, gb = ins[5 * d:5 * d + 5]
            state = tuple([ref[h] for h in range(n_heads)] for ref in scratch[3 * d:3 * d + 3])
            results.append((state, _ml_chunk(state, q[...], k[...], v[...], g[...], gb[...], rev, d)))
        for d, (state, (new, o)) in enumerate(results):
            outs[4 * d][...] = o
            for part in range(3):
                for h in range(n_heads):
                    outs[4 * d + 1 + part][0, h] = state[part][h]
                    scratch[3 * d + part][h] = new[part][h]

    in_specs, out_specs, out_shape = [], [], []
    for d in range(2):
        in_specs += [pl.BlockSpec((chunk,w), lambda j, d=d: (nat[d](j), 0)),
                     pl.BlockSpec((chunk,w), lambda j, d=d: (nat[d](j), 1)),
                     pl.BlockSpec((chunk,w), lambda j, d=d: (nat[d](j), 7)),
                     pl.BlockSpec((chunk,LANE), lambda j, d=d: (nat[d](j), 10 * w // LANE)),
                     pl.BlockSpec((1, LANE), lambda j: (0, 0))]
        out_specs += [pl.BlockSpec((chunk,w), lambda j, d=d: (nat[d](j), 0))]
        out_specs += list(_ml_state_specs(n_heads, dh, lambda j: j))
        out_shape += [jax.ShapeDtypeStruct((r, w), F32)] + list(_ml_state_shapes(n_chunks, n_heads, dh))
    res = pl.pallas_call(
        body, name="ml_scan_fwd", grid=(n_chunks,), in_specs=in_specs, out_specs=tuple(out_specs),
        out_shape=tuple(out_shape), scratch_shapes=_ml_state_scratch(n_heads, dh) * 2,
        compiler_params=_params(("arbitrary",)),
    )(qk, qk, u, u, gate_b, qk, qk, u, u, gate_b)
    return (res[0], res[4]), (res[1:4], res[5:8])


def _ml_scan_bwd(qk, u, gate_b, saved, d_h, w, n_heads, n_lat, n_ctx, chunk, rider=None):
    r = u.shape[0]
    dh = w // n_heads
    n_chunks = n_lat + n_ctx
    step = lambda jj: n_chunks - 1 - jj
    nat = [(lambda jj, o=_scan_order(n_lat, n_ctx, rev): o(step(jj))) for rev in DIRS]

    def body(*refs):
        ins, outs, scratch = refs[:18], refs[18:26], refs[26:]
        jj = pl.program_id(0)

        @pl.when(jj == 0)
        def _():
            for s_ref in scratch:
                s_ref[...] = jnp.zeros_like(s_ref)
            for d in range(2):
                outs[4 * d + 3][...] = jnp.zeros_like(outs[4 * d + 3])

        results = []
        for d, rev in enumerate(DIRS):
            q, k, v, g, gb, sc, sn, sm, dh_ref = ins[9 * d:9 * d + 9]
            state = tuple([ref[0, h] for h in range(n_heads)] for ref in (sc, sn, sm))
            f = lambda st, a, b, c, gg, bb, rev=rev, d=d: _ml_chunk(st, a, b, c, gg, bb, rev, d)
            _, vjp = jax.vjp(f, state, q[...], k[...], v[...], g[...], gb[...])
            d_state = tuple([ref[h] for h in range(n_heads)] for ref in scratch[3 * d:3 * d + 3])
            d_out = dh_ref[...] * (nat[d](jj) < n_lat).astype(F32)
            results.append(vjp((d_state, d_out)))
        for d, (d_state, dq, dk, dv, dg, dgb) in enumerate(results):
            dqk_ref, dv_ref, dg_ref, dgb_ref = outs[4 * d:4 * d + 4]
            for part in range(3):
                for h in range(n_heads):
                    scratch[3 * d + part][h] = d_state[part][h]
            dqk_ref[:, 0:w] = dq
            dqk_ref[:, w:2 * w] = dk
            dv_ref[...] = dv
            dg_ref[...] = dg
            dgb_ref[...] += dgb

    in_specs, out_specs, out_shape, operands = [], [], [], []
    for d in range(2):
        row = lambda jj, d=d: (nat[d](jj), 0)
        in_specs += [pl.BlockSpec((chunk,w), row), pl.BlockSpec((chunk,w), lambda jj, d=d: (nat[d](jj), 1)),
                     pl.BlockSpec((chunk,w), lambda jj, d=d: (nat[d](jj), 7)),
                     pl.BlockSpec((chunk,LANE), lambda jj, d=d: (nat[d](jj), 10 * w // LANE)),
                     pl.BlockSpec((1, LANE), lambda jj: (0, 0))]
        in_specs += list(_ml_state_specs(n_heads, dh, step))
        in_specs += [pl.BlockSpec((chunk,w), lambda jj, d=d: (jnp.minimum(nat[d](jj), n_lat - 1), 0))]
        operands += [qk, qk, u, u, gate_b, *saved[d], d_h]
        out_specs += [pl.BlockSpec((chunk,2 * w), row), pl.BlockSpec((chunk,w), row),
                      pl.BlockSpec((chunk,LANE), row), pl.BlockSpec((1, LANE), lambda jj: (0, 0))]
        out_shape += [jax.ShapeDtypeStruct((r, 2 * w), F32), jax.ShapeDtypeStruct((r, w), F32),
                      jax.ShapeDtypeStruct((r, LANE), F32), jax.ShapeDtypeStruct((1, LANE), F32)]
    res, rode = _call(
        body, operands, name="ml_scan_bwd", grid=(n_chunks,), in_specs=in_specs, out_specs=out_specs,
        out_shape=out_shape, scratch_shapes=_ml_state_scratch(n_heads, dh) * 2, sem=("arbitrary",), rider=rider)
    return res[0:4], res[4:8], rode


def _post_specs(w, tm, lat_tiles, cols):
    return [pl.BlockSpec((tm, w), (lambda i, cb=cb: (jnp.minimum(i, lat_tiles - 1), cb))) for cb in cols]


def _post_fwd(o_f, o_b, h_f, h_b, u, wa, wb, t_rows, w, n_hg, n_ml, tm):
    lat_tiles = t_rows // tm

    def body(of, ob, hf, hb, az, bo, bz, wa_ref, wb_ref, y_ref):
        y_ref[...] = _post_fn(of[...], ob[...], az[...], hf[...], hb[...], bo[...], bz[...],
                              wa_ref[...], wb_ref[...], n_hg, n_ml).astype(BF16)

    rows = pl.BlockSpec((tm, w), lambda i: (i, 0))
    vec = pl.BlockSpec((1, w), lambda i: (0, 0))
    return pl.pallas_call(
        body, name="post_fwd", grid=(lat_tiles,),
        in_specs=[rows] * 4 + _post_specs(w, tm, lat_tiles, (4, 8, 9)) + [vec, vec],
        out_specs=pl.BlockSpec((tm, 2 * w), lambda i: (i, 0)),
        out_shape=jax.ShapeDtypeStruct((t_rows, 2 * w), BF16),
        compiler_params=_params(("parallel",)),
    )(o_f, o_b, h_f, h_b, u, u, u, wa, wb)


def _post_bwd(o_f, o_b, h_f, h_b, u, wa, wb, dy, t_rows, w, n_hg, n_ml, tm, rider=None):
    r = u.shape[0]
    lat_tiles = t_rows // tm
    lat = lambda i: (jnp.minimum(i, lat_tiles - 1), 0)

    def body(of, ob, hf, hb, az, bo, bz, wa_ref, wb_ref, dy_ref, do_ref, dh_ref, daz_ref, dbo_ref, dbz_ref,
             dwa_ref, dwb_ref):
        i = pl.program_id(0)

        @pl.when(i == 0)
        def _():
            dwa_ref[...] = jnp.zeros_like(dwa_ref)
            dwb_ref[...] = jnp.zeros_like(dwb_ref)

        @pl.when(i < lat_tiles)
        def _():
            f = functools.partial(_post_fn, n_hg=n_hg, n_ml=n_ml)
            _, vjp = jax.vjp(f, of[...], ob[...], az[...], hf[...], hb[...], bo[...], bz[...], wa_ref[...], wb_ref[...])
            d_of, _, d_az, d_hf, _, d_bo, d_bz, d_wa, d_wb = vjp(dy_ref[...])
            do_ref[...] = d_of
            dh_ref[...] = d_hf
            daz_ref[...] = d_az
            dbo_ref[...] = d_bo
            dbz_ref[...] = d_bz
            dwa_ref[...] += d_wa
            dwb_ref[...] += d_wb

        @pl.when(i >= lat_tiles)
        def _():
            daz_ref[...] = jnp.zeros_like(daz_ref)
            dbo_ref[...] = jnp.zeros_like(dbo_ref)
            dbz_ref[...] = jnp.zeros_like(dbz_ref)

    lat_rows = pl.BlockSpec((tm, w), lat)
    all_rows = pl.BlockSpec((tm, w), lambda i: (i, 0))
    vec = pl.BlockSpec((1, w), lambda i: (0, 0))
    sd_t = jax.ShapeDtypeStruct((t_rows, w), F32)
    sd_r = jax.ShapeDtypeStruct((r, w), F32)
    sd_v = jax.ShapeDtypeStruct((1, w), F32)
    return _call(
        body, (o_f, o_b, h_f, h_b, u, u, u, wa, wb, dy), name="post_bwd", grid=(r // tm,),
        in_specs=[lat_rows] * 4 + _post_specs(w, tm, lat_tiles, (4, 8, 9)) + [vec, vec]
        + [pl.BlockSpec((tm, 2 * w), lat)],
        out_specs=(lat_rows, lat_rows, all_rows, all_rows, all_rows, vec, vec),
        out_shape=(sd_t, sd_t, sd_r, sd_r, sd_r, sd_v, sd_v), sem=("arbitrary",), rider=rider)


OUT_ROW_GATE, OUT_ROW_LN_G, OUT_ROW_LN_B, OUT_ROW_LOSS = 0, 1, 2, 3


def _out_block(y, w_out, x, target, prm, tm):
    t_rows, dm = x.shape
    di = y.shape[1]

    def body(y_ref, w_ref, x_ref, t_ref, p_ref, dz_ref, dy_ref, gx_ref, acc_ref):
        @pl.when(pl.program_id(0) == 0)
        def _():
            acc_ref[...] = jnp.zeros_like(acc_ref)

        gate, ln_g, ln_b = p_ref[0:1, :], p_ref[1:2, :], p_ref[2:3, :]
        z = _nn(y_ref[...], w_ref[...])
        res = ALPHA * x_ref[...] + gate * z
        mu = jnp.mean(res, axis=-1, keepdims=True)
        rc = res - mu
        rstd = lax.rsqrt(jnp.mean(rc * rc, axis=-1, keepdims=True) + LN_EPS)
        rn = rc * rstd
        err = rn * ln_g + ln_b - t_ref[...]
        d_out = err * (1.0 / dm)
        d_rn = d_out * ln_g
        d_res = rstd * (d_rn - jnp.mean(d_rn, axis=-1, keepdims=True)
                        - rn * jnp.mean(d_rn * rn, axis=-1, keepdims=True))
        acc_ref[OUT_ROW_GATE:OUT_ROW_GATE + 1, :] += jnp.sum(d_res * z, axis=0, keepdims=True)
        acc_ref[OUT_ROW_LN_G:OUT_ROW_LN_G + 1, :] += jnp.sum(d_out * rn, axis=0, keepdims=True)
        acc_ref[OUT_ROW_LN_B:OUT_ROW_LN_B + 1, :] += jnp.sum(d_out, axis=0, keepdims=True)
        acc_ref[OUT_ROW_LOSS:OUT_ROW_LOSS + 1, :] += (0.5 / dm) * jnp.sum(err * err, axis=0, keepdims=True)
        gx_ref[...] = ALPHA * d_res
        dz = (d_res * gate).astype(BF16)
        dz_ref[...] = dz
        dy_ref[...] = _nt(dz, w_ref[...])

    rows_d = pl.BlockSpec((tm, dm), lambda i: (i, 0))
    rows_i = pl.BlockSpec((tm, di), lambda i: (i, 0))
    return pl.pallas_call(
        body, name="out_block", grid=(t_rows // tm,),
        in_specs=[rows_i, pl.BlockSpec((di, dm), lambda i: (0, 0)), rows_d, rows_d,
                  pl.BlockSpec((8, dm), lambda i: (0, 0))],
        out_specs=(rows_d, rows_i, rows_d, pl.BlockSpec((8, dm), lambda i: (0, 0))),
        out_shape=(jax.ShapeDtypeStruct((t_rows, dm), BF16), jax.ShapeDtypeStruct((t_rows, di), F32),
                   jax.ShapeDtypeStruct((t_rows, dm), F32), jax.ShapeDtypeStruct((8, dm), F32)),
        compiler_params=_params(("arbitrary",)),
    )(y, w_out, x, target, prm)


def _mod_fwd(c16, w_mod, tn):
    dm, n = w_mod.shape

    def body(c_ref, w_ref, o_ref, a_ref):
        a = _silu(c_ref[...])
        a_ref[...] = a
        o_ref[...] = _nn(a, w_ref[...], HIGHEST)

    return pl.pallas_call(
        body, name="mod_fwd", grid=(n // tn,),
        in_specs=[pl.BlockSpec((16, dm), lambda j: (0, 0)), pl.BlockSpec((dm, tn), lambda j: (0, j))],
        out_specs=(pl.BlockSpec((16, tn), lambda j: (0, j)), pl.BlockSpec((16, dm), lambda j: (0, 0))),
        out_shape=(jax.ShapeDtypeStruct((16, n), F32), jax.ShapeDtypeStruct((16, dm), F32)),
        compiler_params=_params(("arbitrary",)),
    )(c16, w_mod)


def _mod_bwd(a16, dm16, w_mod, tn):
    dm, n = w_mod.shape

    def body(a_ref, d_ref, w_ref, dw_ref, dc_ref):
        @pl.when(pl.program_id(0) == 0)
        def _():
            dc_ref[...] = jnp.zeros_like(dc_ref)
        dw_ref[...] = _tn(a_ref[...], d_ref[...], HIGHEST)
        dc_ref[...] += _nt(d_ref[...], w_ref[...], HIGHEST)

    return pl.pallas_call(
        body, name="mod_bwd", grid=(n // tn,),
        in_specs=[pl.BlockSpec((16, dm), lambda j: (0, 0)), pl.BlockSpec((16, tn), lambda j: (0, j)),
                  pl.BlockSpec((dm, tn), lambda j: (0, j))],
        out_specs=(pl.BlockSpec((dm, tn), lambda j: (0, j)), pl.BlockSpec((16, dm), lambda j: (0, 0))),
        out_shape=(jax.ShapeDtypeStruct((dm, n), F32), jax.ShapeDtypeStruct((16, dm), F32)),
        compiler_params=_params(("arbitrary",)),
    )(a16, dm16, w_mod)


def _sum_devices(g, fold_rows):
    n_dev, rows, n = g.shape

    def body(g_ref, s_ref, t_ref):
        s = g_ref[0]
        for dev in range(1, n_dev):
            s = s + g_ref[dev]
        t_ref[...] = jnp.broadcast_to(jnp.sum(s, axis=-1, keepdims=True), (rows, LANE))
        s_ref[...] = s
        s_ref[0:fold_rows, :] = s[0:fold_rows] + s[fold_rows:2 * fold_rows]

    return pl.pallas_call(
        body, name="sum_devices",
        out_shape=(jax.ShapeDtypeStruct((rows, n), F32), jax.ShapeDtypeStruct((rows, LANE), F32)),
        compiler_params=_params(),
    )(g)


def _c_ctx_grad(parts, c_ctx_row):
    def body(p_ref, c_ref, o_ref):
        s = p_ref[0]
        for chip in range(1, N_CHIPS):
            s = s + p_ref[2 * chip]
        cv = c_ref[...]
        sg = _sigmoid(cv)
        o_ref[...] = s * (sg * (1.0 + cv * (1.0 - sg)))

    return pl.pallas_call(
        body, name="c_ctx_grad", out_shape=jax.ShapeDtypeStruct(parts.shape[1:], F32), compiler_params=_params(),
    )(parts, c_ctx_row)


def _sum_pair(name, mine, got):
    def body(a_ref, b_ref, o_ref):
        o_ref[...] = (a_ref[...] + b_ref[...]).astype(BF16)

    k, rows, n = mine.shape
    tl = _largest_divisor(n, max(LANE, (1 << 18) // rows), LANE)
    spec = pl.BlockSpec((1, rows, tl), lambda kk, i: (kk, 0, i))
    return pl.pallas_call(
        body, name=name, grid=(k, n // tl), in_specs=[spec, spec], out_specs=spec,
        out_shape=jax.ShapeDtypeStruct(mine.shape, BF16), compiler_params=_params(("parallel", "parallel")),
    )(mine, got)


def _sum_pair_lanes(name, full, got, ci):
    rows, n = got.shape
    tr = _largest_divisor(rows, max(SUBLANE_BF16, (1 << 19) // n), SUBLANE_BF16)

    def body(ci_ref, a_ref, b_ref, o_ref):
        o_ref[...] = (a_ref[...] + b_ref[...].astype(F32)).astype(BF16)

    return pl.pallas_call(
        body, name=name,
        grid_spec=pltpu.PrefetchScalarGridSpec(
            num_scalar_prefetch=1, grid=(rows // tr,),
            in_specs=[pl.BlockSpec((tr, n), lambda i, c: (i, c[0])), pl.BlockSpec((tr, n), lambda i, c: (i, 0))],
            out_specs=pl.BlockSpec((tr, n), lambda i, c: (i, 0))),
        out_shape=jax.ShapeDtypeStruct((rows, n), BF16), compiler_params=_params(("parallel",)),
    )(ci.reshape(1).astype(jnp.int32), full, got)


def _sum_chips(name, got, own, chip):
    k, rows, n = got.shape
    tl = _largest_divisor(n, max(LANE, (1 << 18) // rows), LANE)

    def body(chip_ref, g_ref, own_ref, o_ref):
        total = None
        for kk in range(k):
            term = jnp.where(chip_ref[0] == kk, own_ref[0], g_ref[kk]).astype(F32)
            total = term if total is None else total + term
        o_ref[...] = total

    return pl.pallas_call(
        body, name=name,
        grid_spec=pltpu.PrefetchScalarGridSpec(
            num_scalar_prefetch=1, grid=(n // tl,),
            in_specs=[pl.BlockSpec((k, rows, tl), lambda i, c: (0, 0, i)),
                      pl.BlockSpec((1, rows, tl), lambda i, c: (c[0], 0, i))],
            out_specs=pl.BlockSpec((rows, tl), lambda i, c: (0, i))),
        out_shape=jax.ShapeDtypeStruct((rows, n), F32), compiler_params=_params(("parallel",)),
    )(chip.reshape(1).astype(jnp.int32), got, own)


def _adamw_update(w, g, m, v):
    m2 = ADAM_B1 * m + (1.0 - ADAM_B1) * g
    v2 = ADAM_B2 * v + (1.0 - ADAM_B2) * jnp.square(g)
    m_hat = m2 / (1.0 - ADAM_B1 ** ADAM_STEP)
    v_hat = v2 / (1.0 - ADAM_B2 ** ADAM_STEP)
    return -ADAM_LR * (m_hat / (jnp.sqrt(v_hat) + ADAM_EPS) + ADAM_WD * w), m2, v2


def _adamw(name, w, g, m, v, rider=None):
    rows, n = w.shape
    if rows % 8 == 0:
        tr = _largest_divisor(rows, max(8, (1 << 18) // n), 8)
        block, index, steps = (tr, n), (lambda i: (i, 0)), rows // tr
    else:
        tl = _largest_divisor(n, max(LANE, (1 << 18) // rows), LANE)
        block, index, steps = (rows, tl), (lambda i: (0, i)), n // tl

    def body(w_ref, g_ref, m_ref, v_ref, d_ref, mo_ref, vo_ref):
        d_ref[...], mo_ref[...], vo_ref[...] = _adamw_update(w_ref[...], g_ref[...], m_ref[...], v_ref[...])

    spec = pl.BlockSpec(block, index)
    sds = jax.ShapeDtypeStruct((rows, n), F32)
    return _call(body, (w, g, m, v), name=name, grid=(steps,), in_specs=[spec] * 4, out_specs=(spec,) * 3,
                 out_shape=(sds, sds, sds), sem=("parallel",), rider=rider)


PACK_LANES = 1024


def _pack(pieces):
    flat = jnp.concatenate([p.reshape(-1) for p in pieces])
    total = -(-flat.shape[0] // (8 * PACK_LANES)) * 8 * PACK_LANES
    return jnp.pad(flat, (0, total - flat.shape[0])).reshape(-1, PACK_LANES)


def _unpack(packed, shapes):
    flat = packed.reshape(-1)
    out, off = [], 0
    for shp in shapes:
        size = math.prod(shp)
        out.append(flat[off:off + size].reshape(shp))
        off += size
    return out


def _rows8(rows, width):
    flat = [r.reshape(width) for r in rows] + [jnp.zeros(((8 - len(rows)) * width,), F32)]
    return jnp.concatenate(flat).reshape(8, width)


def kernel(x, c, ctx, c_ctx, w_mod, b_mod, w_in, conv_w, conv_b, hg_lb, ml_gate_b, hg_norm_w, ml_norm_w, w_out, ln_g, ln_b, loss_target, m_c_ctx, m_w_mod, m_b_mod, m_w_in, m_conv_w, m_conv_b, m_hg_lb, m_ml_gate_b, m_hg_norm_w, m_ml_norm_w, m_w_out, m_ln_g, m_ln_b, v_c_ctx, v_w_mod, v_b_mod, v_w_in, v_conv_w, v_conv_b, v_hg_lb, v_ml_gate_b, v_hg_norm_w, v_ml_norm_w, v_w_out, v_ln_g, v_ln_b):
    t_rows, dm = x.shape[1], x.shape[2]
    c_rows = ctx.shape[1]
    w = hg_norm_w.shape[1]
    n_ml = ml_gate_b.shape[-1]
    n_hg = w // HG_DK
    di = 2 * w
    n_in = 10 * w + 4 * n_ml
    ns = w_in.shape[2]
    nm = w_mod.shape[2]
    n_pad = 10 * w + LANE
    r_rows = t_rows + c_rows
    row_gcd = math.gcd(t_rows, c_rows)
    hg_chunk, ml_chunk = math.gcd(HG_CHUNK, row_gcd), math.gcd(ML_CHUNK, row_gcd)
    hg_counts = (t_rows // hg_chunk, c_rows // hg_chunk, hg_chunk)
    ml_counts = (t_rows // ml_chunk, c_rows // ml_chunk, ml_chunk)
    assert ml_norm_w.shape[1] == w and di == dm and N_CHIPS * ns == n_in and N_CHIPS * nm == 3 * dm
    assert w_out.shape[1] * N_CHIPS == di and 4 * n_ml <= LANE and t_rows % GRID_W == 0

    xi, yi, ci = lax.axis_index("x"), lax.axis_index("y"), lax.axis_index("c")
    chip = 2 * xi + yi
    dev = 4 * xi + 2 * yi + ci

    tm = _largest_divisor(math.gcd(t_rows, c_rows), 256, 8)
    tm_mm = _largest_divisor(r_rows, 1088, SUBLANE_BF16)
    tn_mm = LANE * _largest_divisor(n_pad // LANE, 9)
    tn_mod = _largest_divisor(nm, 512, LANE)

    shard_shapes = [(dm,), (2, 2, w // N_CHIPS), (3, 3, di // N_CHIPS)]
    g1 = _all_gather8(_pack([c, hg_lb, conv_w])).run("gather_inputs")[0]
    per_dev = [_unpack(g1[i], shard_shapes) for i in range(N_DEV)]
    c_all = jnp.stack([p[0] for p in per_dev])
    lb_full = jnp.concatenate([per_dev[2 * k][1] for k in range(N_CHIPS)], axis=-1)
    conv_w9 = jnp.concatenate([per_dev[2 * k][2] for k in range(N_CHIPS)], axis=-1).reshape(9, di)

    c16 = jnp.concatenate([c_all, c_ctx[None], jnp.zeros((16 - N_DEV - 1, dm), F32)])
    mod_part, a16 = _mod_fwd(c16, w_mod[0], tn_mod)
    g2 = _all_gather8(mod_part).run("gather_mod")[0]
    mod_all = jnp.concatenate([g2[2 * k] for k in range(N_CHIPS)], axis=1) + b_mod
    mod_x = lax.dynamic_index_in_dim(mod_all, dev, 0, keepdims=False).reshape(3, dm)
    mod_c = mod_all[N_DEV].reshape(3, dm)
    prm = jnp.stack([_rows8(list(mod_x), dm), _rows8(list(mod_c), dm)])

    as_t = lambda a: jnp.transpose(a[0])
    half_in = lax.dynamic_slice_in_dim(as_t(w_in).astype(BF16), ci * (dm // 2), dm // 2, 1)
    half_out = lax.dynamic_slice_in_dim(w_out[0].astype(BF16), ci * (di // (2 * N_CHIPS)), di // (2 * N_CHIPS), 0)
    fetched_in = _own_block(chip, half_in, _all_gather_chips([half_in]).run("gather_w_in")[0])
    gw_in = _join_halves(ci, fetched_in, _sibling_swap([fetched_in]).run("gather_w_in_pair")[0], 2)
    wt_full = jnp.concatenate([gw_in.reshape(n_in, dm), jnp.zeros((n_pad - n_in, dm), BF16)])

    hc = _modulate_fwd(x[0], ctx[0], prm, tm)
    u, (got_out,) = _mm_nt("in_proj", hc, wt_full, tm_mm, tn_mm, F32, rider=_all_gather_chips([half_out]))
    fetched_out = _own_block(chip, half_out, got_out)
    (o_f, o_b), hg_saved, (swapped_out,) = _hg_scan_fwd(u, lb_full, w, *hg_counts,
                                                         rider=_sibling_swap([fetched_out]))
    w_out_full = _join_halves(ci, fetched_out, swapped_out, 1).reshape(di, dm)
    qk = _conv_fwd(u, conv_w9, conv_b, t_rows, c_rows, w, LANE)
    gate_b_row = jnp.pad(ml_gate_b.reshape(1, -1), ((0, 0), (0, LANE - 4 * n_ml)))
    (h_f, h_b), ml_saved = _ml_scan_fwd(qk, u, gate_b_row, w, n_ml, *ml_counts)
    y = _post_fwd(o_f, o_b, h_f, h_b, u, hg_norm_w, ml_norm_w, t_rows, w, n_hg, n_ml, tm)
    prm_out = _rows8([mod_x[2], ln_g, ln_b], dm)
    dz, dy, gx_direct, acc_out = _out_block(y, w_out_full, x[0], loss_target[0], prm_out, tm)

    d_w_out = _mm_tn("d_w_out", y, dz, _largest_divisor(di, 1024, LANE),
                     _largest_divisor(t_rows, 1024, SUBLANE_BF16))
    d_w_out4 = d_w_out.reshape(N_CHIPS, 2, di // (2 * N_CHIPS), dm)
    mine_out = lax.dynamic_index_in_dim(d_w_out4, ci, 1, keepdims=False)
    other_out = lax.dynamic_index_in_dim(d_w_out4, 1 - ci, 1, keepdims=False)
    (d_o, d_h, d_az, d_bo, d_bz, d_wa, d_wb), (got_out,) = _post_bwd(
        o_f, o_b, h_f, h_b, u, hg_norm_w, ml_norm_w, dy, t_rows, w, n_hg, n_ml, tm, rider=_sibling_swap([other_out]))
    pair_out = _sum_pair("rs_pair_sum_w_out", mine_out, got_out)
    (d_aq_f, d_aff, d_ai_f, d_lb_f), (d_aq_b, d_afb, d_ai_b, d_lb_b), (landed_out,) = _hg_scan_bwd(
        u, lb_full, hg_saved, d_o, w, *hg_counts, rider=_chip_scatter([pair_out]))
    half_g_out = _sum_chips("rs_chip_sum_w_out", landed_out, pair_out, chip)
    (d_qk_f, d_v_f, d_g_f, d_gb_f), (d_qk_b, d_v_b, d_g_b, d_gb_b), (sibling_out,) = _ml_scan_bwd(
        qk, u, gate_b_row, ml_saved, d_h, w, n_ml, *ml_counts, rider=_sibling_swap([half_g_out]))
    g_w_out = _join_halves(ci, half_g_out, sibling_out, 0)
    d_bqk, d_cw, d_cb = _conv_bwd(u, (d_qk_f, d_qk_b), conv_w9, conv_b, t_rows, c_rows, w, LANE)
    du = _assemble_du([(d_aq_f, d_aq_b), d_aff, d_afb, (d_ai_f, d_ai_b), d_az, d_bqk, (d_v_f, d_v_b), d_bo, d_bz],
                      (d_g_f, d_g_b), n_pad, tm // 2)
    d_wt_in, d_wt_in_bf16 = _mm_tn("d_w_in", du, hc, tn_mm, tm_mm, with_bf16=True)

    lanes_of = lambda core: (slice(None), pl.ds(core * (dm // 2), dm // 2))
    got_in = _Exchange([d_wt_in_bf16], [jax.ShapeDtypeStruct((n_pad, dm // 2), BF16)],
                       [(SIBLING_MASK, 0, lambda s, r: lanes_of(r[2]), 0, None)]).run("rs_pair_w_in")[0]
    pair_half = _sum_pair_lanes("rs_pair_sum_w_in", d_wt_in, got_in, ci)
    pair_in = jnp.stack([pair_half[k * ns:(k + 1) * ns] for k in range(N_CHIPS)])
    d_hc, (landed_in,) = _mm_acc("d_h", du, wt_full, tm_mm, tn_mm, rider=_chip_scatter([pair_in]))
    half_g_in = _sum_chips("rs_chip_sum_w_in", landed_in, pair_in, chip)
    g_wt_in = _join_halves(ci, half_g_in, _sibling_swap([half_g_in]).run("rs_join_w_in")[0], 1)
    (gx, acc_mod), _ = _modulate_bwd(x[0], ctx[0], d_hc, prm, gx_direct, tm)
    grad_x = gx[None]

    zero_row = jnp.zeros((dm,), F32)
    d_gb = jnp.concatenate([d_gb_f[:, 0:n_ml], d_gb_b[:, n_ml:2 * n_ml], d_gb_f[:, 2 * n_ml:3 * n_ml],
                            d_gb_b[:, 3 * n_ml:4 * n_ml], jnp.zeros((1, dm - 4 * n_ml), F32)], axis=1)
    rows = [acc_mod[0, 0], acc_mod[0, 1], acc_out[OUT_ROW_GATE],
            acc_mod[1, 0], acc_mod[1, 1], zero_row]
    rows += list(d_cw) + [d_cb[0], d_lb_f.reshape(dm), d_lb_b.reshape(dm),
                          jnp.concatenate([d_wa[0], d_wb[0]]), acc_out[OUT_ROW_LN_G], acc_out[OUT_ROW_LN_B],
                          acc_out[OUT_ROW_LOSS], d_gb[0], zero_row]
    ROW_CW, ROW_CB, ROW_LB, ROW_NORM, ROW_LN_G, ROW_LN_B, ROW_LOSS, ROW_GB = 6, 15, 16, 18, 19, 20, 21, 22
    delta, new_m, new_v = {}, {}, {}
    small_rows = jnp.concatenate([r.reshape(dm) for r in rows]).reshape(len(rows), dm)
    g3 = _all_gather8(small_rows).run("gather_small_grads")[0]
    sums, totals = _sum_devices(g3, 3)
    loss = totals[ROW_LOSS, 0]
    dm16 = jnp.concatenate([g3[:, 0:3, :].reshape(N_DEV, 3 * dm), sums[3:6].reshape(1, 3 * dm),
                            jnp.zeros((16 - N_DEV - 1, 3 * dm), F32)])
    g_w_mod, dc16 = _mod_bwd(a16, lax.dynamic_slice_in_dim(dm16, chip * nm, nm, 1), w_mod[0], tn_mod)
    g4 = _all_gather8(jnp.pad(dc16[N_DEV:N_DEV + 1], ((0, 7), (0, 0)))).run("gather_c_ctx")[0]
    g_c_ctx = _c_ctx_grad(g4, jnp.broadcast_to(c_ctx[None], (8, dm)))[0]
    res, _ = _adamw("adamw_w_in", as_t(w_in), g_wt_in, as_t(m_w_in), as_t(v_w_in))
    delta["w_in"], new_m["w_in"], new_v["w_in"] = (jnp.transpose(a)[None] for a in res)
    res, _ = _adamw("adamw_w_mod", w_mod[0], g_w_mod, m_w_mod[0], v_w_mod[0])
    delta["w_mod"], new_m["w_mod"], new_v["w_mod"] = (a[None] for a in res)

    chip_cols = lambda a, width: lax.dynamic_slice_in_dim(a, chip * width, width, a.ndim - 1)
    grads = {
        "c_ctx": g_c_ctx,
        "w_mod": g_w_mod[None],
        "b_mod": sums[0:3].reshape(1, 3 * dm),
        "w_in": jnp.transpose(g_wt_in)[None],
        "conv_w": chip_cols(sums[ROW_CW:ROW_CW + 9].reshape(1, 3, 3, di), di // N_CHIPS),
        "conv_b": sums[ROW_CB][None],
        "hg_lb": chip_cols(sums[ROW_LB:ROW_LB + 2].reshape(2, 2, w), w // N_CHIPS),
        "ml_gate_b": sums[ROW_GB, 0:4 * n_ml].reshape(1, 4, n_ml),
        "hg_norm_w": sums[ROW_NORM, 0:w][None],
        "ml_norm_w": sums[ROW_NORM, w:2 * w][None],
        "w_out": g_w_out[None],
        "ln_g": sums[ROW_LN_G][None],
        "ln_b": sums[ROW_LN_B][None],
    }
    weights = dict(c_ctx=c_ctx, w_mod=w_mod, b_mod=b_mod, w_in=w_in, conv_w=conv_w, conv_b=conv_b, hg_lb=hg_lb,
                   ml_gate_b=ml_gate_b, hg_norm_w=hg_norm_w, ml_norm_w=ml_norm_w, w_out=w_out, ln_g=ln_g, ln_b=ln_b)
    mom1 = dict(c_ctx=m_c_ctx, w_mod=m_w_mod, b_mod=m_b_mod, w_in=m_w_in, conv_w=m_conv_w, conv_b=m_conv_b,
                hg_lb=m_hg_lb, ml_gate_b=m_ml_gate_b, hg_norm_w=m_hg_norm_w, ml_norm_w=m_ml_norm_w, w_out=m_w_out,
                ln_g=m_ln_g, ln_b=m_ln_b)
    mom2 = dict(c_ctx=v_c_ctx, w_mod=v_w_mod, b_mod=v_b_mod, w_in=v_w_in, conv_w=v_conv_w, conv_b=v_conv_b,
                hg_lb=v_hg_lb, ml_gate_b=v_ml_gate_b, hg_norm_w=v_hg_norm_w, ml_norm_w=v_ml_norm_w, w_out=v_w_out,
                ln_g=v_ln_g, ln_b=v_ln_b)
    names = list(weights)
    big = ("w_mod", "w_in", "w_out")
    small = [n for n in names if n not in big]

    res, _ = _adamw("adamw_w_out", w_out[0], g_w_out, m_w_out[0], v_w_out[0])
    delta["w_out"], new_m["w_out"], new_v["w_out"] = (a[None] for a in res)
    small_shapes = [weights[n].shape for n in small]
    res, _ = _adamw("adamw_small", *(_pack([src[n] for n in small]) for src in (weights, grads, mom1, mom2)))
    for out, packed in zip((delta, new_m, new_v), res):
        for n, a in zip(small, _unpack(packed, small_shapes)):
            out[n] = a

    return (loss, grad_x, *[grads[n].reshape(weights[n].shape) for n in names], *[delta[n] for n in names],
            *[new_m[n] for n in names], *[new_v[n] for n in names])
```

```python
import functools
import math

import jax
import jax.numpy as jnp
from jax import lax
from jax.experimental import pallas as pl
from jax.experimental.pallas import tpu as pltpu

F32 = jnp.float32
BF16 = jnp.bfloat16
HIGHEST = lax.Precision.HIGHEST
MESH = pl.DeviceIdType.MESH

HG_CHUNK = 64
ML_CHUNK = 256
GRID_W = 64
HG_DK = 128
LANE = 128
SUBLANE_BF16 = 16
ALPHA = 2.0 ** 0.25
LN_EPS = 1e-5
NORM_EPS = 1e-6
ADAM_LR = 0.001
ADAM_B1 = 0.9
ADAM_B2 = 0.999
ADAM_EPS = 1e-08
ADAM_WD = 0.01
ADAM_STEP = 10
VMEM_LIMIT = 56 * 1024 * 1024
N_CHIPS = 4
N_DEV = 8


def _params(sem=None):
    return pltpu.CompilerParams(dimension_semantics=sem, vmem_limit_bytes=VMEM_LIMIT)


def _largest_divisor(n, cap, multiple=1):
    best = None
    for d in range(multiple, min(n, cap) + 1, multiple):
        if n % d == 0:
            best = d
    assert best is not None, (n, cap, multiple)
    return best


def _sigmoid(x):
    return jax.nn.sigmoid(x)


def _silu(x):
    return x * jax.nn.sigmoid(x)


def _dot(a, b, dims, precision=None):
    return lax.dot_general(a, b, (dims, ((), ())), precision=precision, preferred_element_type=F32)


def _nn(a, b, precision=None):
    return _dot(a, b, ((1,), (0,)), precision)


def _nt(a, b, precision=None):
    return _dot(a, b, ((1,), (1,)), precision)


def _tn(a, b, precision=None):
    return _dot(a, b, ((0,), (0,)), precision)


def _narrow(x):
    return x.astype(BF16)


@jax.custom_vjp
def _bnn(a, b):
    return _nn(_narrow(a), _narrow(b))


def _bnn_fwd(a, b):
    an, bn = _narrow(a), _narrow(b)
    return _nn(an, bn), (an, bn)


def _bnn_bwd(res, ct):
    an, bn = res
    ctn = _narrow(ct)
    return _nt(ctn, bn), _tn(an, ctn)


_bnn.defvjp(_bnn_fwd, _bnn_bwd)


@jax.custom_vjp
def _bnt(a, b):
    return _nt(_narrow(a), _narrow(b))


def _bnt_fwd(a, b):
    an, bn = _narrow(a), _narrow(b)
    return _nt(an, bn), (an, bn)


def _bnt_bwd(res, ct):
    an, bn = res
    ctn = _narrow(ct)
    return _nn(ctn, bn), _tn(ctn, an)


_bnt.defvjp(_bnt_fwd, _bnt_bwd)


@jax.custom_vjp
def _btn(a, b):
    return _tn(_narrow(a), _narrow(b))


def _btn_fwd(a, b):
    an, bn = _narrow(a), _narrow(b)
    return _tn(an, bn), (an, bn)


def _btn_bwd(res, ct):
    an, bn = res
    ctn = _narrow(ct)
    return _nt(bn, ctn), _nn(an, ctn)


_btn.defvjp(_btn_fwd, _btn_bwd)


def _visible(n, rev):
    r = lax.broadcasted_iota(jnp.int32, (n, n), 0)
    c = lax.broadcasted_iota(jnp.int32, (n, n), 1)
    return (r <= c) if rev else (r >= c)


def _mask_matmul(mask, x):
    mb = mask.astype(BF16)
    hi = x.astype(BF16)
    lo = (x - hi.astype(F32)).astype(BF16)
    return _nn(mb, hi) + _nn(mb, lo)


@functools.partial(jax.custom_vjp, nondiff_argnums=(1,))
def _cumulative(x, rev):
    return _mask_matmul(_visible(x.shape[0], rev), x)


def _cumulative_fwd(x, rev):
    return _cumulative(x, rev), None


def _cumulative_bwd(rev, _, ct):
    return (_mask_matmul(_visible(ct.shape[0], not rev), ct),)


_cumulative.defvjp(_cumulative_fwd, _cumulative_bwd)


def _hg_chunk(states, aq, af, ai, lb0, lb1, rev):
    n_heads = len(states)
    lb = _sigmoid(lb0 - lb1)
    f = lb + (1.0 - lb) * _sigmoid(af)
    g = jnp.log(f)
    k = 1.0 - f
    q = _silu(aq)
    chunk = aq.shape[0]
    vis = _visible(chunk, rev)
    b = _cumulative(g, rev)
    last = 0 if rev else chunk - 1
    b_end = b[last:last + 1]
    b_mid = b[chunk // 2:chunk // 2 + 1]
    q_inter = q * jnp.exp(b)
    q_intra = q * jnp.exp(b - b_mid)
    k_intra = k * jnp.exp(b_mid - b)
    k_dec = k * jnp.exp(b_end - b)
    e_end = jnp.exp(b_end)
    new_states, outs = [], []
    for h in range(n_heads):
        sl = slice(h * HG_DK, (h + 1) * HG_DK)
        s_t = states[h]
        scores = jnp.where(vis, _nt(q_intra[:, sl], k_intra[:, sl]), 0.0)
        outs.append(_nt(q_inter[:, sl], s_t) + _nn(scores, ai[:, sl]))
        new_states.append(e_end[:, sl] * s_t + _tn(ai[:, sl], k_dec[:, sl]))
    return new_states, jnp.concatenate(outs, axis=1)


def _ml_chunk(state, q, k, v, g, gb, rev, d):
    cms, nvs, mbs = state
    n_heads = len(cms)
    dh = q.shape[1] // n_heads
    ga = g + gb
    log_f_all = jax.nn.log_sigmoid(ga)
    chunk = q.shape[0]
    vis = _visible(chunk, rev)
    b_all = _cumulative(log_f_all, rev)
    last = 0 if rev else chunk - 1
    k = k * (dh ** -0.5)
    new_c, new_n, new_m, outs = [], [], [], []
    for h in range(n_heads):
        ci = d * n_heads + h
        cf = (2 + d) * n_heads + h
        sl = slice(h * dh, (h + 1) * dh)
        qh, kh, vh = q[:, sl], k[:, sl], v[:, sl]
        li = ga[:, ci:ci + 1]
        b = b_all[:, cf:cf + 1]
        m = mbs[h][:, 0:1]
        row = jnp.transpose(li - b)
        log_w = jnp.where(vis, b + row, -jnp.inf)
        m_inter = b + m
        m_t = jnp.maximum(m_inter, jnp.max(log_w, axis=-1, keepdims=True))
        w_inter = jnp.exp(m_inter - m_t)
        w_qk = jnp.exp(log_w - m_t) * _bnt(qh, kh)
        num = w_inter * _bnt(qh, cms[h]) + _bnn(w_qk, vh)
        den = w_inter * jnp.sum(qh * nvs[h], axis=-1, keepdims=True) + jnp.sum(w_qk, axis=-1, keepdims=True)
        outs.append(num / jnp.maximum(jnp.abs(den), jnp.exp(-m_t)))
        m_new = m_t[last:last + 1]
        b_end = b[last:last + 1]
        w_s = jnp.exp(b_end - b + li - m_new)
        decay = jnp.exp(b_end + m - m_new)
        new_c.append(decay * cms[h] + _btn(w_s * vh, kh))
        new_n.append(decay * nvs[h] + jnp.sum(w_s * kh, axis=0, keepdims=True))
        new_m.append(jnp.broadcast_to(m_new, (1, LANE)))
    return (new_c, new_n, new_m), jnp.concatenate(outs, axis=1)


def _post_fn(o_f, o_b, az, h_f, h_b, bo, bz, wa, wb, n_hg, n_ml):
    o = o_f + o_b
    parts = []
    for h in range(n_hg):
        s = o[:, h * HG_DK:(h + 1) * HG_DK]
        parts.append(s * lax.rsqrt(jnp.mean(s * s, axis=-1, keepdims=True) + NORM_EPS))
    y_a = jnp.concatenate(parts, axis=1) * wa * _silu(az)
    hh = h_f + h_b
    dh = hh.shape[1] // n_ml
    parts = []
    for h in range(n_ml):
        s = hh[:, h * dh:(h + 1) * dh]
        mu = jnp.mean(s, axis=-1, keepdims=True)
        sc = s - mu
        parts.append(sc * lax.rsqrt(jnp.mean(sc * sc, axis=-1, keepdims=True) + NORM_EPS))
    y_b = jnp.concatenate(parts, axis=1) * wb * _sigmoid(bo) * _silu(bz)
    return jnp.concatenate([y_a, y_b], axis=1)


def _chip_of(dev):
    return 2 * dev[0] + dev[1]


def _index_of(dev):
    return 4 * dev[0] + 2 * dev[1] + dev[2]


class _Exchange:
    def __init__(self, srcs, out_shapes, transfers, local_copies=()):
        self.srcs, self.out_shapes = list(srcs), list(out_shapes)
        self.transfers, self.local_copies = list(transfers), list(local_copies)

    def scratch(self):
        return [pltpu.SemaphoreType.DMA((len(self.transfers),)), pltpu.SemaphoreType.DMA((len(self.transfers),)),
                pltpu.SemaphoreType.DMA((max(len(self.local_copies), 1),))]

    def copies(self, ins, outs, send_sems, recv_sems, local_sems):
        me = (lax.axis_index("x"), lax.axis_index("y"), lax.axis_index("c"))

        def pick(ref, fn, *who):
            return ref if fn is None else ref.at[fn(*who)]

        sends, recvs, locs = [], [], []
        for t, (mask, si, sfn, di, dfn) in enumerate(self.transfers):
            peer = tuple(1 - p if flip else p for p, flip in zip(me, mask))
            sends.append(pltpu.make_async_remote_copy(
                src_ref=pick(ins[si], sfn, me, peer), dst_ref=pick(outs[di], dfn, me, peer),
                send_sem=send_sems.at[t], recv_sem=recv_sems.at[t], device_id=peer, device_id_type=MESH))
            landing = pick(outs[di], dfn, peer, me)
            recvs.append(pltpu.make_async_remote_copy(
                src_ref=landing, dst_ref=landing,
                send_sem=send_sems.at[t], recv_sem=recv_sems.at[t], device_id=peer, device_id_type=MESH))
        for l, (si, sfn, di, dfn) in enumerate(self.local_copies):
            locs.append(pltpu.make_async_copy(pick(ins[si], sfn, me), pick(outs[di], dfn, me), local_sems.at[l]))

        def start():
            for cp in locs + sends:
                cp.start()

        def wait():
            for cp in recvs:
                cp.wait_recv()
            for cp in sends:
                cp.wait_send()
            for cp in locs:
                cp.wait()

        return start, wait

    def run(self, name):
        n_in, n_out = len(self.srcs), len(self.out_shapes)

        def body(*refs):
            start, wait = self.copies(refs[:n_in], refs[n_in:n_in + n_out], *refs[n_in + n_out:])
            start()
            wait()

        hbm = pl.BlockSpec(memory_space=pltpu.HBM)
        return pl.pallas_call(
            body, name=name, out_shape=tuple(self.out_shapes), in_specs=[hbm] * n_in,
            out_specs=tuple([hbm] * n_out), scratch_shapes=self.scratch(),
        )(*self.srcs)


def _call(body, operands, *, name, grid, in_specs, out_specs, out_shape, scratch_shapes=(), sem=None, rider=None):
    out_specs, out_shape, scratch_shapes = list(out_specs), list(out_shape), list(scratch_shapes)
    if rider is None:
        res = pl.pallas_call(
            body, name=name, grid=grid, in_specs=list(in_specs), out_specs=tuple(out_specs),
            out_shape=tuple(out_shape), scratch_shapes=scratch_shapes, compiler_params=_params(sem),
        )(*operands)
        return list(res), []
    counts = (len(in_specs), len(rider.srcs), len(out_specs), len(rider.out_shapes), len(scratch_shapes), 3)

    def full(*refs):
        groups, pos = [], 0
        for k in counts:
            groups.append(refs[pos:pos + k])
            pos += k
        own_in, ex_in, own_out, ex_out, own_scr, ex_scr = groups
        ids = [pl.program_id(a) for a in range(len(grid))]
        first = functools.reduce(jnp.logical_and, [i == 0 for i in ids])
        last = functools.reduce(jnp.logical_and, [i == g - 1 for i, g in zip(ids, grid)])
        start, wait = rider.copies(ex_in, ex_out, *ex_scr)
        pl.when(first)(start)
        body(*own_in, *own_out, *own_scr)
        pl.when(last)(wait)

    hbm = pl.BlockSpec(memory_space=pltpu.HBM)
    res = pl.pallas_call(
        full, name=name, grid=grid, in_specs=list(in_specs) + [hbm] * counts[1],
        out_specs=tuple(out_specs + [hbm] * counts[3]), out_shape=tuple(out_shape + rider.out_shapes),
        scratch_shapes=scratch_shapes + rider.scratch(), compiler_params=_params(("arbitrary",) * len(grid)),
    )(*operands, *rider.srcs)
    return list(res[:counts[2]]), list(res[counts[2]:])


ALL_MASKS = [(mx, my, mc) for mx in (0, 1) for my in (0, 1) for mc in (0, 1)][1:]
CHIP_MASKS = [(1, 0, 0), (0, 1, 0), (1, 1, 0)]
SIBLING_MASK = (0, 0, 1)


def _all_gather8(v):
    out = jax.ShapeDtypeStruct((N_DEV,) + v.shape, v.dtype)
    slot = lambda sender, receiver: _index_of(sender)
    transfers = [(mask, 0, None, 0, slot) for mask in ALL_MASKS]
    return _Exchange([v], [out], transfers, [(0, None, 0, lambda me: _index_of(me))])


def _all_gather_chips(arrays):
    outs = [jax.ShapeDtypeStruct((N_CHIPS,) + a.shape, a.dtype) for a in arrays]
    slot = lambda sender, receiver: _chip_of(sender)
    return _Exchange(arrays, outs, [(mask, i, None, i, slot) for i in range(len(arrays)) for mask in CHIP_MASKS])


def _sibling_swap(arrays):
    outs = [jax.ShapeDtypeStruct(a.shape, a.dtype) for a in arrays]
    return _Exchange(arrays, outs, [(SIBLING_MASK, i, None, i, None) for i in range(len(arrays))])


def _chip_scatter(arrays):
    outs = [jax.ShapeDtypeStruct(a.shape, a.dtype) for a in arrays]
    transfers = [(mask, i, lambda s, r: _chip_of(r), i, lambda s, r: _chip_of(s))
                 for i in range(len(arrays)) for mask in CHIP_MASKS]
    return _Exchange(arrays, outs, transfers)


def _own_block(chip, own, blocks):
    sel = (lax.broadcasted_iota(jnp.int32, (N_CHIPS,) + (1,) * (blocks.ndim - 1), 0) == chip)
    return jnp.where(sel, own if own.ndim == blocks.ndim else own[None], blocks)


def _join_halves(ci, mine, other, axis):
    return jnp.where(ci == 0, jnp.concatenate([mine, other], axis=axis), jnp.concatenate([other, mine], axis=axis))


def _mm_nt(name, a, b, tm, tn, out_dtype, rider=None):
    m, k = a.shape
    n = b.shape[0]

    def body(a_ref, b_ref, o_ref):
        o_ref[...] = _nt(a_ref[...], b_ref[...]).astype(out_dtype)

    (out,), rode = _call(
        body, (a, b), name=name, grid=(n // tn, m // tm),
        in_specs=[pl.BlockSpec((tm, k), lambda j, i: (i, 0)), pl.BlockSpec((tn, k), lambda j, i: (j, 0))],
        out_specs=[pl.BlockSpec((tm, tn), lambda j, i: (i, j))],
        out_shape=[jax.ShapeDtypeStruct((m, n), out_dtype)], sem=("parallel", "parallel"), rider=rider)
    return out, rode


def _mm_acc(name, a, b, tm, tk, rider=None):
    m, kc = a.shape
    n = b.shape[1]

    def body(a_ref, b_ref, o_ref):
        @pl.when(pl.program_id(1) == 0)
        def _():
            o_ref[...] = jnp.zeros_like(o_ref)
        o_ref[...] += _nn(a_ref[...], b_ref[...])

    (out,), rode = _call(
        body, (a, b), name=name, grid=(m // tm, kc // tk),
        in_specs=[pl.BlockSpec((tm, tk), lambda i, kk: (i, kk)), pl.BlockSpec((tk, n), lambda i, kk: (kk, 0))],
        out_specs=[pl.BlockSpec((tm, n), lambda i, kk: (i, 0))],
        out_shape=[jax.ShapeDtypeStruct((m, n), F32)], sem=("parallel", "arbitrary"), rider=rider)
    return out, rode


def _mm_tn(name, a, b, tm, tk, with_bf16=False):
    kr, m = a.shape
    n = b.shape[1]
    steps_k = kr // tk

    def body(a_ref, b_ref, o_ref, *narrow):
        @pl.when(pl.program_id(1) == 0)
        def _():
            o_ref[...] = jnp.zeros_like(o_ref)
        o_ref[...] += _tn(a_ref[...], b_ref[...])
        if with_bf16:
            @pl.when(pl.program_id(1) == steps_k - 1)
            def _():
                narrow[0][...] = o_ref[...].astype(BF16)

    out_spec = pl.BlockSpec((tm, n), lambda i, kk: (i, 0))
    res = pl.pallas_call(
        body, name=name, grid=(m // tm, steps_k),
        in_specs=[pl.BlockSpec((tk, tm), lambda i, kk: (kk, i)), pl.BlockSpec((tk, n), lambda i, kk: (kk, 0))],
        out_specs=(out_spec,) * (2 if with_bf16 else 1),
        out_shape=(jax.ShapeDtypeStruct((m, n), F32),) + ((jax.ShapeDtypeStruct((m, n), BF16),) if with_bf16 else ()),
        compiler_params=_params(("parallel", "arbitrary")),
    )(a, b)
    return res if with_bf16 else res[0]


def _modulate_fwd(x, ctx, prm, tm):
    t_rows, dm = x.shape
    lat = t_rows // tm
    r = t_rows + ctx.shape[0]

    def body(x_ref, c_ref, p_ref, h_ref):
        xv = jnp.where(pl.program_id(0) >= lat, c_ref[...], x_ref[...])
        mu = jnp.mean(xv, axis=-1, keepdims=True)
        xm = xv - mu
        n = xm * lax.rsqrt(jnp.mean(xm * xm, axis=-1, keepdims=True) + LN_EPS)
        h_ref[...] = (n * (1.0 + p_ref[0, 1:2, :]) + p_ref[0, 0:1, :]).astype(BF16)

    return pl.pallas_call(
        body, name="modulate_fwd", grid=(r // tm,),
        in_specs=[pl.BlockSpec((tm, dm), lambda i: (jnp.minimum(i, lat - 1), 0)),
                  pl.BlockSpec((tm, dm), lambda i: (jnp.maximum(i - lat, 0), 0)),
                  pl.BlockSpec((1, 8, dm), lambda i: ((i >= lat).astype(jnp.int32), 0, 0))],
        out_specs=pl.BlockSpec((tm, dm), lambda i: (i, 0)),
        out_shape=jax.ShapeDtypeStruct((r, dm), BF16),
        compiler_params=_params(("parallel",)),
    )(x, ctx, prm)


def _modulate_bwd(x, ctx, dh, prm, gx_direct, tm, rider=None):
    t_rows, dm = x.shape
    lat, n_ct = t_rows // tm, ctx.shape[0] // tm
    is_ctx = lambda i: i < n_ct
    cls = lambda i: is_ctx(i).astype(jnp.int32)
    lat_tile = lambda i: (jnp.maximum(i - n_ct, 0), 0)

    def body(x_ref, c_ref, dh_ref, p_ref, gd_ref, gx_ref, acc_ref):
        i = pl.program_id(0)

        @pl.when((i == 0) | (i == n_ct))
        def _():
            acc_ref[...] = jnp.zeros_like(acc_ref)

        x = jnp.where(is_ctx(i), c_ref[...], x_ref[...])
        dh_v = dh_ref[...]
        mu = jnp.mean(x, axis=-1, keepdims=True)
        xm = x - mu
        rstd = lax.rsqrt(jnp.mean(xm * xm, axis=-1, keepdims=True) + LN_EPS)
        n = xm * rstd
        acc_ref[0, 0:1, :] += jnp.sum(dh_v, axis=0, keepdims=True)
        acc_ref[0, 1:2, :] += jnp.sum(dh_v * n, axis=0, keepdims=True)
        dn = dh_v * (1.0 + p_ref[0, 1:2, :])
        dx = rstd * (dn - jnp.mean(dn, axis=-1, keepdims=True) - n * jnp.mean(dn * n, axis=-1, keepdims=True))
        gx_ref[...] = dx + gd_ref[...]

    return _call(
        body, (x, ctx, dh, prm, gx_direct), name="modulate_bwd", grid=(n_ct + lat,),
        in_specs=[pl.BlockSpec((tm, dm), lat_tile),
                  pl.BlockSpec((tm, dm), lambda i: (jnp.minimum(i, n_ct - 1), 0)),
                  pl.BlockSpec((tm, dm), lambda i: (jnp.where(is_ctx(i), lat + i, i - n_ct), 0)),
                  pl.BlockSpec((1, 8, dm), lambda i: (cls(i), 0, 0)),
                  pl.BlockSpec((tm, dm), lat_tile)],
        out_specs=(pl.BlockSpec((tm, dm), lat_tile), pl.BlockSpec((1, 8, dm), lambda i: (cls(i), 0, 0))),
        out_shape=(jax.ShapeDtypeStruct((t_rows, dm), F32), jax.ShapeDtypeStruct((2, 8, dm), F32)),
        sem=("arbitrary",), rider=rider)


def _conv_parts(t_rows, c_rows):
    return ((0, t_rows, t_rows // GRID_W, GRID_W), (t_rows, c_rows, 1, c_rows))


def _col_shifts(x2, rows_g, width_g):
    n, ct = x2.shape
    col = lax.broadcasted_iota(jnp.int32, (width_g, ct), 0)
    as_grid = lambda a: a.reshape(rows_g, width_g, ct)
    left = as_grid(pltpu.roll(x2, 1, 0)) * (col >= 1).astype(F32)
    right = as_grid(pltpu.roll(x2, n - 1, 0)) * (col <= width_g - 2).astype(F32)
    return [left, as_grid(x2), right]


CONV_BLOCK_ROWS = 4


def _conv_blocks(t_rows, c_rows):
    for t0, _, rows_g, width_g in _conv_parts(t_rows, c_rows):
        nb = min(CONV_BLOCK_ROWS, rows_g)
        assert rows_g % nb == 0
        for g0 in range(0, rows_g, nb):
            yield t0, rows_g, width_g, g0, nb


def _slab(ref, t0, rows_g, width_g, g0, nb):
    if rows_g == 1:
        return ref[t0:t0 + width_g, :]
    lo, hi = max(g0 - 1, 0), min(g0 + nb + 1, rows_g)
    parts = [ref[t0 + lo * width_g:t0 + hi * width_g, :]]
    zero = jnp.zeros((width_g, ref.shape[1]), F32)
    if g0 == 0:
        parts.insert(0, zero)
    if g0 + nb == rows_g:
        parts.append(zero)
    return jnp.concatenate(parts, axis=0)


def _conv_taps(cols, w_ref, nb, flip):
    one_row = cols[0].shape[0] == nb
    acc = None
    for a in range(3):
        if one_row and a != 1:
            continue
        for b in range(3):
            tap = (2 - a) * 3 + (2 - b) if flip else a * 3 + b
            term = (cols[b] if one_row else cols[b][a:a + nb]) * w_ref[tap:tap + 1, :]
            acc = term if acc is None else acc + term
    return acc


def _conv_fwd(u, conv_w9, conv_b, t_rows, c_rows, w, ct):
    r = u.shape[0]
    base = 5 * w // ct

    def body(x_ref, w_ref, b_ref, o_ref):
        for t0, rows_g, width_g, g0, nb in _conv_blocks(t_rows, c_rows):
            slab = _slab(x_ref, t0, rows_g, width_g, g0, nb)
            cols = _col_shifts(slab, slab.shape[0] // width_g, width_g)
            pre = _conv_taps(cols, w_ref, nb, False) + b_ref[...]
            o_ref[t0 + g0 * width_g:t0 + (g0 + nb) * width_g, :] = _silu(pre).reshape(nb * width_g, ct)

    return pl.pallas_call(
        body, name="conv_fwd", grid=(2 * w // ct,),
        in_specs=[pl.BlockSpec((r, ct), lambda i: (0, base + i)), pl.BlockSpec((9, ct), lambda i: (0, i)),
                  pl.BlockSpec((1, ct), lambda i: (0, i))],
        out_specs=pl.BlockSpec((r, ct), lambda i: (0, i)),
        out_shape=jax.ShapeDtypeStruct((r, 2 * w), F32),
        compiler_params=_params(("parallel",)),
    )(u, conv_w9, conv_b)


def _conv_bwd(u, dqk_pair, conv_w9, conv_b, t_rows, c_rows, w, ct):
    r = u.shape[0]
    base = 5 * w // ct

    def body(x_ref, d1_ref, d2_ref, w_ref, b_ref, dx_ref, dw_ref, db_ref, dpre_ref):
        dw = [jnp.zeros((1, ct), F32) for _ in range(9)]
        db = jnp.zeros((1, ct), F32)
        for t0, rows_g, width_g, g0, nb in _conv_blocks(t_rows, c_rows):
            rows = slice(t0 + g0 * width_g, t0 + (g0 + nb) * width_g)
            slab = _slab(x_ref, t0, rows_g, width_g, g0, nb)
            cols = _col_shifts(slab, slab.shape[0] // width_g, width_g)
            pre = _conv_taps(cols, w_ref, nb, False) + b_ref[...]
            sg = _sigmoid(pre)
            dpre = (d1_ref[rows, :] + d2_ref[rows, :]).reshape(pre.shape) * (sg * (1.0 + pre * (1.0 - sg)))
            dpre_ref[rows, :] = dpre.reshape(nb * width_g, ct)
            db = db + jnp.sum(jnp.sum(dpre, axis=0), axis=0, keepdims=True)
            for a in range(3):
                if rows_g == 1 and a != 1:
                    continue
                for b in range(3):
                    moved = cols[b] if rows_g == 1 else cols[b][a:a + nb]
                    dw[a * 3 + b] = dw[a * 3 + b] + jnp.sum(jnp.sum(moved * dpre, axis=0), axis=0, keepdims=True)
        for t0, rows_g, width_g, g0, nb in _conv_blocks(t_rows, c_rows):
            slab = _slab(dpre_ref, t0, rows_g, width_g, g0, nb)
            cols = _col_shifts(slab, slab.shape[0] // width_g, width_g)
            dx_ref[t0 + g0 * width_g:t0 + (g0 + nb) * width_g, :] = _conv_taps(cols, w_ref, nb, True).reshape(
                nb * width_g, ct)
        for tap in range(9):
            dw_ref[tap:tap + 1, :] = dw[tap]
        db_ref[...] = db

    return pl.pallas_call(
        body, name="conv_bwd", grid=(2 * w // ct,),
        in_specs=[pl.BlockSpec((r, ct), lambda i: (0, base + i)), pl.BlockSpec((r, ct), lambda i: (0, i)),
                  pl.BlockSpec((r, ct), lambda i: (0, i)),
                  pl.BlockSpec((9, ct), lambda i: (0, i)), pl.BlockSpec((1, ct), lambda i: (0, i))],
        out_specs=(pl.BlockSpec((r, ct), lambda i: (0, i)), pl.BlockSpec((9, ct), lambda i: (0, i)),
                   pl.BlockSpec((1, ct), lambda i: (0, i))),
        out_shape=(jax.ShapeDtypeStruct((r, 2 * w), F32), jax.ShapeDtypeStruct((9, 2 * w), F32),
                   jax.ShapeDtypeStruct((1, 2 * w), F32)),
        scratch_shapes=[pltpu.VMEM((r, ct), F32)],
        compiler_params=_params(("parallel",)),
    )(u, dqk_pair[0], dqk_pair[1], conv_w9, conv_b)


def _assemble_du(groups, gates, n_pad, tm):
    flat, layout = [], []
    for entry in list(groups) + [gates]:
        parts = entry if isinstance(entry, (tuple, list)) else (entry,)
        layout.append((len(flat), len(parts), parts[0].shape[1]))
        flat += list(parts)
    r = flat[0].shape[0]

    def body(*refs):
        o_ref = refs[-1]
        col = 0
        for first, count, width in layout:
            val = refs[first][...]
            for extra in range(1, count):
                val = val + refs[first + extra][...]
            o_ref[:, col:col + width] = val.astype(BF16)
            col += width
        assert col == n_pad

    return pl.pallas_call(
        body, name="assemble_du", grid=(r // tm,),
        in_specs=[pl.BlockSpec((tm, a.shape[1]), lambda i: (i, 0)) for a in flat],
        out_specs=pl.BlockSpec((tm, n_pad), lambda i: (i, 0)),
        out_shape=jax.ShapeDtypeStruct((r, n_pad), BF16),
        compiler_params=_params(("parallel",)),
    )(*flat)


def _scan_order(n_lat, n_ctx, rev):
    n = n_lat + n_ctx
    if rev:
        return lambda j: n - 1 - j
    return lambda j: (j + n_lat) % n


DIRS = (False, True)


def _hg_scan_fwd(u, lb_full, w, n_lat, n_ctx, chunk, rider=None):
    r = u.shape[0]
    n_heads = w // HG_DK
    n_chunks = n_lat + n_ctx
    nat = [_scan_order(n_lat, n_ctx, rev) for rev in DIRS]

    def body(*refs):
        ins, outs, scratch = refs[:8], refs[8:12], refs[12:]

        @pl.when(pl.program_id(0) == 0)
        def _():
            for s_ref in scratch:
                s_ref[...] = jnp.zeros_like(s_ref)

        results = []
        for d, rev in enumerate(DIRS):
            aq, af, ai, lb_ref = ins[4 * d:4 * d + 4]
            state = [scratch[d][h] for h in range(n_heads)]
            results.append((state, _hg_chunk(state, aq[...], af[...], ai[...],
                                             lb_ref[0, 0:1, :], lb_ref[0, 1:2, :], rev)))
        for d, (state, (new, o)) in enumerate(results):
            o_ref, save_ref = outs[2 * d:2 * d + 2]
            o_ref[...] = o
            for h in range(n_heads):
                save_ref[0, h] = state[h]
                scratch[d][h] = new[h]

    in_specs, out_specs, out_shape = [], [], []
    for d in range(2):
        in_specs += [pl.BlockSpec((chunk,w), lambda j, d=d: (nat[d](j), 0)),
                     pl.BlockSpec((chunk,w), lambda j, d=d: (nat[d](j), 1 + d)),
                     pl.BlockSpec((chunk,w), lambda j, d=d: (nat[d](j), 3)),
                     pl.BlockSpec((1, 2, w), lambda j, d=d: (d, 0, 0))]
        out_specs += [pl.BlockSpec((chunk,w), lambda j, d=d: (nat[d](j), 0)),
                      pl.BlockSpec((1, n_heads, HG_DK, HG_DK), lambda j: (j, 0, 0, 0))]
        out_shape += [jax.ShapeDtypeStruct((r, w), F32),
                      jax.ShapeDtypeStruct((n_chunks, n_heads, HG_DK, HG_DK), F32)]
    (o_f, s_f, o_b, s_b), rode = _call(
        body, (u, u, u, lb_full, u, u, u, lb_full), name="hg_scan_fwd", grid=(n_chunks,), in_specs=in_specs,
        out_specs=out_specs, out_shape=out_shape, scratch_shapes=[pltpu.VMEM((n_heads, HG_DK, HG_DK), F32)] * 2,
        sem=("arbitrary",), rider=rider)
    return (o_f, o_b), (s_f, s_b), rode


def _hg_scan_bwd(u, lb_full, saved, d_o, w, n_lat, n_ctx, chunk, rider=None):
    r = u.shape[0]
    n_heads = w // HG_DK
    n_chunks = n_lat + n_ctx
    step = lambda jj: n_chunks - 1 - jj
    nat = [(lambda jj, o=_scan_order(n_lat, n_ctx, rev): o(step(jj))) for rev in DIRS]

    def body(*refs):
        ins, outs, scratch = refs[:12], refs[12:20], refs[20:]
        jj = pl.program_id(0)

        @pl.when(jj == 0)
        def _():
            for d in range(2):
                scratch[d][...] = jnp.zeros_like(scratch[d])
                outs[4 * d + 3][...] = jnp.zeros_like(outs[4 * d + 3])

        results = []
        for d, rev in enumerate(DIRS):
            aq, af, ai, lb_ref, save_ref, do_ref = ins[6 * d:6 * d + 6]
            f = lambda st, a, b, c, l0, l1, rev=rev: _hg_chunk(st, a, b, c, l0, l1, rev)
            _, vjp = jax.vjp(f, [save_ref[0, h] for h in range(n_heads)], aq[...], af[...], ai[...],
                             lb_ref[0, 0:1, :], lb_ref[0, 1:2, :])
            d_out = do_ref[...] * (nat[d](jj) < n_lat).astype(F32)
            results.append(vjp(([scratch[d][h] for h in range(n_heads)], d_out)))
        for d, (dst, daq, daf, dai, dl0, dl1) in enumerate(results):
            daq_ref, daf_ref, dai_ref, dlb_ref = outs[4 * d:4 * d + 4]
            for h in range(n_heads):
                scratch[d][h] = dst[h]
            daq_ref[...] = daq
            daf_ref[...] = daf
            dai_ref[...] = dai
            dlb_ref[0:1, :] += dl0
            dlb_ref[1:2, :] += dl1

    in_specs, out_specs, out_shape, operands = [], [], [], []
    for d in range(2):
        row = lambda jj, d=d: (nat[d](jj), 0)
        in_specs += [pl.BlockSpec((chunk,w), row),
                     pl.BlockSpec((chunk,w), lambda jj, d=d: (nat[d](jj), 1 + d)),
                     pl.BlockSpec((chunk,w), lambda jj, d=d: (nat[d](jj), 3)),
                     pl.BlockSpec((1, 2, w), lambda jj, d=d: (d, 0, 0)),
                     pl.BlockSpec((1, n_heads, HG_DK, HG_DK), lambda jj: (step(jj), 0, 0, 0)),
                     pl.BlockSpec((chunk,w), lambda jj, d=d: (jnp.minimum(nat[d](jj), n_lat - 1), 0))]
        operands += [u, u, u, lb_full, saved[d], d_o]
        out_specs += [pl.BlockSpec((chunk,w), row)] * 3 + [pl.BlockSpec((2, w), lambda jj: (0, 0))]
        out_shape += [jax.ShapeDtypeStruct((r, w), F32)] * 3 + [jax.ShapeDtypeStruct((2, w), F32)]
    res, rode = _call(
        body, operands, name="hg_scan_bwd", grid=(n_chunks,), in_specs=in_specs, out_specs=out_specs,
        out_shape=out_shape, scratch_shapes=[pltpu.VMEM((n_heads, HG_DK, HG_DK), F32)] * 2,
        sem=("arbitrary",), rider=rider)
    return res[0:4], res[4:8], rode


def _ml_state_shapes(n_chunks, n_heads, dh):
    return (jax.ShapeDtypeStruct((n_chunks, n_heads, dh, dh), F32),
            jax.ShapeDtypeStruct((n_chunks, n_heads, 1, dh), F32),
            jax.ShapeDtypeStruct((n_chunks, n_heads, 1, LANE), F32))


def _ml_state_specs(n_heads, dh, index):
    return (pl.BlockSpec((1, n_heads, dh, dh), lambda j: (index(j), 0, 0, 0)),
            pl.BlockSpec((1, n_heads, 1, dh), lambda j: (index(j), 0, 0, 0)),
            pl.BlockSpec((1, n_heads, 1, LANE), lambda j: (index(j), 0, 0, 0)))


def _ml_state_scratch(n_heads, dh):
    return [pltpu.VMEM((n_heads, dh, dh), F32), pltpu.VMEM((n_heads, 1, dh), F32), pltpu.VMEM((n_heads, 1, LANE), F32)]


def _ml_scan_fwd(qk, u, gate_b, w, n_heads, n_lat, n_ctx, chunk):
    r = u.shape[0]
    dh = w // n_heads
    n_chunks = n_lat + n_ctx
    nat = [_scan_order(n_lat, n_ctx, rev) for rev in DIRS]

    def body(*refs):
        ins, outs, scratch = refs[:10], refs[10:18], refs[18:]

        @pl.when(pl.program_id(0) == 0)
        def _():
            for s_ref in scratch:
                s_ref[...] = jnp.zeros_like(s_ref)

        results = []
        for d, rev in enumerate(DIRS):
            q, k, v, g, gb = ins[5 * d:5 * d + 5]
            state = tuple([ref[h] for h in range(n_heads)] for ref in scratch[3 * d:3 * d + 3])
            results.append((state, _ml_chunk(state, q[...], k[...], v[...], g[...], gb[...], rev, d)))
        for d, (state, (new, o)) in enumerate(results):
            outs[4 * d][...] = o
            for part in range(3):
                for h in range(n_heads):
                    outs[4 * d + 1 + part][0, h] = state[part][h]
                    scratch[3 * d + part][h] = new[part][h]

    in_specs, out_specs, out_shape = [], [], []
    for d in range(2):
        in_specs += [pl.BlockSpec((chunk,w), lambda j, d=d: (nat[d](j), 0)),
                     pl.BlockSpec((chunk,w), lambda j, d=d: (nat[d](j), 1)),
                     pl.BlockSpec((chunk,w), lambda j, d=d: (nat[d](j), 7)),
                     pl.BlockSpec((chunk,LANE), lambda j, d=d: (nat[d](j), 10 * w // LANE)),
                     pl.BlockSpec((1, LANE), lambda j: (0, 0))]
        out_specs += [pl.BlockSpec((chunk,w), lambda j, d=d: (nat[d](j), 0))]
        out_specs += list(_ml_state_specs(n_heads, dh, lambda j: j))
        out_shape += [jax.ShapeDtypeStruct((r, w), F32)] + list(_ml_state_shapes(n_chunks, n_heads, dh))
    res = pl.pallas_call(
        body, name="ml_scan_fwd", grid=(n_chunks,), in_specs=in_specs, out_specs=tuple(out_specs),
        out_shape=tuple(out_shape), scratch_shapes=_ml_state_scratch(n_heads, dh) * 2,
        compiler_params=_params(("arbitrary",)),
    )(qk, qk, u, u, gate_b, qk, qk, u, u, gate_b)
    return (res[0], res[4]), (res[1:4], res[5:8])


def _ml_scan_bwd(qk, u, gate_b, saved, d_h, w, n_heads, n_lat, n_ctx, chunk, rider=None):
    r = u.shape[0]
    dh = w // n_heads
    n_chunks = n_lat + n_ctx
    step = lambda jj: n_chunks - 1 - jj
    nat = [(lambda jj, o=_scan_order(n_lat, n_ctx, rev): o(step(jj))) for rev in DIRS]

    def body(*refs):
        ins, outs, scratch = refs[:18], refs[18:26], refs[26:]
        jj = pl.program_id(0)

        @pl.when(jj == 0)
        def _():
            for s_ref in scratch:
                s_ref[...] = jnp.zeros_like(s_ref)
            for d in range(2):
                outs[4 * d + 3][...] = jnp.zeros_like(outs[4 * d + 3])

        results = []
        for d, rev in enumerate(DIRS):
            q, k, v, g, gb, sc, sn, sm, dh_ref = ins[9 * d:9 * d + 9]
            state = tuple([ref[0, h] for h in range(n_heads)] for ref in (sc, sn, sm))
            f = lambda st, a, b, c, gg, bb, rev=rev, d=d: _ml_chunk(st, a, b, c, gg, bb, rev, d)
            _, vjp = jax.vjp(f, state, q[...], k[...], v[...], g[...], gb[...])
            d_state = tuple([ref[h] for h in range(n_heads)] for ref in scratch[3 * d:3 * d + 3])
            d_out = dh_ref[...] * (nat[d](jj) < n_lat).astype(F32)
            results.append(vjp((d_state, d_out)))
        for d, (d_state, dq, dk, dv, dg, dgb) in enumerate(results):
            dqk_ref, dv_ref, dg_ref, dgb_ref = outs[4 * d:4 * d + 4]
            for part in range(3):
                for h in range(n_heads):
                    scratch[3 * d + part][h] = d_state[part][h]
            dqk_ref[:, 0:w] = dq
            dqk_ref[:, w:2 * w] = dk
            dv_ref[...] = dv
            dg_ref[...] = dg
            dgb_ref[...] += dgb

    in_specs, out_specs, out_shape, operands = [], [], [], []
    for d in range(2):
        row = lambda jj, d=d: (nat[d](jj), 0)
        in_specs += [pl.BlockSpec((chunk,w), row), pl.BlockSpec((chunk,w), lambda jj, d=d: (nat[d](jj), 1)),
                     pl.BlockSpec((chunk,w), lambda jj, d=d: (nat[d](jj), 7)),
                     pl.BlockSpec((chunk,LANE), lambda jj, d=d: (nat[d](jj), 10 * w // LANE)),
                     pl.BlockSpec((1, LANE), lambda jj: (0, 0))]
        in_specs += list(_ml_state_specs(n_heads, dh, step))
        in_specs += [pl.BlockSpec((chunk,w), lambda jj, d=d: (jnp.minimum(nat[d](jj), n_lat - 1), 0))]
        operands += [qk, qk, u, u, gate_b, *saved[d], d_h]
        out_specs += [pl.BlockSpec((chunk,2 * w), row), pl.BlockSpec((chunk,w), row),
                      pl.BlockSpec((chunk,LANE), row), pl.BlockSpec((1, LANE), lambda jj: (0, 0))]
        out_shape += [jax.ShapeDtypeStruct((r, 2 * w), F32), jax.ShapeDtypeStruct((r, w), F32),
                      jax.ShapeDtypeStruct((r, LANE), F32), jax.ShapeDtypeStruct((1, LANE), F32)]
    res, rode = _call(
        body, operands, name="ml_scan_bwd", grid=(n_chunks,), in_specs=in_specs, out_specs=out_specs,
        out_shape=out_shape, scratch_shapes=_ml_state_scratch(n_heads, dh) * 2, sem=("arbitrary",), rider=rider)
    return res[0:4], res[4:8], rode


def _post_specs(w, tm, lat_tiles, cols):
    return [pl.BlockSpec((tm, w), (lambda i, cb=cb: (jnp.minimum(i, lat_tiles - 1), cb))) for cb in cols]


def _post_fwd(o_f, o_b, h_f, h_b, u, wa, wb, t_rows, w, n_hg, n_ml, tm):
    lat_tiles = t_rows // tm

    def body(of, ob, hf, hb, az, bo, bz, wa_ref, wb_ref, y_ref):
        y_ref[...] = _post_fn(of[...], ob[...], az[...], hf[...], hb[...], bo[...], bz[...],
                              wa_ref[...], wb_ref[...], n_hg, n_ml).astype(BF16)

    rows = pl.BlockSpec((tm, w), lambda i: (i, 0))
    vec = pl.BlockSpec((1, w), lambda i: (0, 0))
    return pl.pallas_call(
        body, name="post_fwd", grid=(lat_tiles,),
        in_specs=[rows] * 4 + _post_specs(w, tm, lat_tiles, (4, 8, 9)) + [vec, vec],
        out_specs=pl.BlockSpec((tm, 2 * w), lambda i: (i, 0)),
        out_shape=jax.ShapeDtypeStruct((t_rows, 2 * w), BF16),
        compiler_params=_params(("parallel",)),
    )(o_f, o_b, h_f, h_b, u, u, u, wa, wb)


def _post_bwd(o_f, o_b, h_f, h_b, u, wa, wb, dy, t_rows, w, n_hg, n_ml, tm, rider=None):
    r = u.shape[0]
    lat_tiles = t_rows // tm
    lat = lambda i: (jnp.minimum(i, lat_tiles - 1), 0)

    def body(of, ob, hf, hb, az, bo, bz, wa_ref, wb_ref, dy_ref, do_ref, dh_ref, daz_ref, dbo_ref, dbz_ref,
             dwa_ref, dwb_ref):
        i = pl.program_id(0)

        @pl.when(i == 0)
        def _():
            dwa_ref[...] = jnp.zeros_like(dwa_ref)
            dwb_ref[...] = jnp.zeros_like(dwb_ref)

        @pl.when(i < lat_tiles)
        def _():
            f = functools.partial(_post_fn, n_hg=n_hg, n_ml=n_ml)
            _, vjp = jax.vjp(f, of[...], ob[...], az[...], hf[...], hb[...], bo[...], bz[...], wa_ref[...], wb_ref[...])
            d_of, _, d_az, d_hf, _, d_bo, d_bz, d_wa, d_wb = vjp(dy_ref[...])
            do_ref[...] = d_of
            dh_ref[...] = d_hf
            daz_ref[...] = d_az
            dbo_ref[...] = d_bo
            dbz_ref[...] = d_bz
            dwa_ref[...] += d_wa
            dwb_ref[...] += d_wb

        @pl.when(i >= lat_tiles)
        def _():
            daz_ref[...] = jnp.zeros_like(daz_ref)
            dbo_ref[...] = jnp.zeros_like(dbo_ref)
            dbz_ref[...] = jnp.zeros_like(dbz_ref)

    lat_rows = pl.BlockSpec((tm, w), lat)
    all_rows = pl.BlockSpec((tm, w), lambda i: (i, 0))
    vec = pl.BlockSpec((1, w), lambda i: (0, 0))
    sd_t = jax.ShapeDtypeStruct((t_rows, w), F32)
    sd_r = jax.ShapeDtypeStruct((r, w), F32)
    sd_v = jax.ShapeDtypeStruct((1, w), F32)
    return _call(
        body, (o_f, o_b, h_f, h_b, u, u, u, wa, wb, dy), name="post_bwd", grid=(r // tm,),
        in_specs=[lat_rows] * 4 + _post_specs(w, tm, lat_tiles, (4, 8, 9)) + [vec, vec]
        + [pl.BlockSpec((tm, 2 * w), lat)],
        out_specs=(lat_rows, lat_rows, all_rows, all_rows, all_rows, vec, vec),
        out_shape=(sd_t, sd_t, sd_r, sd_r, sd_r, sd_v, sd_v), sem=("arbitrary",), rider=rider)


OUT_ROW_GATE, OUT_ROW_LN_G, OUT_ROW_LN_B, OUT_ROW_LOSS = 0, 1, 2, 3


def _out_block(y, w_out, x, target, prm, tm):
    t_rows, dm = x.shape
    di = y.shape[1]

    def body(y_ref, w_ref, x_ref, t_ref, p_ref, dz_ref, dy_ref, gx_ref, acc_ref):
        @pl.when(pl.program_id(0) == 0)
        def _():
            acc_ref[...] = jnp.zeros_like(acc_ref)

        gate, ln_g, ln_b = p_ref[0:1, :], p_ref[1:2, :], p_ref[2:3, :]
        z = _nn(y_ref[...], w_ref[...])
        res = ALPHA * x_ref[...] + gate * z
        mu = jnp.mean(res, axis=-1, keepdims=True)
        rc = res - mu
        rstd = lax.rsqrt(jnp.mean(rc * rc, axis=-1, keepdims=True) + LN_EPS)
        rn = rc * rstd
        err = rn * ln_g + ln_b - t_ref[...]
        d_out = err * (1.0 / dm)
        d_rn = d_out * ln_g
        d_res = rstd * (d_rn - jnp.mean(d_rn, axis=-1, keepdims=True)
                        - rn * jnp.mean(d_rn * rn, axis=-1, keepdims=True))
        acc_ref[OUT_ROW_GATE:OUT_ROW_GATE + 1, :] += jnp.sum(d_res * z, axis=0, keepdims=True)
        acc_ref[OUT_ROW_LN_G:OUT_ROW_LN_G + 1, :] += jnp.sum(d_out * rn, axis=0, keepdims=True)
        acc_ref[OUT_ROW_LN_B:OUT_ROW_LN_B + 1, :] += jnp.sum(d_out, axis=0, keepdims=True)
        acc_ref[OUT_ROW_LOSS:OUT_ROW_LOSS + 1, :] += (0.5 / dm) * jnp.sum(err * err, axis=0, keepdims=True)
        gx_ref[...] = ALPHA * d_res
        dz = (d_res * gate).astype(BF16)
        dz_ref[...] = dz
        dy_ref[...] = _nt(dz, w_ref[...])

    rows_d = pl.BlockSpec((tm, dm), lambda i: (i, 0))
    rows_i = pl.BlockSpec((tm, di), lambda i: (i, 0))
    return pl.pallas_call(
        body, name="out_block", grid=(t_rows // tm,),
        in_specs=[rows_i, pl.BlockSpec((di, dm), lambda i: (0, 0)), rows_d, rows_d,
                  pl.BlockSpec((8, dm), lambda i: (0, 0))],
        out_specs=(rows_d, rows_i, rows_d, pl.BlockSpec((8, dm), lambda i: (0, 0))),
        out_shape=(jax.ShapeDtypeStruct((t_rows, dm), BF16), jax.ShapeDtypeStruct((t_rows, di), F32),
                   jax.ShapeDtypeStruct((t_rows, dm), F32), jax.ShapeDtypeStruct((8, dm), F32)),
        compiler_params=_params(("arbitrary",)),
    )(y, w_out, x, target, prm)


def _mod_fwd(c16, w_mod, tn):
    dm, n = w_mod.shape

    def body(c_ref, w_ref, o_ref, a_ref):
        a = _silu(c_ref[...])
        a_ref[...] = a
        o_ref[...] = _nn(a, w_ref[...], HIGHEST)

    return pl.pallas_call(
        body, name="mod_fwd", grid=(n // tn,),
        in_specs=[pl.BlockSpec((16, dm), lambda j: (0, 0)), pl.BlockSpec((dm, tn), lambda j: (0, j))],
        out_specs=(pl.BlockSpec((16, tn), lambda j: (0, j)), pl.BlockSpec((16, dm), lambda j: (0, 0))),
        out_shape=(jax.ShapeDtypeStruct((16, n), F32), jax.ShapeDtypeStruct((16, dm), F32)),
        compiler_params=_params(("arbitrary",)),
    )(c16, w_mod)


def _mod_bwd(a16, dm16, w_mod, tn):
    dm, n = w_mod.shape

    def body(a_ref, d_ref, w_ref, dw_ref, dc_ref):
        @pl.when(pl.program_id(0) == 0)
        def _():
            dc_ref[...] = jnp.zeros_like(dc_ref)
        dw_ref[...] = _tn(a_ref[...], d_ref[...], HIGHEST)
        dc_ref[...] += _nt(d_ref[...], w_ref[...], HIGHEST)

    return pl.pallas_call(
        body, name="mod_bwd", grid=(n // tn,),
        in_specs=[pl.BlockSpec((16, dm), lambda j: (0, 0)), pl.BlockSpec((16, tn), lambda j: (0, j)),
                  pl.BlockSpec((dm, tn), lambda j: (0, j))],
        out_specs=(pl.BlockSpec((dm, tn), lambda j: (0, j)), pl.BlockSpec((16, dm), lambda j: (0, 0))),
        out_shape=(jax.ShapeDtypeStruct((dm, n), F32), jax.ShapeDtypeStruct((16, dm), F32)),
        compiler_params=_params(("arbitrary",)),
    )(a16, dm16, w_mod)


def _sum_devices(g, fold_rows):
    n_dev, rows, n = g.shape

    def body(g_ref, s_ref, t_ref):
        s = g_ref[0]
        for dev in range(1, n_dev):
            s = s + g_ref[dev]
        t_ref[...] = jnp.broadcast_to(jnp.sum(s, axis=-1, keepdims=True), (rows, LANE))
        s_ref[...] = s
        s_ref[0:fold_rows, :] = s[0:fold_rows] + s[fold_rows:2 * fold_rows]

    return pl.pallas_call(
        body, name="sum_devices",
        out_shape=(jax.ShapeDtypeStruct((rows, n), F32), jax.ShapeDtypeStruct((rows, LANE), F32)),
        compiler_params=_params(),
    )(g)


def _c_ctx_grad(parts, c_ctx_row):
    def body(p_ref, c_ref, o_ref):
        s = p_ref[0]
        for chip in range(1, N_CHIPS):
            s = s + p_ref[2 * chip]
        cv = c_ref[...]
        sg = _sigmoid(cv)
        o_ref[...] = s * (sg * (1.0 + cv * (1.0 - sg)))

    return pl.pallas_call(
        body, name="c_ctx_grad", out_shape=jax.ShapeDtypeStruct(parts.shape[1:], F32), compiler_params=_params(),
    )(parts, c_ctx_row)


def _sum_pair(name, mine, got):
    def body(a_ref, b_ref, o_ref):
        o_ref[...] = (a_ref[...] + b_ref[...]).astype(BF16)

    k, rows, n = mine.shape
    tl = _largest_divisor(n, max(LANE, (1 << 18) // rows), LANE)
    spec = pl.BlockSpec((1, rows, tl), lambda kk, i: (kk, 0, i))
    return pl.pallas_call(
        body, name=name, grid=(k, n // tl), in_specs=[spec, spec], out_specs=spec,
        out_shape=jax.ShapeDtypeStruct(mine.shape, BF16), compiler_params=_params(("parallel", "parallel")),
    )(mine, got)


def _sum_pair_lanes(name, full, got, ci):
    rows, n = got.shape
    tr = _largest_divisor(rows, max(SUBLANE_BF16, (1 << 19) // n), SUBLANE_BF16)

    def body(ci_ref, a_ref, b_ref, o_ref):
        o_ref[...] = (a_ref[...] + b_ref[...].astype(F32)).astype(BF16)

    return pl.pallas_call(
        body, name=name,
        grid_spec=pltpu.PrefetchScalarGridSpec(
            num_scalar_prefetch=1, grid=(rows // tr,),
            in_specs=[pl.BlockSpec((tr, n), lambda i, c: (i, c[0])), pl.BlockSpec((tr, n), lambda i, c: (i, 0))],
            out_specs=pl.BlockSpec((tr, n), lambda i, c: (i, 0))),
        out_shape=jax.ShapeDtypeStruct((rows, n), BF16), compiler_params=_params(("parallel",)),
    )(ci.reshape(1).astype(jnp.int32), full, got)


def _sum_chips(name, got, own, chip):
    k, rows, n = got.shape
    tl = _largest_divisor(n, max(LANE, (1 << 18) // rows), LANE)

    def body(chip_ref, g_ref, own_ref, o_ref):
        total = None
        for kk in range(k):
            term = jnp.where(chip_ref[0] == kk, own_ref[0], g_ref[kk]).astype(F32)
            total = term if total is None else total + term
        o_ref[...] = total

    return pl.pallas_call(
        body, name=name,
        grid_spec=pltpu.PrefetchScalarGridSpec(
            num_scalar_prefetch=1, grid=(n // tl,),
            in_specs=[pl.BlockSpec((k, rows, tl), lambda i, c: (0, 0, i)),
                      pl.BlockSpec((1, rows, tl), lambda i, c: (c[0], 0, i))],
            out_specs=pl.BlockSpec((rows, tl), lambda i, c: (0, i))),
        out_shape=jax.ShapeDtypeStruct((rows, n), F32), compiler_params=_params(("parallel",)),
    )(chip.reshape(1).astype(jnp.int32), got, own)


def _adamw_update(w, g, m, v):
    m2 = ADAM_B1 * m + (1.0 - ADAM_B1) * g
    v2 = ADAM_B2 * v + (1.0 - ADAM_B2) * jnp.square(g)
    m_hat = m2 / (1.0 - ADAM_B1 ** ADAM_STEP)
    v_hat = v2 / (1.0 - ADAM_B2 ** ADAM_STEP)
    return -ADAM_LR * (m_hat / (jnp.sqrt(v_hat) + ADAM_EPS) + ADAM_WD * w), m2, v2


def _adamw(name, w, g, m, v, rider=None):
    rows, n = w.shape
    if rows % 8 == 0:
        tr = _largest_divisor(rows, max(8, (1 << 18) // n), 8)
        block, index, steps = (tr, n), (lambda i: (i, 0)), rows // tr
    else:
        tl = _largest_divisor(n, max(LANE, (1 << 18) // rows), LANE)
        block, index, steps = (rows, tl), (lambda i: (0, i)), n // tl

    def body(w_ref, g_ref, m_ref, v_ref, d_ref, mo_ref, vo_ref):
        d_ref[...], mo_ref[...], vo_ref[...] = _adamw_update(w_ref[...], g_ref[...], m_ref[...], v_ref[...])

    spec = pl.BlockSpec(block, index)
    sds = jax.ShapeDtypeStruct((rows, n), F32)
    return _call(body, (w, g, m, v), name=name, grid=(steps,), in_specs=[spec] * 4, out_specs=(spec,) * 3,
                 out_shape=(sds, sds, sds), sem=("parallel",), rider=rider)


PACK_LANES = 1024


def _pack(pieces):
    flat = jnp.concatenate([p.reshape(-1) for p in pieces])
    total = -(-flat.shape[0] // (8 * PACK_LANES)) * 8 * PACK_LANES
    return jnp.pad(flat, (0, total - flat.shape[0])).reshape(-1, PACK_LANES)


def _unpack(packed, shapes):
    flat = packed.reshape(-1)
    out, off = [], 0
    for shp in shapes:
        size = math.prod(shp)
        out.append(flat[off:off + size].reshape(shp))
        off += size
    return out


def _rows8(rows, width):
    flat = [r.reshape(width) for r in rows] + [jnp.zeros(((8 - len(rows)) * width,), F32)]
    return jnp.concatenate(flat).reshape(8, width)


def kernel(x, c, ctx, c_ctx, w_mod, b_mod, w_in, conv_w, conv_b, hg_lb, ml_gate_b, hg_norm_w, ml_norm_w, w_out, ln_g, ln_b, loss_target, m_c_ctx, m_w_mod, m_b_mod, m_w_in, m_conv_w, m_conv_b, m_hg_lb, m_ml_gate_b, m_hg_norm_w, m_ml_norm_w, m_w_out, m_ln_g, m_ln_b, v_c_ctx, v_w_mod, v_b_mod, v_w_in, v_conv_w, v_conv_b, v_hg_lb, v_ml_gate_b, v_hg_norm_w, v_ml_norm_w, v_w_out, v_ln_g, v_ln_b):
    t_rows, dm = x.shape[1], x.shape[2]
    c_rows = ctx.shape[1]
    w = hg_norm_w.shape[1]
    n_ml = ml_gate_b.shape[-1]
    n_hg = w // HG_DK
    di = 2 * w
    n_in = 10 * w + 4 * n_ml
    ns = w_in.shape[2]
    nm = w_mod.shape[2]
    n_pad = 10 * w + LANE
    r_rows = t_rows + c_rows
    row_gcd = math.gcd(t_rows, c_rows)
    hg_chunk, ml_chunk = math.gcd(HG_CHUNK, row_gcd), math.gcd(ML_CHUNK, row_gcd)
    hg_counts = (t_rows // hg_chunk, c_rows // hg_chunk, hg_chunk)
    ml_counts = (t_rows // ml_chunk, c_rows // ml_chunk, ml_chunk)
    assert ml_norm_w.shape[1] == w and di == dm and N_CHIPS * ns == n_in and N_CHIPS * nm == 3 * dm
    assert w_out.shape[1] * N_CHIPS == di and 4 * n_ml <= LANE and t_rows % GRID_W == 0

    xi, yi, ci = lax.axis_index("x"), lax.axis_index("y"), lax.axis_index("c")
    chip = 2 * xi + yi
    dev = 4 * xi + 2 * yi + ci

    tm = _largest_divisor(math.gcd(t_rows, c_rows), 256, 8)
    tm_mm = _largest_divisor(r_rows, 1088, SUBLANE_BF16)
    tn_mm = LANE * _largest_divisor(n_pad // LANE, 9)
    tn_mod = _largest_divisor(nm, 512, LANE)

    shard_shapes = [(dm,), (2, 2, w // N_CHIPS), (3, 3, di // N_CHIPS)]
    g1 = _all_gather8(_pack([c, hg_lb, conv_w])).run("gather_inputs")[0]
    per_dev = [_unpack(g1[i], shard_shapes) for i in range(N_DEV)]
    c_all = jnp.stack([p[0] for p in per_dev])
    lb_full = jnp.concatenate([per_dev[2 * k][1] for k in range(N_CHIPS)], axis=-1)
    conv_w9 = jnp.concatenate([per_dev[2 * k][2] for k in range(N_CHIPS)], axis=-1).reshape(9, di)

    c16 = jnp.concatenate([c_all, c_ctx[None], jnp.zeros((16 - N_DEV - 1, dm), F32)])
    mod_part, a16 = _mod_fwd(c16, w_mod[0], tn_mod)
    g2 = _all_gather8(mod_part).run("gather_mod")[0]
    mod_all = jnp.concatenate([g2[2 * k] for k in range(N_CHIPS)], axis=1) + b_mod
    mod_x = lax.dynamic_index_in_dim(mod_all, dev, 0, keepdims=False).reshape(3, dm)
    mod_c = mod_all[N_DEV].reshape(3, dm)
    prm = jnp.stack([_rows8(list(mod_x), dm), _rows8(list(mod_c), dm)])

    as_t = lambda a: jnp.transpose(a[0])
    half_in = lax.dynamic_slice_in_dim(as_t(w_in).astype(BF16), ci * (dm // 2), dm // 2, 1)
    half_out = lax.dynamic_slice_in_dim(w_out[0].astype(BF16), ci * (di // (2 * N_CHIPS)), di // (2 * N_CHIPS), 0)
    fetched_in = _own_block(chip, half_in, _all_gather_chips([half_in]).run("gather_w_in")[0])
    gw_in = _join_halves(ci, fetched_in, _sibling_swap([fetched_in]).run("gather_w_in_pair")[0], 2)
    wt_full = jnp.concatenate([gw_in.reshape(n_in, dm), jnp.zeros((n_pad - n_in, dm), BF16)])

    hc = _modulate_fwd(x[0], ctx[0], prm, tm)
    u, (got_out,) = _mm_nt("in_proj", hc, wt_full, tm_mm, tn_mm, F32, rider=_all_gather_chips([half_out]))
    fetched_out = _own_block(chip, half_out, got_out)
    (o_f, o_b), hg_saved, (swapped_out,) = _hg_scan_fwd(u, lb_full, w, *hg_counts,
                                                         rider=_sibling_swap([fetched_out]))
    w_out_full = _join_halves(ci, fetched_out, swapped_out, 1).reshape(di, dm)
    qk = _conv_fwd(u, conv_w9, conv_b, t_rows, c_rows, w, LANE)
    gate_b_row = jnp.pad(ml_gate_b.reshape(1, -1), ((0, 0), (0, LANE - 4 * n_ml)))
    (h_f, h_b), ml_saved = _ml_scan_fwd(qk, u, gate_b_row, w, n_ml, *ml_counts)
    y = _post_fwd(o_f, o_b, h_f, h_b, u, hg_norm_w, ml_norm_w, t_rows, w, n_hg, n_ml, tm)
    prm_out = _rows8([mod_x[2], ln_g, ln_b], dm)
    dz, dy, gx_direct, acc_out = _out_block(y, w_out_full, x[0], loss_target[0], prm_out, tm)

    d_w_out = _mm_tn("d_w_out", y, dz, _largest_divisor(di, 1024, LANE),
                     _largest_divisor(t_rows, 1024, SUBLANE_BF16))
    d_w_out4 = d_w_out.reshape(N_CHIPS, 2, di // (2 * N_CHIPS), dm)
    mine_out = lax.dynamic_index_in_dim(d_w_out4, ci, 1, keepdims=False)
    other_out = lax.dynamic_index_in_dim(d_w_out4, 1 - ci, 1, keepdims=False)
    (d_o, d_h, d_az, d_bo, d_bz, d_wa, d_wb), (got_out,) = _post_bwd(
        o_f, o_b, h_f, h_b, u, hg_norm_w, ml_norm_w, dy, t_rows, w, n_hg, n_ml, tm, rider=_sibling_swap([other_out]))
    pair_out = _sum_pair("rs_pair_sum_w_out", mine_out, got_out)
    (d_aq_f, d_aff, d_ai_f, d_lb_f), (d_aq_b, d_afb, d_ai_b, d_lb_b), (landed_out,) = _hg_scan_bwd(
        u, lb_full, hg_saved, d_o, w, *hg_counts, rider=_chip_scatter([pair_out]))
    half_g_out = _sum_chips("rs_chip_sum_w_out", landed_out, pair_out, chip)
    (d_qk_f, d_v_f, d_g_f, d_gb_f), (d_qk_b, d_v_b, d_g_b, d_gb_b), (sibling_out,) = _ml_scan_bwd(
        qk, u, gate_b_row, ml_saved, d_h, w, n_ml, *ml_counts, rider=_sibling_swap([half_g_out]))
    g_w_out = _join_halves(ci, half_g_out, sibling_out, 0)
    d_bqk, d_cw, d_cb = _conv_bwd(u, (d_qk_f, d_qk_b), conv_w9, conv_b, t_rows, c_rows, w, LANE)
    du = _assemble_du([(d_aq_f, d_aq_b), d_aff, d_afb, (d_ai_f, d_ai_b), d_az, d_bqk, (d_v_f, d_v_b), d_bo, d_bz],
                      (d_g_f, d_g_b), n_pad, tm // 2)
    d_wt_in, d_wt_in_bf16 = _mm_tn("d_w_in", du, hc, tn_mm, tm_mm, with_bf16=True)

    lanes_of = lambda core: (slice(None), pl.ds(core * (dm // 2), dm // 2))
    got_in = _Exchange([d_wt_in_bf16], [jax.ShapeDtypeStruct((n_pad, dm // 2), BF16)],
                       [(SIBLING_MASK, 0, lambda s, r: lanes_of(r[2]), 0, None)]).run("rs_pair_w_in")[0]
    pair_half = _sum_pair_lanes("rs_pair_sum_w_in", d_wt_in, got_in, ci)
    pair_in = jnp.stack([pair_half[k * ns:(k + 1) * ns] for k in range(N_CHIPS)])
    d_hc, (landed_in,) = _mm_acc("d_h", du, wt_full, tm_mm, tn_mm, rider=_chip_scatter([pair_in]))
    half_g_in = _sum_chips("rs_chip_sum_w_in", landed_in, pair_in, chip)
    g_wt_in = _join_halves(ci, half_g_in, _sibling_swap([half_g_in]).run("rs_join_w_in")[0], 1)
    (gx, acc_mod), _ = _modulate_bwd(x[0], ctx[0], d_hc, prm, gx_direct, tm)
    grad_x = gx[None]

    zero_row = jnp.zeros((dm,), F32)
    d_gb = jnp.concatenate([d_gb_f[:, 0:n_ml], d_gb_b[:, n_ml:2 * n_ml], d_gb_f[:, 2 * n_ml:3 * n_ml],
                            d_gb_b[:, 3 * n_ml:4 * n_ml], jnp.zeros((1, dm - 4 * n_ml), F32)], axis=1)
    rows = [acc_mod[0, 0], acc_mod[0, 1], acc_out[OUT_ROW_GATE],
            acc_mod[1, 0], acc_mod[1, 1], zero_row]
    rows += list(d_cw) + [d_cb[0], d_lb_f.reshape(dm), d_lb_b.reshape(dm),
                          jnp.concatenate([d_wa[0], d_wb[0]]), acc_out[OUT_ROW_LN_G], acc_out[OUT_ROW_LN_B],
                          acc_out[OUT_ROW_LOSS], d_gb[0], zero_row]
    ROW_CW, ROW_CB, ROW_LB, ROW_NORM, ROW_LN_G, ROW_LN_B, ROW_LOSS, ROW_GB = 6, 15, 16, 18, 19, 20, 21, 22
    delta, new_m, new_v = {}, {}, {}
    small_rows = jnp.concatenate([r.reshape(dm) for r in rows]).reshape(len(rows), dm)
    g3 = _all_gather8(small_rows).run("gather_small_grads")[0]
    sums, totals = _sum_devices(g3, 3)
    loss = totals[ROW_LOSS, 0]
    dm16 = jnp.concatenate([g3[:, 0:3, :].reshape(N_DEV, 3 * dm), sums[3:6].reshape(1, 3 * dm),
                            jnp.zeros((16 - N_DEV - 1, 3 * dm), F32)])
    g_w_mod, dc16 = _mod_bwd(a16, lax.dynamic_slice_in_dim(dm16, chip * nm, nm, 1), w_mod[0], tn_mod)
    g4 = _all_gather8(jnp.pad(dc16[N_DEV:N_DEV + 1], ((0, 7), (0, 0)))).run("gather_c_ctx")[0]
    g_c_ctx = _c_ctx_grad(g4, jnp.broadcast_to(c_ctx[None], (8, dm)))[0]
    res, _ = _adamw("adamw_w_in", as_t(w_in), g_wt_in, as_t(m_w_in), as_t(v_w_in))
    delta["w_in"], new_m["w_in"], new_v["w_in"] = (jnp.transpose(a)[None] for a in res)
    res, _ = _adamw("adamw_w_mod", w_mod[0], g_w_mod, m_w_mod[0], v_w_mod[0])
    delta["w_mod"], new_m["w_mod"], new_v["w_mod"] = (a[None] for a in res)

    chip_cols = lambda a, width: lax.dynamic_slice_in_dim(a, chip * width, width, a.ndim - 1)
    grads = {
        "c_ctx": g_c_ctx,
        "w_mod": g_w_mod[None],
        "b_mod": sums[0:3].reshape(1, 3 * dm),
        "w_in": jnp.transpose(g_wt_in)[None],
        "conv_w": chip_cols(sums[ROW_CW:ROW_CW + 9].reshape(1, 3, 3, di), di // N_CHIPS),
        "conv_b": sums[ROW_CB][None],
        "hg_lb": chip_cols(sums[ROW_LB:ROW_LB + 2].reshape(2, 2, w), w // N_CHIPS),
        "ml_gate_b": sums[ROW_GB, 0:4 * n_ml].reshape(1, 4, n_ml),
        "hg_norm_w": sums[ROW_NORM, 0:w][None],
        "ml_norm_w": sums[ROW_NORM, w:2 * w][None],
        "w_out": g_w_out[None],
        "ln_g": sums[ROW_LN_G][None],
        "ln_b": sums[ROW_LN_B][None],
    }
    weights = dict(c_ctx=c_ctx, w_mod=w_mod, b_mod=b_mod, w_in=w_in, conv_w=conv_w, conv_b=conv_b, hg_lb=hg_lb,
                   ml_gate_b=ml_gate_b, hg_norm_w=hg_norm_w, ml_norm_w=ml_norm_w, w_out=w_out, ln_g=ln_g, ln_b=ln_b)
    mom1 = dict(c_ctx=m_c_ctx, w_mod=m_w_mod, b_mod=m_b_mod, w_in=m_w_in, conv_w=m_conv_w, conv_b=m_conv_b,
                hg_lb=m_hg_lb, ml_gate_b=m_ml_gate_b, hg_norm_w=m_hg_norm_w, ml_norm_w=m_ml_norm_w, w_out=m_w_out,
                ln_g=m_ln_g, ln_b=m_ln_b)
    mom2 = dict(c_ctx=v_c_ctx, w_mod=v_w_mod, b_mod=v_b_mod, w_in=v_w_in, conv_w=v_conv_w, conv_b=v_conv_b,
                hg_lb=v_hg_lb, ml_gate_b=v_ml_gate_b, hg_norm_w=v_hg_norm_w, ml_norm_w=v_ml_norm_w, w_out=v_w_out,
                ln_g=v_ln_g, ln_b=v_ln_b)
    names = list(weights)
    big = ("w_mod", "w_in", "w_out")
    small = [n for n in names if n not in big]

    res, _ = _adamw("adamw_w_out", w_out[0], g_w_out, m_w_out[0], v_w_out[0])
    delta["w_out"], new_m["w_out"], new_v["w_out"] = (a[None] for a in res)
    small_shapes = [weights[n].shape for n in small]
    res, _ = _adamw("adamw_small", *(_pack([src[n] for n in small]) for src in (weights, grads, mom1, mom2)))
    for out, packed in zip((delta, new_m, new_v), res):
        for n, a in zip(small, _unpack(packed, small_shapes)):
            out[n] = a

    return (loss, grad_x, *[grads[n].reshape(weights[n].shape) for n in names], *[delta[n] for n in names],
            *[new_m[n] for n in names], *[new_v[n] for n in names])
```

```python
import functools
import math

import jax
import jax.numpy as jnp
from jax import lax
from jax.experimental import pallas as pl
from jax.experimental.pallas import tpu as pltpu

F32 = jnp.float32
BF16 = jnp.bfloat16
HIGHEST = lax.Precision.HIGHEST
MESH = pl.DeviceIdType.MESH

HG_CHUNK = 64
ML_CHUNK = 256
GRID_W = 64
HG_DK = 128
LANE = 128
SUBLANE_BF16 = 16
ALPHA = 2.0 ** 0.25
LN_EPS = 1e-5
NORM_EPS = 1e-6
ADAM_LR = 0.001
ADAM_B1 = 0.9
ADAM_B2 = 0.999
ADAM_EPS = 1e-08
ADAM_WD = 0.01
ADAM_STEP = 10
VMEM_LIMIT = 56 * 1024 * 1024
N_CHIPS = 4
N_DEV = 8


def _params(sem=None):
    return pltpu.CompilerParams(dimension_semantics=sem, vmem_limit_bytes=VMEM_LIMIT)


def _largest_divisor(n, cap, multiple=1):
    best = None
    for d in range(multiple, min(n, cap) + 1, multiple):
        if n % d == 0:
            best = d
    assert best is not None, (n, cap, multiple)
    return best


def _sigmoid(x):
    return jax.nn.sigmoid(x)


def _silu(x):
    return x * jax.nn.sigmoid(x)


def _dot(a, b, dims, precision=None):
    return lax.dot_general(a, b, (dims, ((), ())), precision=precision, preferred_element_type=F32)


def _nn(a, b, precision=None):
    return _dot(a, b, ((1,), (0,)), precision)


def _nt(a, b, precision=None):
    return _dot(a, b, ((1,), (1,)), precision)


def _tn(a, b, precision=None):
    return _dot(a, b, ((0,), (0,)), precision)


def _narrow(x):
    return x.astype(BF16)


@jax.custom_vjp
def _bnn(a, b):
    return _nn(_narrow(a), _narrow(b))


def _bnn_fwd(a, b):
    an, bn = _narrow(a), _narrow(b)
    return _nn(an, bn), (an, bn)


def _bnn_bwd(res, ct):
    an, bn = res
    ctn = _narrow(ct)
    return _nt(ctn, bn), _tn(an, ctn)


_bnn.defvjp(_bnn_fwd, _bnn_bwd)


@jax.custom_vjp
def _bnt(a, b):
    return _nt(_narrow(a), _narrow(b))


def _bnt_fwd(a, b):
    an, bn = _narrow(a), _narrow(b)
    return _nt(an, bn), (an, bn)


def _bnt_bwd(res, ct):
    an, bn = res
    ctn = _narrow(ct)
    return _nn(ctn, bn), _tn(ctn, an)


_bnt.defvjp(_bnt_fwd, _bnt_bwd)


@jax.custom_vjp
def _btn(a, b):
    return _tn(_narrow(a), _narrow(b))


def _btn_fwd(a, b):
    an, bn = _narrow(a), _narrow(b)
    return _tn(an, bn), (an, bn)


def _btn_bwd(res, ct):
    an, bn = res
    ctn = _narrow(ct)
    return _nt(bn, ctn), _nn(an, ctn)


_btn.defvjp(_btn_fwd, _btn_bwd)


def _visible(n, rev):
    r = lax.broadcasted_iota(jnp.int32, (n, n), 0)
    c = lax.broadcasted_iota(jnp.int32, (n, n), 1)
    return (r <= c) if rev else (r >= c)


def _mask_matmul(mask, x):
    mb = mask.astype(BF16)
    hi = x.astype(BF16)
    lo = (x - hi.astype(F32)).astype(BF16)
    return _nn(mb, hi) + _nn(mb, lo)


@functools.partial(jax.custom_vjp, nondiff_argnums=(1,))
def _cumulative(x, rev):
    return _mask_matmul(_visible(x.shape[0], rev), x)


def _cumulative_fwd(x, rev):
    return _cumulative(x, rev), None


def _cumulative_bwd(rev, _, ct):
    return (_mask_matmul(_visible(ct.shape[0], not rev), ct),)


_cumulative.defvjp(_cumulative_fwd, _cumulative_bwd)


def _hg_chunk(states, aq, af, ai, lb0, lb1, rev):
    n_heads = len(states)
    lb = _sigmoid(lb0 - lb1)
    f = lb + (1.0 - lb) * _sigmoid(af)
    g = jnp.log(f)
    k = 1.0 - f
    q = _silu(aq)
    chunk = aq.shape[0]
    vis = _visible(chunk, rev)
    b = _cumulative(g, rev)
    last = 0 if rev else chunk - 1
    b_end = b[last:last + 1]
    b_mid = b[chunk // 2:chunk // 2 + 1]
    q_inter = q * jnp.exp(b)
    q_intra = q * jnp.exp(b - b_mid)
    k_intra = k * jnp.exp(b_mid - b)
    k_dec = k * jnp.exp(b_end - b)
    e_end = jnp.exp(b_end)
    new_states, outs = [], []
    for h in range(n_heads):
        sl = slice(h * HG_DK, (h + 1) * HG_DK)
        s_t = states[h]
        scores = jnp.where(vis, _nt(q_intra[:, sl], k_intra[:, sl]), 0.0)
        outs.append(_nt(q_inter[:, sl], s_t) + _nn(scores, ai[:, sl]))
        new_states.append(e_end[:, sl] * s_t + _tn(ai[:, sl], k_dec[:, sl]))
    return new_states, jnp.concatenate(outs, axis=1)


def _ml_chunk(state, q, k, v, g, gb, rev, d):
    cms, nvs, mbs = state
    n_heads = len(cms)
    dh = q.shape[1] // n_heads
    ga = g + gb
    log_f_all = jax.nn.log_sigmoid(ga)
    chunk = q.shape[0]
    vis = _visible(chunk, rev)
    b_all = _cumulative(log_f_all, rev)
    last = 0 if rev else chunk - 1
    k = k * (dh ** -0.5)
    new_c, new_n, new_m, outs = [], [], [], []
    for h in range(n_heads):
        ci = d * n_heads + h
        cf = (2 + d) * n_heads + h
        sl = slice(h * dh, (h + 1) * dh)
        qh, kh, vh = q[:, sl], k[:, sl], v[:, sl]
        li = ga[:, ci:ci + 1]
        b = b_all[:, cf:cf + 1]
        m = mbs[h][:, 0:1]
        row = jnp.transpose(li - b)
        log_w = jnp.where(vis, b + row, -jnp.inf)
        m_inter = b + m
        m_t = jnp.maximum(m_inter, jnp.max(log_w, axis=-1, keepdims=True))
        w_inter = jnp.exp(m_inter - m_t)
        w_qk = jnp.exp(log_w - m_t) * _bnt(qh, kh)
        num = w_inter * _bnt(qh, cms[h]) + _bnn(w_qk, vh)
        den = w_inter * jnp.sum(qh * nvs[h], axis=-1, keepdims=True) + jnp.sum(w_qk, axis=-1, keepdims=True)
        outs.append(num / jnp.maximum(jnp.abs(den), jnp.exp(-m_t)))
        m_new = m_t[last:last + 1]
        b_end = b[last:last + 1]
        w_s = jnp.exp(b_end - b + li - m_new)
        decay = jnp.exp(b_end + m - m_new)
        new_c.append(decay * cms[h] + _btn(w_s * vh, kh))
        new_n.append(decay * nvs[h] + jnp.sum(w_s * kh, axis=0, keepdims=True))
        new_m.append(jnp.broadcast_to(m_new, (1, LANE)))
    return (new_c, new_n, new_m), jnp.concatenate(outs, axis=1)


def _post_fn(o_f, o_b, az, h_f, h_b, bo, bz, wa, wb, n_hg, n_ml):
    o = o_f + o_b
    parts = []
    for h in range(n_hg):
        s = o[:, h * HG_DK:(h + 1) * HG_DK]
        parts.append(s * lax.rsqrt(jnp.mean(s * s, axis=-1, keepdims=True) + NORM_EPS))
    y_a = jnp.concatenate(parts, axis=1) * wa * _silu(az)
    hh = h_f + h_b
    dh = hh.shape[1] // n_ml
    parts = []
    for h in range(n_ml):
        s = hh[:, h * dh:(h + 1) * dh]
        mu = jnp.mean(s, axis=-1, keepdims=True)
        sc = s - mu
        parts.append(sc * lax.rsqrt(jnp.mean(sc * sc, axis=-1, keepdims=True) + NORM_EPS))
    y_b = jnp.concatenate(parts, axis=1) * wb * _sigmoid(bo) * _silu(bz)
    return jnp.concatenate([y_a, y_b], axis=1)


def _chip_of(dev):
    return 2 * dev[0] + dev[1]


def _index_of(dev):
    return 4 * dev[0] + 2 * dev[1] + dev[2]


class _Exchange:
    def __init__(self, srcs, out_shapes, transfers, local_copies=()):
        self.srcs, self.out_shapes = list(srcs), list(out_shapes)
        self.transfers, self.local_copies = list(transfers), list(local_copies)

    def scratch(self):
        return [pltpu.SemaphoreType.DMA((len(self.transfers),)), pltpu.SemaphoreType.DMA((len(self.transfers),)),
                pltpu.SemaphoreType.DMA((max(len(self.local_copies), 1),))]

    def copies(self, ins, outs, send_sems, recv_sems, local_sems):
        me = (lax.axis_index("x"), lax.axis_index("y"), lax.axis_index("c"))

        def pick(ref, fn, *who):
            return ref if fn is None else ref.at[fn(*who)]

        sends, recvs, locs = [], [], []
        for t, (mask, si, sfn, di, dfn) in enumerate(self.transfers):
            peer = tuple(1 - p if flip else p for p, flip in zip(me, mask))
            sends.append(pltpu.make_async_remote_copy(
                src_ref=pick(ins[si], sfn, me, peer), dst_ref=pick(outs[di], dfn, me, peer),
                send_sem=send_sems.at[t], recv_sem=recv_sems.at[t], device_id=peer, device_id_type=MESH))
            landing = pick(outs[di], dfn, peer, me)
            recvs.append(pltpu.make_async_remote_copy(
                src_ref=landing, dst_ref=landing,
                send_sem=send_sems.at[t], recv_sem=recv_sems.at[t], device_id=peer, device_id_type=MESH))
        for l, (si, sfn, di, dfn) in enumerate(self.local_copies):
            locs.append(pltpu.make_async_copy(pick(ins[si], sfn, me), pick(outs[di], dfn, me), local_sems.at[l]))

        def start():
            for cp in locs + sends:
                cp.start()

        def wait():
            for cp in recvs:
                cp.wait_recv()
            for cp in sends:
                cp.wait_send()
            for cp in locs:
                cp.wait()

        return start, wait

    def run(self, name, in_place=None):
        n_in, n_out = len(self.srcs), len(self.out_shapes)

        def body(*refs):
            start, wait = self.copies(refs[:n_in], refs[n_in:n_in + n_out], *refs[n_in + n_out:])
            start()
            wait()

        hbm = pl.BlockSpec(memory_space=pltpu.HBM)
        return pl.pallas_call(
            body, name=name, out_shape=tuple(self.out_shapes), in_specs=[hbm] * n_in,
            out_specs=tuple([hbm] * n_out), scratch_shapes=self.scratch(),
            input_output_aliases=dict(in_place or {}),
        )(*self.srcs)


def _call(body, operands, *, name, grid, in_specs, out_specs, out_shape, scratch_shapes=(), sem=None, rider=None):
    out_specs, out_shape, scratch_shapes = list(out_specs), list(out_shape), list(scratch_shapes)
    if rider is None:
        res = pl.pallas_call(
            body, name=name, grid=grid, in_specs=list(in_specs), out_specs=tuple(out_specs),
            out_shape=tuple(out_shape), scratch_shapes=scratch_shapes, compiler_params=_params(sem),
        )(*operands)
        return list(res), []
    counts = (len(in_specs), len(rider.srcs), len(out_specs), len(rider.out_shapes), len(scratch_shapes), 3)

    def full(*refs):
        groups, pos = [], 0
        for k in counts:
            groups.append(refs[pos:pos + k])
            pos += k
        own_in, ex_in, own_out, ex_out, own_scr, ex_scr = groups
        ids = [pl.program_id(a) for a in range(len(grid))]
        first = functools.reduce(jnp.logical_and, [i == 0 for i in ids])
        last = functools.reduce(jnp.logical_and, [i == g - 1 for i, g in zip(ids, grid)])
        start, wait = rider.copies(ex_in, ex_out, *ex_scr)
        pl.when(first)(start)
        body(*own_in, *own_out, *own_scr)
        pl.when(last)(wait)

    hbm = pl.BlockSpec(memory_space=pltpu.HBM)
    res = pl.pallas_call(
        full, name=name, grid=grid, in_specs=list(in_specs) + [hbm] * counts[1],
        out_specs=tuple(out_specs + [hbm] * counts[3]), out_shape=tuple(out_shape + rider.out_shapes),
        scratch_shapes=scratch_shapes + rider.scratch(), compiler_params=_params(("arbitrary",) * len(grid)),
    )(*operands, *rider.srcs)
    return list(res[:counts[2]]), list(res[counts[2]:])


ALL_MASKS = [(mx, my, mc) for mx in (0, 1) for my in (0, 1) for mc in (0, 1)][1:]
CHIP_MASKS = [(1, 0, 0), (0, 1, 0), (1, 1, 0)]
SIBLING_MASK = (0, 0, 1)


def _all_gather8(v):
    out = jax.ShapeDtypeStruct((N_DEV,) + v.shape, v.dtype)
    slot = lambda sender, receiver: _index_of(sender)
    transfers = [(mask, 0, None, 0, slot) for mask in ALL_MASKS]
    return _Exchange([v], [out], transfers, [(0, None, 0, lambda me: _index_of(me))])


def _all_gather_chips(arrays):
    outs = [jax.ShapeDtypeStruct((N_CHIPS,) + a.shape, a.dtype) for a in arrays]
    slot = lambda sender, receiver: _chip_of(sender)
    return _Exchange(arrays, outs, [(mask, i, None, i, slot) for i in range(len(arrays)) for mask in CHIP_MASKS])


def _sibling_swap(arrays):
    outs = [jax.ShapeDtypeStruct(a.shape, a.dtype) for a in arrays]
    return _Exchange(arrays, outs, [(SIBLING_MASK, i, None, i, None) for i in range(len(arrays))])


def _chip_scatter(arrays):
    outs = [jax.ShapeDtypeStruct(a.shape, a.dtype) for a in arrays]
    transfers = [(mask, i, lambda s, r: _chip_of(r), i, lambda s, r: _chip_of(s))
                 for i in range(len(arrays)) for mask in CHIP_MASKS]
    return _Exchange(arrays, outs, transfers)


def _own_block(chip, own, blocks):
    sel = (lax.broadcasted_iota(jnp.int32, (N_CHIPS,) + (1,) * (blocks.ndim - 1), 0) == chip)
    return jnp.where(sel, own if own.ndim == blocks.ndim else own[None], blocks)


def _join_halves(ci, mine, other, axis):
    return jnp.where(ci == 0, jnp.concatenate([mine, other], axis=axis), jnp.concatenate([other, mine], axis=axis))


def _mm_nt(name, a, b, tm, tn, out_dtype, rider=None):
    m, k = a.shape
    n = b.shape[0]

    def body(a_ref, b_ref, o_ref):
        o_ref[...] = _nt(a_ref[...], b_ref[...]).astype(out_dtype)

    (out,), rode = _call(
        body, (a, b), name=name, grid=(n // tn, m // tm),
        in_specs=[pl.BlockSpec((tm, k), lambda j, i: (i, 0)), pl.BlockSpec((tn, k), lambda j, i: (j, 0))],
        out_specs=[pl.BlockSpec((tm, tn), lambda j, i: (i, j))],
        out_shape=[jax.ShapeDtypeStruct((m, n), out_dtype)], sem=("parallel", "parallel"), rider=rider)
    return out, rode


def _mm_acc(name, a, b, tm, tk, rider=None):
    m, kc = a.shape
    n = b.shape[1]

    def body(a_ref, b_ref, o_ref):
        @pl.when(pl.program_id(1) == 0)
        def _():
            o_ref[...] = jnp.zeros_like(o_ref)
        o_ref[...] += _nn(a_ref[...], b_ref[...])

    (out,), rode = _call(
        body, (a, b), name=name, grid=(m // tm, kc // tk),
        in_specs=[pl.BlockSpec((tm, tk), lambda i, kk: (i, kk)), pl.BlockSpec((tk, n), lambda i, kk: (kk, 0))],
        out_specs=[pl.BlockSpec((tm, n), lambda i, kk: (i, 0))],
        out_shape=[jax.ShapeDtypeStruct((m, n), F32)], sem=("parallel", "arbitrary"), rider=rider)
    return out, rode


def _mm_tn(name, a, b, tm, tk, with_bf16=False):
    kr, m = a.shape
    n = b.shape[1]
    steps_k = kr // tk

    def body(a_ref, b_ref, o_ref, *narrow):
        @pl.when(pl.program_id(1) == 0)
        def _():
            o_ref[...] = jnp.zeros_like(o_ref)
        o_ref[...] += _tn(a_ref[...], b_ref[...])
        if with_bf16:
            @pl.when(pl.program_id(1) == steps_k - 1)
            def _():
                narrow[0][...] = o_ref[...].astype(BF16)

    out_spec = pl.BlockSpec((tm, n), lambda i, kk: (i, 0))
    res = pl.pallas_call(
        body, name=name, grid=(m // tm, steps_k),
        in_specs=[pl.BlockSpec((tk, tm), lambda i, kk: (kk, i)), pl.BlockSpec((tk, n), lambda i, kk: (kk, 0))],
        out_specs=(out_spec,) * (2 if with_bf16 else 1),
        out_shape=(jax.ShapeDtypeStruct((m, n), F32),) + ((jax.ShapeDtypeStruct((m, n), BF16),) if with_bf16 else ()),
        compiler_params=_params(("parallel", "arbitrary")),
    )(a, b)
    return res if with_bf16 else res[0]


def _modulate_fwd(x, ctx, prm, tm):
    t_rows, dm = x.shape
    lat = t_rows // tm
    r = t_rows + ctx.shape[0]

    def body(x_ref, c_ref, p_ref, h_ref):
        xv = jnp.where(pl.program_id(0) >= lat, c_ref[...], x_ref[...])
        mu = jnp.mean(xv, axis=-1, keepdims=True)
        xm = xv - mu
        n = xm * lax.rsqrt(jnp.mean(xm * xm, axis=-1, keepdims=True) + LN_EPS)
        h_ref[...] = (n * (1.0 + p_ref[0, 1:2, :]) + p_ref[0, 0:1, :]).astype(BF16)

    return pl.pallas_call(
        body, name="modulate_fwd", grid=(r // tm,),
        in_specs=[pl.BlockSpec((tm, dm), lambda i: (jnp.minimum(i, lat - 1), 0)),
                  pl.BlockSpec((tm, dm), lambda i: (jnp.maximum(i - lat, 0), 0)),
                  pl.BlockSpec((1, 8, dm), lambda i: ((i >= lat).astype(jnp.int32), 0, 0))],
        out_specs=pl.BlockSpec((tm, dm), lambda i: (i, 0)),
        out_shape=jax.ShapeDtypeStruct((r, dm), BF16),
        compiler_params=_params(("parallel",)),
    )(x, ctx, prm)


def _modulate_bwd(x, ctx, dh, prm, gx_direct, tm, rider=None):
    t_rows, dm = x.shape
    lat, n_ct = t_rows // tm, ctx.shape[0] // tm
    is_ctx = lambda i: i < n_ct
    cls = lambda i: is_ctx(i).astype(jnp.int32)
    lat_tile = lambda i: (jnp.maximum(i - n_ct, 0), 0)

    def body(x_ref, c_ref, dh_ref, p_ref, gd_ref, gx_ref, acc_ref):
        i = pl.program_id(0)

        @pl.when((i == 0) | (i == n_ct))
        def _():
            acc_ref[...] = jnp.zeros_like(acc_ref)

        x = jnp.where(is_ctx(i), c_ref[...], x_ref[...])
        dh_v = dh_ref[...]
        mu = jnp.mean(x, axis=-1, keepdims=True)
        xm = x - mu
        rstd = lax.rsqrt(jnp.mean(xm * xm, axis=-1, keepdims=True) + LN_EPS)
        n = xm * rstd
        acc_ref[0, 0:1, :] += jnp.sum(dh_v, axis=0, keepdims=True)
        acc_ref[0, 1:2, :] += jnp.sum(dh_v * n, axis=0, keepdims=True)
        dn = dh_v * (1.0 + p_ref[0, 1:2, :])
        dx = rstd * (dn - jnp.mean(dn, axis=-1, keepdims=True) - n * jnp.mean(dn * n, axis=-1, keepdims=True))
        gx_ref[...] = dx + gd_ref[...]

    return _call(
        body, (x, ctx, dh, prm, gx_direct), name="modulate_bwd", grid=(n_ct + lat,),
        in_specs=[pl.BlockSpec((tm, dm), lat_tile),
                  pl.BlockSpec((tm, dm), lambda i: (jnp.minimum(i, n_ct - 1), 0)),
                  pl.BlockSpec((tm, dm), lambda i: (jnp.where(is_ctx(i), lat + i, i - n_ct), 0)),
                  pl.BlockSpec((1, 8, dm), lambda i: (cls(i), 0, 0)),
                  pl.BlockSpec((tm, dm), lat_tile)],
        out_specs=(pl.BlockSpec((tm, dm), lat_tile), pl.BlockSpec((1, 8, dm), lambda i: (cls(i), 0, 0))),
        out_shape=(jax.ShapeDtypeStruct((t_rows, dm), F32), jax.ShapeDtypeStruct((2, 8, dm), F32)),
        sem=("arbitrary",), rider=rider)


def _conv_parts(t_rows, c_rows):
    return ((0, t_rows, t_rows // GRID_W, GRID_W), (t_rows, c_rows, 1, c_rows))


def _col_shifts(x2, rows_g, width_g):
    n, ct = x2.shape
    col = lax.broadcasted_iota(jnp.int32, (width_g, ct), 0)
    as_grid = lambda a: a.reshape(rows_g, width_g, ct)
    left = as_grid(pltpu.roll(x2, 1, 0)) * (col >= 1).astype(F32)
    right = as_grid(pltpu.roll(x2, n - 1, 0)) * (col <= width_g - 2).astype(F32)
    return [left, as_grid(x2), right]


CONV_BLOCK_ROWS = 4


def _conv_blocks(t_rows, c_rows):
    for t0, _, rows_g, width_g in _conv_parts(t_rows, c_rows):
        nb = min(CONV_BLOCK_ROWS, rows_g)
        assert rows_g % nb == 0
        for g0 in range(0, rows_g, nb):
            yield t0, rows_g, width_g, g0, nb


def _slab(ref, t0, rows_g, width_g, g0, nb):
    if rows_g == 1:
        return ref[t0:t0 + width_g, :]
    lo, hi = max(g0 - 1, 0), min(g0 + nb + 1, rows_g)
    parts = [ref[t0 + lo * width_g:t0 + hi * width_g, :]]
    zero = jnp.zeros((width_g, ref.shape[1]), F32)
    if g0 == 0:
        parts.insert(0, zero)
    if g0 + nb == rows_g:
        parts.append(zero)
    return jnp.concatenate(parts, axis=0)


def _conv_taps(cols, w_ref, nb, flip):
    one_row = cols[0].shape[0] == nb
    acc = None
    for a in range(3):
        if one_row and a != 1:
            continue
        for b in range(3):
            tap = (2 - a) * 3 + (2 - b) if flip else a * 3 + b
            term = (cols[b] if one_row else cols[b][a:a + nb]) * w_ref[tap:tap + 1, :]
            acc = term if acc is None else acc + term
    return acc


def _conv_fwd(u, conv_w9, conv_b, t_rows, c_rows, w, ct):
    r = u.shape[0]
    base = 5 * w // ct

    def body(x_ref, w_ref, b_ref, o_ref):
        for t0, rows_g, width_g, g0, nb in _conv_blocks(t_rows, c_rows):
            slab = _slab(x_ref, t0, rows_g, width_g, g0, nb)
            cols = _col_shifts(slab, slab.shape[0] // width_g, width_g)
            pre = _conv_taps(cols, w_ref, nb, False) + b_ref[...]
            o_ref[t0 + g0 * width_g:t0 + (g0 + nb) * width_g, :] = _silu(pre).reshape(nb * width_g, ct)

    return pl.pallas_call(
        body, name="conv_fwd", grid=(2 * w // ct,),
        in_specs=[pl.BlockSpec((r, ct), lambda i: (0, base + i)), pl.BlockSpec((9, ct), lambda i: (0, i)),
                  pl.BlockSpec((1, ct), lambda i: (0, i))],
        out_specs=pl.BlockSpec((r, ct), lambda i: (0, i)),
        out_shape=jax.ShapeDtypeStruct((r, 2 * w), F32),
        compiler_params=_params(("parallel",)),
    )(u, conv_w9, conv_b)


def _conv_bwd(u, dqk_pair, conv_w9, conv_b, t_rows, c_rows, w, ct):
    r = u.shape[0]
    base = 5 * w // ct

    def body(x_ref, d1_ref, d2_ref, w_ref, b_ref, dx_ref, dw_ref, db_ref, dpre_ref):
        dw = [jnp.zeros((1, ct), F32) for _ in range(9)]
        db = jnp.zeros((1, ct), F32)
        for t0, rows_g, width_g, g0, nb in _conv_blocks(t_rows, c_rows):
            rows = slice(t0 + g0 * width_g, t0 + (g0 + nb) * width_g)
            slab = _slab(x_ref, t0, rows_g, width_g, g0, nb)
            cols = _col_shifts(slab, slab.shape[0] // width_g, width_g)
            pre = _conv_taps(cols, w_ref, nb, False) + b_ref[...]
            sg = _sigmoid(pre)
            dpre = (d1_ref[rows, :] + d2_ref[rows, :]).reshape(pre.shape) * (sg * (1.0 + pre * (1.0 - sg)))
            dpre_ref[rows, :] = dpre.reshape(nb * width_g, ct)
            db = db + jnp.sum(jnp.sum(dpre, axis=0), axis=0, keepdims=True)
            for a in range(3):
                if rows_g == 1 and a != 1:
                    continue
                for b in range(3):
                    moved = cols[b] if rows_g == 1 else cols[b][a:a + nb]
                    dw[a * 3 + b] = dw[a * 3 + b] + jnp.sum(jnp.sum(moved * dpre, axis=0), axis=0, keepdims=True)
        for t0, rows_g, width_g, g0, nb in _conv_blocks(t_rows, c_rows):
            slab = _slab(dpre_ref, t0, rows_g, width_g, g0, nb)
            cols = _col_shifts(slab, slab.shape[0] // width_g, width_g)
            dx_ref[t0 + g0 * width_g:t0 + (g0 + nb) * width_g, :] = _conv_taps(cols, w_ref, nb, True).reshape(
                nb * width_g, ct)
        for tap in range(9):
            dw_ref[tap:tap + 1, :] = dw[tap]
        db_ref[...] = db

    return pl.pallas_call(
        body, name="conv_bwd", grid=(2 * w // ct,),
        in_specs=[pl.BlockSpec((r, ct), lambda i: (0, base + i)), pl.BlockSpec((r, ct), lambda i: (0, i)),
                  pl.BlockSpec((r, ct), lambda i: (0, i)),
                  pl.BlockSpec((9, ct), lambda i: (0, i)), pl.BlockSpec((1, ct), lambda i: (0, i))],
        out_specs=(pl.BlockSpec((r, ct), lambda i: (0, i)), pl.BlockSpec((9, ct), lambda i: (0, i)),
                   pl.BlockSpec((1, ct), lambda i: (0, i))),
        out_shape=(jax.ShapeDtypeStruct((r, 2 * w), F32), jax.ShapeDtypeStruct((9, 2 * w), F32),
                   jax.ShapeDtypeStruct((1, 2 * w), F32)),
        scratch_shapes=[pltpu.VMEM((r, ct), F32)],
        compiler_params=_params(("parallel",)),
    )(u, dqk_pair[0], dqk_pair[1], conv_w9, conv_b)


def _assemble_du(groups, gates, n_pad, tm):
    flat, layout = [], []
    for entry in list(groups) + [gates]:
        parts = entry if isinstance(entry, (tuple, list)) else (entry,)
        layout.append((len(flat), len(parts), parts[0].shape[1]))
        flat += list(parts)
    r = flat[0].shape[0]

    def body(*refs):
        o_ref = refs[-1]
        col = 0
        for first, count, width in layout:
            val = refs[first][...]
            for extra in range(1, count):
                val = val + refs[first + extra][...]
            o_ref[:, col:col + width] = val.astype(BF16)
            col += width
        assert col == n_pad

    return pl.pallas_call(
        body, name="assemble_du", grid=(r // tm,),
        in_specs=[pl.BlockSpec((tm, a.shape[1]), lambda i: (i, 0)) for a in flat],
        out_specs=pl.BlockSpec((tm, n_pad), lambda i: (i, 0)),
        out_shape=jax.ShapeDtypeStruct((r, n_pad), BF16),
        compiler_params=_params(("parallel",)),
    )(*flat)


def _scan_order(n_lat, n_ctx, rev):
    n = n_lat + n_ctx
    if rev:
        return lambda j: n - 1 - j
    return lambda j: (j + n_lat) % n


DIRS = (False, True)


def _hg_scan_fwd(u, lb_full, w, n_lat, n_ctx, chunk, rider=None):
    r = u.shape[0]
    n_heads = w // HG_DK
    n_chunks = n_lat + n_ctx
    nat = [_scan_order(n_lat, n_ctx, rev) for rev in DIRS]

    def body(*refs):
        ins, outs, scratch = refs[:8], refs[8:12], refs[12:]

        @pl.when(pl.program_id(0) == 0)
        def _():
            for s_ref in scratch:
                s_ref[...] = jnp.zeros_like(s_ref)

        results = []
        for d, rev in enumerate(DIRS):
            aq, af, ai, lb_ref = ins[4 * d:4 * d + 4]
            state = [scratch[d][h] for h in range(n_heads)]
            results.append((state, _hg_chunk(state, aq[...], af[...], ai[...],
                                             lb_ref[0, 0:1, :], lb_ref[0, 1:2, :], rev)))
        for d, (state, (new, o)) in enumerate(results):
            o_ref, save_ref = outs[2 * d:2 * d + 2]
            o_ref[...] = o
            for h in range(n_heads):
                save_ref[0, h] = state[h]
                scratch[d][h] = new[h]

    in_specs, out_specs, out_shape = [], [], []
    for d in range(2):
        in_specs += [pl.BlockSpec((chunk,w), lambda j, d=d: (nat[d](j), 0)),
                     pl.BlockSpec((chunk,w), lambda j, d=d: (nat[d](j), 1 + d)),
                     pl.BlockSpec((chunk,w), lambda j, d=d: (nat[d](j), 3)),
                     pl.BlockSpec((1, 2, w), lambda j, d=d: (d, 0, 0))]
        out_specs += [pl.BlockSpec((chunk,w), lambda j, d=d: (nat[d](j), 0)),
                      pl.BlockSpec((1, n_heads, HG_DK, HG_DK), lambda j: (j, 0, 0, 0))]
        out_shape += [jax.ShapeDtypeStruct((r, w), F32),
                      jax.ShapeDtypeStruct((n_chunks, n_heads, HG_DK, HG_DK), F32)]
    (o_f, s_f, o_b, s_b), rode = _call(
        body, (u, u, u, lb_full, u, u, u, lb_full), name="hg_scan_fwd", grid=(n_chunks,), in_specs=in_specs,
        out_specs=out_specs, out_shape=out_shape, scratch_shapes=[pltpu.VMEM((n_heads, HG_DK, HG_DK), F32)] * 2,
        sem=("arbitrary",), rider=rider)
    return (o_f, o_b), (s_f, s_b), rode


def _hg_scan_bwd(u, lb_full, saved, d_o, w, n_lat, n_ctx, chunk, rider=None):
    r = u.shape[0]
    n_heads = w // HG_DK
    n_chunks = n_lat + n_ctx
    step = lambda jj: n_chunks - 1 - jj
    nat = [(lambda jj, o=_scan_order(n_lat, n_ctx, rev): o(step(jj))) for rev in DIRS]

    def body(*refs):
        ins, outs, scratch = refs[:12], refs[12:20], refs[20:]
        jj = pl.program_id(0)

        @pl.when(jj == 0)
        def _():
            for d in range(2):
                scratch[d][...] = jnp.zeros_like(scratch[d])
                outs[4 * d + 3][...] = jnp.zeros_like(outs[4 * d + 3])

        results = []
        for d, rev in enumerate(DIRS):
            aq, af, ai, lb_ref, save_ref, do_ref = ins[6 * d:6 * d + 6]
            f = lambda st, a, b, c, l0, l1, rev=rev: _hg_chunk(st, a, b, c, l0, l1, rev)
            _, vjp = jax.vjp(f, [save_ref[0, h] for h in range(n_heads)], aq[...], af[...], ai[...],
                             lb_ref[0, 0:1, :], lb_ref[0, 1:2, :])
            d_out = do_ref[...] * (nat[d](jj) < n_lat).astype(F32)
            results.append(vjp(([scratch[d][h] for h in range(n_heads)], d_out)))
        for d, (dst, daq, daf, dai, dl0, dl1) in enumerate(results):
            daq_ref, daf_ref, dai_ref, dlb_ref = outs[4 * d:4 * d + 4]
            for h in range(n_heads):
                scratch[d][h] = dst[h]
            daq_ref[...] = daq
            daf_ref[...] = daf
            dai_ref[...] = dai
            dlb_ref[0:1, :] += dl0
            dlb_ref[1:2, :] += dl1

    in_specs, out_specs, out_shape, operands = [], [], [], []
    for d in range(2):
        row = lambda jj, d=d: (nat[d](jj), 0)
        in_specs += [pl.BlockSpec((chunk,w), row),
                     pl.BlockSpec((chunk,w), lambda jj, d=d: (nat[d](jj), 1 + d)),
                     pl.BlockSpec((chunk,w), lambda jj, d=d: (nat[d](jj), 3)),
                     pl.BlockSpec((1, 2, w), lambda jj, d=d: (d, 0, 0)),
                     pl.BlockSpec((1, n_heads, HG_DK, HG_DK), lambda jj: (step(jj), 0, 0, 0)),
                     pl.BlockSpec((chunk,w), lambda jj, d=d: (jnp.minimum(nat[d](jj), n_lat - 1), 0))]
        operands += [u, u, u, lb_full, saved[d], d_o]
        out_specs += [pl.BlockSpec((chunk,w), row)] * 3 + [pl.BlockSpec((2, w), lambda jj: (0, 0))]
        out_shape += [jax.ShapeDtypeStruct((r, w), F32)] * 3 + [jax.ShapeDtypeStruct((2, w), F32)]
    res, rode = _call(
        body, operands, name="hg_scan_bwd", grid=(n_chunks,), in_specs=in_specs, out_specs=out_specs,
        out_shape=out_shape, scratch_shapes=[pltpu.VMEM((n_heads, HG_DK, HG_DK), F32)] * 2,
        sem=("arbitrary",), rider=rider)
    return res[0:4], res[4:8], rode


def _ml_state_shapes(n_chunks, n_heads, dh):
    return (jax.ShapeDtypeStruct((n_chunks, n_heads, dh, dh), F32),
            jax.ShapeDtypeStruct((n_chunks, n_heads, 1, dh), F32),
            jax.ShapeDtypeStruct((n_chunks, n_heads, 1, LANE), F32))


def _ml_state_specs(n_heads, dh, index):
    return (pl.BlockSpec((1, n_heads, dh, dh), lambda j: (index(j), 0, 0, 0)),
            pl.BlockSpec((1, n_heads, 1, dh), lambda j: (index(j), 0, 0, 0)),
            pl.BlockSpec((1, n_heads, 1, LANE), lambda j: (index(j), 0, 0, 0)))


def _ml_state_scratch(n_heads, dh):
    return [pltpu.VMEM((n_heads, dh, dh), F32), pltpu.VMEM((n_heads, 1, dh), F32), pltpu.VMEM((n_heads, 1, LANE), F32)]


def _ml_scan_fwd(qk, u, gate_b, w, n_heads, n_lat, n_ctx, chunk):
    r = u.shape[0]
    dh = w // n_heads
    n_chunks = n_lat + n_ctx
    nat = [_scan_order(n_lat, n_ctx, rev) for rev in DIRS]

    def body(*refs):
        ins, outs, scratch = refs[:10], refs[10:18], refs[18:]

        @pl.when(pl.program_id(0) == 0)
        def _():
            for s_ref in scratch:
                s_ref[...] = jnp.zeros_like(s_ref)

        results = []
        for d, rev in enumerate(DIRS):
            q, k, v, g, gb = ins[5 * d:5 * d + 5]
            state = tuple([ref[h] for h in range(n_heads)] for ref in scratch[3 * d:3 * d + 3])
            results.append((state, _ml_chunk(state, q[...], k[...], v[...], g[...], gb[...], rev, d)))
        for d, (state, (new, o)) in enumerate(results):
            outs[4 * d][...] = o
            for part in range(3):
                for h in range(n_heads):
                    outs[4 * d + 1 + part][0, h] = state[part][h]
                    scratch[3 * d + part][h] = new[part][h]

    in_specs, out_specs, out_shape = [], [], []
    for d in range(2):
        in_specs += [pl.BlockSpec((chunk,w), lambda j, d=d: (nat[d](j), 0)),
                     pl.BlockSpec((chunk,w), lambda j, d=d: (nat[d](j), 1)),
                     pl.BlockSpec((chunk,w), lambda j, d=d: (nat[d](j), 7)),
                     pl.BlockSpec((chunk,LANE), lambda j, d=d: (nat[d](j), 10 * w // LANE)),
                     pl.BlockSpec((1, LANE), lambda j: (0, 0))]
        out_specs += [pl.BlockSpec((chunk,w), lambda j, d=d: (nat[d](j), 0))]
        out_specs += list(_ml_state_specs(n_heads, dh, lambda j: j))
        out_shape += [jax.ShapeDtypeStruct((r, w), F32)] + list(_ml_state_shapes(n_chunks, n_heads, dh))
    res = pl.pallas_call(
        body, name="ml_scan_fwd", grid=(n_chunks,), in_specs=in_specs, out_specs=tuple(out_specs),
        out_shape=tuple(out_shape), scratch_shapes=_ml_state_scratch(n_heads, dh) * 2,
        compiler_params=_params(("arbitrary",)),
    )(qk, qk, u, u, gate_b, qk, qk, u, u, gate_b)
    return (res[0], res[4]), (res[1:4], res[5:8])


def _ml_scan_bwd(qk, u, gate_b, saved, d_h, w, n_heads, n_lat, n_ctx, chunk, rider=None):
    r = u.shape[0]
    dh = w // n_heads
    n_chunks = n_lat + n_ctx
    step = lambda jj: n_chunks - 1 - jj
    nat = [(lambda jj, o=_scan_order(n_lat, n_ctx, rev): o(step(jj))) for rev in DIRS]

    def body(*refs):
        ins, outs, scratch = refs[:18], refs[18:26], refs[26:]
        jj = pl.program_id(0)

        @pl.when(jj == 0)
        def _():
            for s_ref in scratch:
                s_ref[...] = jnp.zeros_like(s_ref)
            for d in range(2):
                outs[4 * d + 3][...] = jnp.zeros_like(outs[4 * d + 3])

        results = []
        for d, rev in enumerate(DIRS):
            q, k, v, g, gb, sc, sn, sm, dh_ref = ins[9 * d:9 * d + 9]
            state = tuple([ref[0, h] for h in range(n_heads)] for ref in (sc, sn, sm))
            f = lambda st, a, b, c, gg, bb, rev=rev, d=d: _ml_chunk(st, a, b, c, gg, bb, rev, d)
            _, vjp = jax.vjp(f, state, q[...], k[...], v[...], g[...], gb[...])
            d_state = tuple([ref[h] for h in range(n_heads)] for ref in scratch[3 * d:3 * d + 3])
            d_out = dh_ref[...] * (nat[d](jj) < n_lat).astype(F32)
            results.append(vjp((d_state, d_out)))
        for d, (d_state, dq, dk, dv, dg, dgb) in enumerate(results):
            dqk_ref, dv_ref, dg_ref, dgb_ref = outs[4 * d:4 * d + 4]
            for part in range(3):
                for h in range(n_heads):
                    scratch[3 * d + part][h] = d_state[part][h]
            dqk_ref[:, 0:w] = dq
            dqk_ref[:, w:2 * w] = dk
            dv_ref[...] = dv
            dg_ref[...] = dg
            dgb_ref[...] += dgb

    in_specs, out_specs, out_shape, operands = [], [], [], []
    for d in range(2):
        row = lambda jj, d=d: (nat[d](jj), 0)
        in_specs += [pl.BlockSpec((chunk,w), row), pl.BlockSpec((chunk,w), lambda jj, d=d: (nat[d](jj), 1)),
                     pl.BlockSpec((chunk,w), lambda jj, d=d: (nat[d](jj), 7)),
                     pl.BlockSpec((chunk,LANE), lambda jj, d=d: (nat[d](jj), 10 * w // LANE)),
                     pl.BlockSpec((1, LANE), lambda jj: (0, 0))]
        in_specs += list(_ml_state_specs(n_heads, dh, step))
        in_specs += [pl.BlockSpec((chunk,w), lambda jj, d=d: (jnp.minimum(nat[d](jj), n_lat - 1), 0))]
        operands += [qk, qk, u, u, gate_b, *saved[d], d_h]
        out_specs += [pl.BlockSpec((chunk,2 * w), row), pl.BlockSpec((chunk,w), row),
                      pl.BlockSpec((chunk,LANE), row), pl.BlockSpec((1, LANE), lambda jj: (0, 0))]
        out_shape += [jax.ShapeDtypeStruct((r, 2 * w), F32), jax.ShapeDtypeStruct((r, w), F32),
                      jax.ShapeDtypeStruct((r, LANE), F32), jax.ShapeDtypeStruct((1, LANE), F32)]
    res, rode = _call(
        body, operands, name="ml_scan_bwd", grid=(n_chunks,), in_specs=in_specs, out_specs=out_specs,
        out_shape=out_shape, scratch_shapes=_ml_state_scratch(n_heads, dh) * 2, sem=("arbitrary",), rider=rider)
    return res[0:4], res[4:8], rode


def _post_specs(w, tm, lat_tiles, cols):
    return [pl.BlockSpec((tm, w), (lambda i, cb=cb: (jnp.minimum(i, lat_tiles - 1), cb))) for cb in cols]


def _post_fwd(o_f, o_b, h_f, h_b, u, wa, wb, t_rows, w, n_hg, n_ml, tm):
    lat_tiles = t_rows // tm

    def body(of, ob, hf, hb, az, bo, bz, wa_ref, wb_ref, y_ref):
        y_ref[...] = _post_fn(of[...], ob[...], az[...], hf[...], hb[...], bo[...], bz[...],
                              wa_ref[...], wb_ref[...], n_hg, n_ml).astype(BF16)

    rows = pl.BlockSpec((tm, w), lambda i: (i, 0))
    vec = pl.BlockSpec((1, w), lambda i: (0, 0))
    return pl.pallas_call(
        body, name="post_fwd", grid=(lat_tiles,),
        in_specs=[rows] * 4 + _post_specs(w, tm, lat_tiles, (4, 8, 9)) + [vec, vec],
        out_specs=pl.BlockSpec((tm, 2 * w), lambda i: (i, 0)),
        out_shape=jax.ShapeDtypeStruct((t_rows, 2 * w), BF16),
        compiler_params=_params(("parallel",)),
    )(o_f, o_b, h_f, h_b, u, u, u, wa, wb)


def _post_bwd(o_f, o_b, h_f, h_b, u, wa, wb, dy, t_rows, w, n_hg, n_ml, tm, rider=None):
    r = u.shape[0]
    lat_tiles = t_rows // tm
    lat = lambda i: (jnp.minimum(i, lat_tiles - 1), 0)

    def body(of, ob, hf, hb, az, bo, bz, wa_ref, wb_ref, dy_ref, do_ref, dh_ref, daz_ref, dbo_ref, dbz_ref,
             dwa_ref, dwb_ref):
        i = pl.program_id(0)

        @pl.when(i == 0)
        def _():
            dwa_ref[...] = jnp.zeros_like(dwa_ref)
            dwb_ref[...] = jnp.zeros_like(dwb_ref)

        @pl.when(i < lat_tiles)
        def _():
            f = functools.partial(_post_fn, n_hg=n_hg, n_ml=n_ml)
            _, vjp = jax.vjp(f, of[...], ob[...], az[...], hf[...], hb[...], bo[...], bz[...], wa_ref[...], wb_ref[...])
            d_of, _, d_az, d_hf, _, d_bo, d_bz, d_wa, d_wb = vjp(dy_ref[...])
            do_ref[...] = d_of
            dh_ref[...] = d_hf
            daz_ref[...] = d_az
            dbo_ref[...] = d_bo
            dbz_ref[...] = d_bz
            dwa_ref[...] += d_wa
            dwb_ref[...] += d_wb

        @pl.when(i >= lat_tiles)
        def _():
            daz_ref[...] = jnp.zeros_like(daz_ref)
            dbo_ref[...] = jnp.zeros_like(dbo_ref)
            dbz_ref[...] = jnp.zeros_like(dbz_ref)

    lat_rows = pl.BlockSpec((tm, w), lat)
    all_rows = pl.BlockSpec((tm, w), lambda i: (i, 0))
    vec = pl.BlockSpec((1, w), lambda i: (0, 0))
    sd_t = jax.ShapeDtypeStruct((t_rows, w), F32)
    sd_r = jax.ShapeDtypeStruct((r, w), F32)
    sd_v = jax.ShapeDtypeStruct((1, w), F32)
    return _call(
        body, (o_f, o_b, h_f, h_b, u, u, u, wa, wb, dy), name="post_bwd", grid=(r // tm,),
        in_specs=[lat_rows] * 4 + _post_specs(w, tm, lat_tiles, (4, 8, 9)) + [vec, vec]
        + [pl.BlockSpec((tm, 2 * w), lat)],
        out_specs=(lat_rows, lat_rows, all_rows, all_rows, all_rows, vec, vec),
        out_shape=(sd_t, sd_t, sd_r, sd_r, sd_r, sd_v, sd_v), sem=("arbitrary",), rider=rider)


OUT_ROW_GATE, OUT_ROW_LN_G, OUT_ROW_LN_B, OUT_ROW_LOSS = 0, 1, 2, 3


def _out_block(y, w_out, x, target, prm, tm):
    t_rows, dm = x.shape
    di = y.shape[1]

    def body(y_ref, w_ref, x_ref, t_ref, p_ref, dz_ref, dy_ref, gx_ref, acc_ref):
        @pl.when(pl.program_id(0) == 0)
        def _():
            acc_ref[...] = jnp.zeros_like(acc_ref)

        gate, ln_g, ln_b = p_ref[0:1, :], p_ref[1:2, :], p_ref[2:3, :]
        z = _nn(y_ref[...], w_ref[...])
        res = ALPHA * x_ref[...] + gate * z
        mu = jnp.mean(res, axis=-1, keepdims=True)
        rc = res - mu
        rstd = lax.rsqrt(jnp.mean(rc * rc, axis=-1, keepdims=True) + LN_EPS)
        rn = rc * rstd
        err = rn * ln_g + ln_b - t_ref[...]
        d_out = err * (1.0 / dm)
        d_rn = d_out * ln_g
        d_res = rstd * (d_rn - jnp.mean(d_rn, axis=-1, keepdims=True)
                        - rn * jnp.mean(d_rn * rn, axis=-1, keepdims=True))
        acc_ref[OUT_ROW_GATE:OUT_ROW_GATE + 1, :] += jnp.sum(d_res * z, axis=0, keepdims=True)
        acc_ref[OUT_ROW_LN_G:OUT_ROW_LN_G + 1, :] += jnp.sum(d_out * rn, axis=0, keepdims=True)
        acc_ref[OUT_ROW_LN_B:OUT_ROW_LN_B + 1, :] += jnp.sum(d_out, axis=0, keepdims=True)
        acc_ref[OUT_ROW_LOSS:OUT_ROW_LOSS + 1, :] += (0.5 / dm) * jnp.sum(err * err, axis=0, keepdims=True)
        gx_ref[...] = ALPHA * d_res
        dz = (d_res * gate).astype(BF16)
        dz_ref[...] = dz
        dy_ref[...] = _nt(dz, w_ref[...])

    rows_d = pl.BlockSpec((tm, dm), lambda i: (i, 0))
    rows_i = pl.BlockSpec((tm, di), lambda i: (i, 0))
    return pl.pallas_call(
        body, name="out_block", grid=(t_rows // tm,),
        in_specs=[rows_i, pl.BlockSpec((di, dm), lambda i: (0, 0)), rows_d, rows_d,
                  pl.BlockSpec((8, dm), lambda i: (0, 0))],
        out_specs=(rows_d, rows_i, rows_d, pl.BlockSpec((8, dm), lambda i: (0, 0))),
        out_shape=(jax.ShapeDtypeStruct((t_rows, dm), BF16), jax.ShapeDtypeStruct((t_rows, di), F32),
                   jax.ShapeDtypeStruct((t_rows, dm), F32), jax.ShapeDtypeStruct((8, dm), F32)),
        compiler_params=_params(("arbitrary",)),
    )(y, w_out, x, target, prm)


def _mod_fwd(c16, w_mod, tn):
    dm, n = w_mod.shape

    def body(c_ref, w_ref, o_ref, a_ref):
        a = _silu(c_ref[...])
        a_ref[...] = a
        o_ref[...] = _nn(a, w_ref[...], HIGHEST)

    return pl.pallas_call(
        body, name="mod_fwd", grid=(n // tn,),
        in_specs=[pl.BlockSpec((16, dm), lambda j: (0, 0)), pl.BlockSpec((dm, tn), lambda j: (0, j))],
        out_specs=(pl.BlockSpec((16, tn), lambda j: (0, j)), pl.BlockSpec((16, dm), lambda j: (0, 0))),
        out_shape=(jax.ShapeDtypeStruct((16, n), F32), jax.ShapeDtypeStruct((16, dm), F32)),
        compiler_params=_params(("arbitrary",)),
    )(c16, w_mod)


def _mod_bwd(a16, dm16, w_mod, tn):
    dm, n = w_mod.shape

    def body(a_ref, d_ref, w_ref, dw_ref, dc_ref):
        @pl.when(pl.program_id(0) == 0)
        def _():
            dc_ref[...] = jnp.zeros_like(dc_ref)
        dw_ref[...] = _tn(a_ref[...], d_ref[...], HIGHEST)
        dc_ref[...] += _nt(d_ref[...], w_ref[...], HIGHEST)

    return pl.pallas_call(
        body, name="mod_bwd", grid=(n // tn,),
        in_specs=[pl.BlockSpec((16, dm), lambda j: (0, 0)), pl.BlockSpec((16, tn), lambda j: (0, j)),
                  pl.BlockSpec((dm, tn), lambda j: (0, j))],
        out_specs=(pl.BlockSpec((dm, tn), lambda j: (0, j)), pl.BlockSpec((16, dm), lambda j: (0, 0))),
        out_shape=(jax.ShapeDtypeStruct((dm, n), F32), jax.ShapeDtypeStruct((16, dm), F32)),
        compiler_params=_params(("arbitrary",)),
    )(a16, dm16, w_mod)


def _sum_devices(g, fold_rows):
    n_dev, rows, n = g.shape

    def body(g_ref, s_ref, t_ref):
        s = g_ref[0]
        for dev in range(1, n_dev):
            s = s + g_ref[dev]
        t_ref[...] = jnp.broadcast_to(jnp.sum(s, axis=-1, keepdims=True), (rows, LANE))
        s_ref[...] = s
        s_ref[0:fold_rows, :] = s[0:fold_rows] + s[fold_rows:2 * fold_rows]

    return pl.pallas_call(
        body, name="sum_devices",
        out_shape=(jax.ShapeDtypeStruct((rows, n), F32), jax.ShapeDtypeStruct((rows, LANE), F32)),
        compiler_params=_params(),
    )(g)


def _c_ctx_grad(parts, c_ctx_row):
    def body(p_ref, c_ref, o_ref):
        s = p_ref[0]
        for chip in range(1, N_CHIPS):
            s = s + p_ref[2 * chip]
        cv = c_ref[...]
        sg = _sigmoid(cv)
        o_ref[...] = s * (sg * (1.0 + cv * (1.0 - sg)))

    return pl.pallas_call(
        body, name="c_ctx_grad", out_shape=jax.ShapeDtypeStruct(parts.shape[1:], F32), compiler_params=_params(),
    )(parts, c_ctx_row)


def _sum_pair(name, mine, got):
    def body(a_ref, b_ref, o_ref):
        o_ref[...] = (a_ref[...] + b_ref[...]).astype(BF16)

    k, rows, n = mine.shape
    tl = _largest_divisor(n, max(LANE, (1 << 18) // rows), LANE)
    spec = pl.BlockSpec((1, rows, tl), lambda kk, i: (kk, 0, i))
    return pl.pallas_call(
        body, name=name, grid=(k, n // tl), in_specs=[spec, spec], out_specs=spec,
        out_shape=jax.ShapeDtypeStruct(mine.shape, BF16), compiler_params=_params(("parallel", "parallel")),
    )(mine, got)


def _sum_pair_lanes(name, full, got, ci):
    rows, n = got.shape
    tr = _largest_divisor(rows, max(SUBLANE_BF16, (1 << 19) // n), SUBLANE_BF16)

    def body(ci_ref, a_ref, b_ref, o_ref):
        o_ref[...] = (a_ref[...] + b_ref[...].astype(F32)).astype(BF16)

    return pl.pallas_call(
        body, name=name,
        grid_spec=pltpu.PrefetchScalarGridSpec(
            num_scalar_prefetch=1, grid=(rows // tr,),
            in_specs=[pl.BlockSpec((tr, n), lambda i, c: (i, c[0])), pl.BlockSpec((tr, n), lambda i, c: (i, 0))],
            out_specs=pl.BlockSpec((tr, n), lambda i, c: (i, 0))),
        out_shape=jax.ShapeDtypeStruct((rows, n), BF16), compiler_params=_params(("parallel",)),
    )(ci.reshape(1).astype(jnp.int32), full, got)


def _sum_chips(name, got, own, chip):
    k, rows, n = got.shape
    tl = _largest_divisor(n, max(LANE, (1 << 18) // rows), LANE)

    def body(chip_ref, g_ref, own_ref, o_ref):
        total = None
        for kk in range(k):
            term = jnp.where(chip_ref[0] == kk, own_ref[0], g_ref[kk]).astype(F32)
            total = term if total is None else total + term
        o_ref[...] = total

    return pl.pallas_call(
        body, name=name,
        grid_spec=pltpu.PrefetchScalarGridSpec(
            num_scalar_prefetch=1, grid=(n // tl,),
            in_specs=[pl.BlockSpec((k, rows, tl), lambda i, c: (0, 0, i)),
                      pl.BlockSpec((1, rows, tl), lambda i, c: (c[0], 0, i))],
            out_specs=pl.BlockSpec((rows, tl), lambda i, c: (0, i))),
        out_shape=jax.ShapeDtypeStruct((rows, n), F32), compiler_params=_params(("parallel",)),
    )(chip.reshape(1).astype(jnp.int32), got, own)


def _adamw_update(w, g, m, v):
    m2 = ADAM_B1 * m + (1.0 - ADAM_B1) * g
    v2 = ADAM_B2 * v + (1.0 - ADAM_B2) * jnp.square(g)
    m_hat = m2 / (1.0 - ADAM_B1 ** ADAM_STEP)
    v_hat = v2 / (1.0 - ADAM_B2 ** ADAM_STEP)
    return -ADAM_LR * (m_hat / (jnp.sqrt(v_hat) + ADAM_EPS) + ADAM_WD * w), m2, v2


def _adamw(name, w, g, m, v, rider=None):
    rows, n = w.shape
    if rows % 8 == 0:
        tr = _largest_divisor(rows, max(8, (1 << 18) // n), 8)
        block, index, steps = (tr, n), (lambda i: (i, 0)), rows // tr
    else:
        tl = _largest_divisor(n, max(LANE, (1 << 18) // rows), LANE)
        block, index, steps = (rows, tl), (lambda i: (0, i)), n // tl

    def body(w_ref, g_ref, m_ref, v_ref, d_ref, mo_ref, vo_ref):
        d_ref[...], mo_ref[...], vo_ref[...] = _adamw_update(w_ref[...], g_ref[...], m_ref[...], v_ref[...])

    spec = pl.BlockSpec(block, index)
    sds = jax.ShapeDtypeStruct((rows, n), F32)
    return _call(body, (w, g, m, v), name=name, grid=(steps,), in_specs=[spec] * 4, out_specs=(spec,) * 3,
                 out_shape=(sds, sds, sds), sem=("parallel",), rider=rider)


PACK_LANES = 1024


def _pack(pieces):
    flat = jnp.concatenate([p.reshape(-1) for p in pieces])
    total = -(-flat.shape[0] // (8 * PACK_LANES)) * 8 * PACK_LANES
    return jnp.pad(flat, (0, total - flat.shape[0])).reshape(-1, PACK_LANES)


def _unpack(packed, shapes):
    flat = packed.reshape(-1)
    out, off = [], 0
    for shp in shapes:
        size = math.prod(shp)
        out.append(flat[off:off + size].reshape(shp))
        off += size
    return out


def _rows8(rows, width):
    flat = [r.reshape(width) for r in rows] + [jnp.zeros(((8 - len(rows)) * width,), F32)]
    return jnp.concatenate(flat).reshape(8, width)


def kernel(x, c, ctx, c_ctx, w_mod, b_mod, w_in, conv_w, conv_b, hg_lb, ml_gate_b, hg_norm_w, ml_norm_w, w_out, ln_g, ln_b, loss_target, m_c_ctx, m_w_mod, m_b_mod, m_w_in, m_conv_w, m_conv_b, m_hg_lb, m_ml_gate_b, m_hg_norm_w, m_ml_norm_w, m_w_out, m_ln_g, m_ln_b, v_c_ctx, v_w_mod, v_b_mod, v_w_in, v_conv_w, v_conv_b, v_hg_lb, v_ml_gate_b, v_hg_norm_w, v_ml_norm_w, v_w_out, v_ln_g, v_ln_b):
    t_rows, dm = x.shape[1], x.shape[2]
    c_rows = ctx.shape[1]
    w = hg_norm_w.shape[1]
    n_ml = ml_gate_b.shape[-1]
    n_hg = w // HG_DK
    di = 2 * w
    n_in = 10 * w + 4 * n_ml
    ns = w_in.shape[2]
    nm = w_mod.shape[2]
    n_pad = 10 * w + LANE
    r_rows = t_rows + c_rows
    row_gcd = math.gcd(t_rows, c_rows)
    hg_chunk, ml_chunk = math.gcd(HG_CHUNK, row_gcd), math.gcd(ML_CHUNK, row_gcd)
    hg_counts = (t_rows // hg_chunk, c_rows // hg_chunk, hg_chunk)
    ml_counts = (t_rows // ml_chunk, c_rows // ml_chunk, ml_chunk)
    assert ml_norm_w.shape[1] == w and di == dm and N_CHIPS * ns == n_in and N_CHIPS * nm == 3 * dm
    assert w_out.shape[1] * N_CHIPS == di and 4 * n_ml <= LANE and t_rows % GRID_W == 0

    xi, yi, ci = lax.axis_index("x"), lax.axis_index("y"), lax.axis_index("c")
    chip = 2 * xi + yi
    dev = 4 * xi + 2 * yi + ci

    tm = _largest_divisor(math.gcd(t_rows, c_rows), 256, 8)
    tm_mm = _largest_divisor(r_rows, 1088, SUBLANE_BF16)
    tn_mm = LANE * _largest_divisor(n_pad // LANE, 9)
    tn_mod = _largest_divisor(nm, 512, LANE)

    shard_shapes = [(dm,), (2, 2, w // N_CHIPS), (3, 3, di // N_CHIPS)]
    g1 = _all_gather8(_pack([c, hg_lb, conv_w])).run("gather_inputs")[0]
    per_dev = [_unpack(g1[i], shard_shapes) for i in range(N_DEV)]
    c_all = jnp.stack([p[0] for p in per_dev])
    lb_full = jnp.concatenate([per_dev[2 * k][1] for k in range(N_CHIPS)], axis=-1)
    conv_w9 = jnp.concatenate([per_dev[2 * k][2] for k in range(N_CHIPS)], axis=-1).reshape(9, di)

    c16 = jnp.concatenate([c_all, c_ctx[None], jnp.zeros((16 - N_DEV - 1, dm), F32)])
    mod_part, a16 = _mod_fwd(c16, w_mod[0], tn_mod)
    g2 = _all_gather8(mod_part).run("gather_mod")[0]
    mod_all = jnp.concatenate([g2[2 * k] for k in range(N_CHIPS)], axis=1) + b_mod
    mod_x = lax.dynamic_index_in_dim(mod_all, dev, 0, keepdims=False).reshape(3, dm)
    mod_c = mod_all[N_DEV].reshape(3, dm)
    prm = jnp.stack([_rows8(list(mod_x), dm), _rows8(list(mod_c), dm)])

    as_t = lambda a: jnp.transpose(a[0])
    half_in = lax.dynamic_slice_in_dim(as_t(w_in).astype(BF16), ci * (dm // 2), dm // 2, 1)
    half_out = lax.dynamic_slice_in_dim(w_out[0].astype(BF16), ci * (di // (2 * N_CHIPS)), di // (2 * N_CHIPS), 0)
    lanes_of = lambda core: pl.ds(core * (dm // 2), dm // 2)
    landing = lambda s, r: (_chip_of(s), slice(None), lanes_of(s[2]))
    gw_in = _Exchange([half_in], [jax.ShapeDtypeStruct((N_CHIPS, ns, dm), BF16)],
                      [(mask, 0, None, 0, landing) for mask in CHIP_MASKS]).run("gather_w_in")[0]
    gw_in = lax.dynamic_update_slice(gw_in, half_in[None], (chip, 0, ci * (dm // 2)))
    my_lanes = lambda s, r: (slice(None), slice(None), lanes_of(s[2]))
    gw_in = _Exchange([gw_in], [jax.ShapeDtypeStruct(gw_in.shape, BF16)],
                      [(SIBLING_MASK, 0, my_lanes, 0, my_lanes)]).run("gather_w_in_pair", in_place={0: 0})[0]
    wt_full = jnp.concatenate([gw_in.reshape(n_in, dm), jnp.zeros((n_pad - n_in, dm), BF16)])

    hc = _modulate_fwd(x[0], ctx[0], prm, tm)
    u, (got_out,) = _mm_nt("in_proj", hc, wt_full, tm_mm, tn_mm, F32, rider=_all_gather_chips([half_out]))
    fetched_out = _own_block(chip, half_out, got_out)
    (o_f, o_b), hg_saved, (swapped_out,) = _hg_scan_fwd(u, lb_full, w, *hg_counts,
                                                         rider=_sibling_swap([fetched_out]))
    w_out_full = _join_halves(ci, fetched_out, swapped_out, 1).reshape(di, dm)
    qk = _conv_fwd(u, conv_w9, conv_b, t_rows, c_rows, w, LANE)
    gate_b_row = jnp.pad(ml_gate_b.reshape(1, -1), ((0, 0), (0, LANE - 4 * n_ml)))
    (h_f, h_b), ml_saved = _ml_scan_fwd(qk, u, gate_b_row, w, n_ml, *ml_counts)
    y = _post_fwd(o_f, o_b, h_f, h_b, u, hg_norm_w, ml_norm_w, t_rows, w, n_hg, n_ml, tm)
    prm_out = _rows8([mod_x[2], ln_g, ln_b], dm)
    dz, dy, gx_direct, acc_out = _out_block(y, w_out_full, x[0], loss_target[0], prm_out, tm)

    d_w_out = _mm_tn("d_w_out", y, dz, _largest_divisor(di, 1024, LANE),
                     _largest_divisor(t_rows, 1024, SUBLANE_BF16))
    d_w_out4 = d_w_out.reshape(N_CHIPS, 2, di // (2 * N_CHIPS), dm)
    mine_out = lax.dynamic_index_in_dim(d_w_out4, ci, 1, keepdims=False)
    other_out = lax.dynamic_index_in_dim(d_w_out4, 1 - ci, 1, keepdims=False)
    (d_o, d_h, d_az, d_bo, d_bz, d_wa, d_wb), (got_out,) = _post_bwd(
        o_f, o_b, h_f, h_b, u, hg_norm_w, ml_norm_w, dy, t_rows, w, n_hg, n_ml, tm, rider=_sibling_swap([other_out]))
    pair_out = _sum_pair("rs_pair_sum_w_out", mine_out, got_out)
    (d_aq_f, d_aff, d_ai_f, d_lb_f), (d_aq_b, d_afb, d_ai_b, d_lb_b), (landed_out,) = _hg_scan_bwd(
        u, lb_full, hg_saved, d_o, w, *hg_counts, rider=_chip_scatter([pair_out]))
    half_g_out = _sum_chips("rs_chip_sum_w_out", landed_out, pair_out, chip)
    (d_qk_f, d_v_f, d_g_f, d_gb_f), (d_qk_b, d_v_b, d_g_b, d_gb_b), (sibling_out,) = _ml_scan_bwd(
        qk, u, gate_b_row, ml_saved, d_h, w, n_ml, *ml_counts, rider=_sibling_swap([half_g_out]))
    g_w_out = _join_halves(ci, half_g_out, sibling_out, 0)
    d_bqk, d_cw, d_cb = _conv_bwd(u, (d_qk_f, d_qk_b), conv_w9, conv_b, t_rows, c_rows, w, LANE)
    du = _assemble_du([(d_aq_f, d_aq_b), d_aff, d_afb, (d_ai_f, d_ai_b), d_az, d_bqk, (d_v_f, d_v_b), d_bo, d_bz],
                      (d_g_f, d_g_b), n_pad, tm // 2)
    d_wt_in, d_wt_in_bf16 = _mm_tn("d_w_in", du, hc, tn_mm, tm_mm, with_bf16=True)

    got_in = _Exchange([d_wt_in_bf16], [jax.ShapeDtypeStruct((n_pad, dm // 2), BF16)],
                       [(SIBLING_MASK, 0, lambda s, r: (slice(None), lanes_of(r[2])), 0, None)]).run("rs_pair_w_in")[0]
    pair_half = _sum_pair_lanes("rs_pair_sum_w_in", d_wt_in, got_in, ci)
    pair_in = jnp.stack([pair_half[k * ns:(k + 1) * ns] for k in range(N_CHIPS)])
    d_hc, (landed_in,) = _mm_acc("d_h", du, wt_full, tm_mm, tn_mm, rider=_chip_scatter([pair_in]))
    half_g_in = _sum_chips("rs_chip_sum_w_in", landed_in, pair_in, chip)
    g_wt_in = _join_halves(ci, half_g_in, _sibling_swap([half_g_in]).run("rs_join_w_in")[0], 1)
    (gx, acc_mod), _ = _modulate_bwd(x[0], ctx[0], d_hc, prm, gx_direct, tm)
    grad_x = gx[None]

    zero_row = jnp.zeros((dm,), F32)
    d_gb = jnp.concatenate([d_gb_f[:, 0:n_ml], d_gb_b[:, n_ml:2 * n_ml], d_gb_f[:, 2 * n_ml:3 * n_ml],
                            d_gb_b[:, 3 * n_ml:4 * n_ml], jnp.zeros((1, dm - 4 * n_ml), F32)], axis=1)
    rows = [acc_mod[0, 0], acc_mod[0, 1], acc_out[OUT_ROW_GATE],
            acc_mod[1, 0], acc_mod[1, 1], zero_row]
    rows += list(d_cw) + [d_cb[0], d_lb_f.reshape(dm), d_lb_b.reshape(dm),
                          jnp.concatenate([d_wa[0], d_wb[0]]), acc_out[OUT_ROW_LN_G], acc_out[OUT_ROW_LN_B],
                          acc_out[OUT_ROW_LOSS], d_gb[0], zero_row]
    ROW_CW, ROW_CB, ROW_LB, ROW_NORM, ROW_LN_G, ROW_LN_B, ROW_LOSS, ROW_GB = 6, 15, 16, 18, 19, 20, 21, 22
    delta, new_m, new_v = {}, {}, {}
    small_rows = jnp.concatenate([r.reshape(dm) for r in rows]).reshape(len(rows), dm)
    g3 = _all_gather8(small_rows).run("gather_small_grads")[0]
    sums, totals = _sum_devices(g3, 3)
    loss = totals[ROW_LOSS, 0]
    dm16 = jnp.concatenate([g3[:, 0:3, :].reshape(N_DEV, 3 * dm), sums[3:6].reshape(1, 3 * dm),
                            jnp.zeros((16 - N_DEV - 1, 3 * dm), F32)])
    g_w_mod, dc16 = _mod_bwd(a16, lax.dynamic_slice_in_dim(dm16, chip * nm, nm, 1), w_mod[0], tn_mod)
    g4 = _all_gather8(jnp.pad(dc16[N_DEV:N_DEV + 1], ((0, 7), (0, 0)))).run("gather_c_ctx")[0]
    g_c_ctx = _c_ctx_grad(g4, jnp.broadcast_to(c_ctx[None], (8, dm)))[0]
    res, _ = _adamw("adamw_w_in", as_t(w_in), g_wt_in, as_t(m_w_in), as_t(v_w_in))
    delta["w_in"], new_m["w_in"], new_v["w_in"] = (jnp.transpose(a)[None] for a in res)
    res, _ = _adamw("adamw_w_mod", w_mod[0], g_w_mod, m_w_mod[0], v_w_mod[0])
    delta["w_mod"], new_m["w_mod"], new_v["w_mod"] = (a[None] for a in res)

    chip_cols = lambda a, width: lax.dynamic_slice_in_dim(a, chip * width, width, a.ndim - 1)
    grads = {
        "c_ctx": g_c_ctx,
        "w_mod": g_w_mod[None],
        "b_mod": sums[0:3].reshape(1, 3 * dm),
        "w_in": jnp.transpose(g_wt_in)[None],
        "conv_w": chip_cols(sums[ROW_CW:ROW_CW + 9].reshape(1, 3, 3, di), di // N_CHIPS),
        "conv_b": sums[ROW_CB][None],
        "hg_lb": chip_cols(sums[ROW_LB:ROW_LB + 2].reshape(2, 2, w), w // N_CHIPS),
        "ml_gate_b": sums[ROW_GB, 0:4 * n_ml].reshape(1, 4, n_ml),
        "hg_norm_w": sums[ROW_NORM, 0:w][None],
        "ml_norm_w": sums[ROW_NORM, w:2 * w][None],
        "w_out": g_w_out[None],
        "ln_g": sums[ROW_LN_G][None],
        "ln_b": sums[ROW_LN_B][None],
    }
    weights = dict(c_ctx=c_ctx, w_mod=w_mod, b_mod=b_mod, w_in=w_in, conv_w=conv_w, conv_b=conv_b, hg_lb=hg_lb,
                   ml_gate_b=ml_gate_b, hg_norm_w=hg_norm_w, ml_norm_w=ml_norm_w, w_out=w_out, ln_g=ln_g, ln_b=ln_b)
    mom1 = dict(c_ctx=m_c_ctx, w_mod=m_w_mod, b_mod=m_b_mod, w_in=m_w_in, conv_w=m_conv_w, conv_b=m_conv_b,
                hg_lb=m_hg_lb, ml_gate_b=m_ml_gate_b, hg_norm_w=m_hg_norm_w, ml_norm_w=m_ml_norm_w, w_out=m_w_out,
                ln_g=m_ln_g, ln_b=m_ln_b)
    mom2 = dict(c_ctx=v_c_ctx, w_mod=v_w_mod, b_mod=v_b_mod, w_in=v_w_in, conv_w=v_conv_w, conv_b=v_conv_b,
                hg_lb=v_hg_lb, ml_gate_b=v_ml_gate_b, hg_norm_w=v_hg_norm_w, ml_norm_w=v_ml_norm_w, w_out=v_w_out,
                ln_g=v_ln_g, ln_b=v_ln_b)
    names = list(weights)
    big = ("w_mod", "w_in", "w_out")
    small = [n for n in names if n not in big]

    res, _ = _adamw("adamw_w_out", w_out[0], g_w_out, m_w_out[0], v_w_out[0])
    delta["w_out"], new_m["w_out"], new_v["w_out"] = (a[None] for a in res)
    small_shapes = [weights[n].shape for n in small]
    res, _ = _adamw("adamw_small", *(_pack([src[n] for n in small]) for src in (weights, grads, mom1, mom2)))
    for out, packed in zip((delta, new_m, new_v), res):
        for n, a in zip(small, _unpack(packed, small_shapes)):
            out[n] = a

    return (loss, grad_x, *[grads[n].reshape(weights[n].shape) for n in names], *[delta[n] for n in names],
            *[new_m[n] for n in names], *[new_v[n] for n in names])
```

```python
import functools
import math

import jax
import jax.numpy as jnp
from jax import lax
from jax.experimental import pallas as pl
from jax.experimental.pallas import tpu as pltpu

F32 = jnp.float32
BF16 = jnp.bfloat16
HIGHEST = lax.Precision.HIGHEST
MESH = pl.DeviceIdType.MESH

HG_CHUNK = 64
ML_CHUNK = 256
GRID_W = 64
HG_DK = 128
LANE = 128
SUBLANE_BF16 = 16
ALPHA = 2.0 ** 0.25
LN_EPS = 1e-5
NORM_EPS = 1e-6
ADAM_LR = 0.001
ADAM_B1 = 0.9
ADAM_B2 = 0.999
ADAM_EPS = 1e-08
ADAM_WD = 0.01
ADAM_STEP = 10
VMEM_LIMIT = 56 * 1024 * 1024
N_CHIPS = 4
N_DEV = 8


def _params(sem=None):
    return pltpu.CompilerParams(dimension_semantics=sem, vmem_limit_bytes=VMEM_LIMIT)


def _largest_divisor(n, cap, multiple=1):
    best = None
    for d in range(multiple, min(n, cap) + 1, multiple):
        if n % d == 0:
            best = d
    assert best is not None, (n, cap, multiple)
    return best


def _sigmoid(x):
    return jax.nn.sigmoid(x)


def _silu(x):
    return x * jax.nn.sigmoid(x)


def _dot(a, b, dims, precision=None):
    return lax.dot_general(a, b, (dims, ((), ())), precision=precision, preferred_element_type=F32)


def _nn(a, b, precision=None):
    return _dot(a, b, ((1,), (0,)), precision)


def _nt(a, b, precision=None):
    return _dot(a, b, ((1,), (1,)), precision)


def _tn(a, b, precision=None):
    return _dot(a, b, ((0,), (0,)), precision)


def _narrow(x):
    return x.astype(BF16)


@jax.custom_vjp
def _bnn(a, b):
    return _nn(_narrow(a), _narrow(b))


def _bnn_fwd(a, b):
    an, bn = _narrow(a), _narrow(b)
    return _nn(an, bn), (an, bn)


def _bnn_bwd(res, ct):
    an, bn = res
    ctn = _narrow(ct)
    return _nt(ctn, bn), _tn(an, ctn)


_bnn.defvjp(_bnn_fwd, _bnn_bwd)


@jax.custom_vjp
def _bnt(a, b):
    return _nt(_narrow(a), _narrow(b))


def _bnt_fwd(a, b):
    an, bn = _narrow(a), _narrow(b)
    return _nt(an, bn), (an, bn)


def _bnt_bwd(res, ct):
    an, bn = res
    ctn = _narrow(ct)
    return _nn(ctn, bn), _tn(ctn, an)


_bnt.defvjp(_bnt_fwd, _bnt_bwd)


@jax.custom_vjp
def _btn(a, b):
    return _tn(_narrow(a), _narrow(b))


def _btn_fwd(a, b):
    an, bn = _narrow(a), _narrow(b)
    return _tn(an, bn), (an, bn)


def _btn_bwd(res, ct):
    an, bn = res
    ctn = _narrow(ct)
    return _nt(bn, ctn), _nn(an, ctn)


_btn.defvjp(_btn_fwd, _btn_bwd)


def _visible(n, rev):
    r = lax.broadcasted_iota(jnp.int32, (n, n), 0)
    c = lax.broadcasted_iota(jnp.int32, (n, n), 1)
    return (r <= c) if rev else (r >= c)


def _mask_matmul(mask, x):
    mb = mask.astype(BF16)
    hi = x.astype(BF16)
    lo = (x - hi.astype(F32)).astype(BF16)
    return _nn(mb, hi) + _nn(mb, lo)


@functools.partial(jax.custom_vjp, nondiff_argnums=(1,))
def _cumulative(x, rev):
    return _mask_matmul(_visible(x.shape[0], rev), x)


def _cumulative_fwd(x, rev):
    return _cumulative(x, rev), None


def _cumulative_bwd(rev, _, ct):
    return (_mask_matmul(_visible(ct.shape[0], not rev), ct),)


_cumulative.defvjp(_cumulative_fwd, _cumulative_bwd)


def _hg_chunk(states, aq, af, ai, lb0, lb1, rev):
    n_heads = len(states)
    lb = _sigmoid(lb0 - lb1)
    f = lb + (1.0 - lb) * _sigmoid(af)
    g = jnp.log(f)
    k = 1.0 - f
    q = _silu(aq)
    chunk = aq.shape[0]
    vis = _visible(chunk, rev)
    b = _cumulative(g, rev)
    last = 0 if rev else chunk - 1
    b_end = b[last:last + 1]
    b_mid = b[chunk // 2:chunk // 2 + 1]
    q_inter = q * jnp.exp(b)
    q_intra = q * jnp.exp(b - b_mid)
    k_intra = k * jnp.exp(b_mid - b)
    k_dec = k * jnp.exp(b_end - b)
    e_end = jnp.exp(b_end)
    new_states, outs = [], []
    for h in range(n_heads):
        sl = slice(h * HG_DK, (h + 1) * HG_DK)
        s_t = states[h]
        scores = jnp.where(vis, _nt(q_intra[:, sl], k_intra[:, sl]), 0.0)
        outs.append(_nt(q_inter[:, sl], s_t) + _nn(scores, ai[:, sl]))
        new_states.append(e_end[:, sl] * s_t + _tn(ai[:, sl], k_dec[:, sl]))
    return new_states, jnp.concatenate(outs, axis=1)


def _ml_chunk(state, q, k, v, g, gb, rev, d):
    cms, nvs, mbs = state
    n_heads = len(cms)
    dh = q.shape[1] // n_heads
    ga = g + gb
    log_f_all = jax.nn.log_sigmoid(ga)
    chunk = q.shape[0]
    vis = _visible(chunk, rev)
    b_all = _cumulative(log_f_all, rev)
    last = 0 if rev else chunk - 1
    k = k * (dh ** -0.5)
    new_c, new_n, new_m, outs = [], [], [], []
    for h in range(n_heads):
        ci = d * n_heads + h
        cf = (2 + d) * n_heads + h
        sl = slice(h * dh, (h + 1) * dh)
        qh, kh, vh = q[:, sl], k[:, sl], v[:, sl]
        li = ga[:, ci:ci + 1]
        b = b_all[:, cf:cf + 1]
        m = mbs[h][:, 0:1]
        row = jnp.transpose(li - b)
        log_w = jnp.where(vis, b + row, -jnp.inf)
        m_inter = b + m
        m_t = jnp.maximum(m_inter, jnp.max(log_w, axis=-1, keepdims=True))
        w_inter = jnp.exp(m_inter - m_t)
        w_qk = jnp.exp(log_w - m_t) * _bnt(qh, kh)
        num = w_inter * _bnt(qh, cms[h]) + _bnn(w_qk, vh)
        den = w_inter * jnp.sum(qh * nvs[h], axis=-1, keepdims=True) + jnp.sum(w_qk, axis=-1, keepdims=True)
        outs.append(num / jnp.maximum(jnp.abs(den), jnp.exp(-m_t)))
        m_new = m_t[last:last + 1]
        b_end = b[last:last + 1]
        w_s = jnp.exp(b_end - b + li - m_new)
        decay = jnp.exp(b_end + m - m_new)
        new_c.append(decay * cms[h] + _btn(w_s * vh, kh))
        new_n.append(decay * nvs[h] + jnp.sum(w_s * kh, axis=0, keepdims=True))
        new_m.append(jnp.broadcast_to(m_new, (1, LANE)))
    return (new_c, new_n, new_m), jnp.concatenate(outs, axis=1)


def _post_fn(o_f, o_b, az, h_f, h_b, bo, bz, wa, wb, n_hg, n_ml):
    o = o_f + o_b
    parts = []
    for h in range(n_hg):
        s = o[:, h * HG_DK:(h + 1) * HG_DK]
        parts.append(s * lax.rsqrt(jnp.mean(s * s, axis=-1, keepdims=True) + NORM_EPS))
    y_a = jnp.concatenate(parts, axis=1) * wa * _silu(az)
    hh = h_f + h_b
    dh = hh.shape[1] // n_ml
    parts = []
    for h in range(n_ml):
        s = hh[:, h * dh:(h + 1) * dh]
        mu = jnp.mean(s, axis=-1, keepdims=True)
        sc = s - mu
        parts.append(sc * lax.rsqrt(jnp.mean(sc * sc, axis=-1, keepdims=True) + NORM_EPS))
    y_b = jnp.concatenate(parts, axis=1) * wb * _sigmoid(bo) * _silu(bz)
    return jnp.concatenate([y_a, y_b], axis=1)


def _chip_of(dev):
    return 2 * dev[0] + dev[1]


def _index_of(dev):
    return 4 * dev[0] + 2 * dev[1] + dev[2]


class _Exchange:
    def __init__(self, srcs, out_shapes, transfers, local_copies=(), in_place=None):
        self.srcs, self.out_shapes = list(srcs), list(out_shapes)
        self.transfers, self.local_copies = list(transfers), list(local_copies)
        self.in_place = dict(in_place or {})

    def scratch(self):
        return [pltpu.SemaphoreType.DMA((len(self.transfers),)), pltpu.SemaphoreType.DMA((len(self.transfers),)),
                pltpu.SemaphoreType.DMA((max(len(self.local_copies), 1),))]

    def copies(self, ins, outs, send_sems, recv_sems, local_sems):
        me = (lax.axis_index("x"), lax.axis_index("y"), lax.axis_index("c"))

        def pick(ref, fn, *who):
            return ref if fn is None else ref.at[fn(*who)]

        sends, recvs, locs = [], [], []
        for t, (mask, si, sfn, di, dfn) in enumerate(self.transfers):
            peer = tuple(1 - p if flip else p for p, flip in zip(me, mask))
            sends.append(pltpu.make_async_remote_copy(
                src_ref=pick(ins[si], sfn, me, peer), dst_ref=pick(outs[di], dfn, me, peer),
                send_sem=send_sems.at[t], recv_sem=recv_sems.at[t], device_id=peer, device_id_type=MESH))
            landing = pick(outs[di], dfn, peer, me)
            recvs.append(pltpu.make_async_remote_copy(
                src_ref=landing, dst_ref=landing,
                send_sem=send_sems.at[t], recv_sem=recv_sems.at[t], device_id=peer, device_id_type=MESH))
        for l, (si, sfn, di, dfn) in enumerate(self.local_copies):
            locs.append(pltpu.make_async_copy(pick(ins[si], sfn, me), pick(outs[di], dfn, me), local_sems.at[l]))

        def start():
            for cp in locs + sends:
                cp.start()

        def wait():
            for cp in recvs:
                cp.wait_recv()
            for cp in sends:
                cp.wait_send()
            for cp in locs:
                cp.wait()

        return start, wait

    def run(self, name):
        n_in, n_out = len(self.srcs), len(self.out_shapes)

        def body(*refs):
            start, wait = self.copies(refs[:n_in], refs[n_in:n_in + n_out], *refs[n_in + n_out:])
            start()
            wait()

        hbm = pl.BlockSpec(memory_space=pltpu.HBM)
        return pl.pallas_call(
            body, name=name, out_shape=tuple(self.out_shapes), in_specs=[hbm] * n_in,
            out_specs=tuple([hbm] * n_out), scratch_shapes=self.scratch(), input_output_aliases=self.in_place,
        )(*self.srcs)


def _call(body, operands, *, name, grid, in_specs, out_specs, out_shape, scratch_shapes=(), sem=None, rider=None):
    out_specs, out_shape, scratch_shapes = list(out_specs), list(out_shape), list(scratch_shapes)
    if rider is None:
        res = pl.pallas_call(
            body, name=name, grid=grid, in_specs=list(in_specs), out_specs=tuple(out_specs),
            out_shape=tuple(out_shape), scratch_shapes=scratch_shapes, compiler_params=_params(sem),
        )(*operands)
        return list(res), []
    counts = (len(in_specs), len(rider.srcs), len(out_specs), len(rider.out_shapes), len(scratch_shapes), 3)

    def full(*refs):
        groups, pos = [], 0
        for k in counts:
            groups.append(refs[pos:pos + k])
            pos += k
        own_in, ex_in, own_out, ex_out, own_scr, ex_scr = groups
        ids = [pl.program_id(a) for a in range(len(grid))]
        first = functools.reduce(jnp.logical_and, [i == 0 for i in ids])
        last = functools.reduce(jnp.logical_and, [i == g - 1 for i, g in zip(ids, grid)])
        start, wait = rider.copies(ex_in, ex_out, *ex_scr)
        pl.when(first)(start)
        body(*own_in, *own_out, *own_scr)
        pl.when(last)(wait)

    hbm = pl.BlockSpec(memory_space=pltpu.HBM)
    res = pl.pallas_call(
        full, name=name, grid=grid, in_specs=list(in_specs) + [hbm] * counts[1],
        out_specs=tuple(out_specs + [hbm] * counts[3]), out_shape=tuple(out_shape + rider.out_shapes),
        scratch_shapes=scratch_shapes + rider.scratch(), compiler_params=_params(("arbitrary",) * len(grid)),
        input_output_aliases={counts[0] + i: counts[2] + o for i, o in rider.in_place.items()},
    )(*operands, *rider.srcs)
    return list(res[:counts[2]]), list(res[counts[2]:])


ALL_MASKS = [(mx, my, mc) for mx in (0, 1) for my in (0, 1) for mc in (0, 1)][1:]
CHIP_MASKS = [(1, 0, 0), (0, 1, 0), (1, 1, 0)]
SIBLING_MASK = (0, 0, 1)


def _all_gather8(v):
    out = jax.ShapeDtypeStruct((N_DEV,) + v.shape, v.dtype)
    slot = lambda sender, receiver: _index_of(sender)
    transfers = [(mask, 0, None, 0, slot) for mask in ALL_MASKS]
    return _Exchange([v], [out], transfers, [(0, None, 0, lambda me: _index_of(me))])


def _all_gather_chips(arrays):
    outs = [jax.ShapeDtypeStruct((N_CHIPS,) + a.shape, a.dtype) for a in arrays]
    slot = lambda sender, receiver: _chip_of(sender)
    return _Exchange(arrays, outs, [(mask, i, None, i, slot) for i in range(len(arrays)) for mask in CHIP_MASKS])


def _sibling_swap(arrays):
    outs = [jax.ShapeDtypeStruct(a.shape, a.dtype) for a in arrays]
    return _Exchange(arrays, outs, [(SIBLING_MASK, i, None, i, None) for i in range(len(arrays))])


def _chip_scatter(arrays):
    outs = [jax.ShapeDtypeStruct(a.shape, a.dtype) for a in arrays]
    transfers = [(mask, i, lambda s, r: _chip_of(r), i, lambda s, r: _chip_of(s))
                 for i in range(len(arrays)) for mask in CHIP_MASKS]
    return _Exchange(arrays, outs, transfers)


def _own_block(chip, own, blocks):
    sel = (lax.broadcasted_iota(jnp.int32, (N_CHIPS,) + (1,) * (blocks.ndim - 1), 0) == chip)
    return jnp.where(sel, own if own.ndim == blocks.ndim else own[None], blocks)


def _join_halves(ci, mine, other, axis):
    return jnp.where(ci == 0, jnp.concatenate([mine, other], axis=axis), jnp.concatenate([other, mine], axis=axis))


def _mm_nt(name, a, b, tm, tn, out_dtype, rider=None):
    m, k = a.shape
    n = b.shape[0]

    def body(a_ref, b_ref, o_ref):
        o_ref[...] = _nt(a_ref[...], b_ref[...]).astype(out_dtype)

    (out,), rode = _call(
        body, (a, b), name=name, grid=(n // tn, m // tm),
        in_specs=[pl.BlockSpec((tm, k), lambda j, i: (i, 0)), pl.BlockSpec((tn, k), lambda j, i: (j, 0))],
        out_specs=[pl.BlockSpec((tm, tn), lambda j, i: (i, j))],
        out_shape=[jax.ShapeDtypeStruct((m, n), out_dtype)], sem=("parallel", "parallel"), rider=rider)
    return out, rode


def _mm_acc(name, a, b, tm, tk, rider=None):
    m, kc = a.shape
    n = b.shape[1]

    def body(a_ref, b_ref, o_ref):
        @pl.when(pl.program_id(1) == 0)
        def _():
            o_ref[...] = jnp.zeros_like(o_ref)
        o_ref[...] += _nn(a_ref[...], b_ref[...])

    (out,), rode = _call(
        body, (a, b), name=name, grid=(m // tm, kc // tk),
        in_specs=[pl.BlockSpec((tm, tk), lambda i, kk: (i, kk)), pl.BlockSpec((tk, n), lambda i, kk: (kk, 0))],
        out_specs=[pl.BlockSpec((tm, n), lambda i, kk: (i, 0))],
        out_shape=[jax.ShapeDtypeStruct((m, n), F32)], sem=("parallel", "arbitrary"), rider=rider)
    return out, rode


def _mm_tn(name, a, b, tm, tk, with_bf16=False):
    kr, m = a.shape
    n = b.shape[1]
    steps_k = kr // tk

    def body(a_ref, b_ref, o_ref, *narrow):
        @pl.when(pl.program_id(1) == 0)
        def _():
            o_ref[...] = jnp.zeros_like(o_ref)
        o_ref[...] += _tn(a_ref[...], b_ref[...])
        if with_bf16:
            @pl.when(pl.program_id(1) == steps_k - 1)
            def _():
                narrow[0][...] = o_ref[...].astype(BF16)

    out_spec = pl.BlockSpec((tm, n), lambda i, kk: (i, 0))
    res = pl.pallas_call(
        body, name=name, grid=(m // tm, steps_k),
        in_specs=[pl.BlockSpec((tk, tm), lambda i, kk: (kk, i)), pl.BlockSpec((tk, n), lambda i, kk: (kk, 0))],
        out_specs=(out_spec,) * (2 if with_bf16 else 1),
        out_shape=(jax.ShapeDtypeStruct((m, n), F32),) + ((jax.ShapeDtypeStruct((m, n), BF16),) if with_bf16 else ()),
        compiler_params=_params(("parallel", "arbitrary")),
    )(a, b)
    return res if with_bf16 else res[0]


def _modulate_fwd(x, ctx, prm, tm, rider=None):
    t_rows, dm = x.shape
    lat = t_rows // tm
    r = t_rows + ctx.shape[0]

    def body(x_ref, c_ref, p_ref, h_ref):
        xv = jnp.where(pl.program_id(0) >= lat, c_ref[...], x_ref[...])
        mu = jnp.mean(xv, axis=-1, keepdims=True)
        xm = xv - mu
        n = xm * lax.rsqrt(jnp.mean(xm * xm, axis=-1, keepdims=True) + LN_EPS)
        h_ref[...] = (n * (1.0 + p_ref[0, 1:2, :]) + p_ref[0, 0:1, :]).astype(BF16)

    (h,), rode = _call(
        body, (x, ctx, prm), name="modulate_fwd", grid=(r // tm,),
        in_specs=[pl.BlockSpec((tm, dm), lambda i: (jnp.minimum(i, lat - 1), 0)),
                  pl.BlockSpec((tm, dm), lambda i: (jnp.maximum(i - lat, 0), 0)),
                  pl.BlockSpec((1, 8, dm), lambda i: ((i >= lat).astype(jnp.int32), 0, 0))],
        out_specs=[pl.BlockSpec((tm, dm), lambda i: (i, 0))],
        out_shape=[jax.ShapeDtypeStruct((r, dm), BF16)], sem=("parallel",), rider=rider)
    return h, rode


def _modulate_bwd(x, ctx, dh, prm, gx_direct, tm, rider=None):
    t_rows, dm = x.shape
    lat, n_ct = t_rows // tm, ctx.shape[0] // tm
    is_ctx = lambda i: i < n_ct
    cls = lambda i: is_ctx(i).astype(jnp.int32)
    lat_tile = lambda i: (jnp.maximum(i - n_ct, 0), 0)

    def body(x_ref, c_ref, dh_ref, p_ref, gd_ref, gx_ref, acc_ref):
        i = pl.program_id(0)

        @pl.when((i == 0) | (i == n_ct))
        def _():
            acc_ref[...] = jnp.zeros_like(acc_ref)

        x = jnp.where(is_ctx(i), c_ref[...], x_ref[...])
        dh_v = dh_ref[...]
        mu = jnp.mean(x, axis=-1, keepdims=True)
        xm = x - mu
        rstd = lax.rsqrt(jnp.mean(xm * xm, axis=-1, keepdims=True) + LN_EPS)
        n = xm * rstd
        acc_ref[0, 0:1, :] += jnp.sum(dh_v, axis=0, keepdims=True)
        acc_ref[0, 1:2, :] += jnp.sum(dh_v * n, axis=0, keepdims=True)
        dn = dh_v * (1.0 + p_ref[0, 1:2, :])
        dx = rstd * (dn - jnp.mean(dn, axis=-1, keepdims=True) - n * jnp.mean(dn * n, axis=-1, keepdims=True))
        gx_ref[...] = dx + gd_ref[...]

    return _call(
        body, (x, ctx, dh, prm, gx_direct), name="modulate_bwd", grid=(n_ct + lat,),
        in_specs=[pl.BlockSpec((tm, dm), lat_tile),
                  pl.BlockSpec((tm, dm), lambda i: (jnp.minimum(i, n_ct - 1), 0)),
                  pl.BlockSpec((tm, dm), lambda i: (jnp.where(is_ctx(i), lat + i, i - n_ct), 0)),
                  pl.BlockSpec((1, 8, dm), lambda i: (cls(i), 0, 0)),
                  pl.BlockSpec((tm, dm), lat_tile)],
        out_specs=(pl.BlockSpec((tm, dm), lat_tile), pl.BlockSpec((1, 8, dm), lambda i: (cls(i), 0, 0))),
        out_shape=(jax.ShapeDtypeStruct((t_rows, dm), F32), jax.ShapeDtypeStruct((2, 8, dm), F32)),
        sem=("arbitrary",), rider=rider)


def _conv_parts(t_rows, c_rows):
    return ((0, t_rows, t_rows // GRID_W, GRID_W), (t_rows, c_rows, 1, c_rows))


def _col_shifts(x2, rows_g, width_g):
    n, ct = x2.shape
    col = lax.broadcasted_iota(jnp.int32, (width_g, ct), 0)
    as_grid = lambda a: a.reshape(rows_g, width_g, ct)
    left = as_grid(pltpu.roll(x2, 1, 0)) * (col >= 1).astype(F32)
    right = as_grid(pltpu.roll(x2, n - 1, 0)) * (col <= width_g - 2).astype(F32)
    return [left, as_grid(x2), right]


CONV_BLOCK_ROWS = 4


def _conv_blocks(t_rows, c_rows):
    for t0, _, rows_g, width_g in _conv_parts(t_rows, c_rows):
        nb = min(CONV_BLOCK_ROWS, rows_g)
        assert rows_g % nb == 0
        for g0 in range(0, rows_g, nb):
            yield t0, rows_g, width_g, g0, nb


def _slab(ref, t0, rows_g, width_g, g0, nb):
    if rows_g == 1:
        return ref[t0:t0 + width_g, :]
    lo, hi = max(g0 - 1, 0), min(g0 + nb + 1, rows_g)
    parts = [ref[t0 + lo * width_g:t0 + hi * width_g, :]]
    zero = jnp.zeros((width_g, ref.shape[1]), F32)
    if g0 == 0:
        parts.insert(0, zero)
    if g0 + nb == rows_g:
        parts.append(zero)
    return jnp.concatenate(parts, axis=0)


def _conv_taps(cols, w_ref, nb, flip):
    one_row = cols[0].shape[0] == nb
    acc = None
    for a in range(3):
        if one_row and a != 1:
            continue
        for b in range(3):
            tap = (2 - a) * 3 + (2 - b) if flip else a * 3 + b
            term = (cols[b] if one_row else cols[b][a:a + nb]) * w_ref[tap:tap + 1, :]
            acc = term if acc is None else acc + term
    return acc


def _conv_fwd(u, conv_w9, conv_b, t_rows, c_rows, w, ct):
    r = u.shape[0]
    base = 5 * w // ct

    def body(x_ref, w_ref, b_ref, o_ref):
        for t0, rows_g, width_g, g0, nb in _conv_blocks(t_rows, c_rows):
            slab = _slab(x_ref, t0, rows_g, width_g, g0, nb)
            cols = _col_shifts(slab, slab.shape[0] // width_g, width_g)
            pre = _conv_taps(cols, w_ref, nb, False) + b_ref[...]
            o_ref[t0 + g0 * width_g:t0 + (g0 + nb) * width_g, :] = _silu(pre).reshape(nb * width_g, ct)

    return pl.pallas_call(
        body, name="conv_fwd", grid=(2 * w // ct,),
        in_specs=[pl.BlockSpec((r, ct), lambda i: (0, base + i)), pl.BlockSpec((9, ct), lambda i: (0, i)),
                  pl.BlockSpec((1, ct), lambda i: (0, i))],
        out_specs=pl.BlockSpec((r, ct), lambda i: (0, i)),
        out_shape=jax.ShapeDtypeStruct((r, 2 * w), F32),
        compiler_params=_params(("parallel",)),
    )(u, conv_w9, conv_b)


def _conv_bwd(u, dqk_pair, conv_w9, conv_b, t_rows, c_rows, w, ct):
    r = u.shape[0]
    base = 5 * w // ct

    def body(x_ref, d1_ref, d2_ref, w_ref, b_ref, dx_ref, dw_ref, db_ref, dpre_ref):
        dw = [jnp.zeros((1, ct), F32) for _ in range(9)]
        db = jnp.zeros((1, ct), F32)
        for t0, rows_g, width_g, g0, nb in _conv_blocks(t_rows, c_rows):
            rows = slice(t0 + g0 * width_g, t0 + (g0 + nb) * width_g)
            slab = _slab(x_ref, t0, rows_g, width_g, g0, nb)
            cols = _col_shifts(slab, slab.shape[0] // width_g, width_g)
            pre = _conv_taps(cols, w_ref, nb, False) + b_ref[...]
            sg = _sigmoid(pre)
            dpre = (d1_ref[rows, :] + d2_ref[rows, :]).reshape(pre.shape) * (sg * (1.0 + pre * (1.0 - sg)))
            dpre_ref[rows, :] = dpre.reshape(nb * width_g, ct)
            db = db + jnp.sum(jnp.sum(dpre, axis=0), axis=0, keepdims=True)
            for a in range(3):
                if rows_g == 1 and a != 1:
                    continue
                for b in range(3):
                    moved = cols[b] if rows_g == 1 else cols[b][a:a + nb]
                    dw[a * 3 + b] = dw[a * 3 + b] + jnp.sum(jnp.sum(moved * dpre, axis=0), axis=0, keepdims=True)
        for t0, rows_g, width_g, g0, nb in _conv_blocks(t_rows, c_rows):
            slab = _slab(dpre_ref, t0, rows_g, width_g, g0, nb)
            cols = _col_shifts(slab, slab.shape[0] // width_g, width_g)
            dx_ref[t0 + g0 * width_g:t0 + (g0 + nb) * width_g, :] = _conv_taps(cols, w_ref, nb, True).reshape(
                nb * width_g, ct)
        for tap in range(9):
            dw_ref[tap:tap + 1, :] = dw[tap]
        db_ref[...] = db

    return pl.pallas_call(
        body, name="conv_bwd", grid=(2 * w // ct,),
        in_specs=[pl.BlockSpec((r, ct), lambda i: (0, base + i)), pl.BlockSpec((r, ct), lambda i: (0, i)),
                  pl.BlockSpec((r, ct), lambda i: (0, i)),
                  pl.BlockSpec((9, ct), lambda i: (0, i)), pl.BlockSpec((1, ct), lambda i: (0, i))],
        out_specs=(pl.BlockSpec((r, ct), lambda i: (0, i)), pl.BlockSpec((9, ct), lambda i: (0, i)),
                   pl.BlockSpec((1, ct), lambda i: (0, i))),
        out_shape=(jax.ShapeDtypeStruct((r, 2 * w), F32), jax.ShapeDtypeStruct((9, 2 * w), F32),
                   jax.ShapeDtypeStruct((1, 2 * w), F32)),
        scratch_shapes=[pltpu.VMEM((r, ct), F32)],
        compiler_params=_params(("parallel",)),
    )(u, dqk_pair[0], dqk_pair[1], conv_w9, conv_b)


def _assemble_du(groups, gates, n_pad, tm):
    flat, layout = [], []
    for entry in list(groups) + [gates]:
        parts = entry if isinstance(entry, (tuple, list)) else (entry,)
        layout.append((len(flat), len(parts), parts[0].shape[1]))
        flat += list(parts)
    r = flat[0].shape[0]

    def body(*refs):
        o_ref = refs[-1]
        col = 0
        for first, count, width in layout:
            val = refs[first][...]
            for extra in range(1, count):
                val = val + refs[first + extra][...]
            o_ref[:, col:col + width] = val.astype(BF16)
            col += width
        assert col == n_pad

    return pl.pallas_call(
        body, name="assemble_du", grid=(r // tm,),
        in_specs=[pl.BlockSpec((tm, a.shape[1]), lambda i: (i, 0)) for a in flat],
        out_specs=pl.BlockSpec((tm, n_pad), lambda i: (i, 0)),
        out_shape=jax.ShapeDtypeStruct((r, n_pad), BF16),
        compiler_params=_params(("parallel",)),
    )(*flat)


def _scan_order(n_lat, n_ctx, rev):
    n = n_lat + n_ctx
    if rev:
        return lambda j: n - 1 - j
    return lambda j: (j + n_lat) % n


DIRS = (False, True)


def _hg_scan_fwd(u, lb_full, w, n_lat, n_ctx, chunk, rider=None):
    r = u.shape[0]
    n_heads = w // HG_DK
    n_chunks = n_lat + n_ctx
    nat = [_scan_order(n_lat, n_ctx, rev) for rev in DIRS]

    def body(*refs):
        ins, outs, scratch = refs[:8], refs[8:12], refs[12:]

        @pl.when(pl.program_id(0) == 0)
        def _():
            for s_ref in scratch:
                s_ref[...] = jnp.zeros_like(s_ref)

        results = []
        for d, rev in enumerate(DIRS):
            aq, af, ai, lb_ref = ins[4 * d:4 * d + 4]
            state = [scratch[d][h] for h in range(n_heads)]
            results.append((state, _hg_chunk(state, aq[...], af[...], ai[...],
                                             lb_ref[0, 0:1, :], lb_ref[0, 1:2, :], rev)))
        for d, (state, (new, o)) in enumerate(results):
            o_ref, save_ref = outs[2 * d:2 * d + 2]
            o_ref[...] = o
            for h in range(n_heads):
                save_ref[0, h] = state[h]
                scratch[d][h] = new[h]

    in_specs, out_specs, out_shape = [], [], []
    for d in range(2):
        in_specs += [pl.BlockSpec((chunk,w), lambda j, d=d: (nat[d](j), 0)),
                     pl.BlockSpec((chunk,w), lambda j, d=d: (nat[d](j), 1 + d)),
                     pl.BlockSpec((chunk,w), lambda j, d=d: (nat[d](j), 3)),
                     pl.BlockSpec((1, 2, w), lambda j, d=d: (d, 0, 0))]
        out_specs += [pl.BlockSpec((chunk,w), lambda j, d=d: (nat[d](j), 0)),
                      pl.BlockSpec((1, n_heads, HG_DK, HG_DK), lambda j: (j, 0, 0, 0))]
        out_shape += [jax.ShapeDtypeStruct((r, w), F32),
                      jax.ShapeDtypeStruct((n_chunks, n_heads, HG_DK, HG_DK), F32)]
    (o_f, s_f, o_b, s_b), rode = _call(
        body, (u, u, u, lb_full, u, u, u, lb_full), name="hg_scan_fwd", grid=(n_chunks,), in_specs=in_specs,
        out_specs=out_specs, out_shape=out_shape, scratch_shapes=[pltpu.VMEM((n_heads, HG_DK, HG_DK), F32)] * 2,
        sem=("arbitrary",), rider=rider)
    return (o_f, o_b), (s_f, s_b), rode


def _hg_scan_bwd(u, lb_full, saved, d_o, w, n_lat, n_ctx, chunk, rider=None):
    r = u.shape[0]
    n_heads = w // HG_DK
    n_chunks = n_lat + n_ctx
    step = lambda jj: n_chunks - 1 - jj
    nat = [(lambda jj, o=_scan_order(n_lat, n_ctx, rev): o(step(jj))) for rev in DIRS]

    def body(*refs):
        ins, outs, scratch = refs[:12], refs[12:20], refs[20:]
        jj = pl.program_id(0)

        @pl.when(jj == 0)
        def _():
            for d in range(2):
                scratch[d][...] = jnp.zeros_like(scratch[d])
                outs[4 * d + 3][...] = jnp.zeros_like(outs[4 * d + 3])

        results = []
        for d, rev in enumerate(DIRS):
            aq, af, ai, lb_ref, save_ref, do_ref = ins[6 * d:6 * d + 6]
            f = lambda st, a, b, c, l0, l1, rev=rev: _hg_chunk(st, a, b, c, l0, l1, rev)
            _, vjp = jax.vjp(f, [save_ref[0, h] for h in range(n_heads)], aq[...], af[...], ai[...],
                             lb_ref[0, 0:1, :], lb_ref[0, 1:2, :])
            d_out = do_ref[...] * (nat[d](jj) < n_lat).astype(F32)
            results.append(vjp(([scratch[d][h] for h in range(n_heads)], d_out)))
        for d, (dst, daq, daf, dai, dl0, dl1) in enumerate(results):
            daq_ref, daf_ref, dai_ref, dlb_ref = outs[4 * d:4 * d + 4]
            for h in range(n_heads):
                scratch[d][h] = dst[h]
            daq_ref[...] = daq
            daf_ref[...] = daf
            dai_ref[...] = dai
            dlb_ref[0:1, :] += dl0
            dlb_ref[1:2, :] += dl1

    in_specs, out_specs, out_shape, operands = [], [], [], []
    for d in range(2):
        row = lambda jj, d=d: (nat[d](jj), 0)
        in_specs += [pl.BlockSpec((chunk,w), row),
                     pl.BlockSpec((chunk,w), lambda jj, d=d: (nat[d](jj), 1 + d)),
                     pl.BlockSpec((chunk,w), lambda jj, d=d: (nat[d](jj), 3)),
                     pl.BlockSpec((1, 2, w), lambda jj, d=d: (d, 0, 0)),
                     pl.BlockSpec((1, n_heads, HG_DK, HG_DK), lambda jj: (step(jj), 0, 0, 0)),
                     pl.BlockSpec((chunk,w), lambda jj, d=d: (jnp.minimum(nat[d](jj), n_lat - 1), 0))]
        operands += [u, u, u, lb_full, saved[d], d_o]
        out_specs += [pl.BlockSpec((chunk,w), row)] * 3 + [pl.BlockSpec((2, w), lambda jj: (0, 0))]
        out_shape += [jax.ShapeDtypeStruct((r, w), F32)] * 3 + [jax.ShapeDtypeStruct((2, w), F32)]
    res, rode = _call(
        body, operands, name="hg_scan_bwd", grid=(n_chunks,), in_specs=in_specs, out_specs=out_specs,
        out_shape=out_shape, scratch_shapes=[pltpu.VMEM((n_heads, HG_DK, HG_DK), F32)] * 2,
        sem=("arbitrary",), rider=rider)
    return res[0:4], res[4:8], rode


def _ml_state_shapes(n_chunks, n_heads, dh):
    return (jax.ShapeDtypeStruct((n_chunks, n_heads, dh, dh), F32),
            jax.ShapeDtypeStruct((n_chunks, n_heads, 1, dh), F32),
            jax.ShapeDtypeStruct((n_chunks, n_heads, 1, LANE), F32))


def _ml_state_specs(n_heads, dh, index):
    return (pl.BlockSpec((1, n_heads, dh, dh), lambda j: (index(j), 0, 0, 0)),
            pl.BlockSpec((1, n_heads, 1, dh), lambda j: (index(j), 0, 0, 0)),
            pl.BlockSpec((1, n_heads, 1, LANE), lambda j: (index(j), 0, 0, 0)))


def _ml_state_scratch(n_heads, dh):
    return [pltpu.VMEM((n_heads, dh, dh), F32), pltpu.VMEM((n_heads, 1, dh), F32), pltpu.VMEM((n_heads, 1, LANE), F32)]


def _ml_scan_fwd(qk, u, gate_b, w, n_heads, n_lat, n_ctx, chunk):
    r = u.shape[0]
    dh = w // n_heads
    n_chunks = n_lat + n_ctx
    nat = [_scan_order(n_lat, n_ctx, rev) for rev in DIRS]

    def body(*refs):
        ins, outs, scratch = refs[:10], refs[10:18], refs[18:]

        @pl.when(pl.program_id(0) == 0)
        def _():
            for s_ref in scratch:
                s_ref[...] = jnp.zeros_like(s_ref)

        results = []
        for d, rev in enumerate(DIRS):
            q, k, v, g, gb = ins[5 * d:5 * d + 5]
            state = tuple([ref[h] for h in range(n_heads)] for ref in scratch[3 * d:3 * d + 3])
            results.append((state, _ml_chunk(state, q[...], k[...], v[...], g[...], gb[...], rev, d)))
        for d, (state, (new, o)) in enumerate(results):
            outs[4 * d][...] = o
            for part in range(3):
                for h in range(n_heads):
                    outs[4 * d + 1 + part][0, h] = state[part][h]
                    scratch[3 * d + part][h] = new[part][h]

    in_specs, out_specs, out_shape = [], [], []
    for d in range(2):
        in_specs += [pl.BlockSpec((chunk,w), lambda j, d=d: (nat[d](j), 0)),
                     pl.BlockSpec((chunk,w), lambda j, d=d: (nat[d](j), 1)),
                     pl.BlockSpec((chunk,w), lambda j, d=d: (nat[d](j), 7)),
                     pl.BlockSpec((chunk,LANE), lambda j, d=d: (nat[d](j), 10 * w // LANE)),
                     pl.BlockSpec((1, LANE), lambda j: (0, 0))]
        out_specs += [pl.BlockSpec((chunk,w), lambda j, d=d: (nat[d](j), 0))]
        out_specs += list(_ml_state_specs(n_heads, dh, lambda j: j))
        out_shape += [jax.ShapeDtypeStruct((r, w), F32)] + list(_ml_state_shapes(n_chunks, n_heads, dh))
    res = pl.pallas_call(
        body, name="ml_scan_fwd", grid=(n_chunks,), in_specs=in_specs, out_specs=tuple(out_specs),
        out_shape=tuple(out_shape), scratch_shapes=_ml_state_scratch(n_heads, dh) * 2,
        compiler_params=_params(("arbitrary",)),
    )(qk, qk, u, u, gate_b, qk, qk, u, u, gate_b)
    return (res[0], res[4]), (res[1:4], res[5:8])


def _ml_scan_bwd(qk, u, gate_b, saved, d_h, w, n_heads, n_lat, n_ctx, chunk, rider=None):
    r = u.shape[0]
    dh = w // n_heads
    n_chunks = n_lat + n_ctx
    step = lambda jj: n_chunks - 1 - jj
    nat = [(lambda jj, o=_scan_order(n_lat, n_ctx, rev): o(step(jj))) for rev in DIRS]

    def body(*refs):
        ins, outs, scratch = refs[:18], refs[18:26], refs[26:]
        jj = pl.program_id(0)

        @pl.when(jj == 0)
        def _():
            for s_ref in scratch:
                s_ref[...] = jnp.zeros_like(s_ref)
            for d in range(2):
                outs[4 * d + 3][...] = jnp.zeros_like(outs[4 * d + 3])

        results = []
        for d, rev in enumerate(DIRS):
            q, k, v, g, gb, sc, sn, sm, dh_ref = ins[9 * d:9 * d + 9]
            state = tuple([ref[0, h] for h in range(n_heads)] for ref in (sc, sn, sm))
            f = lambda st, a, b, c, gg, bb, rev=rev, d=d: _ml_chunk(st, a, b, c, gg, bb, rev, d)
            _, vjp = jax.vjp(f, state, q[...], k[...], v[...], g[...], gb[...])
            d_state = tuple([ref[h] for h in range(n_heads)] for ref in scratch[3 * d:3 * d + 3])
            d_out = dh_ref[...] * (nat[d](jj) < n_lat).astype(F32)
            results.append(vjp((d_state, d_out)))
        for d, (d_state, dq, dk, dv, dg, dgb) in enumerate(results):
            dqk_ref, dv_ref, dg_ref, dgb_ref = outs[4 * d:4 * d + 4]
            for part in range(3):
                for h in range(n_heads):
                    scratch[3 * d + part][h] = d_state[part][h]
            dqk_ref[:, 0:w] = dq
            dqk_ref[:, w:2 * w] = dk
            dv_ref[...] = dv
            dg_ref[...] = dg
            dgb_ref[...] += dgb

    in_specs, out_specs, out_shape, operands = [], [], [], []
    for d in range(2):
        row = lambda jj, d=d: (nat[d](jj), 0)
        in_specs += [pl.BlockSpec((chunk,w), row), pl.BlockSpec((chunk,w), lambda jj, d=d: (nat[d](jj), 1)),
                     pl.BlockSpec((chunk,w), lambda jj, d=d: (nat[d](jj), 7)),
                     pl.BlockSpec((chunk,LANE), lambda jj, d=d: (nat[d](jj), 10 * w // LANE)),
                     pl.BlockSpec((1, LANE), lambda jj: (0, 0))]
        in_specs += list(_ml_state_specs(n_heads, dh, step))
        in_specs += [pl.BlockSpec((chunk,w), lambda jj, d=d: (jnp.minimum(nat[d](jj), n_lat - 1), 0))]
        operands += [qk, qk, u, u, gate_b, *saved[d], d_h]
        out_specs += [pl.BlockSpec((chunk,2 * w), row), pl.BlockSpec((chunk,w), row),
                      pl.BlockSpec((chunk,LANE), row), pl.BlockSpec((1, LANE), lambda jj: (0, 0))]
        out_shape += [jax.ShapeDtypeStruct((r, 2 * w), F32), jax.ShapeDtypeStruct((r, w), F32),
                      jax.ShapeDtypeStruct((r, LANE), F32), jax.ShapeDtypeStruct((1, LANE), F32)]
    res, rode = _call(
        body, operands, name="ml_scan_bwd", grid=(n_chunks,), in_specs=in_specs, out_specs=out_specs,
        out_shape=out_shape, scratch_shapes=_ml_state_scratch(n_heads, dh) * 2, sem=("arbitrary",), rider=rider)
    return res[0:4], res[4:8], rode


def _post_specs(w, tm, lat_tiles, cols):
    return [pl.BlockSpec((tm, w), (lambda i, cb=cb: (jnp.minimum(i, lat_tiles - 1), cb))) for cb in cols]


def _post_fwd(o_f, o_b, h_f, h_b, u, wa, wb, t_rows, w, n_hg, n_ml, tm):
    lat_tiles = t_rows // tm

    def body(of, ob, hf, hb, az, bo, bz, wa_ref, wb_ref, y_ref):
        y_ref[...] = _post_fn(of[...], ob[...], az[...], hf[...], hb[...], bo[...], bz[...],
                              wa_ref[...], wb_ref[...], n_hg, n_ml).astype(BF16)

    rows = pl.BlockSpec((tm, w), lambda i: (i, 0))
    vec = pl.BlockSpec((1, w), lambda i: (0, 0))
    return pl.pallas_call(
        body, name="post_fwd", grid=(lat_tiles,),
        in_specs=[rows] * 4 + _post_specs(w, tm, lat_tiles, (4, 8, 9)) + [vec, vec],
        out_specs=pl.BlockSpec((tm, 2 * w), lambda i: (i, 0)),
        out_shape=jax.ShapeDtypeStruct((t_rows, 2 * w), BF16),
        compiler_params=_params(("parallel",)),
    )(o_f, o_b, h_f, h_b, u, u, u, wa, wb)


def _post_bwd(o_f, o_b, h_f, h_b, u, wa, wb, dy, t_rows, w, n_hg, n_ml, tm, rider=None):
    r = u.shape[0]
    lat_tiles = t_rows // tm
    lat = lambda i: (jnp.minimum(i, lat_tiles - 1), 0)

    def body(of, ob, hf, hb, az, bo, bz, wa_ref, wb_ref, dy_ref, do_ref, dh_ref, daz_ref, dbo_ref, dbz_ref,
             dwa_ref, dwb_ref):
        i = pl.program_id(0)

        @pl.when(i == 0)
        def _():
            dwa_ref[...] = jnp.zeros_like(dwa_ref)
            dwb_ref[...] = jnp.zeros_like(dwb_ref)

        @pl.when(i < lat_tiles)
        def _():
            f = functools.partial(_post_fn, n_hg=n_hg, n_ml=n_ml)
            _, vjp = jax.vjp(f, of[...], ob[...], az[...], hf[...], hb[...], bo[...], bz[...], wa_ref[...], wb_ref[...])
            d_of, _, d_az, d_hf, _, d_bo, d_bz, d_wa, d_wb = vjp(dy_ref[...])
            do_ref[...] = d_of
            dh_ref[...] = d_hf
            daz_ref[...] = d_az
            dbo_ref[...] = d_bo
            dbz_ref[...] = d_bz
            dwa_ref[...] += d_wa
            dwb_ref[...] += d_wb

        @pl.when(i >= lat_tiles)
        def _():
            daz_ref[...] = jnp.zeros_like(daz_ref)
            dbo_ref[...] = jnp.zeros_like(dbo_ref)
            dbz_ref[...] = jnp.zeros_like(dbz_ref)

    lat_rows = pl.BlockSpec((tm, w), lat)
    all_rows = pl.BlockSpec((tm, w), lambda i: (i, 0))
    vec = pl.BlockSpec((1, w), lambda i: (0, 0))
    sd_t = jax.ShapeDtypeStruct((t_rows, w), F32)
    sd_r = jax.ShapeDtypeStruct((r, w), F32)
    sd_v = jax.ShapeDtypeStruct((1, w), F32)
    return _call(
        body, (o_f, o_b, h_f, h_b, u, u, u, wa, wb, dy), name="post_bwd", grid=(r // tm,),
        in_specs=[lat_rows] * 4 + _post_specs(w, tm, lat_tiles, (4, 8, 9)) + [vec, vec]
        + [pl.BlockSpec((tm, 2 * w), lat)],
        out_specs=(lat_rows, lat_rows, all_rows, all_rows, all_rows, vec, vec),
        out_shape=(sd_t, sd_t, sd_r, sd_r, sd_r, sd_v, sd_v), sem=("arbitrary",), rider=rider)


OUT_ROW_GATE, OUT_ROW_LN_G, OUT_ROW_LN_B, OUT_ROW_LOSS = 0, 1, 2, 3


def _out_block(y, w_out, x, target, prm, tm):
    t_rows, dm = x.shape
    di = y.shape[1]

    def body(y_ref, w_ref, x_ref, t_ref, p_ref, dz_ref, dy_ref, gx_ref, acc_ref):
        @pl.when(pl.program_id(0) == 0)
        def _():
            acc_ref[...] = jnp.zeros_like(acc_ref)

        gate, ln_g, ln_b = p_ref[0:1, :], p_ref[1:2, :], p_ref[2:3, :]
        z = _nn(y_ref[...], w_ref[...])
        res = ALPHA * x_ref[...] + gate * z
        mu = jnp.mean(res, axis=-1, keepdims=True)
        rc = res - mu
        rstd = lax.rsqrt(jnp.mean(rc * rc, axis=-1, keepdims=True) + LN_EPS)
        rn = rc * rstd
        err = rn * ln_g + ln_b - t_ref[...]
        d_out = err * (1.0 / dm)
        d_rn = d_out * ln_g
        d_res = rstd * (d_rn - jnp.mean(d_rn, axis=-1, keepdims=True)
                        - rn * jnp.mean(d_rn * rn, axis=-1, keepdims=True))
        acc_ref[OUT_ROW_GATE:OUT_ROW_GATE + 1, :] += jnp.sum(d_res * z, axis=0, keepdims=True)
        acc_ref[OUT_ROW_LN_G:OUT_ROW_LN_G + 1, :] += jnp.sum(d_out * rn, axis=0, keepdims=True)
        acc_ref[OUT_ROW_LN_B:OUT_ROW_LN_B + 1, :] += jnp.sum(d_out, axis=0, keepdims=True)
        acc_ref[OUT_ROW_LOSS:OUT_ROW_LOSS + 1, :] += (0.5 / dm) * jnp.sum(err * err, axis=0, keepdims=True)
        gx_ref[...] = ALPHA * d_res
        dz = (d_res * gate).astype(BF16)
        dz_ref[...] = dz
        dy_ref[...] = _nt(dz, w_ref[...])

    rows_d = pl.BlockSpec((tm, dm), lambda i: (i, 0))
    rows_i = pl.BlockSpec((tm, di), lambda i: (i, 0))
    return pl.pallas_call(
        body, name="out_block", grid=(t_rows // tm,),
        in_specs=[rows_i, pl.BlockSpec((di, dm), lambda i: (0, 0)), rows_d, rows_d,
                  pl.BlockSpec((8, dm), lambda i: (0, 0))],
        out_specs=(rows_d, rows_i, rows_d, pl.BlockSpec((8, dm), lambda i: (0, 0))),
        out_shape=(jax.ShapeDtypeStruct((t_rows, dm), BF16), jax.ShapeDtypeStruct((t_rows, di), F32),
                   jax.ShapeDtypeStruct((t_rows, dm), F32), jax.ShapeDtypeStruct((8, dm), F32)),
        compiler_params=_params(("arbitrary",)),
    )(y, w_out, x, target, prm)


def _mod_fwd(c16, w_mod, tn):
    dm, n = w_mod.shape

    def body(c_ref, w_ref, o_ref, a_ref):
        a = _silu(c_ref[...])
        a_ref[...] = a
        o_ref[...] = _nn(a, w_ref[...], HIGHEST)

    return pl.pallas_call(
        body, name="mod_fwd", grid=(n // tn,),
        in_specs=[pl.BlockSpec((16, dm), lambda j: (0, 0)), pl.BlockSpec((dm, tn), lambda j: (0, j))],
        out_specs=(pl.BlockSpec((16, tn), lambda j: (0, j)), pl.BlockSpec((16, dm), lambda j: (0, 0))),
        out_shape=(jax.ShapeDtypeStruct((16, n), F32), jax.ShapeDtypeStruct((16, dm), F32)),
        compiler_params=_params(("arbitrary",)),
    )(c16, w_mod)


def _mod_bwd(a16, dm16, w_mod, tn):
    dm, n = w_mod.shape

    def body(a_ref, d_ref, w_ref, dw_ref, dc_ref):
        @pl.when(pl.program_id(0) == 0)
        def _():
            dc_ref[...] = jnp.zeros_like(dc_ref)
        dw_ref[...] = _tn(a_ref[...], d_ref[...], HIGHEST)
        dc_ref[...] += _nt(d_ref[...], w_ref[...], HIGHEST)

    return pl.pallas_call(
        body, name="mod_bwd", grid=(n // tn,),
        in_specs=[pl.BlockSpec((16, dm), lambda j: (0, 0)), pl.BlockSpec((16, tn), lambda j: (0, j)),
                  pl.BlockSpec((dm, tn), lambda j: (0, j))],
        out_specs=(pl.BlockSpec((dm, tn), lambda j: (0, j)), pl.BlockSpec((16, dm), lambda j: (0, 0))),
        out_shape=(jax.ShapeDtypeStruct((dm, n), F32), jax.ShapeDtypeStruct((16, dm), F32)),
        compiler_params=_params(("arbitrary",)),
    )(a16, dm16, w_mod)


def _sum_devices(g, fold_rows):
    n_dev, rows, n = g.shape

    def body(g_ref, s_ref, t_ref):
        s = g_ref[0]
        for dev in range(1, n_dev):
            s = s + g_ref[dev]
        t_ref[...] = jnp.broadcast_to(jnp.sum(s, axis=-1, keepdims=True), (rows, LANE))
        s_ref[...] = s
        s_ref[0:fold_rows, :] = s[0:fold_rows] + s[fold_rows:2 * fold_rows]

    return pl.pallas_call(
        body, name="sum_devices",
        out_shape=(jax.ShapeDtypeStruct((rows, n), F32), jax.ShapeDtypeStruct((rows, LANE), F32)),
        compiler_params=_params(),
    )(g)


def _c_ctx_grad(parts, c_ctx_row):
    def body(p_ref, c_ref, o_ref):
        s = p_ref[0]
        for chip in range(1, N_CHIPS):
            s = s + p_ref[2 * chip]
        cv = c_ref[...]
        sg = _sigmoid(cv)
        o_ref[...] = s * (sg * (1.0 + cv * (1.0 - sg)))

    return pl.pallas_call(
        body, name="c_ctx_grad", out_shape=jax.ShapeDtypeStruct(parts.shape[1:], F32), compiler_params=_params(),
    )(parts, c_ctx_row)


def _sum_pair(name, mine, got):
    def body(a_ref, b_ref, o_ref):
        o_ref[...] = (a_ref[...] + b_ref[...]).astype(BF16)

    k, rows, n = mine.shape
    tl = _largest_divisor(n, max(LANE, (1 << 18) // rows), LANE)
    spec = pl.BlockSpec((1, rows, tl), lambda kk, i: (kk, 0, i))
    return pl.pallas_call(
        body, name=name, grid=(k, n // tl), in_specs=[spec, spec], out_specs=spec,
        out_shape=jax.ShapeDtypeStruct(mine.shape, BF16), compiler_params=_params(("parallel", "parallel")),
    )(mine, got)


def _sum_pair_lanes(name, full, got, ci):
    rows, n = got.shape
    tr = _largest_divisor(rows, max(SUBLANE_BF16, (1 << 19) // n), SUBLANE_BF16)

    def body(ci_ref, a_ref, b_ref, o_ref):
        o_ref[...] = (a_ref[...] + b_ref[...].astype(F32)).astype(BF16)

    return pl.pallas_call(
        body, name=name,
        grid_spec=pltpu.PrefetchScalarGridSpec(
            num_scalar_prefetch=1, grid=(rows // tr,),
            in_specs=[pl.BlockSpec((tr, n), lambda i, c: (i, c[0])), pl.BlockSpec((tr, n), lambda i, c: (i, 0))],
            out_specs=pl.BlockSpec((tr, n), lambda i, c: (i, 0))),
        out_shape=jax.ShapeDtypeStruct((rows, n), BF16), compiler_params=_params(("parallel",)),
    )(ci.reshape(1).astype(jnp.int32), full, got)


def _sum_chips(name, got, own, chip):
    k, rows, n = got.shape
    tl = _largest_divisor(n, max(LANE, (1 << 18) // rows), LANE)

    def body(chip_ref, g_ref, own_ref, o_ref):
        total = None
        for kk in range(k):
            term = jnp.where(chip_ref[0] == kk, own_ref[0], g_ref[kk]).astype(F32)
            total = term if total is None else total + term
        o_ref[...] = total

    return pl.pallas_call(
        body, name=name,
        grid_spec=pltpu.PrefetchScalarGridSpec(
            num_scalar_prefetch=1, grid=(n // tl,),
            in_specs=[pl.BlockSpec((k, rows, tl), lambda i, c: (0, 0, i)),
                      pl.BlockSpec((1, rows, tl), lambda i, c: (c[0], 0, i))],
            out_specs=pl.BlockSpec((rows, tl), lambda i, c: (0, i))),
        out_shape=jax.ShapeDtypeStruct((rows, n), F32), compiler_params=_params(("parallel",)),
    )(chip.reshape(1).astype(jnp.int32), got, own)


def _adamw_update(w, g, m, v):
    m2 = ADAM_B1 * m + (1.0 - ADAM_B1) * g
    v2 = ADAM_B2 * v + (1.0 - ADAM_B2) * jnp.square(g)
    m_hat = m2 / (1.0 - ADAM_B1 ** ADAM_STEP)
    v_hat = v2 / (1.0 - ADAM_B2 ** ADAM_STEP)
    return -ADAM_LR * (m_hat / (jnp.sqrt(v_hat) + ADAM_EPS) + ADAM_WD * w), m2, v2


def _adamw(name, w, g, m, v, rider=None):
    rows, n = w.shape
    if rows % 8 == 0:
        tr = _largest_divisor(rows, max(8, (1 << 18) // n), 8)
        block, index, steps = (tr, n), (lambda i: (i, 0)), rows // tr
    else:
        tl = _largest_divisor(n, max(LANE, (1 << 18) // rows), LANE)
        block, index, steps = (rows, tl), (lambda i: (0, i)), n // tl

    def body(w_ref, g_ref, m_ref, v_ref, d_ref, mo_ref, vo_ref):
        d_ref[...], mo_ref[...], vo_ref[...] = _adamw_update(w_ref[...], g_ref[...], m_ref[...], v_ref[...])

    spec = pl.BlockSpec(block, index)
    sds = jax.ShapeDtypeStruct((rows, n), F32)
    return _call(body, (w, g, m, v), name=name, grid=(steps,), in_specs=[spec] * 4, out_specs=(spec,) * 3,
                 out_shape=(sds, sds, sds), sem=("parallel",), rider=rider)


PACK_LANES = 1024


def _pack(pieces):
    flat = jnp.concatenate([p.reshape(-1) for p in pieces])
    total = -(-flat.shape[0] // (8 * PACK_LANES)) * 8 * PACK_LANES
    return jnp.pad(flat, (0, total - flat.shape[0])).reshape(-1, PACK_LANES)


def _unpack(packed, shapes):
    flat = packed.reshape(-1)
    out, off = [], 0
    for shp in shapes:
        size = math.prod(shp)
        out.append(flat[off:off + size].reshape(shp))
        off += size
    return out


def _rows8(rows, width):
    flat = [r.reshape(width) for r in rows] + [jnp.zeros(((8 - len(rows)) * width,), F32)]
    return jnp.concatenate(flat).reshape(8, width)


def kernel(x, c, ctx, c_ctx, w_mod, b_mod, w_in, conv_w, conv_b, hg_lb, ml_gate_b, hg_norm_w, ml_norm_w, w_out, ln_g, ln_b, loss_target, m_c_ctx, m_w_mod, m_b_mod, m_w_in, m_conv_w, m_conv_b, m_hg_lb, m_ml_gate_b, m_hg_norm_w, m_ml_norm_w, m_w_out, m_ln_g, m_ln_b, v_c_ctx, v_w_mod, v_b_mod, v_w_in, v_conv_w, v_conv_b, v_hg_lb, v_ml_gate_b, v_hg_norm_w, v_ml_norm_w, v_w_out, v_ln_g, v_ln_b):
    t_rows, dm = x.shape[1], x.shape[2]
    c_rows = ctx.shape[1]
    w = hg_norm_w.shape[1]
    n_ml = ml_gate_b.shape[-1]
    n_hg = w // HG_DK
    di = 2 * w
    n_in = 10 * w + 4 * n_ml
    ns = w_in.shape[2]
    nm = w_mod.shape[2]
    n_pad = 10 * w + LANE
    r_rows = t_rows + c_rows
    row_gcd = math.gcd(t_rows, c_rows)
    hg_chunk, ml_chunk = math.gcd(HG_CHUNK, row_gcd), math.gcd(ML_CHUNK, row_gcd)
    hg_counts = (t_rows // hg_chunk, c_rows // hg_chunk, hg_chunk)
    ml_counts = (t_rows // ml_chunk, c_rows // ml_chunk, ml_chunk)
    assert ml_norm_w.shape[1] == w and di == dm and N_CHIPS * ns == n_in and N_CHIPS * nm == 3 * dm
    assert w_out.shape[1] * N_CHIPS == di and 4 * n_ml <= LANE and t_rows % GRID_W == 0

    xi, yi, ci = lax.axis_index("x"), lax.axis_index("y"), lax.axis_index("c")
    chip = 2 * xi + yi
    dev = 4 * xi + 2 * yi + ci

    tm = _largest_divisor(math.gcd(t_rows, c_rows), 256, 8)
    tm_mm = _largest_divisor(r_rows, 1088, SUBLANE_BF16)
    tn_mm = LANE * _largest_divisor(n_pad // LANE, 9)
    tn_mod = _largest_divisor(nm, 512, LANE)

    shard_shapes = [(dm,), (2, 2, w // N_CHIPS), (3, 3, di // N_CHIPS)]
    g1 = _all_gather8(_pack([c, hg_lb, conv_w])).run("gather_inputs")[0]
    per_dev = [_unpack(g1[i], shard_shapes) for i in range(N_DEV)]
    c_all = jnp.stack([p[0] for p in per_dev])
    lb_full = jnp.concatenate([per_dev[2 * k][1] for k in range(N_CHIPS)], axis=-1)
    conv_w9 = jnp.concatenate([per_dev[2 * k][2] for k in range(N_CHIPS)], axis=-1).reshape(9, di)

    c16 = jnp.concatenate([c_all, c_ctx[None], jnp.zeros((16 - N_DEV - 1, dm), F32)])
    mod_part, a16 = _mod_fwd(c16, w_mod[0], tn_mod)
    g2 = _all_gather8(mod_part).run("gather_mod")[0]
    mod_all = jnp.concatenate([g2[2 * k] for k in range(N_CHIPS)], axis=1) + b_mod
    mod_x = lax.dynamic_index_in_dim(mod_all, dev, 0, keepdims=False).reshape(3, dm)
    mod_c = mod_all[N_DEV].reshape(3, dm)
    prm = jnp.stack([_rows8(list(mod_x), dm), _rows8(list(mod_c), dm)])

    as_t = lambda a: jnp.transpose(a[0])
    half_in = lax.dynamic_slice_in_dim(as_t(w_in).astype(BF16), ci * (dm // 2), dm // 2, 1)
    half_out = lax.dynamic_slice_in_dim(w_out[0].astype(BF16), ci * (di // (2 * N_CHIPS)), di // (2 * N_CHIPS), 0)
    lanes_of = lambda core: pl.ds(core * (dm // 2), dm // 2)
    landing = lambda s, r: (_chip_of(s), slice(None), lanes_of(s[2]))
    own_placed = lax.dynamic_update_slice(jnp.zeros((N_CHIPS, ns, dm), BF16), as_t(w_in).astype(BF16)[None],
                                          (chip, 0, 0))
    gather_in = _Exchange([half_in, own_placed], [jax.ShapeDtypeStruct(own_placed.shape, BF16)],
                          [(mask, 0, None, 0, landing) for mask in CHIP_MASKS], in_place={1: 0})

    hc, (gw_in,) = _modulate_fwd(x[0], ctx[0], prm, tm, rider=gather_in)
    my_lanes = lambda s, r: (slice(None), slice(None), lanes_of(s[2]))
    gw_in = _Exchange([gw_in], [jax.ShapeDtypeStruct(gw_in.shape, BF16)],
                      [(SIBLING_MASK, 0, my_lanes, 0, my_lanes)], in_place={0: 0}).run("gather_w_in_pair")[0]
    wt_full = jnp.concatenate([gw_in.reshape(n_in, dm), jnp.zeros((n_pad - n_in, dm), BF16)])
    u, (got_out,) = _mm_nt("in_proj", hc, wt_full, tm_mm, tn_mm, F32, rider=_all_gather_chips([half_out]))
    fetched_out = _own_block(chip, half_out, got_out)
    (o_f, o_b), hg_saved, (swapped_out,) = _hg_scan_fwd(u, lb_full, w, *hg_counts,
                                                         rider=_sibling_swap([fetched_out]))
    w_out_full = _join_halves(ci, fetched_out, swapped_out, 1).reshape(di, dm)
    qk = _conv_fwd(u, conv_w9, conv_b, t_rows, c_rows, w, LANE)
    gate_b_row = jnp.pad(ml_gate_b.reshape(1, -1), ((0, 0), (0, LANE - 4 * n_ml)))
    (h_f, h_b), ml_saved = _ml_scan_fwd(qk, u, gate_b_row, w, n_ml, *ml_counts)
    y = _post_fwd(o_f, o_b, h_f, h_b, u, hg_norm_w, ml_norm_w, t_rows, w, n_hg, n_ml, tm)
    prm_out = _rows8([mod_x[2], ln_g, ln_b], dm)
    dz, dy, gx_direct, acc_out = _out_block(y, w_out_full, x[0], loss_target[0], prm_out, tm)

    d_w_out = _mm_tn("d_w_out", y, dz, _largest_divisor(di, 1024, LANE),
                     _largest_divisor(t_rows, 1024, SUBLANE_BF16))
    d_w_out4 = d_w_out.reshape(N_CHIPS, 2, di // (2 * N_CHIPS), dm)
    mine_out = lax.dynamic_index_in_dim(d_w_out4, ci, 1, keepdims=False)
    other_out = lax.dynamic_index_in_dim(d_w_out4, 1 - ci, 1, keepdims=False)
    (d_o, d_h, d_az, d_bo, d_bz, d_wa, d_wb), (got_out,) = _post_bwd(
        o_f, o_b, h_f, h_b, u, hg_norm_w, ml_norm_w, dy, t_rows, w, n_hg, n_ml, tm, rider=_sibling_swap([other_out]))
    pair_out = _sum_pair("rs_pair_sum_w_out", mine_out, got_out)
    (d_aq_f, d_aff, d_ai_f, d_lb_f), (d_aq_b, d_afb, d_ai_b, d_lb_b), (landed_out,) = _hg_scan_bwd(
        u, lb_full, hg_saved, d_o, w, *hg_counts, rider=_chip_scatter([pair_out]))
    half_g_out = _sum_chips("rs_chip_sum_w_out", landed_out, pair_out, chip)
    (d_qk_f, d_v_f, d_g_f, d_gb_f), (d_qk_b, d_v_b, d_g_b, d_gb_b), (sibling_out,) = _ml_scan_bwd(
        qk, u, gate_b_row, ml_saved, d_h, w, n_ml, *ml_counts, rider=_sibling_swap([half_g_out]))
    g_w_out = _join_halves(ci, half_g_out, sibling_out, 0)
    d_bqk, d_cw, d_cb = _conv_bwd(u, (d_qk_f, d_qk_b), conv_w9, conv_b, t_rows, c_rows, w, LANE)
    du = _assemble_du([(d_aq_f, d_aq_b), d_aff, d_afb, (d_ai_f, d_ai_b), d_az, d_bqk, (d_v_f, d_v_b), d_bo, d_bz],
                      (d_g_f, d_g_b), n_pad, tm // 2)
    d_wt_in, d_wt_in_bf16 = _mm_tn("d_w_in", du, hc, tn_mm, tm_mm, with_bf16=True)

    got_in = _Exchange([d_wt_in_bf16], [jax.ShapeDtypeStruct((n_pad, dm // 2), BF16)],
                       [(SIBLING_MASK, 0, lambda s, r: (slice(None), lanes_of(r[2])), 0, None)]).run("rs_pair_w_in")[0]
    pair_half = _sum_pair_lanes("rs_pair_sum_w_in", d_wt_in, got_in, ci)
    pair_in = jnp.stack([pair_half[k * ns:(k + 1) * ns] for k in range(N_CHIPS)])
    d_hc, (landed_in,) = _mm_acc("d_h", du, wt_full, tm_mm, tn_mm, rider=_chip_scatter([pair_in]))
    half_g_in = _sum_chips("rs_chip_sum_w_in", landed_in, pair_in, chip)
    g_wt_in = _join_halves(ci, half_g_in, _sibling_swap([half_g_in]).run("rs_join_w_in")[0], 1)
    (gx, acc_mod), _ = _modulate_bwd(x[0], ctx[0], d_hc, prm, gx_direct, tm)
    grad_x = gx[None]

    zero_row = jnp.zeros((dm,), F32)
    d_gb = jnp.concatenate([d_gb_f[:, 0:n_ml], d_gb_b[:, n_ml:2 * n_ml], d_gb_f[:, 2 * n_ml:3 * n_ml],
                            d_gb_b[:, 3 * n_ml:4 * n_ml], jnp.zeros((1, dm - 4 * n_ml), F32)], axis=1)
    rows = [acc_mod[0, 0], acc_mod[0, 1], acc_out[OUT_ROW_GATE],
            acc_mod[1, 0], acc_mod[1, 1], zero_row]
    rows += list(d_cw) + [d_cb[0], d_lb_f.reshape(dm), d_lb_b.reshape(dm),
                          jnp.concatenate([d_wa[0], d_wb[0]]), acc_out[OUT_ROW_LN_G], acc_out[OUT_ROW_LN_B],
                          acc_out[OUT_ROW_LOSS], d_gb[0], zero_row]
    ROW_CW, ROW_CB, ROW_LB, ROW_NORM, ROW_LN_G, ROW_LN_B, ROW_LOSS, ROW_GB = 6, 15, 16, 18, 19, 20, 21, 22
    delta, new_m, new_v = {}, {}, {}
    small_rows = jnp.concatenate([r.reshape(dm) for r in rows]).reshape(len(rows), dm)
    g3 = _all_gather8(small_rows).run("gather_small_grads")[0]
    sums, totals = _sum_devices(g3, 3)
    loss = totals[ROW_LOSS, 0]
    dm16 = jnp.concatenate([g3[:, 0:3, :].reshape(N_DEV, 3 * dm), sums[3:6].reshape(1, 3 * dm),
                            jnp.zeros((16 - N_DEV - 1, 3 * dm), F32)])
    g_w_mod, dc16 = _mod_bwd(a16, lax.dynamic_slice_in_dim(dm16, chip * nm, nm, 1), w_mod[0], tn_mod)
    g4 = _all_gather8(jnp.pad(dc16[N_DEV:N_DEV + 1], ((0, 7), (0, 0)))).run("gather_c_ctx")[0]
    g_c_ctx = _c_ctx_grad(g4, jnp.broadcast_to(c_ctx[None], (8, dm)))[0]
    res, _ = _adamw("adamw_w_in", as_t(w_in), g_wt_in, as_t(m_w_in), as_t(v_w_in))
    delta["w_in"], new_m["w_in"], new_v["w_in"] = (jnp.transpose(a)[None] for a in res)
    res, _ = _adamw("adamw_w_mod", w_mod[0], g_w_mod, m_w_mod[0], v_w_mod[0])
    delta["w_mod"], new_m["w_mod"], new_v["w_mod"] = (a[None] for a in res)

    chip_cols = lambda a, width: lax.dynamic_slice_in_dim(a, chip * width, width, a.ndim - 1)
    grads = {
        "c_ctx": g_c_ctx,
        "w_mod": g_w_mod[None],
        "b_mod": sums[0:3].reshape(1, 3 * dm),
        "w_in": jnp.transpose(g_wt_in)[None],
        "conv_w": chip_cols(sums[ROW_CW:ROW_CW + 9].reshape(1, 3, 3, di), di // N_CHIPS),
        "conv_b": sums[ROW_CB][None],
        "hg_lb": chip_cols(sums[ROW_LB:ROW_LB + 2].reshape(2, 2, w), w // N_CHIPS),
        "ml_gate_b": sums[ROW_GB, 0:4 * n_ml].reshape(1, 4, n_ml),
        "hg_norm_w": sums[ROW_NORM, 0:w][None],
        "ml_norm_w": sums[ROW_NORM, w:2 * w][None],
        "w_out": g_w_out[None],
        "ln_g": sums[ROW_LN_G][None],
        "ln_b": sums[ROW_LN_B][None],
    }
    weights = dict(c_ctx=c_ctx, w_mod=w_mod, b_mod=b_mod, w_in=w_in, conv_w=conv_w, conv_b=conv_b, hg_lb=hg_lb,
                   ml_gate_b=ml_gate_b, hg_norm_w=hg_norm_w, ml_norm_w=ml_norm_w, w_out=w_out, ln_g=ln_g, ln_b=ln_b)
    mom1 = dict(c_ctx=m_c_ctx, w_mod=m_w_mod, b_mod=m_b_mod, w_in=m_w_in, conv_w=m_conv_w, conv_b=m_conv_b,
                hg_lb=m_hg_lb, ml_gate_b=m_ml_gate_b, hg_norm_w=m_hg_norm_w, ml_norm_w=m_ml_norm_w, w_out=m_w_out,
                ln_g=m_ln_g, ln_b=m_ln_b)
    mom2 = dict(c_ctx=v_c_ctx, w_mod=v_w_mod, b_mod=v_b_mod, w_in=v_w_in, conv_w=v_conv_w, conv_b=v_conv_b,
                hg_lb=v_hg_lb, ml_gate_b=v_ml_gate_b, hg_norm_w=v_hg_norm_w, ml_norm_w=v_ml_norm_w, w_out=v_w_out,
                ln_g=v_ln_g, ln_b=v_ln_b)
    names = list(weights)
    big = ("w_mod", "w_in", "w_out")
    small = [n for n in names if n not in big]

    res, _ = _adamw("adamw_w_out", w_out[0], g_w_out, m_w_out[0], v_w_out[0])
    delta["w_out"], new_m["w_out"], new_v["w_out"] = (a[None] for a in res)
    small_shapes = [weights[n].shape for n in small]
    res, _ = _adamw("adamw_small", *(_pack([src[n] for n in small]) for src in (weights, grads, mom1, mom2)))
    for out, packed in zip((delta, new_m, new_v), res):
        for n, a in zip(small, _unpack(packed, small_shapes)):
            out[n] = a

    return (loss, grad_x, *[grads[n].reshape(weights[n].shape) for n in names], *[delta[n] for n in names],
            *[new_m[n] for n in names], *[new_v[n] for n in names])
```

```python
import functools
import math

import jax
import jax.numpy as jnp
from jax import lax
from jax.experimental import pallas as pl
from jax.experimental.pallas import tpu as pltpu

F32 = jnp.float32
BF16 = jnp.bfloat16
HIGHEST = lax.Precision.HIGHEST
MESH = pl.DeviceIdType.MESH

HG_CHUNK = 64
ML_CHUNK = 256
GRID_W = 64
HG_DK = 128
LANE = 128
SUBLANE_BF16 = 16
ALPHA = 2.0 ** 0.25
LN_EPS = 1e-5
NORM_EPS = 1e-6
ADAM_LR = 0.001
ADAM_B1 = 0.9
ADAM_B2 = 0.999
ADAM_EPS = 1e-08
ADAM_WD = 0.01
ADAM_STEP = 10
VMEM_LIMIT = 56 * 1024 * 1024
N_CHIPS = 4
N_DEV = 8


def _params(sem=None):
    return pltpu.CompilerParams(dimension_semantics=sem, vmem_limit_bytes=VMEM_LIMIT)


def _largest_divisor(n, cap, multiple=1):
    best = None
    for d in range(multiple, min(n, cap) + 1, multiple):
        if n % d == 0:
            best = d
    assert best is not None, (n, cap, multiple)
    return best


def _sigmoid(x):
    return jax.nn.sigmoid(x)


def _silu(x):
    return x * jax.nn.sigmoid(x)


def _dot(a, b, dims, precision=None):
    return lax.dot_general(a, b, (dims, ((), ())), precision=precision, preferred_element_type=F32)


def _nn(a, b, precision=None):
    return _dot(a, b, ((1,), (0,)), precision)


def _nt(a, b, precision=None):
    return _dot(a, b, ((1,), (1,)), precision)


def _tn(a, b, precision=None):
    return _dot(a, b, ((0,), (0,)), precision)


def _narrow(x):
    return x.astype(BF16)


@jax.custom_vjp
def _bnn(a, b):
    return _nn(_narrow(a), _narrow(b))


def _bnn_fwd(a, b):
    an, bn = _narrow(a), _narrow(b)
    return _nn(an, bn), (an, bn)


def _bnn_bwd(res, ct):
    an, bn = res
    ctn = _narrow(ct)
    return _nt(ctn, bn), _tn(an, ctn)


_bnn.defvjp(_bnn_fwd, _bnn_bwd)


@jax.custom_vjp
def _bnt(a, b):
    return _nt(_narrow(a), _narrow(b))


def _bnt_fwd(a, b):
    an, bn = _narrow(a), _narrow(b)
    return _nt(an, bn), (an, bn)


def _bnt_bwd(res, ct):
    an, bn = res
    ctn = _narrow(ct)
    return _nn(ctn, bn), _tn(ctn, an)


_bnt.defvjp(_bnt_fwd, _bnt_bwd)


@jax.custom_vjp
def _btn(a, b):
    return _tn(_narrow(a), _narrow(b))


def _btn_fwd(a, b):
    an, bn = _narrow(a), _narrow(b)
    return _tn(an, bn), (an, bn)


def _btn_bwd(res, ct):
    an, bn = res
    ctn = _narrow(ct)
    return _nt(bn, ctn), _nn(an, ctn)


_btn.defvjp(_btn_fwd, _btn_bwd)


def _visible(n, rev):
    r = lax.broadcasted_iota(jnp.int32, (n, n), 0)
    c = lax.broadcasted_iota(jnp.int32, (n, n), 1)
    return (r <= c) if rev else (r >= c)


def _mask_matmul(mask, x):
    mb = mask.astype(BF16)
    hi = x.astype(BF16)
    lo = (x - hi.astype(F32)).astype(BF16)
    return _nn(mb, hi) + _nn(mb, lo)


@functools.partial(jax.custom_vjp, nondiff_argnums=(1,))
def _cumulative(x, rev):
    return _mask_matmul(_visible(x.shape[0], rev), x)


def _cumulative_fwd(x, rev):
    return _cumulative(x, rev), None


def _cumulative_bwd(rev, _, ct):
    return (_mask_matmul(_visible(ct.shape[0], not rev), ct),)


_cumulative.defvjp(_cumulative_fwd, _cumulative_bwd)


def _hg_chunk(states, aq, af, ai, lb0, lb1, rev):
    n_heads = len(states)
    lb = _sigmoid(lb0 - lb1)
    f = lb + (1.0 - lb) * _sigmoid(af)
    g = jnp.log(f)
    k = 1.0 - f
    q = _silu(aq)
    chunk = aq.shape[0]
    vis = _visible(chunk, rev)
    b = _cumulative(g, rev)
    last = 0 if rev else chunk - 1
    b_end = b[last:last + 1]
    b_mid = b[chunk // 2:chunk // 2 + 1]
    q_inter = q * jnp.exp(b)
    q_intra = q * jnp.exp(b - b_mid)
    k_intra = k * jnp.exp(b_mid - b)
    k_dec = k * jnp.exp(b_end - b)
    e_end = jnp.exp(b_end)
    new_states, outs = [], []
    for h in range(n_heads):
        sl = slice(h * HG_DK, (h + 1) * HG_DK)
        s_t = states[h]
        scores = jnp.where(vis, _nt(q_intra[:, sl], k_intra[:, sl]), 0.0)
        outs.append(_nt(q_inter[:, sl], s_t) + _nn(scores, ai[:, sl]))
        new_states.append(e_end[:, sl] * s_t + _tn(ai[:, sl], k_dec[:, sl]))
    return new_states, jnp.concatenate(outs, axis=1)


def _ml_chunk(state, q, k, v, g, gb, rev, d):
    cms, nvs, mbs = state
    n_heads = len(cms)
    dh = q.shape[1] // n_heads
    ga = g + gb
    log_f_all = jax.nn.log_sigmoid(ga)
    chunk = q.shape[0]
    vis = _visible(chunk, rev)
    b_all = _cumulative(log_f_all, rev)
    last = 0 if rev else chunk - 1
    k = k * (dh ** -0.5)
    new_c, new_n, new_m, outs = [], [], [], []
    for h in range(n_heads):
        ci = d * n_heads + h
        cf = (2 + d) * n_heads + h
        sl = slice(h * dh, (h + 1) * dh)
        qh, kh, vh = q[:, sl], k[:, sl], v[:, sl]
        li = ga[:, ci:ci + 1]
        b = b_all[:, cf:cf + 1]
        m = mbs[h][:, 0:1]
        row = jnp.transpose(li - b)
        log_w = jnp.where(vis, b + row, -jnp.inf)
        m_inter = b + m
        m_t = jnp.maximum(m_inter, jnp.max(log_w, axis=-1, keepdims=True))
        w_inter = jnp.exp(m_inter - m_t)
        w_qk = jnp.exp(log_w - m_t) * _bnt(qh, kh)
        num = w_inter * _bnt(qh, cms[h]) + _bnn(w_qk, vh)
        den = w_inter * jnp.sum(qh * nvs[h], axis=-1, keepdims=True) + jnp.sum(w_qk, axis=-1, keepdims=True)
        outs.append(num / jnp.maximum(jnp.abs(den), jnp.exp(-m_t)))
        m_new = m_t[last:last + 1]
        b_end = b[last:last + 1]
        w_s = jnp.exp(b_end - b + li - m_new)
        decay = jnp.exp(b_end + m - m_new)
        new_c.append(decay * cms[h] + _btn(w_s * vh, kh))
        new_n.append(decay * nvs[h] + jnp.sum(w_s * kh, axis=0, keepdims=True))
        new_m.append(jnp.broadcast_to(m_new, (1, LANE)))
    return (new_c, new_n, new_m), jnp.concatenate(outs, axis=1)


def _post_fn(o_f, o_b, az, h_f, h_b, bo, bz, wa, wb, n_hg, n_ml):
    o = o_f + o_b
    parts = []
    for h in range(n_hg):
        s = o[:, h * HG_DK:(h + 1) * HG_DK]
        parts.append(s * lax.rsqrt(jnp.mean(s * s, axis=-1, keepdims=True) + NORM_EPS))
    y_a = jnp.concatenate(parts, axis=1) * wa * _silu(az)
    hh = h_f + h_b
    dh = hh.shape[1] // n_ml
    parts = []
    for h in range(n_ml):
        s = hh[:, h * dh:(h + 1) * dh]
        mu = jnp.mean(s, axis=-1, keepdims=True)
        sc = s - mu
        parts.append(sc * lax.rsqrt(jnp.mean(sc * sc, axis=-1, keepdims=True) + NORM_EPS))
    y_b = jnp.concatenate(parts, axis=1) * wb * _sigmoid(bo) * _silu(bz)
    return jnp.concatenate([y_a, y_b], axis=1)


def _chip_of(dev):
    return 2 * dev[0] + dev[1]


def _index_of(dev):
    return 4 * dev[0] + 2 * dev[1] + dev[2]


class _Exchange:
    def __init__(self, srcs, out_shapes, transfers, local_copies=(), in_place=None):
        self.srcs, self.out_shapes = list(srcs), list(out_shapes)
        self.transfers, self.local_copies = list(transfers), list(local_copies)
        self.in_place = dict(in_place or {})

    def scratch(self):
        return [pltpu.SemaphoreType.DMA((len(self.transfers),)), pltpu.SemaphoreType.DMA((len(self.transfers),)),
                pltpu.SemaphoreType.DMA((max(len(self.local_copies), 1),))]

    def copies(self, ins, outs, send_sems, recv_sems, local_sems):
        me = (lax.axis_index("x"), lax.axis_index("y"), lax.axis_index("c"))

        def pick(ref, fn, *who):
            return ref if fn is None else ref.at[fn(*who)]

        sends, recvs, locs = [], [], []
        for t, (mask, si, sfn, di, dfn) in enumerate(self.transfers):
            peer = tuple(1 - p if flip else p for p, flip in zip(me, mask))
            sends.append(pltpu.make_async_remote_copy(
                src_ref=pick(ins[si], sfn, me, peer), dst_ref=pick(outs[di], dfn, me, peer),
                send_sem=send_sems.at[t], recv_sem=recv_sems.at[t], device_id=peer, device_id_type=MESH))
            landing = pick(outs[di], dfn, peer, me)
            recvs.append(pltpu.make_async_remote_copy(
                src_ref=landing, dst_ref=landing,
                send_sem=send_sems.at[t], recv_sem=recv_sems.at[t], device_id=peer, device_id_type=MESH))
        for l, (si, sfn, di, dfn) in enumerate(self.local_copies):
            locs.append(pltpu.make_async_copy(pick(ins[si], sfn, me), pick(outs[di], dfn, me), local_sems.at[l]))

        def start():
            for cp in locs + sends:
                cp.start()

        def wait():
            for cp in recvs:
                cp.wait_recv()
            for cp in sends:
                cp.wait_send()
            for cp in locs:
                cp.wait()

        return start, wait

    def run(self, name):
        n_in, n_out = len(self.srcs), len(self.out_shapes)

        def body(*refs):
            start, wait = self.copies(refs[:n_in], refs[n_in:n_in + n_out], *refs[n_in + n_out:])
            start()
            wait()

        hbm = pl.BlockSpec(memory_space=pltpu.HBM)
        return pl.pallas_call(
            body, name=name, out_shape=tuple(self.out_shapes), in_specs=[hbm] * n_in,
            out_specs=tuple([hbm] * n_out), scratch_shapes=self.scratch(), input_output_aliases=self.in_place,
        )(*self.srcs)


def _call(body, operands, *, name, grid, in_specs, out_specs, out_shape, scratch_shapes=(), sem=None, rider=None):
    out_specs, out_shape, scratch_shapes = list(out_specs), list(out_shape), list(scratch_shapes)
    if rider is None:
        res = pl.pallas_call(
            body, name=name, grid=grid, in_specs=list(in_specs), out_specs=tuple(out_specs),
            out_shape=tuple(out_shape), scratch_shapes=scratch_shapes, compiler_params=_params(sem),
        )(*operands)
        return list(res), []
    counts = (len(in_specs), len(rider.srcs), len(out_specs), len(rider.out_shapes), len(scratch_shapes), 3)

    def full(*refs):
        groups, pos = [], 0
        for k in counts:
            groups.append(refs[pos:pos + k])
            pos += k
        own_in, ex_in, own_out, ex_out, own_scr, ex_scr = groups
        ids = [pl.program_id(a) for a in range(len(grid))]
        first = functools.reduce(jnp.logical_and, [i == 0 for i in ids])
        last = functools.reduce(jnp.logical_and, [i == g - 1 for i, g in zip(ids, grid)])
        start, wait = rider.copies(ex_in, ex_out, *ex_scr)
        pl.when(first)(start)
        body(*own_in, *own_out, *own_scr)
        pl.when(last)(wait)

    hbm = pl.BlockSpec(memory_space=pltpu.HBM)
    res = pl.pallas_call(
        full, name=name, grid=grid, in_specs=list(in_specs) + [hbm] * counts[1],
        out_specs=tuple(out_specs + [hbm] * counts[3]), out_shape=tuple(out_shape + rider.out_shapes),
        scratch_shapes=scratch_shapes + rider.scratch(), compiler_params=_params(("arbitrary",) * len(grid)),
        input_output_aliases={counts[0] + i: counts[2] + o for i, o in rider.in_place.items()},
    )(*operands, *rider.srcs)
    return list(res[:counts[2]]), list(res[counts[2]:])


ALL_MASKS = [(mx, my, mc) for mx in (0, 1) for my in (0, 1) for mc in (0, 1)][1:]
CHIP_MASKS = [(1, 0, 0), (0, 1, 0), (1, 1, 0)]
SIBLING_MASK = (0, 0, 1)


def _all_gather8(v):
    out = jax.ShapeDtypeStruct((N_DEV,) + v.shape, v.dtype)
    slot = lambda sender, receiver: _index_of(sender)
    transfers = [(mask, 0, None, 0, slot) for mask in ALL_MASKS]
    return _Exchange([v], [out], transfers, [(0, None, 0, lambda me: _index_of(me))])


def _all_gather_chips(arrays):
    outs = [jax.ShapeDtypeStruct((N_CHIPS,) + a.shape, a.dtype) for a in arrays]
    slot = lambda sender, receiver: _chip_of(sender)
    return _Exchange(arrays, outs, [(mask, i, None, i, slot) for i in range(len(arrays)) for mask in CHIP_MASKS])


def _sibling_swap(arrays):
    outs = [jax.ShapeDtypeStruct(a.shape, a.dtype) for a in arrays]
    return _Exchange(arrays, outs, [(SIBLING_MASK, i, None, i, None) for i in range(len(arrays))])


def _chip_scatter(arrays):
    outs = [jax.ShapeDtypeStruct(a.shape, a.dtype) for a in arrays]
    transfers = [(mask, i, lambda s, r: _chip_of(r), i, lambda s, r: _chip_of(s))
                 for i in range(len(arrays)) for mask in CHIP_MASKS]
    return _Exchange(arrays, outs, transfers)


def _own_block(chip, own, blocks):
    sel = (lax.broadcasted_iota(jnp.int32, (N_CHIPS,) + (1,) * (blocks.ndim - 1), 0) == chip)
    return jnp.where(sel, own if own.ndim == blocks.ndim else own[None], blocks)


def _join_halves(ci, mine, other, axis):
    return jnp.where(ci == 0, jnp.concatenate([mine, other], axis=axis), jnp.concatenate([other, mine], axis=axis))


def _mm_nt(name, a, b, tm, tn, out_dtype, rider=None):
    m, k = a.shape
    n = b.shape[0]

    def body(a_ref, b_ref, o_ref):
        o_ref[...] = _nt(a_ref[...], b_ref[...]).astype(out_dtype)

    (out,), rode = _call(
        body, (a, b), name=name, grid=(n // tn, m // tm),
        in_specs=[pl.BlockSpec((tm, k), lambda j, i: (i, 0)), pl.BlockSpec((tn, k), lambda j, i: (j, 0))],
        out_specs=[pl.BlockSpec((tm, tn), lambda j, i: (i, j))],
        out_shape=[jax.ShapeDtypeStruct((m, n), out_dtype)], sem=("parallel", "parallel"), rider=rider)
    return out, rode


def _mm_acc(name, a, b, tm, tk, rider=None):
    m, kc = a.shape
    n = b.shape[1]

    def body(a_ref, b_ref, o_ref):
        @pl.when(pl.program_id(1) == 0)
        def _():
            o_ref[...] = jnp.zeros_like(o_ref)
        o_ref[...] += _nn(a_ref[...], b_ref[...])

    (out,), rode = _call(
        body, (a, b), name=name, grid=(m // tm, kc // tk),
        in_specs=[pl.BlockSpec((tm, tk), lambda i, kk: (i, kk)), pl.BlockSpec((tk, n), lambda i, kk: (kk, 0))],
        out_specs=[pl.BlockSpec((tm, n), lambda i, kk: (i, 0))],
        out_shape=[jax.ShapeDtypeStruct((m, n), F32)], sem=("parallel", "arbitrary"), rider=rider)
    return out, rode


def _mm_tn(name, a, b, tm, tk, with_bf16=False):
    kr, m = a.shape
    n = b.shape[1]
    steps_k = kr // tk

    def body(a_ref, b_ref, o_ref, *narrow):
        @pl.when(pl.program_id(1) == 0)
        def _():
            o_ref[...] = jnp.zeros_like(o_ref)
        o_ref[...] += _tn(a_ref[...], b_ref[...])
        if with_bf16:
            @pl.when(pl.program_id(1) == steps_k - 1)
            def _():
                narrow[0][...] = o_ref[...].astype(BF16)

    out_spec = pl.BlockSpec((tm, n), lambda i, kk: (i, 0))
    res = pl.pallas_call(
        body, name=name, grid=(m // tm, steps_k),
        in_specs=[pl.BlockSpec((tk, tm), lambda i, kk: (kk, i)), pl.BlockSpec((tk, n), lambda i, kk: (kk, 0))],
        out_specs=(out_spec,) * (2 if with_bf16 else 1),
        out_shape=(jax.ShapeDtypeStruct((m, n), F32),) + ((jax.ShapeDtypeStruct((m, n), BF16),) if with_bf16 else ()),
        compiler_params=_params(("parallel", "arbitrary")),
    )(a, b)
    return res if with_bf16 else res[0]


def _modulate_fwd(x, ctx, prm, tm, rider=None):
    t_rows, dm = x.shape
    lat = t_rows // tm
    r = t_rows + ctx.shape[0]

    def body(x_ref, c_ref, p_ref, h_ref):
        xv = jnp.where(pl.program_id(0) >= lat, c_ref[...], x_ref[...])
        mu = jnp.mean(xv, axis=-1, keepdims=True)
        xm = xv - mu
        n = xm * lax.rsqrt(jnp.mean(xm * xm, axis=-1, keepdims=True) + LN_EPS)
        h_ref[...] = (n * (1.0 + p_ref[0, 1:2, :]) + p_ref[0, 0:1, :]).astype(BF16)

    (h,), rode = _call(
        body, (x, ctx, prm), name="modulate_fwd", grid=(r // tm,),
        in_specs=[pl.BlockSpec((tm, dm), lambda i: (jnp.minimum(i, lat - 1), 0)),
                  pl.BlockSpec((tm, dm), lambda i: (jnp.maximum(i - lat, 0), 0)),
                  pl.BlockSpec((1, 8, dm), lambda i: ((i >= lat).astype(jnp.int32), 0, 0))],
        out_specs=[pl.BlockSpec((tm, dm), lambda i: (i, 0))],
        out_shape=[jax.ShapeDtypeStruct((r, dm), BF16)], sem=("parallel",), rider=rider)
    return h, rode


def _modulate_bwd(x, ctx, dh, prm, gx_direct, tm, rider=None):
    t_rows, dm = x.shape
    lat, n_ct = t_rows // tm, ctx.shape[0] // tm
    is_ctx = lambda i: i < n_ct
    cls = lambda i: is_ctx(i).astype(jnp.int32)
    lat_tile = lambda i: (jnp.maximum(i - n_ct, 0), 0)

    def body(x_ref, c_ref, dh_ref, p_ref, gd_ref, gx_ref, acc_ref):
        i = pl.program_id(0)

        @pl.when((i == 0) | (i == n_ct))
        def _():
            acc_ref[...] = jnp.zeros_like(acc_ref)

        x = jnp.where(is_ctx(i), c_ref[...], x_ref[...])
        dh_v = dh_ref[...]
        mu = jnp.mean(x, axis=-1, keepdims=True)
        xm = x - mu
        rstd = lax.rsqrt(jnp.mean(xm * xm, axis=-1, keepdims=True) + LN_EPS)
        n = xm * rstd
        acc_ref[0, 0:1, :] += jnp.sum(dh_v, axis=0, keepdims=True)
        acc_ref[0, 1:2, :] += jnp.sum(dh_v * n, axis=0, keepdims=True)
        dn = dh_v * (1.0 + p_ref[0, 1:2, :])
        dx = rstd * (dn - jnp.mean(dn, axis=-1, keepdims=True) - n * jnp.mean(dn * n, axis=-1, keepdims=True))
        gx_ref[...] = dx + gd_ref[...]

    return _call(
        body, (x, ctx, dh, prm, gx_direct), name="modulate_bwd", grid=(n_ct + lat,),
        in_specs=[pl.BlockSpec((tm, dm), lat_tile),
                  pl.BlockSpec((tm, dm), lambda i: (jnp.minimum(i, n_ct - 1), 0)),
                  pl.BlockSpec((tm, dm), lambda i: (jnp.where(is_ctx(i), lat + i, i - n_ct), 0)),
                  pl.BlockSpec((1, 8, dm), lambda i: (cls(i), 0, 0)),
                  pl.BlockSpec((tm, dm), lat_tile)],
        out_specs=(pl.BlockSpec((tm, dm), lat_tile), pl.BlockSpec((1, 8, dm), lambda i: (cls(i), 0, 0))),
        out_shape=(jax.ShapeDtypeStruct((t_rows, dm), F32), jax.ShapeDtypeStruct((2, 8, dm), F32)),
        sem=("arbitrary",), rider=rider)


def _conv_parts(t_rows, c_rows):
    return ((0, t_rows, t_rows // GRID_W, GRID_W), (t_rows, c_rows, 1, c_rows))


def _col_shifts(x2, rows_g, width_g):
    n, ct = x2.shape
    col = lax.broadcasted_iota(jnp.int32, (width_g, ct), 0)
    as_grid = lambda a: a.reshape(rows_g, width_g, ct)
    left = as_grid(pltpu.roll(x2, 1, 0)) * (col >= 1).astype(F32)
    right = as_grid(pltpu.roll(x2, n - 1, 0)) * (col <= width_g - 2).astype(F32)
    return [left, as_grid(x2), right]


CONV_BLOCK_ROWS = 4


def _conv_blocks(t_rows, c_rows):
    for t0, _, rows_g, width_g in _conv_parts(t_rows, c_rows):
        nb = min(CONV_BLOCK_ROWS, rows_g)
        assert rows_g % nb == 0
        for g0 in range(0, rows_g, nb):
            yield t0, rows_g, width_g, g0, nb


def _slab(ref, t0, rows_g, width_g, g0, nb):
    if rows_g == 1:
        return ref[t0:t0 + width_g, :]
    lo, hi = max(g0 - 1, 0), min(g0 + nb + 1, rows_g)
    parts = [ref[t0 + lo * width_g:t0 + hi * width_g, :]]
    zero = jnp.zeros((width_g, ref.shape[1]), F32)
    if g0 == 0:
        parts.insert(0, zero)
    if g0 + nb == rows_g:
        parts.append(zero)
    return jnp.concatenate(parts, axis=0)


def _conv_taps(cols, w_ref, nb, flip):
    one_row = cols[0].shape[0] == nb
    acc = None
    for a in range(3):
        if one_row and a != 1:
            continue
        for b in range(3):
            tap = (2 - a) * 3 + (2 - b) if flip else a * 3 + b
            term = (cols[b] if one_row else cols[b][a:a + nb]) * w_ref[tap:tap + 1, :]
            acc = term if acc is None else acc + term
    return acc


def _conv_fwd(u, conv_w9, conv_b, t_rows, c_rows, w, ct):
    r = u.shape[0]
    base = 5 * w // ct

    def body(x_ref, w_ref, b_ref, o_ref):
        for t0, rows_g, width_g, g0, nb in _conv_blocks(t_rows, c_rows):
            slab = _slab(x_ref, t0, rows_g, width_g, g0, nb)
            cols = _col_shifts(slab, slab.shape[0] // width_g, width_g)
            pre = _conv_taps(cols, w_ref, nb, False) + b_ref[...]
            o_ref[t0 + g0 * width_g:t0 + (g0 + nb) * width_g, :] = _silu(pre).reshape(nb * width_g, ct)

    return pl.pallas_call(
        body, name="conv_fwd", grid=(2 * w // ct,),
        in_specs=[pl.BlockSpec((r, ct), lambda i: (0, base + i)), pl.BlockSpec((9, ct), lambda i: (0, i)),
                  pl.BlockSpec((1, ct), lambda i: (0, i))],
        out_specs=pl.BlockSpec((r, ct), lambda i: (0, i)),
        out_shape=jax.ShapeDtypeStruct((r, 2 * w), F32),
        compiler_params=_params(("parallel",)),
    )(u, conv_w9, conv_b)


def _conv_bwd(u, dqk_pair, conv_w9, conv_b, t_rows, c_rows, w, ct):
    r = u.shape[0]
    base = 5 * w // ct

    def body(x_ref, d1_ref, d2_ref, w_ref, b_ref, dx_ref, dw_ref, db_ref, dpre_ref):
        dw = [jnp.zeros((1, ct), F32) for _ in range(9)]
        db = jnp.zeros((1, ct), F32)
        for t0, rows_g, width_g, g0, nb in _conv_blocks(t_rows, c_rows):
            rows = slice(t0 + g0 * width_g, t0 + (g0 + nb) * width_g)
            slab = _slab(x_ref, t0, rows_g, width_g, g0, nb)
            cols = _col_shifts(slab, slab.shape[0] // width_g, width_g)
            pre = _conv_taps(cols, w_ref, nb, False) + b_ref[...]
            sg = _sigmoid(pre)
            dpre = (d1_ref[rows, :] + d2_ref[rows, :]).reshape(pre.shape) * (sg * (1.0 + pre * (1.0 - sg)))
            dpre_ref[rows, :] = dpre.reshape(nb * width_g, ct)
            db = db + jnp.sum(jnp.sum(dpre, axis=0), axis=0, keepdims=True)
            for a in range(3):
                if rows_g == 1 and a != 1:
                    continue
                for b in range(3):
                    moved = cols[b] if rows_g == 1 else cols[b][a:a + nb]
                    dw[a * 3 + b] = dw[a * 3 + b] + jnp.sum(jnp.sum(moved * dpre, axis=0), axis=0, keepdims=True)
        for t0, rows_g, width_g, g0, nb in _conv_blocks(t_rows, c_rows):
            slab = _slab(dpre_ref, t0, rows_g, width_g, g0, nb)
            cols = _col_shifts(slab, slab.shape[0] // width_g, width_g)
            dx_ref[t0 + g0 * width_g:t0 + (g0 + nb) * width_g, :] = _conv_taps(cols, w_ref, nb, True).reshape(
                nb * width_g, ct).astype(BF16)
        for tap in range(9):
            dw_ref[tap:tap + 1, :] = dw[tap]
        db_ref[...] = db

    return pl.pallas_call(
        body, name="conv_bwd", grid=(2 * w // ct,),
        in_specs=[pl.BlockSpec((r, ct), lambda i: (0, base + i)), pl.BlockSpec((r, ct), lambda i: (0, i)),
                  pl.BlockSpec((r, ct), lambda i: (0, i)),
                  pl.BlockSpec((9, ct), lambda i: (0, i)), pl.BlockSpec((1, ct), lambda i: (0, i))],
        out_specs=(pl.BlockSpec((r, ct), lambda i: (0, i)), pl.BlockSpec((9, ct), lambda i: (0, i)),
                   pl.BlockSpec((1, ct), lambda i: (0, i))),
        out_shape=(jax.ShapeDtypeStruct((r, 2 * w), BF16), jax.ShapeDtypeStruct((9, 2 * w), F32),
                   jax.ShapeDtypeStruct((1, 2 * w), F32)),
        scratch_shapes=[pltpu.VMEM((r, ct), F32)],
        compiler_params=_params(("parallel",)),
    )(u, dqk_pair[0], dqk_pair[1], conv_w9, conv_b)


def _assemble_du(groups, gates, n_pad, tm):
    flat, layout = [], []
    for entry in list(groups) + [gates]:
        parts = entry if isinstance(entry, (tuple, list)) else (entry,)
        layout.append((len(flat), len(parts), parts[0].shape[1]))
        flat += list(parts)
    r = flat[0].shape[0]

    def body(*refs):
        o_ref = refs[-1]
        col = 0
        for first, count, width in layout:
            val = refs[first][...]
            for extra in range(1, count):
                val = val.astype(F32) + refs[first + extra][...].astype(F32)
            o_ref[:, col:col + width] = val.astype(BF16)
            col += width
        assert col == n_pad

    return pl.pallas_call(
        body, name="assemble_du", grid=(r // tm,),
        in_specs=[pl.BlockSpec((tm, a.shape[1]), lambda i: (i, 0)) for a in flat],
        out_specs=pl.BlockSpec((tm, n_pad), lambda i: (i, 0)),
        out_shape=jax.ShapeDtypeStruct((r, n_pad), BF16),
        compiler_params=_params(("parallel",)),
    )(*flat)


def _scan_order(n_lat, n_ctx, rev):
    n = n_lat + n_ctx
    if rev:
        return lambda j: n - 1 - j
    return lambda j: (j + n_lat) % n


DIRS = (False, True)


def _hg_scan_fwd(u, lb_full, w, n_lat, n_ctx, chunk, rider=None):
    r = u.shape[0]
    n_heads = w // HG_DK
    n_chunks = n_lat + n_ctx
    nat = [_scan_order(n_lat, n_ctx, rev) for rev in DIRS]

    def body(*refs):
        ins, outs, scratch = refs[:8], refs[8:12], refs[12:]

        @pl.when(pl.program_id(0) == 0)
        def _():
            for s_ref in scratch:
                s_ref[...] = jnp.zeros_like(s_ref)

        results = []
        for d, rev in enumerate(DIRS):
            aq, af, ai, lb_ref = ins[4 * d:4 * d + 4]
            state = [scratch[d][h] for h in range(n_heads)]
            results.append((state, _hg_chunk(state, aq[...], af[...], ai[...],
                                             lb_ref[0, 0:1, :], lb_ref[0, 1:2, :], rev)))
        for d, (state, (new, o)) in enumerate(results):
            o_ref, save_ref = outs[2 * d:2 * d + 2]
            o_ref[...] = o
            for h in range(n_heads):
                save_ref[0, h] = state[h]
                scratch[d][h] = new[h]

    in_specs, out_specs, out_shape = [], [], []
    for d in range(2):
        in_specs += [pl.BlockSpec((chunk,w), lambda j, d=d: (nat[d](j), 0)),
                     pl.BlockSpec((chunk,w), lambda j, d=d: (nat[d](j), 1 + d)),
                     pl.BlockSpec((chunk,w), lambda j, d=d: (nat[d](j), 3)),
                     pl.BlockSpec((1, 2, w), lambda j, d=d: (d, 0, 0))]
        out_specs += [pl.BlockSpec((chunk,w), lambda j, d=d: (nat[d](j), 0)),
                      pl.BlockSpec((1, n_heads, HG_DK, HG_DK), lambda j: (j, 0, 0, 0))]
        out_shape += [jax.ShapeDtypeStruct((r, w), F32),
                      jax.ShapeDtypeStruct((n_chunks, n_heads, HG_DK, HG_DK), F32)]
    (o_f, s_f, o_b, s_b), rode = _call(
        body, (u, u, u, lb_full, u, u, u, lb_full), name="hg_scan_fwd", grid=(n_chunks,), in_specs=in_specs,
        out_specs=out_specs, out_shape=out_shape, scratch_shapes=[pltpu.VMEM((n_heads, HG_DK, HG_DK), F32)] * 2,
        sem=("arbitrary",), rider=rider)
    return (o_f, o_b), (s_f, s_b), rode


def _hg_scan_bwd(u, lb_full, saved, d_o, w, n_lat, n_ctx, chunk, rider=None):
    r = u.shape[0]
    n_heads = w // HG_DK
    n_chunks = n_lat + n_ctx
    step = lambda jj: n_chunks - 1 - jj
    nat = [(lambda jj, o=_scan_order(n_lat, n_ctx, rev): o(step(jj))) for rev in DIRS]

    def body(*refs):
        ins, outs, scratch = refs[:12], refs[12:20], refs[20:]
        jj = pl.program_id(0)

        @pl.when(jj == 0)
        def _():
            for d in range(2):
                scratch[d][...] = jnp.zeros_like(scratch[d])
                outs[4 * d + 3][...] = jnp.zeros_like(outs[4 * d + 3])

        results = []
        for d, rev in enumerate(DIRS):
            aq, af, ai, lb_ref, save_ref, do_ref = ins[6 * d:6 * d + 6]
            f = lambda st, a, b, c, l0, l1, rev=rev: _hg_chunk(st, a, b, c, l0, l1, rev)
            _, vjp = jax.vjp(f, [save_ref[0, h] for h in range(n_heads)], aq[...], af[...], ai[...],
                             lb_ref[0, 0:1, :], lb_ref[0, 1:2, :])
            d_out = do_ref[...] * (nat[d](jj) < n_lat).astype(F32)
            results.append(vjp(([scratch[d][h] for h in range(n_heads)], d_out)))
        for d, (dst, daq, daf, dai, dl0, dl1) in enumerate(results):
            daq_ref, daf_ref, dai_ref, dlb_ref = outs[4 * d:4 * d + 4]
            for h in range(n_heads):
                scratch[d][h] = dst[h]
            daq_ref[...] = daq.astype(BF16)
            daf_ref[...] = daf.astype(BF16)
            dai_ref[...] = dai.astype(BF16)
            dlb_ref[0:1, :] += dl0
            dlb_ref[1:2, :] += dl1

    in_specs, out_specs, out_shape, operands = [], [], [], []
    for d in range(2):
        row = lambda jj, d=d: (nat[d](jj), 0)
        in_specs += [pl.BlockSpec((chunk,w), row),
                     pl.BlockSpec((chunk,w), lambda jj, d=d: (nat[d](jj), 1 + d)),
                     pl.BlockSpec((chunk,w), lambda jj, d=d: (nat[d](jj), 3)),
                     pl.BlockSpec((1, 2, w), lambda jj, d=d: (d, 0, 0)),
                     pl.BlockSpec((1, n_heads, HG_DK, HG_DK), lambda jj: (step(jj), 0, 0, 0)),
                     pl.BlockSpec((chunk,w), lambda jj, d=d: (jnp.minimum(nat[d](jj), n_lat - 1), 0))]
        operands += [u, u, u, lb_full, saved[d], d_o]
        out_specs += [pl.BlockSpec((chunk,w), row)] * 3 + [pl.BlockSpec((2, w), lambda jj: (0, 0))]
        out_shape += [jax.ShapeDtypeStruct((r, w), BF16)] * 3 + [jax.ShapeDtypeStruct((2, w), F32)]
    res, rode = _call(
        body, operands, name="hg_scan_bwd", grid=(n_chunks,), in_specs=in_specs, out_specs=out_specs,
        out_shape=out_shape, scratch_shapes=[pltpu.VMEM((n_heads, HG_DK, HG_DK), F32)] * 2,
        sem=("arbitrary",), rider=rider)
    return res[0:4], res[4:8], rode


def _ml_state_shapes(n_chunks, n_heads, dh):
    return (jax.ShapeDtypeStruct((n_chunks, n_heads, dh, dh), F32),
            jax.ShapeDtypeStruct((n_chunks, n_heads, 1, dh), F32),
            jax.ShapeDtypeStruct((n_chunks, n_heads, 1, LANE), F32))


def _ml_state_specs(n_heads, dh, index):
    return (pl.BlockSpec((1, n_heads, dh, dh), lambda j: (index(j), 0, 0, 0)),
            pl.BlockSpec((1, n_heads, 1, dh), lambda j: (index(j), 0, 0, 0)),
            pl.BlockSpec((1, n_heads, 1, LANE), lambda j: (index(j), 0, 0, 0)))


def _ml_state_scratch(n_heads, dh):
    return [pltpu.VMEM((n_heads, dh, dh), F32), pltpu.VMEM((n_heads, 1, dh), F32), pltpu.VMEM((n_heads, 1, LANE), F32)]


def _ml_scan_fwd(qk, u, gate_b, w, n_heads, n_lat, n_ctx, chunk):
    r = u.shape[0]
    dh = w // n_heads
    n_chunks = n_lat + n_ctx
    nat = [_scan_order(n_lat, n_ctx, rev) for rev in DIRS]

    def body(*refs):
        ins, outs, scratch = refs[:10], refs[10:18], refs[18:]

        @pl.when(pl.program_id(0) == 0)
        def _():
            for s_ref in scratch:
                s_ref[...] = jnp.zeros_like(s_ref)

        results = []
        for d, rev in enumerate(DIRS):
            q, k, v, g, gb = ins[5 * d:5 * d + 5]
            state = tuple([ref[h] for h in range(n_heads)] for ref in scratch[3 * d:3 * d + 3])
            results.append((state, _ml_chunk(state, q[...], k[...], v[...], g[...], gb[...], rev, d)))
        for d, (state, (new, o)) in enumerate(results):
            outs[4 * d][...] = o
            for part in range(3):
                for h in range(n_heads):
                    outs[4 * d + 1 + part][0, h] = state[part][h]
                    scratch[3 * d + part][h] = new[part][h]

    in_specs, out_specs, out_shape = [], [], []
    for d in range(2):
        in_specs += [pl.BlockSpec((chunk,w), lambda j, d=d: (nat[d](j), 0)),
                     pl.BlockSpec((chunk,w), lambda j, d=d: (nat[d](j), 1)),
                     pl.BlockSpec((chunk,w), lambda j, d=d: (nat[d](j), 7)),
                     pl.BlockSpec((chunk,LANE), lambda j, d=d: (nat[d](j), 10 * w // LANE)),
                     pl.BlockSpec((1, LANE), lambda j: (0, 0))]
        out_specs += [pl.BlockSpec((chunk,w), lambda j, d=d: (nat[d](j), 0))]
        out_specs += list(_ml_state_specs(n_heads, dh, lambda j: j))
        out_shape += [jax.ShapeDtypeStruct((r, w), F32)] + list(_ml_state_shapes(n_chunks, n_heads, dh))
    res = pl.pallas_call(
        body, name="ml_scan_fwd", grid=(n_chunks,), in_specs=in_specs, out_specs=tuple(out_specs),
        out_shape=tuple(out_shape), scratch_shapes=_ml_state_scratch(n_heads, dh) * 2,
        compiler_params=_params(("arbitrary",)),
    )(qk, qk, u, u, gate_b, qk, qk, u, u, gate_b)
    return (res[0], res[4]), (res[1:4], res[5:8])


def _ml_scan_bwd(qk, u, gate_b, saved, d_h, w, n_heads, n_lat, n_ctx, chunk, rider=None):
    r = u.shape[0]
    dh = w // n_heads
    n_chunks = n_lat + n_ctx
    step = lambda jj: n_chunks - 1 - jj
    nat = [(lambda jj, o=_scan_order(n_lat, n_ctx, rev): o(step(jj))) for rev in DIRS]

    def body(*refs):
        ins, outs, scratch = refs[:18], refs[18:26], refs[26:]
        jj = pl.program_id(0)

        @pl.when(jj == 0)
        def _():
            for s_ref in scratch:
                s_ref[...] = jnp.zeros_like(s_ref)
            for d in range(2):
                outs[4 * d + 3][...] = jnp.zeros_like(outs[4 * d + 3])

        results = []
        for d, rev in enumerate(DIRS):
            q, k, v, g, gb, sc, sn, sm, dh_ref = ins[9 * d:9 * d + 9]
            state = tuple([ref[0, h] for h in range(n_heads)] for ref in (sc, sn, sm))
            f = lambda st, a, b, c, gg, bb, rev=rev, d=d: _ml_chunk(st, a, b, c, gg, bb, rev, d)
            _, vjp = jax.vjp(f, state, q[...], k[...], v[...], g[...], gb[...])
            d_state = tuple([ref[h] for h in range(n_heads)] for ref in scratch[3 * d:3 * d + 3])
            d_out = dh_ref[...] * (nat[d](jj) < n_lat).astype(F32)
            results.append(vjp((d_state, d_out)))
        for d, (d_state, dq, dk, dv, dg, dgb) in enumerate(results):
            dqk_ref, dv_ref, dg_ref, dgb_ref = outs[4 * d:4 * d + 4]
            for part in range(3):
                for h in range(n_heads):
                    scratch[3 * d + part][h] = d_state[part][h]
            dqk_ref[:, 0:w] = dq
            dqk_ref[:, w:2 * w] = dk
            dv_ref[...] = dv.astype(BF16)
            dg_ref[...] = dg
            dgb_ref[...] += dgb

    in_specs, out_specs, out_shape, operands = [], [], [], []
    for d in range(2):
        row = lambda jj, d=d: (nat[d](jj), 0)
        in_specs += [pl.BlockSpec((chunk,w), row), pl.BlockSpec((chunk,w), lambda jj, d=d: (nat[d](jj), 1)),
                     pl.BlockSpec((chunk,w), lambda jj, d=d: (nat[d](jj), 7)),
                     pl.BlockSpec((chunk,LANE), lambda jj, d=d: (nat[d](jj), 10 * w // LANE)),
                     pl.BlockSpec((1, LANE), lambda jj: (0, 0))]
        in_specs += list(_ml_state_specs(n_heads, dh, step))
        in_specs += [pl.BlockSpec((chunk,w), lambda jj, d=d: (jnp.minimum(nat[d](jj), n_lat - 1), 0))]
        operands += [qk, qk, u, u, gate_b, *saved[d], d_h]
        out_specs += [pl.BlockSpec((chunk,2 * w), row), pl.BlockSpec((chunk,w), row),
                      pl.BlockSpec((chunk,LANE), row), pl.BlockSpec((1, LANE), lambda jj: (0, 0))]
        out_shape += [jax.ShapeDtypeStruct((r, 2 * w), F32), jax.ShapeDtypeStruct((r, w), BF16),
                      jax.ShapeDtypeStruct((r, LANE), F32), jax.ShapeDtypeStruct((1, LANE), F32)]
    res, rode = _call(
        body, operands, name="ml_scan_bwd", grid=(n_chunks,), in_specs=in_specs, out_specs=out_specs,
        out_shape=out_shape, scratch_shapes=_ml_state_scratch(n_heads, dh) * 2, sem=("arbitrary",), rider=rider)
    return res[0:4], res[4:8], rode


def _post_specs(w, tm, lat_tiles, cols):
    return [pl.BlockSpec((tm, w), (lambda i, cb=cb: (jnp.minimum(i, lat_tiles - 1), cb))) for cb in cols]


def _post_fwd(o_f, o_b, h_f, h_b, u, wa, wb, t_rows, w, n_hg, n_ml, tm):
    lat_tiles = t_rows // tm

    def body(of, ob, hf, hb, az, bo, bz, wa_ref, wb_ref, y_ref):
        y_ref[...] = _post_fn(of[...], ob[...], az[...], hf[...], hb[...], bo[...], bz[...],
                              wa_ref[...], wb_ref[...], n_hg, n_ml).astype(BF16)

    rows = pl.BlockSpec((tm, w), lambda i: (i, 0))
    vec = pl.BlockSpec((1, w), lambda i: (0, 0))
    return pl.pallas_call(
        body, name="post_fwd", grid=(lat_tiles,),
        in_specs=[rows] * 4 + _post_specs(w, tm, lat_tiles, (4, 8, 9)) + [vec, vec],
        out_specs=pl.BlockSpec((tm, 2 * w), lambda i: (i, 0)),
        out_shape=jax.ShapeDtypeStruct((t_rows, 2 * w), BF16),
        compiler_params=_params(("parallel",)),
    )(o_f, o_b, h_f, h_b, u, u, u, wa, wb)


def _post_bwd(o_f, o_b, h_f, h_b, u, wa, wb, dy, t_rows, w, n_hg, n_ml, tm, rider=None):
    r = u.shape[0]
    lat_tiles = t_rows // tm
    lat = lambda i: (jnp.minimum(i, lat_tiles - 1), 0)

    def body(of, ob, hf, hb, az, bo, bz, wa_ref, wb_ref, dy_ref, do_ref, dh_ref, daz_ref, dbo_ref, dbz_ref,
             dwa_ref, dwb_ref):
        i = pl.program_id(0)

        @pl.when(i == 0)
        def _():
            dwa_ref[...] = jnp.zeros_like(dwa_ref)
            dwb_ref[...] = jnp.zeros_like(dwb_ref)

        @pl.when(i < lat_tiles)
        def _():
            f = functools.partial(_post_fn, n_hg=n_hg, n_ml=n_ml)
            _, vjp = jax.vjp(f, of[...], ob[...], az[...], hf[...], hb[...], bo[...], bz[...], wa_ref[...], wb_ref[...])
            d_of, _, d_az, d_hf, _, d_bo, d_bz, d_wa, d_wb = vjp(dy_ref[...])
            do_ref[...] = d_of
            dh_ref[...] = d_hf
            daz_ref[...] = d_az.astype(BF16)
            dbo_ref[...] = d_bo.astype(BF16)
            dbz_ref[...] = d_bz.astype(BF16)
            dwa_ref[...] += d_wa
            dwb_ref[...] += d_wb

        @pl.when(i >= lat_tiles)
        def _():
            daz_ref[...] = jnp.zeros_like(daz_ref)
            dbo_ref[...] = jnp.zeros_like(dbo_ref)
            dbz_ref[...] = jnp.zeros_like(dbz_ref)

    lat_rows = pl.BlockSpec((tm, w), lat)
    all_rows = pl.BlockSpec((tm, w), lambda i: (i, 0))
    vec = pl.BlockSpec((1, w), lambda i: (0, 0))
    sd_t = jax.ShapeDtypeStruct((t_rows, w), F32)
    sd_r = jax.ShapeDtypeStruct((r, w), BF16)
    sd_v = jax.ShapeDtypeStruct((1, w), F32)
    return _call(
        body, (o_f, o_b, h_f, h_b, u, u, u, wa, wb, dy), name="post_bwd", grid=(r // tm,),
        in_specs=[lat_rows] * 4 + _post_specs(w, tm, lat_tiles, (4, 8, 9)) + [vec, vec]
        + [pl.BlockSpec((tm, 2 * w), lat)],
        out_specs=(lat_rows, lat_rows, all_rows, all_rows, all_rows, vec, vec),
        out_shape=(sd_t, sd_t, sd_r, sd_r, sd_r, sd_v, sd_v), sem=("arbitrary",), rider=rider)


OUT_ROW_GATE, OUT_ROW_LN_G, OUT_ROW_LN_B, OUT_ROW_LOSS = 0, 1, 2, 3


def _out_block(y, w_out, x, target, prm, tm):
    t_rows, dm = x.shape
    di = y.shape[1]

    def body(y_ref, w_ref, x_ref, t_ref, p_ref, dz_ref, dy_ref, gx_ref, acc_ref):
        @pl.when(pl.program_id(0) == 0)
        def _():
            acc_ref[...] = jnp.zeros_like(acc_ref)

        gate, ln_g, ln_b = p_ref[0:1, :], p_ref[1:2, :], p_ref[2:3, :]
        z = _nn(y_ref[...], w_ref[...])
        res = ALPHA * x_ref[...] + gate * z
        mu = jnp.mean(res, axis=-1, keepdims=True)
        rc = res - mu
        rstd = lax.rsqrt(jnp.mean(rc * rc, axis=-1, keepdims=True) + LN_EPS)
        rn = rc * rstd
        err = rn * ln_g + ln_b - t_ref[...]
        d_out = err * (1.0 / dm)
        d_rn = d_out * ln_g
        d_res = rstd * (d_rn - jnp.mean(d_rn, axis=-1, keepdims=True)
                        - rn * jnp.mean(d_rn * rn, axis=-1, keepdims=True))
        acc_ref[OUT_ROW_GATE:OUT_ROW_GATE + 1, :] += jnp.sum(d_res * z, axis=0, keepdims=True)
        acc_ref[OUT_ROW_LN_G:OUT_ROW_LN_G + 1, :] += jnp.sum(d_out * rn, axis=0, keepdims=True)
        acc_ref[OUT_ROW_LN_B:OUT_ROW_LN_B + 1, :] += jnp.sum(d_out, axis=0, keepdims=True)
        acc_ref[OUT_ROW_LOSS:OUT_ROW_LOSS + 1, :] += (0.5 / dm) * jnp.sum(err * err, axis=0, keepdims=True)
        gx_ref[...] = ALPHA * d_res
        dz = (d_res * gate).astype(BF16)
        dz_ref[...] = dz
        dy_ref[...] = _nt(dz, w_ref[...])

    rows_d = pl.BlockSpec((tm, dm), lambda i: (i, 0))
    rows_i = pl.BlockSpec((tm, di), lambda i: (i, 0))
    return pl.pallas_call(
        body, name="out_block", grid=(t_rows // tm,),
        in_specs=[rows_i, pl.BlockSpec((di, dm), lambda i: (0, 0)), rows_d, rows_d,
                  pl.BlockSpec((8, dm), lambda i: (0, 0))],
        out_specs=(rows_d, rows_i, rows_d, pl.BlockSpec((8, dm), lambda i: (0, 0))),
        out_shape=(jax.ShapeDtypeStruct((t_rows, dm), BF16), jax.ShapeDtypeStruct((t_rows, di), F32),
                   jax.ShapeDtypeStruct((t_rows, dm), F32), jax.ShapeDtypeStruct((8, dm), F32)),
        compiler_params=_params(("arbitrary",)),
    )(y, w_out, x, target, prm)


def _mod_fwd(c16, w_mod, tn):
    dm, n = w_mod.shape

    def body(c_ref, w_ref, o_ref, a_ref):
        a = _silu(c_ref[...])
        a_ref[...] = a
        o_ref[...] = _nn(a, w_ref[...], HIGHEST)

    return pl.pallas_call(
        body, name="mod_fwd", grid=(n // tn,),
        in_specs=[pl.BlockSpec((16, dm), lambda j: (0, 0)), pl.BlockSpec((dm, tn), lambda j: (0, j))],
        out_specs=(pl.BlockSpec((16, tn), lambda j: (0, j)), pl.BlockSpec((16, dm), lambda j: (0, 0))),
        out_shape=(jax.ShapeDtypeStruct((16, n), F32), jax.ShapeDtypeStruct((16, dm), F32)),
        compiler_params=_params(("arbitrary",)),
    )(c16, w_mod)


def _mod_bwd(a16, dm16, w_mod, tn):
    dm, n = w_mod.shape

    def body(a_ref, d_ref, w_ref, dw_ref, dc_ref):
        @pl.when(pl.program_id(0) == 0)
        def _():
            dc_ref[...] = jnp.zeros_like(dc_ref)
        dw_ref[...] = _tn(a_ref[...], d_ref[...], HIGHEST)
        dc_ref[...] += _nt(d_ref[...], w_ref[...], HIGHEST)

    return pl.pallas_call(
        body, name="mod_bwd", grid=(n // tn,),
        in_specs=[pl.BlockSpec((16, dm), lambda j: (0, 0)), pl.BlockSpec((16, tn), lambda j: (0, j)),
                  pl.BlockSpec((dm, tn), lambda j: (0, j))],
        out_specs=(pl.BlockSpec((dm, tn), lambda j: (0, j)), pl.BlockSpec((16, dm), lambda j: (0, 0))),
        out_shape=(jax.ShapeDtypeStruct((dm, n), F32), jax.ShapeDtypeStruct((16, dm), F32)),
        compiler_params=_params(("arbitrary",)),
    )(a16, dm16, w_mod)


def _sum_devices(g, fold_rows):
    n_dev, rows, n = g.shape

    def body(g_ref, s_ref, t_ref):
        s = g_ref[0]
        for dev in range(1, n_dev):
            s = s + g_ref[dev]
        t_ref[...] = jnp.broadcast_to(jnp.sum(s, axis=-1, keepdims=True), (rows, LANE))
        s_ref[...] = s
        s_ref[0:fold_rows, :] = s[0:fold_rows] + s[fold_rows:2 * fold_rows]

    return pl.pallas_call(
        body, name="sum_devices",
        out_shape=(jax.ShapeDtypeStruct((rows, n), F32), jax.ShapeDtypeStruct((rows, LANE), F32)),
        compiler_params=_params(),
    )(g)


def _c_ctx_grad(parts, c_ctx_row):
    def body(p_ref, c_ref, o_ref):
        s = p_ref[0]
        for chip in range(1, N_CHIPS):
            s = s + p_ref[2 * chip]
        cv = c_ref[...]
        sg = _sigmoid(cv)
        o_ref[...] = s * (sg * (1.0 + cv * (1.0 - sg)))

    return pl.pallas_call(
        body, name="c_ctx_grad", out_shape=jax.ShapeDtypeStruct(parts.shape[1:], F32), compiler_params=_params(),
    )(parts, c_ctx_row)


def _sum_pair(name, mine, got):
    def body(a_ref, b_ref, o_ref):
        o_ref[...] = (a_ref[...] + b_ref[...]).astype(BF16)

    k, rows, n = mine.shape
    tl = _largest_divisor(n, max(LANE, (1 << 18) // rows), LANE)
    spec = pl.BlockSpec((1, rows, tl), lambda kk, i: (kk, 0, i))
    return pl.pallas_call(
        body, name=name, grid=(k, n // tl), in_specs=[spec, spec], out_specs=spec,
        out_shape=jax.ShapeDtypeStruct(mine.shape, BF16), compiler_params=_params(("parallel", "parallel")),
    )(mine, got)


def _sum_pair_lanes(name, full, got, ci):
    rows, n = got.shape
    tr = _largest_divisor(rows, max(SUBLANE_BF16, (1 << 19) // n), SUBLANE_BF16)

    def body(ci_ref, a_ref, b_ref, o_ref):
        o_ref[...] = (a_ref[...] + b_ref[...].astype(F32)).astype(BF16)

    return pl.pallas_call(
        body, name=name,
        grid_spec=pltpu.PrefetchScalarGridSpec(
            num_scalar_prefetch=1, grid=(rows // tr,),
            in_specs=[pl.BlockSpec((tr, n), lambda i, c: (i, c[0])), pl.BlockSpec((tr, n), lambda i, c: (i, 0))],
            out_specs=pl.BlockSpec((tr, n), lambda i, c: (i, 0))),
        out_shape=jax.ShapeDtypeStruct((rows, n), BF16), compiler_params=_params(("parallel",)),
    )(ci.reshape(1).astype(jnp.int32), full, got)


def _sum_chips(name, got, own, chip):
    k, rows, n = got.shape
    tl = _largest_divisor(n, max(LANE, (1 << 18) // rows), LANE)

    def body(chip_ref, g_ref, own_ref, o_ref):
        total = None
        for kk in range(k):
            term = jnp.where(chip_ref[0] == kk, own_ref[0], g_ref[kk]).astype(F32)
            total = term if total is None else total + term
        o_ref[...] = total

    return pl.pallas_call(
        body, name=name,
        grid_spec=pltpu.PrefetchScalarGridSpec(
            num_scalar_prefetch=1, grid=(n // tl,),
            in_specs=[pl.BlockSpec((k, rows, tl), lambda i, c: (0, 0, i)),
                      pl.BlockSpec((1, rows, tl), lambda i, c: (c[0], 0, i))],
            out_specs=pl.BlockSpec((rows, tl), lambda i, c: (0, i))),
        out_shape=jax.ShapeDtypeStruct((rows, n), F32), compiler_params=_params(("parallel",)),
    )(chip.reshape(1).astype(jnp.int32), got, own)


def _adamw_update(w, g, m, v):
    m2 = ADAM_B1 * m + (1.0 - ADAM_B1) * g
    v2 = ADAM_B2 * v + (1.0 - ADAM_B2) * jnp.square(g)
    m_hat = m2 / (1.0 - ADAM_B1 ** ADAM_STEP)
    v_hat = v2 / (1.0 - ADAM_B2 ** ADAM_STEP)
    return -ADAM_LR * (m_hat / (jnp.sqrt(v_hat) + ADAM_EPS) + ADAM_WD * w), m2, v2


def _adamw(name, w, g, m, v, rider=None):
    rows, n = w.shape
    if rows % 8 == 0:
        tr = _largest_divisor(rows, max(8, (1 << 18) // n), 8)
        block, index, steps = (tr, n), (lambda i: (i, 0)), rows // tr
    else:
        tl = _largest_divisor(n, max(LANE, (1 << 18) // rows), LANE)
        block, index, steps = (rows, tl), (lambda i: (0, i)), n // tl

    def body(w_ref, g_ref, m_ref, v_ref, d_ref, mo_ref, vo_ref):
        d_ref[...], mo_ref[...], vo_ref[...] = _adamw_update(w_ref[...], g_ref[...], m_ref[...], v_ref[...])

    spec = pl.BlockSpec(block, index)
    sds = jax.ShapeDtypeStruct((rows, n), F32)
    return _call(body, (w, g, m, v), name=name, grid=(steps,), in_specs=[spec] * 4, out_specs=(spec,) * 3,
                 out_shape=(sds, sds, sds), sem=("parallel",), rider=rider)


PACK_LANES = 1024


def _pack(pieces):
    flat = jnp.concatenate([p.reshape(-1) for p in pieces])
    total = -(-flat.shape[0] // (8 * PACK_LANES)) * 8 * PACK_LANES
    return jnp.pad(flat, (0, total - flat.shape[0])).reshape(-1, PACK_LANES)


def _unpack(packed, shapes):
    flat = packed.reshape(-1)
    out, off = [], 0
    for shp in shapes:
        size = math.prod(shp)
        out.append(flat[off:off + size].reshape(shp))
        off += size
    return out


def _rows8(rows, width):
    flat = [r.reshape(width) for r in rows] + [jnp.zeros(((8 - len(rows)) * width,), F32)]
    return jnp.concatenate(flat).reshape(8, width)


def kernel(x, c, ctx, c_ctx, w_mod, b_mod, w_in, conv_w, conv_b, hg_lb, ml_gate_b, hg_norm_w, ml_norm_w, w_out, ln_g, ln_b, loss_target, m_c_ctx, m_w_mod, m_b_mod, m_w_in, m_conv_w, m_conv_b, m_hg_lb, m_ml_gate_b, m_hg_norm_w, m_ml_norm_w, m_w_out, m_ln_g, m_ln_b, v_c_ctx, v_w_mod, v_b_mod, v_w_in, v_conv_w, v_conv_b, v_hg_lb, v_ml_gate_b, v_hg_norm_w, v_ml_norm_w, v_w_out, v_ln_g, v_ln_b):
    t_rows, dm = x.shape[1], x.shape[2]
    c_rows = ctx.shape[1]
    w = hg_norm_w.shape[1]
    n_ml = ml_gate_b.shape[-1]
    n_hg = w // HG_DK
    di = 2 * w
    n_in = 10 * w + 4 * n_ml
    ns = w_in.shape[2]
    nm = w_mod.shape[2]
    n_pad = 10 * w + LANE
    r_rows = t_rows + c_rows
    row_gcd = math.gcd(t_rows, c_rows)
    hg_chunk, ml_chunk = math.gcd(HG_CHUNK, row_gcd), math.gcd(ML_CHUNK, row_gcd)
    hg_counts = (t_rows // hg_chunk, c_rows // hg_chunk, hg_chunk)
    ml_counts = (t_rows // ml_chunk, c_rows // ml_chunk, ml_chunk)
    assert ml_norm_w.shape[1] == w and di == dm and N_CHIPS * ns == n_in and N_CHIPS * nm == 3 * dm
    assert w_out.shape[1] * N_CHIPS == di and 4 * n_ml <= LANE and t_rows % GRID_W == 0

    xi, yi, ci = lax.axis_index("x"), lax.axis_index("y"), lax.axis_index("c")
    chip = 2 * xi + yi
    dev = 4 * xi + 2 * yi + ci

    tm = _largest_divisor(math.gcd(t_rows, c_rows), 256, 8)
    tm_mm = _largest_divisor(r_rows, 1088, SUBLANE_BF16)
    tn_mm = LANE * _largest_divisor(n_pad // LANE, 9)
    tn_mod = _largest_divisor(nm, 512, LANE)

    shard_shapes = [(dm,), (2, 2, w // N_CHIPS), (3, 3, di // N_CHIPS)]
    g1 = _all_gather8(_pack([c, hg_lb, conv_w])).run("gather_inputs")[0]
    per_dev = [_unpack(g1[i], shard_shapes) for i in range(N_DEV)]
    c_all = jnp.stack([p[0] for p in per_dev])
    lb_full = jnp.concatenate([per_dev[2 * k][1] for k in range(N_CHIPS)], axis=-1)
    conv_w9 = jnp.concatenate([per_dev[2 * k][2] for k in range(N_CHIPS)], axis=-1).reshape(9, di)

    c16 = jnp.concatenate([c_all, c_ctx[None], jnp.zeros((16 - N_DEV - 1, dm), F32)])
    mod_part, a16 = _mod_fwd(c16, w_mod[0], tn_mod)
    g2 = _all_gather8(mod_part).run("gather_mod")[0]
    mod_all = jnp.concatenate([g2[2 * k] for k in range(N_CHIPS)], axis=1) + b_mod
    mod_x = lax.dynamic_index_in_dim(mod_all, dev, 0, keepdims=False).reshape(3, dm)
    mod_c = mod_all[N_DEV].reshape(3, dm)
    prm = jnp.stack([_rows8(list(mod_x), dm), _rows8(list(mod_c), dm)])

    as_t = lambda a: jnp.transpose(a[0])
    half_in = lax.dynamic_slice_in_dim(as_t(w_in).astype(BF16), ci * (dm // 2), dm // 2, 1)
    half_out = lax.dynamic_slice_in_dim(w_out[0].astype(BF16), ci * (di // (2 * N_CHIPS)), di // (2 * N_CHIPS), 0)
    lanes_of = lambda core: pl.ds(core * (dm // 2), dm // 2)
    landing = lambda s, r: (_chip_of(s), slice(None), lanes_of(s[2]))
    own_placed = lax.dynamic_update_slice(jnp.zeros((N_CHIPS, ns, dm), BF16), as_t(w_in).astype(BF16)[None],
                                          (chip, 0, 0))
    gather_in = _Exchange([half_in, own_placed], [jax.ShapeDtypeStruct(own_placed.shape, BF16)],
                          [(mask, 0, None, 0, landing) for mask in CHIP_MASKS], in_place={1: 0})

    hc, (gw_in,) = _modulate_fwd(x[0], ctx[0], prm, tm, rider=gather_in)
    my_lanes = lambda s, r: (slice(None), slice(None), lanes_of(s[2]))
    gw_in = _Exchange([gw_in], [jax.ShapeDtypeStruct(gw_in.shape, BF16)],
                      [(SIBLING_MASK, 0, my_lanes, 0, my_lanes)], in_place={0: 0}).run("gather_w_in_pair")[0]
    wt_full = jnp.concatenate([gw_in.reshape(n_in, dm), jnp.zeros((n_pad - n_in, dm), BF16)])
    u, (got_out,) = _mm_nt("in_proj", hc, wt_full, tm_mm, tn_mm, F32, rider=_all_gather_chips([half_out]))
    fetched_out = _own_block(chip, half_out, got_out)
    (o_f, o_b), hg_saved, (swapped_out,) = _hg_scan_fwd(u, lb_full, w, *hg_counts,
                                                         rider=_sibling_swap([fetched_out]))
    w_out_full = _join_halves(ci, fetched_out, swapped_out, 1).reshape(di, dm)
    qk = _conv_fwd(u, conv_w9, conv_b, t_rows, c_rows, w, LANE)
    gate_b_row = jnp.pad(ml_gate_b.reshape(1, -1), ((0, 0), (0, LANE - 4 * n_ml)))
    (h_f, h_b), ml_saved = _ml_scan_fwd(qk, u, gate_b_row, w, n_ml, *ml_counts)
    y = _post_fwd(o_f, o_b, h_f, h_b, u, hg_norm_w, ml_norm_w, t_rows, w, n_hg, n_ml, tm)
    prm_out = _rows8([mod_x[2], ln_g, ln_b], dm)
    dz, dy, gx_direct, acc_out = _out_block(y, w_out_full, x[0], loss_target[0], prm_out, tm)

    d_w_out = _mm_tn("d_w_out", y, dz, _largest_divisor(di, 1024, LANE),
                     _largest_divisor(t_rows, 1024, SUBLANE_BF16))
    d_w_out4 = d_w_out.reshape(N_CHIPS, 2, di // (2 * N_CHIPS), dm)
    mine_out = lax.dynamic_index_in_dim(d_w_out4, ci, 1, keepdims=False)
    other_out = lax.dynamic_index_in_dim(d_w_out4, 1 - ci, 1, keepdims=False)
    (d_o, d_h, d_az, d_bo, d_bz, d_wa, d_wb), (got_out,) = _post_bwd(
        o_f, o_b, h_f, h_b, u, hg_norm_w, ml_norm_w, dy, t_rows, w, n_hg, n_ml, tm, rider=_sibling_swap([other_out]))
    pair_out = _sum_pair("rs_pair_sum_w_out", mine_out, got_out)
    (d_aq_f, d_aff, d_ai_f, d_lb_f), (d_aq_b, d_afb, d_ai_b, d_lb_b), (landed_out,) = _hg_scan_bwd(
        u, lb_full, hg_saved, d_o, w, *hg_counts, rider=_chip_scatter([pair_out]))
    half_g_out = _sum_chips("rs_chip_sum_w_out", landed_out, pair_out, chip)
    (d_qk_f, d_v_f, d_g_f, d_gb_f), (d_qk_b, d_v_b, d_g_b, d_gb_b), (sibling_out,) = _ml_scan_bwd(
        qk, u, gate_b_row, ml_saved, d_h, w, n_ml, *ml_counts, rider=_sibling_swap([half_g_out]))
    g_w_out = _join_halves(ci, half_g_out, sibling_out, 0)
    d_bqk, d_cw, d_cb = _conv_bwd(u, (d_qk_f, d_qk_b), conv_w9, conv_b, t_rows, c_rows, w, LANE)
    du = _assemble_du([(d_aq_f, d_aq_b), d_aff, d_afb, (d_ai_f, d_ai_b), d_az, d_bqk, (d_v_f, d_v_b), d_bo, d_bz],
                      (d_g_f, d_g_b), n_pad, tm // 2)
    d_wt_in, d_wt_in_bf16 = _mm_tn("d_w_in", du, hc, tn_mm, tm_mm, with_bf16=True)

    got_in = _Exchange([d_wt_in_bf16], [jax.ShapeDtypeStruct((n_pad, dm // 2), BF16)],
                       [(SIBLING_MASK, 0, lambda s, r: (slice(None), lanes_of(r[2])), 0, None)]).run("rs_pair_w_in")[0]
    pair_half = _sum_pair_lanes("rs_pair_sum_w_in", d_wt_in, got_in, ci)
    pair_in = jnp.stack([pair_half[k * ns:(k + 1) * ns] for k in range(N_CHIPS)])
    d_hc, (landed_in,) = _mm_acc("d_h", du, wt_full, tm_mm, tn_mm, rider=_chip_scatter([pair_in]))
    half_g_in = _sum_chips("rs_chip_sum_w_in", landed_in, pair_in, chip)
    g_wt_in = _join_halves(ci, half_g_in, _sibling_swap([half_g_in]).run("rs_join_w_in")[0], 1)
    (gx, acc_mod), _ = _modulate_bwd(x[0], ctx[0], d_hc, prm, gx_direct, tm)
    grad_x = gx[None]

    zero_row = jnp.zeros((dm,), F32)
    d_gb = jnp.concatenate([d_gb_f[:, 0:n_ml], d_gb_b[:, n_ml:2 * n_ml], d_gb_f[:, 2 * n_ml:3 * n_ml],
                            d_gb_b[:, 3 * n_ml:4 * n_ml], jnp.zeros((1, dm - 4 * n_ml), F32)], axis=1)
    rows = [acc_mod[0, 0], acc_mod[0, 1], acc_out[OUT_ROW_GATE],
            acc_mod[1, 0], acc_mod[1, 1], zero_row]
    rows += list(d_cw) + [d_cb[0], d_lb_f.reshape(dm), d_lb_b.reshape(dm),
                          jnp.concatenate([d_wa[0], d_wb[0]]), acc_out[OUT_ROW_LN_G], acc_out[OUT_ROW_LN_B],
                          acc_out[OUT_ROW_LOSS], d_gb[0], zero_row]
    ROW_CW, ROW_CB, ROW_LB, ROW_NORM, ROW_LN_G, ROW_LN_B, ROW_LOSS, ROW_GB = 6, 15, 16, 18, 19, 20, 21, 22
    delta, new_m, new_v = {}, {}, {}
    small_rows = jnp.concatenate([r.reshape(dm) for r in rows]).reshape(len(rows), dm)
    g3 = _all_gather8(small_rows).run("gather_small_grads")[0]
    sums, totals = _sum_devices(g3, 3)
    loss = totals[ROW_LOSS, 0]
    dm16 = jnp.concatenate([g3[:, 0:3, :].reshape(N_DEV, 3 * dm), sums[3:6].reshape(1, 3 * dm),
                            jnp.zeros((16 - N_DEV - 1, 3 * dm), F32)])
    g_w_mod, dc16 = _mod_bwd(a16, lax.dynamic_slice_in_dim(dm16, chip * nm, nm, 1), w_mod[0], tn_mod)
    g4 = _all_gather8(jnp.pad(dc16[N_DEV:N_DEV + 1], ((0, 7), (0, 0)))).run("gather_c_ctx")[0]
    g_c_ctx = _c_ctx_grad(g4, jnp.broadcast_to(c_ctx[None], (8, dm)))[0]
    res, _ = _adamw("adamw_w_in", as_t(w_in), g_wt_in, as_t(m_w_in), as_t(v_w_in))
    delta["w_in"], new_m["w_in"], new_v["w_in"] = (jnp.transpose(a)[None] for a in res)
    res, _ = _adamw("adamw_w_mod", w_mod[0], g_w_mod, m_w_mod[0], v_w_mod[0])
    delta["w_mod"], new_m["w_mod"], new_v["w_mod"] = (a[None] for a in res)

    chip_cols = lambda a, width: lax.dynamic_slice_in_dim(a, chip * width, width, a.ndim - 1)
    grads = {
        "c_ctx": g_c_ctx,
        "w_mod": g_w_mod[None],
        "b_mod": sums[0:3].reshape(1, 3 * dm),
        "w_in": jnp.transpose(g_wt_in)[None],
        "conv_w": chip_cols(sums[ROW_CW:ROW_CW + 9].reshape(1, 3, 3, di), di // N_CHIPS),
        "conv_b": sums[ROW_CB][None],
        "hg_lb": chip_cols(sums[ROW_LB:ROW_LB + 2].reshape(2, 2, w), w // N_CHIPS),
        "ml_gate_b": sums[ROW_GB, 0:4 * n_ml].reshape(1, 4, n_ml),
        "hg_norm_w": sums[ROW_NORM, 0:w][None],
        "ml_norm_w": sums[ROW_NORM, w:2 * w][None],
        "w_out": g_w_out[None],
        "ln_g": sums[ROW_LN_G][None],
        "ln_b": sums[ROW_LN_B][None],
    }
    weights = dict(c_ctx=c_ctx, w_mod=w_mod, b_mod=b_mod, w_in=w_in, conv_w=conv_w, conv_b=conv_b, hg_lb=hg_lb,
                   ml_gate_b=ml_gate_b, hg_norm_w=hg_norm_w, ml_norm_w=ml_norm_w, w_out=w_out, ln_g=ln_g, ln_b=ln_b)
    mom1 = dict(c_ctx=m_c_ctx, w_mod=m_w_mod, b_mod=m_b_mod, w_in=m_w_in, conv_w=m_conv_w, conv_b=m_conv_b,
                hg_lb=m_hg_lb, ml_gate_b=m_ml_gate_b, hg_norm_w=m_hg_norm_w, ml_norm_w=m_ml_norm_w, w_out=m_w_out,
                ln_g=m_ln_g, ln_b=m_ln_b)
    mom2 = dict(c_ctx=v_c_ctx, w_mod=v_w_mod, b_mod=v_b_mod, w_in=v_w_in, conv_w=v_conv_w, conv_b=v_conv_b,
                hg_lb=v_hg_lb, ml_gate_b=v_ml_gate_b, hg_norm_w=v_hg_norm_w, ml_norm_w=v_ml_norm_w, w_out=v_w_out,
                ln_g=v_ln_g, ln_b=v_ln_b)
    names = list(weights)
    big = ("w_mod", "w_in", "w_out")
    small = [n for n in names if n not in big]

    res, _ = _adamw("adamw_w_out", w_out[0], g_w_out, m_w_out[0], v_w_out[0])
    delta["w_out"], new_m["w_out"], new_v["w_out"] = (a[None] for a in res)
    small_shapes = [weights[n].shape for n in small]
    res, _ = _adamw("adamw_small", *(_pack([src[n] for n in small]) for src in (weights, grads, mom1, mom2)))
    for out, packed in zip((delta, new_m, new_v), res):
        for n, a in zip(small, _unpack(packed, small_shapes)):
            out[n] = a

    return (loss, grad_x, *[grads[n].reshape(weights[n].shape) for n in names], *[delta[n] for n in names],
            *[new_m[n] for n in names], *[new_v[n] for n in names])
```

```python
import functools
import math

import jax
import jax.numpy as jnp
from jax import lax
from jax.experimental import pallas as pl
from jax.experimental.pallas import tpu as pltpu

F32 = jnp.float32
BF16 = jnp.bfloat16
HIGHEST = lax.Precision.HIGHEST
MESH = pl.DeviceIdType.MESH

HG_CHUNK = 64
ML_CHUNK = 256
GRID_W = 64
HG_DK = 128
LANE = 128
SUBLANE_BF16 = 16
ALPHA = 2.0 ** 0.25
LN_EPS = 1e-5
NORM_EPS = 1e-6
ADAM_LR = 0.001
ADAM_B1 = 0.9
ADAM_B2 = 0.999
ADAM_EPS = 1e-08
ADAM_WD = 0.01
ADAM_STEP = 10
VMEM_LIMIT = 56 * 1024 * 1024
N_CHIPS = 4
N_DEV = 8


def _params(sem=None):
    return pltpu.CompilerParams(dimension_semantics=sem, vmem_limit_bytes=VMEM_LIMIT)


def _largest_divisor(n, cap, multiple=1):
    best = None
    for d in range(multiple, min(n, cap) + 1, multiple):
        if n % d == 0:
            best = d
    assert best is not None, (n, cap, multiple)
    return best


def _sigmoid(x):
    return jax.nn.sigmoid(x)


def _silu(x):
    return x * jax.nn.sigmoid(x)


def _dot(a, b, dims, precision=None):
    return lax.dot_general(a, b, (dims, ((), ())), precision=precision, preferred_element_type=F32)


def _nn(a, b, precision=None):
    return _dot(a, b, ((1,), (0,)), precision)


def _nt(a, b, precision=None):
    return _dot(a, b, ((1,), (1,)), precision)


def _tn(a, b, precision=None):
    return _dot(a, b, ((0,), (0,)), precision)


def _narrow(x):
    return x.astype(BF16)


@jax.custom_vjp
def _bnn(a, b):
    return _nn(_narrow(a), _narrow(b))


def _bnn_fwd(a, b):
    an, bn = _narrow(a), _narrow(b)
    return _nn(an, bn), (an, bn)


def _bnn_bwd(res, ct):
    an, bn = res
    ctn = _narrow(ct)
    return _nt(ctn, bn), _tn(an, ctn)


_bnn.defvjp(_bnn_fwd, _bnn_bwd)


@jax.custom_vjp
def _bnt(a, b):
    return _nt(_narrow(a), _narrow(b))


def _bnt_fwd(a, b):
    an, bn = _narrow(a), _narrow(b)
    return _nt(an, bn), (an, bn)


def _bnt_bwd(res, ct):
    an, bn = res
    ctn = _narrow(ct)
    return _nn(ctn, bn), _tn(ctn, an)


_bnt.defvjp(_bnt_fwd, _bnt_bwd)


@jax.custom_vjp
def _btn(a, b):
    return _tn(_narrow(a), _narrow(b))


def _btn_fwd(a, b):
    an, bn = _narrow(a), _narrow(b)
    return _tn(an, bn), (an, bn)


def _btn_bwd(res, ct):
    an, bn = res
    ctn = _narrow(ct)
    return _nt(bn, ctn), _nn(an, ctn)


_btn.defvjp(_btn_fwd, _btn_bwd)


def _visible(n, rev):
    r = lax.broadcasted_iota(jnp.int32, (n, n), 0)
    c = lax.broadcasted_iota(jnp.int32, (n, n), 1)
    return (r <= c) if rev else (r >= c)


def _mask_matmul(mask, x):
    mb = mask.astype(BF16)
    hi = x.astype(BF16)
    lo = (x - hi.astype(F32)).astype(BF16)
    return _nn(mb, hi) + _nn(mb, lo)


@functools.partial(jax.custom_vjp, nondiff_argnums=(1,))
def _cumulative(x, rev):
    return _mask_matmul(_visible(x.shape[0], rev), x)


def _cumulative_fwd(x, rev):
    return _cumulative(x, rev), None


def _cumulative_bwd(rev, _, ct):
    return (_mask_matmul(_visible(ct.shape[0], not rev), ct),)


_cumulative.defvjp(_cumulative_fwd, _cumulative_bwd)


def _hg_chunk(states, aq, af, ai, lb0, lb1, rev):
    n_heads = len(states)
    lb = _sigmoid(lb0 - lb1)
    f = lb + (1.0 - lb) * _sigmoid(af)
    g = jnp.log(f)
    k = 1.0 - f
    q = _silu(aq)
    chunk = aq.shape[0]
    vis = _visible(chunk, rev)
    b = _cumulative(g, rev)
    last = 0 if rev else chunk - 1
    b_end = b[last:last + 1]
    b_mid = b[chunk // 2:chunk // 2 + 1]
    q_inter = q * jnp.exp(b)
    q_intra = q * jnp.exp(b - b_mid)
    k_intra = k * jnp.exp(b_mid - b)
    k_dec = k * jnp.exp(b_end - b)
    e_end = jnp.exp(b_end)
    new_states, outs = [], []
    for h in range(n_heads):
        sl = slice(h * HG_DK, (h + 1) * HG_DK)
        s_t = states[h]
        scores = jnp.where(vis, _nt(q_intra[:, sl], k_intra[:, sl]), 0.0)
        outs.append(_nt(q_inter[:, sl], s_t) + _nn(scores, ai[:, sl]))
        new_states.append(e_end[:, sl] * s_t + _tn(ai[:, sl], k_dec[:, sl]))
    return new_states, jnp.concatenate(outs, axis=1)


def _ml_chunk(state, q, k, v, g, gb, rev, d):
    cms, nvs, mbs = state
    n_heads = len(cms)
    dh = q.shape[1] // n_heads
    ga = g + gb
    log_f_all = jax.nn.log_sigmoid(ga)
    chunk = q.shape[0]
    vis = _visible(chunk, rev)
    b_all = _cumulative(log_f_all, rev)
    last = 0 if rev else chunk - 1
    k = k * (dh ** -0.5)
    new_c, new_n, new_m, outs = [], [], [], []
    for h in range(n_heads):
        ci = d * n_heads + h
        cf = (2 + d) * n_heads + h
        sl = slice(h * dh, (h + 1) * dh)
        qh, kh, vh = q[:, sl], k[:, sl], v[:, sl]
        li = ga[:, ci:ci + 1]
        b = b_all[:, cf:cf + 1]
        m = mbs[h][:, 0:1]
        row = jnp.transpose(li - b)
        log_w = jnp.where(vis, b + row, -jnp.inf)
        m_inter = b + m
        m_t = jnp.maximum(m_inter, jnp.max(log_w, axis=-1, keepdims=True))
        w_inter = jnp.exp(m_inter - m_t)
        w_qk = jnp.exp(log_w - m_t) * _bnt(qh, kh)
        num = w_inter * _bnt(qh, cms[h]) + _bnn(w_qk, vh)
        den = w_inter * jnp.sum(qh * nvs[h], axis=-1, keepdims=True) + jnp.sum(w_qk, axis=-1, keepdims=True)
        outs.append(num / jnp.maximum(jnp.abs(den), jnp.exp(-m_t)))
        m_new = m_t[last:last + 1]
        b_end = b[last:last + 1]
        w_s = jnp.exp(b_end - b + li - m_new)
        decay = jnp.exp(b_end + m - m_new)
        new_c.append(decay * cms[h] + _btn(w_s * vh, kh))
        new_n.append(decay * nvs[h] + jnp.sum(w_s * kh, axis=0, keepdims=True))
        new_m.append(jnp.broadcast_to(m_new, (1, LANE)))
    return (new_c, new_n, new_m), jnp.concatenate(outs, axis=1)


def _post_fn(o_f, o_b, az, h_f, h_b, bo, bz, wa, wb, n_hg, n_ml):
    o = o_f + o_b
    parts = []
    for h in range(n_hg):
        s = o[:, h * HG_DK:(h + 1) * HG_DK]
        parts.append(s * lax.rsqrt(jnp.mean(s * s, axis=-1, keepdims=True) + NORM_EPS))
    y_a = jnp.concatenate(parts, axis=1) * wa * _silu(az)
    hh = h_f + h_b
    dh = hh.shape[1] // n_ml
    parts = []
    for h in range(n_ml):
        s = hh[:, h * dh:(h + 1) * dh]
        mu = jnp.mean(s, axis=-1, keepdims=True)
        sc = s - mu
        parts.append(sc * lax.rsqrt(jnp.mean(sc * sc, axis=-1, keepdims=True) + NORM_EPS))
    y_b = jnp.concatenate(parts, axis=1) * wb * _sigmoid(bo) * _silu(bz)
    return jnp.concatenate([y_a, y_b], axis=1)


def _chip_of(dev):
    return 2 * dev[0] + dev[1]


def _index_of(dev):
    return 4 * dev[0] + 2 * dev[1] + dev[2]


class _Exchange:
    def __init__(self, srcs, out_shapes, transfers, local_copies=(), in_place=None):
        self.srcs, self.out_shapes = list(srcs), list(out_shapes)
        self.transfers, self.local_copies = list(transfers), list(local_copies)
        self.in_place = dict(in_place or {})

    def scratch(self):
        return [pltpu.SemaphoreType.DMA((len(self.transfers),)), pltpu.SemaphoreType.DMA((len(self.transfers),)),
                pltpu.SemaphoreType.DMA((max(len(self.local_copies), 1),))]

    def copies(self, ins, outs, send_sems, recv_sems, local_sems):
        me = (lax.axis_index("x"), lax.axis_index("y"), lax.axis_index("c"))

        def pick(ref, fn, *who):
            return ref if fn is None else ref.at[fn(*who)]

        sends, recvs, locs = [], [], []
        for t, (mask, si, sfn, di, dfn) in enumerate(self.transfers):
            peer = tuple(1 - p if flip else p for p, flip in zip(me, mask))
            sends.append(pltpu.make_async_remote_copy(
                src_ref=pick(ins[si], sfn, me, peer), dst_ref=pick(outs[di], dfn, me, peer),
                send_sem=send_sems.at[t], recv_sem=recv_sems.at[t], device_id=peer, device_id_type=MESH))
            landing = pick(outs[di], dfn, peer, me)
            recvs.append(pltpu.make_async_remote_copy(
                src_ref=landing, dst_ref=landing,
                send_sem=send_sems.at[t], recv_sem=recv_sems.at[t], device_id=peer, device_id_type=MESH))
        for l, (si, sfn, di, dfn) in enumerate(self.local_copies):
            locs.append(pltpu.make_async_copy(pick(ins[si], sfn, me), pick(outs[di], dfn, me), local_sems.at[l]))

        def start():
            for cp in locs + sends:
                cp.start()

        def wait():
            for cp in recvs:
                cp.wait_recv()
            for cp in sends:
                cp.wait_send()
            for cp in locs:
                cp.wait()

        return start, wait

    def run(self, name):
        n_in, n_out = len(self.srcs), len(self.out_shapes)

        def body(*refs):
            start, wait = self.copies(refs[:n_in], refs[n_in:n_in + n_out], *refs[n_in + n_out:])
            start()
            wait()

        hbm = pl.BlockSpec(memory_space=pltpu.HBM)
        return pl.pallas_call(
            body, name=name, out_shape=tuple(self.out_shapes), in_specs=[hbm] * n_in,
            out_specs=tuple([hbm] * n_out), scratch_shapes=self.scratch(), input_output_aliases=self.in_place,
        )(*self.srcs)


class _SemaphoreList:
    def __init__(self, refs):
        self.at = list(refs)


def _start_exchange(name, exchange):
    n_t, n_out = len(exchange.transfers), len(exchange.out_shapes)
    n_buf = len(exchange.srcs)
    n_in = n_buf - n_out

    def body(*refs):
        bufs = refs[:n_buf]
        sends = _SemaphoreList(refs[n_buf:n_buf + n_t])
        recvs = _SemaphoreList(refs[n_buf + n_t:n_buf + 2 * n_t])
        token = refs[-1]
        start, _ = exchange.copies(bufs[:n_in], bufs[n_in:], sends, recvs, None)
        start()
        token[...] = jnp.zeros_like(token)

    hbm = pl.BlockSpec(memory_space=pltpu.HBM)
    sem = pl.BlockSpec(memory_space=pltpu.SEMAPHORE)
    res = pl.pallas_call(
        body, name=name,
        out_shape=tuple([pltpu.SemaphoreType.DMA(())] * (2 * n_t)
                        + [pltpu.HBM(a.shape, a.dtype) for a in exchange.srcs]
                        + [jax.ShapeDtypeStruct((8, LANE), F32)]),
        in_specs=[hbm] * n_buf,
        out_specs=tuple([sem] * (2 * n_t) + [hbm] * n_buf + [pl.BlockSpec(memory_space=pltpu.VMEM)]),
        input_output_aliases={i: 2 * n_t + i for i in range(n_buf)},
        compiler_params=pltpu.CompilerParams(has_side_effects=pltpu.SideEffectType.DATAFLOW_SIDE_EFFECTING),
    )(*[pltpu.with_memory_space_constraint(a, pltpu.HBM) for a in exchange.srcs])
    return res[:-1], res[-1]


def _wait_exchange(name, exchange, handles, after):
    n_t, n_out = len(exchange.transfers), len(exchange.out_shapes)
    n_buf = len(exchange.srcs)
    n_in = n_buf - n_out
    sems, bufs = handles[:2 * n_t], handles[2 * n_t:]

    def body(*refs):
        ins, lands = refs[:n_in], refs[n_in:n_buf]
        sends = _SemaphoreList(refs[n_buf:n_buf + n_t])
        recvs = _SemaphoreList(refs[n_buf + n_t:n_buf + 2 * n_t])
        _, wait = exchange.copies(ins, lands, sends, recvs, None)
        wait()

    hbm = pl.BlockSpec(memory_space=pltpu.HBM)
    sem = pl.BlockSpec(memory_space=pltpu.SEMAPHORE)
    res = pl.pallas_call(
        body, name=name, out_shape=tuple(pltpu.HBM(a.shape, a.dtype) for a in bufs),
        in_specs=[hbm] * n_buf + [sem] * (2 * n_t) + [pl.BlockSpec(memory_space=pl.ANY)],
        out_specs=tuple([hbm] * n_buf), input_output_aliases={i: i for i in range(n_buf)},
        compiler_params=pltpu.CompilerParams(has_side_effects=pltpu.SideEffectType.DATAFLOW_SIDE_EFFECTING),
    )(*bufs, *sems, after)
    return list(res[n_in:])


def _call(body, operands, *, name, grid, in_specs, out_specs, out_shape, scratch_shapes=(), sem=None, rider=None):
    out_specs, out_shape, scratch_shapes = list(out_specs), list(out_shape), list(scratch_shapes)
    if rider is None:
        res = pl.pallas_call(
            body, name=name, grid=grid, in_specs=list(in_specs), out_specs=tuple(out_specs),
            out_shape=tuple(out_shape), scratch_shapes=scratch_shapes, compiler_params=_params(sem),
        )(*operands)
        return list(res), []
    counts = (len(in_specs), len(rider.srcs), len(out_specs), len(rider.out_shapes), len(scratch_shapes), 3)

    def full(*refs):
        groups, pos = [], 0
        for k in counts:
            groups.append(refs[pos:pos + k])
            pos += k
        own_in, ex_in, own_out, ex_out, own_scr, ex_scr = groups
        ids = [pl.program_id(a) for a in range(len(grid))]
        first = functools.reduce(jnp.logical_and, [i == 0 for i in ids])
        last = functools.reduce(jnp.logical_and, [i == g - 1 for i, g in zip(ids, grid)])
        start, wait = rider.copies(ex_in, ex_out, *ex_scr)
        pl.when(first)(start)
        body(*own_in, *own_out, *own_scr)
        pl.when(last)(wait)

    hbm = pl.BlockSpec(memory_space=pltpu.HBM)
    res = pl.pallas_call(
        full, name=name, grid=grid, in_specs=list(in_specs) + [hbm] * counts[1],
        out_specs=tuple(out_specs + [hbm] * counts[3]), out_shape=tuple(out_shape + rider.out_shapes),
        scratch_shapes=scratch_shapes + rider.scratch(), compiler_params=_params(("arbitrary",) * len(grid)),
        input_output_aliases={counts[0] + i: counts[2] + o for i, o in rider.in_place.items()},
    )(*operands, *rider.srcs)
    return list(res[:counts[2]]), list(res[counts[2]:])


ALL_MASKS = [(mx, my, mc) for mx in (0, 1) for my in (0, 1) for mc in (0, 1)][1:]
CHIP_MASKS = [(1, 0, 0), (0, 1, 0), (1, 1, 0)]
SIBLING_MASK = (0, 0, 1)


def _all_gather8(v):
    out = jax.ShapeDtypeStruct((N_DEV,) + v.shape, v.dtype)
    slot = lambda sender, receiver: _index_of(sender)
    transfers = [(mask, 0, None, 0, slot) for mask in ALL_MASKS]
    return _Exchange([v], [out], transfers, [(0, None, 0, lambda me: _index_of(me))])


def _all_gather_chips(arrays):
    outs = [jax.ShapeDtypeStruct((N_CHIPS,) + a.shape, a.dtype) for a in arrays]
    slot = lambda sender, receiver: _chip_of(sender)
    return _Exchange(arrays, outs, [(mask, i, None, i, slot) for i in range(len(arrays)) for mask in CHIP_MASKS])


def _sibling_swap(arrays):
    outs = [jax.ShapeDtypeStruct(a.shape, a.dtype) for a in arrays]
    return _Exchange(arrays, outs, [(SIBLING_MASK, i, None, i, None) for i in range(len(arrays))])


def _chip_scatter(arrays):
    outs = [jax.ShapeDtypeStruct(a.shape, a.dtype) for a in arrays]
    transfers = [(mask, i, lambda s, r: _chip_of(r), i, lambda s, r: _chip_of(s))
                 for i in range(len(arrays)) for mask in CHIP_MASKS]
    return _Exchange(arrays, outs, transfers)


def _own_block(chip, own, blocks):
    sel = (lax.broadcasted_iota(jnp.int32, (N_CHIPS,) + (1,) * (blocks.ndim - 1), 0) == chip)
    return jnp.where(sel, own if own.ndim == blocks.ndim else own[None], blocks)


def _join_halves(ci, mine, other, axis):
    return jnp.where(ci == 0, jnp.concatenate([mine, other], axis=axis), jnp.concatenate([other, mine], axis=axis))


def _mm_nt(name, a, b, tm, tn, out_dtype, rider=None):
    m, k = a.shape
    n = b.shape[0]

    def body(a_ref, b_ref, o_ref):
        o_ref[...] = _nt(a_ref[...], b_ref[...]).astype(out_dtype)

    (out,), rode = _call(
        body, (a, b), name=name, grid=(n // tn, m // tm),
        in_specs=[pl.BlockSpec((tm, k), lambda j, i: (i, 0)), pl.BlockSpec((tn, k), lambda j, i: (j, 0))],
        out_specs=[pl.BlockSpec((tm, tn), lambda j, i: (i, j))],
        out_shape=[jax.ShapeDtypeStruct((m, n), out_dtype)], sem=("parallel", "parallel"), rider=rider)
    return out, rode


def _mm_acc(name, a, b, tm, tk, rider=None):
    m, kc = a.shape
    n = b.shape[1]

    def body(a_ref, b_ref, o_ref):
        @pl.when(pl.program_id(1) == 0)
        def _():
            o_ref[...] = jnp.zeros_like(o_ref)
        o_ref[...] += _nn(a_ref[...], b_ref[...])

    (out,), rode = _call(
        body, (a, b), name=name, grid=(m // tm, kc // tk),
        in_specs=[pl.BlockSpec((tm, tk), lambda i, kk: (i, kk)), pl.BlockSpec((tk, n), lambda i, kk: (kk, 0))],
        out_specs=[pl.BlockSpec((tm, n), lambda i, kk: (i, 0))],
        out_shape=[jax.ShapeDtypeStruct((m, n), F32)], sem=("parallel", "arbitrary"), rider=rider)
    return out, rode


def _mm_tn(name, a, b, tm, tk, with_bf16=False):
    kr, m = a.shape
    n = b.shape[1]
    steps_k = kr // tk

    def body(a_ref, b_ref, o_ref, *narrow):
        @pl.when(pl.program_id(1) == 0)
        def _():
            o_ref[...] = jnp.zeros_like(o_ref)
        o_ref[...] += _tn(a_ref[...], b_ref[...])
        if with_bf16:
            @pl.when(pl.program_id(1) == steps_k - 1)
            def _():
                narrow[0][...] = o_ref[...].astype(BF16)

    out_spec = pl.BlockSpec((tm, n), lambda i, kk: (i, 0))
    res = pl.pallas_call(
        body, name=name, grid=(m // tm, steps_k),
        in_specs=[pl.BlockSpec((tk, tm), lambda i, kk: (kk, i)), pl.BlockSpec((tk, n), lambda i, kk: (kk, 0))],
        out_specs=(out_spec,) * (2 if with_bf16 else 1),
        out_shape=(jax.ShapeDtypeStruct((m, n), F32),) + ((jax.ShapeDtypeStruct((m, n), BF16),) if with_bf16 else ()),
        compiler_params=_params(("parallel", "arbitrary")),
    )(a, b)
    return res if with_bf16 else res[0]


def _modulate_fwd(x, ctx, prm, tm, rider=None):
    t_rows, dm = x.shape
    lat = t_rows // tm
    r = t_rows + ctx.shape[0]

    def body(x_ref, c_ref, p_ref, h_ref):
        xv = jnp.where(pl.program_id(0) >= lat, c_ref[...], x_ref[...])
        mu = jnp.mean(xv, axis=-1, keepdims=True)
        xm = xv - mu
        n = xm * lax.rsqrt(jnp.mean(xm * xm, axis=-1, keepdims=True) + LN_EPS)
        h_ref[...] = (n * (1.0 + p_ref[0, 1:2, :]) + p_ref[0, 0:1, :]).astype(BF16)

    (h,), rode = _call(
        body, (x, ctx, prm), name="modulate_fwd", grid=(r // tm,),
        in_specs=[pl.BlockSpec((tm, dm), lambda i: (jnp.minimum(i, lat - 1), 0)),
                  pl.BlockSpec((tm, dm), lambda i: (jnp.maximum(i - lat, 0), 0)),
                  pl.BlockSpec((1, 8, dm), lambda i: ((i >= lat).astype(jnp.int32), 0, 0))],
        out_specs=[pl.BlockSpec((tm, dm), lambda i: (i, 0))],
        out_shape=[jax.ShapeDtypeStruct((r, dm), BF16)], sem=("parallel",), rider=rider)
    return h, rode


def _modulate_bwd(x, ctx, dh, prm, gx_direct, tm, rider=None):
    t_rows, dm = x.shape
    lat, n_ct = t_rows // tm, ctx.shape[0] // tm
    is_ctx = lambda i: i < n_ct
    cls = lambda i: is_ctx(i).astype(jnp.int32)
    lat_tile = lambda i: (jnp.maximum(i - n_ct, 0), 0)

    def body(x_ref, c_ref, dh_ref, p_ref, gd_ref, gx_ref, acc_ref):
        i = pl.program_id(0)

        @pl.when((i == 0) | (i == n_ct))
        def _():
            acc_ref[...] = jnp.zeros_like(acc_ref)

        x = jnp.where(is_ctx(i), c_ref[...], x_ref[...])
        dh_v = dh_ref[...]
        mu = jnp.mean(x, axis=-1, keepdims=True)
        xm = x - mu
        rstd = lax.rsqrt(jnp.mean(xm * xm, axis=-1, keepdims=True) + LN_EPS)
        n = xm * rstd
        acc_ref[0, 0:1, :] += jnp.sum(dh_v, axis=0, keepdims=True)
        acc_ref[0, 1:2, :] += jnp.sum(dh_v * n, axis=0, keepdims=True)
        dn = dh_v * (1.0 + p_ref[0, 1:2, :])
        dx = rstd * (dn - jnp.mean(dn, axis=-1, keepdims=True) - n * jnp.mean(dn * n, axis=-1, keepdims=True))
        gx_ref[...] = dx + gd_ref[...]

    return _call(
        body, (x, ctx, dh, prm, gx_direct), name="modulate_bwd", grid=(n_ct + lat,),
        in_specs=[pl.BlockSpec((tm, dm), lat_tile),
                  pl.BlockSpec((tm, dm), lambda i: (jnp.minimum(i, n_ct - 1), 0)),
                  pl.BlockSpec((tm, dm), lambda i: (jnp.where(is_ctx(i), lat + i, i - n_ct), 0)),
                  pl.BlockSpec((1, 8, dm), lambda i: (cls(i), 0, 0)),
                  pl.BlockSpec((tm, dm), lat_tile)],
        out_specs=(pl.BlockSpec((tm, dm), lat_tile), pl.BlockSpec((1, 8, dm), lambda i: (cls(i), 0, 0))),
        out_shape=(jax.ShapeDtypeStruct((t_rows, dm), F32), jax.ShapeDtypeStruct((2, 8, dm), F32)),
        sem=("arbitrary",), rider=rider)


def _conv_parts(t_rows, c_rows):
    return ((0, t_rows, t_rows // GRID_W, GRID_W), (t_rows, c_rows, 1, c_rows))


def _col_shifts(x2, rows_g, width_g):
    n, ct = x2.shape
    col = lax.broadcasted_iota(jnp.int32, (width_g, ct), 0)
    as_grid = lambda a: a.reshape(rows_g, width_g, ct)
    left = as_grid(pltpu.roll(x2, 1, 0)) * (col >= 1).astype(F32)
    right = as_grid(pltpu.roll(x2, n - 1, 0)) * (col <= width_g - 2).astype(F32)
    return [left, as_grid(x2), right]


CONV_BLOCK_ROWS = 4


def _conv_blocks(t_rows, c_rows):
    for t0, _, rows_g, width_g in _conv_parts(t_rows, c_rows):
        nb = min(CONV_BLOCK_ROWS, rows_g)
        assert rows_g % nb == 0
        for g0 in range(0, rows_g, nb):
            yield t0, rows_g, width_g, g0, nb


def _slab(ref, t0, rows_g, width_g, g0, nb):
    if rows_g == 1:
        return ref[t0:t0 + width_g, :]
    lo, hi = max(g0 - 1, 0), min(g0 + nb + 1, rows_g)
    parts = [ref[t0 + lo * width_g:t0 + hi * width_g, :]]
    zero = jnp.zeros((width_g, ref.shape[1]), F32)
    if g0 == 0:
        parts.insert(0, zero)
    if g0 + nb == rows_g:
        parts.append(zero)
    return jnp.concatenate(parts, axis=0)


def _conv_taps(cols, w_ref, nb, flip):
    one_row = cols[0].shape[0] == nb
    acc = None
    for a in range(3):
        if one_row and a != 1:
            continue
        for b in range(3):
            tap = (2 - a) * 3 + (2 - b) if flip else a * 3 + b
            term = (cols[b] if one_row else cols[b][a:a + nb]) * w_ref[tap:tap + 1, :]
            acc = term if acc is None else acc + term
    return acc


def _conv_fwd(u, conv_w9, conv_b, t_rows, c_rows, w, ct):
    r = u.shape[0]
    base = 5 * w // ct

    def body(x_ref, w_ref, b_ref, o_ref):
        for t0, rows_g, width_g, g0, nb in _conv_blocks(t_rows, c_rows):
            slab = _slab(x_ref, t0, rows_g, width_g, g0, nb)
            cols = _col_shifts(slab, slab.shape[0] // width_g, width_g)
            pre = _conv_taps(cols, w_ref, nb, False) + b_ref[...]
            o_ref[t0 + g0 * width_g:t0 + (g0 + nb) * width_g, :] = _silu(pre).reshape(nb * width_g, ct)

    return pl.pallas_call(
        body, name="conv_fwd", grid=(2 * w // ct,),
        in_specs=[pl.BlockSpec((r, ct), lambda i: (0, base + i)), pl.BlockSpec((9, ct), lambda i: (0, i)),
                  pl.BlockSpec((1, ct), lambda i: (0, i))],
        out_specs=pl.BlockSpec((r, ct), lambda i: (0, i)),
        out_shape=jax.ShapeDtypeStruct((r, 2 * w), F32),
        compiler_params=_params(("parallel",)),
    )(u, conv_w9, conv_b)


def _conv_bwd(u, dqk_pair, conv_w9, conv_b, t_rows, c_rows, w, ct):
    r = u.shape[0]
    base = 5 * w // ct

    def body(x_ref, d1_ref, d2_ref, w_ref, b_ref, dx_ref, dw_ref, db_ref, dpre_ref):
        dw = [jnp.zeros((1, ct), F32) for _ in range(9)]
        db = jnp.zeros((1, ct), F32)
        for t0, rows_g, width_g, g0, nb in _conv_blocks(t_rows, c_rows):
            rows = slice(t0 + g0 * width_g, t0 + (g0 + nb) * width_g)
            slab = _slab(x_ref, t0, rows_g, width_g, g0, nb)
            cols = _col_shifts(slab, slab.shape[0] // width_g, width_g)
            pre = _conv_taps(cols, w_ref, nb, False) + b_ref[...]
            sg = _sigmoid(pre)
            dpre = (d1_ref[rows, :] + d2_ref[rows, :]).reshape(pre.shape) * (sg * (1.0 + pre * (1.0 - sg)))
            dpre_ref[rows, :] = dpre.reshape(nb * width_g, ct)
            db = db + jnp.sum(jnp.sum(dpre, axis=0), axis=0, keepdims=True)
            for a in range(3):
                if rows_g == 1 and a != 1:
                    continue
                for b in range(3):
                    moved = cols[b] if rows_g == 1 else cols[b][a:a + nb]
                    dw[a * 3 + b] = dw[a * 3 + b] + jnp.sum(jnp.sum(moved * dpre, axis=0), axis=0, keepdims=True)
        for t0, rows_g, width_g, g0, nb in _conv_blocks(t_rows, c_rows):
            slab = _slab(dpre_ref, t0, rows_g, width_g, g0, nb)
            cols = _col_shifts(slab, slab.shape[0] // width_g, width_g)
            dx_ref[t0 + g0 * width_g:t0 + (g0 + nb) * width_g, :] = _conv_taps(cols, w_ref, nb, True).reshape(
                nb * width_g, ct).astype(BF16)
        for tap in range(9):
            dw_ref[tap:tap + 1, :] = dw[tap]
        db_ref[...] = db

    return pl.pallas_call(
        body, name="conv_bwd", grid=(2 * w // ct,),
        in_specs=[pl.BlockSpec((r, ct), lambda i: (0, base + i)), pl.BlockSpec((r, ct), lambda i: (0, i)),
                  pl.BlockSpec((r, ct), lambda i: (0, i)),
                  pl.BlockSpec((9, ct), lambda i: (0, i)), pl.BlockSpec((1, ct), lambda i: (0, i))],
        out_specs=(pl.BlockSpec((r, ct), lambda i: (0, i)), pl.BlockSpec((9, ct), lambda i: (0, i)),
                   pl.BlockSpec((1, ct), lambda i: (0, i))),
        out_shape=(jax.ShapeDtypeStruct((r, 2 * w), BF16), jax.ShapeDtypeStruct((9, 2 * w), F32),
                   jax.ShapeDtypeStruct((1, 2 * w), F32)),
        scratch_shapes=[pltpu.VMEM((r, ct), F32)],
        compiler_params=_params(("parallel",)),
    )(u, dqk_pair[0], dqk_pair[1], conv_w9, conv_b)


def _assemble_du(groups, gates, n_pad, tm):
    flat, layout = [], []
    for entry in list(groups) + [gates]:
        parts = entry if isinstance(entry, (tuple, list)) else (entry,)
        layout.append((len(flat), len(parts), parts[0].shape[1]))
        flat += list(parts)
    r = flat[0].shape[0]

    def body(*refs):
        o_ref = refs[-1]
        col = 0
        for first, count, width in layout:
            val = refs[first][...]
            for extra in range(1, count):
                val = val.astype(F32) + refs[first + extra][...].astype(F32)
            o_ref[:, col:col + width] = val.astype(BF16)
            col += width
        assert col == n_pad

    return pl.pallas_call(
        body, name="assemble_du", grid=(r // tm,),
        in_specs=[pl.BlockSpec((tm, a.shape[1]), lambda i: (i, 0)) for a in flat],
        out_specs=pl.BlockSpec((tm, n_pad), lambda i: (i, 0)),
        out_shape=jax.ShapeDtypeStruct((r, n_pad), BF16),
        compiler_params=_params(("parallel",)),
    )(*flat)


def _scan_order(n_lat, n_ctx, rev):
    n = n_lat + n_ctx
    if rev:
        return lambda j: n - 1 - j
    return lambda j: (j + n_lat) % n


DIRS = (False, True)


def _hg_scan_fwd(u, lb_full, w, n_lat, n_ctx, chunk, rider=None):
    r = u.shape[0]
    n_heads = w // HG_DK
    n_chunks = n_lat + n_ctx
    nat = [_scan_order(n_lat, n_ctx, rev) for rev in DIRS]

    def body(*refs):
        ins, outs, scratch = refs[:8], refs[8:12], refs[12:]

        @pl.when(pl.program_id(0) == 0)
        def _():
            for s_ref in scratch:
                s_ref[...] = jnp.zeros_like(s_ref)

        results = []
        for d, rev in enumerate(DIRS):
            aq, af, ai, lb_ref = ins[4 * d:4 * d + 4]
            state = [scratch[d][h] for h in range(n_heads)]
            results.append((state, _hg_chunk(state, aq[...], af[...], ai[...],
                                             lb_ref[0, 0:1, :], lb_ref[0, 1:2, :], rev)))
        for d, (state, (new, o)) in enumerate(results):
            o_ref, save_ref = outs[2 * d:2 * d + 2]
            o_ref[...] = o
            for h in range(n_heads):
                save_ref[0, h] = state[h]
                scratch[d][h] = new[h]

    in_specs, out_specs, out_shape = [], [], []
    for d in range(2):
        in_specs += [pl.BlockSpec((chunk,w), lambda j, d=d: (nat[d](j), 0)),
                     pl.BlockSpec((chunk,w), lambda j, d=d: (nat[d](j), 1 + d)),
                     pl.BlockSpec((chunk,w), lambda j, d=d: (nat[d](j), 3)),
                     pl.BlockSpec((1, 2, w), lambda j, d=d: (d, 0, 0))]
        out_specs += [pl.BlockSpec((chunk,w), lambda j, d=d: (nat[d](j), 0)),
                      pl.BlockSpec((1, n_heads, HG_DK, HG_DK), lambda j: (j, 0, 0, 0))]
        out_shape += [jax.ShapeDtypeStruct((r, w), F32),
                      jax.ShapeDtypeStruct((n_chunks, n_heads, HG_DK, HG_DK), F32)]
    (o_f, s_f, o_b, s_b), rode = _call(
        body, (u, u, u, lb_full, u, u, u, lb_full), name="hg_scan_fwd", grid=(n_chunks,), in_specs=in_specs,
        out_specs=out_specs, out_shape=out_shape, scratch_shapes=[pltpu.VMEM((n_heads, HG_DK, HG_DK), F32)] * 2,
        sem=("arbitrary",), rider=rider)
    return (o_f, o_b), (s_f, s_b), rode


def _hg_scan_bwd(u, lb_full, saved, d_o, w, n_lat, n_ctx, chunk, rider=None):
    r = u.shape[0]
    n_heads = w // HG_DK
    n_chunks = n_lat + n_ctx
    step = lambda jj: n_chunks - 1 - jj
    nat = [(lambda jj, o=_scan_order(n_lat, n_ctx, rev): o(step(jj))) for rev in DIRS]

    def body(*refs):
        ins, outs, scratch = refs[:12], refs[12:20], refs[20:]
        jj = pl.program_id(0)

        @pl.when(jj == 0)
        def _():
            for d in range(2):
                scratch[d][...] = jnp.zeros_like(scratch[d])
                outs[4 * d + 3][...] = jnp.zeros_like(outs[4 * d + 3])

        results = []
        for d, rev in enumerate(DIRS):
            aq, af, ai, lb_ref, save_ref, do_ref = ins[6 * d:6 * d + 6]
            f = lambda st, a, b, c, l0, l1, rev=rev: _hg_chunk(st, a, b, c, l0, l1, rev)
            _, vjp = jax.vjp(f, [save_ref[0, h] for h in range(n_heads)], aq[...], af[...], ai[...],
                             lb_ref[0, 0:1, :], lb_ref[0, 1:2, :])
            d_out = do_ref[...] * (nat[d](jj) < n_lat).astype(F32)
            results.append(vjp(([scratch[d][h] for h in range(n_heads)], d_out)))
        for d, (dst, daq, daf, dai, dl0, dl1) in enumerate(results):
            daq_ref, daf_ref, dai_ref, dlb_ref = outs[4 * d:4 * d + 4]
            for h in range(n_heads):
                scratch[d][h] = dst[h]
            daq_ref[...] = daq.astype(BF16)
            daf_ref[...] = daf.astype(BF16)
            dai_ref[...] = dai.astype(BF16)
            dlb_ref[0:1, :] += dl0
            dlb_ref[1:2, :] += dl1

    in_specs, out_specs, out_shape, operands = [], [], [], []
    for d in range(2):
        row = lambda jj, d=d: (nat[d](jj), 0)
        in_specs += [pl.BlockSpec((chunk,w), row),
                     pl.BlockSpec((chunk,w), lambda jj, d=d: (nat[d](jj), 1 + d)),
                     pl.BlockSpec((chunk,w), lambda jj, d=d: (nat[d](jj), 3)),
                     pl.BlockSpec((1, 2, w), lambda jj, d=d: (d, 0, 0)),
                     pl.BlockSpec((1, n_heads, HG_DK, HG_DK), lambda jj: (step(jj), 0, 0, 0)),
                     pl.BlockSpec((chunk,w), lambda jj, d=d: (jnp.minimum(nat[d](jj), n_lat - 1), 0))]
        operands += [u, u, u, lb_full, saved[d], d_o]
        out_specs += [pl.BlockSpec((chunk,w), row)] * 3 + [pl.BlockSpec((2, w), lambda jj: (0, 0))]
        out_shape += [jax.ShapeDtypeStruct((r, w), BF16)] * 3 + [jax.ShapeDtypeStruct((2, w), F32)]
    res, rode = _call(
        body, operands, name="hg_scan_bwd", grid=(n_chunks,), in_specs=in_specs, out_specs=out_specs,
        out_shape=out_shape, scratch_shapes=[pltpu.VMEM((n_heads, HG_DK, HG_DK), F32)] * 2,
        sem=("arbitrary",), rider=rider)
    return res[0:4], res[4:8], rode


def _ml_state_shapes(n_chunks, n_heads, dh):
    return (jax.ShapeDtypeStruct((n_chunks, n_heads, dh, dh), F32),
            jax.ShapeDtypeStruct((n_chunks, n_heads, 1, dh), F32),
            jax.ShapeDtypeStruct((n_chunks, n_heads, 1, LANE), F32))


def _ml_state_specs(n_heads, dh, index):
    return (pl.BlockSpec((1, n_heads, dh, dh), lambda j: (index(j), 0, 0, 0)),
            pl.BlockSpec((1, n_heads, 1, dh), lambda j: (index(j), 0, 0, 0)),
            pl.BlockSpec((1, n_heads, 1, LANE), lambda j: (index(j), 0, 0, 0)))


def _ml_state_scratch(n_heads, dh):
    return [pltpu.VMEM((n_heads, dh, dh), F32), pltpu.VMEM((n_heads, 1, dh), F32), pltpu.VMEM((n_heads, 1, LANE), F32)]


def _ml_scan_fwd(qk, u, gate_b, w, n_heads, n_lat, n_ctx, chunk):
    r = u.shape[0]
    dh = w // n_heads
    n_chunks = n_lat + n_ctx
    nat = [_scan_order(n_lat, n_ctx, rev) for rev in DIRS]

    def body(*refs):
        ins, outs, scratch = refs[:10], refs[10:18], refs[18:]

        @pl.when(pl.program_id(0) == 0)
        def _():
            for s_ref in scratch:
                s_ref[...] = jnp.zeros_like(s_ref)

        results = []
        for d, rev in enumerate(DIRS):
            q, k, v, g, gb = ins[5 * d:5 * d + 5]
            state = tuple([ref[h] for h in range(n_heads)] for ref in scratch[3 * d:3 * d + 3])
            results.append((state, _ml_chunk(state, q[...], k[...], v[...], g[...], gb[...], rev, d)))
        for d, (state, (new, o)) in enumerate(results):
            outs[4 * d][...] = o
            for part in range(3):
                for h in range(n_heads):
                    outs[4 * d + 1 + part][0, h] = state[part][h]
                    scratch[3 * d + part][h] = new[part][h]

    in_specs, out_specs, out_shape = [], [], []
    for d in range(2):
        in_specs += [pl.BlockSpec((chunk,w), lambda j, d=d: (nat[d](j), 0)),
                     pl.BlockSpec((chunk,w), lambda j, d=d: (nat[d](j), 1)),
                     pl.BlockSpec((chunk,w), lambda j, d=d: (nat[d](j), 7)),
                     pl.BlockSpec((chunk,LANE), lambda j, d=d: (nat[d](j), 10 * w // LANE)),
                     pl.BlockSpec((1, LANE), lambda j: (0, 0))]
        out_specs += [pl.BlockSpec((chunk,w), lambda j, d=d: (nat[d](j), 0))]
        out_specs += list(_ml_state_specs(n_heads, dh, lambda j: j))
        out_shape += [jax.ShapeDtypeStruct((r, w), F32)] + list(_ml_state_shapes(n_chunks, n_heads, dh))
    res = pl.pallas_call(
        body, name="ml_scan_fwd", grid=(n_chunks,), in_specs=in_specs, out_specs=tuple(out_specs),
        out_shape=tuple(out_shape), scratch_shapes=_ml_state_scratch(n_heads, dh) * 2,
        compiler_params=_params(("arbitrary",)),
    )(qk, qk, u, u, gate_b, qk, qk, u, u, gate_b)
    return (res[0], res[4]), (res[1:4], res[5:8])


def _ml_scan_bwd(qk, u, gate_b, saved, d_h, w, n_heads, n_lat, n_ctx, chunk, rider=None):
    r = u.shape[0]
    dh = w // n_heads
    n_chunks = n_lat + n_ctx
    step = lambda jj: n_chunks - 1 - jj
    nat = [(lambda jj, o=_scan_order(n_lat, n_ctx, rev): o(step(jj))) for rev in DIRS]

    def body(*refs):
        ins, outs, scratch = refs[:18], refs[18:26], refs[26:]
        jj = pl.program_id(0)

        @pl.when(jj == 0)
        def _():
            for s_ref in scratch:
                s_ref[...] = jnp.zeros_like(s_ref)
            for d in range(2):
                outs[4 * d + 3][...] = jnp.zeros_like(outs[4 * d + 3])

        results = []
        for d, rev in enumerate(DIRS):
            q, k, v, g, gb, sc, sn, sm, dh_ref = ins[9 * d:9 * d + 9]
            state = tuple([ref[0, h] for h in range(n_heads)] for ref in (sc, sn, sm))
            f = lambda st, a, b, c, gg, bb, rev=rev, d=d: _ml_chunk(st, a, b, c, gg, bb, rev, d)
            _, vjp = jax.vjp(f, state, q[...], k[...], v[...], g[...], gb[...])
            d_state = tuple([ref[h] for h in range(n_heads)] for ref in scratch[3 * d:3 * d + 3])
            d_out = dh_ref[...] * (nat[d](jj) < n_lat).astype(F32)
            results.append(vjp((d_state, d_out)))
        for d, (d_state, dq, dk, dv, dg, dgb) in enumerate(results):
            dqk_ref, dv_ref, dg_ref, dgb_ref = outs[4 * d:4 * d + 4]
            for part in range(3):
                for h in range(n_heads):
                    scratch[3 * d + part][h] = d_state[part][h]
            dqk_ref[:, 0:w] = dq
            dqk_ref[:, w:2 * w] = dk
            dv_ref[...] = dv.astype(BF16)
            dg_ref[...] = dg
            dgb_ref[...] += dgb

    in_specs, out_specs, out_shape, operands = [], [], [], []
    for d in range(2):
        row = lambda jj, d=d: (nat[d](jj), 0)
        in_specs += [pl.BlockSpec((chunk,w), row), pl.BlockSpec((chunk,w), lambda jj, d=d: (nat[d](jj), 1)),
                     pl.BlockSpec((chunk,w), lambda jj, d=d: (nat[d](jj), 7)),
                     pl.BlockSpec((chunk,LANE), lambda jj, d=d: (nat[d](jj), 10 * w // LANE)),
                     pl.BlockSpec((1, LANE), lambda jj: (0, 0))]
        in_specs += list(_ml_state_specs(n_heads, dh, step))
        in_specs += [pl.BlockSpec((chunk,w), lambda jj, d=d: (jnp.minimum(nat[d](jj), n_lat - 1), 0))]
        operands += [qk, qk, u, u, gate_b, *saved[d], d_h]
        out_specs += [pl.BlockSpec((chunk,2 * w), row), pl.BlockSpec((chunk,w), row),
                      pl.BlockSpec((chunk,LANE), row), pl.BlockSpec((1, LANE), lambda jj: (0, 0))]
        out_shape += [jax.ShapeDtypeStruct((r, 2 * w), F32), jax.ShapeDtypeStruct((r, w), BF16),
                      jax.ShapeDtypeStruct((r, LANE), F32), jax.ShapeDtypeStruct((1, LANE), F32)]
    res, rode = _call(
        body, operands, name="ml_scan_bwd", grid=(n_chunks,), in_specs=in_specs, out_specs=out_specs,
        out_shape=out_shape, scratch_shapes=_ml_state_scratch(n_heads, dh) * 2, sem=("arbitrary",), rider=rider)
    return res[0:4], res[4:8], rode


def _post_specs(w, tm, lat_tiles, cols):
    return [pl.BlockSpec((tm, w), (lambda i, cb=cb: (jnp.minimum(i, lat_tiles - 1), cb))) for cb in cols]


def _post_fwd(o_f, o_b, h_f, h_b, u, wa, wb, t_rows, w, n_hg, n_ml, tm):
    lat_tiles = t_rows // tm

    def body(of, ob, hf, hb, az, bo, bz, wa_ref, wb_ref, y_ref):
        y_ref[...] = _post_fn(of[...], ob[...], az[...], hf[...], hb[...], bo[...], bz[...],
                              wa_ref[...], wb_ref[...], n_hg, n_ml).astype(BF16)

    rows = pl.BlockSpec((tm, w), lambda i: (i, 0))
    vec = pl.BlockSpec((1, w), lambda i: (0, 0))
    return pl.pallas_call(
        body, name="post_fwd", grid=(lat_tiles,),
        in_specs=[rows] * 4 + _post_specs(w, tm, lat_tiles, (4, 8, 9)) + [vec, vec],
        out_specs=pl.BlockSpec((tm, 2 * w), lambda i: (i, 0)),
        out_shape=jax.ShapeDtypeStruct((t_rows, 2 * w), BF16),
        compiler_params=_params(("parallel",)),
    )(o_f, o_b, h_f, h_b, u, u, u, wa, wb)


def _post_bwd(o_f, o_b, h_f, h_b, u, wa, wb, dy, t_rows, w, n_hg, n_ml, tm, rider=None):
    r = u.shape[0]
    lat_tiles = t_rows // tm
    lat = lambda i: (jnp.minimum(i, lat_tiles - 1), 0)

    def body(of, ob, hf, hb, az, bo, bz, wa_ref, wb_ref, dy_ref, do_ref, dh_ref, daz_ref, dbo_ref, dbz_ref,
             dwa_ref, dwb_ref):
        i = pl.program_id(0)

        @pl.when(i == 0)
        def _():
            dwa_ref[...] = jnp.zeros_like(dwa_ref)
            dwb_ref[...] = jnp.zeros_like(dwb_ref)

        @pl.when(i < lat_tiles)
        def _():
            f = functools.partial(_post_fn, n_hg=n_hg, n_ml=n_ml)
            _, vjp = jax.vjp(f, of[...], ob[...], az[...], hf[...], hb[...], bo[...], bz[...], wa_ref[...], wb_ref[...])
            d_of, _, d_az, d_hf, _, d_bo, d_bz, d_wa, d_wb = vjp(dy_ref[...])
            do_ref[...] = d_of
            dh_ref[...] = d_hf
            daz_ref[...] = d_az.astype(BF16)
            dbo_ref[...] = d_bo.astype(BF16)
            dbz_ref[...] = d_bz.astype(BF16)
            dwa_ref[...] += d_wa
            dwb_ref[...] += d_wb

        @pl.when(i >= lat_tiles)
        def _():
            daz_ref[...] = jnp.zeros_like(daz_ref)
            dbo_ref[...] = jnp.zeros_like(dbo_ref)
            dbz_ref[...] = jnp.zeros_like(dbz_ref)

    lat_rows = pl.BlockSpec((tm, w), lat)
    all_rows = pl.BlockSpec((tm, w), lambda i: (i, 0))
    vec = pl.BlockSpec((1, w), lambda i: (0, 0))
    sd_t = jax.ShapeDtypeStruct((t_rows, w), F32)
    sd_r = jax.ShapeDtypeStruct((r, w), BF16)
    sd_v = jax.ShapeDtypeStruct((1, w), F32)
    return _call(
        body, (o_f, o_b, h_f, h_b, u, u, u, wa, wb, dy), name="post_bwd", grid=(r // tm,),
        in_specs=[lat_rows] * 4 + _post_specs(w, tm, lat_tiles, (4, 8, 9)) + [vec, vec]
        + [pl.BlockSpec((tm, 2 * w), lat)],
        out_specs=(lat_rows, lat_rows, all_rows, all_rows, all_rows, vec, vec),
        out_shape=(sd_t, sd_t, sd_r, sd_r, sd_r, sd_v, sd_v), sem=("arbitrary",), rider=rider)


OUT_ROW_GATE, OUT_ROW_LN_G, OUT_ROW_LN_B, OUT_ROW_LOSS = 0, 1, 2, 3


def _out_block(y, w_out, x, target, prm, tm):
    t_rows, dm = x.shape
    di = y.shape[1]

    def body(y_ref, w_ref, x_ref, t_ref, p_ref, dz_ref, dy_ref, gx_ref, acc_ref):
        @pl.when(pl.program_id(0) == 0)
        def _():
            acc_ref[...] = jnp.zeros_like(acc_ref)

        gate, ln_g, ln_b = p_ref[0:1, :], p_ref[1:2, :], p_ref[2:3, :]
        z = _nn(y_ref[...], w_ref[...])
        res = ALPHA * x_ref[...] + gate * z
        mu = jnp.mean(res, axis=-1, keepdims=True)
        rc = res - mu
        rstd = lax.rsqrt(jnp.mean(rc * rc, axis=-1, keepdims=True) + LN_EPS)
        rn = rc * rstd
        err = rn * ln_g + ln_b - t_ref[...]
        d_out = err * (1.0 / dm)
        d_rn = d_out * ln_g
        d_res = rstd * (d_rn - jnp.mean(d_rn, axis=-1, keepdims=True)
                        - rn * jnp.mean(d_rn * rn, axis=-1, keepdims=True))
        acc_ref[OUT_ROW_GATE:OUT_ROW_GATE + 1, :] += jnp.sum(d_res * z, axis=0, keepdims=True)
        acc_ref[OUT_ROW_LN_G:OUT_ROW_LN_G + 1, :] += jnp.sum(d_out * rn, axis=0, keepdims=True)
        acc_ref[OUT_ROW_LN_B:OUT_ROW_LN_B + 1, :] += jnp.sum(d_out, axis=0, keepdims=True)
        acc_ref[OUT_ROW_LOSS:OUT_ROW_LOSS + 1, :] += (0.5 / dm) * jnp.sum(err * err, axis=0, keepdims=True)
        gx_ref[...] = ALPHA * d_res
        dz = (d_res * gate).astype(BF16)
        dz_ref[...] = dz
        dy_ref[...] = _nt(dz, w_ref[...])

    rows_d = pl.BlockSpec((tm, dm), lambda i: (i, 0))
    rows_i = pl.BlockSpec((tm, di), lambda i: (i, 0))
    return pl.pallas_call(
        body, name="out_block", grid=(t_rows // tm,),
        in_specs=[rows_i, pl.BlockSpec((di, dm), lambda i: (0, 0)), rows_d, rows_d,
                  pl.BlockSpec((8, dm), lambda i: (0, 0))],
        out_specs=(rows_d, rows_i, rows_d, pl.BlockSpec((8, dm), lambda i: (0, 0))),
        out_shape=(jax.ShapeDtypeStruct((t_rows, dm), BF16), jax.ShapeDtypeStruct((t_rows, di), F32),
                   jax.ShapeDtypeStruct((t_rows, dm), F32), jax.ShapeDtypeStruct((8, dm), F32)),
        compiler_params=_params(("arbitrary",)),
    )(y, w_out, x, target, prm)


def _mod_fwd(c16, w_mod, tn):
    dm, n = w_mod.shape

    def body(c_ref, w_ref, o_ref, a_ref):
        a = _silu(c_ref[...])
        a_ref[...] = a
        o_ref[...] = _nn(a, w_ref[...], HIGHEST)

    return pl.pallas_call(
        body, name="mod_fwd", grid=(n // tn,),
        in_specs=[pl.BlockSpec((16, dm), lambda j: (0, 0)), pl.BlockSpec((dm, tn), lambda j: (0, j))],
        out_specs=(pl.BlockSpec((16, tn), lambda j: (0, j)), pl.BlockSpec((16, dm), lambda j: (0, 0))),
        out_shape=(jax.ShapeDtypeStruct((16, n), F32), jax.ShapeDtypeStruct((16, dm), F32)),
        compiler_params=_params(("arbitrary",)),
    )(c16, w_mod)


def _mod_bwd(a16, dm16, w_mod, tn):
    dm, n = w_mod.shape

    def body(a_ref, d_ref, w_ref, dw_ref, dc_ref):
        @pl.when(pl.program_id(0) == 0)
        def _():
            dc_ref[...] = jnp.zeros_like(dc_ref)
        dw_ref[...] = _tn(a_ref[...], d_ref[...], HIGHEST)
        dc_ref[...] += _nt(d_ref[...], w_ref[...], HIGHEST)

    return pl.pallas_call(
        body, name="mod_bwd", grid=(n // tn,),
        in_specs=[pl.BlockSpec((16, dm), lambda j: (0, 0)), pl.BlockSpec((16, tn), lambda j: (0, j)),
                  pl.BlockSpec((dm, tn), lambda j: (0, j))],
        out_specs=(pl.BlockSpec((dm, tn), lambda j: (0, j)), pl.BlockSpec((16, dm), lambda j: (0, 0))),
        out_shape=(jax.ShapeDtypeStruct((dm, n), F32), jax.ShapeDtypeStruct((16, dm), F32)),
        compiler_params=_params(("arbitrary",)),
    )(a16, dm16, w_mod)


def _sum_devices(g, fold_rows):
    n_dev, rows, n = g.shape

    def body(g_ref, s_ref, t_ref):
        s = g_ref[0]
        for dev in range(1, n_dev):
            s = s + g_ref[dev]
        t_ref[...] = jnp.broadcast_to(jnp.sum(s, axis=-1, keepdims=True), (rows, LANE))
        s_ref[...] = s
        s_ref[0:fold_rows, :] = s[0:fold_rows] + s[fold_rows:2 * fold_rows]

    return pl.pallas_call(
        body, name="sum_devices",
        out_shape=(jax.ShapeDtypeStruct((rows, n), F32), jax.ShapeDtypeStruct((rows, LANE), F32)),
        compiler_params=_params(),
    )(g)


def _c_ctx_grad(parts, c_ctx_row):
    def body(p_ref, c_ref, o_ref):
        s = p_ref[0]
        for chip in range(1, N_CHIPS):
            s = s + p_ref[2 * chip]
        cv = c_ref[...]
        sg = _sigmoid(cv)
        o_ref[...] = s * (sg * (1.0 + cv * (1.0 - sg)))

    return pl.pallas_call(
        body, name="c_ctx_grad", out_shape=jax.ShapeDtypeStruct(parts.shape[1:], F32), compiler_params=_params(),
    )(parts, c_ctx_row)


def _sum_pair(name, mine, got):
    def body(a_ref, b_ref, o_ref):
        o_ref[...] = (a_ref[...] + b_ref[...]).astype(BF16)

    k, rows, n = mine.shape
    tl = _largest_divisor(n, max(LANE, (1 << 18) // rows), LANE)
    spec = pl.BlockSpec((1, rows, tl), lambda kk, i: (kk, 0, i))
    return pl.pallas_call(
        body, name=name, grid=(k, n // tl), in_specs=[spec, spec], out_specs=spec,
        out_shape=jax.ShapeDtypeStruct(mine.shape, BF16), compiler_params=_params(("parallel", "parallel")),
    )(mine, got)


def _sum_pair_lanes(name, full, got, ci):
    rows, n = got.shape
    tr = _largest_divisor(rows, max(SUBLANE_BF16, (1 << 19) // n), SUBLANE_BF16)

    def body(ci_ref, a_ref, b_ref, o_ref):
        o_ref[...] = (a_ref[...] + b_ref[...].astype(F32)).astype(BF16)

    return pl.pallas_call(
        body, name=name,
        grid_spec=pltpu.PrefetchScalarGridSpec(
            num_scalar_prefetch=1, grid=(rows // tr,),
            in_specs=[pl.BlockSpec((tr, n), lambda i, c: (i, c[0])), pl.BlockSpec((tr, n), lambda i, c: (i, 0))],
            out_specs=pl.BlockSpec((tr, n), lambda i, c: (i, 0))),
        out_shape=jax.ShapeDtypeStruct((rows, n), BF16), compiler_params=_params(("parallel",)),
    )(ci.reshape(1).astype(jnp.int32), full, got)


def _sum_chips(name, got, own, chip):
    k, rows, n = got.shape
    tl = _largest_divisor(n, max(LANE, (1 << 18) // rows), LANE)

    def body(chip_ref, g_ref, own_ref, o_ref):
        total = None
        for kk in range(k):
            term = jnp.where(chip_ref[0] == kk, own_ref[0], g_ref[kk]).astype(F32)
            total = term if total is None else total + term
        o_ref[...] = total

    return pl.pallas_call(
        body, name=name,
        grid_spec=pltpu.PrefetchScalarGridSpec(
            num_scalar_prefetch=1, grid=(n // tl,),
            in_specs=[pl.BlockSpec((k, rows, tl), lambda i, c: (0, 0, i)),
                      pl.BlockSpec((1, rows, tl), lambda i, c: (c[0], 0, i))],
            out_specs=pl.BlockSpec((rows, tl), lambda i, c: (0, i))),
        out_shape=jax.ShapeDtypeStruct((rows, n), F32), compiler_params=_params(("parallel",)),
    )(chip.reshape(1).astype(jnp.int32), got, own)


def _adamw_update(w, g, m, v):
    m2 = ADAM_B1 * m + (1.0 - ADAM_B1) * g
    v2 = ADAM_B2 * v + (1.0 - ADAM_B2) * jnp.square(g)
    m_hat = m2 / (1.0 - ADAM_B1 ** ADAM_STEP)
    v_hat = v2 / (1.0 - ADAM_B2 ** ADAM_STEP)
    return -ADAM_LR * (m_hat / (jnp.sqrt(v_hat) + ADAM_EPS) + ADAM_WD * w), m2, v2


def _adamw(name, w, g, m, v, rider=None):
    rows, n = w.shape
    if rows % 8 == 0:
        tr = _largest_divisor(rows, max(8, (1 << 18) // n), 8)
        block, index, steps = (tr, n), (lambda i: (i, 0)), rows // tr
    else:
        tl = _largest_divisor(n, max(LANE, (1 << 18) // rows), LANE)
        block, index, steps = (rows, tl), (lambda i: (0, i)), n // tl

    def body(w_ref, g_ref, m_ref, v_ref, d_ref, mo_ref, vo_ref):
        d_ref[...], mo_ref[...], vo_ref[...] = _adamw_update(w_ref[...], g_ref[...], m_ref[...], v_ref[...])

    spec = pl.BlockSpec(block, index)
    sds = jax.ShapeDtypeStruct((rows, n), F32)
    return _call(body, (w, g, m, v), name=name, grid=(steps,), in_specs=[spec] * 4, out_specs=(spec,) * 3,
                 out_shape=(sds, sds, sds), sem=("parallel",), rider=rider)


PACK_LANES = 1024


def _pack(pieces):
    flat = jnp.concatenate([p.reshape(-1) for p in pieces])
    total = -(-flat.shape[0] // (8 * PACK_LANES)) * 8 * PACK_LANES
    return jnp.pad(flat, (0, total - flat.shape[0])).reshape(-1, PACK_LANES)


def _unpack(packed, shapes):
    flat = packed.reshape(-1)
    out, off = [], 0
    for shp in shapes:
        size = math.prod(shp)
        out.append(flat[off:off + size].reshape(shp))
        off += size
    return out


def _rows8(rows, width):
    flat = [r.reshape(width) for r in rows] + [jnp.zeros(((8 - len(rows)) * width,), F32)]
    return jnp.concatenate(flat).reshape(8, width)


def kernel(x, c, ctx, c_ctx, w_mod, b_mod, w_in, conv_w, conv_b, hg_lb, ml_gate_b, hg_norm_w, ml_norm_w, w_out, ln_g, ln_b, loss_target, m_c_ctx, m_w_mod, m_b_mod, m_w_in, m_conv_w, m_conv_b, m_hg_lb, m_ml_gate_b, m_hg_norm_w, m_ml_norm_w, m_w_out, m_ln_g, m_ln_b, v_c_ctx, v_w_mod, v_b_mod, v_w_in, v_conv_w, v_conv_b, v_hg_lb, v_ml_gate_b, v_hg_norm_w, v_ml_norm_w, v_w_out, v_ln_g, v_ln_b):
    t_rows, dm = x.shape[1], x.shape[2]
    c_rows = ctx.shape[1]
    w = hg_norm_w.shape[1]
    n_ml = ml_gate_b.shape[-1]
    n_hg = w // HG_DK
    di = 2 * w
    n_in = 10 * w + 4 * n_ml
    ns = w_in.shape[2]
    nm = w_mod.shape[2]
    n_pad = 10 * w + LANE
    r_rows = t_rows + c_rows
    row_gcd = math.gcd(t_rows, c_rows)
    hg_chunk, ml_chunk = math.gcd(HG_CHUNK, row_gcd), math.gcd(ML_CHUNK, row_gcd)
    hg_counts = (t_rows // hg_chunk, c_rows // hg_chunk, hg_chunk)
    ml_counts = (t_rows // ml_chunk, c_rows // ml_chunk, ml_chunk)
    assert ml_norm_w.shape[1] == w and di == dm and N_CHIPS * ns == n_in and N_CHIPS * nm == 3 * dm
    assert w_out.shape[1] * N_CHIPS == di and 4 * n_ml <= LANE and t_rows % GRID_W == 0

    xi, yi, ci = lax.axis_index("x"), lax.axis_index("y"), lax.axis_index("c")
    chip = 2 * xi + yi
    dev = 4 * xi + 2 * yi + ci

    tm = _largest_divisor(math.gcd(t_rows, c_rows), 256, 8)
    tm_mm = _largest_divisor(r_rows, 1088, SUBLANE_BF16)
    tn_mm = LANE * _largest_divisor(n_pad // LANE, 9)
    tn_mod = _largest_divisor(nm, 512, LANE)

    as_t = lambda a: jnp.transpose(a[0])
    half_in = lax.dynamic_slice_in_dim(as_t(w_in).astype(BF16), ci * (dm // 2), dm // 2, 1)
    lanes_of = lambda core: pl.ds(core * (dm // 2), dm // 2)
    landing = lambda s, r: (_chip_of(s), slice(None), lanes_of(s[2]))
    own_placed = lax.dynamic_update_slice(jnp.zeros((N_CHIPS, ns, dm), BF16), as_t(w_in).astype(BF16)[None],
                                          (chip, 0, 0))
    gather_in = _Exchange([half_in, own_placed], [jax.ShapeDtypeStruct(own_placed.shape, BF16)],
                          [(mask, 0, None, 0, landing) for mask in CHIP_MASKS])
    gather_handles, started = _start_exchange("gather_w_in_start", gather_in)

    shard_shapes = [(dm,), (2, 2, w // N_CHIPS), (3, 3, di // N_CHIPS)]
    g1 = _all_gather8(_pack([c + started[0, 0], hg_lb, conv_w])).run("gather_inputs")[0]
    per_dev = [_unpack(g1[i], shard_shapes) for i in range(N_DEV)]
    c_all = jnp.stack([p[0] for p in per_dev])
    lb_full = jnp.concatenate([per_dev[2 * k][1] for k in range(N_CHIPS)], axis=-1)
    conv_w9 = jnp.concatenate([per_dev[2 * k][2] for k in range(N_CHIPS)], axis=-1).reshape(9, di)

    c16 = jnp.concatenate([c_all, c_ctx[None], jnp.zeros((16 - N_DEV - 1, dm), F32)])
    mod_part, a16 = _mod_fwd(c16, w_mod[0], tn_mod)
    g2 = _all_gather8(mod_part).run("gather_mod")[0]
    mod_all = jnp.concatenate([g2[2 * k] for k in range(N_CHIPS)], axis=1) + b_mod
    mod_x = lax.dynamic_index_in_dim(mod_all, dev, 0, keepdims=False).reshape(3, dm)
    mod_c = mod_all[N_DEV].reshape(3, dm)
    prm = jnp.stack([_rows8(list(mod_x), dm), _rows8(list(mod_c), dm)])

    half_out = lax.dynamic_slice_in_dim(w_out[0].astype(BF16), ci * (di // (2 * N_CHIPS)), di // (2 * N_CHIPS), 0)
    hc, _ = _modulate_fwd(x[0], ctx[0], prm, tm)
    (gw_in,) = _wait_exchange("gather_w_in_wait", gather_in, gather_handles, hc)
    my_lanes = lambda s, r: (slice(None), slice(None), lanes_of(s[2]))
    gw_in = _Exchange([gw_in], [jax.ShapeDtypeStruct(gw_in.shape, BF16)],
                      [(SIBLING_MASK, 0, my_lanes, 0, my_lanes)], in_place={0: 0}).run("gather_w_in_pair")[0]
    wt_full = jnp.concatenate([gw_in.reshape(n_in, dm), jnp.zeros((n_pad - n_in, dm), BF16)])
    u, (got_out,) = _mm_nt("in_proj", hc, wt_full, tm_mm, tn_mm, F32, rider=_all_gather_chips([half_out]))
    fetched_out = _own_block(chip, half_out, got_out)
    (o_f, o_b), hg_saved, (swapped_out,) = _hg_scan_fwd(u, lb_full, w, *hg_counts,
                                                         rider=_sibling_swap([fetched_out]))
    w_out_full = _join_halves(ci, fetched_out, swapped_out, 1).reshape(di, dm)
    qk = _conv_fwd(u, conv_w9, conv_b, t_rows, c_rows, w, LANE)
    gate_b_row = jnp.pad(ml_gate_b.reshape(1, -1), ((0, 0), (0, LANE - 4 * n_ml)))
    (h_f, h_b), ml_saved = _ml_scan_fwd(qk, u, gate_b_row, w, n_ml, *ml_counts)
    y = _post_fwd(o_f, o_b, h_f, h_b, u, hg_norm_w, ml_norm_w, t_rows, w, n_hg, n_ml, tm)
    prm_out = _rows8([mod_x[2], ln_g, ln_b], dm)
    dz, dy, gx_direct, acc_out = _out_block(y, w_out_full, x[0], loss_target[0], prm_out, tm)

    d_w_out = _mm_tn("d_w_out", y, dz, _largest_divisor(di, 1024, LANE),
                     _largest_divisor(t_rows, 1024, SUBLANE_BF16))
    d_w_out4 = d_w_out.reshape(N_CHIPS, 2, di // (2 * N_CHIPS), dm)
    mine_out = lax.dynamic_index_in_dim(d_w_out4, ci, 1, keepdims=False)
    other_out = lax.dynamic_index_in_dim(d_w_out4, 1 - ci, 1, keepdims=False)
    (d_o, d_h, d_az, d_bo, d_bz, d_wa, d_wb), (got_out,) = _post_bwd(
        o_f, o_b, h_f, h_b, u, hg_norm_w, ml_norm_w, dy, t_rows, w, n_hg, n_ml, tm, rider=_sibling_swap([other_out]))
    pair_out = _sum_pair("rs_pair_sum_w_out", mine_out, got_out)
    (d_aq_f, d_aff, d_ai_f, d_lb_f), (d_aq_b, d_afb, d_ai_b, d_lb_b), (landed_out,) = _hg_scan_bwd(
        u, lb_full, hg_saved, d_o, w, *hg_counts, rider=_chip_scatter([pair_out]))
    half_g_out = _sum_chips("rs_chip_sum_w_out", landed_out, pair_out, chip)
    (d_qk_f, d_v_f, d_g_f, d_gb_f), (d_qk_b, d_v_b, d_g_b, d_gb_b), (sibling_out,) = _ml_scan_bwd(
        qk, u, gate_b_row, ml_saved, d_h, w, n_ml, *ml_counts, rider=_sibling_swap([half_g_out]))
    g_w_out = _join_halves(ci, half_g_out, sibling_out, 0)
    d_bqk, d_cw, d_cb = _conv_bwd(u, (d_qk_f, d_qk_b), conv_w9, conv_b, t_rows, c_rows, w, LANE)
    du = _assemble_du([(d_aq_f, d_aq_b), d_aff, d_afb, (d_ai_f, d_ai_b), d_az, d_bqk, (d_v_f, d_v_b), d_bo, d_bz],
                      (d_g_f, d_g_b), n_pad, tm // 2)
    d_wt_in, d_wt_in_bf16 = _mm_tn("d_w_in", du, hc, tn_mm, tm_mm, with_bf16=True)

    got_in = _Exchange([d_wt_in_bf16], [jax.ShapeDtypeStruct((n_pad, dm // 2), BF16)],
                       [(SIBLING_MASK, 0, lambda s, r: (slice(None), lanes_of(r[2])), 0, None)]).run("rs_pair_w_in")[0]
    pair_half = _sum_pair_lanes("rs_pair_sum_w_in", d_wt_in, got_in, ci)
    pair_in = jnp.stack([pair_half[k * ns:(k + 1) * ns] for k in range(N_CHIPS)])
    d_hc, (landed_in,) = _mm_acc("d_h", du, wt_full, tm_mm, tn_mm, rider=_chip_scatter([pair_in]))
    half_g_in = _sum_chips("rs_chip_sum_w_in", landed_in, pair_in, chip)
    g_wt_in = _join_halves(ci, half_g_in, _sibling_swap([half_g_in]).run("rs_join_w_in")[0], 1)
    (gx, acc_mod), _ = _modulate_bwd(x[0], ctx[0], d_hc, prm, gx_direct, tm)
    grad_x = gx[None]

    zero_row = jnp.zeros((dm,), F32)
    d_gb = jnp.concatenate([d_gb_f[:, 0:n_ml], d_gb_b[:, n_ml:2 * n_ml], d_gb_f[:, 2 * n_ml:3 * n_ml],
                            d_gb_b[:, 3 * n_ml:4 * n_ml], jnp.zeros((1, dm - 4 * n_ml), F32)], axis=1)
    rows = [acc_mod[0, 0], acc_mod[0, 1], acc_out[OUT_ROW_GATE],
            acc_mod[1, 0], acc_mod[1, 1], zero_row]
    rows += list(d_cw) + [d_cb[0], d_lb_f.reshape(dm), d_lb_b.reshape(dm),
                          jnp.concatenate([d_wa[0], d_wb[0]]), acc_out[OUT_ROW_LN_G], acc_out[OUT_ROW_LN_B],
                          acc_out[OUT_ROW_LOSS], d_gb[0], zero_row]
    ROW_CW, ROW_CB, ROW_LB, ROW_NORM, ROW_LN_G, ROW_LN_B, ROW_LOSS, ROW_GB = 6, 15, 16, 18, 19, 20, 21, 22
    delta, new_m, new_v = {}, {}, {}
    small_rows = jnp.concatenate([r.reshape(dm) for r in rows]).reshape(len(rows), dm)
    g3 = _all_gather8(small_rows).run("gather_small_grads")[0]
    sums, totals = _sum_devices(g3, 3)
    loss = totals[ROW_LOSS, 0]
    dm16 = jnp.concatenate([g3[:, 0:3, :].reshape(N_DEV, 3 * dm), sums[3:6].reshape(1, 3 * dm),
                            jnp.zeros((16 - N_DEV - 1, 3 * dm), F32)])
    g_w_mod, dc16 = _mod_bwd(a16, lax.dynamic_slice_in_dim(dm16, chip * nm, nm, 1), w_mod[0], tn_mod)
    g4 = _all_gather8(jnp.pad(dc16[N_DEV:N_DEV + 1], ((0, 7), (0, 0)))).run("gather_c_ctx")[0]
    g_c_ctx = _c_ctx_grad(g4, jnp.broadcast_to(c_ctx[None], (8, dm)))[0]
    res, _ = _adamw("adamw_w_in", as_t(w_in), g_wt_in, as_t(m_w_in), as_t(v_w_in))
    delta["w_in"], new_m["w_in"], new_v["w_in"] = (jnp.transpose(a)[None] for a in res)
    res, _ = _adamw("adamw_w_mod", w_mod[0], g_w_mod, m_w_mod[0], v_w_mod[0])
    delta["w_mod"], new_m["w_mod"], new_v["w_mod"] = (a[None] for a in res)

    chip_cols = lambda a, width: lax.dynamic_slice_in_dim(a, chip * width, width, a.ndim - 1)
    grads = {
        "c_ctx": g_c_ctx,
        "w_mod": g_w_mod[None],
        "b_mod": sums[0:3].reshape(1, 3 * dm),
        "w_in": jnp.transpose(g_wt_in)[None],
        "conv_w": chip_cols(sums[ROW_CW:ROW_CW + 9].reshape(1, 3, 3, di), di // N_CHIPS),
        "conv_b": sums[ROW_CB][None],
        "hg_lb": chip_cols(sums[ROW_LB:ROW_LB + 2].reshape(2, 2, w), w // N_CHIPS),
        "ml_gate_b": sums[ROW_GB, 0:4 * n_ml].reshape(1, 4, n_ml),
        "hg_norm_w": sums[ROW_NORM, 0:w][None],
        "ml_norm_w": sums[ROW_NORM, w:2 * w][None],
        "w_out": g_w_out[None],
        "ln_g": sums[ROW_LN_G][None],
        "ln_b": sums[ROW_LN_B][None],
    }
    weights = dict(c_ctx=c_ctx, w_mod=w_mod, b_mod=b_mod, w_in=w_in, conv_w=conv_w, conv_b=conv_b, hg_lb=hg_lb,
                   ml_gate_b=ml_gate_b, hg_norm_w=hg_norm_w, ml_norm_w=ml_norm_w, w_out=w_out, ln_g=ln_g, ln_b=ln_b)
    mom1 = dict(c_ctx=m_c_ctx, w_mod=m_w_mod, b_mod=m_b_mod, w_in=m_w_in, conv_w=m_conv_w, conv_b=m_conv_b,
                hg_lb=m_hg_lb, ml_gate_b=m_ml_gate_b, hg_norm_w=m_hg_norm_w, ml_norm_w=m_ml_norm_w, w_out=m_w_out,
                ln_g=m_ln_g, ln_b=m_ln_b)
    mom2 = dict(c_ctx=v_c_ctx, w_mod=v_w_mod, b_mod=v_b_mod, w_in=v_w_in, conv_w=v_conv_w, conv_b=v_conv_b,
                hg_lb=v_hg_lb, ml_gate_b=v_ml_gate_b, hg_norm_w=v_hg_norm_w, ml_norm_w=v_ml_norm_w, w_out=v_w_out,
                ln_g=v_ln_g, ln_b=v_ln_b)
    names = list(weights)
    big = ("w_mod", "w_in", "w_out")
    small = [n for n in names if n not in big]

    res, _ = _adamw("adamw_w_out", w_out[0], g_w_out, m_w_out[0], v_w_out[0])
    delta["w_out"], new_m["w_out"], new_v["w_out"] = (a[None] for a in res)
    small_shapes = [weights[n].shape for n in small]
    res, _ = _adamw("adamw_small", *(_pack([src[n] for n in small]) for src in (weights, grads, mom1, mom2)))
    for out, packed in zip((delta, new_m, new_v), res):
        for n, a in zip(small, _unpack(packed, small_shapes)):
            out[n] = a

    return (loss, grad_x, *[grads[n].reshape(weights[n].shape) for n in names], *[delta[n] for n in names],
            *[new_m[n] for n in names], *[new_v[n] for n in names])
```

```python
import functools
import math

import jax
import jax.numpy as jnp
from jax import lax
from jax.experimental import pallas as pl
from jax.experimental.pallas import tpu as pltpu

F32 = jnp.float32
BF16 = jnp.bfloat16
HIGHEST = lax.Precision.HIGHEST
MESH = pl.DeviceIdType.MESH

HG_CHUNK = 64
ML_CHUNK = 256
GRID_W = 64
HG_DK = 128
LANE = 128
SUBLANE_BF16 = 16
ALPHA = 2.0 ** 0.25
LN_EPS = 1e-5
NORM_EPS = 1e-6
ADAM_LR = 0.001
ADAM_B1 = 0.9
ADAM_B2 = 0.999
ADAM_EPS = 1e-08
ADAM_WD = 0.01
ADAM_STEP = 10
VMEM_LIMIT = 56 * 1024 * 1024
N_CHIPS = 4
N_DEV = 8


def _params(sem=None):
    return pltpu.CompilerParams(dimension_semantics=sem, vmem_limit_bytes=VMEM_LIMIT)


def _largest_divisor(n, cap, multiple=1):
    best = None
    for d in range(multiple, min(n, cap) + 1, multiple):
        if n % d == 0:
            best = d
    assert best is not None, (n, cap, multiple)
    return best


def _sigmoid(x):
    return jax.nn.sigmoid(x)


def _silu(x):
    return x * jax.nn.sigmoid(x)


def _dot(a, b, dims, precision=None):
    return lax.dot_general(a, b, (dims, ((), ())), precision=precision, preferred_element_type=F32)


def _nn(a, b, precision=None):
    return _dot(a, b, ((1,), (0,)), precision)


def _nt(a, b, precision=None):
    return _dot(a, b, ((1,), (1,)), precision)


def _tn(a, b, precision=None):
    return _dot(a, b, ((0,), (0,)), precision)


def _narrow(x):
    return x.astype(BF16)


@jax.custom_vjp
def _bnn(a, b):
    return _nn(_narrow(a), _narrow(b))


def _bnn_fwd(a, b):
    an, bn = _narrow(a), _narrow(b)
    return _nn(an, bn), (an, bn)


def _bnn_bwd(res, ct):
    an, bn = res
    ctn = _narrow(ct)
    return _nt(ctn, bn), _tn(an, ctn)


_bnn.defvjp(_bnn_fwd, _bnn_bwd)


@jax.custom_vjp
def _bnt(a, b):
    return _nt(_narrow(a), _narrow(b))


def _bnt_fwd(a, b):
    an, bn = _narrow(a), _narrow(b)
    return _nt(an, bn), (an, bn)


def _bnt_bwd(res, ct):
    an, bn = res
    ctn = _narrow(ct)
    return _nn(ctn, bn), _tn(ctn, an)


_bnt.defvjp(_bnt_fwd, _bnt_bwd)


@jax.custom_vjp
def _btn(a, b):
    return _tn(_narrow(a), _narrow(b))


def _btn_fwd(a, b):
    an, bn = _narrow(a), _narrow(b)
    return _tn(an, bn), (an, bn)


def _btn_bwd(res, ct):
    an, bn = res
    ctn = _narrow(ct)
    return _nt(bn, ctn), _nn(an, ctn)


_btn.defvjp(_btn_fwd, _btn_bwd)


def _visible(n, rev):
    r = lax.broadcasted_iota(jnp.int32, (n, n), 0)
    c = lax.broadcasted_iota(jnp.int32, (n, n), 1)
    return (r <= c) if rev else (r >= c)


def _mask_matmul(mask, x):
    mb = mask.astype(BF16)
    hi = x.astype(BF16)
    lo = (x - hi.astype(F32)).astype(BF16)
    return _nn(mb, hi) + _nn(mb, lo)


@functools.partial(jax.custom_vjp, nondiff_argnums=(1,))
def _cumulative(x, rev):
    return _mask_matmul(_visible(x.shape[0], rev), x)


def _cumulative_fwd(x, rev):
    return _cumulative(x, rev), None


def _cumulative_bwd(rev, _, ct):
    return (_mask_matmul(_visible(ct.shape[0], not rev), ct),)


_cumulative.defvjp(_cumulative_fwd, _cumulative_bwd)


def _hg_chunk(states, aq, af, ai, lb0, lb1, rev):
    n_heads = len(states)
    lb = _sigmoid(lb0 - lb1)
    f = lb + (1.0 - lb) * _sigmoid(af)
    g = jnp.log(f)
    k = 1.0 - f
    q = _silu(aq)
    chunk = aq.shape[0]
    vis = _visible(chunk, rev)
    b = _cumulative(g, rev)
    last = 0 if rev else chunk - 1
    b_end = b[last:last + 1]
    b_mid = b[chunk // 2:chunk // 2 + 1]
    q_inter = q * jnp.exp(b)
    q_intra = q * jnp.exp(b - b_mid)
    k_intra = k * jnp.exp(b_mid - b)
    k_dec = k * jnp.exp(b_end - b)
    e_end = jnp.exp(b_end)
    new_states, outs = [], []
    for h in range(n_heads):
        sl = slice(h * HG_DK, (h + 1) * HG_DK)
        s_t = states[h]
        scores = jnp.where(vis, _nt(q_intra[:, sl], k_intra[:, sl]), 0.0)
        outs.append(_nt(q_inter[:, sl], s_t) + _nn(scores, ai[:, sl]))
        new_states.append(e_end[:, sl] * s_t + _tn(ai[:, sl], k_dec[:, sl]))
    return new_states, jnp.concatenate(outs, axis=1)


def _ml_chunk(state, q, k, v, g, gb, rev, d):
    cms, nvs, mbs = state
    n_heads = len(cms)
    dh = q.shape[1] // n_heads
    ga = g + gb
    log_f_all = jax.nn.log_sigmoid(ga)
    chunk = q.shape[0]
    vis = _visible(chunk, rev)
    b_all = _cumulative(log_f_all, rev)
    last = 0 if rev else chunk - 1
    k = k * (dh ** -0.5)
    new_c, new_n, new_m, outs = [], [], [], []
    for h in range(n_heads):
        ci = d * n_heads + h
        cf = (2 + d) * n_heads + h
        sl = slice(h * dh, (h + 1) * dh)
        qh, kh, vh = q[:, sl], k[:, sl], v[:, sl]
        li = ga[:, ci:ci + 1]
        b = b_all[:, cf:cf + 1]
        m = mbs[h][:, 0:1]
        row = jnp.transpose(li - b)
        log_w = jnp.where(vis, b + row, -jnp.inf)
        m_inter = b + m
        m_t = jnp.maximum(m_inter, jnp.max(log_w, axis=-1, keepdims=True))
        w_inter = jnp.exp(m_inter - m_t)
        w_qk = jnp.exp(log_w - m_t) * _bnt(qh, kh)
        num = w_inter * _bnt(qh, cms[h]) + _bnn(w_qk, vh)
        den = w_inter * jnp.sum(qh * nvs[h], axis=-1, keepdims=True) + jnp.sum(w_qk, axis=-1, keepdims=True)
        outs.append(num / jnp.maximum(jnp.abs(den), jnp.exp(-m_t)))
        m_new = m_t[last:last + 1]
        b_end = b[last:last + 1]
        w_s = jnp.exp(b_end - b + li - m_new)
        decay = jnp.exp(b_end + m - m_new)
        new_c.append(decay * cms[h] + _btn(w_s * vh, kh))
        new_n.append(decay * nvs[h] + jnp.sum(w_s * kh, axis=0, keepdims=True))
        new_m.append(jnp.broadcast_to(m_new, (1, LANE)))
    return (new_c, new_n, new_m), jnp.concatenate(outs, axis=1)


def _post_fn(o_f, o_b, az, h_f, h_b, bo, bz, wa, wb, n_hg, n_ml):
    o = o_f + o_b
    parts = []
    for h in range(n_hg):
        s = o[:, h * HG_DK:(h + 1) * HG_DK]
        parts.append(s * lax.rsqrt(jnp.mean(s * s, axis=-1, keepdims=True) + NORM_EPS))
    y_a = jnp.concatenate(parts, axis=1) * wa * _silu(az)
    hh = h_f + h_b
    dh = hh.shape[1] // n_ml
    parts = []
    for h in range(n_ml):
        s = hh[:, h * dh:(h + 1) * dh]
        mu = jnp.mean(s, axis=-1, keepdims=True)
        sc = s - mu
        parts.append(sc * lax.rsqrt(jnp.mean(sc * sc, axis=-1, keepdims=True) + NORM_EPS))
    y_b = jnp.concatenate(parts, axis=1) * wb * _sigmoid(bo) * _silu(bz)
    return jnp.concatenate([y_a, y_b], axis=1)


def _chip_of(dev):
    return 2 * dev[0] + dev[1]


def _index_of(dev):
    return 4 * dev[0] + 2 * dev[1] + dev[2]


class _Exchange:
    def __init__(self, srcs, out_shapes, transfers, local_copies=(), in_place=None):
        self.srcs, self.out_shapes = list(srcs), list(out_shapes)
        self.transfers, self.local_copies = list(transfers), list(local_copies)
        self.in_place = dict(in_place or {})

    def scratch(self):
        return [pltpu.SemaphoreType.DMA((len(self.transfers),)), pltpu.SemaphoreType.DMA((len(self.transfers),)),
                pltpu.SemaphoreType.DMA((max(len(self.local_copies), 1),))]

    def copies(self, ins, outs, send_sems, recv_sems, local_sems):
        me = (lax.axis_index("x"), lax.axis_index("y"), lax.axis_index("c"))

        def pick(ref, fn, *who):
            return ref if fn is None else ref.at[fn(*who)]

        sends, recvs, locs = [], [], []
        for t, (mask, si, sfn, di, dfn) in enumerate(self.transfers):
            peer = tuple(1 - p if flip else p for p, flip in zip(me, mask))
            sends.append(pltpu.make_async_remote_copy(
                src_ref=pick(ins[si], sfn, me, peer), dst_ref=pick(outs[di], dfn, me, peer),
                send_sem=send_sems.at[t], recv_sem=recv_sems.at[t], device_id=peer, device_id_type=MESH))
            landing = pick(outs[di], dfn, peer, me)
            recvs.append(pltpu.make_async_remote_copy(
                src_ref=landing, dst_ref=landing,
                send_sem=send_sems.at[t], recv_sem=recv_sems.at[t], device_id=peer, device_id_type=MESH))
        for l, (si, sfn, di, dfn) in enumerate(self.local_copies):
            locs.append(pltpu.make_async_copy(pick(ins[si], sfn, me), pick(outs[di], dfn, me), local_sems.at[l]))

        def start():
            for cp in locs + sends:
                cp.start()

        def wait():
            for cp in recvs:
                cp.wait_recv()
            for cp in sends:
                cp.wait_send()
            for cp in locs:
                cp.wait()

        return start, wait

    def run(self, name):
        n_in, n_out = len(self.srcs), len(self.out_shapes)

        def body(*refs):
            start, wait = self.copies(refs[:n_in], refs[n_in:n_in + n_out], *refs[n_in + n_out:])
            start()
            wait()

        hbm = pl.BlockSpec(memory_space=pltpu.HBM)
        return pl.pallas_call(
            body, name=name, out_shape=tuple(self.out_shapes), in_specs=[hbm] * n_in,
            out_specs=tuple([hbm] * n_out), scratch_shapes=self.scratch(), input_output_aliases=self.in_place,
        )(*self.srcs)


def _call(body, operands, *, name, grid, in_specs, out_specs, out_shape, scratch_shapes=(), sem=None, rider=None):
    out_specs, out_shape, scratch_shapes = list(out_specs), list(out_shape), list(scratch_shapes)
    if rider is None:
        res = pl.pallas_call(
            body, name=name, grid=grid, in_specs=list(in_specs), out_specs=tuple(out_specs),
            out_shape=tuple(out_shape), scratch_shapes=scratch_shapes, compiler_params=_params(sem),
        )(*operands)
        return list(res), []
    counts = (len(in_specs), len(rider.srcs), len(out_specs), len(rider.out_shapes), len(scratch_shapes), 3)

    def full(*refs):
        groups, pos = [], 0
        for k in counts:
            groups.append(refs[pos:pos + k])
            pos += k
        own_in, ex_in, own_out, ex_out, own_scr, ex_scr = groups
        ids = [pl.program_id(a) for a in range(len(grid))]
        first = functools.reduce(jnp.logical_and, [i == 0 for i in ids])
        last = functools.reduce(jnp.logical_and, [i == g - 1 for i, g in zip(ids, grid)])
        start, wait = rider.copies(ex_in, ex_out, *ex_scr)
        pl.when(first)(start)
        body(*own_in, *own_out, *own_scr)
        pl.when(last)(wait)

    hbm = pl.BlockSpec(memory_space=pltpu.HBM)
    res = pl.pallas_call(
        full, name=name, grid=grid, in_specs=list(in_specs) + [hbm] * counts[1],
        out_specs=tuple(out_specs + [hbm] * counts[3]), out_shape=tuple(out_shape + rider.out_shapes),
        scratch_shapes=scratch_shapes + rider.scratch(), compiler_params=_params(("arbitrary",) * len(grid)),
        input_output_aliases={counts[0] + i: counts[2] + o for i, o in rider.in_place.items()},
    )(*operands, *rider.srcs)
    return list(res[:counts[2]]), list(res[counts[2]:])


ALL_MASKS = [(mx, my, mc) for mx in (0, 1) for my in (0, 1) for mc in (0, 1)][1:]
CHIP_MASKS = [(1, 0, 0), (0, 1, 0), (1, 1, 0)]
SIBLING_MASK = (0, 0, 1)


def _all_gather8(v):
    out = jax.ShapeDtypeStruct((N_DEV,) + v.shape, v.dtype)
    slot = lambda sender, receiver: _index_of(sender)
    transfers = [(mask, 0, None, 0, slot) for mask in ALL_MASKS]
    return _Exchange([v], [out], transfers, [(0, None, 0, lambda me: _index_of(me))])


def _all_gather_chips(arrays):
    outs = [jax.ShapeDtypeStruct((N_CHIPS,) + a.shape, a.dtype) for a in arrays]
    slot = lambda sender, receiver: _chip_of(sender)
    return _Exchange(arrays, outs, [(mask, i, None, i, slot) for i in range(len(arrays)) for mask in CHIP_MASKS])


def _sibling_swap(arrays):
    outs = [jax.ShapeDtypeStruct(a.shape, a.dtype) for a in arrays]
    return _Exchange(arrays, outs, [(SIBLING_MASK, i, None, i, None) for i in range(len(arrays))])


def _chip_scatter(arrays):
    outs = [jax.ShapeDtypeStruct(a.shape, a.dtype) for a in arrays]
    transfers = [(mask, i, lambda s, r: _chip_of(r), i, lambda s, r: _chip_of(s))
                 for i in range(len(arrays)) for mask in CHIP_MASKS]
    return _Exchange(arrays, outs, transfers)


def _own_block(chip, own, blocks):
    sel = (lax.broadcasted_iota(jnp.int32, (N_CHIPS,) + (1,) * (blocks.ndim - 1), 0) == chip)
    return jnp.where(sel, own if own.ndim == blocks.ndim else own[None], blocks)


def _join_halves(ci, mine, other, axis):
    return jnp.where(ci == 0, jnp.concatenate([mine, other], axis=axis), jnp.concatenate([other, mine], axis=axis))


def _mm_nt(name, a, b, n, tm, tn, out_dtype, rider=None):
    m, k = a.shape

    def body(a_ref, b_ref, o_ref):
        o_ref[...] = _nt(a_ref[...], b_ref[...]).astype(out_dtype)

    (out,), rode = _call(
        body, (a, b), name=name, grid=(n // tn, m // tm),
        in_specs=[pl.BlockSpec((tm, k), lambda j, i: (i, 0)), pl.BlockSpec((tn, k), lambda j, i: (j, 0))],
        out_specs=[pl.BlockSpec((tm, tn), lambda j, i: (i, j))],
        out_shape=[jax.ShapeDtypeStruct((m, n), out_dtype)], sem=("parallel", "parallel"), rider=rider)
    return out, rode


def _mm_acc(name, a, b, tm, tk, rider=None):
    m, kc = a.shape
    n = b.shape[1]

    def body(a_ref, b_ref, o_ref):
        @pl.when(pl.program_id(1) == 0)
        def _():
            o_ref[...] = jnp.zeros_like(o_ref)
        o_ref[...] += _nn(a_ref[...], b_ref[...])

    (out,), rode = _call(
        body, (a, b), name=name, grid=(m // tm, kc // tk),
        in_specs=[pl.BlockSpec((tm, tk), lambda i, kk: (i, kk)), pl.BlockSpec((tk, n), lambda i, kk: (kk, 0))],
        out_specs=[pl.BlockSpec((tm, n), lambda i, kk: (i, 0))],
        out_shape=[jax.ShapeDtypeStruct((m, n), F32)], sem=("parallel", "arbitrary"), rider=rider)
    return out, rode


def _mm_tn(name, a, b, tm, tk, with_bf16=False):
    kr, m = a.shape
    n = b.shape[1]
    steps_k = kr // tk

    def body(a_ref, b_ref, o_ref, *narrow):
        @pl.when(pl.program_id(1) == 0)
        def _():
            o_ref[...] = jnp.zeros_like(o_ref)
        o_ref[...] += _tn(a_ref[...], b_ref[...])
        if with_bf16:
            @pl.when(pl.program_id(1) == steps_k - 1)
            def _():
                narrow[0][...] = o_ref[...].astype(BF16)

    out_spec = pl.BlockSpec((tm, n), lambda i, kk: (i, 0))
    res = pl.pallas_call(
        body, name=name, grid=(m // tm, steps_k),
        in_specs=[pl.BlockSpec((tk, tm), lambda i, kk: (kk, i)), pl.BlockSpec((tk, n), lambda i, kk: (kk, 0))],
        out_specs=(out_spec,) * (2 if with_bf16 else 1),
        out_shape=(jax.ShapeDtypeStruct((m, n), F32),) + ((jax.ShapeDtypeStruct((m, n), BF16),) if with_bf16 else ()),
        compiler_params=_params(("parallel", "arbitrary")),
    )(a, b)
    return res if with_bf16 else res[0]


def _modulate_fwd(x, ctx, prm, tm, rider=None):
    t_rows, dm = x.shape
    lat = t_rows // tm
    r = t_rows + ctx.shape[0]

    def body(x_ref, c_ref, p_ref, h_ref):
        xv = jnp.where(pl.program_id(0) >= lat, c_ref[...], x_ref[...])
        mu = jnp.mean(xv, axis=-1, keepdims=True)
        xm = xv - mu
        n = xm * lax.rsqrt(jnp.mean(xm * xm, axis=-1, keepdims=True) + LN_EPS)
        h_ref[...] = (n * (1.0 + p_ref[0, 1:2, :]) + p_ref[0, 0:1, :]).astype(BF16)

    (h,), rode = _call(
        body, (x, ctx, prm), name="modulate_fwd", grid=(r // tm,),
        in_specs=[pl.BlockSpec((tm, dm), lambda i: (jnp.minimum(i, lat - 1), 0)),
                  pl.BlockSpec((tm, dm), lambda i: (jnp.maximum(i - lat, 0), 0)),
                  pl.BlockSpec((1, 8, dm), lambda i: ((i >= lat).astype(jnp.int32), 0, 0))],
        out_specs=[pl.BlockSpec((tm, dm), lambda i: (i, 0))],
        out_shape=[jax.ShapeDtypeStruct((r, dm), BF16)], sem=("parallel",), rider=rider)
    return h, rode


def _modulate_bwd(x, ctx, dh, prm, gx_direct, tm, rider=None):
    t_rows, dm = x.shape
    lat, n_ct = t_rows // tm, ctx.shape[0] // tm
    is_ctx = lambda i: i < n_ct
    cls = lambda i: is_ctx(i).astype(jnp.int32)
    lat_tile = lambda i: (jnp.maximum(i - n_ct, 0), 0)

    def body(x_ref, c_ref, dh_ref, p_ref, gd_ref, gx_ref, acc_ref):
        i = pl.program_id(0)

        @pl.when((i == 0) | (i == n_ct))
        def _():
            acc_ref[...] = jnp.zeros_like(acc_ref)

        x = jnp.where(is_ctx(i), c_ref[...], x_ref[...])
        dh_v = dh_ref[...]
        mu = jnp.mean(x, axis=-1, keepdims=True)
        xm = x - mu
        rstd = lax.rsqrt(jnp.mean(xm * xm, axis=-1, keepdims=True) + LN_EPS)
        n = xm * rstd
        acc_ref[0, 0:1, :] += jnp.sum(dh_v, axis=0, keepdims=True)
        acc_ref[0, 1:2, :] += jnp.sum(dh_v * n, axis=0, keepdims=True)
        dn = dh_v * (1.0 + p_ref[0, 1:2, :])
        dx = rstd * (dn - jnp.mean(dn, axis=-1, keepdims=True) - n * jnp.mean(dn * n, axis=-1, keepdims=True))
        gx_ref[...] = dx + gd_ref[...]

    return _call(
        body, (x, ctx, dh, prm, gx_direct), name="modulate_bwd", grid=(n_ct + lat,),
        in_specs=[pl.BlockSpec((tm, dm), lat_tile),
                  pl.BlockSpec((tm, dm), lambda i: (jnp.minimum(i, n_ct - 1), 0)),
                  pl.BlockSpec((tm, dm), lambda i: (jnp.where(is_ctx(i), lat + i, i - n_ct), 0)),
                  pl.BlockSpec((1, 8, dm), lambda i: (cls(i), 0, 0)),
                  pl.BlockSpec((tm, dm), lat_tile)],
        out_specs=(pl.BlockSpec((tm, dm), lat_tile), pl.BlockSpec((1, 8, dm), lambda i: (cls(i), 0, 0))),
        out_shape=(jax.ShapeDtypeStruct((t_rows, dm), F32), jax.ShapeDtypeStruct((2, 8, dm), F32)),
        sem=("arbitrary",), rider=rider)


def _conv_parts(t_rows, c_rows):
    return ((0, t_rows, t_rows // GRID_W, GRID_W), (t_rows, c_rows, 1, c_rows))


def _col_shifts(x2, rows_g, width_g):
    n, ct = x2.shape
    col = lax.broadcasted_iota(jnp.int32, (width_g, ct), 0)
    as_grid = lambda a: a.reshape(rows_g, width_g, ct)
    left = as_grid(pltpu.roll(x2, 1, 0)) * (col >= 1).astype(F32)
    right = as_grid(pltpu.roll(x2, n - 1, 0)) * (col <= width_g - 2).astype(F32)
    return [left, as_grid(x2), right]


CONV_BLOCK_ROWS = 4


def _conv_blocks(t_rows, c_rows):
    for t0, _, rows_g, width_g in _conv_parts(t_rows, c_rows):
        nb = min(CONV_BLOCK_ROWS, rows_g)
        assert rows_g % nb == 0
        for g0 in range(0, rows_g, nb):
            yield t0, rows_g, width_g, g0, nb


def _slab(ref, t0, rows_g, width_g, g0, nb):
    if rows_g == 1:
        return ref[t0:t0 + width_g, :]
    lo, hi = max(g0 - 1, 0), min(g0 + nb + 1, rows_g)
    parts = [ref[t0 + lo * width_g:t0 + hi * width_g, :]]
    zero = jnp.zeros((width_g, ref.shape[1]), F32)
    if g0 == 0:
        parts.insert(0, zero)
    if g0 + nb == rows_g:
        parts.append(zero)
    return jnp.concatenate(parts, axis=0)


def _conv_taps(cols, w_ref, nb, flip):
    one_row = cols[0].shape[0] == nb
    acc = None
    for a in range(3):
        if one_row and a != 1:
            continue
        for b in range(3):
            tap = (2 - a) * 3 + (2 - b) if flip else a * 3 + b
            term = (cols[b] if one_row else cols[b][a:a + nb]) * w_ref[tap:tap + 1, :]
            acc = term if acc is None else acc + term
    return acc


def _conv_fwd(u, conv_w9, conv_b, t_rows, c_rows, w, ct):
    r = u.shape[0]
    base = 5 * w // ct

    def body(x_ref, w_ref, b_ref, o_ref):
        for t0, rows_g, width_g, g0, nb in _conv_blocks(t_rows, c_rows):
            slab = _slab(x_ref, t0, rows_g, width_g, g0, nb)
            cols = _col_shifts(slab, slab.shape[0] // width_g, width_g)
            pre = _conv_taps(cols, w_ref, nb, False) + b_ref[...]
            o_ref[t0 + g0 * width_g:t0 + (g0 + nb) * width_g, :] = _silu(pre).reshape(nb * width_g, ct)

    return pl.pallas_call(
        body, name="conv_fwd", grid=(2 * w // ct,),
        in_specs=[pl.BlockSpec((r, ct), lambda i: (0, base + i)), pl.BlockSpec((9, ct), lambda i: (0, i)),
                  pl.BlockSpec((1, ct), lambda i: (0, i))],
        out_specs=pl.BlockSpec((r, ct), lambda i: (0, i)),
        out_shape=jax.ShapeDtypeStruct((r, 2 * w), F32),
        compiler_params=_params(("parallel",)),
    )(u, conv_w9, conv_b)


def _conv_bwd(u, dqk_pair, conv_w9, conv_b, t_rows, c_rows, w, ct):
    r = u.shape[0]
    base = 5 * w // ct

    def body(x_ref, d1_ref, d2_ref, w_ref, b_ref, dx_ref, dw_ref, db_ref, dpre_ref):
        dw = [jnp.zeros((1, ct), F32) for _ in range(9)]
        db = jnp.zeros((1, ct), F32)
        for t0, rows_g, width_g, g0, nb in _conv_blocks(t_rows, c_rows):
            rows = slice(t0 + g0 * width_g, t0 + (g0 + nb) * width_g)
            slab = _slab(x_ref, t0, rows_g, width_g, g0, nb)
            cols = _col_shifts(slab, slab.shape[0] // width_g, width_g)
            pre = _conv_taps(cols, w_ref, nb, False) + b_ref[...]
            sg = _sigmoid(pre)
            dpre = (d1_ref[rows, :] + d2_ref[rows, :]).reshape(pre.shape) * (sg * (1.0 + pre * (1.0 - sg)))
            dpre_ref[rows, :] = dpre.reshape(nb * width_g, ct)
            db = db + jnp.sum(jnp.sum(dpre, axis=0), axis=0, keepdims=True)
            for a in range(3):
                if rows_g == 1 and a != 1:
                    continue
                for b in range(3):
                    moved = cols[b] if rows_g == 1 else cols[b][a:a + nb]
                    dw[a * 3 + b] = dw[a * 3 + b] + jnp.sum(jnp.sum(moved * dpre, axis=0), axis=0, keepdims=True)
        for t0, rows_g, width_g, g0, nb in _conv_blocks(t_rows, c_rows):
            slab = _slab(dpre_ref, t0, rows_g, width_g, g0, nb)
            cols = _col_shifts(slab, slab.shape[0] // width_g, width_g)
            dx_ref[t0 + g0 * width_g:t0 + (g0 + nb) * width_g, :] = _conv_taps(cols, w_ref, nb, True).reshape(
                nb * width_g, ct).astype(BF16)
        for tap in range(9):
            dw_ref[tap:tap + 1, :] = dw[tap]
        db_ref[...] = db

    return pl.pallas_call(
        body, name="conv_bwd", grid=(2 * w // ct,),
        in_specs=[pl.BlockSpec((r, ct), lambda i: (0, base + i)), pl.BlockSpec((r, ct), lambda i: (0, i)),
                  pl.BlockSpec((r, ct), lambda i: (0, i)),
                  pl.BlockSpec((9, ct), lambda i: (0, i)), pl.BlockSpec((1, ct), lambda i: (0, i))],
        out_specs=(pl.BlockSpec((r, ct), lambda i: (0, i)), pl.BlockSpec((9, ct), lambda i: (0, i)),
                   pl.BlockSpec((1, ct), lambda i: (0, i))),
        out_shape=(jax.ShapeDtypeStruct((r, 2 * w), BF16), jax.ShapeDtypeStruct((9, 2 * w), F32),
                   jax.ShapeDtypeStruct((1, 2 * w), F32)),
        scratch_shapes=[pltpu.VMEM((r, ct), F32)],
        compiler_params=_params(("parallel",)),
    )(u, dqk_pair[0], dqk_pair[1], conv_w9, conv_b)


def _assemble_du(groups, gates, n_pad, tm):
    flat, layout = [], []
    for entry in list(groups) + [gates]:
        parts = entry if isinstance(entry, (tuple, list)) else (entry,)
        layout.append((len(flat), len(parts), parts[0].shape[1]))
        flat += list(parts)
    r = flat[0].shape[0]

    def body(*refs):
        o_ref = refs[-1]
        col = 0
        for first, count, width in layout:
            val = refs[first][...]
            for extra in range(1, count):
                val = val.astype(F32) + refs[first + extra][...].astype(F32)
            o_ref[:, col:col + width] = val.astype(BF16)
            col += width
        assert col == n_pad

    return pl.pallas_call(
        body, name="assemble_du", grid=(r // tm,),
        in_specs=[pl.BlockSpec((tm, a.shape[1]), lambda i: (i, 0)) for a in flat],
        out_specs=pl.BlockSpec((tm, n_pad), lambda i: (i, 0)),
        out_shape=jax.ShapeDtypeStruct((r, n_pad), BF16),
        compiler_params=_params(("parallel",)),
    )(*flat)


def _scan_order(n_lat, n_ctx, rev):
    n = n_lat + n_ctx
    if rev:
        return lambda j: n - 1 - j
    return lambda j: (j + n_lat) % n


DIRS = (False, True)


def _hg_scan_fwd(u, lb_full, w, n_lat, n_ctx, chunk, rider=None):
    r = u.shape[0]
    n_heads = w // HG_DK
    n_chunks = n_lat + n_ctx
    nat = [_scan_order(n_lat, n_ctx, rev) for rev in DIRS]

    def body(*refs):
        ins, outs, scratch = refs[:8], refs[8:12], refs[12:]

        @pl.when(pl.program_id(0) == 0)
        def _():
            for s_ref in scratch:
                s_ref[...] = jnp.zeros_like(s_ref)

        results = []
        for d, rev in enumerate(DIRS):
            aq, af, ai, lb_ref = ins[4 * d:4 * d + 4]
            state = [scratch[d][h] for h in range(n_heads)]
            results.append((state, _hg_chunk(state, aq[...], af[...], ai[...],
                                             lb_ref[0, 0:1, :], lb_ref[0, 1:2, :], rev)))
        for d, (state, (new, o)) in enumerate(results):
            o_ref, save_ref = outs[2 * d:2 * d + 2]
            o_ref[...] = o
            for h in range(n_heads):
                save_ref[0, h] = state[h]
                scratch[d][h] = new[h]

    in_specs, out_specs, out_shape = [], [], []
    for d in range(2):
        in_specs += [pl.BlockSpec((chunk,w), lambda j, d=d: (nat[d](j), 0)),
                     pl.BlockSpec((chunk,w), lambda j, d=d: (nat[d](j), 1 + d)),
                     pl.BlockSpec((chunk,w), lambda j, d=d: (nat[d](j), 3)),
                     pl.BlockSpec((1, 2, w), lambda j, d=d: (d, 0, 0))]
        out_specs += [pl.BlockSpec((chunk,w), lambda j, d=d: (nat[d](j), 0)),
                      pl.BlockSpec((1, n_heads, HG_DK, HG_DK), lambda j: (j, 0, 0, 0))]
        out_shape += [jax.ShapeDtypeStruct((r, w), F32),
                      jax.ShapeDtypeStruct((n_chunks, n_heads, HG_DK, HG_DK), F32)]
    (o_f, s_f, o_b, s_b), rode = _call(
        body, (u, u, u, lb_full, u, u, u, lb_full), name="hg_scan_fwd", grid=(n_chunks,), in_specs=in_specs,
        out_specs=out_specs, out_shape=out_shape, scratch_shapes=[pltpu.VMEM((n_heads, HG_DK, HG_DK), F32)] * 2,
        sem=("arbitrary",), rider=rider)
    return (o_f, o_b), (s_f, s_b), rode


def _hg_scan_bwd(u, lb_full, saved, d_o, w, n_lat, n_ctx, chunk, rider=None):
    r = u.shape[0]
    n_heads = w // HG_DK
    n_chunks = n_lat + n_ctx
    step = lambda jj: n_chunks - 1 - jj
    nat = [(lambda jj, o=_scan_order(n_lat, n_ctx, rev): o(step(jj))) for rev in DIRS]

    def body(*refs):
        ins, outs, scratch = refs[:12], refs[12:20], refs[20:]
        jj = pl.program_id(0)

        @pl.when(jj == 0)
        def _():
            for d in range(2):
                scratch[d][...] = jnp.zeros_like(scratch[d])
                outs[4 * d + 3][...] = jnp.zeros_like(outs[4 * d + 3])

        results = []
        for d, rev in enumerate(DIRS):
            aq, af, ai, lb_ref, save_ref, do_ref = ins[6 * d:6 * d + 6]
            f = lambda st, a, b, c, l0, l1, rev=rev: _hg_chunk(st, a, b, c, l0, l1, rev)
            _, vjp = jax.vjp(f, [save_ref[0, h] for h in range(n_heads)], aq[...], af[...], ai[...],
                             lb_ref[0, 0:1, :], lb_ref[0, 1:2, :])
            d_out = do_ref[...] * (nat[d](jj) < n_lat).astype(F32)
            results.append(vjp(([scratch[d][h] for h in range(n_heads)], d_out)))
        for d, (dst, daq, daf, dai, dl0, dl1) in enumerate(results):
            daq_ref, daf_ref, dai_ref, dlb_ref = outs[4 * d:4 * d + 4]
            for h in range(n_heads):
                scratch[d][h] = dst[h]
            daq_ref[...] = daq.astype(BF16)
            daf_ref[...] = daf.astype(BF16)
            dai_ref[...] = dai.astype(BF16)
            dlb_ref[0:1, :] += dl0
            dlb_ref[1:2, :] += dl1

    in_specs, out_specs, out_shape, operands = [], [], [], []
    for d in range(2):
        row = lambda jj, d=d: (nat[d](jj), 0)
        in_specs += [pl.BlockSpec((chunk,w), row),
                     pl.BlockSpec((chunk,w), lambda jj, d=d: (nat[d](jj), 1 + d)),
                     pl.BlockSpec((chunk,w), lambda jj, d=d: (nat[d](jj), 3)),
                     pl.BlockSpec((1, 2, w), lambda jj, d=d: (d, 0, 0)),
                     pl.BlockSpec((1, n_heads, HG_DK, HG_DK), lambda jj: (step(jj), 0, 0, 0)),
                     pl.BlockSpec((chunk,w), lambda jj, d=d: (jnp.minimum(nat[d](jj), n_lat - 1), 0))]
        operands += [u, u, u, lb_full, saved[d], d_o]
        out_specs += [pl.BlockSpec((chunk,w), row)] * 3 + [pl.BlockSpec((2, w), lambda jj: (0, 0))]
        out_shape += [jax.ShapeDtypeStruct((r, w), BF16)] * 3 + [jax.ShapeDtypeStruct((2, w), F32)]
    res, rode = _call(
        body, operands, name="hg_scan_bwd", grid=(n_chunks,), in_specs=in_specs, out_specs=out_specs,
        out_shape=out_shape, scratch_shapes=[pltpu.VMEM((n_heads, HG_DK, HG_DK), F32)] * 2,
        sem=("arbitrary",), rider=rider)
    return res[0:4], res[4:8], rode


def _ml_state_shapes(n_chunks, n_heads, dh):
    return (jax.ShapeDtypeStruct((n_chunks, n_heads, dh, dh), F32),
            jax.ShapeDtypeStruct((n_chunks, n_heads, 1, dh), F32),
            jax.ShapeDtypeStruct((n_chunks, n_heads, 1, LANE), F32))


def _ml_state_specs(n_heads, dh, index):
    return (pl.BlockSpec((1, n_heads, dh, dh), lambda j: (index(j), 0, 0, 0)),
            pl.BlockSpec((1, n_heads, 1, dh), lambda j: (index(j), 0, 0, 0)),
            pl.BlockSpec((1, n_heads, 1, LANE), lambda j: (index(j), 0, 0, 0)))


def _ml_state_scratch(n_heads, dh):
    return [pltpu.VMEM((n_heads, dh, dh), F32), pltpu.VMEM((n_heads, 1, dh), F32), pltpu.VMEM((n_heads, 1, LANE), F32)]


def _ml_scan_fwd(qk, u, gate_b, w, n_heads, n_lat, n_ctx, chunk):
    r = u.shape[0]
    dh = w // n_heads
    n_chunks = n_lat + n_ctx
    nat = [_scan_order(n_lat, n_ctx, rev) for rev in DIRS]

    def body(*refs):
        ins, outs, scratch = refs[:10], refs[10:18], refs[18:]

        @pl.when(pl.program_id(0) == 0)
        def _():
            for s_ref in scratch:
                s_ref[...] = jnp.zeros_like(s_ref)

        results = []
        for d, rev in enumerate(DIRS):
            q, k, v, g, gb = ins[5 * d:5 * d + 5]
            state = tuple([ref[h] for h in range(n_heads)] for ref in scratch[3 * d:3 * d + 3])
            results.append((state, _ml_chunk(state, q[...], k[...], v[...], g[...], gb[...], rev, d)))
        for d, (state, (new, o)) in enumerate(results):
            outs[4 * d][...] = o
            for part in range(3):
                for h in range(n_heads):
                    outs[4 * d + 1 + part][0, h] = state[part][h]
                    scratch[3 * d + part][h] = new[part][h]

    in_specs, out_specs, out_shape = [], [], []
    for d in range(2):
        in_specs += [pl.BlockSpec((chunk,w), lambda j, d=d: (nat[d](j), 0)),
                     pl.BlockSpec((chunk,w), lambda j, d=d: (nat[d](j), 1)),
                     pl.BlockSpec((chunk,w), lambda j, d=d: (nat[d](j), 7)),
                     pl.BlockSpec((chunk,LANE), lambda j, d=d: (nat[d](j), 10 * w // LANE)),
                     pl.BlockSpec((1, LANE), lambda j: (0, 0))]
        out_specs += [pl.BlockSpec((chunk,w), lambda j, d=d: (nat[d](j), 0))]
        out_specs += list(_ml_state_specs(n_heads, dh, lambda j: j))
        out_shape += [jax.ShapeDtypeStruct((r, w), F32)] + list(_ml_state_shapes(n_chunks, n_heads, dh))
    res = pl.pallas_call(
        body, name="ml_scan_fwd", grid=(n_chunks,), in_specs=in_specs, out_specs=tuple(out_specs),
        out_shape=tuple(out_shape), scratch_shapes=_ml_state_scratch(n_heads, dh) * 2,
        compiler_params=_params(("arbitrary",)),
    )(qk, qk, u, u, gate_b, qk, qk, u, u, gate_b)
    return (res[0], res[4]), (res[1:4], res[5:8])


def _ml_scan_bwd(qk, u, gate_b, saved, d_h, w, n_heads, n_lat, n_ctx, chunk, rider=None):
    r = u.shape[0]
    dh = w // n_heads
    n_chunks = n_lat + n_ctx
    step = lambda jj: n_chunks - 1 - jj
    nat = [(lambda jj, o=_scan_order(n_lat, n_ctx, rev): o(step(jj))) for rev in DIRS]

    def body(*refs):
        ins, outs, scratch = refs[:18], refs[18:26], refs[26:]
        jj = pl.program_id(0)

        @pl.when(jj == 0)
        def _():
            for s_ref in scratch:
                s_ref[...] = jnp.zeros_like(s_ref)
            for d in range(2):
                outs[4 * d + 3][...] = jnp.zeros_like(outs[4 * d + 3])

        results = []
        for d, rev in enumerate(DIRS):
            q, k, v, g, gb, sc, sn, sm, dh_ref = ins[9 * d:9 * d + 9]
            state = tuple([ref[0, h] for h in range(n_heads)] for ref in (sc, sn, sm))
            f = lambda st, a, b, c, gg, bb, rev=rev, d=d: _ml_chunk(st, a, b, c, gg, bb, rev, d)
            _, vjp = jax.vjp(f, state, q[...], k[...], v[...], g[...], gb[...])
            d_state = tuple([ref[h] for h in range(n_heads)] for ref in scratch[3 * d:3 * d + 3])
            d_out = dh_ref[...] * (nat[d](jj) < n_lat).astype(F32)
            results.append(vjp((d_state, d_out)))
        for d, (d_state, dq, dk, dv, dg, dgb) in enumerate(results):
            dqk_ref, dv_ref, dg_ref, dgb_ref = outs[4 * d:4 * d + 4]
            for part in range(3):
                for h in range(n_heads):
                    scratch[3 * d + part][h] = d_state[part][h]
            dqk_ref[:, 0:w] = dq
            dqk_ref[:, w:2 * w] = dk
            dv_ref[...] = dv.astype(BF16)
            dg_ref[...] = dg
            dgb_ref[...] += dgb

    in_specs, out_specs, out_shape, operands = [], [], [], []
    for d in range(2):
        row = lambda jj, d=d: (nat[d](jj), 0)
        in_specs += [pl.BlockSpec((chunk,w), row), pl.BlockSpec((chunk,w), lambda jj, d=d: (nat[d](jj), 1)),
                     pl.BlockSpec((chunk,w), lambda jj, d=d: (nat[d](jj), 7)),
                     pl.BlockSpec((chunk,LANE), lambda jj, d=d: (nat[d](jj), 10 * w // LANE)),
                     pl.BlockSpec((1, LANE), lambda jj: (0, 0))]
        in_specs += list(_ml_state_specs(n_heads, dh, step))
        in_specs += [pl.BlockSpec((chunk,w), lambda jj, d=d: (jnp.minimum(nat[d](jj), n_lat - 1), 0))]
        operands += [qk, qk, u, u, gate_b, *saved[d], d_h]
        out_specs += [pl.BlockSpec((chunk,2 * w), row), pl.BlockSpec((chunk,w), row),
                      pl.BlockSpec((chunk,LANE), row), pl.BlockSpec((1, LANE), lambda jj: (0, 0))]
        out_shape += [jax.ShapeDtypeStruct((r, 2 * w), F32), jax.ShapeDtypeStruct((r, w), BF16),
                      jax.ShapeDtypeStruct((r, LANE), F32), jax.ShapeDtypeStruct((1, LANE), F32)]
    res, rode = _call(
        body, operands, name="ml_scan_bwd", grid=(n_chunks,), in_specs=in_specs, out_specs=out_specs,
        out_shape=out_shape, scratch_shapes=_ml_state_scratch(n_heads, dh) * 2, sem=("arbitrary",), rider=rider)
    return res[0:4], res[4:8], rode


def _post_specs(w, tm, lat_tiles, cols):
    return [pl.BlockSpec((tm, w), (lambda i, cb=cb: (jnp.minimum(i, lat_tiles - 1), cb))) for cb in cols]


def _post_fwd(o_f, o_b, h_f, h_b, u, wa, wb, t_rows, w, n_hg, n_ml, tm):
    lat_tiles = t_rows // tm

    def body(of, ob, hf, hb, az, bo, bz, wa_ref, wb_ref, y_ref):
        y_ref[...] = _post_fn(of[...], ob[...], az[...], hf[...], hb[...], bo[...], bz[...],
                              wa_ref[...], wb_ref[...], n_hg, n_ml).astype(BF16)

    rows = pl.BlockSpec((tm, w), lambda i: (i, 0))
    vec = pl.BlockSpec((1, w), lambda i: (0, 0))
    return pl.pallas_call(
        body, name="post_fwd", grid=(lat_tiles,),
        in_specs=[rows] * 4 + _post_specs(w, tm, lat_tiles, (4, 8, 9)) + [vec, vec],
        out_specs=pl.BlockSpec((tm, 2 * w), lambda i: (i, 0)),
        out_shape=jax.ShapeDtypeStruct((t_rows, 2 * w), BF16),
        compiler_params=_params(("parallel",)),
    )(o_f, o_b, h_f, h_b, u, u, u, wa, wb)


def _post_bwd(o_f, o_b, h_f, h_b, u, wa, wb, dy, t_rows, w, n_hg, n_ml, tm, rider=None):
    r = u.shape[0]
    lat_tiles = t_rows // tm
    lat = lambda i: (jnp.minimum(i, lat_tiles - 1), 0)

    def body(of, ob, hf, hb, az, bo, bz, wa_ref, wb_ref, dy_ref, do_ref, dh_ref, daz_ref, dbo_ref, dbz_ref,
             dwa_ref, dwb_ref):
        i = pl.program_id(0)

        @pl.when(i == 0)
        def _():
            dwa_ref[...] = jnp.zeros_like(dwa_ref)
            dwb_ref[...] = jnp.zeros_like(dwb_ref)

        @pl.when(i < lat_tiles)
        def _():
            f = functools.partial(_post_fn, n_hg=n_hg, n_ml=n_ml)
            _, vjp = jax.vjp(f, of[...], ob[...], az[...], hf[...], hb[...], bo[...], bz[...], wa_ref[...], wb_ref[...])
            d_of, _, d_az, d_hf, _, d_bo, d_bz, d_wa, d_wb = vjp(dy_ref[...])
            do_ref[...] = d_of
            dh_ref[...] = d_hf
            daz_ref[...] = d_az.astype(BF16)
            dbo_ref[...] = d_bo.astype(BF16)
            dbz_ref[...] = d_bz.astype(BF16)
            dwa_ref[...] += d_wa
            dwb_ref[...] += d_wb

        @pl.when(i >= lat_tiles)
        def _():
            daz_ref[...] = jnp.zeros_like(daz_ref)
            dbo_ref[...] = jnp.zeros_like(dbo_ref)
            dbz_ref[...] = jnp.zeros_like(dbz_ref)

    lat_rows = pl.BlockSpec((tm, w), lat)
    all_rows = pl.BlockSpec((tm, w), lambda i: (i, 0))
    vec = pl.BlockSpec((1, w), lambda i: (0, 0))
    sd_t = jax.ShapeDtypeStruct((t_rows, w), F32)
    sd_r = jax.ShapeDtypeStruct((r, w), BF16)
    sd_v = jax.ShapeDtypeStruct((1, w), F32)
    return _call(
        body, (o_f, o_b, h_f, h_b, u, u, u, wa, wb, dy), name="post_bwd", grid=(r // tm,),
        in_specs=[lat_rows] * 4 + _post_specs(w, tm, lat_tiles, (4, 8, 9)) + [vec, vec]
        + [pl.BlockSpec((tm, 2 * w), lat)],
        out_specs=(lat_rows, lat_rows, all_rows, all_rows, all_rows, vec, vec),
        out_shape=(sd_t, sd_t, sd_r, sd_r, sd_r, sd_v, sd_v), sem=("arbitrary",), rider=rider)


OUT_ROW_GATE, OUT_ROW_LN_G, OUT_ROW_LN_B, OUT_ROW_LOSS = 0, 1, 2, 3


def _out_block(y, w_out, x, target, prm, tm):
    t_rows, dm = x.shape
    di = y.shape[1]

    def body(y_ref, w_ref, x_ref, t_ref, p_ref, dz_ref, dy_ref, gx_ref, acc_ref):
        @pl.when(pl.program_id(0) == 0)
        def _():
            acc_ref[...] = jnp.zeros_like(acc_ref)

        gate, ln_g, ln_b = p_ref[0:1, :], p_ref[1:2, :], p_ref[2:3, :]
        z = _nn(y_ref[...], w_ref[...])
        res = ALPHA * x_ref[...] + gate * z
        mu = jnp.mean(res, axis=-1, keepdims=True)
        rc = res - mu
        rstd = lax.rsqrt(jnp.mean(rc * rc, axis=-1, keepdims=True) + LN_EPS)
        rn = rc * rstd
        err = rn * ln_g + ln_b - t_ref[...]
        d_out = err * (1.0 / dm)
        d_rn = d_out * ln_g
        d_res = rstd * (d_rn - jnp.mean(d_rn, axis=-1, keepdims=True)
                        - rn * jnp.mean(d_rn * rn, axis=-1, keepdims=True))
        acc_ref[OUT_ROW_GATE:OUT_ROW_GATE + 1, :] += jnp.sum(d_res * z, axis=0, keepdims=True)
        acc_ref[OUT_ROW_LN_G:OUT_ROW_LN_G + 1, :] += jnp.sum(d_out * rn, axis=0, keepdims=True)
        acc_ref[OUT_ROW_LN_B:OUT_ROW_LN_B + 1, :] += jnp.sum(d_out, axis=0, keepdims=True)
        acc_ref[OUT_ROW_LOSS:OUT_ROW_LOSS + 1, :] += (0.5 / dm) * jnp.sum(err * err, axis=0, keepdims=True)
        gx_ref[...] = ALPHA * d_res
        dz = (d_res * gate).astype(BF16)
        dz_ref[...] = dz
        dy_ref[...] = _nt(dz, w_ref[...])

    rows_d = pl.BlockSpec((tm, dm), lambda i: (i, 0))
    rows_i = pl.BlockSpec((tm, di), lambda i: (i, 0))
    return pl.pallas_call(
        body, name="out_block", grid=(t_rows // tm,),
        in_specs=[rows_i, pl.BlockSpec((di, dm), lambda i: (0, 0)), rows_d, rows_d,
                  pl.BlockSpec((8, dm), lambda i: (0, 0))],
        out_specs=(rows_d, rows_i, rows_d, pl.BlockSpec((8, dm), lambda i: (0, 0))),
        out_shape=(jax.ShapeDtypeStruct((t_rows, dm), BF16), jax.ShapeDtypeStruct((t_rows, di), F32),
                   jax.ShapeDtypeStruct((t_rows, dm), F32), jax.ShapeDtypeStruct((8, dm), F32)),
        compiler_params=_params(("arbitrary",)),
    )(y, w_out, x, target, prm)


def _mod_fwd(c16, w_mod, tn):
    dm, n = w_mod.shape

    def body(c_ref, w_ref, o_ref, a_ref):
        a = _silu(c_ref[...])
        a_ref[...] = a
        o_ref[...] = _nn(a, w_ref[...], HIGHEST)

    return pl.pallas_call(
        body, name="mod_fwd", grid=(n // tn,),
        in_specs=[pl.BlockSpec((16, dm), lambda j: (0, 0)), pl.BlockSpec((dm, tn), lambda j: (0, j))],
        out_specs=(pl.BlockSpec((16, tn), lambda j: (0, j)), pl.BlockSpec((16, dm), lambda j: (0, 0))),
        out_shape=(jax.ShapeDtypeStruct((16, n), F32), jax.ShapeDtypeStruct((16, dm), F32)),
        compiler_params=_params(("arbitrary",)),
    )(c16, w_mod)


def _mod_bwd(a16, dm16, w_mod, tn, rider=None):
    dm, n = w_mod.shape

    def body(a_ref, d_ref, w_ref, dw_ref, dc_ref):
        @pl.when(pl.program_id(0) == 0)
        def _():
            dc_ref[...] = jnp.zeros_like(dc_ref)
        dw_ref[...] = _tn(a_ref[...], d_ref[...], HIGHEST)
        dc_ref[...] += _nt(d_ref[...], w_ref[...], HIGHEST)

    return _call(
        body, (a16, dm16, w_mod), name="mod_bwd", grid=(n // tn,),
        in_specs=[pl.BlockSpec((16, dm), lambda j: (0, 0)), pl.BlockSpec((16, tn), lambda j: (0, j)),
                  pl.BlockSpec((dm, tn), lambda j: (0, j))],
        out_specs=(pl.BlockSpec((dm, tn), lambda j: (0, j)), pl.BlockSpec((16, dm), lambda j: (0, 0))),
        out_shape=(jax.ShapeDtypeStruct((dm, n), F32), jax.ShapeDtypeStruct((16, dm), F32)),
        sem=("arbitrary",), rider=rider)


def _sum_devices(g, fold_rows):
    n_dev, rows, n = g.shape

    def body(g_ref, s_ref, t_ref):
        s = g_ref[0]
        for dev in range(1, n_dev):
            s = s + g_ref[dev]
        t_ref[...] = jnp.broadcast_to(jnp.sum(s, axis=-1, keepdims=True), (rows, LANE))
        s_ref[...] = s
        s_ref[0:fold_rows, :] = s[0:fold_rows] + s[fold_rows:2 * fold_rows]

    return pl.pallas_call(
        body, name="sum_devices",
        out_shape=(jax.ShapeDtypeStruct((rows, n), F32), jax.ShapeDtypeStruct((rows, LANE), F32)),
        compiler_params=_params(),
    )(g)


def _c_ctx_grad(parts, c_ctx_row):
    def body(p_ref, c_ref, o_ref):
        s = p_ref[0]
        for chip in range(1, N_CHIPS):
            s = s + p_ref[2 * chip]
        cv = c_ref[...]
        sg = _sigmoid(cv)
        o_ref[...] = s * (sg * (1.0 + cv * (1.0 - sg)))

    return pl.pallas_call(
        body, name="c_ctx_grad", out_shape=jax.ShapeDtypeStruct(parts.shape[1:], F32), compiler_params=_params(),
    )(parts, c_ctx_row)


def _sum_pair(name, mine, got):
    def body(a_ref, b_ref, o_ref):
        o_ref[...] = (a_ref[...] + b_ref[...]).astype(BF16)

    k, rows, n = mine.shape
    tl = _largest_divisor(n, max(LANE, (1 << 18) // rows), LANE)
    spec = pl.BlockSpec((1, rows, tl), lambda kk, i: (kk, 0, i))
    return pl.pallas_call(
        body, name=name, grid=(k, n // tl), in_specs=[spec, spec], out_specs=spec,
        out_shape=jax.ShapeDtypeStruct(mine.shape, BF16), compiler_params=_params(("parallel", "parallel")),
    )(mine, got)


def _sum_pair_lanes(name, full, got, ci):
    rows, n = got.shape
    tr = _largest_divisor(rows, max(SUBLANE_BF16, (1 << 19) // n), SUBLANE_BF16)

    def body(ci_ref, a_ref, b_ref, o_ref):
        o_ref[...] = (a_ref[...] + b_ref[...].astype(F32)).astype(BF16)

    return pl.pallas_call(
        body, name=name,
        grid_spec=pltpu.PrefetchScalarGridSpec(
            num_scalar_prefetch=1, grid=(rows // tr,),
            in_specs=[pl.BlockSpec((tr, n), lambda i, c: (i, c[0])), pl.BlockSpec((tr, n), lambda i, c: (i, 0))],
            out_specs=pl.BlockSpec((tr, n), lambda i, c: (i, 0))),
        out_shape=jax.ShapeDtypeStruct((rows, n), BF16), compiler_params=_params(("parallel",)),
    )(ci.reshape(1).astype(jnp.int32), full, got)


def _sum_chips(name, got, own, chip):
    k, rows, n = got.shape
    tl = _largest_divisor(n, max(LANE, (1 << 18) // rows), LANE)

    def body(chip_ref, g_ref, own_ref, o_ref):
        total = None
        for kk in range(k):
            term = jnp.where(chip_ref[0] == kk, own_ref[0], g_ref[kk]).astype(F32)
            total = term if total is None else total + term
        o_ref[...] = total

    return pl.pallas_call(
        body, name=name,
        grid_spec=pltpu.PrefetchScalarGridSpec(
            num_scalar_prefetch=1, grid=(n // tl,),
            in_specs=[pl.BlockSpec((k, rows, tl), lambda i, c: (0, 0, i)),
                      pl.BlockSpec((1, rows, tl), lambda i, c: (c[0], 0, i))],
            out_specs=pl.BlockSpec((rows, tl), lambda i, c: (0, i))),
        out_shape=jax.ShapeDtypeStruct((rows, n), F32), compiler_params=_params(("parallel",)),
    )(chip.reshape(1).astype(jnp.int32), got, own)


def _adamw_update(w, g, m, v):
    m2 = ADAM_B1 * m + (1.0 - ADAM_B1) * g
    v2 = ADAM_B2 * v + (1.0 - ADAM_B2) * jnp.square(g)
    m_hat = m2 / (1.0 - ADAM_B1 ** ADAM_STEP)
    v_hat = v2 / (1.0 - ADAM_B2 ** ADAM_STEP)
    return -ADAM_LR * (m_hat / (jnp.sqrt(v_hat) + ADAM_EPS) + ADAM_WD * w), m2, v2


def _adamw(name, w, g, m, v, rider=None):
    rows, n = w.shape
    if rows % 8 == 0:
        tr = _largest_divisor(rows, max(8, (1 << 18) // n), 8)
        block, index, steps = (tr, n), (lambda i: (i, 0)), rows // tr
    else:
        tl = _largest_divisor(n, max(LANE, (1 << 18) // rows), LANE)
        block, index, steps = (rows, tl), (lambda i: (0, i)), n // tl

    def body(w_ref, g_ref, m_ref, v_ref, d_ref, mo_ref, vo_ref):
        d_ref[...], mo_ref[...], vo_ref[...] = _adamw_update(w_ref[...], g_ref[...], m_ref[...], v_ref[...])

    spec = pl.BlockSpec(block, index)
    sds = jax.ShapeDtypeStruct((rows, n), F32)
    return _call(body, (w, g, m, v), name=name, grid=(steps,), in_specs=[spec] * 4, out_specs=(spec,) * 3,
                 out_shape=(sds, sds, sds), sem=("parallel",), rider=rider)


PACK_LANES = 1024


def _pack(pieces):
    flat = jnp.concatenate([p.reshape(-1) for p in pieces])
    total = -(-flat.shape[0] // (8 * PACK_LANES)) * 8 * PACK_LANES
    return jnp.pad(flat, (0, total - flat.shape[0])).reshape(-1, PACK_LANES)


def _unpack(packed, shapes):
    flat = packed.reshape(-1)
    out, off = [], 0
    for shp in shapes:
        size = math.prod(shp)
        out.append(flat[off:off + size].reshape(shp))
        off += size
    return out


def _rows8(rows, width):
    flat = [r.reshape(width) for r in rows] + [jnp.zeros(((8 - len(rows)) * width,), F32)]
    return jnp.concatenate(flat).reshape(8, width)


def kernel(x, c, ctx, c_ctx, w_mod, b_mod, w_in, conv_w, conv_b, hg_lb, ml_gate_b, hg_norm_w, ml_norm_w, w_out, ln_g, ln_b, loss_target, m_c_ctx, m_w_mod, m_b_mod, m_w_in, m_conv_w, m_conv_b, m_hg_lb, m_ml_gate_b, m_hg_norm_w, m_ml_norm_w, m_w_out, m_ln_g, m_ln_b, v_c_ctx, v_w_mod, v_b_mod, v_w_in, v_conv_w, v_conv_b, v_hg_lb, v_ml_gate_b, v_hg_norm_w, v_ml_norm_w, v_w_out, v_ln_g, v_ln_b):
    t_rows, dm = x.shape[1], x.shape[2]
    c_rows = ctx.shape[1]
    w = hg_norm_w.shape[1]
    n_ml = ml_gate_b.shape[-1]
    n_hg = w // HG_DK
    di = 2 * w
    n_in = 10 * w + 4 * n_ml
    ns = w_in.shape[2]
    nm = w_mod.shape[2]
    n_pad = 10 * w + LANE
    r_rows = t_rows + c_rows
    row_gcd = math.gcd(t_rows, c_rows)
    hg_chunk, ml_chunk = math.gcd(HG_CHUNK, row_gcd), math.gcd(ML_CHUNK, row_gcd)
    hg_counts = (t_rows // hg_chunk, c_rows // hg_chunk, hg_chunk)
    ml_counts = (t_rows // ml_chunk, c_rows // ml_chunk, ml_chunk)
    assert ml_norm_w.shape[1] == w and di == dm and N_CHIPS * ns == n_in and N_CHIPS * nm == 3 * dm
    assert w_out.shape[1] * N_CHIPS == di and 4 * n_ml <= LANE and t_rows % GRID_W == 0

    xi, yi, ci = lax.axis_index("x"), lax.axis_index("y"), lax.axis_index("c")
    chip = 2 * xi + yi
    dev = 4 * xi + 2 * yi + ci

    tm = _largest_divisor(math.gcd(t_rows, c_rows), 256, 8)
    tm_mm = _largest_divisor(r_rows, 1088, SUBLANE_BF16)
    tn_mm = LANE * _largest_divisor(n_pad // LANE, 9)
    tn_mod = _largest_divisor(nm, 512, LANE)

    shard_shapes = [(dm,), (2, 2, w // N_CHIPS), (3, 3, di // N_CHIPS)]
    g1 = _all_gather8(_pack([c, hg_lb, conv_w])).run("gather_inputs")[0]
    per_dev = [_unpack(g1[i], shard_shapes) for i in range(N_DEV)]
    c_all = jnp.stack([p[0] for p in per_dev])
    lb_full = jnp.concatenate([per_dev[2 * k][1] for k in range(N_CHIPS)], axis=-1)
    conv_w9 = jnp.concatenate([per_dev[2 * k][2] for k in range(N_CHIPS)], axis=-1).reshape(9, di)

    c16 = jnp.concatenate([c_all, c_ctx[None], jnp.zeros((16 - N_DEV - 1, dm), F32)])
    mod_part, a16 = _mod_fwd(c16, w_mod[0], tn_mod)
    g2 = _all_gather8(mod_part).run("gather_mod")[0]
    mod_all = jnp.concatenate([g2[2 * k] for k in range(N_CHIPS)], axis=1) + b_mod
    mod_x = lax.dynamic_index_in_dim(mod_all, dev, 0, keepdims=False).reshape(3, dm)
    mod_c = mod_all[N_DEV].reshape(3, dm)
    prm = jnp.stack([_rows8(list(mod_x), dm), _rows8(list(mod_c), dm)])

    as_t = lambda a: jnp.transpose(a[0])
    half_in = lax.dynamic_slice_in_dim(as_t(w_in).astype(BF16), ci * (dm // 2), dm // 2, 1)
    half_out = lax.dynamic_slice_in_dim(w_out[0].astype(BF16), ci * (di // (2 * N_CHIPS)), di // (2 * N_CHIPS), 0)
    lanes_of = lambda core: pl.ds(core * (dm // 2), dm // 2)
    landing = lambda s, r: (_chip_of(s), slice(None), lanes_of(s[2]))
    own_placed = lax.dynamic_update_slice(jnp.zeros((N_CHIPS + 1, ns, dm), BF16),
                                          as_t(w_in).astype(BF16)[None], (chip, 0, 0))
    gather_in = _Exchange([half_in, own_placed], [jax.ShapeDtypeStruct(own_placed.shape, BF16)],
                          [(mask, 0, None, 0, landing) for mask in CHIP_MASKS], in_place={1: 0})

    hc, (gw_in,) = _modulate_fwd(x[0], ctx[0], prm, tm, rider=gather_in)
    my_lanes = lambda s, r: (slice(0, N_CHIPS), slice(None), lanes_of(s[2]))
    gw_in = _Exchange([gw_in], [jax.ShapeDtypeStruct(gw_in.shape, BF16)],
                      [(SIBLING_MASK, 0, my_lanes, 0, my_lanes)], in_place={0: 0}).run("gather_w_in_pair")[0]
    wt_full = gw_in.reshape((N_CHIPS + 1) * ns, dm)
    assert wt_full.shape[0] >= n_pad
    u, (got_out,) = _mm_nt("in_proj", hc, wt_full, n_pad, tm_mm, tn_mm, F32, rider=_all_gather_chips([half_out]))
    fetched_out = _own_block(chip, half_out, got_out)
    (o_f, o_b), hg_saved, (swapped_out,) = _hg_scan_fwd(u, lb_full, w, *hg_counts,
                                                         rider=_sibling_swap([fetched_out]))
    w_out_full = _join_halves(ci, fetched_out, swapped_out, 1).reshape(di, dm)
    qk = _conv_fwd(u, conv_w9, conv_b, t_rows, c_rows, w, LANE)
    gate_b_row = jnp.pad(ml_gate_b.reshape(1, -1), ((0, 0), (0, LANE - 4 * n_ml)))
    (h_f, h_b), ml_saved = _ml_scan_fwd(qk, u, gate_b_row, w, n_ml, *ml_counts)
    y = _post_fwd(o_f, o_b, h_f, h_b, u, hg_norm_w, ml_norm_w, t_rows, w, n_hg, n_ml, tm)
    prm_out = _rows8([mod_x[2], ln_g, ln_b], dm)
    dz, dy, gx_direct, acc_out = _out_block(y, w_out_full, x[0], loss_target[0], prm_out, tm)

    d_w_out = _mm_tn("d_w_out", y, dz, _largest_divisor(di, 1024, LANE),
                     _largest_divisor(t_rows, 1024, SUBLANE_BF16))
    d_w_out4 = d_w_out.reshape(N_CHIPS, 2, di // (2 * N_CHIPS), dm)
    mine_out = lax.dynamic_index_in_dim(d_w_out4, ci, 1, keepdims=False)
    other_out = lax.dynamic_index_in_dim(d_w_out4, 1 - ci, 1, keepdims=False)
    (d_o, d_h, d_az, d_bo, d_bz, d_wa, d_wb), (got_out,) = _post_bwd(
        o_f, o_b, h_f, h_b, u, hg_norm_w, ml_norm_w, dy, t_rows, w, n_hg, n_ml, tm, rider=_sibling_swap([other_out]))
    pair_out = _sum_pair("rs_pair_sum_w_out", mine_out, got_out)
    (d_aq_f, d_aff, d_ai_f, d_lb_f), (d_aq_b, d_afb, d_ai_b, d_lb_b), (landed_out,) = _hg_scan_bwd(
        u, lb_full, hg_saved, d_o, w, *hg_counts, rider=_chip_scatter([pair_out]))
    half_g_out = _sum_chips("rs_chip_sum_w_out", landed_out, pair_out, chip)
    (d_qk_f, d_v_f, d_g_f, d_gb_f), (d_qk_b, d_v_b, d_g_b, d_gb_b), (sibling_out,) = _ml_scan_bwd(
        qk, u, gate_b_row, ml_saved, d_h, w, n_ml, *ml_counts, rider=_sibling_swap([half_g_out]))
    g_w_out = _join_halves(ci, half_g_out, sibling_out, 0)
    d_bqk, d_cw, d_cb = _conv_bwd(u, (d_qk_f, d_qk_b), conv_w9, conv_b, t_rows, c_rows, w, LANE)
    du = _assemble_du([(d_aq_f, d_aq_b), d_aff, d_afb, (d_ai_f, d_ai_b), d_az, d_bqk, (d_v_f, d_v_b), d_bo, d_bz],
                      (d_g_f, d_g_b), n_pad, tm // 2)
    d_wt_in, d_wt_in_bf16 = _mm_tn("d_w_in", du, hc, tn_mm, tm_mm, with_bf16=True)

    delta, new_m, new_v = {}, {}, {}
    res, (got_in,) = _adamw(
        "adamw_w_out", w_out[0], g_w_out, m_w_out[0], v_w_out[0],
        rider=_Exchange([d_wt_in_bf16], [jax.ShapeDtypeStruct((n_pad, dm // 2), BF16)],
                        [(SIBLING_MASK, 0, lambda s, r: (slice(None), lanes_of(r[2])), 0, None)]))
    delta["w_out"], new_m["w_out"], new_v["w_out"] = (a[None] for a in res)
    pair_half = _sum_pair_lanes("rs_pair_sum_w_in", d_wt_in, got_in, ci)
    pair_in = jnp.stack([pair_half[k * ns:(k + 1) * ns] for k in range(N_CHIPS)])
    d_hc, (landed_in,) = _mm_acc("d_h", du, wt_full, tm_mm, tn_mm, rider=_chip_scatter([pair_in]))
    half_g_in = _sum_chips("rs_chip_sum_w_in", landed_in, pair_in, chip)
    (gx, acc_mod), _ = _modulate_bwd(x[0], ctx[0], d_hc, prm, gx_direct, tm)
    grad_x = gx[None]

    zero_row = jnp.zeros((dm,), F32)
    d_gb = jnp.concatenate([d_gb_f[:, 0:n_ml], d_gb_b[:, n_ml:2 * n_ml], d_gb_f[:, 2 * n_ml:3 * n_ml],
                            d_gb_b[:, 3 * n_ml:4 * n_ml], jnp.zeros((1, dm - 4 * n_ml), F32)], axis=1)
    rows = [acc_mod[0, 0], acc_mod[0, 1], acc_out[OUT_ROW_GATE],
            acc_mod[1, 0], acc_mod[1, 1], zero_row]
    rows += list(d_cw) + [d_cb[0], d_lb_f.reshape(dm), d_lb_b.reshape(dm),
                          jnp.concatenate([d_wa[0], d_wb[0]]), acc_out[OUT_ROW_LN_G], acc_out[OUT_ROW_LN_B],
                          acc_out[OUT_ROW_LOSS], d_gb[0], zero_row]
    ROW_CW, ROW_CB, ROW_LB, ROW_NORM, ROW_LN_G, ROW_LN_B, ROW_LOSS, ROW_GB = 6, 15, 16, 18, 19, 20, 21, 22
    small_rows = jnp.concatenate([r.reshape(dm) for r in rows]).reshape(len(rows), dm)
    g3 = _all_gather8(small_rows).run("gather_small_grads")[0]
    sums, totals = _sum_devices(g3, 3)
    loss = totals[ROW_LOSS, 0]
    dm16 = jnp.concatenate([g3[:, 0:3, :].reshape(N_DEV, 3 * dm), sums[3:6].reshape(1, 3 * dm),
                            jnp.zeros((16 - N_DEV - 1, 3 * dm), F32)])
    (g_w_mod, dc16), (sibling_g_in,) = _mod_bwd(a16, lax.dynamic_slice_in_dim(dm16, chip * nm, nm, 1), w_mod[0],
                                                tn_mod, rider=_sibling_swap([half_g_in]))
    g_wt_in = _join_halves(ci, half_g_in, sibling_g_in, 1)
    g4 = _all_gather8(jnp.pad(dc16[N_DEV:N_DEV + 1], ((0, 7), (0, 0)))).run("gather_c_ctx")[0]
    g_c_ctx = _c_ctx_grad(g4, jnp.broadcast_to(c_ctx[None], (8, dm)))[0]
    res, _ = _adamw("adamw_w_in", as_t(w_in), g_wt_in, as_t(m_w_in), as_t(v_w_in))
    delta["w_in"], new_m["w_in"], new_v["w_in"] = (jnp.transpose(a)[None] for a in res)
    res, _ = _adamw("adamw_w_mod", w_mod[0], g_w_mod, m_w_mod[0], v_w_mod[0])
    delta["w_mod"], new_m["w_mod"], new_v["w_mod"] = (a[None] for a in res)

    chip_cols = lambda a, width: lax.dynamic_slice_in_dim(a, chip * width, width, a.ndim - 1)
    grads = {
        "c_ctx": g_c_ctx,
        "w_mod": g_w_mod[None],
        "b_mod": sums[0:3].reshape(1, 3 * dm),
        "w_in": jnp.transpose(g_wt_in)[None],
        "conv_w": chip_cols(sums[ROW_CW:ROW_CW + 9].reshape(1, 3, 3, di), di // N_CHIPS),
        "conv_b": sums[ROW_CB][None],
        "hg_lb": chip_cols(sums[ROW_LB:ROW_LB + 2].reshape(2, 2, w), w // N_CHIPS),
        "ml_gate_b": sums[ROW_GB, 0:4 * n_ml].reshape(1, 4, n_ml),
        "hg_norm_w": sums[ROW_NORM, 0:w][None],
        "ml_norm_w": sums[ROW_NORM, w:2 * w][None],
        "w_out": g_w_out[None],
        "ln_g": sums[ROW_LN_G][None],
        "ln_b": sums[ROW_LN_B][None],
    }
    weights = dict(c_ctx=c_ctx, w_mod=w_mod, b_mod=b_mod, w_in=w_in, conv_w=conv_w, conv_b=conv_b, hg_lb=hg_lb,
                   ml_gate_b=ml_gate_b, hg_norm_w=hg_norm_w, ml_norm_w=ml_norm_w, w_out=w_out, ln_g=ln_g, ln_b=ln_b)
    mom1 = dict(c_ctx=m_c_ctx, w_mod=m_w_mod, b_mod=m_b_mod, w_in=m_w_in, conv_w=m_conv_w, conv_b=m_conv_b,
                hg_lb=m_hg_lb, ml_gate_b=m_ml_gate_b, hg_norm_w=m_hg_norm_w, ml_norm_w=m_ml_norm_w, w_out=m_w_out,
                ln_g=m_ln_g, ln_b=m_ln_b)
    mom2 = dict(c_ctx=v_c_ctx, w_mod=v_w_mod, b_mod=v_b_mod, w_in=v_w_in, conv_w=v_conv_w, conv_b=v_conv_b,
                hg_lb=v_hg_lb, ml_gate_b=v_ml_gate_b, hg_norm_w=v_hg_norm_w, ml_norm_w=v_ml_norm_w, w_out=v_w_out,
                ln_g=v_ln_g, ln_b=v_ln_b)
    names = list(weights)
    big = ("w_mod", "w_in", "w_out")
    small = [n for n in names if n not in big]

    small_shapes = [weights[n].shape for n in small]
    res, _ = _adamw("adamw_small", *(_pack([src[n] for n in small]) for src in (weights, grads, mom1, mom2)))
    for out, packed in zip((delta, new_m, new_v), res):
        for n, a in zip(small, _unpack(packed, small_shapes)):
            out[n] = a

    return (loss, grad_x, *[grads[n].reshape(weights[n].shape) for n in names], *[delta[n] for n in names],
            *[new_m[n] for n in names], *[new_v[n] for n in names])
```

```python
import functools
import math

import jax
import jax.numpy as jnp
from jax import lax
from jax.experimental import pallas as pl
from jax.experimental.pallas import tpu as pltpu

F32 = jnp.float32
BF16 = jnp.bfloat16
HIGHEST = lax.Precision.HIGHEST
MESH = pl.DeviceIdType.MESH

HG_CHUNK = 64
ML_CHUNK = 256
HG_CHUNKS_PER_STEP = 4
GRID_W = 64
HG_DK = 128
LANE = 128
SUBLANE_BF16 = 16
ALPHA = 2.0 ** 0.25
LN_EPS = 1e-5
NORM_EPS = 1e-6
ADAM_LR = 0.001
ADAM_B1 = 0.9
ADAM_B2 = 0.999
ADAM_EPS = 1e-08
ADAM_WD = 0.01
ADAM_STEP = 10
VMEM_LIMIT = 56 * 1024 * 1024
N_CHIPS = 4
N_DEV = 8


def _params(sem=None):
    return pltpu.CompilerParams(dimension_semantics=sem, vmem_limit_bytes=VMEM_LIMIT)


def _largest_divisor(n, cap, multiple=1):
    best = None
    for d in range(multiple, min(n, cap) + 1, multiple):
        if n % d == 0:
            best = d
    assert best is not None, (n, cap, multiple)
    return best


def _sigmoid(x):
    return jax.nn.sigmoid(x)


def _silu(x):
    return x * jax.nn.sigmoid(x)


def _dot(a, b, dims, precision=None):
    return lax.dot_general(a, b, (dims, ((), ())), precision=precision, preferred_element_type=F32)


def _nn(a, b, precision=None):
    return _dot(a, b, ((1,), (0,)), precision)


def _nt(a, b, precision=None):
    return _dot(a, b, ((1,), (1,)), precision)


def _tn(a, b, precision=None):
    return _dot(a, b, ((0,), (0,)), precision)


def _narrow(x):
    return x.astype(BF16)


@jax.custom_vjp
def _bnn(a, b):
    return _nn(_narrow(a), _narrow(b))


def _bnn_fwd(a, b):
    an, bn = _narrow(a), _narrow(b)
    return _nn(an, bn), (an, bn)


def _bnn_bwd(res, ct):
    an, bn = res
    ctn = _narrow(ct)
    return _nt(ctn, bn), _tn(an, ctn)


_bnn.defvjp(_bnn_fwd, _bnn_bwd)


@jax.custom_vjp
def _bnt(a, b):
    return _nt(_narrow(a), _narrow(b))


def _bnt_fwd(a, b):
    an, bn = _narrow(a), _narrow(b)
    return _nt(an, bn), (an, bn)


def _bnt_bwd(res, ct):
    an, bn = res
    ctn = _narrow(ct)
    return _nn(ctn, bn), _tn(ctn, an)


_bnt.defvjp(_bnt_fwd, _bnt_bwd)


@jax.custom_vjp
def _btn(a, b):
    return _tn(_narrow(a), _narrow(b))


def _btn_fwd(a, b):
    an, bn = _narrow(a), _narrow(b)
    return _tn(an, bn), (an, bn)


def _btn_bwd(res, ct):
    an, bn = res
    ctn = _narrow(ct)
    return _nt(bn, ctn), _nn(an, ctn)


_btn.defvjp(_btn_fwd, _btn_bwd)


def _visible(n, rev):
    r = lax.broadcasted_iota(jnp.int32, (n, n), 0)
    c = lax.broadcasted_iota(jnp.int32, (n, n), 1)
    return (r <= c) if rev else (r >= c)


def _mask_matmul(mask, x):
    mb = mask.astype(BF16)
    hi = x.astype(BF16)
    lo = (x - hi.astype(F32)).astype(BF16)
    return _nn(mb, hi) + _nn(mb, lo)


@functools.partial(jax.custom_vjp, nondiff_argnums=(1,))
def _cumulative(x, rev):
    return _mask_matmul(_visible(x.shape[0], rev), x)


def _cumulative_fwd(x, rev):
    return _cumulative(x, rev), None


def _cumulative_bwd(rev, _, ct):
    return (_mask_matmul(_visible(ct.shape[0], not rev), ct),)


_cumulative.defvjp(_cumulative_fwd, _cumulative_bwd)


def _hg_chunk(states, aq, af, ai, lb0, lb1, rev):
    n_heads = len(states)
    lb = _sigmoid(lb0 - lb1)
    f = lb + (1.0 - lb) * _sigmoid(af)
    g = jnp.log(f)
    k = 1.0 - f
    q = _silu(aq)
    chunk = aq.shape[0]
    vis = _visible(chunk, rev)
    b = _cumulative(g, rev)
    last = 0 if rev else chunk - 1
    b_end = b[last:last + 1]
    b_mid = b[chunk // 2:chunk // 2 + 1]
    q_inter = q * jnp.exp(b)
    q_intra = q * jnp.exp(b - b_mid)
    k_intra = k * jnp.exp(b_mid - b)
    k_dec = k * jnp.exp(b_end - b)
    e_end = jnp.exp(b_end)
    new_states, outs = [], []
    for h in range(n_heads):
        sl = slice(h * HG_DK, (h + 1) * HG_DK)
        s_t = states[h]
        scores = jnp.where(vis, _nt(q_intra[:, sl], k_intra[:, sl]), 0.0)
        outs.append(_nt(q_inter[:, sl], s_t) + _nn(scores, ai[:, sl]))
        new_states.append(e_end[:, sl] * s_t + _tn(ai[:, sl], k_dec[:, sl]))
    return new_states, jnp.concatenate(outs, axis=1)


def _ml_chunk(state, q, k, v, g, gb, rev, d):
    cms, nvs, mbs = state
    n_heads = len(cms)
    dh = q.shape[1] // n_heads
    ga = g + gb
    log_f_all = jax.nn.log_sigmoid(ga)
    chunk = q.shape[0]
    vis = _visible(chunk, rev)
    b_all = _cumulative(log_f_all, rev)
    last = 0 if rev else chunk - 1
    k = k * (dh ** -0.5)
    new_c, new_n, new_m, outs = [], [], [], []
    for h in range(n_heads):
        ci = d * n_heads + h
        cf = (2 + d) * n_heads + h
        sl = slice(h * dh, (h + 1) * dh)
        qh, kh, vh = q[:, sl], k[:, sl], v[:, sl]
        li = ga[:, ci:ci + 1]
        b = b_all[:, cf:cf + 1]
        m = mbs[h][:, 0:1]
        row = jnp.transpose(li - b)
        log_w = jnp.where(vis, b + row, -jnp.inf)
        m_inter = b + m
        m_t = jnp.maximum(m_inter, jnp.max(log_w, axis=-1, keepdims=True))
        w_inter = jnp.exp(m_inter - m_t)
        w_qk = jnp.exp(log_w - m_t) * _bnt(qh, kh)
        num = w_inter * _bnt(qh, cms[h]) + _bnn(w_qk, vh)
        den = w_inter * jnp.sum(qh * nvs[h], axis=-1, keepdims=True) + jnp.sum(w_qk, axis=-1, keepdims=True)
        outs.append(num / jnp.maximum(jnp.abs(den), jnp.exp(-m_t)))
        m_new = m_t[last:last + 1]
        b_end = b[last:last + 1]
        w_s = jnp.exp(b_end - b + li - m_new)
        decay = jnp.exp(b_end + m - m_new)
        new_c.append(decay * cms[h] + _btn(w_s * vh, kh))
        new_n.append(decay * nvs[h] + jnp.sum(w_s * kh, axis=0, keepdims=True))
        new_m.append(jnp.broadcast_to(m_new, (1, LANE)))
    return (new_c, new_n, new_m), jnp.concatenate(outs, axis=1)


def _post_fn(o_f, o_b, az, h_f, h_b, bo, bz, wa, wb, n_hg, n_ml):
    o = o_f + o_b
    parts = []
    for h in range(n_hg):
        s = o[:, h * HG_DK:(h + 1) * HG_DK]
        parts.append(s * lax.rsqrt(jnp.mean(s * s, axis=-1, keepdims=True) + NORM_EPS))
    y_a = jnp.concatenate(parts, axis=1) * wa * _silu(az)
    hh = h_f + h_b
    dh = hh.shape[1] // n_ml
    parts = []
    for h in range(n_ml):
        s = hh[:, h * dh:(h + 1) * dh]
        mu = jnp.mean(s, axis=-1, keepdims=True)
        sc = s - mu
        parts.append(sc * lax.rsqrt(jnp.mean(sc * sc, axis=-1, keepdims=True) + NORM_EPS))
    y_b = jnp.concatenate(parts, axis=1) * wb * _sigmoid(bo) * _silu(bz)
    return jnp.concatenate([y_a, y_b], axis=1)


def _chip_of(dev):
    return 2 * dev[0] + dev[1]


def _index_of(dev):
    return 4 * dev[0] + 2 * dev[1] + dev[2]


class _Exchange:
    def __init__(self, srcs, out_shapes, transfers, local_copies=(), in_place=None):
        self.srcs, self.out_shapes = list(srcs), list(out_shapes)
        self.transfers, self.local_copies = list(transfers), list(local_copies)
        self.in_place = dict(in_place or {})

    def scratch(self):
        return [pltpu.SemaphoreType.DMA((len(self.transfers),)), pltpu.SemaphoreType.DMA((len(self.transfers),)),
                pltpu.SemaphoreType.DMA((max(len(self.local_copies), 1),))]

    def copies(self, ins, outs, send_sems, recv_sems, local_sems):
        me = (lax.axis_index("x"), lax.axis_index("y"), lax.axis_index("c"))

        def pick(ref, fn, *who):
            return ref if fn is None else ref.at[fn(*who)]

        sends, recvs, locs = [], [], []
        for t, (mask, si, sfn, di, dfn) in enumerate(self.transfers):
            peer = tuple(1 - p if flip else p for p, flip in zip(me, mask))
            sends.append(pltpu.make_async_remote_copy(
                src_ref=pick(ins[si], sfn, me, peer), dst_ref=pick(outs[di], dfn, me, peer),
                send_sem=send_sems.at[t], recv_sem=recv_sems.at[t], device_id=peer, device_id_type=MESH))
            landing = pick(outs[di], dfn, peer, me)
            recvs.append(pltpu.make_async_remote_copy(
                src_ref=landing, dst_ref=landing,
                send_sem=send_sems.at[t], recv_sem=recv_sems.at[t], device_id=peer, device_id_type=MESH))
        for l, (si, sfn, di, dfn) in enumerate(self.local_copies):
            locs.append(pltpu.make_async_copy(pick(ins[si], sfn, me), pick(outs[di], dfn, me), local_sems.at[l]))

        def start():
            for cp in locs + sends:
                cp.start()

        def wait():
            for cp in recvs:
                cp.wait_recv()
            for cp in sends:
                cp.wait_send()
            for cp in locs:
                cp.wait()

        return start, wait

    def run(self, name):
        n_in, n_out = len(self.srcs), len(self.out_shapes)

        def body(*refs):
            start, wait = self.copies(refs[:n_in], refs[n_in:n_in + n_out], *refs[n_in + n_out:])
            start()
            wait()

        hbm = pl.BlockSpec(memory_space=pltpu.HBM)
        return pl.pallas_call(
            body, name=name, out_shape=tuple(self.out_shapes), in_specs=[hbm] * n_in,
            out_specs=tuple([hbm] * n_out), scratch_shapes=self.scratch(), input_output_aliases=self.in_place,
        )(*self.srcs)


def _call(body, operands, *, name, grid, in_specs, out_specs, out_shape, scratch_shapes=(), sem=None, rider=None):
    out_specs, out_shape, scratch_shapes = list(out_specs), list(out_shape), list(scratch_shapes)
    if rider is None:
        res = pl.pallas_call(
            body, name=name, grid=grid, in_specs=list(in_specs), out_specs=tuple(out_specs),
            out_shape=tuple(out_shape), scratch_shapes=scratch_shapes, compiler_params=_params(sem),
        )(*operands)
        return list(res), []
    counts = (len(in_specs), len(rider.srcs), len(out_specs), len(rider.out_shapes), len(scratch_shapes), 3)

    def full(*refs):
        groups, pos = [], 0
        for k in counts:
            groups.append(refs[pos:pos + k])
            pos += k
        own_in, ex_in, own_out, ex_out, own_scr, ex_scr = groups
        ids = [pl.program_id(a) for a in range(len(grid))]
        first = functools.reduce(jnp.logical_and, [i == 0 for i in ids])
        last = functools.reduce(jnp.logical_and, [i == g - 1 for i, g in zip(ids, grid)])
        start, wait = rider.copies(ex_in, ex_out, *ex_scr)
        pl.when(first)(start)
        body(*own_in, *own_out, *own_scr)
        pl.when(last)(wait)

    hbm = pl.BlockSpec(memory_space=pltpu.HBM)
    res = pl.pallas_call(
        full, name=name, grid=grid, in_specs=list(in_specs) + [hbm] * counts[1],
        out_specs=tuple(out_specs + [hbm] * counts[3]), out_shape=tuple(out_shape + rider.out_shapes),
        scratch_shapes=scratch_shapes + rider.scratch(), compiler_params=_params(("arbitrary",) * len(grid)),
        input_output_aliases={counts[0] + i: counts[2] + o for i, o in rider.in_place.items()},
    )(*operands, *rider.srcs)
    return list(res[:counts[2]]), list(res[counts[2]:])


ALL_MASKS = [(mx, my, mc) for mx in (0, 1) for my in (0, 1) for mc in (0, 1)][1:]
CHIP_MASKS = [(1, 0, 0), (0, 1, 0), (1, 1, 0)]
SIBLING_MASK = (0, 0, 1)


def _all_gather8(v):
    out = jax.ShapeDtypeStruct((N_DEV,) + v.shape, v.dtype)
    slot = lambda sender, receiver: _index_of(sender)
    transfers = [(mask, 0, None, 0, slot) for mask in ALL_MASKS]
    return _Exchange([v], [out], transfers, [(0, None, 0, lambda me: _index_of(me))])


def _all_gather_chips(arrays):
    outs = [jax.ShapeDtypeStruct((N_CHIPS,) + a.shape, a.dtype) for a in arrays]
    slot = lambda sender, receiver: _chip_of(sender)
    return _Exchange(arrays, outs, [(mask, i, None, i, slot) for i in range(len(arrays)) for mask in CHIP_MASKS])


def _sibling_swap(arrays):
    outs = [jax.ShapeDtypeStruct(a.shape, a.dtype) for a in arrays]
    return _Exchange(arrays, outs, [(SIBLING_MASK, i, None, i, None) for i in range(len(arrays))])


def _chip_scatter(arrays):
    outs = [jax.ShapeDtypeStruct(a.shape, a.dtype) for a in arrays]
    transfers = [(mask, i, lambda s, r: _chip_of(r), i, lambda s, r: _chip_of(s))
                 for i in range(len(arrays)) for mask in CHIP_MASKS]
    return _Exchange(arrays, outs, transfers)


def _own_block(chip, own, blocks):
    sel = (lax.broadcasted_iota(jnp.int32, (N_CHIPS,) + (1,) * (blocks.ndim - 1), 0) == chip)
    return jnp.where(sel, own if own.ndim == blocks.ndim else own[None], blocks)


def _join_halves(ci, mine, other, axis):
    return jnp.where(ci == 0, jnp.concatenate([mine, other], axis=axis), jnp.concatenate([other, mine], axis=axis))


def _mm_nt(name, a, b, n, tm, tn, out_dtype, rider=None):
    m, k = a.shape

    def body(a_ref, b_ref, o_ref):
        o_ref[...] = _nt(a_ref[...], b_ref[...]).astype(out_dtype)

    (out,), rode = _call(
        body, (a, b), name=name, grid=(n // tn, m // tm),
        in_specs=[pl.BlockSpec((tm, k), lambda j, i: (i, 0)), pl.BlockSpec((tn, k), lambda j, i: (j, 0))],
        out_specs=[pl.BlockSpec((tm, tn), lambda j, i: (i, j))],
        out_shape=[jax.ShapeDtypeStruct((m, n), out_dtype)], sem=("parallel", "parallel"), rider=rider)
    return out, rode


def _mm_acc(name, a, b, tm, tk, rider=None):
    m, kc = a.shape
    n = b.shape[1]

    def body(a_ref, b_ref, o_ref):
        @pl.when(pl.program_id(1) == 0)
        def _():
            o_ref[...] = jnp.zeros_like(o_ref)
        o_ref[...] += _nn(a_ref[...], b_ref[...])

    (out,), rode = _call(
        body, (a, b), name=name, grid=(m // tm, kc // tk),
        in_specs=[pl.BlockSpec((tm, tk), lambda i, kk: (i, kk)), pl.BlockSpec((tk, n), lambda i, kk: (kk, 0))],
        out_specs=[pl.BlockSpec((tm, n), lambda i, kk: (i, 0))],
        out_shape=[jax.ShapeDtypeStruct((m, n), F32)], sem=("parallel", "arbitrary"), rider=rider)
    return out, rode


def _mm_tn(name, a, b, tm, tk, with_bf16=False):
    kr, m = a.shape
    n = b.shape[1]
    steps_k = kr // tk

    def body(a_ref, b_ref, o_ref, *narrow):
        @pl.when(pl.program_id(1) == 0)
        def _():
            o_ref[...] = jnp.zeros_like(o_ref)
        o_ref[...] += _tn(a_ref[...], b_ref[...])
        if with_bf16:
            @pl.when(pl.program_id(1) == steps_k - 1)
            def _():
                narrow[0][...] = o_ref[...].astype(BF16)

    out_spec = pl.BlockSpec((tm, n), lambda i, kk: (i, 0))
    res = pl.pallas_call(
        body, name=name, grid=(m // tm, steps_k),
        in_specs=[pl.BlockSpec((tk, tm), lambda i, kk: (kk, i)), pl.BlockSpec((tk, n), lambda i, kk: (kk, 0))],
        out_specs=(out_spec,) * (2 if with_bf16 else 1),
        out_shape=(jax.ShapeDtypeStruct((m, n), F32),) + ((jax.ShapeDtypeStruct((m, n), BF16),) if with_bf16 else ()),
        compiler_params=_params(("parallel", "arbitrary")),
    )(a, b)
    return res if with_bf16 else res[0]


def _modulate_fwd(x, ctx, prm, tm, rider=None):
    t_rows, dm = x.shape
    lat = t_rows // tm
    r = t_rows + ctx.shape[0]

    def body(x_ref, c_ref, p_ref, h_ref):
        xv = jnp.where(pl.program_id(0) >= lat, c_ref[...], x_ref[...])
        mu = jnp.mean(xv, axis=-1, keepdims=True)
        xm = xv - mu
        n = xm * lax.rsqrt(jnp.mean(xm * xm, axis=-1, keepdims=True) + LN_EPS)
        h_ref[...] = (n * (1.0 + p_ref[0, 1:2, :]) + p_ref[0, 0:1, :]).astype(BF16)

    (h,), rode = _call(
        body, (x, ctx, prm), name="modulate_fwd", grid=(r // tm,),
        in_specs=[pl.BlockSpec((tm, dm), lambda i: (jnp.minimum(i, lat - 1), 0)),
                  pl.BlockSpec((tm, dm), lambda i: (jnp.maximum(i - lat, 0), 0)),
                  pl.BlockSpec((1, 8, dm), lambda i: ((i >= lat).astype(jnp.int32), 0, 0))],
        out_specs=[pl.BlockSpec((tm, dm), lambda i: (i, 0))],
        out_shape=[jax.ShapeDtypeStruct((r, dm), BF16)], sem=("parallel",), rider=rider)
    return h, rode


def _modulate_bwd(x, ctx, dh, prm, gx_direct, tm, rider=None):
    t_rows, dm = x.shape
    lat, n_ct = t_rows // tm, ctx.shape[0] // tm
    is_ctx = lambda i: i < n_ct
    cls = lambda i: is_ctx(i).astype(jnp.int32)
    lat_tile = lambda i: (jnp.maximum(i - n_ct, 0), 0)

    def body(x_ref, c_ref, dh_ref, p_ref, gd_ref, gx_ref, acc_ref):
        i = pl.program_id(0)

        @pl.when((i == 0) | (i == n_ct))
        def _():
            acc_ref[...] = jnp.zeros_like(acc_ref)

        x = jnp.where(is_ctx(i), c_ref[...], x_ref[...])
        dh_v = dh_ref[...]
        mu = jnp.mean(x, axis=-1, keepdims=True)
        xm = x - mu
        rstd = lax.rsqrt(jnp.mean(xm * xm, axis=-1, keepdims=True) + LN_EPS)
        n = xm * rstd
        acc_ref[0, 0:1, :] += jnp.sum(dh_v, axis=0, keepdims=True)
        acc_ref[0, 1:2, :] += jnp.sum(dh_v * n, axis=0, keepdims=True)
        dn = dh_v * (1.0 + p_ref[0, 1:2, :])
        dx = rstd * (dn - jnp.mean(dn, axis=-1, keepdims=True) - n * jnp.mean(dn * n, axis=-1, keepdims=True))
        gx_ref[...] = dx + gd_ref[...]

    return _call(
        body, (x, ctx, dh, prm, gx_direct), name="modulate_bwd", grid=(n_ct + lat,),
        in_specs=[pl.BlockSpec((tm, dm), lat_tile),
                  pl.BlockSpec((tm, dm), lambda i: (jnp.minimum(i, n_ct - 1), 0)),
                  pl.BlockSpec((tm, dm), lambda i: (jnp.where(is_ctx(i), lat + i, i - n_ct), 0)),
                  pl.BlockSpec((1, 8, dm), lambda i: (cls(i), 0, 0)),
                  pl.BlockSpec((tm, dm), lat_tile)],
        out_specs=(pl.BlockSpec((tm, dm), lat_tile), pl.BlockSpec((1, 8, dm), lambda i: (cls(i), 0, 0))),
        out_shape=(jax.ShapeDtypeStruct((t_rows, dm), F32), jax.ShapeDtypeStruct((2, 8, dm), F32)),
        sem=("arbitrary",), rider=rider)


def _conv_parts(t_rows, c_rows):
    return ((0, t_rows, t_rows // GRID_W, GRID_W), (t_rows, c_rows, 1, c_rows))


def _col_shifts(x2, rows_g, width_g):
    n, ct = x2.shape
    col = lax.broadcasted_iota(jnp.int32, (width_g, ct), 0)
    as_grid = lambda a: a.reshape(rows_g, width_g, ct)
    left = as_grid(pltpu.roll(x2, 1, 0)) * (col >= 1).astype(F32)
    right = as_grid(pltpu.roll(x2, n - 1, 0)) * (col <= width_g - 2).astype(F32)
    return [left, as_grid(x2), right]


CONV_BLOCK_ROWS = 4


def _conv_blocks(t_rows, c_rows):
    for t0, _, rows_g, width_g in _conv_parts(t_rows, c_rows):
        nb = min(CONV_BLOCK_ROWS, rows_g)
        assert rows_g % nb == 0
        for g0 in range(0, rows_g, nb):
            yield t0, rows_g, width_g, g0, nb


def _slab(ref, t0, rows_g, width_g, g0, nb):
    if rows_g == 1:
        return ref[t0:t0 + width_g, :]
    lo, hi = max(g0 - 1, 0), min(g0 + nb + 1, rows_g)
    parts = [ref[t0 + lo * width_g:t0 + hi * width_g, :]]
    zero = jnp.zeros((width_g, ref.shape[1]), F32)
    if g0 == 0:
        parts.insert(0, zero)
    if g0 + nb == rows_g:
        parts.append(zero)
    return jnp.concatenate(parts, axis=0)


def _conv_taps(cols, w_ref, nb, flip):
    one_row = cols[0].shape[0] == nb
    acc = None
    for a in range(3):
        if one_row and a != 1:
            continue
        for b in range(3):
            tap = (2 - a) * 3 + (2 - b) if flip else a * 3 + b
            term = (cols[b] if one_row else cols[b][a:a + nb]) * w_ref[tap:tap + 1, :]
            acc = term if acc is None else acc + term
    return acc


def _conv_fwd(u, conv_w9, conv_b, t_rows, c_rows, w, ct):
    r = u.shape[0]
    base = 5 * w // ct

    def body(x_ref, w_ref, b_ref, o_ref):
        for t0, rows_g, width_g, g0, nb in _conv_blocks(t_rows, c_rows):
            slab = _slab(x_ref, t0, rows_g, width_g, g0, nb)
            cols = _col_shifts(slab, slab.shape[0] // width_g, width_g)
            pre = _conv_taps(cols, w_ref, nb, False) + b_ref[...]
            o_ref[t0 + g0 * width_g:t0 + (g0 + nb) * width_g, :] = _silu(pre).reshape(nb * width_g, ct)

    return pl.pallas_call(
        body, name="conv_fwd", grid=(2 * w // ct,),
        in_specs=[pl.BlockSpec((r, ct), lambda i: (0, base + i)), pl.BlockSpec((9, ct), lambda i: (0, i)),
                  pl.BlockSpec((1, ct), lambda i: (0, i))],
        out_specs=pl.BlockSpec((r, ct), lambda i: (0, i)),
        out_shape=jax.ShapeDtypeStruct((r, 2 * w), F32),
        compiler_params=_params(("parallel",)),
    )(u, conv_w9, conv_b)


def _conv_bwd(u, dqk_pair, conv_w9, conv_b, t_rows, c_rows, w, ct):
    r = u.shape[0]
    base = 5 * w // ct

    def body(x_ref, d1_ref, d2_ref, w_ref, b_ref, dx_ref, dw_ref, db_ref, dpre_ref):
        dw = [jnp.zeros((1, ct), F32) for _ in range(9)]
        db = jnp.zeros((1, ct), F32)
        for t0, rows_g, width_g, g0, nb in _conv_blocks(t_rows, c_rows):
            rows = slice(t0 + g0 * width_g, t0 + (g0 + nb) * width_g)
            slab = _slab(x_ref, t0, rows_g, width_g, g0, nb)
            cols = _col_shifts(slab, slab.shape[0] // width_g, width_g)
            pre = _conv_taps(cols, w_ref, nb, False) + b_ref[...]
            sg = _sigmoid(pre)
            dpre = (d1_ref[rows, :] + d2_ref[rows, :]).reshape(pre.shape) * (sg * (1.0 + pre * (1.0 - sg)))
            dpre_ref[rows, :] = dpre.reshape(nb * width_g, ct)
            db = db + jnp.sum(jnp.sum(dpre, axis=0), axis=0, keepdims=True)
            for a in range(3):
                if rows_g == 1 and a != 1:
                    continue
                for b in range(3):
                    moved = cols[b] if rows_g == 1 else cols[b][a:a + nb]
                    dw[a * 3 + b] = dw[a * 3 + b] + jnp.sum(jnp.sum(moved * dpre, axis=0), axis=0, keepdims=True)
        for t0, rows_g, width_g, g0, nb in _conv_blocks(t_rows, c_rows):
            slab = _slab(dpre_ref, t0, rows_g, width_g, g0, nb)
            cols = _col_shifts(slab, slab.shape[0] // width_g, width_g)
            dx_ref[t0 + g0 * width_g:t0 + (g0 + nb) * width_g, :] = _conv_taps(cols, w_ref, nb, True).reshape(
                nb * width_g, ct).astype(BF16)
        for tap in range(9):
            dw_ref[tap:tap + 1, :] = dw[tap]
        db_ref[...] = db

    return pl.pallas_call(
        body, name="conv_bwd", grid=(2 * w // ct,),
        in_specs=[pl.BlockSpec((r, ct), lambda i: (0, base + i)), pl.BlockSpec((r, ct), lambda i: (0, i)),
                  pl.BlockSpec((r, ct), lambda i: (0, i)),
                  pl.BlockSpec((9, ct), lambda i: (0, i)), pl.BlockSpec((1, ct), lambda i: (0, i))],
        out_specs=(pl.BlockSpec((r, ct), lambda i: (0, i)), pl.BlockSpec((9, ct), lambda i: (0, i)),
                   pl.BlockSpec((1, ct), lambda i: (0, i))),
        out_shape=(jax.ShapeDtypeStruct((r, 2 * w), BF16), jax.ShapeDtypeStruct((9, 2 * w), F32),
                   jax.ShapeDtypeStruct((1, 2 * w), F32)),
        scratch_shapes=[pltpu.VMEM((r, ct), F32)],
        compiler_params=_params(("parallel",)),
    )(u, dqk_pair[0], dqk_pair[1], conv_w9, conv_b)


def _assemble_du(groups, gates, n_pad, tm):
    flat, layout = [], []
    for entry in list(groups) + [gates]:
        parts = entry if isinstance(entry, (tuple, list)) else (entry,)
        layout.append((len(flat), len(parts), parts[0].shape[1]))
        flat += list(parts)
    r = flat[0].shape[0]

    def body(*refs):
        o_ref = refs[-1]
        col = 0
        for first, count, width in layout:
            val = refs[first][...]
            for extra in range(1, count):
                val = val.astype(F32) + refs[first + extra][...].astype(F32)
            o_ref[:, col:col + width] = val.astype(BF16)
            col += width
        assert col == n_pad

    return pl.pallas_call(
        body, name="assemble_du", grid=(r // tm,),
        in_specs=[pl.BlockSpec((tm, a.shape[1]), lambda i: (i, 0)) for a in flat],
        out_specs=pl.BlockSpec((tm, n_pad), lambda i: (i, 0)),
        out_shape=jax.ShapeDtypeStruct((r, n_pad), BF16),
        compiler_params=_params(("parallel",)),
    )(*flat)


def _scan_order(n_lat, n_ctx, rev):
    n = n_lat + n_ctx
    if rev:
        return lambda j: n - 1 - j
    return lambda j: (j + n_lat) % n


DIRS = (False, True)


def _hg_scan_fwd(u, lb_full, w, n_lat, n_ctx, chunk, rider=None):
    r = u.shape[0]
    n_heads = w // HG_DK
    sub = HG_CHUNKS_PER_STEP if n_lat % HG_CHUNKS_PER_STEP == 0 and n_ctx % HG_CHUNKS_PER_STEP == 0 else 1
    n_steps = (n_lat + n_ctx) // sub
    nat = [_scan_order(n_lat // sub, n_ctx // sub, rev) for rev in DIRS]
    rows = sub * chunk

    def body(*refs):
        ins, outs, scratch = refs[:8], refs[8:12], refs[12:]

        @pl.when(pl.program_id(0) == 0)
        def _():
            for s_ref in scratch:
                s_ref[...] = jnp.zeros_like(s_ref)

        for d, rev in enumerate(DIRS):
            aq, af, ai, lb_ref = ins[4 * d:4 * d + 4]
            o_ref, save_ref = outs[2 * d:2 * d + 2]
            state = [scratch[d][h] for h in range(n_heads)]
            for p in range(sub):
                sl = slice((sub - 1 - p if rev else p) * chunk, (sub - p if rev else p + 1) * chunk)
                for h in range(n_heads):
                    save_ref[0, p, h] = state[h]
                state, o = _hg_chunk(state, aq[sl, :], af[sl, :], ai[sl, :], lb_ref[0, 0:1, :], lb_ref[0, 1:2, :], rev)
                o_ref[sl, :] = o
            for h in range(n_heads):
                scratch[d][h] = state[h]

    in_specs, out_specs, out_shape = [], [], []
    for d in range(2):
        in_specs += [pl.BlockSpec((rows, w), lambda j, d=d: (nat[d](j), 0)),
                     pl.BlockSpec((rows, w), lambda j, d=d: (nat[d](j), 1 + d)),
                     pl.BlockSpec((rows, w), lambda j, d=d: (nat[d](j), 3)),
                     pl.BlockSpec((1, 2, w), lambda j, d=d: (d, 0, 0))]
        out_specs += [pl.BlockSpec((rows, w), lambda j, d=d: (nat[d](j), 0)),
                      pl.BlockSpec((1, sub, n_heads, HG_DK, HG_DK), lambda j: (j, 0, 0, 0, 0))]
        out_shape += [jax.ShapeDtypeStruct((r, w), F32),
                      jax.ShapeDtypeStruct((n_steps, sub, n_heads, HG_DK, HG_DK), F32)]
    (o_f, s_f, o_b, s_b), rode = _call(
        body, (u, u, u, lb_full, u, u, u, lb_full), name="hg_scan_fwd", grid=(n_steps,), in_specs=in_specs,
        out_specs=out_specs, out_shape=out_shape, scratch_shapes=[pltpu.VMEM((n_heads, HG_DK, HG_DK), F32)] * 2,
        sem=("arbitrary",), rider=rider)
    return (o_f, o_b), (s_f, s_b), rode


def _hg_scan_bwd(u, lb_full, saved, d_o, w, n_lat, n_ctx, chunk, rider=None):
    r = u.shape[0]
    n_heads = w // HG_DK
    n_steps, sub = saved[0].shape[0], saved[0].shape[1]
    n_lat_s = n_lat // sub
    step = lambda jj: n_steps - 1 - jj
    nat = [(lambda jj, o=_scan_order(n_lat_s, n_ctx // sub, rev): o(step(jj))) for rev in DIRS]
    rows = sub * chunk

    def body(*refs):
        ins, outs, scratch = refs[:12], refs[12:20], refs[20:]
        jj = pl.program_id(0)

        @pl.when(jj == 0)
        def _():
            for d in range(2):
                scratch[d][...] = jnp.zeros_like(scratch[d])
                outs[4 * d + 3][...] = jnp.zeros_like(outs[4 * d + 3])

        for d, rev in enumerate(DIRS):
            aq, af, ai, lb_ref, save_ref, do_ref = ins[6 * d:6 * d + 6]
            daq_ref, daf_ref, dai_ref, dlb_ref = outs[4 * d:4 * d + 4]
            f = lambda st, a, b, c, l0, l1, rev=rev: _hg_chunk(st, a, b, c, l0, l1, rev)
            latent = (nat[d](jj) < n_lat_s).astype(F32)
            d_state = [scratch[d][h] for h in range(n_heads)]
            for p in reversed(range(sub)):
                sl = slice((sub - 1 - p if rev else p) * chunk, (sub - p if rev else p + 1) * chunk)
                _, vjp = jax.vjp(f, [save_ref[0, p, h] for h in range(n_heads)], aq[sl, :], af[sl, :], ai[sl, :],
                                 lb_ref[0, 0:1, :], lb_ref[0, 1:2, :])
                d_state, daq, daf, dai, dl0, dl1 = vjp((d_state, do_ref[sl, :] * latent))
                daq_ref[sl, :] = daq.astype(BF16)
                daf_ref[sl, :] = daf.astype(BF16)
                dai_ref[sl, :] = dai.astype(BF16)
                dlb_ref[0:1, :] += dl0
                dlb_ref[1:2, :] += dl1
            for h in range(n_heads):
                scratch[d][h] = d_state[h]

    in_specs, out_specs, out_shape, operands = [], [], [], []
    for d in range(2):
        row = lambda jj, d=d: (nat[d](jj), 0)
        in_specs += [pl.BlockSpec((rows, w), row),
                     pl.BlockSpec((rows, w), lambda jj, d=d: (nat[d](jj), 1 + d)),
                     pl.BlockSpec((rows, w), lambda jj, d=d: (nat[d](jj), 3)),
                     pl.BlockSpec((1, 2, w), lambda jj, d=d: (d, 0, 0)),
                     pl.BlockSpec((1, sub, n_heads, HG_DK, HG_DK), lambda jj: (step(jj), 0, 0, 0, 0)),
                     pl.BlockSpec((rows, w), lambda jj, d=d: (jnp.minimum(nat[d](jj), n_lat_s - 1), 0))]
        operands += [u, u, u, lb_full, saved[d], d_o]
        out_specs += [pl.BlockSpec((rows, w), row)] * 3 + [pl.BlockSpec((2, w), lambda jj: (0, 0))]
        out_shape += [jax.ShapeDtypeStruct((r, w), BF16)] * 3 + [jax.ShapeDtypeStruct((2, w), F32)]
    res, rode = _call(
        body, operands, name="hg_scan_bwd", grid=(n_steps,), in_specs=in_specs, out_specs=out_specs,
        out_shape=out_shape, scratch_shapes=[pltpu.VMEM((n_heads, HG_DK, HG_DK), F32)] * 2,
        sem=("arbitrary",), rider=rider)
    return res[0:4], res[4:8], rode


def _ml_state_shapes(n_chunks, n_heads, dh):
    return (jax.ShapeDtypeStruct((n_chunks, n_heads, dh, dh), F32),
            jax.ShapeDtypeStruct((n_chunks, n_heads, 1, dh), F32),
            jax.ShapeDtypeStruct((n_chunks, n_heads, 1, LANE), F32))


def _ml_state_specs(n_heads, dh, index):
    return (pl.BlockSpec((1, n_heads, dh, dh), lambda j: (index(j), 0, 0, 0)),
            pl.BlockSpec((1, n_heads, 1, dh), lambda j: (index(j), 0, 0, 0)),
            pl.BlockSpec((1, n_heads, 1, LANE), lambda j: (index(j), 0, 0, 0)))


def _ml_state_scratch(n_heads, dh):
    return [pltpu.VMEM((n_heads, dh, dh), F32), pltpu.VMEM((n_heads, 1, dh), F32), pltpu.VMEM((n_heads, 1, LANE), F32)]


def _ml_scan_fwd(qk, u, gate_b, w, n_heads, n_lat, n_ctx, chunk):
    r = u.shape[0]
    dh = w // n_heads
    n_chunks = n_lat + n_ctx
    nat = [_scan_order(n_lat, n_ctx, rev) for rev in DIRS]

    def body(*refs):
        ins, outs, scratch = refs[:10], refs[10:18], refs[18:]

        @pl.when(pl.program_id(0) == 0)
        def _():
            for s_ref in scratch:
                s_ref[...] = jnp.zeros_like(s_ref)

        results = []
        for d, rev in enumerate(DIRS):
            q, k, v, g, gb = ins[5 * d:5 * d + 5]
            state = tuple([ref[h] for h in range(n_heads)] for ref in scratch[3 * d:3 * d + 3])
            results.append((state, _ml_chunk(state, q[...], k[...], v[...], g[...], gb[...], rev, d)))
        for d, (state, (new, o)) in enumerate(results):
            outs[4 * d][...] = o
            for part in range(3):
                for h in range(n_heads):
                    outs[4 * d + 1 + part][0, h] = state[part][h]
                    scratch[3 * d + part][h] = new[part][h]

    in_specs, out_specs, out_shape = [], [], []
    for d in range(2):
        in_specs += [pl.BlockSpec((chunk,w), lambda j, d=d: (nat[d](j), 0)),
                     pl.BlockSpec((chunk,w), lambda j, d=d: (nat[d](j), 1)),
                     pl.BlockSpec((chunk,w), lambda j, d=d: (nat[d](j), 7)),
                     pl.BlockSpec((chunk,LANE), lambda j, d=d: (nat[d](j), 10 * w // LANE)),
                     pl.BlockSpec((1, LANE), lambda j: (0, 0))]
        out_specs += [pl.BlockSpec((chunk,w), lambda j, d=d: (nat[d](j), 0))]
        out_specs += list(_ml_state_specs(n_heads, dh, lambda j: j))
        out_shape += [jax.ShapeDtypeStruct((r, w), F32)] + list(_ml_state_shapes(n_chunks, n_heads, dh))
    res = pl.pallas_call(
        body, name="ml_scan_fwd", grid=(n_chunks,), in_specs=in_specs, out_specs=tuple(out_specs),
        out_shape=tuple(out_shape), scratch_shapes=_ml_state_scratch(n_heads, dh) * 2,
        compiler_params=_params(("arbitrary",)),
    )(qk, qk, u, u, gate_b, qk, qk, u, u, gate_b)
    return (res[0], res[4]), (res[1:4], res[5:8])


def _ml_scan_bwd(qk, u, gate_b, saved, d_h, w, n_heads, n_lat, n_ctx, chunk, rider=None):
    r = u.shape[0]
    dh = w // n_heads
    n_chunks = n_lat + n_ctx
    step = lambda jj: n_chunks - 1 - jj
    nat = [(lambda jj, o=_scan_order(n_lat, n_ctx, rev): o(step(jj))) for rev in DIRS]

    def body(*refs):
        ins, outs, scratch = refs[:18], refs[18:26], refs[26:]
        jj = pl.program_id(0)

        @pl.when(jj == 0)
        def _():
            for s_ref in scratch:
                s_ref[...] = jnp.zeros_like(s_ref)
            for d in range(2):
                outs[4 * d + 3][...] = jnp.zeros_like(outs[4 * d + 3])

        results = []
        for d, rev in enumerate(DIRS):
            q, k, v, g, gb, sc, sn, sm, dh_ref = ins[9 * d:9 * d + 9]
            state = tuple([ref[0, h] for h in range(n_heads)] for ref in (sc, sn, sm))
            f = lambda st, a, b, c, gg, bb, rev=rev, d=d: _ml_chunk(st, a, b, c, gg, bb, rev, d)
            _, vjp = jax.vjp(f, state, q[...], k[...], v[...], g[...], gb[...])
            d_state = tuple([ref[h] for h in range(n_heads)] for ref in scratch[3 * d:3 * d + 3])
            d_out = dh_ref[...] * (nat[d](jj) < n_lat).astype(F32)
            results.append(vjp((d_state, d_out)))
        for d, (d_state, dq, dk, dv, dg, dgb) in enumerate(results):
            dqk_ref, dv_ref, dg_ref, dgb_ref = outs[4 * d:4 * d + 4]
            for part in range(3):
                for h in range(n_heads):
                    scratch[3 * d + part][h] = d_state[part][h]
            dqk_ref[:, 0:w] = dq
            dqk_ref[:, w:2 * w] = dk
            dv_ref[...] = dv.astype(BF16)
            dg_ref[...] = dg
            dgb_ref[...] += dgb

    in_specs, out_specs, out_shape, operands = [], [], [], []
    for d in range(2):
        row = lambda jj, d=d: (nat[d](jj), 0)
        in_specs += [pl.BlockSpec((chunk,w), row), pl.BlockSpec((chunk,w), lambda jj, d=d: (nat[d](jj), 1)),
                     pl.BlockSpec((chunk,w), lambda jj, d=d: (nat[d](jj), 7)),
                     pl.BlockSpec((chunk,LANE), lambda jj, d=d: (nat[d](jj), 10 * w // LANE)),
                     pl.BlockSpec((1, LANE), lambda jj: (0, 0))]
        in_specs += list(_ml_state_specs(n_heads, dh, step))
        in_specs += [pl.BlockSpec((chunk,w), lambda jj, d=d: (jnp.minimum(nat[d](jj), n_lat - 1), 0))]
        operands += [qk, qk, u, u, gate_b, *saved[d], d_h]
        out_specs += [pl.BlockSpec((chunk,2 * w), row), pl.BlockSpec((chunk,w), row),
                      pl.BlockSpec((chunk,LANE), row), pl.BlockSpec((1, LANE), lambda jj: (0, 0))]
        out_shape += [jax.ShapeDtypeStruct((r, 2 * w), F32), jax.ShapeDtypeStruct((r, w), BF16),
                      jax.ShapeDtypeStruct((r, LANE), F32), jax.ShapeDtypeStruct((1, LANE), F32)]
    res, rode = _call(
        body, operands, name="ml_scan_bwd", grid=(n_chunks,), in_specs=in_specs, out_specs=out_specs,
        out_shape=out_shape, scratch_shapes=_ml_state_scratch(n_heads, dh) * 2, sem=("arbitrary",), rider=rider)
    return res[0:4], res[4:8], rode


def _post_specs(w, tm, lat_tiles, cols):
    return [pl.BlockSpec((tm, w), (lambda i, cb=cb: (jnp.minimum(i, lat_tiles - 1), cb))) for cb in cols]


def _post_fwd(o_f, o_b, h_f, h_b, u, wa, wb, t_rows, w, n_hg, n_ml, tm):
    lat_tiles = t_rows // tm

    def body(of, ob, hf, hb, az, bo, bz, wa_ref, wb_ref, y_ref):
        y_ref[...] = _post_fn(of[...], ob[...], az[...], hf[...], hb[...], bo[...], bz[...],
                              wa_ref[...], wb_ref[...], n_hg, n_ml).astype(BF16)

    rows = pl.BlockSpec((tm, w), lambda i: (i, 0))
    vec = pl.BlockSpec((1, w), lambda i: (0, 0))
    return pl.pallas_call(
        body, name="post_fwd", grid=(lat_tiles,),
        in_specs=[rows] * 4 + _post_specs(w, tm, lat_tiles, (4, 8, 9)) + [vec, vec],
        out_specs=pl.BlockSpec((tm, 2 * w), lambda i: (i, 0)),
        out_shape=jax.ShapeDtypeStruct((t_rows, 2 * w), BF16),
        compiler_params=_params(("parallel",)),
    )(o_f, o_b, h_f, h_b, u, u, u, wa, wb)


def _post_bwd(o_f, o_b, h_f, h_b, u, wa, wb, dy, t_rows, w, n_hg, n_ml, tm, rider=None):
    r = u.shape[0]
    lat_tiles = t_rows // tm
    lat = lambda i: (jnp.minimum(i, lat_tiles - 1), 0)

    def body(of, ob, hf, hb, az, bo, bz, wa_ref, wb_ref, dy_ref, do_ref, dh_ref, daz_ref, dbo_ref, dbz_ref,
             dwa_ref, dwb_ref):
        i = pl.program_id(0)

        @pl.when(i == 0)
        def _():
            dwa_ref[...] = jnp.zeros_like(dwa_ref)
            dwb_ref[...] = jnp.zeros_like(dwb_ref)

        @pl.when(i < lat_tiles)
        def _():
            f = functools.partial(_post_fn, n_hg=n_hg, n_ml=n_ml)
            _, vjp = jax.vjp(f, of[...], ob[...], az[...], hf[...], hb[...], bo[...], bz[...], wa_ref[...], wb_ref[...])
            d_of, _, d_az, d_hf, _, d_bo, d_bz, d_wa, d_wb = vjp(dy_ref[...])
            do_ref[...] = d_of
            dh_ref[...] = d_hf
            daz_ref[...] = d_az.astype(BF16)
            dbo_ref[...] = d_bo.astype(BF16)
            dbz_ref[...] = d_bz.astype(BF16)
            dwa_ref[...] += d_wa
            dwb_ref[...] += d_wb

        @pl.when(i >= lat_tiles)
        def _():
            daz_ref[...] = jnp.zeros_like(daz_ref)
            dbo_ref[...] = jnp.zeros_like(dbo_ref)
            dbz_ref[...] = jnp.zeros_like(dbz_ref)

    lat_rows = pl.BlockSpec((tm, w), lat)
    all_rows = pl.BlockSpec((tm, w), lambda i: (i, 0))
    vec = pl.BlockSpec((1, w), lambda i: (0, 0))
    sd_t = jax.ShapeDtypeStruct((t_rows, w), F32)
    sd_r = jax.ShapeDtypeStruct((r, w), BF16)
    sd_v = jax.ShapeDtypeStruct((1, w), F32)
    return _call(
        body, (o_f, o_b, h_f, h_b, u, u, u, wa, wb, dy), name="post_bwd", grid=(r // tm,),
        in_specs=[lat_rows] * 4 + _post_specs(w, tm, lat_tiles, (4, 8, 9)) + [vec, vec]
        + [pl.BlockSpec((tm, 2 * w), lat)],
        out_specs=(lat_rows, lat_rows, all_rows, all_rows, all_rows, vec, vec),
        out_shape=(sd_t, sd_t, sd_r, sd_r, sd_r, sd_v, sd_v), sem=("arbitrary",), rider=rider)


OUT_ROW_GATE, OUT_ROW_LN_G, OUT_ROW_LN_B, OUT_ROW_LOSS = 0, 1, 2, 3


def _out_block(y, w_out, x, target, prm, tm):
    t_rows, dm = x.shape
    di = y.shape[1]

    def body(y_ref, w_ref, x_ref, t_ref, p_ref, dz_ref, dy_ref, gx_ref, acc_ref):
        @pl.when(pl.program_id(0) == 0)
        def _():
            acc_ref[...] = jnp.zeros_like(acc_ref)

        gate, ln_g, ln_b = p_ref[0:1, :], p_ref[1:2, :], p_ref[2:3, :]
        z = _nn(y_ref[...], w_ref[...])
        res = ALPHA * x_ref[...] + gate * z
        mu = jnp.mean(res, axis=-1, keepdims=True)
        rc = res - mu
        rstd = lax.rsqrt(jnp.mean(rc * rc, axis=-1, keepdims=True) + LN_EPS)
        rn = rc * rstd
        err = rn * ln_g + ln_b - t_ref[...]
        d_out = err * (1.0 / dm)
        d_rn = d_out * ln_g
        d_res = rstd * (d_rn - jnp.mean(d_rn, axis=-1, keepdims=True)
                        - rn * jnp.mean(d_rn * rn, axis=-1, keepdims=True))
        acc_ref[OUT_ROW_GATE:OUT_ROW_GATE + 1, :] += jnp.sum(d_res * z, axis=0, keepdims=True)
        acc_ref[OUT_ROW_LN_G:OUT_ROW_LN_G + 1, :] += jnp.sum(d_out * rn, axis=0, keepdims=True)
        acc_ref[OUT_ROW_LN_B:OUT_ROW_LN_B + 1, :] += jnp.sum(d_out, axis=0, keepdims=True)
        acc_ref[OUT_ROW_LOSS:OUT_ROW_LOSS + 1, :] += (0.5 / dm) * jnp.sum(err * err, axis=0, keepdims=True)
        gx_ref[...] = ALPHA * d_res
        dz = (d_res * gate).astype(BF16)
        dz_ref[...] = dz
        dy_ref[...] = _nt(dz, w_ref[...])

    rows_d = pl.BlockSpec((tm, dm), lambda i: (i, 0))
    rows_i = pl.BlockSpec((tm, di), lambda i: (i, 0))
    return pl.pallas_call(
        body, name="out_block", grid=(t_rows // tm,),
        in_specs=[rows_i, pl.BlockSpec((di, dm), lambda i: (0, 0)), rows_d, rows_d,
                  pl.BlockSpec((8, dm), lambda i: (0, 0))],
        out_specs=(rows_d, rows_i, rows_d, pl.BlockSpec((8, dm), lambda i: (0, 0))),
        out_shape=(jax.ShapeDtypeStruct((t_rows, dm), BF16), jax.ShapeDtypeStruct((t_rows, di), F32),
                   jax.ShapeDtypeStruct((t_rows, dm), F32), jax.ShapeDtypeStruct((8, dm), F32)),
        compiler_params=_params(("arbitrary",)),
    )(y, w_out, x, target, prm)


def _mod_fwd(c16, w_mod, tn):
    dm, n = w_mod.shape

    def body(c_ref, w_ref, o_ref, a_ref):
        a = _silu(c_ref[...])
        a_ref[...] = a
        o_ref[...] = _nn(a, w_ref[...], HIGHEST)

    return pl.pallas_call(
        body, name="mod_fwd", grid=(n // tn,),
        in_specs=[pl.BlockSpec((16, dm), lambda j: (0, 0)), pl.BlockSpec((dm, tn), lambda j: (0, j))],
        out_specs=(pl.BlockSpec((16, tn), lambda j: (0, j)), pl.BlockSpec((16, dm), lambda j: (0, 0))),
        out_shape=(jax.ShapeDtypeStruct((16, n), F32), jax.ShapeDtypeStruct((16, dm), F32)),
        compiler_params=_params(("arbitrary",)),
    )(c16, w_mod)


def _mod_bwd(a16, dm16, w_mod, tn, rider=None):
    dm, n = w_mod.shape

    def body(a_ref, d_ref, w_ref, dw_ref, dc_ref):
        @pl.when(pl.program_id(0) == 0)
        def _():
            dc_ref[...] = jnp.zeros_like(dc_ref)
        dw_ref[...] = _tn(a_ref[...], d_ref[...], HIGHEST)
        dc_ref[...] += _nt(d_ref[...], w_ref[...], HIGHEST)

    return _call(
        body, (a16, dm16, w_mod), name="mod_bwd", grid=(n // tn,),
        in_specs=[pl.BlockSpec((16, dm), lambda j: (0, 0)), pl.BlockSpec((16, tn), lambda j: (0, j)),
                  pl.BlockSpec((dm, tn), lambda j: (0, j))],
        out_specs=(pl.BlockSpec((dm, tn), lambda j: (0, j)), pl.BlockSpec((16, dm), lambda j: (0, 0))),
        out_shape=(jax.ShapeDtypeStruct((dm, n), F32), jax.ShapeDtypeStruct((16, dm), F32)),
        sem=("arbitrary",), rider=rider)


def _sum_devices(g, fold_rows):
    n_dev, rows, n = g.shape

    def body(g_ref, s_ref, t_ref):
        s = g_ref[0]
        for dev in range(1, n_dev):
            s = s + g_ref[dev]
        t_ref[...] = jnp.broadcast_to(jnp.sum(s, axis=-1, keepdims=True), (rows, LANE))
        s_ref[...] = s
        s_ref[0:fold_rows, :] = s[0:fold_rows] + s[fold_rows:2 * fold_rows]

    return pl.pallas_call(
        body, name="sum_devices",
        out_shape=(jax.ShapeDtypeStruct((rows, n), F32), jax.ShapeDtypeStruct((rows, LANE), F32)),
        compiler_params=_params(),
    )(g)


def _c_ctx_grad(parts, c_ctx_row):
    def body(p_ref, c_ref, o_ref):
        s = p_ref[0]
        for chip in range(1, N_CHIPS):
            s = s + p_ref[2 * chip]
        cv = c_ref[...]
        sg = _sigmoid(cv)
        o_ref[...] = s * (sg * (1.0 + cv * (1.0 - sg)))

    return pl.pallas_call(
        body, name="c_ctx_grad", out_shape=jax.ShapeDtypeStruct(parts.shape[1:], F32), compiler_params=_params(),
    )(parts, c_ctx_row)


def _sum_pair(name, mine, got):
    def body(a_ref, b_ref, o_ref):
        o_ref[...] = (a_ref[...] + b_ref[...]).astype(BF16)

    k, rows, n = mine.shape
    tl = _largest_divisor(n, max(LANE, (1 << 18) // rows), LANE)
    spec = pl.BlockSpec((1, rows, tl), lambda kk, i: (kk, 0, i))
    return pl.pallas_call(
        body, name=name, grid=(k, n // tl), in_specs=[spec, spec], out_specs=spec,
        out_shape=jax.ShapeDtypeStruct(mine.shape, BF16), compiler_params=_params(("parallel", "parallel")),
    )(mine, got)


def _sum_pair_lanes(name, full, got, ci):
    rows, n = got.shape
    tr = _largest_divisor(rows, max(SUBLANE_BF16, (1 << 19) // n), SUBLANE_BF16)

    def body(ci_ref, a_ref, b_ref, o_ref):
        o_ref[...] = (a_ref[...] + b_ref[...].astype(F32)).astype(BF16)

    return pl.pallas_call(
        body, name=name,
        grid_spec=pltpu.PrefetchScalarGridSpec(
            num_scalar_prefetch=1, grid=(rows // tr,),
            in_specs=[pl.BlockSpec((tr, n), lambda i, c: (i, c[0])), pl.BlockSpec((tr, n), lambda i, c: (i, 0))],
            out_specs=pl.BlockSpec((tr, n), lambda i, c: (i, 0))),
        out_shape=jax.ShapeDtypeStruct((rows, n), BF16), compiler_params=_params(("parallel",)),
    )(ci.reshape(1).astype(jnp.int32), full, got)


def _sum_chips(name, got, own, chip):
    k, rows, n = got.shape
    tl = _largest_divisor(n, max(LANE, (1 << 18) // rows), LANE)

    def body(chip_ref, g_ref, own_ref, o_ref):
        total = None
        for kk in range(k):
            term = jnp.where(chip_ref[0] == kk, own_ref[0], g_ref[kk]).astype(F32)
            total = term if total is None else total + term
        o_ref[...] = total

    return pl.pallas_call(
        body, name=name,
        grid_spec=pltpu.PrefetchScalarGridSpec(
            num_scalar_prefetch=1, grid=(n // tl,),
            in_specs=[pl.BlockSpec((k, rows, tl), lambda i, c: (0, 0, i)),
                      pl.BlockSpec((1, rows, tl), lambda i, c: (c[0], 0, i))],
            out_specs=pl.BlockSpec((rows, tl), lambda i, c: (0, i))),
        out_shape=jax.ShapeDtypeStruct((rows, n), F32), compiler_params=_params(("parallel",)),
    )(chip.reshape(1).astype(jnp.int32), got, own)


def _adamw_update(w, g, m, v):
    m2 = ADAM_B1 * m + (1.0 - ADAM_B1) * g
    v2 = ADAM_B2 * v + (1.0 - ADAM_B2) * jnp.square(g)
    m_hat = m2 / (1.0 - ADAM_B1 ** ADAM_STEP)
    v_hat = v2 / (1.0 - ADAM_B2 ** ADAM_STEP)
    return -ADAM_LR * (m_hat / (jnp.sqrt(v_hat) + ADAM_EPS) + ADAM_WD * w), m2, v2


def _adamw(name, w, g, m, v, rider=None):
    rows, n = w.shape
    if rows % 8 == 0:
        tr = _largest_divisor(rows, max(8, (1 << 18) // n), 8)
        block, index, steps = (tr, n), (lambda i: (i, 0)), rows // tr
    else:
        tl = _largest_divisor(n, max(LANE, (1 << 18) // rows), LANE)
        block, index, steps = (rows, tl), (lambda i: (0, i)), n // tl

    def body(w_ref, g_ref, m_ref, v_ref, d_ref, mo_ref, vo_ref):
        d_ref[...], mo_ref[...], vo_ref[...] = _adamw_update(w_ref[...], g_ref[...], m_ref[...], v_ref[...])

    spec = pl.BlockSpec(block, index)
    sds = jax.ShapeDtypeStruct((rows, n), F32)
    return _call(body, (w, g, m, v), name=name, grid=(steps,), in_specs=[spec] * 4, out_specs=(spec,) * 3,
                 out_shape=(sds, sds, sds), sem=("parallel",), rider=rider)


PACK_LANES = 1024


def _pack(pieces):
    flat = jnp.concatenate([p.reshape(-1) for p in pieces])
    total = -(-flat.shape[0] // (8 * PACK_LANES)) * 8 * PACK_LANES
    return jnp.pad(flat, (0, total - flat.shape[0])).reshape(-1, PACK_LANES)


def _unpack(packed, shapes):
    flat = packed.reshape(-1)
    out, off = [], 0
    for shp in shapes:
        size = math.prod(shp)
        out.append(flat[off:off + size].reshape(shp))
        off += size
    return out


def _rows8(rows, width):
    flat = [r.reshape(width) for r in rows] + [jnp.zeros(((8 - len(rows)) * width,), F32)]
    return jnp.concatenate(flat).reshape(8, width)


def kernel(x, c, ctx, c_ctx, w_mod, b_mod, w_in, conv_w, conv_b, hg_lb, ml_gate_b, hg_norm_w, ml_norm_w, w_out, ln_g, ln_b, loss_target, m_c_ctx, m_w_mod, m_b_mod, m_w_in, m_conv_w, m_conv_b, m_hg_lb, m_ml_gate_b, m_hg_norm_w, m_ml_norm_w, m_w_out, m_ln_g, m_ln_b, v_c_ctx, v_w_mod, v_b_mod, v_w_in, v_conv_w, v_conv_b, v_hg_lb, v_ml_gate_b, v_hg_norm_w, v_ml_norm_w, v_w_out, v_ln_g, v_ln_b):
    t_rows, dm = x.shape[1], x.shape[2]
    c_rows = ctx.shape[1]
    w = hg_norm_w.shape[1]
    n_ml = ml_gate_b.shape[-1]
    n_hg = w // HG_DK
    di = 2 * w
    n_in = 10 * w + 4 * n_ml
    ns = w_in.shape[2]
    nm = w_mod.shape[2]
    n_pad = 10 * w + LANE
    r_rows = t_rows + c_rows
    row_gcd = math.gcd(t_rows, c_rows)
    hg_chunk, ml_chunk = math.gcd(HG_CHUNK, row_gcd), math.gcd(ML_CHUNK, row_gcd)
    hg_counts = (t_rows // hg_chunk, c_rows // hg_chunk, hg_chunk)
    ml_counts = (t_rows // ml_chunk, c_rows // ml_chunk, ml_chunk)
    assert ml_norm_w.shape[1] == w and di == dm and N_CHIPS * ns == n_in and N_CHIPS * nm == 3 * dm
    assert w_out.shape[1] * N_CHIPS == di and 4 * n_ml <= LANE and t_rows % GRID_W == 0

    xi, yi, ci = lax.axis_index("x"), lax.axis_index("y"), lax.axis_index("c")
    chip = 2 * xi + yi
    dev = 4 * xi + 2 * yi + ci

    tm = _largest_divisor(math.gcd(t_rows, c_rows), 256, 8)
    tm_mm = _largest_divisor(r_rows, 1088, SUBLANE_BF16)
    tn_mm = LANE * _largest_divisor(n_pad // LANE, 9)
    tn_mod = _largest_divisor(nm, 512, LANE)

    shard_shapes = [(dm,), (2, 2, w // N_CHIPS), (3, 3, di // N_CHIPS)]
    g1 = _all_gather8(_pack([c, hg_lb, conv_w])).run("gather_inputs")[0]
    per_dev = [_unpack(g1[i], shard_shapes) for i in range(N_DEV)]
    c_all = jnp.stack([p[0] for p in per_dev])
    lb_full = jnp.concatenate([per_dev[2 * k][1] for k in range(N_CHIPS)], axis=-1)
    conv_w9 = jnp.concatenate([per_dev[2 * k][2] for k in range(N_CHIPS)], axis=-1).reshape(9, di)

    c16 = jnp.concatenate([c_all, c_ctx[None], jnp.zeros((16 - N_DEV - 1, dm), F32)])
    mod_part, a16 = _mod_fwd(c16, w_mod[0], tn_mod)
    g2 = _all_gather8(mod_part).run("gather_mod")[0]
    mod_all = jnp.concatenate([g2[2 * k] for k in range(N_CHIPS)], axis=1) + b_mod
    mod_x = lax.dynamic_index_in_dim(mod_all, dev, 0, keepdims=False).reshape(3, dm)
    mod_c = mod_all[N_DEV].reshape(3, dm)
    prm = jnp.stack([_rows8(list(mod_x), dm), _rows8(list(mod_c), dm)])

    as_t = lambda a: jnp.transpose(a[0])
    half_in = lax.dynamic_slice_in_dim(as_t(w_in).astype(BF16), ci * (dm // 2), dm // 2, 1)
    half_out = lax.dynamic_slice_in_dim(w_out[0].astype(BF16), ci * (di // (2 * N_CHIPS)), di // (2 * N_CHIPS), 0)
    lanes_of = lambda core: pl.ds(core * (dm // 2), dm // 2)
    landing = lambda s, r: (_chip_of(s), slice(None), lanes_of(s[2]))
    own_placed = lax.dynamic_update_slice(jnp.zeros((N_CHIPS + 1, ns, dm), BF16),
                                          as_t(w_in).astype(BF16)[None], (chip, 0, 0))
    gather_in = _Exchange([half_in, own_placed], [jax.ShapeDtypeStruct(own_placed.shape, BF16)],
                          [(mask, 0, None, 0, landing) for mask in CHIP_MASKS], in_place={1: 0})

    hc, (gw_in,) = _modulate_fwd(x[0], ctx[0], prm, tm, rider=gather_in)
    my_lanes = lambda s, r: (slice(0, N_CHIPS), slice(None), lanes_of(s[2]))
    gw_in = _Exchange([gw_in], [jax.ShapeDtypeStruct(gw_in.shape, BF16)],
                      [(SIBLING_MASK, 0, my_lanes, 0, my_lanes)], in_place={0: 0}).run("gather_w_in_pair")[0]
    wt_full = gw_in.reshape((N_CHIPS + 1) * ns, dm)
    assert wt_full.shape[0] >= n_pad
    u, (got_out,) = _mm_nt("in_proj", hc, wt_full, n_pad, tm_mm, tn_mm, F32, rider=_all_gather_chips([half_out]))
    fetched_out = _own_block(chip, half_out, got_out)
    (o_f, o_b), hg_saved, (swapped_out,) = _hg_scan_fwd(u, lb_full, w, *hg_counts,
                                                         rider=_sibling_swap([fetched_out]))
    w_out_full = _join_halves(ci, fetched_out, swapped_out, 1).reshape(di, dm)
    qk = _conv_fwd(u, conv_w9, conv_b, t_rows, c_rows, w, LANE)
    gate_b_row = jnp.pad(ml_gate_b.reshape(1, -1), ((0, 0), (0, LANE - 4 * n_ml)))
    (h_f, h_b), ml_saved = _ml_scan_fwd(qk, u, gate_b_row, w, n_ml, *ml_counts)
    y = _post_fwd(o_f, o_b, h_f, h_b, u, hg_norm_w, ml_norm_w, t_rows, w, n_hg, n_ml, tm)
    prm_out = _rows8([mod_x[2], ln_g, ln_b], dm)
    dz, dy, gx_direct, acc_out = _out_block(y, w_out_full, x[0], loss_target[0], prm_out, tm)

    d_w_out = _mm_tn("d_w_out", y, dz, _largest_divisor(di, 1024, LANE),
                     _largest_divisor(t_rows, 1024, SUBLANE_BF16))
    d_w_out4 = d_w_out.reshape(N_CHIPS, 2, di // (2 * N_CHIPS), dm)
    mine_out = lax.dynamic_index_in_dim(d_w_out4, ci, 1, keepdims=False)
    other_out = lax.dynamic_index_in_dim(d_w_out4, 1 - ci, 1, keepdims=False)
    (d_o, d_h, d_az, d_bo, d_bz, d_wa, d_wb), (got_out,) = _post_bwd(
        o_f, o_b, h_f, h_b, u, hg_norm_w, ml_norm_w, dy, t_rows, w, n_hg, n_ml, tm, rider=_sibling_swap([other_out]))
    pair_out = _sum_pair("rs_pair_sum_w_out", mine_out, got_out)
    (d_aq_f, d_aff, d_ai_f, d_lb_f), (d_aq_b, d_afb, d_ai_b, d_lb_b), (landed_out,) = _hg_scan_bwd(
        u, lb_full, hg_saved, d_o, w, *hg_counts, rider=_chip_scatter([pair_out]))
    half_g_out = _sum_chips("rs_chip_sum_w_out", landed_out, pair_out, chip)
    (d_qk_f, d_v_f, d_g_f, d_gb_f), (d_qk_b, d_v_b, d_g_b, d_gb_b), (sibling_out,) = _ml_scan_bwd(
        qk, u, gate_b_row, ml_saved, d_h, w, n_ml, *ml_counts, rider=_sibling_swap([half_g_out]))
    g_w_out = _join_halves(ci, half_g_out, sibling_out, 0)
    d_bqk, d_cw, d_cb = _conv_bwd(u, (d_qk_f, d_qk_b), conv_w9, conv_b, t_rows, c_rows, w, LANE)
    du = _assemble_du([(d_aq_f, d_aq_b), d_aff, d_afb, (d_ai_f, d_ai_b), d_az, d_bqk, (d_v_f, d_v_b), d_bo, d_bz],
                      (d_g_f, d_g_b), n_pad, tm // 2)
    d_wt_in, d_wt_in_bf16 = _mm_tn("d_w_in", du, hc, tn_mm, tm_mm, with_bf16=True)

    delta, new_m, new_v = {}, {}, {}
    res, (got_in,) = _adamw(
        "adamw_w_out", w_out[0], g_w_out, m_w_out[0], v_w_out[0],
        rider=_Exchange([d_wt_in_bf16], [jax.ShapeDtypeStruct((n_pad, dm // 2), BF16)],
                        [(SIBLING_MASK, 0, lambda s, r: (slice(None), lanes_of(r[2])), 0, None)]))
    delta["w_out"], new_m["w_out"], new_v["w_out"] = (a[None] for a in res)
    pair_half = _sum_pair_lanes("rs_pair_sum_w_in", d_wt_in, got_in, ci)
    pair_in = jnp.stack([pair_half[k * ns:(k + 1) * ns] for k in range(N_CHIPS)])
    d_hc, (landed_in,) = _mm_acc("d_h", du, wt_full, tm_mm, tn_mm, rider=_chip_scatter([pair_in]))
    half_g_in = _sum_chips("rs_chip_sum_w_in", landed_in, pair_in, chip)
    (gx, acc_mod), _ = _modulate_bwd(x[0], ctx[0], d_hc, prm, gx_direct, tm)
    grad_x = gx[None]

    zero_row = jnp.zeros((dm,), F32)
    d_gb = jnp.concatenate([d_gb_f[:, 0:n_ml], d_gb_b[:, n_ml:2 * n_ml], d_gb_f[:, 2 * n_ml:3 * n_ml],
                            d_gb_b[:, 3 * n_ml:4 * n_ml], jnp.zeros((1, dm - 4 * n_ml), F32)], axis=1)
    rows = [acc_mod[0, 0], acc_mod[0, 1], acc_out[OUT_ROW_GATE],
            acc_mod[1, 0], acc_mod[1, 1], zero_row]
    rows += list(d_cw) + [d_cb[0], d_lb_f.reshape(dm), d_lb_b.reshape(dm),
                          jnp.concatenate([d_wa[0], d_wb[0]]), acc_out[OUT_ROW_LN_G], acc_out[OUT_ROW_LN_B],
                          acc_out[OUT_ROW_LOSS], d_gb[0], zero_row]
    ROW_CW, ROW_CB, ROW_LB, ROW_NORM, ROW_LN_G, ROW_LN_B, ROW_LOSS, ROW_GB = 6, 15, 16, 18, 19, 20, 21, 22
    small_rows = jnp.concatenate([r.reshape(dm) for r in rows]).reshape(len(rows), dm)
    g3 = _all_gather8(small_rows).run("gather_small_grads")[0]
    sums, totals = _sum_devices(g3, 3)
    loss = totals[ROW_LOSS, 0]
    dm16 = jnp.concatenate([g3[:, 0:3, :].reshape(N_DEV, 3 * dm), sums[3:6].reshape(1, 3 * dm),
                            jnp.zeros((16 - N_DEV - 1, 3 * dm), F32)])
    (g_w_mod, dc16), (sibling_g_in,) = _mod_bwd(a16, lax.dynamic_slice_in_dim(dm16, chip * nm, nm, 1), w_mod[0],
                                                tn_mod, rider=_sibling_swap([half_g_in]))
    g_wt_in = _join_halves(ci, half_g_in, sibling_g_in, 1)
    g4 = _all_gather8(jnp.pad(dc16[N_DEV:N_DEV + 1], ((0, 7), (0, 0)))).run("gather_c_ctx")[0]
    g_c_ctx = _c_ctx_grad(g4, jnp.broadcast_to(c_ctx[None], (8, dm)))[0]
    res, _ = _adamw("adamw_w_in", as_t(w_in), g_wt_in, as_t(m_w_in), as_t(v_w_in))
    delta["w_in"], new_m["w_in"], new_v["w_in"] = (jnp.transpose(a)[None] for a in res)
    res, _ = _adamw("adamw_w_mod", w_mod[0], g_w_mod, m_w_mod[0], v_w_mod[0])
    delta["w_mod"], new_m["w_mod"], new_v["w_mod"] = (a[None] for a in res)

    chip_cols = lambda a, width: lax.dynamic_slice_in_dim(a, chip * width, width, a.ndim - 1)
    grads = {
        "c_ctx": g_c_ctx,
        "w_mod": g_w_mod[None],
        "b_mod": sums[0:3].reshape(1, 3 * dm),
        "w_in": jnp.transpose(g_wt_in)[None],
        "conv_w": chip_cols(sums[ROW_CW:ROW_CW + 9].reshape(1, 3, 3, di), di // N_CHIPS),
        "conv_b": sums[ROW_CB][None],
        "hg_lb": chip_cols(sums[ROW_LB:ROW_LB + 2].reshape(2, 2, w), w // N_CHIPS),
        "ml_gate_b": sums[ROW_GB, 0:4 * n_ml].reshape(1, 4, n_ml),
        "hg_norm_w": sums[ROW_NORM, 0:w][None],
        "ml_norm_w": sums[ROW_NORM, w:2 * w][None],
        "w_out": g_w_out[None],
        "ln_g": sums[ROW_LN_G][None],
        "ln_b": sums[ROW_LN_B][None],
    }
    weights = dict(c_ctx=c_ctx, w_mod=w_mod, b_mod=b_mod, w_in=w_in, conv_w=conv_w, conv_b=conv_b, hg_lb=hg_lb,
                   ml_gate_b=ml_gate_b, hg_norm_w=hg_norm_w, ml_norm_w=ml_norm_w, w_out=w_out, ln_g=ln_g, ln_b=ln_b)
    mom1 = dict(c_ctx=m_c_ctx, w_mod=m_w_mod, b_mod=m_b_mod, w_in=m_w_in, conv_w=m_conv_w, conv_b=m_conv_b,
                hg_lb=m_hg_lb, ml_gate_b=m_ml_gate_b, hg_norm_w=m_hg_norm_w, ml_norm_w=m_ml_norm_w, w_out=m_w_out,
                ln_g=m_ln_g, ln_b=m_ln_b)
    mom2 = dict(c_ctx=v_c_ctx, w_mod=v_w_mod, b_mod=v_b_mod, w_in=v_w_in, conv_w=v_conv_w, conv_b=v_conv_b,
                hg_lb=v_hg_lb, ml_gate_b=v_ml_gate_b, hg_norm_w=v_hg_norm_w, ml_norm_w=v_ml_norm_w, w_out=v_w_out,
                ln_g=v_ln_g, ln_b=v_ln_b)
    names = list(weights)
    big = ("w_mod", "w_in", "w_out")
    small = [n for n in names if n not in big]

    small_shapes = [weights[n].shape for n in small]
    res, _ = _adamw("adamw_small", *(_pack([src[n] for n in small]) for src in (weights, grads, mom1, mom2)))
    for out, packed in zip((delta, new_m, new_v), res):
        for n, a in zip(small, _unpack(packed, small_shapes)):
            out[n] = a

    return (loss, grad_x, *[grads[n].reshape(weights[n].shape) for n in names], *[delta[n] for n in names],
            *[new_m[n] for n in names], *[new_v[n] for n in names])
```

```python
import functools
import math

import jax
import jax.numpy as jnp
from jax import lax
from jax.experimental import pallas as pl
from jax.experimental.pallas import tpu as pltpu

F32 = jnp.float32
BF16 = jnp.bfloat16
HIGHEST = lax.Precision.HIGHEST
MESH = pl.DeviceIdType.MESH

HG_CHUNK = 64
ML_CHUNK = 256
HG_CHUNKS_PER_STEP = 4
GRID_W = 64
HG_DK = 128
LANE = 128
SUBLANE_BF16 = 16
ALPHA = 2.0 ** 0.25
LN_EPS = 1e-5
NORM_EPS = 1e-6
ADAM_LR = 0.001
ADAM_B1 = 0.9
ADAM_B2 = 0.999
ADAM_EPS = 1e-08
ADAM_WD = 0.01
ADAM_STEP = 10
VMEM_LIMIT = 56 * 1024 * 1024
N_CHIPS = 4
N_DEV = 8


def _params(sem=None):
    return pltpu.CompilerParams(dimension_semantics=sem, vmem_limit_bytes=VMEM_LIMIT)


def _largest_divisor(n, cap, multiple=1):
    best = None
    for d in range(multiple, min(n, cap) + 1, multiple):
        if n % d == 0:
            best = d
    assert best is not None, (n, cap, multiple)
    return best


def _sigmoid(x):
    return jax.nn.sigmoid(x)


def _silu(x):
    return x * jax.nn.sigmoid(x)


def _dot(a, b, dims, precision=None):
    return lax.dot_general(a, b, (dims, ((), ())), precision=precision, preferred_element_type=F32)


def _nn(a, b, precision=None):
    return _dot(a, b, ((1,), (0,)), precision)


def _nt(a, b, precision=None):
    return _dot(a, b, ((1,), (1,)), precision)


def _tn(a, b, precision=None):
    return _dot(a, b, ((0,), (0,)), precision)


def _narrow(x):
    return x.astype(BF16)


@jax.custom_vjp
def _bnn(a, b):
    return _nn(_narrow(a), _narrow(b))


def _bnn_fwd(a, b):
    an, bn = _narrow(a), _narrow(b)
    return _nn(an, bn), (an, bn)


def _bnn_bwd(res, ct):
    an, bn = res
    ctn = _narrow(ct)
    return _nt(ctn, bn), _tn(an, ctn)


_bnn.defvjp(_bnn_fwd, _bnn_bwd)


@jax.custom_vjp
def _bnt(a, b):
    return _nt(_narrow(a), _narrow(b))


def _bnt_fwd(a, b):
    an, bn = _narrow(a), _narrow(b)
    return _nt(an, bn), (an, bn)


def _bnt_bwd(res, ct):
    an, bn = res
    ctn = _narrow(ct)
    return _nn(ctn, bn), _tn(ctn, an)


_bnt.defvjp(_bnt_fwd, _bnt_bwd)


@jax.custom_vjp
def _btn(a, b):
    return _tn(_narrow(a), _narrow(b))


def _btn_fwd(a, b):
    an, bn = _narrow(a), _narrow(b)
    return _tn(an, bn), (an, bn)


def _btn_bwd(res, ct):
    an, bn = res
    ctn = _narrow(ct)
    return _nt(bn, ctn), _nn(an, ctn)


_btn.defvjp(_btn_fwd, _btn_bwd)


def _visible(n, rev):
    r = lax.broadcasted_iota(jnp.int32, (n, n), 0)
    c = lax.broadcasted_iota(jnp.int32, (n, n), 1)
    return (r <= c) if rev else (r >= c)


def _mask_matmul(mask, x):
    mb = mask.astype(BF16)
    hi = x.astype(BF16)
    lo = (x - hi.astype(F32)).astype(BF16)
    return _nn(mb, hi) + _nn(mb, lo)


@functools.partial(jax.custom_vjp, nondiff_argnums=(1,))
def _cumulative(x, rev):
    return _mask_matmul(_visible(x.shape[0], rev), x)


def _cumulative_fwd(x, rev):
    return _cumulative(x, rev), None


def _cumulative_bwd(rev, _, ct):
    return (_mask_matmul(_visible(ct.shape[0], not rev), ct),)


_cumulative.defvjp(_cumulative_fwd, _cumulative_bwd)


def _hg_chunk(states, aq, af, ai, lb0, lb1, rev):
    n_heads = len(states)
    lb = _sigmoid(lb0 - lb1)
    f = lb + (1.0 - lb) * _sigmoid(af)
    g = jnp.log(f)
    k = 1.0 - f
    q = _silu(aq)
    chunk = aq.shape[0]
    vis = _visible(chunk, rev)
    b = _cumulative(g, rev)
    last = 0 if rev else chunk - 1
    b_end = b[last:last + 1]
    b_mid = b[chunk // 2:chunk // 2 + 1]
    q_inter = q * jnp.exp(b)
    q_intra = q * jnp.exp(b - b_mid)
    k_intra = k * jnp.exp(b_mid - b)
    k_dec = k * jnp.exp(b_end - b)
    e_end = jnp.exp(b_end)
    new_states, outs = [], []
    for h in range(n_heads):
        sl = slice(h * HG_DK, (h + 1) * HG_DK)
        s_t = states[h]
        scores = jnp.where(vis, _nt(q_intra[:, sl], k_intra[:, sl]), 0.0)
        outs.append(_nt(q_inter[:, sl], s_t) + _nn(scores, ai[:, sl]))
        new_states.append(e_end[:, sl] * s_t + _tn(ai[:, sl], k_dec[:, sl]))
    return new_states, jnp.concatenate(outs, axis=1)


def _ml_chunk(state, q, k, v, g, gb, rev, d):
    cms, nvs, mbs = state
    n_heads = len(cms)
    dh = q.shape[1] // n_heads
    ga = g + gb
    log_f_all = jax.nn.log_sigmoid(ga)
    chunk = q.shape[0]
    vis = _visible(chunk, rev)
    b_all = _cumulative(log_f_all, rev)
    last = 0 if rev else chunk - 1
    k = k * (dh ** -0.5)
    new_c, new_n, new_m, outs = [], [], [], []
    for h in range(n_heads):
        ci = d * n_heads + h
        cf = (2 + d) * n_heads + h
        sl = slice(h * dh, (h + 1) * dh)
        qh, kh, vh = q[:, sl], k[:, sl], v[:, sl]
        li = ga[:, ci:ci + 1]
        b = b_all[:, cf:cf + 1]
        m = mbs[h][:, 0:1]
        row = jnp.transpose(li - b)
        log_w = jnp.where(vis, b + row, -jnp.inf)
        m_inter = b + m
        m_t = jnp.maximum(m_inter, jnp.max(log_w, axis=-1, keepdims=True))
        w_inter = jnp.exp(m_inter - m_t)
        w_qk = jnp.exp(log_w - m_t) * _bnt(qh, kh)
        num = w_inter * _bnt(qh, cms[h]) + _bnn(w_qk, vh)
        den = w_inter * jnp.sum(qh * nvs[h], axis=-1, keepdims=True) + jnp.sum(w_qk, axis=-1, keepdims=True)
        outs.append(num / jnp.maximum(jnp.abs(den), jnp.exp(-m_t)))
        m_new = m_t[last:last + 1]
        b_end = b[last:last + 1]
        w_s = jnp.exp(b_end - b + li - m_new)
        decay = jnp.exp(b_end + m - m_new)
        new_c.append(decay * cms[h] + _btn(w_s * vh, kh))
        new_n.append(decay * nvs[h] + jnp.sum(w_s * kh, axis=0, keepdims=True))
        new_m.append(jnp.broadcast_to(m_new, (1, LANE)))
    return (new_c, new_n, new_m), jnp.concatenate(outs, axis=1)


def _post_fn(o_f, o_b, az, h_f, h_b, bo, bz, wa, wb, n_hg, n_ml):
    o = o_f + o_b
    parts = []
    for h in range(n_hg):
        s = o[:, h * HG_DK:(h + 1) * HG_DK]
        parts.append(s * lax.rsqrt(jnp.mean(s * s, axis=-1, keepdims=True) + NORM_EPS))
    y_a = jnp.concatenate(parts, axis=1) * wa * _silu(az)
    hh = h_f + h_b
    dh = hh.shape[1] // n_ml
    parts = []
    for h in range(n_ml):
        s = hh[:, h * dh:(h + 1) * dh]
        mu = jnp.mean(s, axis=-1, keepdims=True)
        sc = s - mu
        parts.append(sc * lax.rsqrt(jnp.mean(sc * sc, axis=-1, keepdims=True) + NORM_EPS))
    y_b = jnp.concatenate(parts, axis=1) * wb * _sigmoid(bo) * _silu(bz)
    return jnp.concatenate([y_a, y_b], axis=1)


def _chip_of(dev):
    return 2 * dev[0] + dev[1]


def _index_of(dev):
    return 4 * dev[0] + 2 * dev[1] + dev[2]


class _Exchange:
    def __init__(self, srcs, out_shapes, transfers, local_copies=(), in_place=None):
        self.srcs, self.out_shapes = list(srcs), list(out_shapes)
        self.transfers, self.local_copies = list(transfers), list(local_copies)
        self.in_place = dict(in_place or {})

    def scratch(self):
        return [pltpu.SemaphoreType.DMA((len(self.transfers),)), pltpu.SemaphoreType.DMA((len(self.transfers),)),
                pltpu.SemaphoreType.DMA((max(len(self.local_copies), 1),))]

    def copies(self, ins, outs, send_sems, recv_sems, local_sems):
        me = (lax.axis_index("x"), lax.axis_index("y"), lax.axis_index("c"))

        def pick(ref, fn, *who):
            return ref if fn is None else ref.at[fn(*who)]

        sends, recvs, locs = [], [], []
        for t, (mask, si, sfn, di, dfn) in enumerate(self.transfers):
            peer = tuple(1 - p if flip else p for p, flip in zip(me, mask))
            sends.append(pltpu.make_async_remote_copy(
                src_ref=pick(ins[si], sfn, me, peer), dst_ref=pick(outs[di], dfn, me, peer),
                send_sem=send_sems.at[t], recv_sem=recv_sems.at[t], device_id=peer, device_id_type=MESH))
            landing = pick(outs[di], dfn, peer, me)
            recvs.append(pltpu.make_async_remote_copy(
                src_ref=landing, dst_ref=landing,
                send_sem=send_sems.at[t], recv_sem=recv_sems.at[t], device_id=peer, device_id_type=MESH))
        for l, (si, sfn, di, dfn) in enumerate(self.local_copies):
            locs.append(pltpu.make_async_copy(pick(ins[si], sfn, me), pick(outs[di], dfn, me), local_sems.at[l]))

        def start():
            for cp in locs + sends:
                cp.start()

        def wait():
            for cp in recvs:
                cp.wait_recv()
            for cp in sends:
                cp.wait_send()
            for cp in locs:
                cp.wait()

        return start, wait

    def run(self, name):
        n_in, n_out = len(self.srcs), len(self.out_shapes)

        def body(*refs):
            start, wait = self.copies(refs[:n_in], refs[n_in:n_in + n_out], *refs[n_in + n_out:])
            start()
            wait()

        hbm = pl.BlockSpec(memory_space=pltpu.HBM)
        return pl.pallas_call(
            body, name=name, out_shape=tuple(self.out_shapes), in_specs=[hbm] * n_in,
            out_specs=tuple([hbm] * n_out), scratch_shapes=self.scratch(), input_output_aliases=self.in_place,
        )(*self.srcs)


def _call(body, operands, *, name, grid, in_specs, out_specs, out_shape, scratch_shapes=(), sem=None, rider=None):
    out_specs, out_shape, scratch_shapes = list(out_specs), list(out_shape), list(scratch_shapes)
    if rider is None:
        res = pl.pallas_call(
            body, name=name, grid=grid, in_specs=list(in_specs), out_specs=tuple(out_specs),
            out_shape=tuple(out_shape), scratch_shapes=scratch_shapes, compiler_params=_params(sem),
        )(*operands)
        return list(res), []
    counts = (len(in_specs), len(rider.srcs), len(out_specs), len(rider.out_shapes), len(scratch_shapes), 3)

    def full(*refs):
        groups, pos = [], 0
        for k in counts:
            groups.append(refs[pos:pos + k])
            pos += k
        own_in, ex_in, own_out, ex_out, own_scr, ex_scr = groups
        ids = [pl.program_id(a) for a in range(len(grid))]
        first = functools.reduce(jnp.logical_and, [i == 0 for i in ids])
        last = functools.reduce(jnp.logical_and, [i == g - 1 for i, g in zip(ids, grid)])
        start, wait = rider.copies(ex_in, ex_out, *ex_scr)
        pl.when(first)(start)
        body(*own_in, *own_out, *own_scr)
        pl.when(last)(wait)

    hbm = pl.BlockSpec(memory_space=pltpu.HBM)
    res = pl.pallas_call(
        full, name=name, grid=grid, in_specs=list(in_specs) + [hbm] * counts[1],
        out_specs=tuple(out_specs + [hbm] * counts[3]), out_shape=tuple(out_shape + rider.out_shapes),
        scratch_shapes=scratch_shapes + rider.scratch(), compiler_params=_params(("arbitrary",) * len(grid)),
        input_output_aliases={counts[0] + i: counts[2] + o for i, o in rider.in_place.items()},
    )(*operands, *rider.srcs)
    return list(res[:counts[2]]), list(res[counts[2]:])


ALL_MASKS = [(mx, my, mc) for mx in (0, 1) for my in (0, 1) for mc in (0, 1)][1:]
CHIP_MASKS = [(1, 0, 0), (0, 1, 0), (1, 1, 0)]
SIBLING_MASK = (0, 0, 1)


def _all_gather8(v):
    out = jax.ShapeDtypeStruct((N_DEV,) + v.shape, v.dtype)
    slot = lambda sender, receiver: _index_of(sender)
    transfers = [(mask, 0, None, 0, slot) for mask in ALL_MASKS]
    return _Exchange([v], [out], transfers, [(0, None, 0, lambda me: _index_of(me))])


def _all_gather_chips(arrays):
    outs = [jax.ShapeDtypeStruct((N_CHIPS,) + a.shape, a.dtype) for a in arrays]
    slot = lambda sender, receiver: _chip_of(sender)
    return _Exchange(arrays, outs, [(mask, i, None, i, slot) for i in range(len(arrays)) for mask in CHIP_MASKS])


def _sibling_swap(arrays):
    outs = [jax.ShapeDtypeStruct(a.shape, a.dtype) for a in arrays]
    return _Exchange(arrays, outs, [(SIBLING_MASK, i, None, i, None) for i in range(len(arrays))])


def _chip_scatter(arrays):
    outs = [jax.ShapeDtypeStruct(a.shape, a.dtype) for a in arrays]
    transfers = [(mask, i, lambda s, r: _chip_of(r), i, lambda s, r: _chip_of(s))
                 for i in range(len(arrays)) for mask in CHIP_MASKS]
    return _Exchange(arrays, outs, transfers)


def _own_block(chip, own, blocks):
    sel = (lax.broadcasted_iota(jnp.int32, (N_CHIPS,) + (1,) * (blocks.ndim - 1), 0) == chip)
    return jnp.where(sel, own if own.ndim == blocks.ndim else own[None], blocks)


def _join_halves(ci, mine, other, axis):
    return jnp.where(ci == 0, jnp.concatenate([mine, other], axis=axis), jnp.concatenate([other, mine], axis=axis))


def _mm_nt(name, a, b, n, tm, tn, out_dtype, rider=None):
    m, k = a.shape

    def body(a_ref, b_ref, o_ref):
        o_ref[...] = _nt(a_ref[...], b_ref[...]).astype(out_dtype)

    (out,), rode = _call(
        body, (a, b), name=name, grid=(n // tn, m // tm),
        in_specs=[pl.BlockSpec((tm, k), lambda j, i: (i, 0)), pl.BlockSpec((tn, k), lambda j, i: (j, 0))],
        out_specs=[pl.BlockSpec((tm, tn), lambda j, i: (i, j))],
        out_shape=[jax.ShapeDtypeStruct((m, n), out_dtype)], sem=("parallel", "parallel"), rider=rider)
    return out, rode


def _mm_acc(name, a, b, tm, tk, rider=None):
    m, kc = a.shape
    n = b.shape[1]

    def body(a_ref, b_ref, o_ref):
        @pl.when(pl.program_id(1) == 0)
        def _():
            o_ref[...] = jnp.zeros_like(o_ref)
        o_ref[...] += _nn(a_ref[...], b_ref[...])

    (out,), rode = _call(
        body, (a, b), name=name, grid=(m // tm, kc // tk),
        in_specs=[pl.BlockSpec((tm, tk), lambda i, kk: (i, kk)), pl.BlockSpec((tk, n), lambda i, kk: (kk, 0))],
        out_specs=[pl.BlockSpec((tm, n), lambda i, kk: (i, 0))],
        out_shape=[jax.ShapeDtypeStruct((m, n), F32)], sem=("parallel", "arbitrary"), rider=rider)
    return out, rode


def _mm_tn(name, a, b, tm, tk, with_bf16=False):
    kr, m = a.shape
    n = b.shape[1]
    steps_k = kr // tk

    def body(a_ref, b_ref, o_ref, *narrow):
        @pl.when(pl.program_id(1) == 0)
        def _():
            o_ref[...] = jnp.zeros_like(o_ref)
        o_ref[...] += _tn(a_ref[...], b_ref[...])
        if with_bf16:
            @pl.when(pl.program_id(1) == steps_k - 1)
            def _():
                narrow[0][...] = o_ref[...].astype(BF16)

    out_spec = pl.BlockSpec((tm, n), lambda i, kk: (i, 0))
    res = pl.pallas_call(
        body, name=name, grid=(m // tm, steps_k),
        in_specs=[pl.BlockSpec((tk, tm), lambda i, kk: (kk, i)), pl.BlockSpec((tk, n), lambda i, kk: (kk, 0))],
        out_specs=(out_spec,) * (2 if with_bf16 else 1),
        out_shape=(jax.ShapeDtypeStruct((m, n), F32),) + ((jax.ShapeDtypeStruct((m, n), BF16),) if with_bf16 else ()),
        compiler_params=_params(("parallel", "arbitrary")),
    )(a, b)
    return res if with_bf16 else res[0]


def _modulate_fwd(x, ctx, prm, tm, rider=None):
    t_rows, dm = x.shape
    lat = t_rows // tm
    r = t_rows + ctx.shape[0]

    def body(x_ref, c_ref, p_ref, h_ref):
        xv = jnp.where(pl.program_id(0) >= lat, c_ref[...], x_ref[...])
        mu = jnp.mean(xv, axis=-1, keepdims=True)
        xm = xv - mu
        n = xm * lax.rsqrt(jnp.mean(xm * xm, axis=-1, keepdims=True) + LN_EPS)
        h_ref[...] = (n * (1.0 + p_ref[0, 1:2, :]) + p_ref[0, 0:1, :]).astype(BF16)

    (h,), rode = _call(
        body, (x, ctx, prm), name="modulate_fwd", grid=(r // tm,),
        in_specs=[pl.BlockSpec((tm, dm), lambda i: (jnp.minimum(i, lat - 1), 0)),
                  pl.BlockSpec((tm, dm), lambda i: (jnp.maximum(i - lat, 0), 0)),
                  pl.BlockSpec((1, 8, dm), lambda i: ((i >= lat).astype(jnp.int32), 0, 0))],
        out_specs=[pl.BlockSpec((tm, dm), lambda i: (i, 0))],
        out_shape=[jax.ShapeDtypeStruct((r, dm), BF16)], sem=("parallel",), rider=rider)
    return h, rode


def _modulate_bwd(x, ctx, dh, prm, gx_direct, tm, rider=None):
    t_rows, dm = x.shape
    lat, n_ct = t_rows // tm, ctx.shape[0] // tm
    is_ctx = lambda i: i < n_ct
    cls = lambda i: is_ctx(i).astype(jnp.int32)
    lat_tile = lambda i: (jnp.maximum(i - n_ct, 0), 0)

    def body(x_ref, c_ref, dh_ref, p_ref, gd_ref, gx_ref, acc_ref):
        i = pl.program_id(0)

        @pl.when((i == 0) | (i == n_ct))
        def _():
            acc_ref[...] = jnp.zeros_like(acc_ref)

        x = jnp.where(is_ctx(i), c_ref[...], x_ref[...])
        dh_v = dh_ref[...]
        mu = jnp.mean(x, axis=-1, keepdims=True)
        xm = x - mu
        rstd = lax.rsqrt(jnp.mean(xm * xm, axis=-1, keepdims=True) + LN_EPS)
        n = xm * rstd
        acc_ref[0, 0:1, :] += jnp.sum(dh_v, axis=0, keepdims=True)
        acc_ref[0, 1:2, :] += jnp.sum(dh_v * n, axis=0, keepdims=True)
        dn = dh_v * (1.0 + p_ref[0, 1:2, :])
        dx = rstd * (dn - jnp.mean(dn, axis=-1, keepdims=True) - n * jnp.mean(dn * n, axis=-1, keepdims=True))
        gx_ref[...] = dx + gd_ref[...]

    return _call(
        body, (x, ctx, dh, prm, gx_direct), name="modulate_bwd", grid=(n_ct + lat,),
        in_specs=[pl.BlockSpec((tm, dm), lat_tile),
                  pl.BlockSpec((tm, dm), lambda i: (jnp.minimum(i, n_ct - 1), 0)),
                  pl.BlockSpec((tm, dm), lambda i: (jnp.where(is_ctx(i), lat + i, i - n_ct), 0)),
                  pl.BlockSpec((1, 8, dm), lambda i: (cls(i), 0, 0)),
                  pl.BlockSpec((tm, dm), lat_tile)],
        out_specs=(pl.BlockSpec((tm, dm), lat_tile), pl.BlockSpec((1, 8, dm), lambda i: (cls(i), 0, 0))),
        out_shape=(jax.ShapeDtypeStruct((t_rows, dm), F32), jax.ShapeDtypeStruct((2, 8, dm), F32)),
        sem=("arbitrary",), rider=rider)


def _conv_parts(t_rows, c_rows):
    return ((0, t_rows, t_rows // GRID_W, GRID_W), (t_rows, c_rows, 1, c_rows))


def _col_shifts(x2, rows_g, width_g):
    n, ct = x2.shape
    col = lax.broadcasted_iota(jnp.int32, (width_g, ct), 0)
    as_grid = lambda a: a.reshape(rows_g, width_g, ct)
    left = as_grid(pltpu.roll(x2, 1, 0)) * (col >= 1).astype(F32)
    right = as_grid(pltpu.roll(x2, n - 1, 0)) * (col <= width_g - 2).astype(F32)
    return [left, as_grid(x2), right]


CONV_BLOCK_ROWS = 4


def _conv_blocks(t_rows, c_rows):
    for t0, _, rows_g, width_g in _conv_parts(t_rows, c_rows):
        nb = min(CONV_BLOCK_ROWS, rows_g)
        assert rows_g % nb == 0
        for g0 in range(0, rows_g, nb):
            yield t0, rows_g, width_g, g0, nb


def _slab(ref, t0, rows_g, width_g, g0, nb):
    if rows_g == 1:
        return ref[t0:t0 + width_g, :]
    lo, hi = max(g0 - 1, 0), min(g0 + nb + 1, rows_g)
    parts = [ref[t0 + lo * width_g:t0 + hi * width_g, :]]
    zero = jnp.zeros((width_g, ref.shape[1]), F32)
    if g0 == 0:
        parts.insert(0, zero)
    if g0 + nb == rows_g:
        parts.append(zero)
    return jnp.concatenate(parts, axis=0)


def _conv_taps(cols, w_ref, nb, flip):
    one_row = cols[0].shape[0] == nb
    acc = None
    for a in range(3):
        if one_row and a != 1:
            continue
        for b in range(3):
            tap = (2 - a) * 3 + (2 - b) if flip else a * 3 + b
            term = (cols[b] if one_row else cols[b][a:a + nb]) * w_ref[tap:tap + 1, :]
            acc = term if acc is None else acc + term
    return acc


def _conv_fwd(u, conv_w9, conv_b, t_rows, c_rows, w, ct):
    r = u.shape[0]
    base = 5 * w // ct

    def body(x_ref, w_ref, b_ref, o_ref):
        for t0, rows_g, width_g, g0, nb in _conv_blocks(t_rows, c_rows):
            slab = _slab(x_ref, t0, rows_g, width_g, g0, nb)
            cols = _col_shifts(slab, slab.shape[0] // width_g, width_g)
            pre = _conv_taps(cols, w_ref, nb, False) + b_ref[...]
            o_ref[t0 + g0 * width_g:t0 + (g0 + nb) * width_g, :] = _silu(pre).reshape(nb * width_g, ct)

    return pl.pallas_call(
        body, name="conv_fwd", grid=(2 * w // ct,),
        in_specs=[pl.BlockSpec((r, ct), lambda i: (0, base + i)), pl.BlockSpec((9, ct), lambda i: (0, i)),
                  pl.BlockSpec((1, ct), lambda i: (0, i))],
        out_specs=pl.BlockSpec((r, ct), lambda i: (0, i)),
        out_shape=jax.ShapeDtypeStruct((r, 2 * w), F32),
        compiler_params=_params(("parallel",)),
    )(u, conv_w9, conv_b)


def _conv_bwd(u, dqk_pair, conv_w9, conv_b, t_rows, c_rows, w, ct):
    r = u.shape[0]
    base = 5 * w // ct

    def body(x_ref, d1_ref, d2_ref, w_ref, b_ref, dx_ref, dw_ref, db_ref, dpre_ref):
        dw = [jnp.zeros((1, ct), F32) for _ in range(9)]
        db = jnp.zeros((1, ct), F32)
        for t0, rows_g, width_g, g0, nb in _conv_blocks(t_rows, c_rows):
            rows = slice(t0 + g0 * width_g, t0 + (g0 + nb) * width_g)
            slab = _slab(x_ref, t0, rows_g, width_g, g0, nb)
            cols = _col_shifts(slab, slab.shape[0] // width_g, width_g)
            pre = _conv_taps(cols, w_ref, nb, False) + b_ref[...]
            sg = _sigmoid(pre)
            dpre = (d1_ref[rows, :] + d2_ref[rows, :]).reshape(pre.shape) * (sg * (1.0 + pre * (1.0 - sg)))
            dpre_ref[rows, :] = dpre.reshape(nb * width_g, ct)
            db = db + jnp.sum(jnp.sum(dpre, axis=0), axis=0, keepdims=True)
            for a in range(3):
                if rows_g == 1 and a != 1:
                    continue
                for b in range(3):
                    moved = cols[b] if rows_g == 1 else cols[b][a:a + nb]
                    dw[a * 3 + b] = dw[a * 3 + b] + jnp.sum(jnp.sum(moved * dpre, axis=0), axis=0, keepdims=True)
        for t0, rows_g, width_g, g0, nb in _conv_blocks(t_rows, c_rows):
            slab = _slab(dpre_ref, t0, rows_g, width_g, g0, nb)
            cols = _col_shifts(slab, slab.shape[0] // width_g, width_g)
            dx_ref[t0 + g0 * width_g:t0 + (g0 + nb) * width_g, :] = _conv_taps(cols, w_ref, nb, True).reshape(
                nb * width_g, ct).astype(BF16)
        for tap in range(9):
            dw_ref[tap:tap + 1, :] = dw[tap]
        db_ref[...] = db

    return pl.pallas_call(
        body, name="conv_bwd", grid=(2 * w // ct,),
        in_specs=[pl.BlockSpec((r, ct), lambda i: (0, base + i)), pl.BlockSpec((r, ct), lambda i: (0, i)),
                  pl.BlockSpec((r, ct), lambda i: (0, i)),
                  pl.BlockSpec((9, ct), lambda i: (0, i)), pl.BlockSpec((1, ct), lambda i: (0, i))],
        out_specs=(pl.BlockSpec((r, ct), lambda i: (0, i)), pl.BlockSpec((9, ct), lambda i: (0, i)),
                   pl.BlockSpec((1, ct), lambda i: (0, i))),
        out_shape=(jax.ShapeDtypeStruct((r, 2 * w), BF16), jax.ShapeDtypeStruct((9, 2 * w), F32),
                   jax.ShapeDtypeStruct((1, 2 * w), F32)),
        scratch_shapes=[pltpu.VMEM((r, ct), F32)],
        compiler_params=_params(("parallel",)),
    )(u, dqk_pair[0], dqk_pair[1], conv_w9, conv_b)


def _assemble_du(groups, gates, n_pad, tm):
    flat, layout = [], []
    for entry in list(groups) + [gates]:
        parts = entry if isinstance(entry, (tuple, list)) else (entry,)
        layout.append((len(flat), len(parts), parts[0].shape[1]))
        flat += list(parts)
    r = flat[0].shape[0]

    def body(*refs):
        o_ref = refs[-1]
        col = 0
        for first, count, width in layout:
            val = refs[first][...]
            for extra in range(1, count):
                val = val.astype(F32) + refs[first + extra][...].astype(F32)
            o_ref[:, col:col + width] = val.astype(BF16)
            col += width
        assert col == n_pad

    return pl.pallas_call(
        body, name="assemble_du", grid=(r // tm,),
        in_specs=[pl.BlockSpec((tm, a.shape[1]), lambda i: (i, 0)) for a in flat],
        out_specs=pl.BlockSpec((tm, n_pad), lambda i: (i, 0)),
        out_shape=jax.ShapeDtypeStruct((r, n_pad), BF16),
        compiler_params=_params(("parallel",)),
    )(*flat)


def _scan_order(n_lat, n_ctx, rev):
    n = n_lat + n_ctx
    if rev:
        return lambda j: n - 1 - j
    return lambda j: (j + n_lat) % n


DIRS = (False, True)


def _hg_scan_fwd(u, lb_full, w, n_lat, n_ctx, chunk, rider=None):
    r = u.shape[0]
    n_heads = w // HG_DK
    sub = HG_CHUNKS_PER_STEP if n_lat % HG_CHUNKS_PER_STEP == 0 and n_ctx % HG_CHUNKS_PER_STEP == 0 else 1
    n_steps = (n_lat + n_ctx) // sub
    nat = [_scan_order(n_lat // sub, n_ctx // sub, rev) for rev in DIRS]
    rows = sub * chunk

    def body(*refs):
        ins, outs, scratch = refs[:8], refs[8:12], refs[12:]

        @pl.when(pl.program_id(0) == 0)
        def _():
            for s_ref in scratch:
                s_ref[...] = jnp.zeros_like(s_ref)

        for d, rev in enumerate(DIRS):
            aq, af, ai, lb_ref = ins[4 * d:4 * d + 4]
            o_ref, save_ref = outs[2 * d:2 * d + 2]
            state = [scratch[d][h] for h in range(n_heads)]
            for p in range(sub):
                sl = slice((sub - 1 - p if rev else p) * chunk, (sub - p if rev else p + 1) * chunk)
                for h in range(n_heads):
                    save_ref[0, p, h] = state[h]
                state, o = _hg_chunk(state, aq[sl, :], af[sl, :], ai[sl, :], lb_ref[0, 0:1, :], lb_ref[0, 1:2, :], rev)
                o_ref[sl, :] = o
            for h in range(n_heads):
                scratch[d][h] = state[h]

    in_specs, out_specs, out_shape = [], [], []
    for d in range(2):
        in_specs += [pl.BlockSpec((rows, w), lambda j, d=d: (nat[d](j), 0)),
                     pl.BlockSpec((rows, w), lambda j, d=d: (nat[d](j), 1 + d)),
                     pl.BlockSpec((rows, w), lambda j, d=d: (nat[d](j), 3)),
                     pl.BlockSpec((1, 2, w), lambda j, d=d: (d, 0, 0))]
        out_specs += [pl.BlockSpec((rows, w), lambda j, d=d: (nat[d](j), 0)),
                      pl.BlockSpec((1, sub, n_heads, HG_DK, HG_DK), lambda j: (j, 0, 0, 0, 0))]
        out_shape += [jax.ShapeDtypeStruct((r, w), F32),
                      jax.ShapeDtypeStruct((n_steps, sub, n_heads, HG_DK, HG_DK), F32)]
    (o_f, s_f, o_b, s_b), rode = _call(
        body, (u, u, u, lb_full, u, u, u, lb_full), name="hg_scan_fwd", grid=(n_steps,), in_specs=in_specs,
        out_specs=out_specs, out_shape=out_shape, scratch_shapes=[pltpu.VMEM((n_heads, HG_DK, HG_DK), F32)] * 2,
        sem=("arbitrary",), rider=rider)
    return (o_f, o_b), (s_f, s_b), rode


def _hg_scan_bwd(u, lb_full, saved, d_o, w, n_lat, n_ctx, chunk, rider=None):
    r = u.shape[0]
    n_heads = w // HG_DK
    n_steps, sub = saved[0].shape[0], saved[0].shape[1]
    n_lat_s = n_lat // sub
    step = lambda jj: n_steps - 1 - jj
    nat = [(lambda jj, o=_scan_order(n_lat_s, n_ctx // sub, rev): o(step(jj))) for rev in DIRS]
    rows = sub * chunk

    def body(*refs):
        ins, outs, scratch = refs[:12], refs[12:20], refs[20:]
        jj = pl.program_id(0)

        @pl.when(jj == 0)
        def _():
            for d in range(2):
                scratch[d][...] = jnp.zeros_like(scratch[d])
                outs[4 * d + 3][...] = jnp.zeros_like(outs[4 * d + 3])

        for d, rev in enumerate(DIRS):
            aq, af, ai, lb_ref, save_ref, do_ref = ins[6 * d:6 * d + 6]
            daq_ref, daf_ref, dai_ref, dlb_ref = outs[4 * d:4 * d + 4]
            f = lambda st, a, b, c, l0, l1, rev=rev: _hg_chunk(st, a, b, c, l0, l1, rev)
            latent = (nat[d](jj) < n_lat_s).astype(F32)
            d_state = [scratch[d][h] for h in range(n_heads)]
            for p in reversed(range(sub)):
                sl = slice((sub - 1 - p if rev else p) * chunk, (sub - p if rev else p + 1) * chunk)
                _, vjp = jax.vjp(f, [save_ref[0, p, h] for h in range(n_heads)], aq[sl, :], af[sl, :], ai[sl, :],
                                 lb_ref[0, 0:1, :], lb_ref[0, 1:2, :])
                d_state, daq, daf, dai, dl0, dl1 = vjp((d_state, do_ref[sl, :] * latent))
                daq_ref[sl, :] = daq.astype(BF16)
                daf_ref[sl, :] = daf.astype(BF16)
                dai_ref[sl, :] = dai.astype(BF16)
                dlb_ref[0:1, :] += dl0
                dlb_ref[1:2, :] += dl1
            for h in range(n_heads):
                scratch[d][h] = d_state[h]

    in_specs, out_specs, out_shape, operands = [], [], [], []
    for d in range(2):
        row = lambda jj, d=d: (nat[d](jj), 0)
        in_specs += [pl.BlockSpec((rows, w), row),
                     pl.BlockSpec((rows, w), lambda jj, d=d: (nat[d](jj), 1 + d)),
                     pl.BlockSpec((rows, w), lambda jj, d=d: (nat[d](jj), 3)),
                     pl.BlockSpec((1, 2, w), lambda jj, d=d: (d, 0, 0)),
                     pl.BlockSpec((1, sub, n_heads, HG_DK, HG_DK), lambda jj: (step(jj), 0, 0, 0, 0)),
                     pl.BlockSpec((rows, w), lambda jj, d=d: (jnp.minimum(nat[d](jj), n_lat_s - 1), 0))]
        operands += [u, u, u, lb_full, saved[d], d_o]
        out_specs += [pl.BlockSpec((rows, w), row)] * 3 + [pl.BlockSpec((2, w), lambda jj: (0, 0))]
        out_shape += [jax.ShapeDtypeStruct((r, w), BF16)] * 3 + [jax.ShapeDtypeStruct((2, w), F32)]
    res, rode = _call(
        body, operands, name="hg_scan_bwd", grid=(n_steps,), in_specs=in_specs, out_specs=out_specs,
        out_shape=out_shape, scratch_shapes=[pltpu.VMEM((n_heads, HG_DK, HG_DK), F32)] * 2,
        sem=("arbitrary",), rider=rider)
    return res[0:4], res[4:8], rode


def _ml_state_shapes(n_chunks, n_heads, dh):
    return (jax.ShapeDtypeStruct((n_chunks, n_heads, dh, dh), F32),
            jax.ShapeDtypeStruct((n_chunks, n_heads, 1, dh), F32),
            jax.ShapeDtypeStruct((n_chunks, n_heads, 1, LANE), F32))


def _ml_state_specs(n_heads, dh, index):
    return (pl.BlockSpec((1, n_heads, dh, dh), lambda j: (index(j), 0, 0, 0)),
            pl.BlockSpec((1, n_heads, 1, dh), lambda j: (index(j), 0, 0, 0)),
            pl.BlockSpec((1, n_heads, 1, LANE), lambda j: (index(j), 0, 0, 0)))


def _ml_state_scratch(n_heads, dh):
    return [pltpu.VMEM((n_heads, dh, dh), F32), pltpu.VMEM((n_heads, 1, dh), F32), pltpu.VMEM((n_heads, 1, LANE), F32)]


def _ml_scan_fwd(qk, u, gate_b, w, n_heads, n_lat, n_ctx, chunk):
    r = u.shape[0]
    dh = w // n_heads
    n_chunks = n_lat + n_ctx
    nat = [_scan_order(n_lat, n_ctx, rev) for rev in DIRS]

    def body(*refs):
        ins, outs, scratch = refs[:10], refs[10:18], refs[18:]

        @pl.when(pl.program_id(0) == 0)
        def _():
            for s_ref in scratch:
                s_ref[...] = jnp.zeros_like(s_ref)

        results = []
        for d, rev in enumerate(DIRS):
            q, k, v, g, gb = ins[5 * d:5 * d + 5]
            state = tuple([ref[h] for h in range(n_heads)] for ref in scratch[3 * d:3 * d + 3])
            results.append((state, _ml_chunk(state, q[...], k[...], v[...], g[...], gb[...], rev, d)))
        for d, (state, (new, o)) in enumerate(results):
            outs[4 * d][...] = o
            for part in range(3):
                for h in range(n_heads):
                    outs[4 * d + 1 + part][0, h] = state[part][h]
                    scratch[3 * d + part][h] = new[part][h]

    in_specs, out_specs, out_shape = [], [], []
    for d in range(2):
        in_specs += [pl.BlockSpec((chunk,w), lambda j, d=d: (nat[d](j), 0)),
                     pl.BlockSpec((chunk,w), lambda j, d=d: (nat[d](j), 1)),
                     pl.BlockSpec((chunk,w), lambda j, d=d: (nat[d](j), 7)),
                     pl.BlockSpec((chunk,LANE), lambda j, d=d: (nat[d](j), 10 * w // LANE)),
                     pl.BlockSpec((1, LANE), lambda j: (0, 0))]
        out_specs += [pl.BlockSpec((chunk,w), lambda j, d=d: (nat[d](j), 0))]
        out_specs += list(_ml_state_specs(n_heads, dh, lambda j: j))
        out_shape += [jax.ShapeDtypeStruct((r, w), F32)] + list(_ml_state_shapes(n_chunks, n_heads, dh))
    res = pl.pallas_call(
        body, name="ml_scan_fwd", grid=(n_chunks,), in_specs=in_specs, out_specs=tuple(out_specs),
        out_shape=tuple(out_shape), scratch_shapes=_ml_state_scratch(n_heads, dh) * 2,
        compiler_params=_params(("arbitrary",)),
    )(qk, qk, u, u, gate_b, qk, qk, u, u, gate_b)
    return (res[0], res[4]), (res[1:4], res[5:8])


def _ml_scan_bwd(qk, u, gate_b, saved, d_h, w, n_heads, n_lat, n_ctx, chunk, rider=None):
    r = u.shape[0]
    dh = w // n_heads
    n_chunks = n_lat + n_ctx
    step = lambda jj: n_chunks - 1 - jj
    nat = [(lambda jj, o=_scan_order(n_lat, n_ctx, rev): o(step(jj))) for rev in DIRS]

    def body(*refs):
        ins, outs, scratch = refs[:18], refs[18:26], refs[26:]
        jj = pl.program_id(0)

        @pl.when(jj == 0)
        def _():
            for s_ref in scratch:
                s_ref[...] = jnp.zeros_like(s_ref)
            for d in range(2):
                outs[4 * d + 3][...] = jnp.zeros_like(outs[4 * d + 3])

        results = []
        for d, rev in enumerate(DIRS):
            q, k, v, g, gb, sc, sn, sm, dh_ref = ins[9 * d:9 * d + 9]
            state = tuple([ref[0, h] for h in range(n_heads)] for ref in (sc, sn, sm))
            f = lambda st, a, b, c, gg, bb, rev=rev, d=d: _ml_chunk(st, a, b, c, gg, bb, rev, d)
            _, vjp = jax.vjp(f, state, q[...], k[...], v[...], g[...], gb[...])
            d_state = tuple([ref[h] for h in range(n_heads)] for ref in scratch[3 * d:3 * d + 3])
            d_out = dh_ref[...] * (nat[d](jj) < n_lat).astype(F32)
            results.append(vjp((d_state, d_out)))
        for d, (d_state, dq, dk, dv, dg, dgb) in enumerate(results):
            dqk_ref, dv_ref, dg_ref, dgb_ref = outs[4 * d:4 * d + 4]
            for part in range(3):
                for h in range(n_heads):
                    scratch[3 * d + part][h] = d_state[part][h]
            dqk_ref[:, 0:w] = dq
            dqk_ref[:, w:2 * w] = dk
            dv_ref[...] = dv.astype(BF16)
            dg_ref[...] = dg
            dgb_ref[...] += dgb

    in_specs, out_specs, out_shape, operands = [], [], [], []
    for d in range(2):
        row = lambda jj, d=d: (nat[d](jj), 0)
        in_specs += [pl.BlockSpec((chunk,w), row), pl.BlockSpec((chunk,w), lambda jj, d=d: (nat[d](jj), 1)),
                     pl.BlockSpec((chunk,w), lambda jj, d=d: (nat[d](jj), 7)),
                     pl.BlockSpec((chunk,LANE), lambda jj, d=d: (nat[d](jj), 10 * w // LANE)),
                     pl.BlockSpec((1, LANE), lambda jj: (0, 0))]
        in_specs += list(_ml_state_specs(n_heads, dh, step))
        in_specs += [pl.BlockSpec((chunk,w), lambda jj, d=d: (jnp.minimum(nat[d](jj), n_lat - 1), 0))]
        operands += [qk, qk, u, u, gate_b, *saved[d], d_h]
        out_specs += [pl.BlockSpec((chunk,2 * w), row), pl.BlockSpec((chunk,w), row),
                      pl.BlockSpec((chunk,LANE), row), pl.BlockSpec((1, LANE), lambda jj: (0, 0))]
        out_shape += [jax.ShapeDtypeStruct((r, 2 * w), F32), jax.ShapeDtypeStruct((r, w), BF16),
                      jax.ShapeDtypeStruct((r, LANE), F32), jax.ShapeDtypeStruct((1, LANE), F32)]
    res, rode = _call(
        body, operands, name="ml_scan_bwd", grid=(n_chunks,), in_specs=in_specs, out_specs=out_specs,
        out_shape=out_shape, scratch_shapes=_ml_state_scratch(n_heads, dh) * 2, sem=("arbitrary",), rider=rider)
    return res[0:4], res[4:8], rode


def _post_specs(w, tm, lat_tiles, cols):
    return [pl.BlockSpec((tm, w), (lambda i, cb=cb: (jnp.minimum(i, lat_tiles - 1), cb))) for cb in cols]


def _post_fwd(o_f, o_b, h_f, h_b, u, wa, wb, t_rows, w, n_hg, n_ml, tm):
    lat_tiles = t_rows // tm

    def body(of, ob, hf, hb, az, bo, bz, wa_ref, wb_ref, y_ref):
        y_ref[...] = _post_fn(of[...], ob[...], az[...], hf[...], hb[...], bo[...], bz[...],
                              wa_ref[...], wb_ref[...], n_hg, n_ml).astype(BF16)

    rows = pl.BlockSpec((tm, w), lambda i: (i, 0))
    vec = pl.BlockSpec((1, w), lambda i: (0, 0))
    return pl.pallas_call(
        body, name="post_fwd", grid=(lat_tiles,),
        in_specs=[rows] * 4 + _post_specs(w, tm, lat_tiles, (4, 8, 9)) + [vec, vec],
        out_specs=pl.BlockSpec((tm, 2 * w), lambda i: (i, 0)),
        out_shape=jax.ShapeDtypeStruct((t_rows, 2 * w), BF16),
        compiler_params=_params(("parallel",)),
    )(o_f, o_b, h_f, h_b, u, u, u, wa, wb)


def _post_bwd(o_f, o_b, h_f, h_b, u, wa, wb, dy, t_rows, w, n_hg, n_ml, tm, rider=None):
    r = u.shape[0]
    lat_tiles = t_rows // tm
    lat = lambda i: (jnp.minimum(i, lat_tiles - 1), 0)

    def body(of, ob, hf, hb, az, bo, bz, wa_ref, wb_ref, dy_ref, do_ref, dh_ref, daz_ref, dbo_ref, dbz_ref,
             dwa_ref, dwb_ref):
        i = pl.program_id(0)

        @pl.when(i == 0)
        def _():
            dwa_ref[...] = jnp.zeros_like(dwa_ref)
            dwb_ref[...] = jnp.zeros_like(dwb_ref)

        @pl.when(i < lat_tiles)
        def _():
            f = functools.partial(_post_fn, n_hg=n_hg, n_ml=n_ml)
            _, vjp = jax.vjp(f, of[...], ob[...], az[...], hf[...], hb[...], bo[...], bz[...], wa_ref[...], wb_ref[...])
            d_of, _, d_az, d_hf, _, d_bo, d_bz, d_wa, d_wb = vjp(dy_ref[...])
            do_ref[...] = d_of
            dh_ref[...] = d_hf
            daz_ref[...] = d_az.astype(BF16)
            dbo_ref[...] = d_bo.astype(BF16)
            dbz_ref[...] = d_bz.astype(BF16)
            dwa_ref[...] += d_wa
            dwb_ref[...] += d_wb

        @pl.when(i >= lat_tiles)
        def _():
            daz_ref[...] = jnp.zeros_like(daz_ref)
            dbo_ref[...] = jnp.zeros_like(dbo_ref)
            dbz_ref[...] = jnp.zeros_like(dbz_ref)

    lat_rows = pl.BlockSpec((tm, w), lat)
    all_rows = pl.BlockSpec((tm, w), lambda i: (i, 0))
    vec = pl.BlockSpec((1, w), lambda i: (0, 0))
    sd_t = jax.ShapeDtypeStruct((t_rows, w), F32)
    sd_r = jax.ShapeDtypeStruct((r, w), BF16)
    sd_v = jax.ShapeDtypeStruct((1, w), F32)
    return _call(
        body, (o_f, o_b, h_f, h_b, u, u, u, wa, wb, dy), name="post_bwd", grid=(r // tm,),
        in_specs=[lat_rows] * 4 + _post_specs(w, tm, lat_tiles, (4, 8, 9)) + [vec, vec]
        + [pl.BlockSpec((tm, 2 * w), lat)],
        out_specs=(lat_rows, lat_rows, all_rows, all_rows, all_rows, vec, vec),
        out_shape=(sd_t, sd_t, sd_r, sd_r, sd_r, sd_v, sd_v), sem=("arbitrary",), rider=rider)


OUT_ROW_GATE, OUT_ROW_LN_G, OUT_ROW_LN_B, OUT_ROW_LOSS = 0, 1, 2, 3


def _out_block(y, w_out, x, target, prm, tm):
    t_rows, dm = x.shape
    di = y.shape[1]

    def body(y_ref, w_ref, x_ref, t_ref, p_ref, dz_ref, dy_ref, gx_ref, acc_ref):
        @pl.when(pl.program_id(0) == 0)
        def _():
            acc_ref[...] = jnp.zeros_like(acc_ref)

        gate, ln_g, ln_b = p_ref[0:1, :], p_ref[1:2, :], p_ref[2:3, :]
        z = _nn(y_ref[...], w_ref[...])
        res = ALPHA * x_ref[...] + gate * z
        mu = jnp.mean(res, axis=-1, keepdims=True)
        rc = res - mu
        rstd = lax.rsqrt(jnp.mean(rc * rc, axis=-1, keepdims=True) + LN_EPS)
        rn = rc * rstd
        err = rn * ln_g + ln_b - t_ref[...]
        d_out = err * (1.0 / dm)
        d_rn = d_out * ln_g
        d_res = rstd * (d_rn - jnp.mean(d_rn, axis=-1, keepdims=True)
                        - rn * jnp.mean(d_rn * rn, axis=-1, keepdims=True))
        acc_ref[OUT_ROW_GATE:OUT_ROW_GATE + 1, :] += jnp.sum(d_res * z, axis=0, keepdims=True)
        acc_ref[OUT_ROW_LN_G:OUT_ROW_LN_G + 1, :] += jnp.sum(d_out * rn, axis=0, keepdims=True)
        acc_ref[OUT_ROW_LN_B:OUT_ROW_LN_B + 1, :] += jnp.sum(d_out, axis=0, keepdims=True)
        acc_ref[OUT_ROW_LOSS:OUT_ROW_LOSS + 1, :] += (0.5 / dm) * jnp.sum(err * err, axis=0, keepdims=True)
        gx_ref[...] = ALPHA * d_res
        dz = (d_res * gate).astype(BF16)
        dz_ref[...] = dz
        dy_ref[...] = _nt(dz, w_ref[...])

    rows_d = pl.BlockSpec((tm, dm), lambda i: (i, 0))
    rows_i = pl.BlockSpec((tm, di), lambda i: (i, 0))
    return pl.pallas_call(
        body, name="out_block", grid=(t_rows // tm,),
        in_specs=[rows_i, pl.BlockSpec((di, dm), lambda i: (0, 0)), rows_d, rows_d,
                  pl.BlockSpec((8, dm), lambda i: (0, 0))],
        out_specs=(rows_d, rows_i, rows_d, pl.BlockSpec((8, dm), lambda i: (0, 0))),
        out_shape=(jax.ShapeDtypeStruct((t_rows, dm), BF16), jax.ShapeDtypeStruct((t_rows, di), F32),
                   jax.ShapeDtypeStruct((t_rows, dm), F32), jax.ShapeDtypeStruct((8, dm), F32)),
        compiler_params=_params(("arbitrary",)),
    )(y, w_out, x, target, prm)


def _mod_fwd(c16, w_mod, tn):
    dm, n = w_mod.shape

    def body(c_ref, w_ref, o_ref, a_ref):
        a = _silu(c_ref[...])
        a_ref[...] = a
        o_ref[...] = _nn(a, w_ref[...], HIGHEST)

    return pl.pallas_call(
        body, name="mod_fwd", grid=(n // tn,),
        in_specs=[pl.BlockSpec((16, dm), lambda j: (0, 0)), pl.BlockSpec((dm, tn), lambda j: (0, j))],
        out_specs=(pl.BlockSpec((16, tn), lambda j: (0, j)), pl.BlockSpec((16, dm), lambda j: (0, 0))),
        out_shape=(jax.ShapeDtypeStruct((16, n), F32), jax.ShapeDtypeStruct((16, dm), F32)),
        compiler_params=_params(("arbitrary",)),
    )(c16, w_mod)


def _mod_bwd(a16, dm16, w_mod, tn, rider=None):
    dm, n = w_mod.shape

    def body(a_ref, d_ref, w_ref, dw_ref, dc_ref):
        @pl.when(pl.program_id(0) == 0)
        def _():
            dc_ref[...] = jnp.zeros_like(dc_ref)
        dw_ref[...] = _tn(a_ref[...], d_ref[...], HIGHEST)
        dc_ref[...] += _nt(d_ref[...], w_ref[...], HIGHEST)

    return _call(
        body, (a16, dm16, w_mod), name="mod_bwd", grid=(n // tn,),
        in_specs=[pl.BlockSpec((16, dm), lambda j: (0, 0)), pl.BlockSpec((16, tn), lambda j: (0, j)),
                  pl.BlockSpec((dm, tn), lambda j: (0, j))],
        out_specs=(pl.BlockSpec((dm, tn), lambda j: (0, j)), pl.BlockSpec((16, dm), lambda j: (0, 0))),
        out_shape=(jax.ShapeDtypeStruct((dm, n), F32), jax.ShapeDtypeStruct((16, dm), F32)),
        sem=("arbitrary",), rider=rider)


def _sum_devices(g, fold_rows):
    n_dev, rows, n = g.shape

    def body(g_ref, s_ref, t_ref):
        s = g_ref[0]
        for dev in range(1, n_dev):
            s = s + g_ref[dev]
        t_ref[...] = jnp.broadcast_to(jnp.sum(s, axis=-1, keepdims=True), (rows, LANE))
        s_ref[...] = s
        s_ref[0:fold_rows, :] = s[0:fold_rows] + s[fold_rows:2 * fold_rows]

    return pl.pallas_call(
        body, name="sum_devices",
        out_shape=(jax.ShapeDtypeStruct((rows, n), F32), jax.ShapeDtypeStruct((rows, LANE), F32)),
        compiler_params=_params(),
    )(g)


def _c_ctx_grad(parts, c_ctx_row):
    def body(p_ref, c_ref, o_ref):
        s = p_ref[0]
        for chip in range(1, N_CHIPS):
            s = s + p_ref[2 * chip]
        cv = c_ref[...]
        sg = _sigmoid(cv)
        o_ref[...] = s * (sg * (1.0 + cv * (1.0 - sg)))

    return pl.pallas_call(
        body, name="c_ctx_grad", out_shape=jax.ShapeDtypeStruct(parts.shape[1:], F32), compiler_params=_params(),
    )(parts, c_ctx_row)


def _sum_pair(name, mine, got):
    def body(a_ref, b_ref, o_ref):
        o_ref[...] = (a_ref[...] + b_ref[...]).astype(BF16)

    k, rows, n = mine.shape
    tl = _largest_divisor(n, max(LANE, (1 << 18) // rows), LANE)
    spec = pl.BlockSpec((1, rows, tl), lambda kk, i: (kk, 0, i))
    return pl.pallas_call(
        body, name=name, grid=(k, n // tl), in_specs=[spec, spec], out_specs=spec,
        out_shape=jax.ShapeDtypeStruct(mine.shape, BF16), compiler_params=_params(("parallel", "parallel")),
    )(mine, got)


def _sum_pair_lanes(name, full, got, ci):
    rows, n = got.shape
    tr = _largest_divisor(rows, max(SUBLANE_BF16, (1 << 19) // n), SUBLANE_BF16)

    def body(ci_ref, a_ref, b_ref, o_ref):
        o_ref[...] = (a_ref[...] + b_ref[...].astype(F32)).astype(BF16)

    return pl.pallas_call(
        body, name=name,
        grid_spec=pltpu.PrefetchScalarGridSpec(
            num_scalar_prefetch=1, grid=(rows // tr,),
            in_specs=[pl.BlockSpec((tr, n), lambda i, c: (i, c[0])), pl.BlockSpec((tr, n), lambda i, c: (i, 0))],
            out_specs=pl.BlockSpec((tr, n), lambda i, c: (i, 0))),
        out_shape=jax.ShapeDtypeStruct((rows, n), BF16), compiler_params=_params(("parallel",)),
    )(ci.reshape(1).astype(jnp.int32), full, got)


def _sum_chips(name, got, own, chip):
    k, rows, n = got.shape
    tl = _largest_divisor(n, max(LANE, (1 << 18) // rows), LANE)

    def body(chip_ref, g_ref, own_ref, o_ref):
        total = None
        for kk in range(k):
            term = jnp.where(chip_ref[0] == kk, own_ref[0], g_ref[kk]).astype(F32)
            total = term if total is None else total + term
        o_ref[...] = total

    return pl.pallas_call(
        body, name=name,
        grid_spec=pltpu.PrefetchScalarGridSpec(
            num_scalar_prefetch=1, grid=(n // tl,),
            in_specs=[pl.BlockSpec((k, rows, tl), lambda i, c: (0, 0, i)),
                      pl.BlockSpec((1, rows, tl), lambda i, c: (c[0], 0, i))],
            out_specs=pl.BlockSpec((rows, tl), lambda i, c: (0, i))),
        out_shape=jax.ShapeDtypeStruct((rows, n), F32), compiler_params=_params(("parallel",)),
    )(chip.reshape(1).astype(jnp.int32), got, own)


def _adamw_update(w, g, m, v):
    m2 = ADAM_B1 * m + (1.0 - ADAM_B1) * g
    v2 = ADAM_B2 * v + (1.0 - ADAM_B2) * jnp.square(g)
    m_hat = m2 / (1.0 - ADAM_B1 ** ADAM_STEP)
    v_hat = v2 / (1.0 - ADAM_B2 ** ADAM_STEP)
    return -ADAM_LR * (m_hat / (jnp.sqrt(v_hat) + ADAM_EPS) + ADAM_WD * w), m2, v2


def _adamw(name, w, g, m, v, rider=None):
    rows, n = w.shape
    if rows % 8 == 0:
        tr = _largest_divisor(rows, max(8, (1 << 18) // n), 8)
        block, index, steps = (tr, n), (lambda i: (i, 0)), rows // tr
    else:
        tl = _largest_divisor(n, max(LANE, (1 << 18) // rows), LANE)
        block, index, steps = (rows, tl), (lambda i: (0, i)), n // tl

    def body(w_ref, g_ref, m_ref, v_ref, d_ref, mo_ref, vo_ref):
        d_ref[...], mo_ref[...], vo_ref[...] = _adamw_update(w_ref[...], g_ref[...], m_ref[...], v_ref[...])

    spec = pl.BlockSpec(block, index)
    sds = jax.ShapeDtypeStruct((rows, n), F32)
    return _call(body, (w, g, m, v), name=name, grid=(steps,), in_specs=[spec] * 4, out_specs=(spec,) * 3,
                 out_shape=(sds, sds, sds), sem=("parallel",), rider=rider)


PACK_LANES = 1024


def _pack(pieces):
    flat = jnp.concatenate([p.reshape(-1) for p in pieces])
    total = -(-flat.shape[0] // (8 * PACK_LANES)) * 8 * PACK_LANES
    return jnp.pad(flat, (0, total - flat.shape[0])).reshape(-1, PACK_LANES)


def _unpack(packed, shapes):
    flat = packed.reshape(-1)
    out, off = [], 0
    for shp in shapes:
        size = math.prod(shp)
        out.append(flat[off:off + size].reshape(shp))
        off += size
    return out


def _rows8(rows, width):
    flat = [r.reshape(width) for r in rows] + [jnp.zeros(((8 - len(rows)) * width,), F32)]
    return jnp.concatenate(flat).reshape(8, width)


def kernel(x, c, ctx, c_ctx, w_mod, b_mod, w_in, conv_w, conv_b, hg_lb, ml_gate_b, hg_norm_w, ml_norm_w, w_out, ln_g, ln_b, loss_target, m_c_ctx, m_w_mod, m_b_mod, m_w_in, m_conv_w, m_conv_b, m_hg_lb, m_ml_gate_b, m_hg_norm_w, m_ml_norm_w, m_w_out, m_ln_g, m_ln_b, v_c_ctx, v_w_mod, v_b_mod, v_w_in, v_conv_w, v_conv_b, v_hg_lb, v_ml_gate_b, v_hg_norm_w, v_ml_norm_w, v_w_out, v_ln_g, v_ln_b):
    t_rows, dm = x.shape[1], x.shape[2]
    c_rows = ctx.shape[1]
    w = hg_norm_w.shape[1]
    n_ml = ml_gate_b.shape[-1]
    n_hg = w // HG_DK
    di = 2 * w
    n_in = 10 * w + 4 * n_ml
    ns = w_in.shape[2]
    nm = w_mod.shape[2]
    n_pad = 10 * w + LANE
    r_rows = t_rows + c_rows
    row_gcd = math.gcd(t_rows, c_rows)
    hg_chunk, ml_chunk = math.gcd(HG_CHUNK, row_gcd), math.gcd(ML_CHUNK, row_gcd)
    hg_counts = (t_rows // hg_chunk, c_rows // hg_chunk, hg_chunk)
    ml_counts = (t_rows // ml_chunk, c_rows // ml_chunk, ml_chunk)
    assert ml_norm_w.shape[1] == w and di == dm and N_CHIPS * ns == n_in and N_CHIPS * nm == 3 * dm
    assert w_out.shape[1] * N_CHIPS == di and 4 * n_ml <= LANE and t_rows % GRID_W == 0

    xi, yi, ci = lax.axis_index("x"), lax.axis_index("y"), lax.axis_index("c")
    chip = 2 * xi + yi
    dev = 4 * xi + 2 * yi + ci

    tm = _largest_divisor(math.gcd(t_rows, c_rows), 256, 8)
    tm_mm = _largest_divisor(r_rows, 1088, SUBLANE_BF16)
    tn_mm = LANE * _largest_divisor(n_pad // LANE, 9)
    tn_mod = _largest_divisor(nm, 512, LANE)

    shard_shapes = [(dm,), (2, 2, w // N_CHIPS), (3, 3, di // N_CHIPS)]
    g1 = _all_gather8(_pack([c, hg_lb, conv_w])).run("gather_inputs")[0]
    per_dev = [_unpack(g1[i], shard_shapes) for i in range(N_DEV)]
    c_all = jnp.stack([p[0] for p in per_dev])
    lb_full = jnp.concatenate([per_dev[2 * k][1] for k in range(N_CHIPS)], axis=-1)
    conv_w9 = jnp.concatenate([per_dev[2 * k][2] for k in range(N_CHIPS)], axis=-1).reshape(9, di)

    c16 = jnp.concatenate([c_all, c_ctx[None], jnp.zeros((16 - N_DEV - 1, dm), F32)])
    mod_part, a16 = _mod_fwd(c16, w_mod[0], tn_mod)
    g2 = _all_gather8(mod_part).run("gather_mod")[0]
    mod_all = jnp.concatenate([g2[2 * k] for k in range(N_CHIPS)], axis=1) + b_mod
    mod_x = lax.dynamic_index_in_dim(mod_all, dev, 0, keepdims=False).reshape(3, dm)
    mod_c = mod_all[N_DEV].reshape(3, dm)
    prm = jnp.stack([_rows8(list(mod_x), dm), _rows8(list(mod_c), dm)])

    as_t = lambda a: jnp.transpose(a[0])
    half_in = lax.dynamic_slice_in_dim(as_t(w_in).astype(BF16), ci * (dm // 2), dm // 2, 1)
    half_out = lax.dynamic_slice_in_dim(w_out[0].astype(BF16), ci * (di // (2 * N_CHIPS)), di // (2 * N_CHIPS), 0)
    lanes_of = lambda core: pl.ds(core * (dm // 2), dm // 2)
    landing = lambda s, r: (_chip_of(s), slice(None), lanes_of(s[2]))
    own_placed = lax.dynamic_update_slice(jnp.zeros((N_CHIPS + 1, ns, dm), BF16),
                                          as_t(w_in).astype(BF16)[None], (chip, 0, 0))
    gather_in = _Exchange([half_in, own_placed], [jax.ShapeDtypeStruct(own_placed.shape, BF16)],
                          [(mask, 0, None, 0, landing) for mask in CHIP_MASKS], in_place={1: 0})

    hc, (gw_in,) = _modulate_fwd(x[0], ctx[0], prm, tm, rider=gather_in)
    my_lanes = lambda s, r: (slice(0, N_CHIPS), slice(None), lanes_of(s[2]))
    gw_in = _Exchange([gw_in], [jax.ShapeDtypeStruct(gw_in.shape, BF16)],
                      [(SIBLING_MASK, 0, my_lanes, 0, my_lanes)], in_place={0: 0}).run("gather_w_in_pair")[0]
    wt_full = gw_in.reshape((N_CHIPS + 1) * ns, dm)
    assert wt_full.shape[0] >= n_pad
    u, (got_out,) = _mm_nt("in_proj", hc, wt_full, n_pad, tm_mm, tn_mm, F32, rider=_all_gather_chips([half_out]))
    fetched_out = _own_block(chip, half_out, got_out)
    (o_f, o_b), hg_saved, (swapped_out,) = _hg_scan_fwd(u, lb_full, w, *hg_counts,
                                                         rider=_sibling_swap([fetched_out]))
    w_out_full = _join_halves(ci, fetched_out, swapped_out, 1).reshape(di, dm)
    qk = _conv_fwd(u, conv_w9, conv_b, t_rows, c_rows, w, LANE)
    gate_b_row = jnp.pad(ml_gate_b.reshape(1, -1), ((0, 0), (0, LANE - 4 * n_ml)))
    (h_f, h_b), ml_saved = _ml_scan_fwd(qk, u, gate_b_row, w, n_ml, *ml_counts)
    y = _post_fwd(o_f, o_b, h_f, h_b, u, hg_norm_w, ml_norm_w, t_rows, w, n_hg, n_ml, tm)
    prm_out = _rows8([mod_x[2], ln_g, ln_b], dm)
    dz, dy, gx_direct, acc_out = _out_block(y, w_out_full, x[0], loss_target[0], prm_out, tm)

    d_w_out = _mm_tn("d_w_out", y, dz, _largest_divisor(di, 1024, LANE),
                     _largest_divisor(t_rows, 1024, SUBLANE_BF16))
    d_w_out4 = d_w_out.reshape(N_CHIPS, 2, di // (2 * N_CHIPS), dm)
    mine_out = lax.dynamic_index_in_dim(d_w_out4, ci, 1, keepdims=False)
    other_out = lax.dynamic_index_in_dim(d_w_out4, 1 - ci, 1, keepdims=False)
    (d_o, d_h, d_az, d_bo, d_bz, d_wa, d_wb), (got_out,) = _post_bwd(
        o_f, o_b, h_f, h_b, u, hg_norm_w, ml_norm_w, dy, t_rows, w, n_hg, n_ml, tm, rider=_sibling_swap([other_out]))
    pair_out = _sum_pair("rs_pair_sum_w_out", mine_out, got_out)
    (d_aq_f, d_aff, d_ai_f, d_lb_f), (d_aq_b, d_afb, d_ai_b, d_lb_b), (landed_out,) = _hg_scan_bwd(
        u, lb_full, hg_saved, d_o, w, *hg_counts, rider=_chip_scatter([pair_out]))
    half_g_out = _sum_chips("rs_chip_sum_w_out", landed_out, pair_out, chip)
    (d_qk_f, d_v_f, d_g_f, d_gb_f), (d_qk_b, d_v_b, d_g_b, d_gb_b), (sibling_out,) = _ml_scan_bwd(
        qk, u, gate_b_row, ml_saved, d_h, w, n_ml, *ml_counts, rider=_sibling_swap([half_g_out]))
    g_w_out = _join_halves(ci, half_g_out, sibling_out, 0)
    d_bqk, d_cw, d_cb = _conv_bwd(u, (d_qk_f, d_qk_b), conv_w9, conv_b, t_rows, c_rows, w, LANE)
    du = _assemble_du([(d_aq_f, d_aq_b), d_aff, d_afb, (d_ai_f, d_ai_b), d_az, d_bqk, (d_v_f, d_v_b), d_bo, d_bz],
                      (d_g_f, d_g_b), n_pad, tm)
    d_wt_in, d_wt_in_bf16 = _mm_tn("d_w_in", du, hc, tn_mm, tm_mm, with_bf16=True)

    delta, new_m, new_v = {}, {}, {}
    res, (got_in,) = _adamw(
        "adamw_w_out", w_out[0], g_w_out, m_w_out[0], v_w_out[0],
        rider=_Exchange([d_wt_in_bf16], [jax.ShapeDtypeStruct((n_pad, dm // 2), BF16)],
                        [(SIBLING_MASK, 0, lambda s, r: (slice(None), lanes_of(r[2])), 0, None)]))
    delta["w_out"], new_m["w_out"], new_v["w_out"] = (a[None] for a in res)
    pair_half = _sum_pair_lanes("rs_pair_sum_w_in", d_wt_in, got_in, ci)
    pair_in = jnp.stack([pair_half[k * ns:(k + 1) * ns] for k in range(N_CHIPS)])
    d_hc, (landed_in,) = _mm_acc("d_h", du, wt_full, tm_mm, tn_mm, rider=_chip_scatter([pair_in]))
    half_g_in = _sum_chips("rs_chip_sum_w_in", landed_in, pair_in, chip)
    (gx, acc_mod), _ = _modulate_bwd(x[0], ctx[0], d_hc, prm, gx_direct, tm)
    grad_x = gx[None]

    zero_row = jnp.zeros((dm,), F32)
    d_gb = jnp.concatenate([d_gb_f[:, 0:n_ml], d_gb_b[:, n_ml:2 * n_ml], d_gb_f[:, 2 * n_ml:3 * n_ml],
                            d_gb_b[:, 3 * n_ml:4 * n_ml], jnp.zeros((1, dm - 4 * n_ml), F32)], axis=1)
    rows = [acc_mod[0, 0], acc_mod[0, 1], acc_out[OUT_ROW_GATE],
            acc_mod[1, 0], acc_mod[1, 1], zero_row]
    rows += list(d_cw) + [d_cb[0], d_lb_f.reshape(dm), d_lb_b.reshape(dm),
                          jnp.concatenate([d_wa[0], d_wb[0]]), acc_out[OUT_ROW_LN_G], acc_out[OUT_ROW_LN_B],
                          acc_out[OUT_ROW_LOSS], d_gb[0], zero_row]
    ROW_CW, ROW_CB, ROW_LB, ROW_NORM, ROW_LN_G, ROW_LN_B, ROW_LOSS, ROW_GB = 6, 15, 16, 18, 19, 20, 21, 22
    small_rows = jnp.concatenate([r.reshape(dm) for r in rows]).reshape(len(rows), dm)
    g3 = _all_gather8(small_rows).run("gather_small_grads")[0]
    sums, totals = _sum_devices(g3, 3)
    loss = totals[ROW_LOSS, 0]
    dm16 = jnp.concatenate([g3[:, 0:3, :].reshape(N_DEV, 3 * dm), sums[3:6].reshape(1, 3 * dm),
                            jnp.zeros((16 - N_DEV - 1, 3 * dm), F32)])
    (g_w_mod, dc16), (sibling_g_in,) = _mod_bwd(a16, lax.dynamic_slice_in_dim(dm16, chip * nm, nm, 1), w_mod[0],
                                                tn_mod, rider=_sibling_swap([half_g_in]))
    g_wt_in = _join_halves(ci, half_g_in, sibling_g_in, 1)
    g4 = _all_gather8(jnp.pad(dc16[N_DEV:N_DEV + 1], ((0, 7), (0, 0)))).run("gather_c_ctx")[0]
    g_c_ctx = _c_ctx_grad(g4, jnp.broadcast_to(c_ctx[None], (8, dm)))[0]
    res, _ = _adamw("adamw_w_in", as_t(w_in), g_wt_in, as_t(m_w_in), as_t(v_w_in))
    delta["w_in"], new_m["w_in"], new_v["w_in"] = (jnp.transpose(a)[None] for a in res)
    res, _ = _adamw("adamw_w_mod", w_mod[0], g_w_mod, m_w_mod[0], v_w_mod[0])
    delta["w_mod"], new_m["w_mod"], new_v["w_mod"] = (a[None] for a in res)

    chip_cols = lambda a, width: lax.dynamic_slice_in_dim(a, chip * width, width, a.ndim - 1)
    grads = {
        "c_ctx": g_c_ctx,
        "w_mod": g_w_mod[None],
        "b_mod": sums[0:3].reshape(1, 3 * dm),
        "w_in": jnp.transpose(g_wt_in)[None],
        "conv_w": chip_cols(sums[ROW_CW:ROW_CW + 9].reshape(1, 3, 3, di), di // N_CHIPS),
        "conv_b": sums[ROW_CB][None],
        "hg_lb": chip_cols(sums[ROW_LB:ROW_LB + 2].reshape(2, 2, w), w // N_CHIPS),
        "ml_gate_b": sums[ROW_GB, 0:4 * n_ml].reshape(1, 4, n_ml),
        "hg_norm_w": sums[ROW_NORM, 0:w][None],
        "ml_norm_w": sums[ROW_NORM, w:2 * w][None],
        "w_out": g_w_out[None],
        "ln_g": sums[ROW_LN_G][None],
        "ln_b": sums[ROW_LN_B][None],
    }
    weights = dict(c_ctx=c_ctx, w_mod=w_mod, b_mod=b_mod, w_in=w_in, conv_w=conv_w, conv_b=conv_b, hg_lb=hg_lb,
                   ml_gate_b=ml_gate_b, hg_norm_w=hg_norm_w, ml_norm_w=ml_norm_w, w_out=w_out, ln_g=ln_g, ln_b=ln_b)
    mom1 = dict(c_ctx=m_c_ctx, w_mod=m_w_mod, b_mod=m_b_mod, w_in=m_w_in, conv_w=m_conv_w, conv_b=m_conv_b,
                hg_lb=m_hg_lb, ml_gate_b=m_ml_gate_b, hg_norm_w=m_hg_norm_w, ml_norm_w=m_ml_norm_w, w_out=m_w_out,
                ln_g=m_ln_g, ln_b=m_ln_b)
    mom2 = dict(c_ctx=v_c_ctx, w_mod=v_w_mod, b_mod=v_b_mod, w_in=v_w_in, conv_w=v_conv_w, conv_b=v_conv_b,
                hg_lb=v_hg_lb, ml_gate_b=v_ml_gate_b, hg_norm_w=v_hg_norm_w, ml_norm_w=v_ml_norm_w, w_out=v_w_out,
                ln_g=v_ln_g, ln_b=v_ln_b)
    names = list(weights)
    big = ("w_mod", "w_in", "w_out")
    small = [n for n in names if n not in big]

    small_shapes = [weights[n].shape for n in small]
    res, _ = _adamw("adamw_small", *(_pack([src[n] for n in small]) for src in (weights, grads, mom1, mom2)))
    for out, packed in zip((delta, new_m, new_v), res):
        for n, a in zip(small, _unpack(packed, small_shapes)):
            out[n] = a

    return (loss, grad_x, *[grads[n].reshape(weights[n].shape) for n in names], *[delta[n] for n in names],
            *[new_m[n] for n in names], *[new_v[n] for n in names])
```

```python
import functools
import math

import jax
import jax.numpy as jnp
from jax import lax
from jax.experimental import pallas as pl
from jax.experimental.pallas import tpu as pltpu

F32 = jnp.float32
BF16 = jnp.bfloat16
HIGHEST = lax.Precision.HIGHEST
MESH = pl.DeviceIdType.MESH

HG_CHUNK = 64
ML_CHUNK = 256
HG_CHUNKS_PER_STEP = 4
GRID_W = 64
HG_DK = 128
LANE = 128
SUBLANE_BF16 = 16
ALPHA = 2.0 ** 0.25
LN_EPS = 1e-5
NORM_EPS = 1e-6
ADAM_LR = 0.001
ADAM_B1 = 0.9
ADAM_B2 = 0.999
ADAM_EPS = 1e-08
ADAM_WD = 0.01
ADAM_STEP = 10
VMEM_LIMIT = 56 * 1024 * 1024
N_CHIPS = 4
N_DEV = 8


def _params(sem=None):
    return pltpu.CompilerParams(dimension_semantics=sem, vmem_limit_bytes=VMEM_LIMIT)


def _largest_divisor(n, cap, multiple=1):
    best = None
    for d in range(multiple, min(n, cap) + 1, multiple):
        if n % d == 0:
            best = d
    assert best is not None, (n, cap, multiple)
    return best


def _sigmoid(x):
    return jax.nn.sigmoid(x)


def _silu(x):
    return x * jax.nn.sigmoid(x)


def _dot(a, b, dims, precision=None):
    return lax.dot_general(a, b, (dims, ((), ())), precision=precision, preferred_element_type=F32)


def _nn(a, b, precision=None):
    return _dot(a, b, ((1,), (0,)), precision)


def _nt(a, b, precision=None):
    return _dot(a, b, ((1,), (1,)), precision)


def _tn(a, b, precision=None):
    return _dot(a, b, ((0,), (0,)), precision)


def _narrow(x):
    return x.astype(BF16)


@jax.custom_vjp
def _bnn(a, b):
    return _nn(_narrow(a), _narrow(b))


def _bnn_fwd(a, b):
    an, bn = _narrow(a), _narrow(b)
    return _nn(an, bn), (an, bn)


def _bnn_bwd(res, ct):
    an, bn = res
    ctn = _narrow(ct)
    return _nt(ctn, bn), _tn(an, ctn)


_bnn.defvjp(_bnn_fwd, _bnn_bwd)


@jax.custom_vjp
def _bnt(a, b):
    return _nt(_narrow(a), _narrow(b))


def _bnt_fwd(a, b):
    an, bn = _narrow(a), _narrow(b)
    return _nt(an, bn), (an, bn)


def _bnt_bwd(res, ct):
    an, bn = res
    ctn = _narrow(ct)
    return _nn(ctn, bn), _tn(ctn, an)


_bnt.defvjp(_bnt_fwd, _bnt_bwd)


@jax.custom_vjp
def _btn(a, b):
    return _tn(_narrow(a), _narrow(b))


def _btn_fwd(a, b):
    an, bn = _narrow(a), _narrow(b)
    return _tn(an, bn), (an, bn)


def _btn_bwd(res, ct):
    an, bn = res
    ctn = _narrow(ct)
    return _nt(bn, ctn), _nn(an, ctn)


_btn.defvjp(_btn_fwd, _btn_bwd)


def _visible(n, rev):
    r = lax.broadcasted_iota(jnp.int32, (n, n), 0)
    c = lax.broadcasted_iota(jnp.int32, (n, n), 1)
    return (r <= c) if rev else (r >= c)


def _mask_matmul(mask, x):
    mb = mask.astype(BF16)
    hi = x.astype(BF16)
    lo = (x - hi.astype(F32)).astype(BF16)
    return _nn(mb, hi) + _nn(mb, lo)


@functools.partial(jax.custom_vjp, nondiff_argnums=(1,))
def _cumulative(x, rev):
    return _mask_matmul(_visible(x.shape[0], rev), x)


def _cumulative_fwd(x, rev):
    return _cumulative(x, rev), None


def _cumulative_bwd(rev, _, ct):
    return (_mask_matmul(_visible(ct.shape[0], not rev), ct),)


_cumulative.defvjp(_cumulative_fwd, _cumulative_bwd)


def _hg_chunk(states, aq, af, ai, lb0, lb1, rev):
    n_heads = len(states)
    lb = _sigmoid(lb0 - lb1)
    f = lb + (1.0 - lb) * _sigmoid(af)
    g = jnp.log(f)
    k = 1.0 - f
    q = _silu(aq)
    chunk = aq.shape[0]
    vis = _visible(chunk, rev)
    b = _cumulative(g, rev)
    last = 0 if rev else chunk - 1
    b_end = b[last:last + 1]
    b_mid = b[chunk // 2:chunk // 2 + 1]
    q_inter = q * jnp.exp(b)
    q_intra = q * jnp.exp(b - b_mid)
    k_intra = k * jnp.exp(b_mid - b)
    k_dec = k * jnp.exp(b_end - b)
    e_end = jnp.exp(b_end)
    new_states, outs = [], []
    for h in range(n_heads):
        sl = slice(h * HG_DK, (h + 1) * HG_DK)
        s_t = states[h]
        scores = jnp.where(vis, _nt(q_intra[:, sl], k_intra[:, sl]), 0.0)
        outs.append(_nt(q_inter[:, sl], s_t) + _nn(scores, ai[:, sl]))
        new_states.append(e_end[:, sl] * s_t + _tn(ai[:, sl], k_dec[:, sl]))
    return new_states, jnp.concatenate(outs, axis=1)


def _ml_chunk(state, q, k, v, g, gb, rev, d):
    cms, nvs, mbs = state
    n_heads = len(cms)
    dh = q.shape[1] // n_heads
    ga = g + gb
    log_f_all = jax.nn.log_sigmoid(ga)
    chunk = q.shape[0]
    vis = _visible(chunk, rev)
    b_all = _cumulative(log_f_all, rev)
    last = 0 if rev else chunk - 1
    k = k * (dh ** -0.5)
    new_c, new_n, new_m, outs = [], [], [], []
    for h in range(n_heads):
        ci = d * n_heads + h
        cf = (2 + d) * n_heads + h
        sl = slice(h * dh, (h + 1) * dh)
        qh, kh, vh = q[:, sl], k[:, sl], v[:, sl]
        li = ga[:, ci:ci + 1]
        b = b_all[:, cf:cf + 1]
        m = mbs[h][:, 0:1]
        row = jnp.transpose(li - b)
        log_w = jnp.where(vis, b + row, -jnp.inf)
        m_inter = b + m
        m_t = jnp.maximum(m_inter, jnp.max(log_w, axis=-1, keepdims=True))
        w_inter = jnp.exp(m_inter - m_t)
        w_qk = jnp.exp(log_w - m_t) * _bnt(qh, kh)
        num = w_inter * _bnt(qh, cms[h]) + _bnn(w_qk, vh)
        den = w_inter * jnp.sum(qh * nvs[h], axis=-1, keepdims=True) + jnp.sum(w_qk, axis=-1, keepdims=True)
        outs.append(num / jnp.maximum(jnp.abs(den), jnp.exp(-m_t)))
        m_new = m_t[last:last + 1]
        b_end = b[last:last + 1]
        w_s = jnp.exp(b_end - b + li - m_new)
        decay = jnp.exp(b_end + m - m_new)
        new_c.append(decay * cms[h] + _btn(w_s * vh, kh))
        new_n.append(decay * nvs[h] + jnp.sum(w_s * kh, axis=0, keepdims=True))
        new_m.append(jnp.broadcast_to(m_new, (1, LANE)))
    return (new_c, new_n, new_m), jnp.concatenate(outs, axis=1)


def _post_fn(o_f, o_b, az, h_f, h_b, bo, bz, wa, wb, n_hg, n_ml):
    o = o_f + o_b
    parts = []
    for h in range(n_hg):
        s = o[:, h * HG_DK:(h + 1) * HG_DK]
        parts.append(s * lax.rsqrt(jnp.mean(s * s, axis=-1, keepdims=True) + NORM_EPS))
    y_a = jnp.concatenate(parts, axis=1) * wa * _silu(az)
    hh = h_f + h_b
    dh = hh.shape[1] // n_ml
    parts = []
    for h in range(n_ml):
        s = hh[:, h * dh:(h + 1) * dh]
        mu = jnp.mean(s, axis=-1, keepdims=True)
        sc = s - mu
        parts.append(sc * lax.rsqrt(jnp.mean(sc * sc, axis=-1, keepdims=True) + NORM_EPS))
    y_b = jnp.concatenate(parts, axis=1) * wb * _sigmoid(bo) * _silu(bz)
    return jnp.concatenate([y_a, y_b], axis=1)


def _chip_of(dev):
    return 2 * dev[0] + dev[1]


def _index_of(dev):
    return 4 * dev[0] + 2 * dev[1] + dev[2]


class _Exchange:
    def __init__(self, srcs, out_shapes, transfers, local_copies=(), in_place=None):
        self.srcs, self.out_shapes = list(srcs), list(out_shapes)
        self.transfers, self.local_copies = list(transfers), list(local_copies)
        self.in_place = dict(in_place or {})

    def scratch(self):
        return [pltpu.SemaphoreType.DMA((len(self.transfers),)), pltpu.SemaphoreType.DMA((len(self.transfers),)),
                pltpu.SemaphoreType.DMA((max(len(self.local_copies), 1),))]

    def copies(self, ins, outs, send_sems, recv_sems, local_sems):
        me = (lax.axis_index("x"), lax.axis_index("y"), lax.axis_index("c"))

        def pick(ref, fn, *who):
            return ref if fn is None else ref.at[fn(*who)]

        sends, recvs, locs = [], [], []
        for t, (mask, si, sfn, di, dfn) in enumerate(self.transfers):
            peer = tuple(1 - p if flip else p for p, flip in zip(me, mask))
            sends.append(pltpu.make_async_remote_copy(
                src_ref=pick(ins[si], sfn, me, peer), dst_ref=pick(outs[di], dfn, me, peer),
                send_sem=send_sems.at[t], recv_sem=recv_sems.at[t], device_id=peer, device_id_type=MESH))
            landing = pick(outs[di], dfn, peer, me)
            recvs.append(pltpu.make_async_remote_copy(
                src_ref=landing, dst_ref=landing,
                send_sem=send_sems.at[t], recv_sem=recv_sems.at[t], device_id=peer, device_id_type=MESH))
        for l, (si, sfn, di, dfn) in enumerate(self.local_copies):
            locs.append(pltpu.make_async_copy(pick(ins[si], sfn, me), pick(outs[di], dfn, me), local_sems.at[l]))

        def start():
            for cp in locs + sends:
                cp.start()

        def wait():
            for cp in recvs:
                cp.wait_recv()
            for cp in sends:
                cp.wait_send()
            for cp in locs:
                cp.wait()

        return start, wait

    def run(self, name):
        n_in, n_out = len(self.srcs), len(self.out_shapes)

        def body(*refs):
            start, wait = self.copies(refs[:n_in], refs[n_in:n_in + n_out], *refs[n_in + n_out:])
            start()
            wait()

        hbm = pl.BlockSpec(memory_space=pltpu.HBM)
        return pl.pallas_call(
            body, name=name, out_shape=tuple(self.out_shapes), in_specs=[hbm] * n_in,
            out_specs=tuple([hbm] * n_out), scratch_shapes=self.scratch(), input_output_aliases=self.in_place,
        )(*self.srcs)


def _call(body, operands, *, name, grid, in_specs, out_specs, out_shape, scratch_shapes=(), sem=None, rider=None):
    out_specs, out_shape, scratch_shapes = list(out_specs), list(out_shape), list(scratch_shapes)
    if rider is None:
        res = pl.pallas_call(
            body, name=name, grid=grid, in_specs=list(in_specs), out_specs=tuple(out_specs),
            out_shape=tuple(out_shape), scratch_shapes=scratch_shapes, compiler_params=_params(sem),
        )(*operands)
        return list(res), []
    counts = (len(in_specs), len(rider.srcs), len(out_specs), len(rider.out_shapes), len(scratch_shapes), 3)

    def full(*refs):
        groups, pos = [], 0
        for k in counts:
            groups.append(refs[pos:pos + k])
            pos += k
        own_in, ex_in, own_out, ex_out, own_scr, ex_scr = groups
        ids = [pl.program_id(a) for a in range(len(grid))]
        first = functools.reduce(jnp.logical_and, [i == 0 for i in ids])
        last = functools.reduce(jnp.logical_and, [i == g - 1 for i, g in zip(ids, grid)])
        start, wait = rider.copies(ex_in, ex_out, *ex_scr)
        pl.when(first)(start)
        body(*own_in, *own_out, *own_scr)
        pl.when(last)(wait)

    hbm = pl.BlockSpec(memory_space=pltpu.HBM)
    res = pl.pallas_call(
        full, name=name, grid=grid, in_specs=list(in_specs) + [hbm] * counts[1],
        out_specs=tuple(out_specs + [hbm] * counts[3]), out_shape=tuple(out_shape + rider.out_shapes),
        scratch_shapes=scratch_shapes + rider.scratch(), compiler_params=_params(("arbitrary",) * len(grid)),
        input_output_aliases={counts[0] + i: counts[2] + o for i, o in rider.in_place.items()},
    )(*operands, *rider.srcs)
    return list(res[:counts[2]]), list(res[counts[2]:])


ALL_MASKS = [(mx, my, mc) for mx in (0, 1) for my in (0, 1) for mc in (0, 1)][1:]
CHIP_MASKS = [(1, 0, 0), (0, 1, 0), (1, 1, 0)]
SIBLING_MASK = (0, 0, 1)


def _all_gather8(v):
    out = jax.ShapeDtypeStruct((N_DEV,) + v.shape, v.dtype)
    slot = lambda sender, receiver: _index_of(sender)
    transfers = [(mask, 0, None, 0, slot) for mask in ALL_MASKS]
    return _Exchange([v], [out], transfers, [(0, None, 0, lambda me: _index_of(me))])


def _all_gather_chips(arrays):
    outs = [jax.ShapeDtypeStruct((N_CHIPS,) + a.shape, a.dtype) for a in arrays]
    slot = lambda sender, receiver: _chip_of(sender)
    return _Exchange(arrays, outs, [(mask, i, None, i, slot) for i in range(len(arrays)) for mask in CHIP_MASKS])


def _sibling_swap(arrays):
    outs = [jax.ShapeDtypeStruct(a.shape, a.dtype) for a in arrays]
    return _Exchange(arrays, outs, [(SIBLING_MASK, i, None, i, None) for i in range(len(arrays))])


def _chip_scatter(arrays):
    outs = [jax.ShapeDtypeStruct(a.shape, a.dtype) for a in arrays]
    transfers = [(mask, i, lambda s, r: _chip_of(r), i, lambda s, r: _chip_of(s))
                 for i in range(len(arrays)) for mask in CHIP_MASKS]
    return _Exchange(arrays, outs, transfers)


def _own_block(chip, own, blocks):
    sel = (lax.broadcasted_iota(jnp.int32, (N_CHIPS,) + (1,) * (blocks.ndim - 1), 0) == chip)
    return jnp.where(sel, own if own.ndim == blocks.ndim else own[None], blocks)


def _join_halves(ci, mine, other, axis):
    return jnp.where(ci == 0, jnp.concatenate([mine, other], axis=axis), jnp.concatenate([other, mine], axis=axis))


def _mm_nt(name, a, b, n, tm, tn, out_dtype, rider=None):
    m, k = a.shape

    def body(a_ref, b_ref, o_ref):
        o_ref[...] = _nt(a_ref[...], b_ref[...]).astype(out_dtype)

    (out,), rode = _call(
        body, (a, b), name=name, grid=(n // tn, m // tm),
        in_specs=[pl.BlockSpec((tm, k), lambda j, i: (i, 0)), pl.BlockSpec((tn, k), lambda j, i: (j, 0))],
        out_specs=[pl.BlockSpec((tm, tn), lambda j, i: (i, j))],
        out_shape=[jax.ShapeDtypeStruct((m, n), out_dtype)], sem=("parallel", "parallel"), rider=rider)
    return out, rode


def _mm_acc(name, a, b, tm, tk, rider=None):
    m, kc = a.shape
    n = b.shape[1]

    def body(a_ref, b_ref, o_ref):
        @pl.when(pl.program_id(1) == 0)
        def _():
            o_ref[...] = jnp.zeros_like(o_ref)
        o_ref[...] += _nn(a_ref[...], b_ref[...])

    (out,), rode = _call(
        body, (a, b), name=name, grid=(m // tm, kc // tk),
        in_specs=[pl.BlockSpec((tm, tk), lambda i, kk: (i, kk)), pl.BlockSpec((tk, n), lambda i, kk: (kk, 0))],
        out_specs=[pl.BlockSpec((tm, n), lambda i, kk: (i, 0))],
        out_shape=[jax.ShapeDtypeStruct((m, n), F32)], sem=("parallel", "arbitrary"), rider=rider)
    return out, rode


def _mm_tn(name, a, b, tm, tk, with_bf16=False):
    kr, m = a.shape
    n = b.shape[1]
    steps_k = kr // tk

    def body(a_ref, b_ref, o_ref, *narrow):
        @pl.when(pl.program_id(1) == 0)
        def _():
            o_ref[...] = jnp.zeros_like(o_ref)
        o_ref[...] += _tn(a_ref[...], b_ref[...])
        if with_bf16:
            @pl.when(pl.program_id(1) == steps_k - 1)
            def _():
                narrow[0][...] = o_ref[...].astype(BF16)

    out_spec = pl.BlockSpec((tm, n), lambda i, kk: (i, 0))
    res = pl.pallas_call(
        body, name=name, grid=(m // tm, steps_k),
        in_specs=[pl.BlockSpec((tk, tm), lambda i, kk: (kk, i)), pl.BlockSpec((tk, n), lambda i, kk: (kk, 0))],
        out_specs=(out_spec,) * (2 if with_bf16 else 1),
        out_shape=(jax.ShapeDtypeStruct((m, n), F32),) + ((jax.ShapeDtypeStruct((m, n), BF16),) if with_bf16 else ()),
        compiler_params=_params(("parallel", "arbitrary")),
    )(a, b)
    return res if with_bf16 else res[0]


def _modulate_fwd(x, ctx, prm, tm, rider=None):
    t_rows, dm = x.shape
    lat = t_rows // tm
    r = t_rows + ctx.shape[0]

    def body(x_ref, c_ref, p_ref, h_ref):
        xv = jnp.where(pl.program_id(0) >= lat, c_ref[...], x_ref[...])
        mu = jnp.mean(xv, axis=-1, keepdims=True)
        xm = xv - mu
        n = xm * lax.rsqrt(jnp.mean(xm * xm, axis=-1, keepdims=True) + LN_EPS)
        h_ref[...] = (n * (1.0 + p_ref[0, 1:2, :]) + p_ref[0, 0:1, :]).astype(BF16)

    (h,), rode = _call(
        body, (x, ctx, prm), name="modulate_fwd", grid=(r // tm,),
        in_specs=[pl.BlockSpec((tm, dm), lambda i: (jnp.minimum(i, lat - 1), 0)),
                  pl.BlockSpec((tm, dm), lambda i: (jnp.maximum(i - lat, 0), 0)),
                  pl.BlockSpec((1, 8, dm), lambda i: ((i >= lat).astype(jnp.int32), 0, 0))],
        out_specs=[pl.BlockSpec((tm, dm), lambda i: (i, 0))],
        out_shape=[jax.ShapeDtypeStruct((r, dm), BF16)], sem=("parallel",), rider=rider)
    return h, rode


def _modulate_bwd(x, ctx, dh, prm, gx_direct, tm, rider=None):
    t_rows, dm = x.shape
    lat, n_ct = t_rows // tm, ctx.shape[0] // tm
    is_ctx = lambda i: i < n_ct
    cls = lambda i: is_ctx(i).astype(jnp.int32)
    lat_tile = lambda i: (jnp.maximum(i - n_ct, 0), 0)

    def body(x_ref, c_ref, dh_ref, p_ref, gd_ref, gx_ref, acc_ref):
        i = pl.program_id(0)

        @pl.when((i == 0) | (i == n_ct))
        def _():
            acc_ref[...] = jnp.zeros_like(acc_ref)

        x = jnp.where(is_ctx(i), c_ref[...], x_ref[...])
        dh_v = dh_ref[...]
        mu = jnp.mean(x, axis=-1, keepdims=True)
        xm = x - mu
        rstd = lax.rsqrt(jnp.mean(xm * xm, axis=-1, keepdims=True) + LN_EPS)
        n = xm * rstd
        acc_ref[0, 0:1, :] += jnp.sum(dh_v, axis=0, keepdims=True)
        acc_ref[0, 1:2, :] += jnp.sum(dh_v * n, axis=0, keepdims=True)
        dn = dh_v * (1.0 + p_ref[0, 1:2, :])
        dx = rstd * (dn - jnp.mean(dn, axis=-1, keepdims=True) - n * jnp.mean(dn * n, axis=-1, keepdims=True))
        gx_ref[...] = dx + gd_ref[...]

    return _call(
        body, (x, ctx, dh, prm, gx_direct), name="modulate_bwd", grid=(n_ct + lat,),
        in_specs=[pl.BlockSpec((tm, dm), lat_tile),
                  pl.BlockSpec((tm, dm), lambda i: (jnp.minimum(i, n_ct - 1), 0)),
                  pl.BlockSpec((tm, dm), lambda i: (jnp.where(is_ctx(i), lat + i, i - n_ct), 0)),
                  pl.BlockSpec((1, 8, dm), lambda i: (cls(i), 0, 0)),
                  pl.BlockSpec((tm, dm), lat_tile)],
        out_specs=(pl.BlockSpec((tm, dm), lat_tile), pl.BlockSpec((1, 8, dm), lambda i: (cls(i), 0, 0))),
        out_shape=(jax.ShapeDtypeStruct((t_rows, dm), F32), jax.ShapeDtypeStruct((2, 8, dm), F32)),
        sem=("arbitrary",), rider=rider)


def _conv_parts(t_rows, c_rows):
    return ((0, t_rows, t_rows // GRID_W, GRID_W), (t_rows, c_rows, 1, c_rows))


def _col_shifts(x2, rows_g, width_g):
    n, ct = x2.shape
    col = lax.broadcasted_iota(jnp.int32, (width_g, ct), 0)
    as_grid = lambda a: a.reshape(rows_g, width_g, ct)
    left = as_grid(pltpu.roll(x2, 1, 0)) * (col >= 1).astype(F32)
    right = as_grid(pltpu.roll(x2, n - 1, 0)) * (col <= width_g - 2).astype(F32)
    return [left, as_grid(x2), right]


CONV_BLOCK_ROWS = 4


def _conv_blocks(t_rows, c_rows):
    for t0, _, rows_g, width_g in _conv_parts(t_rows, c_rows):
        nb = min(CONV_BLOCK_ROWS, rows_g)
        assert rows_g % nb == 0
        for g0 in range(0, rows_g, nb):
            yield t0, rows_g, width_g, g0, nb


def _slab(ref, t0, rows_g, width_g, g0, nb):
    if rows_g == 1:
        return ref[t0:t0 + width_g, :]
    lo, hi = max(g0 - 1, 0), min(g0 + nb + 1, rows_g)
    parts = [ref[t0 + lo * width_g:t0 + hi * width_g, :]]
    zero = jnp.zeros((width_g, ref.shape[1]), F32)
    if g0 == 0:
        parts.insert(0, zero)
    if g0 + nb == rows_g:
        parts.append(zero)
    return jnp.concatenate(parts, axis=0)


def _conv_taps(cols, w_ref, nb, flip):
    one_row = cols[0].shape[0] == nb
    acc = None
    for a in range(3):
        if one_row and a != 1:
            continue
        for b in range(3):
            tap = (2 - a) * 3 + (2 - b) if flip else a * 3 + b
            term = (cols[b] if one_row else cols[b][a:a + nb]) * w_ref[tap:tap + 1, :]
            acc = term if acc is None else acc + term
    return acc


def _conv_fwd(u, conv_w9, conv_b, t_rows, c_rows, w, ct):
    r = u.shape[0]
    base = 5 * w // ct

    def body(x_ref, w_ref, b_ref, o_ref):
        for t0, rows_g, width_g, g0, nb in _conv_blocks(t_rows, c_rows):
            slab = _slab(x_ref, t0, rows_g, width_g, g0, nb)
            cols = _col_shifts(slab, slab.shape[0] // width_g, width_g)
            pre = _conv_taps(cols, w_ref, nb, False) + b_ref[...]
            o_ref[t0 + g0 * width_g:t0 + (g0 + nb) * width_g, :] = _silu(pre).reshape(nb * width_g, ct)

    return pl.pallas_call(
        body, name="conv_fwd", grid=(2 * w // ct,),
        in_specs=[pl.BlockSpec((r, ct), lambda i: (0, base + i)), pl.BlockSpec((9, ct), lambda i: (0, i)),
                  pl.BlockSpec((1, ct), lambda i: (0, i))],
        out_specs=pl.BlockSpec((r, ct), lambda i: (0, i)),
        out_shape=jax.ShapeDtypeStruct((r, 2 * w), F32),
        compiler_params=_params(("parallel",)),
    )(u, conv_w9, conv_b)


def _conv_bwd(u, dqk_pair, conv_w9, conv_b, t_rows, c_rows, w, ct):
    r = u.shape[0]
    base = 5 * w // ct

    def body(x_ref, d1_ref, d2_ref, w_ref, b_ref, dx_ref, dw_ref, db_ref, dpre_ref):
        dw = [jnp.zeros((1, ct), F32) for _ in range(9)]
        db = jnp.zeros((1, ct), F32)
        for t0, rows_g, width_g, g0, nb in _conv_blocks(t_rows, c_rows):
            rows = slice(t0 + g0 * width_g, t0 + (g0 + nb) * width_g)
            slab = _slab(x_ref, t0, rows_g, width_g, g0, nb)
            cols = _col_shifts(slab, slab.shape[0] // width_g, width_g)
            pre = _conv_taps(cols, w_ref, nb, False) + b_ref[...]
            sg = _sigmoid(pre)
            dpre = (d1_ref[rows, :] + d2_ref[rows, :]).reshape(pre.shape) * (sg * (1.0 + pre * (1.0 - sg)))
            dpre_ref[rows, :] = dpre.reshape(nb * width_g, ct)
            db = db + jnp.sum(jnp.sum(dpre, axis=0), axis=0, keepdims=True)
            for a in range(3):
                if rows_g == 1 and a != 1:
                    continue
                for b in range(3):
                    moved = cols[b] if rows_g == 1 else cols[b][a:a + nb]
                    dw[a * 3 + b] = dw[a * 3 + b] + jnp.sum(jnp.sum(moved * dpre, axis=0), axis=0, keepdims=True)
        for t0, rows_g, width_g, g0, nb in _conv_blocks(t_rows, c_rows):
            slab = _slab(dpre_ref, t0, rows_g, width_g, g0, nb)
            cols = _col_shifts(slab, slab.shape[0] // width_g, width_g)
            dx_ref[t0 + g0 * width_g:t0 + (g0 + nb) * width_g, :] = _conv_taps(cols, w_ref, nb, True).reshape(
                nb * width_g, ct).astype(BF16)
        for tap in range(9):
            dw_ref[tap:tap + 1, :] = dw[tap]
        db_ref[...] = db

    return pl.pallas_call(
        body, name="conv_bwd", grid=(2 * w // ct,),
        in_specs=[pl.BlockSpec((r, ct), lambda i: (0, base + i)), pl.BlockSpec((r, ct), lambda i: (0, i)),
                  pl.BlockSpec((r, ct), lambda i: (0, i)),
                  pl.BlockSpec((9, ct), lambda i: (0, i)), pl.BlockSpec((1, ct), lambda i: (0, i))],
        out_specs=(pl.BlockSpec((r, ct), lambda i: (0, i)), pl.BlockSpec((9, ct), lambda i: (0, i)),
                   pl.BlockSpec((1, ct), lambda i: (0, i))),
        out_shape=(jax.ShapeDtypeStruct((r, 2 * w), BF16), jax.ShapeDtypeStruct((9, 2 * w), F32),
                   jax.ShapeDtypeStruct((1, 2 * w), F32)),
        scratch_shapes=[pltpu.VMEM((r, ct), F32)],
        compiler_params=_params(("parallel",)),
    )(u, dqk_pair[0], dqk_pair[1], conv_w9, conv_b)


def _assemble_du(groups, gates, n_pad, tm):
    flat, layout = [], []
    for entry in list(groups) + [gates]:
        parts = entry if isinstance(entry, (tuple, list)) else (entry,)
        layout.append((len(flat), len(parts), parts[0].shape[1]))
        flat += list(parts)
    r = flat[0].shape[0]

    def body(*refs):
        o_ref = refs[-1]
        col = 0
        for first, count, width in layout:
            val = refs[first][...]
            for extra in range(1, count):
                val = val.astype(F32) + refs[first + extra][...].astype(F32)
            o_ref[:, col:col + width] = val.astype(BF16)
            col += width
        assert col == n_pad

    return pl.pallas_call(
        body, name="assemble_du", grid=(r // tm,),
        in_specs=[pl.BlockSpec((tm, a.shape[1]), lambda i: (i, 0)) for a in flat],
        out_specs=pl.BlockSpec((tm, n_pad), lambda i: (i, 0)),
        out_shape=jax.ShapeDtypeStruct((r, n_pad), BF16),
        compiler_params=_params(("parallel",)),
    )(*flat)


def _scan_order(n_lat, n_ctx, rev):
    n = n_lat + n_ctx
    if rev:
        return lambda j: n - 1 - j
    return lambda j: (j + n_lat) % n


DIRS = (False, True)


def _hg_scan_fwd(u, lb_full, w, n_lat, n_ctx, chunk, rider=None):
    r = u.shape[0]
    n_heads = w // HG_DK
    sub = HG_CHUNKS_PER_STEP if n_lat % HG_CHUNKS_PER_STEP == 0 and n_ctx % HG_CHUNKS_PER_STEP == 0 else 1
    n_steps = (n_lat + n_ctx) // sub
    nat = [_scan_order(n_lat // sub, n_ctx // sub, rev) for rev in DIRS]
    rows = sub * chunk

    def body(*refs):
        ins, outs, scratch = refs[:8], refs[8:12], refs[12:]

        @pl.when(pl.program_id(0) == 0)
        def _():
            for s_ref in scratch:
                s_ref[...] = jnp.zeros_like(s_ref)

        for d, rev in enumerate(DIRS):
            aq, af, ai, lb_ref = ins[4 * d:4 * d + 4]
            o_ref, save_ref = outs[2 * d:2 * d + 2]
            state = [scratch[d][h] for h in range(n_heads)]
            for p in range(sub):
                sl = slice((sub - 1 - p if rev else p) * chunk, (sub - p if rev else p + 1) * chunk)
                for h in range(n_heads):
                    save_ref[0, p, h] = state[h]
                state, o = _hg_chunk(state, aq[sl, :], af[sl, :], ai[sl, :], lb_ref[0, 0:1, :], lb_ref[0, 1:2, :], rev)
                o_ref[sl, :] = o
            for h in range(n_heads):
                scratch[d][h] = state[h]

    in_specs, out_specs, out_shape = [], [], []
    for d in range(2):
        in_specs += [pl.BlockSpec((rows, w), lambda j, d=d: (nat[d](j), 0)),
                     pl.BlockSpec((rows, w), lambda j, d=d: (nat[d](j), 1 + d)),
                     pl.BlockSpec((rows, w), lambda j, d=d: (nat[d](j), 3)),
                     pl.BlockSpec((1, 2, w), lambda j, d=d: (d, 0, 0))]
        out_specs += [pl.BlockSpec((rows, w), lambda j, d=d: (nat[d](j), 0)),
                      pl.BlockSpec((1, sub, n_heads, HG_DK, HG_DK), lambda j: (j, 0, 0, 0, 0))]
        out_shape += [jax.ShapeDtypeStruct((r, w), F32),
                      jax.ShapeDtypeStruct((n_steps, sub, n_heads, HG_DK, HG_DK), F32)]
    (o_f, s_f, o_b, s_b), rode = _call(
        body, (u, u, u, lb_full, u, u, u, lb_full), name="hg_scan_fwd", grid=(n_steps,), in_specs=in_specs,
        out_specs=out_specs, out_shape=out_shape, scratch_shapes=[pltpu.VMEM((n_heads, HG_DK, HG_DK), F32)] * 2,
        sem=("arbitrary",), rider=rider)
    return (o_f, o_b), (s_f, s_b), rode


def _hg_scan_bwd(u, lb_full, saved, d_o, w, n_lat, n_ctx, chunk, rider=None):
    r = u.shape[0]
    n_heads = w // HG_DK
    n_steps, sub = saved[0].shape[0], saved[0].shape[1]
    n_lat_s = n_lat // sub
    step = lambda jj: n_steps - 1 - jj
    nat = [(lambda jj, o=_scan_order(n_lat_s, n_ctx // sub, rev): o(step(jj))) for rev in DIRS]
    rows = sub * chunk

    def body(*refs):
        ins, outs, scratch = refs[:12], refs[12:20], refs[20:]
        jj = pl.program_id(0)

        @pl.when(jj == 0)
        def _():
            for d in range(2):
                scratch[d][...] = jnp.zeros_like(scratch[d])
                outs[4 * d + 3][...] = jnp.zeros_like(outs[4 * d + 3])

        for d, rev in enumerate(DIRS):
            aq, af, ai, lb_ref, save_ref, do_ref = ins[6 * d:6 * d + 6]
            daq_ref, daf_ref, dai_ref, dlb_ref = outs[4 * d:4 * d + 4]
            f = lambda st, a, b, c, l0, l1, rev=rev: _hg_chunk(st, a, b, c, l0, l1, rev)
            latent = (nat[d](jj) < n_lat_s).astype(F32)
            d_state = [scratch[d][h] for h in range(n_heads)]
            for p in reversed(range(sub)):
                sl = slice((sub - 1 - p if rev else p) * chunk, (sub - p if rev else p + 1) * chunk)
                _, vjp = jax.vjp(f, [save_ref[0, p, h] for h in range(n_heads)], aq[sl, :], af[sl, :], ai[sl, :],
                                 lb_ref[0, 0:1, :], lb_ref[0, 1:2, :])
                d_state, daq, daf, dai, dl0, dl1 = vjp((d_state, do_ref[sl, :] * latent))
                daq_ref[sl, :] = daq.astype(BF16)
                daf_ref[sl, :] = daf.astype(BF16)
                dai_ref[sl, :] = dai.astype(BF16)
                dlb_ref[0:1, :] += dl0
                dlb_ref[1:2, :] += dl1
            for h in range(n_heads):
                scratch[d][h] = d_state[h]

    in_specs, out_specs, out_shape, operands = [], [], [], []
    for d in range(2):
        row = lambda jj, d=d: (nat[d](jj), 0)
        in_specs += [pl.BlockSpec((rows, w), row),
                     pl.BlockSpec((rows, w), lambda jj, d=d: (nat[d](jj), 1 + d)),
                     pl.BlockSpec((rows, w), lambda jj, d=d: (nat[d](jj), 3)),
                     pl.BlockSpec((1, 2, w), lambda jj, d=d: (d, 0, 0)),
                     pl.BlockSpec((1, sub, n_heads, HG_DK, HG_DK), lambda jj: (step(jj), 0, 0, 0, 0)),
                     pl.BlockSpec((rows, w), lambda jj, d=d: (jnp.minimum(nat[d](jj), n_lat_s - 1), 0))]
        operands += [u, u, u, lb_full, saved[d], d_o]
        out_specs += [pl.BlockSpec((rows, w), row)] * 3 + [pl.BlockSpec((2, w), lambda jj: (0, 0))]
        out_shape += [jax.ShapeDtypeStruct((r, w), BF16)] * 3 + [jax.ShapeDtypeStruct((2, w), F32)]
    res, rode = _call(
        body, operands, name="hg_scan_bwd", grid=(n_steps,), in_specs=in_specs, out_specs=out_specs,
        out_shape=out_shape, scratch_shapes=[pltpu.VMEM((n_heads, HG_DK, HG_DK), F32)] * 2,
        sem=("arbitrary",), rider=rider)
    return res[0:4], res[4:8], rode


def _ml_state_shapes(n_chunks, n_heads, dh):
    return (jax.ShapeDtypeStruct((n_chunks, n_heads, dh, dh), F32),
            jax.ShapeDtypeStruct((n_chunks, n_heads, 1, dh), F32),
            jax.ShapeDtypeStruct((n_chunks, n_heads, 1, LANE), F32))


def _ml_state_specs(n_heads, dh, index):
    return (pl.BlockSpec((1, n_heads, dh, dh), lambda j: (index(j), 0, 0, 0)),
            pl.BlockSpec((1, n_heads, 1, dh), lambda j: (index(j), 0, 0, 0)),
            pl.BlockSpec((1, n_heads, 1, LANE), lambda j: (index(j), 0, 0, 0)))


def _ml_state_scratch(n_heads, dh):
    return [pltpu.VMEM((n_heads, dh, dh), F32), pltpu.VMEM((n_heads, 1, dh), F32), pltpu.VMEM((n_heads, 1, LANE), F32)]


def _ml_scan_fwd(qk, u, gate_b, w, n_heads, n_lat, n_ctx, chunk):
    r = u.shape[0]
    dh = w // n_heads
    n_chunks = n_lat + n_ctx
    nat = [_scan_order(n_lat, n_ctx, rev) for rev in DIRS]

    def body(*refs):
        ins, outs, scratch = refs[:10], refs[10:18], refs[18:]

        @pl.when(pl.program_id(0) == 0)
        def _():
            for s_ref in scratch:
                s_ref[...] = jnp.zeros_like(s_ref)

        results = []
        for d, rev in enumerate(DIRS):
            q, k, v, g, gb = ins[5 * d:5 * d + 5]
            state = tuple([ref[h] for h in range(n_heads)] for ref in scratch[3 * d:3 * d + 3])
            results.append((state, _ml_chunk(state, q[...], k[...], v[...], g[...], gb[...], rev, d)))
        for d, (state, (new, o)) in enumerate(results):
            outs[4 * d][...] = o
            for part in range(3):
                for h in range(n_heads):
                    outs[4 * d + 1 + part][0, h] = state[part][h]
                    scratch[3 * d + part][h] = new[part][h]

    in_specs, out_specs, out_shape = [], [], []
    for d in range(2):
        in_specs += [pl.BlockSpec((chunk,w), lambda j, d=d: (nat[d](j), 0)),
                     pl.BlockSpec((chunk,w), lambda j, d=d: (nat[d](j), 1)),
                     pl.BlockSpec((chunk,w), lambda j, d=d: (nat[d](j), 7)),
                     pl.BlockSpec((chunk,LANE), lambda j, d=d: (nat[d](j), 10 * w // LANE)),
                     pl.BlockSpec((1, LANE), lambda j: (0, 0))]
        out_specs += [pl.BlockSpec((chunk,w), lambda j, d=d: (nat[d](j), 0))]
        out_specs += list(_ml_state_specs(n_heads, dh, lambda j: j))
        out_shape += [jax.ShapeDtypeStruct((r, w), F32)] + list(_ml_state_shapes(n_chunks, n_heads, dh))
    res = pl.pallas_call(
        body, name="ml_scan_fwd", grid=(n_chunks,), in_specs=in_specs, out_specs=tuple(out_specs),
        out_shape=tuple(out_shape), scratch_shapes=_ml_state_scratch(n_heads, dh) * 2,
        compiler_params=_params(("arbitrary",)),
    )(qk, qk, u, u, gate_b, qk, qk, u, u, gate_b)
    return (res[0], res[4]), (res[1:4], res[5:8])


def _ml_scan_bwd(qk, u, gate_b, saved, d_h, w, n_heads, n_lat, n_ctx, chunk, rider=None):
    r = u.shape[0]
    dh = w // n_heads
    n_chunks = n_lat + n_ctx
    step = lambda jj: n_chunks - 1 - jj
    nat = [(lambda jj, o=_scan_order(n_lat, n_ctx, rev): o(step(jj))) for rev in DIRS]

    def body(*refs):
        ins, outs, scratch = refs[:18], refs[18:26], refs[26:]
        jj = pl.program_id(0)

        @pl.when(jj == 0)
        def _():
            for s_ref in scratch:
                s_ref[...] = jnp.zeros_like(s_ref)
            for d in range(2):
                outs[4 * d + 3][...] = jnp.zeros_like(outs[4 * d + 3])

        results = []
        for d, rev in enumerate(DIRS):
            q, k, v, g, gb, sc, sn, sm, dh_ref = ins[9 * d:9 * d + 9]
            state = tuple([ref[0, h] for h in range(n_heads)] for ref in (sc, sn, sm))
            f = lambda st, a, b, c, gg, bb, rev=rev, d=d: _ml_chunk(st, a, b, c, gg, bb, rev, d)
            _, vjp = jax.vjp(f, state, q[...], k[...], v[...], g[...], gb[...])
            d_state = tuple([ref[h] for h in range(n_heads)] for ref in scratch[3 * d:3 * d + 3])
            d_out = dh_ref[...] * (nat[d](jj) < n_lat).astype(F32)
            results.append(vjp((d_state, d_out)))
        for d, (d_state, dq, dk, dv, dg, dgb) in enumerate(results):
            dqk_ref, dv_ref, dg_ref, dgb_ref = outs[4 * d:4 * d + 4]
            for part in range(3):
                for h in range(n_heads):
                    scratch[3 * d + part][h] = d_state[part][h]
            dqk_ref[:, 0:w] = dq
            dqk_ref[:, w:2 * w] = dk
            dv_ref[...] = dv.astype(BF16)
            dg_ref[...] = dg
            dgb_ref[...] += dgb

    in_specs, out_specs, out_shape, operands = [], [], [], []
    for d in range(2):
        row = lambda jj, d=d: (nat[d](jj), 0)
        in_specs += [pl.BlockSpec((chunk,w), row), pl.BlockSpec((chunk,w), lambda jj, d=d: (nat[d](jj), 1)),
                     pl.BlockSpec((chunk,w), lambda jj, d=d: (nat[d](jj), 7)),
                     pl.BlockSpec((chunk,LANE), lambda jj, d=d: (nat[d](jj), 10 * w // LANE)),
                     pl.BlockSpec((1, LANE), lambda jj: (0, 0))]
        in_specs += list(_ml_state_specs(n_heads, dh, step))
        in_specs += [pl.BlockSpec((chunk,w), lambda jj, d=d: (jnp.minimum(nat[d](jj), n_lat - 1), 0))]
        operands += [qk, qk, u, u, gate_b, *saved[d], d_h]
        out_specs += [pl.BlockSpec((chunk,2 * w), row), pl.BlockSpec((chunk,w), row),
                      pl.BlockSpec((chunk,LANE), row), pl.BlockSpec((1, LANE), lambda jj: (0, 0))]
        out_shape += [jax.ShapeDtypeStruct((r, 2 * w), F32), jax.ShapeDtypeStruct((r, w), BF16),
                      jax.ShapeDtypeStruct((r, LANE), F32), jax.ShapeDtypeStruct((1, LANE), F32)]
    res, rode = _call(
        body, operands, name="ml_scan_bwd", grid=(n_chunks,), in_specs=in_specs, out_specs=out_specs,
        out_shape=out_shape, scratch_shapes=_ml_state_scratch(n_heads, dh) * 2, sem=("arbitrary",), rider=rider)
    return res[0:4], res[4:8], rode


def _post_specs(w, tm, lat_tiles, cols):
    return [pl.BlockSpec((tm, w), (lambda i, cb=cb: (jnp.minimum(i, lat_tiles - 1), cb))) for cb in cols]


def _post_fwd(o_f, o_b, h_f, h_b, u, wa, wb, t_rows, w, n_hg, n_ml, tm):
    lat_tiles = t_rows // tm

    def body(of, ob, hf, hb, az, bo, bz, wa_ref, wb_ref, y_ref):
        y_ref[...] = _post_fn(of[...], ob[...], az[...], hf[...], hb[...], bo[...], bz[...],
                              wa_ref[...], wb_ref[...], n_hg, n_ml).astype(BF16)

    rows = pl.BlockSpec((tm, w), lambda i: (i, 0))
    vec = pl.BlockSpec((1, w), lambda i: (0, 0))
    return pl.pallas_call(
        body, name="post_fwd", grid=(lat_tiles,),
        in_specs=[rows] * 4 + _post_specs(w, tm, lat_tiles, (4, 8, 9)) + [vec, vec],
        out_specs=pl.BlockSpec((tm, 2 * w), lambda i: (i, 0)),
        out_shape=jax.ShapeDtypeStruct((t_rows, 2 * w), BF16),
        compiler_params=_params(("parallel",)),
    )(o_f, o_b, h_f, h_b, u, u, u, wa, wb)


def _post_bwd(o_f, o_b, h_f, h_b, u, wa, wb, dy, t_rows, w, n_hg, n_ml, tm, rider=None):
    r = u.shape[0]
    lat_tiles = t_rows // tm
    lat = lambda i: (jnp.minimum(i, lat_tiles - 1), 0)

    def body(of, ob, hf, hb, az, bo, bz, wa_ref, wb_ref, dy_ref, do_ref, dh_ref, daz_ref, dbo_ref, dbz_ref,
             dwa_ref, dwb_ref):
        i = pl.program_id(0)

        @pl.when(i == 0)
        def _():
            dwa_ref[...] = jnp.zeros_like(dwa_ref)
            dwb_ref[...] = jnp.zeros_like(dwb_ref)

        @pl.when(i < lat_tiles)
        def _():
            f = functools.partial(_post_fn, n_hg=n_hg, n_ml=n_ml)
            _, vjp = jax.vjp(f, of[...], ob[...], az[...], hf[...], hb[...], bo[...], bz[...], wa_ref[...], wb_ref[...])
            d_of, _, d_az, d_hf, _, d_bo, d_bz, d_wa, d_wb = vjp(dy_ref[...])
            do_ref[...] = d_of
            dh_ref[...] = d_hf
            daz_ref[...] = d_az.astype(BF16)
            dbo_ref[...] = d_bo.astype(BF16)
            dbz_ref[...] = d_bz.astype(BF16)
            dwa_ref[...] += d_wa
            dwb_ref[...] += d_wb

        @pl.when(i >= lat_tiles)
        def _():
            daz_ref[...] = jnp.zeros_like(daz_ref)
            dbo_ref[...] = jnp.zeros_like(dbo_ref)
            dbz_ref[...] = jnp.zeros_like(dbz_ref)

    lat_rows = pl.BlockSpec((tm, w), lat)
    all_rows = pl.BlockSpec((tm, w), lambda i: (i, 0))
    vec = pl.BlockSpec((1, w), lambda i: (0, 0))
    sd_t = jax.ShapeDtypeStruct((t_rows, w), F32)
    sd_r = jax.ShapeDtypeStruct((r, w), BF16)
    sd_v = jax.ShapeDtypeStruct((1, w), F32)
    return _call(
        body, (o_f, o_b, h_f, h_b, u, u, u, wa, wb, dy), name="post_bwd", grid=(r // tm,),
        in_specs=[lat_rows] * 4 + _post_specs(w, tm, lat_tiles, (4, 8, 9)) + [vec, vec]
        + [pl.BlockSpec((tm, 2 * w), lat)],
        out_specs=(lat_rows, lat_rows, all_rows, all_rows, all_rows, vec, vec),
        out_shape=(sd_t, sd_t, sd_r, sd_r, sd_r, sd_v, sd_v), sem=("arbitrary",), rider=rider)


OUT_ROW_GATE, OUT_ROW_LN_G, OUT_ROW_LN_B, OUT_ROW_LOSS = 0, 1, 2, 3


def _out_block(y, w_out, x, target, prm, tm):
    t_rows, dm = x.shape
    di = y.shape[1]

    def body(y_ref, w_ref, x_ref, t_ref, p_ref, dz_ref, dy_ref, gx_ref, acc_ref):
        @pl.when(pl.program_id(0) == 0)
        def _():
            acc_ref[...] = jnp.zeros_like(acc_ref)

        gate, ln_g, ln_b = p_ref[0:1, :], p_ref[1:2, :], p_ref[2:3, :]
        z = _nn(y_ref[...], w_ref[...])
        res = ALPHA * x_ref[...] + gate * z
        mu = jnp.mean(res, axis=-1, keepdims=True)
        rc = res - mu
        rstd = lax.rsqrt(jnp.mean(rc * rc, axis=-1, keepdims=True) + LN_EPS)
        rn = rc * rstd
        err = rn * ln_g + ln_b - t_ref[...]
        d_out = err * (1.0 / dm)
        d_rn = d_out * ln_g
        d_res = rstd * (d_rn - jnp.mean(d_rn, axis=-1, keepdims=True)
                        - rn * jnp.mean(d_rn * rn, axis=-1, keepdims=True))
        acc_ref[OUT_ROW_GATE:OUT_ROW_GATE + 1, :] += jnp.sum(d_res * z, axis=0, keepdims=True)
        acc_ref[OUT_ROW_LN_G:OUT_ROW_LN_G + 1, :] += jnp.sum(d_out * rn, axis=0, keepdims=True)
        acc_ref[OUT_ROW_LN_B:OUT_ROW_LN_B + 1, :] += jnp.sum(d_out, axis=0, keepdims=True)
        acc_ref[OUT_ROW_LOSS:OUT_ROW_LOSS + 1, :] += (0.5 / dm) * jnp.sum(err * err, axis=0, keepdims=True)
        gx_ref[...] = ALPHA * d_res
        dz = (d_res * gate).astype(BF16)
        dz_ref[...] = dz
        dy_ref[...] = _nt(dz, w_ref[...])

    rows_d = pl.BlockSpec((tm, dm), lambda i: (i, 0))
    rows_i = pl.BlockSpec((tm, di), lambda i: (i, 0))
    return pl.pallas_call(
        body, name="out_block", grid=(t_rows // tm,),
        in_specs=[rows_i, pl.BlockSpec((di, dm), lambda i: (0, 0)), rows_d, rows_d,
                  pl.BlockSpec((8, dm), lambda i: (0, 0))],
        out_specs=(rows_d, rows_i, rows_d, pl.BlockSpec((8, dm), lambda i: (0, 0))),
        out_shape=(jax.ShapeDtypeStruct((t_rows, dm), BF16), jax.ShapeDtypeStruct((t_rows, di), F32),
                   jax.ShapeDtypeStruct((t_rows, dm), F32), jax.ShapeDtypeStruct((8, dm), F32)),
        compiler_params=_params(("arbitrary",)),
    )(y, w_out, x, target, prm)


def _mod_fwd(c16, w_mod, tn):
    dm, n = w_mod.shape

    def body(c_ref, w_ref, o_ref, a_ref):
        a = _silu(c_ref[...])
        a_ref[...] = a
        o_ref[...] = _nn(a, w_ref[...], HIGHEST)

    return pl.pallas_call(
        body, name="mod_fwd", grid=(n // tn,),
        in_specs=[pl.BlockSpec((16, dm), lambda j: (0, 0)), pl.BlockSpec((dm, tn), lambda j: (0, j))],
        out_specs=(pl.BlockSpec((16, tn), lambda j: (0, j)), pl.BlockSpec((16, dm), lambda j: (0, 0))),
        out_shape=(jax.ShapeDtypeStruct((16, n), F32), jax.ShapeDtypeStruct((16, dm), F32)),
        compiler_params=_params(("arbitrary",)),
    )(c16, w_mod)


def _mod_bwd(a16, dm16, w_mod, tn, rider=None):
    dm, n = w_mod.shape

    def body(a_ref, d_ref, w_ref, dw_ref, dc_ref):
        @pl.when(pl.program_id(0) == 0)
        def _():
            dc_ref[...] = jnp.zeros_like(dc_ref)
        dw_ref[...] = _tn(a_ref[...], d_ref[...], HIGHEST)
        dc_ref[...] += _nt(d_ref[...], w_ref[...], HIGHEST)

    return _call(
        body, (a16, dm16, w_mod), name="mod_bwd", grid=(n // tn,),
        in_specs=[pl.BlockSpec((16, dm), lambda j: (0, 0)), pl.BlockSpec((16, tn), lambda j: (0, j)),
                  pl.BlockSpec((dm, tn), lambda j: (0, j))],
        out_specs=(pl.BlockSpec((dm, tn), lambda j: (0, j)), pl.BlockSpec((16, dm), lambda j: (0, 0))),
        out_shape=(jax.ShapeDtypeStruct((dm, n), F32), jax.ShapeDtypeStruct((16, dm), F32)),
        sem=("arbitrary",), rider=rider)


def _sum_devices(g, fold_rows):
    n_dev, rows, n = g.shape

    def body(g_ref, s_ref, t_ref):
        s = g_ref[0]
        for dev in range(1, n_dev):
            s = s + g_ref[dev]
        t_ref[...] = jnp.broadcast_to(jnp.sum(s, axis=-1, keepdims=True), (rows, LANE))
        s_ref[...] = s
        s_ref[0:fold_rows, :] = s[0:fold_rows] + s[fold_rows:2 * fold_rows]

    return pl.pallas_call(
        body, name="sum_devices",
        out_shape=(jax.ShapeDtypeStruct((rows, n), F32), jax.ShapeDtypeStruct((rows, LANE), F32)),
        compiler_params=_params(),
    )(g)


def _c_ctx_grad(parts, c_ctx_row):
    def body(p_ref, c_ref, o_ref):
        s = p_ref[0]
        for chip in range(1, N_CHIPS):
            s = s + p_ref[2 * chip]
        cv = c_ref[...]
        sg = _sigmoid(cv)
        o_ref[...] = s * (sg * (1.0 + cv * (1.0 - sg)))

    return pl.pallas_call(
        body, name="c_ctx_grad", out_shape=jax.ShapeDtypeStruct(parts.shape[1:], F32), compiler_params=_params(),
    )(parts, c_ctx_row)


def _sum_pair(name, mine, got):
    def body(a_ref, b_ref, o_ref):
        o_ref[...] = (a_ref[...] + b_ref[...]).astype(BF16)

    k, rows, n = mine.shape
    tl = _largest_divisor(n, max(LANE, (1 << 18) // rows), LANE)
    spec = pl.BlockSpec((1, rows, tl), lambda kk, i: (kk, 0, i))
    return pl.pallas_call(
        body, name=name, grid=(k, n // tl), in_specs=[spec, spec], out_specs=spec,
        out_shape=jax.ShapeDtypeStruct(mine.shape, BF16), compiler_params=_params(("parallel", "parallel")),
    )(mine, got)


def _sum_pair_lanes(name, full, got, ci):
    rows, n = got.shape
    tr = _largest_divisor(rows, max(SUBLANE_BF16, (1 << 19) // n), SUBLANE_BF16)

    def body(ci_ref, a_ref, b_ref, o_ref):
        o_ref[...] = (a_ref[...] + b_ref[...].astype(F32)).astype(BF16)

    return pl.pallas_call(
        body, name=name,
        grid_spec=pltpu.PrefetchScalarGridSpec(
            num_scalar_prefetch=1, grid=(rows // tr,),
            in_specs=[pl.BlockSpec((tr, n), lambda i, c: (i, c[0])), pl.BlockSpec((tr, n), lambda i, c: (i, 0))],
            out_specs=pl.BlockSpec((tr, n), lambda i, c: (i, 0))),
        out_shape=jax.ShapeDtypeStruct((rows, n), BF16), compiler_params=_params(("parallel",)),
    )(ci.reshape(1).astype(jnp.int32), full, got)


def _sum_chips(name, got, own, chip):
    k, rows, n = got.shape
    tl = _largest_divisor(n, max(LANE, (1 << 18) // rows), LANE)

    def body(chip_ref, g_ref, own_ref, o_ref):
        total = None
        for kk in range(k):
            term = jnp.where(chip_ref[0] == kk, own_ref[0], g_ref[kk]).astype(F32)
            total = term if total is None else total + term
        o_ref[...] = total

    return pl.pallas_call(
        body, name=name,
        grid_spec=pltpu.PrefetchScalarGridSpec(
            num_scalar_prefetch=1, grid=(n // tl,),
            in_specs=[pl.BlockSpec((k, rows, tl), lambda i, c: (0, 0, i)),
                      pl.BlockSpec((1, rows, tl), lambda i, c: (c[0], 0, i))],
            out_specs=pl.BlockSpec((rows, tl), lambda i, c: (0, i))),
        out_shape=jax.ShapeDtypeStruct((rows, n), F32), compiler_params=_params(("parallel",)),
    )(chip.reshape(1).astype(jnp.int32), got, own)


def _adamw_update(w, g, m, v):
    m2 = ADAM_B1 * m + (1.0 - ADAM_B1) * g
    v2 = ADAM_B2 * v + (1.0 - ADAM_B2) * jnp.square(g)
    m_hat = m2 / (1.0 - ADAM_B1 ** ADAM_STEP)
    v_hat = v2 / (1.0 - ADAM_B2 ** ADAM_STEP)
    return -ADAM_LR * (m_hat / (jnp.sqrt(v_hat) + ADAM_EPS) + ADAM_WD * w), m2, v2


def _adamw(name, w, g, m, v, rider=None):
    rows, n = w.shape
    if rows % 8 == 0:
        tr = _largest_divisor(rows, max(8, (1 << 18) // n), 8)
        block, index, steps = (tr, n), (lambda i: (i, 0)), rows // tr
    else:
        tl = _largest_divisor(n, max(LANE, (1 << 18) // rows), LANE)
        block, index, steps = (rows, tl), (lambda i: (0, i)), n // tl

    def body(w_ref, g_ref, m_ref, v_ref, d_ref, mo_ref, vo_ref):
        d_ref[...], mo_ref[...], vo_ref[...] = _adamw_update(w_ref[...], g_ref[...], m_ref[...], v_ref[...])

    spec = pl.BlockSpec(block, index)
    sds = jax.ShapeDtypeStruct((rows, n), F32)
    return _call(body, (w, g, m, v), name=name, grid=(steps,), in_specs=[spec] * 4, out_specs=(spec,) * 3,
                 out_shape=(sds, sds, sds), sem=("parallel",), rider=rider)


PACK_LANES = 1024


def _pack(pieces):
    flat = jnp.concatenate([p.reshape(-1) for p in pieces])
    total = -(-flat.shape[0] // (8 * PACK_LANES)) * 8 * PACK_LANES
    return jnp.pad(flat, (0, total - flat.shape[0])).reshape(-1, PACK_LANES)


def _unpack(packed, shapes):
    flat = packed.reshape(-1)
    out, off = [], 0
    for shp in shapes:
        size = math.prod(shp)
        out.append(flat[off:off + size].reshape(shp))
        off += size
    return out


def _rows8(rows, width):
    flat = [r.reshape(width) for r in rows] + [jnp.zeros(((8 - len(rows)) * width,), F32)]
    return jnp.concatenate(flat).reshape(8, width)


def kernel(x, c, ctx, c_ctx, w_mod, b_mod, w_in, conv_w, conv_b, hg_lb, ml_gate_b, hg_norm_w, ml_norm_w, w_out, ln_g, ln_b, loss_target, m_c_ctx, m_w_mod, m_b_mod, m_w_in, m_conv_w, m_conv_b, m_hg_lb, m_ml_gate_b, m_hg_norm_w, m_ml_norm_w, m_w_out, m_ln_g, m_ln_b, v_c_ctx, v_w_mod, v_b_mod, v_w_in, v_conv_w, v_conv_b, v_hg_lb, v_ml_gate_b, v_hg_norm_w, v_ml_norm_w, v_w_out, v_ln_g, v_ln_b):
    t_rows, dm = x.shape[1], x.shape[2]
    c_rows = ctx.shape[1]
    w = hg_norm_w.shape[1]
    n_ml = ml_gate_b.shape[-1]
    n_hg = w // HG_DK
    di = 2 * w
    n_in = 10 * w + 4 * n_ml
    ns = w_in.shape[2]
    nm = w_mod.shape[2]
    n_pad = 10 * w + LANE
    r_rows = t_rows + c_rows
    row_gcd = math.gcd(t_rows, c_rows)
    hg_chunk, ml_chunk = math.gcd(HG_CHUNK, row_gcd), math.gcd(ML_CHUNK, row_gcd)
    hg_counts = (t_rows // hg_chunk, c_rows // hg_chunk, hg_chunk)
    ml_counts = (t_rows // ml_chunk, c_rows // ml_chunk, ml_chunk)
    assert ml_norm_w.shape[1] == w and di == dm and N_CHIPS * ns == n_in and N_CHIPS * nm == 3 * dm
    assert w_out.shape[1] * N_CHIPS == di and 4 * n_ml <= LANE and t_rows % GRID_W == 0

    xi, yi, ci = lax.axis_index("x"), lax.axis_index("y"), lax.axis_index("c")
    chip = 2 * xi + yi
    dev = 4 * xi + 2 * yi + ci

    tm = _largest_divisor(math.gcd(t_rows, c_rows), 256, 8)
    tm_mm = _largest_divisor(r_rows, 1088, SUBLANE_BF16)
    tn_mm = LANE * _largest_divisor(n_pad // LANE, 9)
    tn_mod = _largest_divisor(nm, 512, LANE)

    shard_shapes = [(dm,), (2, 2, w // N_CHIPS), (3, 3, di // N_CHIPS)]
    g1 = _all_gather8(_pack([c, hg_lb, conv_w])).run("gather_inputs")[0]
    per_dev = [_unpack(g1[i], shard_shapes) for i in range(N_DEV)]
    c_all = jnp.stack([p[0] for p in per_dev])
    lb_full = jnp.concatenate([per_dev[2 * k][1] for k in range(N_CHIPS)], axis=-1)
    conv_w9 = jnp.concatenate([per_dev[2 * k][2] for k in range(N_CHIPS)], axis=-1).reshape(9, di)

    c16 = jnp.concatenate([c_all, c_ctx[None], jnp.zeros((16 - N_DEV - 1, dm), F32)])
    mod_part, a16 = _mod_fwd(c16, w_mod[0], tn_mod)
    g2 = _all_gather8(mod_part).run("gather_mod")[0]
    mod_all = jnp.concatenate([g2[2 * k] for k in range(N_CHIPS)], axis=1) + b_mod
    mod_x = lax.dynamic_index_in_dim(mod_all, dev, 0, keepdims=False).reshape(3, dm)
    mod_c = mod_all[N_DEV].reshape(3, dm)
    prm = jnp.stack([_rows8(list(mod_x), dm), _rows8(list(mod_c), dm)])

    as_t = lambda a: jnp.transpose(a[0])
    half_in = lax.dynamic_slice_in_dim(as_t(w_in).astype(BF16), ci * (dm // 2), dm // 2, 1)
    half_out = lax.dynamic_slice_in_dim(w_out[0].astype(BF16), ci * (di // (2 * N_CHIPS)), di // (2 * N_CHIPS), 0)
    lanes_of = lambda core: pl.ds(core * (dm // 2), dm // 2)
    landing = lambda s, r: (_chip_of(s), slice(None), lanes_of(s[2]))
    own_placed = lax.dynamic_update_slice(jnp.zeros((N_CHIPS + 1, ns, dm), BF16),
                                          as_t(w_in).astype(BF16)[None], (chip, 0, 0))
    gather_in = _Exchange([half_in, own_placed], [jax.ShapeDtypeStruct(own_placed.shape, BF16)],
                          [(mask, 0, None, 0, landing) for mask in CHIP_MASKS[:2]], in_place={1: 0})

    hc, (gw_in,) = _modulate_fwd(x[0], ctx[0], prm, tm, rider=gather_in)
    chip_own = lambda s: 2 * s[0] + s[1]
    chip_x = lambda s: 2 * (1 - s[0]) + s[1]
    chip_y = lambda s: 2 * s[0] + 1 - s[1]
    chip_xy = lambda s: 2 * (1 - s[0]) + 1 - s[1]
    quarter = lambda core, q: pl.ds(core * (dm // 2) + q * (dm // 4), dm // 4)
    half_of = lambda which: (lambda s, r: (which(s), slice(None), lanes_of(s[2])))
    relay_x = lambda s, r: (chip_y(s), slice(None), quarter(s[2], 0))
    relay_y = lambda s, r: (chip_x(s), slice(None), quarter(s[2], 1))
    whole = jax.ShapeDtypeStruct(gw_in.shape, BF16)
    gw_in = _Exchange([gw_in], [whole],
                      [((1, 0, 0), 0, relay_x, 0, relay_x), ((0, 1, 0), 0, relay_y, 0, relay_y)]
                      + [(SIBLING_MASK, 0, half_of(which), 0, half_of(which)) for which in (chip_own, chip_x, chip_y)],
                      in_place={0: 0}).run("relay_w_in")[0]
    gw_in = _Exchange([gw_in], [whole], [(SIBLING_MASK, 0, half_of(chip_xy), 0, half_of(chip_xy))],
                      in_place={0: 0}).run("gather_w_in_pair")[0]
    wt_full = gw_in.reshape((N_CHIPS + 1) * ns, dm)
    assert wt_full.shape[0] >= n_pad
    u, (got_out,) = _mm_nt("in_proj", hc, wt_full, n_pad, tm_mm, tn_mm, F32, rider=_all_gather_chips([half_out]))
    fetched_out = _own_block(chip, half_out, got_out)
    (o_f, o_b), hg_saved, (swapped_out,) = _hg_scan_fwd(u, lb_full, w, *hg_counts,
                                                         rider=_sibling_swap([fetched_out]))
    w_out_full = _join_halves(ci, fetched_out, swapped_out, 1).reshape(di, dm)
    qk = _conv_fwd(u, conv_w9, conv_b, t_rows, c_rows, w, LANE)
    gate_b_row = jnp.pad(ml_gate_b.reshape(1, -1), ((0, 0), (0, LANE - 4 * n_ml)))
    (h_f, h_b), ml_saved = _ml_scan_fwd(qk, u, gate_b_row, w, n_ml, *ml_counts)
    y = _post_fwd(o_f, o_b, h_f, h_b, u, hg_norm_w, ml_norm_w, t_rows, w, n_hg, n_ml, tm)
    prm_out = _rows8([mod_x[2], ln_g, ln_b], dm)
    dz, dy, gx_direct, acc_out = _out_block(y, w_out_full, x[0], loss_target[0], prm_out, tm)

    d_w_out = _mm_tn("d_w_out", y, dz, _largest_divisor(di, 1024, LANE),
                     _largest_divisor(t_rows, 1024, SUBLANE_BF16))
    d_w_out4 = d_w_out.reshape(N_CHIPS, 2, di // (2 * N_CHIPS), dm)
    mine_out = lax.dynamic_index_in_dim(d_w_out4, ci, 1, keepdims=False)
    other_out = lax.dynamic_index_in_dim(d_w_out4, 1 - ci, 1, keepdims=False)
    (d_o, d_h, d_az, d_bo, d_bz, d_wa, d_wb), (got_out,) = _post_bwd(
        o_f, o_b, h_f, h_b, u, hg_norm_w, ml_norm_w, dy, t_rows, w, n_hg, n_ml, tm, rider=_sibling_swap([other_out]))
    pair_out = _sum_pair("rs_pair_sum_w_out", mine_out, got_out)
    (d_aq_f, d_aff, d_ai_f, d_lb_f), (d_aq_b, d_afb, d_ai_b, d_lb_b), (landed_out,) = _hg_scan_bwd(
        u, lb_full, hg_saved, d_o, w, *hg_counts, rider=_chip_scatter([pair_out]))
    half_g_out = _sum_chips("rs_chip_sum_w_out", landed_out, pair_out, chip)
    (d_qk_f, d_v_f, d_g_f, d_gb_f), (d_qk_b, d_v_b, d_g_b, d_gb_b), (sibling_out,) = _ml_scan_bwd(
        qk, u, gate_b_row, ml_saved, d_h, w, n_ml, *ml_counts, rider=_sibling_swap([half_g_out]))
    g_w_out = _join_halves(ci, half_g_out, sibling_out, 0)
    d_bqk, d_cw, d_cb = _conv_bwd(u, (d_qk_f, d_qk_b), conv_w9, conv_b, t_rows, c_rows, w, LANE)
    du = _assemble_du([(d_aq_f, d_aq_b), d_aff, d_afb, (d_ai_f, d_ai_b), d_az, d_bqk, (d_v_f, d_v_b), d_bo, d_bz],
                      (d_g_f, d_g_b), n_pad, tm)
    d_wt_in, d_wt_in_bf16 = _mm_tn("d_w_in", du, hc, tn_mm, tm_mm, with_bf16=True)

    delta, new_m, new_v = {}, {}, {}
    res, (got_in,) = _adamw(
        "adamw_w_out", w_out[0], g_w_out, m_w_out[0], v_w_out[0],
        rider=_Exchange([d_wt_in_bf16], [jax.ShapeDtypeStruct((n_pad, dm // 2), BF16)],
                        [(SIBLING_MASK, 0, lambda s, r: (slice(None), lanes_of(r[2])), 0, None)]))
    delta["w_out"], new_m["w_out"], new_v["w_out"] = (a[None] for a in res)
    pair_half = _sum_pair_lanes("rs_pair_sum_w_in", d_wt_in, got_in, ci)
    pair_in = jnp.stack([pair_half[k * ns:(k + 1) * ns] for k in range(N_CHIPS)])
    d_hc, (landed_in,) = _mm_acc("d_h", du, wt_full, tm_mm, tn_mm, rider=_chip_scatter([pair_in]))
    half_g_in = _sum_chips("rs_chip_sum_w_in", landed_in, pair_in, chip)
    (gx, acc_mod), _ = _modulate_bwd(x[0], ctx[0], d_hc, prm, gx_direct, tm)
    grad_x = gx[None]

    zero_row = jnp.zeros((dm,), F32)
    d_gb = jnp.concatenate([d_gb_f[:, 0:n_ml], d_gb_b[:, n_ml:2 * n_ml], d_gb_f[:, 2 * n_ml:3 * n_ml],
                            d_gb_b[:, 3 * n_ml:4 * n_ml], jnp.zeros((1, dm - 4 * n_ml), F32)], axis=1)
    rows = [acc_mod[0, 0], acc_mod[0, 1], acc_out[OUT_ROW_GATE],
            acc_mod[1, 0], acc_mod[1, 1], zero_row]
    rows += list(d_cw) + [d_cb[0], d_lb_f.reshape(dm), d_lb_b.reshape(dm),
                          jnp.concatenate([d_wa[0], d_wb[0]]), acc_out[OUT_ROW_LN_G], acc_out[OUT_ROW_LN_B],
                          acc_out[OUT_ROW_LOSS], d_gb[0], zero_row]
    ROW_CW, ROW_CB, ROW_LB, ROW_NORM, ROW_LN_G, ROW_LN_B, ROW_LOSS, ROW_GB = 6, 15, 16, 18, 19, 20, 21, 22
    small_rows = jnp.concatenate([r.reshape(dm) for r in rows]).reshape(len(rows), dm)
    g3 = _all_gather8(small_rows).run("gather_small_grads")[0]
    sums, totals = _sum_devices(g3, 3)
    loss = totals[ROW_LOSS, 0]
    dm16 = jnp.concatenate([g3[:, 0:3, :].reshape(N_DEV, 3 * dm), sums[3:6].reshape(1, 3 * dm),
                            jnp.zeros((16 - N_DEV - 1, 3 * dm), F32)])
    (g_w_mod, dc16), (sibling_g_in,) = _mod_bwd(a16, lax.dynamic_slice_in_dim(dm16, chip * nm, nm, 1), w_mod[0],
                                                tn_mod, rider=_sibling_swap([half_g_in]))
    g_wt_in = _join_halves(ci, half_g_in, sibling_g_in, 1)
    g4 = _all_gather8(jnp.pad(dc16[N_DEV:N_DEV + 1], ((0, 7), (0, 0)))).run("gather_c_ctx")[0]
    g_c_ctx = _c_ctx_grad(g4, jnp.broadcast_to(c_ctx[None], (8, dm)))[0]
    res, _ = _adamw("adamw_w_in", as_t(w_in), g_wt_in, as_t(m_w_in), as_t(v_w_in))
    delta["w_in"], new_m["w_in"], new_v["w_in"] = (jnp.transpose(a)[None] for a in res)
    res, _ = _adamw("adamw_w_mod", w_mod[0], g_w_mod, m_w_mod[0], v_w_mod[0])
    delta["w_mod"], new_m["w_mod"], new_v["w_mod"] = (a[None] for a in res)

    chip_cols = lambda a, width: lax.dynamic_slice_in_dim(a, chip * width, width, a.ndim - 1)
    grads = {
        "c_ctx": g_c_ctx,
        "w_mod": g_w_mod[None],
        "b_mod": sums[0:3].reshape(1, 3 * dm),
        "w_in": jnp.transpose(g_wt_in)[None],
        "conv_w": chip_cols(sums[ROW_CW:ROW_CW + 9].reshape(1, 3, 3, di), di // N_CHIPS),
        "conv_b": sums[ROW_CB][None],
        "hg_lb": chip_cols(sums[ROW_LB:ROW_LB + 2].reshape(2, 2, w), w // N_CHIPS),
        "ml_gate_b": sums[ROW_GB, 0:4 * n_ml].reshape(1, 4, n_ml),
        "hg_norm_w": sums[ROW_NORM, 0:w][None],
        "ml_norm_w": sums[ROW_NORM, w:2 * w][None],
        "w_out": g_w_out[None],
        "ln_g": sums[ROW_LN_G][None],
        "ln_b": sums[ROW_LN_B][None],
    }
    weights = dict(c_ctx=c_ctx, w_mod=w_mod, b_mod=b_mod, w_in=w_in, conv_w=conv_w, conv_b=conv_b, hg_lb=hg_lb,
                   ml_gate_b=ml_gate_b, hg_norm_w=hg_norm_w, ml_norm_w=ml_norm_w, w_out=w_out, ln_g=ln_g, ln_b=ln_b)
    mom1 = dict(c_ctx=m_c_ctx, w_mod=m_w_mod, b_mod=m_b_mod, w_in=m_w_in, conv_w=m_conv_w, conv_b=m_conv_b,
                hg_lb=m_hg_lb, ml_gate_b=m_ml_gate_b, hg_norm_w=m_hg_norm_w, ml_norm_w=m_ml_norm_w, w_out=m_w_out,
                ln_g=m_ln_g, ln_b=m_ln_b)
    mom2 = dict(c_ctx=v_c_ctx, w_mod=v_w_mod, b_mod=v_b_mod, w_in=v_w_in, conv_w=v_conv_w, conv_b=v_conv_b,
                hg_lb=v_hg_lb, ml_gate_b=v_ml_gate_b, hg_norm_w=v_hg_norm_w, ml_norm_w=v_ml_norm_w, w_out=v_w_out,
                ln_g=v_ln_g, ln_b=v_ln_b)
    names = list(weights)
    big = ("w_mod", "w_in", "w_out")
    small = [n for n in names if n not in big]

    small_shapes = [weights[n].shape for n in small]
    res, _ = _adamw("adamw_small", *(_pack([src[n] for n in small]) for src in (weights, grads, mom1, mom2)))
    for out, packed in zip((delta, new_m, new_v), res):
        for n, a in zip(small, _unpack(packed, small_shapes)):
            out[n] = a

    return (loss, grad_x, *[grads[n].reshape(weights[n].shape) for n in names], *[delta[n] for n in names],
            *[new_m[n] for n in names], *[new_v[n] for n in names])
```

```python
import functools
import math

import jax
import jax.numpy as jnp
from jax import lax
from jax.experimental import pallas as pl
from jax.experimental.pallas import tpu as pltpu

F32 = jnp.float32
BF16 = jnp.bfloat16
HIGHEST = lax.Precision.HIGHEST
MESH = pl.DeviceIdType.MESH

HG_CHUNK = 64
ML_CHUNK = 256
HG_CHUNKS_PER_STEP = 4
GRID_W = 64
HG_DK = 128
LANE = 128
SUBLANE_BF16 = 16
ALPHA = 2.0 ** 0.25
LN_EPS = 1e-5
NORM_EPS = 1e-6
ADAM_LR = 0.001
ADAM_B1 = 0.9
ADAM_B2 = 0.999
ADAM_EPS = 1e-08
ADAM_WD = 0.01
ADAM_STEP = 10
VMEM_LIMIT = 56 * 1024 * 1024
N_CHIPS = 4
N_DEV = 8


def _params(sem=None):
    return pltpu.CompilerParams(dimension_semantics=sem, vmem_limit_bytes=VMEM_LIMIT)


def _largest_divisor(n, cap, multiple=1):
    best = None
    for d in range(multiple, min(n, cap) + 1, multiple):
        if n % d == 0:
            best = d
    assert best is not None, (n, cap, multiple)
    return best


def _sigmoid(x):
    return jax.nn.sigmoid(x)


def _silu(x):
    return x * jax.nn.sigmoid(x)


def _dot(a, b, dims, precision=None):
    return lax.dot_general(a, b, (dims, ((), ())), precision=precision, preferred_element_type=F32)


def _nn(a, b, precision=None):
    return _dot(a, b, ((1,), (0,)), precision)


def _nt(a, b, precision=None):
    return _dot(a, b, ((1,), (1,)), precision)


def _tn(a, b, precision=None):
    return _dot(a, b, ((0,), (0,)), precision)


def _narrow(x):
    return x.astype(BF16)


@jax.custom_vjp
def _bnn(a, b):
    return _nn(_narrow(a), _narrow(b))


def _bnn_fwd(a, b):
    an, bn = _narrow(a), _narrow(b)
    return _nn(an, bn), (an, bn)


def _bnn_bwd(res, ct):
    an, bn = res
    ctn = _narrow(ct)
    return _nt(ctn, bn), _tn(an, ctn)


_bnn.defvjp(_bnn_fwd, _bnn_bwd)


@jax.custom_vjp
def _bnt(a, b):
    return _nt(_narrow(a), _narrow(b))


def _bnt_fwd(a, b):
    an, bn = _narrow(a), _narrow(b)
    return _nt(an, bn), (an, bn)


def _bnt_bwd(res, ct):
    an, bn = res
    ctn = _narrow(ct)
    return _nn(ctn, bn), _tn(ctn, an)


_bnt.defvjp(_bnt_fwd, _bnt_bwd)


@jax.custom_vjp
def _btn(a, b):
    return _tn(_narrow(a), _narrow(b))


def _btn_fwd(a, b):
    an, bn = _narrow(a), _narrow(b)
    return _tn(an, bn), (an, bn)


def _btn_bwd(res, ct):
    an, bn = res
    ctn = _narrow(ct)
    return _nt(bn, ctn), _nn(an, ctn)


_btn.defvjp(_btn_fwd, _btn_bwd)


def _visible(n, rev):
    r = lax.broadcasted_iota(jnp.int32, (n, n), 0)
    c = lax.broadcasted_iota(jnp.int32, (n, n), 1)
    return (r <= c) if rev else (r >= c)


def _mask_matmul(mask, x):
    mb = mask.astype(BF16)
    hi = x.astype(BF16)
    lo = (x - hi.astype(F32)).astype(BF16)
    return _nn(mb, hi) + _nn(mb, lo)


@functools.partial(jax.custom_vjp, nondiff_argnums=(1,))
def _cumulative(x, rev):
    return _mask_matmul(_visible(x.shape[0], rev), x)


def _cumulative_fwd(x, rev):
    return _cumulative(x, rev), None


def _cumulative_bwd(rev, _, ct):
    return (_mask_matmul(_visible(ct.shape[0], not rev), ct),)


_cumulative.defvjp(_cumulative_fwd, _cumulative_bwd)


def _hg_chunk(states, aq, af, ai, lb0, lb1, rev):
    n_heads = len(states)
    lb = _sigmoid(lb0 - lb1)
    f = lb + (1.0 - lb) * _sigmoid(af)
    g = jnp.log(f)
    k = 1.0 - f
    q = _silu(aq)
    chunk = aq.shape[0]
    vis = _visible(chunk, rev)
    b = _cumulative(g, rev)
    last = 0 if rev else chunk - 1
    b_end = b[last:last + 1]
    b_mid = b[chunk // 2:chunk // 2 + 1]
    q_inter = q * jnp.exp(b)
    q_intra = q * jnp.exp(b - b_mid)
    k_intra = k * jnp.exp(b_mid - b)
    k_dec = k * jnp.exp(b_end - b)
    e_end = jnp.exp(b_end)
    new_states, outs = [], []
    for h in range(n_heads):
        sl = slice(h * HG_DK, (h + 1) * HG_DK)
        s_t = states[h]
        scores = jnp.where(vis, _nt(q_intra[:, sl], k_intra[:, sl]), 0.0)
        outs.append(_nt(q_inter[:, sl], s_t) + _nn(scores, ai[:, sl]))
        new_states.append(e_end[:, sl] * s_t + _tn(ai[:, sl], k_dec[:, sl]))
    return new_states, jnp.concatenate(outs, axis=1)


def _ml_chunk(state, q, k, v, g, gb, rev, d):
    cms, nvs, mbs = state
    n_heads = len(cms)
    dh = q.shape[1] // n_heads
    ga = g + gb
    log_f_all = jax.nn.log_sigmoid(ga)
    chunk = q.shape[0]
    vis = _visible(chunk, rev)
    b_all = _cumulative(log_f_all, rev)
    last = 0 if rev else chunk - 1
    k = k * (dh ** -0.5)
    new_c, new_n, new_m, outs = [], [], [], []
    for h in range(n_heads):
        ci = d * n_heads + h
        cf = (2 + d) * n_heads + h
        sl = slice(h * dh, (h + 1) * dh)
        qh, kh, vh = q[:, sl], k[:, sl], v[:, sl]
        li = ga[:, ci:ci + 1]
        b = b_all[:, cf:cf + 1]
        m = mbs[h][:, 0:1]
        row = jnp.transpose(li - b)
        log_w = jnp.where(vis, b + row, -jnp.inf)
        m_inter = b + m
        m_t = jnp.maximum(m_inter, jnp.max(log_w, axis=-1, keepdims=True))
        w_inter = jnp.exp(m_inter - m_t)
        w_qk = jnp.exp(log_w - m_t) * _bnt(qh, kh)
        num = w_inter * _bnt(qh, cms[h]) + _bnn(w_qk, vh)
        den = w_inter * jnp.sum(qh * nvs[h], axis=-1, keepdims=True) + jnp.sum(w_qk, axis=-1, keepdims=True)
        outs.append(num / jnp.maximum(jnp.abs(den), jnp.exp(-m_t)))
        m_new = m_t[last:last + 1]
        b_end = b[last:last + 1]
        w_s = jnp.exp(b_end - b + li - m_new)
        decay = jnp.exp(b_end + m - m_new)
        new_c.append(decay * cms[h] + _btn(w_s * vh, kh))
        new_n.append(decay * nvs[h] + jnp.sum(w_s * kh, axis=0, keepdims=True))
        new_m.append(jnp.broadcast_to(m_new, (1, LANE)))
    return (new_c, new_n, new_m), jnp.concatenate(outs, axis=1)


def _post_fn(o_f, o_b, az, h_f, h_b, bo, bz, wa, wb, n_hg, n_ml):
    o = o_f + o_b
    parts = []
    for h in range(n_hg):
        s = o[:, h * HG_DK:(h + 1) * HG_DK]
        parts.append(s * lax.rsqrt(jnp.mean(s * s, axis=-1, keepdims=True) + NORM_EPS))
    y_a = jnp.concatenate(parts, axis=1) * wa * _silu(az)
    hh = h_f + h_b
    dh = hh.shape[1] // n_ml
    parts = []
    for h in range(n_ml):
        s = hh[:, h * dh:(h + 1) * dh]
        mu = jnp.mean(s, axis=-1, keepdims=True)
        sc = s - mu
        parts.append(sc * lax.rsqrt(jnp.mean(sc * sc, axis=-1, keepdims=True) + NORM_EPS))
    y_b = jnp.concatenate(parts, axis=1) * wb * _sigmoid(bo) * _silu(bz)
    return jnp.concatenate([y_a, y_b], axis=1)


def _chip_of(dev):
    return 2 * dev[0] + dev[1]


def _index_of(dev):
    return 4 * dev[0] + 2 * dev[1] + dev[2]


class _Exchange:
    def __init__(self, srcs, out_shapes, transfers, local_copies=(), in_place=None):
        self.srcs, self.out_shapes = list(srcs), list(out_shapes)
        self.transfers, self.local_copies = list(transfers), list(local_copies)
        self.in_place = dict(in_place or {})

    def scratch(self):
        return [pltpu.SemaphoreType.DMA((len(self.transfers),)), pltpu.SemaphoreType.DMA((len(self.transfers),)),
                pltpu.SemaphoreType.DMA((max(len(self.local_copies), 1),))]

    def copies(self, ins, outs, send_sems, recv_sems, local_sems):
        me = (lax.axis_index("x"), lax.axis_index("y"), lax.axis_index("c"))

        def pick(ref, fn, *who):
            return ref if fn is None else ref.at[fn(*who)]

        sends, recvs, locs = [], [], []
        for t, (mask, si, sfn, di, dfn) in enumerate(self.transfers):
            peer = tuple(1 - p if flip else p for p, flip in zip(me, mask))
            sends.append(pltpu.make_async_remote_copy(
                src_ref=pick(ins[si], sfn, me, peer), dst_ref=pick(outs[di], dfn, me, peer),
                send_sem=send_sems.at[t], recv_sem=recv_sems.at[t], device_id=peer, device_id_type=MESH))
            landing = pick(outs[di], dfn, peer, me)
            recvs.append(pltpu.make_async_remote_copy(
                src_ref=landing, dst_ref=landing,
                send_sem=send_sems.at[t], recv_sem=recv_sems.at[t], device_id=peer, device_id_type=MESH))
        for l, (si, sfn, di, dfn) in enumerate(self.local_copies):
            locs.append(pltpu.make_async_copy(pick(ins[si], sfn, me), pick(outs[di], dfn, me), local_sems.at[l]))

        def start():
            for cp in locs + sends:
                cp.start()

        def wait():
            for cp in recvs:
                cp.wait_recv()
            for cp in sends:
                cp.wait_send()
            for cp in locs:
                cp.wait()

        return start, wait

    def run(self, name):
        n_in, n_out = len(self.srcs), len(self.out_shapes)

        def body(*refs):
            start, wait = self.copies(refs[:n_in], refs[n_in:n_in + n_out], *refs[n_in + n_out:])
            start()
            wait()

        hbm = pl.BlockSpec(memory_space=pltpu.HBM)
        return pl.pallas_call(
            body, name=name, out_shape=tuple(self.out_shapes), in_specs=[hbm] * n_in,
            out_specs=tuple([hbm] * n_out), scratch_shapes=self.scratch(), input_output_aliases=self.in_place,
        )(*self.srcs)


def _call(body, operands, *, name, grid, in_specs, out_specs, out_shape, scratch_shapes=(), sem=None, rider=None):
    out_specs, out_shape, scratch_shapes = list(out_specs), list(out_shape), list(scratch_shapes)
    if rider is None:
        res = pl.pallas_call(
            body, name=name, grid=grid, in_specs=list(in_specs), out_specs=tuple(out_specs),
            out_shape=tuple(out_shape), scratch_shapes=scratch_shapes, compiler_params=_params(sem),
        )(*operands)
        return list(res), []
    counts = (len(in_specs), len(rider.srcs), len(out_specs), len(rider.out_shapes), len(scratch_shapes), 3)

    def full(*refs):
        groups, pos = [], 0
        for k in counts:
            groups.append(refs[pos:pos + k])
            pos += k
        own_in, ex_in, own_out, ex_out, own_scr, ex_scr = groups
        ids = [pl.program_id(a) for a in range(len(grid))]
        first = functools.reduce(jnp.logical_and, [i == 0 for i in ids])
        last = functools.reduce(jnp.logical_and, [i == g - 1 for i, g in zip(ids, grid)])
        start, wait = rider.copies(ex_in, ex_out, *ex_scr)
        pl.when(first)(start)
        body(*own_in, *own_out, *own_scr)
        pl.when(last)(wait)

    hbm = pl.BlockSpec(memory_space=pltpu.HBM)
    res = pl.pallas_call(
        full, name=name, grid=grid, in_specs=list(in_specs) + [hbm] * counts[1],
        out_specs=tuple(out_specs + [hbm] * counts[3]), out_shape=tuple(out_shape + rider.out_shapes),
        scratch_shapes=scratch_shapes + rider.scratch(), compiler_params=_params(("arbitrary",) * len(grid)),
        input_output_aliases={counts[0] + i: counts[2] + o for i, o in rider.in_place.items()},
    )(*operands, *rider.srcs)
    return list(res[:counts[2]]), list(res[counts[2]:])


ALL_MASKS = [(mx, my, mc) for mx in (0, 1) for my in (0, 1) for mc in (0, 1)][1:]
CHIP_MASKS = [(1, 0, 0), (0, 1, 0), (1, 1, 0)]
SIBLING_MASK = (0, 0, 1)


def _all_gather8(v):
    out = jax.ShapeDtypeStruct((N_DEV,) + v.shape, v.dtype)
    slot = lambda sender, receiver: _index_of(sender)
    transfers = [(mask, 0, None, 0, slot) for mask in ALL_MASKS]
    return _Exchange([v], [out], transfers, [(0, None, 0, lambda me: _index_of(me))])


def _all_gather_chips(arrays):
    outs = [jax.ShapeDtypeStruct((N_CHIPS,) + a.shape, a.dtype) for a in arrays]
    slot = lambda sender, receiver: _chip_of(sender)
    return _Exchange(arrays, outs, [(mask, i, None, i, slot) for i in range(len(arrays)) for mask in CHIP_MASKS])


def _sibling_swap(arrays):
    outs = [jax.ShapeDtypeStruct(a.shape, a.dtype) for a in arrays]
    return _Exchange(arrays, outs, [(SIBLING_MASK, i, None, i, None) for i in range(len(arrays))])


def _chip_scatter(arrays):
    outs = [jax.ShapeDtypeStruct(a.shape, a.dtype) for a in arrays]
    transfers = [(mask, i, lambda s, r: _chip_of(r), i, lambda s, r: _chip_of(s))
                 for i in range(len(arrays)) for mask in CHIP_MASKS]
    return _Exchange(arrays, outs, transfers)


def _own_block(chip, own, blocks):
    sel = (lax.broadcasted_iota(jnp.int32, (N_CHIPS,) + (1,) * (blocks.ndim - 1), 0) == chip)
    return jnp.where(sel, own if own.ndim == blocks.ndim else own[None], blocks)


def _join_halves(ci, mine, other, axis):
    return jnp.where(ci == 0, jnp.concatenate([mine, other], axis=axis), jnp.concatenate([other, mine], axis=axis))


def _mm_nt(name, a, b, n, tm, tn, out_dtype, rider=None):
    m, k = a.shape

    def body(a_ref, b_ref, o_ref):
        o_ref[...] = _nt(a_ref[...], b_ref[...]).astype(out_dtype)

    (out,), rode = _call(
        body, (a, b), name=name, grid=(n // tn, m // tm),
        in_specs=[pl.BlockSpec((tm, k), lambda j, i: (i, 0)), pl.BlockSpec((tn, k), lambda j, i: (j, 0))],
        out_specs=[pl.BlockSpec((tm, tn), lambda j, i: (i, j))],
        out_shape=[jax.ShapeDtypeStruct((m, n), out_dtype)], sem=("parallel", "parallel"), rider=rider)
    return out, rode


def _mm_acc(name, a, b, tm, tk, rider=None):
    m, kc = a.shape
    n = b.shape[1]

    def body(a_ref, b_ref, o_ref):
        @pl.when(pl.program_id(1) == 0)
        def _():
            o_ref[...] = jnp.zeros_like(o_ref)
        o_ref[...] += _nn(a_ref[...], b_ref[...])

    (out,), rode = _call(
        body, (a, b), name=name, grid=(m // tm, kc // tk),
        in_specs=[pl.BlockSpec((tm, tk), lambda i, kk: (i, kk)), pl.BlockSpec((tk, n), lambda i, kk: (kk, 0))],
        out_specs=[pl.BlockSpec((tm, n), lambda i, kk: (i, 0))],
        out_shape=[jax.ShapeDtypeStruct((m, n), F32)], sem=("parallel", "arbitrary"), rider=rider)
    return out, rode


def _mm_tn(name, a, b, tm, tk, with_bf16=False):
    kr, m = a.shape
    n = b.shape[1]
    steps_k = kr // tk

    def body(a_ref, b_ref, o_ref, *narrow):
        @pl.when(pl.program_id(1) == 0)
        def _():
            o_ref[...] = jnp.zeros_like(o_ref)
        o_ref[...] += _tn(a_ref[...], b_ref[...])
        if with_bf16:
            @pl.when(pl.program_id(1) == steps_k - 1)
            def _():
                narrow[0][...] = o_ref[...].astype(BF16)

    out_spec = pl.BlockSpec((tm, n), lambda i, kk: (i, 0))
    res = pl.pallas_call(
        body, name=name, grid=(m // tm, steps_k),
        in_specs=[pl.BlockSpec((tk, tm), lambda i, kk: (kk, i)), pl.BlockSpec((tk, n), lambda i, kk: (kk, 0))],
        out_specs=(out_spec,) * (2 if with_bf16 else 1),
        out_shape=(jax.ShapeDtypeStruct((m, n), F32),) + ((jax.ShapeDtypeStruct((m, n), BF16),) if with_bf16 else ()),
        compiler_params=_params(("parallel", "arbitrary")),
    )(a, b)
    return res if with_bf16 else res[0]


def _modulate_fwd(x, ctx, prm, tm, rider=None):
    t_rows, dm = x.shape
    lat = t_rows // tm
    r = t_rows + ctx.shape[0]

    def body(x_ref, c_ref, p_ref, h_ref):
        xv = jnp.where(pl.program_id(0) >= lat, c_ref[...], x_ref[...])
        mu = jnp.mean(xv, axis=-1, keepdims=True)
        xm = xv - mu
        n = xm * lax.rsqrt(jnp.mean(xm * xm, axis=-1, keepdims=True) + LN_EPS)
        h_ref[...] = (n * (1.0 + p_ref[0, 1:2, :]) + p_ref[0, 0:1, :]).astype(BF16)

    (h,), rode = _call(
        body, (x, ctx, prm), name="modulate_fwd", grid=(r // tm,),
        in_specs=[pl.BlockSpec((tm, dm), lambda i: (jnp.minimum(i, lat - 1), 0)),
                  pl.BlockSpec((tm, dm), lambda i: (jnp.maximum(i - lat, 0), 0)),
                  pl.BlockSpec((1, 8, dm), lambda i: ((i >= lat).astype(jnp.int32), 0, 0))],
        out_specs=[pl.BlockSpec((tm, dm), lambda i: (i, 0))],
        out_shape=[jax.ShapeDtypeStruct((r, dm), BF16)], sem=("parallel",), rider=rider)
    return h, rode


def _modulate_bwd(x, ctx, dh, prm, gx_direct, tm, rider=None):
    t_rows, dm = x.shape
    lat, n_ct = t_rows // tm, ctx.shape[0] // tm
    is_ctx = lambda i: i < n_ct
    cls = lambda i: is_ctx(i).astype(jnp.int32)
    lat_tile = lambda i: (jnp.maximum(i - n_ct, 0), 0)

    def body(x_ref, c_ref, dh_ref, p_ref, gd_ref, gx_ref, acc_ref):
        i = pl.program_id(0)

        @pl.when((i == 0) | (i == n_ct))
        def _():
            acc_ref[...] = jnp.zeros_like(acc_ref)

        x = jnp.where(is_ctx(i), c_ref[...], x_ref[...])
        dh_v = dh_ref[...]
        mu = jnp.mean(x, axis=-1, keepdims=True)
        xm = x - mu
        rstd = lax.rsqrt(jnp.mean(xm * xm, axis=-1, keepdims=True) + LN_EPS)
        n = xm * rstd
        acc_ref[0, 0:1, :] += jnp.sum(dh_v, axis=0, keepdims=True)
        acc_ref[0, 1:2, :] += jnp.sum(dh_v * n, axis=0, keepdims=True)
        dn = dh_v * (1.0 + p_ref[0, 1:2, :])
        dx = rstd * (dn - jnp.mean(dn, axis=-1, keepdims=True) - n * jnp.mean(dn * n, axis=-1, keepdims=True))
        gx_ref[...] = dx + gd_ref[...]

    return _call(
        body, (x, ctx, dh, prm, gx_direct), name="modulate_bwd", grid=(n_ct + lat,),
        in_specs=[pl.BlockSpec((tm, dm), lat_tile),
                  pl.BlockSpec((tm, dm), lambda i: (jnp.minimum(i, n_ct - 1), 0)),
                  pl.BlockSpec((tm, dm), lambda i: (jnp.where(is_ctx(i), lat + i, i - n_ct), 0)),
                  pl.BlockSpec((1, 8, dm), lambda i: (cls(i), 0, 0)),
                  pl.BlockSpec((tm, dm), lat_tile)],
        out_specs=(pl.BlockSpec((tm, dm), lat_tile), pl.BlockSpec((1, 8, dm), lambda i: (cls(i), 0, 0))),
        out_shape=(jax.ShapeDtypeStruct((t_rows, dm), F32), jax.ShapeDtypeStruct((2, 8, dm), F32)),
        sem=("arbitrary",), rider=rider)


def _conv_parts(t_rows, c_rows):
    return ((0, t_rows, t_rows // GRID_W, GRID_W), (t_rows, c_rows, 1, c_rows))


def _col_shifts(x2, rows_g, width_g):
    n, ct = x2.shape
    col = lax.broadcasted_iota(jnp.int32, (width_g, ct), 0)
    as_grid = lambda a: a.reshape(rows_g, width_g, ct)
    left = as_grid(pltpu.roll(x2, 1, 0)) * (col >= 1).astype(F32)
    right = as_grid(pltpu.roll(x2, n - 1, 0)) * (col <= width_g - 2).astype(F32)
    return [left, as_grid(x2), right]


CONV_BLOCK_ROWS = 4


def _conv_blocks(t_rows, c_rows):
    for t0, _, rows_g, width_g in _conv_parts(t_rows, c_rows):
        nb = min(CONV_BLOCK_ROWS, rows_g)
        assert rows_g % nb == 0
        for g0 in range(0, rows_g, nb):
            yield t0, rows_g, width_g, g0, nb


def _slab(ref, t0, rows_g, width_g, g0, nb):
    if rows_g == 1:
        return ref[t0:t0 + width_g, :]
    lo, hi = max(g0 - 1, 0), min(g0 + nb + 1, rows_g)
    parts = [ref[t0 + lo * width_g:t0 + hi * width_g, :]]
    zero = jnp.zeros((width_g, ref.shape[1]), F32)
    if g0 == 0:
        parts.insert(0, zero)
    if g0 + nb == rows_g:
        parts.append(zero)
    return jnp.concatenate(parts, axis=0)


def _conv_taps(cols, w_ref, nb, flip):
    one_row = cols[0].shape[0] == nb
    acc = None
    for a in range(3):
        if one_row and a != 1:
            continue
        for b in range(3):
            tap = (2 - a) * 3 + (2 - b) if flip else a * 3 + b
            term = (cols[b] if one_row else cols[b][a:a + nb]) * w_ref[tap:tap + 1, :]
            acc = term if acc is None else acc + term
    return acc


def _conv_fwd(u, conv_w9, conv_b, t_rows, c_rows, w, ct):
    r = u.shape[0]
    base = 5 * w // ct

    def body(x_ref, w_ref, b_ref, o_ref):
        for t0, rows_g, width_g, g0, nb in _conv_blocks(t_rows, c_rows):
            slab = _slab(x_ref, t0, rows_g, width_g, g0, nb)
            cols = _col_shifts(slab, slab.shape[0] // width_g, width_g)
            pre = _conv_taps(cols, w_ref, nb, False) + b_ref[...]
            o_ref[t0 + g0 * width_g:t0 + (g0 + nb) * width_g, :] = _silu(pre).reshape(nb * width_g, ct)

    return pl.pallas_call(
        body, name="conv_fwd", grid=(2 * w // ct,),
        in_specs=[pl.BlockSpec((r, ct), lambda i: (0, base + i)), pl.BlockSpec((9, ct), lambda i: (0, i)),
                  pl.BlockSpec((1, ct), lambda i: (0, i))],
        out_specs=pl.BlockSpec((r, ct), lambda i: (0, i)),
        out_shape=jax.ShapeDtypeStruct((r, 2 * w), F32),
        compiler_params=_params(("parallel",)),
    )(u, conv_w9, conv_b)


def _conv_bwd(u, dqk_pair, conv_w9, conv_b, t_rows, c_rows, w, ct):
    r = u.shape[0]
    base = 5 * w // ct

    def body(x_ref, d1_ref, d2_ref, w_ref, b_ref, dx_ref, dw_ref, db_ref, dpre_ref):
        dw = [jnp.zeros((1, ct), F32) for _ in range(9)]
        db = jnp.zeros((1, ct), F32)
        for t0, rows_g, width_g, g0, nb in _conv_blocks(t_rows, c_rows):
            rows = slice(t0 + g0 * width_g, t0 + (g0 + nb) * width_g)
            slab = _slab(x_ref, t0, rows_g, width_g, g0, nb)
            cols = _col_shifts(slab, slab.shape[0] // width_g, width_g)
            pre = _conv_taps(cols, w_ref, nb, False) + b_ref[...]
            sg = _sigmoid(pre)
            dpre = (d1_ref[rows, :] + d2_ref[rows, :]).reshape(pre.shape) * (sg * (1.0 + pre * (1.0 - sg)))
            dpre_ref[rows, :] = dpre.reshape(nb * width_g, ct)
            db = db + jnp.sum(jnp.sum(dpre, axis=0), axis=0, keepdims=True)
            for a in range(3):
                if rows_g == 1 and a != 1:
                    continue
                for b in range(3):
                    moved = cols[b] if rows_g == 1 else cols[b][a:a + nb]
                    dw[a * 3 + b] = dw[a * 3 + b] + jnp.sum(jnp.sum(moved * dpre, axis=0), axis=0, keepdims=True)
        for t0, rows_g, width_g, g0, nb in _conv_blocks(t_rows, c_rows):
            slab = _slab(dpre_ref, t0, rows_g, width_g, g0, nb)
            cols = _col_shifts(slab, slab.shape[0] // width_g, width_g)
            dx_ref[t0 + g0 * width_g:t0 + (g0 + nb) * width_g, :] = _conv_taps(cols, w_ref, nb, True).reshape(
                nb * width_g, ct).astype(BF16)
        for tap in range(9):
            dw_ref[tap:tap + 1, :] = dw[tap]
        db_ref[...] = db

    return pl.pallas_call(
        body, name="conv_bwd", grid=(2 * w // ct,),
        in_specs=[pl.BlockSpec((r, ct), lambda i: (0, base + i)), pl.BlockSpec((r, ct), lambda i: (0, i)),
                  pl.BlockSpec((r, ct), lambda i: (0, i)),
                  pl.BlockSpec((9, ct), lambda i: (0, i)), pl.BlockSpec((1, ct), lambda i: (0, i))],
        out_specs=(pl.BlockSpec((r, ct), lambda i: (0, i)), pl.BlockSpec((9, ct), lambda i: (0, i)),
                   pl.BlockSpec((1, ct), lambda i: (0, i))),
        out_shape=(jax.ShapeDtypeStruct((r, 2 * w), BF16), jax.ShapeDtypeStruct((9, 2 * w), F32),
                   jax.ShapeDtypeStruct((1, 2 * w), F32)),
        scratch_shapes=[pltpu.VMEM((r, ct), F32)],
        compiler_params=_params(("parallel",)),
    )(u, dqk_pair[0], dqk_pair[1], conv_w9, conv_b)


def _assemble_du(groups, gates, n_pad, tm):
    flat, layout = [], []
    for entry in list(groups) + [gates]:
        parts = entry if isinstance(entry, (tuple, list)) else (entry,)
        layout.append((len(flat), len(parts), parts[0].shape[1]))
        flat += list(parts)
    r = flat[0].shape[0]

    def body(*refs):
        o_ref = refs[-1]
        col = 0
        for first, count, width in layout:
            val = refs[first][...]
            for extra in range(1, count):
                val = val.astype(F32) + refs[first + extra][...].astype(F32)
            o_ref[:, col:col + width] = val.astype(BF16)
            col += width
        assert col == n_pad

    return pl.pallas_call(
        body, name="assemble_du", grid=(r // tm,),
        in_specs=[pl.BlockSpec((tm, a.shape[1]), lambda i: (i, 0)) for a in flat],
        out_specs=pl.BlockSpec((tm, n_pad), lambda i: (i, 0)),
        out_shape=jax.ShapeDtypeStruct((r, n_pad), BF16),
        compiler_params=_params(("parallel",)),
    )(*flat)


def _scan_order(n_lat, n_ctx, rev):
    n = n_lat + n_ctx
    if rev:
        return lambda j: n - 1 - j
    return lambda j: (j + n_lat) % n


DIRS = (False, True)


def _hg_scan_fwd(u, lb_full, w, n_lat, n_ctx, chunk, rider=None):
    r = u.shape[0]
    n_heads = w // HG_DK
    sub = HG_CHUNKS_PER_STEP if n_lat % HG_CHUNKS_PER_STEP == 0 and n_ctx % HG_CHUNKS_PER_STEP == 0 else 1
    n_steps = (n_lat + n_ctx) // sub
    nat = [_scan_order(n_lat // sub, n_ctx // sub, rev) for rev in DIRS]
    rows = sub * chunk

    def body(*refs):
        ins, outs, scratch = refs[:8], refs[8:12], refs[12:]

        @pl.when(pl.program_id(0) == 0)
        def _():
            for s_ref in scratch:
                s_ref[...] = jnp.zeros_like(s_ref)

        for d, rev in enumerate(DIRS):
            aq, af, ai, lb_ref = ins[4 * d:4 * d + 4]
            o_ref, save_ref = outs[2 * d:2 * d + 2]
            state = [scratch[d][h] for h in range(n_heads)]
            for p in range(sub):
                sl = slice((sub - 1 - p if rev else p) * chunk, (sub - p if rev else p + 1) * chunk)
                for h in range(n_heads):
                    save_ref[0, p, h] = state[h]
                state, o = _hg_chunk(state, aq[sl, :], af[sl, :], ai[sl, :], lb_ref[0, 0:1, :], lb_ref[0, 1:2, :], rev)
                o_ref[sl, :] = o
            for h in range(n_heads):
                scratch[d][h] = state[h]

    in_specs, out_specs, out_shape = [], [], []
    for d in range(2):
        in_specs += [pl.BlockSpec((rows, w), lambda j, d=d: (nat[d](j), 0)),
                     pl.BlockSpec((rows, w), lambda j, d=d: (nat[d](j), 1 + d)),
                     pl.BlockSpec((rows, w), lambda j, d=d: (nat[d](j), 3)),
                     pl.BlockSpec((1, 2, w), lambda j, d=d: (d, 0, 0))]
        out_specs += [pl.BlockSpec((rows, w), lambda j, d=d: (nat[d](j), 0)),
                      pl.BlockSpec((1, sub, n_heads, HG_DK, HG_DK), lambda j: (j, 0, 0, 0, 0))]
        out_shape += [jax.ShapeDtypeStruct((r, w), F32),
                      jax.ShapeDtypeStruct((n_steps, sub, n_heads, HG_DK, HG_DK), F32)]
    (o_f, s_f, o_b, s_b), rode = _call(
        body, (u, u, u, lb_full, u, u, u, lb_full), name="hg_scan_fwd", grid=(n_steps,), in_specs=in_specs,
        out_specs=out_specs, out_shape=out_shape, scratch_shapes=[pltpu.VMEM((n_heads, HG_DK, HG_DK), F32)] * 2,
        sem=("arbitrary",), rider=rider)
    return (o_f, o_b), (s_f, s_b), rode


def _hg_scan_bwd(u, lb_full, saved, d_o, w, n_lat, n_ctx, chunk, rider=None):
    r = u.shape[0]
    n_heads = w // HG_DK
    n_steps, sub = saved[0].shape[0], saved[0].shape[1]
    n_lat_s = n_lat // sub
    step = lambda jj: n_steps - 1 - jj
    nat = [(lambda jj, o=_scan_order(n_lat_s, n_ctx // sub, rev): o(step(jj))) for rev in DIRS]
    rows = sub * chunk

    def body(*refs):
        ins, outs, scratch = refs[:12], refs[12:20], refs[20:]
        jj = pl.program_id(0)

        @pl.when(jj == 0)
        def _():
            for d in range(2):
                scratch[d][...] = jnp.zeros_like(scratch[d])
                outs[4 * d + 3][...] = jnp.zeros_like(outs[4 * d + 3])

        for d, rev in enumerate(DIRS):
            aq, af, ai, lb_ref, save_ref, do_ref = ins[6 * d:6 * d + 6]
            daq_ref, daf_ref, dai_ref, dlb_ref = outs[4 * d:4 * d + 4]
            f = lambda st, a, b, c, l0, l1, rev=rev: _hg_chunk(st, a, b, c, l0, l1, rev)
            latent = (nat[d](jj) < n_lat_s).astype(F32)
            d_state = [scratch[d][h] for h in range(n_heads)]
            for p in reversed(range(sub)):
                sl = slice((sub - 1 - p if rev else p) * chunk, (sub - p if rev else p + 1) * chunk)
                _, vjp = jax.vjp(f, [save_ref[0, p, h] for h in range(n_heads)], aq[sl, :], af[sl, :], ai[sl, :],
                                 lb_ref[0, 0:1, :], lb_ref[0, 1:2, :])
                d_state, daq, daf, dai, dl0, dl1 = vjp((d_state, do_ref[sl, :] * latent))
                daq_ref[sl, :] = daq.astype(BF16)
                daf_ref[sl, :] = daf.astype(BF16)
                dai_ref[sl, :] = dai.astype(BF16)
                dlb_ref[0:1, :] += dl0
                dlb_ref[1:2, :] += dl1
            for h in range(n_heads):
                scratch[d][h] = d_state[h]

    in_specs, out_specs, out_shape, operands = [], [], [], []
    for d in range(2):
        row = lambda jj, d=d: (nat[d](jj), 0)
        in_specs += [pl.BlockSpec((rows, w), row),
                     pl.BlockSpec((rows, w), lambda jj, d=d: (nat[d](jj), 1 + d)),
                     pl.BlockSpec((rows, w), lambda jj, d=d: (nat[d](jj), 3)),
                     pl.BlockSpec((1, 2, w), lambda jj, d=d: (d, 0, 0)),
                     pl.BlockSpec((1, sub, n_heads, HG_DK, HG_DK), lambda jj: (step(jj), 0, 0, 0, 0)),
                     pl.BlockSpec((rows, w), lambda jj, d=d: (jnp.minimum(nat[d](jj), n_lat_s - 1), 0))]
        operands += [u, u, u, lb_full, saved[d], d_o]
        out_specs += [pl.BlockSpec((rows, w), row)] * 3 + [pl.BlockSpec((2, w), lambda jj: (0, 0))]
        out_shape += [jax.ShapeDtypeStruct((r, w), BF16)] * 3 + [jax.ShapeDtypeStruct((2, w), F32)]
    res, rode = _call(
        body, operands, name="hg_scan_bwd", grid=(n_steps,), in_specs=in_specs, out_specs=out_specs,
        out_shape=out_shape, scratch_shapes=[pltpu.VMEM((n_heads, HG_DK, HG_DK), F32)] * 2,
        sem=("arbitrary",), rider=rider)
    return res[0:4], res[4:8], rode


def _ml_state_shapes(n_chunks, n_heads, dh):
    return (jax.ShapeDtypeStruct((n_chunks, n_heads, dh, dh), F32),
            jax.ShapeDtypeStruct((n_chunks, n_heads, 1, dh), F32),
            jax.ShapeDtypeStruct((n_chunks, n_heads, 1, LANE), F32))


def _ml_state_specs(n_heads, dh, index):
    return (pl.BlockSpec((1, n_heads, dh, dh), lambda j: (index(j), 0, 0, 0)),
            pl.BlockSpec((1, n_heads, 1, dh), lambda j: (index(j), 0, 0, 0)),
            pl.BlockSpec((1, n_heads, 1, LANE), lambda j: (index(j), 0, 0, 0)))


def _ml_state_scratch(n_heads, dh):
    return [pltpu.VMEM((n_heads, dh, dh), F32), pltpu.VMEM((n_heads, 1, dh), F32), pltpu.VMEM((n_heads, 1, LANE), F32)]


def _ml_scan_fwd(qk, u, gate_b, w, n_heads, n_lat, n_ctx, chunk):
    r = u.shape[0]
    dh = w // n_heads
    n_chunks = n_lat + n_ctx
    nat = [_scan_order(n_lat, n_ctx, rev) for rev in DIRS]

    def body(*refs):
        ins, outs, scratch = refs[:10], refs[10:18], refs[18:]

        @pl.when(pl.program_id(0) == 0)
        def _():
            for s_ref in scratch:
                s_ref[...] = jnp.zeros_like(s_ref)

        results = []
        for d, rev in enumerate(DIRS):
            q, k, v, g, gb = ins[5 * d:5 * d + 5]
            state = tuple([ref[h] for h in range(n_heads)] for ref in scratch[3 * d:3 * d + 3])
            results.append((state, _ml_chunk(state, q[...], k[...], v[...], g[...], gb[...], rev, d)))
        for d, (state, (new, o)) in enumerate(results):
            outs[4 * d][...] = o
            for part in range(3):
                for h in range(n_heads):
                    outs[4 * d + 1 + part][0, h] = state[part][h]
                    scratch[3 * d + part][h] = new[part][h]

    in_specs, out_specs, out_shape = [], [], []
    for d in range(2):
        in_specs += [pl.BlockSpec((chunk,w), lambda j, d=d: (nat[d](j), 0)),
                     pl.BlockSpec((chunk,w), lambda j, d=d: (nat[d](j), 1)),
                     pl.BlockSpec((chunk,w), lambda j, d=d: (nat[d](j), 7)),
                     pl.BlockSpec((chunk,LANE), lambda j, d=d: (nat[d](j), 10 * w // LANE)),
                     pl.BlockSpec((1, LANE), lambda j: (0, 0))]
        out_specs += [pl.BlockSpec((chunk,w), lambda j, d=d: (nat[d](j), 0))]
        out_specs += list(_ml_state_specs(n_heads, dh, lambda j: j))
        out_shape += [jax.ShapeDtypeStruct((r, w), F32)] + list(_ml_state_shapes(n_chunks, n_heads, dh))
    res = pl.pallas_call(
        body, name="ml_scan_fwd", grid=(n_chunks,), in_specs=in_specs, out_specs=tuple(out_specs),
        out_shape=tuple(out_shape), scratch_shapes=_ml_state_scratch(n_heads, dh) * 2,
        compiler_params=_params(("arbitrary",)),
    )(qk, qk, u, u, gate_b, qk, qk, u, u, gate_b)
    return (res[0], res[4]), (res[1:4], res[5:8])


def _ml_scan_bwd(qk, u, gate_b, saved, d_h, w, n_heads, n_lat, n_ctx, chunk, rider=None):
    r = u.shape[0]
    dh = w // n_heads
    n_chunks = n_lat + n_ctx
    step = lambda jj: n_chunks - 1 - jj
    nat = [(lambda jj, o=_scan_order(n_lat, n_ctx, rev): o(step(jj))) for rev in DIRS]

    def body(*refs):
        ins, outs, scratch = refs[:18], refs[18:26], refs[26:]
        jj = pl.program_id(0)

        @pl.when(jj == 0)
        def _():
            for s_ref in scratch:
                s_ref[...] = jnp.zeros_like(s_ref)
            for d in range(2):
                outs[4 * d + 3][...] = jnp.zeros_like(outs[4 * d + 3])

        results = []
        for d, rev in enumerate(DIRS):
            q, k, v, g, gb, sc, sn, sm, dh_ref = ins[9 * d:9 * d + 9]
            state = tuple([ref[0, h] for h in range(n_heads)] for ref in (sc, sn, sm))
            f = lambda st, a, b, c, gg, bb, rev=rev, d=d: _ml_chunk(st, a, b, c, gg, bb, rev, d)
            _, vjp = jax.vjp(f, state, q[...], k[...], v[...], g[...], gb[...])
            d_state = tuple([ref[h] for h in range(n_heads)] for ref in scratch[3 * d:3 * d + 3])
            d_out = dh_ref[...] * (nat[d](jj) < n_lat).astype(F32)
            results.append(vjp((d_state, d_out)))
        for d, (d_state, dq, dk, dv, dg, dgb) in enumerate(results):
            dqk_ref, dv_ref, dg_ref, dgb_ref = outs[4 * d:4 * d + 4]
            for part in range(3):
                for h in range(n_heads):
                    scratch[3 * d + part][h] = d_state[part][h]
            dqk_ref[:, 0:w] = dq
            dqk_ref[:, w:2 * w] = dk
            dv_ref[...] = dv.astype(BF16)
            dg_ref[...] = dg
            dgb_ref[...] += dgb

    in_specs, out_specs, out_shape, operands = [], [], [], []
    for d in range(2):
        row = lambda jj, d=d: (nat[d](jj), 0)
        in_specs += [pl.BlockSpec((chunk,w), row), pl.BlockSpec((chunk,w), lambda jj, d=d: (nat[d](jj), 1)),
                     pl.BlockSpec((chunk,w), lambda jj, d=d: (nat[d](jj), 7)),
                     pl.BlockSpec((chunk,LANE), lambda jj, d=d: (nat[d](jj), 10 * w // LANE)),
                     pl.BlockSpec((1, LANE), lambda jj: (0, 0))]
        in_specs += list(_ml_state_specs(n_heads, dh, step))
        in_specs += [pl.BlockSpec((chunk,w), lambda jj, d=d: (jnp.minimum(nat[d](jj), n_lat - 1), 0))]
        operands += [qk, qk, u, u, gate_b, *saved[d], d_h]
        out_specs += [pl.BlockSpec((chunk,2 * w), row), pl.BlockSpec((chunk,w), row),
                      pl.BlockSpec((chunk,LANE), row), pl.BlockSpec((1, LANE), lambda jj: (0, 0))]
        out_shape += [jax.ShapeDtypeStruct((r, 2 * w), F32), jax.ShapeDtypeStruct((r, w), BF16),
                      jax.ShapeDtypeStruct((r, LANE), F32), jax.ShapeDtypeStruct((1, LANE), F32)]
    res, rode = _call(
        body, operands, name="ml_scan_bwd", grid=(n_chunks,), in_specs=in_specs, out_specs=out_specs,
        out_shape=out_shape, scratch_shapes=_ml_state_scratch(n_heads, dh) * 2, sem=("arbitrary",), rider=rider)
    return res[0:4], res[4:8], rode


def _post_specs(w, tm, lat_tiles, cols):
    return [pl.BlockSpec((tm, w), (lambda i, cb=cb: (jnp.minimum(i, lat_tiles - 1), cb))) for cb in cols]


def _post_fwd(o_f, o_b, h_f, h_b, u, wa, wb, t_rows, w, n_hg, n_ml, tm):
    lat_tiles = t_rows // tm

    def body(of, ob, hf, hb, az, bo, bz, wa_ref, wb_ref, y_ref):
        y_ref[...] = _post_fn(of[...], ob[...], az[...], hf[...], hb[...], bo[...], bz[...],
                              wa_ref[...], wb_ref[...], n_hg, n_ml).astype(BF16)

    rows = pl.BlockSpec((tm, w), lambda i: (i, 0))
    vec = pl.BlockSpec((1, w), lambda i: (0, 0))
    return pl.pallas_call(
        body, name="post_fwd", grid=(lat_tiles,),
        in_specs=[rows] * 4 + _post_specs(w, tm, lat_tiles, (4, 8, 9)) + [vec, vec],
        out_specs=pl.BlockSpec((tm, 2 * w), lambda i: (i, 0)),
        out_shape=jax.ShapeDtypeStruct((t_rows, 2 * w), BF16),
        compiler_params=_params(("parallel",)),
    )(o_f, o_b, h_f, h_b, u, u, u, wa, wb)


def _post_bwd(o_f, o_b, h_f, h_b, u, wa, wb, dy, t_rows, w, n_hg, n_ml, tm, rider=None):
    r = u.shape[0]
    lat_tiles = t_rows // tm
    lat = lambda i: (jnp.minimum(i, lat_tiles - 1), 0)

    def body(of, ob, hf, hb, az, bo, bz, wa_ref, wb_ref, dy_ref, do_ref, dh_ref, daz_ref, dbo_ref, dbz_ref,
             dwa_ref, dwb_ref):
        i = pl.program_id(0)

        @pl.when(i == 0)
        def _():
            dwa_ref[...] = jnp.zeros_like(dwa_ref)
            dwb_ref[...] = jnp.zeros_like(dwb_ref)

        @pl.when(i < lat_tiles)
        def _():
            f = functools.partial(_post_fn, n_hg=n_hg, n_ml=n_ml)
            _, vjp = jax.vjp(f, of[...], ob[...], az[...], hf[...], hb[...], bo[...], bz[...], wa_ref[...], wb_ref[...])
            d_of, _, d_az, d_hf, _, d_bo, d_bz, d_wa, d_wb = vjp(dy_ref[...])
            do_ref[...] = d_of
            dh_ref[...] = d_hf
            daz_ref[...] = d_az.astype(BF16)
            dbo_ref[...] = d_bo.astype(BF16)
            dbz_ref[...] = d_bz.astype(BF16)
            dwa_ref[...] += d_wa
            dwb_ref[...] += d_wb

        @pl.when(i >= lat_tiles)
        def _():
            daz_ref[...] = jnp.zeros_like(daz_ref)
            dbo_ref[...] = jnp.zeros_like(dbo_ref)
            dbz_ref[...] = jnp.zeros_like(dbz_ref)

    lat_rows = pl.BlockSpec((tm, w), lat)
    all_rows = pl.BlockSpec((tm, w), lambda i: (i, 0))
    vec = pl.BlockSpec((1, w), lambda i: (0, 0))
    sd_t = jax.ShapeDtypeStruct((t_rows, w), F32)
    sd_r = jax.ShapeDtypeStruct((r, w), BF16)
    sd_v = jax.ShapeDtypeStruct((1, w), F32)
    return _call(
        body, (o_f, o_b, h_f, h_b, u, u, u, wa, wb, dy), name="post_bwd", grid=(r // tm,),
        in_specs=[lat_rows] * 4 + _post_specs(w, tm, lat_tiles, (4, 8, 9)) + [vec, vec]
        + [pl.BlockSpec((tm, 2 * w), lat)],
        out_specs=(lat_rows, lat_rows, all_rows, all_rows, all_rows, vec, vec),
        out_shape=(sd_t, sd_t, sd_r, sd_r, sd_r, sd_v, sd_v), sem=("arbitrary",), rider=rider)


OUT_ROW_GATE, OUT_ROW_LN_G, OUT_ROW_LN_B, OUT_ROW_LOSS = 0, 1, 2, 3


def _out_block(y, w_out, x, target, prm, tm):
    t_rows, dm = x.shape
    di = y.shape[1]

    def body(y_ref, w_ref, x_ref, t_ref, p_ref, dz_ref, dy_ref, gx_ref, acc_ref):
        @pl.when(pl.program_id(0) == 0)
        def _():
            acc_ref[...] = jnp.zeros_like(acc_ref)

        gate, ln_g, ln_b = p_ref[0:1, :], p_ref[1:2, :], p_ref[2:3, :]
        z = _nn(y_ref[...], w_ref[...])
        res = ALPHA * x_ref[...] + gate * z
        mu = jnp.mean(res, axis=-1, keepdims=True)
        rc = res - mu
        rstd = lax.rsqrt(jnp.mean(rc * rc, axis=-1, keepdims=True) + LN_EPS)
        rn = rc * rstd
        err = rn * ln_g + ln_b - t_ref[...]
        d_out = err * (1.0 / dm)
        d_rn = d_out * ln_g
        d_res = rstd * (d_rn - jnp.mean(d_rn, axis=-1, keepdims=True)
                        - rn * jnp.mean(d_rn * rn, axis=-1, keepdims=True))
        acc_ref[OUT_ROW_GATE:OUT_ROW_GATE + 1, :] += jnp.sum(d_res * z, axis=0, keepdims=True)
        acc_ref[OUT_ROW_LN_G:OUT_ROW_LN_G + 1, :] += jnp.sum(d_out * rn, axis=0, keepdims=True)
        acc_ref[OUT_ROW_LN_B:OUT_ROW_LN_B + 1, :] += jnp.sum(d_out, axis=0, keepdims=True)
        acc_ref[OUT_ROW_LOSS:OUT_ROW_LOSS + 1, :] += (0.5 / dm) * jnp.sum(err * err, axis=0, keepdims=True)
        gx_ref[...] = ALPHA * d_res
        dz = (d_res * gate).astype(BF16)
        dz_ref[...] = dz
        dy_ref[...] = _nt(dz, w_ref[...])

    rows_d = pl.BlockSpec((tm, dm), lambda i: (i, 0))
    rows_i = pl.BlockSpec((tm, di), lambda i: (i, 0))
    return pl.pallas_call(
        body, name="out_block", grid=(t_rows // tm,),
        in_specs=[rows_i, pl.BlockSpec((di, dm), lambda i: (0, 0)), rows_d, rows_d,
                  pl.BlockSpec((8, dm), lambda i: (0, 0))],
        out_specs=(rows_d, rows_i, rows_d, pl.BlockSpec((8, dm), lambda i: (0, 0))),
        out_shape=(jax.ShapeDtypeStruct((t_rows, dm), BF16), jax.ShapeDtypeStruct((t_rows, di), F32),
                   jax.ShapeDtypeStruct((t_rows, dm), F32), jax.ShapeDtypeStruct((8, dm), F32)),
        compiler_params=_params(("arbitrary",)),
    )(y, w_out, x, target, prm)


def _mod_fwd(c16, w_mod, tn):
    dm, n = w_mod.shape

    def body(c_ref, w_ref, o_ref, a_ref):
        a = _silu(c_ref[...])
        a_ref[...] = a
        o_ref[...] = _nn(a, w_ref[...], HIGHEST)

    return pl.pallas_call(
        body, name="mod_fwd", grid=(n // tn,),
        in_specs=[pl.BlockSpec((16, dm), lambda j: (0, 0)), pl.BlockSpec((dm, tn), lambda j: (0, j))],
        out_specs=(pl.BlockSpec((16, tn), lambda j: (0, j)), pl.BlockSpec((16, dm), lambda j: (0, 0))),
        out_shape=(jax.ShapeDtypeStruct((16, n), F32), jax.ShapeDtypeStruct((16, dm), F32)),
        compiler_params=_params(("arbitrary",)),
    )(c16, w_mod)


def _mod_bwd(a16, dm16, w_mod, tn, rider=None):
    dm, n = w_mod.shape

    def body(a_ref, d_ref, w_ref, dw_ref, dc_ref):
        @pl.when(pl.program_id(0) == 0)
        def _():
            dc_ref[...] = jnp.zeros_like(dc_ref)
        dw_ref[...] = _tn(a_ref[...], d_ref[...], HIGHEST)
        dc_ref[...] += _nt(d_ref[...], w_ref[...], HIGHEST)

    return _call(
        body, (a16, dm16, w_mod), name="mod_bwd", grid=(n // tn,),
        in_specs=[pl.BlockSpec((16, dm), lambda j: (0, 0)), pl.BlockSpec((16, tn), lambda j: (0, j)),
                  pl.BlockSpec((dm, tn), lambda j: (0, j))],
        out_specs=(pl.BlockSpec((dm, tn), lambda j: (0, j)), pl.BlockSpec((16, dm), lambda j: (0, 0))),
        out_shape=(jax.ShapeDtypeStruct((dm, n), F32), jax.ShapeDtypeStruct((16, dm), F32)),
        sem=("arbitrary",), rider=rider)


def _sum_devices(g, fold_rows):
    n_dev, rows, n = g.shape

    def body(g_ref, s_ref, t_ref):
        s = g_ref[0]
        for dev in range(1, n_dev):
            s = s + g_ref[dev]
        t_ref[...] = jnp.broadcast_to(jnp.sum(s, axis=-1, keepdims=True), (rows, LANE))
        s_ref[...] = s
        s_ref[0:fold_rows, :] = s[0:fold_rows] + s[fold_rows:2 * fold_rows]

    return pl.pallas_call(
        body, name="sum_devices",
        out_shape=(jax.ShapeDtypeStruct((rows, n), F32), jax.ShapeDtypeStruct((rows, LANE), F32)),
        compiler_params=_params(),
    )(g)


def _c_ctx_grad(parts, c_ctx_row):
    def body(p_ref, c_ref, o_ref):
        s = p_ref[0]
        for chip in range(1, N_CHIPS):
            s = s + p_ref[2 * chip]
        cv = c_ref[...]
        sg = _sigmoid(cv)
        o_ref[...] = s * (sg * (1.0 + cv * (1.0 - sg)))

    return pl.pallas_call(
        body, name="c_ctx_grad", out_shape=jax.ShapeDtypeStruct(parts.shape[1:], F32), compiler_params=_params(),
    )(parts, c_ctx_row)


def _sum_pair(name, mine, got):
    def body(a_ref, b_ref, o_ref):
        o_ref[...] = (a_ref[...] + b_ref[...]).astype(BF16)

    k, rows, n = mine.shape
    tl = _largest_divisor(n, max(LANE, (1 << 18) // rows), LANE)
    spec = pl.BlockSpec((1, rows, tl), lambda kk, i: (kk, 0, i))
    return pl.pallas_call(
        body, name=name, grid=(k, n // tl), in_specs=[spec, spec], out_specs=spec,
        out_shape=jax.ShapeDtypeStruct(mine.shape, BF16), compiler_params=_params(("parallel", "parallel")),
    )(mine, got)


def _sum_pair_lanes(name, full, got, ci):
    rows, n = got.shape
    tr = _largest_divisor(rows, max(SUBLANE_BF16, (1 << 19) // n), SUBLANE_BF16)

    def body(ci_ref, a_ref, b_ref, o_ref):
        o_ref[...] = (a_ref[...] + b_ref[...].astype(F32)).astype(BF16)

    return pl.pallas_call(
        body, name=name,
        grid_spec=pltpu.PrefetchScalarGridSpec(
            num_scalar_prefetch=1, grid=(rows // tr,),
            in_specs=[pl.BlockSpec((tr, n), lambda i, c: (i, c[0])), pl.BlockSpec((tr, n), lambda i, c: (i, 0))],
            out_specs=pl.BlockSpec((tr, n), lambda i, c: (i, 0))),
        out_shape=jax.ShapeDtypeStruct((rows, n), BF16), compiler_params=_params(("parallel",)),
    )(ci.reshape(1).astype(jnp.int32), full, got)


def _sum_chips(name, got, own, chip):
    k, rows, n = got.shape
    tl = _largest_divisor(n, max(LANE, (1 << 18) // rows), LANE)

    def body(chip_ref, g_ref, own_ref, o_ref):
        total = None
        for kk in range(k):
            term = jnp.where(chip_ref[0] == kk, own_ref[0], g_ref[kk]).astype(F32)
            total = term if total is None else total + term
        o_ref[...] = total

    return pl.pallas_call(
        body, name=name,
        grid_spec=pltpu.PrefetchScalarGridSpec(
            num_scalar_prefetch=1, grid=(n // tl,),
            in_specs=[pl.BlockSpec((k, rows, tl), lambda i, c: (0, 0, i)),
                      pl.BlockSpec((1, rows, tl), lambda i, c: (c[0], 0, i))],
            out_specs=pl.BlockSpec((rows, tl), lambda i, c: (0, i))),
        out_shape=jax.ShapeDtypeStruct((rows, n), F32), compiler_params=_params(("parallel",)),
    )(chip.reshape(1).astype(jnp.int32), got, own)


def _adamw_update(w, g, m, v):
    m2 = ADAM_B1 * m + (1.0 - ADAM_B1) * g
    v2 = ADAM_B2 * v + (1.0 - ADAM_B2) * jnp.square(g)
    m_hat = m2 / (1.0 - ADAM_B1 ** ADAM_STEP)
    v_hat = v2 / (1.0 - ADAM_B2 ** ADAM_STEP)
    return -ADAM_LR * (m_hat / (jnp.sqrt(v_hat) + ADAM_EPS) + ADAM_WD * w), m2, v2


def _adamw(name, w, g, m, v, rider=None):
    rows, n = w.shape
    if rows % 8 == 0:
        tr = _largest_divisor(rows, max(8, (1 << 18) // n), 8)
        block, index, steps = (tr, n), (lambda i: (i, 0)), rows // tr
    else:
        tl = _largest_divisor(n, max(LANE, (1 << 18) // rows), LANE)
        block, index, steps = (rows, tl), (lambda i: (0, i)), n // tl

    def body(w_ref, g_ref, m_ref, v_ref, d_ref, mo_ref, vo_ref):
        d_ref[...], mo_ref[...], vo_ref[...] = _adamw_update(w_ref[...], g_ref[...], m_ref[...], v_ref[...])

    spec = pl.BlockSpec(block, index)
    sds = jax.ShapeDtypeStruct((rows, n), F32)
    return _call(body, (w, g, m, v), name=name, grid=(steps,), in_specs=[spec] * 4, out_specs=(spec,) * 3,
                 out_shape=(sds, sds, sds), sem=("parallel",), rider=rider)


PACK_LANES = 1024


def _pack(pieces):
    flat = jnp.concatenate([p.reshape(-1) for p in pieces])
    total = -(-flat.shape[0] // (8 * PACK_LANES)) * 8 * PACK_LANES
    return jnp.pad(flat, (0, total - flat.shape[0])).reshape(-1, PACK_LANES)


def _unpack(packed, shapes):
    flat = packed.reshape(-1)
    out, off = [], 0
    for shp in shapes:
        size = math.prod(shp)
        out.append(flat[off:off + size].reshape(shp))
        off += size
    return out


def _rows8(rows, width):
    flat = [r.reshape(width) for r in rows] + [jnp.zeros(((8 - len(rows)) * width,), F32)]
    return jnp.concatenate(flat).reshape(8, width)


def kernel(x, c, ctx, c_ctx, w_mod, b_mod, w_in, conv_w, conv_b, hg_lb, ml_gate_b, hg_norm_w, ml_norm_w, w_out, ln_g, ln_b, loss_target, m_c_ctx, m_w_mod, m_b_mod, m_w_in, m_conv_w, m_conv_b, m_hg_lb, m_ml_gate_b, m_hg_norm_w, m_ml_norm_w, m_w_out, m_ln_g, m_ln_b, v_c_ctx, v_w_mod, v_b_mod, v_w_in, v_conv_w, v_conv_b, v_hg_lb, v_ml_gate_b, v_hg_norm_w, v_ml_norm_w, v_w_out, v_ln_g, v_ln_b):
    t_rows, dm = x.shape[1], x.shape[2]
    c_rows = ctx.shape[1]
    w = hg_norm_w.shape[1]
    n_ml = ml_gate_b.shape[-1]
    n_hg = w // HG_DK
    di = 2 * w
    n_in = 10 * w + 4 * n_ml
    ns = w_in.shape[2]
    nm = w_mod.shape[2]
    n_pad = 10 * w + LANE
    r_rows = t_rows + c_rows
    row_gcd = math.gcd(t_rows, c_rows)
    hg_chunk, ml_chunk = math.gcd(HG_CHUNK, row_gcd), math.gcd(ML_CHUNK, row_gcd)
    hg_counts = (t_rows // hg_chunk, c_rows // hg_chunk, hg_chunk)
    ml_counts = (t_rows // ml_chunk, c_rows // ml_chunk, ml_chunk)
    assert ml_norm_w.shape[1] == w and di == dm and N_CHIPS * ns == n_in and N_CHIPS * nm == 3 * dm
    assert w_out.shape[1] * N_CHIPS == di and 4 * n_ml <= LANE and t_rows % GRID_W == 0

    xi, yi, ci = lax.axis_index("x"), lax.axis_index("y"), lax.axis_index("c")
    chip = 2 * xi + yi
    dev = 4 * xi + 2 * yi + ci

    tm = _largest_divisor(math.gcd(t_rows, c_rows), 256, 8)
    tm_mm = _largest_divisor(r_rows, 1088, SUBLANE_BF16)
    tn_mm = LANE * _largest_divisor(n_pad // LANE, 9)
    tn_mod = _largest_divisor(nm, 512, LANE)

    as_t = lambda a: jnp.transpose(a[0])
    half_in = lax.dynamic_slice_in_dim(as_t(w_in).astype(BF16), ci * (dm // 2), dm // 2, 1)
    half_out = lax.dynamic_slice_in_dim(w_out[0].astype(BF16), ci * (di // (2 * N_CHIPS)), di // (2 * N_CHIPS), 0)
    lanes_of = lambda core: pl.ds(core * (dm // 2), dm // 2)
    landing = lambda s, r: (_chip_of(s), slice(None), lanes_of(s[2]))
    own_placed = lax.dynamic_update_slice(jnp.zeros((N_CHIPS + 1, ns, dm), BF16),
                                          as_t(w_in).astype(BF16)[None], (chip, 0, 0))
    whole = jax.ShapeDtypeStruct(own_placed.shape, BF16)

    shard_shapes = [(dm,), (2, 2, w // N_CHIPS), (3, 3, di // N_CHIPS)]
    small_in = _all_gather8(_pack([c, hg_lb, conv_w]))
    g1, gw_in = _Exchange(small_in.srcs + [half_in, own_placed], small_in.out_shapes + [whole],
                          small_in.transfers + [(mask, 1, None, 1, landing) for mask in CHIP_MASKS[:2]],
                          small_in.local_copies, in_place={2: 1}).run("gather_inputs")
    per_dev = [_unpack(g1[i], shard_shapes) for i in range(N_DEV)]
    c_all = jnp.stack([p[0] for p in per_dev])
    lb_full = jnp.concatenate([per_dev[2 * k][1] for k in range(N_CHIPS)], axis=-1)
    conv_w9 = jnp.concatenate([per_dev[2 * k][2] for k in range(N_CHIPS)], axis=-1).reshape(9, di)

    c16 = jnp.concatenate([c_all, c_ctx[None], jnp.zeros((16 - N_DEV - 1, dm), F32)])
    mod_part, a16 = _mod_fwd(c16, w_mod[0], tn_mod)
    g2 = _all_gather8(mod_part).run("gather_mod")[0]
    mod_all = jnp.concatenate([g2[2 * k] for k in range(N_CHIPS)], axis=1) + b_mod
    mod_x = lax.dynamic_index_in_dim(mod_all, dev, 0, keepdims=False).reshape(3, dm)
    mod_c = mod_all[N_DEV].reshape(3, dm)
    prm = jnp.stack([_rows8(list(mod_x), dm), _rows8(list(mod_c), dm)])

    chip_own = lambda s: 2 * s[0] + s[1]
    chip_x = lambda s: 2 * (1 - s[0]) + s[1]
    chip_y = lambda s: 2 * s[0] + 1 - s[1]
    chip_xy = lambda s: 2 * (1 - s[0]) + 1 - s[1]
    quarter = lambda core, q: pl.ds(core * (dm // 2) + q * (dm // 4), dm // 4)
    half_of = lambda which: (lambda s, r: (which(s), slice(None), lanes_of(s[2])))
    relay_x = lambda s, r: (chip_y(s), slice(None), quarter(s[2], 0))
    relay_y = lambda s, r: (chip_x(s), slice(None), quarter(s[2], 1))
    relay_in = _Exchange([gw_in], [whole],
                         [((1, 0, 0), 0, relay_x, 0, relay_x), ((0, 1, 0), 0, relay_y, 0, relay_y)]
                         + [(SIBLING_MASK, 0, half_of(which), 0, half_of(which)) for which in (chip_own, chip_x, chip_y)],
                         in_place={0: 0})
    hc, (gw_in,) = _modulate_fwd(x[0], ctx[0], prm, tm, rider=relay_in)
    gw_in = _Exchange([gw_in], [whole], [(SIBLING_MASK, 0, half_of(chip_xy), 0, half_of(chip_xy))],
                      in_place={0: 0}).run("gather_w_in_pair")[0]
    wt_full = gw_in.reshape((N_CHIPS + 1) * ns, dm)
    assert wt_full.shape[0] >= n_pad
    u, (got_out,) = _mm_nt("in_proj", hc, wt_full, n_pad, tm_mm, tn_mm, F32, rider=_all_gather_chips([half_out]))
    fetched_out = _own_block(chip, half_out, got_out)
    (o_f, o_b), hg_saved, (swapped_out,) = _hg_scan_fwd(u, lb_full, w, *hg_counts,
                                                         rider=_sibling_swap([fetched_out]))
    w_out_full = _join_halves(ci, fetched_out, swapped_out, 1).reshape(di, dm)
    qk = _conv_fwd(u, conv_w9, conv_b, t_rows, c_rows, w, LANE)
    gate_b_row = jnp.pad(ml_gate_b.reshape(1, -1), ((0, 0), (0, LANE - 4 * n_ml)))
    (h_f, h_b), ml_saved = _ml_scan_fwd(qk, u, gate_b_row, w, n_ml, *ml_counts)
    y = _post_fwd(o_f, o_b, h_f, h_b, u, hg_norm_w, ml_norm_w, t_rows, w, n_hg, n_ml, tm)
    prm_out = _rows8([mod_x[2], ln_g, ln_b], dm)
    dz, dy, gx_direct, acc_out = _out_block(y, w_out_full, x[0], loss_target[0], prm_out, tm)

    d_w_out = _mm_tn("d_w_out", y, dz, _largest_divisor(di, 1024, LANE),
                     _largest_divisor(t_rows, 1024, SUBLANE_BF16))
    d_w_out4 = d_w_out.reshape(N_CHIPS, 2, di // (2 * N_CHIPS), dm)
    mine_out = lax.dynamic_index_in_dim(d_w_out4, ci, 1, keepdims=False)
    other_out = lax.dynamic_index_in_dim(d_w_out4, 1 - ci, 1, keepdims=False)
    (d_o, d_h, d_az, d_bo, d_bz, d_wa, d_wb), (got_out,) = _post_bwd(
        o_f, o_b, h_f, h_b, u, hg_norm_w, ml_norm_w, dy, t_rows, w, n_hg, n_ml, tm, rider=_sibling_swap([other_out]))
    pair_out = _sum_pair("rs_pair_sum_w_out", mine_out, got_out)
    (d_aq_f, d_aff, d_ai_f, d_lb_f), (d_aq_b, d_afb, d_ai_b, d_lb_b), (landed_out,) = _hg_scan_bwd(
        u, lb_full, hg_saved, d_o, w, *hg_counts, rider=_chip_scatter([pair_out]))
    half_g_out = _sum_chips("rs_chip_sum_w_out", landed_out, pair_out, chip)
    (d_qk_f, d_v_f, d_g_f, d_gb_f), (d_qk_b, d_v_b, d_g_b, d_gb_b), (sibling_out,) = _ml_scan_bwd(
        qk, u, gate_b_row, ml_saved, d_h, w, n_ml, *ml_counts, rider=_sibling_swap([half_g_out]))
    g_w_out = _join_halves(ci, half_g_out, sibling_out, 0)
    d_bqk, d_cw, d_cb = _conv_bwd(u, (d_qk_f, d_qk_b), conv_w9, conv_b, t_rows, c_rows, w, LANE)
    du = _assemble_du([(d_aq_f, d_aq_b), d_aff, d_afb, (d_ai_f, d_ai_b), d_az, d_bqk, (d_v_f, d_v_b), d_bo, d_bz],
                      (d_g_f, d_g_b), n_pad, tm)
    d_wt_in, d_wt_in_bf16 = _mm_tn("d_w_in", du, hc, tn_mm, tm_mm, with_bf16=True)

    delta, new_m, new_v = {}, {}, {}
    res, (got_in,) = _adamw(
        "adamw_w_out", w_out[0], g_w_out, m_w_out[0], v_w_out[0],
        rider=_Exchange([d_wt_in_bf16], [jax.ShapeDtypeStruct((n_pad, dm // 2), BF16)],
                        [(SIBLING_MASK, 0, lambda s, r: (slice(None), lanes_of(r[2])), 0, None)]))
    delta["w_out"], new_m["w_out"], new_v["w_out"] = (a[None] for a in res)
    pair_half = _sum_pair_lanes("rs_pair_sum_w_in", d_wt_in, got_in, ci)
    pair_in = jnp.stack([pair_half[k * ns:(k + 1) * ns] for k in range(N_CHIPS)])
    d_hc, (landed_in,) = _mm_acc("d_h", du, wt_full, tm_mm, tn_mm, rider=_chip_scatter([pair_in]))
    half_g_in = _sum_chips("rs_chip_sum_w_in", landed_in, pair_in, chip)
    (gx, acc_mod), _ = _modulate_bwd(x[0], ctx[0], d_hc, prm, gx_direct, tm)
    grad_x = gx[None]

    zero_row = jnp.zeros((dm,), F32)
    d_gb = jnp.concatenate([d_gb_f[:, 0:n_ml], d_gb_b[:, n_ml:2 * n_ml], d_gb_f[:, 2 * n_ml:3 * n_ml],
                            d_gb_b[:, 3 * n_ml:4 * n_ml], jnp.zeros((1, dm - 4 * n_ml), F32)], axis=1)
    rows = [acc_mod[0, 0], acc_mod[0, 1], acc_out[OUT_ROW_GATE],
            acc_mod[1, 0], acc_mod[1, 1], zero_row]
    rows += list(d_cw) + [d_cb[0], d_lb_f.reshape(dm), d_lb_b.reshape(dm),
                          jnp.concatenate([d_wa[0], d_wb[0]]), acc_out[OUT_ROW_LN_G], acc_out[OUT_ROW_LN_B],
                          acc_out[OUT_ROW_LOSS], d_gb[0], zero_row]
    ROW_CW, ROW_CB, ROW_LB, ROW_NORM, ROW_LN_G, ROW_LN_B, ROW_LOSS, ROW_GB = 6, 15, 16, 18, 19, 20, 21, 22
    small_rows = jnp.concatenate([r.reshape(dm) for r in rows]).reshape(len(rows), dm)
    g3 = _all_gather8(small_rows).run("gather_small_grads")[0]
    sums, totals = _sum_devices(g3, 3)
    loss = totals[ROW_LOSS, 0]
    dm16 = jnp.concatenate([g3[:, 0:3, :].reshape(N_DEV, 3 * dm), sums[3:6].reshape(1, 3 * dm),
                            jnp.zeros((16 - N_DEV - 1, 3 * dm), F32)])
    (g_w_mod, dc16), (sibling_g_in,) = _mod_bwd(a16, lax.dynamic_slice_in_dim(dm16, chip * nm, nm, 1), w_mod[0],
                                                tn_mod, rider=_sibling_swap([half_g_in]))
    g_wt_in = _join_halves(ci, half_g_in, sibling_g_in, 1)
    g4 = _all_gather8(jnp.pad(dc16[N_DEV:N_DEV + 1], ((0, 7), (0, 0)))).run("gather_c_ctx")[0]
    g_c_ctx = _c_ctx_grad(g4, jnp.broadcast_to(c_ctx[None], (8, dm)))[0]
    res, _ = _adamw("adamw_w_in", as_t(w_in), g_wt_in, as_t(m_w_in), as_t(v_w_in))
    delta["w_in"], new_m["w_in"], new_v["w_in"] = (jnp.transpose(a)[None] for a in res)
    res, _ = _adamw("adamw_w_mod", w_mod[0], g_w_mod, m_w_mod[0], v_w_mod[0])
    delta["w_mod"], new_m["w_mod"], new_v["w_mod"] = (a[None] for a in res)

    chip_cols = lambda a, width: lax.dynamic_slice_in_dim(a, chip * width, width, a.ndim - 1)
    grads = {
        "c_ctx": g_c_ctx,
        "w_mod": g_w_mod[None],
        "b_mod": sums[0:3].reshape(1, 3 * dm),
        "w_in": jnp.transpose(g_wt_in)[None],
        "conv_w": chip_cols(sums[ROW_CW:ROW_CW + 9].reshape(1, 3, 3, di), di // N_CHIPS),
        "conv_b": sums[ROW_CB][None],
        "hg_lb": chip_cols(sums[ROW_LB:ROW_LB + 2].reshape(2, 2, w), w // N_CHIPS),
        "ml_gate_b": sums[ROW_GB, 0:4 * n_ml].reshape(1, 4, n_ml),
        "hg_norm_w": sums[ROW_NORM, 0:w][None],
        "ml_norm_w": sums[ROW_NORM, w:2 * w][None],
        "w_out": g_w_out[None],
        "ln_g": sums[ROW_LN_G][None],
        "ln_b": sums[ROW_LN_B][None],
    }
    weights = dict(c_ctx=c_ctx, w_mod=w_mod, b_mod=b_mod, w_in=w_in, conv_w=conv_w, conv_b=conv_b, hg_lb=hg_lb,
                   ml_gate_b=ml_gate_b, hg_norm_w=hg_norm_w, ml_norm_w=ml_norm_w, w_out=w_out, ln_g=ln_g, ln_b=ln_b)
    mom1 = dict(c_ctx=m_c_ctx, w_mod=m_w_mod, b_mod=m_b_mod, w_in=m_w_in, conv_w=m_conv_w, conv_b=m_conv_b,
                hg_lb=m_hg_lb, ml_gate_b=m_ml_gate_b, hg_norm_w=m_hg_norm_w, ml_norm_w=m_ml_norm_w, w_out=m_w_out,
                ln_g=m_ln_g, ln_b=m_ln_b)
    mom2 = dict(c_ctx=v_c_ctx, w_mod=v_w_mod, b_mod=v_b_mod, w_in=v_w_in, conv_w=v_conv_w, conv_b=v_conv_b,
                hg_lb=v_hg_lb, ml_gate_b=v_ml_gate_b, hg_norm_w=v_hg_norm_w, ml_norm_w=v_ml_norm_w, w_out=v_w_out,
                ln_g=v_ln_g, ln_b=v_ln_b)
    names = list(weights)
    big = ("w_mod", "w_in", "w_out")
    small = [n for n in names if n not in big]

    small_shapes = [weights[n].shape for n in small]
    res, _ = _adamw("adamw_small", *(_pack([src[n] for n in small]) for src in (weights, grads, mom1, mom2)))
    for out, packed in zip((delta, new_m, new_v), res):
        for n, a in zip(small, _unpack(packed, small_shapes)):
            out[n] = a

    return (loss, grad_x, *[grads[n].reshape(weights[n].shape) for n in names], *[delta[n] for n in names],
            *[new_m[n] for n in names], *[new_v[n] for n in names])
```

```python
import functools
import math

import jax
import jax.numpy as jnp
from jax import lax
from jax.experimental import pallas as pl
from jax.experimental.pallas import tpu as pltpu

F32 = jnp.float32
BF16 = jnp.bfloat16
HIGHEST = lax.Precision.HIGHEST
MESH = pl.DeviceIdType.MESH

HG_CHUNK = 64
ML_CHUNK = 256
HG_CHUNKS_PER_STEP = 4
GRID_W = 64
HG_DK = 128
LANE = 128
SUBLANE_BF16 = 16
ALPHA = 2.0 ** 0.25
LN_EPS = 1e-5
NORM_EPS = 1e-6
ADAM_LR = 0.001
ADAM_B1 = 0.9
ADAM_B2 = 0.999
ADAM_EPS = 1e-08
ADAM_WD = 0.01
ADAM_STEP = 10
VMEM_LIMIT = 56 * 1024 * 1024
N_CHIPS = 4
N_DEV = 8


def _params(sem=None):
    return pltpu.CompilerParams(dimension_semantics=sem, vmem_limit_bytes=VMEM_LIMIT)


def _largest_divisor(n, cap, multiple=1):
    best = None
    for d in range(multiple, min(n, cap) + 1, multiple):
        if n % d == 0:
            best = d
    assert best is not None, (n, cap, multiple)
    return best


def _sigmoid(x):
    return jax.nn.sigmoid(x)


def _silu(x):
    return x * jax.nn.sigmoid(x)


def _dot(a, b, dims, precision=None):
    return lax.dot_general(a, b, (dims, ((), ())), precision=precision, preferred_element_type=F32)


def _nn(a, b, precision=None):
    return _dot(a, b, ((1,), (0,)), precision)


def _nt(a, b, precision=None):
    return _dot(a, b, ((1,), (1,)), precision)


def _tn(a, b, precision=None):
    return _dot(a, b, ((0,), (0,)), precision)


def _narrow(x):
    return x.astype(BF16)


@jax.custom_vjp
def _bnn(a, b):
    return _nn(_narrow(a), _narrow(b))


def _bnn_fwd(a, b):
    an, bn = _narrow(a), _narrow(b)
    return _nn(an, bn), (an, bn)


def _bnn_bwd(res, ct):
    an, bn = res
    ctn = _narrow(ct)
    return _nt(ctn, bn), _tn(an, ctn)


_bnn.defvjp(_bnn_fwd, _bnn_bwd)


@jax.custom_vjp
def _bnt(a, b):
    return _nt(_narrow(a), _narrow(b))


def _bnt_fwd(a, b):
    an, bn = _narrow(a), _narrow(b)
    return _nt(an, bn), (an, bn)


def _bnt_bwd(res, ct):
    an, bn = res
    ctn = _narrow(ct)
    return _nn(ctn, bn), _tn(ctn, an)


_bnt.defvjp(_bnt_fwd, _bnt_bwd)


@jax.custom_vjp
def _btn(a, b):
    return _tn(_narrow(a), _narrow(b))


def _btn_fwd(a, b):
    an, bn = _narrow(a), _narrow(b)
    return _tn(an, bn), (an, bn)


def _btn_bwd(res, ct):
    an, bn = res
    ctn = _narrow(ct)
    return _nt(bn, ctn), _nn(an, ctn)


_btn.defvjp(_btn_fwd, _btn_bwd)


def _visible(n, rev):
    r = lax.broadcasted_iota(jnp.int32, (n, n), 0)
    c = lax.broadcasted_iota(jnp.int32, (n, n), 1)
    return (r <= c) if rev else (r >= c)


def _mask_matmul(mask, x):
    mb = mask.astype(BF16)
    hi = x.astype(BF16)
    lo = (x - hi.astype(F32)).astype(BF16)
    return _nn(mb, hi) + _nn(mb, lo)


@functools.partial(jax.custom_vjp, nondiff_argnums=(1,))
def _cumulative(x, rev):
    return _mask_matmul(_visible(x.shape[0], rev), x)


def _cumulative_fwd(x, rev):
    return _cumulative(x, rev), None


def _cumulative_bwd(rev, _, ct):
    return (_mask_matmul(_visible(ct.shape[0], not rev), ct),)


_cumulative.defvjp(_cumulative_fwd, _cumulative_bwd)


def _hg_chunk(states, aq, af, ai, lb0, lb1, rev):
    n_heads = len(states)
    lb = _sigmoid(lb0 - lb1)
    f = lb + (1.0 - lb) * _sigmoid(af)
    g = jnp.log(f)
    k = 1.0 - f
    q = _silu(aq)
    chunk = aq.shape[0]
    vis = _visible(chunk, rev)
    b = _cumulative(g, rev)
    last = 0 if rev else chunk - 1
    b_end = b[last:last + 1]
    b_mid = b[chunk // 2:chunk // 2 + 1]
    q_inter = q * jnp.exp(b)
    q_intra = q * jnp.exp(b - b_mid)
    k_intra = k * jnp.exp(b_mid - b)
    k_dec = k * jnp.exp(b_end - b)
    e_end = jnp.exp(b_end)
    new_states, outs = [], []
    for h in range(n_heads):
        sl = slice(h * HG_DK, (h + 1) * HG_DK)
        s_t = states[h]
        scores = jnp.where(vis, _nt(q_intra[:, sl], k_intra[:, sl]), 0.0)
        outs.append(_nt(q_inter[:, sl], s_t) + _nn(scores, ai[:, sl]))
        new_states.append(e_end[:, sl] * s_t + _tn(ai[:, sl], k_dec[:, sl]))
    return new_states, jnp.concatenate(outs, axis=1)


def _ml_chunk(state, q, k, v, g, gb, rev, d):
    cms, nvs, mbs = state
    n_heads = len(cms)
    dh = q.shape[1] // n_heads
    ga = g + gb
    log_f_all = jax.nn.log_sigmoid(ga)
    chunk = q.shape[0]
    vis = _visible(chunk, rev)
    b_all = _cumulative(log_f_all, rev)
    last = 0 if rev else chunk - 1
    k = k * (dh ** -0.5)
    new_c, new_n, new_m, outs = [], [], [], []
    for h in range(n_heads):
        ci = d * n_heads + h
        cf = (2 + d) * n_heads + h
        sl = slice(h * dh, (h + 1) * dh)
        qh, kh, vh = q[:, sl], k[:, sl], v[:, sl]
        li = ga[:, ci:ci + 1]
        b = b_all[:, cf:cf + 1]
        m = mbs[h][:, 0:1]
        row = jnp.transpose(li - b)
        log_w = jnp.where(vis, b + row, -jnp.inf)
        m_inter = b + m
        m_t = jnp.maximum(m_inter, jnp.max(log_w, axis=-1, keepdims=True))
        w_inter = jnp.exp(m_inter - m_t)
        w_qk = jnp.exp(log_w - m_t) * _bnt(qh, kh)
        num = w_inter * _bnt(qh, cms[h]) + _bnn(w_qk, vh)
        den = w_inter * jnp.sum(qh * nvs[h], axis=-1, keepdims=True) + jnp.sum(w_qk, axis=-1, keepdims=True)
        outs.append(num / jnp.maximum(jnp.abs(den), jnp.exp(-m_t)))
        m_new = m_t[last:last + 1]
        b_end = b[last:last + 1]
        w_s = jnp.exp(b_end - b + li - m_new)
        decay = jnp.exp(b_end + m - m_new)
        new_c.append(decay * cms[h] + _btn(w_s * vh, kh))
        new_n.append(decay * nvs[h] + jnp.sum(w_s * kh, axis=0, keepdims=True))
        new_m.append(jnp.broadcast_to(m_new, (1, LANE)))
    return (new_c, new_n, new_m), jnp.concatenate(outs, axis=1)


def _post_fn(o_f, o_b, az, h_f, h_b, bo, bz, wa, wb, n_hg, n_ml):
    o = o_f + o_b
    parts = []
    for h in range(n_hg):
        s = o[:, h * HG_DK:(h + 1) * HG_DK]
        parts.append(s * lax.rsqrt(jnp.mean(s * s, axis=-1, keepdims=True) + NORM_EPS))
    y_a = jnp.concatenate(parts, axis=1) * wa * _silu(az)
    hh = h_f + h_b
    dh = hh.shape[1] // n_ml
    parts = []
    for h in range(n_ml):
        s = hh[:, h * dh:(h + 1) * dh]
        mu = jnp.mean(s, axis=-1, keepdims=True)
        sc = s - mu
        parts.append(sc * lax.rsqrt(jnp.mean(sc * sc, axis=-1, keepdims=True) + NORM_EPS))
    y_b = jnp.concatenate(parts, axis=1) * wb * _sigmoid(bo) * _silu(bz)
    return jnp.concatenate([y_a, y_b], axis=1)


def _chip_of(dev):
    return 2 * dev[0] + dev[1]


def _index_of(dev):
    return 4 * dev[0] + 2 * dev[1] + dev[2]


class _Exchange:
    def __init__(self, srcs, out_shapes, transfers, local_copies=(), in_place=None):
        self.srcs, self.out_shapes = list(srcs), list(out_shapes)
        self.transfers, self.local_copies = list(transfers), list(local_copies)
        self.in_place = dict(in_place or {})

    def scratch(self):
        return [pltpu.SemaphoreType.DMA((len(self.transfers),)), pltpu.SemaphoreType.DMA((len(self.transfers),)),
                pltpu.SemaphoreType.DMA((max(len(self.local_copies), 1),))]

    def copies(self, ins, outs, send_sems, recv_sems, local_sems):
        me = (lax.axis_index("x"), lax.axis_index("y"), lax.axis_index("c"))

        def pick(ref, fn, *who):
            return ref if fn is None else ref.at[fn(*who)]

        sends, recvs, locs = [], [], []
        for t, (mask, si, sfn, di, dfn) in enumerate(self.transfers):
            peer = tuple(1 - p if flip else p for p, flip in zip(me, mask))
            sends.append(pltpu.make_async_remote_copy(
                src_ref=pick(ins[si], sfn, me, peer), dst_ref=pick(outs[di], dfn, me, peer),
                send_sem=send_sems.at[t], recv_sem=recv_sems.at[t], device_id=peer, device_id_type=MESH))
            landing = pick(outs[di], dfn, peer, me)
            recvs.append(pltpu.make_async_remote_copy(
                src_ref=landing, dst_ref=landing,
                send_sem=send_sems.at[t], recv_sem=recv_sems.at[t], device_id=peer, device_id_type=MESH))
        for l, (si, sfn, di, dfn) in enumerate(self.local_copies):
            locs.append(pltpu.make_async_copy(pick(ins[si], sfn, me), pick(outs[di], dfn, me), local_sems.at[l]))

        def start():
            for cp in locs + sends:
                cp.start()

        def wait():
            for cp in recvs:
                cp.wait_recv()
            for cp in sends:
                cp.wait_send()
            for cp in locs:
                cp.wait()

        return start, wait

    def run(self, name):
        n_in, n_out = len(self.srcs), len(self.out_shapes)

        def body(*refs):
            start, wait = self.copies(refs[:n_in], refs[n_in:n_in + n_out], *refs[n_in + n_out:])
            start()
            wait()

        hbm = pl.BlockSpec(memory_space=pltpu.HBM)
        return pl.pallas_call(
            body, name=name, out_shape=tuple(self.out_shapes), in_specs=[hbm] * n_in,
            out_specs=tuple([hbm] * n_out), scratch_shapes=self.scratch(), input_output_aliases=self.in_place,
        )(*self.srcs)


def _call(body, operands, *, name, grid, in_specs, out_specs, out_shape, scratch_shapes=(), sem=None, rider=None):
    out_specs, out_shape, scratch_shapes = list(out_specs), list(out_shape), list(scratch_shapes)
    if rider is None:
        res = pl.pallas_call(
            body, name=name, grid=grid, in_specs=list(in_specs), out_specs=tuple(out_specs),
            out_shape=tuple(out_shape), scratch_shapes=scratch_shapes, compiler_params=_params(sem),
        )(*operands)
        return list(res), []
    counts = (len(in_specs), len(rider.srcs), len(out_specs), len(rider.out_shapes), len(scratch_shapes), 3)

    def full(*refs):
        groups, pos = [], 0
        for k in counts:
            groups.append(refs[pos:pos + k])
            pos += k
        own_in, ex_in, own_out, ex_out, own_scr, ex_scr = groups
        ids = [pl.program_id(a) for a in range(len(grid))]
        first = functools.reduce(jnp.logical_and, [i == 0 for i in ids])
        last = functools.reduce(jnp.logical_and, [i == g - 1 for i, g in zip(ids, grid)])
        start, wait = rider.copies(ex_in, ex_out, *ex_scr)
        pl.when(first)(start)
        body(*own_in, *own_out, *own_scr)
        pl.when(last)(wait)

    hbm = pl.BlockSpec(memory_space=pltpu.HBM)
    res = pl.pallas_call(
        full, name=name, grid=grid, in_specs=list(in_specs) + [hbm] * counts[1],
        out_specs=tuple(out_specs + [hbm] * counts[3]), out_shape=tuple(out_shape + rider.out_shapes),
        scratch_shapes=scratch_shapes + rider.scratch(), compiler_params=_params(("arbitrary",) * len(grid)),
        input_output_aliases={counts[0] + i: counts[2] + o for i, o in rider.in_place.items()},
    )(*operands, *rider.srcs)
    return list(res[:counts[2]]), list(res[counts[2]:])


ALL_MASKS = [(mx, my, mc) for mx in (0, 1) for my in (0, 1) for mc in (0, 1)][1:]
CHIP_MASKS = [(1, 0, 0), (0, 1, 0), (1, 1, 0)]
SIBLING_MASK = (0, 0, 1)


def _all_gather8(v):
    out = jax.ShapeDtypeStruct((N_DEV,) + v.shape, v.dtype)
    slot = lambda sender, receiver: _index_of(sender)
    transfers = [(mask, 0, None, 0, slot) for mask in ALL_MASKS]
    return _Exchange([v], [out], transfers, [(0, None, 0, lambda me: _index_of(me))])


def _all_gather_chips(arrays):
    outs = [jax.ShapeDtypeStruct((N_CHIPS,) + a.shape, a.dtype) for a in arrays]
    slot = lambda sender, receiver: _chip_of(sender)
    return _Exchange(arrays, outs, [(mask, i, None, i, slot) for i in range(len(arrays)) for mask in CHIP_MASKS])


def _sibling_swap(arrays):
    outs = [jax.ShapeDtypeStruct(a.shape, a.dtype) for a in arrays]
    return _Exchange(arrays, outs, [(SIBLING_MASK, i, None, i, None) for i in range(len(arrays))])


def _chip_scatter(arrays):
    outs = [jax.ShapeDtypeStruct(a.shape, a.dtype) for a in arrays]
    transfers = [(mask, i, lambda s, r: _chip_of(r), i, lambda s, r: _chip_of(s))
                 for i in range(len(arrays)) for mask in CHIP_MASKS]
    return _Exchange(arrays, outs, transfers)


def _own_block(chip, own, blocks):
    sel = (lax.broadcasted_iota(jnp.int32, (N_CHIPS,) + (1,) * (blocks.ndim - 1), 0) == chip)
    return jnp.where(sel, own if own.ndim == blocks.ndim else own[None], blocks)


def _join_halves(ci, mine, other, axis):
    return jnp.where(ci == 0, jnp.concatenate([mine, other], axis=axis), jnp.concatenate([other, mine], axis=axis))


def _mm_nt(name, a, b, n, tm, tn, out_dtype, rider=None):
    m, k = a.shape

    def body(a_ref, b_ref, o_ref):
        o_ref[...] = _nt(a_ref[...], b_ref[...]).astype(out_dtype)

    (out,), rode = _call(
        body, (a, b), name=name, grid=(n // tn, m // tm),
        in_specs=[pl.BlockSpec((tm, k), lambda j, i: (i, 0)), pl.BlockSpec((tn, k), lambda j, i: (j, 0))],
        out_specs=[pl.BlockSpec((tm, tn), lambda j, i: (i, j))],
        out_shape=[jax.ShapeDtypeStruct((m, n), out_dtype)], sem=("parallel", "parallel"), rider=rider)
    return out, rode


def _mm_acc(name, a, b, tm, tk, rider=None):
    m, kc = a.shape
    n = b.shape[1]

    def body(a_ref, b_ref, o_ref):
        @pl.when(pl.program_id(1) == 0)
        def _():
            o_ref[...] = jnp.zeros_like(o_ref)
        o_ref[...] += _nn(a_ref[...], b_ref[...])

    (out,), rode = _call(
        body, (a, b), name=name, grid=(m // tm, kc // tk),
        in_specs=[pl.BlockSpec((tm, tk), lambda i, kk: (i, kk)), pl.BlockSpec((tk, n), lambda i, kk: (kk, 0))],
        out_specs=[pl.BlockSpec((tm, n), lambda i, kk: (i, 0))],
        out_shape=[jax.ShapeDtypeStruct((m, n), F32)], sem=("parallel", "arbitrary"), rider=rider)
    return out, rode


def _mm_tn(name, a, b, tm, tk, with_bf16=False):
    kr, m = a.shape
    n = b.shape[1]
    steps_k = kr // tk

    def body(a_ref, b_ref, o_ref, *narrow):
        @pl.when(pl.program_id(1) == 0)
        def _():
            o_ref[...] = jnp.zeros_like(o_ref)
        o_ref[...] += _tn(a_ref[...], b_ref[...])
        if with_bf16:
            @pl.when(pl.program_id(1) == steps_k - 1)
            def _():
                narrow[0][...] = o_ref[...].astype(BF16)

    out_spec = pl.BlockSpec((tm, n), lambda i, kk: (i, 0))
    res = pl.pallas_call(
        body, name=name, grid=(m // tm, steps_k),
        in_specs=[pl.BlockSpec((tk, tm), lambda i, kk: (kk, i)), pl.BlockSpec((tk, n), lambda i, kk: (kk, 0))],
        out_specs=(out_spec,) * (2 if with_bf16 else 1),
        out_shape=(jax.ShapeDtypeStruct((m, n), F32),) + ((jax.ShapeDtypeStruct((m, n), BF16),) if with_bf16 else ()),
        compiler_params=_params(("parallel", "arbitrary")),
    )(a, b)
    return res if with_bf16 else res[0]


def _modulate_fwd(x, ctx, prm, tm, rider=None):
    t_rows, dm = x.shape
    lat = t_rows // tm
    r = t_rows + ctx.shape[0]

    def body(x_ref, c_ref, p_ref, h_ref):
        xv = jnp.where(pl.program_id(0) >= lat, c_ref[...], x_ref[...])
        mu = jnp.mean(xv, axis=-1, keepdims=True)
        xm = xv - mu
        n = xm * lax.rsqrt(jnp.mean(xm * xm, axis=-1, keepdims=True) + LN_EPS)
        h_ref[...] = (n * (1.0 + p_ref[0, 1:2, :]) + p_ref[0, 0:1, :]).astype(BF16)

    (h,), rode = _call(
        body, (x, ctx, prm), name="modulate_fwd", grid=(r // tm,),
        in_specs=[pl.BlockSpec((tm, dm), lambda i: (jnp.minimum(i, lat - 1), 0)),
                  pl.BlockSpec((tm, dm), lambda i: (jnp.maximum(i - lat, 0), 0)),
                  pl.BlockSpec((1, 8, dm), lambda i: ((i >= lat).astype(jnp.int32), 0, 0))],
        out_specs=[pl.BlockSpec((tm, dm), lambda i: (i, 0))],
        out_shape=[jax.ShapeDtypeStruct((r, dm), BF16)], sem=("parallel",), rider=rider)
    return h, rode


def _modulate_bwd(x, ctx, dh, prm, gx_direct, tm, rider=None):
    t_rows, dm = x.shape
    lat, n_ct = t_rows // tm, ctx.shape[0] // tm
    is_ctx = lambda i: i < n_ct
    cls = lambda i: is_ctx(i).astype(jnp.int32)
    lat_tile = lambda i: (jnp.maximum(i - n_ct, 0), 0)

    def body(x_ref, c_ref, dh_ref, p_ref, gd_ref, gx_ref, acc_ref):
        i = pl.program_id(0)

        @pl.when((i == 0) | (i == n_ct))
        def _():
            acc_ref[...] = jnp.zeros_like(acc_ref)

        x = jnp.where(is_ctx(i), c_ref[...], x_ref[...])
        dh_v = dh_ref[...]
        mu = jnp.mean(x, axis=-1, keepdims=True)
        xm = x - mu
        rstd = lax.rsqrt(jnp.mean(xm * xm, axis=-1, keepdims=True) + LN_EPS)
        n = xm * rstd
        acc_ref[0, 0:1, :] += jnp.sum(dh_v, axis=0, keepdims=True)
        acc_ref[0, 1:2, :] += jnp.sum(dh_v * n, axis=0, keepdims=True)
        dn = dh_v * (1.0 + p_ref[0, 1:2, :])
        dx = rstd * (dn - jnp.mean(dn, axis=-1, keepdims=True) - n * jnp.mean(dn * n, axis=-1, keepdims=True))
        gx_ref[...] = dx + gd_ref[...]

    return _call(
        body, (x, ctx, dh, prm, gx_direct), name="modulate_bwd", grid=(n_ct + lat,),
        in_specs=[pl.BlockSpec((tm, dm), lat_tile),
                  pl.BlockSpec((tm, dm), lambda i: (jnp.minimum(i, n_ct - 1), 0)),
                  pl.BlockSpec((tm, dm), lambda i: (jnp.where(is_ctx(i), lat + i, i - n_ct), 0)),
                  pl.BlockSpec((1, 8, dm), lambda i: (cls(i), 0, 0)),
                  pl.BlockSpec((tm, dm), lat_tile)],
        out_specs=(pl.BlockSpec((tm, dm), lat_tile), pl.BlockSpec((1, 8, dm), lambda i: (cls(i), 0, 0))),
        out_shape=(jax.ShapeDtypeStruct((t_rows, dm), F32), jax.ShapeDtypeStruct((2, 8, dm), F32)),
        sem=("arbitrary",), rider=rider)


def _conv_parts(t_rows, c_rows):
    return ((0, t_rows, t_rows // GRID_W, GRID_W), (t_rows, c_rows, 1, c_rows))


def _col_shifts(x2, rows_g, width_g):
    n, ct = x2.shape
    col = lax.broadcasted_iota(jnp.int32, (width_g, ct), 0)
    as_grid = lambda a: a.reshape(rows_g, width_g, ct)
    left = as_grid(pltpu.roll(x2, 1, 0)) * (col >= 1).astype(F32)
    right = as_grid(pltpu.roll(x2, n - 1, 0)) * (col <= width_g - 2).astype(F32)
    return [left, as_grid(x2), right]


CONV_BLOCK_ROWS = 4


def _conv_blocks(t_rows, c_rows):
    for t0, _, rows_g, width_g in _conv_parts(t_rows, c_rows):
        nb = min(CONV_BLOCK_ROWS, rows_g)
        assert rows_g % nb == 0
        for g0 in range(0, rows_g, nb):
            yield t0, rows_g, width_g, g0, nb


def _slab(ref, t0, rows_g, width_g, g0, nb):
    if rows_g == 1:
        return ref[t0:t0 + width_g, :]
    lo, hi = max(g0 - 1, 0), min(g0 + nb + 1, rows_g)
    parts = [ref[t0 + lo * width_g:t0 + hi * width_g, :]]
    zero = jnp.zeros((width_g, ref.shape[1]), F32)
    if g0 == 0:
        parts.insert(0, zero)
    if g0 + nb == rows_g:
        parts.append(zero)
    return jnp.concatenate(parts, axis=0)


def _conv_taps(cols, w_ref, nb, flip):
    one_row = cols[0].shape[0] == nb
    acc = None
    for a in range(3):
        if one_row and a != 1:
            continue
        for b in range(3):
            tap = (2 - a) * 3 + (2 - b) if flip else a * 3 + b
            term = (cols[b] if one_row else cols[b][a:a + nb]) * w_ref[tap:tap + 1, :]
            acc = term if acc is None else acc + term
    return acc


def _conv_fwd(u, conv_w9, conv_b, t_rows, c_rows, w, ct):
    r = u.shape[0]
    base = 5 * w // ct

    def body(x_ref, w_ref, b_ref, o_ref):
        for t0, rows_g, width_g, g0, nb in _conv_blocks(t_rows, c_rows):
            slab = _slab(x_ref, t0, rows_g, width_g, g0, nb)
            cols = _col_shifts(slab, slab.shape[0] // width_g, width_g)
            pre = _conv_taps(cols, w_ref, nb, False) + b_ref[...]
            o_ref[t0 + g0 * width_g:t0 + (g0 + nb) * width_g, :] = _silu(pre).reshape(nb * width_g, ct)

    return pl.pallas_call(
        body, name="conv_fwd", grid=(2 * w // ct,),
        in_specs=[pl.BlockSpec((r, ct), lambda i: (0, base + i)), pl.BlockSpec((9, ct), lambda i: (0, i)),
                  pl.BlockSpec((1, ct), lambda i: (0, i))],
        out_specs=pl.BlockSpec((r, ct), lambda i: (0, i)),
        out_shape=jax.ShapeDtypeStruct((r, 2 * w), F32),
        compiler_params=_params(("parallel",)),
    )(u, conv_w9, conv_b)


def _conv_bwd(u, dqk_pair, conv_w9, conv_b, t_rows, c_rows, w, ct):
    r = u.shape[0]
    base = 5 * w // ct

    def body(x_ref, d1_ref, d2_ref, w_ref, b_ref, dx_ref, dw_ref, db_ref, dpre_ref):
        dw = [jnp.zeros((1, ct), F32) for _ in range(9)]
        db = jnp.zeros((1, ct), F32)
        for t0, rows_g, width_g, g0, nb in _conv_blocks(t_rows, c_rows):
            rows = slice(t0 + g0 * width_g, t0 + (g0 + nb) * width_g)
            slab = _slab(x_ref, t0, rows_g, width_g, g0, nb)
            cols = _col_shifts(slab, slab.shape[0] // width_g, width_g)
            pre = _conv_taps(cols, w_ref, nb, False) + b_ref[...]
            sg = _sigmoid(pre)
            dpre = (d1_ref[rows, :] + d2_ref[rows, :]).reshape(pre.shape) * (sg * (1.0 + pre * (1.0 - sg)))
            dpre_ref[rows, :] = dpre.reshape(nb * width_g, ct)
            db = db + jnp.sum(jnp.sum(dpre, axis=0), axis=0, keepdims=True)
            for a in range(3):
                if rows_g == 1 and a != 1:
                    continue
                for b in range(3):
                    moved = cols[b] if rows_g == 1 else cols[b][a:a + nb]
                    dw[a * 3 + b] = dw[a * 3 + b] + jnp.sum(jnp.sum(moved * dpre, axis=0), axis=0, keepdims=True)
        for t0, rows_g, width_g, g0, nb in _conv_blocks(t_rows, c_rows):
            slab = _slab(dpre_ref, t0, rows_g, width_g, g0, nb)
            cols = _col_shifts(slab, slab.shape[0] // width_g, width_g)
            dx_ref[t0 + g0 * width_g:t0 + (g0 + nb) * width_g, :] = _conv_taps(cols, w_ref, nb, True).reshape(
                nb * width_g, ct).astype(BF16)
        for tap in range(9):
            dw_ref[tap:tap + 1, :] = dw[tap]
        db_ref[...] = db

    return pl.pallas_call(
        body, name="conv_bwd", grid=(2 * w // ct,),
        in_specs=[pl.BlockSpec((r, ct), lambda i: (0, base + i)), pl.BlockSpec((r, ct), lambda i: (0, i)),
                  pl.BlockSpec((r, ct), lambda i: (0, i)),
                  pl.BlockSpec((9, ct), lambda i: (0, i)), pl.BlockSpec((1, ct), lambda i: (0, i))],
        out_specs=(pl.BlockSpec((r, ct), lambda i: (0, i)), pl.BlockSpec((9, ct), lambda i: (0, i)),
                   pl.BlockSpec((1, ct), lambda i: (0, i))),
        out_shape=(jax.ShapeDtypeStruct((r, 2 * w), BF16), jax.ShapeDtypeStruct((9, 2 * w), F32),
                   jax.ShapeDtypeStruct((1, 2 * w), F32)),
        scratch_shapes=[pltpu.VMEM((r, ct), F32)],
        compiler_params=_params(("parallel",)),
    )(u, dqk_pair[0], dqk_pair[1], conv_w9, conv_b)


def _assemble_du(groups, gates, n_pad, tm):
    flat, layout = [], []
    for entry in list(groups) + [gates]:
        parts = entry if isinstance(entry, (tuple, list)) else (entry,)
        layout.append((len(flat), len(parts), parts[0].shape[1]))
        flat += list(parts)
    r = flat[0].shape[0]

    def body(*refs):
        o_ref = refs[-1]
        col = 0
        for first, count, width in layout:
            val = refs[first][...]
            for extra in range(1, count):
                val = val.astype(F32) + refs[first + extra][...].astype(F32)
            o_ref[:, col:col + width] = val.astype(BF16)
            col += width
        assert col == n_pad

    return pl.pallas_call(
        body, name="assemble_du", grid=(r // tm,),
        in_specs=[pl.BlockSpec((tm, a.shape[1]), lambda i: (i, 0)) for a in flat],
        out_specs=pl.BlockSpec((tm, n_pad), lambda i: (i, 0)),
        out_shape=jax.ShapeDtypeStruct((r, n_pad), BF16),
        compiler_params=_params(("parallel",)),
    )(*flat)


def _scan_order(n_lat, n_ctx, rev):
    n = n_lat + n_ctx
    if rev:
        return lambda j: n - 1 - j
    return lambda j: (j + n_lat) % n


DIRS = (False, True)


def _hg_scan_fwd(u, lb_full, w, n_lat, n_ctx, chunk, rider=None):
    r = u.shape[0]
    n_heads = w // HG_DK
    sub = HG_CHUNKS_PER_STEP if n_lat % HG_CHUNKS_PER_STEP == 0 and n_ctx % HG_CHUNKS_PER_STEP == 0 else 1
    n_steps = (n_lat + n_ctx) // sub
    nat = [_scan_order(n_lat // sub, n_ctx // sub, rev) for rev in DIRS]
    rows = sub * chunk

    def body(*refs):
        ins, outs, scratch = refs[:8], refs[8:12], refs[12:]

        @pl.when(pl.program_id(0) == 0)
        def _():
            for s_ref in scratch:
                s_ref[...] = jnp.zeros_like(s_ref)

        for d, rev in enumerate(DIRS):
            aq, af, ai, lb_ref = ins[4 * d:4 * d + 4]
            o_ref, save_ref = outs[2 * d:2 * d + 2]
            state = [scratch[d][h] for h in range(n_heads)]
            for p in range(sub):
                sl = slice((sub - 1 - p if rev else p) * chunk, (sub - p if rev else p + 1) * chunk)
                for h in range(n_heads):
                    save_ref[0, p, h] = state[h]
                state, o = _hg_chunk(state, aq[sl, :], af[sl, :], ai[sl, :], lb_ref[0, 0:1, :], lb_ref[0, 1:2, :], rev)
                o_ref[sl, :] = o
            for h in range(n_heads):
                scratch[d][h] = state[h]

    in_specs, out_specs, out_shape = [], [], []
    for d in range(2):
        in_specs += [pl.BlockSpec((rows, w), lambda j, d=d: (nat[d](j), 0)),
                     pl.BlockSpec((rows, w), lambda j, d=d: (nat[d](j), 1 + d)),
                     pl.BlockSpec((rows, w), lambda j, d=d: (nat[d](j), 3)),
                     pl.BlockSpec((1, 2, w), lambda j, d=d: (d, 0, 0))]
        out_specs += [pl.BlockSpec((rows, w), lambda j, d=d: (nat[d](j), 0)),
                      pl.BlockSpec((1, sub, n_heads, HG_DK, HG_DK), lambda j: (j, 0, 0, 0, 0))]
        out_shape += [jax.ShapeDtypeStruct((r, w), F32),
                      jax.ShapeDtypeStruct((n_steps, sub, n_heads, HG_DK, HG_DK), F32)]
    (o_f, s_f, o_b, s_b), rode = _call(
        body, (u, u, u, lb_full, u, u, u, lb_full), name="hg_scan_fwd", grid=(n_steps,), in_specs=in_specs,
        out_specs=out_specs, out_shape=out_shape, scratch_shapes=[pltpu.VMEM((n_heads, HG_DK, HG_DK), F32)] * 2,
        sem=("arbitrary",), rider=rider)
    return (o_f, o_b), (s_f, s_b), rode


def _hg_scan_bwd(u, lb_full, saved, d_o, w, n_lat, n_ctx, chunk, rider=None):
    r = u.shape[0]
    n_heads = w // HG_DK
    n_steps, sub = saved[0].shape[0], saved[0].shape[1]
    n_lat_s = n_lat // sub
    step = lambda jj: n_steps - 1 - jj
    nat = [(lambda jj, o=_scan_order(n_lat_s, n_ctx // sub, rev): o(step(jj))) for rev in DIRS]
    rows = sub * chunk

    def body(*refs):
        ins, outs, scratch = refs[:12], refs[12:20], refs[20:]
        jj = pl.program_id(0)

        @pl.when(jj == 0)
        def _():
            for d in range(2):
                scratch[d][...] = jnp.zeros_like(scratch[d])
                outs[4 * d + 3][...] = jnp.zeros_like(outs[4 * d + 3])

        for d, rev in enumerate(DIRS):
            aq, af, ai, lb_ref, save_ref, do_ref = ins[6 * d:6 * d + 6]
            daq_ref, daf_ref, dai_ref, dlb_ref = outs[4 * d:4 * d + 4]
            f = lambda st, a, b, c, l0, l1, rev=rev: _hg_chunk(st, a, b, c, l0, l1, rev)
            latent = (nat[d](jj) < n_lat_s).astype(F32)
            d_state = [scratch[d][h] for h in range(n_heads)]
            for p in reversed(range(sub)):
                sl = slice((sub - 1 - p if rev else p) * chunk, (sub - p if rev else p + 1) * chunk)
                _, vjp = jax.vjp(f, [save_ref[0, p, h] for h in range(n_heads)], aq[sl, :], af[sl, :], ai[sl, :],
                                 lb_ref[0, 0:1, :], lb_ref[0, 1:2, :])
                d_state, daq, daf, dai, dl0, dl1 = vjp((d_state, do_ref[sl, :] * latent))
                daq_ref[sl, :] = daq.astype(BF16)
                daf_ref[sl, :] = daf.astype(BF16)
                dai_ref[sl, :] = dai.astype(BF16)
                dlb_ref[0:1, :] += dl0
                dlb_ref[1:2, :] += dl1
            for h in range(n_heads):
                scratch[d][h] = d_state[h]

    in_specs, out_specs, out_shape, operands = [], [], [], []
    for d in range(2):
        row = lambda jj, d=d: (nat[d](jj), 0)
        in_specs += [pl.BlockSpec((rows, w), row),
                     pl.BlockSpec((rows, w), lambda jj, d=d: (nat[d](jj), 1 + d)),
                     pl.BlockSpec((rows, w), lambda jj, d=d: (nat[d](jj), 3)),
                     pl.BlockSpec((1, 2, w), lambda jj, d=d: (d, 0, 0)),
                     pl.BlockSpec((1, sub, n_heads, HG_DK, HG_DK), lambda jj: (step(jj), 0, 0, 0, 0)),
                     pl.BlockSpec((rows, w), lambda jj, d=d: (jnp.minimum(nat[d](jj), n_lat_s - 1), 0))]
        operands += [u, u, u, lb_full, saved[d], d_o]
        out_specs += [pl.BlockSpec((rows, w), row)] * 3 + [pl.BlockSpec((2, w), lambda jj: (0, 0))]
        out_shape += [jax.ShapeDtypeStruct((r, w), BF16)] * 3 + [jax.ShapeDtypeStruct((2, w), F32)]
    res, rode = _call(
        body, operands, name="hg_scan_bwd", grid=(n_steps,), in_specs=in_specs, out_specs=out_specs,
        out_shape=out_shape, scratch_shapes=[pltpu.VMEM((n_heads, HG_DK, HG_DK), F32)] * 2,
        sem=("arbitrary",), rider=rider)
    return res[0:4], res[4:8], rode


def _ml_state_shapes(n_chunks, n_heads, dh):
    return (jax.ShapeDtypeStruct((n_chunks, n_heads, dh, dh), F32),
            jax.ShapeDtypeStruct((n_chunks, n_heads, 1, dh), F32),
            jax.ShapeDtypeStruct((n_chunks, n_heads, 1, LANE), F32))


def _ml_state_specs(n_heads, dh, index):
    return (pl.BlockSpec((1, n_heads, dh, dh), lambda j: (index(j), 0, 0, 0)),
            pl.BlockSpec((1, n_heads, 1, dh), lambda j: (index(j), 0, 0, 0)),
            pl.BlockSpec((1, n_heads, 1, LANE), lambda j: (index(j), 0, 0, 0)))


def _ml_state_scratch(n_heads, dh):
    return [pltpu.VMEM((n_heads, dh, dh), F32), pltpu.VMEM((n_heads, 1, dh), F32), pltpu.VMEM((n_heads, 1, LANE), F32)]


def _ml_scan_fwd(qk, u, gate_b, w, n_heads, n_lat, n_ctx, chunk):
    r = u.shape[0]
    dh = w // n_heads
    n_chunks = n_lat + n_ctx
    nat = [_scan_order(n_lat, n_ctx, rev) for rev in DIRS]

    def body(*refs):
        ins, outs, scratch = refs[:10], refs[10:18], refs[18:]

        @pl.when(pl.program_id(0) == 0)
        def _():
            for s_ref in scratch:
                s_ref[...] = jnp.zeros_like(s_ref)

        results = []
        for d, rev in enumerate(DIRS):
            q, k, v, g, gb = ins[5 * d:5 * d + 5]
            state = tuple([ref[h] for h in range(n_heads)] for ref in scratch[3 * d:3 * d + 3])
            results.append((state, _ml_chunk(state, q[...], k[...], v[...], g[...], gb[...], rev, d)))
        for d, (state, (new, o)) in enumerate(results):
            outs[4 * d][...] = o
            for part in range(3):
                for h in range(n_heads):
                    outs[4 * d + 1 + part][0, h] = state[part][h]
                    scratch[3 * d + part][h] = new[part][h]

    in_specs, out_specs, out_shape = [], [], []
    for d in range(2):
        in_specs += [pl.BlockSpec((chunk,w), lambda j, d=d: (nat[d](j), 0)),
                     pl.BlockSpec((chunk,w), lambda j, d=d: (nat[d](j), 1)),
                     pl.BlockSpec((chunk,w), lambda j, d=d: (nat[d](j), 7)),
                     pl.BlockSpec((chunk,LANE), lambda j, d=d: (nat[d](j), 10 * w // LANE)),
                     pl.BlockSpec((1, LANE), lambda j: (0, 0))]
        out_specs += [pl.BlockSpec((chunk,w), lambda j, d=d: (nat[d](j), 0))]
        out_specs += list(_ml_state_specs(n_heads, dh, lambda j: j))
        out_shape += [jax.ShapeDtypeStruct((r, w), F32)] + list(_ml_state_shapes(n_chunks, n_heads, dh))
    res = pl.pallas_call(
        body, name="ml_scan_fwd", grid=(n_chunks,), in_specs=in_specs, out_specs=tuple(out_specs),
        out_shape=tuple(out_shape), scratch_shapes=_ml_state_scratch(n_heads, dh) * 2,
        compiler_params=_params(("arbitrary",)),
    )(qk, qk, u, u, gate_b, qk, qk, u, u, gate_b)
    return (res[0], res[4]), (res[1:4], res[5:8])


def _ml_scan_bwd(qk, u, gate_b, saved, d_h, w, n_heads, n_lat, n_ctx, chunk, rider=None):
    r = u.shape[0]
    dh = w // n_heads
    n_chunks = n_lat + n_ctx
    step = lambda jj: n_chunks - 1 - jj
    nat = [(lambda jj, o=_scan_order(n_lat, n_ctx, rev): o(step(jj))) for rev in DIRS]

    def body(*refs):
        ins, outs, scratch = refs[:18], refs[18:26], refs[26:]
        jj = pl.program_id(0)

        @pl.when(jj == 0)
        def _():
            for s_ref in scratch:
                s_ref[...] = jnp.zeros_like(s_ref)
            for d in range(2):
                outs[4 * d + 3][...] = jnp.zeros_like(outs[4 * d + 3])

        results = []
        for d, rev in enumerate(DIRS):
            q, k, v, g, gb, sc, sn, sm, dh_ref = ins[9 * d:9 * d + 9]
            state = tuple([ref[0, h] for h in range(n_heads)] for ref in (sc, sn, sm))
            f = lambda st, a, b, c, gg, bb, rev=rev, d=d: _ml_chunk(st, a, b, c, gg, bb, rev, d)
            _, vjp = jax.vjp(f, state, q[...], k[...], v[...], g[...], gb[...])
            d_state = tuple([ref[h] for h in range(n_heads)] for ref in scratch[3 * d:3 * d + 3])
            d_out = dh_ref[...] * (nat[d](jj) < n_lat).astype(F32)
            results.append(vjp((d_state, d_out)))
        for d, (d_state, dq, dk, dv, dg, dgb) in enumerate(results):
            dqk_ref, dv_ref, dg_ref, dgb_ref = outs[4 * d:4 * d + 4]
            for part in range(3):
                for h in range(n_heads):
                    scratch[3 * d + part][h] = d_state[part][h]
            dqk_ref[:, 0:w] = dq
            dqk_ref[:, w:2 * w] = dk
            dv_ref[...] = dv.astype(BF16)
            dg_ref[...] = dg
            dgb_ref[...] += dgb

    in_specs, out_specs, out_shape, operands = [], [], [], []
    for d in range(2):
        row = lambda jj, d=d: (nat[d](jj), 0)
        in_specs += [pl.BlockSpec((chunk,w), row), pl.BlockSpec((chunk,w), lambda jj, d=d: (nat[d](jj), 1)),
                     pl.BlockSpec((chunk,w), lambda jj, d=d: (nat[d](jj), 7)),
                     pl.BlockSpec((chunk,LANE), lambda jj, d=d: (nat[d](jj), 10 * w // LANE)),
                     pl.BlockSpec((1, LANE), lambda jj: (0, 0))]
        in_specs += list(_ml_state_specs(n_heads, dh, step))
        in_specs += [pl.BlockSpec((chunk,w), lambda jj, d=d: (jnp.minimum(nat[d](jj), n_lat - 1), 0))]
        operands += [qk, qk, u, u, gate_b, *saved[d], d_h]
        out_specs += [pl.BlockSpec((chunk,2 * w), row), pl.BlockSpec((chunk,w), row),
                      pl.BlockSpec((chunk,LANE), row), pl.BlockSpec((1, LANE), lambda jj: (0, 0))]
        out_shape += [jax.ShapeDtypeStruct((r, 2 * w), F32), jax.ShapeDtypeStruct((r, w), BF16),
                      jax.ShapeDtypeStruct((r, LANE), F32), jax.ShapeDtypeStruct((1, LANE), F32)]
    res, rode = _call(
        body, operands, name="ml_scan_bwd", grid=(n_chunks,), in_specs=in_specs, out_specs=out_specs,
        out_shape=out_shape, scratch_shapes=_ml_state_scratch(n_heads, dh) * 2, sem=("arbitrary",), rider=rider)
    return res[0:4], res[4:8], rode


def _post_specs(w, tm, lat_tiles, cols):
    return [pl.BlockSpec((tm, w), (lambda i, cb=cb: (jnp.minimum(i, lat_tiles - 1), cb))) for cb in cols]


def _post_fwd(o_f, o_b, h_f, h_b, u, wa, wb, t_rows, w, n_hg, n_ml, tm):
    lat_tiles = t_rows // tm

    def body(of, ob, hf, hb, az, bo, bz, wa_ref, wb_ref, y_ref):
        y_ref[...] = _post_fn(of[...], ob[...], az[...], hf[...], hb[...], bo[...], bz[...],
                              wa_ref[...], wb_ref[...], n_hg, n_ml).astype(BF16)

    rows = pl.BlockSpec((tm, w), lambda i: (i, 0))
    vec = pl.BlockSpec((1, w), lambda i: (0, 0))
    return pl.pallas_call(
        body, name="post_fwd", grid=(lat_tiles,),
        in_specs=[rows] * 4 + _post_specs(w, tm, lat_tiles, (4, 8, 9)) + [vec, vec],
        out_specs=pl.BlockSpec((tm, 2 * w), lambda i: (i, 0)),
        out_shape=jax.ShapeDtypeStruct((t_rows, 2 * w), BF16),
        compiler_params=_params(("parallel",)),
    )(o_f, o_b, h_f, h_b, u, u, u, wa, wb)


def _post_bwd(o_f, o_b, h_f, h_b, u, wa, wb, dy, t_rows, w, n_hg, n_ml, tm, rider=None):
    r = u.shape[0]
    lat_tiles = t_rows // tm
    lat = lambda i: (jnp.minimum(i, lat_tiles - 1), 0)

    def body(of, ob, hf, hb, az, bo, bz, wa_ref, wb_ref, dy_ref, do_ref, dh_ref, daz_ref, dbo_ref, dbz_ref,
             dwa_ref, dwb_ref):
        i = pl.program_id(0)

        @pl.when(i == 0)
        def _():
            dwa_ref[...] = jnp.zeros_like(dwa_ref)
            dwb_ref[...] = jnp.zeros_like(dwb_ref)

        @pl.when(i < lat_tiles)
        def _():
            f = functools.partial(_post_fn, n_hg=n_hg, n_ml=n_ml)
            _, vjp = jax.vjp(f, of[...], ob[...], az[...], hf[...], hb[...], bo[...], bz[...], wa_ref[...], wb_ref[...])
            d_of, _, d_az, d_hf, _, d_bo, d_bz, d_wa, d_wb = vjp(dy_ref[...])
            do_ref[...] = d_of
            dh_ref[...] = d_hf
            daz_ref[...] = d_az.astype(BF16)
            dbo_ref[...] = d_bo.astype(BF16)
            dbz_ref[...] = d_bz.astype(BF16)
            dwa_ref[...] += d_wa
            dwb_ref[...] += d_wb

        @pl.when(i >= lat_tiles)
        def _():
            daz_ref[...] = jnp.zeros_like(daz_ref)
            dbo_ref[...] = jnp.zeros_like(dbo_ref)
            dbz_ref[...] = jnp.zeros_like(dbz_ref)

    lat_rows = pl.BlockSpec((tm, w), lat)
    all_rows = pl.BlockSpec((tm, w), lambda i: (i, 0))
    vec = pl.BlockSpec((1, w), lambda i: (0, 0))
    sd_t = jax.ShapeDtypeStruct((t_rows, w), F32)
    sd_r = jax.ShapeDtypeStruct((r, w), BF16)
    sd_v = jax.ShapeDtypeStruct((1, w), F32)
    return _call(
        body, (o_f, o_b, h_f, h_b, u, u, u, wa, wb, dy), name="post_bwd", grid=(r // tm,),
        in_specs=[lat_rows] * 4 + _post_specs(w, tm, lat_tiles, (4, 8, 9)) + [vec, vec]
        + [pl.BlockSpec((tm, 2 * w), lat)],
        out_specs=(lat_rows, lat_rows, all_rows, all_rows, all_rows, vec, vec),
        out_shape=(sd_t, sd_t, sd_r, sd_r, sd_r, sd_v, sd_v), sem=("arbitrary",), rider=rider)


OUT_ROW_GATE, OUT_ROW_LN_G, OUT_ROW_LN_B, OUT_ROW_LOSS = 0, 1, 2, 3


def _out_block(y, w_out, x, target, prm, tm):
    t_rows, dm = x.shape
    di = y.shape[1]

    def body(y_ref, w_ref, x_ref, t_ref, p_ref, dz_ref, dy_ref, gx_ref, acc_ref):
        @pl.when(pl.program_id(0) == 0)
        def _():
            acc_ref[...] = jnp.zeros_like(acc_ref)

        gate, ln_g, ln_b = p_ref[0:1, :], p_ref[1:2, :], p_ref[2:3, :]
        z = _nn(y_ref[...], w_ref[...])
        res = ALPHA * x_ref[...] + gate * z
        mu = jnp.mean(res, axis=-1, keepdims=True)
        rc = res - mu
        rstd = lax.rsqrt(jnp.mean(rc * rc, axis=-1, keepdims=True) + LN_EPS)
        rn = rc * rstd
        err = rn * ln_g + ln_b - t_ref[...]
        d_out = err * (1.0 / dm)
        d_rn = d_out * ln_g
        d_res = rstd * (d_rn - jnp.mean(d_rn, axis=-1, keepdims=True)
                        - rn * jnp.mean(d_rn * rn, axis=-1, keepdims=True))
        acc_ref[OUT_ROW_GATE:OUT_ROW_GATE + 1, :] += jnp.sum(d_res * z, axis=0, keepdims=True)
        acc_ref[OUT_ROW_LN_G:OUT_ROW_LN_G + 1, :] += jnp.sum(d_out * rn, axis=0, keepdims=True)
        acc_ref[OUT_ROW_LN_B:OUT_ROW_LN_B + 1, :] += jnp.sum(d_out, axis=0, keepdims=True)
        acc_ref[OUT_ROW_LOSS:OUT_ROW_LOSS + 1, :] += (0.5 / dm) * jnp.sum(err * err, axis=0, keepdims=True)
        gx_ref[...] = ALPHA * d_res
        dz = (d_res * gate).astype(BF16)
        dz_ref[...] = dz
        dy_ref[...] = _nt(dz, w_ref[...])

    rows_d = pl.BlockSpec((tm, dm), lambda i: (i, 0))
    rows_i = pl.BlockSpec((tm, di), lambda i: (i, 0))
    return pl.pallas_call(
        body, name="out_block", grid=(t_rows // tm,),
        in_specs=[rows_i, pl.BlockSpec((di, dm), lambda i: (0, 0)), rows_d, rows_d,
                  pl.BlockSpec((8, dm), lambda i: (0, 0))],
        out_specs=(rows_d, rows_i, rows_d, pl.BlockSpec((8, dm), lambda i: (0, 0))),
        out_shape=(jax.ShapeDtypeStruct((t_rows, dm), BF16), jax.ShapeDtypeStruct((t_rows, di), F32),
                   jax.ShapeDtypeStruct((t_rows, dm), F32), jax.ShapeDtypeStruct((8, dm), F32)),
        compiler_params=_params(("arbitrary",)),
    )(y, w_out, x, target, prm)


def _mod_fwd(c16, w_mod, tn):
    dm, n = w_mod.shape

    def body(c_ref, w_ref, o_ref, a_ref):
        a = _silu(c_ref[...])
        a_ref[...] = a
        o_ref[...] = _nn(a, w_ref[...], HIGHEST)

    return pl.pallas_call(
        body, name="mod_fwd", grid=(n // tn,),
        in_specs=[pl.BlockSpec((16, dm), lambda j: (0, 0)), pl.BlockSpec((dm, tn), lambda j: (0, j))],
        out_specs=(pl.BlockSpec((16, tn), lambda j: (0, j)), pl.BlockSpec((16, dm), lambda j: (0, 0))),
        out_shape=(jax.ShapeDtypeStruct((16, n), F32), jax.ShapeDtypeStruct((16, dm), F32)),
        compiler_params=_params(("arbitrary",)),
    )(c16, w_mod)


def _mod_bwd(a16, dm16, w_mod, tn, rider=None):
    dm, n = w_mod.shape

    def body(a_ref, d_ref, w_ref, dw_ref, dc_ref):
        @pl.when(pl.program_id(0) == 0)
        def _():
            dc_ref[...] = jnp.zeros_like(dc_ref)
        dw_ref[...] = _tn(a_ref[...], d_ref[...], HIGHEST)
        dc_ref[...] += _nt(d_ref[...], w_ref[...], HIGHEST)

    return _call(
        body, (a16, dm16, w_mod), name="mod_bwd", grid=(n // tn,),
        in_specs=[pl.BlockSpec((16, dm), lambda j: (0, 0)), pl.BlockSpec((16, tn), lambda j: (0, j)),
                  pl.BlockSpec((dm, tn), lambda j: (0, j))],
        out_specs=(pl.BlockSpec((dm, tn), lambda j: (0, j)), pl.BlockSpec((16, dm), lambda j: (0, 0))),
        out_shape=(jax.ShapeDtypeStruct((dm, n), F32), jax.ShapeDtypeStruct((16, dm), F32)),
        sem=("arbitrary",), rider=rider)


def _sum_devices(g, fold_rows):
    n_dev, rows, n = g.shape

    def body(g_ref, s_ref, t_ref):
        s = g_ref[0]
        for dev in range(1, n_dev):
            s = s + g_ref[dev]
        t_ref[...] = jnp.broadcast_to(jnp.sum(s, axis=-1, keepdims=True), (rows, LANE))
        s_ref[...] = s
        s_ref[0:fold_rows, :] = s[0:fold_rows] + s[fold_rows:2 * fold_rows]

    return pl.pallas_call(
        body, name="sum_devices",
        out_shape=(jax.ShapeDtypeStruct((rows, n), F32), jax.ShapeDtypeStruct((rows, LANE), F32)),
        compiler_params=_params(),
    )(g)


def _c_ctx_grad(parts, c_ctx_row):
    def body(p_ref, c_ref, o_ref):
        s = p_ref[0]
        for chip in range(1, N_CHIPS):
            s = s + p_ref[2 * chip]
        cv = c_ref[...]
        sg = _sigmoid(cv)
        o_ref[...] = s * (sg * (1.0 + cv * (1.0 - sg)))

    return pl.pallas_call(
        body, name="c_ctx_grad", out_shape=jax.ShapeDtypeStruct(parts.shape[1:], F32), compiler_params=_params(),
    )(parts, c_ctx_row)


def _sum_pair(name, mine, got):
    def body(a_ref, b_ref, o_ref):
        o_ref[...] = (a_ref[...] + b_ref[...]).astype(BF16)

    k, rows, n = mine.shape
    tl = _largest_divisor(n, max(LANE, (1 << 18) // rows), LANE)
    spec = pl.BlockSpec((1, rows, tl), lambda kk, i: (kk, 0, i))
    return pl.pallas_call(
        body, name=name, grid=(k, n // tl), in_specs=[spec, spec], out_specs=spec,
        out_shape=jax.ShapeDtypeStruct(mine.shape, BF16), compiler_params=_params(("parallel", "parallel")),
    )(mine, got)


def _sum_pair_lanes(name, full, got, ci):
    rows, n = got.shape
    tr = _largest_divisor(rows, max(SUBLANE_BF16, (1 << 19) // n), SUBLANE_BF16)

    def body(ci_ref, a_ref, b_ref, o_ref):
        o_ref[...] = (a_ref[...] + b_ref[...].astype(F32)).astype(BF16)

    return pl.pallas_call(
        body, name=name,
        grid_spec=pltpu.PrefetchScalarGridSpec(
            num_scalar_prefetch=1, grid=(rows // tr,),
            in_specs=[pl.BlockSpec((tr, n), lambda i, c: (i, c[0])), pl.BlockSpec((tr, n), lambda i, c: (i, 0))],
            out_specs=pl.BlockSpec((tr, n), lambda i, c: (i, 0))),
        out_shape=jax.ShapeDtypeStruct((rows, n), BF16), compiler_params=_params(("parallel",)),
    )(ci.reshape(1).astype(jnp.int32), full, got)


def _sum_chips(name, got, own, chip):
    k, rows, n = got.shape
    tl = _largest_divisor(n, max(LANE, (1 << 18) // rows), LANE)

    def body(chip_ref, g_ref, own_ref, o_ref):
        total = None
        for kk in range(k):
            term = jnp.where(chip_ref[0] == kk, own_ref[0], g_ref[kk]).astype(F32)
            total = term if total is None else total + term
        o_ref[...] = total

    return pl.pallas_call(
        body, name=name,
        grid_spec=pltpu.PrefetchScalarGridSpec(
            num_scalar_prefetch=1, grid=(n // tl,),
            in_specs=[pl.BlockSpec((k, rows, tl), lambda i, c: (0, 0, i)),
                      pl.BlockSpec((1, rows, tl), lambda i, c: (c[0], 0, i))],
            out_specs=pl.BlockSpec((rows, tl), lambda i, c: (0, i))),
        out_shape=jax.ShapeDtypeStruct((rows, n), F32), compiler_params=_params(("parallel",)),
    )(chip.reshape(1).astype(jnp.int32), got, own)


def _adamw_update(w, g, m, v):
    m2 = ADAM_B1 * m + (1.0 - ADAM_B1) * g
    v2 = ADAM_B2 * v + (1.0 - ADAM_B2) * jnp.square(g)
    m_hat = m2 / (1.0 - ADAM_B1 ** ADAM_STEP)
    v_hat = v2 / (1.0 - ADAM_B2 ** ADAM_STEP)
    return -ADAM_LR * (m_hat / (jnp.sqrt(v_hat) + ADAM_EPS) + ADAM_WD * w), m2, v2


def _adamw(name, w, g, m, v, rider=None):
    rows, n = w.shape
    if rows % 8 == 0:
        tr = _largest_divisor(rows, max(8, (1 << 18) // n), 8)
        block, index, steps = (tr, n), (lambda i: (i, 0)), rows // tr
    else:
        tl = _largest_divisor(n, max(LANE, (1 << 18) // rows), LANE)
        block, index, steps = (rows, tl), (lambda i: (0, i)), n // tl

    def body(w_ref, g_ref, m_ref, v_ref, d_ref, mo_ref, vo_ref):
        d_ref[...], mo_ref[...], vo_ref[...] = _adamw_update(w_ref[...], g_ref[...], m_ref[...], v_ref[...])

    spec = pl.BlockSpec(block, index)
    sds = jax.ShapeDtypeStruct((rows, n), F32)
    return _call(body, (w, g, m, v), name=name, grid=(steps,), in_specs=[spec] * 4, out_specs=(spec,) * 3,
                 out_shape=(sds, sds, sds), sem=("parallel",), rider=rider)


PACK_LANES = 1024


def _pack(pieces):
    flat = jnp.concatenate([p.reshape(-1) for p in pieces])
    total = -(-flat.shape[0] // (8 * PACK_LANES)) * 8 * PACK_LANES
    return jnp.pad(flat, (0, total - flat.shape[0])).reshape(-1, PACK_LANES)


def _unpack(packed, shapes):
    flat = packed.reshape(-1)
    out, off = [], 0
    for shp in shapes:
        size = math.prod(shp)
        out.append(flat[off:off + size].reshape(shp))
        off += size
    return out


def _rows8(rows, width):
    flat = [r.reshape(width) for r in rows] + [jnp.zeros(((8 - len(rows)) * width,), F32)]
    return jnp.concatenate(flat).reshape(8, width)


def kernel(x, c, ctx, c_ctx, w_mod, b_mod, w_in, conv_w, conv_b, hg_lb, ml_gate_b, hg_norm_w, ml_norm_w, w_out, ln_g, ln_b, loss_target, m_c_ctx, m_w_mod, m_b_mod, m_w_in, m_conv_w, m_conv_b, m_hg_lb, m_ml_gate_b, m_hg_norm_w, m_ml_norm_w, m_w_out, m_ln_g, m_ln_b, v_c_ctx, v_w_mod, v_b_mod, v_w_in, v_conv_w, v_conv_b, v_hg_lb, v_ml_gate_b, v_hg_norm_w, v_ml_norm_w, v_w_out, v_ln_g, v_ln_b):
    t_rows, dm = x.shape[1], x.shape[2]
    c_rows = ctx.shape[1]
    w = hg_norm_w.shape[1]
    n_ml = ml_gate_b.shape[-1]
    n_hg = w // HG_DK
    di = 2 * w
    n_in = 10 * w + 4 * n_ml
    ns = w_in.shape[2]
    nm = w_mod.shape[2]
    n_pad = 10 * w + LANE
    r_rows = t_rows + c_rows
    row_gcd = math.gcd(t_rows, c_rows)
    hg_chunk, ml_chunk = math.gcd(HG_CHUNK, row_gcd), math.gcd(ML_CHUNK, row_gcd)
    hg_counts = (t_rows // hg_chunk, c_rows // hg_chunk, hg_chunk)
    ml_counts = (t_rows // ml_chunk, c_rows // ml_chunk, ml_chunk)
    assert ml_norm_w.shape[1] == w and di == dm and N_CHIPS * ns == n_in and N_CHIPS * nm == 3 * dm
    assert w_out.shape[1] * N_CHIPS == di and 4 * n_ml <= LANE and t_rows % GRID_W == 0

    xi, yi, ci = lax.axis_index("x"), lax.axis_index("y"), lax.axis_index("c")
    chip = 2 * xi + yi
    dev = 4 * xi + 2 * yi + ci

    tm = _largest_divisor(math.gcd(t_rows, c_rows), 256, 8)
    tm_mm = _largest_divisor(r_rows, 1088, SUBLANE_BF16)
    tn_mm = LANE * _largest_divisor(n_pad // LANE, 9)
    tn_mod = _largest_divisor(nm, 512, LANE)

    as_t = lambda a: jnp.transpose(a[0])
    half_in = lax.dynamic_slice_in_dim(as_t(w_in).astype(BF16), ci * (dm // 2), dm // 2, 1)
    half_out = lax.dynamic_slice_in_dim(w_out[0].astype(BF16), ci * (di // (2 * N_CHIPS)), di // (2 * N_CHIPS), 0)
    lanes_of = lambda core: pl.ds(core * (dm // 2), dm // 2)
    landing = lambda s, r: (_chip_of(s), slice(None), lanes_of(s[2]))
    whole = jax.ShapeDtypeStruct((N_CHIPS, ns, dm), BF16)

    shard_shapes = [(dm,), (2, 2, w // N_CHIPS), (3, 3, di // N_CHIPS)]
    small_in = _all_gather8(_pack([c, hg_lb, conv_w]))
    g1, gw_in = _Exchange(small_in.srcs + [half_in], small_in.out_shapes + [whole],
                          small_in.transfers + [(mask, 1, None, 1, landing) for mask in CHIP_MASKS[:2]],
                          small_in.local_copies).run("gather_inputs")
    per_dev = [_unpack(g1[i], shard_shapes) for i in range(N_DEV)]
    c_all = jnp.stack([p[0] for p in per_dev])
    lb_full = jnp.concatenate([per_dev[2 * k][1] for k in range(N_CHIPS)], axis=-1)
    conv_w9 = jnp.concatenate([per_dev[2 * k][2] for k in range(N_CHIPS)], axis=-1).reshape(9, di)

    c16 = jnp.concatenate([c_all, c_ctx[None], jnp.zeros((16 - N_DEV - 1, dm), F32)])
    mod_part, a16 = _mod_fwd(c16, w_mod[0], tn_mod)
    g2 = _all_gather8(mod_part).run("gather_mod")[0]
    mod_all = jnp.concatenate([g2[2 * k] for k in range(N_CHIPS)], axis=1) + b_mod
    mod_x = lax.dynamic_index_in_dim(mod_all, dev, 0, keepdims=False).reshape(3, dm)
    mod_c = mod_all[N_DEV].reshape(3, dm)
    prm = jnp.stack([_rows8(list(mod_x), dm), _rows8(list(mod_c), dm)])

    chip_x = lambda s: 2 * (1 - s[0]) + s[1]
    chip_y = lambda s: 2 * s[0] + 1 - s[1]
    chip_xy = lambda s: 2 * (1 - s[0]) + 1 - s[1]
    quarter = lambda core, q: pl.ds(core * (dm // 2) + q * (dm // 4), dm // 4)
    half_of = lambda which: (lambda s, r: (which(s), slice(None), lanes_of(s[2])))
    relay_x = lambda s, r: (chip_y(s), slice(None), quarter(s[2], 0))
    relay_y = lambda s, r: (chip_x(s), slice(None), quarter(s[2], 1))
    relay_in = _Exchange([gw_in], [whole],
                         [((1, 0, 0), 0, relay_x, 0, relay_x), ((0, 1, 0), 0, relay_y, 0, relay_y)]
                         + [(SIBLING_MASK, 0, half_of(which), 0, half_of(which)) for which in (chip_x, chip_y)],
                         in_place={0: 0})
    hc, (gw_in,) = _modulate_fwd(x[0], ctx[0], prm, tm, rider=relay_in)
    gw_in = _Exchange([gw_in], [whole], [(SIBLING_MASK, 0, half_of(chip_xy), 0, half_of(chip_xy))],
                      in_place={0: 0}).run("gather_w_in_pair")[0]
    wt_full = jnp.concatenate([_own_block(chip, as_t(w_in).astype(BF16), gw_in).reshape(N_CHIPS * ns, dm),
                               jnp.zeros((n_pad - N_CHIPS * ns, dm), BF16)])
    u, (got_out,) = _mm_nt("in_proj", hc, wt_full, n_pad, tm_mm, tn_mm, F32, rider=_all_gather_chips([half_out]))
    fetched_out = _own_block(chip, half_out, got_out)
    (o_f, o_b), hg_saved, (swapped_out,) = _hg_scan_fwd(u, lb_full, w, *hg_counts,
                                                         rider=_sibling_swap([fetched_out]))
    w_out_full = _join_halves(ci, fetched_out, swapped_out, 1).reshape(di, dm)
    qk = _conv_fwd(u, conv_w9, conv_b, t_rows, c_rows, w, LANE)
    gate_b_row = jnp.pad(ml_gate_b.reshape(1, -1), ((0, 0), (0, LANE - 4 * n_ml)))
    (h_f, h_b), ml_saved = _ml_scan_fwd(qk, u, gate_b_row, w, n_ml, *ml_counts)
    y = _post_fwd(o_f, o_b, h_f, h_b, u, hg_norm_w, ml_norm_w, t_rows, w, n_hg, n_ml, tm)
    prm_out = _rows8([mod_x[2], ln_g, ln_b], dm)
    dz, dy, gx_direct, acc_out = _out_block(y, w_out_full, x[0], loss_target[0], prm_out, tm)

    d_w_out = _mm_tn("d_w_out", y, dz, _largest_divisor(di, 1024, LANE),
                     _largest_divisor(t_rows, 1024, SUBLANE_BF16))
    d_w_out4 = d_w_out.reshape(N_CHIPS, 2, di // (2 * N_CHIPS), dm)
    mine_out = lax.dynamic_index_in_dim(d_w_out4, ci, 1, keepdims=False)
    other_out = lax.dynamic_index_in_dim(d_w_out4, 1 - ci, 1, keepdims=False)
    (d_o, d_h, d_az, d_bo, d_bz, d_wa, d_wb), (got_out,) = _post_bwd(
        o_f, o_b, h_f, h_b, u, hg_norm_w, ml_norm_w, dy, t_rows, w, n_hg, n_ml, tm, rider=_sibling_swap([other_out]))
    pair_out = _sum_pair("rs_pair_sum_w_out", mine_out, got_out)
    (d_aq_f, d_aff, d_ai_f, d_lb_f), (d_aq_b, d_afb, d_ai_b, d_lb_b), (landed_out,) = _hg_scan_bwd(
        u, lb_full, hg_saved, d_o, w, *hg_counts, rider=_chip_scatter([pair_out]))
    half_g_out = _sum_chips("rs_chip_sum_w_out", landed_out, pair_out, chip)
    (d_qk_f, d_v_f, d_g_f, d_gb_f), (d_qk_b, d_v_b, d_g_b, d_gb_b), (sibling_out,) = _ml_scan_bwd(
        qk, u, gate_b_row, ml_saved, d_h, w, n_ml, *ml_counts, rider=_sibling_swap([half_g_out]))
    g_w_out = _join_halves(ci, half_g_out, sibling_out, 0)
    d_bqk, d_cw, d_cb = _conv_bwd(u, (d_qk_f, d_qk_b), conv_w9, conv_b, t_rows, c_rows, w, LANE)
    du = _assemble_du([(d_aq_f, d_aq_b), d_aff, d_afb, (d_ai_f, d_ai_b), d_az, d_bqk, (d_v_f, d_v_b), d_bo, d_bz],
                      (d_g_f, d_g_b), n_pad, tm)
    d_wt_in, d_wt_in_bf16 = _mm_tn("d_w_in", du, hc, tn_mm, tm_mm, with_bf16=True)

    delta, new_m, new_v = {}, {}, {}
    res, (got_in,) = _adamw(
        "adamw_w_out", w_out[0], g_w_out, m_w_out[0], v_w_out[0],
        rider=_Exchange([d_wt_in_bf16], [jax.ShapeDtypeStruct((n_pad, dm // 2), BF16)],
                        [(SIBLING_MASK, 0, lambda s, r: (slice(None), lanes_of(r[2])), 0, None)]))
    delta["w_out"], new_m["w_out"], new_v["w_out"] = (a[None] for a in res)
    pair_half = _sum_pair_lanes("rs_pair_sum_w_in", d_wt_in, got_in, ci)
    pair_in = jnp.stack([pair_half[k * ns:(k + 1) * ns] for k in range(N_CHIPS)])
    d_hc, (landed_in,) = _mm_acc("d_h", du, wt_full, tm_mm, tn_mm, rider=_chip_scatter([pair_in]))
    half_g_in = _sum_chips("rs_chip_sum_w_in", landed_in, pair_in, chip)
    (gx, acc_mod), _ = _modulate_bwd(x[0], ctx[0], d_hc, prm, gx_direct, tm)
    grad_x = gx[None]

    zero_row = jnp.zeros((dm,), F32)
    d_gb = jnp.concatenate([d_gb_f[:, 0:n_ml], d_gb_b[:, n_ml:2 * n_ml], d_gb_f[:, 2 * n_ml:3 * n_ml],
                            d_gb_b[:, 3 * n_ml:4 * n_ml], jnp.zeros((1, dm - 4 * n_ml), F32)], axis=1)
    rows = [acc_mod[0, 0], acc_mod[0, 1], acc_out[OUT_ROW_GATE],
            acc_mod[1, 0], acc_mod[1, 1], zero_row]
    rows += list(d_cw) + [d_cb[0], d_lb_f.reshape(dm), d_lb_b.reshape(dm),
                          jnp.concatenate([d_wa[0], d_wb[0]]), acc_out[OUT_ROW_LN_G], acc_out[OUT_ROW_LN_B],
                          acc_out[OUT_ROW_LOSS], d_gb[0], zero_row]
    ROW_CW, ROW_CB, ROW_LB, ROW_NORM, ROW_LN_G, ROW_LN_B, ROW_LOSS, ROW_GB = 6, 15, 16, 18, 19, 20, 21, 22
    small_rows = jnp.concatenate([r.reshape(dm) for r in rows]).reshape(len(rows), dm)
    g3 = _all_gather8(small_rows).run("gather_small_grads")[0]
    sums, totals = _sum_devices(g3, 3)
    loss = totals[ROW_LOSS, 0]
    dm16 = jnp.concatenate([g3[:, 0:3, :].reshape(N_DEV, 3 * dm), sums[3:6].reshape(1, 3 * dm),
                            jnp.zeros((16 - N_DEV - 1, 3 * dm), F32)])
    (g_w_mod, dc16), (sibling_g_in,) = _mod_bwd(a16, lax.dynamic_slice_in_dim(dm16, chip * nm, nm, 1), w_mod[0],
                                                tn_mod, rider=_sibling_swap([half_g_in]))
    g_wt_in = _join_halves(ci, half_g_in, sibling_g_in, 1)
    g4 = _all_gather8(jnp.pad(dc16[N_DEV:N_DEV + 1], ((0, 7), (0, 0)))).run("gather_c_ctx")[0]
    g_c_ctx = _c_ctx_grad(g4, jnp.broadcast_to(c_ctx[None], (8, dm)))[0]
    res, _ = _adamw("adamw_w_in", as_t(w_in), g_wt_in, as_t(m_w_in), as_t(v_w_in))
    delta["w_in"], new_m["w_in"], new_v["w_in"] = (jnp.transpose(a)[None] for a in res)
    res, _ = _adamw("adamw_w_mod", w_mod[0], g_w_mod, m_w_mod[0], v_w_mod[0])
    delta["w_mod"], new_m["w_mod"], new_v["w_mod"] = (a[None] for a in res)

    chip_cols = lambda a, width: lax.dynamic_slice_in_dim(a, chip * width, width, a.ndim - 1)
    grads = {
        "c_ctx": g_c_ctx,
        "w_mod": g_w_mod[None],
        "b_mod": sums[0:3].reshape(1, 3 * dm),
        "w_in": jnp.transpose(g_wt_in)[None],
        "conv_w": chip_cols(sums[ROW_CW:ROW_CW + 9].reshape(1, 3, 3, di), di // N_CHIPS),
        "conv_b": sums[ROW_CB][None],
        "hg_lb": chip_cols(sums[ROW_LB:ROW_LB + 2].reshape(2, 2, w), w // N_CHIPS),
        "ml_gate_b": sums[ROW_GB, 0:4 * n_ml].reshape(1, 4, n_ml),
        "hg_norm_w": sums[ROW_NORM, 0:w][None],
        "ml_norm_w": sums[ROW_NORM, w:2 * w][None],
        "w_out": g_w_out[None],
        "ln_g": sums[ROW_LN_G][None],
        "ln_b": sums[ROW_LN_B][None],
    }
    weights = dict(c_ctx=c_ctx, w_mod=w_mod, b_mod=b_mod, w_in=w_in, conv_w=conv_w, conv_b=conv_b, hg_lb=hg_lb,
                   ml_gate_b=ml_gate_b, hg_norm_w=hg_norm_w, ml_norm_w=ml_norm_w, w_out=w_out, ln_g=ln_g, ln_b=ln_b)
    mom1 = dict(c_ctx=m_c_ctx, w_mod=m_w_mod, b_mod=m_b_mod, w_in=m_w_in, conv_w=m_conv_w, conv_b=m_conv_b,
                hg_lb=m_hg_lb, ml_gate_b=m_ml_gate_b, hg_norm_w=m_hg_norm_w, ml_norm_w=m_ml_norm_w, w_out=m_w_out,
                ln_g=m_ln_g, ln_b=m_ln_b)
    mom2 = dict(c_ctx=v_c_ctx, w_mod=v_w_mod, b_mod=v_b_mod, w_in=v_w_in, conv_w=v_conv_w, conv_b=v_conv_b,
                hg_lb=v_hg_lb, ml_gate_b=v_ml_gate_b, hg_norm_w=v_hg_norm_w, ml_norm_w=v_ml_norm_w, w_out=v_w_out,
                ln_g=v_ln_g, ln_b=v_ln_b)
    names = list(weights)
    big = ("w_mod", "w_in", "w_out")
    small = [n for n in names if n not in big]

    small_shapes = [weights[n].shape for n in small]
    res, _ = _adamw("adamw_small", *(_pack([src[n] for n in small]) for src in (weights, grads, mom1, mom2)))
    for out, packed in zip((delta, new_m, new_v), res):
        for n, a in zip(small, _unpack(packed, small_shapes)):
            out[n] = a

    return (loss, grad_x, *[grads[n].reshape(weights[n].shape) for n in names], *[delta[n] for n in names],
            *[new_m[n] for n in names], *[new_v[n] for n in names])
```

```python
import functools
import math

import jax
import jax.numpy as jnp
from jax import lax
from jax.experimental import pallas as pl
from jax.experimental.pallas import tpu as pltpu

F32 = jnp.float32
BF16 = jnp.bfloat16
HIGHEST = lax.Precision.HIGHEST
MESH = pl.DeviceIdType.MESH

HG_CHUNK = 64
ML_CHUNK = 256
HG_CHUNKS_PER_STEP = 4
GRID_W = 64
HG_DK = 128
LANE = 128
SUBLANE_BF16 = 16
ALPHA = 2.0 ** 0.25
LN_EPS = 1e-5
NORM_EPS = 1e-6
ADAM_LR = 0.001
ADAM_B1 = 0.9
ADAM_B2 = 0.999
ADAM_EPS = 1e-08
ADAM_WD = 0.01
ADAM_STEP = 10
VMEM_LIMIT = 56 * 1024 * 1024
N_CHIPS = 4
N_DEV = 8


def _params(sem=None):
    return pltpu.CompilerParams(dimension_semantics=sem, vmem_limit_bytes=VMEM_LIMIT)


def _largest_divisor(n, cap, multiple=1):
    best = None
    for d in range(multiple, min(n, cap) + 1, multiple):
        if n % d == 0:
            best = d
    assert best is not None, (n, cap, multiple)
    return best


def _sigmoid(x):
    return jax.nn.sigmoid(x)


def _silu(x):
    return x * jax.nn.sigmoid(x)


def _dot(a, b, dims, precision=None):
    return lax.dot_general(a, b, (dims, ((), ())), precision=precision, preferred_element_type=F32)


def _nn(a, b, precision=None):
    return _dot(a, b, ((1,), (0,)), precision)


def _nt(a, b, precision=None):
    return _dot(a, b, ((1,), (1,)), precision)


def _tn(a, b, precision=None):
    return _dot(a, b, ((0,), (0,)), precision)


def _narrow(x):
    return x.astype(BF16)


@jax.custom_vjp
def _bnn(a, b):
    return _nn(_narrow(a), _narrow(b))


def _bnn_fwd(a, b):
    an, bn = _narrow(a), _narrow(b)
    return _nn(an, bn), (an, bn)


def _bnn_bwd(res, ct):
    an, bn = res
    ctn = _narrow(ct)
    return _nt(ctn, bn), _tn(an, ctn)


_bnn.defvjp(_bnn_fwd, _bnn_bwd)


@jax.custom_vjp
def _bnt(a, b):
    return _nt(_narrow(a), _narrow(b))


def _bnt_fwd(a, b):
    an, bn = _narrow(a), _narrow(b)
    return _nt(an, bn), (an, bn)


def _bnt_bwd(res, ct):
    an, bn = res
    ctn = _narrow(ct)
    return _nn(ctn, bn), _tn(ctn, an)


_bnt.defvjp(_bnt_fwd, _bnt_bwd)


@jax.custom_vjp
def _btn(a, b):
    return _tn(_narrow(a), _narrow(b))


def _btn_fwd(a, b):
    an, bn = _narrow(a), _narrow(b)
    return _tn(an, bn), (an, bn)


def _btn_bwd(res, ct):
    an, bn = res
    ctn = _narrow(ct)
    return _nt(bn, ctn), _nn(an, ctn)


_btn.defvjp(_btn_fwd, _btn_bwd)


def _visible(n, rev):
    r = lax.broadcasted_iota(jnp.int32, (n, n), 0)
    c = lax.broadcasted_iota(jnp.int32, (n, n), 1)
    return (r <= c) if rev else (r >= c)


def _mask_matmul(mask, x):
    mb = mask.astype(BF16)
    hi = x.astype(BF16)
    lo = (x - hi.astype(F32)).astype(BF16)
    return _nn(mb, hi) + _nn(mb, lo)


@functools.partial(jax.custom_vjp, nondiff_argnums=(1,))
def _cumulative(x, rev):
    return _mask_matmul(_visible(x.shape[0], rev), x)


def _cumulative_fwd(x, rev):
    return _cumulative(x, rev), None


def _cumulative_bwd(rev, _, ct):
    return (_mask_matmul(_visible(ct.shape[0], not rev), ct),)


_cumulative.defvjp(_cumulative_fwd, _cumulative_bwd)


def _hg_chunk(states, aq, af, ai, lb0, lb1, rev):
    n_heads = len(states)
    lb = _sigmoid(lb0 - lb1)
    f = lb + (1.0 - lb) * _sigmoid(af)
    g = jnp.log(f)
    k = 1.0 - f
    q = _silu(aq)
    chunk = aq.shape[0]
    vis = _visible(chunk, rev)
    b = _cumulative(g, rev)
    last = 0 if rev else chunk - 1
    b_end = b[last:last + 1]
    b_mid = b[chunk // 2:chunk // 2 + 1]
    q_inter = q * jnp.exp(b)
    q_intra = q * jnp.exp(b - b_mid)
    k_intra = k * jnp.exp(b_mid - b)
    k_dec = k * jnp.exp(b_end - b)
    e_end = jnp.exp(b_end)
    new_states, outs = [], []
    for h in range(n_heads):
        sl = slice(h * HG_DK, (h + 1) * HG_DK)
        s_t = states[h]
        scores = jnp.where(vis, _nt(q_intra[:, sl], k_intra[:, sl]), 0.0)
        outs.append(_nt(q_inter[:, sl], s_t) + _nn(scores, ai[:, sl]))
        new_states.append(e_end[:, sl] * s_t + _tn(ai[:, sl], k_dec[:, sl]))
    return new_states, jnp.concatenate(outs, axis=1)


def _ml_chunk(state, q, k, v, g, gb, rev, d):
    cms, nvs, mbs = state
    n_heads = len(cms)
    dh = q.shape[1] // n_heads
    ga = g + gb
    log_f_all = jax.nn.log_sigmoid(ga)
    chunk = q.shape[0]
    vis = _visible(chunk, rev)
    b_all = _cumulative(log_f_all, rev)
    last = 0 if rev else chunk - 1
    k = k * (dh ** -0.5)
    new_c, new_n, new_m, outs = [], [], [], []
    for h in range(n_heads):
        ci = d * n_heads + h
        cf = (2 + d) * n_heads + h
        sl = slice(h * dh, (h + 1) * dh)
        qh, kh, vh = q[:, sl], k[:, sl], v[:, sl]
        li = ga[:, ci:ci + 1]
        b = b_all[:, cf:cf + 1]
        m = mbs[h][:, 0:1]
        row = jnp.transpose(li - b)
        log_w = jnp.where(vis, b + row, -jnp.inf)
        m_inter = b + m
        m_t = jnp.maximum(m_inter, jnp.max(log_w, axis=-1, keepdims=True))
        w_inter = jnp.exp(m_inter - m_t)
        w_qk = jnp.exp(log_w - m_t) * _bnt(qh, kh)
        num = w_inter * _bnt(qh, cms[h]) + _bnn(w_qk, vh)
        den = w_inter * jnp.sum(qh * nvs[h], axis=-1, keepdims=True) + jnp.sum(w_qk, axis=-1, keepdims=True)
        outs.append(num / jnp.maximum(jnp.abs(den), jnp.exp(-m_t)))
        m_new = m_t[last:last + 1]
        b_end = b[last:last + 1]
        w_s = jnp.exp(b_end - b + li - m_new)
        decay = jnp.exp(b_end + m - m_new)
        new_c.append(decay * cms[h] + _btn(w_s * vh, kh))
        new_n.append(decay * nvs[h] + jnp.sum(w_s * kh, axis=0, keepdims=True))
        new_m.append(jnp.broadcast_to(m_new, (1, LANE)))
    return (new_c, new_n, new_m), jnp.concatenate(outs, axis=1)


def _post_fn(o_f, o_b, az, h_f, h_b, bo, bz, wa, wb, n_hg, n_ml):
    o = o_f + o_b
    parts = []
    for h in range(n_hg):
        s = o[:, h * HG_DK:(h + 1) * HG_DK]
        parts.append(s * lax.rsqrt(jnp.mean(s * s, axis=-1, keepdims=True) + NORM_EPS))
    y_a = jnp.concatenate(parts, axis=1) * wa * _silu(az)
    hh = h_f + h_b
    dh = hh.shape[1] // n_ml
    parts = []
    for h in range(n_ml):
        s = hh[:, h * dh:(h + 1) * dh]
        mu = jnp.mean(s, axis=-1, keepdims=True)
        sc = s - mu
        parts.append(sc * lax.rsqrt(jnp.mean(sc * sc, axis=-1, keepdims=True) + NORM_EPS))
    y_b = jnp.concatenate(parts, axis=1) * wb * _sigmoid(bo) * _silu(bz)
    return jnp.concatenate([y_a, y_b], axis=1)


def _chip_of(dev):
    return 2 * dev[0] + dev[1]


def _index_of(dev):
    return 4 * dev[0] + 2 * dev[1] + dev[2]


class _Exchange:
    def __init__(self, srcs, out_shapes, transfers, local_copies=(), in_place=None):
        self.srcs, self.out_shapes = list(srcs), list(out_shapes)
        self.transfers, self.local_copies = list(transfers), list(local_copies)
        self.in_place = dict(in_place or {})

    def scratch(self):
        return [pltpu.SemaphoreType.DMA((len(self.transfers),)), pltpu.SemaphoreType.DMA((len(self.transfers),)),
                pltpu.SemaphoreType.DMA((max(len(self.local_copies), 1),))]

    def copies(self, ins, outs, send_sems, recv_sems, local_sems):
        me = (lax.axis_index("x"), lax.axis_index("y"), lax.axis_index("c"))

        def pick(ref, fn, *who):
            return ref if fn is None else ref.at[fn(*who)]

        sends, recvs, locs = [], [], []
        for t, (mask, si, sfn, di, dfn) in enumerate(self.transfers):
            peer = tuple(1 - p if flip else p for p, flip in zip(me, mask))
            sends.append(pltpu.make_async_remote_copy(
                src_ref=pick(ins[si], sfn, me, peer), dst_ref=pick(outs[di], dfn, me, peer),
                send_sem=send_sems.at[t], recv_sem=recv_sems.at[t], device_id=peer, device_id_type=MESH))
            landing = pick(outs[di], dfn, peer, me)
            recvs.append(pltpu.make_async_remote_copy(
                src_ref=landing, dst_ref=landing,
                send_sem=send_sems.at[t], recv_sem=recv_sems.at[t], device_id=peer, device_id_type=MESH))
        for l, (si, sfn, di, dfn) in enumerate(self.local_copies):
            locs.append(pltpu.make_async_copy(pick(ins[si], sfn, me), pick(outs[di], dfn, me), local_sems.at[l]))

        def start():
            for cp in locs + sends:
                cp.start()

        def wait():
            for cp in recvs:
                cp.wait_recv()
            for cp in sends:
                cp.wait_send()
            for cp in locs:
                cp.wait()

        return start, wait

    def run(self, name):
        n_in, n_out = len(self.srcs), len(self.out_shapes)

        def body(*refs):
            start, wait = self.copies(refs[:n_in], refs[n_in:n_in + n_out], *refs[n_in + n_out:])
            start()
            wait()

        hbm = pl.BlockSpec(memory_space=pltpu.HBM)
        return pl.pallas_call(
            body, name=name, out_shape=tuple(self.out_shapes), in_specs=[hbm] * n_in,
            out_specs=tuple([hbm] * n_out), scratch_shapes=self.scratch(), input_output_aliases=self.in_place,
        )(*self.srcs)


def _call(body, operands, *, name, grid, in_specs, out_specs, out_shape, scratch_shapes=(), sem=None, rider=None):
    out_specs, out_shape, scratch_shapes = list(out_specs), list(out_shape), list(scratch_shapes)
    if rider is None:
        res = pl.pallas_call(
            body, name=name, grid=grid, in_specs=list(in_specs), out_specs=tuple(out_specs),
            out_shape=tuple(out_shape), scratch_shapes=scratch_shapes, compiler_params=_params(sem),
        )(*operands)
        return list(res), []
    counts = (len(in_specs), len(rider.srcs), len(out_specs), len(rider.out_shapes), len(scratch_shapes), 3)

    def full(*refs):
        groups, pos = [], 0
        for k in counts:
            groups.append(refs[pos:pos + k])
            pos += k
        own_in, ex_in, own_out, ex_out, own_scr, ex_scr = groups
        ids = [pl.program_id(a) for a in range(len(grid))]
        first = functools.reduce(jnp.logical_and, [i == 0 for i in ids])
        last = functools.reduce(jnp.logical_and, [i == g - 1 for i, g in zip(ids, grid)])
        start, wait = rider.copies(ex_in, ex_out, *ex_scr)
        pl.when(first)(start)
        body(*own_in, *own_out, *own_scr)
        pl.when(last)(wait)

    hbm = pl.BlockSpec(memory_space=pltpu.HBM)
    res = pl.pallas_call(
        full, name=name, grid=grid, in_specs=list(in_specs) + [hbm] * counts[1],
        out_specs=tuple(out_specs + [hbm] * counts[3]), out_shape=tuple(out_shape + rider.out_shapes),
        scratch_shapes=scratch_shapes + rider.scratch(), compiler_params=_params(("arbitrary",) * len(grid)),
        input_output_aliases={counts[0] + i: counts[2] + o for i, o in rider.in_place.items()},
    )(*operands, *rider.srcs)
    return list(res[:counts[2]]), list(res[counts[2]:])


ALL_MASKS = [(mx, my, mc) for mx in (0, 1) for my in (0, 1) for mc in (0, 1)][1:]
CHIP_MASKS = [(1, 0, 0), (0, 1, 0), (1, 1, 0)]
SIBLING_MASK = (0, 0, 1)


def _all_gather8(v):
    out = jax.ShapeDtypeStruct((N_DEV,) + v.shape, v.dtype)
    slot = lambda sender, receiver: _index_of(sender)
    transfers = [(mask, 0, None, 0, slot) for mask in ALL_MASKS]
    return _Exchange([v], [out], transfers, [(0, None, 0, lambda me: _index_of(me))])


def _all_gather_chips(arrays):
    outs = [jax.ShapeDtypeStruct((N_CHIPS,) + a.shape, a.dtype) for a in arrays]
    slot = lambda sender, receiver: _chip_of(sender)
    return _Exchange(arrays, outs, [(mask, i, None, i, slot) for i in range(len(arrays)) for mask in CHIP_MASKS])


def _sibling_swap(arrays):
    outs = [jax.ShapeDtypeStruct(a.shape, a.dtype) for a in arrays]
    return _Exchange(arrays, outs, [(SIBLING_MASK, i, None, i, None) for i in range(len(arrays))])


def _chip_scatter(arrays):
    outs = [jax.ShapeDtypeStruct(a.shape, a.dtype) for a in arrays]
    transfers = [(mask, i, lambda s, r: _chip_of(r), i, lambda s, r: _chip_of(s))
                 for i in range(len(arrays)) for mask in CHIP_MASKS]
    return _Exchange(arrays, outs, transfers)


def _own_block(chip, own, blocks):
    sel = (lax.broadcasted_iota(jnp.int32, (N_CHIPS,) + (1,) * (blocks.ndim - 1), 0) == chip)
    return jnp.where(sel, own if own.ndim == blocks.ndim else own[None], blocks)


def _join_halves(ci, mine, other, axis):
    return jnp.where(ci == 0, jnp.concatenate([mine, other], axis=axis), jnp.concatenate([other, mine], axis=axis))


def _mm_nt(name, a, b, n, tm, tn, out_dtype, rider=None):
    m, k = a.shape

    def body(a_ref, b_ref, o_ref):
        o_ref[...] = _nt(a_ref[...], b_ref[...]).astype(out_dtype)

    (out,), rode = _call(
        body, (a, b), name=name, grid=(n // tn, m // tm),
        in_specs=[pl.BlockSpec((tm, k), lambda j, i: (i, 0)), pl.BlockSpec((tn, k), lambda j, i: (j, 0))],
        out_specs=[pl.BlockSpec((tm, tn), lambda j, i: (i, j))],
        out_shape=[jax.ShapeDtypeStruct((m, n), out_dtype)], sem=("parallel", "parallel"), rider=rider)
    return out, rode


def _mm_acc(name, a, b, tm, tk, rider=None):
    m, kc = a.shape
    n = b.shape[1]

    def body(a_ref, b_ref, o_ref):
        @pl.when(pl.program_id(1) == 0)
        def _():
            o_ref[...] = jnp.zeros_like(o_ref)
        o_ref[...] += _nn(a_ref[...], b_ref[...])

    (out,), rode = _call(
        body, (a, b), name=name, grid=(m // tm, kc // tk),
        in_specs=[pl.BlockSpec((tm, tk), lambda i, kk: (i, kk)), pl.BlockSpec((tk, n), lambda i, kk: (kk, 0))],
        out_specs=[pl.BlockSpec((tm, n), lambda i, kk: (i, 0))],
        out_shape=[jax.ShapeDtypeStruct((m, n), F32)], sem=("parallel", "arbitrary"), rider=rider)
    return out, rode


def _mm_tn(name, a, b, tm, tk, with_bf16=False):
    kr, m = a.shape
    n = b.shape[1]
    steps_k = kr // tk

    def body(a_ref, b_ref, o_ref, *narrow):
        @pl.when(pl.program_id(1) == 0)
        def _():
            o_ref[...] = jnp.zeros_like(o_ref)
        o_ref[...] += _tn(a_ref[...], b_ref[...])
        if with_bf16:
            @pl.when(pl.program_id(1) == steps_k - 1)
            def _():
                narrow[0][...] = o_ref[...].astype(BF16)

    out_spec = pl.BlockSpec((tm, n), lambda i, kk: (i, 0))
    res = pl.pallas_call(
        body, name=name, grid=(m // tm, steps_k),
        in_specs=[pl.BlockSpec((tk, tm), lambda i, kk: (kk, i)), pl.BlockSpec((tk, n), lambda i, kk: (kk, 0))],
        out_specs=(out_spec,) * (2 if with_bf16 else 1),
        out_shape=(jax.ShapeDtypeStruct((m, n), F32),) + ((jax.ShapeDtypeStruct((m, n), BF16),) if with_bf16 else ()),
        compiler_params=_params(("parallel", "arbitrary")),
    )(a, b)
    return res if with_bf16 else res[0]


def _modulate_fwd(x, ctx, prm, tm, rider=None):
    t_rows, dm = x.shape
    lat = t_rows // tm
    r = t_rows + ctx.shape[0]

    def body(x_ref, c_ref, p_ref, h_ref):
        xv = jnp.where(pl.program_id(0) >= lat, c_ref[...], x_ref[...])
        mu = jnp.mean(xv, axis=-1, keepdims=True)
        xm = xv - mu
        n = xm * lax.rsqrt(jnp.mean(xm * xm, axis=-1, keepdims=True) + LN_EPS)
        h_ref[...] = (n * (1.0 + p_ref[0, 1:2, :]) + p_ref[0, 0:1, :]).astype(BF16)

    (h,), rode = _call(
        body, (x, ctx, prm), name="modulate_fwd", grid=(r // tm,),
        in_specs=[pl.BlockSpec((tm, dm), lambda i: (jnp.minimum(i, lat - 1), 0)),
                  pl.BlockSpec((tm, dm), lambda i: (jnp.maximum(i - lat, 0), 0)),
                  pl.BlockSpec((1, 8, dm), lambda i: ((i >= lat).astype(jnp.int32), 0, 0))],
        out_specs=[pl.BlockSpec((tm, dm), lambda i: (i, 0))],
        out_shape=[jax.ShapeDtypeStruct((r, dm), BF16)], sem=("parallel",), rider=rider)
    return h, rode


def _modulate_bwd(x, ctx, dh, prm, gx_direct, tm, rider=None):
    t_rows, dm = x.shape
    lat, n_ct = t_rows // tm, ctx.shape[0] // tm
    is_ctx = lambda i: i < n_ct
    cls = lambda i: is_ctx(i).astype(jnp.int32)
    lat_tile = lambda i: (jnp.maximum(i - n_ct, 0), 0)

    def body(x_ref, c_ref, dh_ref, p_ref, gd_ref, gx_ref, acc_ref):
        i = pl.program_id(0)

        @pl.when((i == 0) | (i == n_ct))
        def _():
            acc_ref[...] = jnp.zeros_like(acc_ref)

        x = jnp.where(is_ctx(i), c_ref[...], x_ref[...])
        dh_v = dh_ref[...]
        mu = jnp.mean(x, axis=-1, keepdims=True)
        xm = x - mu
        rstd = lax.rsqrt(jnp.mean(xm * xm, axis=-1, keepdims=True) + LN_EPS)
        n = xm * rstd
        acc_ref[0, 0:1, :] += jnp.sum(dh_v, axis=0, keepdims=True)
        acc_ref[0, 1:2, :] += jnp.sum(dh_v * n, axis=0, keepdims=True)
        dn = dh_v * (1.0 + p_ref[0, 1:2, :])
        dx = rstd * (dn - jnp.mean(dn, axis=-1, keepdims=True) - n * jnp.mean(dn * n, axis=-1, keepdims=True))
        gx_ref[...] = dx + gd_ref[...]

    return _call(
        body, (x, ctx, dh, prm, gx_direct), name="modulate_bwd", grid=(n_ct + lat,),
        in_specs=[pl.BlockSpec((tm, dm), lat_tile),
                  pl.BlockSpec((tm, dm), lambda i: (jnp.minimum(i, n_ct - 1), 0)),
                  pl.BlockSpec((tm, dm), lambda i: (jnp.where(is_ctx(i), lat + i, i - n_ct), 0)),
                  pl.BlockSpec((1, 8, dm), lambda i: (cls(i), 0, 0)),
                  pl.BlockSpec((tm, dm), lat_tile)],
        out_specs=(pl.BlockSpec((tm, dm), lat_tile), pl.BlockSpec((1, 8, dm), lambda i: (cls(i), 0, 0))),
        out_shape=(jax.ShapeDtypeStruct((t_rows, dm), F32), jax.ShapeDtypeStruct((2, 8, dm), F32)),
        sem=("arbitrary",), rider=rider)


def _conv_parts(t_rows, c_rows):
    return ((0, t_rows, t_rows // GRID_W, GRID_W), (t_rows, c_rows, 1, c_rows))


def _col_shifts(x2, rows_g, width_g):
    n, ct = x2.shape
    col = lax.broadcasted_iota(jnp.int32, (width_g, ct), 0)
    as_grid = lambda a: a.reshape(rows_g, width_g, ct)
    left = as_grid(pltpu.roll(x2, 1, 0)) * (col >= 1).astype(F32)
    right = as_grid(pltpu.roll(x2, n - 1, 0)) * (col <= width_g - 2).astype(F32)
    return [left, as_grid(x2), right]


CONV_BLOCK_ROWS = 4


def _conv_blocks(t_rows, c_rows):
    for t0, _, rows_g, width_g in _conv_parts(t_rows, c_rows):
        nb = min(CONV_BLOCK_ROWS, rows_g)
        assert rows_g % nb == 0
        for g0 in range(0, rows_g, nb):
            yield t0, rows_g, width_g, g0, nb


def _slab(ref, t0, rows_g, width_g, g0, nb):
    if rows_g == 1:
        return ref[t0:t0 + width_g, :]
    lo, hi = max(g0 - 1, 0), min(g0 + nb + 1, rows_g)
    parts = [ref[t0 + lo * width_g:t0 + hi * width_g, :]]
    zero = jnp.zeros((width_g, ref.shape[1]), F32)
    if g0 == 0:
        parts.insert(0, zero)
    if g0 + nb == rows_g:
        parts.append(zero)
    return jnp.concatenate(parts, axis=0)


def _conv_taps(cols, w_ref, nb, flip):
    one_row = cols[0].shape[0] == nb
    acc = None
    for a in range(3):
        if one_row and a != 1:
            continue
        for b in range(3):
            tap = (2 - a) * 3 + (2 - b) if flip else a * 3 + b
            term = (cols[b] if one_row else cols[b][a:a + nb]) * w_ref[tap:tap + 1, :]
            acc = term if acc is None else acc + term
    return acc


def _conv_fwd(u, conv_w9, conv_b, t_rows, c_rows, w, ct):
    r = u.shape[0]
    base = 5 * w // ct

    def body(x_ref, w_ref, b_ref, o_ref):
        for t0, rows_g, width_g, g0, nb in _conv_blocks(t_rows, c_rows):
            slab = _slab(x_ref, t0, rows_g, width_g, g0, nb)
            cols = _col_shifts(slab, slab.shape[0] // width_g, width_g)
            pre = _conv_taps(cols, w_ref, nb, False) + b_ref[...]
            o_ref[t0 + g0 * width_g:t0 + (g0 + nb) * width_g, :] = _silu(pre).reshape(nb * width_g, ct)

    return pl.pallas_call(
        body, name="conv_fwd", grid=(2 * w // ct,),
        in_specs=[pl.BlockSpec((r, ct), lambda i: (0, base + i)), pl.BlockSpec((9, ct), lambda i: (0, i)),
                  pl.BlockSpec((1, ct), lambda i: (0, i))],
        out_specs=pl.BlockSpec((r, ct), lambda i: (0, i)),
        out_shape=jax.ShapeDtypeStruct((r, 2 * w), F32),
        compiler_params=_params(("parallel",)),
    )(u, conv_w9, conv_b)


def _conv_bwd(u, dqk_pair, conv_w9, conv_b, t_rows, c_rows, w, ct):
    r = u.shape[0]
    base = 5 * w // ct

    def body(x_ref, d1_ref, d2_ref, w_ref, b_ref, dx_ref, dw_ref, db_ref, dpre_ref):
        dw = [jnp.zeros((1, ct), F32) for _ in range(9)]
        db = jnp.zeros((1, ct), F32)
        for t0, rows_g, width_g, g0, nb in _conv_blocks(t_rows, c_rows):
            rows = slice(t0 + g0 * width_g, t0 + (g0 + nb) * width_g)
            slab = _slab(x_ref, t0, rows_g, width_g, g0, nb)
            cols = _col_shifts(slab, slab.shape[0] // width_g, width_g)
            pre = _conv_taps(cols, w_ref, nb, False) + b_ref[...]
            sg = _sigmoid(pre)
            dpre = (d1_ref[rows, :] + d2_ref[rows, :]).reshape(pre.shape) * (sg * (1.0 + pre * (1.0 - sg)))
            dpre_ref[rows, :] = dpre.reshape(nb * width_g, ct)
            db = db + jnp.sum(jnp.sum(dpre, axis=0), axis=0, keepdims=True)
            for a in range(3):
                if rows_g == 1 and a != 1:
                    continue
                for b in range(3):
                    moved = cols[b] if rows_g == 1 else cols[b][a:a + nb]
                    dw[a * 3 + b] = dw[a * 3 + b] + jnp.sum(jnp.sum(moved * dpre, axis=0), axis=0, keepdims=True)
        for t0, rows_g, width_g, g0, nb in _conv_blocks(t_rows, c_rows):
            slab = _slab(dpre_ref, t0, rows_g, width_g, g0, nb)
            cols = _col_shifts(slab, slab.shape[0] // width_g, width_g)
            dx_ref[t0 + g0 * width_g:t0 + (g0 + nb) * width_g, :] = _conv_taps(cols, w_ref, nb, True).reshape(
                nb * width_g, ct).astype(BF16)
        for tap in range(9):
            dw_ref[tap:tap + 1, :] = dw[tap]
        db_ref[...] = db

    return pl.pallas_call(
        body, name="conv_bwd", grid=(2 * w // ct,),
        in_specs=[pl.BlockSpec((r, ct), lambda i: (0, base + i)), pl.BlockSpec((r, ct), lambda i: (0, i)),
                  pl.BlockSpec((r, ct), lambda i: (0, i)),
                  pl.BlockSpec((9, ct), lambda i: (0, i)), pl.BlockSpec((1, ct), lambda i: (0, i))],
        out_specs=(pl.BlockSpec((r, ct), lambda i: (0, i)), pl.BlockSpec((9, ct), lambda i: (0, i)),
                   pl.BlockSpec((1, ct), lambda i: (0, i))),
        out_shape=(jax.ShapeDtypeStruct((r, 2 * w), BF16), jax.ShapeDtypeStruct((9, 2 * w), F32),
                   jax.ShapeDtypeStruct((1, 2 * w), F32)),
        scratch_shapes=[pltpu.VMEM((r, ct), F32)],
        compiler_params=_params(("parallel",)),
    )(u, dqk_pair[0], dqk_pair[1], conv_w9, conv_b)


def _assemble_du(groups, gates, n_pad, tm):
    flat, layout = [], []
    for entry in list(groups) + [gates]:
        parts = entry if isinstance(entry, (tuple, list)) else (entry,)
        layout.append((len(flat), len(parts), parts[0].shape[1]))
        flat += list(parts)
    r = flat[0].shape[0]

    def body(*refs):
        o_ref = refs[-1]
        col = 0
        for first, count, width in layout:
            val = refs[first][...]
            for extra in range(1, count):
                val = val.astype(F32) + refs[first + extra][...].astype(F32)
            o_ref[:, col:col + width] = val.astype(BF16)
            col += width
        assert col == n_pad

    return pl.pallas_call(
        body, name="assemble_du", grid=(r // tm,),
        in_specs=[pl.BlockSpec((tm, a.shape[1]), lambda i: (i, 0)) for a in flat],
        out_specs=pl.BlockSpec((tm, n_pad), lambda i: (i, 0)),
        out_shape=jax.ShapeDtypeStruct((r, n_pad), BF16),
        compiler_params=_params(("parallel",)),
    )(*flat)


def _scan_order(n_lat, n_ctx, rev):
    n = n_lat + n_ctx
    if rev:
        return lambda j: n - 1 - j
    return lambda j: (j + n_lat) % n


DIRS = (False, True)


def _hg_scan_fwd(u, lb_full, w, n_lat, n_ctx, chunk, rider=None):
    r = u.shape[0]
    n_heads = w // HG_DK
    sub = HG_CHUNKS_PER_STEP if n_lat % HG_CHUNKS_PER_STEP == 0 and n_ctx % HG_CHUNKS_PER_STEP == 0 else 1
    n_steps = (n_lat + n_ctx) // sub
    nat = [_scan_order(n_lat // sub, n_ctx // sub, rev) for rev in DIRS]
    rows = sub * chunk

    def body(*refs):
        ins, outs, scratch = refs[:8], refs[8:12], refs[12:]

        @pl.when(pl.program_id(0) == 0)
        def _():
            for s_ref in scratch:
                s_ref[...] = jnp.zeros_like(s_ref)

        for d, rev in enumerate(DIRS):
            aq, af, ai, lb_ref = ins[4 * d:4 * d + 4]
            o_ref, save_ref = outs[2 * d:2 * d + 2]
            state = [scratch[d][h] for h in range(n_heads)]
            for p in range(sub):
                sl = slice((sub - 1 - p if rev else p) * chunk, (sub - p if rev else p + 1) * chunk)
                for h in range(n_heads):
                    save_ref[0, p, h] = state[h]
                state, o = _hg_chunk(state, aq[sl, :], af[sl, :], ai[sl, :], lb_ref[0, 0:1, :], lb_ref[0, 1:2, :], rev)
                o_ref[sl, :] = o
            for h in range(n_heads):
                scratch[d][h] = state[h]

    in_specs, out_specs, out_shape = [], [], []
    for d in range(2):
        in_specs += [pl.BlockSpec((rows, w), lambda j, d=d: (nat[d](j), 0)),
                     pl.BlockSpec((rows, w), lambda j, d=d: (nat[d](j), 1 + d)),
                     pl.BlockSpec((rows, w), lambda j, d=d: (nat[d](j), 3)),
                     pl.BlockSpec((1, 2, w), lambda j, d=d: (d, 0, 0))]
        out_specs += [pl.BlockSpec((rows, w), lambda j, d=d: (nat[d](j), 0)),
                      pl.BlockSpec((1, sub, n_heads, HG_DK, HG_DK), lambda j: (j, 0, 0, 0, 0))]
        out_shape += [jax.ShapeDtypeStruct((r, w), F32),
                      jax.ShapeDtypeStruct((n_steps, sub, n_heads, HG_DK, HG_DK), F32)]
    (o_f, s_f, o_b, s_b), rode = _call(
        body, (u, u, u, lb_full, u, u, u, lb_full), name="hg_scan_fwd", grid=(n_steps,), in_specs=in_specs,
        out_specs=out_specs, out_shape=out_shape, scratch_shapes=[pltpu.VMEM((n_heads, HG_DK, HG_DK), F32)] * 2,
        sem=("arbitrary",), rider=rider)
    return (o_f, o_b), (s_f, s_b), rode


def _hg_scan_bwd(u, lb_full, saved, d_o, w, n_lat, n_ctx, chunk, rider=None):
    r = u.shape[0]
    n_heads = w // HG_DK
    n_steps, sub = saved[0].shape[0], saved[0].shape[1]
    n_lat_s = n_lat // sub
    step = lambda jj: n_steps - 1 - jj
    nat = [(lambda jj, o=_scan_order(n_lat_s, n_ctx // sub, rev): o(step(jj))) for rev in DIRS]
    rows = sub * chunk

    def body(*refs):
        ins, outs, scratch = refs[:12], refs[12:20], refs[20:]
        jj = pl.program_id(0)

        @pl.when(jj == 0)
        def _():
            for d in range(2):
                scratch[d][...] = jnp.zeros_like(scratch[d])
                outs[4 * d + 3][...] = jnp.zeros_like(outs[4 * d + 3])

        for d, rev in enumerate(DIRS):
            aq, af, ai, lb_ref, save_ref, do_ref = ins[6 * d:6 * d + 6]
            daq_ref, daf_ref, dai_ref, dlb_ref = outs[4 * d:4 * d + 4]
            f = lambda st, a, b, c, l0, l1, rev=rev: _hg_chunk(st, a, b, c, l0, l1, rev)
            latent = (nat[d](jj) < n_lat_s).astype(F32)
            d_state = [scratch[d][h] for h in range(n_heads)]
            for p in reversed(range(sub)):
                sl = slice((sub - 1 - p if rev else p) * chunk, (sub - p if rev else p + 1) * chunk)
                _, vjp = jax.vjp(f, [save_ref[0, p, h] for h in range(n_heads)], aq[sl, :], af[sl, :], ai[sl, :],
                                 lb_ref[0, 0:1, :], lb_ref[0, 1:2, :])
                d_state, daq, daf, dai, dl0, dl1 = vjp((d_state, do_ref[sl, :] * latent))
                daq_ref[sl, :] = daq.astype(BF16)
                daf_ref[sl, :] = daf.astype(BF16)
                dai_ref[sl, :] = dai.astype(BF16)
                dlb_ref[0:1, :] += dl0
                dlb_ref[1:2, :] += dl1
            for h in range(n_heads):
                scratch[d][h] = d_state[h]

    in_specs, out_specs, out_shape, operands = [], [], [], []
    for d in range(2):
        row = lambda jj, d=d: (nat[d](jj), 0)
        in_specs += [pl.BlockSpec((rows, w), row),
                     pl.BlockSpec((rows, w), lambda jj, d=d: (nat[d](jj), 1 + d)),
                     pl.BlockSpec((rows, w), lambda jj, d=d: (nat[d](jj), 3)),
                     pl.BlockSpec((1, 2, w), lambda jj, d=d: (d, 0, 0)),
                     pl.BlockSpec((1, sub, n_heads, HG_DK, HG_DK), lambda jj: (step(jj), 0, 0, 0, 0)),
                     pl.BlockSpec((rows, w), lambda jj, d=d: (jnp.minimum(nat[d](jj), n_lat_s - 1), 0))]
        operands += [u, u, u, lb_full, saved[d], d_o]
        out_specs += [pl.BlockSpec((rows, w), row)] * 3 + [pl.BlockSpec((2, w), lambda jj: (0, 0))]
        out_shape += [jax.ShapeDtypeStruct((r, w), BF16)] * 3 + [jax.ShapeDtypeStruct((2, w), F32)]
    res, rode = _call(
        body, operands, name="hg_scan_bwd", grid=(n_steps,), in_specs=in_specs, out_specs=out_specs,
        out_shape=out_shape, scratch_shapes=[pltpu.VMEM((n_heads, HG_DK, HG_DK), F32)] * 2,
        sem=("arbitrary",), rider=rider)
    return res[0:4], res[4:8], rode


def _ml_state_shapes(n_chunks, n_heads, dh):
    return (jax.ShapeDtypeStruct((n_chunks, n_heads, dh, dh), F32),
            jax.ShapeDtypeStruct((n_chunks, n_heads, 1, dh), F32),
            jax.ShapeDtypeStruct((n_chunks, n_heads, 1, LANE), F32))


def _ml_state_specs(n_heads, dh, index):
    return (pl.BlockSpec((1, n_heads, dh, dh), lambda j: (index(j), 0, 0, 0)),
            pl.BlockSpec((1, n_heads, 1, dh), lambda j: (index(j), 0, 0, 0)),
            pl.BlockSpec((1, n_heads, 1, LANE), lambda j: (index(j), 0, 0, 0)))


def _ml_state_scratch(n_heads, dh):
    return [pltpu.VMEM((n_heads, dh, dh), F32), pltpu.VMEM((n_heads, 1, dh), F32), pltpu.VMEM((n_heads, 1, LANE), F32)]


def _ml_scan_fwd(qk, u, gate_b, w, n_heads, n_lat, n_ctx, chunk):
    r = u.shape[0]
    dh = w // n_heads
    n_chunks = n_lat + n_ctx
    nat = [_scan_order(n_lat, n_ctx, rev) for rev in DIRS]

    def body(*refs):
        ins, outs, scratch = refs[:10], refs[10:18], refs[18:]

        @pl.when(pl.program_id(0) == 0)
        def _():
            for s_ref in scratch:
                s_ref[...] = jnp.zeros_like(s_ref)

        results = []
        for d, rev in enumerate(DIRS):
            q, k, v, g, gb = ins[5 * d:5 * d + 5]
            state = tuple([ref[h] for h in range(n_heads)] for ref in scratch[3 * d:3 * d + 3])
            results.append((state, _ml_chunk(state, q[...], k[...], v[...], g[...], gb[...], rev, d)))
        for d, (state, (new, o)) in enumerate(results):
            outs[4 * d][...] = o
            for part in range(3):
                for h in range(n_heads):
                    outs[4 * d + 1 + part][0, h] = state[part][h]
                    scratch[3 * d + part][h] = new[part][h]

    in_specs, out_specs, out_shape = [], [], []
    for d in range(2):
        in_specs += [pl.BlockSpec((chunk,w), lambda j, d=d: (nat[d](j), 0)),
                     pl.BlockSpec((chunk,w), lambda j, d=d: (nat[d](j), 1)),
                     pl.BlockSpec((chunk,w), lambda j, d=d: (nat[d](j), 7)),
                     pl.BlockSpec((chunk,LANE), lambda j, d=d: (nat[d](j), 10 * w // LANE)),
                     pl.BlockSpec((1, LANE), lambda j: (0, 0))]
        out_specs += [pl.BlockSpec((chunk,w), lambda j, d=d: (nat[d](j), 0))]
        out_specs += list(_ml_state_specs(n_heads, dh, lambda j: j))
        out_shape += [jax.ShapeDtypeStruct((r, w), F32)] + list(_ml_state_shapes(n_chunks, n_heads, dh))
    res = pl.pallas_call(
        body, name="ml_scan_fwd", grid=(n_chunks,), in_specs=in_specs, out_specs=tuple(out_specs),
        out_shape=tuple(out_shape), scratch_shapes=_ml_state_scratch(n_heads, dh) * 2,
        compiler_params=_params(("arbitrary",)),
    )(qk, qk, u, u, gate_b, qk, qk, u, u, gate_b)
    return (res[0], res[4]), (res[1:4], res[5:8])


def _ml_scan_bwd(qk, u, gate_b, saved, d_h, w, n_heads, n_lat, n_ctx, chunk, rider=None):
    r = u.shape[0]
    dh = w // n_heads
    n_chunks = n_lat + n_ctx
    step = lambda jj: n_chunks - 1 - jj
    nat = [(lambda jj, o=_scan_order(n_lat, n_ctx, rev): o(step(jj))) for rev in DIRS]

    def body(*refs):
        ins, outs, scratch = refs[:18], refs[18:26], refs[26:]
        jj = pl.program_id(0)

        @pl.when(jj == 0)
        def _():
            for s_ref in scratch:
                s_ref[...] = jnp.zeros_like(s_ref)
            for d in range(2):
                outs[4 * d + 3][...] = jnp.zeros_like(outs[4 * d + 3])

        results = []
        for d, rev in enumerate(DIRS):
            q, k, v, g, gb, sc, sn, sm, dh_ref = ins[9 * d:9 * d + 9]
            state = tuple([ref[0, h] for h in range(n_heads)] for ref in (sc, sn, sm))
            f = lambda st, a, b, c, gg, bb, rev=rev, d=d: _ml_chunk(st, a, b, c, gg, bb, rev, d)
            _, vjp = jax.vjp(f, state, q[...], k[...], v[...], g[...], gb[...])
            d_state = tuple([ref[h] for h in range(n_heads)] for ref in scratch[3 * d:3 * d + 3])
            d_out = dh_ref[...] * (nat[d](jj) < n_lat).astype(F32)
            results.append(vjp((d_state, d_out)))
        for d, (d_state, dq, dk, dv, dg, dgb) in enumerate(results):
            dqk_ref, dv_ref, dg_ref, dgb_ref = outs[4 * d:4 * d + 4]
            for part in range(3):
                for h in range(n_heads):
                    scratch[3 * d + part][h] = d_state[part][h]
            dqk_ref[:, 0:w] = dq
            dqk_ref[:, w:2 * w] = dk
            dv_ref[...] = dv.astype(BF16)
            dg_ref[...] = dg
            dgb_ref[...] += dgb

    in_specs, out_specs, out_shape, operands = [], [], [], []
    for d in range(2):
        row = lambda jj, d=d: (nat[d](jj), 0)
        in_specs += [pl.BlockSpec((chunk,w), row), pl.BlockSpec((chunk,w), lambda jj, d=d: (nat[d](jj), 1)),
                     pl.BlockSpec((chunk,w), lambda jj, d=d: (nat[d](jj), 7)),
                     pl.BlockSpec((chunk,LANE), lambda jj, d=d: (nat[d](jj), 10 * w // LANE)),
                     pl.BlockSpec((1, LANE), lambda jj: (0, 0))]
        in_specs += list(_ml_state_specs(n_heads, dh, step))
        in_specs += [pl.BlockSpec((chunk,w), lambda jj, d=d: (jnp.minimum(nat[d](jj), n_lat - 1), 0))]
        operands += [qk, qk, u, u, gate_b, *saved[d], d_h]
        out_specs += [pl.BlockSpec((chunk,2 * w), row), pl.BlockSpec((chunk,w), row),
                      pl.BlockSpec((chunk,LANE), row), pl.BlockSpec((1, LANE), lambda jj: (0, 0))]
        out_shape += [jax.ShapeDtypeStruct((r, 2 * w), F32), jax.ShapeDtypeStruct((r, w), BF16),
                      jax.ShapeDtypeStruct((r, LANE), F32), jax.ShapeDtypeStruct((1, LANE), F32)]
    res, rode = _call(
        body, operands, name="ml_scan_bwd", grid=(n_chunks,), in_specs=in_specs, out_specs=out_specs,
        out_shape=out_shape, scratch_shapes=_ml_state_scratch(n_heads, dh) * 2, sem=("arbitrary",), rider=rider)
    return res[0:4], res[4:8], rode


def _post_specs(w, tm, lat_tiles, cols):
    return [pl.BlockSpec((tm, w), (lambda i, cb=cb: (jnp.minimum(i, lat_tiles - 1), cb))) for cb in cols]


def _post_fwd(o_f, o_b, h_f, h_b, u, wa, wb, t_rows, w, n_hg, n_ml, tm):
    lat_tiles = t_rows // tm

    def body(of, ob, hf, hb, az, bo, bz, wa_ref, wb_ref, y_ref):
        y_ref[...] = _post_fn(of[...], ob[...], az[...], hf[...], hb[...], bo[...], bz[...],
                              wa_ref[...], wb_ref[...], n_hg, n_ml).astype(BF16)

    rows = pl.BlockSpec((tm, w), lambda i: (i, 0))
    vec = pl.BlockSpec((1, w), lambda i: (0, 0))
    return pl.pallas_call(
        body, name="post_fwd", grid=(lat_tiles,),
        in_specs=[rows] * 4 + _post_specs(w, tm, lat_tiles, (4, 8, 9)) + [vec, vec],
        out_specs=pl.BlockSpec((tm, 2 * w), lambda i: (i, 0)),
        out_shape=jax.ShapeDtypeStruct((t_rows, 2 * w), BF16),
        compiler_params=_params(("parallel",)),
    )(o_f, o_b, h_f, h_b, u, u, u, wa, wb)


def _post_bwd(o_f, o_b, h_f, h_b, u, wa, wb, dy, t_rows, w, n_hg, n_ml, tm, rider=None):
    r = u.shape[0]
    lat_tiles = t_rows // tm
    lat = lambda i: (jnp.minimum(i, lat_tiles - 1), 0)

    def body(of, ob, hf, hb, az, bo, bz, wa_ref, wb_ref, dy_ref, do_ref, dh_ref, daz_ref, dbo_ref, dbz_ref,
             dwa_ref, dwb_ref):
        i = pl.program_id(0)

        @pl.when(i == 0)
        def _():
            dwa_ref[...] = jnp.zeros_like(dwa_ref)
            dwb_ref[...] = jnp.zeros_like(dwb_ref)

        @pl.when(i < lat_tiles)
        def _():
            f = functools.partial(_post_fn, n_hg=n_hg, n_ml=n_ml)
            _, vjp = jax.vjp(f, of[...], ob[...], az[...], hf[...], hb[...], bo[...], bz[...], wa_ref[...], wb_ref[...])
            d_of, _, d_az, d_hf, _, d_bo, d_bz, d_wa, d_wb = vjp(dy_ref[...])
            do_ref[...] = d_of
            dh_ref[...] = d_hf
            daz_ref[...] = d_az.astype(BF16)
            dbo_ref[...] = d_bo.astype(BF16)
            dbz_ref[...] = d_bz.astype(BF16)
            dwa_ref[...] += d_wa
            dwb_ref[...] += d_wb

        @pl.when(i >= lat_tiles)
        def _():
            daz_ref[...] = jnp.zeros_like(daz_ref)
            dbo_ref[...] = jnp.zeros_like(dbo_ref)
            dbz_ref[...] = jnp.zeros_like(dbz_ref)

    lat_rows = pl.BlockSpec((tm, w), lat)
    all_rows = pl.BlockSpec((tm, w), lambda i: (i, 0))
    vec = pl.BlockSpec((1, w), lambda i: (0, 0))
    sd_t = jax.ShapeDtypeStruct((t_rows, w), F32)
    sd_r = jax.ShapeDtypeStruct((r, w), BF16)
    sd_v = jax.ShapeDtypeStruct((1, w), F32)
    return _call(
        body, (o_f, o_b, h_f, h_b, u, u, u, wa, wb, dy), name="post_bwd", grid=(r // tm,),
        in_specs=[lat_rows] * 4 + _post_specs(w, tm, lat_tiles, (4, 8, 9)) + [vec, vec]
        + [pl.BlockSpec((tm, 2 * w), lat)],
        out_specs=(lat_rows, lat_rows, all_rows, all_rows, all_rows, vec, vec),
        out_shape=(sd_t, sd_t, sd_r, sd_r, sd_r, sd_v, sd_v), sem=("arbitrary",), rider=rider)


OUT_ROW_GATE, OUT_ROW_LN_G, OUT_ROW_LN_B, OUT_ROW_LOSS = 0, 1, 2, 3


def _out_block(y, w_out, x, target, prm, tm):
    t_rows, dm = x.shape
    di = y.shape[1]

    def body(y_ref, w_ref, x_ref, t_ref, p_ref, dz_ref, dy_ref, gx_ref, acc_ref):
        @pl.when(pl.program_id(0) == 0)
        def _():
            acc_ref[...] = jnp.zeros_like(acc_ref)

        gate, ln_g, ln_b = p_ref[0:1, :], p_ref[1:2, :], p_ref[2:3, :]
        z = _nn(y_ref[...], w_ref[...])
        res = ALPHA * x_ref[...] + gate * z
        mu = jnp.mean(res, axis=-1, keepdims=True)
        rc = res - mu
        rstd = lax.rsqrt(jnp.mean(rc * rc, axis=-1, keepdims=True) + LN_EPS)
        rn = rc * rstd
        err = rn * ln_g + ln_b - t_ref[...]
        d_out = err * (1.0 / dm)
        d_rn = d_out * ln_g
        d_res = rstd * (d_rn - jnp.mean(d_rn, axis=-1, keepdims=True)
                        - rn * jnp.mean(d_rn * rn, axis=-1, keepdims=True))
        acc_ref[OUT_ROW_GATE:OUT_ROW_GATE + 1, :] += jnp.sum(d_res * z, axis=0, keepdims=True)
        acc_ref[OUT_ROW_LN_G:OUT_ROW_LN_G + 1, :] += jnp.sum(d_out * rn, axis=0, keepdims=True)
        acc_ref[OUT_ROW_LN_B:OUT_ROW_LN_B + 1, :] += jnp.sum(d_out, axis=0, keepdims=True)
        acc_ref[OUT_ROW_LOSS:OUT_ROW_LOSS + 1, :] += (0.5 / dm) * jnp.sum(err * err, axis=0, keepdims=True)
        gx_ref[...] = ALPHA * d_res
        dz = (d_res * gate).astype(BF16)
        dz_ref[...] = dz
        dy_ref[...] = _nt(dz, w_ref[...])

    rows_d = pl.BlockSpec((tm, dm), lambda i: (i, 0))
    rows_i = pl.BlockSpec((tm, di), lambda i: (i, 0))
    return pl.pallas_call(
        body, name="out_block", grid=(t_rows // tm,),
        in_specs=[rows_i, pl.BlockSpec((di, dm), lambda i: (0, 0)), rows_d, rows_d,
                  pl.BlockSpec((8, dm), lambda i: (0, 0))],
        out_specs=(rows_d, rows_i, rows_d, pl.BlockSpec((8, dm), lambda i: (0, 0))),
        out_shape=(jax.ShapeDtypeStruct((t_rows, dm), BF16), jax.ShapeDtypeStruct((t_rows, di), F32),
                   jax.ShapeDtypeStruct((t_rows, dm), F32), jax.ShapeDtypeStruct((8, dm), F32)),
        compiler_params=_params(("arbitrary",)),
    )(y, w_out, x, target, prm)


def _mod_fwd(c16, w_mod, tn, rider=None):
    dm, n = w_mod.shape

    def body(c_ref, w_ref, o_ref, a_ref):
        a = _silu(c_ref[...])
        a_ref[...] = a
        o_ref[...] = _nn(a, w_ref[...], HIGHEST)

    return _call(
        body, (c16, w_mod), name="mod_fwd", grid=(n // tn,),
        in_specs=[pl.BlockSpec((16, dm), lambda j: (0, 0)), pl.BlockSpec((dm, tn), lambda j: (0, j))],
        out_specs=[pl.BlockSpec((16, tn), lambda j: (0, j)), pl.BlockSpec((16, dm), lambda j: (0, 0))],
        out_shape=[jax.ShapeDtypeStruct((16, n), F32), jax.ShapeDtypeStruct((16, dm), F32)],
        sem=("arbitrary",), rider=rider)


def _mod_bwd(a16, dm16, w_mod, tn, rider=None):
    dm, n = w_mod.shape

    def body(a_ref, d_ref, w_ref, dw_ref, dc_ref):
        @pl.when(pl.program_id(0) == 0)
        def _():
            dc_ref[...] = jnp.zeros_like(dc_ref)
        dw_ref[...] = _tn(a_ref[...], d_ref[...], HIGHEST)
        dc_ref[...] += _nt(d_ref[...], w_ref[...], HIGHEST)

    return _call(
        body, (a16, dm16, w_mod), name="mod_bwd", grid=(n // tn,),
        in_specs=[pl.BlockSpec((16, dm), lambda j: (0, 0)), pl.BlockSpec((16, tn), lambda j: (0, j)),
                  pl.BlockSpec((dm, tn), lambda j: (0, j))],
        out_specs=(pl.BlockSpec((dm, tn), lambda j: (0, j)), pl.BlockSpec((16, dm), lambda j: (0, 0))),
        out_shape=(jax.ShapeDtypeStruct((dm, n), F32), jax.ShapeDtypeStruct((16, dm), F32)),
        sem=("arbitrary",), rider=rider)


def _sum_devices(g, fold_rows):
    n_dev, rows, n = g.shape

    def body(g_ref, s_ref, t_ref):
        s = g_ref[0]
        for dev in range(1, n_dev):
            s = s + g_ref[dev]
        t_ref[...] = jnp.broadcast_to(jnp.sum(s, axis=-1, keepdims=True), (rows, LANE))
        s_ref[...] = s
        s_ref[0:fold_rows, :] = s[0:fold_rows] + s[fold_rows:2 * fold_rows]

    return pl.pallas_call(
        body, name="sum_devices",
        out_shape=(jax.ShapeDtypeStruct((rows, n), F32), jax.ShapeDtypeStruct((rows, LANE), F32)),
        compiler_params=_params(),
    )(g)


def _c_ctx_grad(parts, c_ctx_row):
    def body(p_ref, c_ref, o_ref):
        s = p_ref[0]
        for chip in range(1, N_CHIPS):
            s = s + p_ref[2 * chip]
        cv = c_ref[...]
        sg = _sigmoid(cv)
        o_ref[...] = s * (sg * (1.0 + cv * (1.0 - sg)))

    return pl.pallas_call(
        body, name="c_ctx_grad", out_shape=jax.ShapeDtypeStruct(parts.shape[1:], F32), compiler_params=_params(),
    )(parts, c_ctx_row)


def _sum_pair(name, mine, got):
    def body(a_ref, b_ref, o_ref):
        o_ref[...] = (a_ref[...] + b_ref[...]).astype(BF16)

    k, rows, n = mine.shape
    tl = _largest_divisor(n, max(LANE, (1 << 18) // rows), LANE)
    spec = pl.BlockSpec((1, rows, tl), lambda kk, i: (kk, 0, i))
    return pl.pallas_call(
        body, name=name, grid=(k, n // tl), in_specs=[spec, spec], out_specs=spec,
        out_shape=jax.ShapeDtypeStruct(mine.shape, BF16), compiler_params=_params(("parallel", "parallel")),
    )(mine, got)


def _sum_pair_lanes(name, full, got, ci):
    rows, n = got.shape
    tr = _largest_divisor(rows, max(SUBLANE_BF16, (1 << 19) // n), SUBLANE_BF16)

    def body(ci_ref, a_ref, b_ref, o_ref):
        o_ref[...] = (a_ref[...] + b_ref[...].astype(F32)).astype(BF16)

    return pl.pallas_call(
        body, name=name,
        grid_spec=pltpu.PrefetchScalarGridSpec(
            num_scalar_prefetch=1, grid=(rows // tr,),
            in_specs=[pl.BlockSpec((tr, n), lambda i, c: (i, c[0])), pl.BlockSpec((tr, n), lambda i, c: (i, 0))],
            out_specs=pl.BlockSpec((tr, n), lambda i, c: (i, 0))),
        out_shape=jax.ShapeDtypeStruct((rows, n), BF16), compiler_params=_params(("parallel",)),
    )(ci.reshape(1).astype(jnp.int32), full, got)


def _sum_chips(name, got, own, chip):
    k, rows, n = got.shape
    tl = _largest_divisor(n, max(LANE, (1 << 18) // rows), LANE)

    def body(chip_ref, g_ref, own_ref, o_ref):
        total = None
        for kk in range(k):
            term = jnp.where(chip_ref[0] == kk, own_ref[0], g_ref[kk]).astype(F32)
            total = term if total is None else total + term
        o_ref[...] = total

    return pl.pallas_call(
        body, name=name,
        grid_spec=pltpu.PrefetchScalarGridSpec(
            num_scalar_prefetch=1, grid=(n // tl,),
            in_specs=[pl.BlockSpec((k, rows, tl), lambda i, c: (0, 0, i)),
                      pl.BlockSpec((1, rows, tl), lambda i, c: (c[0], 0, i))],
            out_specs=pl.BlockSpec((rows, tl), lambda i, c: (0, i))),
        out_shape=jax.ShapeDtypeStruct((rows, n), F32), compiler_params=_params(("parallel",)),
    )(chip.reshape(1).astype(jnp.int32), got, own)


def _adamw_update(w, g, m, v):
    m2 = ADAM_B1 * m + (1.0 - ADAM_B1) * g
    v2 = ADAM_B2 * v + (1.0 - ADAM_B2) * jnp.square(g)
    m_hat = m2 / (1.0 - ADAM_B1 ** ADAM_STEP)
    v_hat = v2 / (1.0 - ADAM_B2 ** ADAM_STEP)
    return -ADAM_LR * (m_hat / (jnp.sqrt(v_hat) + ADAM_EPS) + ADAM_WD * w), m2, v2


def _adamw(name, w, g, m, v, rider=None):
    rows, n = w.shape
    if rows % 8 == 0:
        tr = _largest_divisor(rows, max(8, (1 << 18) // n), 8)
        block, index, steps = (tr, n), (lambda i: (i, 0)), rows // tr
    else:
        tl = _largest_divisor(n, max(LANE, (1 << 18) // rows), LANE)
        block, index, steps = (rows, tl), (lambda i: (0, i)), n // tl

    def body(w_ref, g_ref, m_ref, v_ref, d_ref, mo_ref, vo_ref):
        d_ref[...], mo_ref[...], vo_ref[...] = _adamw_update(w_ref[...], g_ref[...], m_ref[...], v_ref[...])

    spec = pl.BlockSpec(block, index)
    sds = jax.ShapeDtypeStruct((rows, n), F32)
    return _call(body, (w, g, m, v), name=name, grid=(steps,), in_specs=[spec] * 4, out_specs=(spec,) * 3,
                 out_shape=(sds, sds, sds), sem=("parallel",), rider=rider)


PACK_LANES = 1024


def _pack(pieces):
    flat = jnp.concatenate([p.reshape(-1) for p in pieces])
    total = -(-flat.shape[0] // (8 * PACK_LANES)) * 8 * PACK_LANES
    return jnp.pad(flat, (0, total - flat.shape[0])).reshape(-1, PACK_LANES)


def _unpack(packed, shapes):
    flat = packed.reshape(-1)
    out, off = [], 0
    for shp in shapes:
        size = math.prod(shp)
        out.append(flat[off:off + size].reshape(shp))
        off += size
    return out


def _rows8(rows, width):
    flat = [r.reshape(width) for r in rows] + [jnp.zeros(((8 - len(rows)) * width,), F32)]
    return jnp.concatenate(flat).reshape(8, width)


def kernel(x, c, ctx, c_ctx, w_mod, b_mod, w_in, conv_w, conv_b, hg_lb, ml_gate_b, hg_norm_w, ml_norm_w, w_out, ln_g, ln_b, loss_target, m_c_ctx, m_w_mod, m_b_mod, m_w_in, m_conv_w, m_conv_b, m_hg_lb, m_ml_gate_b, m_hg_norm_w, m_ml_norm_w, m_w_out, m_ln_g, m_ln_b, v_c_ctx, v_w_mod, v_b_mod, v_w_in, v_conv_w, v_conv_b, v_hg_lb, v_ml_gate_b, v_hg_norm_w, v_ml_norm_w, v_w_out, v_ln_g, v_ln_b):
    t_rows, dm = x.shape[1], x.shape[2]
    c_rows = ctx.shape[1]
    w = hg_norm_w.shape[1]
    n_ml = ml_gate_b.shape[-1]
    n_hg = w // HG_DK
    di = 2 * w
    n_in = 10 * w + 4 * n_ml
    ns = w_in.shape[2]
    nm = w_mod.shape[2]
    n_pad = 10 * w + LANE
    r_rows = t_rows + c_rows
    row_gcd = math.gcd(t_rows, c_rows)
    hg_chunk, ml_chunk = math.gcd(HG_CHUNK, row_gcd), math.gcd(ML_CHUNK, row_gcd)
    hg_counts = (t_rows // hg_chunk, c_rows // hg_chunk, hg_chunk)
    ml_counts = (t_rows // ml_chunk, c_rows // ml_chunk, ml_chunk)
    assert ml_norm_w.shape[1] == w and di == dm and N_CHIPS * ns == n_in and N_CHIPS * nm == 3 * dm
    assert w_out.shape[1] * N_CHIPS == di and 4 * n_ml <= LANE and t_rows % GRID_W == 0

    xi, yi, ci = lax.axis_index("x"), lax.axis_index("y"), lax.axis_index("c")
    chip = 2 * xi + yi
    dev = 4 * xi + 2 * yi + ci

    tm = _largest_divisor(math.gcd(t_rows, c_rows), 256, 8)
    tm_mm = _largest_divisor(r_rows, 1088, SUBLANE_BF16)
    tn_mm = LANE * _largest_divisor(n_pad // LANE, 9)
    tn_mod = _largest_divisor(nm, 512, LANE)

    shard_shapes = [(dm,), (2, 2, w // N_CHIPS), (3, 3, di // N_CHIPS)]
    g1 = _all_gather8(_pack([c, hg_lb, conv_w])).run("gather_inputs")[0]
    per_dev = [_unpack(g1[i], shard_shapes) for i in range(N_DEV)]
    c_all = jnp.stack([p[0] for p in per_dev])
    lb_full = jnp.concatenate([per_dev[2 * k][1] for k in range(N_CHIPS)], axis=-1)
    conv_w9 = jnp.concatenate([per_dev[2 * k][2] for k in range(N_CHIPS)], axis=-1).reshape(9, di)

    as_t = lambda a: jnp.transpose(a[0])
    half_in = lax.dynamic_slice_in_dim(as_t(w_in).astype(BF16), ci * (dm // 2), dm // 2, 1)
    half_out = lax.dynamic_slice_in_dim(w_out[0].astype(BF16), ci * (di // (2 * N_CHIPS)), di // (2 * N_CHIPS), 0)
    lanes_of = lambda core: pl.ds(core * (dm // 2), dm // 2)
    landing = lambda s, r: (_chip_of(s), slice(None), lanes_of(s[2]))
    own_placed = lax.dynamic_update_slice(jnp.zeros((N_CHIPS + 1, ns, dm), BF16),
                                          as_t(w_in).astype(BF16)[None], (chip, 0, 0))
    whole = jax.ShapeDtypeStruct(own_placed.shape, BF16)
    gather_in = _Exchange([half_in, own_placed], [whole],
                          [(mask, 0, None, 0, landing) for mask in CHIP_MASKS[:2]], in_place={1: 0})

    c16 = jnp.concatenate([c_all, c_ctx[None], jnp.zeros((16 - N_DEV - 1, dm), F32)])
    (mod_part, a16), (gw_in,) = _mod_fwd(c16, w_mod[0], tn_mod, rider=gather_in)
    g2 = _all_gather8(mod_part).run("gather_mod")[0]
    mod_all = jnp.concatenate([g2[2 * k] for k in range(N_CHIPS)], axis=1) + b_mod
    mod_x = lax.dynamic_index_in_dim(mod_all, dev, 0, keepdims=False).reshape(3, dm)
    mod_c = mod_all[N_DEV].reshape(3, dm)
    prm = jnp.stack([_rows8(list(mod_x), dm), _rows8(list(mod_c), dm)])

    chip_own = lambda s: 2 * s[0] + s[1]
    chip_x = lambda s: 2 * (1 - s[0]) + s[1]
    chip_y = lambda s: 2 * s[0] + 1 - s[1]
    chip_xy = lambda s: 2 * (1 - s[0]) + 1 - s[1]
    quarter = lambda core, q: pl.ds(core * (dm // 2) + q * (dm // 4), dm // 4)
    half_of = lambda which: (lambda s, r: (which(s), slice(None), lanes_of(s[2])))
    relay_x = lambda s, r: (chip_y(s), slice(None), quarter(s[2], 0))
    relay_y = lambda s, r: (chip_x(s), slice(None), quarter(s[2], 1))
    relay_in = _Exchange([gw_in], [whole],
                         [((1, 0, 0), 0, relay_x, 0, relay_x), ((0, 1, 0), 0, relay_y, 0, relay_y)]
                         + [(SIBLING_MASK, 0, half_of(which), 0, half_of(which)) for which in (chip_own, chip_x, chip_y)],
                         in_place={0: 0})
    hc, (gw_in,) = _modulate_fwd(x[0], ctx[0], prm, tm, rider=relay_in)
    gw_in = _Exchange([gw_in], [whole], [(SIBLING_MASK, 0, half_of(chip_xy), 0, half_of(chip_xy))],
                      in_place={0: 0}).run("gather_w_in_pair")[0]
    wt_full = gw_in.reshape((N_CHIPS + 1) * ns, dm)
    assert wt_full.shape[0] >= n_pad
    u, (got_out,) = _mm_nt("in_proj", hc, wt_full, n_pad, tm_mm, tn_mm, F32, rider=_all_gather_chips([half_out]))
    fetched_out = _own_block(chip, half_out, got_out)
    (o_f, o_b), hg_saved, (swapped_out,) = _hg_scan_fwd(u, lb_full, w, *hg_counts,
                                                         rider=_sibling_swap([fetched_out]))
    w_out_full = _join_halves(ci, fetched_out, swapped_out, 1).reshape(di, dm)
    qk = _conv_fwd(u, conv_w9, conv_b, t_rows, c_rows, w, LANE)
    gate_b_row = jnp.pad(ml_gate_b.reshape(1, -1), ((0, 0), (0, LANE - 4 * n_ml)))
    (h_f, h_b), ml_saved = _ml_scan_fwd(qk, u, gate_b_row, w, n_ml, *ml_counts)
    y = _post_fwd(o_f, o_b, h_f, h_b, u, hg_norm_w, ml_norm_w, t_rows, w, n_hg, n_ml, tm)
    prm_out = _rows8([mod_x[2], ln_g, ln_b], dm)
    dz, dy, gx_direct, acc_out = _out_block(y, w_out_full, x[0], loss_target[0], prm_out, tm)

    d_w_out = _mm_tn("d_w_out", y, dz, _largest_divisor(di, 1024, LANE),
                     _largest_divisor(t_rows, 1024, SUBLANE_BF16))
    d_w_out4 = d_w_out.reshape(N_CHIPS, 2, di // (2 * N_CHIPS), dm)
    mine_out = lax.dynamic_index_in_dim(d_w_out4, ci, 1, keepdims=False)
    other_out = lax.dynamic_index_in_dim(d_w_out4, 1 - ci, 1, keepdims=False)
    (d_o, d_h, d_az, d_bo, d_bz, d_wa, d_wb), (got_out,) = _post_bwd(
        o_f, o_b, h_f, h_b, u, hg_norm_w, ml_norm_w, dy, t_rows, w, n_hg, n_ml, tm, rider=_sibling_swap([other_out]))
    pair_out = _sum_pair("rs_pair_sum_w_out", mine_out, got_out)
    (d_aq_f, d_aff, d_ai_f, d_lb_f), (d_aq_b, d_afb, d_ai_b, d_lb_b), (landed_out,) = _hg_scan_bwd(
        u, lb_full, hg_saved, d_o, w, *hg_counts, rider=_chip_scatter([pair_out]))
    half_g_out = _sum_chips("rs_chip_sum_w_out", landed_out, pair_out, chip)
    (d_qk_f, d_v_f, d_g_f, d_gb_f), (d_qk_b, d_v_b, d_g_b, d_gb_b), (sibling_out,) = _ml_scan_bwd(
        qk, u, gate_b_row, ml_saved, d_h, w, n_ml, *ml_counts, rider=_sibling_swap([half_g_out]))
    g_w_out = _join_halves(ci, half_g_out, sibling_out, 0)
    d_bqk, d_cw, d_cb = _conv_bwd(u, (d_qk_f, d_qk_b), conv_w9, conv_b, t_rows, c_rows, w, LANE)
    du = _assemble_du([(d_aq_f, d_aq_b), d_aff, d_afb, (d_ai_f, d_ai_b), d_az, d_bqk, (d_v_f, d_v_b), d_bo, d_bz],
                      (d_g_f, d_g_b), n_pad, tm)
    d_wt_in, d_wt_in_bf16 = _mm_tn("d_w_in", du, hc, tn_mm, tm_mm, with_bf16=True)

    delta, new_m, new_v = {}, {}, {}
    res, (got_in,) = _adamw(
        "adamw_w_out", w_out[0], g_w_out, m_w_out[0], v_w_out[0],
        rider=_Exchange([d_wt_in_bf16], [jax.ShapeDtypeStruct((n_pad, dm // 2), BF16)],
                        [(SIBLING_MASK, 0, lambda s, r: (slice(None), lanes_of(r[2])), 0, None)]))
    delta["w_out"], new_m["w_out"], new_v["w_out"] = (a[None] for a in res)
    pair_half = _sum_pair_lanes("rs_pair_sum_w_in", d_wt_in, got_in, ci)
    pair_in = jnp.stack([pair_half[k * ns:(k + 1) * ns] for k in range(N_CHIPS)])
    d_hc, (landed_in,) = _mm_acc("d_h", du, wt_full, tm_mm, tn_mm, rider=_chip_scatter([pair_in]))
    half_g_in = _sum_chips("rs_chip_sum_w_in", landed_in, pair_in, chip)
    (gx, acc_mod), _ = _modulate_bwd(x[0], ctx[0], d_hc, prm, gx_direct, tm)
    grad_x = gx[None]

    zero_row = jnp.zeros((dm,), F32)
    d_gb = jnp.concatenate([d_gb_f[:, 0:n_ml], d_gb_b[:, n_ml:2 * n_ml], d_gb_f[:, 2 * n_ml:3 * n_ml],
                            d_gb_b[:, 3 * n_ml:4 * n_ml], jnp.zeros((1, dm - 4 * n_ml), F32)], axis=1)
    rows = [acc_mod[0, 0], acc_mod[0, 1], acc_out[OUT_ROW_GATE],
            acc_mod[1, 0], acc_mod[1, 1], zero_row]
    rows += list(d_cw) + [d_cb[0], d_lb_f.reshape(dm), d_lb_b.reshape(dm),
                          jnp.concatenate([d_wa[0], d_wb[0]]), acc_out[OUT_ROW_LN_G], acc_out[OUT_ROW_LN_B],
                          acc_out[OUT_ROW_LOSS], d_gb[0], zero_row]
    ROW_CW, ROW_CB, ROW_LB, ROW_NORM, ROW_LN_G, ROW_LN_B, ROW_LOSS, ROW_GB = 6, 15, 16, 18, 19, 20, 21, 22
    small_rows = jnp.concatenate([r.reshape(dm) for r in rows]).reshape(len(rows), dm)
    g3 = _all_gather8(small_rows).run("gather_small_grads")[0]
    sums, totals = _sum_devices(g3, 3)
    loss = totals[ROW_LOSS, 0]
    dm16 = jnp.concatenate([g3[:, 0:3, :].reshape(N_DEV, 3 * dm), sums[3:6].reshape(1, 3 * dm),
                            jnp.zeros((16 - N_DEV - 1, 3 * dm), F32)])
    (g_w_mod, dc16), (sibling_g_in,) = _mod_bwd(a16, lax.dynamic_slice_in_dim(dm16, chip * nm, nm, 1), w_mod[0],
                                                tn_mod, rider=_sibling_swap([half_g_in]))
    g_wt_in = _join_halves(ci, half_g_in, sibling_g_in, 1)
    g4 = _all_gather8(jnp.pad(dc16[N_DEV:N_DEV + 1], ((0, 7), (0, 0)))).run("gather_c_ctx")[0]
    g_c_ctx = _c_ctx_grad(g4, jnp.broadcast_to(c_ctx[None], (8, dm)))[0]
    res, _ = _adamw("adamw_w_in", as_t(w_in), g_wt_in, as_t(m_w_in), as_t(v_w_in))
    delta["w_in"], new_m["w_in"], new_v["w_in"] = (jnp.transpose(a)[None] for a in res)
    res, _ = _adamw("adamw_w_mod", w_mod[0], g_w_mod, m_w_mod[0], v_w_mod[0])
    delta["w_mod"], new_m["w_mod"], new_v["w_mod"] = (a[None] for a in res)

    chip_cols = lambda a, width: lax.dynamic_slice_in_dim(a, chip * width, width, a.ndim - 1)
    grads = {
        "c_ctx": g_c_ctx,
        "w_mod": g_w_mod[None],
        "b_mod": sums[0:3].reshape(1, 3 * dm),
        "w_in": jnp.transpose(g_wt_in)[None],
        "conv_w": chip_cols(sums[ROW_CW:ROW_CW + 9].reshape(1, 3, 3, di), di // N_CHIPS),
        "conv_b": sums[ROW_CB][None],
        "hg_lb": chip_cols(sums[ROW_LB:ROW_LB + 2].reshape(2, 2, w), w // N_CHIPS),
        "ml_gate_b": sums[ROW_GB, 0:4 * n_ml].reshape(1, 4, n_ml),
        "hg_norm_w": sums[ROW_NORM, 0:w][None],
        "ml_norm_w": sums[ROW_NORM, w:2 * w][None],
        "w_out": g_w_out[None],
        "ln_g": sums[ROW_LN_G][None],
        "ln_b": sums[ROW_LN_B][None],
    }
    weights = dict(c_ctx=c_ctx, w_mod=w_mod, b_mod=b_mod, w_in=w_in, conv_w=conv_w, conv_b=conv_b, hg_lb=hg_lb,
                   ml_gate_b=ml_gate_b, hg_norm_w=hg_norm_w, ml_norm_w=ml_norm_w, w_out=w_out, ln_g=ln_g, ln_b=ln_b)
    mom1 = dict(c_ctx=m_c_ctx, w_mod=m_w_mod, b_mod=m_b_mod, w_in=m_w_in, conv_w=m_conv_w, conv_b=m_conv_b,
                hg_lb=m_hg_lb, ml_gate_b=m_ml_gate_b, hg_norm_w=m_hg_norm_w, ml_norm_w=m_ml_norm_w, w_out=m_w_out,
                ln_g=m_ln_g, ln_b=m_ln_b)
    mom2 = dict(c_ctx=v_c_ctx, w_mod=v_w_mod, b_mod=v_b_mod, w_in=v_w_in, conv_w=v_conv_w, conv_b=v_conv_b,
                hg_lb=v_hg_lb, ml_gate_b=v_ml_gate_b, hg_norm_w=v_hg_norm_w, ml_norm_w=v_ml_norm_w, w_out=v_w_out,
                ln_g=v_ln_g, ln_b=v_ln_b)
    names = list(weights)
    big = ("w_mod", "w_in", "w_out")
    small = [n for n in names if n not in big]

    small_shapes = [weights[n].shape for n in small]
    res, _ = _adamw("adamw_small", *(_pack([src[n] for n in small]) for src in (weights, grads, mom1, mom2)))
    for out, packed in zip((delta, new_m, new_v), res):
        for n, a in zip(small, _unpack(packed, small_shapes)):
            out[n] = a

    return (loss, grad_x, *[grads[n].reshape(weights[n].shape) for n in names], *[delta[n] for n in names],
            *[new_m[n] for n in names], *[new_v[n] for n in names])
```

```python
import functools
import math

import jax
import jax.numpy as jnp
from jax import lax
from jax.experimental import pallas as pl
from jax.experimental.pallas import tpu as pltpu

F32 = jnp.float32
BF16 = jnp.bfloat16
HIGHEST = lax.Precision.HIGHEST
MESH = pl.DeviceIdType.MESH

HG_CHUNK = 64
ML_CHUNK = 256
HG_CHUNKS_PER_STEP = 4
GRID_W = 64
HG_DK = 128
LANE = 128
SUBLANE_BF16 = 16
ALPHA = 2.0 ** 0.25
LN_EPS = 1e-5
NORM_EPS = 1e-6
ADAM_LR = 0.001
ADAM_B1 = 0.9
ADAM_B2 = 0.999
ADAM_EPS = 1e-08
ADAM_WD = 0.01
ADAM_STEP = 10
VMEM_LIMIT = 56 * 1024 * 1024
N_CHIPS = 4
N_DEV = 8


def _params(sem=None):
    return pltpu.CompilerParams(dimension_semantics=sem, vmem_limit_bytes=VMEM_LIMIT)


def _largest_divisor(n, cap, multiple=1):
    best = None
    for d in range(multiple, min(n, cap) + 1, multiple):
        if n % d == 0:
            best = d
    assert best is not None, (n, cap, multiple)
    return best


def _sigmoid(x):
    return jax.nn.sigmoid(x)


def _silu(x):
    return x * jax.nn.sigmoid(x)


def _dot(a, b, dims, precision=None):
    return lax.dot_general(a, b, (dims, ((), ())), precision=precision, preferred_element_type=F32)


def _nn(a, b, precision=None):
    return _dot(a, b, ((1,), (0,)), precision)


def _nt(a, b, precision=None):
    return _dot(a, b, ((1,), (1,)), precision)


def _tn(a, b, precision=None):
    return _dot(a, b, ((0,), (0,)), precision)


def _narrow(x):
    return x.astype(BF16)


@jax.custom_vjp
def _bnn(a, b):
    return _nn(_narrow(a), _narrow(b))


def _bnn_fwd(a, b):
    an, bn = _narrow(a), _narrow(b)
    return _nn(an, bn), (an, bn)


def _bnn_bwd(res, ct):
    an, bn = res
    ctn = _narrow(ct)
    return _nt(ctn, bn), _tn(an, ctn)


_bnn.defvjp(_bnn_fwd, _bnn_bwd)


@jax.custom_vjp
def _bnt(a, b):
    return _nt(_narrow(a), _narrow(b))


def _bnt_fwd(a, b):
    an, bn = _narrow(a), _narrow(b)
    return _nt(an, bn), (an, bn)


def _bnt_bwd(res, ct):
    an, bn = res
    ctn = _narrow(ct)
    return _nn(ctn, bn), _tn(ctn, an)


_bnt.defvjp(_bnt_fwd, _bnt_bwd)


@jax.custom_vjp
def _btn(a, b):
    return _tn(_narrow(a), _narrow(b))


def _btn_fwd(a, b):
    an, bn = _narrow(a), _narrow(b)
    return _tn(an, bn), (an, bn)


def _btn_bwd(res, ct):
    an, bn = res
    ctn = _narrow(ct)
    return _nt(bn, ctn), _nn(an, ctn)


_btn.defvjp(_btn_fwd, _btn_bwd)


def _visible(n, rev):
    r = lax.broadcasted_iota(jnp.int32, (n, n), 0)
    c = lax.broadcasted_iota(jnp.int32, (n, n), 1)
    return (r <= c) if rev else (r >= c)


def _mask_matmul(mask, x):
    mb = mask.astype(BF16)
    hi = x.astype(BF16)
    lo = (x - hi.astype(F32)).astype(BF16)
    return _nn(mb, hi) + _nn(mb, lo)


@functools.partial(jax.custom_vjp, nondiff_argnums=(1,))
def _cumulative(x, rev):
    return _mask_matmul(_visible(x.shape[0], rev), x)


def _cumulative_fwd(x, rev):
    return _cumulative(x, rev), None


def _cumulative_bwd(rev, _, ct):
    return (_mask_matmul(_visible(ct.shape[0], not rev), ct),)


_cumulative.defvjp(_cumulative_fwd, _cumulative_bwd)


def _hg_chunk(states, aq, af, ai, lb0, lb1, rev):
    n_heads = len(states)
    lb = _sigmoid(lb0 - lb1)
    f = lb + (1.0 - lb) * _sigmoid(af)
    g = jnp.log(f)
    k = 1.0 - f
    q = _silu(aq)
    chunk = aq.shape[0]
    vis = _visible(chunk, rev)
    b = _cumulative(g, rev)
    last = 0 if rev else chunk - 1
    b_end = b[last:last + 1]
    b_mid = b[chunk // 2:chunk // 2 + 1]
    q_inter = q * jnp.exp(b)
    q_intra = q * jnp.exp(b - b_mid)
    k_intra = k * jnp.exp(b_mid - b)
    k_dec = k * jnp.exp(b_end - b)
    e_end = jnp.exp(b_end)
    new_states, outs = [], []
    for h in range(n_heads):
        sl = slice(h * HG_DK, (h + 1) * HG_DK)
        s_t = states[h]
        scores = jnp.where(vis, _nt(q_intra[:, sl], k_intra[:, sl]), 0.0)
        outs.append(_nt(q_inter[:, sl], s_t) + _nn(scores, ai[:, sl]))
        new_states.append(e_end[:, sl] * s_t + _tn(ai[:, sl], k_dec[:, sl]))
    return new_states, jnp.concatenate(outs, axis=1)


def _ml_chunk(state, q, k, v, g, gb, rev, d):
    cms, nvs, mbs = state
    n_heads = len(cms)
    dh = q.shape[1] // n_heads
    ga = g + gb
    log_f_all = jax.nn.log_sigmoid(ga)
    chunk = q.shape[0]
    vis = _visible(chunk, rev)
    b_all = _cumulative(log_f_all, rev)
    last = 0 if rev else chunk - 1
    k = k * (dh ** -0.5)
    new_c, new_n, new_m, outs = [], [], [], []
    for h in range(n_heads):
        ci = d * n_heads + h
        cf = (2 + d) * n_heads + h
        sl = slice(h * dh, (h + 1) * dh)
        qh, kh, vh = q[:, sl], k[:, sl], v[:, sl]
        li = ga[:, ci:ci + 1]
        b = b_all[:, cf:cf + 1]
        m = mbs[h][:, 0:1]
        row = jnp.transpose(li - b)
        log_w = jnp.where(vis, b + row, -jnp.inf)
        m_inter = b + m
        m_t = jnp.maximum(m_inter, jnp.max(log_w, axis=-1, keepdims=True))
        w_inter = jnp.exp(m_inter - m_t)
        w_qk = jnp.exp(log_w - m_t) * _bnt(qh, kh)
        num = w_inter * _bnt(qh, cms[h]) + _bnn(w_qk, vh)
        den = w_inter * jnp.sum(qh * nvs[h], axis=-1, keepdims=True) + jnp.sum(w_qk, axis=-1, keepdims=True)
        outs.append(num / jnp.maximum(jnp.abs(den), jnp.exp(-m_t)))
        m_new = m_t[last:last + 1]
        b_end = b[last:last + 1]
        w_s = jnp.exp(b_end - b + li - m_new)
        decay = jnp.exp(b_end + m - m_new)
        new_c.append(decay * cms[h] + _btn(w_s * vh, kh))
        new_n.append(decay * nvs[h] + jnp.sum(w_s * kh, axis=0, keepdims=True))
        new_m.append(jnp.broadcast_to(m_new, (1, LANE)))
    return (new_c, new_n, new_m), jnp.concatenate(outs, axis=1)


def _post_fn(o_f, o_b, az, h_f, h_b, bo, bz, wa, wb, n_hg, n_ml):
    o = o_f + o_b
    parts = []
    for h in range(n_hg):
        s = o[:, h * HG_DK:(h + 1) * HG_DK]
        parts.append(s * lax.rsqrt(jnp.mean(s * s, axis=-1, keepdims=True) + NORM_EPS))
    y_a = jnp.concatenate(parts, axis=1) * wa * _silu(az)
    hh = h_f + h_b
    dh = hh.shape[1] // n_ml
    parts = []
    for h in range(n_ml):
        s = hh[:, h * dh:(h + 1) * dh]
        mu = jnp.mean(s, axis=-1, keepdims=True)
        sc = s - mu
        parts.append(sc * lax.rsqrt(jnp.mean(sc * sc, axis=-1, keepdims=True) + NORM_EPS))
    y_b = jnp.concatenate(parts, axis=1) * wb * _sigmoid(bo) * _silu(bz)
    return jnp.concatenate([y_a, y_b], axis=1)


def _chip_of(dev):
    return 2 * dev[0] + dev[1]


def _index_of(dev):
    return 4 * dev[0] + 2 * dev[1] + dev[2]


class _Exchange:
    def __init__(self, srcs, out_shapes, transfers, local_copies=(), in_place=None):
        self.srcs, self.out_shapes = list(srcs), list(out_shapes)
        self.transfers, self.local_copies = list(transfers), list(local_copies)
        self.in_place = dict(in_place or {})

    def scratch(self):
        return [pltpu.SemaphoreType.DMA((len(self.transfers),)), pltpu.SemaphoreType.DMA((len(self.transfers),)),
                pltpu.SemaphoreType.DMA((max(len(self.local_copies), 1),))]

    def copies(self, ins, outs, send_sems, recv_sems, local_sems):
        me = (lax.axis_index("x"), lax.axis_index("y"), lax.axis_index("c"))

        def pick(ref, fn, *who):
            return ref if fn is None else ref.at[fn(*who)]

        sends, recvs, locs = [], [], []
        for t, (mask, si, sfn, di, dfn) in enumerate(self.transfers):
            peer = tuple(1 - p if flip else p for p, flip in zip(me, mask))
            sends.append(pltpu.make_async_remote_copy(
                src_ref=pick(ins[si], sfn, me, peer), dst_ref=pick(outs[di], dfn, me, peer),
                send_sem=send_sems.at[t], recv_sem=recv_sems.at[t], device_id=peer, device_id_type=MESH))
            landing = pick(outs[di], dfn, peer, me)
            recvs.append(pltpu.make_async_remote_copy(
                src_ref=landing, dst_ref=landing,
                send_sem=send_sems.at[t], recv_sem=recv_sems.at[t], device_id=peer, device_id_type=MESH))
        for l, (si, sfn, di, dfn) in enumerate(self.local_copies):
            locs.append(pltpu.make_async_copy(pick(ins[si], sfn, me), pick(outs[di], dfn, me), local_sems.at[l]))

        def start():
            for cp in locs + sends:
                cp.start()

        def wait():
            for cp in recvs:
                cp.wait_recv()
            for cp in sends:
                cp.wait_send()
            for cp in locs:
                cp.wait()

        return start, wait

    def run(self, name):
        n_in, n_out = len(self.srcs), len(self.out_shapes)

        def body(*refs):
            start, wait = self.copies(refs[:n_in], refs[n_in:n_in + n_out], *refs[n_in + n_out:])
            start()
            wait()

        hbm = pl.BlockSpec(memory_space=pltpu.HBM)
        return pl.pallas_call(
            body, name=name, out_shape=tuple(self.out_shapes), in_specs=[hbm] * n_in,
            out_specs=tuple([hbm] * n_out), scratch_shapes=self.scratch(), input_output_aliases=self.in_place,
        )(*self.srcs)


def _call(body, operands, *, name, grid, in_specs, out_specs, out_shape, scratch_shapes=(), sem=None, rider=None):
    out_specs, out_shape, scratch_shapes = list(out_specs), list(out_shape), list(scratch_shapes)
    if rider is None:
        res = pl.pallas_call(
            body, name=name, grid=grid, in_specs=list(in_specs), out_specs=tuple(out_specs),
            out_shape=tuple(out_shape), scratch_shapes=scratch_shapes, compiler_params=_params(sem),
        )(*operands)
        return list(res), []
    counts = (len(in_specs), len(rider.srcs), len(out_specs), len(rider.out_shapes), len(scratch_shapes), 3)

    def full(*refs):
        groups, pos = [], 0
        for k in counts:
            groups.append(refs[pos:pos + k])
            pos += k
        own_in, ex_in, own_out, ex_out, own_scr, ex_scr = groups
        ids = [pl.program_id(a) for a in range(len(grid))]
        first = functools.reduce(jnp.logical_and, [i == 0 for i in ids])
        last = functools.reduce(jnp.logical_and, [i == g - 1 for i, g in zip(ids, grid)])
        start, wait = rider.copies(ex_in, ex_out, *ex_scr)
        pl.when(first)(start)
        body(*own_in, *own_out, *own_scr)
        pl.when(last)(wait)

    hbm = pl.BlockSpec(memory_space=pltpu.HBM)
    res = pl.pallas_call(
        full, name=name, grid=grid, in_specs=list(in_specs) + [hbm] * counts[1],
        out_specs=tuple(out_specs + [hbm] * counts[3]), out_shape=tuple(out_shape + rider.out_shapes),
        scratch_shapes=scratch_shapes + rider.scratch(), compiler_params=_params(("arbitrary",) * len(grid)),
        input_output_aliases={counts[0] + i: counts[2] + o for i, o in rider.in_place.items()},
    )(*operands, *rider.srcs)
    return list(res[:counts[2]]), list(res[counts[2]:])


ALL_MASKS = [(mx, my, mc) for mx in (0, 1) for my in (0, 1) for mc in (0, 1)][1:]
CHIP_MASKS = [(1, 0, 0), (0, 1, 0), (1, 1, 0)]
SIBLING_MASK = (0, 0, 1)


def _all_gather8(v):
    out = jax.ShapeDtypeStruct((N_DEV,) + v.shape, v.dtype)
    slot = lambda sender, receiver: _index_of(sender)
    transfers = [(mask, 0, None, 0, slot) for mask in ALL_MASKS]
    return _Exchange([v], [out], transfers, [(0, None, 0, lambda me: _index_of(me))])


def _all_gather_chips(arrays):
    outs = [jax.ShapeDtypeStruct((N_CHIPS,) + a.shape, a.dtype) for a in arrays]
    slot = lambda sender, receiver: _chip_of(sender)
    return _Exchange(arrays, outs, [(mask, i, None, i, slot) for i in range(len(arrays)) for mask in CHIP_MASKS])


def _sibling_swap(arrays):
    outs = [jax.ShapeDtypeStruct(a.shape, a.dtype) for a in arrays]
    return _Exchange(arrays, outs, [(SIBLING_MASK, i, None, i, None) for i in range(len(arrays))])


def _chip_scatter(arrays):
    outs = [jax.ShapeDtypeStruct(a.shape, a.dtype) for a in arrays]
    transfers = [(mask, i, lambda s, r: _chip_of(r), i, lambda s, r: _chip_of(s))
                 for i in range(len(arrays)) for mask in CHIP_MASKS]
    return _Exchange(arrays, outs, transfers)


def _own_block(chip, own, blocks):
    sel = (lax.broadcasted_iota(jnp.int32, (N_CHIPS,) + (1,) * (blocks.ndim - 1), 0) == chip)
    return jnp.where(sel, own if own.ndim == blocks.ndim else own[None], blocks)


def _join_halves(ci, mine, other, axis):
    return jnp.where(ci == 0, jnp.concatenate([mine, other], axis=axis), jnp.concatenate([other, mine], axis=axis))


def _mm_nt(name, a, b, n, tm, tn, out_dtype, rider=None):
    m, k = a.shape

    def body(a_ref, b_ref, o_ref):
        o_ref[...] = _nt(a_ref[...], b_ref[...]).astype(out_dtype)

    (out,), rode = _call(
        body, (a, b), name=name, grid=(n // tn, m // tm),
        in_specs=[pl.BlockSpec((tm, k), lambda j, i: (i, 0)), pl.BlockSpec((tn, k), lambda j, i: (j, 0))],
        out_specs=[pl.BlockSpec((tm, tn), lambda j, i: (i, j))],
        out_shape=[jax.ShapeDtypeStruct((m, n), out_dtype)], sem=("parallel", "parallel"), rider=rider)
    return out, rode


def _mm_acc(name, a, b, tm, tk, rider=None):
    m, kc = a.shape
    n = b.shape[1]

    def body(a_ref, b_ref, o_ref):
        @pl.when(pl.program_id(1) == 0)
        def _():
            o_ref[...] = jnp.zeros_like(o_ref)
        o_ref[...] += _nn(a_ref[...], b_ref[...])

    (out,), rode = _call(
        body, (a, b), name=name, grid=(m // tm, kc // tk),
        in_specs=[pl.BlockSpec((tm, tk), lambda i, kk: (i, kk)), pl.BlockSpec((tk, n), lambda i, kk: (kk, 0))],
        out_specs=[pl.BlockSpec((tm, n), lambda i, kk: (i, 0))],
        out_shape=[jax.ShapeDtypeStruct((m, n), F32)], sem=("parallel", "arbitrary"), rider=rider)
    return out, rode


def _mm_tn(name, a, b, tm, tk, with_bf16=False):
    kr, m = a.shape
    n = b.shape[1]
    steps_k = kr // tk

    def body(a_ref, b_ref, o_ref, *narrow):
        @pl.when(pl.program_id(1) == 0)
        def _():
            o_ref[...] = jnp.zeros_like(o_ref)
        o_ref[...] += _tn(a_ref[...], b_ref[...])
        if with_bf16:
            @pl.when(pl.program_id(1) == steps_k - 1)
            def _():
                narrow[0][...] = o_ref[...].astype(BF16)

    out_spec = pl.BlockSpec((tm, n), lambda i, kk: (i, 0))
    res = pl.pallas_call(
        body, name=name, grid=(m // tm, steps_k),
        in_specs=[pl.BlockSpec((tk, tm), lambda i, kk: (kk, i)), pl.BlockSpec((tk, n), lambda i, kk: (kk, 0))],
        out_specs=(out_spec,) * (2 if with_bf16 else 1),
        out_shape=(jax.ShapeDtypeStruct((m, n), F32),) + ((jax.ShapeDtypeStruct((m, n), BF16),) if with_bf16 else ()),
        compiler_params=_params(("parallel", "arbitrary")),
    )(a, b)
    return res if with_bf16 else res[0]


def _modulate_fwd(x, ctx, prm, tm, rider=None):
    t_rows, dm = x.shape
    lat = t_rows // tm
    r = t_rows + ctx.shape[0]

    def body(x_ref, c_ref, p_ref, h_ref):
        xv = jnp.where(pl.program_id(0) >= lat, c_ref[...], x_ref[...])
        mu = jnp.mean(xv, axis=-1, keepdims=True)
        xm = xv - mu
        n = xm * lax.rsqrt(jnp.mean(xm * xm, axis=-1, keepdims=True) + LN_EPS)
        h_ref[...] = (n * (1.0 + p_ref[0, 1:2, :]) + p_ref[0, 0:1, :]).astype(BF16)

    (h,), rode = _call(
        body, (x, ctx, prm), name="modulate_fwd", grid=(r // tm,),
        in_specs=[pl.BlockSpec((tm, dm), lambda i: (jnp.minimum(i, lat - 1), 0)),
                  pl.BlockSpec((tm, dm), lambda i: (jnp.maximum(i - lat, 0), 0)),
                  pl.BlockSpec((1, 8, dm), lambda i: ((i >= lat).astype(jnp.int32), 0, 0))],
        out_specs=[pl.BlockSpec((tm, dm), lambda i: (i, 0))],
        out_shape=[jax.ShapeDtypeStruct((r, dm), BF16)], sem=("parallel",), rider=rider)
    return h, rode


def _modulate_bwd(x, ctx, dh, prm, gx_direct, tm, rider=None):
    t_rows, dm = x.shape
    lat, n_ct = t_rows // tm, ctx.shape[0] // tm
    is_ctx = lambda i: i < n_ct
    cls = lambda i: is_ctx(i).astype(jnp.int32)
    lat_tile = lambda i: (jnp.maximum(i - n_ct, 0), 0)

    def body(x_ref, c_ref, dh_ref, p_ref, gd_ref, gx_ref, acc_ref):
        i = pl.program_id(0)

        @pl.when((i == 0) | (i == n_ct))
        def _():
            acc_ref[...] = jnp.zeros_like(acc_ref)

        x = jnp.where(is_ctx(i), c_ref[...], x_ref[...])
        dh_v = dh_ref[...]
        mu = jnp.mean(x, axis=-1, keepdims=True)
        xm = x - mu
        rstd = lax.rsqrt(jnp.mean(xm * xm, axis=-1, keepdims=True) + LN_EPS)
        n = xm * rstd
        acc_ref[0, 0:1, :] += jnp.sum(dh_v, axis=0, keepdims=True)
        acc_ref[0, 1:2, :] += jnp.sum(dh_v * n, axis=0, keepdims=True)
        dn = dh_v * (1.0 + p_ref[0, 1:2, :])
        dx = rstd * (dn - jnp.mean(dn, axis=-1, keepdims=True) - n * jnp.mean(dn * n, axis=-1, keepdims=True))
        gx_ref[...] = dx + gd_ref[...]

    return _call(
        body, (x, ctx, dh, prm, gx_direct), name="modulate_bwd", grid=(n_ct + lat,),
        in_specs=[pl.BlockSpec((tm, dm), lat_tile),
                  pl.BlockSpec((tm, dm), lambda i: (jnp.minimum(i, n_ct - 1), 0)),
                  pl.BlockSpec((tm, dm), lambda i: (jnp.where(is_ctx(i), lat + i, i - n_ct), 0)),
                  pl.BlockSpec((1, 8, dm), lambda i: (cls(i), 0, 0)),
                  pl.BlockSpec((tm, dm), lat_tile)],
        out_specs=(pl.BlockSpec((tm, dm), lat_tile), pl.BlockSpec((1, 8, dm), lambda i: (cls(i), 0, 0))),
        out_shape=(jax.ShapeDtypeStruct((t_rows, dm), F32), jax.ShapeDtypeStruct((2, 8, dm), F32)),
        sem=("arbitrary",), rider=rider)


def _conv_parts(t_rows, c_rows):
    return ((0, t_rows, t_rows // GRID_W, GRID_W), (t_rows, c_rows, 1, c_rows))


def _col_shifts(x2, rows_g, width_g):
    n, ct = x2.shape
    col = lax.broadcasted_iota(jnp.int32, (width_g, ct), 0)
    as_grid = lambda a: a.reshape(rows_g, width_g, ct)
    left = as_grid(pltpu.roll(x2, 1, 0)) * (col >= 1).astype(F32)
    right = as_grid(pltpu.roll(x2, n - 1, 0)) * (col <= width_g - 2).astype(F32)
    return [left, as_grid(x2), right]


CONV_BLOCK_ROWS = 4


def _conv_blocks(t_rows, c_rows):
    for t0, _, rows_g, width_g in _conv_parts(t_rows, c_rows):
        nb = min(CONV_BLOCK_ROWS, rows_g)
        assert rows_g % nb == 0
        for g0 in range(0, rows_g, nb):
            yield t0, rows_g, width_g, g0, nb


def _slab(ref, t0, rows_g, width_g, g0, nb):
    if rows_g == 1:
        return ref[t0:t0 + width_g, :]
    lo, hi = max(g0 - 1, 0), min(g0 + nb + 1, rows_g)
    parts = [ref[t0 + lo * width_g:t0 + hi * width_g, :]]
    zero = jnp.zeros((width_g, ref.shape[1]), F32)
    if g0 == 0:
        parts.insert(0, zero)
    if g0 + nb == rows_g:
        parts.append(zero)
    return jnp.concatenate(parts, axis=0)


def _conv_taps(cols, w_ref, nb, flip):
    one_row = cols[0].shape[0] == nb
    acc = None
    for a in range(3):
        if one_row and a != 1:
            continue
        for b in range(3):
            tap = (2 - a) * 3 + (2 - b) if flip else a * 3 + b
            term = (cols[b] if one_row else cols[b][a:a + nb]) * w_ref[tap:tap + 1, :]
            acc = term if acc is None else acc + term
    return acc


def _conv_fwd(u, conv_w9, conv_b, t_rows, c_rows, w, ct):
    r = u.shape[0]
    base = 5 * w // ct

    def body(x_ref, w_ref, b_ref, o_ref):
        for t0, rows_g, width_g, g0, nb in _conv_blocks(t_rows, c_rows):
            slab = _slab(x_ref, t0, rows_g, width_g, g0, nb)
            cols = _col_shifts(slab, slab.shape[0] // width_g, width_g)
            pre = _conv_taps(cols, w_ref, nb, False) + b_ref[...]
            o_ref[t0 + g0 * width_g:t0 + (g0 + nb) * width_g, :] = _silu(pre).reshape(nb * width_g, ct)

    return pl.pallas_call(
        body, name="conv_fwd", grid=(2 * w // ct,),
        in_specs=[pl.BlockSpec((r, ct), lambda i: (0, base + i)), pl.BlockSpec((9, ct), lambda i: (0, i)),
                  pl.BlockSpec((1, ct), lambda i: (0, i))],
        out_specs=pl.BlockSpec((r, ct), lambda i: (0, i)),
        out_shape=jax.ShapeDtypeStruct((r, 2 * w), F32),
        compiler_params=_params(("parallel",)),
    )(u, conv_w9, conv_b)


def _conv_bwd(u, dqk_pair, conv_w9, conv_b, t_rows, c_rows, w, ct):
    r = u.shape[0]
    base = 5 * w // ct

    def body(x_ref, d1_ref, d2_ref, w_ref, b_ref, dx_ref, dw_ref, db_ref, dpre_ref):
        dw = [jnp.zeros((1, ct), F32) for _ in range(9)]
        db = jnp.zeros((1, ct), F32)
        for t0, rows_g, width_g, g0, nb in _conv_blocks(t_rows, c_rows):
            rows = slice(t0 + g0 * width_g, t0 + (g0 + nb) * width_g)
            slab = _slab(x_ref, t0, rows_g, width_g, g0, nb)
            cols = _col_shifts(slab, slab.shape[0] // width_g, width_g)
            pre = _conv_taps(cols, w_ref, nb, False) + b_ref[...]
            sg = _sigmoid(pre)
            dpre = (d1_ref[rows, :] + d2_ref[rows, :]).reshape(pre.shape) * (sg * (1.0 + pre * (1.0 - sg)))
            dpre_ref[rows, :] = dpre.reshape(nb * width_g, ct)
            db = db + jnp.sum(jnp.sum(dpre, axis=0), axis=0, keepdims=True)
            for a in range(3):
                if rows_g == 1 and a != 1:
                    continue
                for b in range(3):
                    moved = cols[b] if rows_g == 1 else cols[b][a:a + nb]
                    dw[a * 3 + b] = dw[a * 3 + b] + jnp.sum(jnp.sum(moved * dpre, axis=0), axis=0, keepdims=True)
        for t0, rows_g, width_g, g0, nb in _conv_blocks(t_rows, c_rows):
            slab = _slab(dpre_ref, t0, rows_g, width_g, g0, nb)
            cols = _col_shifts(slab, slab.shape[0] // width_g, width_g)
            dx_ref[t0 + g0 * width_g:t0 + (g0 + nb) * width_g, :] = _conv_taps(cols, w_ref, nb, True).reshape(
                nb * width_g, ct).astype(BF16)
        for tap in range(9):
            dw_ref[tap:tap + 1, :] = dw[tap]
        db_ref[...] = db

    return pl.pallas_call(
        body, name="conv_bwd", grid=(2 * w // ct,),
        in_specs=[pl.BlockSpec((r, ct), lambda i: (0, base + i)), pl.BlockSpec((r, ct), lambda i: (0, i)),
                  pl.BlockSpec((r, ct), lambda i: (0, i)),
                  pl.BlockSpec((9, ct), lambda i: (0, i)), pl.BlockSpec((1, ct), lambda i: (0, i))],
        out_specs=(pl.BlockSpec((r, ct), lambda i: (0, i)), pl.BlockSpec((9, ct), lambda i: (0, i)),
                   pl.BlockSpec((1, ct), lambda i: (0, i))),
        out_shape=(jax.ShapeDtypeStruct((r, 2 * w), BF16), jax.ShapeDtypeStruct((9, 2 * w), F32),
                   jax.ShapeDtypeStruct((1, 2 * w), F32)),
        scratch_shapes=[pltpu.VMEM((r, ct), F32)],
        compiler_params=_params(("parallel",)),
    )(u, dqk_pair[0], dqk_pair[1], conv_w9, conv_b)


def _assemble_du(groups, gates, n_pad, tm):
    flat, layout = [], []
    for entry in list(groups) + [gates]:
        parts = entry if isinstance(entry, (tuple, list)) else (entry,)
        layout.append((len(flat), len(parts), parts[0].shape[1]))
        flat += list(parts)
    r = flat[0].shape[0]

    def body(*refs):
        o_ref = refs[-1]
        col = 0
        for first, count, width in layout:
            val = refs[first][...]
            for extra in range(1, count):
                val = val.astype(F32) + refs[first + extra][...].astype(F32)
            o_ref[:, col:col + width] = val.astype(BF16)
            col += width
        assert col == n_pad

    return pl.pallas_call(
        body, name="assemble_du", grid=(r // tm,),
        in_specs=[pl.BlockSpec((tm, a.shape[1]), lambda i: (i, 0)) for a in flat],
        out_specs=pl.BlockSpec((tm, n_pad), lambda i: (i, 0)),
        out_shape=jax.ShapeDtypeStruct((r, n_pad), BF16),
        compiler_params=_params(("parallel",)),
    )(*flat)


def _scan_order(n_lat, n_ctx, rev):
    n = n_lat + n_ctx
    if rev:
        return lambda j: n - 1 - j
    return lambda j: (j + n_lat) % n


DIRS = (False, True)


def _hg_scan_fwd(u, lb_full, w, n_lat, n_ctx, chunk, rider=None):
    r = u.shape[0]
    n_heads = w // HG_DK
    sub = HG_CHUNKS_PER_STEP if n_lat % HG_CHUNKS_PER_STEP == 0 and n_ctx % HG_CHUNKS_PER_STEP == 0 else 1
    n_steps = (n_lat + n_ctx) // sub
    nat = [_scan_order(n_lat // sub, n_ctx // sub, rev) for rev in DIRS]
    rows = sub * chunk

    def body(*refs):
        ins, outs, scratch = refs[:8], refs[8:12], refs[12:]

        @pl.when(pl.program_id(0) == 0)
        def _():
            for s_ref in scratch:
                s_ref[...] = jnp.zeros_like(s_ref)

        for d, rev in enumerate(DIRS):
            aq, af, ai, lb_ref = ins[4 * d:4 * d + 4]
            o_ref, save_ref = outs[2 * d:2 * d + 2]
            state = [scratch[d][h] for h in range(n_heads)]
            for p in range(sub):
                sl = slice((sub - 1 - p if rev else p) * chunk, (sub - p if rev else p + 1) * chunk)
                for h in range(n_heads):
                    save_ref[0, p, h] = state[h]
                state, o = _hg_chunk(state, aq[sl, :], af[sl, :], ai[sl, :], lb_ref[0, 0:1, :], lb_ref[0, 1:2, :], rev)
                o_ref[sl, :] = o
            for h in range(n_heads):
                scratch[d][h] = state[h]

    in_specs, out_specs, out_shape = [], [], []
    for d in range(2):
        in_specs += [pl.BlockSpec((rows, w), lambda j, d=d: (nat[d](j), 0)),
                     pl.BlockSpec((rows, w), lambda j, d=d: (nat[d](j), 1 + d)),
                     pl.BlockSpec((rows, w), lambda j, d=d: (nat[d](j), 3)),
                     pl.BlockSpec((1, 2, w), lambda j, d=d: (d, 0, 0))]
        out_specs += [pl.BlockSpec((rows, w), lambda j, d=d: (nat[d](j), 0)),
                      pl.BlockSpec((1, sub, n_heads, HG_DK, HG_DK), lambda j: (j, 0, 0, 0, 0))]
        out_shape += [jax.ShapeDtypeStruct((r, w), F32),
                      jax.ShapeDtypeStruct((n_steps, sub, n_heads, HG_DK, HG_DK), F32)]
    (o_f, s_f, o_b, s_b), rode = _call(
        body, (u, u, u, lb_full, u, u, u, lb_full), name="hg_scan_fwd", grid=(n_steps,), in_specs=in_specs,
        out_specs=out_specs, out_shape=out_shape, scratch_shapes=[pltpu.VMEM((n_heads, HG_DK, HG_DK), F32)] * 2,
        sem=("arbitrary",), rider=rider)
    return (o_f, o_b), (s_f, s_b), rode


def _hg_scan_bwd(u, lb_full, saved, d_o, w, n_lat, n_ctx, chunk, rider=None):
    r = u.shape[0]
    n_heads = w // HG_DK
    n_steps, sub = saved[0].shape[0], saved[0].shape[1]
    n_lat_s = n_lat // sub
    step = lambda jj: n_steps - 1 - jj
    nat = [(lambda jj, o=_scan_order(n_lat_s, n_ctx // sub, rev): o(step(jj))) for rev in DIRS]
    rows = sub * chunk

    def body(*refs):
        ins, outs, scratch = refs[:12], refs[12:20], refs[20:]
        jj = pl.program_id(0)

        @pl.when(jj == 0)
        def _():
            for d in range(2):
                scratch[d][...] = jnp.zeros_like(scratch[d])
                outs[4 * d + 3][...] = jnp.zeros_like(outs[4 * d + 3])

        for d, rev in enumerate(DIRS):
            aq, af, ai, lb_ref, save_ref, do_ref = ins[6 * d:6 * d + 6]
            daq_ref, daf_ref, dai_ref, dlb_ref = outs[4 * d:4 * d + 4]
            f = lambda st, a, b, c, l0, l1, rev=rev: _hg_chunk(st, a, b, c, l0, l1, rev)
            latent = (nat[d](jj) < n_lat_s).astype(F32)
            d_state = [scratch[d][h] for h in range(n_heads)]
            for p in reversed(range(sub)):
                sl = slice((sub - 1 - p if rev else p) * chunk, (sub - p if rev else p + 1) * chunk)
                _, vjp = jax.vjp(f, [save_ref[0, p, h] for h in range(n_heads)], aq[sl, :], af[sl, :], ai[sl, :],
                                 lb_ref[0, 0:1, :], lb_ref[0, 1:2, :])
                d_state, daq, daf, dai, dl0, dl1 = vjp((d_state, do_ref[sl, :] * latent))
                daq_ref[sl, :] = daq.astype(BF16)
                daf_ref[sl, :] = daf.astype(BF16)
                dai_ref[sl, :] = dai.astype(BF16)
                dlb_ref[0:1, :] += dl0
                dlb_ref[1:2, :] += dl1
            for h in range(n_heads):
                scratch[d][h] = d_state[h]

    in_specs, out_specs, out_shape, operands = [], [], [], []
    for d in range(2):
        row = lambda jj, d=d: (nat[d](jj), 0)
        in_specs += [pl.BlockSpec((rows, w), row),
                     pl.BlockSpec((rows, w), lambda jj, d=d: (nat[d](jj), 1 + d)),
                     pl.BlockSpec((rows, w), lambda jj, d=d: (nat[d](jj), 3)),
                     pl.BlockSpec((1, 2, w), lambda jj, d=d: (d, 0, 0)),
                     pl.BlockSpec((1, sub, n_heads, HG_DK, HG_DK), lambda jj: (step(jj), 0, 0, 0, 0)),
                     pl.BlockSpec((rows, w), lambda jj, d=d: (jnp.minimum(nat[d](jj), n_lat_s - 1), 0))]
        operands += [u, u, u, lb_full, saved[d], d_o]
        out_specs += [pl.BlockSpec((rows, w), row)] * 3 + [pl.BlockSpec((2, w), lambda jj: (0, 0))]
        out_shape += [jax.ShapeDtypeStruct((r, w), BF16)] * 3 + [jax.ShapeDtypeStruct((2, w), F32)]
    res, rode = _call(
        body, operands, name="hg_scan_bwd", grid=(n_steps,), in_specs=in_specs, out_specs=out_specs,
        out_shape=out_shape, scratch_shapes=[pltpu.VMEM((n_heads, HG_DK, HG_DK), F32)] * 2,
        sem=("arbitrary",), rider=rider)
    return res[0:4], res[4:8], rode


def _ml_state_shapes(n_chunks, n_heads, dh):
    return (jax.ShapeDtypeStruct((n_chunks, n_heads, dh, dh), F32),
            jax.ShapeDtypeStruct((n_chunks, n_heads, 1, dh), F32),
            jax.ShapeDtypeStruct((n_chunks, n_heads, 1, LANE), F32))


def _ml_state_specs(n_heads, dh, index):
    return (pl.BlockSpec((1, n_heads, dh, dh), lambda j: (index(j), 0, 0, 0)),
            pl.BlockSpec((1, n_heads, 1, dh), lambda j: (index(j), 0, 0, 0)),
            pl.BlockSpec((1, n_heads, 1, LANE), lambda j: (index(j), 0, 0, 0)))


def _ml_state_scratch(n_heads, dh):
    return [pltpu.VMEM((n_heads, dh, dh), F32), pltpu.VMEM((n_heads, 1, dh), F32), pltpu.VMEM((n_heads, 1, LANE), F32)]


def _ml_scan_fwd(qk, u, gate_b, w, n_heads, n_lat, n_ctx, chunk):
    r = u.shape[0]
    dh = w // n_heads
    n_chunks = n_lat + n_ctx
    nat = [_scan_order(n_lat, n_ctx, rev) for rev in DIRS]

    def body(*refs):
        ins, outs, scratch = refs[:10], refs[10:18], refs[18:]

        @pl.when(pl.program_id(0) == 0)
        def _():
            for s_ref in scratch:
                s_ref[...] = jnp.zeros_like(s_ref)

        results = []
        for d, rev in enumerate(DIRS):
            q, k, v, g, gb = ins[5 * d:5 * d + 5]
            state = tuple([ref[h] for h in range(n_heads)] for ref in scratch[3 * d:3 * d + 3])
            results.append((state, _ml_chunk(state, q[...], k[...], v[...], g[...], gb[...], rev, d)))
        for d, (state, (new, o)) in enumerate(results):
            outs[4 * d][...] = o
            for part in range(3):
                for h in range(n_heads):
                    outs[4 * d + 1 + part][0, h] = state[part][h]
                    scratch[3 * d + part][h] = new[part][h]

    in_specs, out_specs, out_shape = [], [], []
    for d in range(2):
        in_specs += [pl.BlockSpec((chunk,w), lambda j, d=d: (nat[d](j), 0)),
                     pl.BlockSpec((chunk,w), lambda j, d=d: (nat[d](j), 1)),
                     pl.BlockSpec((chunk,w), lambda j, d=d: (nat[d](j), 7)),
                     pl.BlockSpec((chunk,LANE), lambda j, d=d: (nat[d](j), 10 * w // LANE)),
                     pl.BlockSpec((1, LANE), lambda j: (0, 0))]
        out_specs += [pl.BlockSpec((chunk,w), lambda j, d=d: (nat[d](j), 0))]
        out_specs += list(_ml_state_specs(n_heads, dh, lambda j: j))
        out_shape += [jax.ShapeDtypeStruct((r, w), F32)] + list(_ml_state_shapes(n_chunks, n_heads, dh))
    res = pl.pallas_call(
        body, name="ml_scan_fwd", grid=(n_chunks,), in_specs=in_specs, out_specs=tuple(out_specs),
        out_shape=tuple(out_shape), scratch_shapes=_ml_state_scratch(n_heads, dh) * 2,
        compiler_params=_params(("arbitrary",)),
    )(qk, qk, u, u, gate_b, qk, qk, u, u, gate_b)
    return (res[0], res[4]), (res[1:4], res[5:8])


def _ml_scan_bwd(qk, u, gate_b, saved, d_h, w, n_heads, n_lat, n_ctx, chunk, rider=None):
    r = u.shape[0]
    dh = w // n_heads
    n_chunks = n_lat + n_ctx
    step = lambda jj: n_chunks - 1 - jj
    nat = [(lambda jj, o=_scan_order(n_lat, n_ctx, rev): o(step(jj))) for rev in DIRS]

    def body(*refs):
        ins, outs, scratch = refs[:18], refs[18:26], refs[26:]
        jj = pl.program_id(0)

        @pl.when(jj == 0)
        def _():
            for s_ref in scratch:
                s_ref[...] = jnp.zeros_like(s_ref)
            for d in range(2):
                outs[4 * d + 3][...] = jnp.zeros_like(outs[4 * d + 3])

        results = []
        for d, rev in enumerate(DIRS):
            q, k, v, g, gb, sc, sn, sm, dh_ref = ins[9 * d:9 * d + 9]
            state = tuple([ref[0, h] for h in range(n_heads)] for ref in (sc, sn, sm))
            f = lambda st, a, b, c, gg, bb, rev=rev, d=d: _ml_chunk(st, a, b, c, gg, bb, rev, d)
            _, vjp = jax.vjp(f, state, q[...], k[...], v[...], g[...], gb[...])
            d_state = tuple([ref[h] for h in range(n_heads)] for ref in scratch[3 * d:3 * d + 3])
            d_out = dh_ref[...] * (nat[d](jj) < n_lat).astype(F32)
            results.append(vjp((d_state, d_out)))
        for d, (d_state, dq, dk, dv, dg, dgb) in enumerate(results):
            dqk_ref, dv_ref, dg_ref, dgb_ref = outs[4 * d:4 * d + 4]
            for part in range(3):
                for h in range(n_heads):
                    scratch[3 * d + part][h] = d_state[part][h]
            dqk_ref[:, 0:w] = dq
            dqk_ref[:, w:2 * w] = dk
            dv_ref[...] = dv.astype(BF16)
            dg_ref[...] = dg
            dgb_ref[...] += dgb

    in_specs, out_specs, out_shape, operands = [], [], [], []
    for d in range(2):
        row = lambda jj, d=d: (nat[d](jj), 0)
        in_specs += [pl.BlockSpec((chunk,w), row), pl.BlockSpec((chunk,w), lambda jj, d=d: (nat[d](jj), 1)),
                     pl.BlockSpec((chunk,w), lambda jj, d=d: (nat[d](jj), 7)),
                     pl.BlockSpec((chunk,LANE), lambda jj, d=d: (nat[d](jj), 10 * w // LANE)),
                     pl.BlockSpec((1, LANE), lambda jj: (0, 0))]
        in_specs += list(_ml_state_specs(n_heads, dh, step))
        in_specs += [pl.BlockSpec((chunk,w), lambda jj, d=d: (jnp.minimum(nat[d](jj), n_lat - 1), 0))]
        operands += [qk, qk, u, u, gate_b, *saved[d], d_h]
        out_specs += [pl.BlockSpec((chunk,2 * w), row), pl.BlockSpec((chunk,w), row),
                      pl.BlockSpec((chunk,LANE), row), pl.BlockSpec((1, LANE), lambda jj: (0, 0))]
        out_shape += [jax.ShapeDtypeStruct((r, 2 * w), F32), jax.ShapeDtypeStruct((r, w), BF16),
                      jax.ShapeDtypeStruct((r, LANE), F32), jax.ShapeDtypeStruct((1, LANE), F32)]
    res, rode = _call(
        body, operands, name="ml_scan_bwd", grid=(n_chunks,), in_specs=in_specs, out_specs=out_specs,
        out_shape=out_shape, scratch_shapes=_ml_state_scratch(n_heads, dh) * 2, sem=("arbitrary",), rider=rider)
    return res[0:4], res[4:8], rode


def _post_specs(w, tm, lat_tiles, cols):
    return [pl.BlockSpec((tm, w), (lambda i, cb=cb: (jnp.minimum(i, lat_tiles - 1), cb))) for cb in cols]


def _post_fwd(o_f, o_b, h_f, h_b, u, wa, wb, t_rows, w, n_hg, n_ml, tm):
    lat_tiles = t_rows // tm

    def body(of, ob, hf, hb, az, bo, bz, wa_ref, wb_ref, y_ref):
        y_ref[...] = _post_fn(of[...], ob[...], az[...], hf[...], hb[...], bo[...], bz[...],
                              wa_ref[...], wb_ref[...], n_hg, n_ml).astype(BF16)

    rows = pl.BlockSpec((tm, w), lambda i: (i, 0))
    vec = pl.BlockSpec((1, w), lambda i: (0, 0))
    return pl.pallas_call(
        body, name="post_fwd", grid=(lat_tiles,),
        in_specs=[rows] * 4 + _post_specs(w, tm, lat_tiles, (4, 8, 9)) + [vec, vec],
        out_specs=pl.BlockSpec((tm, 2 * w), lambda i: (i, 0)),
        out_shape=jax.ShapeDtypeStruct((t_rows, 2 * w), BF16),
        compiler_params=_params(("parallel",)),
    )(o_f, o_b, h_f, h_b, u, u, u, wa, wb)


def _post_bwd(o_f, o_b, h_f, h_b, u, wa, wb, dy, t_rows, w, n_hg, n_ml, tm, rider=None):
    r = u.shape[0]
    lat_tiles = t_rows // tm
    lat = lambda i: (jnp.minimum(i, lat_tiles - 1), 0)

    def body(of, ob, hf, hb, az, bo, bz, wa_ref, wb_ref, dy_ref, do_ref, dh_ref, daz_ref, dbo_ref, dbz_ref,
             dwa_ref, dwb_ref):
        i = pl.program_id(0)

        @pl.when(i == 0)
        def _():
            dwa_ref[...] = jnp.zeros_like(dwa_ref)
            dwb_ref[...] = jnp.zeros_like(dwb_ref)

        @pl.when(i < lat_tiles)
        def _():
            f = functools.partial(_post_fn, n_hg=n_hg, n_ml=n_ml)
            _, vjp = jax.vjp(f, of[...], ob[...], az[...], hf[...], hb[...], bo[...], bz[...], wa_ref[...], wb_ref[...])
            d_of, _, d_az, d_hf, _, d_bo, d_bz, d_wa, d_wb = vjp(dy_ref[...])
            do_ref[...] = d_of
            dh_ref[...] = d_hf
            daz_ref[...] = d_az.astype(BF16)
            dbo_ref[...] = d_bo.astype(BF16)
            dbz_ref[...] = d_bz.astype(BF16)
            dwa_ref[...] += d_wa
            dwb_ref[...] += d_wb

        @pl.when(i >= lat_tiles)
        def _():
            daz_ref[...] = jnp.zeros_like(daz_ref)
            dbo_ref[...] = jnp.zeros_like(dbo_ref)
            dbz_ref[...] = jnp.zeros_like(dbz_ref)

    lat_rows = pl.BlockSpec((tm, w), lat)
    all_rows = pl.BlockSpec((tm, w), lambda i: (i, 0))
    vec = pl.BlockSpec((1, w), lambda i: (0, 0))
    sd_t = jax.ShapeDtypeStruct((t_rows, w), F32)
    sd_r = jax.ShapeDtypeStruct((r, w), BF16)
    sd_v = jax.ShapeDtypeStruct((1, w), F32)
    return _call(
        body, (o_f, o_b, h_f, h_b, u, u, u, wa, wb, dy), name="post_bwd", grid=(r // tm,),
        in_specs=[lat_rows] * 4 + _post_specs(w, tm, lat_tiles, (4, 8, 9)) + [vec, vec]
        + [pl.BlockSpec((tm, 2 * w), lat)],
        out_specs=(lat_rows, lat_rows, all_rows, all_rows, all_rows, vec, vec),
        out_shape=(sd_t, sd_t, sd_r, sd_r, sd_r, sd_v, sd_v), sem=("arbitrary",), rider=rider)


OUT_ROW_GATE, OUT_ROW_LN_G, OUT_ROW_LN_B, OUT_ROW_LOSS = 0, 1, 2, 3


def _out_block(y, w_out, x, target, prm, tm):
    t_rows, dm = x.shape
    di = y.shape[1]

    def body(y_ref, w_ref, x_ref, t_ref, p_ref, dz_ref, dy_ref, gx_ref, acc_ref):
        @pl.when(pl.program_id(0) == 0)
        def _():
            acc_ref[...] = jnp.zeros_like(acc_ref)

        gate, ln_g, ln_b = p_ref[0:1, :], p_ref[1:2, :], p_ref[2:3, :]
        z = _nn(y_ref[...], w_ref[...])
        res = ALPHA * x_ref[...] + gate * z
        mu = jnp.mean(res, axis=-1, keepdims=True)
        rc = res - mu
        rstd = lax.rsqrt(jnp.mean(rc * rc, axis=-1, keepdims=True) + LN_EPS)
        rn = rc * rstd
        err = rn * ln_g + ln_b - t_ref[...]
        d_out = err * (1.0 / dm)
        d_rn = d_out * ln_g
        d_res = rstd * (d_rn - jnp.mean(d_rn, axis=-1, keepdims=True)
                        - rn * jnp.mean(d_rn * rn, axis=-1, keepdims=True))
        acc_ref[OUT_ROW_GATE:OUT_ROW_GATE + 1, :] += jnp.sum(d_res * z, axis=0, keepdims=True)
        acc_ref[OUT_ROW_LN_G:OUT_ROW_LN_G + 1, :] += jnp.sum(d_out * rn, axis=0, keepdims=True)
        acc_ref[OUT_ROW_LN_B:OUT_ROW_LN_B + 1, :] += jnp.sum(d_out, axis=0, keepdims=True)
        acc_ref[OUT_ROW_LOSS:OUT_ROW_LOSS + 1, :] += (0.5 / dm) * jnp.sum(err * err, axis=0, keepdims=True)
        gx_ref[...] = ALPHA * d_res
        dz = (d_res * gate).astype(BF16)
        dz_ref[...] = dz
        dy_ref[...] = _nt(dz, w_ref[...])

    rows_d = pl.BlockSpec((tm, dm), lambda i: (i, 0))
    rows_i = pl.BlockSpec((tm, di), lambda i: (i, 0))
    return pl.pallas_call(
        body, name="out_block", grid=(t_rows // tm,),
        in_specs=[rows_i, pl.BlockSpec((di, dm), lambda i: (0, 0)), rows_d, rows_d,
                  pl.BlockSpec((8, dm), lambda i: (0, 0))],
        out_specs=(rows_d, rows_i, rows_d, pl.BlockSpec((8, dm), lambda i: (0, 0))),
        out_shape=(jax.ShapeDtypeStruct((t_rows, dm), BF16), jax.ShapeDtypeStruct((t_rows, di), F32),
                   jax.ShapeDtypeStruct((t_rows, dm), F32), jax.ShapeDtypeStruct((8, dm), F32)),
        compiler_params=_params(("arbitrary",)),
    )(y, w_out, x, target, prm)


def _mod_fwd(c16, w_mod, tn):
    dm, n = w_mod.shape

    def body(c_ref, w_ref, o_ref, a_ref):
        a = _silu(c_ref[...])
        a_ref[...] = a
        o_ref[...] = _nn(a, w_ref[...], HIGHEST)

    return pl.pallas_call(
        body, name="mod_fwd", grid=(n // tn,),
        in_specs=[pl.BlockSpec((16, dm), lambda j: (0, 0)), pl.BlockSpec((dm, tn), lambda j: (0, j))],
        out_specs=(pl.BlockSpec((16, tn), lambda j: (0, j)), pl.BlockSpec((16, dm), lambda j: (0, 0))),
        out_shape=(jax.ShapeDtypeStruct((16, n), F32), jax.ShapeDtypeStruct((16, dm), F32)),
        compiler_params=_params(("arbitrary",)),
    )(c16, w_mod)


def _mod_bwd(a16, dm16, w_mod, tn, rider=None):
    dm, n = w_mod.shape

    def body(a_ref, d_ref, w_ref, dw_ref, dc_ref):
        @pl.when(pl.program_id(0) == 0)
        def _():
            dc_ref[...] = jnp.zeros_like(dc_ref)
        dw_ref[...] = _tn(a_ref[...], d_ref[...], HIGHEST)
        dc_ref[...] += _nt(d_ref[...], w_ref[...], HIGHEST)

    return _call(
        body, (a16, dm16, w_mod), name="mod_bwd", grid=(n // tn,),
        in_specs=[pl.BlockSpec((16, dm), lambda j: (0, 0)), pl.BlockSpec((16, tn), lambda j: (0, j)),
                  pl.BlockSpec((dm, tn), lambda j: (0, j))],
        out_specs=(pl.BlockSpec((dm, tn), lambda j: (0, j)), pl.BlockSpec((16, dm), lambda j: (0, 0))),
        out_shape=(jax.ShapeDtypeStruct((dm, n), F32), jax.ShapeDtypeStruct((16, dm), F32)),
        sem=("arbitrary",), rider=rider)


def _sum_devices(g, fold_rows):
    n_dev, rows, n = g.shape

    def body(g_ref, s_ref, t_ref):
        s = g_ref[0]
        for dev in range(1, n_dev):
            s = s + g_ref[dev]
        t_ref[...] = jnp.broadcast_to(jnp.sum(s, axis=-1, keepdims=True), (rows, LANE))
        s_ref[...] = s
        s_ref[0:fold_rows, :] = s[0:fold_rows] + s[fold_rows:2 * fold_rows]

    return pl.pallas_call(
        body, name="sum_devices",
        out_shape=(jax.ShapeDtypeStruct((rows, n), F32), jax.ShapeDtypeStruct((rows, LANE), F32)),
        compiler_params=_params(),
    )(g)


def _c_ctx_grad(parts, c_ctx_row):
    def body(p_ref, c_ref, o_ref):
        s = p_ref[0]
        for chip in range(1, N_CHIPS):
            s = s + p_ref[2 * chip]
        cv = c_ref[...]
        sg = _sigmoid(cv)
        o_ref[...] = s * (sg * (1.0 + cv * (1.0 - sg)))

    return pl.pallas_call(
        body, name="c_ctx_grad", out_shape=jax.ShapeDtypeStruct(parts.shape[1:], F32), compiler_params=_params(),
    )(parts, c_ctx_row)


def _sum_pair(name, mine, got):
    def body(a_ref, b_ref, o_ref):
        o_ref[...] = (a_ref[...] + b_ref[...]).astype(BF16)

    k, rows, n = mine.shape
    tl = _largest_divisor(n, max(LANE, (1 << 18) // rows), LANE)
    spec = pl.BlockSpec((1, rows, tl), lambda kk, i: (kk, 0, i))
    return pl.pallas_call(
        body, name=name, grid=(k, n // tl), in_specs=[spec, spec], out_specs=spec,
        out_shape=jax.ShapeDtypeStruct(mine.shape, BF16), compiler_params=_params(("parallel", "parallel")),
    )(mine, got)


def _sum_pair_lanes(name, full, got, ci):
    rows, n = got.shape
    tr = _largest_divisor(rows, max(SUBLANE_BF16, (1 << 19) // n), SUBLANE_BF16)

    def body(ci_ref, a_ref, b_ref, o_ref):
        o_ref[...] = (a_ref[...] + b_ref[...].astype(F32)).astype(BF16)

    return pl.pallas_call(
        body, name=name,
        grid_spec=pltpu.PrefetchScalarGridSpec(
            num_scalar_prefetch=1, grid=(rows // tr,),
            in_specs=[pl.BlockSpec((tr, n), lambda i, c: (i, c[0])), pl.BlockSpec((tr, n), lambda i, c: (i, 0))],
            out_specs=pl.BlockSpec((tr, n), lambda i, c: (i, 0))),
        out_shape=jax.ShapeDtypeStruct((rows, n), BF16), compiler_params=_params(("parallel",)),
    )(ci.reshape(1).astype(jnp.int32), full, got)


def _sum_chips(name, got, own, chip):
    k, rows, n = got.shape
    tl = _largest_divisor(n, max(LANE, (1 << 18) // rows), LANE)

    def body(chip_ref, g_ref, own_ref, o_ref):
        total = None
        for kk in range(k):
            term = jnp.where(chip_ref[0] == kk, own_ref[0], g_ref[kk]).astype(F32)
            total = term if total is None else total + term
        o_ref[...] = total

    return pl.pallas_call(
        body, name=name,
        grid_spec=pltpu.PrefetchScalarGridSpec(
            num_scalar_prefetch=1, grid=(n // tl,),
            in_specs=[pl.BlockSpec((k, rows, tl), lambda i, c: (0, 0, i)),
                      pl.BlockSpec((1, rows, tl), lambda i, c: (c[0], 0, i))],
            out_specs=pl.BlockSpec((rows, tl), lambda i, c: (0, i))),
        out_shape=jax.ShapeDtypeStruct((rows, n), F32), compiler_params=_params(("parallel",)),
    )(chip.reshape(1).astype(jnp.int32), got, own)


def _adamw_update(w, g, m, v):
    m2 = ADAM_B1 * m + (1.0 - ADAM_B1) * g
    v2 = ADAM_B2 * v + (1.0 - ADAM_B2) * jnp.square(g)
    m_hat = m2 / (1.0 - ADAM_B1 ** ADAM_STEP)
    v_hat = v2 / (1.0 - ADAM_B2 ** ADAM_STEP)
    return -ADAM_LR * (m_hat / (jnp.sqrt(v_hat) + ADAM_EPS) + ADAM_WD * w), m2, v2


def _adamw(name, w, g, m, v, rider=None):
    rows, n = w.shape
    if rows % 8 == 0:
        tr = _largest_divisor(rows, max(8, (1 << 18) // n), 8)
        block, index, steps = (tr, n), (lambda i: (i, 0)), rows // tr
    else:
        tl = _largest_divisor(n, max(LANE, (1 << 18) // rows), LANE)
        block, index, steps = (rows, tl), (lambda i: (0, i)), n // tl

    def body(w_ref, g_ref, m_ref, v_ref, d_ref, mo_ref, vo_ref):
        d_ref[...], mo_ref[...], vo_ref[...] = _adamw_update(w_ref[...], g_ref[...], m_ref[...], v_ref[...])

    spec = pl.BlockSpec(block, index)
    sds = jax.ShapeDtypeStruct((rows, n), F32)
    return _call(body, (w, g, m, v), name=name, grid=(steps,), in_specs=[spec] * 4, out_specs=(spec,) * 3,
                 out_shape=(sds, sds, sds), sem=("parallel",), rider=rider)


PACK_LANES = 1024


def _pack(pieces):
    flat = jnp.concatenate([p.reshape(-1) for p in pieces])
    total = -(-flat.shape[0] // (8 * PACK_LANES)) * 8 * PACK_LANES
    return jnp.pad(flat, (0, total - flat.shape[0])).reshape(-1, PACK_LANES)


def _unpack(packed, shapes):
    flat = packed.reshape(-1)
    out, off = [], 0
    for shp in shapes:
        size = math.prod(shp)
        out.append(flat[off:off + size].reshape(shp))
        off += size
    return out


def _rows8(rows, width):
    flat = [r.reshape(width) for r in rows] + [jnp.zeros(((8 - len(rows)) * width,), F32)]
    return jnp.concatenate(flat).reshape(8, width)


def kernel(x, c, ctx, c_ctx, w_mod, b_mod, w_in, conv_w, conv_b, hg_lb, ml_gate_b, hg_norm_w, ml_norm_w, w_out, ln_g, ln_b, loss_target, m_c_ctx, m_w_mod, m_b_mod, m_w_in, m_conv_w, m_conv_b, m_hg_lb, m_ml_gate_b, m_hg_norm_w, m_ml_norm_w, m_w_out, m_ln_g, m_ln_b, v_c_ctx, v_w_mod, v_b_mod, v_w_in, v_conv_w, v_conv_b, v_hg_lb, v_ml_gate_b, v_hg_norm_w, v_ml_norm_w, v_w_out, v_ln_g, v_ln_b):
    t_rows, dm = x.shape[1], x.shape[2]
    c_rows = ctx.shape[1]
    w = hg_norm_w.shape[1]
    n_ml = ml_gate_b.shape[-1]
    n_hg = w // HG_DK
    di = 2 * w
    n_in = 10 * w + 4 * n_ml
    ns = w_in.shape[2]
    nm = w_mod.shape[2]
    n_pad = 10 * w + LANE
    r_rows = t_rows + c_rows
    row_gcd = math.gcd(t_rows, c_rows)
    hg_chunk, ml_chunk = math.gcd(HG_CHUNK, row_gcd), math.gcd(ML_CHUNK, row_gcd)
    hg_counts = (t_rows // hg_chunk, c_rows // hg_chunk, hg_chunk)
    ml_counts = (t_rows // ml_chunk, c_rows // ml_chunk, ml_chunk)
    assert ml_norm_w.shape[1] == w and di == dm and N_CHIPS * ns == n_in and N_CHIPS * nm == 3 * dm
    assert w_out.shape[1] * N_CHIPS == di and 4 * n_ml <= LANE and t_rows % GRID_W == 0

    xi, yi, ci = lax.axis_index("x"), lax.axis_index("y"), lax.axis_index("c")
    chip = 2 * xi + yi
    dev = 4 * xi + 2 * yi + ci

    tm = _largest_divisor(math.gcd(t_rows, c_rows), 256, 8)
    tm_mm = _largest_divisor(r_rows, 1088, SUBLANE_BF16)
    tn_mm = LANE * _largest_divisor(n_pad // LANE, 9)
    tn_mod = _largest_divisor(nm, 512, LANE)

    shard_shapes = [(dm,), (2, 2, w // N_CHIPS), (3, 3, di // N_CHIPS)]
    g1 = _all_gather8(_pack([c, hg_lb, conv_w])).run("gather_inputs")[0]
    per_dev = [_unpack(g1[i], shard_shapes) for i in range(N_DEV)]
    c_all = jnp.stack([p[0] for p in per_dev])
    lb_full = jnp.concatenate([per_dev[2 * k][1] for k in range(N_CHIPS)], axis=-1)
    conv_w9 = jnp.concatenate([per_dev[2 * k][2] for k in range(N_CHIPS)], axis=-1).reshape(9, di)

    c16 = jnp.concatenate([c_all, c_ctx[None], jnp.zeros((16 - N_DEV - 1, dm), F32)])
    mod_part, a16 = _mod_fwd(c16, w_mod[0], tn_mod)
    g2 = _all_gather8(mod_part).run("gather_mod")[0]
    mod_all = jnp.concatenate([g2[2 * k] for k in range(N_CHIPS)], axis=1) + b_mod
    mod_x = lax.dynamic_index_in_dim(mod_all, dev, 0, keepdims=False).reshape(3, dm)
    mod_c = mod_all[N_DEV].reshape(3, dm)
    prm = jnp.stack([_rows8(list(mod_x), dm), _rows8(list(mod_c), dm)])

    as_t = lambda a: jnp.transpose(a[0])
    half_in = lax.dynamic_slice_in_dim(as_t(w_in).astype(BF16), ci * (dm // 2), dm // 2, 1)
    half_out = lax.dynamic_slice_in_dim(w_out[0].astype(BF16), ci * (di // (2 * N_CHIPS)), di // (2 * N_CHIPS), 0)
    lanes_of = lambda core: pl.ds(core * (dm // 2), dm // 2)
    landing = lambda s, r: (_chip_of(s), slice(None), lanes_of(s[2]))
    own_placed = lax.dynamic_update_slice(jnp.zeros((N_CHIPS + 1, ns, dm), BF16),
                                          as_t(w_in).astype(BF16)[None], (chip, 0, 0))
    gather_in = _Exchange([half_in, own_placed], [jax.ShapeDtypeStruct(own_placed.shape, BF16)],
                          [(mask, 0, None, 0, landing) for mask in CHIP_MASKS[:2]], in_place={1: 0})

    hc, (gw_in,) = _modulate_fwd(x[0], ctx[0], prm, tm, rider=gather_in)
    chip_x = lambda s: 2 * (1 - s[0]) + s[1]
    chip_y = lambda s: 2 * s[0] + 1 - s[1]
    chip_xy = lambda s: 2 * (1 - s[0]) + 1 - s[1]
    quarter = lambda core, q: pl.ds(core * (dm // 2) + q * (dm // 4), dm // 4)
    half_of = lambda which: (lambda s, r: (which(s), slice(None), lanes_of(s[2])))
    relay_x = lambda s, r: (chip_y(s), slice(None), quarter(s[2], 0))
    relay_y = lambda s, r: (chip_x(s), slice(None), quarter(s[2], 1))
    whole = jax.ShapeDtypeStruct(gw_in.shape, BF16)
    gw_in = _Exchange([gw_in], [whole],
                      [((1, 0, 0), 0, relay_x, 0, relay_x), ((0, 1, 0), 0, relay_y, 0, relay_y)]
                      + [(SIBLING_MASK, 0, half_of(which), 0, half_of(which)) for which in (chip_x, chip_y)],
                      in_place={0: 0}).run("relay_w_in")[0]
    gw_in = _Exchange([gw_in], [whole], [(SIBLING_MASK, 0, half_of(chip_xy), 0, half_of(chip_xy))],
                      in_place={0: 0}).run("gather_w_in_pair")[0]
    wt_full = gw_in.reshape((N_CHIPS + 1) * ns, dm)
    assert wt_full.shape[0] >= n_pad
    u, (got_out,) = _mm_nt("in_proj", hc, wt_full, n_pad, tm_mm, tn_mm, F32, rider=_all_gather_chips([half_out]))
    fetched_out = _own_block(chip, half_out, got_out)
    (o_f, o_b), hg_saved, (swapped_out,) = _hg_scan_fwd(u, lb_full, w, *hg_counts,
                                                         rider=_sibling_swap([fetched_out]))
    w_out_full = _join_halves(ci, fetched_out, swapped_out, 1).reshape(di, dm)
    qk = _conv_fwd(u, conv_w9, conv_b, t_rows, c_rows, w, LANE)
    gate_b_row = jnp.pad(ml_gate_b.reshape(1, -1), ((0, 0), (0, LANE - 4 * n_ml)))
    (h_f, h_b), ml_saved = _ml_scan_fwd(qk, u, gate_b_row, w, n_ml, *ml_counts)
    y = _post_fwd(o_f, o_b, h_f, h_b, u, hg_norm_w, ml_norm_w, t_rows, w, n_hg, n_ml, tm)
    prm_out = _rows8([mod_x[2], ln_g, ln_b], dm)
    dz, dy, gx_direct, acc_out = _out_block(y, w_out_full, x[0], loss_target[0], prm_out, tm)

    d_w_out = _mm_tn("d_w_out", y, dz, _largest_divisor(di, 1024, LANE),
                     _largest_divisor(t_rows, 1024, SUBLANE_BF16))
    d_w_out4 = d_w_out.reshape(N_CHIPS, 2, di // (2 * N_CHIPS), dm)
    mine_out = lax.dynamic_index_in_dim(d_w_out4, ci, 1, keepdims=False)
    other_out = lax.dynamic_index_in_dim(d_w_out4, 1 - ci, 1, keepdims=False)
    (d_o, d_h, d_az, d_bo, d_bz, d_wa, d_wb), (got_out,) = _post_bwd(
        o_f, o_b, h_f, h_b, u, hg_norm_w, ml_norm_w, dy, t_rows, w, n_hg, n_ml, tm, rider=_sibling_swap([other_out]))
    pair_out = _sum_pair("rs_pair_sum_w_out", mine_out, got_out)
    (d_aq_f, d_aff, d_ai_f, d_lb_f), (d_aq_b, d_afb, d_ai_b, d_lb_b), (landed_out,) = _hg_scan_bwd(
        u, lb_full, hg_saved, d_o, w, *hg_counts, rider=_chip_scatter([pair_out]))
    half_g_out = _sum_chips("rs_chip_sum_w_out", landed_out, pair_out, chip)
    (d_qk_f, d_v_f, d_g_f, d_gb_f), (d_qk_b, d_v_b, d_g_b, d_gb_b), (sibling_out,) = _ml_scan_bwd(
        qk, u, gate_b_row, ml_saved, d_h, w, n_ml, *ml_counts, rider=_sibling_swap([half_g_out]))
    g_w_out = _join_halves(ci, half_g_out, sibling_out, 0)
    d_bqk, d_cw, d_cb = _conv_bwd(u, (d_qk_f, d_qk_b), conv_w9, conv_b, t_rows, c_rows, w, LANE)
    du = _assemble_du([(d_aq_f, d_aq_b), d_aff, d_afb, (d_ai_f, d_ai_b), d_az, d_bqk, (d_v_f, d_v_b), d_bo, d_bz],
                      (d_g_f, d_g_b), n_pad, tm)
    d_wt_in, d_wt_in_bf16 = _mm_tn("d_w_in", du, hc, tn_mm, tm_mm, with_bf16=True)

    delta, new_m, new_v = {}, {}, {}
    res, (got_in,) = _adamw(
        "adamw_w_out", w_out[0], g_w_out, m_w_out[0], v_w_out[0],
        rider=_Exchange([d_wt_in_bf16], [jax.ShapeDtypeStruct((n_pad, dm // 2), BF16)],
                        [(SIBLING_MASK, 0, lambda s, r: (slice(None), lanes_of(r[2])), 0, None)]))
    delta["w_out"], new_m["w_out"], new_v["w_out"] = (a[None] for a in res)
    pair_half = _sum_pair_lanes("rs_pair_sum_w_in", d_wt_in, got_in, ci)
    pair_in = jnp.stack([pair_half[k * ns:(k + 1) * ns] for k in range(N_CHIPS)])
    d_hc, (landed_in,) = _mm_acc("d_h", du, wt_full, tm_mm, tn_mm, rider=_chip_scatter([pair_in]))
    half_g_in = _sum_chips("rs_chip_sum_w_in", landed_in, pair_in, chip)
    (gx, acc_mod), _ = _modulate_bwd(x[0], ctx[0], d_hc, prm, gx_direct, tm)
    grad_x = gx[None]

    zero_row = jnp.zeros((dm,), F32)
    d_gb = jnp.concatenate([d_gb_f[:, 0:n_ml], d_gb_b[:, n_ml:2 * n_ml], d_gb_f[:, 2 * n_ml:3 * n_ml],
                            d_gb_b[:, 3 * n_ml:4 * n_ml], jnp.zeros((1, dm - 4 * n_ml), F32)], axis=1)
    rows = [acc_mod[0, 0], acc_mod[0, 1], acc_out[OUT_ROW_GATE],
            acc_mod[1, 0], acc_mod[1, 1], zero_row]
    rows += list(d_cw) + [d_cb[0], d_lb_f.reshape(dm), d_lb_b.reshape(dm),
                          jnp.concatenate([d_wa[0], d_wb[0]]), acc_out[OUT_ROW_LN_G], acc_out[OUT_ROW_LN_B],
                          acc_out[OUT_ROW_LOSS], d_gb[0], zero_row]
    ROW_CW, ROW_CB, ROW_LB, ROW_NORM, ROW_LN_G, ROW_LN_B, ROW_LOSS, ROW_GB = 6, 15, 16, 18, 19, 20, 21, 22
    small_rows = jnp.concatenate([r.reshape(dm) for r in rows]).reshape(len(rows), dm)
    g3 = _all_gather8(small_rows).run("gather_small_grads")[0]
    sums, totals = _sum_devices(g3, 3)
    loss = totals[ROW_LOSS, 0]
    dm16 = jnp.concatenate([g3[:, 0:3, :].reshape(N_DEV, 3 * dm), sums[3:6].reshape(1, 3 * dm),
                            jnp.zeros((16 - N_DEV - 1, 3 * dm), F32)])
    (g_w_mod, dc16), (sibling_g_in,) = _mod_bwd(a16, lax.dynamic_slice_in_dim(dm16, chip * nm, nm, 1), w_mod[0],
                                                tn_mod, rider=_sibling_swap([half_g_in]))
    g_wt_in = _join_halves(ci, half_g_in, sibling_g_in, 1)
    g4 = _all_gather8(jnp.pad(dc16[N_DEV:N_DEV + 1], ((0, 7), (0, 0)))).run("gather_c_ctx")[0]
    g_c_ctx = _c_ctx_grad(g4, jnp.broadcast_to(c_ctx[None], (8, dm)))[0]
    res, _ = _adamw("adamw_w_in", as_t(w_in), g_wt_in, as_t(m_w_in), as_t(v_w_in))
    delta["w_in"], new_m["w_in"], new_v["w_in"] = (jnp.transpose(a)[None] for a in res)
    res, _ = _adamw("adamw_w_mod", w_mod[0], g_w_mod, m_w_mod[0], v_w_mod[0])
    delta["w_mod"], new_m["w_mod"], new_v["w_mod"] = (a[None] for a in res)

    chip_cols = lambda a, width: lax.dynamic_slice_in_dim(a, chip * width, width, a.ndim - 1)
    grads = {
        "c_ctx": g_c_ctx,
        "w_mod": g_w_mod[None],
        "b_mod": sums[0:3].reshape(1, 3 * dm),
        "w_in": jnp.transpose(g_wt_in)[None],
        "conv_w": chip_cols(sums[ROW_CW:ROW_CW + 9].reshape(1, 3, 3, di), di // N_CHIPS),
        "conv_b": sums[ROW_CB][None],
        "hg_lb": chip_cols(sums[ROW_LB:ROW_LB + 2].reshape(2, 2, w), w // N_CHIPS),
        "ml_gate_b": sums[ROW_GB, 0:4 * n_ml].reshape(1, 4, n_ml),
        "hg_norm_w": sums[ROW_NORM, 0:w][None],
        "ml_norm_w": sums[ROW_NORM, w:2 * w][None],
        "w_out": g_w_out[None],
        "ln_g": sums[ROW_LN_G][None],
        "ln_b": sums[ROW_LN_B][None],
    }
    weights = dict(c_ctx=c_ctx, w_mod=w_mod, b_mod=b_mod, w_in=w_in, conv_w=conv_w, conv_b=conv_b, hg_lb=hg_lb,
                   ml_gate_b=ml_gate_b, hg_norm_w=hg_norm_w, ml_norm_w=ml_norm_w, w_out=w_out, ln_g=ln_g, ln_b=ln_b)
    mom1 = dict(c_ctx=m_c_ctx, w_mod=m_w_mod, b_mod=m_b_mod, w_in=m_w_in, conv_w=m_conv_w, conv_b=m_conv_b,
                hg_lb=m_hg_lb, ml_gate_b=m_ml_gate_b, hg_norm_w=m_hg_norm_w, ml_norm_w=m_ml_norm_w, w_out=m_w_out,
                ln_g=m_ln_g, ln_b=m_ln_b)
    mom2 = dict(c_ctx=v_c_ctx, w_mod=v_w_mod, b_mod=v_b_mod, w_in=v_w_in, conv_w=v_conv_w, conv_b=v_conv_b,
                hg_lb=v_hg_lb, ml_gate_b=v_ml_gate_b, hg_norm_w=v_hg_norm_w, ml_norm_w=v_ml_norm_w, w_out=v_w_out,
                ln_g=v_ln_g, ln_b=v_ln_b)
    names = list(weights)
    big = ("w_mod", "w_in", "w_out")
    small = [n for n in names if n not in big]

    small_shapes = [weights[n].shape for n in small]
    res, _ = _adamw("adamw_small", *(_pack([src[n] for n in small]) for src in (weights, grads, mom1, mom2)))
    for out, packed in zip((delta, new_m, new_v), res):
        for n, a in zip(small, _unpack(packed, small_shapes)):
            out[n] = a

    return (loss, grad_x, *[grads[n].reshape(weights[n].shape) for n in names], *[delta[n] for n in names],
            *[new_m[n] for n in names], *[new_v[n] for n in names])
```
